```python
import math
import numpy as np
import jax
import jax.numpy as jnp
from jax import lax

D_MODEL = 2048
BATCH = 8
SEQ = 4096
DEPTH = 2

N_MIXERS = 2
N_A = (DEPTH + 1) // 2
N_B = DEPTH // 2
BLK = 128
EPS = 1e-6

MLA_HEADS = 16
Q_LORA = 512
KV_LORA = 512
NOPE_DIM = 128
ROPE_DIM = 64
V_DIM = 128
QK_DIM = NOPE_DIM + ROPE_DIM
ROPE_THETA = 10000.0

DIL_PAIRS = ((128, 1), (512, 4), (2048, 16))
DIL_GROUPS = len(DIL_PAIRS)
DIL_HEADS = 8
DIL_HEAD_DIM = 128
ALIBI_TOTAL_HEADS = DIL_GROUPS * DIL_HEADS

D_FF = 5632

kernel_name = "hybrid_mla_dilated_macaron"


def rmsnorm(t, g):
    tf = t.astype(jnp.float32)
    y = tf * lax.rsqrt(jnp.mean(tf * tf, axis=-1, keepdims=True) + EPS)
    return (y * g.astype(jnp.float32)).astype(t.dtype)


def swiglu(xn, w_in, w_out):
    gate, up = jnp.split(xn @ w_in, 2, axis=-1)
    return (jax.nn.silu(gate) * up) @ w_out


def rope_tables(S):
    inv = 1.0 / (ROPE_THETA ** (jnp.arange(0, ROPE_DIM, 2, dtype=jnp.float32) / ROPE_DIM))
    ang = jnp.arange(S, dtype=jnp.float32)[:, None] * inv[None, :]
    return jnp.cos(ang), jnp.sin(ang)


def apply_rope(t, cos, sin):
    t1, t2 = jnp.split(t, 2, axis=-1)
    c = cos[None, :, None, :].astype(t.dtype)
    s = sin[None, :, None, :].astype(t.dtype)
    return jnp.concatenate([t1 * c - t2 * s, t1 * s + t2 * c], axis=-1)


def mla_mixer(xn, w_down, g_cq, g_ckv, w_uq, w_ukv, g_qn, g_kn, w_o):
    B, S, _ = xn.shape
    lat = xn @ w_down
    c_q = rmsnorm(lat[..., :Q_LORA], g_cq)
    c_kv = rmsnorm(lat[..., Q_LORA:Q_LORA + KV_LORA], g_ckv)
    k_pe = lat[..., Q_LORA + KV_LORA:]
    q = (c_q @ w_uq).reshape(B, S, MLA_HEADS, QK_DIM)
    kv = (c_kv @ w_ukv).reshape(B, S, MLA_HEADS, NOPE_DIM + V_DIM)
    k_nope, v = kv[..., :NOPE_DIM], kv[..., NOPE_DIM:]
    k = jnp.concatenate(
        [k_nope, jnp.broadcast_to(k_pe[:, :, None, :], (B, S, MLA_HEADS, ROPE_DIM))], axis=-1)
    q = rmsnorm(q, g_qn)
    k = rmsnorm(k, g_kn)
    cos, sin = rope_tables(S)
    q = jnp.concatenate([q[..., :NOPE_DIM], apply_rope(q[..., NOPE_DIM:], cos, sin)], axis=-1)
    k = jnp.concatenate([k[..., :NOPE_DIM], apply_rope(k[..., NOPE_DIM:], cos, sin)], axis=-1)

    scale = 1.0 / math.sqrt(QK_DIM)
    nb = S // BLK
    qb = q.reshape(B, nb, BLK, MLA_HEADS, QK_DIM).transpose(1, 0, 2, 3, 4)
    starts = jnp.arange(nb, dtype=jnp.int32) * BLK
    kf = k.astype(jnp.float32)
    vf = v.astype(jnp.float32)
    kpos = jnp.arange(S, dtype=jnp.int32)

    def block(args):
        qblk, s0 = args
        s = jnp.einsum('bqhe,bkhe->bhqk', qblk.astype(jnp.float32), kf) * scale
        qpos = s0 + jnp.arange(BLK, dtype=jnp.int32)
        s = jnp.where((kpos[None, :] <= qpos[:, None])[None, None], s, -jnp.inf)
        p = jax.nn.softmax(s, axis=-1)
        return jnp.einsum('bhqk,bkhe->bqhe', p, vf)

    o = lax.map(block, (qb, starts))
    o = o.transpose(1, 0, 2, 3, 4).reshape(B, S, MLA_HEADS * V_DIM).astype(xn.dtype)
    return o @ w_o


def dilated_group_attn(q, k, v, window, dilation, slopes):
    B, S, H, dh = q.shape
    win_sub = window // dilation
    L = S // dilation
    Lp = -(-L // BLK) * BLK
    nb = Lp // BLK

    def to_sub(t):
        t = t.astype(jnp.float32).reshape(B, L, dilation, H, dh).transpose(0, 2, 1, 3, 4)
        return jnp.pad(t, ((0, 0), (0, 0), (0, Lp - L), (0, 0), (0, 0)))

    qb = to_sub(q).reshape(B, dilation, nb, BLK, H, dh)
    kb = to_sub(k).reshape(B, dilation, nb, BLK, H, dh)
    vb = to_sub(v).reshape(B, dilation, nb, BLK, H, dh)
    pad_prev = ((0, 0), (0, 0), (1, 0), (0, 0), (0, 0), (0, 0))
    kcat = jnp.concatenate([jnp.pad(kb, pad_prev)[:, :, :-1], kb], axis=3)
    vcat = jnp.concatenate([jnp.pad(vb, pad_prev)[:, :, :-1], vb], axis=3)

    iq = jnp.arange(BLK, dtype=jnp.int32)
    ik = jnp.arange(2 * BLK, dtype=jnp.int32)
    nidx = jnp.arange(nb, dtype=jnp.int32)
    dist = iq[:, None] + BLK - ik[None, :]
    key_ok = (nidx[:, None] * BLK - BLK + ik[None, :]) >= 0
    mask = ((dist >= 0) & (dist <= win_sub))[None] & key_ok[:, None, :]
    bias = -slopes[:, None, None] * (dilation * dist).astype(jnp.float32)[None]

    scale = 1.0 / math.sqrt(dh)
    s = jnp.einsum('bdnqhe,bdnkhe->bdnhqk', qb, kcat) * scale + bias[None, None, None]
    s = jnp.where(mask[None, None, :, None], s, -jnp.inf)
    lse = jax.nn.logsumexp(s, axis=-1)
    p = jnp.exp(s - lse[..., None])
    o = jnp.einsum('bdnhqk,bdnkhe->bdnqhe', p, vcat)

    o = o.reshape(B, dilation, Lp, H, dh)[:, :, :L].transpose(0, 2, 1, 3, 4).reshape(B, S, H, dh)
    lse = lse.transpose(0, 1, 2, 4, 3).reshape(B, dilation, Lp, H)[:, :, :L]
    lse = lse.transpose(0, 2, 1, 3).reshape(B, S, H)
    return o, lse


def alibi_slopes():
    k = np.arange(1, ALIBI_TOTAL_HEADS + 1, dtype=np.float32)
    return jnp.asarray(2.0 ** (-8.0 * k / ALIBI_TOTAL_HEADS), dtype=jnp.float32)


def dilated_mixer(xn, w_qkv, g_qn, g_kn, w_o):
    B, S, _ = xn.shape
    qkv = (xn @ w_qkv).reshape(B, S, 3, DIL_GROUPS, DIL_HEADS, DIL_HEAD_DIM)
    q = rmsnorm(qkv[:, :, 0], g_qn)
    k = rmsnorm(qkv[:, :, 1], g_kn)
    v = qkv[:, :, 2]
    slopes = alibi_slopes().reshape(DIL_GROUPS, DIL_HEADS)
    outs, lses = [], []
    for g, (window, dilation) in enumerate(DIL_PAIRS):
        o_g, lse_g = dilated_group_attn(q[:, :, g], k[:, :, g], v[:, :, g], window, dilation, slopes[g])
        outs.append(o_g)
        lses.append(lse_g)
    o = jnp.stack(outs, axis=2)
    w = jax.nn.softmax(jnp.stack(lses, axis=2), axis=2)
    o = jnp.sum(o * w[..., None], axis=2).reshape(B, S, DIL_HEADS * DIL_HEAD_DIM).astype(xn.dtype)
    return o @ w_o


def _fwd_setup_inputs(seed: int = 0) -> dict:
    key = jax.random.key(seed)
    ks = iter(jax.random.split(key, 40))

    def w(shape, fan_in):
        return jax.random.normal(next(ks), shape, jnp.float32) * (fan_in ** -0.5)

    def gain(shape):
        return 1.0 + 0.02 * jax.random.normal(next(ks), shape, jnp.float32)

    D = D_MODEL
    return {
        "x": jax.random.normal(next(ks), (BATCH, SEQ, D), jnp.float32),
        "ffn1_norm": gain((DEPTH, D)),
        "ffn1_w_in": w((DEPTH, D, 2 * D_FF), D),
        "ffn1_w_out": w((DEPTH, D_FF, D), D_FF),
        "mix_norm": gain((DEPTH, D)),
        "ffn2_norm": gain((DEPTH, D)),
        "ffn2_w_in": w((DEPTH, D, 2 * D_FF), D),
        "ffn2_w_out": w((DEPTH, D_FF, D), D_FF),
        "mla_w_down": w((N_A, D, Q_LORA + KV_LORA + ROPE_DIM), D),
        "mla_g_cq": gain((N_A, Q_LORA)),
        "mla_g_ckv": gain((N_A, KV_LORA)),
        "mla_w_uq": w((N_A, Q_LORA, MLA_HEADS * QK_DIM), Q_LORA),
        "mla_w_ukv": w((N_A, KV_LORA, MLA_HEADS * (NOPE_DIM + V_DIM)), KV_LORA),
        "mla_g_qn": gain((N_A, QK_DIM)),
        "mla_g_kn": gain((N_A, QK_DIM)),
        "mla_w_o": w((N_A, MLA_HEADS * V_DIM, D), MLA_HEADS * V_DIM),
        "dil_w_qkv": w((N_B, D, 3 * DIL_GROUPS * DIL_HEADS * DIL_HEAD_DIM), D),
        "dil_g_qn": gain((N_B, DIL_HEAD_DIM)),
        "dil_g_kn": gain((N_B, DIL_HEAD_DIM)),
        "dil_w_o": w((N_B, DIL_HEADS * DIL_HEAD_DIM, D), DIL_HEADS * DIL_HEAD_DIM),
    }


def _fwd_reference(x, ffn1_norm, ffn1_w_in, ffn1_w_out, mix_norm, ffn2_norm, ffn2_w_in, ffn2_w_out,
              mla_w_down, mla_g_cq, mla_g_ckv, mla_w_uq, mla_w_ukv, mla_g_qn, mla_g_kn, mla_w_o,
              dil_w_qkv, dil_g_qn, dil_g_kn, dil_w_o):
    for i in range(DEPTH):
        j = i // N_MIXERS
        x = x + 0.5 * swiglu(rmsnorm(x, ffn1_norm[i]), ffn1_w_in[i], ffn1_w_out[i])
        xn = rmsnorm(x, mix_norm[i])
        if i % N_MIXERS == 0:
            y = mla_mixer(xn, mla_w_down[j], mla_g_cq[j], mla_g_ckv[j], mla_w_uq[j], mla_w_ukv[j],
                          mla_g_qn[j], mla_g_kn[j], mla_w_o[j])
        else:
            y = dilated_mixer(xn, dil_w_qkv[j], dil_g_qn[j], dil_g_kn[j], dil_w_o[j])
        x = x + y
        x = x + 0.5 * swiglu(rmsnorm(x, ffn2_norm[i]), ffn2_w_in[i], ffn2_w_out[i])
    return x


import jax as _jax
import jax.numpy as _jnp

TWIN_FORMAT = 'train_step'
FWD_PARAMS = ['x', 'ffn1_norm', 'ffn1_w_in', 'ffn1_w_out', 'mix_norm', 'ffn2_norm', 'ffn2_w_in', 'ffn2_w_out', 'mla_w_down', 'mla_g_cq', 'mla_g_ckv', 'mla_w_uq', 'mla_w_ukv', 'mla_g_qn', 'mla_g_kn', 'mla_w_o', 'dil_w_qkv', 'dil_g_qn', 'dil_g_kn', 'dil_w_o']
TWIN_WEIGHTS = ['ffn1_norm', 'ffn1_w_in', 'ffn1_w_out', 'mix_norm', 'ffn2_norm', 'ffn2_w_in', 'ffn2_w_out', 'mla_w_down', 'mla_g_cq', 'mla_g_ckv', 'mla_w_uq', 'mla_w_ukv', 'mla_g_qn', 'mla_g_kn', 'mla_w_o', 'dil_w_qkv', 'dil_g_qn', 'dil_g_kn', 'dil_w_o']
TWIN_DIFF_INPUT = 'x'
TWIN_INPUTS = ['x', 'ffn1_norm', 'ffn1_w_in', 'ffn1_w_out', 'mix_norm', 'ffn2_norm', 'ffn2_w_in', 'ffn2_w_out', 'mla_w_down', 'mla_g_cq', 'mla_g_ckv', 'mla_w_uq', 'mla_w_ukv', 'mla_g_qn', 'mla_g_kn', 'mla_w_o', 'dil_w_qkv', 'dil_g_qn', 'dil_g_kn', 'dil_w_o', 'loss_target', 'm_ffn1_norm', 'm_ffn1_w_in', 'm_ffn1_w_out', 'm_mix_norm', 'm_ffn2_norm', 'm_ffn2_w_in', 'm_ffn2_w_out', 'm_mla_w_down', 'm_mla_g_cq', 'm_mla_g_ckv', 'm_mla_w_uq', 'm_mla_w_ukv', 'm_mla_g_qn', 'm_mla_g_kn', 'm_mla_w_o', 'm_dil_w_qkv', 'm_dil_g_qn', 'm_dil_g_kn', 'm_dil_w_o', 'v_ffn1_norm', 'v_ffn1_w_in', 'v_ffn1_w_out', 'v_mix_norm', 'v_ffn2_norm', 'v_ffn2_w_in', 'v_ffn2_w_out', 'v_mla_w_down', 'v_mla_g_cq', 'v_mla_g_ckv', 'v_mla_w_uq', 'v_mla_w_ukv', 'v_mla_g_qn', 'v_mla_g_kn', 'v_mla_w_o', 'v_dil_w_qkv', 'v_dil_g_qn', 'v_dil_g_kn', 'v_dil_w_o']
TWIN_OUTPUTS = ['loss', 'grad_x', 'grad_ffn1_norm', 'grad_ffn1_w_in', 'grad_ffn1_w_out', 'grad_mix_norm', 'grad_ffn2_norm', 'grad_ffn2_w_in', 'grad_ffn2_w_out', 'grad_mla_w_down', 'grad_mla_g_cq', 'grad_mla_g_ckv', 'grad_mla_w_uq', 'grad_mla_w_ukv', 'grad_mla_g_qn', 'grad_mla_g_kn', 'grad_mla_w_o', 'grad_dil_w_qkv', 'grad_dil_g_qn', 'grad_dil_g_kn', 'grad_dil_w_o', 'delta_ffn1_norm', 'delta_ffn1_w_in', 'delta_ffn1_w_out', 'delta_mix_norm', 'delta_ffn2_norm', 'delta_ffn2_w_in', 'delta_ffn2_w_out', 'delta_mla_w_down', 'delta_mla_g_cq', 'delta_mla_g_ckv', 'delta_mla_w_uq', 'delta_mla_w_ukv', 'delta_mla_g_qn', 'delta_mla_g_kn', 'delta_mla_w_o', 'delta_dil_w_qkv', 'delta_dil_g_qn', 'delta_dil_g_kn', 'delta_dil_w_o', 'new_m_ffn1_norm', 'new_m_ffn1_w_in', 'new_m_ffn1_w_out', 'new_m_mix_norm', 'new_m_ffn2_norm', 'new_m_ffn2_w_in', 'new_m_ffn2_w_out', 'new_m_mla_w_down', 'new_m_mla_g_cq', 'new_m_mla_g_ckv', 'new_m_mla_w_uq', 'new_m_mla_w_ukv', 'new_m_mla_g_qn', 'new_m_mla_g_kn', 'new_m_mla_w_o', 'new_m_dil_w_qkv', 'new_m_dil_g_qn', 'new_m_dil_g_kn', 'new_m_dil_w_o', 'new_v_ffn1_norm', 'new_v_ffn1_w_in', 'new_v_ffn1_w_out', 'new_v_mix_norm', 'new_v_ffn2_norm', 'new_v_ffn2_w_in', 'new_v_ffn2_w_out', 'new_v_mla_w_down', 'new_v_mla_g_cq', 'new_v_mla_g_ckv', 'new_v_mla_w_uq', 'new_v_mla_w_ukv', 'new_v_mla_g_qn', 'new_v_mla_g_kn', 'new_v_mla_w_o', 'new_v_dil_w_qkv', 'new_v_dil_g_qn', 'new_v_dil_g_kn', 'new_v_dil_w_o']
TWIN_LEAF_KINDS = {'loss': 'loss', 'grad_x': 'grad_x', 'grad_ffn1_norm': 'grad_w', 'grad_ffn1_w_in': 'grad_w', 'grad_ffn1_w_out': 'grad_w', 'grad_mix_norm': 'grad_w', 'grad_ffn2_norm': 'grad_w', 'grad_ffn2_w_in': 'grad_w', 'grad_ffn2_w_out': 'grad_w', 'grad_mla_w_down': 'grad_w', 'grad_mla_g_cq': 'grad_w', 'grad_mla_g_ckv': 'grad_w', 'grad_mla_w_uq': 'grad_w', 'grad_mla_w_ukv': 'grad_w', 'grad_mla_g_qn': 'grad_w', 'grad_mla_g_kn': 'grad_w', 'grad_mla_w_o': 'grad_w', 'grad_dil_w_qkv': 'grad_w', 'grad_dil_g_qn': 'grad_w', 'grad_dil_g_kn': 'grad_w', 'grad_dil_w_o': 'grad_w', 'delta_ffn1_norm': 'delta_w', 'delta_ffn1_w_in': 'delta_w', 'delta_ffn1_w_out': 'delta_w', 'delta_mix_norm': 'delta_w', 'delta_ffn2_norm': 'delta_w', 'delta_ffn2_w_in': 'delta_w', 'delta_ffn2_w_out': 'delta_w', 'delta_mla_w_down': 'delta_w', 'delta_mla_g_cq': 'delta_w', 'delta_mla_g_ckv': 'delta_w', 'delta_mla_w_uq': 'delta_w', 'delta_mla_w_ukv': 'delta_w', 'delta_mla_g_qn': 'delta_w', 'delta_mla_g_kn': 'delta_w', 'delta_mla_w_o': 'delta_w', 'delta_dil_w_qkv': 'delta_w', 'delta_dil_g_qn': 'delta_w', 'delta_dil_g_kn': 'delta_w', 'delta_dil_w_o': 'delta_w', 'new_m_ffn1_norm': 'new_m', 'new_m_ffn1_w_in': 'new_m', 'new_m_ffn1_w_out': 'new_m', 'new_m_mix_norm': 'new_m', 'new_m_ffn2_norm': 'new_m', 'new_m_ffn2_w_in': 'new_m', 'new_m_ffn2_w_out': 'new_m', 'new_m_mla_w_down': 'new_m', 'new_m_mla_g_cq': 'new_m', 'new_m_mla_g_ckv': 'new_m', 'new_m_mla_w_uq': 'new_m', 'new_m_mla_w_ukv': 'new_m', 'new_m_mla_g_qn': 'new_m', 'new_m_mla_g_kn': 'new_m', 'new_m_mla_w_o': 'new_m', 'new_m_dil_w_qkv': 'new_m', 'new_m_dil_g_qn': 'new_m', 'new_m_dil_g_kn': 'new_m', 'new_m_dil_w_o': 'new_m', 'new_v_ffn1_norm': 'new_v', 'new_v_ffn1_w_in': 'new_v', 'new_v_ffn1_w_out': 'new_v', 'new_v_mix_norm': 'new_v', 'new_v_ffn2_norm': 'new_v', 'new_v_ffn2_w_in': 'new_v', 'new_v_ffn2_w_out': 'new_v', 'new_v_mla_w_down': 'new_v', 'new_v_mla_g_cq': 'new_v', 'new_v_mla_g_ckv': 'new_v', 'new_v_mla_w_uq': 'new_v', 'new_v_mla_w_ukv': 'new_v', 'new_v_mla_g_qn': 'new_v', 'new_v_mla_g_kn': 'new_v', 'new_v_mla_w_o': 'new_v', 'new_v_dil_w_qkv': 'new_v', 'new_v_dil_g_qn': 'new_v', 'new_v_dil_g_kn': 'new_v', 'new_v_dil_w_o': 'new_v'}


def _forward(args):
    return _fwd_reference(*[args[k] for k in FWD_PARAMS])


def _output_shape():
    def fwd():
        inp = _fwd_setup_inputs(0)
        return _fwd_reference(*[inp[k] for k in FWD_PARAMS])
    out = _jax.eval_shape(fwd)
    return out.shape, out.dtype

N_MICROBATCH = 1
ADAM_LR = 0.001
ADAM_B1 = 0.9
ADAM_B2 = 0.999
ADAM_EPS = 1e-08
ADAM_WD = 0.01
ADAM_STEP = 10
PER_EXAMPLE_BATCH_AXIS = {'x': 0, 'loss_target': 0}
SHARED_INPUTS = []
_WEIGHT_DTYPES = {'ffn1_norm': _jnp.float32, 'ffn1_w_in': _jnp.float32, 'ffn1_w_out': _jnp.float32, 'mix_norm': _jnp.float32, 'ffn2_norm': _jnp.float32, 'ffn2_w_in': _jnp.float32, 'ffn2_w_out': _jnp.float32, 'mla_w_down': _jnp.float32, 'mla_g_cq': _jnp.float32, 'mla_g_ckv': _jnp.float32, 'mla_w_uq': _jnp.float32, 'mla_w_ukv': _jnp.float32, 'mla_g_qn': _jnp.float32, 'mla_g_kn': _jnp.float32, 'mla_w_o': _jnp.float32, 'dil_w_qkv': _jnp.float32, 'dil_g_qn': _jnp.float32, 'dil_g_kn': _jnp.float32, 'dil_w_o': _jnp.float32}
MOMENT_SCALE = {'ffn1_norm': 3.082423e+00, 'ffn1_w_in': 3.942435e-02, 'ffn1_w_out': 6.339103e-02, 'mix_norm': 7.887933e-01, 'ffn2_norm': 3.089597e+00, 'ffn2_w_in': 3.395382e-02, 'ffn2_w_out': 5.728281e-02, 'mla_w_down': 1.339226e-01, 'mla_g_cq': 8.867431e-02, 'mla_g_ckv': 8.548316e-01, 'mla_w_uq': 3.762312e-02, 'mla_w_ukv': 5.519765e-02, 'mla_g_qn': 1.164304e+00, 'mla_g_kn': 1.163151e+00, 'mla_w_o': 6.314622e-02, 'dil_w_qkv': 5.843794e-02, 'dil_g_qn': 1.071070e+01, 'dil_g_kn': 1.069492e+01, 'dil_w_o': 6.053600e-02}


def _to_microbatches(a, axis):
    t = _jnp.moveaxis(a, axis, 0)
    t = t.reshape((N_MICROBATCH, t.shape[0] // N_MICROBATCH) + t.shape[1:])
    return _jnp.moveaxis(t, 1, axis + 1)


def setup_inputs(seed: int = 0) -> dict:
    inp = _fwd_setup_inputs(seed)
    key = _jax.random.fold_in(_jax.random.key(seed), 7919)
    shape, _ = _output_shape()
    out = dict(inp)
    out["loss_target"] = _jax.random.normal(_jax.random.fold_in(key, 0), shape, _jnp.float32)
    for i, name in enumerate(TWIN_WEIGHTS):
        w = inp[name].astype(_jnp.float32)
        if MOMENT_SCALE is None:
            s = _jnp.sqrt(_jnp.mean(_jnp.square(w)) + 1e-30)
        else:
            s = MOMENT_SCALE[name]
        km, kv = _jax.random.split(_jax.random.fold_in(key, i + 1))
        out[name] = w
        out["m_" + name] = s * _jax.random.normal(km, w.shape, _jnp.float32)
        out["v_" + name] = (s * s) * _jax.random.uniform(kv, w.shape, _jnp.float32, 0.5, 1.5)
    if N_MICROBATCH > 1:
        for name, axis in PER_EXAMPLE_BATCH_AXIS.items():
            out[name] = _to_microbatches(out[name], axis)
    return {'x': out['x'], 'ffn1_norm': out['ffn1_norm'], 'ffn1_w_in': out['ffn1_w_in'], 'ffn1_w_out': out['ffn1_w_out'], 'mix_norm': out['mix_norm'], 'ffn2_norm': out['ffn2_norm'], 'ffn2_w_in': out['ffn2_w_in'], 'ffn2_w_out': out['ffn2_w_out'], 'mla_w_down': out['mla_w_down'], 'mla_g_cq': out['mla_g_cq'], 'mla_g_ckv': out['mla_g_ckv'], 'mla_w_uq': out['mla_w_uq'], 'mla_w_ukv': out['mla_w_ukv'], 'mla_g_qn': out['mla_g_qn'], 'mla_g_kn': out['mla_g_kn'], 'mla_w_o': out['mla_w_o'], 'dil_w_qkv': out['dil_w_qkv'], 'dil_g_qn': out['dil_g_qn'], 'dil_g_kn': out['dil_g_kn'], 'dil_w_o': out['dil_w_o'], 'loss_target': out['loss_target'], 'm_ffn1_norm': out['m_ffn1_norm'], 'm_ffn1_w_in': out['m_ffn1_w_in'], 'm_ffn1_w_out': out['m_ffn1_w_out'], 'm_mix_norm': out['m_mix_norm'], 'm_ffn2_norm': out['m_ffn2_norm'], 'm_ffn2_w_in': out['m_ffn2_w_in'], 'm_ffn2_w_out': out['m_ffn2_w_out'], 'm_mla_w_down': out['m_mla_w_down'], 'm_mla_g_cq': out['m_mla_g_cq'], 'm_mla_g_ckv': out['m_mla_g_ckv'], 'm_mla_w_uq': out['m_mla_w_uq'], 'm_mla_w_ukv': out['m_mla_w_ukv'], 'm_mla_g_qn': out['m_mla_g_qn'], 'm_mla_g_kn': out['m_mla_g_kn'], 'm_mla_w_o': out['m_mla_w_o'], 'm_dil_w_qkv': out['m_dil_w_qkv'], 'm_dil_g_qn': out['m_dil_g_qn'], 'm_dil_g_kn': out['m_dil_g_kn'], 'm_dil_w_o': out['m_dil_w_o'], 'v_ffn1_norm': out['v_ffn1_norm'], 'v_ffn1_w_in': out['v_ffn1_w_in'], 'v_ffn1_w_out': out['v_ffn1_w_out'], 'v_mix_norm': out['v_mix_norm'], 'v_ffn2_norm': out['v_ffn2_norm'], 'v_ffn2_w_in': out['v_ffn2_w_in'], 'v_ffn2_w_out': out['v_ffn2_w_out'], 'v_mla_w_down': out['v_mla_w_down'], 'v_mla_g_cq': out['v_mla_g_cq'], 'v_mla_g_ckv': out['v_mla_g_ckv'], 'v_mla_w_uq': out['v_mla_w_uq'], 'v_mla_w_ukv': out['v_mla_w_ukv'], 'v_mla_g_qn': out['v_mla_g_qn'], 'v_mla_g_kn': out['v_mla_g_kn'], 'v_mla_w_o': out['v_mla_w_o'], 'v_dil_w_qkv': out['v_dil_w_qkv'], 'v_dil_g_qn': out['v_dil_g_qn'], 'v_dil_g_kn': out['v_dil_g_kn'], 'v_dil_w_o': out['v_dil_w_o']}


def _loss(weights, diff, rest, loss_target):
    with _jax.named_scope("forward"):
        args = {**rest, TWIN_DIFF_INPUT: diff, **{k: w.astype(_WEIGHT_DTYPES[k]) for k, w in weights.items()}}
        y = _forward(args)
    with _jax.named_scope("loss_head"):
        err = _jnp.square(y.astype(_jnp.float32) - loss_target)
        return 0.5 * _jnp.sum(_jnp.mean(err, axis=-1)) if err.ndim else 0.5 * err


def _adamw(w, g, m, v):
    m = ADAM_B1 * m + (1.0 - ADAM_B1) * g
    v = ADAM_B2 * v + (1.0 - ADAM_B2) * _jnp.square(g)
    m_hat = m / (1.0 - ADAM_B1 ** ADAM_STEP)
    v_hat = v / (1.0 - ADAM_B2 ** ADAM_STEP)
    delta = -ADAM_LR * (m_hat / (_jnp.sqrt(v_hat) + ADAM_EPS) + ADAM_WD * w)
    return delta, m, v


def reference(x, ffn1_norm, ffn1_w_in, ffn1_w_out, mix_norm, ffn2_norm, ffn2_w_in, ffn2_w_out, mla_w_down, mla_g_cq, mla_g_ckv, mla_w_uq, mla_w_ukv, mla_g_qn, mla_g_kn, mla_w_o, dil_w_qkv, dil_g_qn, dil_g_kn, dil_w_o, loss_target, m_ffn1_norm, m_ffn1_w_in, m_ffn1_w_out, m_mix_norm, m_ffn2_norm, m_ffn2_w_in, m_ffn2_w_out, m_mla_w_down, m_mla_g_cq, m_mla_g_ckv, m_mla_w_uq, m_mla_w_ukv, m_mla_g_qn, m_mla_g_kn, m_mla_w_o, m_dil_w_qkv, m_dil_g_qn, m_dil_g_kn, m_dil_w_o, v_ffn1_norm, v_ffn1_w_in, v_ffn1_w_out, v_mix_norm, v_ffn2_norm, v_ffn2_w_in, v_ffn2_w_out, v_mla_w_down, v_mla_g_cq, v_mla_g_ckv, v_mla_w_uq, v_mla_w_ukv, v_mla_g_qn, v_mla_g_kn, v_mla_w_o, v_dil_w_qkv, v_dil_g_qn, v_dil_g_kn, v_dil_w_o):
    given = dict(x=x, ffn1_norm=ffn1_norm, ffn1_w_in=ffn1_w_in, ffn1_w_out=ffn1_w_out, mix_norm=mix_norm, ffn2_norm=ffn2_norm, ffn2_w_in=ffn2_w_in, ffn2_w_out=ffn2_w_out, mla_w_down=mla_w_down, mla_g_cq=mla_g_cq, mla_g_ckv=mla_g_ckv, mla_w_uq=mla_w_uq, mla_w_ukv=mla_w_ukv, mla_g_qn=mla_g_qn, mla_g_kn=mla_g_kn, mla_w_o=mla_w_o, dil_w_qkv=dil_w_qkv, dil_g_qn=dil_g_qn, dil_g_kn=dil_g_kn, dil_w_o=dil_w_o, loss_target=loss_target, m_ffn1_norm=m_ffn1_norm, m_ffn1_w_in=m_ffn1_w_in, m_ffn1_w_out=m_ffn1_w_out, m_mix_norm=m_mix_norm, m_ffn2_norm=m_ffn2_norm, m_ffn2_w_in=m_ffn2_w_in, m_ffn2_w_out=m_ffn2_w_out, m_mla_w_down=m_mla_w_down, m_mla_g_cq=m_mla_g_cq, m_mla_g_ckv=m_mla_g_ckv, m_mla_w_uq=m_mla_w_uq, m_mla_w_ukv=m_mla_w_ukv, m_mla_g_qn=m_mla_g_qn, m_mla_g_kn=m_mla_g_kn, m_mla_w_o=m_mla_w_o, m_dil_w_qkv=m_dil_w_qkv, m_dil_g_qn=m_dil_g_qn, m_dil_g_kn=m_dil_g_kn, m_dil_w_o=m_dil_w_o, v_ffn1_norm=v_ffn1_norm, v_ffn1_w_in=v_ffn1_w_in, v_ffn1_w_out=v_ffn1_w_out, v_mix_norm=v_mix_norm, v_ffn2_norm=v_ffn2_norm, v_ffn2_w_in=v_ffn2_w_in, v_ffn2_w_out=v_ffn2_w_out, v_mla_w_down=v_mla_w_down, v_mla_g_cq=v_mla_g_cq, v_mla_g_ckv=v_mla_g_ckv, v_mla_w_uq=v_mla_w_uq, v_mla_w_ukv=v_mla_w_ukv, v_mla_g_qn=v_mla_g_qn, v_mla_g_kn=v_mla_g_kn, v_mla_w_o=v_mla_w_o, v_dil_w_qkv=v_dil_w_qkv, v_dil_g_qn=v_dil_g_qn, v_dil_g_kn=v_dil_g_kn, v_dil_w_o=v_dil_w_o)
    weights = {n: given[n] for n in TWIN_WEIGHTS}
    shared = {n: given[n] for n in SHARED_INPUTS}
    per_example = {n: given[n] for n in ['x']}
    grad_fn = _jax.value_and_grad(_loss, argnums=(0, 1))

    def one_microbatch(ex, loss_target):
        ex = dict(ex)
        diff = ex.pop(TWIN_DIFF_INPUT)
        return grad_fn(weights, diff, {**shared, **ex}, loss_target)

    if N_MICROBATCH == 1:
        loss, (grad_w, grad_x) = one_microbatch(per_example, given["loss_target"])
    else:
        def body(carry, xs):
            loss_sum, grad_sum = carry
            l_k, (gw_k, gx_k) = one_microbatch(xs[0], xs[1])
            with _jax.named_scope("update"):
                return (loss_sum + l_k, _jax.tree.map(_jnp.add, grad_sum, gw_k)), gx_k

        init = (_jnp.zeros((), _jnp.float32), _jax.tree.map(_jnp.zeros_like, weights))
        (loss, grad_w), grad_x = _jax.lax.scan(body, init, (per_example, given["loss_target"]))
    with _jax.named_scope("update"):
        delta_w, new_m, new_v = {}, {}, {}
        for n in TWIN_WEIGHTS:
            delta_w[n], new_m[n], new_v[n] = _adamw(weights[n], grad_w[n], given["m_" + n], given["v_" + n])
    return (loss, grad_x, *[grad_w[n] for n in TWIN_WEIGHTS], *[delta_w[n] for n in TWIN_WEIGHTS],
            *[new_m[n] for n in TWIN_WEIGHTS], *[new_v[n] for n in TWIN_WEIGHTS])
```

```python
import functools
import math

import jax
import jax.numpy as jnp
import numpy as np
from jax import lax
from jax.experimental import pallas as pl
from jax.experimental.pallas import tpu as pltpu

EPS = 1e-6
MLA_HEADS = 16
Q_LORA = 512
KV_LORA = 512
NOPE_DIM = 128
ROPE_DIM = 64
V_DIM = 128
QK_DIM = NOPE_DIM + ROPE_DIM
ROPE_THETA = 10000.0
HEAD_PAD = 256
LAT_PAD = Q_LORA + KV_LORA + 128
DIL_PAIRS = ((128, 1), (512, 4), (2048, 16))
DIL_GROUPS = 3
DIL_HEADS = 8
DIL_HEAD_DIM = 128
DIL_BLK = 128
ADAM_LR = 0.001
ADAM_B1 = 0.9
ADAM_B2 = 0.999
ADAM_EPS = 1e-08
ADAM_WD = 0.01
ADAM_STEP = 10

N_DEV = 8
MESH_AXES = ("x", "y", "c")
MESH = pl.DeviceIdType.MESH
NEG_BIG = -1e30
VMEM_LIMIT_V7X = 56 * 1024 * 1024
LANES = 128

BF16 = jnp.bfloat16
F32 = jnp.float32


def _pick(n, cands):
    for c in cands:
        if n % c == 0:
            return c
    raise ValueError(f"no tile for {n}")


def _params(sem):
    return pltpu.CompilerParams(dimension_semantics=sem, vmem_limit_bytes=VMEM_LIMIT_V7X)


def _mm(a, b, mode, out_dtype, name, *, scale=1.0, res=None, layer=None):
    b2 = b.shape[-2:]
    if mode == "nn":
        (M, K), (Kb, N) = a.shape, b2
    elif mode == "nt":
        (M, K), (N, Kb) = a.shape, b2
    else:
        (K, M), (Kb, N) = a.shape, b2
    assert K == Kb, (a.shape, b.shape, mode)
    tm = _pick(M, (1024, 512, 384, 256, 128))
    tn = _pick(N, (1024, 512, 384, 256, 128))
    tk = _pick(K, (512, 384, 256, 128))
    nk = K // tk
    dims = {"nn": (((1,), (0,)), ((), ())), "nt": (((1,), (1,)), ((), ())), "tn": (((0,), (0,)), ((), ()))}[mode]

    def body(*refs):
        if res is None:
            a_ref, b_ref, o_ref, acc = refs
            r_ref = None
        else:
            a_ref, b_ref, r_ref, o_ref, acc = refs
        k = pl.program_id(2)

        @pl.when(k == 0)
        def _():
            acc[...] = jnp.zeros_like(acc)

        acc[...] += lax.dot_general(a_ref[...].astype(BF16), b_ref[...].astype(BF16), dims,
                                    preferred_element_type=F32)

        @pl.when(k == nk - 1)
        def _():
            v = acc[...]
            if scale != 1.0:
                v = v * scale
            if r_ref is not None:
                v = r_ref[...] + v
            o_ref[...] = v.astype(o_ref.dtype)

    a_spec = pl.BlockSpec((tk, tm), lambda i, j, k: (k, i)) if mode == "tn" else pl.BlockSpec((tm, tk), lambda i, j, k: (i, k))
    if mode == "nt":
        bshape, bidx = (tn, tk), (lambda i, j, k: (j, k))
    else:
        bshape, bidx = (tk, tn), (lambda i, j, k: (k, j))
    if b.ndim == 3:
        b_spec = pl.BlockSpec((None,) + bshape, lambda i, j, k: (layer,) + bidx(i, j, k))
    else:
        b_spec = pl.BlockSpec(bshape, bidx)
    in_specs = [a_spec, b_spec]
    args = [a, b]
    if res is not None:
        in_specs.append(pl.BlockSpec((tm, tn), lambda i, j, k: (i, j)))
        args.append(res)
    return pl.pallas_call(
        body, name=name, grid=(M // tm, N // tn, nk),
        in_specs=in_specs, out_specs=pl.BlockSpec((tm, tn), lambda i, j, k: (i, j)),
        out_shape=jax.ShapeDtypeStruct((M, N), out_dtype),
        scratch_shapes=[pltpu.VMEM((tm, tn), F32)],
        compiler_params=_params(("parallel", "parallel", "arbitrary")),
    )(*args)


def _cast_bf16(w, name):
    L, R, C = w.shape
    w2 = w.reshape(L * R, C)
    tr = _pick(L * R, (512, 256, 128, 64, 32, 16))

    def body(w_ref, o_ref):
        o_ref[...] = w_ref[...].astype(BF16)

    out = pl.pallas_call(
        body, name=name, grid=(L * R // tr,),
        in_specs=[pl.BlockSpec((tr, C), lambda i: (i, 0))], out_specs=pl.BlockSpec((tr, C), lambda i: (i, 0)),
        out_shape=jax.ShapeDtypeStruct((L * R, C), BF16), compiler_params=_params(("parallel",)),
    )(w2)
    return out.reshape(L, R, C)


def _rms_fwd(x, g, name):
    T, D = x.shape
    tr = _pick(T, (512, 256, 128))

    def body(x_ref, g_ref, o_ref):
        xv = x_ref[...]
        r = lax.rsqrt(jnp.mean(xv * xv, axis=-1, keepdims=True) + EPS)
        o_ref[...] = ((xv * r) * g_ref[...]).astype(BF16)

    return pl.pallas_call(
        body, name=name, grid=(T // tr,),
        in_specs=[pl.BlockSpec((tr, D), lambda i: (i, 0)), pl.BlockSpec((1, D), lambda i: (0, 0))],
        out_specs=pl.BlockSpec((tr, D), lambda i: (i, 0)),
        out_shape=jax.ShapeDtypeStruct((T, D), BF16), compiler_params=_params(("parallel",)),
    )(x, g)


def _rms_bwd(x, g, dh, dres, name):
    T, D = x.shape
    tr = _pick(T, (256, 128))

    def body(x_ref, g_ref, dh_ref, dres_ref, dx_ref, dg_ref):
        xv = x_ref[...]
        dhv = dh_ref[...]
        r = lax.rsqrt(jnp.mean(xv * xv, axis=-1, keepdims=True) + EPS)
        xhat = xv * r
        dxh = dhv * g_ref[...]
        c = jnp.mean(dxh * xhat, axis=-1, keepdims=True)
        dx_ref[...] = dres_ref[...] + r * (dxh - xhat * c)

        @pl.when(pl.program_id(0) == 0)
        def _():
            dg_ref[...] = jnp.zeros_like(dg_ref)

        dg_ref[...] += jnp.sum(dhv * xhat, axis=0, keepdims=True)

    row = pl.BlockSpec((tr, D), lambda i: (i, 0))
    vec = pl.BlockSpec((1, D), lambda i: (0, 0))
    return pl.pallas_call(
        body, name=name, grid=(T // tr,),
        in_specs=[row, vec, row, row], out_specs=[row, vec],
        out_shape=[jax.ShapeDtypeStruct((T, D), F32), jax.ShapeDtypeStruct((1, D), F32)],
        compiler_params=_params(("arbitrary",)),
    )(x, g, dh, dres)


def _swiglu_fwd(u, name):
    T, F2 = u.shape
    F = F2 // 2
    tr = _pick(T, (512, 256, 128))
    tf = _pick(F, (512, 256, 128))
    nf = F // tf

    def body(g_ref, u_ref, o_ref):
        z = g_ref[...]
        o_ref[...] = (z * jax.nn.sigmoid(z) * u_ref[...]).astype(BF16)

    return pl.pallas_call(
        body, name=name, grid=(T // tr, nf),
        in_specs=[pl.BlockSpec((tr, tf), lambda i, j: (i, j)), pl.BlockSpec((tr, tf), lambda i, j: (i, j + nf))],
        out_specs=pl.BlockSpec((tr, tf), lambda i, j: (i, j)),
        out_shape=jax.ShapeDtypeStruct((T, F), BF16), compiler_params=_params(("parallel", "parallel")),
    )(u, u)


def _swiglu_bwd(u, da, name):
    T, F2 = u.shape
    F = F2 // 2
    tr = _pick(T, (128,))
    tf = _pick(F, (512, 256, 128))

    def body(u_ref, da_ref, du_ref):
        for j in range(F // tf):
            z = u_ref[:, j * tf:(j + 1) * tf]
            up = u_ref[:, F + j * tf:F + (j + 1) * tf]
            dav = da_ref[:, j * tf:(j + 1) * tf]
            sg = jax.nn.sigmoid(z)
            silu = z * sg
            du_ref[:, j * tf:(j + 1) * tf] = (dav * up * (sg + silu * (1.0 - sg))).astype(BF16)
            du_ref[:, F + j * tf:F + (j + 1) * tf] = (dav * silu).astype(BF16)

    return pl.pallas_call(
        body, name=name, grid=(T // tr,),
        in_specs=[pl.BlockSpec((tr, F2), lambda i: (i, 0)), pl.BlockSpec((tr, F), lambda i: (i, 0))],
        out_specs=pl.BlockSpec((tr, F2), lambda i: (i, 0)),
        out_shape=jax.ShapeDtypeStruct((T, F2), BF16), compiler_params=_params(("parallel",)),
    )(u, da)


def _loss_head(y, t, name):
    T, D = y.shape
    tr = _pick(T, (512, 256, 128))

    def body(y_ref, t_ref, dy_ref, l_ref):
        e = y_ref[...] - t_ref[...]
        dy_ref[...] = e * (1.0 / D)

        @pl.when(pl.program_id(0) == 0)
        def _():
            l_ref[...] = jnp.zeros_like(l_ref)

        l_ref[...] += 0.5 * jnp.sum(jnp.mean(e * e, axis=-1, keepdims=True), axis=0, keepdims=True)

    row = pl.BlockSpec((tr, D), lambda i: (i, 0))
    return pl.pallas_call(
        body, name=name, grid=(T // tr,),
        in_specs=[row, row], out_specs=[row, pl.BlockSpec((1, 1), lambda i: (0, 0))],
        out_shape=[jax.ShapeDtypeStruct((T, D), F32), jax.ShapeDtypeStruct((1, 1), F32)],
        compiler_params=_params(("arbitrary",)),
    )(y, t)


def _rope_tables(S):
    half = ROPE_DIM // 2
    inv = 1.0 / (ROPE_THETA ** (jnp.arange(0, ROPE_DIM, 2, dtype=F32) / ROPE_DIM))
    ang = jnp.arange(S, dtype=F32)[:, None] * inv[None, :]
    cos, sin = jnp.cos(ang), jnp.sin(ang)
    z = jnp.zeros((S, half), F32)
    z2 = jnp.zeros((S, LANES - ROPE_DIM), F32)
    c = jnp.concatenate([cos, cos, z2], axis=1)
    s1 = jnp.concatenate([-sin, z, z2], axis=1)
    s2 = jnp.concatenate([z, sin, z2], axis=1)
    return c, s1, s2


def _rope(r, c, s1, s2):
    return r * c + pltpu.roll(r, LANES - ROPE_DIM // 2, 1) * s1 + pltpu.roll(r, ROPE_DIM // 2, 1) * s2


def _rope_t(d, c, s1, s2):
    return d * c + pltpu.roll(d * s1, ROPE_DIM // 2, 1) + pltpu.roll(d * s2, LANES - ROPE_DIM // 2, 1)


def _lat_norm_fwd(lat, g_cq, g_ckv, name):
    T = lat.shape[0]
    tr = _pick(T, (512, 256, 128))

    def body(lat_ref, gq_ref, gk_ref, cq_ref, ckv_ref):
        for off, g_ref, o_ref in ((0, gq_ref, cq_ref), (Q_LORA, gk_ref, ckv_ref)):
            xv = lat_ref[:, off:off + Q_LORA]
            r = lax.rsqrt(jnp.mean(xv * xv, axis=-1, keepdims=True) + EPS)
            o_ref[...] = ((xv * r) * g_ref[...]).astype(BF16)

    vec = pl.BlockSpec((1, Q_LORA), lambda i: (0, 0))
    out = pl.BlockSpec((tr, Q_LORA), lambda i: (i, 0))
    return pl.pallas_call(
        body, name=name, grid=(T // tr,),
        in_specs=[pl.BlockSpec((tr, LAT_PAD), lambda i: (i, 0)), vec, vec], out_specs=[out, out],
        out_shape=[jax.ShapeDtypeStruct((T, Q_LORA), BF16)] * 2, compiler_params=_params(("parallel",)),
    )(lat, g_cq, g_ckv)


def _lat_norm_bwd(lat, g_cq, g_ckv, dcq, dckv, dkpe, name):
    T = lat.shape[0]
    tr = _pick(T, (256, 128))

    def body(lat_ref, gq_ref, gk_ref, dcq_ref, dckv_ref, dkpe_ref, dlat_ref, dgq_ref, dgk_ref):
        @pl.when(pl.program_id(0) == 0)
        def _():
            dgq_ref[...] = jnp.zeros_like(dgq_ref)
            dgk_ref[...] = jnp.zeros_like(dgk_ref)

        for off, g_ref, d_ref, dg_ref in ((0, gq_ref, dcq_ref, dgq_ref), (Q_LORA, gk_ref, dckv_ref, dgk_ref)):
            xv = lat_ref[:, off:off + Q_LORA]
            dv = d_ref[...]
            r = lax.rsqrt(jnp.mean(xv * xv, axis=-1, keepdims=True) + EPS)
            xhat = xv * r
            dxh = dv * g_ref[...]
            c = jnp.mean(dxh * xhat, axis=-1, keepdims=True)
            dlat_ref[:, off:off + Q_LORA] = (r * (dxh - xhat * c)).astype(BF16)
            dg_ref[...] += jnp.sum(dv * xhat, axis=0, keepdims=True)
        dlat_ref[:, Q_LORA + KV_LORA:] = dkpe_ref[...].astype(BF16)

    vec = pl.BlockSpec((1, Q_LORA), lambda i: (0, 0))
    half = pl.BlockSpec((tr, Q_LORA), lambda i: (i, 0))
    full = pl.BlockSpec((tr, LAT_PAD), lambda i: (i, 0))
    return pl.pallas_call(
        body, name=name, grid=(T // tr,),
        in_specs=[full, vec, vec, half, half, pl.BlockSpec((tr, LANES), lambda i: (i, 0))],
        out_specs=[full, vec, vec],
        out_shape=[jax.ShapeDtypeStruct((T, LAT_PAD), BF16), jax.ShapeDtypeStruct((1, Q_LORA), F32),
                   jax.ShapeDtypeStruct((1, Q_LORA), F32)],
        compiler_params=_params(("arbitrary",)),
    )(lat, g_cq, g_ckv, dcq, dckv, dkpe)


def _mla_prep_fwd(q_raw, kv, lat, g_qn, g_kn, tabs, name):
    T = q_raw.shape[0]
    H = MLA_HEADS
    tr = _pick(T, (256, 128))

    def body(q_ref, kv_ref, kpe_ref, gq_ref, gk_ref, c_ref, s1_ref, s2_ref, qf_ref, kf_ref, v_ref):
        c, s1, s2 = c_ref[...], s1_ref[...], s2_ref[...]
        gq, gk = gq_ref[...], gk_ref[...]
        kpe = kpe_ref[...]
        kpe_ss = jnp.sum(kpe * kpe, axis=-1, keepdims=True)
        for h in range(H):
            lo = h * HEAD_PAD
            qa = q_ref[:, lo:lo + LANES]
            qb = q_ref[:, lo + LANES:lo + HEAD_PAD]
            ss = jnp.sum(qa * qa, axis=-1, keepdims=True) + jnp.sum(qb * qb, axis=-1, keepdims=True)
            r = lax.rsqrt(ss * (1.0 / QK_DIM) + EPS)
            qf_ref[:, lo:lo + LANES] = (qa * r * gq[:, :LANES]).astype(BF16)
            qf_ref[:, lo + LANES:lo + HEAD_PAD] = _rope(qb * r * gq[:, LANES:], c, s1, s2).astype(BF16)
            ka = kv_ref[:, lo:lo + LANES]
            ss = jnp.sum(ka * ka, axis=-1, keepdims=True) + kpe_ss
            r = lax.rsqrt(ss * (1.0 / QK_DIM) + EPS)
            kf_ref[:, lo:lo + LANES] = (ka * r * gk[:, :LANES]).astype(BF16)
            kf_ref[:, lo + LANES:lo + HEAD_PAD] = _rope(kpe * r * gk[:, LANES:], c, s1, s2).astype(BF16)
            v_ref[:, h * V_DIM:(h + 1) * V_DIM] = kv_ref[:, lo + LANES:lo + HEAD_PAD].astype(BF16)

    wide = pl.BlockSpec((tr, H * HEAD_PAD), lambda i: (i, 0))
    lane = pl.BlockSpec((tr, LANES), lambda i: (i, 0))
    gvec = pl.BlockSpec((1, HEAD_PAD), lambda i: (0, 0))
    return pl.pallas_call(
        body, name=name, grid=(T // tr,),
        in_specs=[wide, wide, pl.BlockSpec((tr, LANES), lambda i: (i, (Q_LORA + KV_LORA) // LANES)), gvec, gvec,
                  lane, lane, lane],
        out_specs=[wide, wide, pl.BlockSpec((tr, H * V_DIM), lambda i: (i, 0))],
        out_shape=[jax.ShapeDtypeStruct((T, H * HEAD_PAD), BF16), jax.ShapeDtypeStruct((T, H * HEAD_PAD), BF16),
                   jax.ShapeDtypeStruct((T, H * V_DIM), BF16)],
        compiler_params=_params(("parallel",)),
    )(q_raw, kv, lat, g_qn, g_kn, *tabs)


def _mla_prep_bwd(q_raw, kv, lat, g_qn, g_kn, tabs, dqf, dkf, dv, name):
    T = q_raw.shape[0]
    H = MLA_HEADS
    tr = _pick(T, (128,))

    def body(q_ref, kv_ref, kpe_ref, gq_ref, gk_ref, c_ref, s1_ref, s2_ref, dqf_ref, dkf_ref, dv_ref,
             dq_ref, dkv_ref, dkpe_ref, dgq_ref, dgk_ref):
        @pl.when(pl.program_id(0) == 0)
        def _():
            dgq_ref[...] = jnp.zeros_like(dgq_ref)
            dgk_ref[...] = jnp.zeros_like(dgk_ref)

        c, s1, s2 = c_ref[...], s1_ref[...], s2_ref[...]
        gq, gk = gq_ref[...], gk_ref[...]
        kpe = kpe_ref[...]
        kpe_ss = jnp.sum(kpe * kpe, axis=-1, keepdims=True)
        dkpe = jnp.zeros_like(kpe)
        dgq_a = jnp.zeros((1, LANES), F32)
        dgq_b = jnp.zeros((1, LANES), F32)
        dgk_a = jnp.zeros((1, LANES), F32)
        dgk_b = jnp.zeros((1, LANES), F32)
        for h in range(H):
            lo = h * HEAD_PAD
            xa = q_ref[:, lo:lo + LANES]
            xb = q_ref[:, lo + LANES:lo + HEAD_PAD]
            ss = jnp.sum(xa * xa, axis=-1, keepdims=True) + jnp.sum(xb * xb, axis=-1, keepdims=True)
            r = lax.rsqrt(ss * (1.0 / QK_DIM) + EPS)
            xa, xb = xa * r, xb * r
            da = dqf_ref[:, lo:lo + LANES]
            db = _rope_t(dqf_ref[:, lo + LANES:lo + HEAD_PAD], c, s1, s2)
            dgq_a += jnp.sum(da * xa, axis=0, keepdims=True)
            dgq_b += jnp.sum(db * xb, axis=0, keepdims=True)
            da, db = da * gq[:, :LANES], db * gq[:, LANES:]
            cc = (jnp.sum(da * xa, axis=-1, keepdims=True) + jnp.sum(db * xb, axis=-1, keepdims=True)) * (1.0 / QK_DIM)
            dq_ref[:, lo:lo + LANES] = (r * (da - xa * cc)).astype(BF16)
            dq_ref[:, lo + LANES:lo + HEAD_PAD] = (r * (db - xb * cc)).astype(BF16)
            xa = kv_ref[:, lo:lo + LANES]
            ss = jnp.sum(xa * xa, axis=-1, keepdims=True) + kpe_ss
            r = lax.rsqrt(ss * (1.0 / QK_DIM) + EPS)
            xa, xb = xa * r, kpe * r
            da = dkf_ref[:, lo:lo + LANES]
            db = _rope_t(dkf_ref[:, lo + LANES:lo + HEAD_PAD], c, s1, s2)
            dgk_a += jnp.sum(da * xa, axis=0, keepdims=True)
            dgk_b += jnp.sum(db * xb, axis=0, keepdims=True)
            da, db = da * gk[:, :LANES], db * gk[:, LANES:]
            cc = (jnp.sum(da * xa, axis=-1, keepdims=True) + jnp.sum(db * xb, axis=-1, keepdims=True)) * (1.0 / QK_DIM)
            dkv_ref[:, lo:lo + LANES] = (r * (da - xa * cc)).astype(BF16)
            dkpe = dkpe + r * (db - xb * cc)
            dkv_ref[:, lo + LANES:lo + HEAD_PAD] = dv_ref[:, h * V_DIM:(h + 1) * V_DIM].astype(BF16)
        dkpe_ref[...] = dkpe
        dgq_ref[:, :LANES] += dgq_a
        dgq_ref[:, LANES:] += dgq_b
        dgk_ref[:, :LANES] += dgk_a
        dgk_ref[:, LANES:] += dgk_b

    wide = pl.BlockSpec((tr, H * HEAD_PAD), lambda i: (i, 0))
    lane = pl.BlockSpec((tr, LANES), lambda i: (i, 0))
    gvec = pl.BlockSpec((1, HEAD_PAD), lambda i: (0, 0))
    vspec = pl.BlockSpec((tr, H * V_DIM), lambda i: (i, 0))
    return pl.pallas_call(
        body, name=name, grid=(T // tr,),
        in_specs=[wide, wide, pl.BlockSpec((tr, LANES), lambda i: (i, (Q_LORA + KV_LORA) // LANES)), gvec, gvec,
                  lane, lane, lane, wide, wide, vspec],
        out_specs=[wide, wide, lane, gvec, gvec],
        out_shape=[jax.ShapeDtypeStruct((T, H * HEAD_PAD), BF16), jax.ShapeDtypeStruct((T, H * HEAD_PAD), BF16),
                   jax.ShapeDtypeStruct((T, LANES), F32), jax.ShapeDtypeStruct((1, HEAD_PAD), F32),
                   jax.ShapeDtypeStruct((1, HEAD_PAD), F32)],
        compiler_params=_params(("arbitrary",)),
    )(q_raw, kv, lat, g_qn, g_kn, *tabs, dqf, dkf, dv)


def _causal_mask(tq, tk):
    return lax.broadcasted_iota(jnp.int32, (tq, tk), 1) <= lax.broadcasted_iota(jnp.int32, (tq, tk), 0)


NT_DIMS = (((1,), (1,)), ((), ()))
TN_DIMS = (((0,), (0,)), ((), ()))


def _flash_fwd(qf, kf, v, name):
    T = qf.shape[0]
    H = MLA_HEADS
    t = _pick(T, (512, 256, 128))
    n = T // t
    scale = 1.0 / math.sqrt(QK_DIM)

    def body(q_ref, k_ref, v_ref, o_ref, lse_ref, m_sc, l_sc, acc_sc):
        i, j = pl.program_id(1), pl.program_id(2)

        @pl.when(j == 0)
        def _():
            m_sc[...] = jnp.full_like(m_sc, NEG_BIG)
            l_sc[...] = jnp.zeros_like(l_sc)
            acc_sc[...] = jnp.zeros_like(acc_sc)

        def step(masked):
            s = lax.dot_general(q_ref[...], k_ref[...], NT_DIMS, preferred_element_type=F32) * scale
            if masked:
                s = jnp.where(_causal_mask(t, t), s, NEG_BIG)
            m_prev = m_sc[:, :1]
            m_new = jnp.maximum(m_prev, jnp.max(s, axis=-1, keepdims=True))
            a = jnp.exp(m_prev - m_new)
            p = jnp.exp(s - m_new)
            l_sc[...] = a * l_sc[...] + jnp.sum(p, axis=-1, keepdims=True)
            acc_sc[...] = a * acc_sc[...] + jnp.dot(p.astype(BF16), v_ref[...], preferred_element_type=F32)
            m_sc[...] = jnp.broadcast_to(m_new, m_sc.shape)

        @pl.when(j < i)
        def _():
            step(False)

        @pl.when(j == i)
        def _():
            step(True)
            o_ref[...] = (acc_sc[...] / l_sc[...]).astype(BF16)
            lse_ref[...] = m_sc[...] + jnp.log(l_sc[...])

    return pl.pallas_call(
        body, name=name, grid=(H, n, n),
        in_specs=[pl.BlockSpec((t, HEAD_PAD), lambda h, i, j: (i, h)),
                  pl.BlockSpec((t, HEAD_PAD), lambda h, i, j: (jnp.minimum(j, i), h)),
                  pl.BlockSpec((t, V_DIM), lambda h, i, j: (jnp.minimum(j, i), h))],
        out_specs=[pl.BlockSpec((t, V_DIM), lambda h, i, j: (i, h)), pl.BlockSpec((t, V_DIM), lambda h, i, j: (i, h))],
        out_shape=[jax.ShapeDtypeStruct((T, H * V_DIM), BF16), jax.ShapeDtypeStruct((T, H * V_DIM), F32)],
        scratch_shapes=[pltpu.VMEM((t, LANES), F32), pltpu.VMEM((t, LANES), F32), pltpu.VMEM((t, V_DIM), F32)],
        compiler_params=_params(("parallel", "parallel", "arbitrary")),
    )(qf, kf, v)


def _attn_delta(do, o, name):
    T, W = do.shape
    nh = W // V_DIM
    tr = _pick(T, (512, 256, 128))

    def body(do_ref, o_ref, d_ref):
        for h in range(nh):
            sl = slice(h * V_DIM, (h + 1) * V_DIM)
            d = jnp.sum(do_ref[:, sl].astype(F32) * o_ref[:, sl].astype(F32), axis=-1, keepdims=True)
            d_ref[:, sl] = jnp.broadcast_to(d, (tr, V_DIM))

    row = pl.BlockSpec((tr, W), lambda i: (i, 0))
    return pl.pallas_call(
        body, name=name, grid=(T // tr,), in_specs=[row, row], out_specs=row,
        out_shape=jax.ShapeDtypeStruct((T, W), F32), compiler_params=_params(("parallel",)),
    )(do, o)


def _flash_bwd_dq(qf, kf, v, do, lse, delta, name):
    T = qf.shape[0]
    H = MLA_HEADS
    t = _pick(T, (512, 256, 128))
    n = T // t
    scale = 1.0 / math.sqrt(QK_DIM)

    def body(q_ref, k_ref, v_ref, do_ref, lse_ref, dl_ref, dq_ref, acc):
        i, j = pl.program_id(1), pl.program_id(2)

        @pl.when(j == 0)
        def _():
            acc[...] = jnp.zeros_like(acc)

        def step(masked):
            s = lax.dot_general(q_ref[...], k_ref[...], NT_DIMS, preferred_element_type=F32) * scale
            if masked:
                s = jnp.where(_causal_mask(t, t), s, NEG_BIG)
            p = jnp.exp(s - lse_ref[:, :1])
            dp = lax.dot_general(do_ref[...], v_ref[...], NT_DIMS, preferred_element_type=F32)
            ds = (p * (dp - dl_ref[:, :1])).astype(BF16)
            acc[...] += jnp.dot(ds, k_ref[...], preferred_element_type=F32)

        @pl.when(j < i)
        def _():
            step(False)

        @pl.when(j == i)
        def _():
            step(True)
            dq_ref[...] = acc[...] * scale

    qs = pl.BlockSpec((t, HEAD_PAD), lambda h, i, j: (i, h))
    ks = pl.BlockSpec((t, HEAD_PAD), lambda h, i, j: (jnp.minimum(j, i), h))
    vs = pl.BlockSpec((t, V_DIM), lambda h, i, j: (jnp.minimum(j, i), h))
    rs = pl.BlockSpec((t, V_DIM), lambda h, i, j: (i, h))
    return pl.pallas_call(
        body, name=name, grid=(H, n, n), in_specs=[qs, ks, vs, rs, rs, rs], out_specs=qs,
        out_shape=jax.ShapeDtypeStruct((T, H * HEAD_PAD), F32),
        scratch_shapes=[pltpu.VMEM((t, HEAD_PAD), F32)],
        compiler_params=_params(("parallel", "parallel", "arbitrary")),
    )(qf, kf, v, do, lse, delta)


def _flash_bwd_dkv(qf, kf, v, do, lse, delta, name):
    T = qf.shape[0]
    H = MLA_HEADS
    t = _pick(T, (512, 256, 128))
    n = T // t
    scale = 1.0 / math.sqrt(QK_DIM)

    def body(q_ref, k_ref, v_ref, do_ref, lse_ref, dl_ref, dk_ref, dv_ref, dk_acc, dv_acc):
        j, i = pl.program_id(1), pl.program_id(2)

        @pl.when(i == 0)
        def _():
            dk_acc[...] = jnp.zeros_like(dk_acc)
            dv_acc[...] = jnp.zeros_like(dv_acc)

        def step(masked):
            s = lax.dot_general(q_ref[...], k_ref[...], NT_DIMS, preferred_element_type=F32) * scale
            if masked:
                s = jnp.where(_causal_mask(t, t), s, NEG_BIG)
            p = jnp.exp(s - lse_ref[:, :1])
            dp = lax.dot_general(do_ref[...], v_ref[...], NT_DIMS, preferred_element_type=F32)
            ds = (p * (dp - dl_ref[:, :1])).astype(BF16)
            dv_acc[...] += lax.dot_general(p.astype(BF16), do_ref[...], TN_DIMS, preferred_element_type=F32)
            dk_acc[...] += lax.dot_general(ds, q_ref[...], TN_DIMS, preferred_element_type=F32)

        @pl.when(i == j)
        def _():
            step(True)

        @pl.when(i > j)
        def _():
            step(False)

        @pl.when(i == n - 1)
        def _():
            dk_ref[...] = dk_acc[...] * scale
            dv_ref[...] = dv_acc[...]

    qs = pl.BlockSpec((t, HEAD_PAD), lambda h, j, i: (jnp.maximum(i, j), h))
    rs = pl.BlockSpec((t, V_DIM), lambda h, j, i: (jnp.maximum(i, j), h))
    ks = pl.BlockSpec((t, HEAD_PAD), lambda h, j, i: (j, h))
    vs = pl.BlockSpec((t, V_DIM), lambda h, j, i: (j, h))
    return pl.pallas_call(
        body, name=name, grid=(H, n, n), in_specs=[qs, ks, vs, rs, rs, rs], out_specs=[ks, vs],
        out_shape=[jax.ShapeDtypeStruct((T, H * HEAD_PAD), F32), jax.ShapeDtypeStruct((T, H * V_DIM), F32)],
        scratch_shapes=[pltpu.VMEM((t, HEAD_PAD), F32), pltpu.VMEM((t, V_DIM), F32)],
        compiler_params=_params(("parallel", "parallel", "arbitrary")),
    )(qf, kf, v, do, lse, delta)


def _alibi_slopes():
    tot = DIL_GROUPS * DIL_HEADS
    return [float(np.float32(2.0) ** (np.float32(-8.0) * np.float32(k) / np.float32(tot))) for k in range(1, tot + 1)]


def _dil_masks():
    iq = lax.broadcasted_iota(jnp.int32, (DIL_BLK, DIL_BLK), 0)
    ik = lax.broadcasted_iota(jnp.int32, (DIL_BLK, DIL_BLK), 1)
    return (ik >= iq), (iq + DIL_BLK - ik).astype(F32), (ik <= iq), (iq - ik).astype(F32)


def _dil_norm(x, g):
    r = lax.rsqrt(jnp.mean(x * x, axis=-1, keepdims=True) + EPS)
    return x * r, r


def _dil_fwd(qkv, g_qn, g_kn, slopes, name):
    T = qkv.shape[0]
    GH = DIL_GROUPS * DIL_HEADS
    scale = 1.0 / math.sqrt(DIL_HEAD_DIM)

    def body(sl_ref, q_ref, k_ref, v_ref, gq_ref, gk_ref, o_ref, lse_ref):
        gh = pl.program_id(0)
        slope = sl_ref[gh]
        ok_p, dist_p, ok_c, dist_c = _dil_masks()
        gq, gk = gq_ref[...], gk_ref[...]
        for g, (_, d) in enumerate(DIL_PAIRS):
            @pl.when((gh >= g * DIL_HEADS) & (gh < (g + 1) * DIL_HEADS))
            def _(d=d):
                nb = T // (d * DIL_BLK)
                bias_p = jnp.where(ok_p, -slope * d * dist_p, NEG_BIG)
                bias_c = jnp.where(ok_c, -slope * d * dist_c, NEG_BIG)

                def phase(r, _):
                    def blk(nn, _):
                        def rows(b):
                            return pl.ds(b * (d * DIL_BLK) + r, DIL_BLK, stride=d) if d > 1 else pl.ds(pl.multiple_of(b * DIL_BLK, DIL_BLK), DIL_BLK)
                        cur, prv = rows(nn), rows(jnp.maximum(nn - 1, 0))
                        q = (_dil_norm(q_ref[cur, :], gq)[0] * gq).astype(BF16)
                        kc = (_dil_norm(k_ref[cur, :], gk)[0] * gk).astype(BF16)
                        kp = (_dil_norm(k_ref[prv, :], gk)[0] * gk).astype(BF16)
                        s_c = lax.dot_general(q, kc, NT_DIMS, preferred_element_type=F32) * scale + bias_c
                        s_p = lax.dot_general(q, kp, NT_DIMS, preferred_element_type=F32) * scale + bias_p
                        s_p = jnp.where(nn > 0, s_p, NEG_BIG)
                        m = jnp.maximum(jnp.max(s_c, axis=-1, keepdims=True), jnp.max(s_p, axis=-1, keepdims=True))
                        p_c = jnp.exp(s_c - m)
                        p_p = jnp.exp(s_p - m)
                        l = jnp.sum(p_c, axis=-1, keepdims=True) + jnp.sum(p_p, axis=-1, keepdims=True)
                        acc = jnp.dot(p_c.astype(BF16), v_ref[cur, :].astype(BF16), preferred_element_type=F32)
                        acc += jnp.dot(p_p.astype(BF16), v_ref[prv, :].astype(BF16), preferred_element_type=F32)
                        o_ref[cur, :] = acc / l
                        lse_ref[cur, :] = jnp.broadcast_to(m + jnp.log(l), (DIL_BLK, DIL_HEAD_DIM))
                        return 0
                    lax.fori_loop(0, nb, blk, 0)
                    return 0
                lax.fori_loop(0, d, phase, 0)

    col = lambda off: pl.BlockSpec((T, DIL_HEAD_DIM), lambda gh, sl: (0, gh + off))
    gvec = pl.BlockSpec((1, DIL_HEAD_DIM), lambda gh, sl: (0, 0))
    return pl.pallas_call(
        body, name=name,
        grid_spec=pltpu.PrefetchScalarGridSpec(
            num_scalar_prefetch=1, grid=(GH,),
            in_specs=[col(0), col(GH), col(2 * GH), gvec, gvec], out_specs=[col(0), col(0)]),
        out_shape=[jax.ShapeDtypeStruct((T, GH * DIL_HEAD_DIM), F32)] * 2,
        compiler_params=_params(("parallel",)),
    )(slopes, qkv, qkv, qkv, g_qn, g_kn)


def _dil_merge(o_g, lse_g, name):
    T = o_g.shape[0]
    W = DIL_HEADS * DIL_HEAD_DIM
    tr = _pick(T, (256, 128))

    def body(o0, o1, o2, l0, l1, l2, o_ref, lse_ref):
        a, b, c = l0[...], l1[...], l2[...]
        m = jnp.maximum(jnp.maximum(a, b), c)
        ea, eb, ec = jnp.exp(a - m), jnp.exp(b - m), jnp.exp(c - m)
        tot = ea + eb + ec
        o_ref[...] = ((o0[...] * ea + o1[...] * eb + o2[...] * ec) / tot).astype(BF16)
        lse_ref[...] = m + jnp.log(tot)

    grp = lambda g: pl.BlockSpec((tr, W), lambda i: (i, g))
    out = pl.BlockSpec((tr, W), lambda i: (i, 0))
    return pl.pallas_call(
        body, name=name, grid=(T // tr,),
        in_specs=[grp(0), grp(1), grp(2), grp(0), grp(1), grp(2)], out_specs=[out, out],
        out_shape=[jax.ShapeDtypeStruct((T, W), BF16), jax.ShapeDtypeStruct((T, W), F32)],
        compiler_params=_params(("parallel",)),
    )(o_g, o_g, o_g, lse_g, lse_g, lse_g)


def _dil_bwd(qkv, g_qn, g_kn, slopes, do, delta, lse, name):
    T = qkv.shape[0]
    GH = DIL_GROUPS * DIL_HEADS
    scale = 1.0 / math.sqrt(DIL_HEAD_DIM)
    nchunk = T // DIL_BLK

    def body(sl_ref, q_ref, k_ref, v_ref, gq_ref, gk_ref, do_ref, dl_ref, lse_ref,
             dq_ref, dk_ref, dv_ref, dgq_ref, dgk_ref, dq_acc, dk_acc, dv_acc):
        gh = pl.program_id(0)
        slope = sl_ref[gh]
        ok_p, dist_p, ok_c, dist_c = _dil_masks()
        gq, gk = gq_ref[...], gk_ref[...]

        @pl.when(gh == 0)
        def _():
            dgq_ref[...] = jnp.zeros_like(dgq_ref)
            dgk_ref[...] = jnp.zeros_like(dgk_ref)

        dk_acc[...] = jnp.zeros_like(dk_acc)
        dv_acc[...] = jnp.zeros_like(dv_acc)
        for g, (_, d) in enumerate(DIL_PAIRS):
            @pl.when((gh >= g * DIL_HEADS) & (gh < (g + 1) * DIL_HEADS))
            def _(d=d):
                nb = T // (d * DIL_BLK)
                bias_p = jnp.where(ok_p, -slope * d * dist_p, NEG_BIG)
                bias_c = jnp.where(ok_c, -slope * d * dist_c, NEG_BIG)

                def phase(r, _):
                    def blk(nn, _):
                        def rows(b):
                            return pl.ds(b * (d * DIL_BLK) + r, DIL_BLK, stride=d) if d > 1 else pl.ds(pl.multiple_of(b * DIL_BLK, DIL_BLK), DIL_BLK)
                        cur, prv = rows(nn), rows(jnp.maximum(nn - 1, 0))
                        q = (_dil_norm(q_ref[cur, :], gq)[0] * gq).astype(BF16)
                        kc = (_dil_norm(k_ref[cur, :], gk)[0] * gk).astype(BF16)
                        kp = (_dil_norm(k_ref[prv, :], gk)[0] * gk).astype(BF16)
                        vc = v_ref[cur, :].astype(BF16)
                        vp = v_ref[prv, :].astype(BF16)
                        dob = do_ref[cur, :].astype(BF16)
                        delta = dl_ref[cur, :][:, :1]
                        ls = lse_ref[cur, :][:, :1]
                        s_c = lax.dot_general(q, kc, NT_DIMS, preferred_element_type=F32) * scale + bias_c
                        s_p = lax.dot_general(q, kp, NT_DIMS, preferred_element_type=F32) * scale + bias_p
                        s_p = jnp.where(nn > 0, s_p, NEG_BIG)
                        p_c = jnp.exp(s_c - ls)
                        p_p = jnp.exp(s_p - ls)
                        dp_c = lax.dot_general(dob, vc, NT_DIMS, preferred_element_type=F32)
                        dp_p = lax.dot_general(dob, vp, NT_DIMS, preferred_element_type=F32)
                        ds_c = (p_c * (dp_c - delta)).astype(BF16)
                        ds_p = (p_p * (dp_p - delta)).astype(BF16)
                        dq_acc[cur, :] = (jnp.dot(ds_c, kc, preferred_element_type=F32)
                                          + jnp.dot(ds_p, kp, preferred_element_type=F32)) * scale
                        dk_acc[cur, :] += lax.dot_general(ds_c, q, TN_DIMS, preferred_element_type=F32) * scale
                        dv_acc[cur, :] += lax.dot_general(p_c.astype(BF16), dob, TN_DIMS, preferred_element_type=F32)
                        dk_acc[prv, :] += lax.dot_general(ds_p, q, TN_DIMS, preferred_element_type=F32) * scale
                        dv_acc[prv, :] += lax.dot_general(p_p.astype(BF16), dob, TN_DIMS, preferred_element_type=F32)
                        return 0
                    lax.fori_loop(0, nb, blk, 0)
                    return 0
                lax.fori_loop(0, d, phase, 0)

        def fin(ci, carry):
            dgq, dgk = carry
            rows = pl.ds(pl.multiple_of(ci * DIL_BLK, DIL_BLK), DIL_BLK)
            outs = []
            for x_ref, d_acc, gvec in ((q_ref, dq_acc, gq), (k_ref, dk_acc, gk)):
                xhat, r = _dil_norm(x_ref[rows, :], gvec)
                dn = d_acc[rows, :]
                dxh = dn * gvec
                c = jnp.mean(dxh * xhat, axis=-1, keepdims=True)
                outs.append(((r * (dxh - xhat * c)).astype(BF16), jnp.sum(dn * xhat, axis=0, keepdims=True)))
            dq_ref[rows, :] = outs[0][0]
            dk_ref[rows, :] = outs[1][0]
            dv_ref[rows, :] = dv_acc[rows, :].astype(BF16)
            return dgq + outs[0][1], dgk + outs[1][1]

        z = jnp.zeros((1, DIL_HEAD_DIM), F32)
        dgq, dgk = lax.fori_loop(0, nchunk, fin, (z, z))
        dgq_ref[...] += dgq
        dgk_ref[...] += dgk

    col = lambda off: pl.BlockSpec((T, DIL_HEAD_DIM), lambda gh, sl: (0, gh + off))
    hcol = pl.BlockSpec((T, DIL_HEAD_DIM), lambda gh, sl: (0, gh % DIL_HEADS))
    gvec = pl.BlockSpec((1, DIL_HEAD_DIM), lambda gh, sl: (0, 0))
    wide = jax.ShapeDtypeStruct((T, GH * DIL_HEAD_DIM), BF16)
    vec = jax.ShapeDtypeStruct((1, DIL_HEAD_DIM), F32)
    return pl.pallas_call(
        body, name=name,
        grid_spec=pltpu.PrefetchScalarGridSpec(
            num_scalar_prefetch=1, grid=(GH,),
            in_specs=[col(0), col(GH), col(2 * GH), gvec, gvec, hcol, hcol, hcol],
            out_specs=[col(0), col(0), col(0), gvec, gvec],
            scratch_shapes=[pltpu.VMEM((T, DIL_HEAD_DIM), F32)] * 3),
        out_shape=[wide, wide, wide, vec, vec],
        compiler_params=_params(("arbitrary",)),
    )(slopes, qkv, qkv, qkv, g_qn, g_kn, do, delta, lse)


def _my_pos():
    return lax.axis_index("x"), lax.axis_index("y"), lax.axis_index("c")


def _peer(pos, j):
    x, y, c = pos
    px = 1 - x if j & 4 else x
    py = 1 - y if j & 2 else y
    pc = 1 - c if j & 1 else c
    return (px, py, pc), 4 * px + 2 * py + pc


def _shard_slice(ref, axis, idx, size):
    sl = [slice(None)] * len(ref.shape)
    sl[axis] = pl.ds(pl.multiple_of(idx * size, 8), size)
    return ref.at[tuple(sl)]


def _all_gather(shards, axes, name):
    na = len(shards)

    def body(*refs):
        ins, outs = refs[:na], refs[na:2 * na]
        send_sems, recv_sems, loc_sems = refs[2 * na:]
        pos = _my_pos()
        me = 4 * pos[0] + 2 * pos[1] + pos[2]
        copies, locals_ = [], []
        for a in range(na):
            size = ins[a].shape[axes[a]]
            lc = pltpu.make_async_copy(ins[a], _shard_slice(outs[a], axes[a], me, size), loc_sems.at[a])
            lc.start()
            locals_.append(lc)
            for j in range(1, N_DEV):
                dev, _ = _peer(pos, j)
                cp = pltpu.make_async_remote_copy(
                    src_ref=ins[a], dst_ref=_shard_slice(outs[a], axes[a], me, size),
                    send_sem=send_sems.at[a * N_DEV + j], recv_sem=recv_sems.at[a * N_DEV + j],
                    device_id=dev, device_id_type=MESH)
                cp.start()
                copies.append(cp)
        for a in range(na):
            size = ins[a].shape[axes[a]]
            for j in range(1, N_DEV):
                dev, pid = _peer(pos, j)
                pltpu.make_async_remote_copy(
                    src_ref=ins[a], dst_ref=_shard_slice(outs[a], axes[a], pid, size),
                    send_sem=send_sems.at[a * N_DEV + j], recv_sem=recv_sems.at[a * N_DEV + j],
                    device_id=dev, device_id_type=MESH).wait_recv()
        for cp in copies:
            cp.wait_send()
        for lc in locals_:
            lc.wait()

    out_shape = []
    for s, ax in zip(shards, axes):
        shp = list(s.shape)
        shp[ax] *= N_DEV
        out_shape.append(jax.ShapeDtypeStruct(tuple(shp), s.dtype))
    anyspec = pl.BlockSpec(memory_space=pl.ANY)
    return pl.pallas_call(
        body, name=name, in_specs=[anyspec] * na, out_specs=[anyspec] * na, out_shape=out_shape,
        scratch_shapes=[pltpu.SemaphoreType.DMA((na * N_DEV,)), pltpu.SemaphoreType.DMA((na * N_DEV,)),
                        pltpu.SemaphoreType.DMA((na,))],
        compiler_params=pltpu.CompilerParams(has_side_effects=True),
    )(*shards)


def _grad_scatter(grads, specs, name):
    ng = len(grads)
    nbuf = 1 + max(s[0] for s in specs)

    def body(*refs):
        ins, outs = refs[:ng], refs[ng:ng + nbuf]
        send_sems, recv_sems, loc_sems = refs[ng + nbuf:]
        pos = _my_pos()
        me = 4 * pos[0] + 2 * pos[1] + pos[2]
        copies, locals_ = [], []
        for a in range(ng):
            b, l, ax, _ = specs[a]
            size = ins[a].shape[ax - 1] // N_DEV
            dst = outs[b].at[me, l]
            lc = pltpu.make_async_copy(_shard_slice(ins[a], ax - 1, me, size), dst, loc_sems.at[a])
            lc.start()
            locals_.append(lc)
            for j in range(1, N_DEV):
                dev, pid = _peer(pos, j)
                cp = pltpu.make_async_remote_copy(
                    src_ref=_shard_slice(ins[a], ax - 1, pid, size), dst_ref=dst,
                    send_sem=send_sems.at[a * N_DEV + j], recv_sem=recv_sems.at[a * N_DEV + j],
                    device_id=dev, device_id_type=MESH)
                cp.start()
                copies.append(cp)
        for a in range(ng):
            b, l, ax, _ = specs[a]
            size = ins[a].shape[ax - 1] // N_DEV
            for j in range(1, N_DEV):
                dev, pid = _peer(pos, j)
                pltpu.make_async_remote_copy(
                    src_ref=_shard_slice(ins[a], ax - 1, me, size), dst_ref=outs[b].at[pid, l],
                    send_sem=send_sems.at[a * N_DEV + j], recv_sem=recv_sems.at[a * N_DEV + j],
                    device_id=dev, device_id_type=MESH).wait_recv()
        for cp in copies:
            cp.wait_send()
        for lc in locals_:
            lc.wait()

    out_shape = [None] * nbuf
    for g, (b, l, ax, nl) in zip(grads, specs):
        shp = list(g.shape)
        shp[ax - 1] //= N_DEV
        out_shape[b] = jax.ShapeDtypeStruct((N_DEV, nl) + tuple(shp), g.dtype)
    anyspec = pl.BlockSpec(memory_space=pl.ANY)
    return pl.pallas_call(
        body, name=name, in_specs=[anyspec] * ng, out_specs=[anyspec] * nbuf, out_shape=out_shape,
        scratch_shapes=[pltpu.SemaphoreType.DMA((ng * N_DEV,)), pltpu.SemaphoreType.DMA((ng * N_DEV,)),
                        pltpu.SemaphoreType.DMA((ng,))],
        compiler_params=pltpu.CompilerParams(has_side_effects=True),
    )(*grads)


def _gain_allreduce(v, name):
    n = v.shape[1]

    def body(v_ref, o_ref, slots, send_sems, recv_sems):
        pos = _my_pos()
        me = 4 * pos[0] + 2 * pos[1] + pos[2]
        slots[me] = v_ref[...]
        copies = []
        for j in range(1, N_DEV):
            dev, _ = _peer(pos, j)
            cp = pltpu.make_async_remote_copy(
                src_ref=slots.at[me], dst_ref=slots.at[me], send_sem=send_sems.at[j], recv_sem=recv_sems.at[j],
                device_id=dev, device_id_type=MESH)
            cp.start()
            copies.append(cp)
        for j in range(1, N_DEV):
            dev, pid = _peer(pos, j)
            pltpu.make_async_remote_copy(
                src_ref=slots.at[me], dst_ref=slots.at[pid], send_sem=send_sems.at[j], recv_sem=recv_sems.at[j],
                device_id=dev, device_id_type=MESH).wait_recv()
        for cp in copies:
            cp.wait_send()
        acc = slots[0]
        for s in range(1, N_DEV):
            acc = acc + slots[s]
        o_ref[...] = acc

    return pl.pallas_call(
        body, name=name, out_shape=jax.ShapeDtypeStruct((1, n), F32),
        in_specs=[pl.BlockSpec(memory_space=pltpu.VMEM)], out_specs=pl.BlockSpec(memory_space=pltpu.VMEM),
        scratch_shapes=[pltpu.VMEM((N_DEV, 1, n), F32), pltpu.SemaphoreType.DMA((N_DEV,)),
                        pltpu.SemaphoreType.DMA((N_DEV,))],
        compiler_params=pltpu.CompilerParams(has_side_effects=True),
    )(v)


def _adamw(parts, w, m, v, name):
    P, R, C = parts.shape
    tr = _pick(R, (128, 64, 32, 16, 8, 1))
    c1 = 1.0 - ADAM_B1 ** ADAM_STEP
    c2 = 1.0 - ADAM_B2 ** ADAM_STEP

    def body(p_ref, w_ref, m_ref, v_ref, g_out, d_out, m_out, v_out):
        g = p_ref[0].astype(F32)
        for s in range(1, P):
            g = g + p_ref[s].astype(F32)
        mn = ADAM_B1 * m_ref[...] + (1.0 - ADAM_B1) * g
        vn = ADAM_B2 * v_ref[...] + (1.0 - ADAM_B2) * (g * g)
        g_out[...] = g
        m_out[...] = mn
        v_out[...] = vn
        d_out[...] = -ADAM_LR * ((mn / c1) / (jnp.sqrt(vn / c2) + ADAM_EPS) + ADAM_WD * w_ref[...])

    row = pl.BlockSpec((tr, C), lambda i: (i, 0))
    shp = jax.ShapeDtypeStruct((R, C), F32)
    return pl.pallas_call(
        body, name=name, grid=(R // tr,),
        in_specs=[pl.BlockSpec((P, tr, C), lambda i: (0, i, 0)), row, row, row], out_specs=[row] * 4,
        out_shape=[shp] * 4, compiler_params=_params(("parallel",)),
    )(parts, w, m, v)


def _pad_heads(w):
    lead = w.shape[:-1]
    n = w.shape[-1] // QK_DIM
    w = w.reshape(lead + (n, QK_DIM))
    w = jnp.pad(w, [(0, 0)] * len(lead) + [(0, 0), (0, HEAD_PAD - QK_DIM)])
    return w.reshape(lead + (n * HEAD_PAD,))


def _unpad_heads(w):
    lead = w.shape[:-1]
    n = w.shape[-1] // HEAD_PAD
    return w.reshape(lead + (n, HEAD_PAD))[..., :QK_DIM].reshape(lead + (n * QK_DIM,))


def kernel(x, ffn1_norm, ffn1_w_in, ffn1_w_out, mix_norm, ffn2_norm, ffn2_w_in, ffn2_w_out, mla_w_down, mla_g_cq, mla_g_ckv, mla_w_uq, mla_w_ukv, mla_g_qn, mla_g_kn, mla_w_o, dil_w_qkv, dil_g_qn, dil_g_kn, dil_w_o, loss_target, m_ffn1_norm, m_ffn1_w_in, m_ffn1_w_out, m_mix_norm, m_ffn2_norm, m_ffn2_w_in, m_ffn2_w_out, m_mla_w_down, m_mla_g_cq, m_mla_g_ckv, m_mla_w_uq, m_mla_w_ukv, m_mla_g_qn, m_mla_g_kn, m_mla_w_o, m_dil_w_qkv, m_dil_g_qn, m_dil_g_kn, m_dil_w_o, v_ffn1_norm, v_ffn1_w_in, v_ffn1_w_out, v_mix_norm, v_ffn2_norm, v_ffn2_w_in, v_ffn2_w_out, v_mla_w_down, v_mla_g_cq, v_mla_g_ckv, v_mla_w_uq, v_mla_w_ukv, v_mla_g_qn, v_mla_g_kn, v_mla_w_o, v_dil_w_qkv, v_dil_g_qn, v_dil_g_kn, v_dil_w_o):
    names = ["ffn1_norm", "ffn1_w_in", "ffn1_w_out", "mix_norm", "ffn2_norm", "ffn2_w_in", "ffn2_w_out", "mla_w_down",
             "mla_g_cq", "mla_g_ckv", "mla_w_uq", "mla_w_ukv", "mla_g_qn", "mla_g_kn", "mla_w_o", "dil_w_qkv",
             "dil_g_qn", "dil_g_kn", "dil_w_o"]
    W = dict(zip(names, [ffn1_norm, ffn1_w_in, ffn1_w_out, mix_norm, ffn2_norm, ffn2_w_in, ffn2_w_out, mla_w_down,
                         mla_g_cq, mla_g_ckv, mla_w_uq, mla_w_ukv, mla_g_qn, mla_g_kn, mla_w_o, dil_w_qkv,
                         dil_g_qn, dil_g_kn, dil_w_o]))
    M1 = dict(zip(names, [m_ffn1_norm, m_ffn1_w_in, m_ffn1_w_out, m_mix_norm, m_ffn2_norm, m_ffn2_w_in, m_ffn2_w_out,
                          m_mla_w_down, m_mla_g_cq, m_mla_g_ckv, m_mla_w_uq, m_mla_w_ukv, m_mla_g_qn, m_mla_g_kn,
                          m_mla_w_o, m_dil_w_qkv, m_dil_g_qn, m_dil_g_kn, m_dil_w_o]))
    V2 = dict(zip(names, [v_ffn1_norm, v_ffn1_w_in, v_ffn1_w_out, v_mix_norm, v_ffn2_norm, v_ffn2_w_in, v_ffn2_w_out,
                          v_mla_w_down, v_mla_g_cq, v_mla_g_ckv, v_mla_w_uq, v_mla_w_ukv, v_mla_g_qn, v_mla_g_kn,
                          v_mla_w_o, v_dil_w_qkv, v_dil_g_qn, v_dil_g_kn, v_dil_w_o]))
    S, D = x.shape[1], x.shape[2]
    x0 = x.reshape(S, D)
    tgt = loss_target.reshape(S, D)

    big = ["ffn1_w_in", "ffn1_w_out", "ffn2_w_in", "ffn2_w_out", "mla_w_down", "mla_w_uq", "mla_w_ukv", "mla_w_o",
           "dil_w_qkv", "dil_w_o"]
    shard_axis = {"ffn1_w_in": 2, "ffn1_w_out": 1, "ffn2_w_in": 2, "ffn2_w_out": 1, "mla_w_down": 1, "mla_w_uq": 2,
                  "mla_w_ukv": 2, "mla_w_o": 1, "dil_w_qkv": 2, "dil_w_o": 2}

    def padded(n, w):
        if n == "mla_w_down":
            return jnp.pad(w, ((0, 0), (0, 0), (0, LAT_PAD - w.shape[2])))
        if n == "mla_w_uq":
            return _pad_heads(w)
        return w

    shards = [_cast_bf16(padded(n, W[n]), "cast_" + n) for n in big]
    full = dict(zip(big, _all_gather(shards, [shard_axis[n] for n in big], "all_gather_weights")))

    g_qn = _pad_heads(mla_g_qn)
    g_kn = _pad_heads(mla_g_kn)
    tabs = _rope_tables(S)
    slopes = jnp.asarray(_alibi_slopes(), F32)

    grads = {}
    gain_g = {}

    def ffn_fwd(xin, norm, w_in, w_out, l, tag):
        h = _rms_fwd(xin, norm[l:l + 1], f"rms_fwd_{tag}")
        u = _mm(h, w_in, "nn", F32, f"mm_in_{tag}", layer=l)
        a = _swiglu_fwd(u, f"swiglu_fwd_{tag}")
        xo = _mm(a, w_out, "nn", F32, f"mm_out_{tag}", scale=0.5, res=xin, layer=l)
        return xo, (xin, h, u, a)

    def ffn_bwd(dxo, saved, norm, w_in, w_out, l, tag, n_norm, n_in, n_out):
        xin, h, u, a = saved
        da = _mm(dxo, w_out, "nt", F32, f"mm_da_{tag}", scale=0.5, layer=l)
        grads.setdefault(n_out, {})[l] = _mm(a, dxo, "tn", BF16, f"mm_dwout_{tag}", scale=0.5)
        du = _swiglu_bwd(u, da, f"swiglu_bwd_{tag}")
        dh = _mm(du, w_in, "nt", F32, f"mm_dh_{tag}", layer=l)
        grads.setdefault(n_in, {})[l] = _mm(h, du, "tn", BF16, f"mm_dwin_{tag}")
        dx, dg = _rms_bwd(xin, norm[l:l + 1], dh, dxo, f"rms_bwd_{tag}")
        gain_g.setdefault(n_norm, {})[l] = dg
        return dx

    def mla_fwd(xin, l):
        xn = _rms_fwd(xin, mix_norm[l:l + 1], "rms_fwd_mla")
        lat = _mm(xn, full["mla_w_down"], "nn", F32, "mm_lat", layer=0)
        cq, ckv = _lat_norm_fwd(lat, mla_g_cq, mla_g_ckv, "lat_norm_fwd")
        q_raw = _mm(cq, full["mla_w_uq"], "nn", F32, "mm_uq", layer=0)
        kv = _mm(ckv, full["mla_w_ukv"], "nn", F32, "mm_ukv", layer=0)
        qf, kf, vb = _mla_prep_fwd(q_raw, kv, lat, g_qn, g_kn, tabs, "mla_prep_fwd")
        o, lse = _flash_fwd(qf, kf, vb, "flash_fwd")
        xo = _mm(o, full["mla_w_o"], "nn", F32, "mm_mla_o", res=xin, layer=0)
        return xo, (xin, xn, lat, cq, ckv, q_raw, kv, qf, kf, vb, o, lse)

    def mla_bwd(dxo, saved, l):
        xin, xn, lat, cq, ckv, q_raw, kv, qf, kf, vb, o, lse = saved
        do = _mm(dxo, full["mla_w_o"], "nt", BF16, "mm_mla_do", layer=0)
        grads["mla_w_o"] = {0: _mm(o, dxo, "tn", BF16, "mm_mla_dwo")}
        delta = _attn_delta(do, o, "attn_delta")
        dqf = _flash_bwd_dq(qf, kf, vb, do, lse, delta, "flash_bwd_dq")
        dkf, dv = _flash_bwd_dkv(qf, kf, vb, do, lse, delta, "flash_bwd_dkv")
        dq_raw, dkv, dkpe, dgq, dgk = _mla_prep_bwd(q_raw, kv, lat, g_qn, g_kn, tabs, dqf, dkf, dv, "mla_prep_bwd")
        gain_g["mla_g_qn"] = {0: dgq}
        gain_g["mla_g_kn"] = {0: dgk}
        dcq = _mm(dq_raw, full["mla_w_uq"], "nt", F32, "mm_dcq", layer=0)
        grads["mla_w_uq"] = {0: _mm(cq, dq_raw, "tn", BF16, "mm_dwuq")}
        dckv = _mm(dkv, full["mla_w_ukv"], "nt", F32, "mm_dckv", layer=0)
        grads["mla_w_ukv"] = {0: _mm(ckv, dkv, "tn", BF16, "mm_dwukv")}
        dlat, dgcq, dgckv = _lat_norm_bwd(lat, mla_g_cq, mla_g_ckv, dcq, dckv, dkpe, "lat_norm_bwd")
        gain_g["mla_g_cq"] = {0: dgcq}
        gain_g["mla_g_ckv"] = {0: dgckv}
        dxn = _mm(dlat, full["mla_w_down"], "nt", F32, "mm_dxn_mla", layer=0)
        grads["mla_w_down"] = {0: _mm(xn, dlat, "tn", BF16, "mm_dwdown")}
        dx, dg = _rms_bwd(xin, mix_norm[l:l + 1], dxn, dxo, "rms_bwd_mla")
        gain_g.setdefault("mix_norm", {})[l] = dg
        return dx

    def dil_fwd(xin, l):
        xn = _rms_fwd(xin, mix_norm[l:l + 1], "rms_fwd_dil")
        qkv = _mm(xn, full["dil_w_qkv"], "nn", F32, "mm_qkv", layer=0)
        o_g, lse_g = _dil_fwd(qkv, dil_g_qn, dil_g_kn, slopes, "dil_fwd")
        o, lse = _dil_merge(o_g, lse_g, "dil_merge")
        xo = _mm(o, full["dil_w_o"], "nn", F32, "mm_dil_o", res=xin, layer=0)
        return xo, (xin, xn, qkv, o, lse)

    def dil_bwd(dxo, saved, l):
        xin, xn, qkv, o, lse = saved
        do = _mm(dxo, full["dil_w_o"], "nt", F32, "mm_dil_do", layer=0)
        grads["dil_w_o"] = {0: _mm(o, dxo, "tn", BF16, "mm_dil_dwo")}
        delta = _attn_delta(do, o, "dil_delta")
        dq, dk, dv, dgq, dgk = _dil_bwd(qkv, dil_g_qn, dil_g_kn, slopes, do, delta, lse, "dil_bwd")
        gain_g["dil_g_qn"] = {0: dgq}
        gain_g["dil_g_kn"] = {0: dgk}
        dqkv = jnp.concatenate([dq, dk, dv], axis=1)
        dxn = _mm(dqkv, full["dil_w_qkv"], "nt", F32, "mm_dxn_dil", layer=0)
        grads["dil_w_qkv"] = {0: _mm(xn, dqkv, "tn", BF16, "mm_dwqkv")}
        dx, dg = _rms_bwd(xin, mix_norm[l:l + 1], dxn, dxo, "rms_bwd_dil")
        gain_g.setdefault("mix_norm", {})[l] = dg
        return dx

    depth = ffn1_norm.shape[0]
    saved = []
    xc = x0
    for l in range(depth):
        xc, s1 = ffn_fwd(xc, ffn1_norm, full["ffn1_w_in"], full["ffn1_w_out"], l, f"ffn1_{l}")
        xc, s2 = (mla_fwd if l % 2 == 0 else dil_fwd)(xc, l)
        xc, s3 = ffn_fwd(xc, ffn2_norm, full["ffn2_w_in"], full["ffn2_w_out"], l, f"ffn2_{l}")
        saved.append((s1, s2, s3))

    dx, loss_part = _loss_head(xc, tgt, "loss_head")
    loss = lax.psum(loss_part[0, 0], MESH_AXES)

    for l in reversed(range(depth)):
        s1, s2, s3 = saved[l]
        dx = ffn_bwd(dx, s3, ffn2_norm, full["ffn2_w_in"], full["ffn2_w_out"], l, f"ffn2_{l}",
                     "ffn2_norm", "ffn2_w_in", "ffn2_w_out")
        dx = (mla_bwd if l % 2 == 0 else dil_bwd)(dx, s2, l)
        dx = ffn_bwd(dx, s1, ffn1_norm, full["ffn1_w_in"], full["ffn1_w_out"], l, f"ffn1_{l}",
                     "ffn1_norm", "ffn1_w_in", "ffn1_w_out")
    grad_x = dx.reshape(x.shape)

    g_list, g_specs = [], []
    for b, n in enumerate(big):
        nl = W[n].shape[0]
        for l in range(nl):
            g_list.append(grads[n][l])
            g_specs.append((b, l, shard_axis[n], nl))
    parts = dict(zip(big, _grad_scatter(g_list, g_specs, "grad_scatter")))

    out_g, out_d, out_m, out_v = {}, {}, {}, {}
    for n in big:
        p = parts[n]
        if n == "mla_w_down":
            p = p[..., :W[n].shape[2]]
        elif n == "mla_w_uq":
            p = _unpad_heads(p)
        L, R, C = W[n].shape
        res = _adamw(p.reshape(N_DEV, L * R, C), W[n].reshape(L * R, C), M1[n].reshape(L * R, C),
                     V2[n].reshape(L * R, C), "adamw_" + n)
        out_g[n], out_d[n], out_m[n], out_v[n] = [r.reshape(L, R, C) for r in res]

    small = [n for n in names if n not in big]

    def gain_local(n):
        rows = [gain_g[n][l] for l in range(W[n].shape[0])]
        g = jnp.concatenate(rows, axis=1)
        return g

    def flat_pad(n, a):
        a = a.reshape(1, -1)
        if n in ("mla_g_qn", "mla_g_kn"):
            a = _pad_heads(a)
        return a

    packed_g = jnp.concatenate([gain_local(n) for n in small], axis=1)
    sizes = [gain_local(n).shape[1] for n in small]
    tot_g = _gain_allreduce(packed_g, "gain_allreduce")
    pw = jnp.concatenate([flat_pad(n, W[n]) for n in small], axis=1)
    pm = jnp.concatenate([flat_pad(n, M1[n]) for n in small], axis=1)
    pv = jnp.concatenate([flat_pad(n, V2[n]) for n in small], axis=1)
    res = _adamw(tot_g.reshape(1, 1, -1), pw, pm, pv, "adamw_gains")
    off = 0
    for n, sz in zip(small, sizes):
        for dst, r in zip((out_g, out_d, out_m, out_v), res):
            piece = r[:, off:off + sz]
            if n in ("mla_g_qn", "mla_g_kn"):
                piece = _unpad_heads(piece)
            dst[n] = piece.reshape(W[n].shape)
        off += sz

    return (loss, grad_x, *[out_g[n] for n in names], *[out_d[n] for n in names],
            *[out_m[n] for n in names], *[out_v[n] for n in names])
```

```python
import functools
import math

import jax
import jax.numpy as jnp
import numpy as np
from jax import lax
from jax.experimental import pallas as pl
from jax.experimental.pallas import tpu as pltpu

EPS = 1e-6
MLA_HEADS = 16
Q_LORA = 512
KV_LORA = 512
NOPE_DIM = 128
ROPE_DIM = 64
V_DIM = 128
QK_DIM = NOPE_DIM + ROPE_DIM
ROPE_THETA = 10000.0
HEAD_PAD = 256
LAT_PAD = Q_LORA + KV_LORA + 128
DIL_PAIRS = ((128, 1), (512, 4), (2048, 16))
DIL_GROUPS = 3
DIL_HEADS = 8
DIL_HEAD_DIM = 128
DIL_BLK = 128
ADAM_LR = 0.001
ADAM_B1 = 0.9
ADAM_B2 = 0.999
ADAM_EPS = 1e-08
ADAM_WD = 0.01
ADAM_STEP = 10

N_DEV = 8
MESH_AXES = ("x", "y", "c")
MESH = pl.DeviceIdType.MESH
NEG_BIG = -1e30
VMEM_LIMIT_V7X = 56 * 1024 * 1024
LANES = 128

BF16 = jnp.bfloat16
F32 = jnp.float32


def _pick(n, cands):
    for c in cands:
        if n % c == 0:
            return c
    raise ValueError(f"no tile for {n}")


def _params(sem):
    return pltpu.CompilerParams(dimension_semantics=sem, vmem_limit_bytes=VMEM_LIMIT_V7X)


def _mm(a, b, mode, out_dtype, name, *, scale=1.0, res=None, layer=None):
    b2 = b.shape[-2:]
    if mode == "nn":
        (M, K), (Kb, N) = a.shape, b2
    elif mode == "nt":
        (M, K), (N, Kb) = a.shape, b2
    else:
        (K, M), (Kb, N) = a.shape, b2
    assert K == Kb, (a.shape, b.shape, mode)
    tm = _pick(M, (1024, 512, 384, 256, 128))
    tn = _pick(N, (1024, 512, 384, 256, 128))
    tk = _pick(K, (512, 384, 256, 128))
    nk = K // tk
    dims = {"nn": (((1,), (0,)), ((), ())), "nt": (((1,), (1,)), ((), ())), "tn": (((0,), (0,)), ((), ()))}[mode]

    def body(*refs):
        if res is None:
            a_ref, b_ref, o_ref, acc = refs
            r_ref = None
        else:
            a_ref, b_ref, r_ref, o_ref, acc = refs
        k = pl.program_id(2)

        @pl.when(k == 0)
        def _():
            acc[...] = jnp.zeros_like(acc)

        acc[...] += lax.dot_general(a_ref[...].astype(BF16), b_ref[...].astype(BF16), dims,
                                    preferred_element_type=F32)

        @pl.when(k == nk - 1)
        def _():
            v = acc[...]
            if scale != 1.0:
                v = v * scale
            if r_ref is not None:
                v = r_ref[...] + v
            o_ref[...] = v.astype(o_ref.dtype)

    a_spec = pl.BlockSpec((tk, tm), lambda i, j, k: (k, i)) if mode == "tn" else pl.BlockSpec((tm, tk), lambda i, j, k: (i, k))
    if mode == "nt":
        bshape, bidx = (tn, tk), (lambda i, j, k: (j, k))
    else:
        bshape, bidx = (tk, tn), (lambda i, j, k: (k, j))
    if b.ndim == 3:
        b_spec = pl.BlockSpec((None,) + bshape, lambda i, j, k: (layer,) + bidx(i, j, k))
    else:
        b_spec = pl.BlockSpec(bshape, bidx)
    in_specs = [a_spec, b_spec]
    args = [a, b]
    if res is not None:
        in_specs.append(pl.BlockSpec((tm, tn), lambda i, j, k: (i, j)))
        args.append(res)
    return pl.pallas_call(
        body, name=name, grid=(M // tm, N // tn, nk),
        in_specs=in_specs, out_specs=pl.BlockSpec((tm, tn), lambda i, j, k: (i, j)),
        out_shape=jax.ShapeDtypeStruct((M, N), out_dtype),
        scratch_shapes=[pltpu.VMEM((tm, tn), F32)],
        compiler_params=_params(("parallel", "parallel", "arbitrary")),
    )(*args)


def _cast_bf16(w, layer, name):
    _, R, C = w.shape
    tr = _pick(R, (512, 256, 128, 64, 32, 16))

    def body(w_ref, o_ref):
        o_ref[...] = w_ref[...].astype(BF16)

    return pl.pallas_call(
        body, name=name, grid=(R // tr,),
        in_specs=[pl.BlockSpec((None, tr, C), lambda i: (layer, i, 0))],
        out_specs=pl.BlockSpec((None, tr, C), lambda i: (0, i, 0)),
        out_shape=jax.ShapeDtypeStruct((1, R, C), BF16), compiler_params=_params(("parallel",)),
    )(w)


def _rms_fwd(x, g, name):
    T, D = x.shape
    tr = _pick(T, (512, 256, 128))

    def body(x_ref, g_ref, o_ref):
        xv = x_ref[...]
        r = lax.rsqrt(jnp.mean(xv * xv, axis=-1, keepdims=True) + EPS)
        o_ref[...] = ((xv * r) * g_ref[...]).astype(BF16)

    return pl.pallas_call(
        body, name=name, grid=(T // tr,),
        in_specs=[pl.BlockSpec((tr, D), lambda i: (i, 0)), pl.BlockSpec((1, D), lambda i: (0, 0))],
        out_specs=pl.BlockSpec((tr, D), lambda i: (i, 0)),
        out_shape=jax.ShapeDtypeStruct((T, D), BF16), compiler_params=_params(("parallel",)),
    )(x, g)


def _rms_bwd(x, g, dh, dres, name):
    T, D = x.shape
    tr = _pick(T, (256, 128))

    def body(x_ref, g_ref, dh_ref, dres_ref, dx_ref, dg_ref):
        xv = x_ref[...]
        dhv = dh_ref[...]
        r = lax.rsqrt(jnp.mean(xv * xv, axis=-1, keepdims=True) + EPS)
        xhat = xv * r
        dxh = dhv * g_ref[...]
        c = jnp.mean(dxh * xhat, axis=-1, keepdims=True)
        dx_ref[...] = dres_ref[...] + r * (dxh - xhat * c)

        @pl.when(pl.program_id(0) == 0)
        def _():
            dg_ref[...] = jnp.zeros_like(dg_ref)

        dg_ref[...] += jnp.sum(dhv * xhat, axis=0, keepdims=True)

    row = pl.BlockSpec((tr, D), lambda i: (i, 0))
    vec = pl.BlockSpec((1, D), lambda i: (0, 0))
    return pl.pallas_call(
        body, name=name, grid=(T // tr,),
        in_specs=[row, vec, row, row], out_specs=[row, vec],
        out_shape=[jax.ShapeDtypeStruct((T, D), F32), jax.ShapeDtypeStruct((1, D), F32)],
        compiler_params=_params(("arbitrary",)),
    )(x, g, dh, dres)


def _swiglu_fwd(u, name):
    T, F2 = u.shape
    F = F2 // 2
    tr = _pick(T, (512, 256, 128))
    tf = _pick(F, (512, 256, 128))
    nf = F // tf

    def body(g_ref, u_ref, o_ref):
        z = g_ref[...]
        o_ref[...] = (z * jax.nn.sigmoid(z) * u_ref[...]).astype(BF16)

    return pl.pallas_call(
        body, name=name, grid=(T // tr, nf),
        in_specs=[pl.BlockSpec((tr, tf), lambda i, j: (i, j)), pl.BlockSpec((tr, tf), lambda i, j: (i, j + nf))],
        out_specs=pl.BlockSpec((tr, tf), lambda i, j: (i, j)),
        out_shape=jax.ShapeDtypeStruct((T, F), BF16), compiler_params=_params(("parallel", "parallel")),
    )(u, u)


def _swiglu_bwd(u, da, name):
    T, F2 = u.shape
    F = F2 // 2
    tr = _pick(T, (128,))
    tf = _pick(F, (512, 256, 128))

    def body(u_ref, da_ref, du_ref):
        for j in range(F // tf):
            z = u_ref[:, j * tf:(j + 1) * tf]
            up = u_ref[:, F + j * tf:F + (j + 1) * tf]
            dav = da_ref[:, j * tf:(j + 1) * tf]
            sg = jax.nn.sigmoid(z)
            silu = z * sg
            du_ref[:, j * tf:(j + 1) * tf] = (dav * up * (sg + silu * (1.0 - sg))).astype(BF16)
            du_ref[:, F + j * tf:F + (j + 1) * tf] = (dav * silu).astype(BF16)

    return pl.pallas_call(
        body, name=name, grid=(T // tr,),
        in_specs=[pl.BlockSpec((tr, F2), lambda i: (i, 0)), pl.BlockSpec((tr, F), lambda i: (i, 0))],
        out_specs=pl.BlockSpec((tr, F2), lambda i: (i, 0)),
        out_shape=jax.ShapeDtypeStruct((T, F2), BF16), compiler_params=_params(("parallel",)),
    )(u, da)


def _loss_head(y, t, name):
    T, D = y.shape
    tr = _pick(T, (512, 256, 128))

    def body(y_ref, t_ref, dy_ref, l_ref):
        e = y_ref[...] - t_ref[...]
        dy_ref[...] = e * (1.0 / D)

        @pl.when(pl.program_id(0) == 0)
        def _():
            l_ref[...] = jnp.zeros_like(l_ref)

        l_ref[...] += 0.5 * jnp.sum(jnp.mean(e * e, axis=-1, keepdims=True), axis=0, keepdims=True)

    row = pl.BlockSpec((tr, D), lambda i: (i, 0))
    return pl.pallas_call(
        body, name=name, grid=(T // tr,),
        in_specs=[row, row], out_specs=[row, pl.BlockSpec((1, 1), lambda i: (0, 0))],
        out_shape=[jax.ShapeDtypeStruct((T, D), F32), jax.ShapeDtypeStruct((1, 1), F32)],
        compiler_params=_params(("arbitrary",)),
    )(y, t)


def _rope_tables(S):
    half = ROPE_DIM // 2
    inv = 1.0 / (ROPE_THETA ** (jnp.arange(0, ROPE_DIM, 2, dtype=F32) / ROPE_DIM))
    ang = jnp.arange(S, dtype=F32)[:, None] * inv[None, :]
    cos, sin = jnp.cos(ang), jnp.sin(ang)
    z = jnp.zeros((S, half), F32)
    z2 = jnp.zeros((S, LANES - ROPE_DIM), F32)
    c = jnp.concatenate([cos, cos, z2], axis=1)
    s1 = jnp.concatenate([-sin, z, z2], axis=1)
    s2 = jnp.concatenate([z, sin, z2], axis=1)
    return c, s1, s2


def _rope(r, c, s1, s2):
    return r * c + pltpu.roll(r, LANES - ROPE_DIM // 2, 1) * s1 + pltpu.roll(r, ROPE_DIM // 2, 1) * s2


def _rope_t(d, c, s1, s2):
    return d * c + pltpu.roll(d * s1, ROPE_DIM // 2, 1) + pltpu.roll(d * s2, LANES - ROPE_DIM // 2, 1)


def _lat_norm_fwd(lat, g_cq, g_ckv, name):
    T = lat.shape[0]
    tr = _pick(T, (512, 256, 128))

    def body(lat_ref, gq_ref, gk_ref, cq_ref, ckv_ref):
        for off, g_ref, o_ref in ((0, gq_ref, cq_ref), (Q_LORA, gk_ref, ckv_ref)):
            xv = lat_ref[:, off:off + Q_LORA]
            r = lax.rsqrt(jnp.mean(xv * xv, axis=-1, keepdims=True) + EPS)
            o_ref[...] = ((xv * r) * g_ref[...]).astype(BF16)

    vec = pl.BlockSpec((1, Q_LORA), lambda i: (0, 0))
    out = pl.BlockSpec((tr, Q_LORA), lambda i: (i, 0))
    return pl.pallas_call(
        body, name=name, grid=(T // tr,),
        in_specs=[pl.BlockSpec((tr, LAT_PAD), lambda i: (i, 0)), vec, vec], out_specs=[out, out],
        out_shape=[jax.ShapeDtypeStruct((T, Q_LORA), BF16)] * 2, compiler_params=_params(("parallel",)),
    )(lat, g_cq, g_ckv)


def _lat_norm_bwd(lat, g_cq, g_ckv, dcq, dckv, dkpe, name):
    T = lat.shape[0]
    tr = _pick(T, (256, 128))

    def body(lat_ref, gq_ref, gk_ref, dcq_ref, dckv_ref, dkpe_ref, dlat_ref, dgq_ref, dgk_ref):
        @pl.when(pl.program_id(0) == 0)
        def _():
            dgq_ref[...] = jnp.zeros_like(dgq_ref)
            dgk_ref[...] = jnp.zeros_like(dgk_ref)

        for off, g_ref, d_ref, dg_ref in ((0, gq_ref, dcq_ref, dgq_ref), (Q_LORA, gk_ref, dckv_ref, dgk_ref)):
            xv = lat_ref[:, off:off + Q_LORA]
            dv = d_ref[...]
            r = lax.rsqrt(jnp.mean(xv * xv, axis=-1, keepdims=True) + EPS)
            xhat = xv * r
            dxh = dv * g_ref[...]
            c = jnp.mean(dxh * xhat, axis=-1, keepdims=True)
            dlat_ref[:, off:off + Q_LORA] = (r * (dxh - xhat * c)).astype(BF16)
            dg_ref[...] += jnp.sum(dv * xhat, axis=0, keepdims=True)
        dlat_ref[:, Q_LORA + KV_LORA:] = dkpe_ref[...].astype(BF16)

    vec = pl.BlockSpec((1, Q_LORA), lambda i: (0, 0))
    half = pl.BlockSpec((tr, Q_LORA), lambda i: (i, 0))
    full = pl.BlockSpec((tr, LAT_PAD), lambda i: (i, 0))
    return pl.pallas_call(
        body, name=name, grid=(T // tr,),
        in_specs=[full, vec, vec, half, half, pl.BlockSpec((tr, LANES), lambda i: (i, 0))],
        out_specs=[full, vec, vec],
        out_shape=[jax.ShapeDtypeStruct((T, LAT_PAD), BF16), jax.ShapeDtypeStruct((1, Q_LORA), F32),
                   jax.ShapeDtypeStruct((1, Q_LORA), F32)],
        compiler_params=_params(("arbitrary",)),
    )(lat, g_cq, g_ckv, dcq, dckv, dkpe)


def _mla_prep_fwd(q_raw, kv, lat, g_qn, g_kn, tabs, name):
    T = q_raw.shape[0]
    H = MLA_HEADS
    tr = _pick(T, (256, 128))

    def body(q_ref, kv_ref, kpe_ref, gq_ref, gk_ref, c_ref, s1_ref, s2_ref, qf_ref, kf_ref, v_ref):
        c, s1, s2 = c_ref[...], s1_ref[...], s2_ref[...]
        gq, gk = gq_ref[...], gk_ref[...]
        kpe = kpe_ref[...]
        kpe_ss = jnp.sum(kpe * kpe, axis=-1, keepdims=True)
        for h in range(H):
            lo = h * HEAD_PAD
            qa = q_ref[:, lo:lo + LANES]
            qb = q_ref[:, lo + LANES:lo + HEAD_PAD]
            ss = jnp.sum(qa * qa, axis=-1, keepdims=True) + jnp.sum(qb * qb, axis=-1, keepdims=True)
            r = lax.rsqrt(ss * (1.0 / QK_DIM) + EPS)
            qf_ref[:, lo:lo + LANES] = (qa * r * gq[:, :LANES]).astype(BF16)
            qf_ref[:, lo + LANES:lo + HEAD_PAD] = _rope(qb * r * gq[:, LANES:], c, s1, s2).astype(BF16)
            ka = kv_ref[:, lo:lo + LANES]
            ss = jnp.sum(ka * ka, axis=-1, keepdims=True) + kpe_ss
            r = lax.rsqrt(ss * (1.0 / QK_DIM) + EPS)
            kf_ref[:, lo:lo + LANES] = (ka * r * gk[:, :LANES]).astype(BF16)
            kf_ref[:, lo + LANES:lo + HEAD_PAD] = _rope(kpe * r * gk[:, LANES:], c, s1, s2).astype(BF16)
            v_ref[:, h * V_DIM:(h + 1) * V_DIM] = kv_ref[:, lo + LANES:lo + HEAD_PAD].astype(BF16)

    wide = pl.BlockSpec((tr, H * HEAD_PAD), lambda i: (i, 0))
    lane = pl.BlockSpec((tr, LANES), lambda i: (i, 0))
    gvec = pl.BlockSpec((1, HEAD_PAD), lambda i: (0, 0))
    return pl.pallas_call(
        body, name=name, grid=(T // tr,),
        in_specs=[wide, wide, pl.BlockSpec((tr, LANES), lambda i: (i, (Q_LORA + KV_LORA) // LANES)), gvec, gvec,
                  lane, lane, lane],
        out_specs=[wide, wide, pl.BlockSpec((tr, H * V_DIM), lambda i: (i, 0))],
        out_shape=[jax.ShapeDtypeStruct((T, H * HEAD_PAD), BF16), jax.ShapeDtypeStruct((T, H * HEAD_PAD), BF16),
                   jax.ShapeDtypeStruct((T, H * V_DIM), BF16)],
        compiler_params=_params(("parallel",)),
    )(q_raw, kv, lat, g_qn, g_kn, *tabs)


def _mla_prep_bwd(q_raw, kv, lat, g_qn, g_kn, tabs, dqf, dkf, dv, name):
    T = q_raw.shape[0]
    H = MLA_HEADS
    tr = _pick(T, (128,))

    def body(q_ref, kv_ref, kpe_ref, gq_ref, gk_ref, c_ref, s1_ref, s2_ref, dqf_ref, dkf_ref, dv_ref,
             dq_ref, dkv_ref, dkpe_ref, dgq_ref, dgk_ref):
        @pl.when(pl.program_id(0) == 0)
        def _():
            dgq_ref[...] = jnp.zeros_like(dgq_ref)
            dgk_ref[...] = jnp.zeros_like(dgk_ref)

        c, s1, s2 = c_ref[...], s1_ref[...], s2_ref[...]
        gq, gk = gq_ref[...], gk_ref[...]
        kpe = kpe_ref[...]
        kpe_ss = jnp.sum(kpe * kpe, axis=-1, keepdims=True)
        dkpe = jnp.zeros_like(kpe)
        dgq_a = jnp.zeros((1, LANES), F32)
        dgq_b = jnp.zeros((1, LANES), F32)
        dgk_a = jnp.zeros((1, LANES), F32)
        dgk_b = jnp.zeros((1, LANES), F32)
        for h in range(H):
            lo = h * HEAD_PAD
            xa = q_ref[:, lo:lo + LANES]
            xb = q_ref[:, lo + LANES:lo + HEAD_PAD]
            ss = jnp.sum(xa * xa, axis=-1, keepdims=True) + jnp.sum(xb * xb, axis=-1, keepdims=True)
            r = lax.rsqrt(ss * (1.0 / QK_DIM) + EPS)
            xa, xb = xa * r, xb * r
            da = dqf_ref[:, lo:lo + LANES]
            db = _rope_t(dqf_ref[:, lo + LANES:lo + HEAD_PAD], c, s1, s2)
            dgq_a += jnp.sum(da * xa, axis=0, keepdims=True)
            dgq_b += jnp.sum(db * xb, axis=0, keepdims=True)
            da, db = da * gq[:, :LANES], db * gq[:, LANES:]
            cc = (jnp.sum(da * xa, axis=-1, keepdims=True) + jnp.sum(db * xb, axis=-1, keepdims=True)) * (1.0 / QK_DIM)
            dq_ref[:, lo:lo + LANES] = (r * (da - xa * cc)).astype(BF16)
            dq_ref[:, lo + LANES:lo + HEAD_PAD] = (r * (db - xb * cc)).astype(BF16)
            xa = kv_ref[:, lo:lo + LANES]
            ss = jnp.sum(xa * xa, axis=-1, keepdims=True) + kpe_ss
            r = lax.rsqrt(ss * (1.0 / QK_DIM) + EPS)
            xa, xb = xa * r, kpe * r
            da = dkf_ref[:, lo:lo + LANES]
            db = _rope_t(dkf_ref[:, lo + LANES:lo + HEAD_PAD], c, s1, s2)
            dgk_a += jnp.sum(da * xa, axis=0, keepdims=True)
            dgk_b += jnp.sum(db * xb, axis=0, keepdims=True)
            da, db = da * gk[:, :LANES], db * gk[:, LANES:]
            cc = (jnp.sum(da * xa, axis=-1, keepdims=True) + jnp.sum(db * xb, axis=-1, keepdims=True)) * (1.0 / QK_DIM)
            dkv_ref[:, lo:lo + LANES] = (r * (da - xa * cc)).astype(BF16)
            dkpe = dkpe + r * (db - xb * cc)
            dkv_ref[:, lo + LANES:lo + HEAD_PAD] = dv_ref[:, h * V_DIM:(h + 1) * V_DIM].astype(BF16)
        dkpe_ref[...] = dkpe
        dgq_ref[:, :LANES] += dgq_a
        dgq_ref[:, LANES:] += dgq_b
        dgk_ref[:, :LANES] += dgk_a
        dgk_ref[:, LANES:] += dgk_b

    wide = pl.BlockSpec((tr, H * HEAD_PAD), lambda i: (i, 0))
    lane = pl.BlockSpec((tr, LANES), lambda i: (i, 0))
    gvec = pl.BlockSpec((1, HEAD_PAD), lambda i: (0, 0))
    vspec = pl.BlockSpec((tr, H * V_DIM), lambda i: (i, 0))
    return pl.pallas_call(
        body, name=name, grid=(T // tr,),
        in_specs=[wide, wide, pl.BlockSpec((tr, LANES), lambda i: (i, (Q_LORA + KV_LORA) // LANES)), gvec, gvec,
                  lane, lane, lane, wide, wide, vspec],
        out_specs=[wide, wide, lane, gvec, gvec],
        out_shape=[jax.ShapeDtypeStruct((T, H * HEAD_PAD), BF16), jax.ShapeDtypeStruct((T, H * HEAD_PAD), BF16),
                   jax.ShapeDtypeStruct((T, LANES), F32), jax.ShapeDtypeStruct((1, HEAD_PAD), F32),
                   jax.ShapeDtypeStruct((1, HEAD_PAD), F32)],
        compiler_params=_params(("arbitrary",)),
    )(q_raw, kv, lat, g_qn, g_kn, *tabs, dqf, dkf, dv)


def _causal_mask(tq, tk):
    return lax.broadcasted_iota(jnp.int32, (tq, tk), 1) <= lax.broadcasted_iota(jnp.int32, (tq, tk), 0)


NT_DIMS = (((1,), (1,)), ((), ()))
TN_DIMS = (((0,), (0,)), ((), ()))


def _flash_fwd(qf, kf, v, name):
    T = qf.shape[0]
    H = MLA_HEADS
    t = _pick(T, (512, 256, 128))
    n = T // t
    scale = 1.0 / math.sqrt(QK_DIM)

    def body(q_ref, k_ref, v_ref, o_ref, lse_ref, m_sc, l_sc, acc_sc):
        i, j = pl.program_id(1), pl.program_id(2)

        @pl.when(j == 0)
        def _():
            m_sc[...] = jnp.full_like(m_sc, NEG_BIG)
            l_sc[...] = jnp.zeros_like(l_sc)
            acc_sc[...] = jnp.zeros_like(acc_sc)

        def step(masked):
            s = lax.dot_general(q_ref[...], k_ref[...], NT_DIMS, preferred_element_type=F32) * scale
            if masked:
                s = jnp.where(_causal_mask(t, t), s, NEG_BIG)
            m_prev = m_sc[:, :1]
            m_new = jnp.maximum(m_prev, jnp.max(s, axis=-1, keepdims=True))
            a = jnp.exp(m_prev - m_new)
            p = jnp.exp(s - m_new)
            l_sc[...] = a * l_sc[...] + jnp.sum(p, axis=-1, keepdims=True)
            acc_sc[...] = a * acc_sc[...] + jnp.dot(p.astype(BF16), v_ref[...], preferred_element_type=F32)
            m_sc[...] = jnp.broadcast_to(m_new, m_sc.shape)

        @pl.when(j < i)
        def _():
            step(False)

        @pl.when(j == i)
        def _():
            step(True)
            o_ref[...] = (acc_sc[...] / l_sc[...]).astype(BF16)
            lse_ref[...] = m_sc[...] + jnp.log(l_sc[...])

    return pl.pallas_call(
        body, name=name, grid=(H, n, n),
        in_specs=[pl.BlockSpec((t, HEAD_PAD), lambda h, i, j: (i, h)),
                  pl.BlockSpec((t, HEAD_PAD), lambda h, i, j: (jnp.minimum(j, i), h)),
                  pl.BlockSpec((t, V_DIM), lambda h, i, j: (jnp.minimum(j, i), h))],
        out_specs=[pl.BlockSpec((t, V_DIM), lambda h, i, j: (i, h)), pl.BlockSpec((t, V_DIM), lambda h, i, j: (i, h))],
        out_shape=[jax.ShapeDtypeStruct((T, H * V_DIM), BF16), jax.ShapeDtypeStruct((T, H * V_DIM), F32)],
        scratch_shapes=[pltpu.VMEM((t, LANES), F32), pltpu.VMEM((t, LANES), F32), pltpu.VMEM((t, V_DIM), F32)],
        compiler_params=_params(("parallel", "parallel", "arbitrary")),
    )(qf, kf, v)


def _attn_delta(do, o, name):
    T, W = do.shape
    nh = W // V_DIM
    tr = _pick(T, (512, 256, 128))

    def body(do_ref, o_ref, d_ref):
        for h in range(nh):
            sl = slice(h * V_DIM, (h + 1) * V_DIM)
            d = jnp.sum(do_ref[:, sl].astype(F32) * o_ref[:, sl].astype(F32), axis=-1, keepdims=True)
            d_ref[:, sl] = jnp.broadcast_to(d, (tr, V_DIM))

    row = pl.BlockSpec((tr, W), lambda i: (i, 0))
    return pl.pallas_call(
        body, name=name, grid=(T // tr,), in_specs=[row, row], out_specs=row,
        out_shape=jax.ShapeDtypeStruct((T, W), F32), compiler_params=_params(("parallel",)),
    )(do, o)


def _flash_bwd_dq(qf, kf, v, do, lse, delta, name):
    T = qf.shape[0]
    H = MLA_HEADS
    t = _pick(T, (512, 256, 128))
    n = T // t
    scale = 1.0 / math.sqrt(QK_DIM)

    def body(q_ref, k_ref, v_ref, do_ref, lse_ref, dl_ref, dq_ref, acc):
        i, j = pl.program_id(1), pl.program_id(2)

        @pl.when(j == 0)
        def _():
            acc[...] = jnp.zeros_like(acc)

        def step(masked):
            s = lax.dot_general(q_ref[...], k_ref[...], NT_DIMS, preferred_element_type=F32) * scale
            if masked:
                s = jnp.where(_causal_mask(t, t), s, NEG_BIG)
            p = jnp.exp(s - lse_ref[:, :1])
            dp = lax.dot_general(do_ref[...], v_ref[...], NT_DIMS, preferred_element_type=F32)
            ds = (p * (dp - dl_ref[:, :1])).astype(BF16)
            acc[...] += jnp.dot(ds, k_ref[...], preferred_element_type=F32)

        @pl.when(j < i)
        def _():
            step(False)

        @pl.when(j == i)
        def _():
            step(True)
            dq_ref[...] = acc[...] * scale

    qs = pl.BlockSpec((t, HEAD_PAD), lambda h, i, j: (i, h))
    ks = pl.BlockSpec((t, HEAD_PAD), lambda h, i, j: (jnp.minimum(j, i), h))
    vs = pl.BlockSpec((t, V_DIM), lambda h, i, j: (jnp.minimum(j, i), h))
    rs = pl.BlockSpec((t, V_DIM), lambda h, i, j: (i, h))
    return pl.pallas_call(
        body, name=name, grid=(H, n, n), in_specs=[qs, ks, vs, rs, rs, rs], out_specs=qs,
        out_shape=jax.ShapeDtypeStruct((T, H * HEAD_PAD), F32),
        scratch_shapes=[pltpu.VMEM((t, HEAD_PAD), F32)],
        compiler_params=_params(("parallel", "parallel", "arbitrary")),
    )(qf, kf, v, do, lse, delta)


def _flash_bwd_dkv(qf, kf, v, do, lse, delta, name):
    T = qf.shape[0]
    H = MLA_HEADS
    t = _pick(T, (512, 256, 128))
    n = T // t
    scale = 1.0 / math.sqrt(QK_DIM)

    def body(q_ref, k_ref, v_ref, do_ref, lse_ref, dl_ref, dk_ref, dv_ref, dk_acc, dv_acc):
        j, i = pl.program_id(1), pl.program_id(2)

        @pl.when(i == 0)
        def _():
            dk_acc[...] = jnp.zeros_like(dk_acc)
            dv_acc[...] = jnp.zeros_like(dv_acc)

        def step(masked):
            s = lax.dot_general(q_ref[...], k_ref[...], NT_DIMS, preferred_element_type=F32) * scale
            if masked:
                s = jnp.where(_causal_mask(t, t), s, NEG_BIG)
            p = jnp.exp(s - lse_ref[:, :1])
            dp = lax.dot_general(do_ref[...], v_ref[...], NT_DIMS, preferred_element_type=F32)
            ds = (p * (dp - dl_ref[:, :1])).astype(BF16)
            dv_acc[...] += lax.dot_general(p.astype(BF16), do_ref[...], TN_DIMS, preferred_element_type=F32)
            dk_acc[...] += lax.dot_general(ds, q_ref[...], TN_DIMS, preferred_element_type=F32)

        @pl.when(i == j)
        def _():
            step(True)

        @pl.when(i > j)
        def _():
            step(False)

        @pl.when(i == n - 1)
        def _():
            dk_ref[...] = dk_acc[...] * scale
            dv_ref[...] = dv_acc[...]

    qs = pl.BlockSpec((t, HEAD_PAD), lambda h, j, i: (jnp.maximum(i, j), h))
    rs = pl.BlockSpec((t, V_DIM), lambda h, j, i: (jnp.maximum(i, j), h))
    ks = pl.BlockSpec((t, HEAD_PAD), lambda h, j, i: (j, h))
    vs = pl.BlockSpec((t, V_DIM), lambda h, j, i: (j, h))
    return pl.pallas_call(
        body, name=name, grid=(H, n, n), in_specs=[qs, ks, vs, rs, rs, rs], out_specs=[ks, vs],
        out_shape=[jax.ShapeDtypeStruct((T, H * HEAD_PAD), F32), jax.ShapeDtypeStruct((T, H * V_DIM), F32)],
        scratch_shapes=[pltpu.VMEM((t, HEAD_PAD), F32), pltpu.VMEM((t, V_DIM), F32)],
        compiler_params=_params(("parallel", "parallel", "arbitrary")),
    )(qf, kf, v, do, lse, delta)


def _alibi_slopes():
    tot = DIL_GROUPS * DIL_HEADS
    return [float(np.float32(2.0) ** (np.float32(-8.0) * np.float32(k) / np.float32(tot))) for k in range(1, tot + 1)]


def _dil_masks():
    iq = lax.broadcasted_iota(jnp.int32, (DIL_BLK, DIL_BLK), 0)
    ik = lax.broadcasted_iota(jnp.int32, (DIL_BLK, DIL_BLK), 1)
    return (ik >= iq), (iq + DIL_BLK - ik).astype(F32), (ik <= iq), (iq - ik).astype(F32)


def _dil_norm(x, g):
    r = lax.rsqrt(jnp.mean(x * x, axis=-1, keepdims=True) + EPS)
    return x * r, r


def _dil_fwd(qkv, g_qn, g_kn, slopes, name):
    T = qkv.shape[0]
    GH = DIL_GROUPS * DIL_HEADS
    scale = 1.0 / math.sqrt(DIL_HEAD_DIM)

    def body(sl_ref, q_ref, k_ref, v_ref, gq_ref, gk_ref, o_ref, lse_ref):
        gh = pl.program_id(0)
        slope = sl_ref[gh]
        ok_p, dist_p, ok_c, dist_c = _dil_masks()
        gq, gk = gq_ref[...], gk_ref[...]
        for g, (_, d) in enumerate(DIL_PAIRS):
            @pl.when((gh >= g * DIL_HEADS) & (gh < (g + 1) * DIL_HEADS))
            def _(d=d):
                nb = T // (d * DIL_BLK)
                bias_p = jnp.where(ok_p, -slope * d * dist_p, NEG_BIG)
                bias_c = jnp.where(ok_c, -slope * d * dist_c, NEG_BIG)

                def phase(r, _):
                    def blk(nn, _):
                        def rows(b):
                            return pl.ds(b * (d * DIL_BLK) + r, DIL_BLK, stride=d) if d > 1 else pl.ds(pl.multiple_of(b * DIL_BLK, DIL_BLK), DIL_BLK)
                        cur, prv = rows(nn), rows(jnp.maximum(nn - 1, 0))
                        q = (_dil_norm(q_ref[cur, :], gq)[0] * gq).astype(BF16)
                        kc = (_dil_norm(k_ref[cur, :], gk)[0] * gk).astype(BF16)
                        kp = (_dil_norm(k_ref[prv, :], gk)[0] * gk).astype(BF16)
                        s_c = lax.dot_general(q, kc, NT_DIMS, preferred_element_type=F32) * scale + bias_c
                        s_p = lax.dot_general(q, kp, NT_DIMS, preferred_element_type=F32) * scale + bias_p
                        s_p = jnp.where(nn > 0, s_p, NEG_BIG)
                        m = jnp.maximum(jnp.max(s_c, axis=-1, keepdims=True), jnp.max(s_p, axis=-1, keepdims=True))
                        p_c = jnp.exp(s_c - m)
                        p_p = jnp.exp(s_p - m)
                        l = jnp.sum(p_c, axis=-1, keepdims=True) + jnp.sum(p_p, axis=-1, keepdims=True)
                        acc = jnp.dot(p_c.astype(BF16), v_ref[cur, :].astype(BF16), preferred_element_type=F32)
                        acc += jnp.dot(p_p.astype(BF16), v_ref[prv, :].astype(BF16), preferred_element_type=F32)
                        o_ref[cur, :] = acc / l
                        lse_ref[cur, :] = jnp.broadcast_to(m + jnp.log(l), (DIL_BLK, DIL_HEAD_DIM))
                        return 0
                    lax.fori_loop(0, nb, blk, 0)
                    return 0
                lax.fori_loop(0, d, phase, 0)

    col = lambda off: pl.BlockSpec((T, DIL_HEAD_DIM), lambda gh, sl: (0, gh + off))
    gvec = pl.BlockSpec((1, DIL_HEAD_DIM), lambda gh, sl: (0, 0))
    return pl.pallas_call(
        body, name=name,
        grid_spec=pltpu.PrefetchScalarGridSpec(
            num_scalar_prefetch=1, grid=(GH,),
            in_specs=[col(0), col(GH), col(2 * GH), gvec, gvec], out_specs=[col(0), col(0)]),
        out_shape=[jax.ShapeDtypeStruct((T, GH * DIL_HEAD_DIM), F32)] * 2,
        compiler_params=_params(("parallel",)),
    )(slopes, qkv, qkv, qkv, g_qn, g_kn)


def _dil_merge(o_g, lse_g, name):
    T = o_g.shape[0]
    W = DIL_HEADS * DIL_HEAD_DIM
    tr = _pick(T, (256, 128))

    def body(o0, o1, o2, l0, l1, l2, o_ref, lse_ref):
        a, b, c = l0[...], l1[...], l2[...]
        m = jnp.maximum(jnp.maximum(a, b), c)
        ea, eb, ec = jnp.exp(a - m), jnp.exp(b - m), jnp.exp(c - m)
        tot = ea + eb + ec
        o_ref[...] = ((o0[...] * ea + o1[...] * eb + o2[...] * ec) / tot).astype(BF16)
        lse_ref[...] = m + jnp.log(tot)

    grp = lambda g: pl.BlockSpec((tr, W), lambda i: (i, g))
    out = pl.BlockSpec((tr, W), lambda i: (i, 0))
    return pl.pallas_call(
        body, name=name, grid=(T // tr,),
        in_specs=[grp(0), grp(1), grp(2), grp(0), grp(1), grp(2)], out_specs=[out, out],
        out_shape=[jax.ShapeDtypeStruct((T, W), BF16), jax.ShapeDtypeStruct((T, W), F32)],
        compiler_params=_params(("parallel",)),
    )(o_g, o_g, o_g, lse_g, lse_g, lse_g)


def _dil_bwd(qkv, g_qn, g_kn, slopes, do, delta, lse, name):
    T = qkv.shape[0]
    GH = DIL_GROUPS * DIL_HEADS
    scale = 1.0 / math.sqrt(DIL_HEAD_DIM)
    nchunk = T // DIL_BLK

    def body(sl_ref, q_ref, k_ref, v_ref, gq_ref, gk_ref, do_ref, dl_ref, lse_ref,
             dq_ref, dk_ref, dv_ref, dgq_ref, dgk_ref, dq_acc, dk_acc, dv_acc):
        gh = pl.program_id(0)
        slope = sl_ref[gh]
        ok_p, dist_p, ok_c, dist_c = _dil_masks()
        gq, gk = gq_ref[...], gk_ref[...]

        @pl.when(gh == 0)
        def _():
            dgq_ref[...] = jnp.zeros_like(dgq_ref)
            dgk_ref[...] = jnp.zeros_like(dgk_ref)

        dk_acc[...] = jnp.zeros_like(dk_acc)
        dv_acc[...] = jnp.zeros_like(dv_acc)
        for g, (_, d) in enumerate(DIL_PAIRS):
            @pl.when((gh >= g * DIL_HEADS) & (gh < (g + 1) * DIL_HEADS))
            def _(d=d):
                nb = T // (d * DIL_BLK)
                bias_p = jnp.where(ok_p, -slope * d * dist_p, NEG_BIG)
                bias_c = jnp.where(ok_c, -slope * d * dist_c, NEG_BIG)

                def phase(r, _):
                    def blk(nn, _):
                        def rows(b):
                            return pl.ds(b * (d * DIL_BLK) + r, DIL_BLK, stride=d) if d > 1 else pl.ds(pl.multiple_of(b * DIL_BLK, DIL_BLK), DIL_BLK)
                        cur, prv = rows(nn), rows(jnp.maximum(nn - 1, 0))
                        q = (_dil_norm(q_ref[cur, :], gq)[0] * gq).astype(BF16)
                        kc = (_dil_norm(k_ref[cur, :], gk)[0] * gk).astype(BF16)
                        kp = (_dil_norm(k_ref[prv, :], gk)[0] * gk).astype(BF16)
                        vc = v_ref[cur, :].astype(BF16)
                        vp = v_ref[prv, :].astype(BF16)
                        dob = do_ref[cur, :].astype(BF16)
                        delta = dl_ref[cur, :][:, :1]
                        ls = lse_ref[cur, :][:, :1]
                        s_c = lax.dot_general(q, kc, NT_DIMS, preferred_element_type=F32) * scale + bias_c
                        s_p = lax.dot_general(q, kp, NT_DIMS, preferred_element_type=F32) * scale + bias_p
                        s_p = jnp.where(nn > 0, s_p, NEG_BIG)
                        p_c = jnp.exp(s_c - ls)
                        p_p = jnp.exp(s_p - ls)
                        dp_c = lax.dot_general(dob, vc, NT_DIMS, preferred_element_type=F32)
                        dp_p = lax.dot_general(dob, vp, NT_DIMS, preferred_element_type=F32)
                        ds_c = (p_c * (dp_c - delta)).astype(BF16)
                        ds_p = (p_p * (dp_p - delta)).astype(BF16)
                        dq_acc[cur, :] = (jnp.dot(ds_c, kc, preferred_element_type=F32)
                                          + jnp.dot(ds_p, kp, preferred_element_type=F32)) * scale
                        dk_acc[cur, :] += lax.dot_general(ds_c, q, TN_DIMS, preferred_element_type=F32) * scale
                        dv_acc[cur, :] += lax.dot_general(p_c.astype(BF16), dob, TN_DIMS, preferred_element_type=F32)
                        dk_acc[prv, :] += lax.dot_general(ds_p, q, TN_DIMS, preferred_element_type=F32) * scale
                        dv_acc[prv, :] += lax.dot_general(p_p.astype(BF16), dob, TN_DIMS, preferred_element_type=F32)
                        return 0
                    lax.fori_loop(0, nb, blk, 0)
                    return 0
                lax.fori_loop(0, d, phase, 0)

        def fin(ci, carry):
            dgq, dgk = carry
            rows = pl.ds(pl.multiple_of(ci * DIL_BLK, DIL_BLK), DIL_BLK)
            outs = []
            for x_ref, d_acc, gvec in ((q_ref, dq_acc, gq), (k_ref, dk_acc, gk)):
                xhat, r = _dil_norm(x_ref[rows, :], gvec)
                dn = d_acc[rows, :]
                dxh = dn * gvec
                c = jnp.mean(dxh * xhat, axis=-1, keepdims=True)
                outs.append(((r * (dxh - xhat * c)).astype(BF16), jnp.sum(dn * xhat, axis=0, keepdims=True)))
            dq_ref[rows, :] = outs[0][0]
            dk_ref[rows, :] = outs[1][0]
            dv_ref[rows, :] = dv_acc[rows, :].astype(BF16)
            return dgq + outs[0][1], dgk + outs[1][1]

        z = jnp.zeros((1, DIL_HEAD_DIM), F32)
        dgq, dgk = lax.fori_loop(0, nchunk, fin, (z, z))
        dgq_ref[...] += dgq
        dgk_ref[...] += dgk

    col = lambda off: pl.BlockSpec((T, DIL_HEAD_DIM), lambda gh, sl: (0, gh + off))
    hcol = pl.BlockSpec((T, DIL_HEAD_DIM), lambda gh, sl: (0, gh % DIL_HEADS))
    gvec = pl.BlockSpec((1, DIL_HEAD_DIM), lambda gh, sl: (0, 0))
    wide = jax.ShapeDtypeStruct((T, GH * DIL_HEAD_DIM), BF16)
    vec = jax.ShapeDtypeStruct((1, DIL_HEAD_DIM), F32)
    return pl.pallas_call(
        body, name=name,
        grid_spec=pltpu.PrefetchScalarGridSpec(
            num_scalar_prefetch=1, grid=(GH,),
            in_specs=[col(0), col(GH), col(2 * GH), gvec, gvec, hcol, hcol, hcol],
            out_specs=[col(0), col(0), col(0), gvec, gvec],
            scratch_shapes=[pltpu.VMEM((T, DIL_HEAD_DIM), F32)] * 3),
        out_shape=[wide, wide, wide, vec, vec],
        compiler_params=_params(("arbitrary",)),
    )(slopes, qkv, qkv, qkv, g_qn, g_kn, do, delta, lse)


def _my_pos():
    return lax.axis_index("x"), lax.axis_index("y"), lax.axis_index("c")


def _peer(pos, j):
    x, y, c = pos
    px = 1 - x if j & 4 else x
    py = 1 - y if j & 2 else y
    pc = 1 - c if j & 1 else c
    return (px, py, pc), 4 * px + 2 * py + pc


def _shard_slice(ref, axis, idx, size):
    sl = [slice(None)] * len(ref.shape)
    sl[axis] = pl.ds(pl.multiple_of(idx * size, 8), size)
    return ref.at[tuple(sl)]


HBM_SPEC = pl.BlockSpec(memory_space=pltpu.HBM)
SEM_SPEC = pl.BlockSpec(memory_space=pltpu.SEMAPHORE)
DATAFLOW = pltpu.SideEffectType.DATAFLOW_SIDE_EFFECTING
N_PEER = N_DEV - 1


def _xchg_refs(kind, axis, in_ref, land_ref, frm, to):
    if kind == "gather":
        return in_ref, _shard_slice(land_ref, axis, frm, in_ref.shape[axis])
    return _shard_slice(in_ref, axis, to, in_ref.shape[axis] // N_DEV), land_ref.at[frm]


def _land_shape(kind, axis, shape):
    shp = list(shape)
    if kind == "gather":
        shp[axis] *= N_DEV
        return tuple(shp)
    shp[axis] //= N_DEV
    return (N_DEV,) + tuple(shp)


def _xchg_start(kind, arrays, axes, name):
    n = len(arrays)

    def body(*refs):
        ins, lands = refs[:n], refs[n:2 * n]
        outs = refs[2 * n:]
        send, recv, token, loc_sems = outs[:n], outs[n:2 * n], outs[4 * n], outs[4 * n + 1]
        pos = _my_pos()
        me = 4 * pos[0] + 2 * pos[1] + pos[2]
        locals_ = []
        for a in range(n):
            src, dst = _xchg_refs(kind, axes[a], ins[a], lands[a], me, me)
            lc = pltpu.make_async_copy(src, dst, loc_sems.at[a])
            lc.start()
            locals_.append(lc)
            for j in range(1, N_DEV):
                dev, pid = _peer(pos, j)
                src, dst = _xchg_refs(kind, axes[a], ins[a], lands[a], me, pid)
                pltpu.make_async_remote_copy(src_ref=src, dst_ref=dst, send_sem=send[a].at[j - 1],
                                             recv_sem=recv[a].at[j - 1], device_id=dev, device_id_type=MESH).start()
        token[...] = jnp.zeros_like(token)
        for lc in locals_:
            lc.wait()

    lands = [lax.empty(_land_shape(kind, ax, a.shape), a.dtype) for a, ax in zip(arrays, axes)]
    sems = [pltpu.SemaphoreType.DMA((N_PEER,))] * (2 * n)
    thru = [pltpu.HBM(a.shape, a.dtype) for a in arrays] + [pltpu.HBM(l.shape, l.dtype) for l in lands]
    res = pl.pallas_call(
        body, name=name,
        out_shape=sems + thru + [jax.ShapeDtypeStruct((8, LANES), F32)],
        in_specs=[HBM_SPEC] * (2 * n),
        out_specs=[SEM_SPEC] * (2 * n) + [HBM_SPEC] * (2 * n) + [pl.BlockSpec(memory_space=pltpu.VMEM)],
        scratch_shapes=[pltpu.SemaphoreType.DMA((n,))],
        input_output_aliases={i: 2 * n + i for i in range(2 * n)},
        compiler_params=pltpu.CompilerParams(has_side_effects=DATAFLOW),
    )(*[pltpu.with_memory_space_constraint(a, pltpu.HBM) for a in arrays],
      *[pltpu.with_memory_space_constraint(l, pltpu.HBM) for l in lands])
    items = [(res[a], res[n + a], res[2 * n + a], res[3 * n + a]) for a in range(n)]
    return items, res[4 * n]


def _xchg_wait(kind, items, axes, after, name):
    n = len(items)

    def body(*refs):
        ins, lands = refs[:n], refs[n:2 * n]
        send, recv = refs[2 * n:3 * n], refs[3 * n:4 * n]
        pos = _my_pos()
        me = 4 * pos[0] + 2 * pos[1] + pos[2]
        for a in range(n):
            for j in range(1, N_DEV):
                dev, pid = _peer(pos, j)
                src, dst = _xchg_refs(kind, axes[a], ins[a], lands[a], pid, me)
                cp = pltpu.make_async_remote_copy(src_ref=src, dst_ref=dst, send_sem=send[a].at[j - 1],
                                                  recv_sem=recv[a].at[j - 1], device_id=dev, device_id_type=MESH)
                cp.wait_send()
                cp.wait_recv()

    ins = [it[2] for it in items]
    lands = [it[3] for it in items]
    res = pl.pallas_call(
        body, name=name,
        out_shape=[pltpu.HBM(a.shape, a.dtype) for a in ins] + [pltpu.HBM(l.shape, l.dtype) for l in lands],
        in_specs=[HBM_SPEC] * (2 * n) + [SEM_SPEC] * (2 * n) + [pl.BlockSpec(memory_space=pl.ANY)],
        out_specs=[HBM_SPEC] * (2 * n),
        input_output_aliases={i: i for i in range(2 * n)},
        compiler_params=pltpu.CompilerParams(has_side_effects=DATAFLOW),
    )(*ins, *lands, *[it[0] for it in items], *[it[1] for it in items], after)
    return list(res[n:])


def _gain_allreduce(v, name):
    n = v.shape[1]

    def body(v_ref, o_ref, slots, send_sems, recv_sems):
        pos = _my_pos()
        me = 4 * pos[0] + 2 * pos[1] + pos[2]
        slots[me] = v_ref[...]
        copies = []
        for j in range(1, N_DEV):
            dev, _ = _peer(pos, j)
            cp = pltpu.make_async_remote_copy(
                src_ref=slots.at[me], dst_ref=slots.at[me], send_sem=send_sems.at[j], recv_sem=recv_sems.at[j],
                device_id=dev, device_id_type=MESH)
            cp.start()
            copies.append(cp)
        for j in range(1, N_DEV):
            dev, pid = _peer(pos, j)
            pltpu.make_async_remote_copy(
                src_ref=slots.at[me], dst_ref=slots.at[pid], send_sem=send_sems.at[j], recv_sem=recv_sems.at[j],
                device_id=dev, device_id_type=MESH).wait_recv()
        for cp in copies:
            cp.wait_send()
        acc = slots[0]
        for s in range(1, N_DEV):
            acc = acc + slots[s]
        o_ref[...] = acc

    return pl.pallas_call(
        body, name=name, out_shape=jax.ShapeDtypeStruct((1, n), F32),
        in_specs=[pl.BlockSpec(memory_space=pltpu.VMEM)], out_specs=pl.BlockSpec(memory_space=pltpu.VMEM),
        scratch_shapes=[pltpu.VMEM((N_DEV, 1, n), F32), pltpu.SemaphoreType.DMA((N_DEV,)),
                        pltpu.SemaphoreType.DMA((N_DEV,))],
        compiler_params=pltpu.CompilerParams(has_side_effects=True),
    )(v)


def _adamw(parts, w, m, v, layer, prev, name):
    L, R, C = w.shape
    P = parts.shape[0]
    tr = _pick(R, (128, 64, 32, 16, 8, 1))
    c1 = 1.0 - ADAM_B1 ** ADAM_STEP
    c2 = 1.0 - ADAM_B2 ** ADAM_STEP

    def body(*refs):
        p_ref, w_ref, m_ref, v_ref = refs[:4]
        g_out, d_out, m_out, v_out = refs[-4:]
        g = p_ref[0].astype(F32)
        for s in range(1, P):
            g = g + p_ref[s].astype(F32)
        mn = ADAM_B1 * m_ref[...] + (1.0 - ADAM_B1) * g
        vn = ADAM_B2 * v_ref[...] + (1.0 - ADAM_B2) * (g * g)
        g_out[...] = g
        m_out[...] = mn
        v_out[...] = vn
        d_out[...] = -ADAM_LR * ((mn / c1) / (jnp.sqrt(vn / c2) + ADAM_EPS) + ADAM_WD * w_ref[...])

    row = pl.BlockSpec((None, tr, C), lambda i: (layer, i, 0))
    in_specs = [pl.BlockSpec((P, tr, C), lambda i: (0, i, 0)), row, row, row]
    args = [parts, w, m, v]
    aliases = {}
    if prev is not None:
        in_specs += [pl.BlockSpec(memory_space=pl.ANY)] * 4
        args += list(prev)
        aliases = {4 + k: k for k in range(4)}
    shp = jax.ShapeDtypeStruct((L, R, C), F32)
    return pl.pallas_call(
        body, name=name, grid=(R // tr,), in_specs=in_specs, out_specs=[row] * 4, out_shape=[shp] * 4,
        input_output_aliases=aliases, compiler_params=_params(("parallel",)),
    )(*args)


def _pad_heads(w):
    lead = w.shape[:-1]
    n = w.shape[-1] // QK_DIM
    w = w.reshape(lead + (n, QK_DIM))
    w = jnp.pad(w, [(0, 0)] * len(lead) + [(0, 0), (0, HEAD_PAD - QK_DIM)])
    return w.reshape(lead + (n * HEAD_PAD,))


def _unpad_heads(w):
    lead = w.shape[:-1]
    n = w.shape[-1] // HEAD_PAD
    return w.reshape(lead + (n, HEAD_PAD))[..., :QK_DIM].reshape(lead + (n * QK_DIM,))


def kernel(x, ffn1_norm, ffn1_w_in, ffn1_w_out, mix_norm, ffn2_norm, ffn2_w_in, ffn2_w_out, mla_w_down, mla_g_cq, mla_g_ckv, mla_w_uq, mla_w_ukv, mla_g_qn, mla_g_kn, mla_w_o, dil_w_qkv, dil_g_qn, dil_g_kn, dil_w_o, loss_target, m_ffn1_norm, m_ffn1_w_in, m_ffn1_w_out, m_mix_norm, m_ffn2_norm, m_ffn2_w_in, m_ffn2_w_out, m_mla_w_down, m_mla_g_cq, m_mla_g_ckv, m_mla_w_uq, m_mla_w_ukv, m_mla_g_qn, m_mla_g_kn, m_mla_w_o, m_dil_w_qkv, m_dil_g_qn, m_dil_g_kn, m_dil_w_o, v_ffn1_norm, v_ffn1_w_in, v_ffn1_w_out, v_mix_norm, v_ffn2_norm, v_ffn2_w_in, v_ffn2_w_out, v_mla_w_down, v_mla_g_cq, v_mla_g_ckv, v_mla_w_uq, v_mla_w_ukv, v_mla_g_qn, v_mla_g_kn, v_mla_w_o, v_dil_w_qkv, v_dil_g_qn, v_dil_g_kn, v_dil_w_o):
    names = ["ffn1_norm", "ffn1_w_in", "ffn1_w_out", "mix_norm", "ffn2_norm", "ffn2_w_in", "ffn2_w_out", "mla_w_down",
             "mla_g_cq", "mla_g_ckv", "mla_w_uq", "mla_w_ukv", "mla_g_qn", "mla_g_kn", "mla_w_o", "dil_w_qkv",
             "dil_g_qn", "dil_g_kn", "dil_w_o"]
    W = dict(zip(names, [ffn1_norm, ffn1_w_in, ffn1_w_out, mix_norm, ffn2_norm, ffn2_w_in, ffn2_w_out, mla_w_down,
                         mla_g_cq, mla_g_ckv, mla_w_uq, mla_w_ukv, mla_g_qn, mla_g_kn, mla_w_o, dil_w_qkv,
                         dil_g_qn, dil_g_kn, dil_w_o]))
    M1 = dict(zip(names, [m_ffn1_norm, m_ffn1_w_in, m_ffn1_w_out, m_mix_norm, m_ffn2_norm, m_ffn2_w_in, m_ffn2_w_out,
                          m_mla_w_down, m_mla_g_cq, m_mla_g_ckv, m_mla_w_uq, m_mla_w_ukv, m_mla_g_qn, m_mla_g_kn,
                          m_mla_w_o, m_dil_w_qkv, m_dil_g_qn, m_dil_g_kn, m_dil_w_o]))
    V2 = dict(zip(names, [v_ffn1_norm, v_ffn1_w_in, v_ffn1_w_out, v_mix_norm, v_ffn2_norm, v_ffn2_w_in, v_ffn2_w_out,
                          v_mla_w_down, v_mla_g_cq, v_mla_g_ckv, v_mla_w_uq, v_mla_w_ukv, v_mla_g_qn, v_mla_g_kn,
                          v_mla_w_o, v_dil_w_qkv, v_dil_g_qn, v_dil_g_kn, v_dil_w_o]))
    S, D = x.shape[1], x.shape[2]
    x0 = x.reshape(S, D)
    tgt = loss_target.reshape(S, D)

    big = ["ffn1_w_in", "ffn1_w_out", "ffn2_w_in", "ffn2_w_out", "mla_w_down", "mla_w_uq", "mla_w_ukv", "mla_w_o",
           "dil_w_qkv", "dil_w_o"]
    shard_axis = {"ffn1_w_in": 2, "ffn1_w_out": 1, "ffn2_w_in": 2, "ffn2_w_out": 1, "mla_w_down": 1, "mla_w_uq": 2,
                  "mla_w_ukv": 2, "mla_w_o": 1, "dil_w_qkv": 2, "dil_w_o": 2}

    def padded(n, w):
        if n == "mla_w_down":
            return jnp.pad(w, ((0, 0), (0, 0), (0, LAT_PAD - w.shape[2])))
        if n == "mla_w_uq":
            return _pad_heads(w)
        return w

    depth = ffn1_norm.shape[0]
    blocks = []
    for l in range(depth):
        mixer = (["mla_w_down", "mla_w_uq", "mla_w_ukv", "mla_w_o"] if l % 2 == 0 else ["dil_w_qkv", "dil_w_o"])
        blocks.append((f"ffn1_{l}", [("ffn1_w_in", l), ("ffn1_w_out", l)]))
        blocks.append((f"mix_{l}", [(n, l // 2) for n in mixer]))
        blocks.append((f"ffn2_{l}", [("ffn2_w_in", l), ("ffn2_w_out", l)]))
    order = [k for _, keys in blocks for k in keys]
    shards = [_cast_bf16(padded(n, W[n]), l, f"cast_{n}_{l}") for n, l in order]
    ag_items, ag_token = _xchg_start("gather", shards, [shard_axis[n] for n, _ in order], "gather_start")
    ag_items = dict(zip(order, ag_items))
    full = {}

    def fetch(keys, after, tag):
        lands = _xchg_wait("gather", [ag_items[k] for k in keys], [shard_axis[k[0]] for k in keys], after,
                           f"gather_wait_{tag}")
        full.update(zip(keys, lands))

    g_qn = _pad_heads(mla_g_qn)
    g_kn = _pad_heads(mla_g_kn)
    tabs = _rope_tables(S)
    slopes = jnp.asarray(_alibi_slopes(), F32)

    grads = {}
    gain_g = {}

    def ffn_fwd(xin, norm_row, which, l, tag):
        k_in, k_out = (which + "_w_in", l), (which + "_w_out", l)
        h = _rms_fwd(xin, norm_row, f"rms_fwd_{tag}")
        fetch([k_in], h, f"in_{tag}")
        u = _mm(h, full[k_in], "nn", F32, f"mm_in_{tag}", layer=0)
        a = _swiglu_fwd(u, f"swiglu_fwd_{tag}")
        fetch([k_out], a, f"out_{tag}")
        xo = _mm(a, full[k_out], "nn", F32, f"mm_out_{tag}", scale=0.5, res=xin, layer=0)
        return xo, (xin, h, u, a)

    def ffn_bwd(dxo, saved, norm_row, which, l, tag):
        k_in, k_out = (which + "_w_in", l), (which + "_w_out", l)
        xin, h, u, a = saved
        da = _mm(dxo, full[k_out], "nt", F32, f"mm_da_{tag}", scale=0.5, layer=0)
        grads[k_out] = _mm(a, dxo, "tn", BF16, f"mm_dwout_{tag}", scale=0.5)
        du = _swiglu_bwd(u, da, f"swiglu_bwd_{tag}")
        dh = _mm(du, full[k_in], "nt", F32, f"mm_dh_{tag}", layer=0)
        grads[k_in] = _mm(h, du, "tn", BF16, f"mm_dwin_{tag}")
        dx, dg = _rms_bwd(xin, norm_row, dh, dxo, f"rms_bwd_{tag}")
        gain_g.setdefault(which + "_norm", {})[l] = dg
        return dx

    def mla_fwd(xin, l):
        j = l // 2
        xn = _rms_fwd(xin, mix_norm[l:l + 1], "rms_fwd_mla")
        fetch([(n, j) for n in ("mla_w_down", "mla_w_uq", "mla_w_ukv", "mla_w_o")], xn, "mla")
        lat = _mm(xn, full[("mla_w_down", j)], "nn", F32, "mm_lat", layer=0)
        cq, ckv = _lat_norm_fwd(lat, mla_g_cq[j:j + 1], mla_g_ckv[j:j + 1], "lat_norm_fwd")
        q_raw = _mm(cq, full[("mla_w_uq", j)], "nn", F32, "mm_uq", layer=0)
        kv = _mm(ckv, full[("mla_w_ukv", j)], "nn", F32, "mm_ukv", layer=0)
        qf, kf, vb = _mla_prep_fwd(q_raw, kv, lat, g_qn[j:j + 1], g_kn[j:j + 1], tabs, "mla_prep_fwd")
        o, lse = _flash_fwd(qf, kf, vb, "flash_fwd")
        xo = _mm(o, full[("mla_w_o", j)], "nn", F32, "mm_mla_o", res=xin, layer=0)
        return xo, (xin, xn, lat, cq, ckv, q_raw, kv, qf, kf, vb, o, lse)

    def mla_bwd(dxo, saved, l):
        j = l // 2
        xin, xn, lat, cq, ckv, q_raw, kv, qf, kf, vb, o, lse = saved
        do = _mm(dxo, full[("mla_w_o", j)], "nt", BF16, "mm_mla_do", layer=0)
        grads[("mla_w_o", j)] = _mm(o, dxo, "tn", BF16, "mm_mla_dwo")
        delta = _attn_delta(do, o, "attn_delta")
        dqf = _flash_bwd_dq(qf, kf, vb, do, lse, delta, "flash_bwd_dq")
        dkf, dv = _flash_bwd_dkv(qf, kf, vb, do, lse, delta, "flash_bwd_dkv")
        dq_raw, dkv, dkpe, dgq, dgk = _mla_prep_bwd(q_raw, kv, lat, g_qn[j:j + 1], g_kn[j:j + 1], tabs, dqf, dkf, dv,
                                                    "mla_prep_bwd")
        gain_g.setdefault("mla_g_qn", {})[j] = dgq
        gain_g.setdefault("mla_g_kn", {})[j] = dgk
        dcq = _mm(dq_raw, full[("mla_w_uq", j)], "nt", F32, "mm_dcq", layer=0)
        grads[("mla_w_uq", j)] = _mm(cq, dq_raw, "tn", BF16, "mm_dwuq")
        dckv = _mm(dkv, full[("mla_w_ukv", j)], "nt", F32, "mm_dckv", layer=0)
        grads[("mla_w_ukv", j)] = _mm(ckv, dkv, "tn", BF16, "mm_dwukv")
        dlat, dgcq, dgckv = _lat_norm_bwd(lat, mla_g_cq[j:j + 1], mla_g_ckv[j:j + 1], dcq, dckv, dkpe, "lat_norm_bwd")
        gain_g.setdefault("mla_g_cq", {})[j] = dgcq
        gain_g.setdefault("mla_g_ckv", {})[j] = dgckv
        dxn = _mm(dlat, full[("mla_w_down", j)], "nt", F32, "mm_dxn_mla", layer=0)
        grads[("mla_w_down", j)] = _mm(xn, dlat, "tn", BF16, "mm_dwdown")
        dx, dg = _rms_bwd(xin, mix_norm[l:l + 1], dxn, dxo, "rms_bwd_mla")
        gain_g.setdefault("mix_norm", {})[l] = dg
        return dx

    def dil_fwd(xin, l):
        j = l // 2
        xn = _rms_fwd(xin, mix_norm[l:l + 1], "rms_fwd_dil")
        fetch([("dil_w_qkv", j), ("dil_w_o", j)], xn, "dil")
        qkv = _mm(xn, full[("dil_w_qkv", j)], "nn", F32, "mm_qkv", layer=0)
        o_g, lse_g = _dil_fwd(qkv, dil_g_qn[j:j + 1], dil_g_kn[j:j + 1], slopes, "dil_fwd")
        o, lse = _dil_merge(o_g, lse_g, "dil_merge")
        xo = _mm(o, full[("dil_w_o", j)], "nn", F32, "mm_dil_o", res=xin, layer=0)
        return xo, (xin, xn, qkv, o, lse)

    def dil_bwd(dxo, saved, l):
        j = l // 2
        xin, xn, qkv, o, lse = saved
        do = _mm(dxo, full[("dil_w_o", j)], "nt", F32, "mm_dil_do", layer=0)
        grads[("dil_w_o", j)] = _mm(o, dxo, "tn", BF16, "mm_dil_dwo")
        delta = _attn_delta(do, o, "dil_delta")
        dq, dk, dv, dgq, dgk = _dil_bwd(qkv, dil_g_qn[j:j + 1], dil_g_kn[j:j + 1], slopes, do, delta, lse, "dil_bwd")
        gain_g.setdefault("dil_g_qn", {})[j] = dgq
        gain_g.setdefault("dil_g_kn", {})[j] = dgk
        dqkv = jnp.concatenate([dq, dk, dv], axis=1)
        dxn = _mm(dqkv, full[("dil_w_qkv", j)], "nt", F32, "mm_dxn_dil", layer=0)
        grads[("dil_w_qkv", j)] = _mm(xn, dqkv, "tn", BF16, "mm_dwqkv")
        dx, dg = _rms_bwd(xin, mix_norm[l:l + 1], dxn, dxo, "rms_bwd_dil")
        gain_g.setdefault("mix_norm", {})[l] = dg
        return dx

    zero = ag_token[0:1, 0:1]
    saved = []
    xc = x0
    for l in range(depth):
        n1 = ffn1_norm[l:l + 1] + zero if l == 0 else ffn1_norm[l:l + 1]
        xc, s1 = ffn_fwd(xc, n1, "ffn1", l, f"ffn1_{l}")
        xc, s2 = (mla_fwd if l % 2 == 0 else dil_fwd)(xc, l)
        xc, s3 = ffn_fwd(xc, ffn2_norm[l:l + 1], "ffn2", l, f"ffn2_{l}")
        saved.append((s1, s2, s3))

    dx, loss_part = _loss_head(xc, tgt, "loss_head")
    loss = lax.psum(loss_part[0, 0], MESH_AXES)

    out_g, out_d, out_m, out_v = {}, {}, {}, {}
    pending = []

    def scatter_start(tag, keys):
        items, _ = _xchg_start("scatter", [grads[k] for k in keys], [shard_axis[k[0]] - 1 for k in keys],
                               f"scatter_start_{tag}")
        pending.append((tag, keys, items))

    def scatter_finish(after):
        tag, keys, items = pending.pop(0)
        lands = _xchg_wait("scatter", items, [shard_axis[k[0]] - 1 for k in keys], after, f"scatter_wait_{tag}")
        for (n, l), p in zip(keys, lands):
            if n == "mla_w_down":
                p = p[..., :W[n].shape[2]]
            elif n == "mla_w_uq":
                p = _unpad_heads(p)
            prev = (out_g[n], out_d[n], out_m[n], out_v[n]) if n in out_g else None
            out_g[n], out_d[n], out_m[n], out_v[n] = _adamw(p, W[n], M1[n], V2[n], l, prev, f"adamw_{n}_{l}")

    for bi in reversed(range(len(blocks))):
        tag, keys = blocks[bi]
        l = bi // 3
        s = saved[l][bi % 3]
        if bi % 3 == 2:
            dx = ffn_bwd(dx, s, ffn2_norm[l:l + 1], "ffn2", l, tag)
        elif bi % 3 == 1:
            dx = (mla_bwd if l % 2 == 0 else dil_bwd)(dx, s, l)
        else:
            dx = ffn_bwd(dx, s, ffn1_norm[l:l + 1], "ffn1", l, tag)
        scatter_start(tag, keys)
        if len(pending) > 2:
            scatter_finish(dx)
    grad_x = dx.reshape(x.shape)
    while pending:
        scatter_finish(dx)

    small = [n for n in names if n not in big]

    def gain_local(n):
        rows = [gain_g[n][l] for l in range(W[n].shape[0])]
        g = jnp.concatenate(rows, axis=1)
        return g

    def flat_pad(n, a):
        a = a.reshape(1, -1)
        if n in ("mla_g_qn", "mla_g_kn"):
            a = _pad_heads(a)
        return a

    packed_g = jnp.concatenate([gain_local(n) for n in small], axis=1)
    sizes = [gain_local(n).shape[1] for n in small]
    tot_g = _gain_allreduce(packed_g, "gain_allreduce")
    pw = jnp.concatenate([flat_pad(n, W[n]) for n in small], axis=1)
    pm = jnp.concatenate([flat_pad(n, M1[n]) for n in small], axis=1)
    pv = jnp.concatenate([flat_pad(n, V2[n]) for n in small], axis=1)
    res = _adamw(tot_g.reshape(1, 1, -1), pw.reshape(1, 1, -1), pm.reshape(1, 1, -1), pv.reshape(1, 1, -1), 0, None,
                 "adamw_gains")
    res = [r.reshape(1, -1) for r in res]
    off = 0
    for n, sz in zip(small, sizes):
        for dst, r in zip((out_g, out_d, out_m, out_v), res):
            piece = r[:, off:off + sz]
            if n in ("mla_g_qn", "mla_g_kn"):
                piece = _unpad_heads(piece)
            dst[n] = piece.reshape(W[n].shape)
        off += sz

    return (loss, grad_x, *[out_g[n] for n in names], *[out_d[n] for n in names],
            *[out_m[n] for n in names], *[out_v[n] for n in names])
```

```python
import functools
import math

import jax
import jax.numpy as jnp
import numpy as np
from jax import lax
from jax.experimental import pallas as pl
from jax.experimental.pallas import tpu as pltpu

EPS = 1e-6
MLA_HEADS = 16
Q_LORA = 512
KV_LORA = 512
NOPE_DIM = 128
ROPE_DIM = 64
V_DIM = 128
QK_DIM = NOPE_DIM + ROPE_DIM
ROPE_THETA = 10000.0
HEAD_PAD = 256
LAT_PAD = Q_LORA + KV_LORA + 128
DIL_PAIRS = ((128, 1), (512, 4), (2048, 16))
DIL_GROUPS = 3
DIL_HEADS = 8
DIL_HEAD_DIM = 128
DIL_BLK = 128
ADAM_LR = 0.001
ADAM_B1 = 0.9
ADAM_B2 = 0.999
ADAM_EPS = 1e-08
ADAM_WD = 0.01
ADAM_STEP = 10

N_DEV = 8
MESH_AXES = ("x", "y", "c")
MESH = pl.DeviceIdType.MESH
NEG_BIG = -1e30
VMEM_LIMIT_V7X = 56 * 1024 * 1024
LANES = 128

BF16 = jnp.bfloat16
F32 = jnp.float32


def _pick(n, cands):
    for c in cands:
        if n % c == 0:
            return c
    raise ValueError(f"no tile for {n}")


def _params(sem):
    return pltpu.CompilerParams(dimension_semantics=sem, vmem_limit_bytes=VMEM_LIMIT_V7X)


ANY_SPEC = pl.BlockSpec(memory_space=pl.ANY)


def _mm(a, b, mode, out_dtype, name, *, scale=1.0, res=None, layer=None, deps=()):
    b2 = b.shape[-2:]
    if mode == "nn":
        (M, K), (Kb, N) = a.shape, b2
    elif mode == "nt":
        (M, K), (N, Kb) = a.shape, b2
    else:
        (K, M), (Kb, N) = a.shape, b2
    assert K == Kb, (a.shape, b.shape, mode)
    tm = _pick(M, (1024, 512, 384, 256, 128))
    tn = _pick(N, (1024, 512, 384, 256, 128))
    tk = _pick(K, (512, 384, 256, 128))
    nk = K // tk
    dims = {"nn": (((1,), (0,)), ((), ())), "nt": (((1,), (1,)), ((), ())), "tn": (((0,), (0,)), ((), ()))}[mode]

    def body(*refs):
        a_ref, b_ref = refs[:2]
        r_ref = refs[2] if res is not None else None
        o_ref, acc = refs[-2:]
        k = pl.program_id(2)

        @pl.when(k == 0)
        def _():
            acc[...] = jnp.zeros_like(acc)

        acc[...] += lax.dot_general(a_ref[...].astype(BF16), b_ref[...].astype(BF16), dims,
                                    preferred_element_type=F32)

        @pl.when(k == nk - 1)
        def _():
            v = acc[...]
            if scale != 1.0:
                v = v * scale
            if r_ref is not None:
                v = r_ref[...] + v
            o_ref[...] = v.astype(o_ref.dtype)

    a_spec = pl.BlockSpec((tk, tm), lambda i, j, k: (k, i)) if mode == "tn" else pl.BlockSpec((tm, tk), lambda i, j, k: (i, k))
    if mode == "nt":
        bshape, bidx = (tn, tk), (lambda i, j, k: (j, k))
    else:
        bshape, bidx = (tk, tn), (lambda i, j, k: (k, j))
    if b.ndim == 3:
        b_spec = pl.BlockSpec((None,) + bshape, lambda i, j, k: (layer,) + bidx(i, j, k))
    else:
        b_spec = pl.BlockSpec(bshape, bidx)
    in_specs = [a_spec, b_spec]
    args = [a, b]
    if res is not None:
        in_specs.append(pl.BlockSpec((tm, tn), lambda i, j, k: (i, j)))
        args.append(res)
    in_specs += [ANY_SPEC] * len(deps)
    args += list(deps)
    return pl.pallas_call(
        body, name=name, grid=(M // tm, N // tn, nk),
        in_specs=in_specs, out_specs=pl.BlockSpec((tm, tn), lambda i, j, k: (i, j)),
        out_shape=jax.ShapeDtypeStruct((M, N), out_dtype),
        scratch_shapes=[pltpu.VMEM((tm, tn), F32)],
        compiler_params=_params(("parallel", "parallel", "arbitrary")),
    )(*args)


def _cast_into_gathered(w, layer, axis, me, name):
    _, R, C = w.shape
    tr = _pick(R, (512, 256, 128, 64, 32, 16))
    nr = R // tr

    def body(me_ref, w_ref, o_ref):
        o_ref[...] = w_ref[...].astype(BF16)

    if axis == 1:
        out_idx = lambda i, me_ref: (0, me_ref[0] * nr + i, 0)
        shape = (1, R * N_DEV, C)
    else:
        out_idx = lambda i, me_ref: (0, i, me_ref[0])
        shape = (1, R, C * N_DEV)
    return pl.pallas_call(
        body, name=name,
        grid_spec=pltpu.PrefetchScalarGridSpec(
            num_scalar_prefetch=1, grid=(nr,),
            in_specs=[pl.BlockSpec((None, tr, C), lambda i, me_ref: (layer, i, 0))],
            out_specs=pl.BlockSpec((None, tr, C), out_idx)),
        out_shape=jax.ShapeDtypeStruct(shape, BF16), compiler_params=_params(("parallel",)),
    )(me, w)


def _rms_fwd(x, g, name, deps=()):
    T, D = x.shape
    tr = _pick(T, (512, 256, 128))

    def body(x_ref, g_ref, *rest):
        o_ref = rest[-1]
        xv = x_ref[...]
        r = lax.rsqrt(jnp.mean(xv * xv, axis=-1, keepdims=True) + EPS)
        o_ref[...] = ((xv * r) * g_ref[...]).astype(BF16)

    return pl.pallas_call(
        body, name=name, grid=(T // tr,),
        in_specs=[pl.BlockSpec((tr, D), lambda i: (i, 0)), pl.BlockSpec((1, D), lambda i: (0, 0))]
        + [ANY_SPEC] * len(deps),
        out_specs=pl.BlockSpec((tr, D), lambda i: (i, 0)),
        out_shape=jax.ShapeDtypeStruct((T, D), BF16), compiler_params=_params(("parallel",)),
    )(x, g, *deps)


def _rms_bwd(x, g, dh, dres, name, deps=()):
    T, D = x.shape
    tr = _pick(T, (256, 128))

    def body(x_ref, g_ref, dh_ref, dres_ref, *rest):
        dx_ref, dg_ref = rest[-2:]
        xv = x_ref[...]
        dhv = dh_ref[...]
        r = lax.rsqrt(jnp.mean(xv * xv, axis=-1, keepdims=True) + EPS)
        xhat = xv * r
        dxh = dhv * g_ref[...]
        c = jnp.mean(dxh * xhat, axis=-1, keepdims=True)
        dx_ref[...] = dres_ref[...] + r * (dxh - xhat * c)

        @pl.when(pl.program_id(0) == 0)
        def _():
            dg_ref[...] = jnp.zeros_like(dg_ref)

        dg_ref[...] += jnp.sum(dhv * xhat, axis=0, keepdims=True)

    row = pl.BlockSpec((tr, D), lambda i: (i, 0))
    vec = pl.BlockSpec((1, D), lambda i: (0, 0))
    return pl.pallas_call(
        body, name=name, grid=(T // tr,),
        in_specs=[row, vec, row, row] + [ANY_SPEC] * len(deps), out_specs=[row, vec],
        out_shape=[jax.ShapeDtypeStruct((T, D), F32), jax.ShapeDtypeStruct((1, D), F32)],
        compiler_params=_params(("arbitrary",)),
    )(x, g, dh, dres, *deps)


def _swiglu_fwd(u, name):
    T, F2 = u.shape
    F = F2 // 2
    tr = _pick(T, (512, 256, 128))
    tf = _pick(F, (512, 256, 128))
    nf = F // tf

    def body(g_ref, u_ref, o_ref):
        z = g_ref[...]
        o_ref[...] = (z * jax.nn.sigmoid(z) * u_ref[...]).astype(BF16)

    return pl.pallas_call(
        body, name=name, grid=(T // tr, nf),
        in_specs=[pl.BlockSpec((tr, tf), lambda i, j: (i, j)), pl.BlockSpec((tr, tf), lambda i, j: (i, j + nf))],
        out_specs=pl.BlockSpec((tr, tf), lambda i, j: (i, j)),
        out_shape=jax.ShapeDtypeStruct((T, F), BF16), compiler_params=_params(("parallel", "parallel")),
    )(u, u)


def _swiglu_bwd(u, da, name):
    T, F2 = u.shape
    F = F2 // 2
    tr = _pick(T, (128,))
    tf = _pick(F, (512, 256, 128))

    def body(u_ref, da_ref, du_ref):
        for j in range(F // tf):
            z = u_ref[:, j * tf:(j + 1) * tf]
            up = u_ref[:, F + j * tf:F + (j + 1) * tf]
            dav = da_ref[:, j * tf:(j + 1) * tf]
            sg = jax.nn.sigmoid(z)
            silu = z * sg
            du_ref[:, j * tf:(j + 1) * tf] = (dav * up * (sg + silu * (1.0 - sg))).astype(BF16)
            du_ref[:, F + j * tf:F + (j + 1) * tf] = (dav * silu).astype(BF16)

    return pl.pallas_call(
        body, name=name, grid=(T // tr,),
        in_specs=[pl.BlockSpec((tr, F2), lambda i: (i, 0)), pl.BlockSpec((tr, F), lambda i: (i, 0))],
        out_specs=pl.BlockSpec((tr, F2), lambda i: (i, 0)),
        out_shape=jax.ShapeDtypeStruct((T, F2), BF16), compiler_params=_params(("parallel",)),
    )(u, da)


def _loss_head(y, t, name):
    T, D = y.shape
    tr = _pick(T, (512, 256, 128))

    def body(y_ref, t_ref, dy_ref, l_ref):
        e = y_ref[...] - t_ref[...]
        dy_ref[...] = e * (1.0 / D)

        @pl.when(pl.program_id(0) == 0)
        def _():
            l_ref[...] = jnp.zeros_like(l_ref)

        l_ref[...] += 0.5 * jnp.sum(jnp.mean(e * e, axis=-1, keepdims=True), axis=0, keepdims=True)

    row = pl.BlockSpec((tr, D), lambda i: (i, 0))
    return pl.pallas_call(
        body, name=name, grid=(T // tr,),
        in_specs=[row, row], out_specs=[row, pl.BlockSpec((1, 1), lambda i: (0, 0))],
        out_shape=[jax.ShapeDtypeStruct((T, D), F32), jax.ShapeDtypeStruct((1, 1), F32)],
        compiler_params=_params(("arbitrary",)),
    )(y, t)


def _rope_tables(S):
    half = ROPE_DIM // 2
    inv = 1.0 / (ROPE_THETA ** (jnp.arange(0, ROPE_DIM, 2, dtype=F32) / ROPE_DIM))
    ang = jnp.arange(S, dtype=F32)[:, None] * inv[None, :]
    cos, sin = jnp.cos(ang), jnp.sin(ang)
    z = jnp.zeros((S, half), F32)
    z2 = jnp.zeros((S, LANES - ROPE_DIM), F32)
    c = jnp.concatenate([cos, cos, z2], axis=1)
    s1 = jnp.concatenate([-sin, z, z2], axis=1)
    s2 = jnp.concatenate([z, sin, z2], axis=1)
    return c, s1, s2


def _rope(r, c, s1, s2):
    return r * c + pltpu.roll(r, LANES - ROPE_DIM // 2, 1) * s1 + pltpu.roll(r, ROPE_DIM // 2, 1) * s2


def _rope_t(d, c, s1, s2):
    return d * c + pltpu.roll(d * s1, ROPE_DIM // 2, 1) + pltpu.roll(d * s2, LANES - ROPE_DIM // 2, 1)


def _lat_norm_fwd(lat, g_cq, g_ckv, name):
    T = lat.shape[0]
    tr = _pick(T, (512, 256, 128))

    def body(lat_ref, gq_ref, gk_ref, cq_ref, ckv_ref):
        for off, g_ref, o_ref in ((0, gq_ref, cq_ref), (Q_LORA, gk_ref, ckv_ref)):
            xv = lat_ref[:, off:off + Q_LORA]
            r = lax.rsqrt(jnp.mean(xv * xv, axis=-1, keepdims=True) + EPS)
            o_ref[...] = ((xv * r) * g_ref[...]).astype(BF16)

    vec = pl.BlockSpec((1, Q_LORA), lambda i: (0, 0))
    out = pl.BlockSpec((tr, Q_LORA), lambda i: (i, 0))
    return pl.pallas_call(
        body, name=name, grid=(T // tr,),
        in_specs=[pl.BlockSpec((tr, LAT_PAD), lambda i: (i, 0)), vec, vec], out_specs=[out, out],
        out_shape=[jax.ShapeDtypeStruct((T, Q_LORA), BF16)] * 2, compiler_params=_params(("parallel",)),
    )(lat, g_cq, g_ckv)


def _lat_norm_bwd(lat, g_cq, g_ckv, dcq, dckv, dkpe, name):
    T = lat.shape[0]
    tr = _pick(T, (256, 128))

    def body(lat_ref, gq_ref, gk_ref, dcq_ref, dckv_ref, dkpe_ref, dlat_ref, dgq_ref, dgk_ref):
        @pl.when(pl.program_id(0) == 0)
        def _():
            dgq_ref[...] = jnp.zeros_like(dgq_ref)
            dgk_ref[...] = jnp.zeros_like(dgk_ref)

        for off, g_ref, d_ref, dg_ref in ((0, gq_ref, dcq_ref, dgq_ref), (Q_LORA, gk_ref, dckv_ref, dgk_ref)):
            xv = lat_ref[:, off:off + Q_LORA]
            dv = d_ref[...]
            r = lax.rsqrt(jnp.mean(xv * xv, axis=-1, keepdims=True) + EPS)
            xhat = xv * r
            dxh = dv * g_ref[...]
            c = jnp.mean(dxh * xhat, axis=-1, keepdims=True)
            dlat_ref[:, off:off + Q_LORA] = (r * (dxh - xhat * c)).astype(BF16)
            dg_ref[...] += jnp.sum(dv * xhat, axis=0, keepdims=True)
        dlat_ref[:, Q_LORA + KV_LORA:] = dkpe_ref[...].astype(BF16)

    vec = pl.BlockSpec((1, Q_LORA), lambda i: (0, 0))
    half = pl.BlockSpec((tr, Q_LORA), lambda i: (i, 0))
    full = pl.BlockSpec((tr, LAT_PAD), lambda i: (i, 0))
    return pl.pallas_call(
        body, name=name, grid=(T // tr,),
        in_specs=[full, vec, vec, half, half, pl.BlockSpec((tr, LANES), lambda i: (i, 0))],
        out_specs=[full, vec, vec],
        out_shape=[jax.ShapeDtypeStruct((T, LAT_PAD), BF16), jax.ShapeDtypeStruct((1, Q_LORA), F32),
                   jax.ShapeDtypeStruct((1, Q_LORA), F32)],
        compiler_params=_params(("arbitrary",)),
    )(lat, g_cq, g_ckv, dcq, dckv, dkpe)


def _mla_prep_fwd(q_raw, kv, lat, g_qn, g_kn, tabs, name):
    T = q_raw.shape[0]
    H = MLA_HEADS
    tr = _pick(T, (256, 128))

    def body(q_ref, kv_ref, kpe_ref, gq_ref, gk_ref, c_ref, s1_ref, s2_ref, qf_ref, kf_ref, v_ref):
        c, s1, s2 = c_ref[...], s1_ref[...], s2_ref[...]
        gq, gk = gq_ref[...], gk_ref[...]
        kpe = kpe_ref[...]
        kpe_ss = jnp.sum(kpe * kpe, axis=-1, keepdims=True)
        for h in range(H):
            lo = h * HEAD_PAD
            qa = q_ref[:, lo:lo + LANES]
            qb = q_ref[:, lo + LANES:lo + HEAD_PAD]
            ss = jnp.sum(qa * qa, axis=-1, keepdims=True) + jnp.sum(qb * qb, axis=-1, keepdims=True)
            r = lax.rsqrt(ss * (1.0 / QK_DIM) + EPS)
            qf_ref[:, lo:lo + LANES] = (qa * r * gq[:, :LANES]).astype(BF16)
            qf_ref[:, lo + LANES:lo + HEAD_PAD] = _rope(qb * r * gq[:, LANES:], c, s1, s2).astype(BF16)
            ka = kv_ref[:, lo:lo + LANES]
            ss = jnp.sum(ka * ka, axis=-1, keepdims=True) + kpe_ss
            r = lax.rsqrt(ss * (1.0 / QK_DIM) + EPS)
            kf_ref[:, lo:lo + LANES] = (ka * r * gk[:, :LANES]).astype(BF16)
            kf_ref[:, lo + LANES:lo + HEAD_PAD] = _rope(kpe * r * gk[:, LANES:], c, s1, s2).astype(BF16)
            v_ref[:, h * V_DIM:(h + 1) * V_DIM] = kv_ref[:, lo + LANES:lo + HEAD_PAD].astype(BF16)

    wide = pl.BlockSpec((tr, H * HEAD_PAD), lambda i: (i, 0))
    lane = pl.BlockSpec((tr, LANES), lambda i: (i, 0))
    gvec = pl.BlockSpec((1, HEAD_PAD), lambda i: (0, 0))
    return pl.pallas_call(
        body, name=name, grid=(T // tr,),
        in_specs=[wide, wide, pl.BlockSpec((tr, LANES), lambda i: (i, (Q_LORA + KV_LORA) // LANES)), gvec, gvec,
                  lane, lane, lane],
        out_specs=[wide, wide, pl.BlockSpec((tr, H * V_DIM), lambda i: (i, 0))],
        out_shape=[jax.ShapeDtypeStruct((T, H * HEAD_PAD), BF16), jax.ShapeDtypeStruct((T, H * HEAD_PAD), BF16),
                   jax.ShapeDtypeStruct((T, H * V_DIM), BF16)],
        compiler_params=_params(("parallel",)),
    )(q_raw, kv, lat, g_qn, g_kn, *tabs)


def _mla_prep_bwd(q_raw, kv, lat, g_qn, g_kn, tabs, dqf, dkf, dv, name):
    T = q_raw.shape[0]
    H = MLA_HEADS
    tr = _pick(T, (128,))

    def body(q_ref, kv_ref, kpe_ref, gq_ref, gk_ref, c_ref, s1_ref, s2_ref, dqf_ref, dkf_ref, dv_ref,
             dq_ref, dkv_ref, dkpe_ref, dgq_ref, dgk_ref):
        @pl.when(pl.program_id(0) == 0)
        def _():
            dgq_ref[...] = jnp.zeros_like(dgq_ref)
            dgk_ref[...] = jnp.zeros_like(dgk_ref)

        c, s1, s2 = c_ref[...], s1_ref[...], s2_ref[...]
        gq, gk = gq_ref[...], gk_ref[...]
        kpe = kpe_ref[...]
        kpe_ss = jnp.sum(kpe * kpe, axis=-1, keepdims=True)
        dkpe = jnp.zeros_like(kpe)
        dgq_a = jnp.zeros((1, LANES), F32)
        dgq_b = jnp.zeros((1, LANES), F32)
        dgk_a = jnp.zeros((1, LANES), F32)
        dgk_b = jnp.zeros((1, LANES), F32)
        for h in range(H):
            lo = h * HEAD_PAD
            xa = q_ref[:, lo:lo + LANES]
            xb = q_ref[:, lo + LANES:lo + HEAD_PAD]
            ss = jnp.sum(xa * xa, axis=-1, keepdims=True) + jnp.sum(xb * xb, axis=-1, keepdims=True)
            r = lax.rsqrt(ss * (1.0 / QK_DIM) + EPS)
            xa, xb = xa * r, xb * r
            da = dqf_ref[:, lo:lo + LANES]
            db = _rope_t(dqf_ref[:, lo + LANES:lo + HEAD_PAD], c, s1, s2)
            dgq_a += jnp.sum(da * xa, axis=0, keepdims=True)
            dgq_b += jnp.sum(db * xb, axis=0, keepdims=True)
            da, db = da * gq[:, :LANES], db * gq[:, LANES:]
            cc = (jnp.sum(da * xa, axis=-1, keepdims=True) + jnp.sum(db * xb, axis=-1, keepdims=True)) * (1.0 / QK_DIM)
            dq_ref[:, lo:lo + LANES] = (r * (da - xa * cc)).astype(BF16)
            dq_ref[:, lo + LANES:lo + HEAD_PAD] = (r * (db - xb * cc)).astype(BF16)
            xa = kv_ref[:, lo:lo + LANES]
            ss = jnp.sum(xa * xa, axis=-1, keepdims=True) + kpe_ss
            r = lax.rsqrt(ss * (1.0 / QK_DIM) + EPS)
            xa, xb = xa * r, kpe * r
            da = dkf_ref[:, lo:lo + LANES]
            db = _rope_t(dkf_ref[:, lo + LANES:lo + HEAD_PAD], c, s1, s2)
            dgk_a += jnp.sum(da * xa, axis=0, keepdims=True)
            dgk_b += jnp.sum(db * xb, axis=0, keepdims=True)
            da, db = da * gk[:, :LANES], db * gk[:, LANES:]
            cc = (jnp.sum(da * xa, axis=-1, keepdims=True) + jnp.sum(db * xb, axis=-1, keepdims=True)) * (1.0 / QK_DIM)
            dkv_ref[:, lo:lo + LANES] = (r * (da - xa * cc)).astype(BF16)
            dkpe = dkpe + r * (db - xb * cc)
            dkv_ref[:, lo + LANES:lo + HEAD_PAD] = dv_ref[:, h * V_DIM:(h + 1) * V_DIM].astype(BF16)
        dkpe_ref[...] = dkpe
        dgq_ref[:, :LANES] += dgq_a
        dgq_ref[:, LANES:] += dgq_b
        dgk_ref[:, :LANES] += dgk_a
        dgk_ref[:, LANES:] += dgk_b

    wide = pl.BlockSpec((tr, H * HEAD_PAD), lambda i: (i, 0))
    lane = pl.BlockSpec((tr, LANES), lambda i: (i, 0))
    gvec = pl.BlockSpec((1, HEAD_PAD), lambda i: (0, 0))
    vspec = pl.BlockSpec((tr, H * V_DIM), lambda i: (i, 0))
    return pl.pallas_call(
        body, name=name, grid=(T // tr,),
        in_specs=[wide, wide, pl.BlockSpec((tr, LANES), lambda i: (i, (Q_LORA + KV_LORA) // LANES)), gvec, gvec,
                  lane, lane, lane, wide, wide, vspec],
        out_specs=[wide, wide, lane, gvec, gvec],
        out_shape=[jax.ShapeDtypeStruct((T, H * HEAD_PAD), BF16), jax.ShapeDtypeStruct((T, H * HEAD_PAD), BF16),
                   jax.ShapeDtypeStruct((T, LANES), F32), jax.ShapeDtypeStruct((1, HEAD_PAD), F32),
                   jax.ShapeDtypeStruct((1, HEAD_PAD), F32)],
        compiler_params=_params(("arbitrary",)),
    )(q_raw, kv, lat, g_qn, g_kn, *tabs, dqf, dkf, dv)


def _causal_mask(tq, tk):
    return lax.broadcasted_iota(jnp.int32, (tq, tk), 1) <= lax.broadcasted_iota(jnp.int32, (tq, tk), 0)


NT_DIMS = (((1,), (1,)), ((), ()))
TN_DIMS = (((0,), (0,)), ((), ()))


def _flash_fwd(qf, kf, v, name):
    T = qf.shape[0]
    H = MLA_HEADS
    t = _pick(T, (512, 256, 128))
    n = T // t
    scale = 1.0 / math.sqrt(QK_DIM)

    def body(q_ref, k_ref, v_ref, o_ref, lse_ref, m_sc, l_sc, acc_sc):
        i, j = pl.program_id(1), pl.program_id(2)

        @pl.when(j == 0)
        def _():
            m_sc[...] = jnp.full_like(m_sc, NEG_BIG)
            l_sc[...] = jnp.zeros_like(l_sc)
            acc_sc[...] = jnp.zeros_like(acc_sc)

        def step(masked):
            s = lax.dot_general(q_ref[...], k_ref[...], NT_DIMS, preferred_element_type=F32) * scale
            if masked:
                s = jnp.where(_causal_mask(t, t), s, NEG_BIG)
            m_prev = m_sc[:, :1]
            m_new = jnp.maximum(m_prev, jnp.max(s, axis=-1, keepdims=True))
            a = jnp.exp(m_prev - m_new)
            p = jnp.exp(s - m_new)
            l_sc[...] = a * l_sc[...] + jnp.sum(p, axis=-1, keepdims=True)
            acc_sc[...] = a * acc_sc[...] + jnp.dot(p.astype(BF16), v_ref[...], preferred_element_type=F32)
            m_sc[...] = jnp.broadcast_to(m_new, m_sc.shape)

        @pl.when(j < i)
        def _():
            step(False)

        @pl.when(j == i)
        def _():
            step(True)
            o_ref[...] = (acc_sc[...] / l_sc[...]).astype(BF16)
            lse_ref[...] = m_sc[...] + jnp.log(l_sc[...])

    return pl.pallas_call(
        body, name=name, grid=(H, n, n),
        in_specs=[pl.BlockSpec((t, HEAD_PAD), lambda h, i, j: (i, h)),
                  pl.BlockSpec((t, HEAD_PAD), lambda h, i, j: (jnp.minimum(j, i), h)),
                  pl.BlockSpec((t, V_DIM), lambda h, i, j: (jnp.minimum(j, i), h))],
        out_specs=[pl.BlockSpec((t, V_DIM), lambda h, i, j: (i, h)), pl.BlockSpec((t, V_DIM), lambda h, i, j: (i, h))],
        out_shape=[jax.ShapeDtypeStruct((T, H * V_DIM), BF16), jax.ShapeDtypeStruct((T, H * V_DIM), F32)],
        scratch_shapes=[pltpu.VMEM((t, LANES), F32), pltpu.VMEM((t, LANES), F32), pltpu.VMEM((t, V_DIM), F32)],
        compiler_params=_params(("parallel", "parallel", "arbitrary")),
    )(qf, kf, v)


def _attn_delta(do, o, name):
    T, W = do.shape
    nh = W // V_DIM
    tr = _pick(T, (512, 256, 128))

    def body(do_ref, o_ref, d_ref):
        for h in range(nh):
            sl = slice(h * V_DIM, (h + 1) * V_DIM)
            d = jnp.sum(do_ref[:, sl].astype(F32) * o_ref[:, sl].astype(F32), axis=-1, keepdims=True)
            d_ref[:, sl] = jnp.broadcast_to(d, (tr, V_DIM))

    row = pl.BlockSpec((tr, W), lambda i: (i, 0))
    return pl.pallas_call(
        body, name=name, grid=(T // tr,), in_specs=[row, row], out_specs=row,
        out_shape=jax.ShapeDtypeStruct((T, W), F32), compiler_params=_params(("parallel",)),
    )(do, o)


def _flash_bwd_dq(qf, kf, v, do, lse, delta, name):
    T = qf.shape[0]
    H = MLA_HEADS
    t = _pick(T, (512, 256, 128))
    n = T // t
    scale = 1.0 / math.sqrt(QK_DIM)

    def body(q_ref, k_ref, v_ref, do_ref, lse_ref, dl_ref, dq_ref, acc):
        i, j = pl.program_id(1), pl.program_id(2)

        @pl.when(j == 0)
        def _():
            acc[...] = jnp.zeros_like(acc)

        def step(masked):
            s = lax.dot_general(q_ref[...], k_ref[...], NT_DIMS, preferred_element_type=F32) * scale
            if masked:
                s = jnp.where(_causal_mask(t, t), s, NEG_BIG)
            p = jnp.exp(s - lse_ref[:, :1])
            dp = lax.dot_general(do_ref[...], v_ref[...], NT_DIMS, preferred_element_type=F32)
            ds = (p * (dp - dl_ref[:, :1])).astype(BF16)
            acc[...] += jnp.dot(ds, k_ref[...], preferred_element_type=F32)

        @pl.when(j < i)
        def _():
            step(False)

        @pl.when(j == i)
        def _():
            step(True)
            dq_ref[...] = acc[...] * scale

    qs = pl.BlockSpec((t, HEAD_PAD), lambda h, i, j: (i, h))
    ks = pl.BlockSpec((t, HEAD_PAD), lambda h, i, j: (jnp.minimum(j, i), h))
    vs = pl.BlockSpec((t, V_DIM), lambda h, i, j: (jnp.minimum(j, i), h))
    rs = pl.BlockSpec((t, V_DIM), lambda h, i, j: (i, h))
    return pl.pallas_call(
        body, name=name, grid=(H, n, n), in_specs=[qs, ks, vs, rs, rs, rs], out_specs=qs,
        out_shape=jax.ShapeDtypeStruct((T, H * HEAD_PAD), F32),
        scratch_shapes=[pltpu.VMEM((t, HEAD_PAD), F32)],
        compiler_params=_params(("parallel", "parallel", "arbitrary")),
    )(qf, kf, v, do, lse, delta)


def _flash_bwd_dkv(qf, kf, v, do, lse, delta, name):
    T = qf.shape[0]
    H = MLA_HEADS
    t = _pick(T, (512, 256, 128))
    n = T // t
    scale = 1.0 / math.sqrt(QK_DIM)

    def body(q_ref, k_ref, v_ref, do_ref, lse_ref, dl_ref, dk_ref, dv_ref, dk_acc, dv_acc):
        j, i = pl.program_id(1), pl.program_id(2)

        @pl.when(i == 0)
        def _():
            dk_acc[...] = jnp.zeros_like(dk_acc)
            dv_acc[...] = jnp.zeros_like(dv_acc)

        def step(masked):
            s = lax.dot_general(q_ref[...], k_ref[...], NT_DIMS, preferred_element_type=F32) * scale
            if masked:
                s = jnp.where(_causal_mask(t, t), s, NEG_BIG)
            p = jnp.exp(s - lse_ref[:, :1])
            dp = lax.dot_general(do_ref[...], v_ref[...], NT_DIMS, preferred_element_type=F32)
            ds = (p * (dp - dl_ref[:, :1])).astype(BF16)
            dv_acc[...] += lax.dot_general(p.astype(BF16), do_ref[...], TN_DIMS, preferred_element_type=F32)
            dk_acc[...] += lax.dot_general(ds, q_ref[...], TN_DIMS, preferred_element_type=F32)

        @pl.when(i == j)
        def _():
            step(True)

        @pl.when(i > j)
        def _():
            step(False)

        @pl.when(i == n - 1)
        def _():
            dk_ref[...] = dk_acc[...] * scale
            dv_ref[...] = dv_acc[...]

    qs = pl.BlockSpec((t, HEAD_PAD), lambda h, j, i: (jnp.maximum(i, j), h))
    rs = pl.BlockSpec((t, V_DIM), lambda h, j, i: (jnp.maximum(i, j), h))
    ks = pl.BlockSpec((t, HEAD_PAD), lambda h, j, i: (j, h))
    vs = pl.BlockSpec((t, V_DIM), lambda h, j, i: (j, h))
    return pl.pallas_call(
        body, name=name, grid=(H, n, n), in_specs=[qs, ks, vs, rs, rs, rs], out_specs=[ks, vs],
        out_shape=[jax.ShapeDtypeStruct((T, H * HEAD_PAD), F32), jax.ShapeDtypeStruct((T, H * V_DIM), F32)],
        scratch_shapes=[pltpu.VMEM((t, HEAD_PAD), F32), pltpu.VMEM((t, V_DIM), F32)],
        compiler_params=_params(("parallel", "parallel", "arbitrary")),
    )(qf, kf, v, do, lse, delta)


def _alibi_slopes():
    tot = DIL_GROUPS * DIL_HEADS
    return [float(np.float32(2.0) ** (np.float32(-8.0) * np.float32(k) / np.float32(tot))) for k in range(1, tot + 1)]


def _dil_masks():
    iq = lax.broadcasted_iota(jnp.int32, (DIL_BLK, DIL_BLK), 0)
    ik = lax.broadcasted_iota(jnp.int32, (DIL_BLK, DIL_BLK), 1)
    return (ik >= iq), (iq + DIL_BLK - ik).astype(F32), (ik <= iq), (iq - ik).astype(F32)


def _dil_norm(x, g):
    r = lax.rsqrt(jnp.mean(x * x, axis=-1, keepdims=True) + EPS)
    return x * r, r


def _dil_fwd(qkv, g_qn, g_kn, slopes, name):
    T = qkv.shape[0]
    GH = DIL_GROUPS * DIL_HEADS
    scale = 1.0 / math.sqrt(DIL_HEAD_DIM)

    def body(sl_ref, q_ref, k_ref, v_ref, gq_ref, gk_ref, o_ref, lse_ref):
        gh = pl.program_id(0)
        slope = sl_ref[gh]
        ok_p, dist_p, ok_c, dist_c = _dil_masks()
        gq, gk = gq_ref[...], gk_ref[...]
        for g, (_, d) in enumerate(DIL_PAIRS):
            @pl.when((gh >= g * DIL_HEADS) & (gh < (g + 1) * DIL_HEADS))
            def _(d=d):
                nb = T // (d * DIL_BLK)
                bias_p = jnp.where(ok_p, -slope * d * dist_p, NEG_BIG)
                bias_c = jnp.where(ok_c, -slope * d * dist_c, NEG_BIG)

                def phase(r, _):
                    def blk(nn, _):
                        def rows(b):
                            return pl.ds(b * (d * DIL_BLK) + r, DIL_BLK, stride=d) if d > 1 else pl.ds(pl.multiple_of(b * DIL_BLK, DIL_BLK), DIL_BLK)
                        cur, prv = rows(nn), rows(jnp.maximum(nn - 1, 0))
                        q = (_dil_norm(q_ref[cur, :], gq)[0] * gq).astype(BF16)
                        kc = (_dil_norm(k_ref[cur, :], gk)[0] * gk).astype(BF16)
                        kp = (_dil_norm(k_ref[prv, :], gk)[0] * gk).astype(BF16)
                        s_c = lax.dot_general(q, kc, NT_DIMS, preferred_element_type=F32) * scale + bias_c
                        s_p = lax.dot_general(q, kp, NT_DIMS, preferred_element_type=F32) * scale + bias_p
                        s_p = jnp.where(nn > 0, s_p, NEG_BIG)
                        m = jnp.maximum(jnp.max(s_c, axis=-1, keepdims=True), jnp.max(s_p, axis=-1, keepdims=True))
                        p_c = jnp.exp(s_c - m)
                        p_p = jnp.exp(s_p - m)
                        l = jnp.sum(p_c, axis=-1, keepdims=True) + jnp.sum(p_p, axis=-1, keepdims=True)
                        acc = jnp.dot(p_c.astype(BF16), v_ref[cur, :].astype(BF16), preferred_element_type=F32)
                        acc += jnp.dot(p_p.astype(BF16), v_ref[prv, :].astype(BF16), preferred_element_type=F32)
                        o_ref[cur, :] = acc / l
                        lse_ref[cur, :] = jnp.broadcast_to(m + jnp.log(l), (DIL_BLK, DIL_HEAD_DIM))
                        return 0
                    lax.fori_loop(0, nb, blk, 0)
                    return 0
                lax.fori_loop(0, d, phase, 0)

    col = lambda off: pl.BlockSpec((T, DIL_HEAD_DIM), lambda gh, sl: (0, gh + off))
    gvec = pl.BlockSpec((1, DIL_HEAD_DIM), lambda gh, sl: (0, 0))
    return pl.pallas_call(
        body, name=name,
        grid_spec=pltpu.PrefetchScalarGridSpec(
            num_scalar_prefetch=1, grid=(GH,),
            in_specs=[col(0), col(GH), col(2 * GH), gvec, gvec], out_specs=[col(0), col(0)]),
        out_shape=[jax.ShapeDtypeStruct((T, GH * DIL_HEAD_DIM), F32)] * 2,
        compiler_params=_params(("parallel",)),
    )(slopes, qkv, qkv, qkv, g_qn, g_kn)


def _dil_merge(o_g, lse_g, name):
    T = o_g.shape[0]
    W = DIL_HEADS * DIL_HEAD_DIM
    tr = _pick(T, (256, 128))

    def body(o0, o1, o2, l0, l1, l2, o_ref, lse_ref):
        a, b, c = l0[...], l1[...], l2[...]
        m = jnp.maximum(jnp.maximum(a, b), c)
        ea, eb, ec = jnp.exp(a - m), jnp.exp(b - m), jnp.exp(c - m)
        tot = ea + eb + ec
        o_ref[...] = ((o0[...] * ea + o1[...] * eb + o2[...] * ec) / tot).astype(BF16)
        lse_ref[...] = m + jnp.log(tot)

    grp = lambda g: pl.BlockSpec((tr, W), lambda i: (i, g))
    out = pl.BlockSpec((tr, W), lambda i: (i, 0))
    return pl.pallas_call(
        body, name=name, grid=(T // tr,),
        in_specs=[grp(0), grp(1), grp(2), grp(0), grp(1), grp(2)], out_specs=[out, out],
        out_shape=[jax.ShapeDtypeStruct((T, W), BF16), jax.ShapeDtypeStruct((T, W), F32)],
        compiler_params=_params(("parallel",)),
    )(o_g, o_g, o_g, lse_g, lse_g, lse_g)


def _dil_bwd(qkv, g_qn, g_kn, slopes, do, delta, lse, name):
    T = qkv.shape[0]
    GH = DIL_GROUPS * DIL_HEADS
    scale = 1.0 / math.sqrt(DIL_HEAD_DIM)
    nchunk = T // DIL_BLK

    def body(sl_ref, q_ref, k_ref, v_ref, gq_ref, gk_ref, do_ref, dl_ref, lse_ref,
             dq_ref, dk_ref, dv_ref, dgq_ref, dgk_ref, dq_acc, dk_acc, dv_acc):
        gh = pl.program_id(0)
        slope = sl_ref[gh]
        ok_p, dist_p, ok_c, dist_c = _dil_masks()
        gq, gk = gq_ref[...], gk_ref[...]

        @pl.when(gh == 0)
        def _():
            dgq_ref[...] = jnp.zeros_like(dgq_ref)
            dgk_ref[...] = jnp.zeros_like(dgk_ref)

        dk_acc[...] = jnp.zeros_like(dk_acc)
        dv_acc[...] = jnp.zeros_like(dv_acc)
        for g, (_, d) in enumerate(DIL_PAIRS):
            @pl.when((gh >= g * DIL_HEADS) & (gh < (g + 1) * DIL_HEADS))
            def _(d=d):
                nb = T // (d * DIL_BLK)
                bias_p = jnp.where(ok_p, -slope * d * dist_p, NEG_BIG)
                bias_c = jnp.where(ok_c, -slope * d * dist_c, NEG_BIG)

                def phase(r, _):
                    def blk(nn, _):
                        def rows(b):
                            return pl.ds(b * (d * DIL_BLK) + r, DIL_BLK, stride=d) if d > 1 else pl.ds(pl.multiple_of(b * DIL_BLK, DIL_BLK), DIL_BLK)
                        cur, prv = rows(nn), rows(jnp.maximum(nn - 1, 0))
                        q = (_dil_norm(q_ref[cur, :], gq)[0] * gq).astype(BF16)
                        kc = (_dil_norm(k_ref[cur, :], gk)[0] * gk).astype(BF16)
                        kp = (_dil_norm(k_ref[prv, :], gk)[0] * gk).astype(BF16)
                        vc = v_ref[cur, :].astype(BF16)
                        vp = v_ref[prv, :].astype(BF16)
                        dob = do_ref[cur, :].astype(BF16)
                        delta = dl_ref[cur, :][:, :1]
                        ls = lse_ref[cur, :][:, :1]
                        s_c = lax.dot_general(q, kc, NT_DIMS, preferred_element_type=F32) * scale + bias_c
                        s_p = lax.dot_general(q, kp, NT_DIMS, preferred_element_type=F32) * scale + bias_p
                        s_p = jnp.where(nn > 0, s_p, NEG_BIG)
                        p_c = jnp.exp(s_c - ls)
                        p_p = jnp.exp(s_p - ls)
                        dp_c = lax.dot_general(dob, vc, NT_DIMS, preferred_element_type=F32)
                        dp_p = lax.dot_general(dob, vp, NT_DIMS, preferred_element_type=F32)
                        ds_c = (p_c * (dp_c - delta)).astype(BF16)
                        ds_p = (p_p * (dp_p - delta)).astype(BF16)
                        dq_acc[cur, :] = (jnp.dot(ds_c, kc, preferred_element_type=F32)
                                          + jnp.dot(ds_p, kp, preferred_element_type=F32)) * scale
                        dk_acc[cur, :] += lax.dot_general(ds_c, q, TN_DIMS, preferred_element_type=F32) * scale
                        dv_acc[cur, :] += lax.dot_general(p_c.astype(BF16), dob, TN_DIMS, preferred_element_type=F32)
                        dk_acc[prv, :] += lax.dot_general(ds_p, q, TN_DIMS, preferred_element_type=F32) * scale
                        dv_acc[prv, :] += lax.dot_general(p_p.astype(BF16), dob, TN_DIMS, preferred_element_type=F32)
                        return 0
                    lax.fori_loop(0, nb, blk, 0)
                    return 0
                lax.fori_loop(0, d, phase, 0)

        def fin(ci, carry):
            dgq, dgk = carry
            rows = pl.ds(pl.multiple_of(ci * DIL_BLK, DIL_BLK), DIL_BLK)
            outs = []
            for x_ref, d_acc, gvec in ((q_ref, dq_acc, gq), (k_ref, dk_acc, gk)):
                xhat, r = _dil_norm(x_ref[rows, :], gvec)
                dn = d_acc[rows, :]
                dxh = dn * gvec
                c = jnp.mean(dxh * xhat, axis=-1, keepdims=True)
                outs.append(((r * (dxh - xhat * c)).astype(BF16), jnp.sum(dn * xhat, axis=0, keepdims=True)))
            dq_ref[rows, :] = outs[0][0]
            dk_ref[rows, :] = outs[1][0]
            dv_ref[rows, :] = dv_acc[rows, :].astype(BF16)
            return dgq + outs[0][1], dgk + outs[1][1]

        z = jnp.zeros((1, DIL_HEAD_DIM), F32)
        dgq, dgk = lax.fori_loop(0, nchunk, fin, (z, z))
        dgq_ref[...] += dgq
        dgk_ref[...] += dgk

    col = lambda off: pl.BlockSpec((T, DIL_HEAD_DIM), lambda gh, sl: (0, gh + off))
    hcol = pl.BlockSpec((T, DIL_HEAD_DIM), lambda gh, sl: (0, gh % DIL_HEADS))
    gvec = pl.BlockSpec((1, DIL_HEAD_DIM), lambda gh, sl: (0, 0))
    wide = jax.ShapeDtypeStruct((T, GH * DIL_HEAD_DIM), BF16)
    vec = jax.ShapeDtypeStruct((1, DIL_HEAD_DIM), F32)
    return pl.pallas_call(
        body, name=name,
        grid_spec=pltpu.PrefetchScalarGridSpec(
            num_scalar_prefetch=1, grid=(GH,),
            in_specs=[col(0), col(GH), col(2 * GH), gvec, gvec, hcol, hcol, hcol],
            out_specs=[col(0), col(0), col(0), gvec, gvec],
            scratch_shapes=[pltpu.VMEM((T, DIL_HEAD_DIM), F32)] * 3),
        out_shape=[wide, wide, wide, vec, vec],
        compiler_params=_params(("arbitrary",)),
    )(slopes, qkv, qkv, qkv, g_qn, g_kn, do, delta, lse)


def _my_pos():
    return lax.axis_index("x"), lax.axis_index("y"), lax.axis_index("c")


def _peer(pos, j):
    x, y, c = pos
    px = 1 - x if j & 4 else x
    py = 1 - y if j & 2 else y
    pc = 1 - c if j & 1 else c
    return (px, py, pc), 4 * px + 2 * py + pc


def _shard_slice(ref, axis, idx, size):
    sl = [slice(None)] * len(ref.shape)
    sl[axis] = pl.ds(pl.multiple_of(idx * size, 8), size)
    return ref.at[tuple(sl)]


HBM_SPEC = pl.BlockSpec(memory_space=pltpu.HBM)
SEM_SPEC = pl.BlockSpec(memory_space=pltpu.SEMAPHORE)
DATAFLOW = pltpu.SideEffectType.DATAFLOW_SIDE_EFFECTING
N_PEER = N_DEV - 1


def _xchg_refs(kind, axis, bufs, frm, to):
    if kind == "gather":
        size = bufs[0].shape[axis] // N_DEV
        piece = _shard_slice(bufs[0], axis, frm, size)
        return piece, piece
    grad, slots = bufs
    return _shard_slice(grad, axis, to, grad.shape[axis] // N_DEV), slots.at[frm]


def _xchg_start(kind, arrays, axes, name, deps=()):
    n = len(arrays)
    nb = 1 if kind == "gather" else 2

    def body(*refs):
        bufs = [refs[a * nb:(a + 1) * nb] for a in range(n)]
        outs = refs[nb * n + len(deps):]
        send, recv, token = outs[:n], outs[n:2 * n], outs[2 * n + nb * n]
        pos = _my_pos()
        me = 4 * pos[0] + 2 * pos[1] + pos[2]
        for a in range(n):
            for j in range(1, N_DEV):
                dev, pid = _peer(pos, j)
                src, dst = _xchg_refs(kind, axes[a], bufs[a], me, pid)
                pltpu.make_async_remote_copy(src_ref=src, dst_ref=dst, send_sem=send[a].at[j - 1],
                                             recv_sem=recv[a].at[j - 1], device_id=dev, device_id_type=MESH).start()
        token[...] = jnp.zeros_like(token)

    ops = []
    for g, ax in zip(arrays, axes):
        ops.append(g)
        if kind == "scatter":
            shp = list(g.shape)
            shp[ax] //= N_DEV
            ops.append(lax.empty((N_DEV,) + tuple(shp), g.dtype))
    sems = [pltpu.SemaphoreType.DMA((N_PEER,))] * (2 * n)
    res = pl.pallas_call(
        body, name=name,
        out_shape=sems + [pltpu.HBM(o.shape, o.dtype) for o in ops] + [jax.ShapeDtypeStruct((8, LANES), F32)],
        in_specs=[HBM_SPEC] * len(ops) + [ANY_SPEC] * len(deps),
        out_specs=[SEM_SPEC] * (2 * n) + [HBM_SPEC] * len(ops) + [pl.BlockSpec(memory_space=pltpu.VMEM)],
        input_output_aliases={i: 2 * n + i for i in range(len(ops))},
        compiler_params=pltpu.CompilerParams(has_side_effects=DATAFLOW),
    )(*[pltpu.with_memory_space_constraint(o, pltpu.HBM) for o in ops], *deps)
    items = [(res[a], res[n + a], tuple(res[2 * n + a * nb:2 * n + (a + 1) * nb])) for a in range(n)]
    return items, res[2 * n + len(ops)]


def _xchg_wait(kind, items, axes, after, name):
    n = len(items)
    nb = 1 if kind == "gather" else 2

    def body(*refs):
        bufs = [refs[a * nb:(a + 1) * nb] for a in range(n)]
        send, recv = refs[nb * n:nb * n + n], refs[nb * n + n:nb * n + 2 * n]
        pos = _my_pos()
        me = 4 * pos[0] + 2 * pos[1] + pos[2]
        for a in range(n):
            for j in range(1, N_DEV):
                dev, pid = _peer(pos, j)
                src, dst = _xchg_refs(kind, axes[a], bufs[a], pid, me)
                cp = pltpu.make_async_remote_copy(src_ref=src, dst_ref=dst, send_sem=send[a].at[j - 1],
                                                  recv_sem=recv[a].at[j - 1], device_id=dev, device_id_type=MESH)
                cp.wait_send()
                cp.wait_recv()

    ops = [b for it in items for b in it[2]]
    res = pl.pallas_call(
        body, name=name,
        out_shape=[pltpu.HBM(o.shape, o.dtype) for o in ops],
        in_specs=[HBM_SPEC] * len(ops) + [SEM_SPEC] * (2 * n) + [ANY_SPEC],
        out_specs=[HBM_SPEC] * len(ops),
        input_output_aliases={i: i for i in range(len(ops))},
        compiler_params=pltpu.CompilerParams(has_side_effects=DATAFLOW),
    )(*ops, *[it[0] for it in items], *[it[1] for it in items], after)
    return [res[a * nb + nb - 1] for a in range(n)]


def _gain_allreduce(v, name):
    n = v.shape[1]

    def body(v_ref, o_ref, slots, send_sems, recv_sems):
        pos = _my_pos()
        me = 4 * pos[0] + 2 * pos[1] + pos[2]
        slots[me] = v_ref[...]
        copies = []
        for j in range(1, N_DEV):
            dev, _ = _peer(pos, j)
            cp = pltpu.make_async_remote_copy(
                src_ref=slots.at[me], dst_ref=slots.at[me], send_sem=send_sems.at[j], recv_sem=recv_sems.at[j],
                device_id=dev, device_id_type=MESH)
            cp.start()
            copies.append(cp)
        for j in range(1, N_DEV):
            dev, pid = _peer(pos, j)
            pltpu.make_async_remote_copy(
                src_ref=slots.at[me], dst_ref=slots.at[pid], send_sem=send_sems.at[j], recv_sem=recv_sems.at[j],
                device_id=dev, device_id_type=MESH).wait_recv()
        for cp in copies:
            cp.wait_send()
        acc = slots[0]
        for s in range(1, N_DEV):
            acc = acc + slots[s]
        o_ref[...] = acc

    return pl.pallas_call(
        body, name=name, out_shape=jax.ShapeDtypeStruct((1, n), F32),
        in_specs=[pl.BlockSpec(memory_space=pltpu.VMEM)], out_specs=pl.BlockSpec(memory_space=pltpu.VMEM),
        scratch_shapes=[pltpu.VMEM((N_DEV, 1, n), F32), pltpu.SemaphoreType.DMA((N_DEV,)),
                        pltpu.SemaphoreType.DMA((N_DEV,))],
        compiler_params=pltpu.CompilerParams(has_side_effects=True),
    )(v)


def _adamw(parts, own, me, w, m, v, layer, prev, name):
    L, R, C = w.shape
    P = parts.shape[0]
    tr = _pick(R, (128, 64, 32, 16, 8, 1))
    c1 = 1.0 - ADAM_B1 ** ADAM_STEP
    c2 = 1.0 - ADAM_B2 ** ADAM_STEP
    n_in = 4 if own is None else 5

    def body(me_ref, *refs):
        p_ref = refs[0]
        w_ref, m_ref, v_ref = refs[n_in - 3:n_in]
        g_out, d_out, m_out, v_out, tok = refs[-5:]
        g = None
        for s in range(P):
            part = p_ref[s]
            if own is not None:
                part = jnp.where(me_ref[0] == s, refs[1][...], part)
            g = part.astype(F32) if g is None else g + part.astype(F32)
        mn = ADAM_B1 * m_ref[...] + (1.0 - ADAM_B1) * g
        vn = ADAM_B2 * v_ref[...] + (1.0 - ADAM_B2) * (g * g)
        g_out[...] = g
        m_out[...] = mn
        v_out[...] = vn
        d_out[...] = -ADAM_LR * ((mn / c1) / (jnp.sqrt(vn / c2) + ADAM_EPS) + ADAM_WD * w_ref[...])
        tok[...] = jnp.zeros_like(tok)

    row = pl.BlockSpec((None, tr, C), lambda i, me_ref: (layer, i, 0))
    in_specs = [pl.BlockSpec((P, tr, C), lambda i, me_ref: (0, i, 0))]
    args = [parts]
    if own is not None:
        in_specs.append(pl.BlockSpec((tr, C), lambda i, me_ref: (i, 0)))
        args.append(own)
    in_specs += [row, row, row]
    args += [w, m, v]
    aliases = {}
    if prev is not None:
        in_specs += [ANY_SPEC] * 4
        aliases = {1 + len(args) + k: k for k in range(4)}
        args += list(prev)
    shp = jax.ShapeDtypeStruct((L, R, C), F32)
    res = pl.pallas_call(
        body, name=name,
        grid_spec=pltpu.PrefetchScalarGridSpec(
            num_scalar_prefetch=1, grid=(R // tr,), in_specs=in_specs,
            out_specs=[row] * 4 + [pl.BlockSpec((8, LANES), lambda i, me_ref: (0, 0))]),
        out_shape=[shp] * 4 + [jax.ShapeDtypeStruct((8, LANES), F32)],
        input_output_aliases=aliases, compiler_params=_params(("arbitrary",)),
    )(me, *args)
    return res[:4], res[4]


def _pad_heads(w):
    lead = w.shape[:-1]
    n = w.shape[-1] // QK_DIM
    w = w.reshape(lead + (n, QK_DIM))
    w = jnp.pad(w, [(0, 0)] * len(lead) + [(0, 0), (0, HEAD_PAD - QK_DIM)])
    return w.reshape(lead + (n * HEAD_PAD,))


def _unpad_heads(w):
    lead = w.shape[:-1]
    n = w.shape[-1] // HEAD_PAD
    return w.reshape(lead + (n, HEAD_PAD))[..., :QK_DIM].reshape(lead + (n * QK_DIM,))


def kernel(x, ffn1_norm, ffn1_w_in, ffn1_w_out, mix_norm, ffn2_norm, ffn2_w_in, ffn2_w_out, mla_w_down, mla_g_cq, mla_g_ckv, mla_w_uq, mla_w_ukv, mla_g_qn, mla_g_kn, mla_w_o, dil_w_qkv, dil_g_qn, dil_g_kn, dil_w_o, loss_target, m_ffn1_norm, m_ffn1_w_in, m_ffn1_w_out, m_mix_norm, m_ffn2_norm, m_ffn2_w_in, m_ffn2_w_out, m_mla_w_down, m_mla_g_cq, m_mla_g_ckv, m_mla_w_uq, m_mla_w_ukv, m_mla_g_qn, m_mla_g_kn, m_mla_w_o, m_dil_w_qkv, m_dil_g_qn, m_dil_g_kn, m_dil_w_o, v_ffn1_norm, v_ffn1_w_in, v_ffn1_w_out, v_mix_norm, v_ffn2_norm, v_ffn2_w_in, v_ffn2_w_out, v_mla_w_down, v_mla_g_cq, v_mla_g_ckv, v_mla_w_uq, v_mla_w_ukv, v_mla_g_qn, v_mla_g_kn, v_mla_w_o, v_dil_w_qkv, v_dil_g_qn, v_dil_g_kn, v_dil_w_o):
    names = ["ffn1_norm", "ffn1_w_in", "ffn1_w_out", "mix_norm", "ffn2_norm", "ffn2_w_in", "ffn2_w_out", "mla_w_down",
             "mla_g_cq", "mla_g_ckv", "mla_w_uq", "mla_w_ukv", "mla_g_qn", "mla_g_kn", "mla_w_o", "dil_w_qkv",
             "dil_g_qn", "dil_g_kn", "dil_w_o"]
    W = dict(zip(names, [ffn1_norm, ffn1_w_in, ffn1_w_out, mix_norm, ffn2_norm, ffn2_w_in, ffn2_w_out, mla_w_down,
                         mla_g_cq, mla_g_ckv, mla_w_uq, mla_w_ukv, mla_g_qn, mla_g_kn, mla_w_o, dil_w_qkv,
                         dil_g_qn, dil_g_kn, dil_w_o]))
    M1 = dict(zip(names, [m_ffn1_norm, m_ffn1_w_in, m_ffn1_w_out, m_mix_norm, m_ffn2_norm, m_ffn2_w_in, m_ffn2_w_out,
                          m_mla_w_down, m_mla_g_cq, m_mla_g_ckv, m_mla_w_uq, m_mla_w_ukv, m_mla_g_qn, m_mla_g_kn,
                          m_mla_w_o, m_dil_w_qkv, m_dil_g_qn, m_dil_g_kn, m_dil_w_o]))
    V2 = dict(zip(names, [v_ffn1_norm, v_ffn1_w_in, v_ffn1_w_out, v_mix_norm, v_ffn2_norm, v_ffn2_w_in, v_ffn2_w_out,
                          v_mla_w_down, v_mla_g_cq, v_mla_g_ckv, v_mla_w_uq, v_mla_w_ukv, v_mla_g_qn, v_mla_g_kn,
                          v_mla_w_o, v_dil_w_qkv, v_dil_g_qn, v_dil_g_kn, v_dil_w_o]))
    S, D = x.shape[1], x.shape[2]
    x0 = x.reshape(S, D)
    tgt = loss_target.reshape(S, D)

    big = ["ffn1_w_in", "ffn1_w_out", "ffn2_w_in", "ffn2_w_out", "mla_w_down", "mla_w_uq", "mla_w_ukv", "mla_w_o",
           "dil_w_qkv", "dil_w_o"]
    shard_axis = {"ffn1_w_in": 2, "ffn1_w_out": 1, "ffn2_w_in": 2, "ffn2_w_out": 1, "mla_w_down": 1, "mla_w_uq": 2,
                  "mla_w_ukv": 2, "mla_w_o": 1, "dil_w_qkv": 2, "dil_w_o": 2}

    def padded(n, w):
        if n == "mla_w_down":
            return jnp.pad(w, ((0, 0), (0, 0), (0, LAT_PAD - w.shape[2])))
        if n == "mla_w_uq":
            return _pad_heads(w)
        return w

    depth = ffn1_norm.shape[0]
    blocks = []
    for l in range(depth):
        mixer = (["mla_w_down", "mla_w_uq", "mla_w_ukv", "mla_w_o"] if l % 2 == 0 else ["dil_w_qkv", "dil_w_o"])
        blocks.append((f"ffn1_{l}", [("ffn1_w_in", l), ("ffn1_w_out", l)]))
        blocks.append((f"mix_{l}", [(n, l // 2) for n in mixer]))
        blocks.append((f"ffn2_{l}", [("ffn2_w_in", l), ("ffn2_w_out", l)]))
    order = [k for _, keys in blocks for k in keys]
    me = (4 * lax.axis_index("x") + 2 * lax.axis_index("y") + lax.axis_index("c")).astype(jnp.int32).reshape(1)
    mine = [_cast_into_gathered(padded(n, W[n]), l, shard_axis[n], me, f"cast_{n}_{l}") for n, l in order]
    ag_items, ag_token = _xchg_start("gather", mine, [shard_axis[n] for n, _ in order], "gather_start")
    ag_items = dict(zip(order, ag_items))
    full = {}

    def fetch(keys, after, tag):
        lands = _xchg_wait("gather", [ag_items[k] for k in keys], [shard_axis[k[0]] for k in keys], after,
                           f"gather_wait_{tag}")
        full.update(zip(keys, lands))

    g_qn = _pad_heads(mla_g_qn)
    g_kn = _pad_heads(mla_g_kn)
    tabs = _rope_tables(S)
    slopes = jnp.asarray(_alibi_slopes(), F32)

    grads = {}
    gain_g = {}

    out_g, out_d, out_m, out_v = {}, {}, {}, {}
    pending = []
    lag = 3

    def scatter_start(tag, keys):
        items, token = _xchg_start("scatter", [grads[k] for k in keys], [shard_axis[k[0]] - 1 for k in keys],
                                   f"scatter_start_{tag}")
        pending.append((tag, keys, items))
        return token

    def scatter_finish(after):
        tag, keys, items = pending.pop(0)
        lands = _xchg_wait("scatter", items, [shard_axis[k[0]] - 1 for k in keys], after, f"scatter_wait_{tag}")
        tokens = []
        for (n, l), p in zip(keys, lands):
            ax = shard_axis[n] - 1
            size = grads[(n, l)].shape[ax] // N_DEV
            own = lax.dynamic_slice_in_dim(grads[(n, l)], me[0] * size, size, axis=ax)
            if n == "mla_w_down":
                p, own = p[..., :W[n].shape[2]], own[..., :W[n].shape[2]]
            elif n == "mla_w_uq":
                p, own = _unpad_heads(p), _unpad_heads(own)
            prev = (out_g[n], out_d[n], out_m[n], out_v[n]) if n in out_g else None
            (out_g[n], out_d[n], out_m[n], out_v[n]), tok = _adamw(p, own, me, W[n], M1[n], V2[n], l, prev,
                                                                    f"adamw_{n}_{l}")
            tokens.append(tok)
        return tokens

    def finish_due(after):
        tokens = []
        while len(pending) > lag:
            tokens += scatter_finish(after)
        return tokens

    def ffn_fwd(xin, norm_row, which, l, tag, deps=()):
        k_in, k_out = (which + "_w_in", l), (which + "_w_out", l)
        h = _rms_fwd(xin, norm_row, f"rms_fwd_{tag}", deps=deps)
        fetch([k_in], h, f"in_{tag}")
        u = _mm(h, full[k_in], "nn", F32, f"mm_in_{tag}", layer=0)
        a = _swiglu_fwd(u, f"swiglu_fwd_{tag}")
        fetch([k_out], a, f"out_{tag}")
        xo = _mm(a, full[k_out], "nn", F32, f"mm_out_{tag}", scale=0.5, res=xin, layer=0)
        return xo, (xin, h, u, a)

    def ffn_bwd(dxo, saved, norm_row, which, l, tag):
        k_in, k_out = (which + "_w_in", l), (which + "_w_out", l)
        xin, h, u, a = saved
        da = _mm(dxo, full[k_out], "nt", F32, f"mm_da_{tag}", scale=0.5, layer=0)
        grads[k_out] = _mm(a, dxo, "tn", BF16, f"mm_dwout_{tag}", scale=0.5)
        t_out = scatter_start(f"{tag}_out", [k_out])
        du = _swiglu_bwd(u, da, f"swiglu_bwd_{tag}")
        grads[k_in] = _mm(h, du, "tn", BF16, f"mm_dwin_{tag}", deps=[t_out])
        t_in = scatter_start(f"{tag}_in", [k_in])
        dh = _mm(du, full[k_in], "nt", F32, f"mm_dh_{tag}", layer=0, deps=[t_in])
        toks = finish_due(dh)
        dx, dg = _rms_bwd(xin, norm_row, dh, dxo, f"rms_bwd_{tag}", deps=toks)
        gain_g.setdefault(which + "_norm", {})[l] = dg
        return dx

    def mla_fwd(xin, l):
        j = l // 2
        xn = _rms_fwd(xin, mix_norm[l:l + 1], "rms_fwd_mla")
        fetch([(n, j) for n in ("mla_w_down", "mla_w_uq", "mla_w_ukv", "mla_w_o")], xn, "mla")
        lat = _mm(xn, full[("mla_w_down", j)], "nn", F32, "mm_lat", layer=0)
        cq, ckv = _lat_norm_fwd(lat, mla_g_cq[j:j + 1], mla_g_ckv[j:j + 1], "lat_norm_fwd")
        q_raw = _mm(cq, full[("mla_w_uq", j)], "nn", F32, "mm_uq", layer=0)
        kv = _mm(ckv, full[("mla_w_ukv", j)], "nn", F32, "mm_ukv", layer=0)
        qf, kf, vb = _mla_prep_fwd(q_raw, kv, lat, g_qn[j:j + 1], g_kn[j:j + 1], tabs, "mla_prep_fwd")
        o, lse = _flash_fwd(qf, kf, vb, "flash_fwd")
        xo = _mm(o, full[("mla_w_o", j)], "nn", F32, "mm_mla_o", res=xin, layer=0)
        return xo, (xin, xn, lat, cq, ckv, q_raw, kv, qf, kf, vb, o, lse)

    def mla_bwd(dxo, saved, l):
        j = l // 2
        xin, xn, lat, cq, ckv, q_raw, kv, qf, kf, vb, o, lse = saved
        do = _mm(dxo, full[("mla_w_o", j)], "nt", BF16, "mm_mla_do", layer=0)
        grads[("mla_w_o", j)] = _mm(o, dxo, "tn", BF16, "mm_mla_dwo")
        delta = _attn_delta(do, o, "attn_delta")
        dqf = _flash_bwd_dq(qf, kf, vb, do, lse, delta, "flash_bwd_dq")
        dkf, dv = _flash_bwd_dkv(qf, kf, vb, do, lse, delta, "flash_bwd_dkv")
        dq_raw, dkv, dkpe, dgq, dgk = _mla_prep_bwd(q_raw, kv, lat, g_qn[j:j + 1], g_kn[j:j + 1], tabs, dqf, dkf, dv,
                                                    "mla_prep_bwd")
        gain_g.setdefault("mla_g_qn", {})[j] = dgq
        gain_g.setdefault("mla_g_kn", {})[j] = dgk
        dcq = _mm(dq_raw, full[("mla_w_uq", j)], "nt", F32, "mm_dcq", layer=0)
        grads[("mla_w_uq", j)] = _mm(cq, dq_raw, "tn", BF16, "mm_dwuq")
        dckv = _mm(dkv, full[("mla_w_ukv", j)], "nt", F32, "mm_dckv", layer=0)
        grads[("mla_w_ukv", j)] = _mm(ckv, dkv, "tn", BF16, "mm_dwukv")
        dlat, dgcq, dgckv = _lat_norm_bwd(lat, mla_g_cq[j:j + 1], mla_g_ckv[j:j + 1], dcq, dckv, dkpe, "lat_norm_bwd")
        gain_g.setdefault("mla_g_cq", {})[j] = dgcq
        gain_g.setdefault("mla_g_ckv", {})[j] = dgckv
        dxn = _mm(dlat, full[("mla_w_down", j)], "nt", F32, "mm_dxn_mla", layer=0)
        grads[("mla_w_down", j)] = _mm(xn, dlat, "tn", BF16, "mm_dwdown")
        tok = scatter_start(f"mix_{l}", [(n, j) for n in ("mla_w_down", "mla_w_uq", "mla_w_ukv", "mla_w_o")])
        toks = finish_due(dxn)
        dx, dg = _rms_bwd(xin, mix_norm[l:l + 1], dxn, dxo, "rms_bwd_mla", deps=[tok] + toks)
        gain_g.setdefault("mix_norm", {})[l] = dg
        return dx

    def dil_fwd(xin, l):
        j = l // 2
        xn = _rms_fwd(xin, mix_norm[l:l + 1], "rms_fwd_dil")
        fetch([("dil_w_qkv", j), ("dil_w_o", j)], xn, "dil")
        qkv = _mm(xn, full[("dil_w_qkv", j)], "nn", F32, "mm_qkv", layer=0)
        o_g, lse_g = _dil_fwd(qkv, dil_g_qn[j:j + 1], dil_g_kn[j:j + 1], slopes, "dil_fwd")
        o, lse = _dil_merge(o_g, lse_g, "dil_merge")
        xo = _mm(o, full[("dil_w_o", j)], "nn", F32, "mm_dil_o", res=xin, layer=0)
        return xo, (xin, xn, qkv, o, lse)

    def dil_bwd(dxo, saved, l):
        j = l // 2
        xin, xn, qkv, o, lse = saved
        do = _mm(dxo, full[("dil_w_o", j)], "nt", F32, "mm_dil_do", layer=0)
        grads[("dil_w_o", j)] = _mm(o, dxo, "tn", BF16, "mm_dil_dwo")
        delta = _attn_delta(do, o, "dil_delta")
        dq, dk, dv, dgq, dgk = _dil_bwd(qkv, dil_g_qn[j:j + 1], dil_g_kn[j:j + 1], slopes, do, delta, lse, "dil_bwd")
        gain_g.setdefault("dil_g_qn", {})[j] = dgq
        gain_g.setdefault("dil_g_kn", {})[j] = dgk
        dqkv = jnp.concatenate([dq, dk, dv], axis=1)
        dxn = _mm(dqkv, full[("dil_w_qkv", j)], "nt", F32, "mm_dxn_dil", layer=0)
        grads[("dil_w_qkv", j)] = _mm(xn, dqkv, "tn", BF16, "mm_dwqkv")
        tok = scatter_start(f"mix_{l}", [("dil_w_qkv", j), ("dil_w_o", j)])
        toks = finish_due(dxn)
        dx, dg = _rms_bwd(xin, mix_norm[l:l + 1], dxn, dxo, "rms_bwd_dil", deps=[tok] + toks)
        gain_g.setdefault("mix_norm", {})[l] = dg
        return dx

    saved = []
    xc = x0
    for l in range(depth):
        xc, s1 = ffn_fwd(xc, ffn1_norm[l:l + 1], "ffn1", l, f"ffn1_{l}", deps=[ag_token] if l == 0 else ())
        xc, s2 = (mla_fwd if l % 2 == 0 else dil_fwd)(xc, l)
        xc, s3 = ffn_fwd(xc, ffn2_norm[l:l + 1], "ffn2", l, f"ffn2_{l}")
        saved.append((s1, s2, s3))

    dx, loss_part = _loss_head(xc, tgt, "loss_head")
    loss = lax.psum(loss_part[0, 0], MESH_AXES)

    for bi in reversed(range(len(blocks))):
        tag, _ = blocks[bi]
        l = bi // 3
        s = saved[l][bi % 3]
        if bi % 3 == 2:
            dx = ffn_bwd(dx, s, ffn2_norm[l:l + 1], "ffn2", l, tag)
        elif bi % 3 == 1:
            dx = (mla_bwd if l % 2 == 0 else dil_bwd)(dx, s, l)
        else:
            dx = ffn_bwd(dx, s, ffn1_norm[l:l + 1], "ffn1", l, tag)
    grad_x = dx.reshape(x.shape)
    after = dx
    while pending:
        after = scatter_finish(after)[-1]

    small = [n for n in names if n not in big]

    def gain_local(n):
        rows = [gain_g[n][l] for l in range(W[n].shape[0])]
        g = jnp.concatenate(rows, axis=1)
        return g

    def flat_pad(n, a):
        a = a.reshape(1, -1)
        if n in ("mla_g_qn", "mla_g_kn"):
            a = _pad_heads(a)
        return a

    packed_g = jnp.concatenate([gain_local(n) for n in small], axis=1)
    sizes = [gain_local(n).shape[1] for n in small]
    tot_g = _gain_allreduce(packed_g, "gain_allreduce")
    pw = jnp.concatenate([flat_pad(n, W[n]) for n in small], axis=1)
    pm = jnp.concatenate([flat_pad(n, M1[n]) for n in small], axis=1)
    pv = jnp.concatenate([flat_pad(n, V2[n]) for n in small], axis=1)
    res, _ = _adamw(tot_g.reshape(1, 1, -1), None, me, pw.reshape(1, 1, -1), pm.reshape(1, 1, -1),
                    pv.reshape(1, 1, -1), 0, None, "adamw_gains")
    res = [r.reshape(1, -1) for r in res]
    off = 0
    for n, sz in zip(small, sizes):
        for dst, r in zip((out_g, out_d, out_m, out_v), res):
            piece = r[:, off:off + sz]
            if n in ("mla_g_qn", "mla_g_kn"):
                piece = _unpad_heads(piece)
            dst[n] = piece.reshape(W[n].shape)
        off += sz

    return (loss, grad_x, *[out_g[n] for n in names], *[out_d[n] for n in names],
            *[out_m[n] for n in names], *[out_v[n] for n in names])
```

```python
import functools
import math

import jax
import jax.numpy as jnp
import numpy as np
from jax import lax
from jax.experimental import pallas as pl
from jax.experimental.pallas import tpu as pltpu

EPS = 1e-6
MLA_HEADS = 16
Q_LORA = 512
KV_LORA = 512
NOPE_DIM = 128
ROPE_DIM = 64
V_DIM = 128
QK_DIM = NOPE_DIM + ROPE_DIM
ROPE_THETA = 10000.0
HEAD_PAD = 256
LAT_PAD = Q_LORA + KV_LORA + 128
DIL_PAIRS = ((128, 1), (512, 4), (2048, 16))
DIL_GROUPS = 3
DIL_HEADS = 8
DIL_HEAD_DIM = 128
DIL_BLK = 128
ADAM_LR = 0.001
ADAM_B1 = 0.9
ADAM_B2 = 0.999
ADAM_EPS = 1e-08
ADAM_WD = 0.01
ADAM_STEP = 10

N_DEV = 8
MESH_AXES = ("x", "y", "c")
MESH = pl.DeviceIdType.MESH
NEG_BIG = -1e30
VMEM_LIMIT_V7X = 56 * 1024 * 1024
LANES = 128

BF16 = jnp.bfloat16
F32 = jnp.float32


def _pick(n, cands):
    for c in cands:
        if n % c == 0:
            return c
    raise ValueError(f"no tile for {n}")


def _params(sem):
    return pltpu.CompilerParams(dimension_semantics=sem, vmem_limit_bytes=VMEM_LIMIT_V7X)


ANY_SPEC = pl.BlockSpec(memory_space=pl.ANY)


MM_VMEM_BUDGET = 44 * 1024 * 1024
MM_HBM_BYTES_PER_S = 1.8e12
MM_MXU_FLOPS_PER_S = 8.5e14
MM_STEP_S = 0.4e-6
MM_MAX_TILE_MACS = 3.3e9
MXU_DIM = 256


@functools.lru_cache(maxsize=None)
def _mm_tiles(M, K, N, a_bytes, b_bytes, out_bytes, has_res):
    best = None
    for tk in [K] + [c for c in (1408, 1024, 512, 384, 256, 128) if K % c == 0 and c < K]:
        nk = K // tk
        for tm in [c for c in (2048, 1024, 512, 256, 128) if M % c == 0]:
            for tn in [c for c in (2816, 2048, 1408, 1152, 1024, 512, 384, 256, 128) if N % c == 0]:
                if tm * tk * tn > MM_MAX_TILE_MACS:
                    continue
                fill = (tn / (-(-tn // MXU_DIM) * MXU_DIM)) * (tk / (-(-tk // MXU_DIM) * MXU_DIM))
                fill *= tm / (tm + MXU_DIM // 2)
                vmem = 2 * (tm * tk * a_bytes + tk * tn * b_bytes) + 2 * tm * tn * out_bytes + tm * tn * 4
                vmem += (tm * tk + tk * tn) * 2 if max(a_bytes, b_bytes) > 2 else 0
                vmem += 2 * tm * tn * 4 if has_res else 0
                if vmem > MM_VMEM_BUDGET:
                    continue
                a_all, b_all = M * K * a_bytes, K * N * b_bytes
                if nk == 1:
                    t_i = a_all + (M // tm) * b_all
                    t_j = b_all + (N // tn) * a_all
                    traffic, i_outer = min((t_i, True), (t_j, False))
                else:
                    traffic, i_outer = (N // tn) * a_all + (M // tm) * b_all, True
                traffic += M * N * (out_bytes + (4 if has_res else 0))
                mxu = 2.0 * M * K * N / (MM_MXU_FLOPS_PER_S * fill) * (1.15 if nk > 1 else 1.0)
                cost = max(traffic / MM_HBM_BYTES_PER_S, mxu) + (M // tm) * (N // tn) * nk * MM_STEP_S
                if best is None or cost < best[0]:
                    best = (cost, tm, tn, tk, i_outer)
    assert best is not None, (M, K, N)
    return best[1:]


def _mm(a, b, mode, out_dtype, name, *, scale=1.0, res=None, layer=None, deps=()):
    b2 = b.shape[-2:]
    if mode == "nn":
        (M, K), (Kb, N) = a.shape, b2
    elif mode == "nt":
        (M, K), (N, Kb) = a.shape, b2
    else:
        (K, M), (Kb, N) = a.shape, b2
    assert K == Kb, (a.shape, b.shape, mode)
    tm, tn, tk, i_outer = _mm_tiles(M, K, N, a.dtype.itemsize, b.dtype.itemsize, jnp.dtype(out_dtype).itemsize,
                                    res is not None)
    nk = K // tk
    dims = {"nn": (((1,), (0,)), ((), ())), "nt": (((1,), (1,)), ((), ())), "tn": (((0,), (0,)), ((), ()))}[mode]

    def finish(v, r_ref, o_ref):
        if scale != 1.0:
            v = v * scale
        if r_ref is not None:
            v = r_ref[...] + v
        o_ref[...] = v.astype(o_ref.dtype)

    def body(*refs):
        a_ref, b_ref = refs[:2]
        r_ref = refs[2] if res is not None else None
        prod = lambda: lax.dot_general(a_ref[...].astype(BF16), b_ref[...].astype(BF16), dims,
                                       preferred_element_type=F32)
        if nk == 1:
            finish(prod(), r_ref, refs[-1])
            return
        o_ref, acc = refs[-2:]
        k = pl.program_id(2)

        @pl.when(k == 0)
        def _():
            acc[...] = prod()

        @pl.when(k > 0)
        def _():
            acc[...] += prod()

        @pl.when(k == nk - 1)
        def _():
            finish(acc[...], r_ref, o_ref)

    ij = (lambda p, q: (p, q)) if i_outer else (lambda p, q: (q, p))

    def spec(shape, f, lead=None):
        full = lambda p, q, k: f(*ij(p, q), k)
        if lead is None:
            return pl.BlockSpec(shape, full)
        return pl.BlockSpec((None,) + shape, lambda p, q, k: (lead,) + full(p, q, k))

    a_spec = spec((tk, tm), lambda i, j, k: (k, i)) if mode == "tn" else spec((tm, tk), lambda i, j, k: (i, k))
    lead = layer if b.ndim == 3 else None
    b_spec = spec((tn, tk), lambda i, j, k: (j, k), lead) if mode == "nt" else spec((tk, tn), lambda i, j, k: (k, j), lead)
    in_specs = [a_spec, b_spec]
    args = [a, b]
    if res is not None:
        in_specs.append(spec((tm, tn), lambda i, j, k: (i, j)))
        args.append(res)
    in_specs += [ANY_SPEC] * len(deps)
    args += list(deps)
    outer, inner = (M // tm, N // tn) if i_outer else (N // tn, M // tm)
    return pl.pallas_call(
        body, name=name, grid=(outer, inner, nk),
        in_specs=in_specs, out_specs=spec((tm, tn), lambda i, j, k: (i, j)),
        out_shape=jax.ShapeDtypeStruct((M, N), out_dtype),
        scratch_shapes=[pltpu.VMEM((tm, tn), F32)] if nk > 1 else [],
        compiler_params=_params(("parallel", "parallel", "arbitrary")),
    )(*args)


def _cast_into_gathered(w, layer, axis, me, name):
    _, R, C = w.shape
    tr = _pick(R, (512, 256, 128, 64, 32, 16))
    nr = R // tr
    axis, paired = axis

    def body(me_ref, w_ref, o_ref):
        o_ref[...] = w_ref[...].astype(BF16)

    if axis == 1:
        out_idx = lambda i, me_ref: (0, _slot(me_ref[0], paired) * nr + i, 0)
        shape = (1, R * N_DEV, C)
    else:
        out_idx = lambda i, me_ref: (0, i, _slot(me_ref[0], paired))
        shape = (1, R, C * N_DEV)
    return pl.pallas_call(
        body, name=name,
        grid_spec=pltpu.PrefetchScalarGridSpec(
            num_scalar_prefetch=1, grid=(nr,),
            in_specs=[pl.BlockSpec((None, tr, C), lambda i, me_ref: (layer, i, 0))],
            out_specs=pl.BlockSpec((None, tr, C), out_idx)),
        out_shape=jax.ShapeDtypeStruct(shape, BF16), compiler_params=_params(("parallel",)),
    )(me, w)


def _rms_fwd(x, g, name, deps=()):
    T, D = x.shape
    tr = _pick(T, (512, 256, 128))

    def body(x_ref, g_ref, *rest):
        o_ref = rest[-1]
        xv = x_ref[...]
        r = lax.rsqrt(jnp.mean(xv * xv, axis=-1, keepdims=True) + EPS)
        o_ref[...] = ((xv * r) * g_ref[...]).astype(BF16)

    return pl.pallas_call(
        body, name=name, grid=(T // tr,),
        in_specs=[pl.BlockSpec((tr, D), lambda i: (i, 0)), pl.BlockSpec((1, D), lambda i: (0, 0))]
        + [ANY_SPEC] * len(deps),
        out_specs=pl.BlockSpec((tr, D), lambda i: (i, 0)),
        out_shape=jax.ShapeDtypeStruct((T, D), BF16), compiler_params=_params(("parallel",)),
    )(x, g, *deps)


def _rms_bwd(x, g, dh, dres, name, deps=()):
    T, D = x.shape
    tr = _pick(T, (256, 128))

    def body(x_ref, g_ref, dh_ref, dres_ref, *rest):
        dx_ref, dxb_ref, dg_ref = rest[-3:]
        xv = x_ref[...]
        dhv = dh_ref[...]
        r = lax.rsqrt(jnp.mean(xv * xv, axis=-1, keepdims=True) + EPS)
        xhat = xv * r
        dxh = dhv * g_ref[...]
        c = jnp.mean(dxh * xhat, axis=-1, keepdims=True)
        dx = dres_ref[...] + r * (dxh - xhat * c)
        dx_ref[...] = dx
        dxb_ref[...] = dx.astype(BF16)

        @pl.when(pl.program_id(0) == 0)
        def _():
            dg_ref[...] = jnp.zeros_like(dg_ref)

        dg_ref[...] += jnp.sum(dhv * xhat, axis=0, keepdims=True)

    row = pl.BlockSpec((tr, D), lambda i: (i, 0))
    vec = pl.BlockSpec((1, D), lambda i: (0, 0))
    return pl.pallas_call(
        body, name=name, grid=(T // tr,),
        in_specs=[row, vec, row, row] + [ANY_SPEC] * len(deps), out_specs=[row, row, vec],
        out_shape=[jax.ShapeDtypeStruct((T, D), F32), jax.ShapeDtypeStruct((T, D), BF16),
                   jax.ShapeDtypeStruct((1, D), F32)],
        compiler_params=_params(("arbitrary",)),
    )(x, g, dh, dres, *deps)


N_PANEL = N_DEV // 2


def _ffn_in(h, w_in, name):
    T, D = h.shape
    F2 = w_in.shape[2]
    pw = F2 // N_PANEL
    half = pw // 2
    tm = _pick(T, (512, 256, 128))

    def body(h_ref, w_ref, u_ref, a_ref):
        r = jnp.dot(h_ref[...], w_ref[...], preferred_element_type=F32)
        u_ref[...] = r.astype(BF16)
        g, up = r[:, :half], r[:, half:]
        a_ref[...] = (g * jax.nn.sigmoid(g) * up).astype(BF16)

    return pl.pallas_call(
        body, name=name, grid=(N_PANEL, T // tm),
        in_specs=[pl.BlockSpec((tm, D), lambda p, i: (i, 0)), pl.BlockSpec((None, D, pw), lambda p, i: (0, 0, p))],
        out_specs=[pl.BlockSpec((tm, pw), lambda p, i: (i, p)), pl.BlockSpec((tm, half), lambda p, i: (i, p))],
        out_shape=[jax.ShapeDtypeStruct((T, F2), BF16), jax.ShapeDtypeStruct((T, F2 // 2), BF16)],
        compiler_params=_params(("parallel", "parallel")),
    )(h, w_in)


def _ffn_da(dxo, w_out, u, name, deps=()):
    T, D = dxo.shape
    F2 = u.shape[1]
    pw = F2 // N_PANEL
    half = pw // 2
    tm = _pick(T, (512, 256, 128))

    def body(d_ref, w_ref, u_ref, *rest):
        du_ref = rest[-1]
        da = 0.5 * lax.dot_general(d_ref[...], w_ref[...], NT_DIMS, preferred_element_type=F32)
        g = u_ref[:, :half].astype(F32)
        up = u_ref[:, half:].astype(F32)
        sg = jax.nn.sigmoid(g)
        silu = g * sg
        du_ref[:, :half] = (da * up * (sg + silu * (1.0 - sg))).astype(BF16)
        du_ref[:, half:] = (da * silu).astype(BF16)

    return pl.pallas_call(
        body, name=name, grid=(N_PANEL, T // tm),
        in_specs=[pl.BlockSpec((tm, D), lambda p, i: (i, 0)), pl.BlockSpec((None, half, D), lambda p, i: (0, p, 0)),
                  pl.BlockSpec((tm, pw), lambda p, i: (i, p))] + [ANY_SPEC] * len(deps),
        out_specs=pl.BlockSpec((tm, pw), lambda p, i: (i, p)),
        out_shape=jax.ShapeDtypeStruct((T, F2), BF16), compiler_params=_params(("parallel", "parallel")),
    )(dxo, w_out, u, *deps)


def _loss_head(y, t, name):
    T, D = y.shape
    tr = _pick(T, (512, 256, 128))

    def body(y_ref, t_ref, dy_ref, dyb_ref, l_ref):
        e = y_ref[...] - t_ref[...]
        dy = e * (1.0 / D)
        dy_ref[...] = dy
        dyb_ref[...] = dy.astype(BF16)

        @pl.when(pl.program_id(0) == 0)
        def _():
            l_ref[...] = jnp.zeros_like(l_ref)

        l_ref[...] += 0.5 * jnp.sum(jnp.mean(e * e, axis=-1, keepdims=True), axis=0, keepdims=True)

    row = pl.BlockSpec((tr, D), lambda i: (i, 0))
    return pl.pallas_call(
        body, name=name, grid=(T // tr,),
        in_specs=[row, row], out_specs=[row, row, pl.BlockSpec((1, 1), lambda i: (0, 0))],
        out_shape=[jax.ShapeDtypeStruct((T, D), F32), jax.ShapeDtypeStruct((T, D), BF16),
                   jax.ShapeDtypeStruct((1, 1), F32)],
        compiler_params=_params(("arbitrary",)),
    )(y, t)


def _rope_tables(S):
    half = ROPE_DIM // 2
    inv = 1.0 / (ROPE_THETA ** (jnp.arange(0, ROPE_DIM, 2, dtype=F32) / ROPE_DIM))
    ang = jnp.arange(S, dtype=F32)[:, None] * inv[None, :]
    cos, sin = jnp.cos(ang), jnp.sin(ang)
    z = jnp.zeros((S, half), F32)
    z2 = jnp.zeros((S, LANES - ROPE_DIM), F32)
    c = jnp.concatenate([cos, cos, z2], axis=1)
    s1 = jnp.concatenate([-sin, z, z2], axis=1)
    s2 = jnp.concatenate([z, sin, z2], axis=1)
    return c, s1, s2


def _rope(r, c, s1, s2):
    return r * c + pltpu.roll(r, LANES - ROPE_DIM // 2, 1) * s1 + pltpu.roll(r, ROPE_DIM // 2, 1) * s2


def _rope_t(d, c, s1, s2):
    return d * c + pltpu.roll(d * s1, ROPE_DIM // 2, 1) + pltpu.roll(d * s2, LANES - ROPE_DIM // 2, 1)


def _lat_norm_fwd(lat, g_cq, g_ckv, name):
    T = lat.shape[0]
    tr = _pick(T, (512, 256, 128))

    def body(lat_ref, gq_ref, gk_ref, cq_ref, ckv_ref):
        for off, g_ref, o_ref in ((0, gq_ref, cq_ref), (Q_LORA, gk_ref, ckv_ref)):
            xv = lat_ref[:, off:off + Q_LORA]
            r = lax.rsqrt(jnp.mean(xv * xv, axis=-1, keepdims=True) + EPS)
            o_ref[...] = ((xv * r) * g_ref[...]).astype(BF16)

    vec = pl.BlockSpec((1, Q_LORA), lambda i: (0, 0))
    out = pl.BlockSpec((tr, Q_LORA), lambda i: (i, 0))
    return pl.pallas_call(
        body, name=name, grid=(T // tr,),
        in_specs=[pl.BlockSpec((tr, LAT_PAD), lambda i: (i, 0)), vec, vec], out_specs=[out, out],
        out_shape=[jax.ShapeDtypeStruct((T, Q_LORA), BF16)] * 2, compiler_params=_params(("parallel",)),
    )(lat, g_cq, g_ckv)


def _lat_norm_bwd(lat, g_cq, g_ckv, dcq, dckv, dkpe, name):
    T = lat.shape[0]
    tr = _pick(T, (256, 128))

    def body(lat_ref, gq_ref, gk_ref, dcq_ref, dckv_ref, dkpe_ref, dlat_ref, dgq_ref, dgk_ref):
        @pl.when(pl.program_id(0) == 0)
        def _():
            dgq_ref[...] = jnp.zeros_like(dgq_ref)
            dgk_ref[...] = jnp.zeros_like(dgk_ref)

        for off, g_ref, d_ref, dg_ref in ((0, gq_ref, dcq_ref, dgq_ref), (Q_LORA, gk_ref, dckv_ref, dgk_ref)):
            xv = lat_ref[:, off:off + Q_LORA]
            dv = d_ref[...]
            r = lax.rsqrt(jnp.mean(xv * xv, axis=-1, keepdims=True) + EPS)
            xhat = xv * r
            dxh = dv * g_ref[...]
            c = jnp.mean(dxh * xhat, axis=-1, keepdims=True)
            dlat_ref[:, off:off + Q_LORA] = (r * (dxh - xhat * c)).astype(BF16)
            dg_ref[...] += jnp.sum(dv * xhat, axis=0, keepdims=True)
        dlat_ref[:, Q_LORA + KV_LORA:] = dkpe_ref[...].astype(BF16)

    vec = pl.BlockSpec((1, Q_LORA), lambda i: (0, 0))
    half = pl.BlockSpec((tr, Q_LORA), lambda i: (i, 0))
    full = pl.BlockSpec((tr, LAT_PAD), lambda i: (i, 0))
    return pl.pallas_call(
        body, name=name, grid=(T // tr,),
        in_specs=[full, vec, vec, half, half, pl.BlockSpec((tr, LANES), lambda i: (i, 0))],
        out_specs=[full, vec, vec],
        out_shape=[jax.ShapeDtypeStruct((T, LAT_PAD), BF16), jax.ShapeDtypeStruct((1, Q_LORA), F32),
                   jax.ShapeDtypeStruct((1, Q_LORA), F32)],
        compiler_params=_params(("arbitrary",)),
    )(lat, g_cq, g_ckv, dcq, dckv, dkpe)


def _mla_prep_fwd(q_raw, kv, lat, g_qn, g_kn, tabs, name):
    T = q_raw.shape[0]
    H = MLA_HEADS
    tr = _pick(T, (256, 128))

    def body(q_ref, kv_ref, kpe_ref, gq_ref, gk_ref, c_ref, s1_ref, s2_ref, qf_ref, kf_ref, v_ref):
        c, s1, s2 = c_ref[...], s1_ref[...], s2_ref[...]
        gq, gk = gq_ref[...], gk_ref[...]
        kpe = kpe_ref[...]
        kpe_ss = jnp.sum(kpe * kpe, axis=-1, keepdims=True)
        for h in range(H):
            lo = h * HEAD_PAD
            qa = q_ref[:, lo:lo + LANES]
            qb = q_ref[:, lo + LANES:lo + HEAD_PAD]
            ss = jnp.sum(qa * qa, axis=-1, keepdims=True) + jnp.sum(qb * qb, axis=-1, keepdims=True)
            r = lax.rsqrt(ss * (1.0 / QK_DIM) + EPS)
            qf_ref[:, lo:lo + LANES] = (qa * r * gq[:, :LANES]).astype(BF16)
            qf_ref[:, lo + LANES:lo + HEAD_PAD] = _rope(qb * r * gq[:, LANES:], c, s1, s2).astype(BF16)
            ka = kv_ref[:, lo:lo + LANES]
            ss = jnp.sum(ka * ka, axis=-1, keepdims=True) + kpe_ss
            r = lax.rsqrt(ss * (1.0 / QK_DIM) + EPS)
            kf_ref[:, lo:lo + LANES] = (ka * r * gk[:, :LANES]).astype(BF16)
            kf_ref[:, lo + LANES:lo + HEAD_PAD] = _rope(kpe * r * gk[:, LANES:], c, s1, s2).astype(BF16)
            v_ref[:, h * V_DIM:(h + 1) * V_DIM] = kv_ref[:, lo + LANES:lo + HEAD_PAD].astype(BF16)

    wide = pl.BlockSpec((tr, H * HEAD_PAD), lambda i: (i, 0))
    lane = pl.BlockSpec((tr, LANES), lambda i: (i, 0))
    gvec = pl.BlockSpec((1, HEAD_PAD), lambda i: (0, 0))
    return pl.pallas_call(
        body, name=name, grid=(T // tr,),
        in_specs=[wide, wide, pl.BlockSpec((tr, LANES), lambda i: (i, (Q_LORA + KV_LORA) // LANES)), gvec, gvec,
                  lane, lane, lane],
        out_specs=[wide, wide, pl.BlockSpec((tr, H * V_DIM), lambda i: (i, 0))],
        out_shape=[jax.ShapeDtypeStruct((T, H * HEAD_PAD), BF16), jax.ShapeDtypeStruct((T, H * HEAD_PAD), BF16),
                   jax.ShapeDtypeStruct((T, H * V_DIM), BF16)],
        compiler_params=_params(("parallel",)),
    )(q_raw, kv, lat, g_qn, g_kn, *tabs)


def _mla_prep_bwd(q_raw, kv, lat, g_qn, g_kn, tabs, dqf, dkf, dv, name):
    T = q_raw.shape[0]
    H = MLA_HEADS
    tr = _pick(T, (128,))

    def body(q_ref, kv_ref, kpe_ref, gq_ref, gk_ref, c_ref, s1_ref, s2_ref, dqf_ref, dkf_ref, dv_ref,
             dq_ref, dkv_ref, dkpe_ref, dgq_ref, dgk_ref):
        @pl.when(pl.program_id(0) == 0)
        def _():
            dgq_ref[...] = jnp.zeros_like(dgq_ref)
            dgk_ref[...] = jnp.zeros_like(dgk_ref)

        c, s1, s2 = c_ref[...], s1_ref[...], s2_ref[...]
        gq, gk = gq_ref[...], gk_ref[...]
        kpe = kpe_ref[...]
        kpe_ss = jnp.sum(kpe * kpe, axis=-1, keepdims=True)
        dkpe = jnp.zeros_like(kpe)
        dgq_a = jnp.zeros((1, LANES), F32)
        dgq_b = jnp.zeros((1, LANES), F32)
        dgk_a = jnp.zeros((1, LANES), F32)
        dgk_b = jnp.zeros((1, LANES), F32)
        for h in range(H):
            lo = h * HEAD_PAD
            xa = q_ref[:, lo:lo + LANES]
            xb = q_ref[:, lo + LANES:lo + HEAD_PAD]
            ss = jnp.sum(xa * xa, axis=-1, keepdims=True) + jnp.sum(xb * xb, axis=-1, keepdims=True)
            r = lax.rsqrt(ss * (1.0 / QK_DIM) + EPS)
            xa, xb = xa * r, xb * r
            da = dqf_ref[:, lo:lo + LANES]
            db = _rope_t(dqf_ref[:, lo + LANES:lo + HEAD_PAD], c, s1, s2)
            dgq_a += jnp.sum(da * xa, axis=0, keepdims=True)
            dgq_b += jnp.sum(db * xb, axis=0, keepdims=True)
            da, db = da * gq[:, :LANES], db * gq[:, LANES:]
            cc = (jnp.sum(da * xa, axis=-1, keepdims=True) + jnp.sum(db * xb, axis=-1, keepdims=True)) * (1.0 / QK_DIM)
            dq_ref[:, lo:lo + LANES] = (r * (da - xa * cc)).astype(BF16)
            dq_ref[:, lo + LANES:lo + HEAD_PAD] = (r * (db - xb * cc)).astype(BF16)
            xa = kv_ref[:, lo:lo + LANES]
            ss = jnp.sum(xa * xa, axis=-1, keepdims=True) + kpe_ss
            r = lax.rsqrt(ss * (1.0 / QK_DIM) + EPS)
            xa, xb = xa * r, kpe * r
            da = dkf_ref[:, lo:lo + LANES]
            db = _rope_t(dkf_ref[:, lo + LANES:lo + HEAD_PAD], c, s1, s2)
            dgk_a += jnp.sum(da * xa, axis=0, keepdims=True)
            dgk_b += jnp.sum(db * xb, axis=0, keepdims=True)
            da, db = da * gk[:, :LANES], db * gk[:, LANES:]
            cc = (jnp.sum(da * xa, axis=-1, keepdims=True) + jnp.sum(db * xb, axis=-1, keepdims=True)) * (1.0 / QK_DIM)
            dkv_ref[:, lo:lo + LANES] = (r * (da - xa * cc)).astype(BF16)
            dkpe = dkpe + r * (db - xb * cc)
            dkv_ref[:, lo + LANES:lo + HEAD_PAD] = dv_ref[:, h * V_DIM:(h + 1) * V_DIM].astype(BF16)
        dkpe_ref[...] = dkpe
        dgq_ref[:, :LANES] += dgq_a
        dgq_ref[:, LANES:] += dgq_b
        dgk_ref[:, :LANES] += dgk_a
        dgk_ref[:, LANES:] += dgk_b

    wide = pl.BlockSpec((tr, H * HEAD_PAD), lambda i: (i, 0))
    lane = pl.BlockSpec((tr, LANES), lambda i: (i, 0))
    gvec = pl.BlockSpec((1, HEAD_PAD), lambda i: (0, 0))
    vspec = pl.BlockSpec((tr, H * V_DIM), lambda i: (i, 0))
    return pl.pallas_call(
        body, name=name, grid=(T // tr,),
        in_specs=[wide, wide, pl.BlockSpec((tr, LANES), lambda i: (i, (Q_LORA + KV_LORA) // LANES)), gvec, gvec,
                  lane, lane, lane, wide, wide, vspec],
        out_specs=[wide, wide, lane, gvec, gvec],
        out_shape=[jax.ShapeDtypeStruct((T, H * HEAD_PAD), BF16), jax.ShapeDtypeStruct((T, H * HEAD_PAD), BF16),
                   jax.ShapeDtypeStruct((T, LANES), F32), jax.ShapeDtypeStruct((1, HEAD_PAD), F32),
                   jax.ShapeDtypeStruct((1, HEAD_PAD), F32)],
        compiler_params=_params(("arbitrary",)),
    )(q_raw, kv, lat, g_qn, g_kn, *tabs, dqf, dkf, dv)


def _causal_mask(tq, tk):
    return lax.broadcasted_iota(jnp.int32, (tq, tk), 1) <= lax.broadcasted_iota(jnp.int32, (tq, tk), 0)


NT_DIMS = (((1,), (1,)), ((), ()))
TN_DIMS = (((0,), (0,)), ((), ()))


def _flash_fwd(qf, kf, v, name):
    T = qf.shape[0]
    H = MLA_HEADS
    t = _pick(T, (512, 256, 128))
    n = T // t
    scale = 1.0 / math.sqrt(QK_DIM)

    def body(q_ref, k_ref, v_ref, o_ref, lse_ref, m_sc, l_sc, acc_sc):
        i, j = pl.program_id(1), pl.program_id(2)

        @pl.when(j == 0)
        def _():
            m_sc[...] = jnp.full_like(m_sc, NEG_BIG)
            l_sc[...] = jnp.zeros_like(l_sc)
            acc_sc[...] = jnp.zeros_like(acc_sc)

        def step(masked):
            s = lax.dot_general(q_ref[...], k_ref[...], NT_DIMS, preferred_element_type=F32) * scale
            if masked:
                s = jnp.where(_causal_mask(t, t), s, NEG_BIG)
            m_prev = m_sc[:, :1]
            m_new = jnp.maximum(m_prev, jnp.max(s, axis=-1, keepdims=True))
            a = jnp.exp(m_prev - m_new)
            p = jnp.exp(s - m_new)
            l_sc[...] = a * l_sc[...] + jnp.sum(p, axis=-1, keepdims=True)
            acc_sc[...] = a * acc_sc[...] + jnp.dot(p.astype(BF16), v_ref[...], preferred_element_type=F32)
            m_sc[...] = jnp.broadcast_to(m_new, m_sc.shape)

        @pl.when(j < i)
        def _():
            step(False)

        @pl.when(j == i)
        def _():
            step(True)
            o_ref[...] = (acc_sc[...] / l_sc[...]).astype(BF16)
            lse_ref[...] = m_sc[...] + jnp.log(l_sc[...])

    return pl.pallas_call(
        body, name=name, grid=(H, n, n),
        in_specs=[pl.BlockSpec((t, HEAD_PAD), lambda h, i, j: (i, h)),
                  pl.BlockSpec((t, HEAD_PAD), lambda h, i, j: (jnp.minimum(j, i), h)),
                  pl.BlockSpec((t, V_DIM), lambda h, i, j: (jnp.minimum(j, i), h))],
        out_specs=[pl.BlockSpec((t, V_DIM), lambda h, i, j: (i, h)), pl.BlockSpec((t, V_DIM), lambda h, i, j: (i, h))],
        out_shape=[jax.ShapeDtypeStruct((T, H * V_DIM), BF16), jax.ShapeDtypeStruct((T, H * V_DIM), F32)],
        scratch_shapes=[pltpu.VMEM((t, LANES), F32), pltpu.VMEM((t, LANES), F32), pltpu.VMEM((t, V_DIM), F32)],
        compiler_params=_params(("parallel", "parallel", "arbitrary")),
    )(qf, kf, v)


def _attn_delta(do, o, name):
    T, W = do.shape
    nh = W // V_DIM
    tr = _pick(T, (512, 256, 128))

    def body(do_ref, o_ref, d_ref):
        for h in range(nh):
            sl = slice(h * V_DIM, (h + 1) * V_DIM)
            d = jnp.sum(do_ref[:, sl].astype(F32) * o_ref[:, sl].astype(F32), axis=-1, keepdims=True)
            d_ref[:, sl] = jnp.broadcast_to(d, (tr, V_DIM))

    row = pl.BlockSpec((tr, W), lambda i: (i, 0))
    return pl.pallas_call(
        body, name=name, grid=(T // tr,), in_specs=[row, row], out_specs=row,
        out_shape=jax.ShapeDtypeStruct((T, W), F32), compiler_params=_params(("parallel",)),
    )(do, o)


def _flash_bwd_dq(qf, kf, v, do, lse, delta, name):
    T = qf.shape[0]
    H = MLA_HEADS
    t = _pick(T, (512, 256, 128))
    n = T // t
    scale = 1.0 / math.sqrt(QK_DIM)

    def body(q_ref, k_ref, v_ref, do_ref, lse_ref, dl_ref, dq_ref, acc):
        i, j = pl.program_id(1), pl.program_id(2)

        @pl.when(j == 0)
        def _():
            acc[...] = jnp.zeros_like(acc)

        def step(masked):
            s = lax.dot_general(q_ref[...], k_ref[...], NT_DIMS, preferred_element_type=F32) * scale
            if masked:
                s = jnp.where(_causal_mask(t, t), s, NEG_BIG)
            p = jnp.exp(s - lse_ref[:, :1])
            dp = lax.dot_general(do_ref[...], v_ref[...], NT_DIMS, preferred_element_type=F32)
            ds = (p * (dp - dl_ref[:, :1])).astype(BF16)
            acc[...] += jnp.dot(ds, k_ref[...], preferred_element_type=F32)

        @pl.when(j < i)
        def _():
            step(False)

        @pl.when(j == i)
        def _():
            step(True)
            dq_ref[...] = acc[...] * scale

    qs = pl.BlockSpec((t, HEAD_PAD), lambda h, i, j: (i, h))
    ks = pl.BlockSpec((t, HEAD_PAD), lambda h, i, j: (jnp.minimum(j, i), h))
    vs = pl.BlockSpec((t, V_DIM), lambda h, i, j: (jnp.minimum(j, i), h))
    rs = pl.BlockSpec((t, V_DIM), lambda h, i, j: (i, h))
    return pl.pallas_call(
        body, name=name, grid=(H, n, n), in_specs=[qs, ks, vs, rs, rs, rs], out_specs=qs,
        out_shape=jax.ShapeDtypeStruct((T, H * HEAD_PAD), F32),
        scratch_shapes=[pltpu.VMEM((t, HEAD_PAD), F32)],
        compiler_params=_params(("parallel", "parallel", "arbitrary")),
    )(qf, kf, v, do, lse, delta)


def _flash_bwd_dkv(qf, kf, v, do, lse, delta, name):
    T = qf.shape[0]
    H = MLA_HEADS
    t = _pick(T, (512, 256, 128))
    n = T // t
    scale = 1.0 / math.sqrt(QK_DIM)

    def body(q_ref, k_ref, v_ref, do_ref, lse_ref, dl_ref, dk_ref, dv_ref, dk_acc, dv_acc):
        j, i = pl.program_id(1), pl.program_id(2)

        @pl.when(i == 0)
        def _():
            dk_acc[...] = jnp.zeros_like(dk_acc)
            dv_acc[...] = jnp.zeros_like(dv_acc)

        def step(masked):
            s = lax.dot_general(q_ref[...], k_ref[...], NT_DIMS, preferred_element_type=F32) * scale
            if masked:
                s = jnp.where(_causal_mask(t, t), s, NEG_BIG)
            p = jnp.exp(s - lse_ref[:, :1])
            dp = lax.dot_general(do_ref[...], v_ref[...], NT_DIMS, preferred_element_type=F32)
            ds = (p * (dp - dl_ref[:, :1])).astype(BF16)
            dv_acc[...] += lax.dot_general(p.astype(BF16), do_ref[...], TN_DIMS, preferred_element_type=F32)
            dk_acc[...] += lax.dot_general(ds, q_ref[...], TN_DIMS, preferred_element_type=F32)

        @pl.when(i == j)
        def _():
            step(True)

        @pl.when(i > j)
        def _():
            step(False)

        @pl.when(i == n - 1)
        def _():
            dk_ref[...] = dk_acc[...] * scale
            dv_ref[...] = dv_acc[...]

    qs = pl.BlockSpec((t, HEAD_PAD), lambda h, j, i: (jnp.maximum(i, j), h))
    rs = pl.BlockSpec((t, V_DIM), lambda h, j, i: (jnp.maximum(i, j), h))
    ks = pl.BlockSpec((t, HEAD_PAD), lambda h, j, i: (j, h))
    vs = pl.BlockSpec((t, V_DIM), lambda h, j, i: (j, h))
    return pl.pallas_call(
        body, name=name, grid=(H, n, n), in_specs=[qs, ks, vs, rs, rs, rs], out_specs=[ks, vs],
        out_shape=[jax.ShapeDtypeStruct((T, H * HEAD_PAD), F32), jax.ShapeDtypeStruct((T, H * V_DIM), F32)],
        scratch_shapes=[pltpu.VMEM((t, HEAD_PAD), F32), pltpu.VMEM((t, V_DIM), F32)],
        compiler_params=_params(("parallel", "parallel", "arbitrary")),
    )(qf, kf, v, do, lse, delta)


def _alibi_slopes():
    tot = DIL_GROUPS * DIL_HEADS
    return [float(np.float32(2.0) ** (np.float32(-8.0) * np.float32(k) / np.float32(tot))) for k in range(1, tot + 1)]


def _dil_masks():
    iq = lax.broadcasted_iota(jnp.int32, (DIL_BLK, DIL_BLK), 0)
    ik = lax.broadcasted_iota(jnp.int32, (DIL_BLK, DIL_BLK), 1)
    return (ik >= iq), (iq + DIL_BLK - ik).astype(F32), (ik <= iq), (iq - ik).astype(F32)


def _dil_norm(x, g):
    r = lax.rsqrt(jnp.mean(x * x, axis=-1, keepdims=True) + EPS)
    return x * r, r


def _dil_fwd(qkv, g_qn, g_kn, slopes, name):
    T = qkv.shape[0]
    GH = DIL_GROUPS * DIL_HEADS
    scale = 1.0 / math.sqrt(DIL_HEAD_DIM)

    def body(sl_ref, q_ref, k_ref, v_ref, gq_ref, gk_ref, o_ref, lse_ref):
        gh = pl.program_id(0)
        slope = sl_ref[gh]
        ok_p, dist_p, ok_c, dist_c = _dil_masks()
        gq, gk = gq_ref[...], gk_ref[...]
        for g, (_, d) in enumerate(DIL_PAIRS):
            @pl.when((gh >= g * DIL_HEADS) & (gh < (g + 1) * DIL_HEADS))
            def _(d=d):
                nb = T // (d * DIL_BLK)
                bias_p = jnp.where(ok_p, -slope * d * dist_p, NEG_BIG)
                bias_c = jnp.where(ok_c, -slope * d * dist_c, NEG_BIG)

                def phase(r, _):
                    def blk(nn, _):
                        def rows(b):
                            return pl.ds(b * (d * DIL_BLK) + r, DIL_BLK, stride=d) if d > 1 else pl.ds(pl.multiple_of(b * DIL_BLK, DIL_BLK), DIL_BLK)
                        cur, prv = rows(nn), rows(jnp.maximum(nn - 1, 0))
                        q = (_dil_norm(q_ref[cur, :], gq)[0] * gq).astype(BF16)
                        kc = (_dil_norm(k_ref[cur, :], gk)[0] * gk).astype(BF16)
                        kp = (_dil_norm(k_ref[prv, :], gk)[0] * gk).astype(BF16)
                        s_c = lax.dot_general(q, kc, NT_DIMS, preferred_element_type=F32) * scale + bias_c
                        s_p = lax.dot_general(q, kp, NT_DIMS, preferred_element_type=F32) * scale + bias_p
                        s_p = jnp.where(nn > 0, s_p, NEG_BIG)
                        m = jnp.maximum(jnp.max(s_c, axis=-1, keepdims=True), jnp.max(s_p, axis=-1, keepdims=True))
                        p_c = jnp.exp(s_c - m)
                        p_p = jnp.exp(s_p - m)
                        l = jnp.sum(p_c, axis=-1, keepdims=True) + jnp.sum(p_p, axis=-1, keepdims=True)
                        acc = jnp.dot(p_c.astype(BF16), v_ref[cur, :].astype(BF16), preferred_element_type=F32)
                        acc += jnp.dot(p_p.astype(BF16), v_ref[prv, :].astype(BF16), preferred_element_type=F32)
                        o_ref[cur, :] = acc / l
                        lse_ref[cur, :] = jnp.broadcast_to(m + jnp.log(l), (DIL_BLK, DIL_HEAD_DIM))
                        return 0
                    lax.fori_loop(0, nb, blk, 0)
                    return 0
                lax.fori_loop(0, d, phase, 0)

    col = lambda off: pl.BlockSpec((T, DIL_HEAD_DIM), lambda gh, sl: (0, gh + off))
    gvec = pl.BlockSpec((1, DIL_HEAD_DIM), lambda gh, sl: (0, 0))
    return pl.pallas_call(
        body, name=name,
        grid_spec=pltpu.PrefetchScalarGridSpec(
            num_scalar_prefetch=1, grid=(GH,),
            in_specs=[col(0), col(GH), col(2 * GH), gvec, gvec], out_specs=[col(0), col(0)]),
        out_shape=[jax.ShapeDtypeStruct((T, GH * DIL_HEAD_DIM), F32)] * 2,
        compiler_params=_params(("parallel",)),
    )(slopes, qkv, qkv, qkv, g_qn, g_kn)


def _dil_merge(o_g, lse_g, name):
    T = o_g.shape[0]
    W = DIL_HEADS * DIL_HEAD_DIM
    tr = _pick(T, (256, 128))

    def body(o0, o1, o2, l0, l1, l2, o_ref, lse_ref):
        a, b, c = l0[...], l1[...], l2[...]
        m = jnp.maximum(jnp.maximum(a, b), c)
        ea, eb, ec = jnp.exp(a - m), jnp.exp(b - m), jnp.exp(c - m)
        tot = ea + eb + ec
        o_ref[...] = ((o0[...] * ea + o1[...] * eb + o2[...] * ec) / tot).astype(BF16)
        lse_ref[...] = m + jnp.log(tot)

    grp = lambda g: pl.BlockSpec((tr, W), lambda i: (i, g))
    out = pl.BlockSpec((tr, W), lambda i: (i, 0))
    return pl.pallas_call(
        body, name=name, grid=(T // tr,),
        in_specs=[grp(0), grp(1), grp(2), grp(0), grp(1), grp(2)], out_specs=[out, out],
        out_shape=[jax.ShapeDtypeStruct((T, W), BF16), jax.ShapeDtypeStruct((T, W), F32)],
        compiler_params=_params(("parallel",)),
    )(o_g, o_g, o_g, lse_g, lse_g, lse_g)


def _dil_bwd(qkv, g_qn, g_kn, slopes, do, delta, lse, name):
    T = qkv.shape[0]
    GH = DIL_GROUPS * DIL_HEADS
    scale = 1.0 / math.sqrt(DIL_HEAD_DIM)
    nchunk = T // DIL_BLK

    def body(sl_ref, q_ref, k_ref, v_ref, gq_ref, gk_ref, do_ref, dl_ref, lse_ref,
             dq_ref, dk_ref, dv_ref, dgq_ref, dgk_ref, dq_acc, dk_acc, dv_acc):
        gh = pl.program_id(0)
        slope = sl_ref[gh]
        ok_p, dist_p, ok_c, dist_c = _dil_masks()
        gq, gk = gq_ref[...], gk_ref[...]

        @pl.when(gh == 0)
        def _():
            dgq_ref[...] = jnp.zeros_like(dgq_ref)
            dgk_ref[...] = jnp.zeros_like(dgk_ref)

        dk_acc[...] = jnp.zeros_like(dk_acc)
        dv_acc[...] = jnp.zeros_like(dv_acc)
        for g, (_, d) in enumerate(DIL_PAIRS):
            @pl.when((gh >= g * DIL_HEADS) & (gh < (g + 1) * DIL_HEADS))
            def _(d=d):
                nb = T // (d * DIL_BLK)
                bias_p = jnp.where(ok_p, -slope * d * dist_p, NEG_BIG)
                bias_c = jnp.where(ok_c, -slope * d * dist_c, NEG_BIG)

                def phase(r, _):
                    def blk(nn, _):
                        def rows(b):
                            return pl.ds(b * (d * DIL_BLK) + r, DIL_BLK, stride=d) if d > 1 else pl.ds(pl.multiple_of(b * DIL_BLK, DIL_BLK), DIL_BLK)
                        cur, prv = rows(nn), rows(jnp.maximum(nn - 1, 0))
                        q = (_dil_norm(q_ref[cur, :], gq)[0] * gq).astype(BF16)
                        kc = (_dil_norm(k_ref[cur, :], gk)[0] * gk).astype(BF16)
                        kp = (_dil_norm(k_ref[prv, :], gk)[0] * gk).astype(BF16)
                        vc = v_ref[cur, :].astype(BF16)
                        vp = v_ref[prv, :].astype(BF16)
                        dob = do_ref[cur, :].astype(BF16)
                        delta = dl_ref[cur, :][:, :1]
                        ls = lse_ref[cur, :][:, :1]
                        s_c = lax.dot_general(q, kc, NT_DIMS, preferred_element_type=F32) * scale + bias_c
                        s_p = lax.dot_general(q, kp, NT_DIMS, preferred_element_type=F32) * scale + bias_p
                        s_p = jnp.where(nn > 0, s_p, NEG_BIG)
                        p_c = jnp.exp(s_c - ls)
                        p_p = jnp.exp(s_p - ls)
                        dp_c = lax.dot_general(dob, vc, NT_DIMS, preferred_element_type=F32)
                        dp_p = lax.dot_general(dob, vp, NT_DIMS, preferred_element_type=F32)
                        ds_c = (p_c * (dp_c - delta)).astype(BF16)
                        ds_p = (p_p * (dp_p - delta)).astype(BF16)
                        dq_acc[cur, :] = (jnp.dot(ds_c, kc, preferred_element_type=F32)
                                          + jnp.dot(ds_p, kp, preferred_element_type=F32)) * scale
                        dk_acc[cur, :] += lax.dot_general(ds_c, q, TN_DIMS, preferred_element_type=F32) * scale
                        dv_acc[cur, :] += lax.dot_general(p_c.astype(BF16), dob, TN_DIMS, preferred_element_type=F32)
                        dk_acc[prv, :] += lax.dot_general(ds_p, q, TN_DIMS, preferred_element_type=F32) * scale
                        dv_acc[prv, :] += lax.dot_general(p_p.astype(BF16), dob, TN_DIMS, preferred_element_type=F32)
                        return 0
                    lax.fori_loop(0, nb, blk, 0)
                    return 0
                lax.fori_loop(0, d, phase, 0)

        def fin(ci, carry):
            dgq, dgk = carry
            rows = pl.ds(pl.multiple_of(ci * DIL_BLK, DIL_BLK), DIL_BLK)
            outs = []
            for x_ref, d_acc, gvec in ((q_ref, dq_acc, gq), (k_ref, dk_acc, gk)):
                xhat, r = _dil_norm(x_ref[rows, :], gvec)
                dn = d_acc[rows, :]
                dxh = dn * gvec
                c = jnp.mean(dxh * xhat, axis=-1, keepdims=True)
                outs.append(((r * (dxh - xhat * c)).astype(BF16), jnp.sum(dn * xhat, axis=0, keepdims=True)))
            dq_ref[rows, :] = outs[0][0]
            dk_ref[rows, :] = outs[1][0]
            dv_ref[rows, :] = dv_acc[rows, :].astype(BF16)
            return dgq + outs[0][1], dgk + outs[1][1]

        z = jnp.zeros((1, DIL_HEAD_DIM), F32)
        dgq, dgk = lax.fori_loop(0, nchunk, fin, (z, z))
        dgq_ref[...] += dgq
        dgk_ref[...] += dgk

    col = lambda off: pl.BlockSpec((T, DIL_HEAD_DIM), lambda gh, sl: (0, gh + off))
    hcol = pl.BlockSpec((T, DIL_HEAD_DIM), lambda gh, sl: (0, gh % DIL_HEADS))
    gvec = pl.BlockSpec((1, DIL_HEAD_DIM), lambda gh, sl: (0, 0))
    wide = jax.ShapeDtypeStruct((T, GH * DIL_HEAD_DIM), BF16)
    vec = jax.ShapeDtypeStruct((1, DIL_HEAD_DIM), F32)
    return pl.pallas_call(
        body, name=name,
        grid_spec=pltpu.PrefetchScalarGridSpec(
            num_scalar_prefetch=1, grid=(GH,),
            in_specs=[col(0), col(GH), col(2 * GH), gvec, gvec, hcol, hcol, hcol],
            out_specs=[col(0), col(0), col(0), gvec, gvec],
            scratch_shapes=[pltpu.VMEM((T, DIL_HEAD_DIM), F32)] * 3),
        out_shape=[wide, wide, wide, vec, vec],
        compiler_params=_params(("arbitrary",)),
    )(slopes, qkv, qkv, qkv, g_qn, g_kn, do, delta, lse)


def _my_pos():
    return lax.axis_index("x"), lax.axis_index("y"), lax.axis_index("c")


def _peer(pos, j):
    x, y, c = pos
    px = 1 - x if j & 4 else x
    py = 1 - y if j & 2 else y
    pc = 1 - c if j & 1 else c
    return (px, py, pc), 4 * px + 2 * py + pc


def _slot(idx, paired):
    if not paired:
        return idx
    return jnp.where(idx < N_DEV // 2, 2 * idx, 2 * idx - (N_DEV - 1))


def _shard_slice(ref, axis, idx, size, paired=False):
    sl = [slice(None)] * len(ref.shape)
    sl[axis] = pl.ds(pl.multiple_of(_slot(idx, paired) * size, 8), size)
    return ref.at[tuple(sl)]


HBM_SPEC = pl.BlockSpec(memory_space=pltpu.HBM)
SEM_SPEC = pl.BlockSpec(memory_space=pltpu.SEMAPHORE)
DATAFLOW = pltpu.SideEffectType.DATAFLOW_SIDE_EFFECTING
N_PEER = N_DEV - 1


def _xchg_refs(kind, axis, bufs, frm, to):
    axis, paired = axis
    if kind == "gather":
        size = bufs[0].shape[axis] // N_DEV
        piece = _shard_slice(bufs[0], axis, frm, size, paired)
        return piece, piece
    grad, slots = bufs
    return _shard_slice(grad, axis, to, grad.shape[axis] // N_DEV, paired), slots.at[frm]


def _xchg_start(kind, arrays, axes, name, deps=()):
    n = len(arrays)
    nb = 1 if kind == "gather" else 2

    def body(*refs):
        bufs = [refs[a * nb:(a + 1) * nb] for a in range(n)]
        outs = refs[nb * n + len(deps):]
        send, recv, token = outs[:n], outs[n:2 * n], outs[2 * n + nb * n]
        pos = _my_pos()
        me = 4 * pos[0] + 2 * pos[1] + pos[2]
        for a in range(n):
            for j in range(1, N_DEV):
                dev, pid = _peer(pos, j)
                src, dst = _xchg_refs(kind, axes[a], bufs[a], me, pid)
                pltpu.make_async_remote_copy(src_ref=src, dst_ref=dst, send_sem=send[a].at[j - 1],
                                             recv_sem=recv[a].at[j - 1], device_id=dev, device_id_type=MESH).start()
        token[...] = jnp.zeros_like(token)

    ops = []
    for g, (ax, _) in zip(arrays, axes):
        ops.append(g)
        if kind == "scatter":
            shp = list(g.shape)
            shp[ax] //= N_DEV
            ops.append(lax.empty((N_DEV,) + tuple(shp), g.dtype))
    sems = [pltpu.SemaphoreType.DMA((N_PEER,))] * (2 * n)
    res = pl.pallas_call(
        body, name=name,
        out_shape=sems + [pltpu.HBM(o.shape, o.dtype) for o in ops] + [jax.ShapeDtypeStruct((8, LANES), F32)],
        in_specs=[HBM_SPEC] * len(ops) + [ANY_SPEC] * len(deps),
        out_specs=[SEM_SPEC] * (2 * n) + [HBM_SPEC] * len(ops) + [pl.BlockSpec(memory_space=pltpu.VMEM)],
        input_output_aliases={i: 2 * n + i for i in range(len(ops))},
        compiler_params=pltpu.CompilerParams(has_side_effects=DATAFLOW),
    )(*[pltpu.with_memory_space_constraint(o, pltpu.HBM) for o in ops], *deps)
    items = [(res[a], res[n + a], tuple(res[2 * n + a * nb:2 * n + (a + 1) * nb])) for a in range(n)]
    return items, res[2 * n + len(ops)]


def _xchg_wait(kind, items, axes, after, name):
    n = len(items)
    nb = 1 if kind == "gather" else 2

    def body(*refs):
        bufs = [refs[a * nb:(a + 1) * nb] for a in range(n)]
        send, recv = refs[nb * n:nb * n + n], refs[nb * n + n:nb * n + 2 * n]
        pos = _my_pos()
        me = 4 * pos[0] + 2 * pos[1] + pos[2]
        for a in range(n):
            for j in range(1, N_DEV):
                dev, pid = _peer(pos, j)
                src, dst = _xchg_refs(kind, axes[a], bufs[a], pid, me)
                cp = pltpu.make_async_remote_copy(src_ref=src, dst_ref=dst, send_sem=send[a].at[j - 1],
                                                  recv_sem=recv[a].at[j - 1], device_id=dev, device_id_type=MESH)
                cp.wait_send()
                cp.wait_recv()

    ops = [b for it in items for b in it[2]]
    res = pl.pallas_call(
        body, name=name,
        out_shape=[pltpu.HBM(o.shape, o.dtype) for o in ops],
        in_specs=[HBM_SPEC] * len(ops) + [SEM_SPEC] * (2 * n) + [ANY_SPEC],
        out_specs=[HBM_SPEC] * len(ops),
        input_output_aliases={i: i for i in range(len(ops))},
        compiler_params=pltpu.CompilerParams(has_side_effects=DATAFLOW),
    )(*ops, *[it[0] for it in items], *[it[1] for it in items], after)
    return [res[a * nb + nb - 1] for a in range(n)]


def _gain_allreduce(v, name):
    n = v.shape[1]

    def body(v_ref, o_ref, slots, send_sems, recv_sems):
        pos = _my_pos()
        me = 4 * pos[0] + 2 * pos[1] + pos[2]
        slots[me] = v_ref[...]
        copies = []
        for j in range(1, N_DEV):
            dev, _ = _peer(pos, j)
            cp = pltpu.make_async_remote_copy(
                src_ref=slots.at[me], dst_ref=slots.at[me], send_sem=send_sems.at[j], recv_sem=recv_sems.at[j],
                device_id=dev, device_id_type=MESH)
            cp.start()
            copies.append(cp)
        for j in range(1, N_DEV):
            dev, pid = _peer(pos, j)
            pltpu.make_async_remote_copy(
                src_ref=slots.at[me], dst_ref=slots.at[pid], send_sem=send_sems.at[j], recv_sem=recv_sems.at[j],
                device_id=dev, device_id_type=MESH).wait_recv()
        for cp in copies:
            cp.wait_send()
        acc = slots[0]
        for s in range(1, N_DEV):
            acc = acc + slots[s]
        o_ref[...] = acc

    return pl.pallas_call(
        body, name=name, out_shape=jax.ShapeDtypeStruct((1, n), F32),
        in_specs=[pl.BlockSpec(memory_space=pltpu.VMEM)], out_specs=pl.BlockSpec(memory_space=pltpu.VMEM),
        scratch_shapes=[pltpu.VMEM((N_DEV, 1, n), F32), pltpu.SemaphoreType.DMA((N_DEV,)),
                        pltpu.SemaphoreType.DMA((N_DEV,))],
        compiler_params=pltpu.CompilerParams(has_side_effects=True),
    )(v)


def _adamw(parts, own, me, w, m, v, layer, prev, name):
    L, R, C = w.shape
    P = parts.shape[0]
    tr = _pick(R, (128, 64, 32, 16, 8, 1))
    c1 = 1.0 - ADAM_B1 ** ADAM_STEP
    c2 = 1.0 - ADAM_B2 ** ADAM_STEP
    n_in = 4 if own is None else 5

    def body(me_ref, *refs):
        p_ref = refs[0]
        w_ref, m_ref, v_ref = refs[n_in - 3:n_in]
        g_out, d_out, m_out, v_out, tok = refs[-5:]
        g = None
        for s in range(P):
            part = p_ref[s]
            if own is not None:
                part = jnp.where(me_ref[0] == s, refs[1][...], part)
            g = part.astype(F32) if g is None else g + part.astype(F32)
        mn = ADAM_B1 * m_ref[...] + (1.0 - ADAM_B1) * g
        vn = ADAM_B2 * v_ref[...] + (1.0 - ADAM_B2) * (g * g)
        g_out[...] = g
        m_out[...] = mn
        v_out[...] = vn
        d_out[...] = -ADAM_LR * ((mn / c1) / (jnp.sqrt(vn / c2) + ADAM_EPS) + ADAM_WD * w_ref[...])
        tok[...] = jnp.zeros_like(tok)

    row = pl.BlockSpec((None, tr, C), lambda i, me_ref: (layer, i, 0))
    in_specs = [pl.BlockSpec((P, tr, C), lambda i, me_ref: (0, i, 0))]
    args = [parts]
    if own is not None:
        in_specs.append(pl.BlockSpec((tr, C), lambda i, me_ref: (i, 0)))
        args.append(own)
    in_specs += [row, row, row]
    args += [w, m, v]
    aliases = {}
    if prev is not None:
        in_specs += [ANY_SPEC] * 4
        aliases = {1 + len(args) + k: k for k in range(4)}
        args += list(prev)
    shp = jax.ShapeDtypeStruct((L, R, C), F32)
    res = pl.pallas_call(
        body, name=name,
        grid_spec=pltpu.PrefetchScalarGridSpec(
            num_scalar_prefetch=1, grid=(R // tr,), in_specs=in_specs,
            out_specs=[row] * 4 + [pl.BlockSpec((8, LANES), lambda i, me_ref: (0, 0))]),
        out_shape=[shp] * 4 + [jax.ShapeDtypeStruct((8, LANES), F32)],
        input_output_aliases=aliases, compiler_params=_params(("arbitrary",)),
    )(me, *args)
    return res[:4], res[4]


def _pad_heads(w):
    lead = w.shape[:-1]
    n = w.shape[-1] // QK_DIM
    w = w.reshape(lead + (n, QK_DIM))
    w = jnp.pad(w, [(0, 0)] * len(lead) + [(0, 0), (0, HEAD_PAD - QK_DIM)])
    return w.reshape(lead + (n * HEAD_PAD,))


def _unpad_heads(w):
    lead = w.shape[:-1]
    n = w.shape[-1] // HEAD_PAD
    return w.reshape(lead + (n, HEAD_PAD))[..., :QK_DIM].reshape(lead + (n * QK_DIM,))


def kernel(x, ffn1_norm, ffn1_w_in, ffn1_w_out, mix_norm, ffn2_norm, ffn2_w_in, ffn2_w_out, mla_w_down, mla_g_cq, mla_g_ckv, mla_w_uq, mla_w_ukv, mla_g_qn, mla_g_kn, mla_w_o, dil_w_qkv, dil_g_qn, dil_g_kn, dil_w_o, loss_target, m_ffn1_norm, m_ffn1_w_in, m_ffn1_w_out, m_mix_norm, m_ffn2_norm, m_ffn2_w_in, m_ffn2_w_out, m_mla_w_down, m_mla_g_cq, m_mla_g_ckv, m_mla_w_uq, m_mla_w_ukv, m_mla_g_qn, m_mla_g_kn, m_mla_w_o, m_dil_w_qkv, m_dil_g_qn, m_dil_g_kn, m_dil_w_o, v_ffn1_norm, v_ffn1_w_in, v_ffn1_w_out, v_mix_norm, v_ffn2_norm, v_ffn2_w_in, v_ffn2_w_out, v_mla_w_down, v_mla_g_cq, v_mla_g_ckv, v_mla_w_uq, v_mla_w_ukv, v_mla_g_qn, v_mla_g_kn, v_mla_w_o, v_dil_w_qkv, v_dil_g_qn, v_dil_g_kn, v_dil_w_o):
    names = ["ffn1_norm", "ffn1_w_in", "ffn1_w_out", "mix_norm", "ffn2_norm", "ffn2_w_in", "ffn2_w_out", "mla_w_down",
             "mla_g_cq", "mla_g_ckv", "mla_w_uq", "mla_w_ukv", "mla_g_qn", "mla_g_kn", "mla_w_o", "dil_w_qkv",
             "dil_g_qn", "dil_g_kn", "dil_w_o"]
    W = dict(zip(names, [ffn1_norm, ffn1_w_in, ffn1_w_out, mix_norm, ffn2_norm, ffn2_w_in, ffn2_w_out, mla_w_down,
                         mla_g_cq, mla_g_ckv, mla_w_uq, mla_w_ukv, mla_g_qn, mla_g_kn, mla_w_o, dil_w_qkv,
                         dil_g_qn, dil_g_kn, dil_w_o]))
    M1 = dict(zip(names, [m_ffn1_norm, m_ffn1_w_in, m_ffn1_w_out, m_mix_norm, m_ffn2_norm, m_ffn2_w_in, m_ffn2_w_out,
                          m_mla_w_down, m_mla_g_cq, m_mla_g_ckv, m_mla_w_uq, m_mla_w_ukv, m_mla_g_qn, m_mla_g_kn,
                          m_mla_w_o, m_dil_w_qkv, m_dil_g_qn, m_dil_g_kn, m_dil_w_o]))
    V2 = dict(zip(names, [v_ffn1_norm, v_ffn1_w_in, v_ffn1_w_out, v_mix_norm, v_ffn2_norm, v_ffn2_w_in, v_ffn2_w_out,
                          v_mla_w_down, v_mla_g_cq, v_mla_g_ckv, v_mla_w_uq, v_mla_w_ukv, v_mla_g_qn, v_mla_g_kn,
                          v_mla_w_o, v_dil_w_qkv, v_dil_g_qn, v_dil_g_kn, v_dil_w_o]))
    S, D = x.shape[1], x.shape[2]
    x0 = x.reshape(S, D)
    tgt = loss_target.reshape(S, D)

    big = ["ffn1_w_in", "ffn1_w_out", "ffn2_w_in", "ffn2_w_out", "mla_w_down", "mla_w_uq", "mla_w_ukv", "mla_w_o",
           "dil_w_qkv", "dil_w_o"]
    shard_dim = {"ffn1_w_in": 2, "ffn1_w_out": 1, "ffn2_w_in": 2, "ffn2_w_out": 1, "mla_w_down": 1, "mla_w_uq": 2,
                 "mla_w_ukv": 2, "mla_w_o": 1, "dil_w_qkv": 2, "dil_w_o": 2}
    paired = ("ffn1_w_in", "ffn2_w_in")
    shard_axis = {n: (d, n in paired) for n, d in shard_dim.items()}
    grad_axis = {n: (d - 1, n in paired) for n, d in shard_dim.items()}

    def padded(n, w):
        if n == "mla_w_down":
            return jnp.pad(w, ((0, 0), (0, 0), (0, LAT_PAD - w.shape[2])))
        if n == "mla_w_uq":
            return _pad_heads(w)
        return w

    depth = ffn1_norm.shape[0]
    blocks = []
    for l in range(depth):
        mixer = (["mla_w_down", "mla_w_uq", "mla_w_ukv", "mla_w_o"] if l % 2 == 0 else ["dil_w_qkv", "dil_w_o"])
        blocks.append((f"ffn1_{l}", [("ffn1_w_in", l), ("ffn1_w_out", l)]))
        blocks.append((f"mix_{l}", [(n, l // 2) for n in mixer]))
        blocks.append((f"ffn2_{l}", [("ffn2_w_in", l), ("ffn2_w_out", l)]))
    order = [k for _, keys in blocks for k in keys]
    me = (4 * lax.axis_index("x") + 2 * lax.axis_index("y") + lax.axis_index("c")).astype(jnp.int32).reshape(1)
    mine = [_cast_into_gathered(padded(n, W[n]), l, shard_axis[n], me, f"cast_{n}_{l}") for n, l in order]
    ag_items, ag_token = _xchg_start("gather", mine, [shard_axis[n] for n, _ in order], "gather_start")
    ag_items = dict(zip(order, ag_items))
    full = {}

    def fetch(keys, after, tag):
        lands = _xchg_wait("gather", [ag_items[k] for k in keys], [shard_axis[k[0]] for k in keys], after,
                           f"gather_wait_{tag}")
        full.update(zip(keys, lands))

    g_qn = _pad_heads(mla_g_qn)
    g_kn = _pad_heads(mla_g_kn)
    tabs = _rope_tables(S)
    slopes = jnp.asarray(_alibi_slopes(), F32)

    grads = {}
    gain_g = {}

    out_g, out_d, out_m, out_v = {}, {}, {}, {}
    pending = []
    lag = 3

    def scatter_start(tag, keys):
        items, token = _xchg_start("scatter", [grads[k] for k in keys], [grad_axis[k[0]] for k in keys],
                                   f"scatter_start_{tag}")
        pending.append((tag, keys, items))
        return token

    def scatter_finish(after):
        tag, keys, items = pending.pop(0)
        lands = _xchg_wait("scatter", items, [grad_axis[k[0]] for k in keys], after, f"scatter_wait_{tag}")
        tokens = []
        for (n, l), p in zip(keys, lands):
            ax, pair = grad_axis[n]
            size = grads[(n, l)].shape[ax] // N_DEV
            own = lax.dynamic_slice_in_dim(grads[(n, l)], _slot(me[0], pair) * size, size, axis=ax)
            if n == "mla_w_down":
                p, own = p[..., :W[n].shape[2]], own[..., :W[n].shape[2]]
            elif n == "mla_w_uq":
                p, own = _unpad_heads(p), _unpad_heads(own)
            prev = (out_g[n], out_d[n], out_m[n], out_v[n]) if n in out_g else None
            (out_g[n], out_d[n], out_m[n], out_v[n]), tok = _adamw(p, own, me, W[n], M1[n], V2[n], l, prev,
                                                                    f"adamw_{n}_{l}")
            tokens.append(tok)
        return tokens

    def finish_due(after):
        tokens = []
        while len(pending) > lag:
            tokens += scatter_finish(after)
        return tokens

    def ffn_fwd(xin, norm_row, which, l, tag, deps=()):
        k_in, k_out = (which + "_w_in", l), (which + "_w_out", l)
        h = _rms_fwd(xin, norm_row, f"rms_fwd_{tag}", deps=deps)
        fetch([k_in], h, f"in_{tag}")
        u, a = _ffn_in(h, full[k_in], f"ffn_in_{tag}")
        fetch([k_out], a, f"out_{tag}")
        xo = _mm(a, full[k_out], "nn", F32, f"mm_out_{tag}", scale=0.5, res=xin, layer=0)
        return xo, (xin, h, u, a)

    def ffn_bwd(dx_pair, saved, norm_row, which, l, tag):
        dxo, dxob = dx_pair
        k_in, k_out = (which + "_w_in", l), (which + "_w_out", l)
        xin, h, u, a = saved
        grads[k_out] = _mm(a, dxob, "tn", BF16, f"mm_dwout_{tag}", scale=0.5)
        t_out = scatter_start(f"{tag}_out", [k_out])
        du = _ffn_da(dxob, full[k_out], u, f"ffn_da_{tag}", deps=[t_out])
        grads[k_in] = _mm(h, du, "tn", BF16, f"mm_dwin_{tag}")
        t_in = scatter_start(f"{tag}_in", [k_in])
        dh = _mm(du, full[k_in], "nt", F32, f"mm_dh_{tag}", layer=0, deps=[t_in])
        toks = finish_due(dh)
        dx, dxb, dg = _rms_bwd(xin, norm_row, dh, dxo, f"rms_bwd_{tag}", deps=toks)
        gain_g.setdefault(which + "_norm", {})[l] = dg
        return dx, dxb

    def mla_fwd(xin, l):
        j = l // 2
        xn = _rms_fwd(xin, mix_norm[l:l + 1], "rms_fwd_mla")
        fetch([(n, j) for n in ("mla_w_down", "mla_w_uq", "mla_w_ukv", "mla_w_o")], xn, "mla")
        lat = _mm(xn, full[("mla_w_down", j)], "nn", F32, "mm_lat", layer=0)
        cq, ckv = _lat_norm_fwd(lat, mla_g_cq[j:j + 1], mla_g_ckv[j:j + 1], "lat_norm_fwd")
        q_raw = _mm(cq, full[("mla_w_uq", j)], "nn", F32, "mm_uq", layer=0)
        kv = _mm(ckv, full[("mla_w_ukv", j)], "nn", F32, "mm_ukv", layer=0)
        qf, kf, vb = _mla_prep_fwd(q_raw, kv, lat, g_qn[j:j + 1], g_kn[j:j + 1], tabs, "mla_prep_fwd")
        o, lse = _flash_fwd(qf, kf, vb, "flash_fwd")
        xo = _mm(o, full[("mla_w_o", j)], "nn", F32, "mm_mla_o", res=xin, layer=0)
        return xo, (xin, xn, lat, cq, ckv, q_raw, kv, qf, kf, vb, o, lse)

    def mla_bwd(dx_pair, saved, l):
        dxo, dxob = dx_pair
        j = l // 2
        xin, xn, lat, cq, ckv, q_raw, kv, qf, kf, vb, o, lse = saved
        do = _mm(dxob, full[("mla_w_o", j)], "nt", BF16, "mm_mla_do", layer=0)
        grads[("mla_w_o", j)] = _mm(o, dxob, "tn", BF16, "mm_mla_dwo")
        delta = _attn_delta(do, o, "attn_delta")
        dqf = _flash_bwd_dq(qf, kf, vb, do, lse, delta, "flash_bwd_dq")
        dkf, dv = _flash_bwd_dkv(qf, kf, vb, do, lse, delta, "flash_bwd_dkv")
        dq_raw, dkv, dkpe, dgq, dgk = _mla_prep_bwd(q_raw, kv, lat, g_qn[j:j + 1], g_kn[j:j + 1], tabs, dqf, dkf, dv,
                                                    "mla_prep_bwd")
        gain_g.setdefault("mla_g_qn", {})[j] = dgq
        gain_g.setdefault("mla_g_kn", {})[j] = dgk
        dcq = _mm(dq_raw, full[("mla_w_uq", j)], "nt", F32, "mm_dcq", layer=0)
        grads[("mla_w_uq", j)] = _mm(cq, dq_raw, "tn", BF16, "mm_dwuq")
        dckv = _mm(dkv, full[("mla_w_ukv", j)], "nt", F32, "mm_dckv", layer=0)
        grads[("mla_w_ukv", j)] = _mm(ckv, dkv, "tn", BF16, "mm_dwukv")
        dlat, dgcq, dgckv = _lat_norm_bwd(lat, mla_g_cq[j:j + 1], mla_g_ckv[j:j + 1], dcq, dckv, dkpe, "lat_norm_bwd")
        gain_g.setdefault("mla_g_cq", {})[j] = dgcq
        gain_g.setdefault("mla_g_ckv", {})[j] = dgckv
        dxn = _mm(dlat, full[("mla_w_down", j)], "nt", F32, "mm_dxn_mla", layer=0)
        grads[("mla_w_down", j)] = _mm(xn, dlat, "tn", BF16, "mm_dwdown")
        tok = scatter_start(f"mix_{l}", [(n, j) for n in ("mla_w_down", "mla_w_uq", "mla_w_ukv", "mla_w_o")])
        toks = finish_due(dxn)
        dx, dxb, dg = _rms_bwd(xin, mix_norm[l:l + 1], dxn, dxo, "rms_bwd_mla", deps=[tok] + toks)
        gain_g.setdefault("mix_norm", {})[l] = dg
        return dx, dxb

    def dil_fwd(xin, l):
        j = l // 2
        xn = _rms_fwd(xin, mix_norm[l:l + 1], "rms_fwd_dil")
        fetch([("dil_w_qkv", j), ("dil_w_o", j)], xn, "dil")
        qkv = _mm(xn, full[("dil_w_qkv", j)], "nn", F32, "mm_qkv", layer=0)
        o_g, lse_g = _dil_fwd(qkv, dil_g_qn[j:j + 1], dil_g_kn[j:j + 1], slopes, "dil_fwd")
        o, lse = _dil_merge(o_g, lse_g, "dil_merge")
        xo = _mm(o, full[("dil_w_o", j)], "nn", F32, "mm_dil_o", res=xin, layer=0)
        return xo, (xin, xn, qkv, o, lse)

    def dil_bwd(dx_pair, saved, l):
        dxo, dxob = dx_pair
        j = l // 2
        xin, xn, qkv, o, lse = saved
        do = _mm(dxob, full[("dil_w_o", j)], "nt", F32, "mm_dil_do", layer=0)
        grads[("dil_w_o", j)] = _mm(o, dxob, "tn", BF16, "mm_dil_dwo")
        delta = _attn_delta(do, o, "dil_delta")
        dq, dk, dv, dgq, dgk = _dil_bwd(qkv, dil_g_qn[j:j + 1], dil_g_kn[j:j + 1], slopes, do, delta, lse, "dil_bwd")
        gain_g.setdefault("dil_g_qn", {})[j] = dgq
        gain_g.setdefault("dil_g_kn", {})[j] = dgk
        dqkv = jnp.concatenate([dq, dk, dv], axis=1)
        dxn = _mm(dqkv, full[("dil_w_qkv", j)], "nt", F32, "mm_dxn_dil", layer=0)
        grads[("dil_w_qkv", j)] = _mm(xn, dqkv, "tn", BF16, "mm_dwqkv")
        tok = scatter_start(f"mix_{l}", [("dil_w_qkv", j), ("dil_w_o", j)])
        toks = finish_due(dxn)
        dx, dxb, dg = _rms_bwd(xin, mix_norm[l:l + 1], dxn, dxo, "rms_bwd_dil", deps=[tok] + toks)
        gain_g.setdefault("mix_norm", {})[l] = dg
        return dx, dxb

    saved = []
    xc = x0
    for l in range(depth):
        xc, s1 = ffn_fwd(xc, ffn1_norm[l:l + 1], "ffn1", l, f"ffn1_{l}", deps=[ag_token] if l == 0 else ())
        xc, s2 = (mla_fwd if l % 2 == 0 else dil_fwd)(xc, l)
        xc, s3 = ffn_fwd(xc, ffn2_norm[l:l + 1], "ffn2", l, f"ffn2_{l}")
        saved.append((s1, s2, s3))

    dy, dyb, loss_part = _loss_head(xc, tgt, "loss_head")
    dx = (dy, dyb)
    loss = lax.psum(loss_part[0, 0], MESH_AXES)

    for bi in reversed(range(len(blocks))):
        tag, _ = blocks[bi]
        l = bi // 3
        s = saved[l][bi % 3]
        if bi % 3 == 2:
            dx = ffn_bwd(dx, s, ffn2_norm[l:l + 1], "ffn2", l, tag)
        elif bi % 3 == 1:
            dx = (mla_bwd if l % 2 == 0 else dil_bwd)(dx, s, l)
        else:
            dx = ffn_bwd(dx, s, ffn1_norm[l:l + 1], "ffn1", l, tag)
    grad_x = dx[0].reshape(x.shape)
    after = dx[1]
    while pending:
        after = scatter_finish(after)[-1]

    small = [n for n in names if n not in big]

    def gain_local(n):
        rows = [gain_g[n][l] for l in range(W[n].shape[0])]
        g = jnp.concatenate(rows, axis=1)
        return g

    def flat_pad(n, a):
        a = a.reshape(1, -1)
        if n in ("mla_g_qn", "mla_g_kn"):
            a = _pad_heads(a)
        return a

    packed_g = jnp.concatenate([gain_local(n) for n in small], axis=1)
    sizes = [gain_local(n).shape[1] for n in small]
    tot_g = _gain_allreduce(packed_g, "gain_allreduce")
    pw = jnp.concatenate([flat_pad(n, W[n]) for n in small], axis=1)
    pm = jnp.concatenate([flat_pad(n, M1[n]) for n in small], axis=1)
    pv = jnp.concatenate([flat_pad(n, V2[n]) for n in small], axis=1)
    res, _ = _adamw(tot_g.reshape(1, 1, -1), None, me, pw.reshape(1, 1, -1), pm.reshape(1, 1, -1),
                    pv.reshape(1, 1, -1), 0, None, "adamw_gains")
    res = [r.reshape(1, -1) for r in res]
    off = 0
    for n, sz in zip(small, sizes):
        for dst, r in zip((out_g, out_d, out_m, out_v), res):
            piece = r[:, off:off + sz]
            if n in ("mla_g_qn", "mla_g_kn"):
                piece = _unpad_heads(piece)
            dst[n] = piece.reshape(W[n].shape)
        off += sz

    return (loss, grad_x, *[out_g[n] for n in names], *[out_d[n] for n in names],
            *[out_m[n] for n in names], *[out_v[n] for n in names])
```

```python
import functools
import math

import jax
import jax.numpy as jnp
import numpy as np
from jax import lax
from jax.experimental import pallas as pl
from jax.experimental.pallas import tpu as pltpu

EPS = 1e-6
MLA_HEADS = 16
Q_LORA = 512
KV_LORA = 512
NOPE_DIM = 128
ROPE_DIM = 64
V_DIM = 128
QK_DIM = NOPE_DIM + ROPE_DIM
ROPE_THETA = 10000.0
HEAD_PAD = 256
LAT_PAD = Q_LORA + KV_LORA + 128
DIL_PAIRS = ((128, 1), (512, 4), (2048, 16))
DIL_GROUPS = 3
DIL_HEADS = 8
DIL_HEAD_DIM = 128
DIL_BLK = 128
DIL_UNROLL = 4
ADAM_LR = 0.001
ADAM_B1 = 0.9
ADAM_B2 = 0.999
ADAM_EPS = 1e-08
ADAM_WD = 0.01
ADAM_STEP = 10

N_DEV = 8
MESH_AXES = ("x", "y", "c")
MESH = pl.DeviceIdType.MESH
NEG_BIG = -1e30
VMEM_LIMIT_V7X = 56 * 1024 * 1024
LANES = 128

BF16 = jnp.bfloat16
F32 = jnp.float32


def _pick(n, cands):
    for c in cands:
        if n % c == 0:
            return c
    raise ValueError(f"no tile for {n}")


def _params(sem):
    return pltpu.CompilerParams(dimension_semantics=sem, vmem_limit_bytes=VMEM_LIMIT_V7X)


ANY_SPEC = pl.BlockSpec(memory_space=pl.ANY)


MM_VMEM_BUDGET = 44 * 1024 * 1024
MM_HBM_BYTES_PER_S = 1.8e12
MM_MXU_FLOPS_PER_S = 8.5e14
MM_STEP_S = 0.4e-6
MM_MAX_TILE_MACS = 3.3e9
MXU_DIM = 256


@functools.lru_cache(maxsize=None)
def _mm_tiles(M, K, N, a_bytes, b_bytes, out_bytes, has_res):
    best = None
    for tk in [K] + [c for c in (1408, 1024, 512, 384, 256, 128) if K % c == 0 and c < K]:
        nk = K // tk
        for tm in [c for c in (2048, 1024, 512, 256, 128) if M % c == 0]:
            for tn in [c for c in (2816, 2048, 1408, 1152, 1024, 512, 384, 256, 128) if N % c == 0]:
                if tm * tk * tn > MM_MAX_TILE_MACS:
                    continue
                fill = (tn / (-(-tn // MXU_DIM) * MXU_DIM)) * (tk / (-(-tk // MXU_DIM) * MXU_DIM))
                fill *= tm / (tm + MXU_DIM // 2)
                vmem = 2 * (tm * tk * a_bytes + tk * tn * b_bytes) + 2 * tm * tn * out_bytes + tm * tn * 4
                vmem += (tm * tk + tk * tn) * 2 if max(a_bytes, b_bytes) > 2 else 0
                vmem += 2 * tm * tn * 4 if has_res else 0
                if vmem > MM_VMEM_BUDGET:
                    continue
                a_all, b_all = M * K * a_bytes, K * N * b_bytes
                if nk == 1:
                    t_i = a_all + (M // tm) * b_all
                    t_j = b_all + (N // tn) * a_all
                    traffic, i_outer = min((t_i, True), (t_j, False))
                else:
                    traffic, i_outer = (N // tn) * a_all + (M // tm) * b_all, True
                traffic += M * N * (out_bytes + (4 if has_res else 0))
                mxu = 2.0 * M * K * N / (MM_MXU_FLOPS_PER_S * fill) * (1.15 if nk > 1 else 1.0)
                cost = max(traffic / MM_HBM_BYTES_PER_S, mxu) + (M // tm) * (N // tn) * nk * MM_STEP_S
                if best is None or cost < best[0]:
                    best = (cost, tm, tn, tk, i_outer)
    assert best is not None, (M, K, N)
    return best[1:]


def _mm(a, b, mode, out_dtype, name, *, scale=1.0, res=None, layer=None, deps=()):
    b2 = b.shape[-2:]
    if mode == "nn":
        (M, K), (Kb, N) = a.shape, b2
    elif mode == "nt":
        (M, K), (N, Kb) = a.shape, b2
    else:
        (K, M), (Kb, N) = a.shape, b2
    assert K == Kb, (a.shape, b.shape, mode)
    tm, tn, tk, i_outer = _mm_tiles(M, K, N, a.dtype.itemsize, b.dtype.itemsize, jnp.dtype(out_dtype).itemsize,
                                    res is not None)
    nk = K // tk
    dims = {"nn": (((1,), (0,)), ((), ())), "nt": (((1,), (1,)), ((), ())), "tn": (((0,), (0,)), ((), ()))}[mode]

    def finish(v, r_ref, o_ref):
        if scale != 1.0:
            v = v * scale
        if r_ref is not None:
            v = r_ref[...] + v
        o_ref[...] = v.astype(o_ref.dtype)

    def body(*refs):
        a_ref, b_ref = refs[:2]
        r_ref = refs[2] if res is not None else None
        prod = lambda: lax.dot_general(a_ref[...].astype(BF16), b_ref[...].astype(BF16), dims,
                                       preferred_element_type=F32)
        if nk == 1:
            finish(prod(), r_ref, refs[-1])
            return
        o_ref, acc = refs[-2:]
        k = pl.program_id(2)

        @pl.when(k == 0)
        def _():
            acc[...] = prod()

        @pl.when(k > 0)
        def _():
            acc[...] += prod()

        @pl.when(k == nk - 1)
        def _():
            finish(acc[...], r_ref, o_ref)

    ij = (lambda p, q: (p, q)) if i_outer else (lambda p, q: (q, p))

    def spec(shape, f, lead=None):
        full = lambda p, q, k: f(*ij(p, q), k)
        if lead is None:
            return pl.BlockSpec(shape, full)
        return pl.BlockSpec((None,) + shape, lambda p, q, k: (lead,) + full(p, q, k))

    a_spec = spec((tk, tm), lambda i, j, k: (k, i)) if mode == "tn" else spec((tm, tk), lambda i, j, k: (i, k))
    lead = layer if b.ndim == 3 else None
    b_spec = spec((tn, tk), lambda i, j, k: (j, k), lead) if mode == "nt" else spec((tk, tn), lambda i, j, k: (k, j), lead)
    in_specs = [a_spec, b_spec]
    args = [a, b]
    if res is not None:
        in_specs.append(spec((tm, tn), lambda i, j, k: (i, j)))
        args.append(res)
    in_specs += [ANY_SPEC] * len(deps)
    args += list(deps)
    outer, inner = (M // tm, N // tn) if i_outer else (N // tn, M // tm)
    return pl.pallas_call(
        body, name=name, grid=(outer, inner, nk),
        in_specs=in_specs, out_specs=spec((tm, tn), lambda i, j, k: (i, j)),
        out_shape=jax.ShapeDtypeStruct((M, N), out_dtype),
        scratch_shapes=[pltpu.VMEM((tm, tn), F32)] if nk > 1 else [],
        compiler_params=_params(("parallel", "parallel", "arbitrary")),
    )(*args)


def _cast_into_gathered(w, layer, axis, me, name):
    _, R, C = w.shape
    tr = _pick(R, (512, 256, 128, 64, 32, 16))
    nr = R // tr
    axis, paired = axis

    def body(me_ref, w_ref, o_ref):
        o_ref[...] = w_ref[...].astype(BF16)

    if axis == 1:
        out_idx = lambda i, me_ref: (0, _slot(me_ref[0], paired) * nr + i, 0)
        shape = (1, R * N_DEV, C)
    else:
        out_idx = lambda i, me_ref: (0, i, _slot(me_ref[0], paired))
        shape = (1, R, C * N_DEV)
    return pl.pallas_call(
        body, name=name,
        grid_spec=pltpu.PrefetchScalarGridSpec(
            num_scalar_prefetch=1, grid=(nr,),
            in_specs=[pl.BlockSpec((None, tr, C), lambda i, me_ref: (layer, i, 0))],
            out_specs=pl.BlockSpec((None, tr, C), out_idx)),
        out_shape=jax.ShapeDtypeStruct(shape, BF16), compiler_params=_params(("parallel",)),
    )(me, w)


def _rms_fwd(x, g, name, deps=()):
    T, D = x.shape
    tr = _pick(T, (512, 256, 128))

    def body(x_ref, g_ref, *rest):
        o_ref = rest[-1]
        xv = x_ref[...]
        r = lax.rsqrt(jnp.mean(xv * xv, axis=-1, keepdims=True) + EPS)
        o_ref[...] = ((xv * r) * g_ref[...]).astype(BF16)

    return pl.pallas_call(
        body, name=name, grid=(T // tr,),
        in_specs=[pl.BlockSpec((tr, D), lambda i: (i, 0)), pl.BlockSpec((1, D), lambda i: (0, 0))]
        + [ANY_SPEC] * len(deps),
        out_specs=pl.BlockSpec((tr, D), lambda i: (i, 0)),
        out_shape=jax.ShapeDtypeStruct((T, D), BF16), compiler_params=_params(("parallel",)),
    )(x, g, *deps)


def _rms_bwd(x, g, dh, dres, name, deps=()):
    T, D = x.shape
    tr = _pick(T, (256, 128))

    def body(x_ref, g_ref, dh_ref, dres_ref, *rest):
        dx_ref, dxb_ref, dg_ref = rest[-3:]
        xv = x_ref[...]
        dhv = dh_ref[...]
        r = lax.rsqrt(jnp.mean(xv * xv, axis=-1, keepdims=True) + EPS)
        xhat = xv * r
        dxh = dhv * g_ref[...]
        c = jnp.mean(dxh * xhat, axis=-1, keepdims=True)
        dx = dres_ref[...] + r * (dxh - xhat * c)
        dx_ref[...] = dx
        dxb_ref[...] = dx.astype(BF16)

        @pl.when(pl.program_id(0) == 0)
        def _():
            dg_ref[...] = jnp.zeros_like(dg_ref)

        dg_ref[...] += jnp.sum(dhv * xhat, axis=0, keepdims=True)

    row = pl.BlockSpec((tr, D), lambda i: (i, 0))
    vec = pl.BlockSpec((1, D), lambda i: (0, 0))
    return pl.pallas_call(
        body, name=name, grid=(T // tr,),
        in_specs=[row, vec, row, row] + [ANY_SPEC] * len(deps), out_specs=[row, row, vec],
        out_shape=[jax.ShapeDtypeStruct((T, D), F32), jax.ShapeDtypeStruct((T, D), BF16),
                   jax.ShapeDtypeStruct((1, D), F32)],
        compiler_params=_params(("arbitrary",)),
    )(x, g, dh, dres, *deps)


N_PANEL = N_DEV // 2


def _ffn_in(h, w_in, name):
    T, D = h.shape
    F2 = w_in.shape[2]
    pw = F2 // N_PANEL
    half = pw // 2
    tm = _pick(T, (512, 256, 128))

    def body(h_ref, w_ref, u_ref, a_ref):
        r = jnp.dot(h_ref[...], w_ref[...], preferred_element_type=F32)
        u_ref[...] = r.astype(BF16)
        g, up = r[:, :half], r[:, half:]
        a_ref[...] = (g * jax.nn.sigmoid(g) * up).astype(BF16)

    return pl.pallas_call(
        body, name=name, grid=(N_PANEL, T // tm),
        in_specs=[pl.BlockSpec((tm, D), lambda p, i: (i, 0)), pl.BlockSpec((None, D, pw), lambda p, i: (0, 0, p))],
        out_specs=[pl.BlockSpec((tm, pw), lambda p, i: (i, p)), pl.BlockSpec((tm, half), lambda p, i: (i, p))],
        out_shape=[jax.ShapeDtypeStruct((T, F2), BF16), jax.ShapeDtypeStruct((T, F2 // 2), BF16)],
        compiler_params=_params(("parallel", "parallel")),
    )(h, w_in)


def _ffn_da(dxo, w_out, u, name, deps=()):
    T, D = dxo.shape
    F2 = u.shape[1]
    pw = F2 // N_PANEL
    half = pw // 2
    tm = _pick(T, (512, 256, 128))

    def body(d_ref, w_ref, u_ref, *rest):
        du_ref = rest[-1]
        da = 0.5 * lax.dot_general(d_ref[...], w_ref[...], NT_DIMS, preferred_element_type=F32)
        g = u_ref[:, :half].astype(F32)
        up = u_ref[:, half:].astype(F32)
        sg = jax.nn.sigmoid(g)
        silu = g * sg
        du_ref[:, :half] = (da * up * (sg + silu * (1.0 - sg))).astype(BF16)
        du_ref[:, half:] = (da * silu).astype(BF16)

    return pl.pallas_call(
        body, name=name, grid=(N_PANEL, T // tm),
        in_specs=[pl.BlockSpec((tm, D), lambda p, i: (i, 0)), pl.BlockSpec((None, half, D), lambda p, i: (0, p, 0)),
                  pl.BlockSpec((tm, pw), lambda p, i: (i, p))] + [ANY_SPEC] * len(deps),
        out_specs=pl.BlockSpec((tm, pw), lambda p, i: (i, p)),
        out_shape=jax.ShapeDtypeStruct((T, F2), BF16), compiler_params=_params(("parallel", "parallel")),
    )(dxo, w_out, u, *deps)


def _loss_head(y, t, name):
    T, D = y.shape
    tr = _pick(T, (512, 256, 128))

    def body(y_ref, t_ref, dy_ref, dyb_ref, l_ref):
        e = y_ref[...] - t_ref[...]
        dy = e * (1.0 / D)
        dy_ref[...] = dy
        dyb_ref[...] = dy.astype(BF16)

        @pl.when(pl.program_id(0) == 0)
        def _():
            l_ref[...] = jnp.zeros_like(l_ref)

        l_ref[...] += 0.5 * jnp.sum(jnp.mean(e * e, axis=-1, keepdims=True), axis=0, keepdims=True)

    row = pl.BlockSpec((tr, D), lambda i: (i, 0))
    return pl.pallas_call(
        body, name=name, grid=(T // tr,),
        in_specs=[row, row], out_specs=[row, row, pl.BlockSpec((1, 1), lambda i: (0, 0))],
        out_shape=[jax.ShapeDtypeStruct((T, D), F32), jax.ShapeDtypeStruct((T, D), BF16),
                   jax.ShapeDtypeStruct((1, 1), F32)],
        compiler_params=_params(("arbitrary",)),
    )(y, t)


def _rope_tables(S):
    half = ROPE_DIM // 2
    inv = 1.0 / (ROPE_THETA ** (jnp.arange(0, ROPE_DIM, 2, dtype=F32) / ROPE_DIM))
    ang = jnp.arange(S, dtype=F32)[:, None] * inv[None, :]
    cos, sin = jnp.cos(ang), jnp.sin(ang)
    z = jnp.zeros((S, half), F32)
    z2 = jnp.zeros((S, LANES - ROPE_DIM), F32)
    c = jnp.concatenate([cos, cos, z2], axis=1)
    s1 = jnp.concatenate([-sin, z, z2], axis=1)
    s2 = jnp.concatenate([z, sin, z2], axis=1)
    return c, s1, s2


def _rope(r, c, s1, s2):
    return r * c + pltpu.roll(r, LANES - ROPE_DIM // 2, 1) * s1 + pltpu.roll(r, ROPE_DIM // 2, 1) * s2


def _rope_t(d, c, s1, s2):
    return d * c + pltpu.roll(d * s1, ROPE_DIM // 2, 1) + pltpu.roll(d * s2, LANES - ROPE_DIM // 2, 1)


def _lat_norm_fwd(lat, g_cq, g_ckv, name):
    T = lat.shape[0]
    tr = _pick(T, (512, 256, 128))

    def body(lat_ref, gq_ref, gk_ref, cq_ref, ckv_ref):
        for off, g_ref, o_ref in ((0, gq_ref, cq_ref), (Q_LORA, gk_ref, ckv_ref)):
            xv = lat_ref[:, off:off + Q_LORA]
            r = lax.rsqrt(jnp.mean(xv * xv, axis=-1, keepdims=True) + EPS)
            o_ref[...] = ((xv * r) * g_ref[...]).astype(BF16)

    vec = pl.BlockSpec((1, Q_LORA), lambda i: (0, 0))
    out = pl.BlockSpec((tr, Q_LORA), lambda i: (i, 0))
    return pl.pallas_call(
        body, name=name, grid=(T // tr,),
        in_specs=[pl.BlockSpec((tr, LAT_PAD), lambda i: (i, 0)), vec, vec], out_specs=[out, out],
        out_shape=[jax.ShapeDtypeStruct((T, Q_LORA), BF16)] * 2, compiler_params=_params(("parallel",)),
    )(lat, g_cq, g_ckv)


def _lat_norm_bwd(lat, g_cq, g_ckv, dcq, dckv, dkpe, name):
    T = lat.shape[0]
    tr = _pick(T, (256, 128))

    def body(lat_ref, gq_ref, gk_ref, dcq_ref, dckv_ref, dkpe_ref, dlat_ref, dgq_ref, dgk_ref):
        @pl.when(pl.program_id(0) == 0)
        def _():
            dgq_ref[...] = jnp.zeros_like(dgq_ref)
            dgk_ref[...] = jnp.zeros_like(dgk_ref)

        for off, g_ref, d_ref, dg_ref in ((0, gq_ref, dcq_ref, dgq_ref), (Q_LORA, gk_ref, dckv_ref, dgk_ref)):
            xv = lat_ref[:, off:off + Q_LORA]
            dv = d_ref[...]
            r = lax.rsqrt(jnp.mean(xv * xv, axis=-1, keepdims=True) + EPS)
            xhat = xv * r
            dxh = dv * g_ref[...]
            c = jnp.mean(dxh * xhat, axis=-1, keepdims=True)
            dlat_ref[:, off:off + Q_LORA] = (r * (dxh - xhat * c)).astype(BF16)
            dg_ref[...] += jnp.sum(dv * xhat, axis=0, keepdims=True)
        dlat_ref[:, Q_LORA + KV_LORA:] = dkpe_ref[...].astype(BF16)

    vec = pl.BlockSpec((1, Q_LORA), lambda i: (0, 0))
    half = pl.BlockSpec((tr, Q_LORA), lambda i: (i, 0))
    full = pl.BlockSpec((tr, LAT_PAD), lambda i: (i, 0))
    return pl.pallas_call(
        body, name=name, grid=(T // tr,),
        in_specs=[full, vec, vec, half, half, pl.BlockSpec((tr, LANES), lambda i: (i, 0))],
        out_specs=[full, vec, vec],
        out_shape=[jax.ShapeDtypeStruct((T, LAT_PAD), BF16), jax.ShapeDtypeStruct((1, Q_LORA), F32),
                   jax.ShapeDtypeStruct((1, Q_LORA), F32)],
        compiler_params=_params(("arbitrary",)),
    )(lat, g_cq, g_ckv, dcq, dckv, dkpe)


def _mla_prep_fwd(q_raw, kv, lat, g_qn, g_kn, tabs, name):
    T = q_raw.shape[0]
    H = MLA_HEADS
    tr = _pick(T, (256, 128))
    scale = 1.0 / math.sqrt(QK_DIM)

    def body(q_ref, kv_ref, kpe_ref, gq_ref, gk_ref, c_ref, s1_ref, s2_ref, qf_ref, kf_ref, v_ref):
        c, s1, s2 = c_ref[...], s1_ref[...], s2_ref[...]
        gq, gk = gq_ref[...], gk_ref[...]
        kpe = kpe_ref[...]
        kpe_ss = jnp.sum(kpe * kpe, axis=-1, keepdims=True)
        for h in range(H):
            lo = h * HEAD_PAD
            qa = q_ref[:, lo:lo + LANES]
            qb = q_ref[:, lo + LANES:lo + HEAD_PAD]
            ss = jnp.sum(qa * qa, axis=-1, keepdims=True) + jnp.sum(qb * qb, axis=-1, keepdims=True)
            r = lax.rsqrt(ss * (1.0 / QK_DIM) + EPS)
            qf_ref[:, lo:lo + LANES] = (qa * r * gq[:, :LANES] * scale).astype(BF16)
            qf_ref[:, lo + LANES:lo + HEAD_PAD] = (_rope(qb * r * gq[:, LANES:], c, s1, s2) * scale).astype(BF16)
            ka = kv_ref[:, lo:lo + LANES]
            ss = jnp.sum(ka * ka, axis=-1, keepdims=True) + kpe_ss
            r = lax.rsqrt(ss * (1.0 / QK_DIM) + EPS)
            kf_ref[:, lo:lo + LANES] = (ka * r * gk[:, :LANES]).astype(BF16)
            kf_ref[:, lo + LANES:lo + HEAD_PAD] = _rope(kpe * r * gk[:, LANES:], c, s1, s2).astype(BF16)
            v_ref[:, h * V_DIM:(h + 1) * V_DIM] = kv_ref[:, lo + LANES:lo + HEAD_PAD].astype(BF16)

    wide = pl.BlockSpec((tr, H * HEAD_PAD), lambda i: (i, 0))
    lane = pl.BlockSpec((tr, LANES), lambda i: (i, 0))
    gvec = pl.BlockSpec((1, HEAD_PAD), lambda i: (0, 0))
    return pl.pallas_call(
        body, name=name, grid=(T // tr,),
        in_specs=[wide, wide, pl.BlockSpec((tr, LANES), lambda i: (i, (Q_LORA + KV_LORA) // LANES)), gvec, gvec,
                  lane, lane, lane],
        out_specs=[wide, wide, pl.BlockSpec((tr, H * V_DIM), lambda i: (i, 0))],
        out_shape=[jax.ShapeDtypeStruct((T, H * HEAD_PAD), BF16), jax.ShapeDtypeStruct((T, H * HEAD_PAD), BF16),
                   jax.ShapeDtypeStruct((T, H * V_DIM), BF16)],
        compiler_params=_params(("parallel",)),
    )(q_raw, kv, lat, g_qn, g_kn, *tabs)


def _mla_prep_bwd(q_raw, kv, lat, g_qn, g_kn, tabs, dqf, dkf, dv, name):
    T = q_raw.shape[0]
    H = MLA_HEADS
    tr = _pick(T, (128,))

    def body(q_ref, kv_ref, kpe_ref, gq_ref, gk_ref, c_ref, s1_ref, s2_ref, dqf_ref, dkf_ref, dv_ref,
             dq_ref, dkv_ref, dkpe_ref, dgq_ref, dgk_ref):
        @pl.when(pl.program_id(0) == 0)
        def _():
            dgq_ref[...] = jnp.zeros_like(dgq_ref)
            dgk_ref[...] = jnp.zeros_like(dgk_ref)

        c, s1, s2 = c_ref[...], s1_ref[...], s2_ref[...]
        gq, gk = gq_ref[...], gk_ref[...]
        kpe = kpe_ref[...]
        kpe_ss = jnp.sum(kpe * kpe, axis=-1, keepdims=True)
        dkpe = jnp.zeros_like(kpe)
        dgq_a = jnp.zeros((1, LANES), F32)
        dgq_b = jnp.zeros((1, LANES), F32)
        dgk_a = jnp.zeros((1, LANES), F32)
        dgk_b = jnp.zeros((1, LANES), F32)
        for h in range(H):
            lo = h * HEAD_PAD
            xa = q_ref[:, lo:lo + LANES]
            xb = q_ref[:, lo + LANES:lo + HEAD_PAD]
            ss = jnp.sum(xa * xa, axis=-1, keepdims=True) + jnp.sum(xb * xb, axis=-1, keepdims=True)
            r = lax.rsqrt(ss * (1.0 / QK_DIM) + EPS)
            xa, xb = xa * r, xb * r
            da = dqf_ref[:, lo:lo + LANES]
            db = _rope_t(dqf_ref[:, lo + LANES:lo + HEAD_PAD], c, s1, s2)
            dgq_a += jnp.sum(da * xa, axis=0, keepdims=True)
            dgq_b += jnp.sum(db * xb, axis=0, keepdims=True)
            da, db = da * gq[:, :LANES], db * gq[:, LANES:]
            cc = (jnp.sum(da * xa, axis=-1, keepdims=True) + jnp.sum(db * xb, axis=-1, keepdims=True)) * (1.0 / QK_DIM)
            dq_ref[:, lo:lo + LANES] = (r * (da - xa * cc)).astype(BF16)
            dq_ref[:, lo + LANES:lo + HEAD_PAD] = (r * (db - xb * cc)).astype(BF16)
            xa = kv_ref[:, lo:lo + LANES]
            ss = jnp.sum(xa * xa, axis=-1, keepdims=True) + kpe_ss
            r = lax.rsqrt(ss * (1.0 / QK_DIM) + EPS)
            xa, xb = xa * r, kpe * r
            da = dkf_ref[:, lo:lo + LANES]
            db = _rope_t(dkf_ref[:, lo + LANES:lo + HEAD_PAD], c, s1, s2)
            dgk_a += jnp.sum(da * xa, axis=0, keepdims=True)
            dgk_b += jnp.sum(db * xb, axis=0, keepdims=True)
            da, db = da * gk[:, :LANES], db * gk[:, LANES:]
            cc = (jnp.sum(da * xa, axis=-1, keepdims=True) + jnp.sum(db * xb, axis=-1, keepdims=True)) * (1.0 / QK_DIM)
            dkv_ref[:, lo:lo + LANES] = (r * (da - xa * cc)).astype(BF16)
            dkpe = dkpe + r * (db - xb * cc)
            dkv_ref[:, lo + LANES:lo + HEAD_PAD] = dv_ref[:, h * V_DIM:(h + 1) * V_DIM].astype(BF16)
        dkpe_ref[...] = dkpe
        dgq_ref[:, :LANES] += dgq_a
        dgq_ref[:, LANES:] += dgq_b
        dgk_ref[:, :LANES] += dgk_a
        dgk_ref[:, LANES:] += dgk_b

    wide = pl.BlockSpec((tr, H * HEAD_PAD), lambda i: (i, 0))
    lane = pl.BlockSpec((tr, LANES), lambda i: (i, 0))
    gvec = pl.BlockSpec((1, HEAD_PAD), lambda i: (0, 0))
    vspec = pl.BlockSpec((tr, H * V_DIM), lambda i: (i, 0))
    return pl.pallas_call(
        body, name=name, grid=(T // tr,),
        in_specs=[wide, wide, pl.BlockSpec((tr, LANES), lambda i: (i, (Q_LORA + KV_LORA) // LANES)), gvec, gvec,
                  lane, lane, lane, wide, wide, vspec],
        out_specs=[wide, wide, lane, gvec, gvec],
        out_shape=[jax.ShapeDtypeStruct((T, H * HEAD_PAD), BF16), jax.ShapeDtypeStruct((T, H * HEAD_PAD), BF16),
                   jax.ShapeDtypeStruct((T, LANES), F32), jax.ShapeDtypeStruct((1, HEAD_PAD), F32),
                   jax.ShapeDtypeStruct((1, HEAD_PAD), F32)],
        compiler_params=_params(("arbitrary",)),
    )(q_raw, kv, lat, g_qn, g_kn, *tabs, dqf, dkf, dv)


def _causal_mask(tq, tk):
    return lax.broadcasted_iota(jnp.int32, (tq, tk), 1) <= lax.broadcasted_iota(jnp.int32, (tq, tk), 0)


NT_DIMS = (((1,), (1,)), ((), ()))
TN_DIMS = (((0,), (0,)), ((), ()))


def _flash_fwd(qf, kf, v, name):
    T = qf.shape[0]
    H = MLA_HEADS
    t = _pick(T, (512, 256, 128))
    n = T // t
    pairs = [(i, j) for i in range(n) for j in range(i + 1)]
    qi = jnp.asarray([p[0] for p in pairs], jnp.int32)
    kj = jnp.asarray([p[1] for p in pairs], jnp.int32)

    def body(qi_ref, kj_ref, q_ref, k_ref, v_ref, o_ref, lse_ref, m_sc, l_sc, acc_sc):
        sid = pl.program_id(1)
        i, j = qi_ref[sid], kj_ref[sid]

        @pl.when(j == 0)
        def _():
            m_sc[...] = jnp.full_like(m_sc, NEG_BIG)
            l_sc[...] = jnp.zeros_like(l_sc)
            acc_sc[...] = jnp.zeros_like(acc_sc)

        def step(masked):
            s = lax.dot_general(q_ref[...], k_ref[...], NT_DIMS, preferred_element_type=F32)
            if masked:
                s = jnp.where(_causal_mask(t, t), s, NEG_BIG)
            m_prev = m_sc[:, :1]
            m_new = jnp.maximum(m_prev, jnp.max(s, axis=-1, keepdims=True))
            a = jnp.exp(m_prev - m_new)
            p = jnp.exp(s - m_new)
            l_sc[...] = a * l_sc[...] + jnp.sum(p, axis=-1, keepdims=True)
            acc_sc[...] = a * acc_sc[...] + jnp.dot(p.astype(BF16), v_ref[...], preferred_element_type=F32)
            m_sc[...] = jnp.broadcast_to(m_new, m_sc.shape)

        @pl.when(j < i)
        def _():
            step(False)

        @pl.when(j == i)
        def _():
            step(True)
            o_ref[...] = (acc_sc[...] / l_sc[...]).astype(BF16)
            lse_ref[...] = m_sc[...] + jnp.log(l_sc[...])

    row = pl.BlockSpec((t, V_DIM), lambda h, s, qi, kj: (qi[s], h))
    return pl.pallas_call(
        body, name=name,
        grid_spec=pltpu.PrefetchScalarGridSpec(
            num_scalar_prefetch=2, grid=(H, len(pairs)),
            in_specs=[pl.BlockSpec((t, HEAD_PAD), lambda h, s, qi, kj: (qi[s], h)),
                      pl.BlockSpec((t, HEAD_PAD), lambda h, s, qi, kj: (kj[s], h)),
                      pl.BlockSpec((t, V_DIM), lambda h, s, qi, kj: (kj[s], h))],
            out_specs=[row, row],
            scratch_shapes=[pltpu.VMEM((t, LANES), F32), pltpu.VMEM((t, LANES), F32), pltpu.VMEM((t, V_DIM), F32)]),
        out_shape=[jax.ShapeDtypeStruct((T, H * V_DIM), BF16), jax.ShapeDtypeStruct((T, H * V_DIM), F32)],
        compiler_params=_params(("parallel", "arbitrary")),
    )(qi, kj, qf, kf, v)


def _attn_delta(do, o, name):
    T, W = do.shape
    nh = W // V_DIM
    tr = _pick(T, (512, 256, 128))

    def body(do_ref, o_ref, d_ref):
        for h in range(nh):
            sl = slice(h * V_DIM, (h + 1) * V_DIM)
            d = jnp.sum(do_ref[:, sl].astype(F32) * o_ref[:, sl].astype(F32), axis=-1, keepdims=True)
            d_ref[:, sl] = jnp.broadcast_to(d, (tr, V_DIM))

    row = pl.BlockSpec((tr, W), lambda i: (i, 0))
    return pl.pallas_call(
        body, name=name, grid=(T // tr,), in_specs=[row, row], out_specs=row,
        out_shape=jax.ShapeDtypeStruct((T, W), F32), compiler_params=_params(("parallel",)),
    )(do, o)


def _flash_bwd(qf, kf, v, do, lse, delta, name):
    T = qf.shape[0]
    H = MLA_HEADS
    t = _pick(T, (512, 256, 128))
    n = T // t
    scale = 1.0 / math.sqrt(QK_DIM)
    pairs = [(i, j) for j in range(n) for i in range(j, n)]
    qi = jnp.asarray([p[0] for p in pairs], jnp.int32)
    kj = jnp.asarray([p[1] for p in pairs], jnp.int32)

    def body(qi_ref, kj_ref, q_ref, k_ref, v_ref, do_ref, lse_ref, dl_ref, dq_ref, dk_ref, dv_ref, dk_acc, dv_acc):
        sid = pl.program_id(1)
        i, j = qi_ref[sid], kj_ref[sid]

        @pl.when(sid == 0)
        def _():
            dq_ref[...] = jnp.zeros_like(dq_ref)

        def step(masked):
            s = lax.dot_general(q_ref[...], k_ref[...], NT_DIMS, preferred_element_type=F32)
            if masked:
                s = jnp.where(_causal_mask(t, t), s, NEG_BIG)
            p = jnp.exp(s - lse_ref[:, :1])
            dp = lax.dot_general(do_ref[...], v_ref[...], NT_DIMS, preferred_element_type=F32)
            ds = (p * (dp - dl_ref[:, :1])).astype(BF16)
            dv = lax.dot_general(p.astype(BF16), do_ref[...], TN_DIMS, preferred_element_type=F32)
            dk = lax.dot_general(ds, q_ref[...], TN_DIMS, preferred_element_type=F32)
            if masked:
                dv_acc[...] = dv
                dk_acc[...] = dk
            else:
                dv_acc[...] += dv
                dk_acc[...] += dk
            rows = pl.ds(pl.multiple_of(i * t, t), t)
            dq_ref[rows, :] += jnp.dot(ds, k_ref[...], preferred_element_type=F32) * scale

        @pl.when(i == j)
        def _():
            step(True)

        @pl.when(i > j)
        def _():
            step(False)

        @pl.when(i == n - 1)
        def _():
            dk_ref[...] = dk_acc[...]
            dv_ref[...] = dv_acc[...]

    qs = pl.BlockSpec((t, HEAD_PAD), lambda h, s, qi, kj: (qi[s], h))
    rs = pl.BlockSpec((t, V_DIM), lambda h, s, qi, kj: (qi[s], h))
    ks = pl.BlockSpec((t, HEAD_PAD), lambda h, s, qi, kj: (kj[s], h))
    vs = pl.BlockSpec((t, V_DIM), lambda h, s, qi, kj: (kj[s], h))
    return pl.pallas_call(
        body, name=name,
        grid_spec=pltpu.PrefetchScalarGridSpec(
            num_scalar_prefetch=2, grid=(H, len(pairs)), in_specs=[qs, ks, vs, rs, rs, rs],
            out_specs=[pl.BlockSpec((T, HEAD_PAD), lambda h, s, qi, kj: (0, h)), ks, vs],
            scratch_shapes=[pltpu.VMEM((t, HEAD_PAD), F32), pltpu.VMEM((t, V_DIM), F32)]),
        out_shape=[jax.ShapeDtypeStruct((T, H * HEAD_PAD), F32), jax.ShapeDtypeStruct((T, H * HEAD_PAD), F32),
                   jax.ShapeDtypeStruct((T, H * V_DIM), F32)],
        compiler_params=_params(("parallel", "arbitrary")),
    )(qi, kj, qf, kf, v, do, lse, delta)


def _alibi_slopes():
    tot = DIL_GROUPS * DIL_HEADS
    return [float(np.float32(2.0) ** (np.float32(-8.0) * np.float32(k) / np.float32(tot))) for k in range(1, tot + 1)]


def _dil_masks():
    iq = lax.broadcasted_iota(jnp.int32, (DIL_BLK, DIL_BLK), 0)
    ik = lax.broadcasted_iota(jnp.int32, (DIL_BLK, DIL_BLK), 1)
    return (ik >= iq), (iq + DIL_BLK - ik).astype(F32), (ik <= iq), (iq - ik).astype(F32)


def _dil_norm(x, g):
    r = lax.rsqrt(jnp.mean(x * x, axis=-1, keepdims=True) + EPS)
    return x * r, r


def _dil_fwd(qkv, g_qn, g_kn, slopes, name):
    T = qkv.shape[0]
    GH = DIL_GROUPS * DIL_HEADS
    scale = 1.0 / math.sqrt(DIL_HEAD_DIM)

    def body(sl_ref, q_ref, k_ref, v_ref, gq_ref, gk_ref, o_ref, lse_ref):
        gh = pl.program_id(0)
        slope = sl_ref[gh]
        ok_p, dist_p, ok_c, dist_c = _dil_masks()
        gq, gk = gq_ref[...], gk_ref[...]
        for g, (_, d) in enumerate(DIL_PAIRS):
            @pl.when((gh >= g * DIL_HEADS) & (gh < (g + 1) * DIL_HEADS))
            def _(d=d):
                nb = T // (d * DIL_BLK)
                bias_p = jnp.where(ok_p, -slope * d * dist_p, NEG_BIG)
                bias_c = jnp.where(ok_c, -slope * d * dist_c, NEG_BIG)

                def phase(r, _):
                    def blk(nn, _):
                        def rows(b):
                            return pl.ds(b * (d * DIL_BLK) + r, DIL_BLK, stride=d) if d > 1 else pl.ds(pl.multiple_of(b * DIL_BLK, DIL_BLK), DIL_BLK)
                        cur, prv = rows(nn), rows(jnp.maximum(nn - 1, 0))
                        q = (_dil_norm(q_ref[cur, :], gq)[0] * gq).astype(BF16)
                        kc = (_dil_norm(k_ref[cur, :], gk)[0] * gk).astype(BF16)
                        kp = (_dil_norm(k_ref[prv, :], gk)[0] * gk).astype(BF16)
                        s_c = lax.dot_general(q, kc, NT_DIMS, preferred_element_type=F32) * scale + bias_c
                        s_p = lax.dot_general(q, kp, NT_DIMS, preferred_element_type=F32) * scale + bias_p
                        s_p = jnp.where(nn > 0, s_p, NEG_BIG)
                        m = jnp.maximum(jnp.max(s_c, axis=-1, keepdims=True), jnp.max(s_p, axis=-1, keepdims=True))
                        p_c = jnp.exp(s_c - m)
                        p_p = jnp.exp(s_p - m)
                        l = jnp.sum(p_c, axis=-1, keepdims=True) + jnp.sum(p_p, axis=-1, keepdims=True)
                        acc = jnp.dot(p_c.astype(BF16), v_ref[cur, :].astype(BF16), preferred_element_type=F32)
                        acc += jnp.dot(p_p.astype(BF16), v_ref[prv, :].astype(BF16), preferred_element_type=F32)
                        o_ref[cur, :] = acc / l
                        lse_ref[cur, :] = jnp.broadcast_to(m + jnp.log(l), (DIL_BLK, DIL_HEAD_DIM))
                        return 0
                    lax.fori_loop(0, nb, blk, 0, unroll=min(nb, DIL_UNROLL))
                    return 0
                lax.fori_loop(0, d, phase, 0, unroll=min(d, max(1, DIL_UNROLL // nb)))

    col = lambda off: pl.BlockSpec((T, DIL_HEAD_DIM), lambda gh, sl: (0, gh + off))
    gvec = pl.BlockSpec((1, DIL_HEAD_DIM), lambda gh, sl: (0, 0))
    return pl.pallas_call(
        body, name=name,
        grid_spec=pltpu.PrefetchScalarGridSpec(
            num_scalar_prefetch=1, grid=(GH,),
            in_specs=[col(0), col(GH), col(2 * GH), gvec, gvec], out_specs=[col(0), col(0)]),
        out_shape=[jax.ShapeDtypeStruct((T, GH * DIL_HEAD_DIM), F32)] * 2,
        compiler_params=_params(("parallel",)),
    )(slopes, qkv, qkv, qkv, g_qn, g_kn)


def _dil_merge(o_g, lse_g, name):
    T = o_g.shape[0]
    W = DIL_HEADS * DIL_HEAD_DIM
    tr = _pick(T, (256, 128))

    def body(o0, o1, o2, l0, l1, l2, o_ref, lse_ref):
        a, b, c = l0[...], l1[...], l2[...]
        m = jnp.maximum(jnp.maximum(a, b), c)
        ea, eb, ec = jnp.exp(a - m), jnp.exp(b - m), jnp.exp(c - m)
        tot = ea + eb + ec
        o_ref[...] = ((o0[...] * ea + o1[...] * eb + o2[...] * ec) / tot).astype(BF16)
        lse_ref[...] = m + jnp.log(tot)

    grp = lambda g: pl.BlockSpec((tr, W), lambda i: (i, g))
    out = pl.BlockSpec((tr, W), lambda i: (i, 0))
    return pl.pallas_call(
        body, name=name, grid=(T // tr,),
        in_specs=[grp(0), grp(1), grp(2), grp(0), grp(1), grp(2)], out_specs=[out, out],
        out_shape=[jax.ShapeDtypeStruct((T, W), BF16), jax.ShapeDtypeStruct((T, W), F32)],
        compiler_params=_params(("parallel",)),
    )(o_g, o_g, o_g, lse_g, lse_g, lse_g)


def _dil_bwd(qkv, g_qn, g_kn, slopes, do, delta, lse, name):
    T = qkv.shape[0]
    GH = DIL_GROUPS * DIL_HEADS
    scale = 1.0 / math.sqrt(DIL_HEAD_DIM)
    nchunk = T // DIL_BLK

    def body(sl_ref, q_ref, k_ref, v_ref, gq_ref, gk_ref, do_ref, dl_ref, lse_ref,
             dq_ref, dk_ref, dv_ref, dgq_ref, dgk_ref, dq_acc, dk_acc, dv_acc):
        gh = pl.program_id(0)
        slope = sl_ref[gh]
        ok_p, dist_p, ok_c, dist_c = _dil_masks()
        gq, gk = gq_ref[...], gk_ref[...]

        @pl.when(gh == 0)
        def _():
            dgq_ref[...] = jnp.zeros_like(dgq_ref)
            dgk_ref[...] = jnp.zeros_like(dgk_ref)

        dk_acc[...] = jnp.zeros_like(dk_acc)
        dv_acc[...] = jnp.zeros_like(dv_acc)
        for g, (_, d) in enumerate(DIL_PAIRS):
            @pl.when((gh >= g * DIL_HEADS) & (gh < (g + 1) * DIL_HEADS))
            def _(d=d):
                nb = T // (d * DIL_BLK)
                bias_p = jnp.where(ok_p, -slope * d * dist_p, NEG_BIG)
                bias_c = jnp.where(ok_c, -slope * d * dist_c, NEG_BIG)

                def phase(r, _):
                    def blk(nn, _):
                        def rows(b):
                            return pl.ds(b * (d * DIL_BLK) + r, DIL_BLK, stride=d) if d > 1 else pl.ds(pl.multiple_of(b * DIL_BLK, DIL_BLK), DIL_BLK)
                        cur, prv = rows(nn), rows(jnp.maximum(nn - 1, 0))
                        q = (_dil_norm(q_ref[cur, :], gq)[0] * gq).astype(BF16)
                        kc = (_dil_norm(k_ref[cur, :], gk)[0] * gk).astype(BF16)
                        kp = (_dil_norm(k_ref[prv, :], gk)[0] * gk).astype(BF16)
                        vc = v_ref[cur, :].astype(BF16)
                        vp = v_ref[prv, :].astype(BF16)
                        dob = do_ref[cur, :].astype(BF16)
                        delta = dl_ref[cur, :][:, :1]
                        ls = lse_ref[cur, :][:, :1]
                        s_c = lax.dot_general(q, kc, NT_DIMS, preferred_element_type=F32) * scale + bias_c
                        s_p = lax.dot_general(q, kp, NT_DIMS, preferred_element_type=F32) * scale + bias_p
                        s_p = jnp.where(nn > 0, s_p, NEG_BIG)
                        p_c = jnp.exp(s_c - ls)
                        p_p = jnp.exp(s_p - ls)
                        dp_c = lax.dot_general(dob, vc, NT_DIMS, preferred_element_type=F32)
                        dp_p = lax.dot_general(dob, vp, NT_DIMS, preferred_element_type=F32)
                        ds_c = (p_c * (dp_c - delta)).astype(BF16)
                        ds_p = (p_p * (dp_p - delta)).astype(BF16)
                        dq_acc[cur, :] = (jnp.dot(ds_c, kc, preferred_element_type=F32)
                                          + jnp.dot(ds_p, kp, preferred_element_type=F32)) * scale
                        dk_acc[cur, :] += lax.dot_general(ds_c, q, TN_DIMS, preferred_element_type=F32) * scale
                        dv_acc[cur, :] += lax.dot_general(p_c.astype(BF16), dob, TN_DIMS, preferred_element_type=F32)
                        dk_acc[prv, :] += lax.dot_general(ds_p, q, TN_DIMS, preferred_element_type=F32) * scale
                        dv_acc[prv, :] += lax.dot_general(p_p.astype(BF16), dob, TN_DIMS, preferred_element_type=F32)
                        return 0
                    lax.fori_loop(0, nb, blk, 0, unroll=min(nb, DIL_UNROLL))
                    return 0
                lax.fori_loop(0, d, phase, 0, unroll=min(d, max(1, DIL_UNROLL // nb)))

        def fin(ci, carry):
            dgq, dgk = carry
            rows = pl.ds(pl.multiple_of(ci * DIL_BLK, DIL_BLK), DIL_BLK)
            outs = []
            for x_ref, d_acc, gvec in ((q_ref, dq_acc, gq), (k_ref, dk_acc, gk)):
                xhat, r = _dil_norm(x_ref[rows, :], gvec)
                dn = d_acc[rows, :]
                dxh = dn * gvec
                c = jnp.mean(dxh * xhat, axis=-1, keepdims=True)
                outs.append(((r * (dxh - xhat * c)).astype(BF16), jnp.sum(dn * xhat, axis=0, keepdims=True)))
            dq_ref[rows, :] = outs[0][0]
            dk_ref[rows, :] = outs[1][0]
            dv_ref[rows, :] = dv_acc[rows, :].astype(BF16)
            return dgq + outs[0][1], dgk + outs[1][1]

        z = jnp.zeros((1, DIL_HEAD_DIM), F32)
        dgq, dgk = lax.fori_loop(0, nchunk, fin, (z, z))
        dgq_ref[...] += dgq
        dgk_ref[...] += dgk

    col = lambda off: pl.BlockSpec((T, DIL_HEAD_DIM), lambda gh, sl: (0, gh + off))
    hcol = pl.BlockSpec((T, DIL_HEAD_DIM), lambda gh, sl: (0, gh % DIL_HEADS))
    gvec = pl.BlockSpec((1, DIL_HEAD_DIM), lambda gh, sl: (0, 0))
    wide = jax.ShapeDtypeStruct((T, GH * DIL_HEAD_DIM), BF16)
    vec = jax.ShapeDtypeStruct((1, DIL_HEAD_DIM), F32)
    return pl.pallas_call(
        body, name=name,
        grid_spec=pltpu.PrefetchScalarGridSpec(
            num_scalar_prefetch=1, grid=(GH,),
            in_specs=[col(0), col(GH), col(2 * GH), gvec, gvec, hcol, hcol, hcol],
            out_specs=[col(0), col(0), col(0), gvec, gvec],
            scratch_shapes=[pltpu.VMEM((T, DIL_HEAD_DIM), F32)] * 3),
        out_shape=[wide, wide, wide, vec, vec],
        compiler_params=_params(("arbitrary",)),
    )(slopes, qkv, qkv, qkv, g_qn, g_kn, do, delta, lse)


def _my_pos():
    return lax.axis_index("x"), lax.axis_index("y"), lax.axis_index("c")


def _peer(pos, j):
    x, y, c = pos
    px = 1 - x if j & 4 else x
    py = 1 - y if j & 2 else y
    pc = 1 - c if j & 1 else c
    return (px, py, pc), 4 * px + 2 * py + pc


def _slot(idx, paired):
    if not paired:
        return idx
    return jnp.where(idx < N_DEV // 2, 2 * idx, 2 * idx - (N_DEV - 1))


def _shard_slice(ref, axis, idx, size, paired=False):
    sl = [slice(None)] * len(ref.shape)
    sl[axis] = pl.ds(pl.multiple_of(_slot(idx, paired) * size, 8), size)
    return ref.at[tuple(sl)]


HBM_SPEC = pl.BlockSpec(memory_space=pltpu.HBM)
SEM_SPEC = pl.BlockSpec(memory_space=pltpu.SEMAPHORE)
DATAFLOW = pltpu.SideEffectType.DATAFLOW_SIDE_EFFECTING
N_PEER = N_DEV - 1


def _xchg_refs(kind, axis, bufs, frm, to):
    axis, paired = axis
    if kind == "gather":
        size = bufs[0].shape[axis] // N_DEV
        piece = _shard_slice(bufs[0], axis, frm, size, paired)
        return piece, piece
    grad, slots = bufs
    return _shard_slice(grad, axis, to, grad.shape[axis] // N_DEV, paired), slots.at[frm]


def _xchg_start(kind, arrays, axes, name, deps=()):
    n = len(arrays)
    nb = 1 if kind == "gather" else 2

    def body(*refs):
        bufs = [refs[a * nb:(a + 1) * nb] for a in range(n)]
        outs = refs[nb * n + len(deps):]
        send, recv, token = outs[:n], outs[n:2 * n], outs[2 * n + nb * n]
        pos = _my_pos()
        me = 4 * pos[0] + 2 * pos[1] + pos[2]
        for a in range(n):
            for j in range(1, N_DEV):
                dev, pid = _peer(pos, j)
                src, dst = _xchg_refs(kind, axes[a], bufs[a], me, pid)
                pltpu.make_async_remote_copy(src_ref=src, dst_ref=dst, send_sem=send[a].at[j - 1],
                                             recv_sem=recv[a].at[j - 1], device_id=dev, device_id_type=MESH).start()
        token[...] = jnp.zeros_like(token)

    ops = []
    for g, (ax, _) in zip(arrays, axes):
        ops.append(g)
        if kind == "scatter":
            shp = list(g.shape)
            shp[ax] //= N_DEV
            ops.append(lax.empty((N_DEV,) + tuple(shp), g.dtype))
    sems = [pltpu.SemaphoreType.DMA((N_PEER,))] * (2 * n)
    res = pl.pallas_call(
        body, name=name,
        out_shape=sems + [pltpu.HBM(o.shape, o.dtype) for o in ops] + [jax.ShapeDtypeStruct((8, LANES), F32)],
        in_specs=[HBM_SPEC] * len(ops) + [ANY_SPEC] * len(deps),
        out_specs=[SEM_SPEC] * (2 * n) + [HBM_SPEC] * len(ops) + [pl.BlockSpec(memory_space=pltpu.VMEM)],
        input_output_aliases={i: 2 * n + i for i in range(len(ops))},
        compiler_params=pltpu.CompilerParams(has_side_effects=DATAFLOW),
    )(*[pltpu.with_memory_space_constraint(o, pltpu.HBM) for o in ops], *deps)
    items = [(res[a], res[n + a], tuple(res[2 * n + a * nb:2 * n + (a + 1) * nb])) for a in range(n)]
    return items, res[2 * n + len(ops)]


def _xchg_wait(kind, items, axes, after, name):
    n = len(items)
    nb = 1 if kind == "gather" else 2

    def body(*refs):
        bufs = [refs[a * nb:(a + 1) * nb] for a in range(n)]
        send, recv = refs[nb * n:nb * n + n], refs[nb * n + n:nb * n + 2 * n]
        pos = _my_pos()
        me = 4 * pos[0] + 2 * pos[1] + pos[2]
        for a in range(n):
            for j in range(1, N_DEV):
                dev, pid = _peer(pos, j)
                src, dst = _xchg_refs(kind, axes[a], bufs[a], pid, me)
                cp = pltpu.make_async_remote_copy(src_ref=src, dst_ref=dst, send_sem=send[a].at[j - 1],
                                                  recv_sem=recv[a].at[j - 1], device_id=dev, device_id_type=MESH)
                cp.wait_send()
                cp.wait_recv()

    ops = [b for it in items for b in it[2]]
    res = pl.pallas_call(
        body, name=name,
        out_shape=[pltpu.HBM(o.shape, o.dtype) for o in ops],
        in_specs=[HBM_SPEC] * len(ops) + [SEM_SPEC] * (2 * n) + [ANY_SPEC],
        out_specs=[HBM_SPEC] * len(ops),
        input_output_aliases={i: i for i in range(len(ops))},
        compiler_params=pltpu.CompilerParams(has_side_effects=DATAFLOW),
    )(*ops, *[it[0] for it in items], *[it[1] for it in items], after)
    return [res[a * nb + nb - 1] for a in range(n)]


def _gain_allreduce(v, name):
    n = v.shape[1]

    def body(v_ref, o_ref, slots, send_sems, recv_sems):
        pos = _my_pos()
        me = 4 * pos[0] + 2 * pos[1] + pos[2]
        slots[me] = v_ref[...]
        copies = []
        for j in range(1, N_DEV):
            dev, _ = _peer(pos, j)
            cp = pltpu.make_async_remote_copy(
                src_ref=slots.at[me], dst_ref=slots.at[me], send_sem=send_sems.at[j], recv_sem=recv_sems.at[j],
                device_id=dev, device_id_type=MESH)
            cp.start()
            copies.append(cp)
        for j in range(1, N_DEV):
            dev, pid = _peer(pos, j)
            pltpu.make_async_remote_copy(
                src_ref=slots.at[me], dst_ref=slots.at[pid], send_sem=send_sems.at[j], recv_sem=recv_sems.at[j],
                device_id=dev, device_id_type=MESH).wait_recv()
        for cp in copies:
            cp.wait_send()
        acc = slots[0]
        for s in range(1, N_DEV):
            acc = acc + slots[s]
        o_ref[...] = acc

    return pl.pallas_call(
        body, name=name, out_shape=jax.ShapeDtypeStruct((1, n), F32),
        in_specs=[pl.BlockSpec(memory_space=pltpu.VMEM)], out_specs=pl.BlockSpec(memory_space=pltpu.VMEM),
        scratch_shapes=[pltpu.VMEM((N_DEV, 1, n), F32), pltpu.SemaphoreType.DMA((N_DEV,)),
                        pltpu.SemaphoreType.DMA((N_DEV,))],
        compiler_params=pltpu.CompilerParams(has_side_effects=True),
    )(v)


def _adamw(parts, own, me, w, m, v, layer, prev, name):
    L, R, C = w.shape
    P = parts.shape[0]
    tr = _pick(R, (128, 64, 32, 16, 8, 1))
    c1 = 1.0 - ADAM_B1 ** ADAM_STEP
    c2 = 1.0 - ADAM_B2 ** ADAM_STEP
    n_in = 4 if own is None else 5

    def body(me_ref, *refs):
        p_ref = refs[0]
        w_ref, m_ref, v_ref = refs[n_in - 3:n_in]
        g_out, d_out, m_out, v_out, tok = refs[-5:]
        g = None
        for s in range(P):
            part = p_ref[s]
            if own is not None:
                part = jnp.where(me_ref[0] == s, refs[1][...], part)
            g = part.astype(F32) if g is None else g + part.astype(F32)
        mn = ADAM_B1 * m_ref[...] + (1.0 - ADAM_B1) * g
        vn = ADAM_B2 * v_ref[...] + (1.0 - ADAM_B2) * (g * g)
        g_out[...] = g
        m_out[...] = mn
        v_out[...] = vn
        d_out[...] = -ADAM_LR * ((mn / c1) / (jnp.sqrt(vn / c2) + ADAM_EPS) + ADAM_WD * w_ref[...])
        tok[...] = jnp.zeros_like(tok)

    row = pl.BlockSpec((None, tr, C), lambda i, me_ref: (layer, i, 0))
    in_specs = [pl.BlockSpec((P, tr, C), lambda i, me_ref: (0, i, 0))]
    args = [parts]
    if own is not None:
        in_specs.append(pl.BlockSpec((tr, C), lambda i, me_ref: (i, 0)))
        args.append(own)
    in_specs += [row, row, row]
    args += [w, m, v]
    aliases = {}
    if prev is not None:
        in_specs += [ANY_SPEC] * 4
        aliases = {1 + len(args) + k: k for k in range(4)}
        args += list(prev)
    shp = jax.ShapeDtypeStruct((L, R, C), F32)
    res = pl.pallas_call(
        body, name=name,
        grid_spec=pltpu.PrefetchScalarGridSpec(
            num_scalar_prefetch=1, grid=(R // tr,), in_specs=in_specs,
            out_specs=[row] * 4 + [pl.BlockSpec((8, LANES), lambda i, me_ref: (0, 0))]),
        out_shape=[shp] * 4 + [jax.ShapeDtypeStruct((8, LANES), F32)],
        input_output_aliases=aliases, compiler_params=_params(("arbitrary",)),
    )(me, *args)
    return res[:4], res[4]


def _pad_heads(w):
    lead = w.shape[:-1]
    n = w.shape[-1] // QK_DIM
    w = w.reshape(lead + (n, QK_DIM))
    w = jnp.pad(w, [(0, 0)] * len(lead) + [(0, 0), (0, HEAD_PAD - QK_DIM)])
    return w.reshape(lead + (n * HEAD_PAD,))


def _unpad_heads(w):
    lead = w.shape[:-1]
    n = w.shape[-1] // HEAD_PAD
    return w.reshape(lead + (n, HEAD_PAD))[..., :QK_DIM].reshape(lead + (n * QK_DIM,))


def kernel(x, ffn1_norm, ffn1_w_in, ffn1_w_out, mix_norm, ffn2_norm, ffn2_w_in, ffn2_w_out, mla_w_down, mla_g_cq, mla_g_ckv, mla_w_uq, mla_w_ukv, mla_g_qn, mla_g_kn, mla_w_o, dil_w_qkv, dil_g_qn, dil_g_kn, dil_w_o, loss_target, m_ffn1_norm, m_ffn1_w_in, m_ffn1_w_out, m_mix_norm, m_ffn2_norm, m_ffn2_w_in, m_ffn2_w_out, m_mla_w_down, m_mla_g_cq, m_mla_g_ckv, m_mla_w_uq, m_mla_w_ukv, m_mla_g_qn, m_mla_g_kn, m_mla_w_o, m_dil_w_qkv, m_dil_g_qn, m_dil_g_kn, m_dil_w_o, v_ffn1_norm, v_ffn1_w_in, v_ffn1_w_out, v_mix_norm, v_ffn2_norm, v_ffn2_w_in, v_ffn2_w_out, v_mla_w_down, v_mla_g_cq, v_mla_g_ckv, v_mla_w_uq, v_mla_w_ukv, v_mla_g_qn, v_mla_g_kn, v_mla_w_o, v_dil_w_qkv, v_dil_g_qn, v_dil_g_kn, v_dil_w_o):
    names = ["ffn1_norm", "ffn1_w_in", "ffn1_w_out", "mix_norm", "ffn2_norm", "ffn2_w_in", "ffn2_w_out", "mla_w_down",
             "mla_g_cq", "mla_g_ckv", "mla_w_uq", "mla_w_ukv", "mla_g_qn", "mla_g_kn", "mla_w_o", "dil_w_qkv",
             "dil_g_qn", "dil_g_kn", "dil_w_o"]
    W = dict(zip(names, [ffn1_norm, ffn1_w_in, ffn1_w_out, mix_norm, ffn2_norm, ffn2_w_in, ffn2_w_out, mla_w_down,
                         mla_g_cq, mla_g_ckv, mla_w_uq, mla_w_ukv, mla_g_qn, mla_g_kn, mla_w_o, dil_w_qkv,
                         dil_g_qn, dil_g_kn, dil_w_o]))
    M1 = dict(zip(names, [m_ffn1_norm, m_ffn1_w_in, m_ffn1_w_out, m_mix_norm, m_ffn2_norm, m_ffn2_w_in, m_ffn2_w_out,
                          m_mla_w_down, m_mla_g_cq, m_mla_g_ckv, m_mla_w_uq, m_mla_w_ukv, m_mla_g_qn, m_mla_g_kn,
                          m_mla_w_o, m_dil_w_qkv, m_dil_g_qn, m_dil_g_kn, m_dil_w_o]))
    V2 = dict(zip(names, [v_ffn1_norm, v_ffn1_w_in, v_ffn1_w_out, v_mix_norm, v_ffn2_norm, v_ffn2_w_in, v_ffn2_w_out,
                          v_mla_w_down, v_mla_g_cq, v_mla_g_ckv, v_mla_w_uq, v_mla_w_ukv, v_mla_g_qn, v_mla_g_kn,
                          v_mla_w_o, v_dil_w_qkv, v_dil_g_qn, v_dil_g_kn, v_dil_w_o]))
    S, D = x.shape[1], x.shape[2]
    x0 = x.reshape(S, D)
    tgt = loss_target.reshape(S, D)

    big = ["ffn1_w_in", "ffn1_w_out", "ffn2_w_in", "ffn2_w_out", "mla_w_down", "mla_w_uq", "mla_w_ukv", "mla_w_o",
           "dil_w_qkv", "dil_w_o"]
    shard_dim = {"ffn1_w_in": 2, "ffn1_w_out": 1, "ffn2_w_in": 2, "ffn2_w_out": 1, "mla_w_down": 1, "mla_w_uq": 2,
                 "mla_w_ukv": 2, "mla_w_o": 1, "dil_w_qkv": 2, "dil_w_o": 2}
    paired = ("ffn1_w_in", "ffn2_w_in")
    shard_axis = {n: (d, n in paired) for n, d in shard_dim.items()}
    grad_axis = {n: (d - 1, n in paired) for n, d in shard_dim.items()}

    def padded(n, w):
        if n == "mla_w_down":
            return jnp.pad(w, ((0, 0), (0, 0), (0, LAT_PAD - w.shape[2])))
        if n == "mla_w_uq":
            return _pad_heads(w)
        return w

    depth = ffn1_norm.shape[0]
    blocks = []
    for l in range(depth):
        mixer = (["mla_w_down", "mla_w_uq", "mla_w_ukv", "mla_w_o"] if l % 2 == 0 else ["dil_w_qkv", "dil_w_o"])
        blocks.append((f"ffn1_{l}", [("ffn1_w_in", l), ("ffn1_w_out", l)]))
        blocks.append((f"mix_{l}", [(n, l // 2) for n in mixer]))
        blocks.append((f"ffn2_{l}", [("ffn2_w_in", l), ("ffn2_w_out", l)]))
    order = [k for _, keys in blocks for k in keys]
    me = (4 * lax.axis_index("x") + 2 * lax.axis_index("y") + lax.axis_index("c")).astype(jnp.int32).reshape(1)
    mine = [_cast_into_gathered(padded(n, W[n]), l, shard_axis[n], me, f"cast_{n}_{l}") for n, l in order]
    ag_items, ag_token = _xchg_start("gather", mine, [shard_axis[n] for n, _ in order], "gather_start")
    ag_items = dict(zip(order, ag_items))
    full = {}

    def fetch(keys, after, tag):
        lands = _xchg_wait("gather", [ag_items[k] for k in keys], [shard_axis[k[0]] for k in keys], after,
                           f"gather_wait_{tag}")
        full.update(zip(keys, lands))

    g_qn = _pad_heads(mla_g_qn)
    g_kn = _pad_heads(mla_g_kn)
    tabs = _rope_tables(S)
    slopes = jnp.asarray(_alibi_slopes(), F32)

    grads = {}
    gain_g = {}

    out_g, out_d, out_m, out_v = {}, {}, {}, {}
    pending = []
    lag = 3

    def scatter_start(tag, keys):
        items, token = _xchg_start("scatter", [grads[k] for k in keys], [grad_axis[k[0]] for k in keys],
                                   f"scatter_start_{tag}")
        pending.append((tag, keys, items))
        return token

    def scatter_finish(after):
        tag, keys, items = pending.pop(0)
        lands = _xchg_wait("scatter", items, [grad_axis[k[0]] for k in keys], after, f"scatter_wait_{tag}")
        tokens = []
        for (n, l), p in zip(keys, lands):
            ax, pair = grad_axis[n]
            size = grads[(n, l)].shape[ax] // N_DEV
            own = lax.dynamic_slice_in_dim(grads[(n, l)], _slot(me[0], pair) * size, size, axis=ax)
            if n == "mla_w_down":
                p, own = p[..., :W[n].shape[2]], own[..., :W[n].shape[2]]
            elif n == "mla_w_uq":
                p, own = _unpad_heads(p), _unpad_heads(own)
            prev = (out_g[n], out_d[n], out_m[n], out_v[n]) if n in out_g else None
            (out_g[n], out_d[n], out_m[n], out_v[n]), tok = _adamw(p, own, me, W[n], M1[n], V2[n], l, prev,
                                                                    f"adamw_{n}_{l}")
            tokens.append(tok)
        return tokens

    def finish_due(after):
        tokens = []
        while len(pending) > lag:
            tokens += scatter_finish(after)
        return tokens

    def ffn_fwd(xin, norm_row, which, l, tag, deps=()):
        k_in, k_out = (which + "_w_in", l), (which + "_w_out", l)
        h = _rms_fwd(xin, norm_row, f"rms_fwd_{tag}", deps=deps)
        fetch([k_in], h, f"in_{tag}")
        u, a = _ffn_in(h, full[k_in], f"ffn_in_{tag}")
        fetch([k_out], a, f"out_{tag}")
        xo = _mm(a, full[k_out], "nn", F32, f"mm_out_{tag}", scale=0.5, res=xin, layer=0)
        return xo, (xin, h, u, a)

    def ffn_bwd(dx_pair, saved, norm_row, which, l, tag):
        dxo, dxob = dx_pair
        k_in, k_out = (which + "_w_in", l), (which + "_w_out", l)
        xin, h, u, a = saved
        grads[k_out] = _mm(a, dxob, "tn", BF16, f"mm_dwout_{tag}", scale=0.5)
        t_out = scatter_start(f"{tag}_out", [k_out])
        du = _ffn_da(dxob, full[k_out], u, f"ffn_da_{tag}", deps=[t_out])
        grads[k_in] = _mm(h, du, "tn", BF16, f"mm_dwin_{tag}")
        t_in = scatter_start(f"{tag}_in", [k_in])
        dh = _mm(du, full[k_in], "nt", F32, f"mm_dh_{tag}", layer=0, deps=[t_in])
        toks = finish_due(dh)
        dx, dxb, dg = _rms_bwd(xin, norm_row, dh, dxo, f"rms_bwd_{tag}", deps=toks)
        gain_g.setdefault(which + "_norm", {})[l] = dg
        return dx, dxb

    def mla_fwd(xin, l):
        j = l // 2
        xn = _rms_fwd(xin, mix_norm[l:l + 1], "rms_fwd_mla")
        fetch([(n, j) for n in ("mla_w_down", "mla_w_uq", "mla_w_ukv", "mla_w_o")], xn, "mla")
        lat = _mm(xn, full[("mla_w_down", j)], "nn", F32, "mm_lat", layer=0)
        cq, ckv = _lat_norm_fwd(lat, mla_g_cq[j:j + 1], mla_g_ckv[j:j + 1], "lat_norm_fwd")
        q_raw = _mm(cq, full[("mla_w_uq", j)], "nn", F32, "mm_uq", layer=0)
        kv = _mm(ckv, full[("mla_w_ukv", j)], "nn", F32, "mm_ukv", layer=0)
        qf, kf, vb = _mla_prep_fwd(q_raw, kv, lat, g_qn[j:j + 1], g_kn[j:j + 1], tabs, "mla_prep_fwd")
        o, lse = _flash_fwd(qf, kf, vb, "flash_fwd")
        xo = _mm(o, full[("mla_w_o", j)], "nn", F32, "mm_mla_o", res=xin, layer=0)
        return xo, (xin, xn, lat, cq, ckv, q_raw, kv, qf, kf, vb, o, lse)

    def mla_bwd(dx_pair, saved, l):
        dxo, dxob = dx_pair
        j = l // 2
        xin, xn, lat, cq, ckv, q_raw, kv, qf, kf, vb, o, lse = saved
        do = _mm(dxob, full[("mla_w_o", j)], "nt", BF16, "mm_mla_do", layer=0)
        grads[("mla_w_o", j)] = _mm(o, dxob, "tn", BF16, "mm_mla_dwo")
        delta = _attn_delta(do, o, "attn_delta")
        dqf, dkf, dv = _flash_bwd(qf, kf, vb, do, lse, delta, "flash_bwd")
        dq_raw, dkv, dkpe, dgq, dgk = _mla_prep_bwd(q_raw, kv, lat, g_qn[j:j + 1], g_kn[j:j + 1], tabs, dqf, dkf, dv,
                                                    "mla_prep_bwd")
        gain_g.setdefault("mla_g_qn", {})[j] = dgq
        gain_g.setdefault("mla_g_kn", {})[j] = dgk
        dcq = _mm(dq_raw, full[("mla_w_uq", j)], "nt", F32, "mm_dcq", layer=0)
        grads[("mla_w_uq", j)] = _mm(cq, dq_raw, "tn", BF16, "mm_dwuq")
        dckv = _mm(dkv, full[("mla_w_ukv", j)], "nt", F32, "mm_dckv", layer=0)
        grads[("mla_w_ukv", j)] = _mm(ckv, dkv, "tn", BF16, "mm_dwukv")
        dlat, dgcq, dgckv = _lat_norm_bwd(lat, mla_g_cq[j:j + 1], mla_g_ckv[j:j + 1], dcq, dckv, dkpe, "lat_norm_bwd")
        gain_g.setdefault("mla_g_cq", {})[j] = dgcq
        gain_g.setdefault("mla_g_ckv", {})[j] = dgckv
        dxn = _mm(dlat, full[("mla_w_down", j)], "nt", F32, "mm_dxn_mla", layer=0)
        grads[("mla_w_down", j)] = _mm(xn, dlat, "tn", BF16, "mm_dwdown")
        tok = scatter_start(f"mix_{l}", [(n, j) for n in ("mla_w_down", "mla_w_uq", "mla_w_ukv", "mla_w_o")])
        toks = finish_due(dxn)
        dx, dxb, dg = _rms_bwd(xin, mix_norm[l:l + 1], dxn, dxo, "rms_bwd_mla", deps=[tok] + toks)
        gain_g.setdefault("mix_norm", {})[l] = dg
        return dx, dxb

    def dil_fwd(xin, l):
        j = l // 2
        xn = _rms_fwd(xin, mix_norm[l:l + 1], "rms_fwd_dil")
        fetch([("dil_w_qkv", j), ("dil_w_o", j)], xn, "dil")
        qkv = _mm(xn, full[("dil_w_qkv", j)], "nn", F32, "mm_qkv", layer=0)
        o_g, lse_g = _dil_fwd(qkv, dil_g_qn[j:j + 1], dil_g_kn[j:j + 1], slopes, "dil_fwd")
        o, lse = _dil_merge(o_g, lse_g, "dil_merge")
        xo = _mm(o, full[("dil_w_o", j)], "nn", F32, "mm_dil_o", res=xin, layer=0)
        return xo, (xin, xn, qkv, o, lse)

    def dil_bwd(dx_pair, saved, l):
        dxo, dxob = dx_pair
        j = l // 2
        xin, xn, qkv, o, lse = saved
        do = _mm(dxob, full[("dil_w_o", j)], "nt", F32, "mm_dil_do", layer=0)
        grads[("dil_w_o", j)] = _mm(o, dxob, "tn", BF16, "mm_dil_dwo")
        delta = _attn_delta(do, o, "dil_delta")
        dq, dk, dv, dgq, dgk = _dil_bwd(qkv, dil_g_qn[j:j + 1], dil_g_kn[j:j + 1], slopes, do, delta, lse, "dil_bwd")
        gain_g.setdefault("dil_g_qn", {})[j] = dgq
        gain_g.setdefault("dil_g_kn", {})[j] = dgk
        dqkv = jnp.concatenate([dq, dk, dv], axis=1)
        dxn = _mm(dqkv, full[("dil_w_qkv", j)], "nt", F32, "mm_dxn_dil", layer=0)
        grads[("dil_w_qkv", j)] = _mm(xn, dqkv, "tn", BF16, "mm_dwqkv")
        tok = scatter_start(f"mix_{l}", [("dil_w_qkv", j), ("dil_w_o", j)])
        toks = finish_due(dxn)
        dx, dxb, dg = _rms_bwd(xin, mix_norm[l:l + 1], dxn, dxo, "rms_bwd_dil", deps=[tok] + toks)
        gain_g.setdefault("mix_norm", {})[l] = dg
        return dx, dxb

    saved = []
    xc = x0
    for l in range(depth):
        xc, s1 = ffn_fwd(xc, ffn1_norm[l:l + 1], "ffn1", l, f"ffn1_{l}", deps=[ag_token] if l == 0 else ())
        xc, s2 = (mla_fwd if l % 2 == 0 else dil_fwd)(xc, l)
        xc, s3 = ffn_fwd(xc, ffn2_norm[l:l + 1], "ffn2", l, f"ffn2_{l}")
        saved.append((s1, s2, s3))

    dy, dyb, loss_part = _loss_head(xc, tgt, "loss_head")
    dx = (dy, dyb)
    loss = lax.psum(loss_part[0, 0], MESH_AXES)

    for bi in reversed(range(len(blocks))):
        tag, _ = blocks[bi]
        l = bi // 3
        s = saved[l][bi % 3]
        if bi % 3 == 2:
            dx = ffn_bwd(dx, s, ffn2_norm[l:l + 1], "ffn2", l, tag)
        elif bi % 3 == 1:
            dx = (mla_bwd if l % 2 == 0 else dil_bwd)(dx, s, l)
        else:
            dx = ffn_bwd(dx, s, ffn1_norm[l:l + 1], "ffn1", l, tag)
    grad_x = dx[0].reshape(x.shape)
    after = dx[1]
    while pending:
        after = scatter_finish(after)[-1]

    small = [n for n in names if n not in big]

    def gain_local(n):
        rows = [gain_g[n][l] for l in range(W[n].shape[0])]
        g = jnp.concatenate(rows, axis=1)
        return g

    def flat_pad(n, a):
        a = a.reshape(1, -1)
        if n in ("mla_g_qn", "mla_g_kn"):
            a = _pad_heads(a)
        return a

    packed_g = jnp.concatenate([gain_local(n) for n in small], axis=1)
    sizes = [gain_local(n).shape[1] for n in small]
    tot_g = _gain_allreduce(packed_g, "gain_allreduce")
    pw = jnp.concatenate([flat_pad(n, W[n]) for n in small], axis=1)
    pm = jnp.concatenate([flat_pad(n, M1[n]) for n in small], axis=1)
    pv = jnp.concatenate([flat_pad(n, V2[n]) for n in small], axis=1)
    res, _ = _adamw(tot_g.reshape(1, 1, -1), None, me, pw.reshape(1, 1, -1), pm.reshape(1, 1, -1),
                    pv.reshape(1, 1, -1), 0, None, "adamw_gains")
    res = [r.reshape(1, -1) for r in res]
    off = 0
    for n, sz in zip(small, sizes):
        for dst, r in zip((out_g, out_d, out_m, out_v), res):
            piece = r[:, off:off + sz]
            if n in ("mla_g_qn", "mla_g_kn"):
                piece = _unpad_heads(piece)
            dst[n] = piece.reshape(W[n].shape)
        off += sz

    return (loss, grad_x, *[out_g[n] for n in names], *[out_d[n] for n in names],
            *[out_m[n] for n in names], *[out_v[n] for n in names])
```

```python
import functools
import math

import jax
import jax.numpy as jnp
import numpy as np
from jax import lax
from jax.experimental import pallas as pl
from jax.experimental.pallas import tpu as pltpu

EPS = 1e-6
MLA_HEADS = 16
Q_LORA = 512
KV_LORA = 512
NOPE_DIM = 128
ROPE_DIM = 64
V_DIM = 128
QK_DIM = NOPE_DIM + ROPE_DIM
ROPE_THETA = 10000.0
HEAD_PAD = 256
LAT_PAD = Q_LORA + KV_LORA + 128
DIL_PAIRS = ((128, 1), (512, 4), (2048, 16))
DIL_GROUPS = 3
DIL_HEADS = 8
DIL_HEAD_DIM = 128
DIL_BLK = 128
DIL_UNROLL = 4
ADAM_LR = 0.001
ADAM_B1 = 0.9
ADAM_B2 = 0.999
ADAM_EPS = 1e-08
ADAM_WD = 0.01
ADAM_STEP = 10

N_DEV = 8
MESH_AXES = ("x", "y", "c")
MESH = pl.DeviceIdType.MESH
NEG_BIG = -1e30
VMEM_LIMIT_V7X = 56 * 1024 * 1024
LANES = 128

BF16 = jnp.bfloat16
F32 = jnp.float32


def _pick(n, cands):
    for c in cands:
        if n % c == 0:
            return c
    raise ValueError(f"no tile for {n}")


def _params(sem):
    return pltpu.CompilerParams(dimension_semantics=sem, vmem_limit_bytes=VMEM_LIMIT_V7X)


ANY_SPEC = pl.BlockSpec(memory_space=pl.ANY)


MM_VMEM_BUDGET = 44 * 1024 * 1024
MM_HBM_BYTES_PER_S = 1.8e12
MM_MXU_FLOPS_PER_S = 8.5e14
MM_STEP_S = 0.4e-6
MM_MAX_TILE_MACS = 3.3e9
MXU_DIM = 256


@functools.lru_cache(maxsize=None)
def _mm_tiles(M, K, N, a_bytes, b_bytes, out_bytes, has_res):
    best = None
    for tk in [K] + [c for c in (1408, 1024, 512, 384, 256, 128) if K % c == 0 and c < K]:
        nk = K // tk
        for tm in [c for c in (2048, 1024, 512, 256, 128) if M % c == 0]:
            for tn in [c for c in (2816, 2048, 1408, 1152, 1024, 512, 384, 256, 128) if N % c == 0]:
                if tm * tk * tn > MM_MAX_TILE_MACS:
                    continue
                fill = (tn / (-(-tn // MXU_DIM) * MXU_DIM)) * (tk / (-(-tk // MXU_DIM) * MXU_DIM))
                fill *= tm / (tm + MXU_DIM // 2)
                vmem = 2 * (tm * tk * a_bytes + tk * tn * b_bytes) + 2 * tm * tn * out_bytes + tm * tn * 4
                vmem += (tm * tk + tk * tn) * 2 if max(a_bytes, b_bytes) > 2 else 0
                vmem += 2 * tm * tn * 4 if has_res else 0
                if vmem > MM_VMEM_BUDGET:
                    continue
                a_all, b_all = M * K * a_bytes, K * N * b_bytes
                if nk == 1:
                    t_i = a_all + (M // tm) * b_all
                    t_j = b_all + (N // tn) * a_all
                    traffic, i_outer = min((t_i, True), (t_j, False))
                else:
                    traffic, i_outer = (N // tn) * a_all + (M // tm) * b_all, True
                traffic += M * N * (out_bytes + (4 if has_res else 0))
                mxu = 2.0 * M * K * N / (MM_MXU_FLOPS_PER_S * fill) * (1.15 if nk > 1 else 1.0)
                cost = max(traffic / MM_HBM_BYTES_PER_S, mxu) + (M // tm) * (N // tn) * nk * MM_STEP_S
                if best is None or cost < best[0]:
                    best = (cost, tm, tn, tk, i_outer)
    assert best is not None, (M, K, N)
    return best[1:]


def _mm(a, b, mode, out_dtype, name, *, scale=1.0, res=None, layer=None, deps=()):
    b2 = b.shape[-2:]
    if mode == "nn":
        (M, K), (Kb, N) = a.shape, b2
    elif mode == "nt":
        (M, K), (N, Kb) = a.shape, b2
    else:
        (K, M), (Kb, N) = a.shape, b2
    assert K == Kb, (a.shape, b.shape, mode)
    tm, tn, tk, i_outer = _mm_tiles(M, K, N, a.dtype.itemsize, b.dtype.itemsize, jnp.dtype(out_dtype).itemsize,
                                    res is not None)
    nk = K // tk
    dims = {"nn": (((1,), (0,)), ((), ())), "nt": (((1,), (1,)), ((), ())), "tn": (((0,), (0,)), ((), ()))}[mode]

    def finish(v, r_ref, o_ref):
        if scale != 1.0:
            v = v * scale
        if r_ref is not None:
            v = r_ref[...] + v
        o_ref[...] = v.astype(o_ref.dtype)

    def body(*refs):
        a_ref, b_ref = refs[:2]
        r_ref = refs[2] if res is not None else None
        prod = lambda: lax.dot_general(a_ref[...].astype(BF16), b_ref[...].astype(BF16), dims,
                                       preferred_element_type=F32)
        if nk == 1:
            finish(prod(), r_ref, refs[-1])
            return
        o_ref, acc = refs[-2:]
        k = pl.program_id(2)

        @pl.when(k == 0)
        def _():
            acc[...] = prod()

        @pl.when(k > 0)
        def _():
            acc[...] += prod()

        @pl.when(k == nk - 1)
        def _():
            finish(acc[...], r_ref, o_ref)

    ij = (lambda p, q: (p, q)) if i_outer else (lambda p, q: (q, p))

    def spec(shape, f, lead=None):
        full = lambda p, q, k: f(*ij(p, q), k)
        if lead is None:
            return pl.BlockSpec(shape, full)
        return pl.BlockSpec((None,) + shape, lambda p, q, k: (lead,) + full(p, q, k))

    a_spec = spec((tk, tm), lambda i, j, k: (k, i)) if mode == "tn" else spec((tm, tk), lambda i, j, k: (i, k))
    lead = layer if b.ndim == 3 else None
    b_spec = spec((tn, tk), lambda i, j, k: (j, k), lead) if mode == "nt" else spec((tk, tn), lambda i, j, k: (k, j), lead)
    in_specs = [a_spec, b_spec]
    args = [a, b]
    if res is not None:
        in_specs.append(spec((tm, tn), lambda i, j, k: (i, j)))
        args.append(res)
    in_specs += [ANY_SPEC] * len(deps)
    args += list(deps)
    outer, inner = (M // tm, N // tn) if i_outer else (N // tn, M // tm)
    return pl.pallas_call(
        body, name=name, grid=(outer, inner, nk),
        in_specs=in_specs, out_specs=spec((tm, tn), lambda i, j, k: (i, j)),
        out_shape=jax.ShapeDtypeStruct((M, N), out_dtype),
        scratch_shapes=[pltpu.VMEM((tm, tn), F32)] if nk > 1 else [],
        compiler_params=_params(("parallel", "parallel", "arbitrary")),
    )(*args)


def _cast_into_gathered(w, layer, axis, me, name):
    _, R, C = w.shape
    tr = _pick(R, (512, 256, 128, 64, 32, 16))
    nr = R // tr
    axis, paired = axis

    def body(me_ref, w_ref, o_ref):
        o_ref[...] = w_ref[...].astype(BF16)

    if axis == 1:
        out_idx = lambda i, me_ref: (0, _slot(me_ref[0], paired) * nr + i, 0)
        shape = (1, R * N_DEV, C)
    else:
        out_idx = lambda i, me_ref: (0, i, _slot(me_ref[0], paired))
        shape = (1, R, C * N_DEV)
    return pl.pallas_call(
        body, name=name,
        grid_spec=pltpu.PrefetchScalarGridSpec(
            num_scalar_prefetch=1, grid=(nr,),
            in_specs=[pl.BlockSpec((None, tr, C), lambda i, me_ref: (layer, i, 0))],
            out_specs=pl.BlockSpec((None, tr, C), out_idx)),
        out_shape=jax.ShapeDtypeStruct(shape, BF16), compiler_params=_params(("parallel",)),
    )(me, w)


def _rms_fwd(x, g, name, deps=()):
    T, D = x.shape
    tr = _pick(T, (512, 256, 128))

    def body(x_ref, g_ref, *rest):
        o_ref = rest[-1]
        xv = x_ref[...]
        r = lax.rsqrt(jnp.mean(xv * xv, axis=-1, keepdims=True) + EPS)
        o_ref[...] = ((xv * r) * g_ref[...]).astype(BF16)

    return pl.pallas_call(
        body, name=name, grid=(T // tr,),
        in_specs=[pl.BlockSpec((tr, D), lambda i: (i, 0)), pl.BlockSpec((1, D), lambda i: (0, 0))]
        + [ANY_SPEC] * len(deps),
        out_specs=pl.BlockSpec((tr, D), lambda i: (i, 0)),
        out_shape=jax.ShapeDtypeStruct((T, D), BF16), compiler_params=_params(("parallel",)),
    )(x, g, *deps)


def _rms_bwd(x, g, dh, dres, name, deps=()):
    T, D = x.shape
    tr = _pick(T, (256, 128))

    def body(x_ref, g_ref, dh_ref, dres_ref, *rest):
        dx_ref, dxb_ref, dg_ref = rest[-3:]
        xv = x_ref[...]
        dhv = dh_ref[...]
        r = lax.rsqrt(jnp.mean(xv * xv, axis=-1, keepdims=True) + EPS)
        xhat = xv * r
        dxh = dhv * g_ref[...]
        c = jnp.mean(dxh * xhat, axis=-1, keepdims=True)
        dx = dres_ref[...] + r * (dxh - xhat * c)
        dx_ref[...] = dx
        dxb_ref[...] = dx.astype(BF16)

        @pl.when(pl.program_id(0) == 0)
        def _():
            dg_ref[...] = jnp.zeros_like(dg_ref)

        dg_ref[...] += jnp.sum(dhv * xhat, axis=0, keepdims=True)

    row = pl.BlockSpec((tr, D), lambda i: (i, 0))
    vec = pl.BlockSpec((1, D), lambda i: (0, 0))
    return pl.pallas_call(
        body, name=name, grid=(T // tr,),
        in_specs=[row, vec, row, row] + [ANY_SPEC] * len(deps), out_specs=[row, row, vec],
        out_shape=[jax.ShapeDtypeStruct((T, D), F32), jax.ShapeDtypeStruct((T, D), BF16),
                   jax.ShapeDtypeStruct((1, D), F32)],
        compiler_params=_params(("arbitrary",)),
    )(x, g, dh, dres, *deps)


N_PANEL = N_DEV // 2


def _ffn_in(h, w_in, name):
    T, D = h.shape
    F2 = w_in.shape[2]
    pw = F2 // N_PANEL
    half = pw // 2
    tm = _pick(T, (512, 256, 128))

    def body(h_ref, w_ref, u_ref, a_ref):
        r = jnp.dot(h_ref[...], w_ref[...], preferred_element_type=F32)
        u_ref[...] = r.astype(BF16)
        g, up = r[:, :half], r[:, half:]
        a_ref[...] = (g * jax.nn.sigmoid(g) * up).astype(BF16)

    return pl.pallas_call(
        body, name=name, grid=(N_PANEL, T // tm),
        in_specs=[pl.BlockSpec((tm, D), lambda p, i: (i, 0)), pl.BlockSpec((None, D, pw), lambda p, i: (0, 0, p))],
        out_specs=[pl.BlockSpec((tm, pw), lambda p, i: (i, p)), pl.BlockSpec((tm, half), lambda p, i: (i, p))],
        out_shape=[jax.ShapeDtypeStruct((T, F2), BF16), jax.ShapeDtypeStruct((T, F2 // 2), BF16)],
        compiler_params=_params(("parallel", "parallel")),
    )(h, w_in)


def _ffn_da(dxo, w_out, u, name, deps=()):
    T, D = dxo.shape
    F2 = u.shape[1]
    pw = F2 // N_PANEL
    half = pw // 2
    tm = _pick(T, (512, 256, 128))

    def body(d_ref, w_ref, u_ref, *rest):
        du_ref = rest[-1]
        da = 0.5 * lax.dot_general(d_ref[...], w_ref[...], NT_DIMS, preferred_element_type=F32)
        g = u_ref[:, :half].astype(F32)
        up = u_ref[:, half:].astype(F32)
        sg = jax.nn.sigmoid(g)
        silu = g * sg
        du_ref[:, :half] = (da * up * (sg + silu * (1.0 - sg))).astype(BF16)
        du_ref[:, half:] = (da * silu).astype(BF16)

    return pl.pallas_call(
        body, name=name, grid=(N_PANEL, T // tm),
        in_specs=[pl.BlockSpec((tm, D), lambda p, i: (i, 0)), pl.BlockSpec((None, half, D), lambda p, i: (0, p, 0)),
                  pl.BlockSpec((tm, pw), lambda p, i: (i, p))] + [ANY_SPEC] * len(deps),
        out_specs=pl.BlockSpec((tm, pw), lambda p, i: (i, p)),
        out_shape=jax.ShapeDtypeStruct((T, F2), BF16), compiler_params=_params(("parallel", "parallel")),
    )(dxo, w_out, u, *deps)


def _loss_head(y, t, name):
    T, D = y.shape
    tr = _pick(T, (512, 256, 128))

    def body(y_ref, t_ref, dy_ref, dyb_ref, l_ref):
        e = y_ref[...] - t_ref[...]
        dy = e * (1.0 / D)
        dy_ref[...] = dy
        dyb_ref[...] = dy.astype(BF16)

        @pl.when(pl.program_id(0) == 0)
        def _():
            l_ref[...] = jnp.zeros_like(l_ref)

        l_ref[...] += 0.5 * jnp.sum(jnp.mean(e * e, axis=-1, keepdims=True), axis=0, keepdims=True)

    row = pl.BlockSpec((tr, D), lambda i: (i, 0))
    return pl.pallas_call(
        body, name=name, grid=(T // tr,),
        in_specs=[row, row], out_specs=[row, row, pl.BlockSpec((1, 1), lambda i: (0, 0))],
        out_shape=[jax.ShapeDtypeStruct((T, D), F32), jax.ShapeDtypeStruct((T, D), BF16),
                   jax.ShapeDtypeStruct((1, 1), F32)],
        compiler_params=_params(("arbitrary",)),
    )(y, t)


def _rope_tables(S):
    half = ROPE_DIM // 2
    inv = 1.0 / (ROPE_THETA ** (jnp.arange(0, ROPE_DIM, 2, dtype=F32) / ROPE_DIM))
    ang = jnp.arange(S, dtype=F32)[:, None] * inv[None, :]
    cos, sin = jnp.cos(ang), jnp.sin(ang)
    z = jnp.zeros((S, half), F32)
    z2 = jnp.zeros((S, LANES - ROPE_DIM), F32)
    c = jnp.concatenate([cos, cos, z2], axis=1)
    s1 = jnp.concatenate([-sin, z, z2], axis=1)
    s2 = jnp.concatenate([z, sin, z2], axis=1)
    return c, s1, s2


def _rope(r, c, s1, s2):
    return r * c + pltpu.roll(r, LANES - ROPE_DIM // 2, 1) * s1 + pltpu.roll(r, ROPE_DIM // 2, 1) * s2


def _rope_t(d, c, s1, s2):
    return d * c + pltpu.roll(d * s1, ROPE_DIM // 2, 1) + pltpu.roll(d * s2, LANES - ROPE_DIM // 2, 1)


def _lat_norm_fwd(lat, g_cq, g_ckv, name):
    T = lat.shape[0]
    tr = _pick(T, (512, 256, 128))

    def body(lat_ref, gq_ref, gk_ref, cq_ref, ckv_ref):
        for off, g_ref, o_ref in ((0, gq_ref, cq_ref), (Q_LORA, gk_ref, ckv_ref)):
            xv = lat_ref[:, off:off + Q_LORA]
            r = lax.rsqrt(jnp.mean(xv * xv, axis=-1, keepdims=True) + EPS)
            o_ref[...] = ((xv * r) * g_ref[...]).astype(BF16)

    vec = pl.BlockSpec((1, Q_LORA), lambda i: (0, 0))
    out = pl.BlockSpec((tr, Q_LORA), lambda i: (i, 0))
    return pl.pallas_call(
        body, name=name, grid=(T // tr,),
        in_specs=[pl.BlockSpec((tr, LAT_PAD), lambda i: (i, 0)), vec, vec], out_specs=[out, out],
        out_shape=[jax.ShapeDtypeStruct((T, Q_LORA), BF16)] * 2, compiler_params=_params(("parallel",)),
    )(lat, g_cq, g_ckv)


def _lat_norm_bwd(lat, g_cq, g_ckv, dcq, dckv, dkpe, name):
    T = lat.shape[0]
    tr = _pick(T, (256, 128))

    def body(lat_ref, gq_ref, gk_ref, dcq_ref, dckv_ref, dkpe_ref, dlat_ref, dgq_ref, dgk_ref):
        @pl.when(pl.program_id(0) == 0)
        def _():
            dgq_ref[...] = jnp.zeros_like(dgq_ref)
            dgk_ref[...] = jnp.zeros_like(dgk_ref)

        for off, g_ref, d_ref, dg_ref in ((0, gq_ref, dcq_ref, dgq_ref), (Q_LORA, gk_ref, dckv_ref, dgk_ref)):
            xv = lat_ref[:, off:off + Q_LORA]
            dv = d_ref[...]
            r = lax.rsqrt(jnp.mean(xv * xv, axis=-1, keepdims=True) + EPS)
            xhat = xv * r
            dxh = dv * g_ref[...]
            c = jnp.mean(dxh * xhat, axis=-1, keepdims=True)
            dlat_ref[:, off:off + Q_LORA] = (r * (dxh - xhat * c)).astype(BF16)
            dg_ref[...] += jnp.sum(dv * xhat, axis=0, keepdims=True)
        dlat_ref[:, Q_LORA + KV_LORA:] = dkpe_ref[...].astype(BF16)

    vec = pl.BlockSpec((1, Q_LORA), lambda i: (0, 0))
    half = pl.BlockSpec((tr, Q_LORA), lambda i: (i, 0))
    full = pl.BlockSpec((tr, LAT_PAD), lambda i: (i, 0))
    return pl.pallas_call(
        body, name=name, grid=(T // tr,),
        in_specs=[full, vec, vec, half, half, pl.BlockSpec((tr, LANES), lambda i: (i, 0))],
        out_specs=[full, vec, vec],
        out_shape=[jax.ShapeDtypeStruct((T, LAT_PAD), BF16), jax.ShapeDtypeStruct((1, Q_LORA), F32),
                   jax.ShapeDtypeStruct((1, Q_LORA), F32)],
        compiler_params=_params(("arbitrary",)),
    )(lat, g_cq, g_ckv, dcq, dckv, dkpe)


def _mla_prep_fwd(q_raw, kv, lat, g_qn, g_kn, tabs, name):
    T = q_raw.shape[0]
    H = MLA_HEADS
    tr = _pick(T, (256, 128))
    scale = 1.0 / math.sqrt(QK_DIM)

    def body(q_ref, kv_ref, kpe_ref, gq_ref, gk_ref, c_ref, s1_ref, s2_ref, qf_ref, kf_ref, v_ref):
        c, s1, s2 = c_ref[...], s1_ref[...], s2_ref[...]
        gq, gk = gq_ref[...], gk_ref[...]
        kpe = kpe_ref[...]
        kpe_ss = jnp.sum(kpe * kpe, axis=-1, keepdims=True)
        for h in range(H):
            lo = h * HEAD_PAD
            qa = q_ref[:, lo:lo + LANES]
            qb = q_ref[:, lo + LANES:lo + HEAD_PAD]
            ss = jnp.sum(qa * qa, axis=-1, keepdims=True) + jnp.sum(qb * qb, axis=-1, keepdims=True)
            r = lax.rsqrt(ss * (1.0 / QK_DIM) + EPS)
            qf_ref[:, lo:lo + LANES] = (qa * r * gq[:, :LANES] * scale).astype(BF16)
            qf_ref[:, lo + LANES:lo + HEAD_PAD] = (_rope(qb * r * gq[:, LANES:], c, s1, s2) * scale).astype(BF16)
            ka = kv_ref[:, lo:lo + LANES]
            ss = jnp.sum(ka * ka, axis=-1, keepdims=True) + kpe_ss
            r = lax.rsqrt(ss * (1.0 / QK_DIM) + EPS)
            kf_ref[:, lo:lo + LANES] = (ka * r * gk[:, :LANES]).astype(BF16)
            kf_ref[:, lo + LANES:lo + HEAD_PAD] = _rope(kpe * r * gk[:, LANES:], c, s1, s2).astype(BF16)
            v_ref[:, h * V_DIM:(h + 1) * V_DIM] = kv_ref[:, lo + LANES:lo + HEAD_PAD].astype(BF16)

    wide = pl.BlockSpec((tr, H * HEAD_PAD), lambda i: (i, 0))
    lane = pl.BlockSpec((tr, LANES), lambda i: (i, 0))
    gvec = pl.BlockSpec((1, HEAD_PAD), lambda i: (0, 0))
    return pl.pallas_call(
        body, name=name, grid=(T // tr,),
        in_specs=[wide, wide, pl.BlockSpec((tr, LANES), lambda i: (i, (Q_LORA + KV_LORA) // LANES)), gvec, gvec,
                  lane, lane, lane],
        out_specs=[wide, wide, pl.BlockSpec((tr, H * V_DIM), lambda i: (i, 0))],
        out_shape=[jax.ShapeDtypeStruct((T, H * HEAD_PAD), BF16), jax.ShapeDtypeStruct((T, H * HEAD_PAD), BF16),
                   jax.ShapeDtypeStruct((T, H * V_DIM), BF16)],
        compiler_params=_params(("parallel",)),
    )(q_raw, kv, lat, g_qn, g_kn, *tabs)


def _mla_prep_bwd(q_raw, kv, lat, g_qn, g_kn, tabs, dqf, dkf, dv, name):
    T = q_raw.shape[0]
    H = MLA_HEADS
    tr = _pick(T, (128,))

    def body(q_ref, kv_ref, kpe_ref, gq_ref, gk_ref, c_ref, s1_ref, s2_ref, dqf_ref, dkf_ref, dv_ref,
             dq_ref, dkv_ref, dkpe_ref, dgq_ref, dgk_ref):
        @pl.when(pl.program_id(0) == 0)
        def _():
            dgq_ref[...] = jnp.zeros_like(dgq_ref)
            dgk_ref[...] = jnp.zeros_like(dgk_ref)

        c, s1, s2 = c_ref[...], s1_ref[...], s2_ref[...]
        gq, gk = gq_ref[...], gk_ref[...]
        kpe = kpe_ref[...]
        kpe_ss = jnp.sum(kpe * kpe, axis=-1, keepdims=True)
        dkpe = jnp.zeros_like(kpe)
        dgq_a = jnp.zeros((1, LANES), F32)
        dgq_b = jnp.zeros((1, LANES), F32)
        dgk_a = jnp.zeros((1, LANES), F32)
        dgk_b = jnp.zeros((1, LANES), F32)
        for h in range(H):
            lo = h * HEAD_PAD
            xa = q_ref[:, lo:lo + LANES]
            xb = q_ref[:, lo + LANES:lo + HEAD_PAD]
            ss = jnp.sum(xa * xa, axis=-1, keepdims=True) + jnp.sum(xb * xb, axis=-1, keepdims=True)
            r = lax.rsqrt(ss * (1.0 / QK_DIM) + EPS)
            xa, xb = xa * r, xb * r
            da = dqf_ref[:, lo:lo + LANES]
            db = _rope_t(dqf_ref[:, lo + LANES:lo + HEAD_PAD], c, s1, s2)
            dgq_a += jnp.sum(da * xa, axis=0, keepdims=True)
            dgq_b += jnp.sum(db * xb, axis=0, keepdims=True)
            da, db = da * gq[:, :LANES], db * gq[:, LANES:]
            cc = (jnp.sum(da * xa, axis=-1, keepdims=True) + jnp.sum(db * xb, axis=-1, keepdims=True)) * (1.0 / QK_DIM)
            dq_ref[:, lo:lo + LANES] = (r * (da - xa * cc)).astype(BF16)
            dq_ref[:, lo + LANES:lo + HEAD_PAD] = (r * (db - xb * cc)).astype(BF16)
            xa = kv_ref[:, lo:lo + LANES]
            ss = jnp.sum(xa * xa, axis=-1, keepdims=True) + kpe_ss
            r = lax.rsqrt(ss * (1.0 / QK_DIM) + EPS)
            xa, xb = xa * r, kpe * r
            da = dkf_ref[:, lo:lo + LANES]
            db = _rope_t(dkf_ref[:, lo + LANES:lo + HEAD_PAD], c, s1, s2)
            dgk_a += jnp.sum(da * xa, axis=0, keepdims=True)
            dgk_b += jnp.sum(db * xb, axis=0, keepdims=True)
            da, db = da * gk[:, :LANES], db * gk[:, LANES:]
            cc = (jnp.sum(da * xa, axis=-1, keepdims=True) + jnp.sum(db * xb, axis=-1, keepdims=True)) * (1.0 / QK_DIM)
            dkv_ref[:, lo:lo + LANES] = (r * (da - xa * cc)).astype(BF16)
            dkpe = dkpe + r * (db - xb * cc)
            dkv_ref[:, lo + LANES:lo + HEAD_PAD] = dv_ref[:, h * V_DIM:(h + 1) * V_DIM].astype(BF16)
        dkpe_ref[...] = dkpe
        dgq_ref[:, :LANES] += dgq_a
        dgq_ref[:, LANES:] += dgq_b
        dgk_ref[:, :LANES] += dgk_a
        dgk_ref[:, LANES:] += dgk_b

    wide = pl.BlockSpec((tr, H * HEAD_PAD), lambda i: (i, 0))
    lane = pl.BlockSpec((tr, LANES), lambda i: (i, 0))
    gvec = pl.BlockSpec((1, HEAD_PAD), lambda i: (0, 0))
    vspec = pl.BlockSpec((tr, H * V_DIM), lambda i: (i, 0))
    return pl.pallas_call(
        body, name=name, grid=(T // tr,),
        in_specs=[wide, wide, pl.BlockSpec((tr, LANES), lambda i: (i, (Q_LORA + KV_LORA) // LANES)), gvec, gvec,
                  lane, lane, lane, wide, wide, vspec],
        out_specs=[wide, wide, lane, gvec, gvec],
        out_shape=[jax.ShapeDtypeStruct((T, H * HEAD_PAD), BF16), jax.ShapeDtypeStruct((T, H * HEAD_PAD), BF16),
                   jax.ShapeDtypeStruct((T, LANES), F32), jax.ShapeDtypeStruct((1, HEAD_PAD), F32),
                   jax.ShapeDtypeStruct((1, HEAD_PAD), F32)],
        compiler_params=_params(("arbitrary",)),
    )(q_raw, kv, lat, g_qn, g_kn, *tabs, dqf, dkf, dv)


def _causal_mask(tq, tk):
    return lax.broadcasted_iota(jnp.int32, (tq, tk), 1) <= lax.broadcasted_iota(jnp.int32, (tq, tk), 0)


NT_DIMS = (((1,), (1,)), ((), ()))
TN_DIMS = (((0,), (0,)), ((), ()))


def _flash_fwd(qf, kf, v, name):
    T = qf.shape[0]
    H = MLA_HEADS
    t = _pick(T, (512, 256, 128))
    n = T // t
    pairs = [(i, j) for i in range(n) for j in range(i + 1)]
    qi = jnp.asarray([p[0] for p in pairs], jnp.int32)
    kj = jnp.asarray([p[1] for p in pairs], jnp.int32)

    def body(qi_ref, kj_ref, q_ref, k_ref, v_ref, o_ref, lse_ref, m_sc, l_sc, acc_sc):
        sid = pl.program_id(1)
        i, j = qi_ref[sid], kj_ref[sid]

        @pl.when(j == 0)
        def _():
            m_sc[...] = jnp.full_like(m_sc, NEG_BIG)
            l_sc[...] = jnp.zeros_like(l_sc)
            acc_sc[...] = jnp.zeros_like(acc_sc)

        def step(masked):
            s = lax.dot_general(q_ref[...], k_ref[...], NT_DIMS, preferred_element_type=F32)
            if masked:
                s = jnp.where(_causal_mask(t, t), s, NEG_BIG)
            m_prev = m_sc[:, :1]
            m_new = jnp.maximum(m_prev, jnp.max(s, axis=-1, keepdims=True))
            a = jnp.exp(m_prev - m_new)
            p = jnp.exp(s - m_new)
            l_sc[...] = a * l_sc[...] + jnp.sum(p, axis=-1, keepdims=True)
            acc_sc[...] = a * acc_sc[...] + jnp.dot(p.astype(BF16), v_ref[...], preferred_element_type=F32)
            m_sc[...] = jnp.broadcast_to(m_new, m_sc.shape)

        @pl.when(j < i)
        def _():
            step(False)

        @pl.when(j == i)
        def _():
            step(True)
            o_ref[...] = (acc_sc[...] / l_sc[...]).astype(BF16)
            lse_ref[...] = m_sc[...] + jnp.log(l_sc[...])

    row = pl.BlockSpec((t, V_DIM), lambda h, s, qi, kj: (qi[s], h))
    return pl.pallas_call(
        body, name=name,
        grid_spec=pltpu.PrefetchScalarGridSpec(
            num_scalar_prefetch=2, grid=(H, len(pairs)),
            in_specs=[pl.BlockSpec((t, HEAD_PAD), lambda h, s, qi, kj: (qi[s], h)),
                      pl.BlockSpec((t, HEAD_PAD), lambda h, s, qi, kj: (kj[s], h)),
                      pl.BlockSpec((t, V_DIM), lambda h, s, qi, kj: (kj[s], h))],
            out_specs=[row, row],
            scratch_shapes=[pltpu.VMEM((t, LANES), F32), pltpu.VMEM((t, LANES), F32), pltpu.VMEM((t, V_DIM), F32)]),
        out_shape=[jax.ShapeDtypeStruct((T, H * V_DIM), BF16), jax.ShapeDtypeStruct((T, H * V_DIM), F32)],
        compiler_params=_params(("parallel", "arbitrary")),
    )(qi, kj, qf, kf, v)


def _attn_delta(do, o, name):
    T, W = do.shape
    nh = W // V_DIM
    tr = _pick(T, (512, 256, 128))

    def body(do_ref, o_ref, d_ref):
        for h in range(nh):
            sl = slice(h * V_DIM, (h + 1) * V_DIM)
            d = jnp.sum(do_ref[:, sl].astype(F32) * o_ref[:, sl].astype(F32), axis=-1, keepdims=True)
            d_ref[:, sl] = jnp.broadcast_to(d, (tr, V_DIM))

    row = pl.BlockSpec((tr, W), lambda i: (i, 0))
    return pl.pallas_call(
        body, name=name, grid=(T // tr,), in_specs=[row, row], out_specs=row,
        out_shape=jax.ShapeDtypeStruct((T, W), F32), compiler_params=_params(("parallel",)),
    )(do, o)


def _flash_bwd(qf, kf, v, do, lse, delta, name):
    T = qf.shape[0]
    H = MLA_HEADS
    t = _pick(T, (512, 256, 128))
    n = T // t
    scale = 1.0 / math.sqrt(QK_DIM)
    pairs = [(i, j) for j in range(n) for i in range(j, n)]
    qi = jnp.asarray([p[0] for p in pairs], jnp.int32)
    kj = jnp.asarray([p[1] for p in pairs], jnp.int32)

    def body(qi_ref, kj_ref, q_ref, k_ref, v_ref, do_ref, lse_ref, dl_ref, dq_ref, dk_ref, dv_ref, dk_acc, dv_acc):
        sid = pl.program_id(1)
        i, j = qi_ref[sid], kj_ref[sid]

        @pl.when(sid == 0)
        def _():
            dq_ref[...] = jnp.zeros_like(dq_ref)

        def step(masked):
            s = lax.dot_general(q_ref[...], k_ref[...], NT_DIMS, preferred_element_type=F32)
            if masked:
                s = jnp.where(_causal_mask(t, t), s, NEG_BIG)
            p = jnp.exp(s - lse_ref[:, :1])
            dp = lax.dot_general(do_ref[...], v_ref[...], NT_DIMS, preferred_element_type=F32)
            ds = (p * (dp - dl_ref[:, :1])).astype(BF16)
            dv = lax.dot_general(p.astype(BF16), do_ref[...], TN_DIMS, preferred_element_type=F32)
            dk = lax.dot_general(ds, q_ref[...], TN_DIMS, preferred_element_type=F32)
            if masked:
                dv_acc[...] = dv
                dk_acc[...] = dk
            else:
                dv_acc[...] += dv
                dk_acc[...] += dk
            rows = pl.ds(pl.multiple_of(i * t, t), t)
            dq_ref[rows, :] += jnp.dot(ds, k_ref[...], preferred_element_type=F32) * scale

        @pl.when(i == j)
        def _():
            step(True)

        @pl.when(i > j)
        def _():
            step(False)

        @pl.when(i == n - 1)
        def _():
            dk_ref[...] = dk_acc[...]
            dv_ref[...] = dv_acc[...]

    qs = pl.BlockSpec((t, HEAD_PAD), lambda h, s, qi, kj: (qi[s], h))
    rs = pl.BlockSpec((t, V_DIM), lambda h, s, qi, kj: (qi[s], h))
    ks = pl.BlockSpec((t, HEAD_PAD), lambda h, s, qi, kj: (kj[s], h))
    vs = pl.BlockSpec((t, V_DIM), lambda h, s, qi, kj: (kj[s], h))
    return pl.pallas_call(
        body, name=name,
        grid_spec=pltpu.PrefetchScalarGridSpec(
            num_scalar_prefetch=2, grid=(H, len(pairs)), in_specs=[qs, ks, vs, rs, rs, rs],
            out_specs=[pl.BlockSpec((T, HEAD_PAD), lambda h, s, qi, kj: (0, h)), ks, vs],
            scratch_shapes=[pltpu.VMEM((t, HEAD_PAD), F32), pltpu.VMEM((t, V_DIM), F32)]),
        out_shape=[jax.ShapeDtypeStruct((T, H * HEAD_PAD), F32), jax.ShapeDtypeStruct((T, H * HEAD_PAD), F32),
                   jax.ShapeDtypeStruct((T, H * V_DIM), F32)],
        compiler_params=_params(("parallel", "arbitrary")),
    )(qi, kj, qf, kf, v, do, lse, delta)


def _alibi_slopes():
    tot = DIL_GROUPS * DIL_HEADS
    return [float(np.float32(2.0) ** (np.float32(-8.0) * np.float32(k) / np.float32(tot))) for k in range(1, tot + 1)]


def _dil_masks():
    iq = lax.broadcasted_iota(jnp.int32, (DIL_BLK, DIL_BLK), 0)
    ik = lax.broadcasted_iota(jnp.int32, (DIL_BLK, DIL_BLK), 1)
    return (ik >= iq), (iq + DIL_BLK - ik).astype(F32), (ik <= iq), (iq - ik).astype(F32)


def _dil_norm(x, g):
    r = lax.rsqrt(jnp.mean(x * x, axis=-1, keepdims=True) + EPS)
    return x * r, r


def _dil_fwd(qkv, g_qn, g_kn, slopes, name):
    T = qkv.shape[0]
    GH = DIL_GROUPS * DIL_HEADS
    scale = 1.0 / math.sqrt(DIL_HEAD_DIM)

    def body(sl_ref, q_ref, k_ref, v_ref, gq_ref, gk_ref, o_ref, lse_ref):
        gh = pl.program_id(0)
        slope = sl_ref[gh]
        ok_p, dist_p, ok_c, dist_c = _dil_masks()
        gq, gk = gq_ref[...], gk_ref[...]
        for g, (_, d) in enumerate(DIL_PAIRS):
            @pl.when((gh >= g * DIL_HEADS) & (gh < (g + 1) * DIL_HEADS))
            def _(d=d):
                nb = T // (d * DIL_BLK)
                bias_p = jnp.where(ok_p, -slope * d * dist_p, NEG_BIG)
                bias_c = jnp.where(ok_c, -slope * d * dist_c, NEG_BIG)

                def phase(r, _):
                    def blk(nn, _):
                        def rows(b):
                            return pl.ds(b * (d * DIL_BLK) + r, DIL_BLK, stride=d) if d > 1 else pl.ds(pl.multiple_of(b * DIL_BLK, DIL_BLK), DIL_BLK)
                        cur, prv = rows(nn), rows(jnp.maximum(nn - 1, 0))
                        q = (_dil_norm(q_ref[cur, :], gq)[0] * gq).astype(BF16)
                        kc = (_dil_norm(k_ref[cur, :], gk)[0] * gk).astype(BF16)
                        kp = (_dil_norm(k_ref[prv, :], gk)[0] * gk).astype(BF16)
                        s_c = lax.dot_general(q, kc, NT_DIMS, preferred_element_type=F32) * scale + bias_c
                        s_p = lax.dot_general(q, kp, NT_DIMS, preferred_element_type=F32) * scale + bias_p
                        s_p = jnp.where(nn > 0, s_p, NEG_BIG)
                        m = jnp.maximum(jnp.max(s_c, axis=-1, keepdims=True), jnp.max(s_p, axis=-1, keepdims=True))
                        p_c = jnp.exp(s_c - m)
                        p_p = jnp.exp(s_p - m)
                        l = jnp.sum(p_c, axis=-1, keepdims=True) + jnp.sum(p_p, axis=-1, keepdims=True)
                        acc = jnp.dot(p_c.astype(BF16), v_ref[cur, :].astype(BF16), preferred_element_type=F32)
                        acc += jnp.dot(p_p.astype(BF16), v_ref[prv, :].astype(BF16), preferred_element_type=F32)
                        o_ref[cur, :] = acc / l
                        lse_ref[cur, :] = jnp.broadcast_to(m + jnp.log(l), (DIL_BLK, DIL_HEAD_DIM))
                        return 0
                    lax.fori_loop(0, nb, blk, 0, unroll=min(nb, DIL_UNROLL))
                    return 0
                lax.fori_loop(0, d, phase, 0, unroll=min(d, max(1, DIL_UNROLL // nb)))

    col = lambda off: pl.BlockSpec((T, DIL_HEAD_DIM), lambda gh, sl: (0, gh + off))
    gvec = pl.BlockSpec((1, DIL_HEAD_DIM), lambda gh, sl: (0, 0))
    return pl.pallas_call(
        body, name=name,
        grid_spec=pltpu.PrefetchScalarGridSpec(
            num_scalar_prefetch=1, grid=(GH,),
            in_specs=[col(0), col(GH), col(2 * GH), gvec, gvec], out_specs=[col(0), col(0)]),
        out_shape=[jax.ShapeDtypeStruct((T, GH * DIL_HEAD_DIM), F32)] * 2,
        compiler_params=_params(("parallel",)),
    )(slopes, qkv, qkv, qkv, g_qn, g_kn)


def _dil_merge(o_g, lse_g, name):
    T = o_g.shape[0]
    W = DIL_HEADS * DIL_HEAD_DIM
    tr = _pick(T, (256, 128))

    def body(o0, o1, o2, l0, l1, l2, o_ref, lse_ref):
        a, b, c = l0[...], l1[...], l2[...]
        m = jnp.maximum(jnp.maximum(a, b), c)
        ea, eb, ec = jnp.exp(a - m), jnp.exp(b - m), jnp.exp(c - m)
        tot = ea + eb + ec
        o_ref[...] = ((o0[...] * ea + o1[...] * eb + o2[...] * ec) / tot).astype(BF16)
        lse_ref[...] = m + jnp.log(tot)

    grp = lambda g: pl.BlockSpec((tr, W), lambda i: (i, g))
    out = pl.BlockSpec((tr, W), lambda i: (i, 0))
    return pl.pallas_call(
        body, name=name, grid=(T // tr,),
        in_specs=[grp(0), grp(1), grp(2), grp(0), grp(1), grp(2)], out_specs=[out, out],
        out_shape=[jax.ShapeDtypeStruct((T, W), BF16), jax.ShapeDtypeStruct((T, W), F32)],
        compiler_params=_params(("parallel",)),
    )(o_g, o_g, o_g, lse_g, lse_g, lse_g)


def _dil_bwd(qkv, g_qn, g_kn, slopes, do, delta, lse, name):
    T = qkv.shape[0]
    GH = DIL_GROUPS * DIL_HEADS
    scale = 1.0 / math.sqrt(DIL_HEAD_DIM)
    nchunk = T // DIL_BLK

    def body(sl_ref, q_ref, k_ref, v_ref, gq_ref, gk_ref, do_ref, dl_ref, lse_ref,
             dq_ref, dk_ref, dv_ref, dgq_ref, dgk_ref, dq_acc, dk_acc, dv_acc):
        gh = pl.program_id(0)
        slope = sl_ref[gh]
        ok_p, dist_p, ok_c, dist_c = _dil_masks()
        gq, gk = gq_ref[...], gk_ref[...]

        @pl.when(gh == 0)
        def _():
            dgq_ref[...] = jnp.zeros_like(dgq_ref)
            dgk_ref[...] = jnp.zeros_like(dgk_ref)

        dk_acc[...] = jnp.zeros_like(dk_acc)
        dv_acc[...] = jnp.zeros_like(dv_acc)
        for g, (_, d) in enumerate(DIL_PAIRS):
            @pl.when((gh >= g * DIL_HEADS) & (gh < (g + 1) * DIL_HEADS))
            def _(d=d):
                nb = T // (d * DIL_BLK)
                bias_p = jnp.where(ok_p, -slope * d * dist_p, NEG_BIG)
                bias_c = jnp.where(ok_c, -slope * d * dist_c, NEG_BIG)

                def phase(r, _):
                    def blk(nn, _):
                        def rows(b):
                            return pl.ds(b * (d * DIL_BLK) + r, DIL_BLK, stride=d) if d > 1 else pl.ds(pl.multiple_of(b * DIL_BLK, DIL_BLK), DIL_BLK)
                        cur, prv = rows(nn), rows(jnp.maximum(nn - 1, 0))
                        q = (_dil_norm(q_ref[cur, :], gq)[0] * gq).astype(BF16)
                        kc = (_dil_norm(k_ref[cur, :], gk)[0] * gk).astype(BF16)
                        kp = (_dil_norm(k_ref[prv, :], gk)[0] * gk).astype(BF16)
                        vc = v_ref[cur, :].astype(BF16)
                        vp = v_ref[prv, :].astype(BF16)
                        dob = do_ref[cur, :].astype(BF16)
                        delta = dl_ref[cur, :][:, :1]
                        ls = lse_ref[cur, :][:, :1]
                        s_c = lax.dot_general(q, kc, NT_DIMS, preferred_element_type=F32) * scale + bias_c
                        s_p = lax.dot_general(q, kp, NT_DIMS, preferred_element_type=F32) * scale + bias_p
                        s_p = jnp.where(nn > 0, s_p, NEG_BIG)
                        p_c = jnp.exp(s_c - ls)
                        p_p = jnp.exp(s_p - ls)
                        dp_c = lax.dot_general(dob, vc, NT_DIMS, preferred_element_type=F32)
                        dp_p = lax.dot_general(dob, vp, NT_DIMS, preferred_element_type=F32)
                        ds_c = (p_c * (dp_c - delta)).astype(BF16)
                        ds_p = (p_p * (dp_p - delta)).astype(BF16)
                        dq_acc[cur, :] = (jnp.dot(ds_c, kc, preferred_element_type=F32)
                                          + jnp.dot(ds_p, kp, preferred_element_type=F32)) * scale
                        dk_acc[cur, :] += lax.dot_general(ds_c, q, TN_DIMS, preferred_element_type=F32) * scale
                        dv_acc[cur, :] += lax.dot_general(p_c.astype(BF16), dob, TN_DIMS, preferred_element_type=F32)
                        dk_acc[prv, :] += lax.dot_general(ds_p, q, TN_DIMS, preferred_element_type=F32) * scale
                        dv_acc[prv, :] += lax.dot_general(p_p.astype(BF16), dob, TN_DIMS, preferred_element_type=F32)
                        return 0
                    lax.fori_loop(0, nb, blk, 0, unroll=min(nb, DIL_UNROLL))
                    return 0
                lax.fori_loop(0, d, phase, 0, unroll=min(d, max(1, DIL_UNROLL // nb)))

        def fin(ci, carry):
            dgq, dgk = carry
            rows = pl.ds(pl.multiple_of(ci * DIL_BLK, DIL_BLK), DIL_BLK)
            outs = []
            for x_ref, d_acc, gvec in ((q_ref, dq_acc, gq), (k_ref, dk_acc, gk)):
                xhat, r = _dil_norm(x_ref[rows, :], gvec)
                dn = d_acc[rows, :]
                dxh = dn * gvec
                c = jnp.mean(dxh * xhat, axis=-1, keepdims=True)
                outs.append(((r * (dxh - xhat * c)).astype(BF16), jnp.sum(dn * xhat, axis=0, keepdims=True)))
            dq_ref[rows, :] = outs[0][0]
            dk_ref[rows, :] = outs[1][0]
            dv_ref[rows, :] = dv_acc[rows, :].astype(BF16)
            return dgq + outs[0][1], dgk + outs[1][1]

        z = jnp.zeros((1, DIL_HEAD_DIM), F32)
        dgq, dgk = lax.fori_loop(0, nchunk, fin, (z, z))
        dgq_ref[...] += dgq
        dgk_ref[...] += dgk

    col = lambda off: pl.BlockSpec((T, DIL_HEAD_DIM), lambda gh, sl: (0, gh + off))
    hcol = pl.BlockSpec((T, DIL_HEAD_DIM), lambda gh, sl: (0, gh % DIL_HEADS))
    gvec = pl.BlockSpec((1, DIL_HEAD_DIM), lambda gh, sl: (0, 0))
    wide = jax.ShapeDtypeStruct((T, GH * DIL_HEAD_DIM), BF16)
    vec = jax.ShapeDtypeStruct((1, DIL_HEAD_DIM), F32)
    return pl.pallas_call(
        body, name=name,
        grid_spec=pltpu.PrefetchScalarGridSpec(
            num_scalar_prefetch=1, grid=(GH,),
            in_specs=[col(0), col(GH), col(2 * GH), gvec, gvec, hcol, hcol, hcol],
            out_specs=[col(0), col(0), col(0), gvec, gvec],
            scratch_shapes=[pltpu.VMEM((T, DIL_HEAD_DIM), F32)] * 3),
        out_shape=[wide, wide, wide, vec, vec],
        compiler_params=_params(("arbitrary",)),
    )(slopes, qkv, qkv, qkv, g_qn, g_kn, do, delta, lse)


def _my_pos():
    return lax.axis_index("x"), lax.axis_index("y"), lax.axis_index("c")


def _peer(pos, j):
    x, y, c = pos
    px = 1 - x if j & 4 else x
    py = 1 - y if j & 2 else y
    pc = 1 - c if j & 1 else c
    return (px, py, pc), 4 * px + 2 * py + pc


def _slot(idx, paired):
    if not paired:
        return idx
    return jnp.where(idx < N_DEV // 2, 2 * idx, 2 * idx - (N_DEV - 1))


def _shard_slice(ref, axis, idx, size, paired=False):
    sl = [slice(None)] * len(ref.shape)
    sl[axis] = pl.ds(pl.multiple_of(_slot(idx, paired) * size, 8), size)
    return ref.at[tuple(sl)]


HBM_SPEC = pl.BlockSpec(memory_space=pltpu.HBM)
SEM_SPEC = pl.BlockSpec(memory_space=pltpu.SEMAPHORE)
DATAFLOW = pltpu.SideEffectType.DATAFLOW_SIDE_EFFECTING
N_PEER = N_DEV - 1


def _scatter_copy(axis, grad, slots, frm, to, dev, send_sem, recv_sem):
    ax, paired = axis
    src = _shard_slice(grad, ax, to, grad.shape[ax] // N_DEV, paired)
    return pltpu.make_async_remote_copy(src_ref=src, dst_ref=slots.at[frm], send_sem=send_sem, recv_sem=recv_sem,
                                        device_id=dev, device_id_type=MESH)


def _scatter_start(grads, axes, name):
    n = len(grads)

    def body(*refs):
        outs = refs[2 * n:]
        send, recv, token = outs[:n], outs[n:2 * n], outs[4 * n]
        pos = _my_pos()
        me = 4 * pos[0] + 2 * pos[1] + pos[2]
        for a in range(n):
            for j in range(1, N_DEV):
                dev, pid = _peer(pos, j)
                _scatter_copy(axes[a], refs[2 * a], refs[2 * a + 1], me, pid, dev, send[a].at[j - 1],
                              recv[a].at[j - 1]).start()
        token[...] = jnp.zeros_like(token)

    ops = []
    for g, (ax, _) in zip(grads, axes):
        shp = list(g.shape)
        shp[ax] //= N_DEV
        ops += [g, lax.empty((N_DEV,) + tuple(shp), g.dtype)]
    sems = [pltpu.SemaphoreType.DMA((N_PEER,))] * (2 * n)
    res = pl.pallas_call(
        body, name=name,
        out_shape=sems + [pltpu.HBM(o.shape, o.dtype) for o in ops] + [jax.ShapeDtypeStruct((8, LANES), F32)],
        in_specs=[HBM_SPEC] * len(ops),
        out_specs=[SEM_SPEC] * (2 * n) + [HBM_SPEC] * len(ops) + [pl.BlockSpec(memory_space=pltpu.VMEM)],
        input_output_aliases={i: 2 * n + i for i in range(len(ops))},
        compiler_params=pltpu.CompilerParams(has_side_effects=DATAFLOW),
    )(*[pltpu.with_memory_space_constraint(o, pltpu.HBM) for o in ops])
    items = [(res[a], res[n + a], res[2 * n + 2 * a], res[2 * n + 2 * a + 1]) for a in range(n)]
    return items, res[4 * n]


def _scatter_wait(items, axes, after, name):
    n = len(items)

    def body(*refs):
        send, recv = refs[2 * n:3 * n], refs[3 * n:4 * n]
        pos = _my_pos()
        me = 4 * pos[0] + 2 * pos[1] + pos[2]
        for a in range(n):
            for j in range(1, N_DEV):
                dev, pid = _peer(pos, j)
                cp = _scatter_copy(axes[a], refs[2 * a], refs[2 * a + 1], pid, me, dev, send[a].at[j - 1],
                                   recv[a].at[j - 1])
                cp.wait_send()
                cp.wait_recv()

    ops = [b for it in items for b in it[2:]]
    res = pl.pallas_call(
        body, name=name,
        out_shape=[pltpu.HBM(o.shape, o.dtype) for o in ops],
        in_specs=[HBM_SPEC] * len(ops) + [SEM_SPEC] * (2 * n) + [ANY_SPEC],
        out_specs=[HBM_SPEC] * len(ops),
        input_output_aliases={i: i for i in range(len(ops))},
        compiler_params=pltpu.CompilerParams(has_side_effects=DATAFLOW),
    )(*ops, *[it[0] for it in items], *[it[1] for it in items], after)
    return [res[2 * a + 1] for a in range(n)]


SIBLING = 1
ICI_PEERS = (2, 4, 6)


def _gather_copy(buf, axis, shard, dev, send_sem, recv_sem):
    ax, paired = axis
    piece = _shard_slice(buf, ax, shard, buf.shape[ax] // N_DEV, paired)
    return pltpu.make_async_remote_copy(src_ref=piece, dst_ref=piece, send_sem=send_sem, recv_sem=recv_sem,
                                        device_id=dev, device_id_type=MESH)


def _gather_start(bufs, axes, name):
    n = len(bufs)

    def body(*refs):
        ins, outs = refs[:n], refs[n:]
        send, r_sib, r_ici, token = outs[:n], outs[n:2 * n], outs[2 * n:3 * n], outs[4 * n]
        pos = _my_pos()
        me = 4 * pos[0] + 2 * pos[1] + pos[2]
        for a in range(n):
            dev, _ = _peer(pos, SIBLING)
            _gather_copy(ins[a], axes[a], me, dev, send[a].at[0], r_sib[a].at[0]).start()
            for k, j in enumerate(ICI_PEERS):
                dev, _ = _peer(pos, j)
                _gather_copy(ins[a], axes[a], me, dev, send[a].at[1 + k], r_ici[a].at[k]).start()
        token[...] = jnp.zeros_like(token)

    sems = ([pltpu.SemaphoreType.DMA((1 + len(ICI_PEERS),))] * n + [pltpu.SemaphoreType.DMA((1,))] * n
            + [pltpu.SemaphoreType.DMA((len(ICI_PEERS),))] * n)
    res = pl.pallas_call(
        body, name=name,
        out_shape=sems + [pltpu.HBM(b.shape, b.dtype) for b in bufs] + [jax.ShapeDtypeStruct((8, LANES), F32)],
        in_specs=[HBM_SPEC] * n,
        out_specs=[SEM_SPEC] * (3 * n) + [HBM_SPEC] * n + [pl.BlockSpec(memory_space=pltpu.VMEM)],
        input_output_aliases={i: 3 * n + i for i in range(n)},
        compiler_params=pltpu.CompilerParams(has_side_effects=DATAFLOW),
    )(*[pltpu.with_memory_space_constraint(b, pltpu.HBM) for b in bufs])
    items = [dict(send=res[a], r_sib=res[n + a], r_ici=res[2 * n + a], buf=res[3 * n + a]) for a in range(n)]
    return items, res[4 * n]


def _gather_relay(items, axes, after, name):
    n = len(items)

    def body(*refs):
        ins, r_ici = refs[:n], refs[n:2 * n]
        outs = refs[2 * n + 1:]
        s_rel, r_rel, token = outs[:n], outs[n:2 * n], outs[3 * n]
        pos = _my_pos()
        sib, _ = _peer(pos, SIBLING)
        for a in range(n):
            for k, j in enumerate(ICI_PEERS):
                dev, pid = _peer(pos, j)
                _gather_copy(ins[a], axes[a], pid, dev, s_rel[a].at[k], r_ici[a].at[k]).wait_recv()
                _gather_copy(ins[a], axes[a], pid, sib, s_rel[a].at[k], r_rel[a].at[k]).start()
        token[...] = jnp.zeros_like(token)

    bufs = [it["buf"] for it in items]
    sems = [pltpu.SemaphoreType.DMA((len(ICI_PEERS),))] * (2 * n)
    res = pl.pallas_call(
        body, name=name,
        out_shape=sems + [pltpu.HBM(b.shape, b.dtype) for b in bufs] + [jax.ShapeDtypeStruct((8, LANES), F32)],
        in_specs=[HBM_SPEC] * n + [SEM_SPEC] * n + [ANY_SPEC],
        out_specs=[SEM_SPEC] * (2 * n) + [HBM_SPEC] * n + [pl.BlockSpec(memory_space=pltpu.VMEM)],
        input_output_aliases={i: 2 * n + i for i in range(n)},
        compiler_params=pltpu.CompilerParams(has_side_effects=DATAFLOW),
    )(*bufs, *[it["r_ici"] for it in items], after)
    out = [dict(send=it["send"], r_sib=it["r_sib"], s_rel=res[a], r_rel=res[n + a], buf=res[2 * n + a])
           for a, it in enumerate(items)]
    return out, res[3 * n]


def _gather_wait(items, axes, after, name):
    n = len(items)

    def body(*refs):
        ins = refs[:n]
        send, r_sib, s_rel, r_rel = (refs[(1 + q) * n:(2 + q) * n] for q in range(4))
        pos = _my_pos()
        me = 4 * pos[0] + 2 * pos[1] + pos[2]
        sib, sib_id = _peer(pos, SIBLING)
        for a in range(n):
            for k in range(1 + len(ICI_PEERS)):
                _gather_copy(ins[a], axes[a], me, sib, send[a].at[k], r_sib[a].at[0]).wait_send()
            _gather_copy(ins[a], axes[a], sib_id, sib, send[a].at[0], r_sib[a].at[0]).wait_recv()
            for k, j in enumerate(ICI_PEERS):
                _, pid = _peer(pos, j)
                _, far = _peer(pos, j ^ SIBLING)
                _gather_copy(ins[a], axes[a], pid, sib, s_rel[a].at[k], r_rel[a].at[k]).wait_send()
                _gather_copy(ins[a], axes[a], far, sib, s_rel[a].at[k], r_rel[a].at[k]).wait_recv()

    bufs = [it["buf"] for it in items]
    res = pl.pallas_call(
        body, name=name,
        out_shape=[pltpu.HBM(b.shape, b.dtype) for b in bufs],
        in_specs=[HBM_SPEC] * n + [SEM_SPEC] * (4 * n) + [ANY_SPEC],
        out_specs=[HBM_SPEC] * n,
        input_output_aliases={i: i for i in range(n)},
        compiler_params=pltpu.CompilerParams(has_side_effects=DATAFLOW),
    )(*bufs, *[it["send"] for it in items], *[it["r_sib"] for it in items], *[it["s_rel"] for it in items],
      *[it["r_rel"] for it in items], after)
    return list(res)


def _gain_allreduce(v, name):
    n = v.shape[1]

    def body(v_ref, o_ref, slots, send_sems, recv_sems):
        pos = _my_pos()
        me = 4 * pos[0] + 2 * pos[1] + pos[2]
        slots[me] = v_ref[...]
        copies = []
        for j in range(1, N_DEV):
            dev, _ = _peer(pos, j)
            cp = pltpu.make_async_remote_copy(
                src_ref=slots.at[me], dst_ref=slots.at[me], send_sem=send_sems.at[j], recv_sem=recv_sems.at[j],
                device_id=dev, device_id_type=MESH)
            cp.start()
            copies.append(cp)
        for j in range(1, N_DEV):
            dev, pid = _peer(pos, j)
            pltpu.make_async_remote_copy(
                src_ref=slots.at[me], dst_ref=slots.at[pid], send_sem=send_sems.at[j], recv_sem=recv_sems.at[j],
                device_id=dev, device_id_type=MESH).wait_recv()
        for cp in copies:
            cp.wait_send()
        acc = slots[0]
        for s in range(1, N_DEV):
            acc = acc + slots[s]
        o_ref[...] = acc

    return pl.pallas_call(
        body, name=name, out_shape=jax.ShapeDtypeStruct((1, n), F32),
        in_specs=[pl.BlockSpec(memory_space=pltpu.VMEM)], out_specs=pl.BlockSpec(memory_space=pltpu.VMEM),
        scratch_shapes=[pltpu.VMEM((N_DEV, 1, n), F32), pltpu.SemaphoreType.DMA((N_DEV,)),
                        pltpu.SemaphoreType.DMA((N_DEV,))],
        compiler_params=pltpu.CompilerParams(has_side_effects=True),
    )(v)


def _adamw(parts, own, me, w, m, v, layer, prev, name):
    L, R, C = w.shape
    P = parts.shape[0]
    tr = _pick(R, (128, 64, 32, 16, 8, 1))
    c1 = 1.0 - ADAM_B1 ** ADAM_STEP
    c2 = 1.0 - ADAM_B2 ** ADAM_STEP
    n_in = 4 if own is None else 5

    def body(me_ref, *refs):
        p_ref = refs[0]
        w_ref, m_ref, v_ref = refs[n_in - 3:n_in]
        g_out, d_out, m_out, v_out, tok = refs[-5:]
        g = None
        for s in range(P):
            part = p_ref[s]
            if own is not None:
                part = jnp.where(me_ref[0] == s, refs[1][...], part)
            g = part.astype(F32) if g is None else g + part.astype(F32)
        mn = ADAM_B1 * m_ref[...] + (1.0 - ADAM_B1) * g
        vn = ADAM_B2 * v_ref[...] + (1.0 - ADAM_B2) * (g * g)
        g_out[...] = g
        m_out[...] = mn
        v_out[...] = vn
        d_out[...] = -ADAM_LR * ((mn / c1) / (jnp.sqrt(vn / c2) + ADAM_EPS) + ADAM_WD * w_ref[...])
        tok[...] = jnp.zeros_like(tok)

    row = pl.BlockSpec((None, tr, C), lambda i, me_ref: (layer, i, 0))
    in_specs = [pl.BlockSpec((P, tr, C), lambda i, me_ref: (0, i, 0))]
    args = [parts]
    if own is not None:
        in_specs.append(pl.BlockSpec((tr, C), lambda i, me_ref: (i, 0)))
        args.append(own)
    in_specs += [row, row, row]
    args += [w, m, v]
    aliases = {}
    if prev is not None:
        in_specs += [ANY_SPEC] * 4
        aliases = {1 + len(args) + k: k for k in range(4)}
        args += list(prev)
    shp = jax.ShapeDtypeStruct((L, R, C), F32)
    res = pl.pallas_call(
        body, name=name,
        grid_spec=pltpu.PrefetchScalarGridSpec(
            num_scalar_prefetch=1, grid=(R // tr,), in_specs=in_specs,
            out_specs=[row] * 4 + [pl.BlockSpec((8, LANES), lambda i, me_ref: (0, 0))]),
        out_shape=[shp] * 4 + [jax.ShapeDtypeStruct((8, LANES), F32)],
        input_output_aliases=aliases, compiler_params=_params(("arbitrary",)),
    )(me, *args)
    return res[:4], res[4]


def _pad_heads(w):
    lead = w.shape[:-1]
    n = w.shape[-1] // QK_DIM
    w = w.reshape(lead + (n, QK_DIM))
    w = jnp.pad(w, [(0, 0)] * len(lead) + [(0, 0), (0, HEAD_PAD - QK_DIM)])
    return w.reshape(lead + (n * HEAD_PAD,))


def _unpad_heads(w):
    lead = w.shape[:-1]
    n = w.shape[-1] // HEAD_PAD
    return w.reshape(lead + (n, HEAD_PAD))[..., :QK_DIM].reshape(lead + (n * QK_DIM,))


def kernel(x, ffn1_norm, ffn1_w_in, ffn1_w_out, mix_norm, ffn2_norm, ffn2_w_in, ffn2_w_out, mla_w_down, mla_g_cq, mla_g_ckv, mla_w_uq, mla_w_ukv, mla_g_qn, mla_g_kn, mla_w_o, dil_w_qkv, dil_g_qn, dil_g_kn, dil_w_o, loss_target, m_ffn1_norm, m_ffn1_w_in, m_ffn1_w_out, m_mix_norm, m_ffn2_norm, m_ffn2_w_in, m_ffn2_w_out, m_mla_w_down, m_mla_g_cq, m_mla_g_ckv, m_mla_w_uq, m_mla_w_ukv, m_mla_g_qn, m_mla_g_kn, m_mla_w_o, m_dil_w_qkv, m_dil_g_qn, m_dil_g_kn, m_dil_w_o, v_ffn1_norm, v_ffn1_w_in, v_ffn1_w_out, v_mix_norm, v_ffn2_norm, v_ffn2_w_in, v_ffn2_w_out, v_mla_w_down, v_mla_g_cq, v_mla_g_ckv, v_mla_w_uq, v_mla_w_ukv, v_mla_g_qn, v_mla_g_kn, v_mla_w_o, v_dil_w_qkv, v_dil_g_qn, v_dil_g_kn, v_dil_w_o):
    names = ["ffn1_norm", "ffn1_w_in", "ffn1_w_out", "mix_norm", "ffn2_norm", "ffn2_w_in", "ffn2_w_out", "mla_w_down",
             "mla_g_cq", "mla_g_ckv", "mla_w_uq", "mla_w_ukv", "mla_g_qn", "mla_g_kn", "mla_w_o", "dil_w_qkv",
             "dil_g_qn", "dil_g_kn", "dil_w_o"]
    W = dict(zip(names, [ffn1_norm, ffn1_w_in, ffn1_w_out, mix_norm, ffn2_norm, ffn2_w_in, ffn2_w_out, mla_w_down,
                         mla_g_cq, mla_g_ckv, mla_w_uq, mla_w_ukv, mla_g_qn, mla_g_kn, mla_w_o, dil_w_qkv,
                         dil_g_qn, dil_g_kn, dil_w_o]))
    M1 = dict(zip(names, [m_ffn1_norm, m_ffn1_w_in, m_ffn1_w_out, m_mix_norm, m_ffn2_norm, m_ffn2_w_in, m_ffn2_w_out,
                          m_mla_w_down, m_mla_g_cq, m_mla_g_ckv, m_mla_w_uq, m_mla_w_ukv, m_mla_g_qn, m_mla_g_kn,
                          m_mla_w_o, m_dil_w_qkv, m_dil_g_qn, m_dil_g_kn, m_dil_w_o]))
    V2 = dict(zip(names, [v_ffn1_norm, v_ffn1_w_in, v_ffn1_w_out, v_mix_norm, v_ffn2_norm, v_ffn2_w_in, v_ffn2_w_out,
                          v_mla_w_down, v_mla_g_cq, v_mla_g_ckv, v_mla_w_uq, v_mla_w_ukv, v_mla_g_qn, v_mla_g_kn,
                          v_mla_w_o, v_dil_w_qkv, v_dil_g_qn, v_dil_g_kn, v_dil_w_o]))
    S, D = x.shape[1], x.shape[2]
    x0 = x.reshape(S, D)
    tgt = loss_target.reshape(S, D)

    big = ["ffn1_w_in", "ffn1_w_out", "ffn2_w_in", "ffn2_w_out", "mla_w_down", "mla_w_uq", "mla_w_ukv", "mla_w_o",
           "dil_w_qkv", "dil_w_o"]
    shard_dim = {"ffn1_w_in": 2, "ffn1_w_out": 1, "ffn2_w_in": 2, "ffn2_w_out": 1, "mla_w_down": 1, "mla_w_uq": 2,
                 "mla_w_ukv": 2, "mla_w_o": 1, "dil_w_qkv": 2, "dil_w_o": 2}
    paired = ("ffn1_w_in", "ffn2_w_in")
    shard_axis = {n: (d, n in paired) for n, d in shard_dim.items()}
    grad_axis = {n: (d - 1, n in paired) for n, d in shard_dim.items()}

    def padded(n, w):
        if n == "mla_w_down":
            return jnp.pad(w, ((0, 0), (0, 0), (0, LAT_PAD - w.shape[2])))
        if n == "mla_w_uq":
            return _pad_heads(w)
        return w

    depth = ffn1_norm.shape[0]
    blocks = []
    for l in range(depth):
        mixer = (["mla_w_down", "mla_w_uq", "mla_w_ukv", "mla_w_o"] if l % 2 == 0 else ["dil_w_qkv", "dil_w_o"])
        blocks.append((f"ffn1_{l}", [("ffn1_w_in", l), ("ffn1_w_out", l)]))
        blocks.append((f"mix_{l}", [(n, l // 2) for n in mixer]))
        blocks.append((f"ffn2_{l}", [("ffn2_w_in", l), ("ffn2_w_out", l)]))
    order = [k for _, keys in blocks for k in keys]
    me = (4 * lax.axis_index("x") + 2 * lax.axis_index("y") + lax.axis_index("c")).astype(jnp.int32).reshape(1)
    mine = [_cast_into_gathered(padded(n, W[n]), l, shard_axis[n], me, f"cast_{n}_{l}") for n, l in order]
    ag_items, ag_token = _gather_start(mine, [shard_axis[n] for n, _ in order], "gather_start")
    ag_items = dict(zip(order, ag_items))
    full = {}

    def relay(keys, after, tag):
        out, token = _gather_relay([ag_items[k] for k in keys], [shard_axis[k[0]] for k in keys], after,
                                   f"gather_relay_{tag}")
        ag_items.update(zip(keys, out))
        return [token]

    def relay_next(bi, after):
        return relay(blocks[bi + 1][1], after, blocks[bi + 1][0]) if bi + 1 < len(blocks) else []

    def fetch(keys, after, tag):
        lands = _gather_wait([ag_items[k] for k in keys], [shard_axis[k[0]] for k in keys], after,
                             f"gather_wait_{tag}")
        full.update(zip(keys, lands))

    g_qn = _pad_heads(mla_g_qn)
    g_kn = _pad_heads(mla_g_kn)
    tabs = _rope_tables(S)
    slopes = jnp.asarray(_alibi_slopes(), F32)

    grads = {}
    gain_g = {}

    out_g, out_d, out_m, out_v = {}, {}, {}, {}
    pending = []
    lag = 3

    def scatter_start(tag, keys):
        items, token = _scatter_start([grads[k] for k in keys], [grad_axis[k[0]] for k in keys],
                                      f"scatter_start_{tag}")
        pending.append((tag, keys, items))
        return token

    def scatter_finish(after):
        tag, keys, items = pending.pop(0)
        lands = _scatter_wait(items, [grad_axis[k[0]] for k in keys], after, f"scatter_wait_{tag}")
        tokens = []
        for (n, l), p in zip(keys, lands):
            ax, pair = grad_axis[n]
            size = grads[(n, l)].shape[ax] // N_DEV
            own = lax.dynamic_slice_in_dim(grads[(n, l)], _slot(me[0], pair) * size, size, axis=ax)
            if n == "mla_w_down":
                p, own = p[..., :W[n].shape[2]], own[..., :W[n].shape[2]]
            elif n == "mla_w_uq":
                p, own = _unpad_heads(p), _unpad_heads(own)
            prev = (out_g[n], out_d[n], out_m[n], out_v[n]) if n in out_g else None
            (out_g[n], out_d[n], out_m[n], out_v[n]), tok = _adamw(p, own, me, W[n], M1[n], V2[n], l, prev,
                                                                    f"adamw_{n}_{l}")
            tokens.append(tok)
        return tokens

    def finish_due(after):
        tokens = []
        while len(pending) > lag:
            tokens += scatter_finish(after)
        return tokens

    def ffn_fwd(xin, norm_row, which, l, bi, deps=()):
        tag = blocks[bi][0]
        k_in, k_out = (which + "_w_in", l), (which + "_w_out", l)
        h = _rms_fwd(xin, norm_row, f"rms_fwd_{tag}", deps=deps)
        if bi == 0:
            relay([k_in], h, f"{tag}_in")
        fetch([k_in], h, f"in_{tag}")
        u, a = _ffn_in(h, full[k_in], f"ffn_in_{tag}")
        if bi == 0:
            relay([k_out], a, f"{tag}_out")
        fetch([k_out], a, f"out_{tag}")
        toks = relay_next(bi, a)
        xo = _mm(a, full[k_out], "nn", F32, f"mm_out_{tag}", scale=0.5, res=xin, layer=0, deps=toks)
        return xo, (xin, h, u, a)

    def ffn_bwd(dx_pair, saved, norm_row, which, l, tag):
        dxo, dxob = dx_pair
        k_in, k_out = (which + "_w_in", l), (which + "_w_out", l)
        xin, h, u, a = saved
        grads[k_out] = _mm(a, dxob, "tn", BF16, f"mm_dwout_{tag}", scale=0.5)
        t_out = scatter_start(f"{tag}_out", [k_out])
        du = _ffn_da(dxob, full[k_out], u, f"ffn_da_{tag}", deps=[t_out])
        grads[k_in] = _mm(h, du, "tn", BF16, f"mm_dwin_{tag}")
        t_in = scatter_start(f"{tag}_in", [k_in])
        dh = _mm(du, full[k_in], "nt", F32, f"mm_dh_{tag}", layer=0, deps=[t_in])
        toks = finish_due(dh)
        dx, dxb, dg = _rms_bwd(xin, norm_row, dh, dxo, f"rms_bwd_{tag}", deps=toks)
        gain_g.setdefault(which + "_norm", {})[l] = dg
        return dx, dxb

    def mla_fwd(xin, l, bi):
        j = l // 2
        xn = _rms_fwd(xin, mix_norm[l:l + 1], "rms_fwd_mla")
        fetch([(n, j) for n in ("mla_w_down", "mla_w_uq", "mla_w_ukv", "mla_w_o")], xn, "mla")
        lat = _mm(xn, full[("mla_w_down", j)], "nn", F32, "mm_lat", layer=0)
        cq, ckv = _lat_norm_fwd(lat, mla_g_cq[j:j + 1], mla_g_ckv[j:j + 1], "lat_norm_fwd")
        q_raw = _mm(cq, full[("mla_w_uq", j)], "nn", F32, "mm_uq", layer=0)
        kv = _mm(ckv, full[("mla_w_ukv", j)], "nn", F32, "mm_ukv", layer=0)
        qf, kf, vb = _mla_prep_fwd(q_raw, kv, lat, g_qn[j:j + 1], g_kn[j:j + 1], tabs, "mla_prep_fwd")
        o, lse = _flash_fwd(qf, kf, vb, "flash_fwd")
        toks = relay_next(bi, o)
        xo = _mm(o, full[("mla_w_o", j)], "nn", F32, "mm_mla_o", res=xin, layer=0, deps=toks)
        return xo, (xin, xn, lat, cq, ckv, q_raw, kv, qf, kf, vb, o, lse)

    def mla_bwd(dx_pair, saved, l):
        dxo, dxob = dx_pair
        j = l // 2
        xin, xn, lat, cq, ckv, q_raw, kv, qf, kf, vb, o, lse = saved
        do = _mm(dxob, full[("mla_w_o", j)], "nt", BF16, "mm_mla_do", layer=0)
        grads[("mla_w_o", j)] = _mm(o, dxob, "tn", BF16, "mm_mla_dwo")
        delta = _attn_delta(do, o, "attn_delta")
        dqf, dkf, dv = _flash_bwd(qf, kf, vb, do, lse, delta, "flash_bwd")
        dq_raw, dkv, dkpe, dgq, dgk = _mla_prep_bwd(q_raw, kv, lat, g_qn[j:j + 1], g_kn[j:j + 1], tabs, dqf, dkf, dv,
                                                    "mla_prep_bwd")
        gain_g.setdefault("mla_g_qn", {})[j] = dgq
        gain_g.setdefault("mla_g_kn", {})[j] = dgk
        dcq = _mm(dq_raw, full[("mla_w_uq", j)], "nt", F32, "mm_dcq", layer=0)
        grads[("mla_w_uq", j)] = _mm(cq, dq_raw, "tn", BF16, "mm_dwuq")
        dckv = _mm(dkv, full[("mla_w_ukv", j)], "nt", F32, "mm_dckv", layer=0)
        grads[("mla_w_ukv", j)] = _mm(ckv, dkv, "tn", BF16, "mm_dwukv")
        dlat, dgcq, dgckv = _lat_norm_bwd(lat, mla_g_cq[j:j + 1], mla_g_ckv[j:j + 1], dcq, dckv, dkpe, "lat_norm_bwd")
        gain_g.setdefault("mla_g_cq", {})[j] = dgcq
        gain_g.setdefault("mla_g_ckv", {})[j] = dgckv
        dxn = _mm(dlat, full[("mla_w_down", j)], "nt", F32, "mm_dxn_mla", layer=0)
        grads[("mla_w_down", j)] = _mm(xn, dlat, "tn", BF16, "mm_dwdown")
        tok = scatter_start(f"mix_{l}", [(n, j) for n in ("mla_w_down", "mla_w_uq", "mla_w_ukv", "mla_w_o")])
        toks = finish_due(dxn)
        dx, dxb, dg = _rms_bwd(xin, mix_norm[l:l + 1], dxn, dxo, "rms_bwd_mla", deps=[tok] + toks)
        gain_g.setdefault("mix_norm", {})[l] = dg
        return dx, dxb

    def dil_fwd(xin, l, bi):
        j = l // 2
        xn = _rms_fwd(xin, mix_norm[l:l + 1], "rms_fwd_dil")
        fetch([("dil_w_qkv", j), ("dil_w_o", j)], xn, "dil")
        qkv = _mm(xn, full[("dil_w_qkv", j)], "nn", F32, "mm_qkv", layer=0)
        o_g, lse_g = _dil_fwd(qkv, dil_g_qn[j:j + 1], dil_g_kn[j:j + 1], slopes, "dil_fwd")
        o, lse = _dil_merge(o_g, lse_g, "dil_merge")
        toks = relay_next(bi, o)
        xo = _mm(o, full[("dil_w_o", j)], "nn", F32, "mm_dil_o", res=xin, layer=0, deps=toks)
        return xo, (xin, xn, qkv, o, lse)

    def dil_bwd(dx_pair, saved, l):
        dxo, dxob = dx_pair
        j = l // 2
        xin, xn, qkv, o, lse = saved
        do = _mm(dxob, full[("dil_w_o", j)], "nt", F32, "mm_dil_do", layer=0)
        grads[("dil_w_o", j)] = _mm(o, dxob, "tn", BF16, "mm_dil_dwo")
        delta = _attn_delta(do, o, "dil_delta")
        dq, dk, dv, dgq, dgk = _dil_bwd(qkv, dil_g_qn[j:j + 1], dil_g_kn[j:j + 1], slopes, do, delta, lse, "dil_bwd")
        gain_g.setdefault("dil_g_qn", {})[j] = dgq
        gain_g.setdefault("dil_g_kn", {})[j] = dgk
        dqkv = jnp.concatenate([dq, dk, dv], axis=1)
        dxn = _mm(dqkv, full[("dil_w_qkv", j)], "nt", F32, "mm_dxn_dil", layer=0)
        grads[("dil_w_qkv", j)] = _mm(xn, dqkv, "tn", BF16, "mm_dwqkv")
        tok = scatter_start(f"mix_{l}", [("dil_w_qkv", j), ("dil_w_o", j)])
        toks = finish_due(dxn)
        dx, dxb, dg = _rms_bwd(xin, mix_norm[l:l + 1], dxn, dxo, "rms_bwd_dil", deps=[tok] + toks)
        gain_g.setdefault("mix_norm", {})[l] = dg
        return dx, dxb

    saved = []
    xc = x0
    for l in range(depth):
        xc, s1 = ffn_fwd(xc, ffn1_norm[l:l + 1], "ffn1", l, 3 * l, deps=[ag_token] if l == 0 else ())
        xc, s2 = (mla_fwd if l % 2 == 0 else dil_fwd)(xc, l, 3 * l + 1)
        xc, s3 = ffn_fwd(xc, ffn2_norm[l:l + 1], "ffn2", l, 3 * l + 2)
        saved.append((s1, s2, s3))

    dy, dyb, loss_part = _loss_head(xc, tgt, "loss_head")
    dx = (dy, dyb)
    loss = lax.psum(loss_part[0, 0], MESH_AXES)

    for bi in reversed(range(len(blocks))):
        tag, _ = blocks[bi]
        l = bi // 3
        s = saved[l][bi % 3]
        if bi % 3 == 2:
            dx = ffn_bwd(dx, s, ffn2_norm[l:l + 1], "ffn2", l, tag)
        elif bi % 3 == 1:
            dx = (mla_bwd if l % 2 == 0 else dil_bwd)(dx, s, l)
        else:
            dx = ffn_bwd(dx, s, ffn1_norm[l:l + 1], "ffn1", l, tag)
    grad_x = dx[0].reshape(x.shape)
    after = dx[1]
    while pending:
        after = scatter_finish(after)[-1]

    small = [n for n in names if n not in big]

    def gain_local(n):
        rows = [gain_g[n][l] for l in range(W[n].shape[0])]
        g = jnp.concatenate(rows, axis=1)
        return g

    def flat_pad(n, a):
        a = a.reshape(1, -1)
        if n in ("mla_g_qn", "mla_g_kn"):
            a = _pad_heads(a)
        return a

    packed_g = jnp.concatenate([gain_local(n) for n in small], axis=1)
    sizes = [gain_local(n).shape[1] for n in small]
    tot_g = _gain_allreduce(packed_g, "gain_allreduce")
    pw = jnp.concatenate([flat_pad(n, W[n]) for n in small], axis=1)
    pm = jnp.concatenate([flat_pad(n, M1[n]) for n in small], axis=1)
    pv = jnp.concatenate([flat_pad(n, V2[n]) for n in small], axis=1)
    res, _ = _adamw(tot_g.reshape(1, 1, -1), None, me, pw.reshape(1, 1, -1), pm.reshape(1, 1, -1),
                    pv.reshape(1, 1, -1), 0, None, "adamw_gains")
    res = [r.reshape(1, -1) for r in res]
    off = 0
    for n, sz in zip(small, sizes):
        for dst, r in zip((out_g, out_d, out_m, out_v), res):
            piece = r[:, off:off + sz]
            if n in ("mla_g_qn", "mla_g_kn"):
                piece = _unpad_heads(piece)
            dst[n] = piece.reshape(W[n].shape)
        off += sz

    return (loss, grad_x, *[out_g[n] for n in names], *[out_d[n] for n in names],
            *[out_m[n] for n in names], *[out_v[n] for n in names])
```

```python
import functools
import math

import jax
import jax.numpy as jnp
import numpy as np
from jax import lax
from jax.experimental import pallas as pl
from jax.experimental.pallas import tpu as pltpu

EPS = 1e-6
MLA_HEADS = 16
Q_LORA = 512
KV_LORA = 512
NOPE_DIM = 128
ROPE_DIM = 64
V_DIM = 128
QK_DIM = NOPE_DIM + ROPE_DIM
ROPE_THETA = 10000.0
HEAD_PAD = 256
LAT_PAD = Q_LORA + KV_LORA + 128
DIL_PAIRS = ((128, 1), (512, 4), (2048, 16))
DIL_GROUPS = 3
DIL_HEADS = 8
DIL_HEAD_DIM = 128
DIL_BLK = 128
DIL_UNROLL = 8
FLASH_HEADS = 2
LOG2E = math.log2(math.e)
LN2 = math.log(2.0)
ADAM_LR = 0.001
ADAM_B1 = 0.9
ADAM_B2 = 0.999
ADAM_EPS = 1e-08
ADAM_WD = 0.01
ADAM_STEP = 10

N_DEV = 8
MESH_AXES = ("x", "y", "c")
MESH = pl.DeviceIdType.MESH
NEG_BIG = -1e30
VMEM_LIMIT_V7X = 56 * 1024 * 1024
LANES = 128

BF16 = jnp.bfloat16
F32 = jnp.float32


def _pick(n, cands):
    for c in cands:
        if n % c == 0:
            return c
    raise ValueError(f"no tile for {n}")


def _params(sem):
    return pltpu.CompilerParams(dimension_semantics=sem, vmem_limit_bytes=VMEM_LIMIT_V7X)


ANY_SPEC = pl.BlockSpec(memory_space=pl.ANY)


MM_VMEM_BUDGET = 44 * 1024 * 1024
MM_HBM_BYTES_PER_S = 1.8e12
MM_MXU_FLOPS_PER_S = 8.5e14
MM_STEP_S = 0.4e-6
MM_MAX_TILE_MACS = 3.3e9
MXU_DIM = 256


@functools.lru_cache(maxsize=None)
def _mm_tiles(M, K, N, a_bytes, b_bytes, out_bytes, has_res):
    best = None
    for tk in [K] + [c for c in (1408, 1024, 512, 384, 256, 128) if K % c == 0 and c < K]:
        nk = K // tk
        for tm in [c for c in (2048, 1024, 512, 256, 128) if M % c == 0]:
            for tn in [c for c in (2816, 2048, 1408, 1152, 1024, 512, 384, 256, 128) if N % c == 0]:
                if tm * tk * tn > MM_MAX_TILE_MACS:
                    continue
                fill = (tn / (-(-tn // MXU_DIM) * MXU_DIM)) * (tk / (-(-tk // MXU_DIM) * MXU_DIM))
                fill *= tm / (tm + MXU_DIM // 2)
                vmem = 2 * (tm * tk * a_bytes + tk * tn * b_bytes) + 2 * tm * tn * out_bytes + tm * tn * 4
                vmem += (tm * tk + tk * tn) * 2 if max(a_bytes, b_bytes) > 2 else 0
                vmem += 2 * tm * tn * 4 if has_res else 0
                if vmem > MM_VMEM_BUDGET:
                    continue
                a_all, b_all = M * K * a_bytes, K * N * b_bytes
                if nk == 1:
                    t_i = a_all + (M // tm) * b_all
                    t_j = b_all + (N // tn) * a_all
                    traffic, i_outer = min((t_i, True), (t_j, False))
                else:
                    traffic, i_outer = (N // tn) * a_all + (M // tm) * b_all, True
                traffic += M * N * (out_bytes + (4 if has_res else 0))
                mxu = 2.0 * M * K * N / (MM_MXU_FLOPS_PER_S * fill) * (1.15 if nk > 1 else 1.0)
                cost = max(traffic / MM_HBM_BYTES_PER_S, mxu) + (M // tm) * (N // tn) * nk * MM_STEP_S
                if best is None or cost < best[0]:
                    best = (cost, tm, tn, tk, i_outer)
    assert best is not None, (M, K, N)
    return best[1:]


def _mm(a, b, mode, out_dtype, name, *, scale=1.0, res=None, layer=None, deps=()):
    b2 = b.shape[-2:]
    if mode == "nn":
        (M, K), (Kb, N) = a.shape, b2
    elif mode == "nt":
        (M, K), (N, Kb) = a.shape, b2
    else:
        (K, M), (Kb, N) = a.shape, b2
    assert K == Kb, (a.shape, b.shape, mode)
    tm, tn, tk, i_outer = _mm_tiles(M, K, N, a.dtype.itemsize, b.dtype.itemsize, jnp.dtype(out_dtype).itemsize,
                                    res is not None)
    nk = K // tk
    dims = {"nn": (((1,), (0,)), ((), ())), "nt": (((1,), (1,)), ((), ())), "tn": (((0,), (0,)), ((), ()))}[mode]

    def finish(v, r_ref, o_ref):
        if scale != 1.0:
            v = v * scale
        if r_ref is not None:
            v = r_ref[...] + v
        o_ref[...] = v.astype(o_ref.dtype)

    def body(*refs):
        a_ref, b_ref = refs[:2]
        r_ref = refs[2] if res is not None else None
        prod = lambda: lax.dot_general(a_ref[...].astype(BF16), b_ref[...].astype(BF16), dims,
                                       preferred_element_type=F32)
        if nk == 1:
            finish(prod(), r_ref, refs[-1])
            return
        o_ref, acc = refs[-2:]
        k = pl.program_id(2)

        @pl.when(k == 0)
        def _():
            acc[...] = prod()

        @pl.when(k > 0)
        def _():
            acc[...] += prod()

        @pl.when(k == nk - 1)
        def _():
            finish(acc[...], r_ref, o_ref)

    ij = (lambda p, q: (p, q)) if i_outer else (lambda p, q: (q, p))

    def spec(shape, f, lead=None):
        full = lambda p, q, k: f(*ij(p, q), k)
        if lead is None:
            return pl.BlockSpec(shape, full)
        return pl.BlockSpec((None,) + shape, lambda p, q, k: (lead,) + full(p, q, k))

    a_spec = spec((tk, tm), lambda i, j, k: (k, i)) if mode == "tn" else spec((tm, tk), lambda i, j, k: (i, k))
    lead = layer if b.ndim == 3 else None
    b_spec = spec((tn, tk), lambda i, j, k: (j, k), lead) if mode == "nt" else spec((tk, tn), lambda i, j, k: (k, j), lead)
    in_specs = [a_spec, b_spec]
    args = [a, b]
    if res is not None:
        in_specs.append(spec((tm, tn), lambda i, j, k: (i, j)))
        args.append(res)
    in_specs += [ANY_SPEC] * len(deps)
    args += list(deps)
    outer, inner = (M // tm, N // tn) if i_outer else (N // tn, M // tm)
    return pl.pallas_call(
        body, name=name, grid=(outer, inner, nk),
        in_specs=in_specs, out_specs=spec((tm, tn), lambda i, j, k: (i, j)),
        out_shape=jax.ShapeDtypeStruct((M, N), out_dtype),
        scratch_shapes=[pltpu.VMEM((tm, tn), F32)] if nk > 1 else [],
        compiler_params=_params(("parallel", "parallel", "arbitrary")),
    )(*args)


def _cast_into_gathered(w, layer, axis, me, name):
    _, R, C = w.shape
    tr = _pick(R, (512, 256, 128, 64, 32, 16))
    nr = R // tr
    axis, paired = axis

    def body(me_ref, w_ref, o_ref):
        o_ref[...] = w_ref[...].astype(BF16)

    if axis == 1:
        out_idx = lambda i, me_ref: (0, _slot(me_ref[0], paired) * nr + i, 0)
        shape = (1, R * N_DEV, C)
    else:
        out_idx = lambda i, me_ref: (0, i, _slot(me_ref[0], paired))
        shape = (1, R, C * N_DEV)
    return pl.pallas_call(
        body, name=name,
        grid_spec=pltpu.PrefetchScalarGridSpec(
            num_scalar_prefetch=1, grid=(nr,),
            in_specs=[pl.BlockSpec((None, tr, C), lambda i, me_ref: (layer, i, 0))],
            out_specs=pl.BlockSpec((None, tr, C), out_idx)),
        out_shape=jax.ShapeDtypeStruct(shape, BF16), compiler_params=_params(("parallel",)),
    )(me, w)


def _rms_fwd(x, g, name, deps=()):
    T, D = x.shape
    tr = _pick(T, (512, 256, 128))

    def body(x_ref, g_ref, *rest):
        o_ref = rest[-1]
        xv = x_ref[...]
        r = lax.rsqrt(jnp.mean(xv * xv, axis=-1, keepdims=True) + EPS)
        o_ref[...] = ((xv * r) * g_ref[...]).astype(BF16)

    return pl.pallas_call(
        body, name=name, grid=(T // tr,),
        in_specs=[pl.BlockSpec((tr, D), lambda i: (i, 0)), pl.BlockSpec((1, D), lambda i: (0, 0))]
        + [ANY_SPEC] * len(deps),
        out_specs=pl.BlockSpec((tr, D), lambda i: (i, 0)),
        out_shape=jax.ShapeDtypeStruct((T, D), BF16), compiler_params=_params(("parallel",)),
    )(x, g, *deps)


def _rms_bwd(x, g, dh, dres, name, deps=()):
    T, D = x.shape
    tr = _pick(T, (256, 128))

    def body(x_ref, g_ref, dh_ref, dres_ref, *rest):
        dx_ref, dxb_ref, dg_ref = rest[-3:]
        xv = x_ref[...]
        dhv = dh_ref[...]
        r = lax.rsqrt(jnp.mean(xv * xv, axis=-1, keepdims=True) + EPS)
        xhat = xv * r
        dxh = dhv * g_ref[...]
        c = jnp.mean(dxh * xhat, axis=-1, keepdims=True)
        dx = dres_ref[...] + r * (dxh - xhat * c)
        dx_ref[...] = dx
        dxb_ref[...] = dx.astype(BF16)

        @pl.when(pl.program_id(0) == 0)
        def _():
            dg_ref[...] = jnp.zeros_like(dg_ref)

        dg_ref[...] += jnp.sum(dhv * xhat, axis=0, keepdims=True)

    row = pl.BlockSpec((tr, D), lambda i: (i, 0))
    vec = pl.BlockSpec((1, D), lambda i: (0, 0))
    return pl.pallas_call(
        body, name=name, grid=(T // tr,),
        in_specs=[row, vec, row, row] + [ANY_SPEC] * len(deps), out_specs=[row, row, vec],
        out_shape=[jax.ShapeDtypeStruct((T, D), F32), jax.ShapeDtypeStruct((T, D), BF16),
                   jax.ShapeDtypeStruct((1, D), F32)],
        compiler_params=_params(("arbitrary",)),
    )(x, g, dh, dres, *deps)


N_PANEL = N_DEV // 2


def _ffn_in(h, w_in, name):
    T, D = h.shape
    F2 = w_in.shape[2]
    pw = F2 // N_PANEL
    half = pw // 2
    tm = _pick(T, (512, 256, 128))

    def body(h_ref, w_ref, u_ref, a_ref):
        r = jnp.dot(h_ref[...], w_ref[...], preferred_element_type=F32)
        u_ref[...] = r.astype(BF16)
        g, up = r[:, :half], r[:, half:]
        a_ref[...] = (g * jax.nn.sigmoid(g) * up).astype(BF16)

    return pl.pallas_call(
        body, name=name, grid=(N_PANEL, T // tm),
        in_specs=[pl.BlockSpec((tm, D), lambda p, i: (i, 0)), pl.BlockSpec((None, D, pw), lambda p, i: (0, 0, p))],
        out_specs=[pl.BlockSpec((tm, pw), lambda p, i: (i, p)), pl.BlockSpec((tm, half), lambda p, i: (i, p))],
        out_shape=[jax.ShapeDtypeStruct((T, F2), BF16), jax.ShapeDtypeStruct((T, F2 // 2), BF16)],
        compiler_params=_params(("parallel", "parallel")),
    )(h, w_in)


def _ffn_da(dxo, w_out, u, name, deps=()):
    T, D = dxo.shape
    F2 = u.shape[1]
    pw = F2 // N_PANEL
    half = pw // 2
    tm = _pick(T, (512, 256, 128))

    def body(d_ref, w_ref, u_ref, *rest):
        du_ref = rest[-1]
        da = 0.5 * lax.dot_general(d_ref[...], w_ref[...], NT_DIMS, preferred_element_type=F32)
        g = u_ref[:, :half].astype(F32)
        up = u_ref[:, half:].astype(F32)
        sg = jax.nn.sigmoid(g)
        silu = g * sg
        du_ref[:, :half] = (da * up * (sg + silu * (1.0 - sg))).astype(BF16)
        du_ref[:, half:] = (da * silu).astype(BF16)

    return pl.pallas_call(
        body, name=name, grid=(N_PANEL, T // tm),
        in_specs=[pl.BlockSpec((tm, D), lambda p, i: (i, 0)), pl.BlockSpec((None, half, D), lambda p, i: (0, p, 0)),
                  pl.BlockSpec((tm, pw), lambda p, i: (i, p))] + [ANY_SPEC] * len(deps),
        out_specs=pl.BlockSpec((tm, pw), lambda p, i: (i, p)),
        out_shape=jax.ShapeDtypeStruct((T, F2), BF16), compiler_params=_params(("parallel", "parallel")),
    )(dxo, w_out, u, *deps)


def _loss_head(y, t, name):
    T, D = y.shape
    tr = _pick(T, (512, 256, 128))

    def body(y_ref, t_ref, dy_ref, dyb_ref, l_ref):
        e = y_ref[...] - t_ref[...]
        dy = e * (1.0 / D)
        dy_ref[...] = dy
        dyb_ref[...] = dy.astype(BF16)

        @pl.when(pl.program_id(0) == 0)
        def _():
            l_ref[...] = jnp.zeros_like(l_ref)

        l_ref[...] += 0.5 * jnp.sum(jnp.mean(e * e, axis=-1, keepdims=True), axis=0, keepdims=True)

    row = pl.BlockSpec((tr, D), lambda i: (i, 0))
    return pl.pallas_call(
        body, name=name, grid=(T // tr,),
        in_specs=[row, row], out_specs=[row, row, pl.BlockSpec((1, 1), lambda i: (0, 0))],
        out_shape=[jax.ShapeDtypeStruct((T, D), F32), jax.ShapeDtypeStruct((T, D), BF16),
                   jax.ShapeDtypeStruct((1, 1), F32)],
        compiler_params=_params(("arbitrary",)),
    )(y, t)


def _rope_tables(S):
    half = ROPE_DIM // 2
    inv = 1.0 / (ROPE_THETA ** (jnp.arange(0, ROPE_DIM, 2, dtype=F32) / ROPE_DIM))
    ang = jnp.arange(S, dtype=F32)[:, None] * inv[None, :]
    cos, sin = jnp.cos(ang), jnp.sin(ang)
    z = jnp.zeros((S, half), F32)
    z2 = jnp.zeros((S, LANES - ROPE_DIM), F32)
    c = jnp.concatenate([cos, cos, z2], axis=1)
    s1 = jnp.concatenate([-sin, z, z2], axis=1)
    s2 = jnp.concatenate([z, sin, z2], axis=1)
    return c, s1, s2


def _rope(r, c, s1, s2):
    return r * c + pltpu.roll(r, LANES - ROPE_DIM // 2, 1) * s1 + pltpu.roll(r, ROPE_DIM // 2, 1) * s2


def _rope_t(d, c, s1, s2):
    return d * c + pltpu.roll(d * s1, ROPE_DIM // 2, 1) + pltpu.roll(d * s2, LANES - ROPE_DIM // 2, 1)


def _lat_norm_fwd(lat, g_cq, g_ckv, name):
    T = lat.shape[0]
    tr = _pick(T, (512, 256, 128))

    def body(lat_ref, gq_ref, gk_ref, cq_ref, ckv_ref):
        for off, g_ref, o_ref in ((0, gq_ref, cq_ref), (Q_LORA, gk_ref, ckv_ref)):
            xv = lat_ref[:, off:off + Q_LORA]
            r = lax.rsqrt(jnp.mean(xv * xv, axis=-1, keepdims=True) + EPS)
            o_ref[...] = ((xv * r) * g_ref[...]).astype(BF16)

    vec = pl.BlockSpec((1, Q_LORA), lambda i: (0, 0))
    out = pl.BlockSpec((tr, Q_LORA), lambda i: (i, 0))
    return pl.pallas_call(
        body, name=name, grid=(T // tr,),
        in_specs=[pl.BlockSpec((tr, LAT_PAD), lambda i: (i, 0)), vec, vec], out_specs=[out, out],
        out_shape=[jax.ShapeDtypeStruct((T, Q_LORA), BF16)] * 2, compiler_params=_params(("parallel",)),
    )(lat, g_cq, g_ckv)


def _lat_norm_bwd(lat, g_cq, g_ckv, dcq, dckv, dkpe, name):
    T = lat.shape[0]
    tr = _pick(T, (256, 128))

    def body(lat_ref, gq_ref, gk_ref, dcq_ref, dckv_ref, dkpe_ref, dlat_ref, dgq_ref, dgk_ref):
        @pl.when(pl.program_id(0) == 0)
        def _():
            dgq_ref[...] = jnp.zeros_like(dgq_ref)
            dgk_ref[...] = jnp.zeros_like(dgk_ref)

        for off, g_ref, d_ref, dg_ref in ((0, gq_ref, dcq_ref, dgq_ref), (Q_LORA, gk_ref, dckv_ref, dgk_ref)):
            xv = lat_ref[:, off:off + Q_LORA]
            dv = d_ref[...]
            r = lax.rsqrt(jnp.mean(xv * xv, axis=-1, keepdims=True) + EPS)
            xhat = xv * r
            dxh = dv * g_ref[...]
            c = jnp.mean(dxh * xhat, axis=-1, keepdims=True)
            dlat_ref[:, off:off + Q_LORA] = (r * (dxh - xhat * c)).astype(BF16)
            dg_ref[...] += jnp.sum(dv * xhat, axis=0, keepdims=True)
        dlat_ref[:, Q_LORA + KV_LORA:] = dkpe_ref[...].astype(BF16)

    vec = pl.BlockSpec((1, Q_LORA), lambda i: (0, 0))
    half = pl.BlockSpec((tr, Q_LORA), lambda i: (i, 0))
    full = pl.BlockSpec((tr, LAT_PAD), lambda i: (i, 0))
    return pl.pallas_call(
        body, name=name, grid=(T // tr,),
        in_specs=[full, vec, vec, half, half, pl.BlockSpec((tr, LANES), lambda i: (i, 0))],
        out_specs=[full, vec, vec],
        out_shape=[jax.ShapeDtypeStruct((T, LAT_PAD), BF16), jax.ShapeDtypeStruct((1, Q_LORA), F32),
                   jax.ShapeDtypeStruct((1, Q_LORA), F32)],
        compiler_params=_params(("arbitrary",)),
    )(lat, g_cq, g_ckv, dcq, dckv, dkpe)


def _mla_prep_fwd(q_raw, kv, lat, g_qn, g_kn, tabs, name):
    T = q_raw.shape[0]
    H = MLA_HEADS
    tr = _pick(T, (256, 128))
    scale = LOG2E / math.sqrt(QK_DIM)

    def body(q_ref, kv_ref, kpe_ref, gq_ref, gk_ref, c_ref, s1_ref, s2_ref, qf_ref, kf_ref, v_ref):
        c, s1, s2 = c_ref[...], s1_ref[...], s2_ref[...]
        gq, gk = gq_ref[...], gk_ref[...]
        kpe = kpe_ref[...]
        kpe_ss = jnp.sum(kpe * kpe, axis=-1, keepdims=True)
        for h in range(H):
            lo = h * HEAD_PAD
            qa = q_ref[:, lo:lo + LANES]
            qb = q_ref[:, lo + LANES:lo + HEAD_PAD]
            ss = jnp.sum(qa * qa, axis=-1, keepdims=True) + jnp.sum(qb * qb, axis=-1, keepdims=True)
            r = lax.rsqrt(ss * (1.0 / QK_DIM) + EPS)
            qf_ref[:, lo:lo + LANES] = (qa * r * gq[:, :LANES] * scale).astype(BF16)
            qf_ref[:, lo + LANES:lo + HEAD_PAD] = (_rope(qb * r * gq[:, LANES:], c, s1, s2) * scale).astype(BF16)
            ka = kv_ref[:, lo:lo + LANES]
            ss = jnp.sum(ka * ka, axis=-1, keepdims=True) + kpe_ss
            r = lax.rsqrt(ss * (1.0 / QK_DIM) + EPS)
            kf_ref[:, lo:lo + LANES] = (ka * r * gk[:, :LANES]).astype(BF16)
            kf_ref[:, lo + LANES:lo + HEAD_PAD] = _rope(kpe * r * gk[:, LANES:], c, s1, s2).astype(BF16)
            v_ref[:, h * V_DIM:(h + 1) * V_DIM] = kv_ref[:, lo + LANES:lo + HEAD_PAD].astype(BF16)

    wide = pl.BlockSpec((tr, H * HEAD_PAD), lambda i: (i, 0))
    lane = pl.BlockSpec((tr, LANES), lambda i: (i, 0))
    gvec = pl.BlockSpec((1, HEAD_PAD), lambda i: (0, 0))
    return pl.pallas_call(
        body, name=name, grid=(T // tr,),
        in_specs=[wide, wide, pl.BlockSpec((tr, LANES), lambda i: (i, (Q_LORA + KV_LORA) // LANES)), gvec, gvec,
                  lane, lane, lane],
        out_specs=[wide, wide, pl.BlockSpec((tr, H * V_DIM), lambda i: (i, 0))],
        out_shape=[jax.ShapeDtypeStruct((T, H * HEAD_PAD), BF16), jax.ShapeDtypeStruct((T, H * HEAD_PAD), BF16),
                   jax.ShapeDtypeStruct((T, H * V_DIM), BF16)],
        compiler_params=_params(("parallel",)),
    )(q_raw, kv, lat, g_qn, g_kn, *tabs)


def _mla_prep_bwd(q_raw, kv, lat, g_qn, g_kn, tabs, dqf, dkf, dv, name):
    T = q_raw.shape[0]
    H = MLA_HEADS
    tr = _pick(T, (128,))

    def body(q_ref, kv_ref, kpe_ref, gq_ref, gk_ref, c_ref, s1_ref, s2_ref, dqf_ref, dkf_ref, dv_ref,
             dq_ref, dkv_ref, dkpe_ref, dgq_ref, dgk_ref):
        @pl.when(pl.program_id(0) == 0)
        def _():
            dgq_ref[...] = jnp.zeros_like(dgq_ref)
            dgk_ref[...] = jnp.zeros_like(dgk_ref)

        c, s1, s2 = c_ref[...], s1_ref[...], s2_ref[...]
        gq, gk = gq_ref[...], gk_ref[...]
        kpe = kpe_ref[...]
        kpe_ss = jnp.sum(kpe * kpe, axis=-1, keepdims=True)
        dkpe = jnp.zeros_like(kpe)
        dgq_a = jnp.zeros((1, LANES), F32)
        dgq_b = jnp.zeros((1, LANES), F32)
        dgk_a = jnp.zeros((1, LANES), F32)
        dgk_b = jnp.zeros((1, LANES), F32)
        for h in range(H):
            lo = h * HEAD_PAD
            xa = q_ref[:, lo:lo + LANES]
            xb = q_ref[:, lo + LANES:lo + HEAD_PAD]
            ss = jnp.sum(xa * xa, axis=-1, keepdims=True) + jnp.sum(xb * xb, axis=-1, keepdims=True)
            r = lax.rsqrt(ss * (1.0 / QK_DIM) + EPS)
            xa, xb = xa * r, xb * r
            da = dqf_ref[:, lo:lo + LANES]
            db = _rope_t(dqf_ref[:, lo + LANES:lo + HEAD_PAD], c, s1, s2)
            dgq_a += jnp.sum(da * xa, axis=0, keepdims=True)
            dgq_b += jnp.sum(db * xb, axis=0, keepdims=True)
            da, db = da * gq[:, :LANES], db * gq[:, LANES:]
            cc = (jnp.sum(da * xa, axis=-1, keepdims=True) + jnp.sum(db * xb, axis=-1, keepdims=True)) * (1.0 / QK_DIM)
            dq_ref[:, lo:lo + LANES] = (r * (da - xa * cc)).astype(BF16)
            dq_ref[:, lo + LANES:lo + HEAD_PAD] = (r * (db - xb * cc)).astype(BF16)
            xa = kv_ref[:, lo:lo + LANES]
            ss = jnp.sum(xa * xa, axis=-1, keepdims=True) + kpe_ss
            r = lax.rsqrt(ss * (1.0 / QK_DIM) + EPS)
            xa, xb = xa * r, kpe * r
            da = dkf_ref[:, lo:lo + LANES]
            db = _rope_t(dkf_ref[:, lo + LANES:lo + HEAD_PAD], c, s1, s2)
            dgk_a += jnp.sum(da * xa, axis=0, keepdims=True)
            dgk_b += jnp.sum(db * xb, axis=0, keepdims=True)
            da, db = da * gk[:, :LANES], db * gk[:, LANES:]
            cc = (jnp.sum(da * xa, axis=-1, keepdims=True) + jnp.sum(db * xb, axis=-1, keepdims=True)) * (1.0 / QK_DIM)
            dkv_ref[:, lo:lo + LANES] = (r * (da - xa * cc)).astype(BF16)
            dkpe = dkpe + r * (db - xb * cc)
            dkv_ref[:, lo + LANES:lo + HEAD_PAD] = dv_ref[:, h * V_DIM:(h + 1) * V_DIM].astype(BF16)
        dkpe_ref[...] = dkpe
        dgq_ref[:, :LANES] += dgq_a
        dgq_ref[:, LANES:] += dgq_b
        dgk_ref[:, :LANES] += dgk_a
        dgk_ref[:, LANES:] += dgk_b

    wide = pl.BlockSpec((tr, H * HEAD_PAD), lambda i: (i, 0))
    lane = pl.BlockSpec((tr, LANES), lambda i: (i, 0))
    gvec = pl.BlockSpec((1, HEAD_PAD), lambda i: (0, 0))
    vspec = pl.BlockSpec((tr, H * V_DIM), lambda i: (i, 0))
    return pl.pallas_call(
        body, name=name, grid=(T // tr,),
        in_specs=[wide, wide, pl.BlockSpec((tr, LANES), lambda i: (i, (Q_LORA + KV_LORA) // LANES)), gvec, gvec,
                  lane, lane, lane, wide, wide, vspec],
        out_specs=[wide, wide, lane, gvec, gvec],
        out_shape=[jax.ShapeDtypeStruct((T, H * HEAD_PAD), BF16), jax.ShapeDtypeStruct((T, H * HEAD_PAD), BF16),
                   jax.ShapeDtypeStruct((T, LANES), F32), jax.ShapeDtypeStruct((1, HEAD_PAD), F32),
                   jax.ShapeDtypeStruct((1, HEAD_PAD), F32)],
        compiler_params=_params(("arbitrary",)),
    )(q_raw, kv, lat, g_qn, g_kn, *tabs, dqf, dkf, dv)


def _causal_mask(tq, tk):
    return lax.broadcasted_iota(jnp.int32, (tq, tk), 1) <= lax.broadcasted_iota(jnp.int32, (tq, tk), 0)


NT_DIMS = (((1,), (1,)), ((), ()))
TN_DIMS = (((0,), (0,)), ((), ()))


def _flash_fwd(qf, kf, v, name):
    T = qf.shape[0]
    H, G = MLA_HEADS, FLASH_HEADS
    t = _pick(T, (512, 256, 128))
    n = T // t
    pairs = [(i, j) for i in range(n) for j in range(i + 1)]
    qi = jnp.asarray([p[0] for p in pairs], jnp.int32)
    kj = jnp.asarray([p[1] for p in pairs], jnp.int32)

    def body(qi_ref, kj_ref, q_ref, k_ref, v_ref, o_ref, lse_ref, m_sc, l_sc, acc_sc):
        sid = pl.program_id(1)
        i, j = qi_ref[sid], kj_ref[sid]

        @pl.when(j == 0)
        def _():
            m_sc[...] = jnp.full_like(m_sc, NEG_BIG)
            l_sc[...] = jnp.zeros_like(l_sc)
            acc_sc[...] = jnp.zeros_like(acc_sc)

        def step(masked):
            for g in range(G):
                qk = slice(g * HEAD_PAD, (g + 1) * HEAD_PAD)
                vo = slice(g * V_DIM, (g + 1) * V_DIM)
                s = lax.dot_general(q_ref[:, qk], k_ref[:, qk], NT_DIMS, preferred_element_type=F32)
                if masked:
                    s = jnp.where(_causal_mask(t, t), s, NEG_BIG)
                m_prev = m_sc[g, :, :1]
                m_new = jnp.maximum(m_prev, jnp.max(s, axis=-1, keepdims=True))
                a = jnp.exp2(m_prev - m_new)
                p = jnp.exp2(s - m_new)
                l_sc[g] = a * l_sc[g] + jnp.sum(p, axis=-1, keepdims=True)
                acc_sc[:, vo] = a * acc_sc[:, vo] + jnp.dot(p.astype(BF16), v_ref[:, vo], preferred_element_type=F32)
                m_sc[g] = jnp.broadcast_to(m_new, (t, LANES))

        @pl.when(j < i)
        def _():
            step(False)

        @pl.when(j == i)
        def _():
            step(True)
            for g in range(G):
                vo = slice(g * V_DIM, (g + 1) * V_DIM)
                o_ref[:, vo] = (acc_sc[:, vo] / l_sc[g]).astype(BF16)
                lse_ref[:, vo] = m_sc[g] + jnp.log2(l_sc[g])

    row = pl.BlockSpec((t, G * V_DIM), lambda h, s, qi, kj: (qi[s], h))
    return pl.pallas_call(
        body, name=name,
        grid_spec=pltpu.PrefetchScalarGridSpec(
            num_scalar_prefetch=2, grid=(H // G, len(pairs)),
            in_specs=[pl.BlockSpec((t, G * HEAD_PAD), lambda h, s, qi, kj: (qi[s], h)),
                      pl.BlockSpec((t, G * HEAD_PAD), lambda h, s, qi, kj: (kj[s], h)),
                      pl.BlockSpec((t, G * V_DIM), lambda h, s, qi, kj: (kj[s], h))],
            out_specs=[row, row],
            scratch_shapes=[pltpu.VMEM((G, t, LANES), F32), pltpu.VMEM((G, t, LANES), F32),
                            pltpu.VMEM((t, G * V_DIM), F32)]),
        out_shape=[jax.ShapeDtypeStruct((T, H * V_DIM), BF16), jax.ShapeDtypeStruct((T, H * V_DIM), F32)],
        compiler_params=_params(("parallel", "arbitrary")),
    )(qi, kj, qf, kf, v)


def _attn_delta(do, o, name):
    T, W = do.shape
    nh = W // V_DIM
    tr = _pick(T, (512, 256, 128))

    def body(do_ref, o_ref, d_ref):
        for h in range(nh):
            sl = slice(h * V_DIM, (h + 1) * V_DIM)
            d = jnp.sum(do_ref[:, sl].astype(F32) * o_ref[:, sl].astype(F32), axis=-1, keepdims=True)
            d_ref[:, sl] = jnp.broadcast_to(d, (tr, V_DIM))

    row = pl.BlockSpec((tr, W), lambda i: (i, 0))
    return pl.pallas_call(
        body, name=name, grid=(T // tr,), in_specs=[row, row], out_specs=row,
        out_shape=jax.ShapeDtypeStruct((T, W), F32), compiler_params=_params(("parallel",)),
    )(do, o)


def _flash_bwd(qf, kf, v, do, lse, delta, name):
    T = qf.shape[0]
    H, G = MLA_HEADS, FLASH_HEADS
    t = _pick(T, (512, 256, 128))
    n = T // t
    scale = 1.0 / math.sqrt(QK_DIM)
    pairs = [(i, j) for j in range(n) for i in range(j, n)]
    qi = jnp.asarray([p[0] for p in pairs], jnp.int32)
    kj = jnp.asarray([p[1] for p in pairs], jnp.int32)

    def body(qi_ref, kj_ref, q_ref, k_ref, v_ref, do_ref, lse_ref, dl_ref, dq_ref, dk_ref, dv_ref, dk_acc, dv_acc):
        sid = pl.program_id(1)
        i, j = qi_ref[sid], kj_ref[sid]

        @pl.when(sid == 0)
        def _():
            dq_ref[...] = jnp.zeros_like(dq_ref)

        def step(masked):
            rows = pl.ds(pl.multiple_of(i * t, t), t)
            for g in range(G):
                qk = slice(g * HEAD_PAD, (g + 1) * HEAD_PAD)
                vo = slice(g * V_DIM, (g + 1) * V_DIM)
                q, k, v_, do_ = q_ref[:, qk], k_ref[:, qk], v_ref[:, vo], do_ref[:, vo]
                s = lax.dot_general(q, k, NT_DIMS, preferred_element_type=F32)
                if masked:
                    s = jnp.where(_causal_mask(t, t), s, NEG_BIG)
                p = jnp.exp2(s - lse_ref[:, g * V_DIM:g * V_DIM + 1])
                dp = lax.dot_general(do_, v_, NT_DIMS, preferred_element_type=F32)
                ds = (p * (dp - dl_ref[:, g * V_DIM:g * V_DIM + 1])).astype(BF16)
                dv = lax.dot_general(p.astype(BF16), do_, TN_DIMS, preferred_element_type=F32)
                dk = lax.dot_general(ds, q, TN_DIMS, preferred_element_type=F32)
                if masked:
                    dv_acc[:, vo] = dv
                    dk_acc[:, qk] = dk
                else:
                    dv_acc[:, vo] += dv
                    dk_acc[:, qk] += dk
                dq_ref[rows, qk] += jnp.dot(ds, k, preferred_element_type=F32) * scale

        @pl.when(i == j)
        def _():
            step(True)

        @pl.when(i > j)
        def _():
            step(False)

        @pl.when(i == n - 1)
        def _():
            dk_ref[...] = dk_acc[...] * LN2
            dv_ref[...] = dv_acc[...]

    qs = pl.BlockSpec((t, G * HEAD_PAD), lambda h, s, qi, kj: (qi[s], h))
    rs = pl.BlockSpec((t, G * V_DIM), lambda h, s, qi, kj: (qi[s], h))
    ks = pl.BlockSpec((t, G * HEAD_PAD), lambda h, s, qi, kj: (kj[s], h))
    vs = pl.BlockSpec((t, G * V_DIM), lambda h, s, qi, kj: (kj[s], h))
    return pl.pallas_call(
        body, name=name,
        grid_spec=pltpu.PrefetchScalarGridSpec(
            num_scalar_prefetch=2, grid=(H // G, len(pairs)), in_specs=[qs, ks, vs, rs, rs, rs],
            out_specs=[pl.BlockSpec((T, G * HEAD_PAD), lambda h, s, qi, kj: (0, h)), ks, vs],
            scratch_shapes=[pltpu.VMEM((t, G * HEAD_PAD), F32), pltpu.VMEM((t, G * V_DIM), F32)]),
        out_shape=[jax.ShapeDtypeStruct((T, H * HEAD_PAD), F32), jax.ShapeDtypeStruct((T, H * HEAD_PAD), F32),
                   jax.ShapeDtypeStruct((T, H * V_DIM), F32)],
        compiler_params=_params(("parallel", "arbitrary")),
    )(qi, kj, qf, kf, v, do, lse, delta)


def _alibi_slopes():
    tot = DIL_GROUPS * DIL_HEADS
    return [float(np.float32(2.0) ** (np.float32(-8.0) * np.float32(k) / np.float32(tot))) for k in range(1, tot + 1)]


def _dil_masks():
    iq = lax.broadcasted_iota(jnp.int32, (DIL_BLK, DIL_BLK), 0)
    ik = lax.broadcasted_iota(jnp.int32, (DIL_BLK, DIL_BLK), 1)
    return (ik >= iq), (iq + DIL_BLK - ik).astype(F32), (ik <= iq), (iq - ik).astype(F32)


def _dil_norm(x, g):
    r = lax.rsqrt(jnp.mean(x * x, axis=-1, keepdims=True) + EPS)
    return x * r, r


def _dil_fwd(qkv, g_qn, g_kn, slopes, name):
    T = qkv.shape[0]
    GH = DIL_GROUPS * DIL_HEADS
    scale = 1.0 / math.sqrt(DIL_HEAD_DIM)

    def body(sl_ref, q_ref, k_ref, v_ref, gq_ref, gk_ref, o_ref, lse_ref):
        gh = pl.program_id(0)
        slope = sl_ref[gh]
        ok_p, dist_p, ok_c, dist_c = _dil_masks()
        gq, gk = gq_ref[...], gk_ref[...]
        for g, (_, d) in enumerate(DIL_PAIRS):
            @pl.when((gh >= g * DIL_HEADS) & (gh < (g + 1) * DIL_HEADS))
            def _(d=d):
                nb = T // (d * DIL_BLK)
                bias_p = jnp.where(ok_p, -slope * d * dist_p, NEG_BIG)
                bias_c = jnp.where(ok_c, -slope * d * dist_c, NEG_BIG)

                def phase(r, _):
                    def blk(nn, _):
                        def rows(b):
                            return pl.ds(b * (d * DIL_BLK) + r, DIL_BLK, stride=d) if d > 1 else pl.ds(pl.multiple_of(b * DIL_BLK, DIL_BLK), DIL_BLK)
                        cur, prv = rows(nn), rows(jnp.maximum(nn - 1, 0))
                        q = (_dil_norm(q_ref[cur, :], gq)[0] * gq).astype(BF16)
                        kc = (_dil_norm(k_ref[cur, :], gk)[0] * gk).astype(BF16)
                        kp = (_dil_norm(k_ref[prv, :], gk)[0] * gk).astype(BF16)
                        s_c = lax.dot_general(q, kc, NT_DIMS, preferred_element_type=F32) * scale + bias_c
                        s_p = lax.dot_general(q, kp, NT_DIMS, preferred_element_type=F32) * scale + bias_p
                        s_p = jnp.where(nn > 0, s_p, NEG_BIG)
                        m = jnp.maximum(jnp.max(s_c, axis=-1, keepdims=True), jnp.max(s_p, axis=-1, keepdims=True))
                        p_c = jnp.exp(s_c - m)
                        p_p = jnp.exp(s_p - m)
                        l = jnp.sum(p_c, axis=-1, keepdims=True) + jnp.sum(p_p, axis=-1, keepdims=True)
                        acc = jnp.dot(p_c.astype(BF16), v_ref[cur, :].astype(BF16), preferred_element_type=F32)
                        acc += jnp.dot(p_p.astype(BF16), v_ref[prv, :].astype(BF16), preferred_element_type=F32)
                        o_ref[cur, :] = acc / l
                        lse_ref[cur, :] = jnp.broadcast_to(m + jnp.log(l), (DIL_BLK, DIL_HEAD_DIM))
                        return 0
                    lax.fori_loop(0, nb, blk, 0, unroll=min(nb, DIL_UNROLL))
                    return 0
                lax.fori_loop(0, d, phase, 0, unroll=min(d, max(1, DIL_UNROLL // nb)))

    col = lambda off: pl.BlockSpec((T, DIL_HEAD_DIM), lambda gh, sl: (0, gh + off))
    gvec = pl.BlockSpec((1, DIL_HEAD_DIM), lambda gh, sl: (0, 0))
    return pl.pallas_call(
        body, name=name,
        grid_spec=pltpu.PrefetchScalarGridSpec(
            num_scalar_prefetch=1, grid=(GH,),
            in_specs=[col(0), col(GH), col(2 * GH), gvec, gvec], out_specs=[col(0), col(0)]),
        out_shape=[jax.ShapeDtypeStruct((T, GH * DIL_HEAD_DIM), F32)] * 2,
        compiler_params=_params(("parallel",)),
    )(slopes, qkv, qkv, qkv, g_qn, g_kn)


def _dil_merge(o_g, lse_g, name):
    T = o_g.shape[0]
    W = DIL_HEADS * DIL_HEAD_DIM
    tr = _pick(T, (256, 128))

    def body(o0, o1, o2, l0, l1, l2, o_ref, lse_ref):
        a, b, c = l0[...], l1[...], l2[...]
        m = jnp.maximum(jnp.maximum(a, b), c)
        ea, eb, ec = jnp.exp(a - m), jnp.exp(b - m), jnp.exp(c - m)
        tot = ea + eb + ec
        o_ref[...] = ((o0[...] * ea + o1[...] * eb + o2[...] * ec) / tot).astype(BF16)
        lse_ref[...] = m + jnp.log(tot)

    grp = lambda g: pl.BlockSpec((tr, W), lambda i: (i, g))
    out = pl.BlockSpec((tr, W), lambda i: (i, 0))
    return pl.pallas_call(
        body, name=name, grid=(T // tr,),
        in_specs=[grp(0), grp(1), grp(2), grp(0), grp(1), grp(2)], out_specs=[out, out],
        out_shape=[jax.ShapeDtypeStruct((T, W), BF16), jax.ShapeDtypeStruct((T, W), F32)],
        compiler_params=_params(("parallel",)),
    )(o_g, o_g, o_g, lse_g, lse_g, lse_g)


def _dil_bwd(qkv, g_qn, g_kn, slopes, do, delta, lse, name):
    T = qkv.shape[0]
    GH = DIL_GROUPS * DIL_HEADS
    scale = 1.0 / math.sqrt(DIL_HEAD_DIM)
    nchunk = T // DIL_BLK

    def body(sl_ref, q_ref, k_ref, v_ref, gq_ref, gk_ref, do_ref, dl_ref, lse_ref,
             dq_ref, dk_ref, dv_ref, dgq_ref, dgk_ref, dq_acc, dk_acc, dv_acc):
        gh = pl.program_id(0)
        slope = sl_ref[gh]
        ok_p, dist_p, ok_c, dist_c = _dil_masks()
        gq, gk = gq_ref[...], gk_ref[...]

        @pl.when(gh == 0)
        def _():
            dgq_ref[...] = jnp.zeros_like(dgq_ref)
            dgk_ref[...] = jnp.zeros_like(dgk_ref)

        dk_acc[...] = jnp.zeros_like(dk_acc)
        dv_acc[...] = jnp.zeros_like(dv_acc)
        for g, (_, d) in enumerate(DIL_PAIRS):
            @pl.when((gh >= g * DIL_HEADS) & (gh < (g + 1) * DIL_HEADS))
            def _(d=d):
                nb = T // (d * DIL_BLK)
                bias_p = jnp.where(ok_p, -slope * d * dist_p, NEG_BIG)
                bias_c = jnp.where(ok_c, -slope * d * dist_c, NEG_BIG)

                def phase(r, _):
                    def blk(nn, _):
                        def rows(b):
                            return pl.ds(b * (d * DIL_BLK) + r, DIL_BLK, stride=d) if d > 1 else pl.ds(pl.multiple_of(b * DIL_BLK, DIL_BLK), DIL_BLK)
                        cur, prv = rows(nn), rows(jnp.maximum(nn - 1, 0))
                        q = (_dil_norm(q_ref[cur, :], gq)[0] * gq).astype(BF16)
                        kc = (_dil_norm(k_ref[cur, :], gk)[0] * gk).astype(BF16)
                        kp = (_dil_norm(k_ref[prv, :], gk)[0] * gk).astype(BF16)
                        vc = v_ref[cur, :].astype(BF16)
                        vp = v_ref[prv, :].astype(BF16)
                        dob = do_ref[cur, :].astype(BF16)
                        delta = dl_ref[cur, :][:, :1]
                        ls = lse_ref[cur, :][:, :1]
                        s_c = lax.dot_general(q, kc, NT_DIMS, preferred_element_type=F32) * scale + bias_c
                        s_p = lax.dot_general(q, kp, NT_DIMS, preferred_element_type=F32) * scale + bias_p
                        s_p = jnp.where(nn > 0, s_p, NEG_BIG)
                        p_c = jnp.exp(s_c - ls)
                        p_p = jnp.exp(s_p - ls)
                        dp_c = lax.dot_general(dob, vc, NT_DIMS, preferred_element_type=F32)
                        dp_p = lax.dot_general(dob, vp, NT_DIMS, preferred_element_type=F32)
                        ds_c = (p_c * (dp_c - delta)).astype(BF16)
                        ds_p = (p_p * (dp_p - delta)).astype(BF16)
                        dq_acc[cur, :] = (jnp.dot(ds_c, kc, preferred_element_type=F32)
                                          + jnp.dot(ds_p, kp, preferred_element_type=F32)) * scale
                        dk_acc[cur, :] += lax.dot_general(ds_c, q, TN_DIMS, preferred_element_type=F32) * scale
                        dv_acc[cur, :] += lax.dot_general(p_c.astype(BF16), dob, TN_DIMS, preferred_element_type=F32)
                        dk_acc[prv, :] += lax.dot_general(ds_p, q, TN_DIMS, preferred_element_type=F32) * scale
                        dv_acc[prv, :] += lax.dot_general(p_p.astype(BF16), dob, TN_DIMS, preferred_element_type=F32)
                        return 0
                    lax.fori_loop(0, nb, blk, 0, unroll=min(nb, DIL_UNROLL))
                    return 0
                lax.fori_loop(0, d, phase, 0, unroll=min(d, max(1, DIL_UNROLL // nb)))

        def fin(ci, carry):
            dgq, dgk = carry
            rows = pl.ds(pl.multiple_of(ci * DIL_BLK, DIL_BLK), DIL_BLK)
            outs = []
            for x_ref, d_acc, gvec in ((q_ref, dq_acc, gq), (k_ref, dk_acc, gk)):
                xhat, r = _dil_norm(x_ref[rows, :], gvec)
                dn = d_acc[rows, :]
                dxh = dn * gvec
                c = jnp.mean(dxh * xhat, axis=-1, keepdims=True)
                outs.append(((r * (dxh - xhat * c)).astype(BF16), jnp.sum(dn * xhat, axis=0, keepdims=True)))
            dq_ref[rows, :] = outs[0][0]
            dk_ref[rows, :] = outs[1][0]
            dv_ref[rows, :] = dv_acc[rows, :].astype(BF16)
            return dgq + outs[0][1], dgk + outs[1][1]

        z = jnp.zeros((1, DIL_HEAD_DIM), F32)
        dgq, dgk = lax.fori_loop(0, nchunk, fin, (z, z))
        dgq_ref[...] += dgq
        dgk_ref[...] += dgk

    col = lambda off: pl.BlockSpec((T, DIL_HEAD_DIM), lambda gh, sl: (0, gh + off))
    hcol = pl.BlockSpec((T, DIL_HEAD_DIM), lambda gh, sl: (0, gh % DIL_HEADS))
    gvec = pl.BlockSpec((1, DIL_HEAD_DIM), lambda gh, sl: (0, 0))
    wide = jax.ShapeDtypeStruct((T, GH * DIL_HEAD_DIM), BF16)
    vec = jax.ShapeDtypeStruct((1, DIL_HEAD_DIM), F32)
    return pl.pallas_call(
        body, name=name,
        grid_spec=pltpu.PrefetchScalarGridSpec(
            num_scalar_prefetch=1, grid=(GH,),
            in_specs=[col(0), col(GH), col(2 * GH), gvec, gvec, hcol, hcol, hcol],
            out_specs=[col(0), col(0), col(0), gvec, gvec],
            scratch_shapes=[pltpu.VMEM((T, DIL_HEAD_DIM), F32)] * 3),
        out_shape=[wide, wide, wide, vec, vec],
        compiler_params=_params(("arbitrary",)),
    )(slopes, qkv, qkv, qkv, g_qn, g_kn, do, delta, lse)


def _my_pos():
    return lax.axis_index("x"), lax.axis_index("y"), lax.axis_index("c")


def _peer(pos, j):
    x, y, c = pos
    px = 1 - x if j & 4 else x
    py = 1 - y if j & 2 else y
    pc = 1 - c if j & 1 else c
    return (px, py, pc), 4 * px + 2 * py + pc


def _slot(idx, paired):
    if not paired:
        return idx
    return jnp.where(idx < N_DEV // 2, 2 * idx, 2 * idx - (N_DEV - 1))


def _shard_slice(ref, axis, idx, size, paired=False):
    sl = [slice(None)] * len(ref.shape)
    sl[axis] = pl.ds(pl.multiple_of(_slot(idx, paired) * size, 8), size)
    return ref.at[tuple(sl)]


HBM_SPEC = pl.BlockSpec(memory_space=pltpu.HBM)
SEM_SPEC = pl.BlockSpec(memory_space=pltpu.SEMAPHORE)
DATAFLOW = pltpu.SideEffectType.DATAFLOW_SIDE_EFFECTING
N_PEER = N_DEV - 1


def _scatter_copy(axis, grad, slots, frm, to, dev, send_sem, recv_sem):
    ax, paired = axis
    src = _shard_slice(grad, ax, to, grad.shape[ax] // N_DEV, paired)
    return pltpu.make_async_remote_copy(src_ref=src, dst_ref=slots.at[frm], send_sem=send_sem, recv_sem=recv_sem,
                                        device_id=dev, device_id_type=MESH)


def _scatter_start(grads, axes, name):
    n = len(grads)

    def body(*refs):
        outs = refs[2 * n:]
        send, recv, token = outs[:n], outs[n:2 * n], outs[4 * n]
        pos = _my_pos()
        me = 4 * pos[0] + 2 * pos[1] + pos[2]
        for a in range(n):
            for j in range(1, N_DEV):
                dev, pid = _peer(pos, j)
                _scatter_copy(axes[a], refs[2 * a], refs[2 * a + 1], me, pid, dev, send[a].at[j - 1],
                              recv[a].at[j - 1]).start()
        token[...] = jnp.zeros_like(token)

    ops = []
    for g, (ax, _) in zip(grads, axes):
        shp = list(g.shape)
        shp[ax] //= N_DEV
        ops += [g, lax.empty((N_DEV,) + tuple(shp), g.dtype)]
    sems = [pltpu.SemaphoreType.DMA((N_PEER,))] * (2 * n)
    res = pl.pallas_call(
        body, name=name,
        out_shape=sems + [pltpu.HBM(o.shape, o.dtype) for o in ops] + [jax.ShapeDtypeStruct((8, LANES), F32)],
        in_specs=[HBM_SPEC] * len(ops),
        out_specs=[SEM_SPEC] * (2 * n) + [HBM_SPEC] * len(ops) + [pl.BlockSpec(memory_space=pltpu.VMEM)],
        input_output_aliases={i: 2 * n + i for i in range(len(ops))},
        compiler_params=pltpu.CompilerParams(has_side_effects=DATAFLOW),
    )(*[pltpu.with_memory_space_constraint(o, pltpu.HBM) for o in ops])
    items = [(res[a], res[n + a], res[2 * n + 2 * a], res[2 * n + 2 * a + 1]) for a in range(n)]
    return items, res[4 * n]


def _scatter_wait(items, axes, after, name):
    n = len(items)

    def body(*refs):
        send, recv = refs[2 * n:3 * n], refs[3 * n:4 * n]
        pos = _my_pos()
        me = 4 * pos[0] + 2 * pos[1] + pos[2]
        for a in range(n):
            for j in range(1, N_DEV):
                dev, pid = _peer(pos, j)
                cp = _scatter_copy(axes[a], refs[2 * a], refs[2 * a + 1], pid, me, dev, send[a].at[j - 1],
                                   recv[a].at[j - 1])
                cp.wait_send()
                cp.wait_recv()

    ops = [b for it in items for b in it[2:]]
    res = pl.pallas_call(
        body, name=name,
        out_shape=[pltpu.HBM(o.shape, o.dtype) for o in ops],
        in_specs=[HBM_SPEC] * len(ops) + [SEM_SPEC] * (2 * n) + [ANY_SPEC],
        out_specs=[HBM_SPEC] * len(ops),
        input_output_aliases={i: i for i in range(len(ops))},
        compiler_params=pltpu.CompilerParams(has_side_effects=DATAFLOW),
    )(*ops, *[it[0] for it in items], *[it[1] for it in items], after)
    return [res[2 * a + 1] for a in range(n)]


SIBLING = 1
ICI_PEERS = (2, 4, 6)


def _gather_copy(buf, axis, shard, dev, send_sem, recv_sem):
    ax, paired = axis
    piece = _shard_slice(buf, ax, shard, buf.shape[ax] // N_DEV, paired)
    return pltpu.make_async_remote_copy(src_ref=piece, dst_ref=piece, send_sem=send_sem, recv_sem=recv_sem,
                                        device_id=dev, device_id_type=MESH)


def _gather_start(bufs, axes, name):
    n = len(bufs)

    def body(*refs):
        ins, outs = refs[:n], refs[n:]
        send, r_sib, r_ici, token = outs[:n], outs[n:2 * n], outs[2 * n:3 * n], outs[4 * n]
        pos = _my_pos()
        me = 4 * pos[0] + 2 * pos[1] + pos[2]
        for a in range(n):
            dev, _ = _peer(pos, SIBLING)
            _gather_copy(ins[a], axes[a], me, dev, send[a].at[0], r_sib[a].at[0]).start()
            for k, j in enumerate(ICI_PEERS):
                dev, _ = _peer(pos, j)
                _gather_copy(ins[a], axes[a], me, dev, send[a].at[1 + k], r_ici[a].at[k]).start()
        token[...] = jnp.zeros_like(token)

    sems = ([pltpu.SemaphoreType.DMA((1 + len(ICI_PEERS),))] * n + [pltpu.SemaphoreType.DMA((1,))] * n
            + [pltpu.SemaphoreType.DMA((len(ICI_PEERS),))] * n)
    res = pl.pallas_call(
        body, name=name,
        out_shape=sems + [pltpu.HBM(b.shape, b.dtype) for b in bufs] + [jax.ShapeDtypeStruct((8, LANES), F32)],
        in_specs=[HBM_SPEC] * n,
        out_specs=[SEM_SPEC] * (3 * n) + [HBM_SPEC] * n + [pl.BlockSpec(memory_space=pltpu.VMEM)],
        input_output_aliases={i: 3 * n + i for i in range(n)},
        compiler_params=pltpu.CompilerParams(has_side_effects=DATAFLOW),
    )(*[pltpu.with_memory_space_constraint(b, pltpu.HBM) for b in bufs])
    items = [dict(send=res[a], r_sib=res[n + a], r_ici=res[2 * n + a], buf=res[3 * n + a]) for a in range(n)]
    return items, res[4 * n]


def _gather_relay(items, axes, after, name):
    n = len(items)

    def body(*refs):
        ins, r_ici = refs[:n], refs[n:2 * n]
        outs = refs[2 * n + 1:]
        s_rel, r_rel, token = outs[:n], outs[n:2 * n], outs[3 * n]
        pos = _my_pos()
        sib, _ = _peer(pos, SIBLING)
        for a in range(n):
            for k, j in enumerate(ICI_PEERS):
                dev, pid = _peer(pos, j)
                _gather_copy(ins[a], axes[a], pid, dev, s_rel[a].at[k], r_ici[a].at[k]).wait_recv()
                _gather_copy(ins[a], axes[a], pid, sib, s_rel[a].at[k], r_rel[a].at[k]).start()
        token[...] = jnp.zeros_like(token)

    bufs = [it["buf"] for it in items]
    sems = [pltpu.SemaphoreType.DMA((len(ICI_PEERS),))] * (2 * n)
    res = pl.pallas_call(
        body, name=name,
        out_shape=sems + [pltpu.HBM(b.shape, b.dtype) for b in bufs] + [jax.ShapeDtypeStruct((8, LANES), F32)],
        in_specs=[HBM_SPEC] * n + [SEM_SPEC] * n + [ANY_SPEC],
        out_specs=[SEM_SPEC] * (2 * n) + [HBM_SPEC] * n + [pl.BlockSpec(memory_space=pltpu.VMEM)],
        input_output_aliases={i: 2 * n + i for i in range(n)},
        compiler_params=pltpu.CompilerParams(has_side_effects=DATAFLOW),
    )(*bufs, *[it["r_ici"] for it in items], after)
    out = [dict(send=it["send"], r_sib=it["r_sib"], s_rel=res[a], r_rel=res[n + a], buf=res[2 * n + a])
           for a, it in enumerate(items)]
    return out, res[3 * n]


def _gather_wait(items, axes, after, name):
    n = len(items)

    def body(*refs):
        ins = refs[:n]
        send, r_sib, s_rel, r_rel = (refs[(1 + q) * n:(2 + q) * n] for q in range(4))
        pos = _my_pos()
        me = 4 * pos[0] + 2 * pos[1] + pos[2]
        sib, sib_id = _peer(pos, SIBLING)
        for a in range(n):
            for k in range(1 + len(ICI_PEERS)):
                _gather_copy(ins[a], axes[a], me, sib, send[a].at[k], r_sib[a].at[0]).wait_send()
            _gather_copy(ins[a], axes[a], sib_id, sib, send[a].at[0], r_sib[a].at[0]).wait_recv()
            for k, j in enumerate(ICI_PEERS):
                _, pid = _peer(pos, j)
                _, far = _peer(pos, j ^ SIBLING)
                _gather_copy(ins[a], axes[a], pid, sib, s_rel[a].at[k], r_rel[a].at[k]).wait_send()
                _gather_copy(ins[a], axes[a], far, sib, s_rel[a].at[k], r_rel[a].at[k]).wait_recv()

    bufs = [it["buf"] for it in items]
    res = pl.pallas_call(
        body, name=name,
        out_shape=[pltpu.HBM(b.shape, b.dtype) for b in bufs],
        in_specs=[HBM_SPEC] * n + [SEM_SPEC] * (4 * n) + [ANY_SPEC],
        out_specs=[HBM_SPEC] * n,
        input_output_aliases={i: i for i in range(n)},
        compiler_params=pltpu.CompilerParams(has_side_effects=DATAFLOW),
    )(*bufs, *[it["send"] for it in items], *[it["r_sib"] for it in items], *[it["s_rel"] for it in items],
      *[it["r_rel"] for it in items], after)
    return list(res)


def _gain_allreduce(v, name):
    n = v.shape[1]

    def body(v_ref, o_ref, slots, send_sems, recv_sems):
        pos = _my_pos()
        me = 4 * pos[0] + 2 * pos[1] + pos[2]
        slots[me] = v_ref[...]
        copies = []
        for j in range(1, N_DEV):
            dev, _ = _peer(pos, j)
            cp = pltpu.make_async_remote_copy(
                src_ref=slots.at[me], dst_ref=slots.at[me], send_sem=send_sems.at[j], recv_sem=recv_sems.at[j],
                device_id=dev, device_id_type=MESH)
            cp.start()
            copies.append(cp)
        for j in range(1, N_DEV):
            dev, pid = _peer(pos, j)
            pltpu.make_async_remote_copy(
                src_ref=slots.at[me], dst_ref=slots.at[pid], send_sem=send_sems.at[j], recv_sem=recv_sems.at[j],
                device_id=dev, device_id_type=MESH).wait_recv()
        for cp in copies:
            cp.wait_send()
        acc = slots[0]
        for s in range(1, N_DEV):
            acc = acc + slots[s]
        o_ref[...] = acc

    return pl.pallas_call(
        body, name=name, out_shape=jax.ShapeDtypeStruct((1, n), F32),
        in_specs=[pl.BlockSpec(memory_space=pltpu.VMEM)], out_specs=pl.BlockSpec(memory_space=pltpu.VMEM),
        scratch_shapes=[pltpu.VMEM((N_DEV, 1, n), F32), pltpu.SemaphoreType.DMA((N_DEV,)),
                        pltpu.SemaphoreType.DMA((N_DEV,))],
        compiler_params=pltpu.CompilerParams(has_side_effects=True),
    )(v)


def _adamw(parts, own, me, w, m, v, layer, prev, name):
    L, R, C = w.shape
    P = parts.shape[0]
    tr = _pick(R, (128, 64, 32, 16, 8, 1))
    c1 = 1.0 - ADAM_B1 ** ADAM_STEP
    c2 = 1.0 - ADAM_B2 ** ADAM_STEP
    n_in = 4 if own is None else 5

    def body(me_ref, *refs):
        p_ref = refs[0]
        w_ref, m_ref, v_ref = refs[n_in - 3:n_in]
        g_out, d_out, m_out, v_out, tok = refs[-5:]
        g = None
        for s in range(P):
            part = p_ref[s]
            if own is not None:
                part = jnp.where(me_ref[0] == s, refs[1][...], part)
            g = part.astype(F32) if g is None else g + part.astype(F32)
        mn = ADAM_B1 * m_ref[...] + (1.0 - ADAM_B1) * g
        vn = ADAM_B2 * v_ref[...] + (1.0 - ADAM_B2) * (g * g)
        g_out[...] = g
        m_out[...] = mn
        v_out[...] = vn
        d_out[...] = -ADAM_LR * ((mn / c1) / (jnp.sqrt(vn / c2) + ADAM_EPS) + ADAM_WD * w_ref[...])
        tok[...] = jnp.zeros_like(tok)

    row = pl.BlockSpec((None, tr, C), lambda i, me_ref: (layer, i, 0))
    in_specs = [pl.BlockSpec((P, tr, C), lambda i, me_ref: (0, i, 0))]
    args = [parts]
    if own is not None:
        in_specs.append(pl.BlockSpec((tr, C), lambda i, me_ref: (i, 0)))
        args.append(own)
    in_specs += [row, row, row]
    args += [w, m, v]
    aliases = {}
    if prev is not None:
        in_specs += [ANY_SPEC] * 4
        aliases = {1 + len(args) + k: k for k in range(4)}
        args += list(prev)
    shp = jax.ShapeDtypeStruct((L, R, C), F32)
    res = pl.pallas_call(
        body, name=name,
        grid_spec=pltpu.PrefetchScalarGridSpec(
            num_scalar_prefetch=1, grid=(R // tr,), in_specs=in_specs,
            out_specs=[row] * 4 + [pl.BlockSpec((8, LANES), lambda i, me_ref: (0, 0))]),
        out_shape=[shp] * 4 + [jax.ShapeDtypeStruct((8, LANES), F32)],
        input_output_aliases=aliases, compiler_params=_params(("arbitrary",)),
    )(me, *args)
    return res[:4], res[4]


def _pad_heads(w):
    lead = w.shape[:-1]
    n = w.shape[-1] // QK_DIM
    w = w.reshape(lead + (n, QK_DIM))
    w = jnp.pad(w, [(0, 0)] * len(lead) + [(0, 0), (0, HEAD_PAD - QK_DIM)])
    return w.reshape(lead + (n * HEAD_PAD,))


def _unpad_heads(w):
    lead = w.shape[:-1]
    n = w.shape[-1] // HEAD_PAD
    return w.reshape(lead + (n, HEAD_PAD))[..., :QK_DIM].reshape(lead + (n * QK_DIM,))


def kernel(x, ffn1_norm, ffn1_w_in, ffn1_w_out, mix_norm, ffn2_norm, ffn2_w_in, ffn2_w_out, mla_w_down, mla_g_cq, mla_g_ckv, mla_w_uq, mla_w_ukv, mla_g_qn, mla_g_kn, mla_w_o, dil_w_qkv, dil_g_qn, dil_g_kn, dil_w_o, loss_target, m_ffn1_norm, m_ffn1_w_in, m_ffn1_w_out, m_mix_norm, m_ffn2_norm, m_ffn2_w_in, m_ffn2_w_out, m_mla_w_down, m_mla_g_cq, m_mla_g_ckv, m_mla_w_uq, m_mla_w_ukv, m_mla_g_qn, m_mla_g_kn, m_mla_w_o, m_dil_w_qkv, m_dil_g_qn, m_dil_g_kn, m_dil_w_o, v_ffn1_norm, v_ffn1_w_in, v_ffn1_w_out, v_mix_norm, v_ffn2_norm, v_ffn2_w_in, v_ffn2_w_out, v_mla_w_down, v_mla_g_cq, v_mla_g_ckv, v_mla_w_uq, v_mla_w_ukv, v_mla_g_qn, v_mla_g_kn, v_mla_w_o, v_dil_w_qkv, v_dil_g_qn, v_dil_g_kn, v_dil_w_o):
    names = ["ffn1_norm", "ffn1_w_in", "ffn1_w_out", "mix_norm", "ffn2_norm", "ffn2_w_in", "ffn2_w_out", "mla_w_down",
             "mla_g_cq", "mla_g_ckv", "mla_w_uq", "mla_w_ukv", "mla_g_qn", "mla_g_kn", "mla_w_o", "dil_w_qkv",
             "dil_g_qn", "dil_g_kn", "dil_w_o"]
    W = dict(zip(names, [ffn1_norm, ffn1_w_in, ffn1_w_out, mix_norm, ffn2_norm, ffn2_w_in, ffn2_w_out, mla_w_down,
                         mla_g_cq, mla_g_ckv, mla_w_uq, mla_w_ukv, mla_g_qn, mla_g_kn, mla_w_o, dil_w_qkv,
                         dil_g_qn, dil_g_kn, dil_w_o]))
    M1 = dict(zip(names, [m_ffn1_norm, m_ffn1_w_in, m_ffn1_w_out, m_mix_norm, m_ffn2_norm, m_ffn2_w_in, m_ffn2_w_out,
                          m_mla_w_down, m_mla_g_cq, m_mla_g_ckv, m_mla_w_uq, m_mla_w_ukv, m_mla_g_qn, m_mla_g_kn,
                          m_mla_w_o, m_dil_w_qkv, m_dil_g_qn, m_dil_g_kn, m_dil_w_o]))
    V2 = dict(zip(names, [v_ffn1_norm, v_ffn1_w_in, v_ffn1_w_out, v_mix_norm, v_ffn2_norm, v_ffn2_w_in, v_ffn2_w_out,
                          v_mla_w_down, v_mla_g_cq, v_mla_g_ckv, v_mla_w_uq, v_mla_w_ukv, v_mla_g_qn, v_mla_g_kn,
                          v_mla_w_o, v_dil_w_qkv, v_dil_g_qn, v_dil_g_kn, v_dil_w_o]))
    S, D = x.shape[1], x.shape[2]
    x0 = x.reshape(S, D)
    tgt = loss_target.reshape(S, D)

    big = ["ffn1_w_in", "ffn1_w_out", "ffn2_w_in", "ffn2_w_out", "mla_w_down", "mla_w_uq", "mla_w_ukv", "mla_w_o",
           "dil_w_qkv", "dil_w_o"]
    shard_dim = {"ffn1_w_in": 2, "ffn1_w_out": 1, "ffn2_w_in": 2, "ffn2_w_out": 1, "mla_w_down": 1, "mla_w_uq": 2,
                 "mla_w_ukv": 2, "mla_w_o": 1, "dil_w_qkv": 2, "dil_w_o": 2}
    paired = ("ffn1_w_in", "ffn2_w_in")
    shard_axis = {n: (d, n in paired) for n, d in shard_dim.items()}
    grad_axis = {n: (d - 1, n in paired) for n, d in shard_dim.items()}

    def padded(n, w):
        if n == "mla_w_down":
            return jnp.pad(w, ((0, 0), (0, 0), (0, LAT_PAD - w.shape[2])))
        if n == "mla_w_uq":
            return _pad_heads(w)
        return w

    depth = ffn1_norm.shape[0]
    blocks = []
    for l in range(depth):
        mixer = (["mla_w_down", "mla_w_uq", "mla_w_ukv", "mla_w_o"] if l % 2 == 0 else ["dil_w_qkv", "dil_w_o"])
        blocks.append((f"ffn1_{l}", [("ffn1_w_in", l), ("ffn1_w_out", l)]))
        blocks.append((f"mix_{l}", [(n, l // 2) for n in mixer]))
        blocks.append((f"ffn2_{l}", [("ffn2_w_in", l), ("ffn2_w_out", l)]))
    order = [k for _, keys in blocks for k in keys]
    me = (4 * lax.axis_index("x") + 2 * lax.axis_index("y") + lax.axis_index("c")).astype(jnp.int32).reshape(1)
    mine = [_cast_into_gathered(padded(n, W[n]), l, shard_axis[n], me, f"cast_{n}_{l}") for n, l in order]
    ag_items, ag_token = _gather_start(mine, [shard_axis[n] for n, _ in order], "gather_start")
    ag_items = dict(zip(order, ag_items))
    full = {}

    def relay(keys, after, tag):
        out, token = _gather_relay([ag_items[k] for k in keys], [shard_axis[k[0]] for k in keys], after,
                                   f"gather_relay_{tag}")
        ag_items.update(zip(keys, out))
        return [token]

    def relay_next(bi, after):
        return relay(blocks[bi + 1][1], after, blocks[bi + 1][0]) if bi + 1 < len(blocks) else []

    def fetch(keys, after, tag):
        lands = _gather_wait([ag_items[k] for k in keys], [shard_axis[k[0]] for k in keys], after,
                             f"gather_wait_{tag}")
        full.update(zip(keys, lands))

    g_qn = _pad_heads(mla_g_qn)
    g_kn = _pad_heads(mla_g_kn)
    tabs = _rope_tables(S)
    slopes = jnp.asarray(_alibi_slopes(), F32)

    grads = {}
    gain_g = {}

    out_g, out_d, out_m, out_v = {}, {}, {}, {}
    pending = []
    lag = 3

    def scatter_start(tag, keys):
        items, token = _scatter_start([grads[k] for k in keys], [grad_axis[k[0]] for k in keys],
                                      f"scatter_start_{tag}")
        pending.append((tag, keys, items))
        return token

    def scatter_finish(after):
        tag, keys, items = pending.pop(0)
        lands = _scatter_wait(items, [grad_axis[k[0]] for k in keys], after, f"scatter_wait_{tag}")
        tokens = []
        for (n, l), p in zip(keys, lands):
            ax, pair = grad_axis[n]
            size = grads[(n, l)].shape[ax] // N_DEV
            own = lax.dynamic_slice_in_dim(grads[(n, l)], _slot(me[0], pair) * size, size, axis=ax)
            if n == "mla_w_down":
                p, own = p[..., :W[n].shape[2]], own[..., :W[n].shape[2]]
            elif n == "mla_w_uq":
                p, own = _unpad_heads(p), _unpad_heads(own)
            prev = (out_g[n], out_d[n], out_m[n], out_v[n]) if n in out_g else None
            (out_g[n], out_d[n], out_m[n], out_v[n]), tok = _adamw(p, own, me, W[n], M1[n], V2[n], l, prev,
                                                                    f"adamw_{n}_{l}")
            tokens.append(tok)
        return tokens

    def finish_due(after):
        tokens = []
        while len(pending) > lag:
            tokens += scatter_finish(after)
        return tokens

    def ffn_fwd(xin, norm_row, which, l, bi, deps=()):
        tag = blocks[bi][0]
        k_in, k_out = (which + "_w_in", l), (which + "_w_out", l)
        h = _rms_fwd(xin, norm_row, f"rms_fwd_{tag}", deps=deps)
        if bi == 0:
            relay([k_in], h, f"{tag}_in")
        fetch([k_in], h, f"in_{tag}")
        u, a = _ffn_in(h, full[k_in], f"ffn_in_{tag}")
        if bi == 0:
            relay([k_out], a, f"{tag}_out")
        fetch([k_out], a, f"out_{tag}")
        toks = relay_next(bi, a)
        xo = _mm(a, full[k_out], "nn", F32, f"mm_out_{tag}", scale=0.5, res=xin, layer=0, deps=toks)
        return xo, (xin, h, u, a)

    def ffn_bwd(dx_pair, saved, norm_row, which, l, tag):
        dxo, dxob = dx_pair
        k_in, k_out = (which + "_w_in", l), (which + "_w_out", l)
        xin, h, u, a = saved
        grads[k_out] = _mm(a, dxob, "tn", BF16, f"mm_dwout_{tag}", scale=0.5)
        t_out = scatter_start(f"{tag}_out", [k_out])
        du = _ffn_da(dxob, full[k_out], u, f"ffn_da_{tag}", deps=[t_out])
        grads[k_in] = _mm(h, du, "tn", BF16, f"mm_dwin_{tag}")
        t_in = scatter_start(f"{tag}_in", [k_in])
        dh = _mm(du, full[k_in], "nt", F32, f"mm_dh_{tag}", layer=0, deps=[t_in])
        toks = finish_due(dh)
        dx, dxb, dg = _rms_bwd(xin, norm_row, dh, dxo, f"rms_bwd_{tag}", deps=toks)
        gain_g.setdefault(which + "_norm", {})[l] = dg
        return dx, dxb

    def mla_fwd(xin, l, bi):
        j = l // 2
        xn = _rms_fwd(xin, mix_norm[l:l + 1], "rms_fwd_mla")
        fetch([(n, j) for n in ("mla_w_down", "mla_w_uq", "mla_w_ukv", "mla_w_o")], xn, "mla")
        lat = _mm(xn, full[("mla_w_down", j)], "nn", F32, "mm_lat", layer=0)
        cq, ckv = _lat_norm_fwd(lat, mla_g_cq[j:j + 1], mla_g_ckv[j:j + 1], "lat_norm_fwd")
        q_raw = _mm(cq, full[("mla_w_uq", j)], "nn", F32, "mm_uq", layer=0)
        kv = _mm(ckv, full[("mla_w_ukv", j)], "nn", F32, "mm_ukv", layer=0)
        qf, kf, vb = _mla_prep_fwd(q_raw, kv, lat, g_qn[j:j + 1], g_kn[j:j + 1], tabs, "mla_prep_fwd")
        o, lse = _flash_fwd(qf, kf, vb, "flash_fwd")
        toks = relay_next(bi, o)
        xo = _mm(o, full[("mla_w_o", j)], "nn", F32, "mm_mla_o", res=xin, layer=0, deps=toks)
        return xo, (xin, xn, lat, cq, ckv, q_raw, kv, qf, kf, vb, o, lse)

    def mla_bwd(dx_pair, saved, l):
        dxo, dxob = dx_pair
        j = l // 2
        xin, xn, lat, cq, ckv, q_raw, kv, qf, kf, vb, o, lse = saved
        do = _mm(dxob, full[("mla_w_o", j)], "nt", BF16, "mm_mla_do", layer=0)
        grads[("mla_w_o", j)] = _mm(o, dxob, "tn", BF16, "mm_mla_dwo")
        delta = _attn_delta(do, o, "attn_delta")
        dqf, dkf, dv = _flash_bwd(qf, kf, vb, do, lse, delta, "flash_bwd")
        dq_raw, dkv, dkpe, dgq, dgk = _mla_prep_bwd(q_raw, kv, lat, g_qn[j:j + 1], g_kn[j:j + 1], tabs, dqf, dkf, dv,
                                                    "mla_prep_bwd")
        gain_g.setdefault("mla_g_qn", {})[j] = dgq
        gain_g.setdefault("mla_g_kn", {})[j] = dgk
        dcq = _mm(dq_raw, full[("mla_w_uq", j)], "nt", F32, "mm_dcq", layer=0)
        grads[("mla_w_uq", j)] = _mm(cq, dq_raw, "tn", BF16, "mm_dwuq")
        dckv = _mm(dkv, full[("mla_w_ukv", j)], "nt", F32, "mm_dckv", layer=0)
        grads[("mla_w_ukv", j)] = _mm(ckv, dkv, "tn", BF16, "mm_dwukv")
        dlat, dgcq, dgckv = _lat_norm_bwd(lat, mla_g_cq[j:j + 1], mla_g_ckv[j:j + 1], dcq, dckv, dkpe, "lat_norm_bwd")
        gain_g.setdefault("mla_g_cq", {})[j] = dgcq
        gain_g.setdefault("mla_g_ckv", {})[j] = dgckv
        dxn = _mm(dlat, full[("mla_w_down", j)], "nt", F32, "mm_dxn_mla", layer=0)
        grads[("mla_w_down", j)] = _mm(xn, dlat, "tn", BF16, "mm_dwdown")
        tok = scatter_start(f"mix_{l}", [(n, j) for n in ("mla_w_down", "mla_w_uq", "mla_w_ukv", "mla_w_o")])
        toks = finish_due(dxn)
        dx, dxb, dg = _rms_bwd(xin, mix_norm[l:l + 1], dxn, dxo, "rms_bwd_mla", deps=[tok] + toks)
        gain_g.setdefault("mix_norm", {})[l] = dg
        return dx, dxb

    def dil_fwd(xin, l, bi):
        j = l // 2
        xn = _rms_fwd(xin, mix_norm[l:l + 1], "rms_fwd_dil")
        fetch([("dil_w_qkv", j), ("dil_w_o", j)], xn, "dil")
        qkv = _mm(xn, full[("dil_w_qkv", j)], "nn", F32, "mm_qkv", layer=0)
        o_g, lse_g = _dil_fwd(qkv, dil_g_qn[j:j + 1], dil_g_kn[j:j + 1], slopes, "dil_fwd")
        o, lse = _dil_merge(o_g, lse_g, "dil_merge")
        toks = relay_next(bi, o)
        xo = _mm(o, full[("dil_w_o", j)], "nn", F32, "mm_dil_o", res=xin, layer=0, deps=toks)
        return xo, (xin, xn, qkv, o, lse)

    def dil_bwd(dx_pair, saved, l):
        dxo, dxob = dx_pair
        j = l // 2
        xin, xn, qkv, o, lse = saved
        do = _mm(dxob, full[("dil_w_o", j)], "nt", F32, "mm_dil_do", layer=0)
        grads[("dil_w_o", j)] = _mm(o, dxob, "tn", BF16, "mm_dil_dwo")
        delta = _attn_delta(do, o, "dil_delta")
        dq, dk, dv, dgq, dgk = _dil_bwd(qkv, dil_g_qn[j:j + 1], dil_g_kn[j:j + 1], slopes, do, delta, lse, "dil_bwd")
        gain_g.setdefault("dil_g_qn", {})[j] = dgq
        gain_g.setdefault("dil_g_kn", {})[j] = dgk
        dqkv = jnp.concatenate([dq, dk, dv], axis=1)
        dxn = _mm(dqkv, full[("dil_w_qkv", j)], "nt", F32, "mm_dxn_dil", layer=0)
        grads[("dil_w_qkv", j)] = _mm(xn, dqkv, "tn", BF16, "mm_dwqkv")
        tok = scatter_start(f"mix_{l}", [("dil_w_qkv", j), ("dil_w_o", j)])
        toks = finish_due(dxn)
        dx, dxb, dg = _rms_bwd(xin, mix_norm[l:l + 1], dxn, dxo, "rms_bwd_dil", deps=[tok] + toks)
        gain_g.setdefault("mix_norm", {})[l] = dg
        return dx, dxb

    saved = []
    xc = x0
    for l in range(depth):
        xc, s1 = ffn_fwd(xc, ffn1_norm[l:l + 1], "ffn1", l, 3 * l, deps=[ag_token] if l == 0 else ())
        xc, s2 = (mla_fwd if l % 2 == 0 else dil_fwd)(xc, l, 3 * l + 1)
        xc, s3 = ffn_fwd(xc, ffn2_norm[l:l + 1], "ffn2", l, 3 * l + 2)
        saved.append((s1, s2, s3))

    dy, dyb, loss_part = _loss_head(xc, tgt, "loss_head")
    dx = (dy, dyb)
    loss = lax.psum(loss_part[0, 0], MESH_AXES)

    for bi in reversed(range(len(blocks))):
        tag, _ = blocks[bi]
        l = bi // 3
        s = saved[l][bi % 3]
        if bi % 3 == 2:
            dx = ffn_bwd(dx, s, ffn2_norm[l:l + 1], "ffn2", l, tag)
        elif bi % 3 == 1:
            dx = (mla_bwd if l % 2 == 0 else dil_bwd)(dx, s, l)
        else:
            dx = ffn_bwd(dx, s, ffn1_norm[l:l + 1], "ffn1", l, tag)
    grad_x = dx[0].reshape(x.shape)
    after = dx[1]
    while pending:
        after = scatter_finish(after)[-1]

    small = [n for n in names if n not in big]

    def gain_local(n):
        rows = [gain_g[n][l] for l in range(W[n].shape[0])]
        g = jnp.concatenate(rows, axis=1)
        return g

    def flat_pad(n, a):
        a = a.reshape(1, -1)
        if n in ("mla_g_qn", "mla_g_kn"):
            a = _pad_heads(a)
        return a

    packed_g = jnp.concatenate([gain_local(n) for n in small], axis=1)
    sizes = [gain_local(n).shape[1] for n in small]
    tot_g = _gain_allreduce(packed_g, "gain_allreduce")
    pw = jnp.concatenate([flat_pad(n, W[n]) for n in small], axis=1)
    pm = jnp.concatenate([flat_pad(n, M1[n]) for n in small], axis=1)
    pv = jnp.concatenate([flat_pad(n, V2[n]) for n in small], axis=1)
    res, _ = _adamw(tot_g.reshape(1, 1, -1), None, me, pw.reshape(1, 1, -1), pm.reshape(1, 1, -1),
                    pv.reshape(1, 1, -1), 0, None, "adamw_gains")
    res = [r.reshape(1, -1) for r in res]
    off = 0
    for n, sz in zip(small, sizes):
        for dst, r in zip((out_g, out_d, out_m, out_v), res):
            piece = r[:, off:off + sz]
            if n in ("mla_g_qn", "mla_g_kn"):
                piece = _unpad_heads(piece)
            dst[n] = piece.reshape(W[n].shape)
        off += sz

    return (loss, grad_x, *[out_g[n] for n in names], *[out_d[n] for n in names],
            *[out_m[n] for n in names], *[out_v[n] for n in names])
```

```python
import functools
import math

import jax
import jax.numpy as jnp
import numpy as np
from jax import lax
from jax.experimental import pallas as pl
from jax.experimental.pallas import tpu as pltpu

EPS = 1e-6
MLA_HEADS = 16
Q_LORA = 512
KV_LORA = 512
NOPE_DIM = 128
ROPE_DIM = 64
V_DIM = 128
QK_DIM = NOPE_DIM + ROPE_DIM
ROPE_THETA = 10000.0
HEAD_PAD = 256
LAT_PAD = Q_LORA + KV_LORA + 128
DIL_PAIRS = ((128, 1), (512, 4), (2048, 16))
DIL_GROUPS = 3
DIL_HEADS = 8
DIL_HEAD_DIM = 128
DIL_BLK = 128
DIL_UNROLL = 8
FLASH_HEADS = 2
LOG2E = math.log2(math.e)
LN2 = math.log(2.0)
ADAM_LR = 0.001
ADAM_B1 = 0.9
ADAM_B2 = 0.999
ADAM_EPS = 1e-08
ADAM_WD = 0.01
ADAM_STEP = 10

N_DEV = 8
MESH_AXES = ("x", "y", "c")
MESH = pl.DeviceIdType.MESH
NEG_BIG = -1e30
VMEM_LIMIT_V7X = 56 * 1024 * 1024
LANES = 128

BF16 = jnp.bfloat16
F32 = jnp.float32


def _pick(n, cands):
    for c in cands:
        if n % c == 0:
            return c
    raise ValueError(f"no tile for {n}")


def _params(sem):
    return pltpu.CompilerParams(dimension_semantics=sem, vmem_limit_bytes=VMEM_LIMIT_V7X)


ANY_SPEC = pl.BlockSpec(memory_space=pl.ANY)


MM_VMEM_BUDGET = 44 * 1024 * 1024
MM_HBM_BYTES_PER_S = 1.8e12
MM_MXU_FLOPS_PER_S = 8.5e14
MM_STEP_S = 0.4e-6
MM_MAX_TILE_MACS = 3.3e9
MXU_DIM = 256


@functools.lru_cache(maxsize=None)
def _mm_tiles(M, K, N, a_bytes, b_bytes, out_bytes, has_res):
    best = None
    for tk in [K] + [c for c in (1408, 1024, 512, 384, 256, 128) if K % c == 0 and c < K]:
        nk = K // tk
        for tm in [c for c in (2048, 1024, 512, 256, 128) if M % c == 0]:
            for tn in [c for c in (2816, 2048, 1408, 1152, 1024, 512, 384, 256, 128) if N % c == 0]:
                if tm * tk * tn > MM_MAX_TILE_MACS:
                    continue
                fill = (tn / (-(-tn // MXU_DIM) * MXU_DIM)) * (tk / (-(-tk // MXU_DIM) * MXU_DIM))
                fill *= tm / (tm + MXU_DIM // 2)
                vmem = 2 * (tm * tk * a_bytes + tk * tn * b_bytes) + 2 * tm * tn * out_bytes + tm * tn * 4
                vmem += (tm * tk + tk * tn) * 2 if max(a_bytes, b_bytes) > 2 else 0
                vmem += 2 * tm * tn * 4 if has_res else 0
                if vmem > MM_VMEM_BUDGET:
                    continue
                a_all, b_all = M * K * a_bytes, K * N * b_bytes
                if nk == 1:
                    t_i = a_all + (M // tm) * b_all
                    t_j = b_all + (N // tn) * a_all
                    traffic, i_outer = min((t_i, True), (t_j, False))
                else:
                    traffic, i_outer = (N // tn) * a_all + (M // tm) * b_all, True
                traffic += M * N * (out_bytes + (4 if has_res else 0))
                mxu = 2.0 * M * K * N / (MM_MXU_FLOPS_PER_S * fill) * (1.15 if nk > 1 else 1.0)
                cost = max(traffic / MM_HBM_BYTES_PER_S, mxu) + (M // tm) * (N // tn) * nk * MM_STEP_S
                if best is None or cost < best[0]:
                    best = (cost, tm, tn, tk, i_outer)
    assert best is not None, (M, K, N)
    return best[1:]


def _mm(a, b, mode, out_dtype, name, *, scale=1.0, res=None, layer=None, deps=()):
    b2 = b.shape[-2:]
    if mode == "nn":
        (M, K), (Kb, N) = a.shape, b2
    elif mode == "nt":
        (M, K), (N, Kb) = a.shape, b2
    else:
        (K, M), (Kb, N) = a.shape, b2
    assert K == Kb, (a.shape, b.shape, mode)
    tm, tn, tk, i_outer = _mm_tiles(M, K, N, a.dtype.itemsize, b.dtype.itemsize, jnp.dtype(out_dtype).itemsize,
                                    res is not None)
    nk = K // tk
    dims = {"nn": (((1,), (0,)), ((), ())), "nt": (((1,), (1,)), ((), ())), "tn": (((0,), (0,)), ((), ()))}[mode]

    def finish(v, r_ref, o_ref):
        if scale != 1.0:
            v = v * scale
        if r_ref is not None:
            v = r_ref[...] + v
        o_ref[...] = v.astype(o_ref.dtype)

    def body(*refs):
        a_ref, b_ref = refs[:2]
        r_ref = refs[2] if res is not None else None
        prod = lambda: lax.dot_general(a_ref[...].astype(BF16), b_ref[...].astype(BF16), dims,
                                       preferred_element_type=F32)
        if nk == 1:
            finish(prod(), r_ref, refs[-1])
            return
        o_ref, acc = refs[-2:]
        k = pl.program_id(2)

        @pl.when(k == 0)
        def _():
            acc[...] = prod()

        @pl.when(k > 0)
        def _():
            acc[...] += prod()

        @pl.when(k == nk - 1)
        def _():
            finish(acc[...], r_ref, o_ref)

    ij = (lambda p, q: (p, q)) if i_outer else (lambda p, q: (q, p))

    def spec(shape, f, lead=None):
        full = lambda p, q, k: f(*ij(p, q), k)
        if lead is None:
            return pl.BlockSpec(shape, full)
        return pl.BlockSpec((None,) + shape, lambda p, q, k: (lead,) + full(p, q, k))

    a_spec = spec((tk, tm), lambda i, j, k: (k, i)) if mode == "tn" else spec((tm, tk), lambda i, j, k: (i, k))
    lead = layer if b.ndim == 3 else None
    b_spec = spec((tn, tk), lambda i, j, k: (j, k), lead) if mode == "nt" else spec((tk, tn), lambda i, j, k: (k, j), lead)
    in_specs = [a_spec, b_spec]
    args = [a, b]
    if res is not None:
        in_specs.append(spec((tm, tn), lambda i, j, k: (i, j)))
        args.append(res)
    in_specs += [ANY_SPEC] * len(deps)
    args += list(deps)
    outer, inner = (M // tm, N // tn) if i_outer else (N // tn, M // tm)
    return pl.pallas_call(
        body, name=name, grid=(outer, inner, nk),
        in_specs=in_specs, out_specs=spec((tm, tn), lambda i, j, k: (i, j)),
        out_shape=jax.ShapeDtypeStruct((M, N), out_dtype),
        scratch_shapes=[pltpu.VMEM((tm, tn), F32)] if nk > 1 else [],
        compiler_params=_params(("parallel", "parallel", "arbitrary")),
    )(*args)


def _cast_into_gathered(w, layer, axis, me, name, deps=()):
    _, R, C = w.shape
    tr = _pick(R, (512, 256, 128, 64, 32, 16))
    nr = R // tr
    axis, paired = axis

    def body(me_ref, w_ref, *rest):
        o_ref = rest[-1]
        o_ref[...] = w_ref[...].astype(BF16)

    if axis == 1:
        out_idx = lambda i, me_ref: (0, _slot(me_ref[0], paired) * nr + i, 0)
        shape = (1, R * N_DEV, C)
    else:
        out_idx = lambda i, me_ref: (0, i, _slot(me_ref[0], paired))
        shape = (1, R, C * N_DEV)
    return pl.pallas_call(
        body, name=name,
        grid_spec=pltpu.PrefetchScalarGridSpec(
            num_scalar_prefetch=1, grid=(nr,),
            in_specs=[pl.BlockSpec((None, tr, C), lambda i, me_ref: (layer, i, 0))] + [ANY_SPEC] * len(deps),
            out_specs=pl.BlockSpec((None, tr, C), out_idx)),
        out_shape=jax.ShapeDtypeStruct(shape, BF16), compiler_params=_params(("parallel",)),
    )(me, w, *deps)


def _rms_fwd(x, g, name, deps=()):
    T, D = x.shape
    tr = _pick(T, (512, 256, 128))

    def body(x_ref, g_ref, *rest):
        o_ref = rest[-1]
        xv = x_ref[...]
        r = lax.rsqrt(jnp.mean(xv * xv, axis=-1, keepdims=True) + EPS)
        o_ref[...] = ((xv * r) * g_ref[...]).astype(BF16)

    return pl.pallas_call(
        body, name=name, grid=(T // tr,),
        in_specs=[pl.BlockSpec((tr, D), lambda i: (i, 0)), pl.BlockSpec((1, D), lambda i: (0, 0))]
        + [ANY_SPEC] * len(deps),
        out_specs=pl.BlockSpec((tr, D), lambda i: (i, 0)),
        out_shape=jax.ShapeDtypeStruct((T, D), BF16), compiler_params=_params(("parallel",)),
    )(x, g, *deps)


def _rms_bwd(x, g, dh, dres, name, deps=()):
    T, D = x.shape
    tr = _pick(T, (256, 128))

    def body(x_ref, g_ref, dh_ref, dres_ref, *rest):
        dx_ref, dxb_ref, dg_ref = rest[-3:]
        xv = x_ref[...]
        dhv = dh_ref[...]
        r = lax.rsqrt(jnp.mean(xv * xv, axis=-1, keepdims=True) + EPS)
        xhat = xv * r
        dxh = dhv * g_ref[...]
        c = jnp.mean(dxh * xhat, axis=-1, keepdims=True)
        dx = dres_ref[...] + r * (dxh - xhat * c)
        dx_ref[...] = dx
        dxb_ref[...] = dx.astype(BF16)

        @pl.when(pl.program_id(0) == 0)
        def _():
            dg_ref[...] = jnp.zeros_like(dg_ref)

        dg_ref[...] += jnp.sum(dhv * xhat, axis=0, keepdims=True)

    row = pl.BlockSpec((tr, D), lambda i: (i, 0))
    vec = pl.BlockSpec((1, D), lambda i: (0, 0))
    return pl.pallas_call(
        body, name=name, grid=(T // tr,),
        in_specs=[row, vec, row, row] + [ANY_SPEC] * len(deps), out_specs=[row, row, vec],
        out_shape=[jax.ShapeDtypeStruct((T, D), F32), jax.ShapeDtypeStruct((T, D), BF16),
                   jax.ShapeDtypeStruct((1, D), F32)],
        compiler_params=_params(("arbitrary",)),
    )(x, g, dh, dres, *deps)


N_PANEL = N_DEV // 2


def _ffn_in(h, w_in, name):
    T, D = h.shape
    F2 = w_in.shape[2]
    pw = F2 // N_PANEL
    half = pw // 2
    tm = _pick(T, (512, 256, 128))

    def body(h_ref, w_ref, u_ref, a_ref):
        r = jnp.dot(h_ref[...], w_ref[...], preferred_element_type=F32)
        u_ref[...] = r.astype(BF16)
        g, up = r[:, :half], r[:, half:]
        a_ref[...] = (g * jax.nn.sigmoid(g) * up).astype(BF16)

    return pl.pallas_call(
        body, name=name, grid=(N_PANEL, T // tm),
        in_specs=[pl.BlockSpec((tm, D), lambda p, i: (i, 0)), pl.BlockSpec((None, D, pw), lambda p, i: (0, 0, p))],
        out_specs=[pl.BlockSpec((tm, pw), lambda p, i: (i, p)), pl.BlockSpec((tm, half), lambda p, i: (i, p))],
        out_shape=[jax.ShapeDtypeStruct((T, F2), BF16), jax.ShapeDtypeStruct((T, F2 // 2), BF16)],
        compiler_params=_params(("parallel", "parallel")),
    )(h, w_in)


def _ffn_da(dxo, w_out, u, name, deps=()):
    T, D = dxo.shape
    F2 = u.shape[1]
    pw = F2 // N_PANEL
    half = pw // 2
    tm = _pick(T, (512, 256, 128))

    def body(d_ref, w_ref, u_ref, *rest):
        du_ref = rest[-1]
        da = 0.5 * lax.dot_general(d_ref[...], w_ref[...], NT_DIMS, preferred_element_type=F32)
        g = u_ref[:, :half].astype(F32)
        up = u_ref[:, half:].astype(F32)
        sg = jax.nn.sigmoid(g)
        silu = g * sg
        du_ref[:, :half] = (da * up * (sg + silu * (1.0 - sg))).astype(BF16)
        du_ref[:, half:] = (da * silu).astype(BF16)

    return pl.pallas_call(
        body, name=name, grid=(N_PANEL, T // tm),
        in_specs=[pl.BlockSpec((tm, D), lambda p, i: (i, 0)), pl.BlockSpec((None, half, D), lambda p, i: (0, p, 0)),
                  pl.BlockSpec((tm, pw), lambda p, i: (i, p))] + [ANY_SPEC] * len(deps),
        out_specs=pl.BlockSpec((tm, pw), lambda p, i: (i, p)),
        out_shape=jax.ShapeDtypeStruct((T, F2), BF16), compiler_params=_params(("parallel", "parallel")),
    )(dxo, w_out, u, *deps)


def _loss_head(y, t, name):
    T, D = y.shape
    tr = _pick(T, (512, 256, 128))

    def body(y_ref, t_ref, dy_ref, dyb_ref, l_ref):
        e = y_ref[...] - t_ref[...]
        dy = e * (1.0 / D)
        dy_ref[...] = dy
        dyb_ref[...] = dy.astype(BF16)

        @pl.when(pl.program_id(0) == 0)
        def _():
            l_ref[...] = jnp.zeros_like(l_ref)

        l_ref[...] += 0.5 * jnp.sum(jnp.mean(e * e, axis=-1, keepdims=True), axis=0, keepdims=True)

    row = pl.BlockSpec((tr, D), lambda i: (i, 0))
    return pl.pallas_call(
        body, name=name, grid=(T // tr,),
        in_specs=[row, row], out_specs=[row, row, pl.BlockSpec((1, 1), lambda i: (0, 0))],
        out_shape=[jax.ShapeDtypeStruct((T, D), F32), jax.ShapeDtypeStruct((T, D), BF16),
                   jax.ShapeDtypeStruct((1, 1), F32)],
        compiler_params=_params(("arbitrary",)),
    )(y, t)


def _rope_tables(S):
    half = ROPE_DIM // 2
    inv = 1.0 / (ROPE_THETA ** (jnp.arange(0, ROPE_DIM, 2, dtype=F32) / ROPE_DIM))
    ang = jnp.arange(S, dtype=F32)[:, None] * inv[None, :]
    cos, sin = jnp.cos(ang), jnp.sin(ang)
    z = jnp.zeros((S, half), F32)
    z2 = jnp.zeros((S, LANES - ROPE_DIM), F32)
    c = jnp.concatenate([cos, cos, z2], axis=1)
    s1 = jnp.concatenate([-sin, z, z2], axis=1)
    s2 = jnp.concatenate([z, sin, z2], axis=1)
    return c, s1, s2


def _rope(r, c, s1, s2):
    return r * c + pltpu.roll(r, LANES - ROPE_DIM // 2, 1) * s1 + pltpu.roll(r, ROPE_DIM // 2, 1) * s2


def _rope_t(d, c, s1, s2):
    return d * c + pltpu.roll(d * s1, ROPE_DIM // 2, 1) + pltpu.roll(d * s2, LANES - ROPE_DIM // 2, 1)


def _lat_norm_fwd(lat, g_cq, g_ckv, name):
    T = lat.shape[0]
    tr = _pick(T, (512, 256, 128))

    def body(lat_ref, gq_ref, gk_ref, cq_ref, ckv_ref):
        for off, g_ref, o_ref in ((0, gq_ref, cq_ref), (Q_LORA, gk_ref, ckv_ref)):
            xv = lat_ref[:, off:off + Q_LORA]
            r = lax.rsqrt(jnp.mean(xv * xv, axis=-1, keepdims=True) + EPS)
            o_ref[...] = ((xv * r) * g_ref[...]).astype(BF16)

    vec = pl.BlockSpec((1, Q_LORA), lambda i: (0, 0))
    out = pl.BlockSpec((tr, Q_LORA), lambda i: (i, 0))
    return pl.pallas_call(
        body, name=name, grid=(T // tr,),
        in_specs=[pl.BlockSpec((tr, LAT_PAD), lambda i: (i, 0)), vec, vec], out_specs=[out, out],
        out_shape=[jax.ShapeDtypeStruct((T, Q_LORA), BF16)] * 2, compiler_params=_params(("parallel",)),
    )(lat, g_cq, g_ckv)


def _lat_norm_bwd(lat, g_cq, g_ckv, dcq, dckv, dkpe, name):
    T = lat.shape[0]
    tr = _pick(T, (256, 128))

    def body(lat_ref, gq_ref, gk_ref, dcq_ref, dckv_ref, dkpe_ref, dlat_ref, dgq_ref, dgk_ref):
        @pl.when(pl.program_id(0) == 0)
        def _():
            dgq_ref[...] = jnp.zeros_like(dgq_ref)
            dgk_ref[...] = jnp.zeros_like(dgk_ref)

        for off, g_ref, d_ref, dg_ref in ((0, gq_ref, dcq_ref, dgq_ref), (Q_LORA, gk_ref, dckv_ref, dgk_ref)):
            xv = lat_ref[:, off:off + Q_LORA]
            dv = d_ref[...]
            r = lax.rsqrt(jnp.mean(xv * xv, axis=-1, keepdims=True) + EPS)
            xhat = xv * r
            dxh = dv * g_ref[...]
            c = jnp.mean(dxh * xhat, axis=-1, keepdims=True)
            dlat_ref[:, off:off + Q_LORA] = (r * (dxh - xhat * c)).astype(BF16)
            dg_ref[...] += jnp.sum(dv * xhat, axis=0, keepdims=True)
        dlat_ref[:, Q_LORA + KV_LORA:] = dkpe_ref[...].astype(BF16)

    vec = pl.BlockSpec((1, Q_LORA), lambda i: (0, 0))
    half = pl.BlockSpec((tr, Q_LORA), lambda i: (i, 0))
    full = pl.BlockSpec((tr, LAT_PAD), lambda i: (i, 0))
    return pl.pallas_call(
        body, name=name, grid=(T // tr,),
        in_specs=[full, vec, vec, half, half, pl.BlockSpec((tr, LANES), lambda i: (i, 0))],
        out_specs=[full, vec, vec],
        out_shape=[jax.ShapeDtypeStruct((T, LAT_PAD), BF16), jax.ShapeDtypeStruct((1, Q_LORA), F32),
                   jax.ShapeDtypeStruct((1, Q_LORA), F32)],
        compiler_params=_params(("arbitrary",)),
    )(lat, g_cq, g_ckv, dcq, dckv, dkpe)


def _mla_prep_fwd(q_raw, kv, lat, g_qn, g_kn, tabs, name):
    T = q_raw.shape[0]
    H = MLA_HEADS
    tr = _pick(T, (256, 128))
    scale = LOG2E / math.sqrt(QK_DIM)

    def body(q_ref, kv_ref, kpe_ref, gq_ref, gk_ref, c_ref, s1_ref, s2_ref, qf_ref, kf_ref, v_ref):
        c, s1, s2 = c_ref[...], s1_ref[...], s2_ref[...]
        gq, gk = gq_ref[...], gk_ref[...]
        kpe = kpe_ref[...]
        kpe_ss = jnp.sum(kpe * kpe, axis=-1, keepdims=True)
        for h in range(H):
            lo = h * HEAD_PAD
            qa = q_ref[:, lo:lo + LANES]
            qb = q_ref[:, lo + LANES:lo + HEAD_PAD]
            ss = jnp.sum(qa * qa, axis=-1, keepdims=True) + jnp.sum(qb * qb, axis=-1, keepdims=True)
            r = lax.rsqrt(ss * (1.0 / QK_DIM) + EPS)
            qf_ref[:, lo:lo + LANES] = (qa * r * gq[:, :LANES] * scale).astype(BF16)
            qf_ref[:, lo + LANES:lo + HEAD_PAD] = (_rope(qb * r * gq[:, LANES:], c, s1, s2) * scale).astype(BF16)
            ka = kv_ref[:, lo:lo + LANES]
            ss = jnp.sum(ka * ka, axis=-1, keepdims=True) + kpe_ss
            r = lax.rsqrt(ss * (1.0 / QK_DIM) + EPS)
            kf_ref[:, lo:lo + LANES] = (ka * r * gk[:, :LANES]).astype(BF16)
            kf_ref[:, lo + LANES:lo + HEAD_PAD] = _rope(kpe * r * gk[:, LANES:], c, s1, s2).astype(BF16)
            v_ref[:, h * V_DIM:(h + 1) * V_DIM] = kv_ref[:, lo + LANES:lo + HEAD_PAD].astype(BF16)

    wide = pl.BlockSpec((tr, H * HEAD_PAD), lambda i: (i, 0))
    lane = pl.BlockSpec((tr, LANES), lambda i: (i, 0))
    gvec = pl.BlockSpec((1, HEAD_PAD), lambda i: (0, 0))
    return pl.pallas_call(
        body, name=name, grid=(T // tr,),
        in_specs=[wide, wide, pl.BlockSpec((tr, LANES), lambda i: (i, (Q_LORA + KV_LORA) // LANES)), gvec, gvec,
                  lane, lane, lane],
        out_specs=[wide, wide, pl.BlockSpec((tr, H * V_DIM), lambda i: (i, 0))],
        out_shape=[jax.ShapeDtypeStruct((T, H * HEAD_PAD), BF16), jax.ShapeDtypeStruct((T, H * HEAD_PAD), BF16),
                   jax.ShapeDtypeStruct((T, H * V_DIM), BF16)],
        compiler_params=_params(("parallel",)),
    )(q_raw, kv, lat, g_qn, g_kn, *tabs)


def _mla_prep_bwd(q_raw, kv, lat, g_qn, g_kn, tabs, dqf, dkf, dv, name):
    T = q_raw.shape[0]
    H = MLA_HEADS
    tr = _pick(T, (128,))

    def body(q_ref, kv_ref, kpe_ref, gq_ref, gk_ref, c_ref, s1_ref, s2_ref, dqf_ref, dkf_ref, dv_ref,
             dq_ref, dkv_ref, dkpe_ref, dgq_ref, dgk_ref):
        @pl.when(pl.program_id(0) == 0)
        def _():
            dgq_ref[...] = jnp.zeros_like(dgq_ref)
            dgk_ref[...] = jnp.zeros_like(dgk_ref)

        c, s1, s2 = c_ref[...], s1_ref[...], s2_ref[...]
        gq, gk = gq_ref[...], gk_ref[...]
        kpe = kpe_ref[...]
        kpe_ss = jnp.sum(kpe * kpe, axis=-1, keepdims=True)
        dkpe = jnp.zeros_like(kpe)
        dgq_a = jnp.zeros((1, LANES), F32)
        dgq_b = jnp.zeros((1, LANES), F32)
        dgk_a = jnp.zeros((1, LANES), F32)
        dgk_b = jnp.zeros((1, LANES), F32)
        for h in range(H):
            lo = h * HEAD_PAD
            xa = q_ref[:, lo:lo + LANES]
            xb = q_ref[:, lo + LANES:lo + HEAD_PAD]
            ss = jnp.sum(xa * xa, axis=-1, keepdims=True) + jnp.sum(xb * xb, axis=-1, keepdims=True)
            r = lax.rsqrt(ss * (1.0 / QK_DIM) + EPS)
            xa, xb = xa * r, xb * r
            da = dqf_ref[:, lo:lo + LANES]
            db = _rope_t(dqf_ref[:, lo + LANES:lo + HEAD_PAD], c, s1, s2)
            dgq_a += jnp.sum(da * xa, axis=0, keepdims=True)
            dgq_b += jnp.sum(db * xb, axis=0, keepdims=True)
            da, db = da * gq[:, :LANES], db * gq[:, LANES:]
            cc = (jnp.sum(da * xa, axis=-1, keepdims=True) + jnp.sum(db * xb, axis=-1, keepdims=True)) * (1.0 / QK_DIM)
            dq_ref[:, lo:lo + LANES] = (r * (da - xa * cc)).astype(BF16)
            dq_ref[:, lo + LANES:lo + HEAD_PAD] = (r * (db - xb * cc)).astype(BF16)
            xa = kv_ref[:, lo:lo + LANES]
            ss = jnp.sum(xa * xa, axis=-1, keepdims=True) + kpe_ss
            r = lax.rsqrt(ss * (1.0 / QK_DIM) + EPS)
            xa, xb = xa * r, kpe * r
            da = dkf_ref[:, lo:lo + LANES]
            db = _rope_t(dkf_ref[:, lo + LANES:lo + HEAD_PAD], c, s1, s2)
            dgk_a += jnp.sum(da * xa, axis=0, keepdims=True)
            dgk_b += jnp.sum(db * xb, axis=0, keepdims=True)
            da, db = da * gk[:, :LANES], db * gk[:, LANES:]
            cc = (jnp.sum(da * xa, axis=-1, keepdims=True) + jnp.sum(db * xb, axis=-1, keepdims=True)) * (1.0 / QK_DIM)
            dkv_ref[:, lo:lo + LANES] = (r * (da - xa * cc)).astype(BF16)
            dkpe = dkpe + r * (db - xb * cc)
            dkv_ref[:, lo + LANES:lo + HEAD_PAD] = dv_ref[:, h * V_DIM:(h + 1) * V_DIM].astype(BF16)
        dkpe_ref[...] = dkpe
        dgq_ref[:, :LANES] += dgq_a
        dgq_ref[:, LANES:] += dgq_b
        dgk_ref[:, :LANES] += dgk_a
        dgk_ref[:, LANES:] += dgk_b

    wide = pl.BlockSpec((tr, H * HEAD_PAD), lambda i: (i, 0))
    lane = pl.BlockSpec((tr, LANES), lambda i: (i, 0))
    gvec = pl.BlockSpec((1, HEAD_PAD), lambda i: (0, 0))
    vspec = pl.BlockSpec((tr, H * V_DIM), lambda i: (i, 0))
    return pl.pallas_call(
        body, name=name, grid=(T // tr,),
        in_specs=[wide, wide, pl.BlockSpec((tr, LANES), lambda i: (i, (Q_LORA + KV_LORA) // LANES)), gvec, gvec,
                  lane, lane, lane, wide, wide, vspec],
        out_specs=[wide, wide, lane, gvec, gvec],
        out_shape=[jax.ShapeDtypeStruct((T, H * HEAD_PAD), BF16), jax.ShapeDtypeStruct((T, H * HEAD_PAD), BF16),
                   jax.ShapeDtypeStruct((T, LANES), F32), jax.ShapeDtypeStruct((1, HEAD_PAD), F32),
                   jax.ShapeDtypeStruct((1, HEAD_PAD), F32)],
        compiler_params=_params(("arbitrary",)),
    )(q_raw, kv, lat, g_qn, g_kn, *tabs, dqf, dkf, dv)


def _causal_mask(tq, tk):
    return lax.broadcasted_iota(jnp.int32, (tq, tk), 1) <= lax.broadcasted_iota(jnp.int32, (tq, tk), 0)


NT_DIMS = (((1,), (1,)), ((), ()))
TN_DIMS = (((0,), (0,)), ((), ()))


def _flash_fwd(qf, kf, v, name):
    T = qf.shape[0]
    H, G = MLA_HEADS, FLASH_HEADS
    t = _pick(T, (512, 256, 128))
    n = T // t
    pairs = [(i, j) for i in range(n) for j in range(i + 1)]
    qi = jnp.asarray([p[0] for p in pairs], jnp.int32)
    kj = jnp.asarray([p[1] for p in pairs], jnp.int32)

    def body(qi_ref, kj_ref, q_ref, k_ref, v_ref, o_ref, lse_ref, m_sc, l_sc, acc_sc):
        sid = pl.program_id(1)
        i, j = qi_ref[sid], kj_ref[sid]

        @pl.when(j == 0)
        def _():
            m_sc[...] = jnp.full_like(m_sc, NEG_BIG)
            l_sc[...] = jnp.zeros_like(l_sc)
            acc_sc[...] = jnp.zeros_like(acc_sc)

        def step(masked):
            for g in range(G):
                qk = slice(g * HEAD_PAD, (g + 1) * HEAD_PAD)
                vo = slice(g * V_DIM, (g + 1) * V_DIM)
                s = lax.dot_general(q_ref[:, qk], k_ref[:, qk], NT_DIMS, preferred_element_type=F32)
                if masked:
                    s = jnp.where(_causal_mask(t, t), s, NEG_BIG)
                m_prev = m_sc[g, :, :1]
                m_new = jnp.maximum(m_prev, jnp.max(s, axis=-1, keepdims=True))
                a = jnp.exp2(m_prev - m_new)
                p = jnp.exp2(s - m_new)
                l_sc[g] = a * l_sc[g] + jnp.sum(p, axis=-1, keepdims=True)
                acc_sc[:, vo] = a * acc_sc[:, vo] + jnp.dot(p.astype(BF16), v_ref[:, vo], preferred_element_type=F32)
                m_sc[g] = jnp.broadcast_to(m_new, (t, LANES))

        @pl.when(j < i)
        def _():
            step(False)

        @pl.when(j == i)
        def _():
            step(True)
            for g in range(G):
                vo = slice(g * V_DIM, (g + 1) * V_DIM)
                o_ref[:, vo] = (acc_sc[:, vo] / l_sc[g]).astype(BF16)
                lse_ref[:, vo] = m_sc[g] + jnp.log2(l_sc[g])

    row = pl.BlockSpec((t, G * V_DIM), lambda h, s, qi, kj: (qi[s], h))
    return pl.pallas_call(
        body, name=name,
        grid_spec=pltpu.PrefetchScalarGridSpec(
            num_scalar_prefetch=2, grid=(H // G, len(pairs)),
            in_specs=[pl.BlockSpec((t, G * HEAD_PAD), lambda h, s, qi, kj: (qi[s], h)),
                      pl.BlockSpec((t, G * HEAD_PAD), lambda h, s, qi, kj: (kj[s], h)),
                      pl.BlockSpec((t, G * V_DIM), lambda h, s, qi, kj: (kj[s], h))],
            out_specs=[row, row],
            scratch_shapes=[pltpu.VMEM((G, t, LANES), F32), pltpu.VMEM((G, t, LANES), F32),
                            pltpu.VMEM((t, G * V_DIM), F32)]),
        out_shape=[jax.ShapeDtypeStruct((T, H * V_DIM), BF16), jax.ShapeDtypeStruct((T, H * V_DIM), F32)],
        compiler_params=_params(("parallel", "arbitrary")),
    )(qi, kj, qf, kf, v)


def _attn_delta(do, o, name):
    T, W = do.shape
    nh = W // V_DIM
    tr = _pick(T, (512, 256, 128))

    def body(do_ref, o_ref, d_ref):
        for h in range(nh):
            sl = slice(h * V_DIM, (h + 1) * V_DIM)
            d = jnp.sum(do_ref[:, sl].astype(F32) * o_ref[:, sl].astype(F32), axis=-1, keepdims=True)
            d_ref[:, sl] = jnp.broadcast_to(d, (tr, V_DIM))

    row = pl.BlockSpec((tr, W), lambda i: (i, 0))
    return pl.pallas_call(
        body, name=name, grid=(T // tr,), in_specs=[row, row], out_specs=row,
        out_shape=jax.ShapeDtypeStruct((T, W), F32), compiler_params=_params(("parallel",)),
    )(do, o)


def _flash_bwd(qf, kf, v, do, lse, delta, name):
    T = qf.shape[0]
    H, G = MLA_HEADS, FLASH_HEADS
    t = _pick(T, (512, 256, 128))
    n = T // t
    scale = 1.0 / math.sqrt(QK_DIM)
    pairs = [(i, j) for j in range(n) for i in range(j, n)]
    qi = jnp.asarray([p[0] for p in pairs], jnp.int32)
    kj = jnp.asarray([p[1] for p in pairs], jnp.int32)

    def body(qi_ref, kj_ref, q_ref, k_ref, v_ref, do_ref, lse_ref, dl_ref, dq_ref, dk_ref, dv_ref, dk_acc, dv_acc):
        sid = pl.program_id(1)
        i, j = qi_ref[sid], kj_ref[sid]

        @pl.when(sid == 0)
        def _():
            dq_ref[...] = jnp.zeros_like(dq_ref)

        def step(masked):
            rows = pl.ds(pl.multiple_of(i * t, t), t)
            for g in range(G):
                qk = slice(g * HEAD_PAD, (g + 1) * HEAD_PAD)
                vo = slice(g * V_DIM, (g + 1) * V_DIM)
                q, k, v_, do_ = q_ref[:, qk], k_ref[:, qk], v_ref[:, vo], do_ref[:, vo]
                s = lax.dot_general(q, k, NT_DIMS, preferred_element_type=F32)
                if masked:
                    s = jnp.where(_causal_mask(t, t), s, NEG_BIG)
                p = jnp.exp2(s - lse_ref[:, g * V_DIM:g * V_DIM + 1])
                dp = lax.dot_general(do_, v_, NT_DIMS, preferred_element_type=F32)
                ds = (p * (dp - dl_ref[:, g * V_DIM:g * V_DIM + 1])).astype(BF16)
                dv = lax.dot_general(p.astype(BF16), do_, TN_DIMS, preferred_element_type=F32)
                dk = lax.dot_general(ds, q, TN_DIMS, preferred_element_type=F32)
                if masked:
                    dv_acc[:, vo] = dv
                    dk_acc[:, qk] = dk
                else:
                    dv_acc[:, vo] += dv
                    dk_acc[:, qk] += dk
                dq_ref[rows, qk] += jnp.dot(ds, k, preferred_element_type=F32) * scale

        @pl.when(i == j)
        def _():
            step(True)

        @pl.when(i > j)
        def _():
            step(False)

        @pl.when(i == n - 1)
        def _():
            dk_ref[...] = dk_acc[...] * LN2
            dv_ref[...] = dv_acc[...]

    qs = pl.BlockSpec((t, G * HEAD_PAD), lambda h, s, qi, kj: (qi[s], h))
    rs = pl.BlockSpec((t, G * V_DIM), lambda h, s, qi, kj: (qi[s], h))
    ks = pl.BlockSpec((t, G * HEAD_PAD), lambda h, s, qi, kj: (kj[s], h))
    vs = pl.BlockSpec((t, G * V_DIM), lambda h, s, qi, kj: (kj[s], h))
    return pl.pallas_call(
        body, name=name,
        grid_spec=pltpu.PrefetchScalarGridSpec(
            num_scalar_prefetch=2, grid=(H // G, len(pairs)), in_specs=[qs, ks, vs, rs, rs, rs],
            out_specs=[pl.BlockSpec((T, G * HEAD_PAD), lambda h, s, qi, kj: (0, h)), ks, vs],
            scratch_shapes=[pltpu.VMEM((t, G * HEAD_PAD), F32), pltpu.VMEM((t, G * V_DIM), F32)]),
        out_shape=[jax.ShapeDtypeStruct((T, H * HEAD_PAD), F32), jax.ShapeDtypeStruct((T, H * HEAD_PAD), F32),
                   jax.ShapeDtypeStruct((T, H * V_DIM), F32)],
        compiler_params=_params(("parallel", "arbitrary")),
    )(qi, kj, qf, kf, v, do, lse, delta)


def _alibi_slopes():
    tot = DIL_GROUPS * DIL_HEADS
    return [float(np.float32(2.0) ** (np.float32(-8.0) * np.float32(k) / np.float32(tot))) for k in range(1, tot + 1)]


def _dil_masks():
    iq = lax.broadcasted_iota(jnp.int32, (DIL_BLK, DIL_BLK), 0)
    ik = lax.broadcasted_iota(jnp.int32, (DIL_BLK, DIL_BLK), 1)
    return (ik >= iq), (iq + DIL_BLK - ik).astype(F32), (ik <= iq), (iq - ik).astype(F32)


def _dil_norm(x, g):
    r = lax.rsqrt(jnp.mean(x * x, axis=-1, keepdims=True) + EPS)
    return x * r, r


def _dil_fwd(qkv, g_qn, g_kn, slopes, name):
    T = qkv.shape[0]
    GH = DIL_GROUPS * DIL_HEADS
    scale = 1.0 / math.sqrt(DIL_HEAD_DIM)

    def body(sl_ref, q_ref, k_ref, v_ref, gq_ref, gk_ref, o_ref, lse_ref):
        gh = pl.program_id(0)
        slope = sl_ref[gh]
        ok_p, dist_p, ok_c, dist_c = _dil_masks()
        gq, gk = gq_ref[...], gk_ref[...]
        for g, (_, d) in enumerate(DIL_PAIRS):
            @pl.when((gh >= g * DIL_HEADS) & (gh < (g + 1) * DIL_HEADS))
            def _(d=d):
                nb = T // (d * DIL_BLK)
                bias_p = jnp.where(ok_p, -slope * d * dist_p, NEG_BIG)
                bias_c = jnp.where(ok_c, -slope * d * dist_c, NEG_BIG)

                def phase(r, _):
                    def blk(nn, _):
                        def rows(b):
                            return pl.ds(b * (d * DIL_BLK) + r, DIL_BLK, stride=d) if d > 1 else pl.ds(pl.multiple_of(b * DIL_BLK, DIL_BLK), DIL_BLK)
                        cur, prv = rows(nn), rows(jnp.maximum(nn - 1, 0))
                        q = (_dil_norm(q_ref[cur, :], gq)[0] * gq).astype(BF16)
                        kc = (_dil_norm(k_ref[cur, :], gk)[0] * gk).astype(BF16)
                        kp = (_dil_norm(k_ref[prv, :], gk)[0] * gk).astype(BF16)
                        s_c = lax.dot_general(q, kc, NT_DIMS, preferred_element_type=F32) * scale + bias_c
                        s_p = lax.dot_general(q, kp, NT_DIMS, preferred_element_type=F32) * scale + bias_p
                        s_p = jnp.where(nn > 0, s_p, NEG_BIG)
                        m = jnp.maximum(jnp.max(s_c, axis=-1, keepdims=True), jnp.max(s_p, axis=-1, keepdims=True))
                        p_c = jnp.exp(s_c - m)
                        p_p = jnp.exp(s_p - m)
                        l = jnp.sum(p_c, axis=-1, keepdims=True) + jnp.sum(p_p, axis=-1, keepdims=True)
                        acc = jnp.dot(p_c.astype(BF16), v_ref[cur, :].astype(BF16), preferred_element_type=F32)
                        acc += jnp.dot(p_p.astype(BF16), v_ref[prv, :].astype(BF16), preferred_element_type=F32)
                        o_ref[cur, :] = acc / l
                        lse_ref[cur, :] = jnp.broadcast_to(m + jnp.log(l), (DIL_BLK, DIL_HEAD_DIM))
                        return 0
                    lax.fori_loop(0, nb, blk, 0, unroll=min(nb, DIL_UNROLL))
                    return 0
                lax.fori_loop(0, d, phase, 0, unroll=min(d, max(1, DIL_UNROLL // nb)))

    col = lambda off: pl.BlockSpec((T, DIL_HEAD_DIM), lambda gh, sl: (0, gh + off))
    gvec = pl.BlockSpec((1, DIL_HEAD_DIM), lambda gh, sl: (0, 0))
    return pl.pallas_call(
        body, name=name,
        grid_spec=pltpu.PrefetchScalarGridSpec(
            num_scalar_prefetch=1, grid=(GH,),
            in_specs=[col(0), col(GH), col(2 * GH), gvec, gvec], out_specs=[col(0), col(0)]),
        out_shape=[jax.ShapeDtypeStruct((T, GH * DIL_HEAD_DIM), F32)] * 2,
        compiler_params=_params(("parallel",)),
    )(slopes, qkv, qkv, qkv, g_qn, g_kn)


def _dil_merge(o_g, lse_g, name):
    T = o_g.shape[0]
    W = DIL_HEADS * DIL_HEAD_DIM
    tr = _pick(T, (256, 128))

    def body(o0, o1, o2, l0, l1, l2, o_ref, lse_ref):
        a, b, c = l0[...], l1[...], l2[...]
        m = jnp.maximum(jnp.maximum(a, b), c)
        ea, eb, ec = jnp.exp(a - m), jnp.exp(b - m), jnp.exp(c - m)
        tot = ea + eb + ec
        o_ref[...] = ((o0[...] * ea + o1[...] * eb + o2[...] * ec) / tot).astype(BF16)
        lse_ref[...] = m + jnp.log(tot)

    grp = lambda g: pl.BlockSpec((tr, W), lambda i: (i, g))
    out = pl.BlockSpec((tr, W), lambda i: (i, 0))
    return pl.pallas_call(
        body, name=name, grid=(T // tr,),
        in_specs=[grp(0), grp(1), grp(2), grp(0), grp(1), grp(2)], out_specs=[out, out],
        out_shape=[jax.ShapeDtypeStruct((T, W), BF16), jax.ShapeDtypeStruct((T, W), F32)],
        compiler_params=_params(("parallel",)),
    )(o_g, o_g, o_g, lse_g, lse_g, lse_g)


def _dil_bwd(qkv, g_qn, g_kn, slopes, do, delta, lse, name):
    T = qkv.shape[0]
    GH = DIL_GROUPS * DIL_HEADS
    scale = 1.0 / math.sqrt(DIL_HEAD_DIM)
    nchunk = T // DIL_BLK

    def body(sl_ref, q_ref, k_ref, v_ref, gq_ref, gk_ref, do_ref, dl_ref, lse_ref,
             dq_ref, dk_ref, dv_ref, dgq_ref, dgk_ref, dq_acc, dk_acc, dv_acc):
        gh = pl.program_id(0)
        slope = sl_ref[gh]
        ok_p, dist_p, ok_c, dist_c = _dil_masks()
        gq, gk = gq_ref[...], gk_ref[...]

        @pl.when(gh == 0)
        def _():
            dgq_ref[...] = jnp.zeros_like(dgq_ref)
            dgk_ref[...] = jnp.zeros_like(dgk_ref)

        dk_acc[...] = jnp.zeros_like(dk_acc)
        dv_acc[...] = jnp.zeros_like(dv_acc)
        for g, (_, d) in enumerate(DIL_PAIRS):
            @pl.when((gh >= g * DIL_HEADS) & (gh < (g + 1) * DIL_HEADS))
            def _(d=d):
                nb = T // (d * DIL_BLK)
                bias_p = jnp.where(ok_p, -slope * d * dist_p, NEG_BIG)
                bias_c = jnp.where(ok_c, -slope * d * dist_c, NEG_BIG)

                def phase(r, _):
                    def blk(nn, _):
                        def rows(b):
                            return pl.ds(b * (d * DIL_BLK) + r, DIL_BLK, stride=d) if d > 1 else pl.ds(pl.multiple_of(b * DIL_BLK, DIL_BLK), DIL_BLK)
                        cur, prv = rows(nn), rows(jnp.maximum(nn - 1, 0))
                        q = (_dil_norm(q_ref[cur, :], gq)[0] * gq).astype(BF16)
                        kc = (_dil_norm(k_ref[cur, :], gk)[0] * gk).astype(BF16)
                        kp = (_dil_norm(k_ref[prv, :], gk)[0] * gk).astype(BF16)
                        vc = v_ref[cur, :].astype(BF16)
                        vp = v_ref[prv, :].astype(BF16)
                        dob = do_ref[cur, :].astype(BF16)
                        delta = dl_ref[cur, :][:, :1]
                        ls = lse_ref[cur, :][:, :1]
                        s_c = lax.dot_general(q, kc, NT_DIMS, preferred_element_type=F32) * scale + bias_c
                        s_p = lax.dot_general(q, kp, NT_DIMS, preferred_element_type=F32) * scale + bias_p
                        s_p = jnp.where(nn > 0, s_p, NEG_BIG)
                        p_c = jnp.exp(s_c - ls)
                        p_p = jnp.exp(s_p - ls)
                        dp_c = lax.dot_general(dob, vc, NT_DIMS, preferred_element_type=F32)
                        dp_p = lax.dot_general(dob, vp, NT_DIMS, preferred_element_type=F32)
                        ds_c = (p_c * (dp_c - delta)).astype(BF16)
                        ds_p = (p_p * (dp_p - delta)).astype(BF16)
                        dq_acc[cur, :] = (jnp.dot(ds_c, kc, preferred_element_type=F32)
                                          + jnp.dot(ds_p, kp, preferred_element_type=F32)) * scale
                        dk_acc[cur, :] += lax.dot_general(ds_c, q, TN_DIMS, preferred_element_type=F32) * scale
                        dv_acc[cur, :] += lax.dot_general(p_c.astype(BF16), dob, TN_DIMS, preferred_element_type=F32)
                        dk_acc[prv, :] += lax.dot_general(ds_p, q, TN_DIMS, preferred_element_type=F32) * scale
                        dv_acc[prv, :] += lax.dot_general(p_p.astype(BF16), dob, TN_DIMS, preferred_element_type=F32)
                        return 0
                    lax.fori_loop(0, nb, blk, 0, unroll=min(nb, DIL_UNROLL))
                    return 0
                lax.fori_loop(0, d, phase, 0, unroll=min(d, max(1, DIL_UNROLL // nb)))

        def fin(ci, carry):
            dgq, dgk = carry
            rows = pl.ds(pl.multiple_of(ci * DIL_BLK, DIL_BLK), DIL_BLK)
            outs = []
            for x_ref, d_acc, gvec in ((q_ref, dq_acc, gq), (k_ref, dk_acc, gk)):
                xhat, r = _dil_norm(x_ref[rows, :], gvec)
                dn = d_acc[rows, :]
                dxh = dn * gvec
                c = jnp.mean(dxh * xhat, axis=-1, keepdims=True)
                outs.append(((r * (dxh - xhat * c)).astype(BF16), jnp.sum(dn * xhat, axis=0, keepdims=True)))
            dq_ref[rows, :] = outs[0][0]
            dk_ref[rows, :] = outs[1][0]
            dv_ref[rows, :] = dv_acc[rows, :].astype(BF16)
            return dgq + outs[0][1], dgk + outs[1][1]

        z = jnp.zeros((1, DIL_HEAD_DIM), F32)
        dgq, dgk = lax.fori_loop(0, nchunk, fin, (z, z))
        dgq_ref[...] += dgq
        dgk_ref[...] += dgk

    col = lambda off: pl.BlockSpec((T, DIL_HEAD_DIM), lambda gh, sl: (0, gh + off))
    hcol = pl.BlockSpec((T, DIL_HEAD_DIM), lambda gh, sl: (0, gh % DIL_HEADS))
    gvec = pl.BlockSpec((1, DIL_HEAD_DIM), lambda gh, sl: (0, 0))
    wide = jax.ShapeDtypeStruct((T, GH * DIL_HEAD_DIM), BF16)
    vec = jax.ShapeDtypeStruct((1, DIL_HEAD_DIM), F32)
    return pl.pallas_call(
        body, name=name,
        grid_spec=pltpu.PrefetchScalarGridSpec(
            num_scalar_prefetch=1, grid=(GH,),
            in_specs=[col(0), col(GH), col(2 * GH), gvec, gvec, hcol, hcol, hcol],
            out_specs=[col(0), col(0), col(0), gvec, gvec],
            scratch_shapes=[pltpu.VMEM((T, DIL_HEAD_DIM), F32)] * 3),
        out_shape=[wide, wide, wide, vec, vec],
        compiler_params=_params(("arbitrary",)),
    )(slopes, qkv, qkv, qkv, g_qn, g_kn, do, delta, lse)


def _my_pos():
    return lax.axis_index("x"), lax.axis_index("y"), lax.axis_index("c")


def _peer(pos, j):
    x, y, c = pos
    px = 1 - x if j & 4 else x
    py = 1 - y if j & 2 else y
    pc = 1 - c if j & 1 else c
    return (px, py, pc), 4 * px + 2 * py + pc


def _slot(idx, paired):
    if not paired:
        return idx
    return jnp.where(idx < N_DEV // 2, 2 * idx, 2 * idx - (N_DEV - 1))


def _shard_slice(ref, axis, idx, size, paired=False):
    sl = [slice(None)] * len(ref.shape)
    sl[axis] = pl.ds(pl.multiple_of(_slot(idx, paired) * size, 8), size)
    return ref.at[tuple(sl)]


HBM_SPEC = pl.BlockSpec(memory_space=pltpu.HBM)
SEM_SPEC = pl.BlockSpec(memory_space=pltpu.SEMAPHORE)
DATAFLOW = pltpu.SideEffectType.DATAFLOW_SIDE_EFFECTING
N_PEER = N_DEV - 1


def _scatter_copy(axis, grad, slots, frm, to, dev, send_sem, recv_sem):
    ax, paired = axis
    src = _shard_slice(grad, ax, to, grad.shape[ax] // N_DEV, paired)
    return pltpu.make_async_remote_copy(src_ref=src, dst_ref=slots.at[frm], send_sem=send_sem, recv_sem=recv_sem,
                                        device_id=dev, device_id_type=MESH)


def _scatter_start(grads, axes, name):
    n = len(grads)

    def body(*refs):
        outs = refs[2 * n:]
        send, recv, token = outs[:n], outs[n:2 * n], outs[4 * n]
        pos = _my_pos()
        me = 4 * pos[0] + 2 * pos[1] + pos[2]
        for a in range(n):
            for j in range(1, N_DEV):
                dev, pid = _peer(pos, j)
                _scatter_copy(axes[a], refs[2 * a], refs[2 * a + 1], me, pid, dev, send[a].at[j - 1],
                              recv[a].at[j - 1]).start()
        token[...] = jnp.zeros_like(token)

    ops = []
    for g, (ax, _) in zip(grads, axes):
        shp = list(g.shape)
        shp[ax] //= N_DEV
        ops += [g, lax.empty((N_DEV,) + tuple(shp), g.dtype)]
    sems = [pltpu.SemaphoreType.DMA((N_PEER,))] * (2 * n)
    res = pl.pallas_call(
        body, name=name,
        out_shape=sems + [pltpu.HBM(o.shape, o.dtype) for o in ops] + [jax.ShapeDtypeStruct((8, LANES), F32)],
        in_specs=[HBM_SPEC] * len(ops),
        out_specs=[SEM_SPEC] * (2 * n) + [HBM_SPEC] * len(ops) + [pl.BlockSpec(memory_space=pltpu.VMEM)],
        input_output_aliases={i: 2 * n + i for i in range(len(ops))},
        compiler_params=pltpu.CompilerParams(has_side_effects=DATAFLOW),
    )(*[pltpu.with_memory_space_constraint(o, pltpu.HBM) for o in ops])
    items = [(res[a], res[n + a], res[2 * n + 2 * a], res[2 * n + 2 * a + 1]) for a in range(n)]
    return items, res[4 * n]


def _scatter_wait(items, axes, after, name):
    n = len(items)

    def body(*refs):
        send, recv = refs[2 * n:3 * n], refs[3 * n:4 * n]
        pos = _my_pos()
        me = 4 * pos[0] + 2 * pos[1] + pos[2]
        for a in range(n):
            for j in range(1, N_DEV):
                dev, pid = _peer(pos, j)
                cp = _scatter_copy(axes[a], refs[2 * a], refs[2 * a + 1], pid, me, dev, send[a].at[j - 1],
                                   recv[a].at[j - 1])
                cp.wait_send()
                cp.wait_recv()

    ops = [b for it in items for b in it[2:]]
    res = pl.pallas_call(
        body, name=name,
        out_shape=[pltpu.HBM(o.shape, o.dtype) for o in ops],
        in_specs=[HBM_SPEC] * len(ops) + [SEM_SPEC] * (2 * n) + [ANY_SPEC],
        out_specs=[HBM_SPEC] * len(ops),
        input_output_aliases={i: i for i in range(len(ops))},
        compiler_params=pltpu.CompilerParams(has_side_effects=DATAFLOW),
    )(*ops, *[it[0] for it in items], *[it[1] for it in items], after)
    return [res[2 * a + 1] for a in range(n)]


SIBLING = 1
ICI_PEERS = (2, 4, 6)


def _gather_copy(buf, axis, shard, dev, send_sem, recv_sem):
    ax, paired = axis
    piece = _shard_slice(buf, ax, shard, buf.shape[ax] // N_DEV, paired)
    return pltpu.make_async_remote_copy(src_ref=piece, dst_ref=piece, send_sem=send_sem, recv_sem=recv_sem,
                                        device_id=dev, device_id_type=MESH)


def _gather_start(bufs, axes, name):
    n = len(bufs)

    def body(*refs):
        ins, outs = refs[:n], refs[n:]
        send, r_sib, r_ici, token = outs[:n], outs[n:2 * n], outs[2 * n:3 * n], outs[4 * n]
        pos = _my_pos()
        me = 4 * pos[0] + 2 * pos[1] + pos[2]
        for a in range(n):
            dev, _ = _peer(pos, SIBLING)
            _gather_copy(ins[a], axes[a], me, dev, send[a].at[0], r_sib[a].at[0]).start()
            for k, j in enumerate(ICI_PEERS):
                dev, _ = _peer(pos, j)
                _gather_copy(ins[a], axes[a], me, dev, send[a].at[1 + k], r_ici[a].at[k]).start()
        token[...] = jnp.zeros_like(token)

    sems = ([pltpu.SemaphoreType.DMA((1 + len(ICI_PEERS),))] * n + [pltpu.SemaphoreType.DMA((1,))] * n
            + [pltpu.SemaphoreType.DMA((len(ICI_PEERS),))] * n)
    res = pl.pallas_call(
        body, name=name,
        out_shape=sems + [pltpu.HBM(b.shape, b.dtype) for b in bufs] + [jax.ShapeDtypeStruct((8, LANES), F32)],
        in_specs=[HBM_SPEC] * n,
        out_specs=[SEM_SPEC] * (3 * n) + [HBM_SPEC] * n + [pl.BlockSpec(memory_space=pltpu.VMEM)],
        input_output_aliases={i: 3 * n + i for i in range(n)},
        compiler_params=pltpu.CompilerParams(has_side_effects=DATAFLOW),
    )(*[pltpu.with_memory_space_constraint(b, pltpu.HBM) for b in bufs])
    items = [dict(send=res[a], r_sib=res[n + a], r_ici=res[2 * n + a], buf=res[3 * n + a]) for a in range(n)]
    return items, res[4 * n]


def _gather_relay(items, axes, after, name):
    n = len(items)

    def body(*refs):
        ins, r_ici = refs[:n], refs[n:2 * n]
        outs = refs[2 * n + 1:]
        s_rel, r_rel, token = outs[:n], outs[n:2 * n], outs[3 * n]
        pos = _my_pos()
        sib, _ = _peer(pos, SIBLING)
        for a in range(n):
            for k, j in enumerate(ICI_PEERS):
                dev, pid = _peer(pos, j)
                _gather_copy(ins[a], axes[a], pid, dev, s_rel[a].at[k], r_ici[a].at[k]).wait_recv()
                _gather_copy(ins[a], axes[a], pid, sib, s_rel[a].at[k], r_rel[a].at[k]).start()
        token[...] = jnp.zeros_like(token)

    bufs = [it["buf"] for it in items]
    sems = [pltpu.SemaphoreType.DMA((len(ICI_PEERS),))] * (2 * n)
    res = pl.pallas_call(
        body, name=name,
        out_shape=sems + [pltpu.HBM(b.shape, b.dtype) for b in bufs] + [jax.ShapeDtypeStruct((8, LANES), F32)],
        in_specs=[HBM_SPEC] * n + [SEM_SPEC] * n + [ANY_SPEC],
        out_specs=[SEM_SPEC] * (2 * n) + [HBM_SPEC] * n + [pl.BlockSpec(memory_space=pltpu.VMEM)],
        input_output_aliases={i: 2 * n + i for i in range(n)},
        compiler_params=pltpu.CompilerParams(has_side_effects=DATAFLOW),
    )(*bufs, *[it["r_ici"] for it in items], after)
    out = [dict(send=it["send"], r_sib=it["r_sib"], s_rel=res[a], r_rel=res[n + a], buf=res[2 * n + a])
           for a, it in enumerate(items)]
    return out, res[3 * n]


def _gather_wait(items, axes, after, name):
    n = len(items)

    def body(*refs):
        ins = refs[:n]
        send, r_sib, s_rel, r_rel = (refs[(1 + q) * n:(2 + q) * n] for q in range(4))
        pos = _my_pos()
        me = 4 * pos[0] + 2 * pos[1] + pos[2]
        sib, sib_id = _peer(pos, SIBLING)
        for a in range(n):
            for k in range(1 + len(ICI_PEERS)):
                _gather_copy(ins[a], axes[a], me, sib, send[a].at[k], r_sib[a].at[0]).wait_send()
            _gather_copy(ins[a], axes[a], sib_id, sib, send[a].at[0], r_sib[a].at[0]).wait_recv()
            for k, j in enumerate(ICI_PEERS):
                _, pid = _peer(pos, j)
                _, far = _peer(pos, j ^ SIBLING)
                _gather_copy(ins[a], axes[a], pid, sib, s_rel[a].at[k], r_rel[a].at[k]).wait_send()
                _gather_copy(ins[a], axes[a], far, sib, s_rel[a].at[k], r_rel[a].at[k]).wait_recv()

    bufs = [it["buf"] for it in items]
    res = pl.pallas_call(
        body, name=name,
        out_shape=[pltpu.HBM(b.shape, b.dtype) for b in bufs],
        in_specs=[HBM_SPEC] * n + [SEM_SPEC] * (4 * n) + [ANY_SPEC],
        out_specs=[HBM_SPEC] * n,
        input_output_aliases={i: i for i in range(n)},
        compiler_params=pltpu.CompilerParams(has_side_effects=DATAFLOW),
    )(*bufs, *[it["send"] for it in items], *[it["r_sib"] for it in items], *[it["s_rel"] for it in items],
      *[it["r_rel"] for it in items], after)
    return list(res)


def _gain_allreduce(v, name):
    n = v.shape[1]

    def body(v_ref, o_ref, slots, send_sems, recv_sems):
        pos = _my_pos()
        me = 4 * pos[0] + 2 * pos[1] + pos[2]
        slots[me] = v_ref[...]
        copies = []
        for j in range(1, N_DEV):
            dev, _ = _peer(pos, j)
            cp = pltpu.make_async_remote_copy(
                src_ref=slots.at[me], dst_ref=slots.at[me], send_sem=send_sems.at[j], recv_sem=recv_sems.at[j],
                device_id=dev, device_id_type=MESH)
            cp.start()
            copies.append(cp)
        for j in range(1, N_DEV):
            dev, pid = _peer(pos, j)
            pltpu.make_async_remote_copy(
                src_ref=slots.at[me], dst_ref=slots.at[pid], send_sem=send_sems.at[j], recv_sem=recv_sems.at[j],
                device_id=dev, device_id_type=MESH).wait_recv()
        for cp in copies:
            cp.wait_send()
        acc = slots[0]
        for s in range(1, N_DEV):
            acc = acc + slots[s]
        o_ref[...] = acc

    return pl.pallas_call(
        body, name=name, out_shape=jax.ShapeDtypeStruct((1, n), F32),
        in_specs=[pl.BlockSpec(memory_space=pltpu.VMEM)], out_specs=pl.BlockSpec(memory_space=pltpu.VMEM),
        scratch_shapes=[pltpu.VMEM((N_DEV, 1, n), F32), pltpu.SemaphoreType.DMA((N_DEV,)),
                        pltpu.SemaphoreType.DMA((N_DEV,))],
        compiler_params=pltpu.CompilerParams(has_side_effects=True),
    )(v)


def _adamw(parts, own, me, w, m, v, layer, prev, name, own_axis=None):
    L, R, C = w.shape
    P = parts.shape[0]
    tr = _pick(R, (128, 64, 32, 16, 8, 1))
    c1 = 1.0 - ADAM_B1 ** ADAM_STEP
    c2 = 1.0 - ADAM_B2 ** ADAM_STEP
    n_in = 4 if own is None else 5

    def body(me_ref, *refs):
        p_ref = refs[0]
        w_ref, m_ref, v_ref = refs[n_in - 3:n_in]
        g_out, d_out, m_out, v_out, tok = refs[-5:]
        g = None
        for s in range(P):
            part = p_ref[s]
            if own is not None:
                part = jnp.where(me_ref[0] == s, refs[1][...], part)
            g = part.astype(F32) if g is None else g + part.astype(F32)
        mn = ADAM_B1 * m_ref[...] + (1.0 - ADAM_B1) * g
        vn = ADAM_B2 * v_ref[...] + (1.0 - ADAM_B2) * (g * g)
        g_out[...] = g
        m_out[...] = mn
        v_out[...] = vn
        d_out[...] = -ADAM_LR * ((mn / c1) / (jnp.sqrt(vn / c2) + ADAM_EPS) + ADAM_WD * w_ref[...])
        tok[...] = jnp.zeros_like(tok)

    row = pl.BlockSpec((None, tr, C), lambda i, me_ref: (layer, i, 0))
    in_specs = [pl.BlockSpec((P, tr, C), lambda i, me_ref: (0, i, 0))]
    args = [parts]
    if own is not None:
        if own_axis is None:
            own_idx = lambda i, me_ref: (i, 0)
        elif own_axis[0] == 0:
            own_idx = lambda i, me_ref: (_slot(me_ref[0], own_axis[1]) * (R // tr) + i, 0)
        else:
            own_idx = lambda i, me_ref: (i, _slot(me_ref[0], own_axis[1]))
        in_specs.append(pl.BlockSpec((tr, C), own_idx))
        args.append(own)
    in_specs += [row, row, row]
    args += [w, m, v]
    aliases = {}
    if prev is not None:
        in_specs += [ANY_SPEC] * 4
        aliases = {1 + len(args) + k: k for k in range(4)}
        args += list(prev)
    shp = jax.ShapeDtypeStruct((L, R, C), F32)
    res = pl.pallas_call(
        body, name=name,
        grid_spec=pltpu.PrefetchScalarGridSpec(
            num_scalar_prefetch=1, grid=(R // tr,), in_specs=in_specs,
            out_specs=[row] * 4 + [pl.BlockSpec((8, LANES), lambda i, me_ref: (0, 0))]),
        out_shape=[shp] * 4 + [jax.ShapeDtypeStruct((8, LANES), F32)],
        input_output_aliases=aliases, compiler_params=_params(("arbitrary",)),
    )(me, *args)
    return res[:4], res[4]


def _pad_heads(w):
    lead = w.shape[:-1]
    n = w.shape[-1] // QK_DIM
    w = w.reshape(lead + (n, QK_DIM))
    w = jnp.pad(w, [(0, 0)] * len(lead) + [(0, 0), (0, HEAD_PAD - QK_DIM)])
    return w.reshape(lead + (n * HEAD_PAD,))


def _unpad_heads(w):
    lead = w.shape[:-1]
    n = w.shape[-1] // HEAD_PAD
    return w.reshape(lead + (n, HEAD_PAD))[..., :QK_DIM].reshape(lead + (n * QK_DIM,))


def kernel(x, ffn1_norm, ffn1_w_in, ffn1_w_out, mix_norm, ffn2_norm, ffn2_w_in, ffn2_w_out, mla_w_down, mla_g_cq, mla_g_ckv, mla_w_uq, mla_w_ukv, mla_g_qn, mla_g_kn, mla_w_o, dil_w_qkv, dil_g_qn, dil_g_kn, dil_w_o, loss_target, m_ffn1_norm, m_ffn1_w_in, m_ffn1_w_out, m_mix_norm, m_ffn2_norm, m_ffn2_w_in, m_ffn2_w_out, m_mla_w_down, m_mla_g_cq, m_mla_g_ckv, m_mla_w_uq, m_mla_w_ukv, m_mla_g_qn, m_mla_g_kn, m_mla_w_o, m_dil_w_qkv, m_dil_g_qn, m_dil_g_kn, m_dil_w_o, v_ffn1_norm, v_ffn1_w_in, v_ffn1_w_out, v_mix_norm, v_ffn2_norm, v_ffn2_w_in, v_ffn2_w_out, v_mla_w_down, v_mla_g_cq, v_mla_g_ckv, v_mla_w_uq, v_mla_w_ukv, v_mla_g_qn, v_mla_g_kn, v_mla_w_o, v_dil_w_qkv, v_dil_g_qn, v_dil_g_kn, v_dil_w_o):
    names = ["ffn1_norm", "ffn1_w_in", "ffn1_w_out", "mix_norm", "ffn2_norm", "ffn2_w_in", "ffn2_w_out", "mla_w_down",
             "mla_g_cq", "mla_g_ckv", "mla_w_uq", "mla_w_ukv", "mla_g_qn", "mla_g_kn", "mla_w_o", "dil_w_qkv",
             "dil_g_qn", "dil_g_kn", "dil_w_o"]
    W = dict(zip(names, [ffn1_norm, ffn1_w_in, ffn1_w_out, mix_norm, ffn2_norm, ffn2_w_in, ffn2_w_out, mla_w_down,
                         mla_g_cq, mla_g_ckv, mla_w_uq, mla_w_ukv, mla_g_qn, mla_g_kn, mla_w_o, dil_w_qkv,
                         dil_g_qn, dil_g_kn, dil_w_o]))
    M1 = dict(zip(names, [m_ffn1_norm, m_ffn1_w_in, m_ffn1_w_out, m_mix_norm, m_ffn2_norm, m_ffn2_w_in, m_ffn2_w_out,
                          m_mla_w_down, m_mla_g_cq, m_mla_g_ckv, m_mla_w_uq, m_mla_w_ukv, m_mla_g_qn, m_mla_g_kn,
                          m_mla_w_o, m_dil_w_qkv, m_dil_g_qn, m_dil_g_kn, m_dil_w_o]))
    V2 = dict(zip(names, [v_ffn1_norm, v_ffn1_w_in, v_ffn1_w_out, v_mix_norm, v_ffn2_norm, v_ffn2_w_in, v_ffn2_w_out,
                          v_mla_w_down, v_mla_g_cq, v_mla_g_ckv, v_mla_w_uq, v_mla_w_ukv, v_mla_g_qn, v_mla_g_kn,
                          v_mla_w_o, v_dil_w_qkv, v_dil_g_qn, v_dil_g_kn, v_dil_w_o]))
    S, D = x.shape[1], x.shape[2]
    x0 = x.reshape(S, D)
    tgt = loss_target.reshape(S, D)

    big = ["ffn1_w_in", "ffn1_w_out", "ffn2_w_in", "ffn2_w_out", "mla_w_down", "mla_w_uq", "mla_w_ukv", "mla_w_o",
           "dil_w_qkv", "dil_w_o"]
    shard_dim = {"ffn1_w_in": 2, "ffn1_w_out": 1, "ffn2_w_in": 2, "ffn2_w_out": 1, "mla_w_down": 1, "mla_w_uq": 2,
                 "mla_w_ukv": 2, "mla_w_o": 1, "dil_w_qkv": 2, "dil_w_o": 2}
    paired = ("ffn1_w_in", "ffn2_w_in")
    shard_axis = {n: (d, n in paired) for n, d in shard_dim.items()}
    grad_axis = {n: (d - 1, n in paired) for n, d in shard_dim.items()}

    def padded(n, w):
        if n == "mla_w_down":
            return jnp.pad(w, ((0, 0), (0, 0), (0, LAT_PAD - w.shape[2])))
        if n == "mla_w_uq":
            return _pad_heads(w)
        return w

    depth = ffn1_norm.shape[0]
    blocks = []
    for l in range(depth):
        mixer = (["mla_w_down", "mla_w_uq", "mla_w_ukv", "mla_w_o"] if l % 2 == 0 else ["dil_w_qkv", "dil_w_o"])
        blocks.append((f"ffn1_{l}", [("ffn1_w_in", l), ("ffn1_w_out", l)]))
        blocks.append((f"mix_{l}", [(n, l // 2) for n in mixer]))
        blocks.append((f"ffn2_{l}", [("ffn2_w_in", l), ("ffn2_w_out", l)]))
    order = [k for _, keys in blocks for k in keys]
    me = (4 * lax.axis_index("x") + 2 * lax.axis_index("y") + lax.axis_index("c")).astype(jnp.int32).reshape(1)
    def cast(key, deps=()):
        n, l = key
        return _cast_into_gathered(padded(n, W[n]), l, shard_axis[n], me, f"cast_{n}_{l}", deps=deps)

    items0, token0 = _gather_start([cast(order[0])], [shard_axis[order[0][0]]], "gather_start_first")
    rest = order[1:]
    items1, ag_token = _gather_start([cast(k, deps=[token0]) for k in rest], [shard_axis[k[0]] for k in rest],
                                     "gather_start_rest")
    ag_items = dict(zip(order, items0 + items1))
    full = {}

    def relay(keys, after, tag):
        out, token = _gather_relay([ag_items[k] for k in keys], [shard_axis[k[0]] for k in keys], after,
                                   f"gather_relay_{tag}")
        ag_items.update(zip(keys, out))
        return [token]

    def relay_next(bi, after):
        return relay(blocks[bi + 1][1], after, blocks[bi + 1][0]) if bi + 1 < len(blocks) else []

    def fetch(keys, after, tag):
        lands = _gather_wait([ag_items[k] for k in keys], [shard_axis[k[0]] for k in keys], after,
                             f"gather_wait_{tag}")
        full.update(zip(keys, lands))

    g_qn = _pad_heads(mla_g_qn)
    g_kn = _pad_heads(mla_g_kn)
    tabs = _rope_tables(S)
    slopes = jnp.asarray(_alibi_slopes(), F32)

    grads = {}
    gain_g = {}

    out_g, out_d, out_m, out_v = {}, {}, {}, {}
    pending = []
    lag = 3

    def scatter_start(tag, keys):
        items, token = _scatter_start([grads[k] for k in keys], [grad_axis[k[0]] for k in keys],
                                      f"scatter_start_{tag}")
        pending.append((tag, keys, items))
        return token

    def scatter_finish(after):
        tag, keys, items = pending.pop(0)
        lands = _scatter_wait(items, [grad_axis[k[0]] for k in keys], after, f"scatter_wait_{tag}")
        tokens = []
        for (n, l), p in zip(keys, lands):
            own, own_axis = grads[(n, l)], grad_axis[n]
            if n in ("mla_w_down", "mla_w_uq"):
                ax, pair = grad_axis[n]
                size = own.shape[ax] // N_DEV
                own = lax.dynamic_slice_in_dim(own, _slot(me[0], pair) * size, size, axis=ax)
                own_axis = None
                if n == "mla_w_down":
                    p, own = p[..., :W[n].shape[2]], own[..., :W[n].shape[2]]
                else:
                    p, own = _unpad_heads(p), _unpad_heads(own)
            prev = (out_g[n], out_d[n], out_m[n], out_v[n]) if n in out_g else None
            (out_g[n], out_d[n], out_m[n], out_v[n]), tok = _adamw(p, own, me, W[n], M1[n], V2[n], l, prev,
                                                                    f"adamw_{n}_{l}", own_axis=own_axis)
            tokens.append(tok)
        return tokens

    def finish_due(after):
        tokens = []
        while len(pending) > lag:
            tokens += scatter_finish(after)
        return tokens

    def ffn_fwd(xin, norm_row, which, l, bi, deps=()):
        tag = blocks[bi][0]
        k_in, k_out = (which + "_w_in", l), (which + "_w_out", l)
        h = _rms_fwd(xin, norm_row, f"rms_fwd_{tag}", deps=deps)
        if bi == 0:
            relay([k_in], h, f"{tag}_in")
        fetch([k_in], h, f"in_{tag}")
        u, a = _ffn_in(h, full[k_in], f"ffn_in_{tag}")
        if bi == 0:
            relay([k_out], a, f"{tag}_out")
        fetch([k_out], a, f"out_{tag}")
        toks = relay_next(bi, a)
        xo = _mm(a, full[k_out], "nn", F32, f"mm_out_{tag}", scale=0.5, res=xin, layer=0, deps=toks)
        return xo, (xin, h, u, a)

    def ffn_bwd(dx_pair, saved, norm_row, which, l, tag):
        dxo, dxob = dx_pair
        k_in, k_out = (which + "_w_in", l), (which + "_w_out", l)
        xin, h, u, a = saved
        grads[k_out] = _mm(a, dxob, "tn", BF16, f"mm_dwout_{tag}", scale=0.5)
        t_out = scatter_start(f"{tag}_out", [k_out])
        du = _ffn_da(dxob, full[k_out], u, f"ffn_da_{tag}", deps=[t_out])
        grads[k_in] = _mm(h, du, "tn", BF16, f"mm_dwin_{tag}")
        t_in = scatter_start(f"{tag}_in", [k_in])
        dh = _mm(du, full[k_in], "nt", F32, f"mm_dh_{tag}", layer=0, deps=[t_in])
        toks = finish_due(dh)
        dx, dxb, dg = _rms_bwd(xin, norm_row, dh, dxo, f"rms_bwd_{tag}", deps=toks)
        gain_g.setdefault(which + "_norm", {})[l] = dg
        return dx, dxb

    def mla_fwd(xin, l, bi):
        j = l // 2
        xn = _rms_fwd(xin, mix_norm[l:l + 1], "rms_fwd_mla")
        fetch([(n, j) for n in ("mla_w_down", "mla_w_uq", "mla_w_ukv", "mla_w_o")], xn, "mla")
        lat = _mm(xn, full[("mla_w_down", j)], "nn", F32, "mm_lat", layer=0)
        cq, ckv = _lat_norm_fwd(lat, mla_g_cq[j:j + 1], mla_g_ckv[j:j + 1], "lat_norm_fwd")
        q_raw = _mm(cq, full[("mla_w_uq", j)], "nn", F32, "mm_uq", layer=0)
        kv = _mm(ckv, full[("mla_w_ukv", j)], "nn", F32, "mm_ukv", layer=0)
        qf, kf, vb = _mla_prep_fwd(q_raw, kv, lat, g_qn[j:j + 1], g_kn[j:j + 1], tabs, "mla_prep_fwd")
        o, lse = _flash_fwd(qf, kf, vb, "flash_fwd")
        toks = relay_next(bi, o)
        xo = _mm(o, full[("mla_w_o", j)], "nn", F32, "mm_mla_o", res=xin, layer=0, deps=toks)
        return xo, (xin, xn, lat, cq, ckv, q_raw, kv, qf, kf, vb, o, lse)

    def mla_bwd(dx_pair, saved, l):
        dxo, dxob = dx_pair
        j = l // 2
        xin, xn, lat, cq, ckv, q_raw, kv, qf, kf, vb, o, lse = saved
        do = _mm(dxob, full[("mla_w_o", j)], "nt", BF16, "mm_mla_do", layer=0)
        grads[("mla_w_o", j)] = _mm(o, dxob, "tn", BF16, "mm_mla_dwo")
        delta = _attn_delta(do, o, "attn_delta")
        dqf, dkf, dv = _flash_bwd(qf, kf, vb, do, lse, delta, "flash_bwd")
        dq_raw, dkv, dkpe, dgq, dgk = _mla_prep_bwd(q_raw, kv, lat, g_qn[j:j + 1], g_kn[j:j + 1], tabs, dqf, dkf, dv,
                                                    "mla_prep_bwd")
        gain_g.setdefault("mla_g_qn", {})[j] = dgq
        gain_g.setdefault("mla_g_kn", {})[j] = dgk
        dcq = _mm(dq_raw, full[("mla_w_uq", j)], "nt", F32, "mm_dcq", layer=0)
        grads[("mla_w_uq", j)] = _mm(cq, dq_raw, "tn", BF16, "mm_dwuq")
        dckv = _mm(dkv, full[("mla_w_ukv", j)], "nt", F32, "mm_dckv", layer=0)
        grads[("mla_w_ukv", j)] = _mm(ckv, dkv, "tn", BF16, "mm_dwukv")
        dlat, dgcq, dgckv = _lat_norm_bwd(lat, mla_g_cq[j:j + 1], mla_g_ckv[j:j + 1], dcq, dckv, dkpe, "lat_norm_bwd")
        gain_g.setdefault("mla_g_cq", {})[j] = dgcq
        gain_g.setdefault("mla_g_ckv", {})[j] = dgckv
        dxn = _mm(dlat, full[("mla_w_down", j)], "nt", F32, "mm_dxn_mla", layer=0)
        grads[("mla_w_down", j)] = _mm(xn, dlat, "tn", BF16, "mm_dwdown")
        tok = scatter_start(f"mix_{l}", [(n, j) for n in ("mla_w_down", "mla_w_uq", "mla_w_ukv", "mla_w_o")])
        toks = finish_due(dxn)
        dx, dxb, dg = _rms_bwd(xin, mix_norm[l:l + 1], dxn, dxo, "rms_bwd_mla", deps=[tok] + toks)
        gain_g.setdefault("mix_norm", {})[l] = dg
        return dx, dxb

    def dil_fwd(xin, l, bi):
        j = l // 2
        xn = _rms_fwd(xin, mix_norm[l:l + 1], "rms_fwd_dil")
        fetch([("dil_w_qkv", j), ("dil_w_o", j)], xn, "dil")
        qkv = _mm(xn, full[("dil_w_qkv", j)], "nn", F32, "mm_qkv", layer=0)
        o_g, lse_g = _dil_fwd(qkv, dil_g_qn[j:j + 1], dil_g_kn[j:j + 1], slopes, "dil_fwd")
        o, lse = _dil_merge(o_g, lse_g, "dil_merge")
        toks = relay_next(bi, o)
        xo = _mm(o, full[("dil_w_o", j)], "nn", F32, "mm_dil_o", res=xin, layer=0, deps=toks)
        return xo, (xin, xn, qkv, o, lse)

    def dil_bwd(dx_pair, saved, l):
        dxo, dxob = dx_pair
        j = l // 2
        xin, xn, qkv, o, lse = saved
        do = _mm(dxob, full[("dil_w_o", j)], "nt", F32, "mm_dil_do", layer=0)
        grads[("dil_w_o", j)] = _mm(o, dxob, "tn", BF16, "mm_dil_dwo")
        delta = _attn_delta(do, o, "dil_delta")
        dq, dk, dv, dgq, dgk = _dil_bwd(qkv, dil_g_qn[j:j + 1], dil_g_kn[j:j + 1], slopes, do, delta, lse, "dil_bwd")
        gain_g.setdefault("dil_g_qn", {})[j] = dgq
        gain_g.setdefault("dil_g_kn", {})[j] = dgk
        dqkv = jnp.concatenate([dq, dk, dv], axis=1)
        dxn = _mm(dqkv, full[("dil_w_qkv", j)], "nt", F32, "mm_dxn_dil", layer=0)
        grads[("dil_w_qkv", j)] = _mm(xn, dqkv, "tn", BF16, "mm_dwqkv")
        tok = scatter_start(f"mix_{l}", [("dil_w_qkv", j), ("dil_w_o", j)])
        toks = finish_due(dxn)
        dx, dxb, dg = _rms_bwd(xin, mix_norm[l:l + 1], dxn, dxo, "rms_bwd_dil", deps=[tok] + toks)
        gain_g.setdefault("mix_norm", {})[l] = dg
        return dx, dxb

    saved = []
    xc = x0
    for l in range(depth):
        xc, s1 = ffn_fwd(xc, ffn1_norm[l:l + 1], "ffn1", l, 3 * l, deps=[ag_token] if l == 0 else ())
        xc, s2 = (mla_fwd if l % 2 == 0 else dil_fwd)(xc, l, 3 * l + 1)
        xc, s3 = ffn_fwd(xc, ffn2_norm[l:l + 1], "ffn2", l, 3 * l + 2)
        saved.append((s1, s2, s3))

    dy, dyb, loss_part = _loss_head(xc, tgt, "loss_head")
    dx = (dy, dyb)
    loss = lax.psum(loss_part[0, 0], MESH_AXES)

    for bi in reversed(range(len(blocks))):
        tag, _ = blocks[bi]
        l = bi // 3
        s = saved[l][bi % 3]
        if bi % 3 == 2:
            dx = ffn_bwd(dx, s, ffn2_norm[l:l + 1], "ffn2", l, tag)
        elif bi % 3 == 1:
            dx = (mla_bwd if l % 2 == 0 else dil_bwd)(dx, s, l)
        else:
            dx = ffn_bwd(dx, s, ffn1_norm[l:l + 1], "ffn1", l, tag)
    grad_x = dx[0].reshape(x.shape)
    after = dx[1]
    while pending:
        after = scatter_finish(after)[-1]

    small = [n for n in names if n not in big]

    def gain_local(n):
        rows = [gain_g[n][l] for l in range(W[n].shape[0])]
        g = jnp.concatenate(rows, axis=1)
        return g

    def flat_pad(n, a):
        a = a.reshape(1, -1)
        if n in ("mla_g_qn", "mla_g_kn"):
            a = _pad_heads(a)
        return a

    packed_g = jnp.concatenate([gain_local(n) for n in small], axis=1)
    sizes = [gain_local(n).shape[1] for n in small]
    tot_g = _gain_allreduce(packed_g, "gain_allreduce")
    pw = jnp.concatenate([flat_pad(n, W[n]) for n in small], axis=1)
    pm = jnp.concatenate([flat_pad(n, M1[n]) for n in small], axis=1)
    pv = jnp.concatenate([flat_pad(n, V2[n]) for n in small], axis=1)
    res, _ = _adamw(tot_g.reshape(1, 1, -1), None, me, pw.reshape(1, 1, -1), pm.reshape(1, 1, -1),
                    pv.reshape(1, 1, -1), 0, None, "adamw_gains")
    res = [r.reshape(1, -1) for r in res]
    off = 0
    for n, sz in zip(small, sizes):
        for dst, r in zip((out_g, out_d, out_m, out_v), res):
            piece = r[:, off:off + sz]
            if n in ("mla_g_qn", "mla_g_kn"):
                piece = _unpad_heads(piece)
            dst[n] = piece.reshape(W[n].shape)
        off += sz

    return (loss, grad_x, *[out_g[n] for n in names], *[out_d[n] for n in names],
            *[out_m[n] for n in names], *[out_v[n] for n in names])
```

```python
import functools
import math

import jax
import jax.numpy as jnp
import numpy as np
from jax import lax
from jax.experimental import pallas as pl
from jax.experimental.pallas import tpu as pltpu

EPS = 1e-6
MLA_HEADS = 16
Q_LORA = 512
KV_LORA = 512
NOPE_DIM = 128
ROPE_DIM = 64
V_DIM = 128
QK_DIM = NOPE_DIM + ROPE_DIM
ROPE_THETA = 10000.0
HEAD_PAD = 256
LAT_PAD = Q_LORA + KV_LORA + 128
DIL_PAIRS = ((128, 1), (512, 4), (2048, 16))
DIL_GROUPS = 3
DIL_HEADS = 8
DIL_HEAD_DIM = 128
DIL_BLK = 128
DIL_UNROLL = 8
FLASH_HEADS = 2
LOG2E = math.log2(math.e)
LN2 = math.log(2.0)
ADAM_LR = 0.001
ADAM_B1 = 0.9
ADAM_B2 = 0.999
ADAM_EPS = 1e-08
ADAM_WD = 0.01
ADAM_STEP = 10

N_DEV = 8
MESH_AXES = ("x", "y", "c")
MESH = pl.DeviceIdType.MESH
NEG_BIG = -1e30
VMEM_LIMIT_V7X = 56 * 1024 * 1024
LANES = 128

BF16 = jnp.bfloat16
F32 = jnp.float32


def _pick(n, cands):
    for c in cands:
        if n % c == 0:
            return c
    raise ValueError(f"no tile for {n}")


def _params(sem):
    return pltpu.CompilerParams(dimension_semantics=sem, vmem_limit_bytes=VMEM_LIMIT_V7X)


ANY_SPEC = pl.BlockSpec(memory_space=pl.ANY)


MM_VMEM_BUDGET = 44 * 1024 * 1024
MM_HBM_BYTES_PER_S = 1.8e12
MM_MXU_FLOPS_PER_S = 8.5e14
MM_STEP_S = 0.4e-6
MM_MAX_TILE_MACS = 3.3e9
MXU_DIM = 256


@functools.lru_cache(maxsize=None)
def _mm_tiles(M, K, N, a_bytes, b_bytes, out_bytes, has_res):
    best = None
    for tk in [K] + [c for c in (1408, 1024, 512, 384, 256, 128) if K % c == 0 and c < K]:
        nk = K // tk
        for tm in [c for c in (2048, 1024, 512, 256, 128) if M % c == 0]:
            for tn in [c for c in (2816, 2048, 1408, 1152, 1024, 512, 384, 256, 128) if N % c == 0]:
                if tm * tk * tn > MM_MAX_TILE_MACS:
                    continue
                fill = (tn / (-(-tn // MXU_DIM) * MXU_DIM)) * (tk / (-(-tk // MXU_DIM) * MXU_DIM))
                fill *= tm / (tm + MXU_DIM // 2)
                vmem = 2 * (tm * tk * a_bytes + tk * tn * b_bytes) + 2 * tm * tn * out_bytes + tm * tn * 4
                vmem += (tm * tk + tk * tn) * 2 if max(a_bytes, b_bytes) > 2 else 0
                vmem += 2 * tm * tn * 4 if has_res else 0
                if vmem > MM_VMEM_BUDGET:
                    continue
                a_all, b_all = M * K * a_bytes, K * N * b_bytes
                if nk == 1:
                    t_i = a_all + (M // tm) * b_all
                    t_j = b_all + (N // tn) * a_all
                    traffic, i_outer = min((t_i, True), (t_j, False))
                else:
                    traffic, i_outer = (N // tn) * a_all + (M // tm) * b_all, True
                traffic += M * N * (out_bytes + (4 if has_res else 0))
                mxu = 2.0 * M * K * N / (MM_MXU_FLOPS_PER_S * fill) * (1.15 if nk > 1 else 1.0)
                cost = max(traffic / MM_HBM_BYTES_PER_S, mxu) + (M // tm) * (N // tn) * nk * MM_STEP_S
                if best is None or cost < best[0]:
                    best = (cost, tm, tn, tk, i_outer)
    assert best is not None, (M, K, N)
    return best[1:]


def _mm(a, b, mode, out_dtype, name, *, scale=1.0, res=None, layer=None, deps=()):
    b2 = b.shape[-2:]
    if mode == "nn":
        (M, K), (Kb, N) = a.shape, b2
    elif mode == "nt":
        (M, K), (N, Kb) = a.shape, b2
    else:
        (K, M), (Kb, N) = a.shape, b2
    assert K == Kb, (a.shape, b.shape, mode)
    tm, tn, tk, i_outer = _mm_tiles(M, K, N, a.dtype.itemsize, b.dtype.itemsize, jnp.dtype(out_dtype).itemsize,
                                    res is not None)
    nk = K // tk
    dims = {"nn": (((1,), (0,)), ((), ())), "nt": (((1,), (1,)), ((), ())), "tn": (((0,), (0,)), ((), ()))}[mode]

    def finish(v, r_ref, o_ref):
        if scale != 1.0:
            v = v * scale
        if r_ref is not None:
            v = r_ref[...] + v
        o_ref[...] = v.astype(o_ref.dtype)

    def body(*refs):
        a_ref, b_ref = refs[:2]
        r_ref = refs[2] if res is not None else None
        prod = lambda: lax.dot_general(a_ref[...].astype(BF16), b_ref[...].astype(BF16), dims,
                                       preferred_element_type=F32)
        if nk == 1:
            finish(prod(), r_ref, refs[-1])
            return
        o_ref, acc = refs[-2:]
        k = pl.program_id(2)

        @pl.when(k == 0)
        def _():
            acc[...] = prod()

        @pl.when(k > 0)
        def _():
            acc[...] += prod()

        @pl.when(k == nk - 1)
        def _():
            finish(acc[...], r_ref, o_ref)

    ij = (lambda p, q: (p, q)) if i_outer else (lambda p, q: (q, p))

    def spec(shape, f, lead=None):
        full = lambda p, q, k: f(*ij(p, q), k)
        if lead is None:
            return pl.BlockSpec(shape, full)
        return pl.BlockSpec((None,) + shape, lambda p, q, k: (lead,) + full(p, q, k))

    a_spec = spec((tk, tm), lambda i, j, k: (k, i)) if mode == "tn" else spec((tm, tk), lambda i, j, k: (i, k))
    lead = layer if b.ndim == 3 else None
    b_spec = spec((tn, tk), lambda i, j, k: (j, k), lead) if mode == "nt" else spec((tk, tn), lambda i, j, k: (k, j), lead)
    in_specs = [a_spec, b_spec]
    args = [a, b]
    if res is not None:
        in_specs.append(spec((tm, tn), lambda i, j, k: (i, j)))
        args.append(res)
    in_specs += [ANY_SPEC] * len(deps)
    args += list(deps)
    outer, inner = (M // tm, N // tn) if i_outer else (N // tn, M // tm)
    return pl.pallas_call(
        body, name=name, grid=(outer, inner, nk),
        in_specs=in_specs, out_specs=spec((tm, tn), lambda i, j, k: (i, j)),
        out_shape=jax.ShapeDtypeStruct((M, N), out_dtype),
        scratch_shapes=[pltpu.VMEM((tm, tn), F32)] if nk > 1 else [],
        compiler_params=_params(("parallel", "parallel", "arbitrary")),
    )(*args)


def _cast_into_gathered(w, layer, axis, me, name, deps=()):
    _, R, C = w.shape
    tr = _pick(R, (512, 256, 128, 64, 32, 16))
    nr = R // tr
    axis, paired = axis

    def body(me_ref, w_ref, *rest):
        o_ref = rest[-1]
        o_ref[...] = w_ref[...].astype(BF16)

    if axis == 1:
        out_idx = lambda i, me_ref: (0, _slot(me_ref[0], paired) * nr + i, 0)
        shape = (1, R * N_DEV, C)
    else:
        out_idx = lambda i, me_ref: (0, i, _slot(me_ref[0], paired))
        shape = (1, R, C * N_DEV)
    return pl.pallas_call(
        body, name=name,
        grid_spec=pltpu.PrefetchScalarGridSpec(
            num_scalar_prefetch=1, grid=(nr,),
            in_specs=[pl.BlockSpec((None, tr, C), lambda i, me_ref: (layer, i, 0))] + [ANY_SPEC] * len(deps),
            out_specs=pl.BlockSpec((None, tr, C), out_idx)),
        out_shape=jax.ShapeDtypeStruct(shape, BF16), compiler_params=_params(("parallel",)),
    )(me, w, *deps)


def _rms_fwd(x, g, name, deps=()):
    T, D = x.shape
    tr = _pick(T, (512, 256, 128))

    def body(x_ref, g_ref, *rest):
        o_ref = rest[-1]
        xv = x_ref[...]
        r = lax.rsqrt(jnp.mean(xv * xv, axis=-1, keepdims=True) + EPS)
        o_ref[...] = ((xv * r) * g_ref[...]).astype(BF16)

    return pl.pallas_call(
        body, name=name, grid=(T // tr,),
        in_specs=[pl.BlockSpec((tr, D), lambda i: (i, 0)), pl.BlockSpec((1, D), lambda i: (0, 0))]
        + [ANY_SPEC] * len(deps),
        out_specs=pl.BlockSpec((tr, D), lambda i: (i, 0)),
        out_shape=jax.ShapeDtypeStruct((T, D), BF16), compiler_params=_params(("parallel",)),
    )(x, g, *deps)


def _rms_bwd(x, g, dh, dres, name, deps=()):
    T, D = x.shape
    tr = _pick(T, (256, 128))

    def body(x_ref, g_ref, dh_ref, dres_ref, *rest):
        dx_ref, dxb_ref, dg_ref = rest[-3:]
        xv = x_ref[...]
        dhv = dh_ref[...]
        r = lax.rsqrt(jnp.mean(xv * xv, axis=-1, keepdims=True) + EPS)
        xhat = xv * r
        dxh = dhv * g_ref[...]
        c = jnp.mean(dxh * xhat, axis=-1, keepdims=True)
        dx = dres_ref[...] + r * (dxh - xhat * c)
        dx_ref[...] = dx
        dxb_ref[...] = dx.astype(BF16)

        @pl.when(pl.program_id(0) == 0)
        def _():
            dg_ref[...] = jnp.zeros_like(dg_ref)

        dg_ref[...] += jnp.sum(dhv * xhat, axis=0, keepdims=True)

    row = pl.BlockSpec((tr, D), lambda i: (i, 0))
    vec = pl.BlockSpec((1, D), lambda i: (0, 0))
    return pl.pallas_call(
        body, name=name, grid=(T // tr,),
        in_specs=[row, vec, row, row] + [ANY_SPEC] * len(deps), out_specs=[row, row, vec],
        out_shape=[jax.ShapeDtypeStruct((T, D), F32), jax.ShapeDtypeStruct((T, D), BF16),
                   jax.ShapeDtypeStruct((1, D), F32)],
        compiler_params=_params(("arbitrary",)),
    )(x, g, dh, dres, *deps)


N_PANEL = N_DEV // 2


def _ffn_in(h, w_in, name):
    T, D = h.shape
    F2 = w_in.shape[2]
    pw = F2 // N_PANEL
    half = pw // 2
    tm = _pick(T, (512, 256, 128))

    def body(h_ref, w_ref, u_ref, a_ref):
        r = jnp.dot(h_ref[...], w_ref[...], preferred_element_type=F32)
        u_ref[...] = r.astype(BF16)
        g, up = r[:, :half], r[:, half:]
        a_ref[...] = (g * jax.nn.sigmoid(g) * up).astype(BF16)

    return pl.pallas_call(
        body, name=name, grid=(N_PANEL, T // tm),
        in_specs=[pl.BlockSpec((tm, D), lambda p, i: (i, 0)), pl.BlockSpec((None, D, pw), lambda p, i: (0, 0, p))],
        out_specs=[pl.BlockSpec((tm, pw), lambda p, i: (i, p)), pl.BlockSpec((tm, half), lambda p, i: (i, p))],
        out_shape=[jax.ShapeDtypeStruct((T, F2), BF16), jax.ShapeDtypeStruct((T, F2 // 2), BF16)],
        compiler_params=_params(("parallel", "parallel")),
    )(h, w_in)


def _ffn_da(dxo, w_out, u, name, deps=()):
    T, D = dxo.shape
    F2 = u.shape[1]
    pw = F2 // N_PANEL
    half = pw // 2
    tm = _pick(T, (512, 256, 128))

    def body(d_ref, w_ref, u_ref, *rest):
        du_ref = rest[-1]
        da = 0.5 * lax.dot_general(d_ref[...], w_ref[...], NT_DIMS, preferred_element_type=F32)
        g = u_ref[:, :half].astype(F32)
        up = u_ref[:, half:].astype(F32)
        sg = jax.nn.sigmoid(g)
        silu = g * sg
        du_ref[:, :half] = (da * up * (sg + silu * (1.0 - sg))).astype(BF16)
        du_ref[:, half:] = (da * silu).astype(BF16)

    return pl.pallas_call(
        body, name=name, grid=(N_PANEL, T // tm),
        in_specs=[pl.BlockSpec((tm, D), lambda p, i: (i, 0)), pl.BlockSpec((None, half, D), lambda p, i: (0, p, 0)),
                  pl.BlockSpec((tm, pw), lambda p, i: (i, p))] + [ANY_SPEC] * len(deps),
        out_specs=pl.BlockSpec((tm, pw), lambda p, i: (i, p)),
        out_shape=jax.ShapeDtypeStruct((T, F2), BF16), compiler_params=_params(("parallel", "parallel")),
    )(dxo, w_out, u, *deps)


def _loss_head(y, t, name):
    T, D = y.shape
    tr = _pick(T, (512, 256, 128))

    def body(y_ref, t_ref, dy_ref, dyb_ref, l_ref):
        e = y_ref[...] - t_ref[...]
        dy = e * (1.0 / D)
        dy_ref[...] = dy
        dyb_ref[...] = dy.astype(BF16)

        @pl.when(pl.program_id(0) == 0)
        def _():
            l_ref[...] = jnp.zeros_like(l_ref)

        l_ref[...] += 0.5 * jnp.sum(jnp.mean(e * e, axis=-1, keepdims=True), axis=0, keepdims=True)

    row = pl.BlockSpec((tr, D), lambda i: (i, 0))
    return pl.pallas_call(
        body, name=name, grid=(T // tr,),
        in_specs=[row, row], out_specs=[row, row, pl.BlockSpec((1, 1), lambda i: (0, 0))],
        out_shape=[jax.ShapeDtypeStruct((T, D), F32), jax.ShapeDtypeStruct((T, D), BF16),
                   jax.ShapeDtypeStruct((1, 1), F32)],
        compiler_params=_params(("arbitrary",)),
    )(y, t)


def _rope_tables(S):
    half = ROPE_DIM // 2
    inv = 1.0 / (ROPE_THETA ** (jnp.arange(0, ROPE_DIM, 2, dtype=F32) / ROPE_DIM))
    ang = jnp.arange(S, dtype=F32)[:, None] * inv[None, :]
    cos, sin = jnp.cos(ang), jnp.sin(ang)
    z = jnp.zeros((S, half), F32)
    z2 = jnp.zeros((S, LANES - ROPE_DIM), F32)
    c = jnp.concatenate([cos, cos, z2], axis=1)
    s1 = jnp.concatenate([-sin, z, z2], axis=1)
    s2 = jnp.concatenate([z, sin, z2], axis=1)
    return c, s1, s2


def _rope(r, c, s1, s2):
    return r * c + pltpu.roll(r, LANES - ROPE_DIM // 2, 1) * s1 + pltpu.roll(r, ROPE_DIM // 2, 1) * s2


def _rope_t(d, c, s1, s2):
    return d * c + pltpu.roll(d * s1, ROPE_DIM // 2, 1) + pltpu.roll(d * s2, LANES - ROPE_DIM // 2, 1)


def _lat_norm_fwd(lat, g_cq, g_ckv, name):
    T = lat.shape[0]
    tr = _pick(T, (512, 256, 128))

    def body(lat_ref, gq_ref, gk_ref, cq_ref, ckv_ref):
        for off, g_ref, o_ref in ((0, gq_ref, cq_ref), (Q_LORA, gk_ref, ckv_ref)):
            xv = lat_ref[:, off:off + Q_LORA]
            r = lax.rsqrt(jnp.mean(xv * xv, axis=-1, keepdims=True) + EPS)
            o_ref[...] = ((xv * r) * g_ref[...]).astype(BF16)

    vec = pl.BlockSpec((1, Q_LORA), lambda i: (0, 0))
    out = pl.BlockSpec((tr, Q_LORA), lambda i: (i, 0))
    return pl.pallas_call(
        body, name=name, grid=(T // tr,),
        in_specs=[pl.BlockSpec((tr, LAT_PAD), lambda i: (i, 0)), vec, vec], out_specs=[out, out],
        out_shape=[jax.ShapeDtypeStruct((T, Q_LORA), BF16)] * 2, compiler_params=_params(("parallel",)),
    )(lat, g_cq, g_ckv)


def _lat_norm_bwd(lat, g_cq, g_ckv, dcq, dckv, dkpe, name):
    T = lat.shape[0]
    tr = _pick(T, (256, 128))

    def body(lat_ref, gq_ref, gk_ref, dcq_ref, dckv_ref, dkpe_ref, dlat_ref, dgq_ref, dgk_ref):
        @pl.when(pl.program_id(0) == 0)
        def _():
            dgq_ref[...] = jnp.zeros_like(dgq_ref)
            dgk_ref[...] = jnp.zeros_like(dgk_ref)

        for off, g_ref, d_ref, dg_ref in ((0, gq_ref, dcq_ref, dgq_ref), (Q_LORA, gk_ref, dckv_ref, dgk_ref)):
            xv = lat_ref[:, off:off + Q_LORA]
            dv = d_ref[...]
            r = lax.rsqrt(jnp.mean(xv * xv, axis=-1, keepdims=True) + EPS)
            xhat = xv * r
            dxh = dv * g_ref[...]
            c = jnp.mean(dxh * xhat, axis=-1, keepdims=True)
            dlat_ref[:, off:off + Q_LORA] = (r * (dxh - xhat * c)).astype(BF16)
            dg_ref[...] += jnp.sum(dv * xhat, axis=0, keepdims=True)
        dlat_ref[:, Q_LORA + KV_LORA:] = dkpe_ref[...].astype(BF16)

    vec = pl.BlockSpec((1, Q_LORA), lambda i: (0, 0))
    half = pl.BlockSpec((tr, Q_LORA), lambda i: (i, 0))
    full = pl.BlockSpec((tr, LAT_PAD), lambda i: (i, 0))
    return pl.pallas_call(
        body, name=name, grid=(T // tr,),
        in_specs=[full, vec, vec, half, half, pl.BlockSpec((tr, LANES), lambda i: (i, 0))],
        out_specs=[full, vec, vec],
        out_shape=[jax.ShapeDtypeStruct((T, LAT_PAD), BF16), jax.ShapeDtypeStruct((1, Q_LORA), F32),
                   jax.ShapeDtypeStruct((1, Q_LORA), F32)],
        compiler_params=_params(("arbitrary",)),
    )(lat, g_cq, g_ckv, dcq, dckv, dkpe)


def _mla_prep_fwd(q_raw, kv, lat, g_qn, g_kn, tabs, name):
    T = q_raw.shape[0]
    H = MLA_HEADS
    tr = _pick(T, (256, 128))
    scale = LOG2E / math.sqrt(QK_DIM)

    def body(q_ref, kv_ref, kpe_ref, gq_ref, gk_ref, c_ref, s1_ref, s2_ref, qf_ref, kf_ref, v_ref):
        c, s1, s2 = c_ref[...], s1_ref[...], s2_ref[...]
        gq, gk = gq_ref[...], gk_ref[...]
        kpe = kpe_ref[...]
        kpe_ss = jnp.sum(kpe * kpe, axis=-1, keepdims=True)
        for h in range(H):
            lo = h * HEAD_PAD
            qa = q_ref[:, lo:lo + LANES]
            qb = q_ref[:, lo + LANES:lo + HEAD_PAD]
            ss = jnp.sum(qa * qa, axis=-1, keepdims=True) + jnp.sum(qb * qb, axis=-1, keepdims=True)
            r = lax.rsqrt(ss * (1.0 / QK_DIM) + EPS)
            qf_ref[:, lo:lo + LANES] = (qa * r * gq[:, :LANES] * scale).astype(BF16)
            qf_ref[:, lo + LANES:lo + HEAD_PAD] = (_rope(qb * r * gq[:, LANES:], c, s1, s2) * scale).astype(BF16)
            ka = kv_ref[:, lo:lo + LANES]
            ss = jnp.sum(ka * ka, axis=-1, keepdims=True) + kpe_ss
            r = lax.rsqrt(ss * (1.0 / QK_DIM) + EPS)
            kf_ref[:, lo:lo + LANES] = (ka * r * gk[:, :LANES]).astype(BF16)
            kf_ref[:, lo + LANES:lo + HEAD_PAD] = _rope(kpe * r * gk[:, LANES:], c, s1, s2).astype(BF16)
            v_ref[:, h * V_DIM:(h + 1) * V_DIM] = kv_ref[:, lo + LANES:lo + HEAD_PAD].astype(BF16)

    wide = pl.BlockSpec((tr, H * HEAD_PAD), lambda i: (i, 0))
    lane = pl.BlockSpec((tr, LANES), lambda i: (i, 0))
    gvec = pl.BlockSpec((1, HEAD_PAD), lambda i: (0, 0))
    return pl.pallas_call(
        body, name=name, grid=(T // tr,),
        in_specs=[wide, wide, pl.BlockSpec((tr, LANES), lambda i: (i, (Q_LORA + KV_LORA) // LANES)), gvec, gvec,
                  lane, lane, lane],
        out_specs=[wide, wide, pl.BlockSpec((tr, H * V_DIM), lambda i: (i, 0))],
        out_shape=[jax.ShapeDtypeStruct((T, H * HEAD_PAD), BF16), jax.ShapeDtypeStruct((T, H * HEAD_PAD), BF16),
                   jax.ShapeDtypeStruct((T, H * V_DIM), BF16)],
        compiler_params=_params(("parallel",)),
    )(q_raw, kv, lat, g_qn, g_kn, *tabs)


def _mla_prep_bwd(q_raw, kv, lat, g_qn, g_kn, tabs, dqf, dkf, dv, name):
    T = q_raw.shape[0]
    H = MLA_HEADS
    tr = _pick(T, (128,))

    def body(q_ref, kv_ref, kpe_ref, gq_ref, gk_ref, c_ref, s1_ref, s2_ref, dqf_ref, dkf_ref, dv_ref,
             dq_ref, dkv_ref, dkpe_ref, dgq_ref, dgk_ref):
        @pl.when(pl.program_id(0) == 0)
        def _():
            dgq_ref[...] = jnp.zeros_like(dgq_ref)
            dgk_ref[...] = jnp.zeros_like(dgk_ref)

        c, s1, s2 = c_ref[...], s1_ref[...], s2_ref[...]
        gq, gk = gq_ref[...], gk_ref[...]
        kpe = kpe_ref[...]
        kpe_ss = jnp.sum(kpe * kpe, axis=-1, keepdims=True)
        dkpe = jnp.zeros_like(kpe)
        dgq_a = jnp.zeros((1, LANES), F32)
        dgq_b = jnp.zeros((1, LANES), F32)
        dgk_a = jnp.zeros((1, LANES), F32)
        dgk_b = jnp.zeros((1, LANES), F32)
        for h in range(H):
            lo = h * HEAD_PAD
            xa = q_ref[:, lo:lo + LANES]
            xb = q_ref[:, lo + LANES:lo + HEAD_PAD]
            ss = jnp.sum(xa * xa, axis=-1, keepdims=True) + jnp.sum(xb * xb, axis=-1, keepdims=True)
            r = lax.rsqrt(ss * (1.0 / QK_DIM) + EPS)
            xa, xb = xa * r, xb * r
            da = dqf_ref[:, lo:lo + LANES]
            db = _rope_t(dqf_ref[:, lo + LANES:lo + HEAD_PAD], c, s1, s2)
            dgq_a += jnp.sum(da * xa, axis=0, keepdims=True)
            dgq_b += jnp.sum(db * xb, axis=0, keepdims=True)
            da, db = da * gq[:, :LANES], db * gq[:, LANES:]
            cc = (jnp.sum(da * xa, axis=-1, keepdims=True) + jnp.sum(db * xb, axis=-1, keepdims=True)) * (1.0 / QK_DIM)
            dq_ref[:, lo:lo + LANES] = (r * (da - xa * cc)).astype(BF16)
            dq_ref[:, lo + LANES:lo + HEAD_PAD] = (r * (db - xb * cc)).astype(BF16)
            xa = kv_ref[:, lo:lo + LANES]
            ss = jnp.sum(xa * xa, axis=-1, keepdims=True) + kpe_ss
            r = lax.rsqrt(ss * (1.0 / QK_DIM) + EPS)
            xa, xb = xa * r, kpe * r
            da = dkf_ref[:, lo:lo + LANES]
            db = _rope_t(dkf_ref[:, lo + LANES:lo + HEAD_PAD], c, s1, s2)
            dgk_a += jnp.sum(da * xa, axis=0, keepdims=True)
            dgk_b += jnp.sum(db * xb, axis=0, keepdims=True)
            da, db = da * gk[:, :LANES], db * gk[:, LANES:]
            cc = (jnp.sum(da * xa, axis=-1, keepdims=True) + jnp.sum(db * xb, axis=-1, keepdims=True)) * (1.0 / QK_DIM)
            dkv_ref[:, lo:lo + LANES] = (r * (da - xa * cc)).astype(BF16)
            dkpe = dkpe + r * (db - xb * cc)
            dkv_ref[:, lo + LANES:lo + HEAD_PAD] = dv_ref[:, h * V_DIM:(h + 1) * V_DIM].astype(BF16)
        dkpe_ref[...] = dkpe
        dgq_ref[:, :LANES] += dgq_a
        dgq_ref[:, LANES:] += dgq_b
        dgk_ref[:, :LANES] += dgk_a
        dgk_ref[:, LANES:] += dgk_b

    wide = pl.BlockSpec((tr, H * HEAD_PAD), lambda i: (i, 0))
    lane = pl.BlockSpec((tr, LANES), lambda i: (i, 0))
    gvec = pl.BlockSpec((1, HEAD_PAD), lambda i: (0, 0))
    vspec = pl.BlockSpec((tr, H * V_DIM), lambda i: (i, 0))
    return pl.pallas_call(
        body, name=name, grid=(T // tr,),
        in_specs=[wide, wide, pl.BlockSpec((tr, LANES), lambda i: (i, (Q_LORA + KV_LORA) // LANES)), gvec, gvec,
                  lane, lane, lane, wide, wide, vspec],
        out_specs=[wide, wide, lane, gvec, gvec],
        out_shape=[jax.ShapeDtypeStruct((T, H * HEAD_PAD), BF16), jax.ShapeDtypeStruct((T, H * HEAD_PAD), BF16),
                   jax.ShapeDtypeStruct((T, LANES), F32), jax.ShapeDtypeStruct((1, HEAD_PAD), F32),
                   jax.ShapeDtypeStruct((1, HEAD_PAD), F32)],
        compiler_params=_params(("arbitrary",)),
    )(q_raw, kv, lat, g_qn, g_kn, *tabs, dqf, dkf, dv)


def _causal_mask(tq, tk):
    return lax.broadcasted_iota(jnp.int32, (tq, tk), 1) <= lax.broadcasted_iota(jnp.int32, (tq, tk), 0)


NT_DIMS = (((1,), (1,)), ((), ()))
TN_DIMS = (((0,), (0,)), ((), ()))


def _flash_fwd(qf, kf, v, name):
    T = qf.shape[0]
    H, G = MLA_HEADS, FLASH_HEADS
    t = _pick(T, (512, 256, 128))
    n = T // t
    pairs = [(i, j) for i in range(n) for j in range(i + 1)]
    qi = jnp.asarray([p[0] for p in pairs], jnp.int32)
    kj = jnp.asarray([p[1] for p in pairs], jnp.int32)

    def body(qi_ref, kj_ref, q_ref, k_ref, v_ref, o_ref, lse_ref, m_sc, l_sc, acc_sc):
        sid = pl.program_id(1)
        i, j = qi_ref[sid], kj_ref[sid]

        @pl.when(j == 0)
        def _():
            m_sc[...] = jnp.full_like(m_sc, NEG_BIG)
            l_sc[...] = jnp.zeros_like(l_sc)
            acc_sc[...] = jnp.zeros_like(acc_sc)

        def step(masked):
            for g in range(G):
                qk = slice(g * HEAD_PAD, (g + 1) * HEAD_PAD)
                vo = slice(g * V_DIM, (g + 1) * V_DIM)
                s = lax.dot_general(q_ref[:, qk], k_ref[:, qk], NT_DIMS, preferred_element_type=F32)
                if masked:
                    s = jnp.where(_causal_mask(t, t), s, NEG_BIG)
                m_prev = m_sc[g, :, :1]
                m_new = jnp.maximum(m_prev, jnp.max(s, axis=-1, keepdims=True))
                a = jnp.exp2(m_prev - m_new)
                p = jnp.exp2(s - m_new)
                l_sc[g] = a * l_sc[g] + jnp.sum(p, axis=-1, keepdims=True)
                acc_sc[:, vo] = a * acc_sc[:, vo] + jnp.dot(p.astype(BF16), v_ref[:, vo], preferred_element_type=F32)
                m_sc[g] = jnp.broadcast_to(m_new, (t, LANES))

        @pl.when(j < i)
        def _():
            step(False)

        @pl.when(j == i)
        def _():
            step(True)
            for g in range(G):
                vo = slice(g * V_DIM, (g + 1) * V_DIM)
                o_ref[:, vo] = (acc_sc[:, vo] / l_sc[g]).astype(BF16)
                lse_ref[:, vo] = m_sc[g] + jnp.log2(l_sc[g])

    row = pl.BlockSpec((t, G * V_DIM), lambda h, s, qi, kj: (qi[s], h))
    return pl.pallas_call(
        body, name=name,
        grid_spec=pltpu.PrefetchScalarGridSpec(
            num_scalar_prefetch=2, grid=(H // G, len(pairs)),
            in_specs=[pl.BlockSpec((t, G * HEAD_PAD), lambda h, s, qi, kj: (qi[s], h)),
                      pl.BlockSpec((t, G * HEAD_PAD), lambda h, s, qi, kj: (kj[s], h)),
                      pl.BlockSpec((t, G * V_DIM), lambda h, s, qi, kj: (kj[s], h))],
            out_specs=[row, row],
            scratch_shapes=[pltpu.VMEM((G, t, LANES), F32), pltpu.VMEM((G, t, LANES), F32),
                            pltpu.VMEM((t, G * V_DIM), F32)]),
        out_shape=[jax.ShapeDtypeStruct((T, H * V_DIM), BF16), jax.ShapeDtypeStruct((T, H * V_DIM), F32)],
        compiler_params=_params(("parallel", "arbitrary")),
    )(qi, kj, qf, kf, v)


def _attn_delta(do, o, name):
    T, W = do.shape
    nh = W // V_DIM
    tr = _pick(T, (512, 256, 128))

    def body(do_ref, o_ref, d_ref):
        for h in range(nh):
            sl = slice(h * V_DIM, (h + 1) * V_DIM)
            d = jnp.sum(do_ref[:, sl].astype(F32) * o_ref[:, sl].astype(F32), axis=-1, keepdims=True)
            d_ref[:, sl] = jnp.broadcast_to(d, (tr, V_DIM))

    row = pl.BlockSpec((tr, W), lambda i: (i, 0))
    return pl.pallas_call(
        body, name=name, grid=(T // tr,), in_specs=[row, row], out_specs=row,
        out_shape=jax.ShapeDtypeStruct((T, W), F32), compiler_params=_params(("parallel",)),
    )(do, o)


def _flash_bwd(qf, kf, v, do, lse, delta, name):
    T = qf.shape[0]
    H, G = MLA_HEADS, FLASH_HEADS
    t = _pick(T, (512, 256, 128))
    n = T // t
    scale = 1.0 / math.sqrt(QK_DIM)
    pairs = [(i, j) for j in range(n) for i in range(j, n)]
    qi = jnp.asarray([p[0] for p in pairs], jnp.int32)
    kj = jnp.asarray([p[1] for p in pairs], jnp.int32)

    def body(qi_ref, kj_ref, q_ref, k_ref, v_ref, do_ref, lse_ref, dl_ref, dq_ref, dk_ref, dv_ref, dk_acc, dv_acc):
        sid = pl.program_id(1)
        i, j = qi_ref[sid], kj_ref[sid]

        @pl.when(sid == 0)
        def _():
            dq_ref[...] = jnp.zeros_like(dq_ref)

        def step(masked):
            rows = pl.ds(pl.multiple_of(i * t, t), t)
            for g in range(G):
                qk = slice(g * HEAD_PAD, (g + 1) * HEAD_PAD)
                vo = slice(g * V_DIM, (g + 1) * V_DIM)
                q, k, v_, do_ = q_ref[:, qk], k_ref[:, qk], v_ref[:, vo], do_ref[:, vo]
                s = lax.dot_general(q, k, NT_DIMS, preferred_element_type=F32)
                if masked:
                    s = jnp.where(_causal_mask(t, t), s, NEG_BIG)
                p = jnp.exp2(s - lse_ref[:, g * V_DIM:g * V_DIM + 1])
                dp = lax.dot_general(do_, v_, NT_DIMS, preferred_element_type=F32)
                ds = (p * (dp - dl_ref[:, g * V_DIM:g * V_DIM + 1])).astype(BF16)
                dv = lax.dot_general(p.astype(BF16), do_, TN_DIMS, preferred_element_type=F32)
                dk = lax.dot_general(ds, q, TN_DIMS, preferred_element_type=F32)
                if masked:
                    dv_acc[:, vo] = dv
                    dk_acc[:, qk] = dk
                else:
                    dv_acc[:, vo] += dv
                    dk_acc[:, qk] += dk
                dq_ref[rows, qk] += jnp.dot(ds, k, preferred_element_type=F32) * scale

        @pl.when(i == j)
        def _():
            step(True)

        @pl.when(i > j)
        def _():
            step(False)

        @pl.when(i == n - 1)
        def _():
            dk_ref[...] = dk_acc[...] * LN2
            dv_ref[...] = dv_acc[...]

    qs = pl.BlockSpec((t, G * HEAD_PAD), lambda h, s, qi, kj: (qi[s], h))
    rs = pl.BlockSpec((t, G * V_DIM), lambda h, s, qi, kj: (qi[s], h))
    ks = pl.BlockSpec((t, G * HEAD_PAD), lambda h, s, qi, kj: (kj[s], h))
    vs = pl.BlockSpec((t, G * V_DIM), lambda h, s, qi, kj: (kj[s], h))
    return pl.pallas_call(
        body, name=name,
        grid_spec=pltpu.PrefetchScalarGridSpec(
            num_scalar_prefetch=2, grid=(H // G, len(pairs)), in_specs=[qs, ks, vs, rs, rs, rs],
            out_specs=[pl.BlockSpec((T, G * HEAD_PAD), lambda h, s, qi, kj: (0, h)), ks, vs],
            scratch_shapes=[pltpu.VMEM((t, G * HEAD_PAD), F32), pltpu.VMEM((t, G * V_DIM), F32)]),
        out_shape=[jax.ShapeDtypeStruct((T, H * HEAD_PAD), F32), jax.ShapeDtypeStruct((T, H * HEAD_PAD), F32),
                   jax.ShapeDtypeStruct((T, H * V_DIM), F32)],
        compiler_params=_params(("parallel", "arbitrary")),
    )(qi, kj, qf, kf, v, do, lse, delta)


def _alibi_slopes():
    tot = DIL_GROUPS * DIL_HEADS
    return [float(np.float32(2.0) ** (np.float32(-8.0) * np.float32(k) / np.float32(tot))) for k in range(1, tot + 1)]


def _dil_masks():
    iq = lax.broadcasted_iota(jnp.int32, (DIL_BLK, DIL_BLK), 0)
    ik = lax.broadcasted_iota(jnp.int32, (DIL_BLK, DIL_BLK), 1)
    return (ik >= iq), (iq + DIL_BLK - ik).astype(F32), (ik <= iq), (iq - ik).astype(F32)


def _dil_norm(x, g):
    r = lax.rsqrt(jnp.mean(x * x, axis=-1, keepdims=True) + EPS)
    return x * r, r


def _dil_fwd(qkv, g_qn, g_kn, slopes, name):
    T = qkv.shape[0]
    GH = DIL_GROUPS * DIL_HEADS
    scale = 1.0 / math.sqrt(DIL_HEAD_DIM)

    def body(sl_ref, q_ref, k_ref, v_ref, gq_ref, gk_ref, o_ref, lse_ref):
        gh = pl.program_id(0)
        slope = sl_ref[gh]
        ok_p, dist_p, ok_c, dist_c = _dil_masks()
        gq, gk = gq_ref[...], gk_ref[...]
        for g, (_, d) in enumerate(DIL_PAIRS):
            @pl.when((gh >= g * DIL_HEADS) & (gh < (g + 1) * DIL_HEADS))
            def _(d=d):
                nb = T // (d * DIL_BLK)
                bias_p = jnp.where(ok_p, -slope * d * dist_p, NEG_BIG)
                bias_c = jnp.where(ok_c, -slope * d * dist_c, NEG_BIG)

                def phase(r, _):
                    def blk(nn, _):
                        def rows(b):
                            return pl.ds(b * (d * DIL_BLK) + r, DIL_BLK, stride=d) if d > 1 else pl.ds(pl.multiple_of(b * DIL_BLK, DIL_BLK), DIL_BLK)
                        cur, prv = rows(nn), rows(jnp.maximum(nn - 1, 0))
                        q = (_dil_norm(q_ref[cur, :], gq)[0] * gq).astype(BF16)
                        kc = (_dil_norm(k_ref[cur, :], gk)[0] * gk).astype(BF16)
                        kp = (_dil_norm(k_ref[prv, :], gk)[0] * gk).astype(BF16)
                        s_c = lax.dot_general(q, kc, NT_DIMS, preferred_element_type=F32) * scale + bias_c
                        s_p = lax.dot_general(q, kp, NT_DIMS, preferred_element_type=F32) * scale + bias_p
                        s_p = jnp.where(nn > 0, s_p, NEG_BIG)
                        m = jnp.maximum(jnp.max(s_c, axis=-1, keepdims=True), jnp.max(s_p, axis=-1, keepdims=True))
                        p_c = jnp.exp(s_c - m)
                        p_p = jnp.exp(s_p - m)
                        l = jnp.sum(p_c, axis=-1, keepdims=True) + jnp.sum(p_p, axis=-1, keepdims=True)
                        acc = jnp.dot(p_c.astype(BF16), v_ref[cur, :].astype(BF16), preferred_element_type=F32)
                        acc += jnp.dot(p_p.astype(BF16), v_ref[prv, :].astype(BF16), preferred_element_type=F32)
                        o_ref[cur, :] = acc / l
                        lse_ref[cur, :] = jnp.broadcast_to(m + jnp.log(l), (DIL_BLK, DIL_HEAD_DIM))
                        return 0
                    lax.fori_loop(0, nb, blk, 0, unroll=min(nb, DIL_UNROLL))
                    return 0
                lax.fori_loop(0, d, phase, 0, unroll=min(d, max(1, DIL_UNROLL // nb)))

    col = lambda off: pl.BlockSpec((T, DIL_HEAD_DIM), lambda gh, sl: (0, gh + off))
    gvec = pl.BlockSpec((1, DIL_HEAD_DIM), lambda gh, sl: (0, 0))
    return pl.pallas_call(
        body, name=name,
        grid_spec=pltpu.PrefetchScalarGridSpec(
            num_scalar_prefetch=1, grid=(GH,),
            in_specs=[col(0), col(GH), col(2 * GH), gvec, gvec], out_specs=[col(0), col(0)]),
        out_shape=[jax.ShapeDtypeStruct((T, GH * DIL_HEAD_DIM), F32)] * 2,
        compiler_params=_params(("parallel",)),
    )(slopes, qkv, qkv, qkv, g_qn, g_kn)


def _dil_merge(o_g, lse_g, name):
    T = o_g.shape[0]
    W = DIL_HEADS * DIL_HEAD_DIM
    tr = _pick(T, (256, 128))

    def body(o0, o1, o2, l0, l1, l2, o_ref, lse_ref):
        a, b, c = l0[...], l1[...], l2[...]
        m = jnp.maximum(jnp.maximum(a, b), c)
        ea, eb, ec = jnp.exp(a - m), jnp.exp(b - m), jnp.exp(c - m)
        tot = ea + eb + ec
        o_ref[...] = ((o0[...] * ea + o1[...] * eb + o2[...] * ec) / tot).astype(BF16)
        lse_ref[...] = m + jnp.log(tot)

    grp = lambda g: pl.BlockSpec((tr, W), lambda i: (i, g))
    out = pl.BlockSpec((tr, W), lambda i: (i, 0))
    return pl.pallas_call(
        body, name=name, grid=(T // tr,),
        in_specs=[grp(0), grp(1), grp(2), grp(0), grp(1), grp(2)], out_specs=[out, out],
        out_shape=[jax.ShapeDtypeStruct((T, W), BF16), jax.ShapeDtypeStruct((T, W), F32)],
        compiler_params=_params(("parallel",)),
    )(o_g, o_g, o_g, lse_g, lse_g, lse_g)


def _dil_bwd(qkv, g_qn, g_kn, slopes, do, delta, lse, name):
    T = qkv.shape[0]
    GH = DIL_GROUPS * DIL_HEADS
    scale = 1.0 / math.sqrt(DIL_HEAD_DIM)
    nchunk = T // DIL_BLK

    def body(sl_ref, q_ref, k_ref, v_ref, gq_ref, gk_ref, do_ref, dl_ref, lse_ref,
             dq_ref, dk_ref, dv_ref, dgq_ref, dgk_ref, dq_acc, dk_acc, dv_acc):
        gh = pl.program_id(0)
        slope = sl_ref[gh]
        ok_p, dist_p, ok_c, dist_c = _dil_masks()
        gq, gk = gq_ref[...], gk_ref[...]

        @pl.when(gh == 0)
        def _():
            dgq_ref[...] = jnp.zeros_like(dgq_ref)
            dgk_ref[...] = jnp.zeros_like(dgk_ref)

        dk_acc[...] = jnp.zeros_like(dk_acc)
        dv_acc[...] = jnp.zeros_like(dv_acc)
        for g, (_, d) in enumerate(DIL_PAIRS):
            @pl.when((gh >= g * DIL_HEADS) & (gh < (g + 1) * DIL_HEADS))
            def _(d=d):
                nb = T // (d * DIL_BLK)
                bias_p = jnp.where(ok_p, -slope * d * dist_p, NEG_BIG)
                bias_c = jnp.where(ok_c, -slope * d * dist_c, NEG_BIG)

                def phase(r, _):
                    def blk(nn, _):
                        def rows(b):
                            return pl.ds(b * (d * DIL_BLK) + r, DIL_BLK, stride=d) if d > 1 else pl.ds(pl.multiple_of(b * DIL_BLK, DIL_BLK), DIL_BLK)
                        cur, prv = rows(nn), rows(jnp.maximum(nn - 1, 0))
                        q = (_dil_norm(q_ref[cur, :], gq)[0] * gq).astype(BF16)
                        kc = (_dil_norm(k_ref[cur, :], gk)[0] * gk).astype(BF16)
                        kp = (_dil_norm(k_ref[prv, :], gk)[0] * gk).astype(BF16)
                        vc = v_ref[cur, :].astype(BF16)
                        vp = v_ref[prv, :].astype(BF16)
                        dob = do_ref[cur, :].astype(BF16)
                        delta = dl_ref[cur, :][:, :1]
                        ls = lse_ref[cur, :][:, :1]
                        s_c = lax.dot_general(q, kc, NT_DIMS, preferred_element_type=F32) * scale + bias_c
                        s_p = lax.dot_general(q, kp, NT_DIMS, preferred_element_type=F32) * scale + bias_p
                        s_p = jnp.where(nn > 0, s_p, NEG_BIG)
                        p_c = jnp.exp(s_c - ls)
                        p_p = jnp.exp(s_p - ls)
                        dp_c = lax.dot_general(dob, vc, NT_DIMS, preferred_element_type=F32)
                        dp_p = lax.dot_general(dob, vp, NT_DIMS, preferred_element_type=F32)
                        ds_c = (p_c * (dp_c - delta)).astype(BF16)
                        ds_p = (p_p * (dp_p - delta)).astype(BF16)
                        dq_acc[cur, :] = (jnp.dot(ds_c, kc, preferred_element_type=F32)
                                          + jnp.dot(ds_p, kp, preferred_element_type=F32)) * scale
                        dk_acc[cur, :] += lax.dot_general(ds_c, q, TN_DIMS, preferred_element_type=F32) * scale
                        dv_acc[cur, :] += lax.dot_general(p_c.astype(BF16), dob, TN_DIMS, preferred_element_type=F32)
                        dk_acc[prv, :] += lax.dot_general(ds_p, q, TN_DIMS, preferred_element_type=F32) * scale
                        dv_acc[prv, :] += lax.dot_general(p_p.astype(BF16), dob, TN_DIMS, preferred_element_type=F32)
                        return 0
                    lax.fori_loop(0, nb, blk, 0, unroll=min(nb, DIL_UNROLL))
                    return 0
                lax.fori_loop(0, d, phase, 0, unroll=min(d, max(1, DIL_UNROLL // nb)))

        def fin(ci, carry):
            dgq, dgk = carry
            rows = pl.ds(pl.multiple_of(ci * DIL_BLK, DIL_BLK), DIL_BLK)
            outs = []
            for x_ref, d_acc, gvec in ((q_ref, dq_acc, gq), (k_ref, dk_acc, gk)):
                xhat, r = _dil_norm(x_ref[rows, :], gvec)
                dn = d_acc[rows, :]
                dxh = dn * gvec
                c = jnp.mean(dxh * xhat, axis=-1, keepdims=True)
                outs.append(((r * (dxh - xhat * c)).astype(BF16), jnp.sum(dn * xhat, axis=0, keepdims=True)))
            dq_ref[rows, :] = outs[0][0]
            dk_ref[rows, :] = outs[1][0]
            dv_ref[rows, :] = dv_acc[rows, :].astype(BF16)
            return dgq + outs[0][1], dgk + outs[1][1]

        z = jnp.zeros((1, DIL_HEAD_DIM), F32)
        dgq, dgk = lax.fori_loop(0, nchunk, fin, (z, z))
        dgq_ref[...] += dgq
        dgk_ref[...] += dgk

    col = lambda off: pl.BlockSpec((T, DIL_HEAD_DIM), lambda gh, sl: (0, gh + off))
    hcol = pl.BlockSpec((T, DIL_HEAD_DIM), lambda gh, sl: (0, gh % DIL_HEADS))
    gvec = pl.BlockSpec((1, DIL_HEAD_DIM), lambda gh, sl: (0, 0))
    wide = jax.ShapeDtypeStruct((T, GH * DIL_HEAD_DIM), BF16)
    vec = jax.ShapeDtypeStruct((1, DIL_HEAD_DIM), F32)
    return pl.pallas_call(
        body, name=name,
        grid_spec=pltpu.PrefetchScalarGridSpec(
            num_scalar_prefetch=1, grid=(GH,),
            in_specs=[col(0), col(GH), col(2 * GH), gvec, gvec, hcol, hcol, hcol],
            out_specs=[col(0), col(0), col(0), gvec, gvec],
            scratch_shapes=[pltpu.VMEM((T, DIL_HEAD_DIM), F32)] * 3),
        out_shape=[wide, wide, wide, vec, vec],
        compiler_params=_params(("arbitrary",)),
    )(slopes, qkv, qkv, qkv, g_qn, g_kn, do, delta, lse)


def _my_pos():
    return lax.axis_index("x"), lax.axis_index("y"), lax.axis_index("c")


def _peer(pos, j):
    x, y, c = pos
    px = 1 - x if j & 4 else x
    py = 1 - y if j & 2 else y
    pc = 1 - c if j & 1 else c
    return (px, py, pc), 4 * px + 2 * py + pc


def _slot(idx, paired):
    if not paired:
        return idx
    return jnp.where(idx < N_DEV // 2, 2 * idx, 2 * idx - (N_DEV - 1))


def _shard_slice(ref, axis, idx, size, paired=False):
    sl = [slice(None)] * len(ref.shape)
    sl[axis] = pl.ds(pl.multiple_of(_slot(idx, paired) * size, 8), size)
    return ref.at[tuple(sl)]


HBM_SPEC = pl.BlockSpec(memory_space=pltpu.HBM)
SEM_SPEC = pl.BlockSpec(memory_space=pltpu.SEMAPHORE)
DATAFLOW = pltpu.SideEffectType.DATAFLOW_SIDE_EFFECTING
N_PEER = N_DEV - 1


def _scatter_copy(axis, grad, slots, frm, to, dev, send_sem, recv_sem):
    ax, paired = axis
    src = _shard_slice(grad, ax, to, grad.shape[ax] // N_DEV, paired)
    return pltpu.make_async_remote_copy(src_ref=src, dst_ref=slots.at[frm], send_sem=send_sem, recv_sem=recv_sem,
                                        device_id=dev, device_id_type=MESH)


def _scatter_start(grads, axes, name):
    n = len(grads)

    def body(*refs):
        outs = refs[2 * n:]
        send, recv, token = outs[:n], outs[n:2 * n], outs[4 * n]
        pos = _my_pos()
        me = 4 * pos[0] + 2 * pos[1] + pos[2]
        for a in range(n):
            for j in range(1, N_DEV):
                dev, pid = _peer(pos, j)
                _scatter_copy(axes[a], refs[2 * a], refs[2 * a + 1], me, pid, dev, send[a].at[j - 1],
                              recv[a].at[j - 1]).start()
        token[...] = jnp.zeros_like(token)

    ops = []
    for g, (ax, _) in zip(grads, axes):
        shp = list(g.shape)
        shp[ax] //= N_DEV
        ops += [g, lax.empty((N_DEV,) + tuple(shp), g.dtype)]
    sems = [pltpu.SemaphoreType.DMA((N_PEER,))] * (2 * n)
    res = pl.pallas_call(
        body, name=name,
        out_shape=sems + [pltpu.HBM(o.shape, o.dtype) for o in ops] + [jax.ShapeDtypeStruct((8, LANES), F32)],
        in_specs=[HBM_SPEC] * len(ops),
        out_specs=[SEM_SPEC] * (2 * n) + [HBM_SPEC] * len(ops) + [pl.BlockSpec(memory_space=pltpu.VMEM)],
        input_output_aliases={i: 2 * n + i for i in range(len(ops))},
        compiler_params=pltpu.CompilerParams(has_side_effects=DATAFLOW),
    )(*[pltpu.with_memory_space_constraint(o, pltpu.HBM) for o in ops])
    items = [(res[a], res[n + a], res[2 * n + 2 * a], res[2 * n + 2 * a + 1]) for a in range(n)]
    return items, res[4 * n]


def _scatter_wait(items, axes, after, name):
    n = len(items)

    def body(*refs):
        send, recv = refs[2 * n:3 * n], refs[3 * n:4 * n]
        pos = _my_pos()
        me = 4 * pos[0] + 2 * pos[1] + pos[2]
        for a in range(n):
            for j in range(1, N_DEV):
                dev, pid = _peer(pos, j)
                cp = _scatter_copy(axes[a], refs[2 * a], refs[2 * a + 1], pid, me, dev, send[a].at[j - 1],
                                   recv[a].at[j - 1])
                cp.wait_send()
                cp.wait_recv()

    ops = [b for it in items for b in it[2:]]
    res = pl.pallas_call(
        body, name=name,
        out_shape=[pltpu.HBM(o.shape, o.dtype) for o in ops],
        in_specs=[HBM_SPEC] * len(ops) + [SEM_SPEC] * (2 * n) + [ANY_SPEC],
        out_specs=[HBM_SPEC] * len(ops),
        input_output_aliases={i: i for i in range(len(ops))},
        compiler_params=pltpu.CompilerParams(has_side_effects=DATAFLOW),
    )(*ops, *[it[0] for it in items], *[it[1] for it in items], after)
    return [(res[2 * a], res[2 * a + 1]) for a in range(n)]


SIBLING = 1
ICI_PEERS = (2, 4, 6)


def _gather_copy(buf, axis, shard, dev, send_sem, recv_sem):
    ax, paired = axis
    piece = _shard_slice(buf, ax, shard, buf.shape[ax] // N_DEV, paired)
    return pltpu.make_async_remote_copy(src_ref=piece, dst_ref=piece, send_sem=send_sem, recv_sem=recv_sem,
                                        device_id=dev, device_id_type=MESH)


def _gather_start(bufs, axes, name):
    n = len(bufs)

    def body(*refs):
        ins, outs = refs[:n], refs[n:]
        send, r_sib, r_ici, token = outs[:n], outs[n:2 * n], outs[2 * n:3 * n], outs[4 * n]
        pos = _my_pos()
        me = 4 * pos[0] + 2 * pos[1] + pos[2]
        for a in range(n):
            dev, _ = _peer(pos, SIBLING)
            _gather_copy(ins[a], axes[a], me, dev, send[a].at[0], r_sib[a].at[0]).start()
            for k, j in enumerate(ICI_PEERS):
                dev, _ = _peer(pos, j)
                _gather_copy(ins[a], axes[a], me, dev, send[a].at[1 + k], r_ici[a].at[k]).start()
        token[...] = jnp.zeros_like(token)

    sems = ([pltpu.SemaphoreType.DMA((1 + len(ICI_PEERS),))] * n + [pltpu.SemaphoreType.DMA((1,))] * n
            + [pltpu.SemaphoreType.DMA((len(ICI_PEERS),))] * n)
    res = pl.pallas_call(
        body, name=name,
        out_shape=sems + [pltpu.HBM(b.shape, b.dtype) for b in bufs] + [jax.ShapeDtypeStruct((8, LANES), F32)],
        in_specs=[HBM_SPEC] * n,
        out_specs=[SEM_SPEC] * (3 * n) + [HBM_SPEC] * n + [pl.BlockSpec(memory_space=pltpu.VMEM)],
        input_output_aliases={i: 3 * n + i for i in range(n)},
        compiler_params=pltpu.CompilerParams(has_side_effects=DATAFLOW),
    )(*[pltpu.with_memory_space_constraint(b, pltpu.HBM) for b in bufs])
    items = [dict(send=res[a], r_sib=res[n + a], r_ici=res[2 * n + a], buf=res[3 * n + a]) for a in range(n)]
    return items, res[4 * n]


def _gather_relay(items, axes, after, name):
    n = len(items)

    def body(*refs):
        ins, r_ici = refs[:n], refs[n:2 * n]
        outs = refs[2 * n + 1:]
        s_rel, r_rel, token = outs[:n], outs[n:2 * n], outs[3 * n]
        pos = _my_pos()
        sib, _ = _peer(pos, SIBLING)
        for a in range(n):
            for k, j in enumerate(ICI_PEERS):
                dev, pid = _peer(pos, j)
                _gather_copy(ins[a], axes[a], pid, dev, s_rel[a].at[k], r_ici[a].at[k]).wait_recv()
                _gather_copy(ins[a], axes[a], pid, sib, s_rel[a].at[k], r_rel[a].at[k]).start()
        token[...] = jnp.zeros_like(token)

    bufs = [it["buf"] for it in items]
    sems = [pltpu.SemaphoreType.DMA((len(ICI_PEERS),))] * (2 * n)
    res = pl.pallas_call(
        body, name=name,
        out_shape=sems + [pltpu.HBM(b.shape, b.dtype) for b in bufs] + [jax.ShapeDtypeStruct((8, LANES), F32)],
        in_specs=[HBM_SPEC] * n + [SEM_SPEC] * n + [ANY_SPEC],
        out_specs=[SEM_SPEC] * (2 * n) + [HBM_SPEC] * n + [pl.BlockSpec(memory_space=pltpu.VMEM)],
        input_output_aliases={i: 2 * n + i for i in range(n)},
        compiler_params=pltpu.CompilerParams(has_side_effects=DATAFLOW),
    )(*bufs, *[it["r_ici"] for it in items], after)
    out = [dict(send=it["send"], r_sib=it["r_sib"], s_rel=res[a], r_rel=res[n + a], buf=res[2 * n + a])
           for a, it in enumerate(items)]
    return out, res[3 * n]


def _gather_wait(items, axes, after, name):
    n = len(items)

    def body(*refs):
        ins = refs[:n]
        send, r_sib, s_rel, r_rel = (refs[(1 + q) * n:(2 + q) * n] for q in range(4))
        pos = _my_pos()
        me = 4 * pos[0] + 2 * pos[1] + pos[2]
        sib, sib_id = _peer(pos, SIBLING)
        for a in range(n):
            for k in range(1 + len(ICI_PEERS)):
                _gather_copy(ins[a], axes[a], me, sib, send[a].at[k], r_sib[a].at[0]).wait_send()
            _gather_copy(ins[a], axes[a], sib_id, sib, send[a].at[0], r_sib[a].at[0]).wait_recv()
            for k, j in enumerate(ICI_PEERS):
                _, pid = _peer(pos, j)
                _, far = _peer(pos, j ^ SIBLING)
                _gather_copy(ins[a], axes[a], pid, sib, s_rel[a].at[k], r_rel[a].at[k]).wait_send()
                _gather_copy(ins[a], axes[a], far, sib, s_rel[a].at[k], r_rel[a].at[k]).wait_recv()

    bufs = [it["buf"] for it in items]
    res = pl.pallas_call(
        body, name=name,
        out_shape=[pltpu.HBM(b.shape, b.dtype) for b in bufs],
        in_specs=[HBM_SPEC] * n + [SEM_SPEC] * (4 * n) + [ANY_SPEC],
        out_specs=[HBM_SPEC] * n,
        input_output_aliases={i: i for i in range(n)},
        compiler_params=pltpu.CompilerParams(has_side_effects=DATAFLOW),
    )(*bufs, *[it["send"] for it in items], *[it["r_sib"] for it in items], *[it["s_rel"] for it in items],
      *[it["r_rel"] for it in items], after)
    return list(res)


def _gain_allreduce(v, name):
    n = v.shape[1]

    def body(v_ref, o_ref, slots, send_sems, recv_sems):
        pos = _my_pos()
        me = 4 * pos[0] + 2 * pos[1] + pos[2]
        slots[me] = v_ref[...]
        copies = []
        for j in range(1, N_DEV):
            dev, _ = _peer(pos, j)
            cp = pltpu.make_async_remote_copy(
                src_ref=slots.at[me], dst_ref=slots.at[me], send_sem=send_sems.at[j], recv_sem=recv_sems.at[j],
                device_id=dev, device_id_type=MESH)
            cp.start()
            copies.append(cp)
        for j in range(1, N_DEV):
            dev, pid = _peer(pos, j)
            pltpu.make_async_remote_copy(
                src_ref=slots.at[me], dst_ref=slots.at[pid], send_sem=send_sems.at[j], recv_sem=recv_sems.at[j],
                device_id=dev, device_id_type=MESH).wait_recv()
        for cp in copies:
            cp.wait_send()
        acc = slots[0]
        for s in range(1, N_DEV):
            acc = acc + slots[s]
        o_ref[...] = acc

    return pl.pallas_call(
        body, name=name, out_shape=jax.ShapeDtypeStruct((1, n), F32),
        in_specs=[pl.BlockSpec(memory_space=pltpu.VMEM)], out_specs=pl.BlockSpec(memory_space=pltpu.VMEM),
        scratch_shapes=[pltpu.VMEM((N_DEV, 1, n), F32), pltpu.SemaphoreType.DMA((N_DEV,)),
                        pltpu.SemaphoreType.DMA((N_DEV,))],
        compiler_params=pltpu.CompilerParams(has_side_effects=True),
    )(v)


def _adamw(parts, own, me, w, m, v, layer, prev, name, own_axis=None):
    L, R, C = w.shape
    P = parts.shape[0]
    tr = _pick(R, (128, 64, 32, 16, 8, 1))
    c1 = 1.0 - ADAM_B1 ** ADAM_STEP
    c2 = 1.0 - ADAM_B2 ** ADAM_STEP
    n_in = 4 if own is None else 5

    def body(me_ref, *refs):
        p_ref = refs[0]
        w_ref, m_ref, v_ref = refs[n_in - 3:n_in]
        g_out, d_out, m_out, v_out, tok = refs[-5:]
        g = None
        for s in range(P):
            part = p_ref[s]
            if own is not None:
                part = jnp.where(me_ref[0] == s, refs[1][...], part)
            g = part.astype(F32) if g is None else g + part.astype(F32)
        mn = ADAM_B1 * m_ref[...] + (1.0 - ADAM_B1) * g
        vn = ADAM_B2 * v_ref[...] + (1.0 - ADAM_B2) * (g * g)
        g_out[...] = g
        m_out[...] = mn
        v_out[...] = vn
        d_out[...] = -ADAM_LR * ((mn / c1) / (jnp.sqrt(vn / c2) + ADAM_EPS) + ADAM_WD * w_ref[...])
        tok[...] = jnp.zeros_like(tok)

    row = pl.BlockSpec((None, tr, C), lambda i, me_ref: (layer, i, 0))
    in_specs = [pl.BlockSpec((P, tr, C), lambda i, me_ref: (0, i, 0))]
    args = [parts]
    if own is not None:
        if own_axis is None:
            own_idx = lambda i, me_ref: (i, 0)
        elif own_axis[0] == 0:
            own_idx = lambda i, me_ref: (_slot(me_ref[0], own_axis[1]) * (R // tr) + i, 0)
        else:
            own_idx = lambda i, me_ref: (i, _slot(me_ref[0], own_axis[1]))
        in_specs.append(pl.BlockSpec((tr, C), own_idx))
        args.append(own)
    in_specs += [row, row, row]
    args += [w, m, v]
    aliases = {}
    if prev is not None:
        in_specs += [ANY_SPEC] * 4
        aliases = {1 + len(args) + k: k for k in range(4)}
        args += list(prev)
    shp = jax.ShapeDtypeStruct((L, R, C), F32)
    res = pl.pallas_call(
        body, name=name,
        grid_spec=pltpu.PrefetchScalarGridSpec(
            num_scalar_prefetch=1, grid=(R // tr,), in_specs=in_specs,
            out_specs=[row] * 4 + [pl.BlockSpec((8, LANES), lambda i, me_ref: (0, 0))]),
        out_shape=[shp] * 4 + [jax.ShapeDtypeStruct((8, LANES), F32)],
        input_output_aliases=aliases, compiler_params=_params(("arbitrary",)),
    )(me, *args)
    return res[:4], res[4]


def _pad_heads(w):
    lead = w.shape[:-1]
    n = w.shape[-1] // QK_DIM
    w = w.reshape(lead + (n, QK_DIM))
    w = jnp.pad(w, [(0, 0)] * len(lead) + [(0, 0), (0, HEAD_PAD - QK_DIM)])
    return w.reshape(lead + (n * HEAD_PAD,))


def _unpad_heads(w):
    lead = w.shape[:-1]
    n = w.shape[-1] // HEAD_PAD
    return w.reshape(lead + (n, HEAD_PAD))[..., :QK_DIM].reshape(lead + (n * QK_DIM,))


def kernel(x, ffn1_norm, ffn1_w_in, ffn1_w_out, mix_norm, ffn2_norm, ffn2_w_in, ffn2_w_out, mla_w_down, mla_g_cq, mla_g_ckv, mla_w_uq, mla_w_ukv, mla_g_qn, mla_g_kn, mla_w_o, dil_w_qkv, dil_g_qn, dil_g_kn, dil_w_o, loss_target, m_ffn1_norm, m_ffn1_w_in, m_ffn1_w_out, m_mix_norm, m_ffn2_norm, m_ffn2_w_in, m_ffn2_w_out, m_mla_w_down, m_mla_g_cq, m_mla_g_ckv, m_mla_w_uq, m_mla_w_ukv, m_mla_g_qn, m_mla_g_kn, m_mla_w_o, m_dil_w_qkv, m_dil_g_qn, m_dil_g_kn, m_dil_w_o, v_ffn1_norm, v_ffn1_w_in, v_ffn1_w_out, v_mix_norm, v_ffn2_norm, v_ffn2_w_in, v_ffn2_w_out, v_mla_w_down, v_mla_g_cq, v_mla_g_ckv, v_mla_w_uq, v_mla_w_ukv, v_mla_g_qn, v_mla_g_kn, v_mla_w_o, v_dil_w_qkv, v_dil_g_qn, v_dil_g_kn, v_dil_w_o):
    names = ["ffn1_norm", "ffn1_w_in", "ffn1_w_out", "mix_norm", "ffn2_norm", "ffn2_w_in", "ffn2_w_out", "mla_w_down",
             "mla_g_cq", "mla_g_ckv", "mla_w_uq", "mla_w_ukv", "mla_g_qn", "mla_g_kn", "mla_w_o", "dil_w_qkv",
             "dil_g_qn", "dil_g_kn", "dil_w_o"]
    W = dict(zip(names, [ffn1_norm, ffn1_w_in, ffn1_w_out, mix_norm, ffn2_norm, ffn2_w_in, ffn2_w_out, mla_w_down,
                         mla_g_cq, mla_g_ckv, mla_w_uq, mla_w_ukv, mla_g_qn, mla_g_kn, mla_w_o, dil_w_qkv,
                         dil_g_qn, dil_g_kn, dil_w_o]))
    M1 = dict(zip(names, [m_ffn1_norm, m_ffn1_w_in, m_ffn1_w_out, m_mix_norm, m_ffn2_norm, m_ffn2_w_in, m_ffn2_w_out,
                          m_mla_w_down, m_mla_g_cq, m_mla_g_ckv, m_mla_w_uq, m_mla_w_ukv, m_mla_g_qn, m_mla_g_kn,
                          m_mla_w_o, m_dil_w_qkv, m_dil_g_qn, m_dil_g_kn, m_dil_w_o]))
    V2 = dict(zip(names, [v_ffn1_norm, v_ffn1_w_in, v_ffn1_w_out, v_mix_norm, v_ffn2_norm, v_ffn2_w_in, v_ffn2_w_out,
                          v_mla_w_down, v_mla_g_cq, v_mla_g_ckv, v_mla_w_uq, v_mla_w_ukv, v_mla_g_qn, v_mla_g_kn,
                          v_mla_w_o, v_dil_w_qkv, v_dil_g_qn, v_dil_g_kn, v_dil_w_o]))
    S, D = x.shape[1], x.shape[2]
    x0 = x.reshape(S, D)
    tgt = loss_target.reshape(S, D)

    big = ["ffn1_w_in", "ffn1_w_out", "ffn2_w_in", "ffn2_w_out", "mla_w_down", "mla_w_uq", "mla_w_ukv", "mla_w_o",
           "dil_w_qkv", "dil_w_o"]
    shard_dim = {"ffn1_w_in": 2, "ffn1_w_out": 1, "ffn2_w_in": 2, "ffn2_w_out": 1, "mla_w_down": 1, "mla_w_uq": 2,
                 "mla_w_ukv": 2, "mla_w_o": 1, "dil_w_qkv": 2, "dil_w_o": 2}
    paired = ("ffn1_w_in", "ffn2_w_in")
    shard_axis = {n: (d, n in paired) for n, d in shard_dim.items()}
    grad_axis = {n: (d - 1, n in paired) for n, d in shard_dim.items()}

    def padded(n, w):
        if n == "mla_w_down":
            return jnp.pad(w, ((0, 0), (0, 0), (0, LAT_PAD - w.shape[2])))
        if n == "mla_w_uq":
            return _pad_heads(w)
        return w

    depth = ffn1_norm.shape[0]
    blocks = []
    for l in range(depth):
        mixer = (["mla_w_down", "mla_w_uq", "mla_w_ukv", "mla_w_o"] if l % 2 == 0 else ["dil_w_qkv", "dil_w_o"])
        blocks.append((f"ffn1_{l}", [("ffn1_w_in", l), ("ffn1_w_out", l)]))
        blocks.append((f"mix_{l}", [(n, l // 2) for n in mixer]))
        blocks.append((f"ffn2_{l}", [("ffn2_w_in", l), ("ffn2_w_out", l)]))
    order = [k for _, keys in blocks for k in keys]
    me = (4 * lax.axis_index("x") + 2 * lax.axis_index("y") + lax.axis_index("c")).astype(jnp.int32).reshape(1)
    def cast(key, deps=()):
        n, l = key
        return _cast_into_gathered(padded(n, W[n]), l, shard_axis[n], me, f"cast_{n}_{l}", deps=deps)

    items0, token0 = _gather_start([cast(order[0])], [shard_axis[order[0][0]]], "gather_start_first")
    rest = order[1:]
    items1, ag_token = _gather_start([cast(k, deps=[token0]) for k in rest], [shard_axis[k[0]] for k in rest],
                                     "gather_start_rest")
    ag_items = dict(zip(order, items0 + items1))
    full = {}

    def relay(keys, after, tag):
        out, token = _gather_relay([ag_items[k] for k in keys], [shard_axis[k[0]] for k in keys], after,
                                   f"gather_relay_{tag}")
        ag_items.update(zip(keys, out))
        return [token]

    def relay_next(bi, after):
        return relay(blocks[bi + 1][1], after, blocks[bi + 1][0]) if bi + 1 < len(blocks) else []

    def fetch(keys, after, tag):
        lands = _gather_wait([ag_items[k] for k in keys], [shard_axis[k[0]] for k in keys], after,
                             f"gather_wait_{tag}")
        full.update(zip(keys, lands))

    g_qn = _pad_heads(mla_g_qn)
    g_kn = _pad_heads(mla_g_kn)
    tabs = _rope_tables(S)
    slopes = jnp.asarray(_alibi_slopes(), F32)

    grads = {}
    gain_g = {}

    out_g, out_d, out_m, out_v = {}, {}, {}, {}
    pending = []
    lag = 3

    def scatter_start(tag, keys):
        items, token = _scatter_start([grads[k] for k in keys], [grad_axis[k[0]] for k in keys],
                                      f"scatter_start_{tag}")
        pending.append((tag, keys, items))
        return token

    def scatter_finish(after):
        tag, keys, items = pending.pop(0)
        lands = _scatter_wait(items, [grad_axis[k[0]] for k in keys], after, f"scatter_wait_{tag}")
        tokens = []
        for (n, l), (own, p) in zip(keys, lands):
            own_axis = grad_axis[n]
            if n in ("mla_w_down", "mla_w_uq"):
                ax, pair = grad_axis[n]
                size = own.shape[ax] // N_DEV
                own = lax.dynamic_slice_in_dim(own, _slot(me[0], pair) * size, size, axis=ax)
                own_axis = None
                if n == "mla_w_down":
                    p, own = p[..., :W[n].shape[2]], own[..., :W[n].shape[2]]
                else:
                    p, own = _unpad_heads(p), _unpad_heads(own)
            prev = (out_g[n], out_d[n], out_m[n], out_v[n]) if n in out_g else None
            (out_g[n], out_d[n], out_m[n], out_v[n]), tok = _adamw(p, own, me, W[n], M1[n], V2[n], l, prev,
                                                                    f"adamw_{n}_{l}", own_axis=own_axis)
            tokens.append(tok)
        return tokens

    def finish_due(after):
        tokens = []
        while len(pending) > lag:
            tokens += scatter_finish(after)
        return tokens

    def ffn_fwd(xin, norm_row, which, l, bi, deps=()):
        tag = blocks[bi][0]
        k_in, k_out = (which + "_w_in", l), (which + "_w_out", l)
        h = _rms_fwd(xin, norm_row, f"rms_fwd_{tag}", deps=deps)
        if bi == 0:
            relay([k_in], h, f"{tag}_in")
        fetch([k_in], h, f"in_{tag}")
        u, a = _ffn_in(h, full[k_in], f"ffn_in_{tag}")
        if bi == 0:
            relay([k_out], a, f"{tag}_out")
        fetch([k_out], a, f"out_{tag}")
        toks = relay_next(bi, a)
        xo = _mm(a, full[k_out], "nn", F32, f"mm_out_{tag}", scale=0.5, res=xin, layer=0, deps=toks)
        return xo, (xin, h, u, a)

    def ffn_bwd(dx_pair, saved, norm_row, which, l, tag):
        dxo, dxob = dx_pair
        k_in, k_out = (which + "_w_in", l), (which + "_w_out", l)
        xin, h, u, a = saved
        grads[k_out] = _mm(a, dxob, "tn", BF16, f"mm_dwout_{tag}", scale=0.5)
        t_out = scatter_start(f"{tag}_out", [k_out])
        du = _ffn_da(dxob, full[k_out], u, f"ffn_da_{tag}", deps=[t_out])
        grads[k_in] = _mm(h, du, "tn", BF16, f"mm_dwin_{tag}")
        t_in = scatter_start(f"{tag}_in", [k_in])
        dh = _mm(du, full[k_in], "nt", F32, f"mm_dh_{tag}", layer=0, deps=[t_in])
        toks = finish_due(dh)
        dx, dxb, dg = _rms_bwd(xin, norm_row, dh, dxo, f"rms_bwd_{tag}", deps=toks)
        gain_g.setdefault(which + "_norm", {})[l] = dg
        return dx, dxb

    def mla_fwd(xin, l, bi):
        j = l // 2
        xn = _rms_fwd(xin, mix_norm[l:l + 1], "rms_fwd_mla")
        fetch([(n, j) for n in ("mla_w_down", "mla_w_uq", "mla_w_ukv", "mla_w_o")], xn, "mla")
        lat = _mm(xn, full[("mla_w_down", j)], "nn", F32, "mm_lat", layer=0)
        cq, ckv = _lat_norm_fwd(lat, mla_g_cq[j:j + 1], mla_g_ckv[j:j + 1], "lat_norm_fwd")
        q_raw = _mm(cq, full[("mla_w_uq", j)], "nn", F32, "mm_uq", layer=0)
        kv = _mm(ckv, full[("mla_w_ukv", j)], "nn", F32, "mm_ukv", layer=0)
        qf, kf, vb = _mla_prep_fwd(q_raw, kv, lat, g_qn[j:j + 1], g_kn[j:j + 1], tabs, "mla_prep_fwd")
        o, lse = _flash_fwd(qf, kf, vb, "flash_fwd")
        toks = relay_next(bi, o)
        xo = _mm(o, full[("mla_w_o", j)], "nn", F32, "mm_mla_o", res=xin, layer=0, deps=toks)
        return xo, (xin, xn, lat, cq, ckv, q_raw, kv, qf, kf, vb, o, lse)

    def mla_bwd(dx_pair, saved, l):
        dxo, dxob = dx_pair
        j = l // 2
        xin, xn, lat, cq, ckv, q_raw, kv, qf, kf, vb, o, lse = saved
        do = _mm(dxob, full[("mla_w_o", j)], "nt", BF16, "mm_mla_do", layer=0)
        grads[("mla_w_o", j)] = _mm(o, dxob, "tn", BF16, "mm_mla_dwo")
        delta = _attn_delta(do, o, "attn_delta")
        dqf, dkf, dv = _flash_bwd(qf, kf, vb, do, lse, delta, "flash_bwd")
        dq_raw, dkv, dkpe, dgq, dgk = _mla_prep_bwd(q_raw, kv, lat, g_qn[j:j + 1], g_kn[j:j + 1], tabs, dqf, dkf, dv,
                                                    "mla_prep_bwd")
        gain_g.setdefault("mla_g_qn", {})[j] = dgq
        gain_g.setdefault("mla_g_kn", {})[j] = dgk
        dcq = _mm(dq_raw, full[("mla_w_uq", j)], "nt", F32, "mm_dcq", layer=0)
        grads[("mla_w_uq", j)] = _mm(cq, dq_raw, "tn", BF16, "mm_dwuq")
        dckv = _mm(dkv, full[("mla_w_ukv", j)], "nt", F32, "mm_dckv", layer=0)
        grads[("mla_w_ukv", j)] = _mm(ckv, dkv, "tn", BF16, "mm_dwukv")
        dlat, dgcq, dgckv = _lat_norm_bwd(lat, mla_g_cq[j:j + 1], mla_g_ckv[j:j + 1], dcq, dckv, dkpe, "lat_norm_bwd")
        gain_g.setdefault("mla_g_cq", {})[j] = dgcq
        gain_g.setdefault("mla_g_ckv", {})[j] = dgckv
        dxn = _mm(dlat, full[("mla_w_down", j)], "nt", F32, "mm_dxn_mla", layer=0)
        grads[("mla_w_down", j)] = _mm(xn, dlat, "tn", BF16, "mm_dwdown")
        tok = scatter_start(f"mix_{l}", [(n, j) for n in ("mla_w_down", "mla_w_uq", "mla_w_ukv", "mla_w_o")])
        toks = finish_due(dxn)
        dx, dxb, dg = _rms_bwd(xin, mix_norm[l:l + 1], dxn, dxo, "rms_bwd_mla", deps=[tok] + toks)
        gain_g.setdefault("mix_norm", {})[l] = dg
        return dx, dxb

    def dil_fwd(xin, l, bi):
        j = l // 2
        xn = _rms_fwd(xin, mix_norm[l:l + 1], "rms_fwd_dil")
        fetch([("dil_w_qkv", j), ("dil_w_o", j)], xn, "dil")
        qkv = _mm(xn, full[("dil_w_qkv", j)], "nn", F32, "mm_qkv", layer=0)
        o_g, lse_g = _dil_fwd(qkv, dil_g_qn[j:j + 1], dil_g_kn[j:j + 1], slopes, "dil_fwd")
        o, lse = _dil_merge(o_g, lse_g, "dil_merge")
        toks = relay_next(bi, o)
        xo = _mm(o, full[("dil_w_o", j)], "nn", F32, "mm_dil_o", res=xin, layer=0, deps=toks)
        return xo, (xin, xn, qkv, o, lse)

    def dil_bwd(dx_pair, saved, l):
        dxo, dxob = dx_pair
        j = l // 2
        xin, xn, qkv, o, lse = saved
        do = _mm(dxob, full[("dil_w_o", j)], "nt", F32, "mm_dil_do", layer=0)
        grads[("dil_w_o", j)] = _mm(o, dxob, "tn", BF16, "mm_dil_dwo")
        delta = _attn_delta(do, o, "dil_delta")
        dq, dk, dv, dgq, dgk = _dil_bwd(qkv, dil_g_qn[j:j + 1], dil_g_kn[j:j + 1], slopes, do, delta, lse, "dil_bwd")
        gain_g.setdefault("dil_g_qn", {})[j] = dgq
        gain_g.setdefault("dil_g_kn", {})[j] = dgk
        dqkv = jnp.concatenate([dq, dk, dv], axis=1)
        dxn = _mm(dqkv, full[("dil_w_qkv", j)], "nt", F32, "mm_dxn_dil", layer=0)
        grads[("dil_w_qkv", j)] = _mm(xn, dqkv, "tn", BF16, "mm_dwqkv")
        tok = scatter_start(f"mix_{l}", [("dil_w_qkv", j), ("dil_w_o", j)])
        toks = finish_due(dxn)
        dx, dxb, dg = _rms_bwd(xin, mix_norm[l:l + 1], dxn, dxo, "rms_bwd_dil", deps=[tok] + toks)
        gain_g.setdefault("mix_norm", {})[l] = dg
        return dx, dxb

    saved = []
    xc = x0
    for l in range(depth):
        xc, s1 = ffn_fwd(xc, ffn1_norm[l:l + 1], "ffn1", l, 3 * l, deps=[ag_token] if l == 0 else ())
        xc, s2 = (mla_fwd if l % 2 == 0 else dil_fwd)(xc, l, 3 * l + 1)
        xc, s3 = ffn_fwd(xc, ffn2_norm[l:l + 1], "ffn2", l, 3 * l + 2)
        saved.append((s1, s2, s3))

    dy, dyb, loss_part = _loss_head(xc, tgt, "loss_head")
    dx = (dy, dyb)
    loss = lax.psum(loss_part[0, 0], MESH_AXES)

    for bi in reversed(range(len(blocks))):
        tag, _ = blocks[bi]
        l = bi // 3
        s = saved[l][bi % 3]
        if bi % 3 == 2:
            dx = ffn_bwd(dx, s, ffn2_norm[l:l + 1], "ffn2", l, tag)
        elif bi % 3 == 1:
            dx = (mla_bwd if l % 2 == 0 else dil_bwd)(dx, s, l)
        else:
            dx = ffn_bwd(dx, s, ffn1_norm[l:l + 1], "ffn1", l, tag)
    grad_x = dx[0].reshape(x.shape)
    after = dx[1]
    while pending:
        after = scatter_finish(after)[-1]

    small = [n for n in names if n not in big]

    def gain_local(n):
        rows = [gain_g[n][l] for l in range(W[n].shape[0])]
        g = jnp.concatenate(rows, axis=1)
        return g

    def flat_pad(n, a):
        a = a.reshape(1, -1)
        if n in ("mla_g_qn", "mla_g_kn"):
            a = _pad_heads(a)
        return a

    packed_g = jnp.concatenate([gain_local(n) for n in small], axis=1)
    sizes = [gain_local(n).shape[1] for n in small]
    tot_g = _gain_allreduce(packed_g, "gain_allreduce")
    pw = jnp.concatenate([flat_pad(n, W[n]) for n in small], axis=1)
    pm = jnp.concatenate([flat_pad(n, M1[n]) for n in small], axis=1)
    pv = jnp.concatenate([flat_pad(n, V2[n]) for n in small], axis=1)
    res, _ = _adamw(tot_g.reshape(1, 1, -1), None, me, pw.reshape(1, 1, -1), pm.reshape(1, 1, -1),
                    pv.reshape(1, 1, -1), 0, None, "adamw_gains")
    res = [r.reshape(1, -1) for r in res]
    off = 0
    for n, sz in zip(small, sizes):
        for dst, r in zip((out_g, out_d, out_m, out_v), res):
            piece = r[:, off:off + sz]
            if n in ("mla_g_qn", "mla_g_kn"):
                piece = _unpad_heads(piece)
            dst[n] = piece.reshape(W[n].shape)
        off += sz

    return (loss, grad_x, *[out_g[n] for n in names], *[out_d[n] for n in names],
            *[out_m[n] for n in names], *[out_v[n] for n in names])
```

```python
import functools
import math

import jax
import jax.numpy as jnp
import numpy as np
from jax import lax
from jax.experimental import pallas as pl
from jax.experimental.pallas import tpu as pltpu

EPS = 1e-6
MLA_HEADS = 16
Q_LORA = 512
KV_LORA = 512
NOPE_DIM = 128
ROPE_DIM = 64
V_DIM = 128
QK_DIM = NOPE_DIM + ROPE_DIM
ROPE_THETA = 10000.0
HEAD_PAD = 256
LAT_PAD = Q_LORA + KV_LORA + 128
DIL_PAIRS = ((128, 1), (512, 4), (2048, 16))
DIL_GROUPS = 3
DIL_HEADS = 8
DIL_HEAD_DIM = 128
DIL_BLK = 128
FLASH_HEADS = 2
LOG2E = math.log2(math.e)
LN2 = math.log(2.0)
ADAM_LR = 0.001
ADAM_B1 = 0.9
ADAM_B2 = 0.999
ADAM_EPS = 1e-08
ADAM_WD = 0.01
ADAM_STEP = 10

N_DEV = 8
MESH_AXES = ("x", "y", "c")
MESH = pl.DeviceIdType.MESH
NEG_BIG = -1e30
VMEM_LIMIT_V7X = 56 * 1024 * 1024
LANES = 128

BF16 = jnp.bfloat16
F32 = jnp.float32


def _pick(n, cands):
    for c in cands:
        if n % c == 0:
            return c
    raise ValueError(f"no tile for {n}")


def _params(sem):
    return pltpu.CompilerParams(dimension_semantics=sem, vmem_limit_bytes=VMEM_LIMIT_V7X)


ANY_SPEC = pl.BlockSpec(memory_space=pl.ANY)


MM_VMEM_BUDGET = 44 * 1024 * 1024
MM_HBM_BYTES_PER_S = 1.8e12
MM_MXU_FLOPS_PER_S = 8.5e14
MM_STEP_S = 0.4e-6
MM_MAX_TILE_MACS = 3.3e9
MXU_DIM = 256


@functools.lru_cache(maxsize=None)
def _mm_tiles(M, K, N, a_bytes, b_bytes, out_bytes, has_res):
    best = None
    for tk in [K] + [c for c in (1408, 1024, 512, 384, 256, 128) if K % c == 0 and c < K]:
        nk = K // tk
        for tm in [c for c in (2048, 1024, 512, 256, 128) if M % c == 0]:
            for tn in [c for c in (2816, 2048, 1408, 1152, 1024, 512, 384, 256, 128) if N % c == 0]:
                if tm * tk * tn > MM_MAX_TILE_MACS:
                    continue
                fill = (tn / (-(-tn // MXU_DIM) * MXU_DIM)) * (tk / (-(-tk // MXU_DIM) * MXU_DIM))
                fill *= tm / (tm + MXU_DIM // 2)
                vmem = 2 * (tm * tk * a_bytes + tk * tn * b_bytes) + 2 * tm * tn * out_bytes + tm * tn * 4
                vmem += (tm * tk + tk * tn) * 2 if max(a_bytes, b_bytes) > 2 else 0
                vmem += 2 * tm * tn * 4 if has_res else 0
                if vmem > MM_VMEM_BUDGET:
                    continue
                a_all, b_all = M * K * a_bytes, K * N * b_bytes
                if nk == 1:
                    t_i = a_all + (M // tm) * b_all
                    t_j = b_all + (N // tn) * a_all
                    traffic, i_outer = min((t_i, True), (t_j, False))
                else:
                    traffic, i_outer = (N // tn) * a_all + (M // tm) * b_all, True
                traffic += M * N * (out_bytes + (4 if has_res else 0))
                mxu = 2.0 * M * K * N / (MM_MXU_FLOPS_PER_S * fill) * (1.15 if nk > 1 else 1.0)
                cost = max(traffic / MM_HBM_BYTES_PER_S, mxu) + (M // tm) * (N // tn) * nk * MM_STEP_S
                if best is None or cost < best[0]:
                    best = (cost, tm, tn, tk, i_outer)
    assert best is not None, (M, K, N)
    return best[1:]


def _mm(a, b, mode, out_dtype, name, *, scale=1.0, res=None, layer=None, deps=()):
    b2 = b.shape[-2:]
    if mode == "nn":
        (M, K), (Kb, N) = a.shape, b2
    elif mode == "nt":
        (M, K), (N, Kb) = a.shape, b2
    else:
        (K, M), (Kb, N) = a.shape, b2
    assert K == Kb, (a.shape, b.shape, mode)
    tm, tn, tk, i_outer = _mm_tiles(M, K, N, a.dtype.itemsize, b.dtype.itemsize, jnp.dtype(out_dtype).itemsize,
                                    res is not None)
    nk = K // tk
    dims = {"nn": (((1,), (0,)), ((), ())), "nt": (((1,), (1,)), ((), ())), "tn": (((0,), (0,)), ((), ()))}[mode]

    def finish(v, r_ref, o_ref):
        if scale != 1.0:
            v = v * scale
        if r_ref is not None:
            v = r_ref[...] + v
        o_ref[...] = v.astype(o_ref.dtype)

    def body(*refs):
        a_ref, b_ref = refs[:2]
        r_ref = refs[2] if res is not None else None
        prod = lambda: lax.dot_general(a_ref[...].astype(BF16), b_ref[...].astype(BF16), dims,
                                       preferred_element_type=F32)
        if nk == 1:
            finish(prod(), r_ref, refs[-1])
            return
        o_ref, acc = refs[-2:]
        k = pl.program_id(2)

        @pl.when(k == 0)
        def _():
            acc[...] = prod()

        @pl.when(k > 0)
        def _():
            acc[...] += prod()

        @pl.when(k == nk - 1)
        def _():
            finish(acc[...], r_ref, o_ref)

    ij = (lambda p, q: (p, q)) if i_outer else (lambda p, q: (q, p))

    def spec(shape, f, lead=None):
        full = lambda p, q, k: f(*ij(p, q), k)
        if lead is None:
            return pl.BlockSpec(shape, full)
        return pl.BlockSpec((None,) + shape, lambda p, q, k: (lead,) + full(p, q, k))

    a_spec = spec((tk, tm), lambda i, j, k: (k, i)) if mode == "tn" else spec((tm, tk), lambda i, j, k: (i, k))
    lead = layer if b.ndim == 3 else None
    b_spec = spec((tn, tk), lambda i, j, k: (j, k), lead) if mode == "nt" else spec((tk, tn), lambda i, j, k: (k, j), lead)
    in_specs = [a_spec, b_spec]
    args = [a, b]
    if res is not None:
        in_specs.append(spec((tm, tn), lambda i, j, k: (i, j)))
        args.append(res)
    in_specs += [ANY_SPEC] * len(deps)
    args += list(deps)
    outer, inner = (M // tm, N // tn) if i_outer else (N // tn, M // tm)
    return pl.pallas_call(
        body, name=name, grid=(outer, inner, nk),
        in_specs=in_specs, out_specs=spec((tm, tn), lambda i, j, k: (i, j)),
        out_shape=jax.ShapeDtypeStruct((M, N), out_dtype),
        scratch_shapes=[pltpu.VMEM((tm, tn), F32)] if nk > 1 else [],
        compiler_params=_params(("parallel", "parallel", "arbitrary")),
    )(*args)


def _cast_into_gathered(w, layer, axis, me, name, deps=()):
    _, R, C = w.shape
    tr = _pick(R, (512, 256, 128, 64, 32, 16))
    nr = R // tr
    axis, paired = axis

    def body(me_ref, w_ref, *rest):
        o_ref = rest[-1]
        o_ref[...] = w_ref[...].astype(BF16)

    if axis == 1:
        out_idx = lambda i, me_ref: (0, _slot(me_ref[0], paired) * nr + i, 0)
        shape = (1, R * N_DEV, C)
    else:
        out_idx = lambda i, me_ref: (0, i, _slot(me_ref[0], paired))
        shape = (1, R, C * N_DEV)
    return pl.pallas_call(
        body, name=name,
        grid_spec=pltpu.PrefetchScalarGridSpec(
            num_scalar_prefetch=1, grid=(nr,),
            in_specs=[pl.BlockSpec((None, tr, C), lambda i, me_ref: (layer, i, 0))] + [ANY_SPEC] * len(deps),
            out_specs=pl.BlockSpec((None, tr, C), out_idx)),
        out_shape=jax.ShapeDtypeStruct(shape, BF16), compiler_params=_params(("parallel",)),
    )(me, w, *deps)


def _rms_fwd(x, g, name, deps=()):
    T, D = x.shape
    tr = _pick(T, (512, 256, 128))

    def body(x_ref, g_ref, *rest):
        o_ref = rest[-1]
        xv = x_ref[...]
        r = lax.rsqrt(jnp.mean(xv * xv, axis=-1, keepdims=True) + EPS)
        o_ref[...] = ((xv * r) * g_ref[...]).astype(BF16)

    return pl.pallas_call(
        body, name=name, grid=(T // tr,),
        in_specs=[pl.BlockSpec((tr, D), lambda i: (i, 0)), pl.BlockSpec((1, D), lambda i: (0, 0))]
        + [ANY_SPEC] * len(deps),
        out_specs=pl.BlockSpec((tr, D), lambda i: (i, 0)),
        out_shape=jax.ShapeDtypeStruct((T, D), BF16), compiler_params=_params(("parallel",)),
    )(x, g, *deps)


def _rms_bwd(x, g, dh, dres, name, deps=()):
    T, D = x.shape
    tr = _pick(T, (256, 128))

    def body(x_ref, g_ref, dh_ref, dres_ref, *rest):
        dx_ref, dxb_ref, dg_ref = rest[-3:]
        xv = x_ref[...]
        dhv = dh_ref[...]
        r = lax.rsqrt(jnp.mean(xv * xv, axis=-1, keepdims=True) + EPS)
        xhat = xv * r
        dxh = dhv * g_ref[...]
        c = jnp.mean(dxh * xhat, axis=-1, keepdims=True)
        dx = dres_ref[...] + r * (dxh - xhat * c)
        dx_ref[...] = dx
        dxb_ref[...] = dx.astype(BF16)

        @pl.when(pl.program_id(0) == 0)
        def _():
            dg_ref[...] = jnp.zeros_like(dg_ref)

        dg_ref[...] += jnp.sum(dhv * xhat, axis=0, keepdims=True)

    row = pl.BlockSpec((tr, D), lambda i: (i, 0))
    vec = pl.BlockSpec((1, D), lambda i: (0, 0))
    return pl.pallas_call(
        body, name=name, grid=(T // tr,),
        in_specs=[row, vec, row, row] + [ANY_SPEC] * len(deps), out_specs=[row, row, vec],
        out_shape=[jax.ShapeDtypeStruct((T, D), F32), jax.ShapeDtypeStruct((T, D), BF16),
                   jax.ShapeDtypeStruct((1, D), F32)],
        compiler_params=_params(("arbitrary",)),
    )(x, g, dh, dres, *deps)


N_PANEL = N_DEV // 2


def _ffn_in(h, w_in, name):
    T, D = h.shape
    F2 = w_in.shape[2]
    pw = F2 // N_PANEL
    half = pw // 2
    tm = _pick(T, (512, 256, 128))

    def body(h_ref, w_ref, u_ref, a_ref):
        r = jnp.dot(h_ref[...], w_ref[...], preferred_element_type=F32)
        u_ref[...] = r.astype(BF16)
        g, up = r[:, :half], r[:, half:]
        a_ref[...] = (g * jax.nn.sigmoid(g) * up).astype(BF16)

    return pl.pallas_call(
        body, name=name, grid=(N_PANEL, T // tm),
        in_specs=[pl.BlockSpec((tm, D), lambda p, i: (i, 0)), pl.BlockSpec((None, D, pw), lambda p, i: (0, 0, p))],
        out_specs=[pl.BlockSpec((tm, pw), lambda p, i: (i, p)), pl.BlockSpec((tm, half), lambda p, i: (i, p))],
        out_shape=[jax.ShapeDtypeStruct((T, F2), BF16), jax.ShapeDtypeStruct((T, F2 // 2), BF16)],
        compiler_params=_params(("parallel", "parallel")),
    )(h, w_in)


def _ffn_da(dxo, w_out, u, name, deps=()):
    T, D = dxo.shape
    F2 = u.shape[1]
    pw = F2 // N_PANEL
    half = pw // 2
    tm = _pick(T, (512, 256, 128))

    def body(d_ref, w_ref, u_ref, *rest):
        du_ref = rest[-1]
        da = 0.5 * lax.dot_general(d_ref[...], w_ref[...], NT_DIMS, preferred_element_type=F32)
        g = u_ref[:, :half].astype(F32)
        up = u_ref[:, half:].astype(F32)
        sg = jax.nn.sigmoid(g)
        silu = g * sg
        du_ref[:, :half] = (da * up * (sg + silu * (1.0 - sg))).astype(BF16)
        du_ref[:, half:] = (da * silu).astype(BF16)

    return pl.pallas_call(
        body, name=name, grid=(N_PANEL, T // tm),
        in_specs=[pl.BlockSpec((tm, D), lambda p, i: (i, 0)), pl.BlockSpec((None, half, D), lambda p, i: (0, p, 0)),
                  pl.BlockSpec((tm, pw), lambda p, i: (i, p))] + [ANY_SPEC] * len(deps),
        out_specs=pl.BlockSpec((tm, pw), lambda p, i: (i, p)),
        out_shape=jax.ShapeDtypeStruct((T, F2), BF16), compiler_params=_params(("parallel", "parallel")),
    )(dxo, w_out, u, *deps)


def _loss_head(y, t, name):
    T, D = y.shape
    tr = _pick(T, (512, 256, 128))

    def body(y_ref, t_ref, dy_ref, dyb_ref, l_ref):
        e = y_ref[...] - t_ref[...]
        dy = e * (1.0 / D)
        dy_ref[...] = dy
        dyb_ref[...] = dy.astype(BF16)

        @pl.when(pl.program_id(0) == 0)
        def _():
            l_ref[...] = jnp.zeros_like(l_ref)

        l_ref[...] += 0.5 * jnp.sum(jnp.mean(e * e, axis=-1, keepdims=True), axis=0, keepdims=True)

    row = pl.BlockSpec((tr, D), lambda i: (i, 0))
    return pl.pallas_call(
        body, name=name, grid=(T // tr,),
        in_specs=[row, row], out_specs=[row, row, pl.BlockSpec((1, 1), lambda i: (0, 0))],
        out_shape=[jax.ShapeDtypeStruct((T, D), F32), jax.ShapeDtypeStruct((T, D), BF16),
                   jax.ShapeDtypeStruct((1, 1), F32)],
        compiler_params=_params(("arbitrary",)),
    )(y, t)


def _rope_tables(S):
    half = ROPE_DIM // 2
    inv = 1.0 / (ROPE_THETA ** (jnp.arange(0, ROPE_DIM, 2, dtype=F32) / ROPE_DIM))
    ang = jnp.arange(S, dtype=F32)[:, None] * inv[None, :]
    cos, sin = jnp.cos(ang), jnp.sin(ang)
    z = jnp.zeros((S, half), F32)
    z2 = jnp.zeros((S, LANES - ROPE_DIM), F32)
    c = jnp.concatenate([cos, cos, z2], axis=1)
    s1 = jnp.concatenate([-sin, z, z2], axis=1)
    s2 = jnp.concatenate([z, sin, z2], axis=1)
    return c, s1, s2


def _rope(r, c, s1, s2):
    return r * c + pltpu.roll(r, LANES - ROPE_DIM // 2, 1) * s1 + pltpu.roll(r, ROPE_DIM // 2, 1) * s2


def _rope_t(d, c, s1, s2):
    return d * c + pltpu.roll(d * s1, ROPE_DIM // 2, 1) + pltpu.roll(d * s2, LANES - ROPE_DIM // 2, 1)


def _lat_norm_fwd(lat, g_cq, g_ckv, name):
    T = lat.shape[0]
    tr = _pick(T, (512, 256, 128))

    def body(lat_ref, gq_ref, gk_ref, cq_ref, ckv_ref):
        for off, g_ref, o_ref in ((0, gq_ref, cq_ref), (Q_LORA, gk_ref, ckv_ref)):
            xv = lat_ref[:, off:off + Q_LORA]
            r = lax.rsqrt(jnp.mean(xv * xv, axis=-1, keepdims=True) + EPS)
            o_ref[...] = ((xv * r) * g_ref[...]).astype(BF16)

    vec = pl.BlockSpec((1, Q_LORA), lambda i: (0, 0))
    out = pl.BlockSpec((tr, Q_LORA), lambda i: (i, 0))
    return pl.pallas_call(
        body, name=name, grid=(T // tr,),
        in_specs=[pl.BlockSpec((tr, LAT_PAD), lambda i: (i, 0)), vec, vec], out_specs=[out, out],
        out_shape=[jax.ShapeDtypeStruct((T, Q_LORA), BF16)] * 2, compiler_params=_params(("parallel",)),
    )(lat, g_cq, g_ckv)


def _lat_norm_bwd(lat, g_cq, g_ckv, dcq, dckv, dkpe, name):
    T = lat.shape[0]
    tr = _pick(T, (256, 128))

    def body(lat_ref, gq_ref, gk_ref, dcq_ref, dckv_ref, dkpe_ref, dlat_ref, dgq_ref, dgk_ref):
        @pl.when(pl.program_id(0) == 0)
        def _():
            dgq_ref[...] = jnp.zeros_like(dgq_ref)
            dgk_ref[...] = jnp.zeros_like(dgk_ref)

        for off, g_ref, d_ref, dg_ref in ((0, gq_ref, dcq_ref, dgq_ref), (Q_LORA, gk_ref, dckv_ref, dgk_ref)):
            xv = lat_ref[:, off:off + Q_LORA]
            dv = d_ref[...]
            r = lax.rsqrt(jnp.mean(xv * xv, axis=-1, keepdims=True) + EPS)
            xhat = xv * r
            dxh = dv * g_ref[...]
            c = jnp.mean(dxh * xhat, axis=-1, keepdims=True)
            dlat_ref[:, off:off + Q_LORA] = (r * (dxh - xhat * c)).astype(BF16)
            dg_ref[...] += jnp.sum(dv * xhat, axis=0, keepdims=True)
        dlat_ref[:, Q_LORA + KV_LORA:] = dkpe_ref[...].astype(BF16)

    vec = pl.BlockSpec((1, Q_LORA), lambda i: (0, 0))
    half = pl.BlockSpec((tr, Q_LORA), lambda i: (i, 0))
    full = pl.BlockSpec((tr, LAT_PAD), lambda i: (i, 0))
    return pl.pallas_call(
        body, name=name, grid=(T // tr,),
        in_specs=[full, vec, vec, half, half, pl.BlockSpec((tr, LANES), lambda i: (i, 0))],
        out_specs=[full, vec, vec],
        out_shape=[jax.ShapeDtypeStruct((T, LAT_PAD), BF16), jax.ShapeDtypeStruct((1, Q_LORA), F32),
                   jax.ShapeDtypeStruct((1, Q_LORA), F32)],
        compiler_params=_params(("arbitrary",)),
    )(lat, g_cq, g_ckv, dcq, dckv, dkpe)


def _mla_prep_fwd(q_raw, kv, lat, g_qn, g_kn, tabs, name):
    T = q_raw.shape[0]
    H = MLA_HEADS
    tr = _pick(T, (256, 128))
    scale = LOG2E / math.sqrt(QK_DIM)

    def body(q_ref, kv_ref, kpe_ref, gq_ref, gk_ref, c_ref, s1_ref, s2_ref, qf_ref, kf_ref, v_ref):
        c, s1, s2 = c_ref[...], s1_ref[...], s2_ref[...]
        gq, gk = gq_ref[...], gk_ref[...]
        kpe = kpe_ref[...]
        kpe_ss = jnp.sum(kpe * kpe, axis=-1, keepdims=True)
        for h in range(H):
            lo = h * HEAD_PAD
            qa = q_ref[:, lo:lo + LANES]
            qb = q_ref[:, lo + LANES:lo + HEAD_PAD]
            ss = jnp.sum(qa * qa, axis=-1, keepdims=True) + jnp.sum(qb * qb, axis=-1, keepdims=True)
            r = lax.rsqrt(ss * (1.0 / QK_DIM) + EPS)
            qf_ref[:, lo:lo + LANES] = (qa * r * gq[:, :LANES] * scale).astype(BF16)
            qf_ref[:, lo + LANES:lo + HEAD_PAD] = (_rope(qb * r * gq[:, LANES:], c, s1, s2) * scale).astype(BF16)
            ka = kv_ref[:, lo:lo + LANES]
            ss = jnp.sum(ka * ka, axis=-1, keepdims=True) + kpe_ss
            r = lax.rsqrt(ss * (1.0 / QK_DIM) + EPS)
            kf_ref[:, lo:lo + LANES] = (ka * r * gk[:, :LANES]).astype(BF16)
            kf_ref[:, lo + LANES:lo + HEAD_PAD] = _rope(kpe * r * gk[:, LANES:], c, s1, s2).astype(BF16)
            v_ref[:, h * V_DIM:(h + 1) * V_DIM] = kv_ref[:, lo + LANES:lo + HEAD_PAD].astype(BF16)

    wide = pl.BlockSpec((tr, H * HEAD_PAD), lambda i: (i, 0))
    lane = pl.BlockSpec((tr, LANES), lambda i: (i, 0))
    gvec = pl.BlockSpec((1, HEAD_PAD), lambda i: (0, 0))
    return pl.pallas_call(
        body, name=name, grid=(T // tr,),
        in_specs=[wide, wide, pl.BlockSpec((tr, LANES), lambda i: (i, (Q_LORA + KV_LORA) // LANES)), gvec, gvec,
                  lane, lane, lane],
        out_specs=[wide, wide, pl.BlockSpec((tr, H * V_DIM), lambda i: (i, 0))],
        out_shape=[jax.ShapeDtypeStruct((T, H * HEAD_PAD), BF16), jax.ShapeDtypeStruct((T, H * HEAD_PAD), BF16),
                   jax.ShapeDtypeStruct((T, H * V_DIM), BF16)],
        compiler_params=_params(("parallel",)),
    )(q_raw, kv, lat, g_qn, g_kn, *tabs)


def _mla_prep_bwd(q_raw, kv, lat, g_qn, g_kn, tabs, dqf, dkf, dv, name):
    T = q_raw.shape[0]
    H = MLA_HEADS
    tr = _pick(T, (128,))

    def body(q_ref, kv_ref, kpe_ref, gq_ref, gk_ref, c_ref, s1_ref, s2_ref, dqf_ref, dkf_ref, dv_ref,
             dq_ref, dkv_ref, dkpe_ref, dgq_ref, dgk_ref):
        @pl.when(pl.program_id(0) == 0)
        def _():
            dgq_ref[...] = jnp.zeros_like(dgq_ref)
            dgk_ref[...] = jnp.zeros_like(dgk_ref)

        c, s1, s2 = c_ref[...], s1_ref[...], s2_ref[...]
        gq, gk = gq_ref[...], gk_ref[...]
        kpe = kpe_ref[...]
        kpe_ss = jnp.sum(kpe * kpe, axis=-1, keepdims=True)
        dkpe = jnp.zeros_like(kpe)
        dgq_a = jnp.zeros((1, LANES), F32)
        dgq_b = jnp.zeros((1, LANES), F32)
        dgk_a = jnp.zeros((1, LANES), F32)
        dgk_b = jnp.zeros((1, LANES), F32)
        for h in range(H):
            lo = h * HEAD_PAD
            xa = q_ref[:, lo:lo + LANES]
            xb = q_ref[:, lo + LANES:lo + HEAD_PAD]
            ss = jnp.sum(xa * xa, axis=-1, keepdims=True) + jnp.sum(xb * xb, axis=-1, keepdims=True)
            r = lax.rsqrt(ss * (1.0 / QK_DIM) + EPS)
            xa, xb = xa * r, xb * r
            da = dqf_ref[:, lo:lo + LANES]
            db = _rope_t(dqf_ref[:, lo + LANES:lo + HEAD_PAD], c, s1, s2)
            dgq_a += jnp.sum(da * xa, axis=0, keepdims=True)
            dgq_b += jnp.sum(db * xb, axis=0, keepdims=True)
            da, db = da * gq[:, :LANES], db * gq[:, LANES:]
            cc = (jnp.sum(da * xa, axis=-1, keepdims=True) + jnp.sum(db * xb, axis=-1, keepdims=True)) * (1.0 / QK_DIM)
            dq_ref[:, lo:lo + LANES] = (r * (da - xa * cc)).astype(BF16)
            dq_ref[:, lo + LANES:lo + HEAD_PAD] = (r * (db - xb * cc)).astype(BF16)
            xa = kv_ref[:, lo:lo + LANES]
            ss = jnp.sum(xa * xa, axis=-1, keepdims=True) + kpe_ss
            r = lax.rsqrt(ss * (1.0 / QK_DIM) + EPS)
            xa, xb = xa * r, kpe * r
            da = dkf_ref[:, lo:lo + LANES]
            db = _rope_t(dkf_ref[:, lo + LANES:lo + HEAD_PAD], c, s1, s2)
            dgk_a += jnp.sum(da * xa, axis=0, keepdims=True)
            dgk_b += jnp.sum(db * xb, axis=0, keepdims=True)
            da, db = da * gk[:, :LANES], db * gk[:, LANES:]
            cc = (jnp.sum(da * xa, axis=-1, keepdims=True) + jnp.sum(db * xb, axis=-1, keepdims=True)) * (1.0 / QK_DIM)
            dkv_ref[:, lo:lo + LANES] = (r * (da - xa * cc)).astype(BF16)
            dkpe = dkpe + r * (db - xb * cc)
            dkv_ref[:, lo + LANES:lo + HEAD_PAD] = dv_ref[:, h * V_DIM:(h + 1) * V_DIM].astype(BF16)
        dkpe_ref[...] = dkpe
        dgq_ref[:, :LANES] += dgq_a
        dgq_ref[:, LANES:] += dgq_b
        dgk_ref[:, :LANES] += dgk_a
        dgk_ref[:, LANES:] += dgk_b

    wide = pl.BlockSpec((tr, H * HEAD_PAD), lambda i: (i, 0))
    lane = pl.BlockSpec((tr, LANES), lambda i: (i, 0))
    gvec = pl.BlockSpec((1, HEAD_PAD), lambda i: (0, 0))
    vspec = pl.BlockSpec((tr, H * V_DIM), lambda i: (i, 0))
    return pl.pallas_call(
        body, name=name, grid=(T // tr,),
        in_specs=[wide, wide, pl.BlockSpec((tr, LANES), lambda i: (i, (Q_LORA + KV_LORA) // LANES)), gvec, gvec,
                  lane, lane, lane, wide, wide, vspec],
        out_specs=[wide, wide, lane, gvec, gvec],
        out_shape=[jax.ShapeDtypeStruct((T, H * HEAD_PAD), BF16), jax.ShapeDtypeStruct((T, H * HEAD_PAD), BF16),
                   jax.ShapeDtypeStruct((T, LANES), F32), jax.ShapeDtypeStruct((1, HEAD_PAD), F32),
                   jax.ShapeDtypeStruct((1, HEAD_PAD), F32)],
        compiler_params=_params(("arbitrary",)),
    )(q_raw, kv, lat, g_qn, g_kn, *tabs, dqf, dkf, dv)


def _causal_mask(tq, tk):
    return lax.broadcasted_iota(jnp.int32, (tq, tk), 1) <= lax.broadcasted_iota(jnp.int32, (tq, tk), 0)


NT_DIMS = (((1,), (1,)), ((), ()))
TN_DIMS = (((0,), (0,)), ((), ()))


def _flash_fwd(qf, kf, v, name):
    T = qf.shape[0]
    H, G = MLA_HEADS, FLASH_HEADS
    t = _pick(T, (512, 256, 128))
    n = T // t
    pairs = [(i, j) for i in range(n) for j in range(i + 1)]
    qi = jnp.asarray([p[0] for p in pairs], jnp.int32)
    kj = jnp.asarray([p[1] for p in pairs], jnp.int32)

    def body(qi_ref, kj_ref, q_ref, k_ref, v_ref, o_ref, lse_ref, m_sc, l_sc, acc_sc):
        sid = pl.program_id(1)
        i, j = qi_ref[sid], kj_ref[sid]

        @pl.when(j == 0)
        def _():
            m_sc[...] = jnp.full_like(m_sc, NEG_BIG)
            l_sc[...] = jnp.zeros_like(l_sc)
            acc_sc[...] = jnp.zeros_like(acc_sc)

        def step(masked):
            for g in range(G):
                qk = slice(g * HEAD_PAD, (g + 1) * HEAD_PAD)
                vo = slice(g * V_DIM, (g + 1) * V_DIM)
                s = lax.dot_general(q_ref[:, qk], k_ref[:, qk], NT_DIMS, preferred_element_type=F32)
                if masked:
                    s = jnp.where(_causal_mask(t, t), s, NEG_BIG)
                m_prev = m_sc[g, :, :1]
                m_new = jnp.maximum(m_prev, jnp.max(s, axis=-1, keepdims=True))
                a = jnp.exp2(m_prev - m_new)
                p = jnp.exp2(s - m_new)
                l_sc[g] = a * l_sc[g] + jnp.sum(p, axis=-1, keepdims=True)
                acc_sc[:, vo] = a * acc_sc[:, vo] + jnp.dot(p.astype(BF16), v_ref[:, vo], preferred_element_type=F32)
                m_sc[g] = jnp.broadcast_to(m_new, (t, LANES))

        @pl.when(j < i)
        def _():
            step(False)

        @pl.when(j == i)
        def _():
            step(True)
            for g in range(G):
                vo = slice(g * V_DIM, (g + 1) * V_DIM)
                o_ref[:, vo] = (acc_sc[:, vo] / l_sc[g]).astype(BF16)
                lse_ref[:, vo] = m_sc[g] + jnp.log2(l_sc[g])

    row = pl.BlockSpec((t, G * V_DIM), lambda h, s, qi, kj: (qi[s], h))
    return pl.pallas_call(
        body, name=name,
        grid_spec=pltpu.PrefetchScalarGridSpec(
            num_scalar_prefetch=2, grid=(H // G, len(pairs)),
            in_specs=[pl.BlockSpec((t, G * HEAD_PAD), lambda h, s, qi, kj: (qi[s], h)),
                      pl.BlockSpec((t, G * HEAD_PAD), lambda h, s, qi, kj: (kj[s], h)),
                      pl.BlockSpec((t, G * V_DIM), lambda h, s, qi, kj: (kj[s], h))],
            out_specs=[row, row],
            scratch_shapes=[pltpu.VMEM((G, t, LANES), F32), pltpu.VMEM((G, t, LANES), F32),
                            pltpu.VMEM((t, G * V_DIM), F32)]),
        out_shape=[jax.ShapeDtypeStruct((T, H * V_DIM), BF16), jax.ShapeDtypeStruct((T, H * V_DIM), F32)],
        compiler_params=_params(("parallel", "arbitrary")),
    )(qi, kj, qf, kf, v)


def _attn_delta(do, o, name):
    T, W = do.shape
    nh = W // V_DIM
    tr = _pick(T, (512, 256, 128))

    def body(do_ref, o_ref, d_ref):
        for h in range(nh):
            sl = slice(h * V_DIM, (h + 1) * V_DIM)
            d = jnp.sum(do_ref[:, sl].astype(F32) * o_ref[:, sl].astype(F32), axis=-1, keepdims=True)
            d_ref[:, sl] = jnp.broadcast_to(d, (tr, V_DIM))

    row = pl.BlockSpec((tr, W), lambda i: (i, 0))
    return pl.pallas_call(
        body, name=name, grid=(T // tr,), in_specs=[row, row], out_specs=row,
        out_shape=jax.ShapeDtypeStruct((T, W), F32), compiler_params=_params(("parallel",)),
    )(do, o)


def _flash_bwd(qf, kf, v, do, lse, delta, name):
    T = qf.shape[0]
    H, G = MLA_HEADS, FLASH_HEADS
    t = _pick(T, (512, 256, 128))
    n = T // t
    scale = 1.0 / math.sqrt(QK_DIM)
    pairs = [(i, j) for j in range(n) for i in range(j, n)]
    qi = jnp.asarray([p[0] for p in pairs], jnp.int32)
    kj = jnp.asarray([p[1] for p in pairs], jnp.int32)

    def body(qi_ref, kj_ref, q_ref, k_ref, v_ref, do_ref, lse_ref, dl_ref, dq_ref, dk_ref, dv_ref, dk_acc, dv_acc):
        sid = pl.program_id(1)
        i, j = qi_ref[sid], kj_ref[sid]

        @pl.when(sid == 0)
        def _():
            dq_ref[...] = jnp.zeros_like(dq_ref)

        def step(masked):
            rows = pl.ds(pl.multiple_of(i * t, t), t)
            for g in range(G):
                qk = slice(g * HEAD_PAD, (g + 1) * HEAD_PAD)
                vo = slice(g * V_DIM, (g + 1) * V_DIM)
                q, k, v_, do_ = q_ref[:, qk], k_ref[:, qk], v_ref[:, vo], do_ref[:, vo]
                s = lax.dot_general(q, k, NT_DIMS, preferred_element_type=F32)
                if masked:
                    s = jnp.where(_causal_mask(t, t), s, NEG_BIG)
                p = jnp.exp2(s - lse_ref[:, g * V_DIM:g * V_DIM + 1])
                dp = lax.dot_general(do_, v_, NT_DIMS, preferred_element_type=F32)
                ds = (p * (dp - dl_ref[:, g * V_DIM:g * V_DIM + 1])).astype(BF16)
                dv = lax.dot_general(p.astype(BF16), do_, TN_DIMS, preferred_element_type=F32)
                dk = lax.dot_general(ds, q, TN_DIMS, preferred_element_type=F32)
                if masked:
                    dv_acc[:, vo] = dv
                    dk_acc[:, qk] = dk
                else:
                    dv_acc[:, vo] += dv
                    dk_acc[:, qk] += dk
                dq_ref[rows, qk] += jnp.dot(ds, k, preferred_element_type=F32) * scale

        @pl.when(i == j)
        def _():
            step(True)

        @pl.when(i > j)
        def _():
            step(False)

        @pl.when(i == n - 1)
        def _():
            dk_ref[...] = dk_acc[...] * LN2
            dv_ref[...] = dv_acc[...]

    qs = pl.BlockSpec((t, G * HEAD_PAD), lambda h, s, qi, kj: (qi[s], h))
    rs = pl.BlockSpec((t, G * V_DIM), lambda h, s, qi, kj: (qi[s], h))
    ks = pl.BlockSpec((t, G * HEAD_PAD), lambda h, s, qi, kj: (kj[s], h))
    vs = pl.BlockSpec((t, G * V_DIM), lambda h, s, qi, kj: (kj[s], h))
    return pl.pallas_call(
        body, name=name,
        grid_spec=pltpu.PrefetchScalarGridSpec(
            num_scalar_prefetch=2, grid=(H // G, len(pairs)), in_specs=[qs, ks, vs, rs, rs, rs],
            out_specs=[pl.BlockSpec((T, G * HEAD_PAD), lambda h, s, qi, kj: (0, h)), ks, vs],
            scratch_shapes=[pltpu.VMEM((t, G * HEAD_PAD), F32), pltpu.VMEM((t, G * V_DIM), F32)]),
        out_shape=[jax.ShapeDtypeStruct((T, H * HEAD_PAD), F32), jax.ShapeDtypeStruct((T, H * HEAD_PAD), F32),
                   jax.ShapeDtypeStruct((T, H * V_DIM), F32)],
        compiler_params=_params(("parallel", "arbitrary")),
    )(qi, kj, qf, kf, v, do, lse, delta)


def _alibi_slopes():
    tot = DIL_GROUPS * DIL_HEADS
    return [float(np.float32(2.0) ** (np.float32(-8.0) * np.float32(k) / np.float32(tot))) for k in range(1, tot + 1)]


def _dil_masks():
    iq = lax.broadcasted_iota(jnp.int32, (DIL_BLK, DIL_BLK), 0)
    ik = lax.broadcasted_iota(jnp.int32, (DIL_BLK, DIL_BLK), 1)
    return (ik >= iq), (iq + DIL_BLK - ik).astype(F32), (ik <= iq), (iq - ik).astype(F32)


def _dil_norm(x, g):
    r = lax.rsqrt(jnp.mean(x * x, axis=-1, keepdims=True) + EPS)
    return x * r, r


DIL_SUPER = 8
BNT_DIMS = (((2,), (2,)), ((0,), (0,)))
BNN_DIMS = (((2,), (1,)), ((0,), (0,)))
BTN_DIMS = (((1,), (1,)), ((0,), (0,)))


def _dil_chunk(it, nb, d):
    assert nb & (nb - 1) == 0, nb
    r, n = it >> (nb.bit_length() - 1), it & (nb - 1)
    if d > 1:
        tok = pl.ds(n * (d * DIL_BLK) + r, DIL_BLK, stride=d)
    else:
        tok = pl.ds(pl.multiple_of(it * DIL_BLK, DIL_BLK), DIL_BLK)
    return tok, pl.ds(pl.multiple_of((it + 1) * DIL_BLK, DIL_BLK), DIL_BLK)


def _dil_token_rows(bidx, nb, d):
    r, n = divmod(bidx, nb)
    return pl.ds(n * DIL_BLK * d + r, DIL_BLK, stride=d) if d > 1 else pl.ds(bidx * DIL_BLK, DIL_BLK)


def _dil_super_rows(ss):
    base = (1 + ss * DIL_SUPER) * DIL_BLK
    return pl.ds(base, DIL_SUPER * DIL_BLK), pl.ds(base - DIL_BLK, DIL_SUPER * DIL_BLK)


def _dil_b3(x):
    return x.reshape(DIL_SUPER, DIL_BLK, x.shape[-1])


def _dil_scores(q3, kc3, kp3, slope, d, ss, nb):
    ok_p, dist_p, ok_c, dist_c = _dil_masks()
    scale = 1.0 / math.sqrt(DIL_HEAD_DIM)
    bias_p = jnp.where(ok_p, -slope * d * dist_p, NEG_BIG)
    bias_c = jnp.where(ok_c, -slope * d * dist_c, NEG_BIG)
    s_c = lax.dot_general(q3, kc3, BNT_DIMS, preferred_element_type=F32) * scale + bias_c[None]
    s_p = lax.dot_general(q3, kp3, BNT_DIMS, preferred_element_type=F32) * scale + bias_p[None]
    bidx = ss * DIL_SUPER + lax.broadcasted_iota(jnp.int32, s_p.shape, 0)
    s_p = jnp.where((bidx & (nb - 1)) == 0, NEG_BIG, s_p)
    return s_c, s_p


def _dil_fwd(qkv, g_qn, g_kn, slopes, name):
    T = qkv.shape[0]
    GH = DIL_GROUPS * DIL_HEADS
    scale = 1.0 / math.sqrt(DIL_HEAD_DIM)

    def body(sl_ref, q_ref, k_ref, v_ref, gq_ref, gk_ref, o_ref, lse_ref, qn_pm, kn_pm, v_pm):
        gh = pl.program_id(0)
        slope = sl_ref[gh]
        gq, gk = gq_ref[...], gk_ref[...]
        pad = pl.ds(0, DIL_BLK)
        kn_pm[pad, :] = jnp.zeros((DIL_BLK, DIL_HEAD_DIM), BF16)
        v_pm[pad, :] = jnp.zeros((DIL_BLK, DIL_HEAD_DIM), BF16)
        for g, (_, d) in enumerate(DIL_PAIRS):
            @pl.when((gh >= g * DIL_HEADS) & (gh < (g + 1) * DIL_HEADS))
            def _(d=d):
                nb = T // (d * DIL_BLK)

                def fill(it, _):
                    tok, dst = _dil_chunk(it, nb, d)
                    qn_pm[dst, :] = (_dil_norm(q_ref[tok, :], gq)[0] * gq).astype(BF16)
                    kn_pm[dst, :] = (_dil_norm(k_ref[tok, :], gk)[0] * gk).astype(BF16)
                    v_pm[dst, :] = v_ref[tok, :].astype(BF16)
                    return 0
                lax.fori_loop(0, T // DIL_BLK, fill, 0, unroll=4)

                for ss in range(T // DIL_BLK // DIL_SUPER):
                    cur, prv = _dil_super_rows(ss)
                    q3, kc3, kp3 = _dil_b3(qn_pm[cur, :]), _dil_b3(kn_pm[cur, :]), _dil_b3(kn_pm[prv, :])
                    s_c, s_p = _dil_scores(q3, kc3, kp3, slope, d, ss, nb)
                    m = jnp.max(jnp.maximum(s_c, s_p), axis=-1, keepdims=True)
                    p_c = jnp.exp(s_c - m)
                    p_p = jnp.exp(s_p - m)
                    l = jnp.sum(p_c, axis=-1, keepdims=True) + jnp.sum(p_p, axis=-1, keepdims=True)
                    acc = lax.dot_general(p_c.astype(BF16), _dil_b3(v_pm[cur, :]), BNN_DIMS, preferred_element_type=F32)
                    acc += lax.dot_general(p_p.astype(BF16), _dil_b3(v_pm[prv, :]), BNN_DIMS, preferred_element_type=F32)
                    o3 = acc / l
                    lse3 = jnp.broadcast_to(m + jnp.log(l), o3.shape)
                    for b in range(DIL_SUPER):
                        tok = _dil_token_rows(ss * DIL_SUPER + b, nb, d)
                        o_ref[tok, :] = o3[b]
                        lse_ref[tok, :] = lse3[b]

    col = lambda off: pl.BlockSpec((T, DIL_HEAD_DIM), lambda gh, sl: (0, gh + off))
    gvec = pl.BlockSpec((1, DIL_HEAD_DIM), lambda gh, sl: (0, 0))
    return pl.pallas_call(
        body, name=name,
        grid_spec=pltpu.PrefetchScalarGridSpec(
            num_scalar_prefetch=1, grid=(GH,),
            in_specs=[col(0), col(GH), col(2 * GH), gvec, gvec], out_specs=[col(0), col(0)],
            scratch_shapes=[pltpu.VMEM((DIL_BLK + T, DIL_HEAD_DIM), BF16)] * 3),
        out_shape=[jax.ShapeDtypeStruct((T, GH * DIL_HEAD_DIM), F32)] * 2,
        compiler_params=_params(("parallel",)),
    )(slopes, qkv, qkv, qkv, g_qn, g_kn)


def _dil_merge(o_g, lse_g, name):
    T = o_g.shape[0]
    W = DIL_HEADS * DIL_HEAD_DIM
    tr = _pick(T, (256, 128))

    def body(o0, o1, o2, l0, l1, l2, o_ref, lse_ref):
        a, b, c = l0[...], l1[...], l2[...]
        m = jnp.maximum(jnp.maximum(a, b), c)
        ea, eb, ec = jnp.exp(a - m), jnp.exp(b - m), jnp.exp(c - m)
        tot = ea + eb + ec
        o_ref[...] = ((o0[...] * ea + o1[...] * eb + o2[...] * ec) / tot).astype(BF16)
        lse_ref[...] = m + jnp.log(tot)

    grp = lambda g: pl.BlockSpec((tr, W), lambda i: (i, g))
    out = pl.BlockSpec((tr, W), lambda i: (i, 0))
    return pl.pallas_call(
        body, name=name, grid=(T // tr,),
        in_specs=[grp(0), grp(1), grp(2), grp(0), grp(1), grp(2)], out_specs=[out, out],
        out_shape=[jax.ShapeDtypeStruct((T, W), BF16), jax.ShapeDtypeStruct((T, W), F32)],
        compiler_params=_params(("parallel",)),
    )(o_g, o_g, o_g, lse_g, lse_g, lse_g)


def _dil_bwd(qkv, g_qn, g_kn, slopes, do, delta, lse, name):
    T = qkv.shape[0]
    GH = DIL_GROUPS * DIL_HEADS
    scale = 1.0 / math.sqrt(DIL_HEAD_DIM)
    nchunk = T // DIL_BLK

    def body(sl_ref, q_ref, k_ref, v_ref, gq_ref, gk_ref, do_ref, dl_ref, lse_ref,
             dq_ref, dk_ref, dv_ref, dgq_ref, dgk_ref,
             qn_pm, kn_pm, v_pm, do_pm, lse_pm, dl_pm, dq_pm, dk_pm, dv_pm, tok_sc):
        gh = pl.program_id(0)
        slope = sl_ref[gh]
        gq, gk = gq_ref[...], gk_ref[...]

        @pl.when(gh == 0)
        def _():
            dgq_ref[...] = jnp.zeros_like(dgq_ref)
            dgk_ref[...] = jnp.zeros_like(dgk_ref)

        pad = pl.ds(0, DIL_BLK)
        kn_pm[pad, :] = jnp.zeros((DIL_BLK, DIL_HEAD_DIM), BF16)
        v_pm[pad, :] = jnp.zeros((DIL_BLK, DIL_HEAD_DIM), BF16)
        dk_pm[...] = jnp.zeros_like(dk_pm)
        dv_pm[...] = jnp.zeros_like(dv_pm)
        for g, (_, d) in enumerate(DIL_PAIRS):
            @pl.when((gh >= g * DIL_HEADS) & (gh < (g + 1) * DIL_HEADS))
            def _(d=d):
                nb = T // (d * DIL_BLK)

                def fill(it, _):
                    tok, dst = _dil_chunk(it, nb, d)
                    qn_pm[dst, :] = (_dil_norm(q_ref[tok, :], gq)[0] * gq).astype(BF16)
                    kn_pm[dst, :] = (_dil_norm(k_ref[tok, :], gk)[0] * gk).astype(BF16)
                    v_pm[dst, :] = v_ref[tok, :].astype(BF16)
                    do_pm[dst, :] = do_ref[tok, :].astype(BF16)
                    lse_pm[dst, :] = lse_ref[tok, :]
                    dl_pm[dst, :] = dl_ref[tok, :]
                    return 0
                lax.fori_loop(0, nchunk, fill, 0, unroll=4)

                for ss in range(nchunk // DIL_SUPER):
                    cur, prv = _dil_super_rows(ss)
                    q3, kc3, kp3 = _dil_b3(qn_pm[cur, :]), _dil_b3(kn_pm[cur, :]), _dil_b3(kn_pm[prv, :])
                    vc3, vp3, do3 = _dil_b3(v_pm[cur, :]), _dil_b3(v_pm[prv, :]), _dil_b3(do_pm[cur, :])
                    ls = _dil_b3(lse_pm[cur, :])[:, :, :1]
                    delta = _dil_b3(dl_pm[cur, :])[:, :, :1]
                    s_c, s_p = _dil_scores(q3, kc3, kp3, slope, d, ss, nb)
                    p_c = jnp.exp(s_c - ls)
                    p_p = jnp.exp(s_p - ls)
                    dp_c = lax.dot_general(do3, vc3, BNT_DIMS, preferred_element_type=F32)
                    dp_p = lax.dot_general(do3, vp3, BNT_DIMS, preferred_element_type=F32)
                    ds_c = (p_c * (dp_c - delta)).astype(BF16)
                    ds_p = (p_p * (dp_p - delta)).astype(BF16)
                    dq3 = (lax.dot_general(ds_c, kc3, BNN_DIMS, preferred_element_type=F32)
                           + lax.dot_general(ds_p, kp3, BNN_DIMS, preferred_element_type=F32)) * scale
                    flat = lambda x: x.reshape(DIL_SUPER * DIL_BLK, DIL_HEAD_DIM)
                    dq_pm[cur, :] = flat(dq3)
                    dk_pm[cur, :] += flat(lax.dot_general(ds_c, q3, BTN_DIMS, preferred_element_type=F32)) * scale
                    dv_pm[cur, :] += flat(lax.dot_general(p_c.astype(BF16), do3, BTN_DIMS, preferred_element_type=F32))
                    dk_pm[prv, :] += flat(lax.dot_general(ds_p, q3, BTN_DIMS, preferred_element_type=F32)) * scale
                    dv_pm[prv, :] += flat(lax.dot_general(p_p.astype(BF16), do3, BTN_DIMS, preferred_element_type=F32))

                def to_tokens(src_pm):
                    def move(it, _):
                        tok, src = _dil_chunk(it, nb, d)
                        tok_sc[tok, :] = src_pm[src, :]
                        return 0
                    lax.fori_loop(0, nchunk, move, 0, unroll=4)

                def norm_bwd(x_ref, gvec, out_ref):
                    big = 4 * DIL_BLK

                    def fin(ci, dg):
                        rows = pl.ds(pl.multiple_of(ci * big, big), big)
                        xhat, r = _dil_norm(x_ref[rows, :], gvec)
                        dn = tok_sc[rows, :]
                        dxh = dn * gvec
                        c = jnp.mean(dxh * xhat, axis=-1, keepdims=True)
                        out_ref[rows, :] = (r * (dxh - xhat * c)).astype(BF16)
                        return dg + jnp.sum(dn * xhat, axis=0, keepdims=True)
                    return lax.fori_loop(0, T // big, fin, jnp.zeros((1, DIL_HEAD_DIM), F32))

                to_tokens(dq_pm)
                dgq_ref[...] += norm_bwd(q_ref, gq, dq_ref)
                to_tokens(dk_pm)
                dgk_ref[...] += norm_bwd(k_ref, gk, dk_ref)
                to_tokens(dv_pm)
                dv_ref[...] = tok_sc[...].astype(BF16)

    col = lambda off: pl.BlockSpec((T, DIL_HEAD_DIM), lambda gh, sl: (0, gh + off))
    hcol = pl.BlockSpec((T, DIL_HEAD_DIM), lambda gh, sl: (0, gh % DIL_HEADS))
    gvec = pl.BlockSpec((1, DIL_HEAD_DIM), lambda gh, sl: (0, 0))
    wide = jax.ShapeDtypeStruct((T, GH * DIL_HEAD_DIM), BF16)
    vec = jax.ShapeDtypeStruct((1, DIL_HEAD_DIM), F32)
    pm = lambda dt: pltpu.VMEM((DIL_BLK + T, DIL_HEAD_DIM), dt)
    return pl.pallas_call(
        body, name=name,
        grid_spec=pltpu.PrefetchScalarGridSpec(
            num_scalar_prefetch=1, grid=(GH,),
            in_specs=[col(0), col(GH), col(2 * GH), gvec, gvec, hcol, hcol, hcol],
            out_specs=[col(0), col(0), col(0), gvec, gvec],
            scratch_shapes=[pm(BF16)] * 4 + [pm(F32)] * 5 + [pltpu.VMEM((T, DIL_HEAD_DIM), F32)]),
        out_shape=[wide, wide, wide, vec, vec],
        compiler_params=_params(("arbitrary",)),
    )(slopes, qkv, qkv, qkv, g_qn, g_kn, do, delta, lse)


def _my_pos():
    return lax.axis_index("x"), lax.axis_index("y"), lax.axis_index("c")


def _peer(pos, j):
    x, y, c = pos
    px = 1 - x if j & 4 else x
    py = 1 - y if j & 2 else y
    pc = 1 - c if j & 1 else c
    return (px, py, pc), 4 * px + 2 * py + pc


def _slot(idx, paired):
    if not paired:
        return idx
    return jnp.where(idx < N_DEV // 2, 2 * idx, 2 * idx - (N_DEV - 1))


def _shard_slice(ref, axis, idx, size, paired=False):
    sl = [slice(None)] * len(ref.shape)
    sl[axis] = pl.ds(pl.multiple_of(_slot(idx, paired) * size, 8), size)
    return ref.at[tuple(sl)]


HBM_SPEC = pl.BlockSpec(memory_space=pltpu.HBM)
SEM_SPEC = pl.BlockSpec(memory_space=pltpu.SEMAPHORE)
DATAFLOW = pltpu.SideEffectType.DATAFLOW_SIDE_EFFECTING
N_PEER = N_DEV - 1


def _scatter_copy(axis, grad, slots, frm, to, dev, send_sem, recv_sem):
    ax, paired = axis
    src = _shard_slice(grad, ax, to, grad.shape[ax] // N_DEV, paired)
    return pltpu.make_async_remote_copy(src_ref=src, dst_ref=slots.at[frm], send_sem=send_sem, recv_sem=recv_sem,
                                        device_id=dev, device_id_type=MESH)


def _scatter_start(grads, axes, name):
    n = len(grads)

    def body(*refs):
        outs = refs[2 * n:]
        send, recv, token = outs[:n], outs[n:2 * n], outs[4 * n]
        pos = _my_pos()
        me = 4 * pos[0] + 2 * pos[1] + pos[2]
        for a in range(n):
            for j in range(1, N_DEV):
                dev, pid = _peer(pos, j)
                _scatter_copy(axes[a], refs[2 * a], refs[2 * a + 1], me, pid, dev, send[a].at[j - 1],
                              recv[a].at[j - 1]).start()
        token[...] = jnp.zeros_like(token)

    ops = []
    for g, (ax, _) in zip(grads, axes):
        shp = list(g.shape)
        shp[ax] //= N_DEV
        ops += [g, lax.empty((N_DEV,) + tuple(shp), g.dtype)]
    sems = [pltpu.SemaphoreType.DMA((N_PEER,))] * (2 * n)
    res = pl.pallas_call(
        body, name=name,
        out_shape=sems + [pltpu.HBM(o.shape, o.dtype) for o in ops] + [jax.ShapeDtypeStruct((8, LANES), F32)],
        in_specs=[HBM_SPEC] * len(ops),
        out_specs=[SEM_SPEC] * (2 * n) + [HBM_SPEC] * len(ops) + [pl.BlockSpec(memory_space=pltpu.VMEM)],
        input_output_aliases={i: 2 * n + i for i in range(len(ops))},
        compiler_params=pltpu.CompilerParams(has_side_effects=DATAFLOW),
    )(*[pltpu.with_memory_space_constraint(o, pltpu.HBM) for o in ops])
    items = [(res[a], res[n + a], res[2 * n + 2 * a], res[2 * n + 2 * a + 1]) for a in range(n)]
    return items, res[4 * n]


def _scatter_wait(items, axes, after, name):
    n = len(items)

    def body(*refs):
        send, recv = refs[2 * n:3 * n], refs[3 * n:4 * n]
        pos = _my_pos()
        me = 4 * pos[0] + 2 * pos[1] + pos[2]
        for a in range(n):
            for j in range(1, N_DEV):
                dev, pid = _peer(pos, j)
                cp = _scatter_copy(axes[a], refs[2 * a], refs[2 * a + 1], pid, me, dev, send[a].at[j - 1],
                                   recv[a].at[j - 1])
                cp.wait_send()
                cp.wait_recv()

    ops = [b for it in items for b in it[2:]]
    res = pl.pallas_call(
        body, name=name,
        out_shape=[pltpu.HBM(o.shape, o.dtype) for o in ops],
        in_specs=[HBM_SPEC] * len(ops) + [SEM_SPEC] * (2 * n) + [ANY_SPEC],
        out_specs=[HBM_SPEC] * len(ops),
        input_output_aliases={i: i for i in range(len(ops))},
        compiler_params=pltpu.CompilerParams(has_side_effects=DATAFLOW),
    )(*ops, *[it[0] for it in items], *[it[1] for it in items], after)
    return [(res[2 * a], res[2 * a + 1]) for a in range(n)]


SIBLING = 1
ICI_PEERS = (2, 4, 6)


def _gather_copy(buf, axis, shard, dev, send_sem, recv_sem):
    ax, paired = axis
    piece = _shard_slice(buf, ax, shard, buf.shape[ax] // N_DEV, paired)
    return pltpu.make_async_remote_copy(src_ref=piece, dst_ref=piece, send_sem=send_sem, recv_sem=recv_sem,
                                        device_id=dev, device_id_type=MESH)


def _gather_start(bufs, axes, name):
    n = len(bufs)

    def body(*refs):
        ins, outs = refs[:n], refs[n:]
        send, r_sib, r_ici, token = outs[:n], outs[n:2 * n], outs[2 * n:3 * n], outs[4 * n]
        pos = _my_pos()
        me = 4 * pos[0] + 2 * pos[1] + pos[2]
        for a in range(n):
            dev, _ = _peer(pos, SIBLING)
            _gather_copy(ins[a], axes[a], me, dev, send[a].at[0], r_sib[a].at[0]).start()
            for k, j in enumerate(ICI_PEERS):
                dev, _ = _peer(pos, j)
                _gather_copy(ins[a], axes[a], me, dev, send[a].at[1 + k], r_ici[a].at[k]).start()
        token[...] = jnp.zeros_like(token)

    sems = ([pltpu.SemaphoreType.DMA((1 + len(ICI_PEERS),))] * n + [pltpu.SemaphoreType.DMA((1,))] * n
            + [pltpu.SemaphoreType.DMA((len(ICI_PEERS),))] * n)
    res = pl.pallas_call(
        body, name=name,
        out_shape=sems + [pltpu.HBM(b.shape, b.dtype) for b in bufs] + [jax.ShapeDtypeStruct((8, LANES), F32)],
        in_specs=[HBM_SPEC] * n,
        out_specs=[SEM_SPEC] * (3 * n) + [HBM_SPEC] * n + [pl.BlockSpec(memory_space=pltpu.VMEM)],
        input_output_aliases={i: 3 * n + i for i in range(n)},
        compiler_params=pltpu.CompilerParams(has_side_effects=DATAFLOW),
    )(*[pltpu.with_memory_space_constraint(b, pltpu.HBM) for b in bufs])
    items = [dict(send=res[a], r_sib=res[n + a], r_ici=res[2 * n + a], buf=res[3 * n + a]) for a in range(n)]
    return items, res[4 * n]


def _gather_relay(items, axes, after, name):
    n = len(items)

    def body(*refs):
        ins, r_ici = refs[:n], refs[n:2 * n]
        outs = refs[2 * n + 1:]
        s_rel, r_rel, token = outs[:n], outs[n:2 * n], outs[3 * n]
        pos = _my_pos()
        sib, _ = _peer(pos, SIBLING)
        for a in range(n):
            for k, j in enumerate(ICI_PEERS):
                dev, pid = _peer(pos, j)
                _gather_copy(ins[a], axes[a], pid, dev, s_rel[a].at[k], r_ici[a].at[k]).wait_recv()
                _gather_copy(ins[a], axes[a], pid, sib, s_rel[a].at[k], r_rel[a].at[k]).start()
        token[...] = jnp.zeros_like(token)

    bufs = [it["buf"] for it in items]
    sems = [pltpu.SemaphoreType.DMA((len(ICI_PEERS),))] * (2 * n)
    res = pl.pallas_call(
        body, name=name,
        out_shape=sems + [pltpu.HBM(b.shape, b.dtype) for b in bufs] + [jax.ShapeDtypeStruct((8, LANES), F32)],
        in_specs=[HBM_SPEC] * n + [SEM_SPEC] * n + [ANY_SPEC],
        out_specs=[SEM_SPEC] * (2 * n) + [HBM_SPEC] * n + [pl.BlockSpec(memory_space=pltpu.VMEM)],
        input_output_aliases={i: 2 * n + i for i in range(n)},
        compiler_params=pltpu.CompilerParams(has_side_effects=DATAFLOW),
    )(*bufs, *[it["r_ici"] for it in items], after)
    out = [dict(send=it["send"], r_sib=it["r_sib"], s_rel=res[a], r_rel=res[n + a], buf=res[2 * n + a])
           for a, it in enumerate(items)]
    return out, res[3 * n]


def _gather_wait(items, axes, after, name):
    n = len(items)

    def body(*refs):
        ins = refs[:n]
        send, r_sib, s_rel, r_rel = (refs[(1 + q) * n:(2 + q) * n] for q in range(4))
        pos = _my_pos()
        me = 4 * pos[0] + 2 * pos[1] + pos[2]
        sib, sib_id = _peer(pos, SIBLING)
        for a in range(n):
            for k in range(1 + len(ICI_PEERS)):
                _gather_copy(ins[a], axes[a], me, sib, send[a].at[k], r_sib[a].at[0]).wait_send()
            _gather_copy(ins[a], axes[a], sib_id, sib, send[a].at[0], r_sib[a].at[0]).wait_recv()
            for k, j in enumerate(ICI_PEERS):
                _, pid = _peer(pos, j)
                _, far = _peer(pos, j ^ SIBLING)
                _gather_copy(ins[a], axes[a], pid, sib, s_rel[a].at[k], r_rel[a].at[k]).wait_send()
                _gather_copy(ins[a], axes[a], far, sib, s_rel[a].at[k], r_rel[a].at[k]).wait_recv()

    bufs = [it["buf"] for it in items]
    res = pl.pallas_call(
        body, name=name,
        out_shape=[pltpu.HBM(b.shape, b.dtype) for b in bufs],
        in_specs=[HBM_SPEC] * n + [SEM_SPEC] * (4 * n) + [ANY_SPEC],
        out_specs=[HBM_SPEC] * n,
        input_output_aliases={i: i for i in range(n)},
        compiler_params=pltpu.CompilerParams(has_side_effects=DATAFLOW),
    )(*bufs, *[it["send"] for it in items], *[it["r_sib"] for it in items], *[it["s_rel"] for it in items],
      *[it["r_rel"] for it in items], after)
    return list(res)


def _gain_allreduce(v, name):
    n = v.shape[1]

    def body(v_ref, o_ref, slots, send_sems, recv_sems):
        pos = _my_pos()
        me = 4 * pos[0] + 2 * pos[1] + pos[2]
        slots[me] = v_ref[...]
        copies = []
        for j in range(1, N_DEV):
            dev, _ = _peer(pos, j)
            cp = pltpu.make_async_remote_copy(
                src_ref=slots.at[me], dst_ref=slots.at[me], send_sem=send_sems.at[j], recv_sem=recv_sems.at[j],
                device_id=dev, device_id_type=MESH)
            cp.start()
            copies.append(cp)
        for j in range(1, N_DEV):
            dev, pid = _peer(pos, j)
            pltpu.make_async_remote_copy(
                src_ref=slots.at[me], dst_ref=slots.at[pid], send_sem=send_sems.at[j], recv_sem=recv_sems.at[j],
                device_id=dev, device_id_type=MESH).wait_recv()
        for cp in copies:
            cp.wait_send()
        acc = slots[0]
        for s in range(1, N_DEV):
            acc = acc + slots[s]
        o_ref[...] = acc

    return pl.pallas_call(
        body, name=name, out_shape=jax.ShapeDtypeStruct((1, n), F32),
        in_specs=[pl.BlockSpec(memory_space=pltpu.VMEM)], out_specs=pl.BlockSpec(memory_space=pltpu.VMEM),
        scratch_shapes=[pltpu.VMEM((N_DEV, 1, n), F32), pltpu.SemaphoreType.DMA((N_DEV,)),
                        pltpu.SemaphoreType.DMA((N_DEV,))],
        compiler_params=pltpu.CompilerParams(has_side_effects=True),
    )(v)


def _adamw(parts, own, me, w, m, v, layer, prev, name, own_axis=None):
    L, R, C = w.shape
    P = parts.shape[0]
    tr = _pick(R, (128, 64, 32, 16, 8, 1))
    c1 = 1.0 - ADAM_B1 ** ADAM_STEP
    c2 = 1.0 - ADAM_B2 ** ADAM_STEP
    n_in = 4 if own is None else 5

    def body(me_ref, *refs):
        p_ref = refs[0]
        w_ref, m_ref, v_ref = refs[n_in - 3:n_in]
        g_out, d_out, m_out, v_out, tok = refs[-5:]
        g = None
        for s in range(P):
            part = p_ref[s]
            if own is not None:
                part = jnp.where(me_ref[0] == s, refs[1][...], part)
            g = part.astype(F32) if g is None else g + part.astype(F32)
        mn = ADAM_B1 * m_ref[...] + (1.0 - ADAM_B1) * g
        vn = ADAM_B2 * v_ref[...] + (1.0 - ADAM_B2) * (g * g)
        g_out[...] = g
        m_out[...] = mn
        v_out[...] = vn
        d_out[...] = -ADAM_LR * ((mn / c1) / (jnp.sqrt(vn / c2) + ADAM_EPS) + ADAM_WD * w_ref[...])
        tok[...] = jnp.zeros_like(tok)

    row = pl.BlockSpec((None, tr, C), lambda i, me_ref: (layer, i, 0))
    in_specs = [pl.BlockSpec((P, tr, C), lambda i, me_ref: (0, i, 0))]
    args = [parts]
    if own is not None:
        if own_axis is None:
            own_idx = lambda i, me_ref: (i, 0)
        elif own_axis[0] == 0:
            own_idx = lambda i, me_ref: (_slot(me_ref[0], own_axis[1]) * (R // tr) + i, 0)
        else:
            own_idx = lambda i, me_ref: (i, _slot(me_ref[0], own_axis[1]))
        in_specs.append(pl.BlockSpec((tr, C), own_idx))
        args.append(own)
    in_specs += [row, row, row]
    args += [w, m, v]
    aliases = {}
    if prev is not None:
        in_specs += [ANY_SPEC] * 4
        aliases = {1 + len(args) + k: k for k in range(4)}
        args += list(prev)
    shp = jax.ShapeDtypeStruct((L, R, C), F32)
    res = pl.pallas_call(
        body, name=name,
        grid_spec=pltpu.PrefetchScalarGridSpec(
            num_scalar_prefetch=1, grid=(R // tr,), in_specs=in_specs,
            out_specs=[row] * 4 + [pl.BlockSpec((8, LANES), lambda i, me_ref: (0, 0))]),
        out_shape=[shp] * 4 + [jax.ShapeDtypeStruct((8, LANES), F32)],
        input_output_aliases=aliases, compiler_params=_params(("arbitrary",)),
    )(me, *args)
    return res[:4], res[4]


def _pad_heads(w):
    lead = w.shape[:-1]
    n = w.shape[-1] // QK_DIM
    w = w.reshape(lead + (n, QK_DIM))
    w = jnp.pad(w, [(0, 0)] * len(lead) + [(0, 0), (0, HEAD_PAD - QK_DIM)])
    return w.reshape(lead + (n * HEAD_PAD,))


def _unpad_heads(w):
    lead = w.shape[:-1]
    n = w.shape[-1] // HEAD_PAD
    return w.reshape(lead + (n, HEAD_PAD))[..., :QK_DIM].reshape(lead + (n * QK_DIM,))


def kernel(x, ffn1_norm, ffn1_w_in, ffn1_w_out, mix_norm, ffn2_norm, ffn2_w_in, ffn2_w_out, mla_w_down, mla_g_cq, mla_g_ckv, mla_w_uq, mla_w_ukv, mla_g_qn, mla_g_kn, mla_w_o, dil_w_qkv, dil_g_qn, dil_g_kn, dil_w_o, loss_target, m_ffn1_norm, m_ffn1_w_in, m_ffn1_w_out, m_mix_norm, m_ffn2_norm, m_ffn2_w_in, m_ffn2_w_out, m_mla_w_down, m_mla_g_cq, m_mla_g_ckv, m_mla_w_uq, m_mla_w_ukv, m_mla_g_qn, m_mla_g_kn, m_mla_w_o, m_dil_w_qkv, m_dil_g_qn, m_dil_g_kn, m_dil_w_o, v_ffn1_norm, v_ffn1_w_in, v_ffn1_w_out, v_mix_norm, v_ffn2_norm, v_ffn2_w_in, v_ffn2_w_out, v_mla_w_down, v_mla_g_cq, v_mla_g_ckv, v_mla_w_uq, v_mla_w_ukv, v_mla_g_qn, v_mla_g_kn, v_mla_w_o, v_dil_w_qkv, v_dil_g_qn, v_dil_g_kn, v_dil_w_o):
    names = ["ffn1_norm", "ffn1_w_in", "ffn1_w_out", "mix_norm", "ffn2_norm", "ffn2_w_in", "ffn2_w_out", "mla_w_down",
             "mla_g_cq", "mla_g_ckv", "mla_w_uq", "mla_w_ukv", "mla_g_qn", "mla_g_kn", "mla_w_o", "dil_w_qkv",
             "dil_g_qn", "dil_g_kn", "dil_w_o"]
    W = dict(zip(names, [ffn1_norm, ffn1_w_in, ffn1_w_out, mix_norm, ffn2_norm, ffn2_w_in, ffn2_w_out, mla_w_down,
                         mla_g_cq, mla_g_ckv, mla_w_uq, mla_w_ukv, mla_g_qn, mla_g_kn, mla_w_o, dil_w_qkv,
                         dil_g_qn, dil_g_kn, dil_w_o]))
    M1 = dict(zip(names, [m_ffn1_norm, m_ffn1_w_in, m_ffn1_w_out, m_mix_norm, m_ffn2_norm, m_ffn2_w_in, m_ffn2_w_out,
                          m_mla_w_down, m_mla_g_cq, m_mla_g_ckv, m_mla_w_uq, m_mla_w_ukv, m_mla_g_qn, m_mla_g_kn,
                          m_mla_w_o, m_dil_w_qkv, m_dil_g_qn, m_dil_g_kn, m_dil_w_o]))
    V2 = dict(zip(names, [v_ffn1_norm, v_ffn1_w_in, v_ffn1_w_out, v_mix_norm, v_ffn2_norm, v_ffn2_w_in, v_ffn2_w_out,
                          v_mla_w_down, v_mla_g_cq, v_mla_g_ckv, v_mla_w_uq, v_mla_w_ukv, v_mla_g_qn, v_mla_g_kn,
                          v_mla_w_o, v_dil_w_qkv, v_dil_g_qn, v_dil_g_kn, v_dil_w_o]))
    S, D = x.shape[1], x.shape[2]
    x0 = x.reshape(S, D)
    tgt = loss_target.reshape(S, D)

    big = ["ffn1_w_in", "ffn1_w_out", "ffn2_w_in", "ffn2_w_out", "mla_w_down", "mla_w_uq", "mla_w_ukv", "mla_w_o",
           "dil_w_qkv", "dil_w_o"]
    shard_dim = {"ffn1_w_in": 2, "ffn1_w_out": 1, "ffn2_w_in": 2, "ffn2_w_out": 1, "mla_w_down": 1, "mla_w_uq": 2,
                 "mla_w_ukv": 2, "mla_w_o": 1, "dil_w_qkv": 2, "dil_w_o": 2}
    paired = ("ffn1_w_in", "ffn2_w_in")
    shard_axis = {n: (d, n in paired) for n, d in shard_dim.items()}
    grad_axis = {n: (d - 1, n in paired) for n, d in shard_dim.items()}

    def padded(n, w):
        if n == "mla_w_down":
            return jnp.pad(w, ((0, 0), (0, 0), (0, LAT_PAD - w.shape[2])))
        if n == "mla_w_uq":
            return _pad_heads(w)
        return w

    depth = ffn1_norm.shape[0]
    blocks = []
    for l in range(depth):
        mixer = (["mla_w_down", "mla_w_uq", "mla_w_ukv", "mla_w_o"] if l % 2 == 0 else ["dil_w_qkv", "dil_w_o"])
        blocks.append((f"ffn1_{l}", [("ffn1_w_in", l), ("ffn1_w_out", l)]))
        blocks.append((f"mix_{l}", [(n, l // 2) for n in mixer]))
        blocks.append((f"ffn2_{l}", [("ffn2_w_in", l), ("ffn2_w_out", l)]))
    order = [k for _, keys in blocks for k in keys]
    me = (4 * lax.axis_index("x") + 2 * lax.axis_index("y") + lax.axis_index("c")).astype(jnp.int32).reshape(1)
    def cast(key, deps=()):
        n, l = key
        return _cast_into_gathered(padded(n, W[n]), l, shard_axis[n], me, f"cast_{n}_{l}", deps=deps)

    items0, token0 = _gather_start([cast(order[0])], [shard_axis[order[0][0]]], "gather_start_first")
    rest = order[1:]
    items1, ag_token = _gather_start([cast(k, deps=[token0]) for k in rest], [shard_axis[k[0]] for k in rest],
                                     "gather_start_rest")
    ag_items = dict(zip(order, items0 + items1))
    full = {}

    def relay(keys, after, tag):
        out, token = _gather_relay([ag_items[k] for k in keys], [shard_axis[k[0]] for k in keys], after,
                                   f"gather_relay_{tag}")
        ag_items.update(zip(keys, out))
        return [token]

    def relay_next(bi, after):
        return relay(blocks[bi + 1][1], after, blocks[bi + 1][0]) if bi + 1 < len(blocks) else []

    def fetch(keys, after, tag):
        lands = _gather_wait([ag_items[k] for k in keys], [shard_axis[k[0]] for k in keys], after,
                             f"gather_wait_{tag}")
        full.update(zip(keys, lands))

    g_qn = _pad_heads(mla_g_qn)
    g_kn = _pad_heads(mla_g_kn)
    tabs = _rope_tables(S)
    slopes = jnp.asarray(_alibi_slopes(), F32)

    grads = {}
    gain_g = {}

    out_g, out_d, out_m, out_v = {}, {}, {}, {}
    pending = []
    lag = 3

    def scatter_start(tag, keys):
        items, token = _scatter_start([grads[k] for k in keys], [grad_axis[k[0]] for k in keys],
                                      f"scatter_start_{tag}")
        pending.append((tag, keys, items))
        return token

    def scatter_finish(after):
        tag, keys, items = pending.pop(0)
        lands = _scatter_wait(items, [grad_axis[k[0]] for k in keys], after, f"scatter_wait_{tag}")
        tokens = []
        for (n, l), (own, p) in zip(keys, lands):
            own_axis = grad_axis[n]
            if n in ("mla_w_down", "mla_w_uq"):
                ax, pair = grad_axis[n]
                size = own.shape[ax] // N_DEV
                own = lax.dynamic_slice_in_dim(own, _slot(me[0], pair) * size, size, axis=ax)
                own_axis = None
                if n == "mla_w_down":
                    p, own = p[..., :W[n].shape[2]], own[..., :W[n].shape[2]]
                else:
                    p, own = _unpad_heads(p), _unpad_heads(own)
            prev = (out_g[n], out_d[n], out_m[n], out_v[n]) if n in out_g else None
            (out_g[n], out_d[n], out_m[n], out_v[n]), tok = _adamw(p, own, me, W[n], M1[n], V2[n], l, prev,
                                                                    f"adamw_{n}_{l}", own_axis=own_axis)
            tokens.append(tok)
        return tokens

    def finish_due(after):
        tokens = []
        while len(pending) > lag:
            tokens += scatter_finish(after)
        return tokens

    def ffn_fwd(xin, norm_row, which, l, bi, deps=()):
        tag = blocks[bi][0]
        k_in, k_out = (which + "_w_in", l), (which + "_w_out", l)
        h = _rms_fwd(xin, norm_row, f"rms_fwd_{tag}", deps=deps)
        if bi == 0:
            relay([k_in], h, f"{tag}_in")
        fetch([k_in], h, f"in_{tag}")
        u, a = _ffn_in(h, full[k_in], f"ffn_in_{tag}")
        if bi == 0:
            relay([k_out], a, f"{tag}_out")
        fetch([k_out], a, f"out_{tag}")
        toks = relay_next(bi, a)
        xo = _mm(a, full[k_out], "nn", F32, f"mm_out_{tag}", scale=0.5, res=xin, layer=0, deps=toks)
        return xo, (xin, h, u, a)

    def ffn_bwd(dx_pair, saved, norm_row, which, l, tag):
        dxo, dxob = dx_pair
        k_in, k_out = (which + "_w_in", l), (which + "_w_out", l)
        xin, h, u, a = saved
        grads[k_out] = _mm(a, dxob, "tn", BF16, f"mm_dwout_{tag}", scale=0.5)
        t_out = scatter_start(f"{tag}_out", [k_out])
        du = _ffn_da(dxob, full[k_out], u, f"ffn_da_{tag}", deps=[t_out])
        grads[k_in] = _mm(h, du, "tn", BF16, f"mm_dwin_{tag}")
        t_in = scatter_start(f"{tag}_in", [k_in])
        dh = _mm(du, full[k_in], "nt", F32, f"mm_dh_{tag}", layer=0, deps=[t_in])
        toks = finish_due(dh)
        dx, dxb, dg = _rms_bwd(xin, norm_row, dh, dxo, f"rms_bwd_{tag}", deps=toks)
        gain_g.setdefault(which + "_norm", {})[l] = dg
        return dx, dxb

    def mla_fwd(xin, l, bi):
        j = l // 2
        xn = _rms_fwd(xin, mix_norm[l:l + 1], "rms_fwd_mla")
        fetch([(n, j) for n in ("mla_w_down", "mla_w_uq", "mla_w_ukv", "mla_w_o")], xn, "mla")
        lat = _mm(xn, full[("mla_w_down", j)], "nn", F32, "mm_lat", layer=0)
        cq, ckv = _lat_norm_fwd(lat, mla_g_cq[j:j + 1], mla_g_ckv[j:j + 1], "lat_norm_fwd")
        q_raw = _mm(cq, full[("mla_w_uq", j)], "nn", F32, "mm_uq", layer=0)
        kv = _mm(ckv, full[("mla_w_ukv", j)], "nn", F32, "mm_ukv", layer=0)
        qf, kf, vb = _mla_prep_fwd(q_raw, kv, lat, g_qn[j:j + 1], g_kn[j:j + 1], tabs, "mla_prep_fwd")
        o, lse = _flash_fwd(qf, kf, vb, "flash_fwd")
        toks = relay_next(bi, o)
        xo = _mm(o, full[("mla_w_o", j)], "nn", F32, "mm_mla_o", res=xin, layer=0, deps=toks)
        return xo, (xin, xn, lat, cq, ckv, q_raw, kv, qf, kf, vb, o, lse)

    def mla_bwd(dx_pair, saved, l):
        dxo, dxob = dx_pair
        j = l // 2
        xin, xn, lat, cq, ckv, q_raw, kv, qf, kf, vb, o, lse = saved
        do = _mm(dxob, full[("mla_w_o", j)], "nt", BF16, "mm_mla_do", layer=0)
        grads[("mla_w_o", j)] = _mm(o, dxob, "tn", BF16, "mm_mla_dwo")
        delta = _attn_delta(do, o, "attn_delta")
        dqf, dkf, dv = _flash_bwd(qf, kf, vb, do, lse, delta, "flash_bwd")
        dq_raw, dkv, dkpe, dgq, dgk = _mla_prep_bwd(q_raw, kv, lat, g_qn[j:j + 1], g_kn[j:j + 1], tabs, dqf, dkf, dv,
                                                    "mla_prep_bwd")
        gain_g.setdefault("mla_g_qn", {})[j] = dgq
        gain_g.setdefault("mla_g_kn", {})[j] = dgk
        dcq = _mm(dq_raw, full[("mla_w_uq", j)], "nt", F32, "mm_dcq", layer=0)
        grads[("mla_w_uq", j)] = _mm(cq, dq_raw, "tn", BF16, "mm_dwuq")
        dckv = _mm(dkv, full[("mla_w_ukv", j)], "nt", F32, "mm_dckv", layer=0)
        grads[("mla_w_ukv", j)] = _mm(ckv, dkv, "tn", BF16, "mm_dwukv")
        dlat, dgcq, dgckv = _lat_norm_bwd(lat, mla_g_cq[j:j + 1], mla_g_ckv[j:j + 1], dcq, dckv, dkpe, "lat_norm_bwd")
        gain_g.setdefault("mla_g_cq", {})[j] = dgcq
        gain_g.setdefault("mla_g_ckv", {})[j] = dgckv
        dxn = _mm(dlat, full[("mla_w_down", j)], "nt", F32, "mm_dxn_mla", layer=0)
        grads[("mla_w_down", j)] = _mm(xn, dlat, "tn", BF16, "mm_dwdown")
        tok = scatter_start(f"mix_{l}", [(n, j) for n in ("mla_w_down", "mla_w_uq", "mla_w_ukv", "mla_w_o")])
        toks = finish_due(dxn)
        dx, dxb, dg = _rms_bwd(xin, mix_norm[l:l + 1], dxn, dxo, "rms_bwd_mla", deps=[tok] + toks)
        gain_g.setdefault("mix_norm", {})[l] = dg
        return dx, dxb

    def dil_fwd(xin, l, bi):
        j = l // 2
        xn = _rms_fwd(xin, mix_norm[l:l + 1], "rms_fwd_dil")
        fetch([("dil_w_qkv", j), ("dil_w_o", j)], xn, "dil")
        qkv = _mm(xn, full[("dil_w_qkv", j)], "nn", F32, "mm_qkv", layer=0)
        o_g, lse_g = _dil_fwd(qkv, dil_g_qn[j:j + 1], dil_g_kn[j:j + 1], slopes, "dil_fwd")
        o, lse = _dil_merge(o_g, lse_g, "dil_merge")
        toks = relay_next(bi, o)
        xo = _mm(o, full[("dil_w_o", j)], "nn", F32, "mm_dil_o", res=xin, layer=0, deps=toks)
        return xo, (xin, xn, qkv, o, lse)

    def dil_bwd(dx_pair, saved, l):
        dxo, dxob = dx_pair
        j = l // 2
        xin, xn, qkv, o, lse = saved
        do = _mm(dxob, full[("dil_w_o", j)], "nt", F32, "mm_dil_do", layer=0)
        grads[("dil_w_o", j)] = _mm(o, dxob, "tn", BF16, "mm_dil_dwo")
        delta = _attn_delta(do, o, "dil_delta")
        dq, dk, dv, dgq, dgk = _dil_bwd(qkv, dil_g_qn[j:j + 1], dil_g_kn[j:j + 1], slopes, do, delta, lse, "dil_bwd")
        gain_g.setdefault("dil_g_qn", {})[j] = dgq
        gain_g.setdefault("dil_g_kn", {})[j] = dgk
        dqkv = jnp.concatenate([dq, dk, dv], axis=1)
        dxn = _mm(dqkv, full[("dil_w_qkv", j)], "nt", F32, "mm_dxn_dil", layer=0)
        grads[("dil_w_qkv", j)] = _mm(xn, dqkv, "tn", BF16, "mm_dwqkv")
        tok = scatter_start(f"mix_{l}", [("dil_w_qkv", j), ("dil_w_o", j)])
        toks = finish_due(dxn)
        dx, dxb, dg = _rms_bwd(xin, mix_norm[l:l + 1], dxn, dxo, "rms_bwd_dil", deps=[tok] + toks)
        gain_g.setdefault("mix_norm", {})[l] = dg
        return dx, dxb

    saved = []
    xc = x0
    for l in range(depth):
        xc, s1 = ffn_fwd(xc, ffn1_norm[l:l + 1], "ffn1", l, 3 * l, deps=[ag_token] if l == 0 else ())
        xc, s2 = (mla_fwd if l % 2 == 0 else dil_fwd)(xc, l, 3 * l + 1)
        xc, s3 = ffn_fwd(xc, ffn2_norm[l:l + 1], "ffn2", l, 3 * l + 2)
        saved.append((s1, s2, s3))

    dy, dyb, loss_part = _loss_head(xc, tgt, "loss_head")
    dx = (dy, dyb)
    loss = lax.psum(loss_part[0, 0], MESH_AXES)

    for bi in reversed(range(len(blocks))):
        tag, _ = blocks[bi]
        l = bi // 3
        s = saved[l][bi % 3]
        if bi % 3 == 2:
            dx = ffn_bwd(dx, s, ffn2_norm[l:l + 1], "ffn2", l, tag)
        elif bi % 3 == 1:
            dx = (mla_bwd if l % 2 == 0 else dil_bwd)(dx, s, l)
        else:
            dx = ffn_bwd(dx, s, ffn1_norm[l:l + 1], "ffn1", l, tag)
    grad_x = dx[0].reshape(x.shape)
    after = dx[1]
    while pending:
        after = scatter_finish(after)[-1]

    small = [n for n in names if n not in big]

    def gain_local(n):
        rows = [gain_g[n][l] for l in range(W[n].shape[0])]
        g = jnp.concatenate(rows, axis=1)
        return g

    def flat_pad(n, a):
        a = a.reshape(1, -1)
        if n in ("mla_g_qn", "mla_g_kn"):
            a = _pad_heads(a)
        return a

    packed_g = jnp.concatenate([gain_local(n) for n in small], axis=1)
    sizes = [gain_local(n).shape[1] for n in small]
    tot_g = _gain_allreduce(packed_g, "gain_allreduce")
    pw = jnp.concatenate([flat_pad(n, W[n]) for n in small], axis=1)
    pm = jnp.concatenate([flat_pad(n, M1[n]) for n in small], axis=1)
    pv = jnp.concatenate([flat_pad(n, V2[n]) for n in small], axis=1)
    res, _ = _adamw(tot_g.reshape(1, 1, -1), None, me, pw.reshape(1, 1, -1), pm.reshape(1, 1, -1),
                    pv.reshape(1, 1, -1), 0, None, "adamw_gains")
    res = [r.reshape(1, -1) for r in res]
    off = 0
    for n, sz in zip(small, sizes):
        for dst, r in zip((out_g, out_d, out_m, out_v), res):
            piece = r[:, off:off + sz]
            if n in ("mla_g_qn", "mla_g_kn"):
                piece = _unpad_heads(piece)
            dst[n] = piece.reshape(W[n].shape)
        off += sz

    return (loss, grad_x, *[out_g[n] for n in names], *[out_d[n] for n in names],
            *[out_m[n] for n in names], *[out_v[n] for n in names])
```

```python
import functools
import math

import jax
import jax.numpy as jnp
import numpy as np
from jax import lax
from jax.experimental import pallas as pl
from jax.experimental.pallas import tpu as pltpu

EPS = 1e-6
MLA_HEADS = 16
Q_LORA = 512
KV_LORA = 512
NOPE_DIM = 128
ROPE_DIM = 64
V_DIM = 128
QK_DIM = NOPE_DIM + ROPE_DIM
ROPE_THETA = 10000.0
HEAD_PAD = 256
LAT_PAD = Q_LORA + KV_LORA + 128
DIL_PAIRS = ((128, 1), (512, 4), (2048, 16))
DIL_GROUPS = 3
DIL_HEADS = 8
DIL_HEAD_DIM = 128
DIL_BLK = 128
FLASH_HEADS = 2
LOG2E = math.log2(math.e)
LN2 = math.log(2.0)
ADAM_LR = 0.001
ADAM_B1 = 0.9
ADAM_B2 = 0.999
ADAM_EPS = 1e-08
ADAM_WD = 0.01
ADAM_STEP = 10

N_DEV = 8
MESH_AXES = ("x", "y", "c")
MESH = pl.DeviceIdType.MESH
NEG_BIG = -1e30
VMEM_LIMIT_V7X = 56 * 1024 * 1024
LANES = 128

BF16 = jnp.bfloat16
F32 = jnp.float32


def _pick(n, cands):
    for c in cands:
        if n % c == 0:
            return c
    raise ValueError(f"no tile for {n}")


def _params(sem):
    return pltpu.CompilerParams(dimension_semantics=sem, vmem_limit_bytes=VMEM_LIMIT_V7X)


ANY_SPEC = pl.BlockSpec(memory_space=pl.ANY)


MM_VMEM_BUDGET = 44 * 1024 * 1024
MM_HBM_BYTES_PER_S = 1.8e12
MM_MXU_FLOPS_PER_S = 8.5e14
MM_STEP_S = 0.4e-6
MM_MAX_TILE_MACS = 3.3e9
MXU_DIM = 256


MM_TIMED_TILES = {
    (2048, 4096, 11264, 2, 2, 2, False): (1024, 1024, 4096, True),
    (4096, 11264, 2048, 2, 2, 4, False): (1024, 1024, 2816, True),
    (4096, 5632, 2048, 2, 2, 4, True): (512, 1024, 5632, False),
    (2048, 4096, 9216, 2, 2, 2, False): (1024, 1024, 4096, True),
}


@functools.lru_cache(maxsize=None)
def _mm_tiles(M, K, N, a_bytes, b_bytes, out_bytes, has_res):
    if (M, K, N, a_bytes, b_bytes, out_bytes, has_res) in MM_TIMED_TILES:
        return MM_TIMED_TILES[(M, K, N, a_bytes, b_bytes, out_bytes, has_res)]
    best = None
    for tk in [K] + [c for c in (1408, 1024, 512, 384, 256, 128) if K % c == 0 and c < K]:
        nk = K // tk
        for tm in [c for c in (2048, 1024, 512, 256, 128) if M % c == 0]:
            for tn in [c for c in (2816, 2048, 1408, 1152, 1024, 512, 384, 256, 128) if N % c == 0]:
                if tm * tk * tn > MM_MAX_TILE_MACS:
                    continue
                fill = (tn / (-(-tn // MXU_DIM) * MXU_DIM)) * (tk / (-(-tk // MXU_DIM) * MXU_DIM))
                fill *= tm / (tm + MXU_DIM // 2)
                vmem = 2 * (tm * tk * a_bytes + tk * tn * b_bytes) + 2 * tm * tn * out_bytes + tm * tn * 4
                vmem += (tm * tk + tk * tn) * 2 if max(a_bytes, b_bytes) > 2 else 0
                vmem += 2 * tm * tn * 4 if has_res else 0
                if vmem > MM_VMEM_BUDGET:
                    continue
                a_all, b_all = M * K * a_bytes, K * N * b_bytes
                if nk == 1:
                    t_i = a_all + (M // tm) * b_all
                    t_j = b_all + (N // tn) * a_all
                    traffic, i_outer = min((t_i, True), (t_j, False))
                else:
                    traffic, i_outer = (N // tn) * a_all + (M // tm) * b_all, True
                traffic += M * N * (out_bytes + (4 if has_res else 0))
                mxu = 2.0 * M * K * N / (MM_MXU_FLOPS_PER_S * fill) * (1.15 if nk > 1 else 1.0)
                cost = max(traffic / MM_HBM_BYTES_PER_S, mxu) + (M // tm) * (N // tn) * nk * MM_STEP_S
                if best is None or cost < best[0]:
                    best = (cost, tm, tn, tk, i_outer)
    assert best is not None, (M, K, N)
    return best[1:]


def _mm(a, b, mode, out_dtype, name, *, scale=1.0, res=None, layer=None, deps=()):
    b2 = b.shape[-2:]
    if mode == "nn":
        (M, K), (Kb, N) = a.shape, b2
    elif mode == "nt":
        (M, K), (N, Kb) = a.shape, b2
    else:
        (K, M), (Kb, N) = a.shape, b2
    assert K == Kb, (a.shape, b.shape, mode)
    tm, tn, tk, i_outer = _mm_tiles(M, K, N, a.dtype.itemsize, b.dtype.itemsize, jnp.dtype(out_dtype).itemsize,
                                    res is not None)
    nk = K // tk
    dims = {"nn": (((1,), (0,)), ((), ())), "nt": (((1,), (1,)), ((), ())), "tn": (((0,), (0,)), ((), ()))}[mode]

    def finish(v, r_ref, o_ref):
        if scale != 1.0:
            v = v * scale
        if r_ref is not None:
            v = r_ref[...] + v
        o_ref[...] = v.astype(o_ref.dtype)

    def body(*refs):
        a_ref, b_ref = refs[:2]
        r_ref = refs[2] if res is not None else None
        prod = lambda: lax.dot_general(a_ref[...].astype(BF16), b_ref[...].astype(BF16), dims,
                                       preferred_element_type=F32)
        if nk == 1:
            finish(prod(), r_ref, refs[-1])
            return
        o_ref, acc = refs[-2:]
        k = pl.program_id(2)

        @pl.when(k == 0)
        def _():
            acc[...] = prod()

        @pl.when(k > 0)
        def _():
            acc[...] += prod()

        @pl.when(k == nk - 1)
        def _():
            finish(acc[...], r_ref, o_ref)

    ij = (lambda p, q: (p, q)) if i_outer else (lambda p, q: (q, p))

    def spec(shape, f, lead=None):
        full = lambda p, q, k: f(*ij(p, q), k)
        if lead is None:
            return pl.BlockSpec(shape, full)
        return pl.BlockSpec((None,) + shape, lambda p, q, k: (lead,) + full(p, q, k))

    a_spec = spec((tk, tm), lambda i, j, k: (k, i)) if mode == "tn" else spec((tm, tk), lambda i, j, k: (i, k))
    lead = layer if b.ndim == 3 else None
    b_spec = spec((tn, tk), lambda i, j, k: (j, k), lead) if mode == "nt" else spec((tk, tn), lambda i, j, k: (k, j), lead)
    in_specs = [a_spec, b_spec]
    args = [a, b]
    if res is not None:
        in_specs.append(spec((tm, tn), lambda i, j, k: (i, j)))
        args.append(res)
    in_specs += [ANY_SPEC] * len(deps)
    args += list(deps)
    outer, inner = (M // tm, N // tn) if i_outer else (N // tn, M // tm)
    return pl.pallas_call(
        body, name=name, grid=(outer, inner, nk),
        in_specs=in_specs, out_specs=spec((tm, tn), lambda i, j, k: (i, j)),
        out_shape=jax.ShapeDtypeStruct((M, N), out_dtype),
        scratch_shapes=[pltpu.VMEM((tm, tn), F32)] if nk > 1 else [],
        compiler_params=_params(("parallel", "parallel", "arbitrary")),
    )(*args)


def _cast_into_gathered(w, layer, axis, me, name, deps=()):
    _, R, C = w.shape
    tr = _pick(R, (512, 256, 128, 64, 32, 16))
    nr = R // tr
    axis, paired = axis

    def body(me_ref, w_ref, *rest):
        o_ref = rest[-1]
        o_ref[...] = w_ref[...].astype(BF16)

    if axis == 1:
        out_idx = lambda i, me_ref: (0, _slot(me_ref[0], paired) * nr + i, 0)
        shape = (1, R * N_DEV, C)
    else:
        out_idx = lambda i, me_ref: (0, i, _slot(me_ref[0], paired))
        shape = (1, R, C * N_DEV)
    return pl.pallas_call(
        body, name=name,
        grid_spec=pltpu.PrefetchScalarGridSpec(
            num_scalar_prefetch=1, grid=(nr,),
            in_specs=[pl.BlockSpec((None, tr, C), lambda i, me_ref: (layer, i, 0))] + [ANY_SPEC] * len(deps),
            out_specs=pl.BlockSpec((None, tr, C), out_idx)),
        out_shape=jax.ShapeDtypeStruct(shape, BF16), compiler_params=_params(("parallel",)),
    )(me, w, *deps)


def _rms_fwd(x, g, name, deps=()):
    T, D = x.shape
    tr = _pick(T, (512, 256, 128))

    def body(x_ref, g_ref, *rest):
        o_ref = rest[-1]
        xv = x_ref[...]
        r = lax.rsqrt(jnp.mean(xv * xv, axis=-1, keepdims=True) + EPS)
        o_ref[...] = ((xv * r) * g_ref[...]).astype(BF16)

    return pl.pallas_call(
        body, name=name, grid=(T // tr,),
        in_specs=[pl.BlockSpec((tr, D), lambda i: (i, 0)), pl.BlockSpec((1, D), lambda i: (0, 0))]
        + [ANY_SPEC] * len(deps),
        out_specs=pl.BlockSpec((tr, D), lambda i: (i, 0)),
        out_shape=jax.ShapeDtypeStruct((T, D), BF16), compiler_params=_params(("parallel",)),
    )(x, g, *deps)


def _rms_bwd(x, g, dh, dres, name, deps=()):
    T, D = x.shape
    tr = _pick(T, (256, 128))

    def body(x_ref, g_ref, dh_ref, dres_ref, *rest):
        dx_ref, dxb_ref, dg_ref = rest[-3:]
        xv = x_ref[...]
        dhv = dh_ref[...]
        r = lax.rsqrt(jnp.mean(xv * xv, axis=-1, keepdims=True) + EPS)
        xhat = xv * r
        dxh = dhv * g_ref[...]
        c = jnp.mean(dxh * xhat, axis=-1, keepdims=True)
        dx = dres_ref[...] + r * (dxh - xhat * c)
        dx_ref[...] = dx
        dxb_ref[...] = dx.astype(BF16)

        @pl.when(pl.program_id(0) == 0)
        def _():
            dg_ref[...] = jnp.zeros_like(dg_ref)

        dg_ref[...] += jnp.sum(dhv * xhat, axis=0, keepdims=True)

    row = pl.BlockSpec((tr, D), lambda i: (i, 0))
    vec = pl.BlockSpec((1, D), lambda i: (0, 0))
    return pl.pallas_call(
        body, name=name, grid=(T // tr,),
        in_specs=[row, vec, row, row] + [ANY_SPEC] * len(deps), out_specs=[row, row, vec],
        out_shape=[jax.ShapeDtypeStruct((T, D), F32), jax.ShapeDtypeStruct((T, D), BF16),
                   jax.ShapeDtypeStruct((1, D), F32)],
        compiler_params=_params(("arbitrary",)),
    )(x, g, dh, dres, *deps)


N_PANEL = N_DEV // 2


def _ffn_in(h, w_in, name):
    T, D = h.shape
    F2 = w_in.shape[2]
    pw = F2 // N_PANEL
    half = pw // 2
    tm = _pick(T, (512, 256, 128))

    def body(h_ref, w_ref, u_ref, a_ref):
        r = jnp.dot(h_ref[...], w_ref[...], preferred_element_type=F32)
        u_ref[...] = r.astype(BF16)
        g, up = r[:, :half], r[:, half:]
        a_ref[...] = (g * jax.nn.sigmoid(g) * up).astype(BF16)

    return pl.pallas_call(
        body, name=name, grid=(N_PANEL, T // tm),
        in_specs=[pl.BlockSpec((tm, D), lambda p, i: (i, 0)), pl.BlockSpec((None, D, pw), lambda p, i: (0, 0, p))],
        out_specs=[pl.BlockSpec((tm, pw), lambda p, i: (i, p)), pl.BlockSpec((tm, half), lambda p, i: (i, p))],
        out_shape=[jax.ShapeDtypeStruct((T, F2), BF16), jax.ShapeDtypeStruct((T, F2 // 2), BF16)],
        compiler_params=_params(("parallel", "parallel")),
    )(h, w_in)


def _ffn_da(dxo, w_out, u, name, deps=()):
    T, D = dxo.shape
    F2 = u.shape[1]
    pw = F2 // N_PANEL
    half = pw // 2
    tm = _pick(T, (512, 256, 128))

    def body(d_ref, w_ref, u_ref, *rest):
        du_ref = rest[-1]
        da = 0.5 * lax.dot_general(d_ref[...], w_ref[...], NT_DIMS, preferred_element_type=F32)
        g = u_ref[:, :half].astype(F32)
        up = u_ref[:, half:].astype(F32)
        sg = jax.nn.sigmoid(g)
        silu = g * sg
        du_ref[:, :half] = (da * up * (sg + silu * (1.0 - sg))).astype(BF16)
        du_ref[:, half:] = (da * silu).astype(BF16)

    return pl.pallas_call(
        body, name=name, grid=(N_PANEL, T // tm),
        in_specs=[pl.BlockSpec((tm, D), lambda p, i: (i, 0)), pl.BlockSpec((None, half, D), lambda p, i: (0, p, 0)),
                  pl.BlockSpec((tm, pw), lambda p, i: (i, p))] + [ANY_SPEC] * len(deps),
        out_specs=pl.BlockSpec((tm, pw), lambda p, i: (i, p)),
        out_shape=jax.ShapeDtypeStruct((T, F2), BF16), compiler_params=_params(("parallel", "parallel")),
    )(dxo, w_out, u, *deps)


def _loss_head(y, t, name):
    T, D = y.shape
    tr = _pick(T, (512, 256, 128))

    def body(y_ref, t_ref, dy_ref, dyb_ref, l_ref):
        e = y_ref[...] - t_ref[...]
        dy = e * (1.0 / D)
        dy_ref[...] = dy
        dyb_ref[...] = dy.astype(BF16)

        @pl.when(pl.program_id(0) == 0)
        def _():
            l_ref[...] = jnp.zeros_like(l_ref)

        l_ref[...] += 0.5 * jnp.sum(jnp.mean(e * e, axis=-1, keepdims=True), axis=0, keepdims=True)

    row = pl.BlockSpec((tr, D), lambda i: (i, 0))
    return pl.pallas_call(
        body, name=name, grid=(T // tr,),
        in_specs=[row, row], out_specs=[row, row, pl.BlockSpec((1, 1), lambda i: (0, 0))],
        out_shape=[jax.ShapeDtypeStruct((T, D), F32), jax.ShapeDtypeStruct((T, D), BF16),
                   jax.ShapeDtypeStruct((1, 1), F32)],
        compiler_params=_params(("arbitrary",)),
    )(y, t)


def _rope_tables(S):
    half = ROPE_DIM // 2
    inv = 1.0 / (ROPE_THETA ** (jnp.arange(0, ROPE_DIM, 2, dtype=F32) / ROPE_DIM))
    ang = jnp.arange(S, dtype=F32)[:, None] * inv[None, :]
    cos, sin = jnp.cos(ang), jnp.sin(ang)
    z = jnp.zeros((S, half), F32)
    z2 = jnp.zeros((S, LANES - ROPE_DIM), F32)
    c = jnp.concatenate([cos, cos, z2], axis=1)
    s1 = jnp.concatenate([-sin, z, z2], axis=1)
    s2 = jnp.concatenate([z, sin, z2], axis=1)
    return c, s1, s2


def _rope(r, c, s1, s2):
    return r * c + pltpu.roll(r, LANES - ROPE_DIM // 2, 1) * s1 + pltpu.roll(r, ROPE_DIM // 2, 1) * s2


def _rope_t(d, c, s1, s2):
    return d * c + pltpu.roll(d * s1, ROPE_DIM // 2, 1) + pltpu.roll(d * s2, LANES - ROPE_DIM // 2, 1)


def _lat_norm_fwd(lat, g_cq, g_ckv, name):
    T = lat.shape[0]
    tr = _pick(T, (512, 256, 128))

    def body(lat_ref, gq_ref, gk_ref, cq_ref, ckv_ref):
        for off, g_ref, o_ref in ((0, gq_ref, cq_ref), (Q_LORA, gk_ref, ckv_ref)):
            xv = lat_ref[:, off:off + Q_LORA]
            r = lax.rsqrt(jnp.mean(xv * xv, axis=-1, keepdims=True) + EPS)
            o_ref[...] = ((xv * r) * g_ref[...]).astype(BF16)

    vec = pl.BlockSpec((1, Q_LORA), lambda i: (0, 0))
    out = pl.BlockSpec((tr, Q_LORA), lambda i: (i, 0))
    return pl.pallas_call(
        body, name=name, grid=(T // tr,),
        in_specs=[pl.BlockSpec((tr, LAT_PAD), lambda i: (i, 0)), vec, vec], out_specs=[out, out],
        out_shape=[jax.ShapeDtypeStruct((T, Q_LORA), BF16)] * 2, compiler_params=_params(("parallel",)),
    )(lat, g_cq, g_ckv)


def _lat_norm_bwd(lat, g_cq, g_ckv, dcq, dckv, dkpe, name):
    T = lat.shape[0]
    tr = _pick(T, (256, 128))

    def body(lat_ref, gq_ref, gk_ref, dcq_ref, dckv_ref, dkpe_ref, dlat_ref, dgq_ref, dgk_ref):
        @pl.when(pl.program_id(0) == 0)
        def _():
            dgq_ref[...] = jnp.zeros_like(dgq_ref)
            dgk_ref[...] = jnp.zeros_like(dgk_ref)

        for off, g_ref, d_ref, dg_ref in ((0, gq_ref, dcq_ref, dgq_ref), (Q_LORA, gk_ref, dckv_ref, dgk_ref)):
            xv = lat_ref[:, off:off + Q_LORA]
            dv = d_ref[...]
            r = lax.rsqrt(jnp.mean(xv * xv, axis=-1, keepdims=True) + EPS)
            xhat = xv * r
            dxh = dv * g_ref[...]
            c = jnp.mean(dxh * xhat, axis=-1, keepdims=True)
            dlat_ref[:, off:off + Q_LORA] = (r * (dxh - xhat * c)).astype(BF16)
            dg_ref[...] += jnp.sum(dv * xhat, axis=0, keepdims=True)
        dlat_ref[:, Q_LORA + KV_LORA:] = dkpe_ref[...].astype(BF16)

    vec = pl.BlockSpec((1, Q_LORA), lambda i: (0, 0))
    half = pl.BlockSpec((tr, Q_LORA), lambda i: (i, 0))
    full = pl.BlockSpec((tr, LAT_PAD), lambda i: (i, 0))
    return pl.pallas_call(
        body, name=name, grid=(T // tr,),
        in_specs=[full, vec, vec, half, half, pl.BlockSpec((tr, LANES), lambda i: (i, 0))],
        out_specs=[full, vec, vec],
        out_shape=[jax.ShapeDtypeStruct((T, LAT_PAD), BF16), jax.ShapeDtypeStruct((1, Q_LORA), F32),
                   jax.ShapeDtypeStruct((1, Q_LORA), F32)],
        compiler_params=_params(("arbitrary",)),
    )(lat, g_cq, g_ckv, dcq, dckv, dkpe)


def _mla_prep_fwd(q_raw, kv, lat, g_qn, g_kn, tabs, name):
    T = q_raw.shape[0]
    H = MLA_HEADS
    tr = _pick(T, (256, 128))
    scale = LOG2E / math.sqrt(QK_DIM)

    def body(q_ref, kv_ref, kpe_ref, gq_ref, gk_ref, c_ref, s1_ref, s2_ref, qf_ref, kf_ref, v_ref):
        c, s1, s2 = c_ref[...], s1_ref[...], s2_ref[...]
        gq, gk = gq_ref[...], gk_ref[...]
        kpe = kpe_ref[...]
        kpe_ss = jnp.sum(kpe * kpe, axis=-1, keepdims=True)
        for h in range(H):
            lo = h * HEAD_PAD
            qa = q_ref[:, lo:lo + LANES]
            qb = q_ref[:, lo + LANES:lo + HEAD_PAD]
            ss = jnp.sum(qa * qa, axis=-1, keepdims=True) + jnp.sum(qb * qb, axis=-1, keepdims=True)
            r = lax.rsqrt(ss * (1.0 / QK_DIM) + EPS)
            qf_ref[:, lo:lo + LANES] = (qa * r * gq[:, :LANES] * scale).astype(BF16)
            qf_ref[:, lo + LANES:lo + HEAD_PAD] = (_rope(qb * r * gq[:, LANES:], c, s1, s2) * scale).astype(BF16)
            ka = kv_ref[:, lo:lo + LANES]
            ss = jnp.sum(ka * ka, axis=-1, keepdims=True) + kpe_ss
            r = lax.rsqrt(ss * (1.0 / QK_DIM) + EPS)
            kf_ref[:, lo:lo + LANES] = (ka * r * gk[:, :LANES]).astype(BF16)
            kf_ref[:, lo + LANES:lo + HEAD_PAD] = _rope(kpe * r * gk[:, LANES:], c, s1, s2).astype(BF16)
            v_ref[:, lo:lo + V_DIM] = kv_ref[:, lo + LANES:lo + HEAD_PAD].astype(BF16)
            v_ref[:, lo + V_DIM:lo + HEAD_PAD] = jnp.ones((tr, HEAD_PAD - V_DIM), BF16)

    wide = pl.BlockSpec((tr, H * HEAD_PAD), lambda i: (i, 0))
    lane = pl.BlockSpec((tr, LANES), lambda i: (i, 0))
    gvec = pl.BlockSpec((1, HEAD_PAD), lambda i: (0, 0))
    return pl.pallas_call(
        body, name=name, grid=(T // tr,),
        in_specs=[wide, wide, pl.BlockSpec((tr, LANES), lambda i: (i, (Q_LORA + KV_LORA) // LANES)), gvec, gvec,
                  lane, lane, lane],
        out_specs=[wide, wide, wide],
        out_shape=[jax.ShapeDtypeStruct((T, H * HEAD_PAD), BF16)] * 3,
        compiler_params=_params(("parallel",)),
    )(q_raw, kv, lat, g_qn, g_kn, *tabs)


def _mla_prep_bwd(q_raw, kv, lat, g_qn, g_kn, tabs, dqf, dkf, dv, name):
    T = q_raw.shape[0]
    H = MLA_HEADS
    tr = _pick(T, (128,))

    def body(q_ref, kv_ref, kpe_ref, gq_ref, gk_ref, c_ref, s1_ref, s2_ref, dqf_ref, dkf_ref, dv_ref,
             dq_ref, dkv_ref, dkpe_ref, dgq_ref, dgk_ref):
        @pl.when(pl.program_id(0) == 0)
        def _():
            dgq_ref[...] = jnp.zeros_like(dgq_ref)
            dgk_ref[...] = jnp.zeros_like(dgk_ref)

        c, s1, s2 = c_ref[...], s1_ref[...], s2_ref[...]
        gq, gk = gq_ref[...], gk_ref[...]
        kpe = kpe_ref[...]
        kpe_ss = jnp.sum(kpe * kpe, axis=-1, keepdims=True)
        dkpe = jnp.zeros_like(kpe)
        dgq_a = jnp.zeros((1, LANES), F32)
        dgq_b = jnp.zeros((1, LANES), F32)
        dgk_a = jnp.zeros((1, LANES), F32)
        dgk_b = jnp.zeros((1, LANES), F32)
        for h in range(H):
            lo = h * HEAD_PAD
            xa = q_ref[:, lo:lo + LANES]
            xb = q_ref[:, lo + LANES:lo + HEAD_PAD]
            ss = jnp.sum(xa * xa, axis=-1, keepdims=True) + jnp.sum(xb * xb, axis=-1, keepdims=True)
            r = lax.rsqrt(ss * (1.0 / QK_DIM) + EPS)
            xa, xb = xa * r, xb * r
            da = dqf_ref[:, lo:lo + LANES]
            db = _rope_t(dqf_ref[:, lo + LANES:lo + HEAD_PAD], c, s1, s2)
            dgq_a += jnp.sum(da * xa, axis=0, keepdims=True)
            dgq_b += jnp.sum(db * xb, axis=0, keepdims=True)
            da, db = da * gq[:, :LANES], db * gq[:, LANES:]
            cc = (jnp.sum(da * xa, axis=-1, keepdims=True) + jnp.sum(db * xb, axis=-1, keepdims=True)) * (1.0 / QK_DIM)
            dq_ref[:, lo:lo + LANES] = (r * (da - xa * cc)).astype(BF16)
            dq_ref[:, lo + LANES:lo + HEAD_PAD] = (r * (db - xb * cc)).astype(BF16)
            xa = kv_ref[:, lo:lo + LANES]
            ss = jnp.sum(xa * xa, axis=-1, keepdims=True) + kpe_ss
            r = lax.rsqrt(ss * (1.0 / QK_DIM) + EPS)
            xa, xb = xa * r, kpe * r
            da = dkf_ref[:, lo:lo + LANES]
            db = _rope_t(dkf_ref[:, lo + LANES:lo + HEAD_PAD], c, s1, s2)
            dgk_a += jnp.sum(da * xa, axis=0, keepdims=True)
            dgk_b += jnp.sum(db * xb, axis=0, keepdims=True)
            da, db = da * gk[:, :LANES], db * gk[:, LANES:]
            cc = (jnp.sum(da * xa, axis=-1, keepdims=True) + jnp.sum(db * xb, axis=-1, keepdims=True)) * (1.0 / QK_DIM)
            dkv_ref[:, lo:lo + LANES] = (r * (da - xa * cc)).astype(BF16)
            dkpe = dkpe + r * (db - xb * cc)
            dkv_ref[:, lo + LANES:lo + HEAD_PAD] = dv_ref[:, h * V_DIM:(h + 1) * V_DIM].astype(BF16)
        dkpe_ref[...] = dkpe
        dgq_ref[:, :LANES] += dgq_a
        dgq_ref[:, LANES:] += dgq_b
        dgk_ref[:, :LANES] += dgk_a
        dgk_ref[:, LANES:] += dgk_b

    wide = pl.BlockSpec((tr, H * HEAD_PAD), lambda i: (i, 0))
    lane = pl.BlockSpec((tr, LANES), lambda i: (i, 0))
    gvec = pl.BlockSpec((1, HEAD_PAD), lambda i: (0, 0))
    vspec = pl.BlockSpec((tr, H * V_DIM), lambda i: (i, 0))
    return pl.pallas_call(
        body, name=name, grid=(T // tr,),
        in_specs=[wide, wide, pl.BlockSpec((tr, LANES), lambda i: (i, (Q_LORA + KV_LORA) // LANES)), gvec, gvec,
                  lane, lane, lane, wide, wide, vspec],
        out_specs=[wide, wide, lane, gvec, gvec],
        out_shape=[jax.ShapeDtypeStruct((T, H * HEAD_PAD), BF16), jax.ShapeDtypeStruct((T, H * HEAD_PAD), BF16),
                   jax.ShapeDtypeStruct((T, LANES), F32), jax.ShapeDtypeStruct((1, HEAD_PAD), F32),
                   jax.ShapeDtypeStruct((1, HEAD_PAD), F32)],
        compiler_params=_params(("arbitrary",)),
    )(q_raw, kv, lat, g_qn, g_kn, *tabs, dqf, dkf, dv)


def _causal_mask(tq, tk):
    return lax.broadcasted_iota(jnp.int32, (tq, tk), 1) <= lax.broadcasted_iota(jnp.int32, (tq, tk), 0)


NT_DIMS = (((1,), (1,)), ((), ()))
TN_DIMS = (((0,), (0,)), ((), ()))


def _flash_fwd(qf, kf, v, name):
    T = qf.shape[0]
    H, G = MLA_HEADS, FLASH_HEADS
    t = _pick(T, (512, 256, 128))
    n = T // t
    pairs = [(i, j) for i in range(n) for j in range(i + 1)]
    qi = jnp.asarray([p[0] for p in pairs], jnp.int32)
    kj = jnp.asarray([p[1] for p in pairs], jnp.int32)

    def body(qi_ref, kj_ref, q_ref, k_ref, v_ref, o_ref, lse_ref, *scratch):
        m_sc, acc_sc = scratch[:G], scratch[G:]
        sid = pl.program_id(1)
        i, j = qi_ref[sid], kj_ref[sid]

        @pl.when(j == 0)
        def _():
            for g in range(G):
                m_sc[g][...] = jnp.full_like(m_sc[g], NEG_BIG)
                acc_sc[g][...] = jnp.zeros_like(acc_sc[g])

        def step(masked):
            for g in range(G):
                qk = slice(g * HEAD_PAD, (g + 1) * HEAD_PAD)
                s = lax.dot_general(q_ref[:, qk], k_ref[:, qk], NT_DIMS, preferred_element_type=F32)
                if masked:
                    s = jnp.where(_causal_mask(t, t), s, NEG_BIG)
                m_prev = m_sc[g][:, :1]
                m_new = jnp.maximum(m_prev, jnp.max(s, axis=-1, keepdims=True))
                a = jnp.exp2(m_prev - m_new)
                p = jnp.exp2((s - m_new).astype(BF16))
                acc_sc[g][...] = a * acc_sc[g][...] + jnp.dot(p, v_ref[:, qk], preferred_element_type=F32)
                m_sc[g][...] = jnp.broadcast_to(m_new, (t, LANES))

        @pl.when(j < i)
        def _():
            step(False)

        @pl.when(j == i)
        def _():
            step(True)
            for g in range(G):
                vo = slice(g * V_DIM, (g + 1) * V_DIM)
                l = acc_sc[g][:, V_DIM:]
                o_ref[:, vo] = (acc_sc[g][:, :V_DIM] / l).astype(BF16)
                lse_ref[:, vo] = m_sc[g][...] + jnp.log2(l)

    row = pl.BlockSpec((t, G * V_DIM), lambda h, s, qi, kj: (qi[s], h))
    return pl.pallas_call(
        body, name=name,
        grid_spec=pltpu.PrefetchScalarGridSpec(
            num_scalar_prefetch=2, grid=(H // G, len(pairs)),
            in_specs=[pl.BlockSpec((t, G * HEAD_PAD), lambda h, s, qi, kj: (qi[s], h)),
                      pl.BlockSpec((t, G * HEAD_PAD), lambda h, s, qi, kj: (kj[s], h)),
                      pl.BlockSpec((t, G * HEAD_PAD), lambda h, s, qi, kj: (kj[s], h))],
            out_specs=[row, row],
            scratch_shapes=[pltpu.VMEM((t, LANES), F32)] * G + [pltpu.VMEM((t, HEAD_PAD), F32)] * G),
        out_shape=[jax.ShapeDtypeStruct((T, H * V_DIM), BF16), jax.ShapeDtypeStruct((T, H * V_DIM), F32)],
        compiler_params=_params(("parallel", "arbitrary")),
    )(qi, kj, qf, kf, v)


def _attn_delta(do, o, name):
    T, W = do.shape
    nh = W // V_DIM
    tr = _pick(T, (512, 256, 128))

    def body(do_ref, o_ref, d_ref):
        for h in range(nh):
            sl = slice(h * V_DIM, (h + 1) * V_DIM)
            d = jnp.sum(do_ref[:, sl].astype(F32) * o_ref[:, sl].astype(F32), axis=-1, keepdims=True)
            d_ref[:, sl] = jnp.broadcast_to(d, (tr, V_DIM))

    row = pl.BlockSpec((tr, W), lambda i: (i, 0))
    return pl.pallas_call(
        body, name=name, grid=(T // tr,), in_specs=[row, row], out_specs=row,
        out_shape=jax.ShapeDtypeStruct((T, W), F32), compiler_params=_params(("parallel",)),
    )(do, o)


def _flash_bwd(qf, kf, v, do, lse, delta, name):
    T = qf.shape[0]
    H, G = MLA_HEADS, FLASH_HEADS
    t = _pick(T, (512, 256, 128))
    n = T // t
    scale = 1.0 / math.sqrt(QK_DIM)
    pairs = [(i, j) for j in range(n) for i in range(j, n)]
    qi = jnp.asarray([p[0] for p in pairs], jnp.int32)
    kj = jnp.asarray([p[1] for p in pairs], jnp.int32)

    def body(qi_ref, kj_ref, q_ref, k_ref, v_ref, do_ref, lse_ref, dl_ref, dq_ref, dk_ref, dv_ref, dk_acc, dv_acc):
        sid = pl.program_id(1)
        i, j = qi_ref[sid], kj_ref[sid]

        @pl.when(sid == 0)
        def _():
            dq_ref[...] = jnp.zeros_like(dq_ref)

        def step(masked):
            rows = pl.ds(pl.multiple_of(i * t, t), t)
            for g in range(G):
                qk = slice(g * HEAD_PAD, (g + 1) * HEAD_PAD)
                vo = slice(g * V_DIM, (g + 1) * V_DIM)
                q, k, do_ = q_ref[:, qk], k_ref[:, qk], do_ref[:, vo]
                v_ = v_ref[:, g * HEAD_PAD:g * HEAD_PAD + V_DIM]
                s = lax.dot_general(q, k, NT_DIMS, preferred_element_type=F32)
                if masked:
                    s = jnp.where(_causal_mask(t, t), s, NEG_BIG)
                p = jnp.exp2(s - lse_ref[:, g * V_DIM:g * V_DIM + 1])
                dp = lax.dot_general(do_, v_, NT_DIMS, preferred_element_type=F32)
                ds = (p * (dp - dl_ref[:, g * V_DIM:g * V_DIM + 1])).astype(BF16)
                dv = lax.dot_general(p.astype(BF16), do_, TN_DIMS, preferred_element_type=F32)
                dk = lax.dot_general(ds, q, TN_DIMS, preferred_element_type=F32)
                if masked:
                    dv_acc[:, vo] = dv
                    dk_acc[:, qk] = dk
                else:
                    dv_acc[:, vo] += dv
                    dk_acc[:, qk] += dk
                dq_ref[rows, qk] += jnp.dot(ds, k, preferred_element_type=F32) * scale

        @pl.when(i == j)
        def _():
            step(True)

        @pl.when(i > j)
        def _():
            step(False)

        @pl.when(i == n - 1)
        def _():
            dk_ref[...] = dk_acc[...] * LN2
            dv_ref[...] = dv_acc[...]

    qs = pl.BlockSpec((t, G * HEAD_PAD), lambda h, s, qi, kj: (qi[s], h))
    rs = pl.BlockSpec((t, G * V_DIM), lambda h, s, qi, kj: (qi[s], h))
    ks = pl.BlockSpec((t, G * HEAD_PAD), lambda h, s, qi, kj: (kj[s], h))
    vs = pl.BlockSpec((t, G * V_DIM), lambda h, s, qi, kj: (kj[s], h))
    return pl.pallas_call(
        body, name=name,
        grid_spec=pltpu.PrefetchScalarGridSpec(
            num_scalar_prefetch=2, grid=(H // G, len(pairs)), in_specs=[qs, ks, ks, rs, rs, rs],
            out_specs=[pl.BlockSpec((T, G * HEAD_PAD), lambda h, s, qi, kj: (0, h)), ks, vs],
            scratch_shapes=[pltpu.VMEM((t, G * HEAD_PAD), F32), pltpu.VMEM((t, G * V_DIM), F32)]),
        out_shape=[jax.ShapeDtypeStruct((T, H * HEAD_PAD), F32), jax.ShapeDtypeStruct((T, H * HEAD_PAD), F32),
                   jax.ShapeDtypeStruct((T, H * V_DIM), F32)],
        compiler_params=_params(("parallel", "arbitrary")),
    )(qi, kj, qf, kf, v, do, lse, delta)


def _alibi_slopes():
    tot = DIL_GROUPS * DIL_HEADS
    return [float(np.float32(2.0) ** (np.float32(-8.0) * np.float32(k) / np.float32(tot))) for k in range(1, tot + 1)]


def _dil_masks():
    iq = lax.broadcasted_iota(jnp.int32, (DIL_BLK, DIL_BLK), 0)
    ik = lax.broadcasted_iota(jnp.int32, (DIL_BLK, DIL_BLK), 1)
    return (ik >= iq), (iq + DIL_BLK - ik).astype(F32), (ik <= iq), (iq - ik).astype(F32)


def _dil_norm(x, g):
    r = lax.rsqrt(jnp.mean(x * x, axis=-1, keepdims=True) + EPS)
    return x * r, r


DIL_SUPER = 8
BNT_DIMS = (((2,), (2,)), ((0,), (0,)))
BNN_DIMS = (((2,), (1,)), ((0,), (0,)))
BTN_DIMS = (((1,), (1,)), ((0,), (0,)))


def _dil_chunk(it, nb, d):
    assert nb & (nb - 1) == 0, nb
    r, n = it >> (nb.bit_length() - 1), it & (nb - 1)
    if d > 1:
        tok = pl.ds(n * (d * DIL_BLK) + r, DIL_BLK, stride=d)
    else:
        tok = pl.ds(pl.multiple_of(it * DIL_BLK, DIL_BLK), DIL_BLK)
    return tok, pl.ds(pl.multiple_of((it + 1) * DIL_BLK, DIL_BLK), DIL_BLK)


def _dil_token_rows(bidx, nb, d):
    r, n = divmod(bidx, nb)
    return pl.ds(n * DIL_BLK * d + r, DIL_BLK, stride=d) if d > 1 else pl.ds(bidx * DIL_BLK, DIL_BLK)


def _dil_super_rows(ss):
    base = (1 + ss * DIL_SUPER) * DIL_BLK
    return pl.ds(base, DIL_SUPER * DIL_BLK), pl.ds(base - DIL_BLK, DIL_SUPER * DIL_BLK)


def _dil_b3(x):
    return x.reshape(DIL_SUPER, DIL_BLK, x.shape[-1])


def _dil_scores(q3, kc3, kp3, slope, d, ss, nb):
    ok_p, dist_p, ok_c, dist_c = _dil_masks()
    scale = 1.0 / math.sqrt(DIL_HEAD_DIM)
    bias_p = jnp.where(ok_p, -slope * d * dist_p, NEG_BIG)
    bias_c = jnp.where(ok_c, -slope * d * dist_c, NEG_BIG)
    s_c = lax.dot_general(q3, kc3, BNT_DIMS, preferred_element_type=F32) * scale + bias_c[None]
    s_p = lax.dot_general(q3, kp3, BNT_DIMS, preferred_element_type=F32) * scale + bias_p[None]
    bidx = ss * DIL_SUPER + lax.broadcasted_iota(jnp.int32, s_p.shape, 0)
    s_p = jnp.where((bidx & (nb - 1)) == 0, NEG_BIG, s_p)
    return s_c, s_p


def _dil_fwd(qkv, g_qn, g_kn, slopes, name):
    T = qkv.shape[0]
    GH = DIL_GROUPS * DIL_HEADS
    scale = 1.0 / math.sqrt(DIL_HEAD_DIM)

    def body(sl_ref, q_ref, k_ref, v_ref, gq_ref, gk_ref, o_ref, lse_ref, qn_pm, kn_pm, v_pm):
        gh = pl.program_id(0)
        slope = sl_ref[gh]
        gq, gk = gq_ref[...], gk_ref[...]
        pad = pl.ds(0, DIL_BLK)
        kn_pm[pad, :] = jnp.zeros((DIL_BLK, DIL_HEAD_DIM), BF16)
        v_pm[pad, :] = jnp.zeros((DIL_BLK, DIL_HEAD_DIM), BF16)
        for g, (_, d) in enumerate(DIL_PAIRS):
            @pl.when((gh >= g * DIL_HEADS) & (gh < (g + 1) * DIL_HEADS))
            def _(d=d):
                nb = T // (d * DIL_BLK)

                def fill(it, _):
                    tok, dst = _dil_chunk(it, nb, d)
                    qn_pm[dst, :] = (_dil_norm(q_ref[tok, :], gq)[0] * gq).astype(BF16)
                    kn_pm[dst, :] = (_dil_norm(k_ref[tok, :], gk)[0] * gk).astype(BF16)
                    v_pm[dst, :] = v_ref[tok, :].astype(BF16)
                    return 0
                lax.fori_loop(0, T // DIL_BLK, fill, 0, unroll=4)

                for ss in range(T // DIL_BLK // DIL_SUPER):
                    cur, prv = _dil_super_rows(ss)
                    q3, kc3, kp3 = _dil_b3(qn_pm[cur, :]), _dil_b3(kn_pm[cur, :]), _dil_b3(kn_pm[prv, :])
                    s_c, s_p = _dil_scores(q3, kc3, kp3, slope, d, ss, nb)
                    m = jnp.max(jnp.maximum(s_c, s_p), axis=-1, keepdims=True)
                    p_c = jnp.exp(s_c - m)
                    p_p = jnp.exp(s_p - m)
                    l = jnp.sum(p_c, axis=-1, keepdims=True) + jnp.sum(p_p, axis=-1, keepdims=True)
                    acc = lax.dot_general(p_c.astype(BF16), _dil_b3(v_pm[cur, :]), BNN_DIMS, preferred_element_type=F32)
                    acc += lax.dot_general(p_p.astype(BF16), _dil_b3(v_pm[prv, :]), BNN_DIMS, preferred_element_type=F32)
                    o3 = acc / l
                    lse3 = jnp.broadcast_to(m + jnp.log(l), o3.shape)
                    for b in range(DIL_SUPER):
                        tok = _dil_token_rows(ss * DIL_SUPER + b, nb, d)
                        o_ref[tok, :] = o3[b]
                        lse_ref[tok, :] = lse3[b]

    col = lambda off: pl.BlockSpec((T, DIL_HEAD_DIM), lambda gh, sl: (0, gh + off))
    gvec = pl.BlockSpec((1, DIL_HEAD_DIM), lambda gh, sl: (0, 0))
    return pl.pallas_call(
        body, name=name,
        grid_spec=pltpu.PrefetchScalarGridSpec(
            num_scalar_prefetch=1, grid=(GH,),
            in_specs=[col(0), col(GH), col(2 * GH), gvec, gvec], out_specs=[col(0), col(0)],
            scratch_shapes=[pltpu.VMEM((DIL_BLK + T, DIL_HEAD_DIM), BF16)] * 3),
        out_shape=[jax.ShapeDtypeStruct((T, GH * DIL_HEAD_DIM), F32)] * 2,
        compiler_params=_params(("parallel",)),
    )(slopes, qkv, qkv, qkv, g_qn, g_kn)


def _dil_merge(o_g, lse_g, name):
    T = o_g.shape[0]
    W = DIL_HEADS * DIL_HEAD_DIM
    tr = _pick(T, (256, 128))

    def body(o0, o1, o2, l0, l1, l2, o_ref, lse_ref):
        a, b, c = l0[...], l1[...], l2[...]
        m = jnp.maximum(jnp.maximum(a, b), c)
        ea, eb, ec = jnp.exp(a - m), jnp.exp(b - m), jnp.exp(c - m)
        tot = ea + eb + ec
        o_ref[...] = ((o0[...] * ea + o1[...] * eb + o2[...] * ec) / tot).astype(BF16)
        lse_ref[...] = m + jnp.log(tot)

    grp = lambda g: pl.BlockSpec((tr, W), lambda i: (i, g))
    out = pl.BlockSpec((tr, W), lambda i: (i, 0))
    return pl.pallas_call(
        body, name=name, grid=(T // tr,),
        in_specs=[grp(0), grp(1), grp(2), grp(0), grp(1), grp(2)], out_specs=[out, out],
        out_shape=[jax.ShapeDtypeStruct((T, W), BF16), jax.ShapeDtypeStruct((T, W), F32)],
        compiler_params=_params(("parallel",)),
    )(o_g, o_g, o_g, lse_g, lse_g, lse_g)


def _dil_bwd(qkv, g_qn, g_kn, slopes, do, delta, lse, name):
    T = qkv.shape[0]
    GH = DIL_GROUPS * DIL_HEADS
    scale = 1.0 / math.sqrt(DIL_HEAD_DIM)
    nchunk = T // DIL_BLK

    def body(sl_ref, q_ref, k_ref, v_ref, gq_ref, gk_ref, do_ref, dl_ref, lse_ref,
             dq_ref, dk_ref, dv_ref, dgq_ref, dgk_ref,
             qn_pm, kn_pm, v_pm, do_pm, lse_pm, dl_pm, dq_pm, dk_pm, dv_pm, tok_sc):
        gh = pl.program_id(0)
        slope = sl_ref[gh]
        gq, gk = gq_ref[...], gk_ref[...]

        @pl.when(gh == 0)
        def _():
            dgq_ref[...] = jnp.zeros_like(dgq_ref)
            dgk_ref[...] = jnp.zeros_like(dgk_ref)

        pad = pl.ds(0, DIL_BLK)
        kn_pm[pad, :] = jnp.zeros((DIL_BLK, DIL_HEAD_DIM), BF16)
        v_pm[pad, :] = jnp.zeros((DIL_BLK, DIL_HEAD_DIM), BF16)
        dk_pm[...] = jnp.zeros_like(dk_pm)
        dv_pm[...] = jnp.zeros_like(dv_pm)
        for g, (_, d) in enumerate(DIL_PAIRS):
            @pl.when((gh >= g * DIL_HEADS) & (gh < (g + 1) * DIL_HEADS))
            def _(d=d):
                nb = T // (d * DIL_BLK)

                def fill(it, _):
                    tok, dst = _dil_chunk(it, nb, d)
                    qn_pm[dst, :] = (_dil_norm(q_ref[tok, :], gq)[0] * gq).astype(BF16)
                    kn_pm[dst, :] = (_dil_norm(k_ref[tok, :], gk)[0] * gk).astype(BF16)
                    v_pm[dst, :] = v_ref[tok, :].astype(BF16)
                    do_pm[dst, :] = do_ref[tok, :].astype(BF16)
                    lse_pm[dst, :] = lse_ref[tok, :]
                    dl_pm[dst, :] = dl_ref[tok, :]
                    return 0
                lax.fori_loop(0, nchunk, fill, 0, unroll=4)

                for ss in range(nchunk // DIL_SUPER):
                    cur, prv = _dil_super_rows(ss)
                    q3, kc3, kp3 = _dil_b3(qn_pm[cur, :]), _dil_b3(kn_pm[cur, :]), _dil_b3(kn_pm[prv, :])
                    vc3, vp3, do3 = _dil_b3(v_pm[cur, :]), _dil_b3(v_pm[prv, :]), _dil_b3(do_pm[cur, :])
                    ls = _dil_b3(lse_pm[cur, :])[:, :, :1]
                    delta = _dil_b3(dl_pm[cur, :])[:, :, :1]
                    s_c, s_p = _dil_scores(q3, kc3, kp3, slope, d, ss, nb)
                    p_c = jnp.exp(s_c - ls)
                    p_p = jnp.exp(s_p - ls)
                    dp_c = lax.dot_general(do3, vc3, BNT_DIMS, preferred_element_type=F32)
                    dp_p = lax.dot_general(do3, vp3, BNT_DIMS, preferred_element_type=F32)
                    ds_c = (p_c * (dp_c - delta)).astype(BF16)
                    ds_p = (p_p * (dp_p - delta)).astype(BF16)
                    dq3 = (lax.dot_general(ds_c, kc3, BNN_DIMS, preferred_element_type=F32)
                           + lax.dot_general(ds_p, kp3, BNN_DIMS, preferred_element_type=F32)) * scale
                    flat = lambda x: x.reshape(DIL_SUPER * DIL_BLK, DIL_HEAD_DIM)
                    dq_pm[cur, :] = flat(dq3)
                    dk_pm[cur, :] += flat(lax.dot_general(ds_c, q3, BTN_DIMS, preferred_element_type=F32)) * scale
                    dv_pm[cur, :] += flat(lax.dot_general(p_c.astype(BF16), do3, BTN_DIMS, preferred_element_type=F32))
                    dk_pm[prv, :] += flat(lax.dot_general(ds_p, q3, BTN_DIMS, preferred_element_type=F32)) * scale
                    dv_pm[prv, :] += flat(lax.dot_general(p_p.astype(BF16), do3, BTN_DIMS, preferred_element_type=F32))

                def to_tokens(src_pm):
                    def move(it, _):
                        tok, src = _dil_chunk(it, nb, d)
                        tok_sc[tok, :] = src_pm[src, :]
                        return 0
                    lax.fori_loop(0, nchunk, move, 0, unroll=4)

                def norm_bwd(x_ref, gvec, out_ref):
                    big = 4 * DIL_BLK

                    def fin(ci, dg):
                        rows = pl.ds(pl.multiple_of(ci * big, big), big)
                        xhat, r = _dil_norm(x_ref[rows, :], gvec)
                        dn = tok_sc[rows, :]
                        dxh = dn * gvec
                        c = jnp.mean(dxh * xhat, axis=-1, keepdims=True)
                        out_ref[rows, :] = (r * (dxh - xhat * c)).astype(BF16)
                        return dg + jnp.sum(dn * xhat, axis=0, keepdims=True)
                    return lax.fori_loop(0, T // big, fin, jnp.zeros((1, DIL_HEAD_DIM), F32))

                to_tokens(dq_pm)
                dgq_ref[...] += norm_bwd(q_ref, gq, dq_ref)
                to_tokens(dk_pm)
                dgk_ref[...] += norm_bwd(k_ref, gk, dk_ref)
                to_tokens(dv_pm)
                dv_ref[...] = tok_sc[...].astype(BF16)

    col = lambda off: pl.BlockSpec((T, DIL_HEAD_DIM), lambda gh, sl: (0, gh + off))
    hcol = pl.BlockSpec((T, DIL_HEAD_DIM), lambda gh, sl: (0, gh % DIL_HEADS))
    gvec = pl.BlockSpec((1, DIL_HEAD_DIM), lambda gh, sl: (0, 0))
    wide = jax.ShapeDtypeStruct((T, GH * DIL_HEAD_DIM), BF16)
    vec = jax.ShapeDtypeStruct((1, DIL_HEAD_DIM), F32)
    pm = lambda dt: pltpu.VMEM((DIL_BLK + T, DIL_HEAD_DIM), dt)
    return pl.pallas_call(
        body, name=name,
        grid_spec=pltpu.PrefetchScalarGridSpec(
            num_scalar_prefetch=1, grid=(GH,),
            in_specs=[col(0), col(GH), col(2 * GH), gvec, gvec, hcol, hcol, hcol],
            out_specs=[col(0), col(0), col(0), gvec, gvec],
            scratch_shapes=[pm(BF16)] * 4 + [pm(F32)] * 5 + [pltpu.VMEM((T, DIL_HEAD_DIM), F32)]),
        out_shape=[wide, wide, wide, vec, vec],
        compiler_params=_params(("arbitrary",)),
    )(slopes, qkv, qkv, qkv, g_qn, g_kn, do, delta, lse)


def _my_pos():
    return lax.axis_index("x"), lax.axis_index("y"), lax.axis_index("c")


def _peer(pos, j):
    x, y, c = pos
    px = 1 - x if j & 4 else x
    py = 1 - y if j & 2 else y
    pc = 1 - c if j & 1 else c
    return (px, py, pc), 4 * px + 2 * py + pc


def _slot(idx, paired):
    if not paired:
        return idx
    return jnp.where(idx < N_DEV // 2, 2 * idx, 2 * idx - (N_DEV - 1))


def _shard_slice(ref, axis, idx, size, paired=False):
    sl = [slice(None)] * len(ref.shape)
    sl[axis] = pl.ds(pl.multiple_of(_slot(idx, paired) * size, 8), size)
    return ref.at[tuple(sl)]


HBM_SPEC = pl.BlockSpec(memory_space=pltpu.HBM)
SEM_SPEC = pl.BlockSpec(memory_space=pltpu.SEMAPHORE)
DATAFLOW = pltpu.SideEffectType.DATAFLOW_SIDE_EFFECTING
N_PEER = N_DEV - 1


def _scatter_copy(axis, grad, slots, frm, to, dev, send_sem, recv_sem):
    ax, paired = axis
    src = _shard_slice(grad, ax, to, grad.shape[ax] // N_DEV, paired)
    return pltpu.make_async_remote_copy(src_ref=src, dst_ref=slots.at[frm], send_sem=send_sem, recv_sem=recv_sem,
                                        device_id=dev, device_id_type=MESH)


def _scatter_start(grads, axes, name):
    n = len(grads)

    def body(*refs):
        outs = refs[2 * n:]
        send, recv, token = outs[:n], outs[n:2 * n], outs[4 * n]
        pos = _my_pos()
        me = 4 * pos[0] + 2 * pos[1] + pos[2]
        for a in range(n):
            for j in range(1, N_DEV):
                dev, pid = _peer(pos, j)
                _scatter_copy(axes[a], refs[2 * a], refs[2 * a + 1], me, pid, dev, send[a].at[j - 1],
                              recv[a].at[j - 1]).start()
        token[...] = jnp.zeros_like(token)

    ops = []
    for g, (ax, _) in zip(grads, axes):
        shp = list(g.shape)
        shp[ax] //= N_DEV
        ops += [g, lax.empty((N_DEV,) + tuple(shp), g.dtype)]
    sems = [pltpu.SemaphoreType.DMA((N_PEER,))] * (2 * n)
    res = pl.pallas_call(
        body, name=name,
        out_shape=sems + [pltpu.HBM(o.shape, o.dtype) for o in ops] + [jax.ShapeDtypeStruct((8, LANES), F32)],
        in_specs=[HBM_SPEC] * len(ops),
        out_specs=[SEM_SPEC] * (2 * n) + [HBM_SPEC] * len(ops) + [pl.BlockSpec(memory_space=pltpu.VMEM)],
        input_output_aliases={i: 2 * n + i for i in range(len(ops))},
        compiler_params=pltpu.CompilerParams(has_side_effects=DATAFLOW),
    )(*[pltpu.with_memory_space_constraint(o, pltpu.HBM) for o in ops])
    items = [(res[a], res[n + a], res[2 * n + 2 * a], res[2 * n + 2 * a + 1]) for a in range(n)]
    return items, res[4 * n]


def _scatter_wait(items, axes, after, name):
    n = len(items)

    def body(*refs):
        send, recv = refs[2 * n:3 * n], refs[3 * n:4 * n]
        pos = _my_pos()
        me = 4 * pos[0] + 2 * pos[1] + pos[2]
        for a in range(n):
            for j in range(1, N_DEV):
                dev, pid = _peer(pos, j)
                cp = _scatter_copy(axes[a], refs[2 * a], refs[2 * a + 1], pid, me, dev, send[a].at[j - 1],
                                   recv[a].at[j - 1])
                cp.wait_send()
                cp.wait_recv()

    ops = [b for it in items for b in it[2:]]
    res = pl.pallas_call(
        body, name=name,
        out_shape=[pltpu.HBM(o.shape, o.dtype) for o in ops],
        in_specs=[HBM_SPEC] * len(ops) + [SEM_SPEC] * (2 * n) + [ANY_SPEC],
        out_specs=[HBM_SPEC] * len(ops),
        input_output_aliases={i: i for i in range(len(ops))},
        compiler_params=pltpu.CompilerParams(has_side_effects=DATAFLOW),
    )(*ops, *[it[0] for it in items], *[it[1] for it in items], after)
    return [(res[2 * a], res[2 * a + 1]) for a in range(n)]


SIBLING = 1
ICI_PEERS = (2, 4, 6)


def _gather_copy(buf, axis, shard, dev, send_sem, recv_sem):
    ax, paired = axis
    piece = _shard_slice(buf, ax, shard, buf.shape[ax] // N_DEV, paired)
    return pltpu.make_async_remote_copy(src_ref=piece, dst_ref=piece, send_sem=send_sem, recv_sem=recv_sem,
                                        device_id=dev, device_id_type=MESH)


def _gather_start(bufs, axes, name):
    n = len(bufs)

    def body(*refs):
        ins, outs = refs[:n], refs[n:]
        send, r_sib, r_ici, token = outs[:n], outs[n:2 * n], outs[2 * n:3 * n], outs[4 * n]
        pos = _my_pos()
        me = 4 * pos[0] + 2 * pos[1] + pos[2]
        for a in range(n):
            dev, _ = _peer(pos, SIBLING)
            _gather_copy(ins[a], axes[a], me, dev, send[a].at[0], r_sib[a].at[0]).start()
            for k, j in enumerate(ICI_PEERS):
                dev, _ = _peer(pos, j)
                _gather_copy(ins[a], axes[a], me, dev, send[a].at[1 + k], r_ici[a].at[k]).start()
        token[...] = jnp.zeros_like(token)

    sems = ([pltpu.SemaphoreType.DMA((1 + len(ICI_PEERS),))] * n + [pltpu.SemaphoreType.DMA((1,))] * n
            + [pltpu.SemaphoreType.DMA((len(ICI_PEERS),))] * n)
    res = pl.pallas_call(
        body, name=name,
        out_shape=sems + [pltpu.HBM(b.shape, b.dtype) for b in bufs] + [jax.ShapeDtypeStruct((8, LANES), F32)],
        in_specs=[HBM_SPEC] * n,
        out_specs=[SEM_SPEC] * (3 * n) + [HBM_SPEC] * n + [pl.BlockSpec(memory_space=pltpu.VMEM)],
        input_output_aliases={i: 3 * n + i for i in range(n)},
        compiler_params=pltpu.CompilerParams(has_side_effects=DATAFLOW),
    )(*[pltpu.with_memory_space_constraint(b, pltpu.HBM) for b in bufs])
    items = [dict(send=res[a], r_sib=res[n + a], r_ici=res[2 * n + a], buf=res[3 * n + a]) for a in range(n)]
    return items, res[4 * n]


def _gather_relay(items, axes, after, name):
    n = len(items)

    def body(*refs):
        ins, r_ici = refs[:n], refs[n:2 * n]
        outs = refs[2 * n + 1:]
        s_rel, r_rel, token = outs[:n], outs[n:2 * n], outs[3 * n]
        pos = _my_pos()
        sib, _ = _peer(pos, SIBLING)
        for a in range(n):
            for k, j in enumerate(ICI_PEERS):
                dev, pid = _peer(pos, j)
                _gather_copy(ins[a], axes[a], pid, dev, s_rel[a].at[k], r_ici[a].at[k]).wait_recv()
                _gather_copy(ins[a], axes[a], pid, sib, s_rel[a].at[k], r_rel[a].at[k]).start()
        token[...] = jnp.zeros_like(token)

    bufs = [it["buf"] for it in items]
    sems = [pltpu.SemaphoreType.DMA((len(ICI_PEERS),))] * (2 * n)
    res = pl.pallas_call(
        body, name=name,
        out_shape=sems + [pltpu.HBM(b.shape, b.dtype) for b in bufs] + [jax.ShapeDtypeStruct((8, LANES), F32)],
        in_specs=[HBM_SPEC] * n + [SEM_SPEC] * n + [ANY_SPEC],
        out_specs=[SEM_SPEC] * (2 * n) + [HBM_SPEC] * n + [pl.BlockSpec(memory_space=pltpu.VMEM)],
        input_output_aliases={i: 2 * n + i for i in range(n)},
        compiler_params=pltpu.CompilerParams(has_side_effects=DATAFLOW),
    )(*bufs, *[it["r_ici"] for it in items], after)
    out = [dict(send=it["send"], r_sib=it["r_sib"], s_rel=res[a], r_rel=res[n + a], buf=res[2 * n + a])
           for a, it in enumerate(items)]
    return out, res[3 * n]


def _gather_wait(items, axes, after, name):
    n = len(items)

    def body(*refs):
        ins = refs[:n]
        send, r_sib, s_rel, r_rel = (refs[(1 + q) * n:(2 + q) * n] for q in range(4))
        pos = _my_pos()
        me = 4 * pos[0] + 2 * pos[1] + pos[2]
        sib, sib_id = _peer(pos, SIBLING)
        for a in range(n):
            for k in range(1 + len(ICI_PEERS)):
                _gather_copy(ins[a], axes[a], me, sib, send[a].at[k], r_sib[a].at[0]).wait_send()
            _gather_copy(ins[a], axes[a], sib_id, sib, send[a].at[0], r_sib[a].at[0]).wait_recv()
            for k, j in enumerate(ICI_PEERS):
                _, pid = _peer(pos, j)
                _, far = _peer(pos, j ^ SIBLING)
                _gather_copy(ins[a], axes[a], pid, sib, s_rel[a].at[k], r_rel[a].at[k]).wait_send()
                _gather_copy(ins[a], axes[a], far, sib, s_rel[a].at[k], r_rel[a].at[k]).wait_recv()

    bufs = [it["buf"] for it in items]
    res = pl.pallas_call(
        body, name=name,
        out_shape=[pltpu.HBM(b.shape, b.dtype) for b in bufs],
        in_specs=[HBM_SPEC] * n + [SEM_SPEC] * (4 * n) + [ANY_SPEC],
        out_specs=[HBM_SPEC] * n,
        input_output_aliases={i: i for i in range(n)},
        compiler_params=pltpu.CompilerParams(has_side_effects=DATAFLOW),
    )(*bufs, *[it["send"] for it in items], *[it["r_sib"] for it in items], *[it["s_rel"] for it in items],
      *[it["r_rel"] for it in items], after)
    return list(res)


def _gain_allreduce(v, name):
    n = v.shape[1]

    def body(v_ref, o_ref, slots, send_sems, recv_sems):
        pos = _my_pos()
        me = 4 * pos[0] + 2 * pos[1] + pos[2]
        slots[me] = v_ref[...]
        copies = []
        for j in range(1, N_DEV):
            dev, _ = _peer(pos, j)
            cp = pltpu.make_async_remote_copy(
                src_ref=slots.at[me], dst_ref=slots.at[me], send_sem=send_sems.at[j], recv_sem=recv_sems.at[j],
                device_id=dev, device_id_type=MESH)
            cp.start()
            copies.append(cp)
        for j in range(1, N_DEV):
            dev, pid = _peer(pos, j)
            pltpu.make_async_remote_copy(
                src_ref=slots.at[me], dst_ref=slots.at[pid], send_sem=send_sems.at[j], recv_sem=recv_sems.at[j],
                device_id=dev, device_id_type=MESH).wait_recv()
        for cp in copies:
            cp.wait_send()
        acc = slots[0]
        for s in range(1, N_DEV):
            acc = acc + slots[s]
        o_ref[...] = acc

    return pl.pallas_call(
        body, name=name, out_shape=jax.ShapeDtypeStruct((1, n), F32),
        in_specs=[pl.BlockSpec(memory_space=pltpu.VMEM)], out_specs=pl.BlockSpec(memory_space=pltpu.VMEM),
        scratch_shapes=[pltpu.VMEM((N_DEV, 1, n), F32), pltpu.SemaphoreType.DMA((N_DEV,)),
                        pltpu.SemaphoreType.DMA((N_DEV,))],
        compiler_params=pltpu.CompilerParams(has_side_effects=True),
    )(v)


def _adamw(parts, own, me, w, m, v, layer, prev, name, own_axis=None):
    L, R, C = w.shape
    P = parts.shape[0]
    tr = _pick(R, (128, 64, 32, 16, 8, 1))
    c1 = 1.0 - ADAM_B1 ** ADAM_STEP
    c2 = 1.0 - ADAM_B2 ** ADAM_STEP
    n_in = 4 if own is None else 5

    def body(me_ref, *refs):
        p_ref = refs[0]
        w_ref, m_ref, v_ref = refs[n_in - 3:n_in]
        g_out, d_out, m_out, v_out, tok = refs[-5:]
        g = None
        for s in range(P):
            part = p_ref[s]
            if own is not None:
                part = jnp.where(me_ref[0] == s, refs[1][...], part)
            g = part.astype(F32) if g is None else g + part.astype(F32)
        mn = ADAM_B1 * m_ref[...] + (1.0 - ADAM_B1) * g
        vn = ADAM_B2 * v_ref[...] + (1.0 - ADAM_B2) * (g * g)
        g_out[...] = g
        m_out[...] = mn
        v_out[...] = vn
        d_out[...] = -ADAM_LR * ((mn / c1) / (jnp.sqrt(vn / c2) + ADAM_EPS) + ADAM_WD * w_ref[...])
        tok[...] = jnp.zeros_like(tok)

    row = pl.BlockSpec((None, tr, C), lambda i, me_ref: (layer, i, 0))
    in_specs = [pl.BlockSpec((P, tr, C), lambda i, me_ref: (0, i, 0))]
    args = [parts]
    if own is not None:
        if own_axis is None:
            own_idx = lambda i, me_ref: (i, 0)
        elif own_axis[0] == 0:
            own_idx = lambda i, me_ref: (_slot(me_ref[0], own_axis[1]) * (R // tr) + i, 0)
        else:
            own_idx = lambda i, me_ref: (i, _slot(me_ref[0], own_axis[1]))
        in_specs.append(pl.BlockSpec((tr, C), own_idx))
        args.append(own)
    in_specs += [row, row, row]
    args += [w, m, v]
    aliases = {}
    if prev is not None:
        in_specs += [ANY_SPEC] * 4
        aliases = {1 + len(args) + k: k for k in range(4)}
        args += list(prev)
    shp = jax.ShapeDtypeStruct((L, R, C), F32)
    res = pl.pallas_call(
        body, name=name,
        grid_spec=pltpu.PrefetchScalarGridSpec(
            num_scalar_prefetch=1, grid=(R // tr,), in_specs=in_specs,
            out_specs=[row] * 4 + [pl.BlockSpec((8, LANES), lambda i, me_ref: (0, 0))]),
        out_shape=[shp] * 4 + [jax.ShapeDtypeStruct((8, LANES), F32)],
        input_output_aliases=aliases, compiler_params=_params(("arbitrary",)),
    )(me, *args)
    return res[:4], res[4]


def _pad_heads(w):
    lead = w.shape[:-1]
    n = w.shape[-1] // QK_DIM
    w = w.reshape(lead + (n, QK_DIM))
    w = jnp.pad(w, [(0, 0)] * len(lead) + [(0, 0), (0, HEAD_PAD - QK_DIM)])
    return w.reshape(lead + (n * HEAD_PAD,))


def _unpad_heads(w):
    lead = w.shape[:-1]
    n = w.shape[-1] // HEAD_PAD
    return w.reshape(lead + (n, HEAD_PAD))[..., :QK_DIM].reshape(lead + (n * QK_DIM,))


def kernel(x, ffn1_norm, ffn1_w_in, ffn1_w_out, mix_norm, ffn2_norm, ffn2_w_in, ffn2_w_out, mla_w_down, mla_g_cq, mla_g_ckv, mla_w_uq, mla_w_ukv, mla_g_qn, mla_g_kn, mla_w_o, dil_w_qkv, dil_g_qn, dil_g_kn, dil_w_o, loss_target, m_ffn1_norm, m_ffn1_w_in, m_ffn1_w_out, m_mix_norm, m_ffn2_norm, m_ffn2_w_in, m_ffn2_w_out, m_mla_w_down, m_mla_g_cq, m_mla_g_ckv, m_mla_w_uq, m_mla_w_ukv, m_mla_g_qn, m_mla_g_kn, m_mla_w_o, m_dil_w_qkv, m_dil_g_qn, m_dil_g_kn, m_dil_w_o, v_ffn1_norm, v_ffn1_w_in, v_ffn1_w_out, v_mix_norm, v_ffn2_norm, v_ffn2_w_in, v_ffn2_w_out, v_mla_w_down, v_mla_g_cq, v_mla_g_ckv, v_mla_w_uq, v_mla_w_ukv, v_mla_g_qn, v_mla_g_kn, v_mla_w_o, v_dil_w_qkv, v_dil_g_qn, v_dil_g_kn, v_dil_w_o):
    names = ["ffn1_norm", "ffn1_w_in", "ffn1_w_out", "mix_norm", "ffn2_norm", "ffn2_w_in", "ffn2_w_out", "mla_w_down",
             "mla_g_cq", "mla_g_ckv", "mla_w_uq", "mla_w_ukv", "mla_g_qn", "mla_g_kn", "mla_w_o", "dil_w_qkv",
             "dil_g_qn", "dil_g_kn", "dil_w_o"]
    W = dict(zip(names, [ffn1_norm, ffn1_w_in, ffn1_w_out, mix_norm, ffn2_norm, ffn2_w_in, ffn2_w_out, mla_w_down,
                         mla_g_cq, mla_g_ckv, mla_w_uq, mla_w_ukv, mla_g_qn, mla_g_kn, mla_w_o, dil_w_qkv,
                         dil_g_qn, dil_g_kn, dil_w_o]))
    M1 = dict(zip(names, [m_ffn1_norm, m_ffn1_w_in, m_ffn1_w_out, m_mix_norm, m_ffn2_norm, m_ffn2_w_in, m_ffn2_w_out,
                          m_mla_w_down, m_mla_g_cq, m_mla_g_ckv, m_mla_w_uq, m_mla_w_ukv, m_mla_g_qn, m_mla_g_kn,
                          m_mla_w_o, m_dil_w_qkv, m_dil_g_qn, m_dil_g_kn, m_dil_w_o]))
    V2 = dict(zip(names, [v_ffn1_norm, v_ffn1_w_in, v_ffn1_w_out, v_mix_norm, v_ffn2_norm, v_ffn2_w_in, v_ffn2_w_out,
                          v_mla_w_down, v_mla_g_cq, v_mla_g_ckv, v_mla_w_uq, v_mla_w_ukv, v_mla_g_qn, v_mla_g_kn,
                          v_mla_w_o, v_dil_w_qkv, v_dil_g_qn, v_dil_g_kn, v_dil_w_o]))
    S, D = x.shape[1], x.shape[2]
    x0 = x.reshape(S, D)
    tgt = loss_target.reshape(S, D)

    big = ["ffn1_w_in", "ffn1_w_out", "ffn2_w_in", "ffn2_w_out", "mla_w_down", "mla_w_uq", "mla_w_ukv", "mla_w_o",
           "dil_w_qkv", "dil_w_o"]
    shard_dim = {"ffn1_w_in": 2, "ffn1_w_out": 1, "ffn2_w_in": 2, "ffn2_w_out": 1, "mla_w_down": 1, "mla_w_uq": 2,
                 "mla_w_ukv": 2, "mla_w_o": 1, "dil_w_qkv": 2, "dil_w_o": 2}
    paired = ("ffn1_w_in", "ffn2_w_in")
    shard_axis = {n: (d, n in paired) for n, d in shard_dim.items()}
    grad_axis = {n: (d - 1, n in paired) for n, d in shard_dim.items()}

    def padded(n, w):
        if n == "mla_w_down":
            return jnp.pad(w, ((0, 0), (0, 0), (0, LAT_PAD - w.shape[2])))
        if n == "mla_w_uq":
            return _pad_heads(w)
        return w

    depth = ffn1_norm.shape[0]
    blocks = []
    for l in range(depth):
        mixer = (["mla_w_down", "mla_w_uq", "mla_w_ukv", "mla_w_o"] if l % 2 == 0 else ["dil_w_qkv", "dil_w_o"])
        blocks.append((f"ffn1_{l}", [("ffn1_w_in", l), ("ffn1_w_out", l)]))
        blocks.append((f"mix_{l}", [(n, l // 2) for n in mixer]))
        blocks.append((f"ffn2_{l}", [("ffn2_w_in", l), ("ffn2_w_out", l)]))
    order = [k for _, keys in blocks for k in keys]
    me = (4 * lax.axis_index("x") + 2 * lax.axis_index("y") + lax.axis_index("c")).astype(jnp.int32).reshape(1)
    def cast(key, deps=()):
        n, l = key
        return _cast_into_gathered(padded(n, W[n]), l, shard_axis[n], me, f"cast_{n}_{l}", deps=deps)

    items0, token0 = _gather_start([cast(order[0])], [shard_axis[order[0][0]]], "gather_start_first")
    rest = order[1:]
    items1, ag_token = _gather_start([cast(k, deps=[token0]) for k in rest], [shard_axis[k[0]] for k in rest],
                                     "gather_start_rest")
    ag_items = dict(zip(order, items0 + items1))
    full = {}

    def relay(keys, after, tag):
        out, token = _gather_relay([ag_items[k] for k in keys], [shard_axis[k[0]] for k in keys], after,
                                   f"gather_relay_{tag}")
        ag_items.update(zip(keys, out))
        return [token]

    def relay_next(bi, after):
        return relay(blocks[bi + 1][1], after, blocks[bi + 1][0]) if bi + 1 < len(blocks) else []

    def fetch(keys, after, tag):
        lands = _gather_wait([ag_items[k] for k in keys], [shard_axis[k[0]] for k in keys], after,
                             f"gather_wait_{tag}")
        full.update(zip(keys, lands))

    g_qn = _pad_heads(mla_g_qn)
    g_kn = _pad_heads(mla_g_kn)
    tabs = _rope_tables(S)
    slopes = jnp.asarray(_alibi_slopes(), F32)

    grads = {}
    gain_g = {}

    out_g, out_d, out_m, out_v = {}, {}, {}, {}
    pending = []
    lag = 3

    def scatter_start(tag, keys):
        items, token = _scatter_start([grads[k] for k in keys], [grad_axis[k[0]] for k in keys],
                                      f"scatter_start_{tag}")
        pending.append((tag, keys, items))
        return token

    def scatter_finish(after):
        tag, keys, items = pending.pop(0)
        lands = _scatter_wait(items, [grad_axis[k[0]] for k in keys], after, f"scatter_wait_{tag}")
        tokens = []
        for (n, l), (own, p) in zip(keys, lands):
            own_axis = grad_axis[n]
            if n in ("mla_w_down", "mla_w_uq"):
                ax, pair = grad_axis[n]
                size = own.shape[ax] // N_DEV
                own = lax.dynamic_slice_in_dim(own, _slot(me[0], pair) * size, size, axis=ax)
                own_axis = None
                if n == "mla_w_down":
                    p, own = p[..., :W[n].shape[2]], own[..., :W[n].shape[2]]
                else:
                    p, own = _unpad_heads(p), _unpad_heads(own)
            prev = (out_g[n], out_d[n], out_m[n], out_v[n]) if n in out_g else None
            (out_g[n], out_d[n], out_m[n], out_v[n]), tok = _adamw(p, own, me, W[n], M1[n], V2[n], l, prev,
                                                                    f"adamw_{n}_{l}", own_axis=own_axis)
            tokens.append(tok)
        return tokens

    def finish_due(after):
        tokens = []
        while len(pending) > lag:
            tokens += scatter_finish(after)
        return tokens

    def ffn_fwd(xin, norm_row, which, l, bi, deps=()):
        tag = blocks[bi][0]
        k_in, k_out = (which + "_w_in", l), (which + "_w_out", l)
        h = _rms_fwd(xin, norm_row, f"rms_fwd_{tag}", deps=deps)
        if bi == 0:
            relay([k_in], h, f"{tag}_in")
        fetch([k_in], h, f"in_{tag}")
        u, a = _ffn_in(h, full[k_in], f"ffn_in_{tag}")
        if bi == 0:
            relay([k_out], a, f"{tag}_out")
        fetch([k_out], a, f"out_{tag}")
        toks = relay_next(bi, a)
        xo = _mm(a, full[k_out], "nn", F32, f"mm_out_{tag}", scale=0.5, res=xin, layer=0, deps=toks)
        return xo, (xin, h, u, a)

    def ffn_bwd(dx_pair, saved, norm_row, which, l, tag):
        dxo, dxob = dx_pair
        k_in, k_out = (which + "_w_in", l), (which + "_w_out", l)
        xin, h, u, a = saved
        grads[k_out] = _mm(a, dxob, "tn", BF16, f"mm_dwout_{tag}", scale=0.5)
        t_out = scatter_start(f"{tag}_out", [k_out])
        du = _ffn_da(dxob, full[k_out], u, f"ffn_da_{tag}", deps=[t_out])
        grads[k_in] = _mm(h, du, "tn", BF16, f"mm_dwin_{tag}")
        t_in = scatter_start(f"{tag}_in", [k_in])
        dh = _mm(du, full[k_in], "nt", F32, f"mm_dh_{tag}", layer=0, deps=[t_in])
        toks = finish_due(dh)
        dx, dxb, dg = _rms_bwd(xin, norm_row, dh, dxo, f"rms_bwd_{tag}", deps=toks)
        gain_g.setdefault(which + "_norm", {})[l] = dg
        return dx, dxb

    def mla_fwd(xin, l, bi):
        j = l // 2
        xn = _rms_fwd(xin, mix_norm[l:l + 1], "rms_fwd_mla")
        fetch([(n, j) for n in ("mla_w_down", "mla_w_uq", "mla_w_ukv", "mla_w_o")], xn, "mla")
        lat = _mm(xn, full[("mla_w_down", j)], "nn", F32, "mm_lat", layer=0)
        cq, ckv = _lat_norm_fwd(lat, mla_g_cq[j:j + 1], mla_g_ckv[j:j + 1], "lat_norm_fwd")
        q_raw = _mm(cq, full[("mla_w_uq", j)], "nn", F32, "mm_uq", layer=0)
        kv = _mm(ckv, full[("mla_w_ukv", j)], "nn", F32, "mm_ukv", layer=0)
        qf, kf, vb = _mla_prep_fwd(q_raw, kv, lat, g_qn[j:j + 1], g_kn[j:j + 1], tabs, "mla_prep_fwd")
        o, lse = _flash_fwd(qf, kf, vb, "flash_fwd")
        toks = relay_next(bi, o)
        xo = _mm(o, full[("mla_w_o", j)], "nn", F32, "mm_mla_o", res=xin, layer=0, deps=toks)
        return xo, (xin, xn, lat, cq, ckv, q_raw, kv, qf, kf, vb, o, lse)

    def mla_bwd(dx_pair, saved, l):
        dxo, dxob = dx_pair
        j = l // 2
        xin, xn, lat, cq, ckv, q_raw, kv, qf, kf, vb, o, lse = saved
        do = _mm(dxob, full[("mla_w_o", j)], "nt", BF16, "mm_mla_do", layer=0)
        grads[("mla_w_o", j)] = _mm(o, dxob, "tn", BF16, "mm_mla_dwo")
        delta = _attn_delta(do, o, "attn_delta")
        dqf, dkf, dv = _flash_bwd(qf, kf, vb, do, lse, delta, "flash_bwd")
        dq_raw, dkv, dkpe, dgq, dgk = _mla_prep_bwd(q_raw, kv, lat, g_qn[j:j + 1], g_kn[j:j + 1], tabs, dqf, dkf, dv,
                                                    "mla_prep_bwd")
        gain_g.setdefault("mla_g_qn", {})[j] = dgq
        gain_g.setdefault("mla_g_kn", {})[j] = dgk
        dcq = _mm(dq_raw, full[("mla_w_uq", j)], "nt", F32, "mm_dcq", layer=0)
        grads[("mla_w_uq", j)] = _mm(cq, dq_raw, "tn", BF16, "mm_dwuq")
        dckv = _mm(dkv, full[("mla_w_ukv", j)], "nt", F32, "mm_dckv", layer=0)
        grads[("mla_w_ukv", j)] = _mm(ckv, dkv, "tn", BF16, "mm_dwukv")
        dlat, dgcq, dgckv = _lat_norm_bwd(lat, mla_g_cq[j:j + 1], mla_g_ckv[j:j + 1], dcq, dckv, dkpe, "lat_norm_bwd")
        gain_g.setdefault("mla_g_cq", {})[j] = dgcq
        gain_g.setdefault("mla_g_ckv", {})[j] = dgckv
        dxn = _mm(dlat, full[("mla_w_down", j)], "nt", F32, "mm_dxn_mla", layer=0)
        grads[("mla_w_down", j)] = _mm(xn, dlat, "tn", BF16, "mm_dwdown")
        tok = scatter_start(f"mix_{l}", [(n, j) for n in ("mla_w_down", "mla_w_uq", "mla_w_ukv", "mla_w_o")])
        toks = finish_due(dxn)
        dx, dxb, dg = _rms_bwd(xin, mix_norm[l:l + 1], dxn, dxo, "rms_bwd_mla", deps=[tok] + toks)
        gain_g.setdefault("mix_norm", {})[l] = dg
        return dx, dxb

    def dil_fwd(xin, l, bi):
        j = l // 2
        xn = _rms_fwd(xin, mix_norm[l:l + 1], "rms_fwd_dil")
        fetch([("dil_w_qkv", j), ("dil_w_o", j)], xn, "dil")
        qkv = _mm(xn, full[("dil_w_qkv", j)], "nn", F32, "mm_qkv", layer=0)
        o_g, lse_g = _dil_fwd(qkv, dil_g_qn[j:j + 1], dil_g_kn[j:j + 1], slopes, "dil_fwd")
        o, lse = _dil_merge(o_g, lse_g, "dil_merge")
        toks = relay_next(bi, o)
        xo = _mm(o, full[("dil_w_o", j)], "nn", F32, "mm_dil_o", res=xin, layer=0, deps=toks)
        return xo, (xin, xn, qkv, o, lse)

    def dil_bwd(dx_pair, saved, l):
        dxo, dxob = dx_pair
        j = l // 2
        xin, xn, qkv, o, lse = saved
        do = _mm(dxob, full[("dil_w_o", j)], "nt", F32, "mm_dil_do", layer=0)
        grads[("dil_w_o", j)] = _mm(o, dxob, "tn", BF16, "mm_dil_dwo")
        delta = _attn_delta(do, o, "dil_delta")
        dq, dk, dv, dgq, dgk = _dil_bwd(qkv, dil_g_qn[j:j + 1], dil_g_kn[j:j + 1], slopes, do, delta, lse, "dil_bwd")
        gain_g.setdefault("dil_g_qn", {})[j] = dgq
        gain_g.setdefault("dil_g_kn", {})[j] = dgk
        dqkv = jnp.concatenate([dq, dk, dv], axis=1)
        dxn = _mm(dqkv, full[("dil_w_qkv", j)], "nt", F32, "mm_dxn_dil", layer=0)
        grads[("dil_w_qkv", j)] = _mm(xn, dqkv, "tn", BF16, "mm_dwqkv")
        tok = scatter_start(f"mix_{l}", [("dil_w_qkv", j), ("dil_w_o", j)])
        toks = finish_due(dxn)
        dx, dxb, dg = _rms_bwd(xin, mix_norm[l:l + 1], dxn, dxo, "rms_bwd_dil", deps=[tok] + toks)
        gain_g.setdefault("mix_norm", {})[l] = dg
        return dx, dxb

    saved = []
    xc = x0
    for l in range(depth):
        xc, s1 = ffn_fwd(xc, ffn1_norm[l:l + 1], "ffn1", l, 3 * l, deps=[ag_token] if l == 0 else ())
        xc, s2 = (mla_fwd if l % 2 == 0 else dil_fwd)(xc, l, 3 * l + 1)
        xc, s3 = ffn_fwd(xc, ffn2_norm[l:l + 1], "ffn2", l, 3 * l + 2)
        saved.append((s1, s2, s3))

    dy, dyb, loss_part = _loss_head(xc, tgt, "loss_head")
    dx = (dy, dyb)
    loss = lax.psum(loss_part[0, 0], MESH_AXES)

    for bi in reversed(range(len(blocks))):
        tag, _ = blocks[bi]
        l = bi // 3
        s = saved[l][bi % 3]
        if bi % 3 == 2:
            dx = ffn_bwd(dx, s, ffn2_norm[l:l + 1], "ffn2", l, tag)
        elif bi % 3 == 1:
            dx = (mla_bwd if l % 2 == 0 else dil_bwd)(dx, s, l)
        else:
            dx = ffn_bwd(dx, s, ffn1_norm[l:l + 1], "ffn1", l, tag)
    grad_x = dx[0].reshape(x.shape)
    after = dx[1]
    while pending:
        after = scatter_finish(after)[-1]

    small = [n for n in names if n not in big]

    def gain_local(n):
        rows = [gain_g[n][l] for l in range(W[n].shape[0])]
        g = jnp.concatenate(rows, axis=1)
        return g

    def flat_pad(n, a):
        a = a.reshape(1, -1)
        if n in ("mla_g_qn", "mla_g_kn"):
            a = _pad_heads(a)
        return a

    packed_g = jnp.concatenate([gain_local(n) for n in small], axis=1)
    sizes = [gain_local(n).shape[1] for n in small]
    tot_g = _gain_allreduce(packed_g, "gain_allreduce")
    pw = jnp.concatenate([flat_pad(n, W[n]) for n in small], axis=1)
    pm = jnp.concatenate([flat_pad(n, M1[n]) for n in small], axis=1)
    pv = jnp.concatenate([flat_pad(n, V2[n]) for n in small], axis=1)
    res, _ = _adamw(tot_g.reshape(1, 1, -1), None, me, pw.reshape(1, 1, -1), pm.reshape(1, 1, -1),
                    pv.reshape(1, 1, -1), 0, None, "adamw_gains")
    res = [r.reshape(1, -1) for r in res]
    off = 0
    for n, sz in zip(small, sizes):
        for dst, r in zip((out_g, out_d, out_m, out_v), res):
            piece = r[:, off:off + sz]
            if n in ("mla_g_qn", "mla_g_kn"):
                piece = _unpad_heads(piece)
            dst[n] = piece.reshape(W[n].shape)
        off += sz

    return (loss, grad_x, *[out_g[n] for n in names], *[out_d[n] for n in names],
            *[out_m[n] for n in names], *[out_v[n] for n in names])
```

```python
import functools
import math

import jax
import jax.numpy as jnp
import numpy as np
from jax import lax
from jax.experimental import pallas as pl
from jax.experimental.pallas import tpu as pltpu

EPS = 1e-6
MLA_HEADS = 16
Q_LORA = 512
KV_LORA = 512
NOPE_DIM = 128
ROPE_DIM = 64
V_DIM = 128
QK_DIM = NOPE_DIM + ROPE_DIM
ROPE_THETA = 10000.0
HEAD_PAD = 256
LAT_PAD = Q_LORA + KV_LORA + 128
DIL_PAIRS = ((128, 1), (512, 4), (2048, 16))
DIL_GROUPS = 3
DIL_HEADS = 8
DIL_HEAD_DIM = 128
DIL_BLK = 128
FLASH_HEADS = 2
FLASH_HEADS_FWD = 4
LOG2E = math.log2(math.e)
LN2 = math.log(2.0)
ADAM_LR = 0.001
ADAM_B1 = 0.9
ADAM_B2 = 0.999
ADAM_EPS = 1e-08
ADAM_WD = 0.01
ADAM_STEP = 10

N_DEV = 8
MESH_AXES = ("x", "y", "c")
MESH = pl.DeviceIdType.MESH
NEG_BIG = -1e30
VMEM_LIMIT_V7X = 56 * 1024 * 1024
LANES = 128

BF16 = jnp.bfloat16
F32 = jnp.float32


def _pick(n, cands):
    for c in cands:
        if n % c == 0:
            return c
    raise ValueError(f"no tile for {n}")


def _params(sem):
    return pltpu.CompilerParams(dimension_semantics=sem, vmem_limit_bytes=VMEM_LIMIT_V7X)


ANY_SPEC = pl.BlockSpec(memory_space=pl.ANY)


MM_VMEM_BUDGET = 44 * 1024 * 1024
MM_HBM_BYTES_PER_S = 1.8e12
MM_MXU_FLOPS_PER_S = 8.5e14
MM_STEP_S = 0.4e-6
MM_MAX_TILE_MACS = 3.3e9
MXU_DIM = 256


MM_TIMED_TILES = {
    (2048, 4096, 11264, 2, 2, 2, False): (1024, 1024, 4096, True),
    (4096, 11264, 2048, 2, 2, 4, False): (1024, 1024, 2816, True),
    (4096, 5632, 2048, 2, 2, 4, True): (512, 1024, 5632, False),
    (2048, 4096, 9216, 2, 2, 2, False): (1024, 1024, 4096, True),
}


@functools.lru_cache(maxsize=None)
def _mm_tiles(M, K, N, a_bytes, b_bytes, out_bytes, has_res):
    if (M, K, N, a_bytes, b_bytes, out_bytes, has_res) in MM_TIMED_TILES:
        return MM_TIMED_TILES[(M, K, N, a_bytes, b_bytes, out_bytes, has_res)]
    best = None
    for tk in [K] + [c for c in (1408, 1024, 512, 384, 256, 128) if K % c == 0 and c < K]:
        nk = K // tk
        for tm in [c for c in (2048, 1024, 512, 256, 128) if M % c == 0]:
            for tn in [c for c in (2816, 2048, 1408, 1152, 1024, 512, 384, 256, 128) if N % c == 0]:
                if tm * tk * tn > MM_MAX_TILE_MACS:
                    continue
                fill = (tn / (-(-tn // MXU_DIM) * MXU_DIM)) * (tk / (-(-tk // MXU_DIM) * MXU_DIM))
                fill *= tm / (tm + MXU_DIM // 2)
                vmem = 2 * (tm * tk * a_bytes + tk * tn * b_bytes) + 2 * tm * tn * out_bytes + tm * tn * 4
                vmem += (tm * tk + tk * tn) * 2 if max(a_bytes, b_bytes) > 2 else 0
                vmem += 2 * tm * tn * 4 if has_res else 0
                if vmem > MM_VMEM_BUDGET:
                    continue
                a_all, b_all = M * K * a_bytes, K * N * b_bytes
                if nk == 1:
                    t_i = a_all + (M // tm) * b_all
                    t_j = b_all + (N // tn) * a_all
                    traffic, i_outer = min((t_i, True), (t_j, False))
                else:
                    traffic, i_outer = (N // tn) * a_all + (M // tm) * b_all, True
                traffic += M * N * (out_bytes + (4 if has_res else 0))
                mxu = 2.0 * M * K * N / (MM_MXU_FLOPS_PER_S * fill) * (1.15 if nk > 1 else 1.0)
                cost = max(traffic / MM_HBM_BYTES_PER_S, mxu) + (M // tm) * (N // tn) * nk * MM_STEP_S
                if best is None or cost < best[0]:
                    best = (cost, tm, tn, tk, i_outer)
    assert best is not None, (M, K, N)
    return best[1:]


def _mm(a, b, mode, out_dtype, name, *, scale=1.0, res=None, layer=None, deps=()):
    b2 = b.shape[-2:]
    if mode == "nn":
        (M, K), (Kb, N) = a.shape, b2
    elif mode == "nt":
        (M, K), (N, Kb) = a.shape, b2
    else:
        (K, M), (Kb, N) = a.shape, b2
    assert K == Kb, (a.shape, b.shape, mode)
    tm, tn, tk, i_outer = _mm_tiles(M, K, N, a.dtype.itemsize, b.dtype.itemsize, jnp.dtype(out_dtype).itemsize,
                                    res is not None)
    nk = K // tk
    dims = {"nn": (((1,), (0,)), ((), ())), "nt": (((1,), (1,)), ((), ())), "tn": (((0,), (0,)), ((), ()))}[mode]

    def finish(v, r_ref, o_ref):
        if scale != 1.0:
            v = v * scale
        if r_ref is not None:
            v = r_ref[...] + v
        o_ref[...] = v.astype(o_ref.dtype)

    def body(*refs):
        a_ref, b_ref = refs[:2]
        r_ref = refs[2] if res is not None else None
        prod = lambda: lax.dot_general(a_ref[...].astype(BF16), b_ref[...].astype(BF16), dims,
                                       preferred_element_type=F32)
        if nk == 1:
            finish(prod(), r_ref, refs[-1])
            return
        o_ref, acc = refs[-2:]
        k = pl.program_id(2)

        @pl.when(k == 0)
        def _():
            acc[...] = prod()

        @pl.when(k > 0)
        def _():
            acc[...] += prod()

        @pl.when(k == nk - 1)
        def _():
            finish(acc[...], r_ref, o_ref)

    ij = (lambda p, q: (p, q)) if i_outer else (lambda p, q: (q, p))

    def spec(shape, f, lead=None):
        full = lambda p, q, k: f(*ij(p, q), k)
        if lead is None:
            return pl.BlockSpec(shape, full)
        return pl.BlockSpec((None,) + shape, lambda p, q, k: (lead,) + full(p, q, k))

    a_spec = spec((tk, tm), lambda i, j, k: (k, i)) if mode == "tn" else spec((tm, tk), lambda i, j, k: (i, k))
    lead = layer if b.ndim == 3 else None
    b_spec = spec((tn, tk), lambda i, j, k: (j, k), lead) if mode == "nt" else spec((tk, tn), lambda i, j, k: (k, j), lead)
    in_specs = [a_spec, b_spec]
    args = [a, b]
    if res is not None:
        in_specs.append(spec((tm, tn), lambda i, j, k: (i, j)))
        args.append(res)
    in_specs += [ANY_SPEC] * len(deps)
    args += list(deps)
    outer, inner = (M // tm, N // tn) if i_outer else (N // tn, M // tm)
    return pl.pallas_call(
        body, name=name, grid=(outer, inner, nk),
        in_specs=in_specs, out_specs=spec((tm, tn), lambda i, j, k: (i, j)),
        out_shape=jax.ShapeDtypeStruct((M, N), out_dtype),
        scratch_shapes=[pltpu.VMEM((tm, tn), F32)] if nk > 1 else [],
        compiler_params=_params(("parallel", "parallel", "arbitrary")),
    )(*args)


def _cast_into_gathered(w, layer, axis, me, name, deps=()):
    _, R, C = w.shape
    tr = _pick(R, (512, 256, 128, 64, 32, 16))
    nr = R // tr
    axis, paired = axis

    def body(me_ref, w_ref, *rest):
        o_ref = rest[-1]
        o_ref[...] = w_ref[...].astype(BF16)

    if axis == 1:
        out_idx = lambda i, me_ref: (0, _slot(me_ref[0], paired) * nr + i, 0)
        shape = (1, R * N_DEV, C)
    else:
        out_idx = lambda i, me_ref: (0, i, _slot(me_ref[0], paired))
        shape = (1, R, C * N_DEV)
    return pl.pallas_call(
        body, name=name,
        grid_spec=pltpu.PrefetchScalarGridSpec(
            num_scalar_prefetch=1, grid=(nr,),
            in_specs=[pl.BlockSpec((None, tr, C), lambda i, me_ref: (layer, i, 0))] + [ANY_SPEC] * len(deps),
            out_specs=pl.BlockSpec((None, tr, C), out_idx)),
        out_shape=jax.ShapeDtypeStruct(shape, BF16), compiler_params=_params(("parallel",)),
    )(me, w, *deps)


def _rms_fwd(x, g, name, deps=()):
    T, D = x.shape
    tr = _pick(T, (512, 256, 128))

    def body(x_ref, g_ref, *rest):
        o_ref = rest[-1]
        xv = x_ref[...]
        r = lax.rsqrt(jnp.mean(xv * xv, axis=-1, keepdims=True) + EPS)
        o_ref[...] = ((xv * r) * g_ref[...]).astype(BF16)

    return pl.pallas_call(
        body, name=name, grid=(T // tr,),
        in_specs=[pl.BlockSpec((tr, D), lambda i: (i, 0)), pl.BlockSpec((1, D), lambda i: (0, 0))]
        + [ANY_SPEC] * len(deps),
        out_specs=pl.BlockSpec((tr, D), lambda i: (i, 0)),
        out_shape=jax.ShapeDtypeStruct((T, D), BF16), compiler_params=_params(("parallel",)),
    )(x, g, *deps)


def _rms_bwd(x, g, dh, dres, name, deps=()):
    T, D = x.shape
    tr = _pick(T, (256, 128))

    def body(x_ref, g_ref, dh_ref, dres_ref, *rest):
        dx_ref, dxb_ref, dg_ref = rest[-3:]
        xv = x_ref[...]
        dhv = dh_ref[...]
        r = lax.rsqrt(jnp.mean(xv * xv, axis=-1, keepdims=True) + EPS)
        xhat = xv * r
        dxh = dhv * g_ref[...]
        c = jnp.mean(dxh * xhat, axis=-1, keepdims=True)
        dx = dres_ref[...] + r * (dxh - xhat * c)
        dx_ref[...] = dx
        dxb_ref[...] = dx.astype(BF16)

        @pl.when(pl.program_id(0) == 0)
        def _():
            dg_ref[...] = jnp.zeros_like(dg_ref)

        dg_ref[...] += jnp.sum(dhv * xhat, axis=0, keepdims=True)

    row = pl.BlockSpec((tr, D), lambda i: (i, 0))
    vec = pl.BlockSpec((1, D), lambda i: (0, 0))
    return pl.pallas_call(
        body, name=name, grid=(T // tr,),
        in_specs=[row, vec, row, row] + [ANY_SPEC] * len(deps), out_specs=[row, row, vec],
        out_shape=[jax.ShapeDtypeStruct((T, D), F32), jax.ShapeDtypeStruct((T, D), BF16),
                   jax.ShapeDtypeStruct((1, D), F32)],
        compiler_params=_params(("arbitrary",)),
    )(x, g, dh, dres, *deps)


N_PANEL = N_DEV // 2


def _ffn_in(h, w_in, name):
    T, D = h.shape
    F2 = w_in.shape[2]
    pw = F2 // N_PANEL
    half = pw // 2
    tm = _pick(T, (512, 256, 128))

    def body(h_ref, w_ref, u_ref, a_ref):
        r = jnp.dot(h_ref[...], w_ref[...], preferred_element_type=F32)
        u_ref[...] = r.astype(BF16)
        g, up = r[:, :half], r[:, half:]
        a_ref[...] = (g * jax.nn.sigmoid(g) * up).astype(BF16)

    return pl.pallas_call(
        body, name=name, grid=(N_PANEL, T // tm),
        in_specs=[pl.BlockSpec((tm, D), lambda p, i: (i, 0)), pl.BlockSpec((None, D, pw), lambda p, i: (0, 0, p))],
        out_specs=[pl.BlockSpec((tm, pw), lambda p, i: (i, p)), pl.BlockSpec((tm, half), lambda p, i: (i, p))],
        out_shape=[jax.ShapeDtypeStruct((T, F2), BF16), jax.ShapeDtypeStruct((T, F2 // 2), BF16)],
        compiler_params=_params(("parallel", "parallel")),
    )(h, w_in)


def _ffn_da(dxo, w_out, u, name, deps=()):
    T, D = dxo.shape
    F2 = u.shape[1]
    pw = F2 // N_PANEL
    half = pw // 2
    tm = _pick(T, (512, 256, 128))

    def body(d_ref, w_ref, u_ref, *rest):
        du_ref = rest[-1]
        da = 0.5 * lax.dot_general(d_ref[...], w_ref[...], NT_DIMS, preferred_element_type=F32)
        g = u_ref[:, :half].astype(F32)
        up = u_ref[:, half:].astype(F32)
        sg = jax.nn.sigmoid(g)
        silu = g * sg
        du_ref[:, :half] = (da * up * (sg + silu * (1.0 - sg))).astype(BF16)
        du_ref[:, half:] = (da * silu).astype(BF16)

    return pl.pallas_call(
        body, name=name, grid=(N_PANEL, T // tm),
        in_specs=[pl.BlockSpec((tm, D), lambda p, i: (i, 0)), pl.BlockSpec((None, half, D), lambda p, i: (0, p, 0)),
                  pl.BlockSpec((tm, pw), lambda p, i: (i, p))] + [ANY_SPEC] * len(deps),
        out_specs=pl.BlockSpec((tm, pw), lambda p, i: (i, p)),
        out_shape=jax.ShapeDtypeStruct((T, F2), BF16), compiler_params=_params(("parallel", "parallel")),
    )(dxo, w_out, u, *deps)


def _loss_head(y, t, name):
    T, D = y.shape
    tr = _pick(T, (512, 256, 128))

    def body(y_ref, t_ref, dy_ref, dyb_ref, l_ref):
        e = y_ref[...] - t_ref[...]
        dy = e * (1.0 / D)
        dy_ref[...] = dy
        dyb_ref[...] = dy.astype(BF16)

        @pl.when(pl.program_id(0) == 0)
        def _():
            l_ref[...] = jnp.zeros_like(l_ref)

        l_ref[...] += 0.5 * jnp.sum(jnp.mean(e * e, axis=-1, keepdims=True), axis=0, keepdims=True)

    row = pl.BlockSpec((tr, D), lambda i: (i, 0))
    return pl.pallas_call(
        body, name=name, grid=(T // tr,),
        in_specs=[row, row], out_specs=[row, row, pl.BlockSpec((1, 1), lambda i: (0, 0))],
        out_shape=[jax.ShapeDtypeStruct((T, D), F32), jax.ShapeDtypeStruct((T, D), BF16),
                   jax.ShapeDtypeStruct((1, 1), F32)],
        compiler_params=_params(("arbitrary",)),
    )(y, t)


def _rope_tables(S):
    half = ROPE_DIM // 2
    inv = 1.0 / (ROPE_THETA ** (jnp.arange(0, ROPE_DIM, 2, dtype=F32) / ROPE_DIM))
    ang = jnp.arange(S, dtype=F32)[:, None] * inv[None, :]
    cos, sin = jnp.cos(ang), jnp.sin(ang)
    z = jnp.zeros((S, half), F32)
    z2 = jnp.zeros((S, LANES - ROPE_DIM), F32)
    c = jnp.concatenate([cos, cos, z2], axis=1)
    s1 = jnp.concatenate([-sin, z, z2], axis=1)
    s2 = jnp.concatenate([z, sin, z2], axis=1)
    return c, s1, s2


def _rope(r, c, s1, s2):
    return r * c + pltpu.roll(r, LANES - ROPE_DIM // 2, 1) * s1 + pltpu.roll(r, ROPE_DIM // 2, 1) * s2


def _rope_t(d, c, s1, s2):
    return d * c + pltpu.roll(d * s1, ROPE_DIM // 2, 1) + pltpu.roll(d * s2, LANES - ROPE_DIM // 2, 1)


def _lat_norm_fwd(lat, g_cq, g_ckv, name):
    T = lat.shape[0]
    tr = _pick(T, (512, 256, 128))

    def body(lat_ref, gq_ref, gk_ref, cq_ref, ckv_ref):
        for off, g_ref, o_ref in ((0, gq_ref, cq_ref), (Q_LORA, gk_ref, ckv_ref)):
            xv = lat_ref[:, off:off + Q_LORA]
            r = lax.rsqrt(jnp.mean(xv * xv, axis=-1, keepdims=True) + EPS)
            o_ref[...] = ((xv * r) * g_ref[...]).astype(BF16)

    vec = pl.BlockSpec((1, Q_LORA), lambda i: (0, 0))
    out = pl.BlockSpec((tr, Q_LORA), lambda i: (i, 0))
    return pl.pallas_call(
        body, name=name, grid=(T // tr,),
        in_specs=[pl.BlockSpec((tr, LAT_PAD), lambda i: (i, 0)), vec, vec], out_specs=[out, out],
        out_shape=[jax.ShapeDtypeStruct((T, Q_LORA), BF16)] * 2, compiler_params=_params(("parallel",)),
    )(lat, g_cq, g_ckv)


def _lat_norm_bwd(lat, g_cq, g_ckv, dcq, dckv, dkpe, name):
    T = lat.shape[0]
    tr = _pick(T, (256, 128))

    def body(lat_ref, gq_ref, gk_ref, dcq_ref, dckv_ref, dkpe_ref, dlat_ref, dgq_ref, dgk_ref):
        @pl.when(pl.program_id(0) == 0)
        def _():
            dgq_ref[...] = jnp.zeros_like(dgq_ref)
            dgk_ref[...] = jnp.zeros_like(dgk_ref)

        for off, g_ref, d_ref, dg_ref in ((0, gq_ref, dcq_ref, dgq_ref), (Q_LORA, gk_ref, dckv_ref, dgk_ref)):
            xv = lat_ref[:, off:off + Q_LORA]
            dv = d_ref[...]
            r = lax.rsqrt(jnp.mean(xv * xv, axis=-1, keepdims=True) + EPS)
            xhat = xv * r
            dxh = dv * g_ref[...]
            c = jnp.mean(dxh * xhat, axis=-1, keepdims=True)
            dlat_ref[:, off:off + Q_LORA] = (r * (dxh - xhat * c)).astype(BF16)
            dg_ref[...] += jnp.sum(dv * xhat, axis=0, keepdims=True)
        dlat_ref[:, Q_LORA + KV_LORA:] = dkpe_ref[...].astype(BF16)

    vec = pl.BlockSpec((1, Q_LORA), lambda i: (0, 0))
    half = pl.BlockSpec((tr, Q_LORA), lambda i: (i, 0))
    full = pl.BlockSpec((tr, LAT_PAD), lambda i: (i, 0))
    return pl.pallas_call(
        body, name=name, grid=(T // tr,),
        in_specs=[full, vec, vec, half, half, pl.BlockSpec((tr, LANES), lambda i: (i, 0))],
        out_specs=[full, vec, vec],
        out_shape=[jax.ShapeDtypeStruct((T, LAT_PAD), BF16), jax.ShapeDtypeStruct((1, Q_LORA), F32),
                   jax.ShapeDtypeStruct((1, Q_LORA), F32)],
        compiler_params=_params(("arbitrary",)),
    )(lat, g_cq, g_ckv, dcq, dckv, dkpe)


def _mla_prep_fwd(q_raw, kv, lat, g_qn, g_kn, tabs, name):
    T = q_raw.shape[0]
    H = MLA_HEADS
    tr = _pick(T, (256, 128))
    scale = LOG2E / math.sqrt(QK_DIM)

    def body(q_ref, kv_ref, kpe_ref, gq_ref, gk_ref, c_ref, s1_ref, s2_ref, qf_ref, kf_ref, v_ref):
        c, s1, s2 = c_ref[...], s1_ref[...], s2_ref[...]
        gq, gk = gq_ref[...], gk_ref[...]
        kpe = kpe_ref[...]
        kpe_ss = jnp.sum(kpe * kpe, axis=-1, keepdims=True)
        for h in range(H):
            lo = h * HEAD_PAD
            qa = q_ref[:, lo:lo + LANES]
            qb = q_ref[:, lo + LANES:lo + HEAD_PAD]
            ss = jnp.sum(qa * qa, axis=-1, keepdims=True) + jnp.sum(qb * qb, axis=-1, keepdims=True)
            r = lax.rsqrt(ss * (1.0 / QK_DIM) + EPS)
            qf_ref[:, lo:lo + LANES] = (qa * r * gq[:, :LANES] * scale).astype(BF16)
            qf_ref[:, lo + LANES:lo + HEAD_PAD] = (_rope(qb * r * gq[:, LANES:], c, s1, s2) * scale).astype(BF16)
            ka = kv_ref[:, lo:lo + LANES]
            ss = jnp.sum(ka * ka, axis=-1, keepdims=True) + kpe_ss
            r = lax.rsqrt(ss * (1.0 / QK_DIM) + EPS)
            kf_ref[:, lo:lo + LANES] = (ka * r * gk[:, :LANES]).astype(BF16)
            kf_ref[:, lo + LANES:lo + HEAD_PAD] = _rope(kpe * r * gk[:, LANES:], c, s1, s2).astype(BF16)
            v_ref[:, lo:lo + V_DIM] = kv_ref[:, lo + LANES:lo + HEAD_PAD].astype(BF16)
            v_ref[:, lo + V_DIM:lo + HEAD_PAD] = jnp.ones((tr, HEAD_PAD - V_DIM), BF16)

    wide = pl.BlockSpec((tr, H * HEAD_PAD), lambda i: (i, 0))
    lane = pl.BlockSpec((tr, LANES), lambda i: (i, 0))
    gvec = pl.BlockSpec((1, HEAD_PAD), lambda i: (0, 0))
    return pl.pallas_call(
        body, name=name, grid=(T // tr,),
        in_specs=[wide, wide, pl.BlockSpec((tr, LANES), lambda i: (i, (Q_LORA + KV_LORA) // LANES)), gvec, gvec,
                  lane, lane, lane],
        out_specs=[wide, wide, wide],
        out_shape=[jax.ShapeDtypeStruct((T, H * HEAD_PAD), BF16)] * 3,
        compiler_params=_params(("parallel",)),
    )(q_raw, kv, lat, g_qn, g_kn, *tabs)


def _mla_prep_bwd(q_raw, kv, lat, g_qn, g_kn, tabs, dqf, dkf, dv, name):
    T = q_raw.shape[0]
    H = MLA_HEADS
    tr = _pick(T, (128,))

    def body(q_ref, kv_ref, kpe_ref, gq_ref, gk_ref, c_ref, s1_ref, s2_ref, dqf_ref, dkf_ref, dv_ref,
             dq_ref, dkv_ref, dkpe_ref, dgq_ref, dgk_ref):
        @pl.when(pl.program_id(0) == 0)
        def _():
            dgq_ref[...] = jnp.zeros_like(dgq_ref)
            dgk_ref[...] = jnp.zeros_like(dgk_ref)

        c, s1, s2 = c_ref[...], s1_ref[...], s2_ref[...]
        gq, gk = gq_ref[...], gk_ref[...]
        kpe = kpe_ref[...]
        kpe_ss = jnp.sum(kpe * kpe, axis=-1, keepdims=True)
        dkpe = jnp.zeros_like(kpe)
        dgq_a = jnp.zeros((1, LANES), F32)
        dgq_b = jnp.zeros((1, LANES), F32)
        dgk_a = jnp.zeros((1, LANES), F32)
        dgk_b = jnp.zeros((1, LANES), F32)
        for h in range(H):
            lo = h * HEAD_PAD
            xa = q_ref[:, lo:lo + LANES]
            xb = q_ref[:, lo + LANES:lo + HEAD_PAD]
            ss = jnp.sum(xa * xa, axis=-1, keepdims=True) + jnp.sum(xb * xb, axis=-1, keepdims=True)
            r = lax.rsqrt(ss * (1.0 / QK_DIM) + EPS)
            xa, xb = xa * r, xb * r
            da = dqf_ref[:, lo:lo + LANES].astype(F32)
            db = _rope_t(dqf_ref[:, lo + LANES:lo + HEAD_PAD].astype(F32), c, s1, s2)
            dgq_a += jnp.sum(da * xa, axis=0, keepdims=True)
            dgq_b += jnp.sum(db * xb, axis=0, keepdims=True)
            da, db = da * gq[:, :LANES], db * gq[:, LANES:]
            cc = (jnp.sum(da * xa, axis=-1, keepdims=True) + jnp.sum(db * xb, axis=-1, keepdims=True)) * (1.0 / QK_DIM)
            dq_ref[:, lo:lo + LANES] = (r * (da - xa * cc)).astype(BF16)
            dq_ref[:, lo + LANES:lo + HEAD_PAD] = (r * (db - xb * cc)).astype(BF16)
            xa = kv_ref[:, lo:lo + LANES]
            ss = jnp.sum(xa * xa, axis=-1, keepdims=True) + kpe_ss
            r = lax.rsqrt(ss * (1.0 / QK_DIM) + EPS)
            xa, xb = xa * r, kpe * r
            da = dkf_ref[:, lo:lo + LANES].astype(F32)
            db = _rope_t(dkf_ref[:, lo + LANES:lo + HEAD_PAD].astype(F32), c, s1, s2)
            dgk_a += jnp.sum(da * xa, axis=0, keepdims=True)
            dgk_b += jnp.sum(db * xb, axis=0, keepdims=True)
            da, db = da * gk[:, :LANES], db * gk[:, LANES:]
            cc = (jnp.sum(da * xa, axis=-1, keepdims=True) + jnp.sum(db * xb, axis=-1, keepdims=True)) * (1.0 / QK_DIM)
            dkv_ref[:, lo:lo + LANES] = (r * (da - xa * cc)).astype(BF16)
            dkpe = dkpe + r * (db - xb * cc)
            dkv_ref[:, lo + LANES:lo + HEAD_PAD] = dv_ref[:, h * V_DIM:(h + 1) * V_DIM].astype(BF16)
        dkpe_ref[...] = dkpe
        dgq_ref[:, :LANES] += dgq_a
        dgq_ref[:, LANES:] += dgq_b
        dgk_ref[:, :LANES] += dgk_a
        dgk_ref[:, LANES:] += dgk_b

    wide = pl.BlockSpec((tr, H * HEAD_PAD), lambda i: (i, 0))
    lane = pl.BlockSpec((tr, LANES), lambda i: (i, 0))
    gvec = pl.BlockSpec((1, HEAD_PAD), lambda i: (0, 0))
    vspec = pl.BlockSpec((tr, H * V_DIM), lambda i: (i, 0))
    return pl.pallas_call(
        body, name=name, grid=(T // tr,),
        in_specs=[wide, wide, pl.BlockSpec((tr, LANES), lambda i: (i, (Q_LORA + KV_LORA) // LANES)), gvec, gvec,
                  lane, lane, lane, wide, wide, vspec],
        out_specs=[wide, wide, lane, gvec, gvec],
        out_shape=[jax.ShapeDtypeStruct((T, H * HEAD_PAD), BF16), jax.ShapeDtypeStruct((T, H * HEAD_PAD), BF16),
                   jax.ShapeDtypeStruct((T, LANES), F32), jax.ShapeDtypeStruct((1, HEAD_PAD), F32),
                   jax.ShapeDtypeStruct((1, HEAD_PAD), F32)],
        compiler_params=_params(("arbitrary",)),
    )(q_raw, kv, lat, g_qn, g_kn, *tabs, dqf, dkf, dv)


def _causal_mask(tq, tk):
    return lax.broadcasted_iota(jnp.int32, (tq, tk), 1) <= lax.broadcasted_iota(jnp.int32, (tq, tk), 0)


NT_DIMS = (((1,), (1,)), ((), ()))
TN_DIMS = (((0,), (0,)), ((), ()))


def _flash_fwd(qf, kf, v, name):
    T = qf.shape[0]
    H, G = MLA_HEADS, FLASH_HEADS_FWD
    t = _pick(T, (512, 256, 128))
    n = T // t
    pairs = [(i, j) for i in range(n) for j in range(i + 1)]
    qi = jnp.asarray([p[0] for p in pairs], jnp.int32)
    kj = jnp.asarray([p[1] for p in pairs], jnp.int32)

    def body(qi_ref, kj_ref, q_ref, k_ref, v_ref, o_ref, lse_ref, *scratch):
        m_sc, acc_sc = scratch[:G], scratch[G:]
        sid = pl.program_id(1)
        i, j = qi_ref[sid], kj_ref[sid]

        @pl.when(j == 0)
        def _():
            for g in range(G):
                m_sc[g][...] = jnp.full_like(m_sc[g], NEG_BIG)
                acc_sc[g][...] = jnp.zeros_like(acc_sc[g])

        def step(masked):
            for g in range(G):
                qk = slice(g * HEAD_PAD, (g + 1) * HEAD_PAD)
                s = lax.dot_general(q_ref[:, qk], k_ref[:, qk], NT_DIMS, preferred_element_type=F32)
                if masked:
                    s = jnp.where(_causal_mask(t, t), s, NEG_BIG)
                m_prev = m_sc[g][:, :1]
                m_new = jnp.maximum(m_prev, jnp.max(s, axis=-1, keepdims=True))
                a = jnp.exp2(m_prev - m_new)
                p = jnp.exp2((s - m_new).astype(BF16))
                acc_sc[g][...] = a * acc_sc[g][...] + jnp.dot(p, v_ref[:, qk], preferred_element_type=F32)
                m_sc[g][...] = jnp.broadcast_to(m_new, (t, LANES))

        @pl.when(j < i)
        def _():
            step(False)

        @pl.when(j == i)
        def _():
            step(True)
            for g in range(G):
                vo = slice(g * V_DIM, (g + 1) * V_DIM)
                l = acc_sc[g][:, V_DIM:]
                o_ref[:, vo] = (acc_sc[g][:, :V_DIM] / l).astype(BF16)
                lse_ref[:, vo] = m_sc[g][...] + jnp.log2(l)

    row = pl.BlockSpec((t, G * V_DIM), lambda h, s, qi, kj: (qi[s], h))
    return pl.pallas_call(
        body, name=name,
        grid_spec=pltpu.PrefetchScalarGridSpec(
            num_scalar_prefetch=2, grid=(H // G, len(pairs)),
            in_specs=[pl.BlockSpec((t, G * HEAD_PAD), lambda h, s, qi, kj: (qi[s], h)),
                      pl.BlockSpec((t, G * HEAD_PAD), lambda h, s, qi, kj: (kj[s], h)),
                      pl.BlockSpec((t, G * HEAD_PAD), lambda h, s, qi, kj: (kj[s], h))],
            out_specs=[row, row],
            scratch_shapes=[pltpu.VMEM((t, LANES), F32)] * G + [pltpu.VMEM((t, HEAD_PAD), F32)] * G),
        out_shape=[jax.ShapeDtypeStruct((T, H * V_DIM), BF16), jax.ShapeDtypeStruct((T, H * V_DIM), F32)],
        compiler_params=_params(("parallel", "arbitrary")),
    )(qi, kj, qf, kf, v)


def _attn_delta(do, o, name):
    T, W = do.shape
    nh = W // V_DIM
    tr = _pick(T, (512, 256, 128))

    def body(do_ref, o_ref, d_ref):
        for h in range(nh):
            sl = slice(h * V_DIM, (h + 1) * V_DIM)
            d = jnp.sum(do_ref[:, sl].astype(F32) * o_ref[:, sl].astype(F32), axis=-1, keepdims=True)
            d_ref[:, sl] = jnp.broadcast_to(d, (tr, V_DIM))

    row = pl.BlockSpec((tr, W), lambda i: (i, 0))
    return pl.pallas_call(
        body, name=name, grid=(T // tr,), in_specs=[row, row], out_specs=row,
        out_shape=jax.ShapeDtypeStruct((T, W), F32), compiler_params=_params(("parallel",)),
    )(do, o)


def _flash_bwd(qf, kf, v, do, lse, delta, name):
    T = qf.shape[0]
    H, G = MLA_HEADS, FLASH_HEADS
    t = _pick(T, (512, 256, 128))
    n = T // t
    scale = 1.0 / math.sqrt(QK_DIM)
    pairs = [(i, j) for j in range(n) for i in range(j, n)]
    qi = jnp.asarray([p[0] for p in pairs], jnp.int32)
    kj = jnp.asarray([p[1] for p in pairs], jnp.int32)

    def body(qi_ref, kj_ref, q_ref, k_ref, v_ref, do_ref, lse_ref, dl_ref, dq_ref, dk_ref, dv_ref,
             dq_acc, dk_acc, dv_acc):
        sid = pl.program_id(1)
        i, j = qi_ref[sid], kj_ref[sid]

        @pl.when(sid == 0)
        def _():
            dq_acc[...] = jnp.zeros_like(dq_acc)

        def step(masked):
            rows = pl.ds(pl.multiple_of(i * t, t), t)
            for g in range(G):
                qk = slice(g * HEAD_PAD, (g + 1) * HEAD_PAD)
                vo = slice(g * V_DIM, (g + 1) * V_DIM)
                q, k, do_ = q_ref[:, qk], k_ref[:, qk], do_ref[:, vo]
                v_ = v_ref[:, g * HEAD_PAD:g * HEAD_PAD + V_DIM]
                s = lax.dot_general(q, k, NT_DIMS, preferred_element_type=F32)
                if masked:
                    s = jnp.where(_causal_mask(t, t), s, NEG_BIG)
                p = jnp.exp2(s - lse_ref[:, g * V_DIM:g * V_DIM + 1])
                dp = lax.dot_general(do_, v_, NT_DIMS, preferred_element_type=F32)
                ds = (p * (dp - dl_ref[:, g * V_DIM:g * V_DIM + 1])).astype(BF16)
                dv = lax.dot_general(p.astype(BF16), do_, TN_DIMS, preferred_element_type=F32)
                dk = lax.dot_general(ds, q, TN_DIMS, preferred_element_type=F32)
                if masked:
                    dv_acc[:, vo] = dv
                    dk_acc[:, qk] = dk
                else:
                    dv_acc[:, vo] += dv
                    dk_acc[:, qk] += dk
                dq_acc[rows, qk] += jnp.dot(ds, k, preferred_element_type=F32) * scale

        @pl.when(i == j)
        def _():
            step(True)

        @pl.when(i > j)
        def _():
            step(False)

        @pl.when(i == n - 1)
        def _():
            dk_ref[...] = (dk_acc[...] * LN2).astype(BF16)
            dv_ref[...] = dv_acc[...].astype(BF16)

        @pl.when(sid == len(pairs) - 1)
        def _():
            dq_ref[...] = dq_acc[...].astype(BF16)

    qs = pl.BlockSpec((t, G * HEAD_PAD), lambda h, s, qi, kj: (qi[s], h))
    rs = pl.BlockSpec((t, G * V_DIM), lambda h, s, qi, kj: (qi[s], h))
    ks = pl.BlockSpec((t, G * HEAD_PAD), lambda h, s, qi, kj: (kj[s], h))
    vs = pl.BlockSpec((t, G * V_DIM), lambda h, s, qi, kj: (kj[s], h))
    return pl.pallas_call(
        body, name=name,
        grid_spec=pltpu.PrefetchScalarGridSpec(
            num_scalar_prefetch=2, grid=(H // G, len(pairs)), in_specs=[qs, ks, ks, rs, rs, rs],
            out_specs=[pl.BlockSpec((T, G * HEAD_PAD), lambda h, s, qi, kj: (0, h)), ks, vs],
            scratch_shapes=[pltpu.VMEM((T, G * HEAD_PAD), F32), pltpu.VMEM((t, G * HEAD_PAD), F32),
                            pltpu.VMEM((t, G * V_DIM), F32)]),
        out_shape=[jax.ShapeDtypeStruct((T, H * HEAD_PAD), BF16), jax.ShapeDtypeStruct((T, H * HEAD_PAD), BF16),
                   jax.ShapeDtypeStruct((T, H * V_DIM), BF16)],
        compiler_params=_params(("parallel", "arbitrary")),
    )(qi, kj, qf, kf, v, do, lse, delta)


def _alibi_slopes():
    tot = DIL_GROUPS * DIL_HEADS
    return [float(np.float32(2.0) ** (np.float32(-8.0) * np.float32(k) / np.float32(tot))) for k in range(1, tot + 1)]


def _dil_masks():
    iq = lax.broadcasted_iota(jnp.int32, (DIL_BLK, DIL_BLK), 0)
    ik = lax.broadcasted_iota(jnp.int32, (DIL_BLK, DIL_BLK), 1)
    return (ik >= iq), (iq + DIL_BLK - ik).astype(F32), (ik <= iq), (iq - ik).astype(F32)


def _dil_norm(x, g):
    r = lax.rsqrt(jnp.mean(x * x, axis=-1, keepdims=True) + EPS)
    return x * r, r


DIL_SUPER = 8
BNT_DIMS = (((2,), (2,)), ((0,), (0,)))
BNN_DIMS = (((2,), (1,)), ((0,), (0,)))
BTN_DIMS = (((1,), (1,)), ((0,), (0,)))


def _dil_chunk(it, nb, d):
    assert nb & (nb - 1) == 0, nb
    r, n = it >> (nb.bit_length() - 1), it & (nb - 1)
    if d > 1:
        tok = pl.ds(n * (d * DIL_BLK) + r, DIL_BLK, stride=d)
    else:
        tok = pl.ds(pl.multiple_of(it * DIL_BLK, DIL_BLK), DIL_BLK)
    return tok, pl.ds(pl.multiple_of((it + 1) * DIL_BLK, DIL_BLK), DIL_BLK)


def _dil_token_rows(bidx, nb, d):
    r, n = divmod(bidx, nb)
    return pl.ds(n * DIL_BLK * d + r, DIL_BLK, stride=d) if d > 1 else pl.ds(bidx * DIL_BLK, DIL_BLK)


def _dil_super_rows(ss):
    base = (1 + ss * DIL_SUPER) * DIL_BLK
    return pl.ds(base, DIL_SUPER * DIL_BLK), pl.ds(base - DIL_BLK, DIL_SUPER * DIL_BLK)


def _dil_b3(x):
    return x.reshape(DIL_SUPER, DIL_BLK, x.shape[-1])


def _dil_scores(q3, kc3, kp3, slope, d, ss, nb):
    ok_p, dist_p, ok_c, dist_c = _dil_masks()
    scale = 1.0 / math.sqrt(DIL_HEAD_DIM)
    bias_p = jnp.where(ok_p, -slope * d * dist_p, NEG_BIG)
    bias_c = jnp.where(ok_c, -slope * d * dist_c, NEG_BIG)
    s_c = lax.dot_general(q3, kc3, BNT_DIMS, preferred_element_type=F32) * scale + bias_c[None]
    s_p = lax.dot_general(q3, kp3, BNT_DIMS, preferred_element_type=F32) * scale + bias_p[None]
    bidx = ss * DIL_SUPER + lax.broadcasted_iota(jnp.int32, s_p.shape, 0)
    s_p = jnp.where((bidx & (nb - 1)) == 0, NEG_BIG, s_p)
    return s_c, s_p


def _dil_fwd(qkv, g_qn, g_kn, slopes, name):
    T = qkv.shape[0]
    GH = DIL_GROUPS * DIL_HEADS
    scale = 1.0 / math.sqrt(DIL_HEAD_DIM)

    def body(sl_ref, q_ref, k_ref, v_ref, gq_ref, gk_ref, o_ref, lse_ref, qn_pm, kn_pm, v_pm):
        gh = pl.program_id(0)
        slope = sl_ref[gh]
        gq, gk = gq_ref[...], gk_ref[...]
        pad = pl.ds(0, DIL_BLK)
        kn_pm[pad, :] = jnp.zeros((DIL_BLK, DIL_HEAD_DIM), BF16)
        v_pm[pad, :] = jnp.zeros((DIL_BLK, DIL_HEAD_DIM), BF16)
        for g, (_, d) in enumerate(DIL_PAIRS):
            @pl.when((gh >= g * DIL_HEADS) & (gh < (g + 1) * DIL_HEADS))
            def _(d=d):
                nb = T // (d * DIL_BLK)

                def fill(it, _):
                    tok, dst = _dil_chunk(it, nb, d)
                    qn_pm[dst, :] = (_dil_norm(q_ref[tok, :], gq)[0] * gq).astype(BF16)
                    kn_pm[dst, :] = (_dil_norm(k_ref[tok, :], gk)[0] * gk).astype(BF16)
                    v_pm[dst, :] = v_ref[tok, :].astype(BF16)
                    return 0
                lax.fori_loop(0, T // DIL_BLK, fill, 0, unroll=4)

                for ss in range(T // DIL_BLK // DIL_SUPER):
                    cur, prv = _dil_super_rows(ss)
                    q3, kc3, kp3 = _dil_b3(qn_pm[cur, :]), _dil_b3(kn_pm[cur, :]), _dil_b3(kn_pm[prv, :])
                    s_c, s_p = _dil_scores(q3, kc3, kp3, slope, d, ss, nb)
                    m = jnp.max(jnp.maximum(s_c, s_p), axis=-1, keepdims=True)
                    p_c = jnp.exp(s_c - m)
                    p_p = jnp.exp(s_p - m)
                    l = jnp.sum(p_c, axis=-1, keepdims=True) + jnp.sum(p_p, axis=-1, keepdims=True)
                    acc = lax.dot_general(p_c.astype(BF16), _dil_b3(v_pm[cur, :]), BNN_DIMS, preferred_element_type=F32)
                    acc += lax.dot_general(p_p.astype(BF16), _dil_b3(v_pm[prv, :]), BNN_DIMS, preferred_element_type=F32)
                    o3 = acc / l
                    lse3 = jnp.broadcast_to(m + jnp.log(l), o3.shape)
                    for b in range(DIL_SUPER):
                        tok = _dil_token_rows(ss * DIL_SUPER + b, nb, d)
                        o_ref[tok, :] = o3[b]
                        lse_ref[tok, :] = lse3[b]

    col = lambda off: pl.BlockSpec((T, DIL_HEAD_DIM), lambda gh, sl: (0, gh + off))
    gvec = pl.BlockSpec((1, DIL_HEAD_DIM), lambda gh, sl: (0, 0))
    return pl.pallas_call(
        body, name=name,
        grid_spec=pltpu.PrefetchScalarGridSpec(
            num_scalar_prefetch=1, grid=(GH,),
            in_specs=[col(0), col(GH), col(2 * GH), gvec, gvec], out_specs=[col(0), col(0)],
            scratch_shapes=[pltpu.VMEM((DIL_BLK + T, DIL_HEAD_DIM), BF16)] * 3),
        out_shape=[jax.ShapeDtypeStruct((T, GH * DIL_HEAD_DIM), F32)] * 2,
        compiler_params=_params(("parallel",)),
    )(slopes, qkv, qkv, qkv, g_qn, g_kn)


def _dil_merge(o_g, lse_g, name):
    T = o_g.shape[0]
    W = DIL_HEADS * DIL_HEAD_DIM
    tr = _pick(T, (256, 128))

    def body(o0, o1, o2, l0, l1, l2, o_ref, lse_ref):
        a, b, c = l0[...], l1[...], l2[...]
        m = jnp.maximum(jnp.maximum(a, b), c)
        ea, eb, ec = jnp.exp(a - m), jnp.exp(b - m), jnp.exp(c - m)
        tot = ea + eb + ec
        o_ref[...] = ((o0[...] * ea + o1[...] * eb + o2[...] * ec) / tot).astype(BF16)
        lse_ref[...] = m + jnp.log(tot)

    grp = lambda g: pl.BlockSpec((tr, W), lambda i: (i, g))
    out = pl.BlockSpec((tr, W), lambda i: (i, 0))
    return pl.pallas_call(
        body, name=name, grid=(T // tr,),
        in_specs=[grp(0), grp(1), grp(2), grp(0), grp(1), grp(2)], out_specs=[out, out],
        out_shape=[jax.ShapeDtypeStruct((T, W), BF16), jax.ShapeDtypeStruct((T, W), F32)],
        compiler_params=_params(("parallel",)),
    )(o_g, o_g, o_g, lse_g, lse_g, lse_g)


def _dil_bwd(qkv, g_qn, g_kn, slopes, do, delta, lse, name):
    T = qkv.shape[0]
    GH = DIL_GROUPS * DIL_HEADS
    scale = 1.0 / math.sqrt(DIL_HEAD_DIM)
    nchunk = T // DIL_BLK

    def body(sl_ref, q_ref, k_ref, v_ref, gq_ref, gk_ref, do_ref, dl_ref, lse_ref,
             dq_ref, dk_ref, dv_ref, dgq_ref, dgk_ref,
             qn_pm, kn_pm, v_pm, do_pm, lse_pm, dl_pm, dq_pm, dk_pm, dv_pm, tok_sc):
        gh = pl.program_id(0)
        slope = sl_ref[gh]
        gq, gk = gq_ref[...], gk_ref[...]

        @pl.when(gh == 0)
        def _():
            dgq_ref[...] = jnp.zeros_like(dgq_ref)
            dgk_ref[...] = jnp.zeros_like(dgk_ref)

        pad = pl.ds(0, DIL_BLK)
        kn_pm[pad, :] = jnp.zeros((DIL_BLK, DIL_HEAD_DIM), BF16)
        v_pm[pad, :] = jnp.zeros((DIL_BLK, DIL_HEAD_DIM), BF16)
        dk_pm[...] = jnp.zeros_like(dk_pm)
        dv_pm[...] = jnp.zeros_like(dv_pm)
        for g, (_, d) in enumerate(DIL_PAIRS):
            @pl.when((gh >= g * DIL_HEADS) & (gh < (g + 1) * DIL_HEADS))
            def _(d=d):
                nb = T // (d * DIL_BLK)

                def fill(it, _):
                    tok, dst = _dil_chunk(it, nb, d)
                    qn_pm[dst, :] = (_dil_norm(q_ref[tok, :], gq)[0] * gq).astype(BF16)
                    kn_pm[dst, :] = (_dil_norm(k_ref[tok, :], gk)[0] * gk).astype(BF16)
                    v_pm[dst, :] = v_ref[tok, :].astype(BF16)
                    do_pm[dst, :] = do_ref[tok, :].astype(BF16)
                    lse_pm[dst, :] = lse_ref[tok, :]
                    dl_pm[dst, :] = dl_ref[tok, :]
                    return 0
                lax.fori_loop(0, nchunk, fill, 0, unroll=4)

                for ss in range(nchunk // DIL_SUPER):
                    cur, prv = _dil_super_rows(ss)
                    q3, kc3, kp3 = _dil_b3(qn_pm[cur, :]), _dil_b3(kn_pm[cur, :]), _dil_b3(kn_pm[prv, :])
                    vc3, vp3, do3 = _dil_b3(v_pm[cur, :]), _dil_b3(v_pm[prv, :]), _dil_b3(do_pm[cur, :])
                    ls = _dil_b3(lse_pm[cur, :])[:, :, :1]
                    delta = _dil_b3(dl_pm[cur, :])[:, :, :1]
                    s_c, s_p = _dil_scores(q3, kc3, kp3, slope, d, ss, nb)
                    p_c = jnp.exp(s_c - ls)
                    p_p = jnp.exp(s_p - ls)
                    dp_c = lax.dot_general(do3, vc3, BNT_DIMS, preferred_element_type=F32)
                    dp_p = lax.dot_general(do3, vp3, BNT_DIMS, preferred_element_type=F32)
                    ds_c = (p_c * (dp_c - delta)).astype(BF16)
                    ds_p = (p_p * (dp_p - delta)).astype(BF16)
                    dq3 = (lax.dot_general(ds_c, kc3, BNN_DIMS, preferred_element_type=F32)
                           + lax.dot_general(ds_p, kp3, BNN_DIMS, preferred_element_type=F32)) * scale
                    flat = lambda x: x.reshape(DIL_SUPER * DIL_BLK, DIL_HEAD_DIM)
                    dq_pm[cur, :] = flat(dq3)
                    dk_pm[cur, :] += flat(lax.dot_general(ds_c, q3, BTN_DIMS, preferred_element_type=F32)) * scale
                    dv_pm[cur, :] += flat(lax.dot_general(p_c.astype(BF16), do3, BTN_DIMS, preferred_element_type=F32))
                    dk_pm[prv, :] += flat(lax.dot_general(ds_p, q3, BTN_DIMS, preferred_element_type=F32)) * scale
                    dv_pm[prv, :] += flat(lax.dot_general(p_p.astype(BF16), do3, BTN_DIMS, preferred_element_type=F32))

                def to_tokens(src_pm):
                    def move(it, _):
                        tok, src = _dil_chunk(it, nb, d)
                        tok_sc[tok, :] = src_pm[src, :]
                        return 0
                    lax.fori_loop(0, nchunk, move, 0, unroll=4)

                def norm_bwd(x_ref, gvec, out_ref):
                    big = 4 * DIL_BLK

                    def fin(ci, dg):
                        rows = pl.ds(pl.multiple_of(ci * big, big), big)
                        xhat, r = _dil_norm(x_ref[rows, :], gvec)
                        dn = tok_sc[rows, :]
                        dxh = dn * gvec
                        c = jnp.mean(dxh * xhat, axis=-1, keepdims=True)
                        out_ref[rows, :] = (r * (dxh - xhat * c)).astype(BF16)
                        return dg + jnp.sum(dn * xhat, axis=0, keepdims=True)
                    return lax.fori_loop(0, T // big, fin, jnp.zeros((1, DIL_HEAD_DIM), F32))

                to_tokens(dq_pm)
                dgq_ref[...] += norm_bwd(q_ref, gq, dq_ref)
                to_tokens(dk_pm)
                dgk_ref[...] += norm_bwd(k_ref, gk, dk_ref)
                to_tokens(dv_pm)
                dv_ref[...] = tok_sc[...].astype(BF16)

    col = lambda off: pl.BlockSpec((T, DIL_HEAD_DIM), lambda gh, sl: (0, gh + off))
    hcol = pl.BlockSpec((T, DIL_HEAD_DIM), lambda gh, sl: (0, gh % DIL_HEADS))
    gvec = pl.BlockSpec((1, DIL_HEAD_DIM), lambda gh, sl: (0, 0))
    wide = jax.ShapeDtypeStruct((T, GH * DIL_HEAD_DIM), BF16)
    vec = jax.ShapeDtypeStruct((1, DIL_HEAD_DIM), F32)
    pm = lambda dt: pltpu.VMEM((DIL_BLK + T, DIL_HEAD_DIM), dt)
    return pl.pallas_call(
        body, name=name,
        grid_spec=pltpu.PrefetchScalarGridSpec(
            num_scalar_prefetch=1, grid=(GH,),
            in_specs=[col(0), col(GH), col(2 * GH), gvec, gvec, hcol, hcol, hcol],
            out_specs=[col(0), col(0), col(0), gvec, gvec],
            scratch_shapes=[pm(BF16)] * 4 + [pm(F32)] * 5 + [pltpu.VMEM((T, DIL_HEAD_DIM), F32)]),
        out_shape=[wide, wide, wide, vec, vec],
        compiler_params=_params(("arbitrary",)),
    )(slopes, qkv, qkv, qkv, g_qn, g_kn, do, delta, lse)


def _my_pos():
    return lax.axis_index("x"), lax.axis_index("y"), lax.axis_index("c")


def _peer(pos, j):
    x, y, c = pos
    px = 1 - x if j & 4 else x
    py = 1 - y if j & 2 else y
    pc = 1 - c if j & 1 else c
    return (px, py, pc), 4 * px + 2 * py + pc


def _slot(idx, paired):
    if not paired:
        return idx
    return jnp.where(idx < N_DEV // 2, 2 * idx, 2 * idx - (N_DEV - 1))


def _shard_slice(ref, axis, idx, size, paired=False):
    sl = [slice(None)] * len(ref.shape)
    sl[axis] = pl.ds(pl.multiple_of(_slot(idx, paired) * size, 8), size)
    return ref.at[tuple(sl)]


HBM_SPEC = pl.BlockSpec(memory_space=pltpu.HBM)
SEM_SPEC = pl.BlockSpec(memory_space=pltpu.SEMAPHORE)
DATAFLOW = pltpu.SideEffectType.DATAFLOW_SIDE_EFFECTING
N_PEER = N_DEV - 1


def _scatter_copy(axis, grad, slots, frm, to, dev, send_sem, recv_sem):
    ax, paired = axis
    src = _shard_slice(grad, ax, to, grad.shape[ax] // N_DEV, paired)
    return pltpu.make_async_remote_copy(src_ref=src, dst_ref=slots.at[frm], send_sem=send_sem, recv_sem=recv_sem,
                                        device_id=dev, device_id_type=MESH)


def _scatter_start(grads, axes, name):
    n = len(grads)

    def body(*refs):
        outs = refs[2 * n:]
        send, recv, token = outs[:n], outs[n:2 * n], outs[4 * n]
        pos = _my_pos()
        me = 4 * pos[0] + 2 * pos[1] + pos[2]
        for a in range(n):
            for j in range(1, N_DEV):
                dev, pid = _peer(pos, j)
                _scatter_copy(axes[a], refs[2 * a], refs[2 * a + 1], me, pid, dev, send[a].at[j - 1],
                              recv[a].at[j - 1]).start()
        token[...] = jnp.zeros_like(token)

    ops = []
    for g, (ax, _) in zip(grads, axes):
        shp = list(g.shape)
        shp[ax] //= N_DEV
        ops += [g, lax.empty((N_DEV,) + tuple(shp), g.dtype)]
    sems = [pltpu.SemaphoreType.DMA((N_PEER,))] * (2 * n)
    res = pl.pallas_call(
        body, name=name,
        out_shape=sems + [pltpu.HBM(o.shape, o.dtype) for o in ops] + [jax.ShapeDtypeStruct((8, LANES), F32)],
        in_specs=[HBM_SPEC] * len(ops),
        out_specs=[SEM_SPEC] * (2 * n) + [HBM_SPEC] * len(ops) + [pl.BlockSpec(memory_space=pltpu.VMEM)],
        input_output_aliases={i: 2 * n + i for i in range(len(ops))},
        compiler_params=pltpu.CompilerParams(has_side_effects=DATAFLOW),
    )(*[pltpu.with_memory_space_constraint(o, pltpu.HBM) for o in ops])
    items = [(res[a], res[n + a], res[2 * n + 2 * a], res[2 * n + 2 * a + 1]) for a in range(n)]
    return items, res[4 * n]


def _scatter_wait(items, axes, after, name):
    n = len(items)

    def body(*refs):
        send, recv = refs[2 * n:3 * n], refs[3 * n:4 * n]
        pos = _my_pos()
        me = 4 * pos[0] + 2 * pos[1] + pos[2]
        for a in range(n):
            for j in range(1, N_DEV):
                dev, pid = _peer(pos, j)
                cp = _scatter_copy(axes[a], refs[2 * a], refs[2 * a + 1], pid, me, dev, send[a].at[j - 1],
                                   recv[a].at[j - 1])
                cp.wait_send()
                cp.wait_recv()

    ops = [b for it in items for b in it[2:]]
    res = pl.pallas_call(
        body, name=name,
        out_shape=[pltpu.HBM(o.shape, o.dtype) for o in ops],
        in_specs=[HBM_SPEC] * len(ops) + [SEM_SPEC] * (2 * n) + [ANY_SPEC],
        out_specs=[HBM_SPEC] * len(ops),
        input_output_aliases={i: i for i in range(len(ops))},
        compiler_params=pltpu.CompilerParams(has_side_effects=DATAFLOW),
    )(*ops, *[it[0] for it in items], *[it[1] for it in items], after)
    return [(res[2 * a], res[2 * a + 1]) for a in range(n)]


SIBLING = 1
ICI_PEERS = (2, 4, 6)


def _gather_copy(buf, axis, shard, dev, send_sem, recv_sem):
    ax, paired = axis
    piece = _shard_slice(buf, ax, shard, buf.shape[ax] // N_DEV, paired)
    return pltpu.make_async_remote_copy(src_ref=piece, dst_ref=piece, send_sem=send_sem, recv_sem=recv_sem,
                                        device_id=dev, device_id_type=MESH)


def _gather_start(bufs, axes, name):
    n = len(bufs)

    def body(*refs):
        ins, outs = refs[:n], refs[n:]
        send, r_sib, r_ici, token = outs[:n], outs[n:2 * n], outs[2 * n:3 * n], outs[4 * n]
        pos = _my_pos()
        me = 4 * pos[0] + 2 * pos[1] + pos[2]
        for a in range(n):
            dev, _ = _peer(pos, SIBLING)
            _gather_copy(ins[a], axes[a], me, dev, send[a].at[0], r_sib[a].at[0]).start()
            for k, j in enumerate(ICI_PEERS):
                dev, _ = _peer(pos, j)
                _gather_copy(ins[a], axes[a], me, dev, send[a].at[1 + k], r_ici[a].at[k]).start()
        token[...] = jnp.zeros_like(token)

    sems = ([pltpu.SemaphoreType.DMA((1 + len(ICI_PEERS),))] * n + [pltpu.SemaphoreType.DMA((1,))] * n
            + [pltpu.SemaphoreType.DMA((len(ICI_PEERS),))] * n)
    res = pl.pallas_call(
        body, name=name,
        out_shape=sems + [pltpu.HBM(b.shape, b.dtype) for b in bufs] + [jax.ShapeDtypeStruct((8, LANES), F32)],
        in_specs=[HBM_SPEC] * n,
        out_specs=[SEM_SPEC] * (3 * n) + [HBM_SPEC] * n + [pl.BlockSpec(memory_space=pltpu.VMEM)],
        input_output_aliases={i: 3 * n + i for i in range(n)},
        compiler_params=pltpu.CompilerParams(has_side_effects=DATAFLOW),
    )(*[pltpu.with_memory_space_constraint(b, pltpu.HBM) for b in bufs])
    items = [dict(send=res[a], r_sib=res[n + a], r_ici=res[2 * n + a], buf=res[3 * n + a]) for a in range(n)]
    return items, res[4 * n]


def _gather_relay(items, axes, after, name):
    n = len(items)

    def body(*refs):
        ins, r_ici = refs[:n], refs[n:2 * n]
        outs = refs[2 * n + 1:]
        s_rel, r_rel, token = outs[:n], outs[n:2 * n], outs[3 * n]
        pos = _my_pos()
        sib, _ = _peer(pos, SIBLING)
        for a in range(n):
            for k, j in enumerate(ICI_PEERS):
                dev, pid = _peer(pos, j)
                _gather_copy(ins[a], axes[a], pid, dev, s_rel[a].at[k], r_ici[a].at[k]).wait_recv()
                _gather_copy(ins[a], axes[a], pid, sib, s_rel[a].at[k], r_rel[a].at[k]).start()
        token[...] = jnp.zeros_like(token)

    bufs = [it["buf"] for it in items]
    sems = [pltpu.SemaphoreType.DMA((len(ICI_PEERS),))] * (2 * n)
    res = pl.pallas_call(
        body, name=name,
        out_shape=sems + [pltpu.HBM(b.shape, b.dtype) for b in bufs] + [jax.ShapeDtypeStruct((8, LANES), F32)],
        in_specs=[HBM_SPEC] * n + [SEM_SPEC] * n + [ANY_SPEC],
        out_specs=[SEM_SPEC] * (2 * n) + [HBM_SPEC] * n + [pl.BlockSpec(memory_space=pltpu.VMEM)],
        input_output_aliases={i: 2 * n + i for i in range(n)},
        compiler_params=pltpu.CompilerParams(has_side_effects=DATAFLOW),
    )(*bufs, *[it["r_ici"] for it in items], after)
    out = [dict(send=it["send"], r_sib=it["r_sib"], s_rel=res[a], r_rel=res[n + a], buf=res[2 * n + a])
           for a, it in enumerate(items)]
    return out, res[3 * n]


def _gather_wait(items, axes, after, name):
    n = len(items)

    def body(*refs):
        ins = refs[:n]
        send, r_sib, s_rel, r_rel = (refs[(1 + q) * n:(2 + q) * n] for q in range(4))
        pos = _my_pos()
        me = 4 * pos[0] + 2 * pos[1] + pos[2]
        sib, sib_id = _peer(pos, SIBLING)
        for a in range(n):
            for k in range(1 + len(ICI_PEERS)):
                _gather_copy(ins[a], axes[a], me, sib, send[a].at[k], r_sib[a].at[0]).wait_send()
            _gather_copy(ins[a], axes[a], sib_id, sib, send[a].at[0], r_sib[a].at[0]).wait_recv()
            for k, j in enumerate(ICI_PEERS):
                _, pid = _peer(pos, j)
                _, far = _peer(pos, j ^ SIBLING)
                _gather_copy(ins[a], axes[a], pid, sib, s_rel[a].at[k], r_rel[a].at[k]).wait_send()
                _gather_copy(ins[a], axes[a], far, sib, s_rel[a].at[k], r_rel[a].at[k]).wait_recv()

    bufs = [it["buf"] for it in items]
    res = pl.pallas_call(
        body, name=name,
        out_shape=[pltpu.HBM(b.shape, b.dtype) for b in bufs],
        in_specs=[HBM_SPEC] * n + [SEM_SPEC] * (4 * n) + [ANY_SPEC],
        out_specs=[HBM_SPEC] * n,
        input_output_aliases={i: i for i in range(n)},
        compiler_params=pltpu.CompilerParams(has_side_effects=DATAFLOW),
    )(*bufs, *[it["send"] for it in items], *[it["r_sib"] for it in items], *[it["s_rel"] for it in items],
      *[it["r_rel"] for it in items], after)
    return list(res)


def _gain_allreduce(v, name):
    n = v.shape[1]

    def body(v_ref, o_ref, slots, send_sems, recv_sems):
        pos = _my_pos()
        me = 4 * pos[0] + 2 * pos[1] + pos[2]
        slots[me] = v_ref[...]
        copies = []
        for j in range(1, N_DEV):
            dev, _ = _peer(pos, j)
            cp = pltpu.make_async_remote_copy(
                src_ref=slots.at[me], dst_ref=slots.at[me], send_sem=send_sems.at[j], recv_sem=recv_sems.at[j],
                device_id=dev, device_id_type=MESH)
            cp.start()
            copies.append(cp)
        for j in range(1, N_DEV):
            dev, pid = _peer(pos, j)
            pltpu.make_async_remote_copy(
                src_ref=slots.at[me], dst_ref=slots.at[pid], send_sem=send_sems.at[j], recv_sem=recv_sems.at[j],
                device_id=dev, device_id_type=MESH).wait_recv()
        for cp in copies:
            cp.wait_send()
        acc = slots[0]
        for s in range(1, N_DEV):
            acc = acc + slots[s]
        o_ref[...] = acc

    return pl.pallas_call(
        body, name=name, out_shape=jax.ShapeDtypeStruct((1, n), F32),
        in_specs=[pl.BlockSpec(memory_space=pltpu.VMEM)], out_specs=pl.BlockSpec(memory_space=pltpu.VMEM),
        scratch_shapes=[pltpu.VMEM((N_DEV, 1, n), F32), pltpu.SemaphoreType.DMA((N_DEV,)),
                        pltpu.SemaphoreType.DMA((N_DEV,))],
        compiler_params=pltpu.CompilerParams(has_side_effects=True),
    )(v)


def _adamw(parts, own, me, w, m, v, layer, prev, name, own_axis=None):
    L, R, C = w.shape
    P = parts.shape[0]
    tr = _pick(R, (128, 64, 32, 16, 8, 1))
    c1 = 1.0 - ADAM_B1 ** ADAM_STEP
    c2 = 1.0 - ADAM_B2 ** ADAM_STEP
    n_in = 4 if own is None else 5

    def body(me_ref, *refs):
        p_ref = refs[0]
        w_ref, m_ref, v_ref = refs[n_in - 3:n_in]
        g_out, d_out, m_out, v_out, tok = refs[-5:]
        g = None
        for s in range(P):
            part = p_ref[s]
            if own is not None:
                part = jnp.where(me_ref[0] == s, refs[1][...], part)
            g = part.astype(F32) if g is None else g + part.astype(F32)
        mn = ADAM_B1 * m_ref[...] + (1.0 - ADAM_B1) * g
        vn = ADAM_B2 * v_ref[...] + (1.0 - ADAM_B2) * (g * g)
        g_out[...] = g
        m_out[...] = mn
        v_out[...] = vn
        d_out[...] = -ADAM_LR * ((mn / c1) / (jnp.sqrt(vn / c2) + ADAM_EPS) + ADAM_WD * w_ref[...])
        tok[...] = jnp.zeros_like(tok)

    row = pl.BlockSpec((None, tr, C), lambda i, me_ref: (layer, i, 0))
    in_specs = [pl.BlockSpec((P, tr, C), lambda i, me_ref: (0, i, 0))]
    args = [parts]
    if own is not None:
        if own_axis is None:
            own_idx = lambda i, me_ref: (i, 0)
        elif own_axis[0] == 0:
            own_idx = lambda i, me_ref: (_slot(me_ref[0], own_axis[1]) * (R // tr) + i, 0)
        else:
            own_idx = lambda i, me_ref: (i, _slot(me_ref[0], own_axis[1]))
        in_specs.append(pl.BlockSpec((tr, C), own_idx))
        args.append(own)
    in_specs += [row, row, row]
    args += [w, m, v]
    aliases = {}
    if prev is not None:
        in_specs += [ANY_SPEC] * 4
        aliases = {1 + len(args) + k: k for k in range(4)}
        args += list(prev)
    shp = jax.ShapeDtypeStruct((L, R, C), F32)
    res = pl.pallas_call(
        body, name=name,
        grid_spec=pltpu.PrefetchScalarGridSpec(
            num_scalar_prefetch=1, grid=(R // tr,), in_specs=in_specs,
            out_specs=[row] * 4 + [pl.BlockSpec((8, LANES), lambda i, me_ref: (0, 0))]),
        out_shape=[shp] * 4 + [jax.ShapeDtypeStruct((8, LANES), F32)],
        input_output_aliases=aliases, compiler_params=_params(("arbitrary",)),
    )(me, *args)
    return res[:4], res[4]


def _pad_heads(w):
    lead = w.shape[:-1]
    n = w.shape[-1] // QK_DIM
    w = w.reshape(lead + (n, QK_DIM))
    w = jnp.pad(w, [(0, 0)] * len(lead) + [(0, 0), (0, HEAD_PAD - QK_DIM)])
    return w.reshape(lead + (n * HEAD_PAD,))


def _unpad_heads(w):
    lead = w.shape[:-1]
    n = w.shape[-1] // HEAD_PAD
    return w.reshape(lead + (n, HEAD_PAD))[..., :QK_DIM].reshape(lead + (n * QK_DIM,))


def kernel(x, ffn1_norm, ffn1_w_in, ffn1_w_out, mix_norm, ffn2_norm, ffn2_w_in, ffn2_w_out, mla_w_down, mla_g_cq, mla_g_ckv, mla_w_uq, mla_w_ukv, mla_g_qn, mla_g_kn, mla_w_o, dil_w_qkv, dil_g_qn, dil_g_kn, dil_w_o, loss_target, m_ffn1_norm, m_ffn1_w_in, m_ffn1_w_out, m_mix_norm, m_ffn2_norm, m_ffn2_w_in, m_ffn2_w_out, m_mla_w_down, m_mla_g_cq, m_mla_g_ckv, m_mla_w_uq, m_mla_w_ukv, m_mla_g_qn, m_mla_g_kn, m_mla_w_o, m_dil_w_qkv, m_dil_g_qn, m_dil_g_kn, m_dil_w_o, v_ffn1_norm, v_ffn1_w_in, v_ffn1_w_out, v_mix_norm, v_ffn2_norm, v_ffn2_w_in, v_ffn2_w_out, v_mla_w_down, v_mla_g_cq, v_mla_g_ckv, v_mla_w_uq, v_mla_w_ukv, v_mla_g_qn, v_mla_g_kn, v_mla_w_o, v_dil_w_qkv, v_dil_g_qn, v_dil_g_kn, v_dil_w_o):
    names = ["ffn1_norm", "ffn1_w_in", "ffn1_w_out", "mix_norm", "ffn2_norm", "ffn2_w_in", "ffn2_w_out", "mla_w_down",
             "mla_g_cq", "mla_g_ckv", "mla_w_uq", "mla_w_ukv", "mla_g_qn", "mla_g_kn", "mla_w_o", "dil_w_qkv",
             "dil_g_qn", "dil_g_kn", "dil_w_o"]
    W = dict(zip(names, [ffn1_norm, ffn1_w_in, ffn1_w_out, mix_norm, ffn2_norm, ffn2_w_in, ffn2_w_out, mla_w_down,
                         mla_g_cq, mla_g_ckv, mla_w_uq, mla_w_ukv, mla_g_qn, mla_g_kn, mla_w_o, dil_w_qkv,
                         dil_g_qn, dil_g_kn, dil_w_o]))
    M1 = dict(zip(names, [m_ffn1_norm, m_ffn1_w_in, m_ffn1_w_out, m_mix_norm, m_ffn2_norm, m_ffn2_w_in, m_ffn2_w_out,
                          m_mla_w_down, m_mla_g_cq, m_mla_g_ckv, m_mla_w_uq, m_mla_w_ukv, m_mla_g_qn, m_mla_g_kn,
                          m_mla_w_o, m_dil_w_qkv, m_dil_g_qn, m_dil_g_kn, m_dil_w_o]))
    V2 = dict(zip(names, [v_ffn1_norm, v_ffn1_w_in, v_ffn1_w_out, v_mix_norm, v_ffn2_norm, v_ffn2_w_in, v_ffn2_w_out,
                          v_mla_w_down, v_mla_g_cq, v_mla_g_ckv, v_mla_w_uq, v_mla_w_ukv, v_mla_g_qn, v_mla_g_kn,
                          v_mla_w_o, v_dil_w_qkv, v_dil_g_qn, v_dil_g_kn, v_dil_w_o]))
    S, D = x.shape[1], x.shape[2]
    x0 = x.reshape(S, D)
    tgt = loss_target.reshape(S, D)

    big = ["ffn1_w_in", "ffn1_w_out", "ffn2_w_in", "ffn2_w_out", "mla_w_down", "mla_w_uq", "mla_w_ukv", "mla_w_o",
           "dil_w_qkv", "dil_w_o"]
    shard_dim = {"ffn1_w_in": 2, "ffn1_w_out": 1, "ffn2_w_in": 2, "ffn2_w_out": 1, "mla_w_down": 1, "mla_w_uq": 2,
                 "mla_w_ukv": 2, "mla_w_o": 1, "dil_w_qkv": 2, "dil_w_o": 2}
    paired = ("ffn1_w_in", "ffn2_w_in")
    shard_axis = {n: (d, n in paired) for n, d in shard_dim.items()}
    grad_axis = {n: (d - 1, n in paired) for n, d in shard_dim.items()}

    def padded(n, w):
        if n == "mla_w_down":
            return jnp.pad(w, ((0, 0), (0, 0), (0, LAT_PAD - w.shape[2])))
        if n == "mla_w_uq":
            return _pad_heads(w)
        return w

    depth = ffn1_norm.shape[0]
    blocks = []
    for l in range(depth):
        mixer = (["mla_w_down", "mla_w_uq", "mla_w_ukv", "mla_w_o"] if l % 2 == 0 else ["dil_w_qkv", "dil_w_o"])
        blocks.append((f"ffn1_{l}", [("ffn1_w_in", l), ("ffn1_w_out", l)]))
        blocks.append((f"mix_{l}", [(n, l // 2) for n in mixer]))
        blocks.append((f"ffn2_{l}", [("ffn2_w_in", l), ("ffn2_w_out", l)]))
    order = [k for _, keys in blocks for k in keys]
    me = (4 * lax.axis_index("x") + 2 * lax.axis_index("y") + lax.axis_index("c")).astype(jnp.int32).reshape(1)
    def cast(key, deps=()):
        n, l = key
        return _cast_into_gathered(padded(n, W[n]), l, shard_axis[n], me, f"cast_{n}_{l}", deps=deps)

    items0, token0 = _gather_start([cast(order[0])], [shard_axis[order[0][0]]], "gather_start_first")
    rest = order[1:]
    items1, ag_token = _gather_start([cast(k, deps=[token0]) for k in rest], [shard_axis[k[0]] for k in rest],
                                     "gather_start_rest")
    ag_items = dict(zip(order, items0 + items1))
    full = {}

    def relay(keys, after, tag):
        out, token = _gather_relay([ag_items[k] for k in keys], [shard_axis[k[0]] for k in keys], after,
                                   f"gather_relay_{tag}")
        ag_items.update(zip(keys, out))
        return [token]

    def relay_next(bi, after):
        return relay(blocks[bi + 1][1], after, blocks[bi + 1][0]) if bi + 1 < len(blocks) else []

    def fetch(keys, after, tag):
        lands = _gather_wait([ag_items[k] for k in keys], [shard_axis[k[0]] for k in keys], after,
                             f"gather_wait_{tag}")
        full.update(zip(keys, lands))

    g_qn = _pad_heads(mla_g_qn)
    g_kn = _pad_heads(mla_g_kn)
    tabs = _rope_tables(S)
    slopes = jnp.asarray(_alibi_slopes(), F32)

    grads = {}
    gain_g = {}

    out_g, out_d, out_m, out_v = {}, {}, {}, {}
    pending = []
    lag = 3

    def scatter_start(tag, keys):
        items, token = _scatter_start([grads[k] for k in keys], [grad_axis[k[0]] for k in keys],
                                      f"scatter_start_{tag}")
        pending.append((tag, keys, items))
        return token

    def scatter_finish(after):
        tag, keys, items = pending.pop(0)
        lands = _scatter_wait(items, [grad_axis[k[0]] for k in keys], after, f"scatter_wait_{tag}")
        tokens = []
        for (n, l), (own, p) in zip(keys, lands):
            own_axis = grad_axis[n]
            if n in ("mla_w_down", "mla_w_uq"):
                ax, pair = grad_axis[n]
                size = own.shape[ax] // N_DEV
                own = lax.dynamic_slice_in_dim(own, _slot(me[0], pair) * size, size, axis=ax)
                own_axis = None
                if n == "mla_w_down":
                    p, own = p[..., :W[n].shape[2]], own[..., :W[n].shape[2]]
                else:
                    p, own = _unpad_heads(p), _unpad_heads(own)
            prev = (out_g[n], out_d[n], out_m[n], out_v[n]) if n in out_g else None
            (out_g[n], out_d[n], out_m[n], out_v[n]), tok = _adamw(p, own, me, W[n], M1[n], V2[n], l, prev,
                                                                    f"adamw_{n}_{l}", own_axis=own_axis)
            tokens.append(tok)
        return tokens

    def finish_due(after):
        tokens = []
        while len(pending) > lag:
            tokens += scatter_finish(after)
        return tokens

    def ffn_fwd(xin, norm_row, which, l, bi, deps=()):
        tag = blocks[bi][0]
        k_in, k_out = (which + "_w_in", l), (which + "_w_out", l)
        h = _rms_fwd(xin, norm_row, f"rms_fwd_{tag}", deps=deps)
        if bi == 0:
            relay([k_in], h, f"{tag}_in")
        fetch([k_in], h, f"in_{tag}")
        u, a = _ffn_in(h, full[k_in], f"ffn_in_{tag}")
        if bi == 0:
            relay([k_out], a, f"{tag}_out")
        fetch([k_out], a, f"out_{tag}")
        toks = relay_next(bi, a)
        xo = _mm(a, full[k_out], "nn", F32, f"mm_out_{tag}", scale=0.5, res=xin, layer=0, deps=toks)
        return xo, (xin, h, u, a)

    def ffn_bwd(dx_pair, saved, norm_row, which, l, tag):
        dxo, dxob = dx_pair
        k_in, k_out = (which + "_w_in", l), (which + "_w_out", l)
        xin, h, u, a = saved
        grads[k_out] = _mm(a, dxob, "tn", BF16, f"mm_dwout_{tag}", scale=0.5)
        t_out = scatter_start(f"{tag}_out", [k_out])
        du = _ffn_da(dxob, full[k_out], u, f"ffn_da_{tag}", deps=[t_out])
        grads[k_in] = _mm(h, du, "tn", BF16, f"mm_dwin_{tag}")
        t_in = scatter_start(f"{tag}_in", [k_in])
        dh = _mm(du, full[k_in], "nt", F32, f"mm_dh_{tag}", layer=0, deps=[t_in])
        toks = finish_due(dh)
        dx, dxb, dg = _rms_bwd(xin, norm_row, dh, dxo, f"rms_bwd_{tag}", deps=toks)
        gain_g.setdefault(which + "_norm", {})[l] = dg
        return dx, dxb

    def mla_fwd(xin, l, bi):
        j = l // 2
        xn = _rms_fwd(xin, mix_norm[l:l + 1], "rms_fwd_mla")
        fetch([(n, j) for n in ("mla_w_down", "mla_w_uq", "mla_w_ukv", "mla_w_o")], xn, "mla")
        lat = _mm(xn, full[("mla_w_down", j)], "nn", F32, "mm_lat", layer=0)
        cq, ckv = _lat_norm_fwd(lat, mla_g_cq[j:j + 1], mla_g_ckv[j:j + 1], "lat_norm_fwd")
        q_raw = _mm(cq, full[("mla_w_uq", j)], "nn", F32, "mm_uq", layer=0)
        kv = _mm(ckv, full[("mla_w_ukv", j)], "nn", F32, "mm_ukv", layer=0)
        qf, kf, vb = _mla_prep_fwd(q_raw, kv, lat, g_qn[j:j + 1], g_kn[j:j + 1], tabs, "mla_prep_fwd")
        o, lse = _flash_fwd(qf, kf, vb, "flash_fwd")
        toks = relay_next(bi, o)
        xo = _mm(o, full[("mla_w_o", j)], "nn", F32, "mm_mla_o", res=xin, layer=0, deps=toks)
        return xo, (xin, xn, lat, cq, ckv, q_raw, kv, qf, kf, vb, o, lse)

    def mla_bwd(dx_pair, saved, l):
        dxo, dxob = dx_pair
        j = l // 2
        xin, xn, lat, cq, ckv, q_raw, kv, qf, kf, vb, o, lse = saved
        do = _mm(dxob, full[("mla_w_o", j)], "nt", BF16, "mm_mla_do", layer=0)
        grads[("mla_w_o", j)] = _mm(o, dxob, "tn", BF16, "mm_mla_dwo")
        delta = _attn_delta(do, o, "attn_delta")
        dqf, dkf, dv = _flash_bwd(qf, kf, vb, do, lse, delta, "flash_bwd")
        dq_raw, dkv, dkpe, dgq, dgk = _mla_prep_bwd(q_raw, kv, lat, g_qn[j:j + 1], g_kn[j:j + 1], tabs, dqf, dkf, dv,
                                                    "mla_prep_bwd")
        gain_g.setdefault("mla_g_qn", {})[j] = dgq
        gain_g.setdefault("mla_g_kn", {})[j] = dgk
        dcq = _mm(dq_raw, full[("mla_w_uq", j)], "nt", F32, "mm_dcq", layer=0)
        grads[("mla_w_uq", j)] = _mm(cq, dq_raw, "tn", BF16, "mm_dwuq")
        dckv = _mm(dkv, full[("mla_w_ukv", j)], "nt", F32, "mm_dckv", layer=0)
        grads[("mla_w_ukv", j)] = _mm(ckv, dkv, "tn", BF16, "mm_dwukv")
        dlat, dgcq, dgckv = _lat_norm_bwd(lat, mla_g_cq[j:j + 1], mla_g_ckv[j:j + 1], dcq, dckv, dkpe, "lat_norm_bwd")
        gain_g.setdefault("mla_g_cq", {})[j] = dgcq
        gain_g.setdefault("mla_g_ckv", {})[j] = dgckv
        dxn = _mm(dlat, full[("mla_w_down", j)], "nt", F32, "mm_dxn_mla", layer=0)
        grads[("mla_w_down", j)] = _mm(xn, dlat, "tn", BF16, "mm_dwdown")
        tok = scatter_start(f"mix_{l}", [(n, j) for n in ("mla_w_down", "mla_w_uq", "mla_w_ukv", "mla_w_o")])
        toks = finish_due(dxn)
        dx, dxb, dg = _rms_bwd(xin, mix_norm[l:l + 1], dxn, dxo, "rms_bwd_mla", deps=[tok] + toks)
        gain_g.setdefault("mix_norm", {})[l] = dg
        return dx, dxb

    def dil_fwd(xin, l, bi):
        j = l // 2
        xn = _rms_fwd(xin, mix_norm[l:l + 1], "rms_fwd_dil")
        fetch([("dil_w_qkv", j), ("dil_w_o", j)], xn, "dil")
        qkv = _mm(xn, full[("dil_w_qkv", j)], "nn", F32, "mm_qkv", layer=0)
        o_g, lse_g = _dil_fwd(qkv, dil_g_qn[j:j + 1], dil_g_kn[j:j + 1], slopes, "dil_fwd")
        o, lse = _dil_merge(o_g, lse_g, "dil_merge")
        toks = relay_next(bi, o)
        xo = _mm(o, full[("dil_w_o", j)], "nn", F32, "mm_dil_o", res=xin, layer=0, deps=toks)
        return xo, (xin, xn, qkv, o, lse)

    def dil_bwd(dx_pair, saved, l):
        dxo, dxob = dx_pair
        j = l // 2
        xin, xn, qkv, o, lse = saved
        do = _mm(dxob, full[("dil_w_o", j)], "nt", F32, "mm_dil_do", layer=0)
        grads[("dil_w_o", j)] = _mm(o, dxob, "tn", BF16, "mm_dil_dwo")
        delta = _attn_delta(do, o, "dil_delta")
        dq, dk, dv, dgq, dgk = _dil_bwd(qkv, dil_g_qn[j:j + 1], dil_g_kn[j:j + 1], slopes, do, delta, lse, "dil_bwd")
        gain_g.setdefault("dil_g_qn", {})[j] = dgq
        gain_g.setdefault("dil_g_kn", {})[j] = dgk
        dqkv = jnp.concatenate([dq, dk, dv], axis=1)
        dxn = _mm(dqkv, full[("dil_w_qkv", j)], "nt", F32, "mm_dxn_dil", layer=0)
        grads[("dil_w_qkv", j)] = _mm(xn, dqkv, "tn", BF16, "mm_dwqkv")
        tok = scatter_start(f"mix_{l}", [("dil_w_qkv", j), ("dil_w_o", j)])
        toks = finish_due(dxn)
        dx, dxb, dg = _rms_bwd(xin, mix_norm[l:l + 1], dxn, dxo, "rms_bwd_dil", deps=[tok] + toks)
        gain_g.setdefault("mix_norm", {})[l] = dg
        return dx, dxb

    saved = []
    xc = x0
    for l in range(depth):
        xc, s1 = ffn_fwd(xc, ffn1_norm[l:l + 1], "ffn1", l, 3 * l, deps=[ag_token] if l == 0 else ())
        xc, s2 = (mla_fwd if l % 2 == 0 else dil_fwd)(xc, l, 3 * l + 1)
        xc, s3 = ffn_fwd(xc, ffn2_norm[l:l + 1], "ffn2", l, 3 * l + 2)
        saved.append((s1, s2, s3))

    dy, dyb, loss_part = _loss_head(xc, tgt, "loss_head")
    dx = (dy, dyb)
    loss = lax.psum(loss_part[0, 0], MESH_AXES)

    for bi in reversed(range(len(blocks))):
        tag, _ = blocks[bi]
        l = bi // 3
        s = saved[l][bi % 3]
        if bi % 3 == 2:
            dx = ffn_bwd(dx, s, ffn2_norm[l:l + 1], "ffn2", l, tag)
        elif bi % 3 == 1:
            dx = (mla_bwd if l % 2 == 0 else dil_bwd)(dx, s, l)
        else:
            dx = ffn_bwd(dx, s, ffn1_norm[l:l + 1], "ffn1", l, tag)
    grad_x = dx[0].reshape(x.shape)
    after = dx[1]
    while pending:
        after = scatter_finish(after)[-1]

    small = [n for n in names if n not in big]

    def gain_local(n):
        rows = [gain_g[n][l] for l in range(W[n].shape[0])]
        g = jnp.concatenate(rows, axis=1)
        return g

    def flat_pad(n, a):
        a = a.reshape(1, -1)
        if n in ("mla_g_qn", "mla_g_kn"):
            a = _pad_heads(a)
        return a

    packed_g = jnp.concatenate([gain_local(n) for n in small], axis=1)
    sizes = [gain_local(n).shape[1] for n in small]
    tot_g = _gain_allreduce(packed_g, "gain_allreduce")
    pw = jnp.concatenate([flat_pad(n, W[n]) for n in small], axis=1)
    pm = jnp.concatenate([flat_pad(n, M1[n]) for n in small], axis=1)
    pv = jnp.concatenate([flat_pad(n, V2[n]) for n in small], axis=1)
    res, _ = _adamw(tot_g.reshape(1, 1, -1), None, me, pw.reshape(1, 1, -1), pm.reshape(1, 1, -1),
                    pv.reshape(1, 1, -1), 0, None, "adamw_gains")
    res = [r.reshape(1, -1) for r in res]
    off = 0
    for n, sz in zip(small, sizes):
        for dst, r in zip((out_g, out_d, out_m, out_v), res):
            piece = r[:, off:off + sz]
            if n in ("mla_g_qn", "mla_g_kn"):
                piece = _unpad_heads(piece)
            dst[n] = piece.reshape(W[n].shape)
        off += sz

    return (loss, grad_x, *[out_g[n] for n in names], *[out_d[n] for n in names],
            *[out_m[n] for n in names], *[out_v[n] for n in names])
```

```python
import functools
import math

import jax
import jax.numpy as jnp
import numpy as np
from jax import lax
from jax.experimental import pallas as pl
from jax.experimental.pallas import tpu as pltpu

EPS = 1e-6
MLA_HEADS = 16
Q_LORA = 512
KV_LORA = 512
NOPE_DIM = 128
ROPE_DIM = 64
V_DIM = 128
QK_DIM = NOPE_DIM + ROPE_DIM
ROPE_THETA = 10000.0
HEAD_PAD = 256
LAT_PAD = Q_LORA + KV_LORA + 128
DIL_PAIRS = ((128, 1), (512, 4), (2048, 16))
DIL_GROUPS = 3
DIL_HEADS = 8
DIL_HEAD_DIM = 128
DIL_BLK = 128
FLASH_HEADS = 2
FLASH_HEADS_FWD = 4
LOG2E = math.log2(math.e)
LN2 = math.log(2.0)
ADAM_LR = 0.001
ADAM_B1 = 0.9
ADAM_B2 = 0.999
ADAM_EPS = 1e-08
ADAM_WD = 0.01
ADAM_STEP = 10

N_DEV = 8
MESH_AXES = ("x", "y", "c")
MESH = pl.DeviceIdType.MESH
NEG_BIG = -1e30
VMEM_LIMIT_V7X = 56 * 1024 * 1024
LANES = 128

BF16 = jnp.bfloat16
F32 = jnp.float32


def _pick(n, cands):
    for c in cands:
        if n % c == 0:
            return c
    raise ValueError(f"no tile for {n}")


def _params(sem):
    return pltpu.CompilerParams(dimension_semantics=sem, vmem_limit_bytes=VMEM_LIMIT_V7X)


ANY_SPEC = pl.BlockSpec(memory_space=pl.ANY)


MM_VMEM_BUDGET = 44 * 1024 * 1024
MM_HBM_BYTES_PER_S = 1.8e12
MM_MXU_FLOPS_PER_S = 8.5e14
MM_STEP_S = 0.4e-6
MM_MAX_TILE_MACS = 3.3e9
MXU_DIM = 256


MM_TIMED_TILES = {
    (2048, 4096, 11264, 2, 2, 2, False): (1024, 1024, 4096, True),
    (4096, 11264, 2048, 2, 2, 4, False): (1024, 1024, 2816, True),
    (4096, 5632, 2048, 2, 2, 4, True): (512, 1024, 5632, False),
    (2048, 4096, 9216, 2, 2, 2, False): (1024, 1024, 4096, True),
}


@functools.lru_cache(maxsize=None)
def _mm_tiles(M, K, N, a_bytes, b_bytes, out_bytes, has_res):
    if (M, K, N, a_bytes, b_bytes, out_bytes, has_res) in MM_TIMED_TILES:
        return MM_TIMED_TILES[(M, K, N, a_bytes, b_bytes, out_bytes, has_res)]
    best = None
    for tk in [K] + [c for c in (1408, 1024, 512, 384, 256, 128) if K % c == 0 and c < K]:
        nk = K // tk
        for tm in [c for c in (2048, 1024, 512, 256, 128) if M % c == 0]:
            for tn in [c for c in (2816, 2048, 1408, 1152, 1024, 512, 384, 256, 128) if N % c == 0]:
                if tm * tk * tn > MM_MAX_TILE_MACS:
                    continue
                fill = (tn / (-(-tn // MXU_DIM) * MXU_DIM)) * (tk / (-(-tk // MXU_DIM) * MXU_DIM))
                fill *= tm / (tm + MXU_DIM // 2)
                vmem = 2 * (tm * tk * a_bytes + tk * tn * b_bytes) + 2 * tm * tn * out_bytes + tm * tn * 4
                vmem += (tm * tk + tk * tn) * 2 if max(a_bytes, b_bytes) > 2 else 0
                vmem += 2 * tm * tn * 4 if has_res else 0
                if vmem > MM_VMEM_BUDGET:
                    continue
                a_all, b_all = M * K * a_bytes, K * N * b_bytes
                if nk == 1:
                    t_i = a_all + (M // tm) * b_all
                    t_j = b_all + (N // tn) * a_all
                    traffic, i_outer = min((t_i, True), (t_j, False))
                else:
                    traffic, i_outer = (N // tn) * a_all + (M // tm) * b_all, True
                traffic += M * N * (out_bytes + (4 if has_res else 0))
                mxu = 2.0 * M * K * N / (MM_MXU_FLOPS_PER_S * fill) * (1.15 if nk > 1 else 1.0)
                cost = max(traffic / MM_HBM_BYTES_PER_S, mxu) + (M // tm) * (N // tn) * nk * MM_STEP_S
                if best is None or cost < best[0]:
                    best = (cost, tm, tn, tk, i_outer)
    assert best is not None, (M, K, N)
    return best[1:]


def _mm(a, b, mode, out_dtype, name, *, scale=1.0, res=None, layer=None, deps=()):
    b2 = b.shape[-2:]
    if mode == "nn":
        (M, K), (Kb, N) = a.shape, b2
    elif mode == "nt":
        (M, K), (N, Kb) = a.shape, b2
    else:
        (K, M), (Kb, N) = a.shape, b2
    assert K == Kb, (a.shape, b.shape, mode)
    tm, tn, tk, i_outer = _mm_tiles(M, K, N, a.dtype.itemsize, b.dtype.itemsize, jnp.dtype(out_dtype).itemsize,
                                    res is not None)
    nk = K // tk
    dims = {"nn": (((1,), (0,)), ((), ())), "nt": (((1,), (1,)), ((), ())), "tn": (((0,), (0,)), ((), ()))}[mode]

    def finish(v, r_ref, o_ref):
        if scale != 1.0:
            v = v * scale
        if r_ref is not None:
            v = r_ref[...] + v
        o_ref[...] = v.astype(o_ref.dtype)

    def body(*refs):
        a_ref, b_ref = refs[:2]
        r_ref = refs[2] if res is not None else None
        prod = lambda: lax.dot_general(a_ref[...].astype(BF16), b_ref[...].astype(BF16), dims,
                                       preferred_element_type=F32)
        if nk == 1:
            finish(prod(), r_ref, refs[-1])
            return
        o_ref, acc = refs[-2:]
        k = pl.program_id(2)

        @pl.when(k == 0)
        def _():
            acc[...] = prod()

        @pl.when(k > 0)
        def _():
            acc[...] += prod()

        @pl.when(k == nk - 1)
        def _():
            finish(acc[...], r_ref, o_ref)

    ij = (lambda p, q: (p, q)) if i_outer else (lambda p, q: (q, p))

    def spec(shape, f, lead=None):
        full = lambda p, q, k: f(*ij(p, q), k)
        if lead is None:
            return pl.BlockSpec(shape, full)
        return pl.BlockSpec((None,) + shape, lambda p, q, k: (lead,) + full(p, q, k))

    a_spec = spec((tk, tm), lambda i, j, k: (k, i)) if mode == "tn" else spec((tm, tk), lambda i, j, k: (i, k))
    lead = layer if b.ndim == 3 else None
    b_spec = spec((tn, tk), lambda i, j, k: (j, k), lead) if mode == "nt" else spec((tk, tn), lambda i, j, k: (k, j), lead)
    in_specs = [a_spec, b_spec]
    args = [a, b]
    if res is not None:
        in_specs.append(spec((tm, tn), lambda i, j, k: (i, j)))
        args.append(res)
    in_specs += [ANY_SPEC] * len(deps)
    args += list(deps)
    outer, inner = (M // tm, N // tn) if i_outer else (N // tn, M // tm)
    return pl.pallas_call(
        body, name=name, grid=(outer, inner, nk),
        in_specs=in_specs, out_specs=spec((tm, tn), lambda i, j, k: (i, j)),
        out_shape=jax.ShapeDtypeStruct((M, N), out_dtype),
        scratch_shapes=[pltpu.VMEM((tm, tn), F32)] if nk > 1 else [],
        compiler_params=_params(("parallel", "parallel", "arbitrary")),
    )(*args)


def _cast_into_gathered(w, layer, axis, me, name, deps=()):
    _, R, C = w.shape
    tr = _pick(R, (512, 256, 128, 64, 32, 16))
    nr = R // tr
    axis, paired = axis

    def body(me_ref, w_ref, *rest):
        o_ref = rest[-1]
        o_ref[...] = w_ref[...].astype(BF16)

    if axis == 1:
        out_idx = lambda i, me_ref: (0, _slot(me_ref[0], paired) * nr + i, 0)
        shape = (1, R * N_DEV, C)
    else:
        out_idx = lambda i, me_ref: (0, i, _slot(me_ref[0], paired))
        shape = (1, R, C * N_DEV)
    return pl.pallas_call(
        body, name=name,
        grid_spec=pltpu.PrefetchScalarGridSpec(
            num_scalar_prefetch=1, grid=(nr,),
            in_specs=[pl.BlockSpec((None, tr, C), lambda i, me_ref: (layer, i, 0))] + [ANY_SPEC] * len(deps),
            out_specs=pl.BlockSpec((None, tr, C), out_idx)),
        out_shape=jax.ShapeDtypeStruct(shape, BF16), compiler_params=_params(("parallel",)),
    )(me, w, *deps)


def _rms_fwd(x, g, name, deps=()):
    T, D = x.shape
    tr = _pick(T, (512, 256, 128))

    def body(x_ref, g_ref, *rest):
        o_ref = rest[-1]
        xv = x_ref[...]
        r = lax.rsqrt(jnp.mean(xv * xv, axis=-1, keepdims=True) + EPS)
        o_ref[...] = ((xv * r) * g_ref[...]).astype(BF16)

    return pl.pallas_call(
        body, name=name, grid=(T // tr,),
        in_specs=[pl.BlockSpec((tr, D), lambda i: (i, 0)), pl.BlockSpec((1, D), lambda i: (0, 0))]
        + [ANY_SPEC] * len(deps),
        out_specs=pl.BlockSpec((tr, D), lambda i: (i, 0)),
        out_shape=jax.ShapeDtypeStruct((T, D), BF16), compiler_params=_params(("parallel",)),
    )(x, g, *deps)


def _rms_bwd(x, g, dh, dres, name, deps=()):
    T, D = x.shape
    tr = _pick(T, (256, 128))

    def body(x_ref, g_ref, dh_ref, dres_ref, *rest):
        dx_ref, dxb_ref, dg_ref = rest[-3:]
        xv = x_ref[...]
        dhv = dh_ref[...]
        r = lax.rsqrt(jnp.mean(xv * xv, axis=-1, keepdims=True) + EPS)
        xhat = xv * r
        dxh = dhv * g_ref[...]
        c = jnp.mean(dxh * xhat, axis=-1, keepdims=True)
        dx = dres_ref[...] + r * (dxh - xhat * c)
        dx_ref[...] = dx
        dxb_ref[...] = dx.astype(BF16)

        @pl.when(pl.program_id(0) == 0)
        def _():
            dg_ref[...] = jnp.zeros_like(dg_ref)

        dg_ref[...] += jnp.sum(dhv * xhat, axis=0, keepdims=True)

    row = pl.BlockSpec((tr, D), lambda i: (i, 0))
    vec = pl.BlockSpec((1, D), lambda i: (0, 0))
    return pl.pallas_call(
        body, name=name, grid=(T // tr,),
        in_specs=[row, vec, row, row] + [ANY_SPEC] * len(deps), out_specs=[row, row, vec],
        out_shape=[jax.ShapeDtypeStruct((T, D), F32), jax.ShapeDtypeStruct((T, D), BF16),
                   jax.ShapeDtypeStruct((1, D), F32)],
        compiler_params=_params(("arbitrary",)),
    )(x, g, dh, dres, *deps)


N_PANEL = N_DEV // 2


def _ffn_in(h, w_in, name):
    T, D = h.shape
    F2 = w_in.shape[2]
    pw = F2 // N_PANEL
    half = pw // 2
    tm = _pick(T, (512, 256, 128))

    def body(h_ref, w_ref, u_ref, a_ref):
        r = jnp.dot(h_ref[...], w_ref[...], preferred_element_type=F32)
        u_ref[...] = r.astype(BF16)
        g, up = r[:, :half], r[:, half:]
        a_ref[...] = (g * jax.nn.sigmoid(g) * up).astype(BF16)

    return pl.pallas_call(
        body, name=name, grid=(N_PANEL, T // tm),
        in_specs=[pl.BlockSpec((tm, D), lambda p, i: (i, 0)), pl.BlockSpec((None, D, pw), lambda p, i: (0, 0, p))],
        out_specs=[pl.BlockSpec((tm, pw), lambda p, i: (i, p)), pl.BlockSpec((tm, half), lambda p, i: (i, p))],
        out_shape=[jax.ShapeDtypeStruct((T, F2), BF16), jax.ShapeDtypeStruct((T, F2 // 2), BF16)],
        compiler_params=_params(("parallel", "parallel")),
    )(h, w_in)


def _ffn_da(dxo, w_out, u, name, deps=()):
    T, D = dxo.shape
    F2 = u.shape[1]
    pw = F2 // N_PANEL
    half = pw // 2
    tm = _pick(T, (512, 256, 128))

    def body(d_ref, w_ref, u_ref, *rest):
        du_ref = rest[-1]
        da = 0.5 * lax.dot_general(d_ref[...], w_ref[...], NT_DIMS, preferred_element_type=F32)
        g = u_ref[:, :half].astype(F32)
        up = u_ref[:, half:].astype(F32)
        sg = jax.nn.sigmoid(g)
        silu = g * sg
        du_ref[:, :half] = (da * up * (sg + silu * (1.0 - sg))).astype(BF16)
        du_ref[:, half:] = (da * silu).astype(BF16)

    return pl.pallas_call(
        body, name=name, grid=(N_PANEL, T // tm),
        in_specs=[pl.BlockSpec((tm, D), lambda p, i: (i, 0)), pl.BlockSpec((None, half, D), lambda p, i: (0, p, 0)),
                  pl.BlockSpec((tm, pw), lambda p, i: (i, p))] + [ANY_SPEC] * len(deps),
        out_specs=pl.BlockSpec((tm, pw), lambda p, i: (i, p)),
        out_shape=jax.ShapeDtypeStruct((T, F2), BF16), compiler_params=_params(("parallel", "parallel")),
    )(dxo, w_out, u, *deps)


def _loss_head(y, t, name):
    T, D = y.shape
    tr = _pick(T, (512, 256, 128))

    def body(y_ref, t_ref, dy_ref, dyb_ref, l_ref):
        e = y_ref[...] - t_ref[...]
        dy = e * (1.0 / D)
        dy_ref[...] = dy
        dyb_ref[...] = dy.astype(BF16)

        @pl.when(pl.program_id(0) == 0)
        def _():
            l_ref[...] = jnp.zeros_like(l_ref)

        l_ref[...] += 0.5 * jnp.sum(jnp.mean(e * e, axis=-1, keepdims=True), axis=0, keepdims=True)

    row = pl.BlockSpec((tr, D), lambda i: (i, 0))
    return pl.pallas_call(
        body, name=name, grid=(T // tr,),
        in_specs=[row, row], out_specs=[row, row, pl.BlockSpec((1, 1), lambda i: (0, 0))],
        out_shape=[jax.ShapeDtypeStruct((T, D), F32), jax.ShapeDtypeStruct((T, D), BF16),
                   jax.ShapeDtypeStruct((1, 1), F32)],
        compiler_params=_params(("arbitrary",)),
    )(y, t)


def _rope_tables(S):
    half = ROPE_DIM // 2
    inv = 1.0 / (ROPE_THETA ** (jnp.arange(0, ROPE_DIM, 2, dtype=F32) / ROPE_DIM))
    ang = jnp.arange(S, dtype=F32)[:, None] * inv[None, :]
    cos, sin = jnp.cos(ang), jnp.sin(ang)
    z = jnp.zeros((S, half), F32)
    z2 = jnp.zeros((S, LANES - ROPE_DIM), F32)
    c = jnp.concatenate([cos, cos, z2], axis=1)
    s1 = jnp.concatenate([-sin, z, z2], axis=1)
    s2 = jnp.concatenate([z, sin, z2], axis=1)
    return c, s1, s2


def _rope(r, c, s1, s2):
    return r * c + pltpu.roll(r, LANES - ROPE_DIM // 2, 1) * s1 + pltpu.roll(r, ROPE_DIM // 2, 1) * s2


def _rope_t(d, c, s1, s2):
    return d * c + pltpu.roll(d * s1, ROPE_DIM // 2, 1) + pltpu.roll(d * s2, LANES - ROPE_DIM // 2, 1)


def _lat_norm_fwd(lat, g_cq, g_ckv, name):
    T = lat.shape[0]
    tr = _pick(T, (512, 256, 128))

    def body(lat_ref, gq_ref, gk_ref, cq_ref, ckv_ref):
        for off, g_ref, o_ref in ((0, gq_ref, cq_ref), (Q_LORA, gk_ref, ckv_ref)):
            xv = lat_ref[:, off:off + Q_LORA]
            r = lax.rsqrt(jnp.mean(xv * xv, axis=-1, keepdims=True) + EPS)
            o_ref[...] = ((xv * r) * g_ref[...]).astype(BF16)

    vec = pl.BlockSpec((1, Q_LORA), lambda i: (0, 0))
    out = pl.BlockSpec((tr, Q_LORA), lambda i: (i, 0))
    return pl.pallas_call(
        body, name=name, grid=(T // tr,),
        in_specs=[pl.BlockSpec((tr, LAT_PAD), lambda i: (i, 0)), vec, vec], out_specs=[out, out],
        out_shape=[jax.ShapeDtypeStruct((T, Q_LORA), BF16)] * 2, compiler_params=_params(("parallel",)),
    )(lat, g_cq, g_ckv)


def _lat_norm_bwd(lat, g_cq, g_ckv, dcq, dckv, dkpe, name):
    T = lat.shape[0]
    tr = _pick(T, (256, 128))

    def body(lat_ref, gq_ref, gk_ref, dcq_ref, dckv_ref, dkpe_ref, dlat_ref, dgq_ref, dgk_ref):
        @pl.when(pl.program_id(0) == 0)
        def _():
            dgq_ref[...] = jnp.zeros_like(dgq_ref)
            dgk_ref[...] = jnp.zeros_like(dgk_ref)

        for off, g_ref, d_ref, dg_ref in ((0, gq_ref, dcq_ref, dgq_ref), (Q_LORA, gk_ref, dckv_ref, dgk_ref)):
            xv = lat_ref[:, off:off + Q_LORA]
            dv = d_ref[...]
            r = lax.rsqrt(jnp.mean(xv * xv, axis=-1, keepdims=True) + EPS)
            xhat = xv * r
            dxh = dv * g_ref[...]
            c = jnp.mean(dxh * xhat, axis=-1, keepdims=True)
            dlat_ref[:, off:off + Q_LORA] = (r * (dxh - xhat * c)).astype(BF16)
            dg_ref[...] += jnp.sum(dv * xhat, axis=0, keepdims=True)
        dlat_ref[:, Q_LORA + KV_LORA:] = dkpe_ref[...].astype(BF16)

    vec = pl.BlockSpec((1, Q_LORA), lambda i: (0, 0))
    half = pl.BlockSpec((tr, Q_LORA), lambda i: (i, 0))
    full = pl.BlockSpec((tr, LAT_PAD), lambda i: (i, 0))
    return pl.pallas_call(
        body, name=name, grid=(T // tr,),
        in_specs=[full, vec, vec, half, half, pl.BlockSpec((tr, LANES), lambda i: (i, 0))],
        out_specs=[full, vec, vec],
        out_shape=[jax.ShapeDtypeStruct((T, LAT_PAD), BF16), jax.ShapeDtypeStruct((1, Q_LORA), F32),
                   jax.ShapeDtypeStruct((1, Q_LORA), F32)],
        compiler_params=_params(("arbitrary",)),
    )(lat, g_cq, g_ckv, dcq, dckv, dkpe)


def _mla_prep_fwd(q_raw, kv, lat, g_qn, g_kn, tabs, name):
    T = q_raw.shape[0]
    H = MLA_HEADS
    tr = _pick(T, (256, 128))
    scale = LOG2E / math.sqrt(QK_DIM)

    def body(q_ref, kv_ref, kpe_ref, gq_ref, gk_ref, c_ref, s1_ref, s2_ref, qf_ref, kf_ref, v_ref):
        c, s1, s2 = c_ref[...], s1_ref[...], s2_ref[...]
        gq, gk = gq_ref[...], gk_ref[...]
        kpe = kpe_ref[...]
        kpe_ss = jnp.sum(kpe * kpe, axis=-1, keepdims=True)
        for h in range(H):
            lo = h * HEAD_PAD
            qa = q_ref[:, lo:lo + LANES]
            qb = q_ref[:, lo + LANES:lo + HEAD_PAD]
            ss = jnp.sum(qa * qa + qb * qb, axis=-1, keepdims=True)
            r = lax.rsqrt(ss * (1.0 / QK_DIM) + EPS)
            qf_ref[:, lo:lo + LANES] = (qa * r * gq[:, :LANES] * scale).astype(BF16)
            qf_ref[:, lo + LANES:lo + HEAD_PAD] = (_rope(qb * r * gq[:, LANES:], c, s1, s2) * scale).astype(BF16)
            ka = kv_ref[:, lo:lo + LANES]
            ss = jnp.sum(ka * ka, axis=-1, keepdims=True) + kpe_ss
            r = lax.rsqrt(ss * (1.0 / QK_DIM) + EPS)
            kf_ref[:, lo:lo + LANES] = (ka * r * gk[:, :LANES]).astype(BF16)
            kf_ref[:, lo + LANES:lo + HEAD_PAD] = _rope(kpe * r * gk[:, LANES:], c, s1, s2).astype(BF16)
            v_ref[:, lo:lo + V_DIM] = kv_ref[:, lo + LANES:lo + HEAD_PAD].astype(BF16)
            v_ref[:, lo + V_DIM:lo + HEAD_PAD] = jnp.ones((tr, HEAD_PAD - V_DIM), BF16)

    wide = pl.BlockSpec((tr, H * HEAD_PAD), lambda i: (i, 0))
    lane = pl.BlockSpec((tr, LANES), lambda i: (i, 0))
    gvec = pl.BlockSpec((1, HEAD_PAD), lambda i: (0, 0))
    return pl.pallas_call(
        body, name=name, grid=(T // tr,),
        in_specs=[wide, wide, pl.BlockSpec((tr, LANES), lambda i: (i, (Q_LORA + KV_LORA) // LANES)), gvec, gvec,
                  lane, lane, lane],
        out_specs=[wide, wide, wide],
        out_shape=[jax.ShapeDtypeStruct((T, H * HEAD_PAD), BF16)] * 3,
        compiler_params=_params(("parallel",)),
    )(q_raw, kv, lat, g_qn, g_kn, *tabs)


def _mla_prep_bwd(q_raw, kv, lat, g_qn, g_kn, tabs, dqf, dkf, dv, name):
    T = q_raw.shape[0]
    H = MLA_HEADS
    tr = _pick(T, (128,))

    def body(q_ref, kv_ref, kpe_ref, gq_ref, gk_ref, c_ref, s1_ref, s2_ref, dqf_ref, dkf_ref, dv_ref,
             dq_ref, dkv_ref, dkpe_ref, dgq_ref, dgk_ref):
        @pl.when(pl.program_id(0) == 0)
        def _():
            dgq_ref[...] = jnp.zeros_like(dgq_ref)
            dgk_ref[...] = jnp.zeros_like(dgk_ref)

        c, s1, s2 = c_ref[...], s1_ref[...], s2_ref[...]
        gq, gk = gq_ref[...], gk_ref[...]
        kpe = kpe_ref[...]
        kpe_ss = jnp.sum(kpe * kpe, axis=-1, keepdims=True)
        dkpe = jnp.zeros_like(kpe)
        dgq_a = jnp.zeros((1, LANES), F32)
        dgq_b = jnp.zeros((1, LANES), F32)
        dgk_a = jnp.zeros((1, LANES), F32)
        dgk_b = jnp.zeros((1, LANES), F32)
        for h in range(H):
            lo = h * HEAD_PAD
            xa = q_ref[:, lo:lo + LANES]
            xb = q_ref[:, lo + LANES:lo + HEAD_PAD]
            ss = jnp.sum(xa * xa + xb * xb, axis=-1, keepdims=True)
            r = lax.rsqrt(ss * (1.0 / QK_DIM) + EPS)
            xa, xb = xa * r, xb * r
            da = dqf_ref[:, lo:lo + LANES].astype(F32)
            db = _rope_t(dqf_ref[:, lo + LANES:lo + HEAD_PAD].astype(F32), c, s1, s2)
            dgq_a += jnp.sum(da * xa, axis=0, keepdims=True)
            dgq_b += jnp.sum(db * xb, axis=0, keepdims=True)
            da, db = da * gq[:, :LANES], db * gq[:, LANES:]
            cc = jnp.sum(da * xa + db * xb, axis=-1, keepdims=True) * (1.0 / QK_DIM)
            dq_ref[:, lo:lo + LANES] = (r * (da - xa * cc)).astype(BF16)
            dq_ref[:, lo + LANES:lo + HEAD_PAD] = (r * (db - xb * cc)).astype(BF16)
            xa = kv_ref[:, lo:lo + LANES]
            ss = jnp.sum(xa * xa, axis=-1, keepdims=True) + kpe_ss
            r = lax.rsqrt(ss * (1.0 / QK_DIM) + EPS)
            xa, xb = xa * r, kpe * r
            da = dkf_ref[:, lo:lo + LANES].astype(F32)
            db = _rope_t(dkf_ref[:, lo + LANES:lo + HEAD_PAD].astype(F32), c, s1, s2)
            dgk_a += jnp.sum(da * xa, axis=0, keepdims=True)
            dgk_b += jnp.sum(db * xb, axis=0, keepdims=True)
            da, db = da * gk[:, :LANES], db * gk[:, LANES:]
            cc = jnp.sum(da * xa + db * xb, axis=-1, keepdims=True) * (1.0 / QK_DIM)
            dkv_ref[:, lo:lo + LANES] = (r * (da - xa * cc)).astype(BF16)
            dkpe = dkpe + r * (db - xb * cc)
            dkv_ref[:, lo + LANES:lo + HEAD_PAD] = dv_ref[:, h * V_DIM:(h + 1) * V_DIM].astype(BF16)
        dkpe_ref[...] = dkpe
        dgq_ref[:, :LANES] += dgq_a
        dgq_ref[:, LANES:] += dgq_b
        dgk_ref[:, :LANES] += dgk_a
        dgk_ref[:, LANES:] += dgk_b

    wide = pl.BlockSpec((tr, H * HEAD_PAD), lambda i: (i, 0))
    lane = pl.BlockSpec((tr, LANES), lambda i: (i, 0))
    gvec = pl.BlockSpec((1, HEAD_PAD), lambda i: (0, 0))
    vspec = pl.BlockSpec((tr, H * V_DIM), lambda i: (i, 0))
    return pl.pallas_call(
        body, name=name, grid=(T // tr,),
        in_specs=[wide, wide, pl.BlockSpec((tr, LANES), lambda i: (i, (Q_LORA + KV_LORA) // LANES)), gvec, gvec,
                  lane, lane, lane, wide, wide, vspec],
        out_specs=[wide, wide, lane, gvec, gvec],
        out_shape=[jax.ShapeDtypeStruct((T, H * HEAD_PAD), BF16), jax.ShapeDtypeStruct((T, H * HEAD_PAD), BF16),
                   jax.ShapeDtypeStruct((T, LANES), F32), jax.ShapeDtypeStruct((1, HEAD_PAD), F32),
                   jax.ShapeDtypeStruct((1, HEAD_PAD), F32)],
        compiler_params=_params(("arbitrary",)),
    )(q_raw, kv, lat, g_qn, g_kn, *tabs, dqf, dkf, dv)


def _causal_mask(tq, tk):
    return lax.broadcasted_iota(jnp.int32, (tq, tk), 1) <= lax.broadcasted_iota(jnp.int32, (tq, tk), 0)


NT_DIMS = (((1,), (1,)), ((), ()))
TN_DIMS = (((0,), (0,)), ((), ()))


def _flash_fwd(qf, kf, v, name):
    T = qf.shape[0]
    H, G = MLA_HEADS, FLASH_HEADS_FWD
    t = _pick(T, (512, 256, 128))
    n = T // t
    pairs = [(i, j) for i in range(n) for j in range(i + 1)]
    qi = jnp.asarray([p[0] for p in pairs], jnp.int32)
    kj = jnp.asarray([p[1] for p in pairs], jnp.int32)

    def body(qi_ref, kj_ref, q_ref, k_ref, v_ref, o_ref, lse_ref, *scratch):
        m_sc, acc_sc = scratch[:G], scratch[G:]
        sid = pl.program_id(1)
        i, j = qi_ref[sid], kj_ref[sid]

        @pl.when(j == 0)
        def _():
            for g in range(G):
                m_sc[g][...] = jnp.full_like(m_sc[g], NEG_BIG)
                acc_sc[g][...] = jnp.zeros_like(acc_sc[g])

        def step(masked):
            for g in range(G):
                qk = slice(g * HEAD_PAD, (g + 1) * HEAD_PAD)
                s = lax.dot_general(q_ref[:, qk], k_ref[:, qk], NT_DIMS, preferred_element_type=F32)
                if masked:
                    s = jnp.where(_causal_mask(t, t), s, NEG_BIG)
                m_prev = m_sc[g][:, :1]
                m_new = jnp.maximum(m_prev, jnp.max(s, axis=-1, keepdims=True))
                a = jnp.exp2(m_prev - m_new)
                p = jnp.exp2((s - m_new).astype(BF16))
                acc_sc[g][...] = a * acc_sc[g][...] + jnp.dot(p, v_ref[:, qk], preferred_element_type=F32)
                m_sc[g][...] = jnp.broadcast_to(m_new, (t, LANES))

        @pl.when(j < i)
        def _():
            step(False)

        @pl.when(j == i)
        def _():
            step(True)
            for g in range(G):
                vo = slice(g * V_DIM, (g + 1) * V_DIM)
                l = acc_sc[g][:, V_DIM:]
                o_ref[:, vo] = (acc_sc[g][:, :V_DIM] / l).astype(BF16)
                lse_ref[:, vo] = m_sc[g][...] + jnp.log2(l)

    row = pl.BlockSpec((t, G * V_DIM), lambda h, s, qi, kj: (qi[s], h))
    return pl.pallas_call(
        body, name=name,
        grid_spec=pltpu.PrefetchScalarGridSpec(
            num_scalar_prefetch=2, grid=(H // G, len(pairs)),
            in_specs=[pl.BlockSpec((t, G * HEAD_PAD), lambda h, s, qi, kj: (qi[s], h)),
                      pl.BlockSpec((t, G * HEAD_PAD), lambda h, s, qi, kj: (kj[s], h)),
                      pl.BlockSpec((t, G * HEAD_PAD), lambda h, s, qi, kj: (kj[s], h))],
            out_specs=[row, row],
            scratch_shapes=[pltpu.VMEM((t, LANES), F32)] * G + [pltpu.VMEM((t, HEAD_PAD), F32)] * G),
        out_shape=[jax.ShapeDtypeStruct((T, H * V_DIM), BF16), jax.ShapeDtypeStruct((T, H * V_DIM), F32)],
        compiler_params=_params(("parallel", "arbitrary")),
    )(qi, kj, qf, kf, v)


def _attn_delta(do, o, name):
    T, W = do.shape
    nh = W // V_DIM
    tr = _pick(T, (512, 256, 128))

    def body(do_ref, o_ref, d_ref):
        for h in range(nh):
            sl = slice(h * V_DIM, (h + 1) * V_DIM)
            d = jnp.sum(do_ref[:, sl].astype(F32) * o_ref[:, sl].astype(F32), axis=-1, keepdims=True)
            d_ref[:, sl] = jnp.broadcast_to(d, (tr, V_DIM))

    row = pl.BlockSpec((tr, W), lambda i: (i, 0))
    return pl.pallas_call(
        body, name=name, grid=(T // tr,), in_specs=[row, row], out_specs=row,
        out_shape=jax.ShapeDtypeStruct((T, W), F32), compiler_params=_params(("parallel",)),
    )(do, o)


def _flash_bwd(qf, kf, v, do, lse, delta, name):
    T = qf.shape[0]
    H, G = MLA_HEADS, FLASH_HEADS
    t = _pick(T, (512, 256, 128))
    n = T // t
    scale = 1.0 / math.sqrt(QK_DIM)
    pairs = [(i, j) for j in range(n) for i in range(j, n)]
    qi = jnp.asarray([p[0] for p in pairs], jnp.int32)
    kj = jnp.asarray([p[1] for p in pairs], jnp.int32)

    def body(qi_ref, kj_ref, q_ref, k_ref, v_ref, do_ref, lse_ref, dl_ref, dq_ref, dk_ref, dv_ref,
             dq_acc, dk_acc, dv_acc):
        sid = pl.program_id(1)
        i, j = qi_ref[sid], kj_ref[sid]

        @pl.when(sid == 0)
        def _():
            dq_acc[...] = jnp.zeros_like(dq_acc)

        def step(masked):
            rows = pl.ds(pl.multiple_of(i * t, t), t)
            for g in range(G):
                qk = slice(g * HEAD_PAD, (g + 1) * HEAD_PAD)
                vo = slice(g * V_DIM, (g + 1) * V_DIM)
                q, k, do_ = q_ref[:, qk], k_ref[:, qk], do_ref[:, vo]
                v_ = v_ref[:, g * HEAD_PAD:g * HEAD_PAD + V_DIM]
                s = lax.dot_general(q, k, NT_DIMS, preferred_element_type=F32)
                if masked:
                    s = jnp.where(_causal_mask(t, t), s, NEG_BIG)
                p = jnp.exp2(s - lse_ref[:, g * V_DIM:g * V_DIM + 1])
                dp = lax.dot_general(do_, v_, NT_DIMS, preferred_element_type=F32)
                ds = (p * (dp - dl_ref[:, g * V_DIM:g * V_DIM + 1])).astype(BF16)
                dv = lax.dot_general(p.astype(BF16), do_, TN_DIMS, preferred_element_type=F32)
                dk = lax.dot_general(ds, q, TN_DIMS, preferred_element_type=F32)
                if masked:
                    dv_acc[:, vo] = dv
                    dk_acc[:, qk] = dk
                else:
                    dv_acc[:, vo] += dv
                    dk_acc[:, qk] += dk
                dq_acc[rows, qk] += jnp.dot(ds, k, preferred_element_type=F32) * scale

        @pl.when(i == j)
        def _():
            step(True)

        @pl.when(i > j)
        def _():
            step(False)

        @pl.when(i == n - 1)
        def _():
            dk_ref[...] = (dk_acc[...] * LN2).astype(BF16)
            dv_ref[...] = dv_acc[...].astype(BF16)

        @pl.when(sid == len(pairs) - 1)
        def _():
            dq_ref[...] = dq_acc[...].astype(BF16)

    qs = pl.BlockSpec((t, G * HEAD_PAD), lambda h, s, qi, kj: (qi[s], h))
    rs = pl.BlockSpec((t, G * V_DIM), lambda h, s, qi, kj: (qi[s], h))
    ks = pl.BlockSpec((t, G * HEAD_PAD), lambda h, s, qi, kj: (kj[s], h))
    vs = pl.BlockSpec((t, G * V_DIM), lambda h, s, qi, kj: (kj[s], h))
    return pl.pallas_call(
        body, name=name,
        grid_spec=pltpu.PrefetchScalarGridSpec(
            num_scalar_prefetch=2, grid=(H // G, len(pairs)), in_specs=[qs, ks, ks, rs, rs, rs],
            out_specs=[pl.BlockSpec((T, G * HEAD_PAD), lambda h, s, qi, kj: (0, h)), ks, vs],
            scratch_shapes=[pltpu.VMEM((T, G * HEAD_PAD), F32), pltpu.VMEM((t, G * HEAD_PAD), F32),
                            pltpu.VMEM((t, G * V_DIM), F32)]),
        out_shape=[jax.ShapeDtypeStruct((T, H * HEAD_PAD), BF16), jax.ShapeDtypeStruct((T, H * HEAD_PAD), BF16),
                   jax.ShapeDtypeStruct((T, H * V_DIM), BF16)],
        compiler_params=_params(("parallel", "arbitrary")),
    )(qi, kj, qf, kf, v, do, lse, delta)


def _alibi_slopes():
    tot = DIL_GROUPS * DIL_HEADS
    return [float(np.float32(2.0) ** (np.float32(-8.0) * np.float32(k) / np.float32(tot))) for k in range(1, tot + 1)]


def _dil_masks():
    iq = lax.broadcasted_iota(jnp.int32, (DIL_BLK, DIL_BLK), 0)
    ik = lax.broadcasted_iota(jnp.int32, (DIL_BLK, DIL_BLK), 1)
    return (ik >= iq), (iq + DIL_BLK - ik).astype(F32), (ik <= iq), (iq - ik).astype(F32)


def _dil_norm(x, g):
    r = lax.rsqrt(jnp.mean(x * x, axis=-1, keepdims=True) + EPS)
    return x * r, r


DIL_SUPER = 8
BNT_DIMS = (((2,), (2,)), ((0,), (0,)))
BNN_DIMS = (((2,), (1,)), ((0,), (0,)))
BTN_DIMS = (((1,), (1,)), ((0,), (0,)))


def _dil_chunk(it, nb, d):
    assert nb & (nb - 1) == 0, nb
    r, n = it >> (nb.bit_length() - 1), it & (nb - 1)
    if d > 1:
        tok = pl.ds(n * (d * DIL_BLK) + r, DIL_BLK, stride=d)
    else:
        tok = pl.ds(pl.multiple_of(it * DIL_BLK, DIL_BLK), DIL_BLK)
    return tok, pl.ds(pl.multiple_of((it + 1) * DIL_BLK, DIL_BLK), DIL_BLK)


def _dil_token_rows(bidx, nb, d):
    r, n = divmod(bidx, nb)
    return pl.ds(n * DIL_BLK * d + r, DIL_BLK, stride=d) if d > 1 else pl.ds(bidx * DIL_BLK, DIL_BLK)


def _dil_super_rows(ss):
    base = (1 + ss * DIL_SUPER) * DIL_BLK
    return pl.ds(base, DIL_SUPER * DIL_BLK), pl.ds(base - DIL_BLK, DIL_SUPER * DIL_BLK)


def _dil_b3(x):
    return x.reshape(DIL_SUPER, DIL_BLK, x.shape[-1])


def _dil_scores(q3, kc3, kp3, slope, d, ss, nb):
    ok_p, dist_p, ok_c, dist_c = _dil_masks()
    scale = 1.0 / math.sqrt(DIL_HEAD_DIM)
    bias_p = jnp.where(ok_p, -slope * d * dist_p, NEG_BIG)
    bias_c = jnp.where(ok_c, -slope * d * dist_c, NEG_BIG)
    s_c = lax.dot_general(q3, kc3, BNT_DIMS, preferred_element_type=F32) * scale + bias_c[None]
    s_p = lax.dot_general(q3, kp3, BNT_DIMS, preferred_element_type=F32) * scale + bias_p[None]
    bidx = ss * DIL_SUPER + lax.broadcasted_iota(jnp.int32, s_p.shape, 0)
    s_p = jnp.where((bidx & (nb - 1)) == 0, NEG_BIG, s_p)
    return s_c, s_p


def _dil_fwd(qkv, g_qn, g_kn, slopes, name):
    T = qkv.shape[0]
    GH = DIL_GROUPS * DIL_HEADS
    scale = 1.0 / math.sqrt(DIL_HEAD_DIM)

    def body(sl_ref, q_ref, k_ref, v_ref, gq_ref, gk_ref, o_ref, lse_ref, qn_pm, kn_pm, v_pm):
        gh = pl.program_id(0)
        slope = sl_ref[gh]
        gq, gk = gq_ref[...], gk_ref[...]
        pad = pl.ds(0, DIL_BLK)
        kn_pm[pad, :] = jnp.zeros((DIL_BLK, DIL_HEAD_DIM), BF16)
        v_pm[pad, :] = jnp.zeros((DIL_BLK, DIL_HEAD_DIM), BF16)
        for g, (_, d) in enumerate(DIL_PAIRS):
            @pl.when((gh >= g * DIL_HEADS) & (gh < (g + 1) * DIL_HEADS))
            def _(d=d):
                nb = T // (d * DIL_BLK)

                def fill(it, _):
                    tok, dst = _dil_chunk(it, nb, d)
                    qn_pm[dst, :] = (_dil_norm(q_ref[tok, :], gq)[0] * gq).astype(BF16)
                    kn_pm[dst, :] = (_dil_norm(k_ref[tok, :], gk)[0] * gk).astype(BF16)
                    v_pm[dst, :] = v_ref[tok, :].astype(BF16)
                    return 0
                lax.fori_loop(0, T // DIL_BLK, fill, 0, unroll=4)

                for ss in range(T // DIL_BLK // DIL_SUPER):
                    cur, prv = _dil_super_rows(ss)
                    q3, kc3, kp3 = _dil_b3(qn_pm[cur, :]), _dil_b3(kn_pm[cur, :]), _dil_b3(kn_pm[prv, :])
                    s_c, s_p = _dil_scores(q3, kc3, kp3, slope, d, ss, nb)
                    m = jnp.max(jnp.maximum(s_c, s_p), axis=-1, keepdims=True)
                    p_c = jnp.exp(s_c - m)
                    p_p = jnp.exp(s_p - m)
                    l = jnp.sum(p_c, axis=-1, keepdims=True) + jnp.sum(p_p, axis=-1, keepdims=True)
                    acc = lax.dot_general(p_c.astype(BF16), _dil_b3(v_pm[cur, :]), BNN_DIMS, preferred_element_type=F32)
                    acc += lax.dot_general(p_p.astype(BF16), _dil_b3(v_pm[prv, :]), BNN_DIMS, preferred_element_type=F32)
                    o3 = acc / l
                    lse3 = jnp.broadcast_to(m + jnp.log(l), o3.shape)
                    for b in range(DIL_SUPER):
                        tok = _dil_token_rows(ss * DIL_SUPER + b, nb, d)
                        o_ref[tok, :] = o3[b]
                        lse_ref[tok, :] = lse3[b]

    col = lambda off: pl.BlockSpec((T, DIL_HEAD_DIM), lambda gh, sl: (0, gh + off))
    gvec = pl.BlockSpec((1, DIL_HEAD_DIM), lambda gh, sl: (0, 0))
    return pl.pallas_call(
        body, name=name,
        grid_spec=pltpu.PrefetchScalarGridSpec(
            num_scalar_prefetch=1, grid=(GH,),
            in_specs=[col(0), col(GH), col(2 * GH), gvec, gvec], out_specs=[col(0), col(0)],
            scratch_shapes=[pltpu.VMEM((DIL_BLK + T, DIL_HEAD_DIM), BF16)] * 3),
        out_shape=[jax.ShapeDtypeStruct((T, GH * DIL_HEAD_DIM), F32)] * 2,
        compiler_params=_params(("parallel",)),
    )(slopes, qkv, qkv, qkv, g_qn, g_kn)


def _dil_merge(o_g, lse_g, name):
    T = o_g.shape[0]
    W = DIL_HEADS * DIL_HEAD_DIM
    tr = _pick(T, (256, 128))

    def body(o0, o1, o2, l0, l1, l2, o_ref, lse_ref):
        a, b, c = l0[...], l1[...], l2[...]
        m = jnp.maximum(jnp.maximum(a, b), c)
        ea, eb, ec = jnp.exp(a - m), jnp.exp(b - m), jnp.exp(c - m)
        tot = ea + eb + ec
        o_ref[...] = ((o0[...] * ea + o1[...] * eb + o2[...] * ec) / tot).astype(BF16)
        lse_ref[...] = m + jnp.log(tot)

    grp = lambda g: pl.BlockSpec((tr, W), lambda i: (i, g))
    out = pl.BlockSpec((tr, W), lambda i: (i, 0))
    return pl.pallas_call(
        body, name=name, grid=(T // tr,),
        in_specs=[grp(0), grp(1), grp(2), grp(0), grp(1), grp(2)], out_specs=[out, out],
        out_shape=[jax.ShapeDtypeStruct((T, W), BF16), jax.ShapeDtypeStruct((T, W), F32)],
        compiler_params=_params(("parallel",)),
    )(o_g, o_g, o_g, lse_g, lse_g, lse_g)


def _dil_bwd(qkv, g_qn, g_kn, slopes, do, delta, lse, name):
    T = qkv.shape[0]
    GH = DIL_GROUPS * DIL_HEADS
    scale = 1.0 / math.sqrt(DIL_HEAD_DIM)
    nchunk = T // DIL_BLK

    def body(sl_ref, q_ref, k_ref, v_ref, gq_ref, gk_ref, do_ref, dl_ref, lse_ref,
             dq_ref, dk_ref, dv_ref, dgq_ref, dgk_ref,
             qn_pm, kn_pm, v_pm, do_pm, lse_pm, dl_pm, dq_pm, dk_pm, dv_pm, tok_sc):
        gh = pl.program_id(0)
        slope = sl_ref[gh]
        gq, gk = gq_ref[...], gk_ref[...]

        @pl.when(gh == 0)
        def _():
            dgq_ref[...] = jnp.zeros_like(dgq_ref)
            dgk_ref[...] = jnp.zeros_like(dgk_ref)

        pad = pl.ds(0, DIL_BLK)
        kn_pm[pad, :] = jnp.zeros((DIL_BLK, DIL_HEAD_DIM), BF16)
        v_pm[pad, :] = jnp.zeros((DIL_BLK, DIL_HEAD_DIM), BF16)
        dk_pm[...] = jnp.zeros_like(dk_pm)
        dv_pm[...] = jnp.zeros_like(dv_pm)
        for g, (_, d) in enumerate(DIL_PAIRS):
            @pl.when((gh >= g * DIL_HEADS) & (gh < (g + 1) * DIL_HEADS))
            def _(d=d):
                nb = T // (d * DIL_BLK)

                def fill(it, _):
                    tok, dst = _dil_chunk(it, nb, d)
                    qn_pm[dst, :] = (_dil_norm(q_ref[tok, :], gq)[0] * gq).astype(BF16)
                    kn_pm[dst, :] = (_dil_norm(k_ref[tok, :], gk)[0] * gk).astype(BF16)
                    v_pm[dst, :] = v_ref[tok, :].astype(BF16)
                    do_pm[dst, :] = do_ref[tok, :].astype(BF16)
                    lse_pm[dst, :] = lse_ref[tok, :]
                    dl_pm[dst, :] = dl_ref[tok, :]
                    return 0
                lax.fori_loop(0, nchunk, fill, 0, unroll=4)

                for ss in range(nchunk // DIL_SUPER):
                    cur, prv = _dil_super_rows(ss)
                    q3, kc3, kp3 = _dil_b3(qn_pm[cur, :]), _dil_b3(kn_pm[cur, :]), _dil_b3(kn_pm[prv, :])
                    vc3, vp3, do3 = _dil_b3(v_pm[cur, :]), _dil_b3(v_pm[prv, :]), _dil_b3(do_pm[cur, :])
                    ls = _dil_b3(lse_pm[cur, :])[:, :, :1]
                    delta = _dil_b3(dl_pm[cur, :])[:, :, :1]
                    s_c, s_p = _dil_scores(q3, kc3, kp3, slope, d, ss, nb)
                    p_c = jnp.exp(s_c - ls)
                    p_p = jnp.exp(s_p - ls)
                    dp_c = lax.dot_general(do3, vc3, BNT_DIMS, preferred_element_type=F32)
                    dp_p = lax.dot_general(do3, vp3, BNT_DIMS, preferred_element_type=F32)
                    ds_c = (p_c * (dp_c - delta)).astype(BF16)
                    ds_p = (p_p * (dp_p - delta)).astype(BF16)
                    dq3 = (lax.dot_general(ds_c, kc3, BNN_DIMS, preferred_element_type=F32)
                           + lax.dot_general(ds_p, kp3, BNN_DIMS, preferred_element_type=F32)) * scale
                    flat = lambda x: x.reshape(DIL_SUPER * DIL_BLK, DIL_HEAD_DIM)
                    dq_pm[cur, :] = flat(dq3)
                    dk_pm[cur, :] += flat(lax.dot_general(ds_c, q3, BTN_DIMS, preferred_element_type=F32)) * scale
                    dv_pm[cur, :] += flat(lax.dot_general(p_c.astype(BF16), do3, BTN_DIMS, preferred_element_type=F32))
                    dk_pm[prv, :] += flat(lax.dot_general(ds_p, q3, BTN_DIMS, preferred_element_type=F32)) * scale
                    dv_pm[prv, :] += flat(lax.dot_general(p_p.astype(BF16), do3, BTN_DIMS, preferred_element_type=F32))

                def to_tokens(src_pm):
                    def move(it, _):
                        tok, src = _dil_chunk(it, nb, d)
                        tok_sc[tok, :] = src_pm[src, :]
                        return 0
                    lax.fori_loop(0, nchunk, move, 0, unroll=4)

                def norm_bwd(x_ref, gvec, out_ref):
                    big = 4 * DIL_BLK

                    def fin(ci, dg):
                        rows = pl.ds(pl.multiple_of(ci * big, big), big)
                        xhat, r = _dil_norm(x_ref[rows, :], gvec)
                        dn = tok_sc[rows, :]
                        dxh = dn * gvec
                        c = jnp.mean(dxh * xhat, axis=-1, keepdims=True)
                        out_ref[rows, :] = (r * (dxh - xhat * c)).astype(BF16)
                        return dg + jnp.sum(dn * xhat, axis=0, keepdims=True)
                    return lax.fori_loop(0, T // big, fin, jnp.zeros((1, DIL_HEAD_DIM), F32))

                to_tokens(dq_pm)
                dgq_ref[...] += norm_bwd(q_ref, gq, dq_ref)
                to_tokens(dk_pm)
                dgk_ref[...] += norm_bwd(k_ref, gk, dk_ref)
                to_tokens(dv_pm)
                dv_ref[...] = tok_sc[...].astype(BF16)

    col = lambda off: pl.BlockSpec((T, DIL_HEAD_DIM), lambda gh, sl: (0, gh + off))
    hcol = pl.BlockSpec((T, DIL_HEAD_DIM), lambda gh, sl: (0, gh % DIL_HEADS))
    gvec = pl.BlockSpec((1, DIL_HEAD_DIM), lambda gh, sl: (0, 0))
    wide = jax.ShapeDtypeStruct((T, GH * DIL_HEAD_DIM), BF16)
    vec = jax.ShapeDtypeStruct((1, DIL_HEAD_DIM), F32)
    pm = lambda dt: pltpu.VMEM((DIL_BLK + T, DIL_HEAD_DIM), dt)
    return pl.pallas_call(
        body, name=name,
        grid_spec=pltpu.PrefetchScalarGridSpec(
            num_scalar_prefetch=1, grid=(GH,),
            in_specs=[col(0), col(GH), col(2 * GH), gvec, gvec, hcol, hcol, hcol],
            out_specs=[col(0), col(0), col(0), gvec, gvec],
            scratch_shapes=[pm(BF16)] * 4 + [pm(F32)] * 5 + [pltpu.VMEM((T, DIL_HEAD_DIM), F32)]),
        out_shape=[wide, wide, wide, vec, vec],
        compiler_params=_params(("arbitrary",)),
    )(slopes, qkv, qkv, qkv, g_qn, g_kn, do, delta, lse)


def _my_pos():
    return lax.axis_index("x"), lax.axis_index("y"), lax.axis_index("c")


def _peer(pos, j):
    x, y, c = pos
    px = 1 - x if j & 4 else x
    py = 1 - y if j & 2 else y
    pc = 1 - c if j & 1 else c
    return (px, py, pc), 4 * px + 2 * py + pc


def _slot(idx, paired):
    if not paired:
        return idx
    return jnp.where(idx < N_DEV // 2, 2 * idx, 2 * idx - (N_DEV - 1))


def _shard_slice(ref, axis, idx, size, paired=False):
    sl = [slice(None)] * len(ref.shape)
    sl[axis] = pl.ds(pl.multiple_of(_slot(idx, paired) * size, 8), size)
    return ref.at[tuple(sl)]


HBM_SPEC = pl.BlockSpec(memory_space=pltpu.HBM)
SEM_SPEC = pl.BlockSpec(memory_space=pltpu.SEMAPHORE)
DATAFLOW = pltpu.SideEffectType.DATAFLOW_SIDE_EFFECTING
N_PEER = N_DEV - 1


def _scatter_copy(axis, grad, slots, frm, to, dev, send_sem, recv_sem):
    ax, paired = axis
    src = _shard_slice(grad, ax, to, grad.shape[ax] // N_DEV, paired)
    return pltpu.make_async_remote_copy(src_ref=src, dst_ref=slots.at[frm], send_sem=send_sem, recv_sem=recv_sem,
                                        device_id=dev, device_id_type=MESH)


def _scatter_start(grads, axes, name):
    n = len(grads)

    def body(*refs):
        outs = refs[2 * n:]
        send, recv, token = outs[:n], outs[n:2 * n], outs[4 * n]
        pos = _my_pos()
        me = 4 * pos[0] + 2 * pos[1] + pos[2]
        for a in range(n):
            for j in range(1, N_DEV):
                dev, pid = _peer(pos, j)
                _scatter_copy(axes[a], refs[2 * a], refs[2 * a + 1], me, pid, dev, send[a].at[j - 1],
                              recv[a].at[j - 1]).start()
        token[...] = jnp.zeros_like(token)

    ops = []
    for g, (ax, _) in zip(grads, axes):
        shp = list(g.shape)
        shp[ax] //= N_DEV
        ops += [g, lax.empty((N_DEV,) + tuple(shp), g.dtype)]
    sems = [pltpu.SemaphoreType.DMA((N_PEER,))] * (2 * n)
    res = pl.pallas_call(
        body, name=name,
        out_shape=sems + [pltpu.HBM(o.shape, o.dtype) for o in ops] + [jax.ShapeDtypeStruct((8, LANES), F32)],
        in_specs=[HBM_SPEC] * len(ops),
        out_specs=[SEM_SPEC] * (2 * n) + [HBM_SPEC] * len(ops) + [pl.BlockSpec(memory_space=pltpu.VMEM)],
        input_output_aliases={i: 2 * n + i for i in range(len(ops))},
        compiler_params=pltpu.CompilerParams(has_side_effects=DATAFLOW),
    )(*[pltpu.with_memory_space_constraint(o, pltpu.HBM) for o in ops])
    items = [(res[a], res[n + a], res[2 * n + 2 * a], res[2 * n + 2 * a + 1]) for a in range(n)]
    return items, res[4 * n]


def _scatter_wait(items, axes, after, name):
    n = len(items)

    def body(*refs):
        send, recv = refs[2 * n:3 * n], refs[3 * n:4 * n]
        pos = _my_pos()
        me = 4 * pos[0] + 2 * pos[1] + pos[2]
        for a in range(n):
            for j in range(1, N_DEV):
                dev, pid = _peer(pos, j)
                cp = _scatter_copy(axes[a], refs[2 * a], refs[2 * a + 1], pid, me, dev, send[a].at[j - 1],
                                   recv[a].at[j - 1])
                cp.wait_send()
                cp.wait_recv()

    ops = [b for it in items for b in it[2:]]
    res = pl.pallas_call(
        body, name=name,
        out_shape=[pltpu.HBM(o.shape, o.dtype) for o in ops],
        in_specs=[HBM_SPEC] * len(ops) + [SEM_SPEC] * (2 * n) + [ANY_SPEC],
        out_specs=[HBM_SPEC] * len(ops),
        input_output_aliases={i: i for i in range(len(ops))},
        compiler_params=pltpu.CompilerParams(has_side_effects=DATAFLOW),
    )(*ops, *[it[0] for it in items], *[it[1] for it in items], after)
    return [(res[2 * a], res[2 * a + 1]) for a in range(n)]


SIBLING = 1
ICI_PEERS = (2, 4, 6)


def _gather_copy(buf, axis, shard, dev, send_sem, recv_sem):
    ax, paired = axis
    piece = _shard_slice(buf, ax, shard, buf.shape[ax] // N_DEV, paired)
    return pltpu.make_async_remote_copy(src_ref=piece, dst_ref=piece, send_sem=send_sem, recv_sem=recv_sem,
                                        device_id=dev, device_id_type=MESH)


def _gather_start(bufs, axes, name):
    n = len(bufs)

    def body(*refs):
        ins, outs = refs[:n], refs[n:]
        send, r_sib, r_ici, token = outs[:n], outs[n:2 * n], outs[2 * n:3 * n], outs[4 * n]
        pos = _my_pos()
        me = 4 * pos[0] + 2 * pos[1] + pos[2]
        for a in range(n):
            dev, _ = _peer(pos, SIBLING)
            _gather_copy(ins[a], axes[a], me, dev, send[a].at[0], r_sib[a].at[0]).start()
            for k, j in enumerate(ICI_PEERS):
                dev, _ = _peer(pos, j)
                _gather_copy(ins[a], axes[a], me, dev, send[a].at[1 + k], r_ici[a].at[k]).start()
        token[...] = jnp.zeros_like(token)

    sems = ([pltpu.SemaphoreType.DMA((1 + len(ICI_PEERS),))] * n + [pltpu.SemaphoreType.DMA((1,))] * n
            + [pltpu.SemaphoreType.DMA((len(ICI_PEERS),))] * n)
    res = pl.pallas_call(
        body, name=name,
        out_shape=sems + [pltpu.HBM(b.shape, b.dtype) for b in bufs] + [jax.ShapeDtypeStruct((8, LANES), F32)],
        in_specs=[HBM_SPEC] * n,
        out_specs=[SEM_SPEC] * (3 * n) + [HBM_SPEC] * n + [pl.BlockSpec(memory_space=pltpu.VMEM)],
        input_output_aliases={i: 3 * n + i for i in range(n)},
        compiler_params=pltpu.CompilerParams(has_side_effects=DATAFLOW),
    )(*[pltpu.with_memory_space_constraint(b, pltpu.HBM) for b in bufs])
    items = [dict(send=res[a], r_sib=res[n + a], r_ici=res[2 * n + a], buf=res[3 * n + a]) for a in range(n)]
    return items, res[4 * n]


def _gather_relay(items, axes, after, name):
    n = len(items)

    def body(*refs):
        ins, r_ici = refs[:n], refs[n:2 * n]
        outs = refs[2 * n + 1:]
        s_rel, r_rel, token = outs[:n], outs[n:2 * n], outs[3 * n]
        pos = _my_pos()
        sib, _ = _peer(pos, SIBLING)
        for a in range(n):
            for k, j in enumerate(ICI_PEERS):
                dev, pid = _peer(pos, j)
                _gather_copy(ins[a], axes[a], pid, dev, s_rel[a].at[k], r_ici[a].at[k]).wait_recv()
                _gather_copy(ins[a], axes[a], pid, sib, s_rel[a].at[k], r_rel[a].at[k]).start()
        token[...] = jnp.zeros_like(token)

    bufs = [it["buf"] for it in items]
    sems = [pltpu.SemaphoreType.DMA((len(ICI_PEERS),))] * (2 * n)
    res = pl.pallas_call(
        body, name=name,
        out_shape=sems + [pltpu.HBM(b.shape, b.dtype) for b in bufs] + [jax.ShapeDtypeStruct((8, LANES), F32)],
        in_specs=[HBM_SPEC] * n + [SEM_SPEC] * n + [ANY_SPEC],
        out_specs=[SEM_SPEC] * (2 * n) + [HBM_SPEC] * n + [pl.BlockSpec(memory_space=pltpu.VMEM)],
        input_output_aliases={i: 2 * n + i for i in range(n)},
        compiler_params=pltpu.CompilerParams(has_side_effects=DATAFLOW),
    )(*bufs, *[it["r_ici"] for it in items], after)
    out = [dict(send=it["send"], r_sib=it["r_sib"], s_rel=res[a], r_rel=res[n + a], buf=res[2 * n + a])
           for a, it in enumerate(items)]
    return out, res[3 * n]


def _gather_wait(items, axes, after, name):
    n = len(items)

    def body(*refs):
        ins = refs[:n]
        send, r_sib, s_rel, r_rel = (refs[(1 + q) * n:(2 + q) * n] for q in range(4))
        pos = _my_pos()
        me = 4 * pos[0] + 2 * pos[1] + pos[2]
        sib, sib_id = _peer(pos, SIBLING)
        for a in range(n):
            for k in range(1 + len(ICI_PEERS)):
                _gather_copy(ins[a], axes[a], me, sib, send[a].at[k], r_sib[a].at[0]).wait_send()
            _gather_copy(ins[a], axes[a], sib_id, sib, send[a].at[0], r_sib[a].at[0]).wait_recv()
            for k, j in enumerate(ICI_PEERS):
                _, pid = _peer(pos, j)
                _, far = _peer(pos, j ^ SIBLING)
                _gather_copy(ins[a], axes[a], pid, sib, s_rel[a].at[k], r_rel[a].at[k]).wait_send()
                _gather_copy(ins[a], axes[a], far, sib, s_rel[a].at[k], r_rel[a].at[k]).wait_recv()

    bufs = [it["buf"] for it in items]
    res = pl.pallas_call(
        body, name=name,
        out_shape=[pltpu.HBM(b.shape, b.dtype) for b in bufs],
        in_specs=[HBM_SPEC] * n + [SEM_SPEC] * (4 * n) + [ANY_SPEC],
        out_specs=[HBM_SPEC] * n,
        input_output_aliases={i: i for i in range(n)},
        compiler_params=pltpu.CompilerParams(has_side_effects=DATAFLOW),
    )(*bufs, *[it["send"] for it in items], *[it["r_sib"] for it in items], *[it["s_rel"] for it in items],
      *[it["r_rel"] for it in items], after)
    return list(res)


def _gain_allreduce(v, name):
    n = v.shape[1]

    def body(v_ref, o_ref, slots, send_sems, recv_sems):
        pos = _my_pos()
        me = 4 * pos[0] + 2 * pos[1] + pos[2]
        slots[me] = v_ref[...]
        copies = []
        for j in range(1, N_DEV):
            dev, _ = _peer(pos, j)
            cp = pltpu.make_async_remote_copy(
                src_ref=slots.at[me], dst_ref=slots.at[me], send_sem=send_sems.at[j], recv_sem=recv_sems.at[j],
                device_id=dev, device_id_type=MESH)
            cp.start()
            copies.append(cp)
        for j in range(1, N_DEV):
            dev, pid = _peer(pos, j)
            pltpu.make_async_remote_copy(
                src_ref=slots.at[me], dst_ref=slots.at[pid], send_sem=send_sems.at[j], recv_sem=recv_sems.at[j],
                device_id=dev, device_id_type=MESH).wait_recv()
        for cp in copies:
            cp.wait_send()
        acc = slots[0]
        for s in range(1, N_DEV):
            acc = acc + slots[s]
        o_ref[...] = acc

    return pl.pallas_call(
        body, name=name, out_shape=jax.ShapeDtypeStruct((1, n), F32),
        in_specs=[pl.BlockSpec(memory_space=pltpu.VMEM)], out_specs=pl.BlockSpec(memory_space=pltpu.VMEM),
        scratch_shapes=[pltpu.VMEM((N_DEV, 1, n), F32), pltpu.SemaphoreType.DMA((N_DEV,)),
                        pltpu.SemaphoreType.DMA((N_DEV,))],
        compiler_params=pltpu.CompilerParams(has_side_effects=True),
    )(v)


def _adamw(parts, own, me, w, m, v, layer, prev, name, own_axis=None):
    L, R, C = w.shape
    P = parts.shape[0]
    tr = _pick(R, (128, 64, 32, 16, 8, 1))
    c1 = 1.0 - ADAM_B1 ** ADAM_STEP
    c2 = 1.0 - ADAM_B2 ** ADAM_STEP
    n_in = 4 if own is None else 5

    def body(me_ref, *refs):
        p_ref = refs[0]
        w_ref, m_ref, v_ref = refs[n_in - 3:n_in]
        g_out, d_out, m_out, v_out, tok = refs[-5:]
        g = None
        for s in range(P):
            part = p_ref[s]
            if own is not None:
                part = jnp.where(me_ref[0] == s, refs[1][...], part)
            g = part.astype(F32) if g is None else g + part.astype(F32)
        mn = ADAM_B1 * m_ref[...] + (1.0 - ADAM_B1) * g
        vn = ADAM_B2 * v_ref[...] + (1.0 - ADAM_B2) * (g * g)
        g_out[...] = g
        m_out[...] = mn
        v_out[...] = vn
        d_out[...] = -ADAM_LR * ((mn / c1) / (jnp.sqrt(vn / c2) + ADAM_EPS) + ADAM_WD * w_ref[...])
        tok[...] = jnp.zeros_like(tok)

    row = pl.BlockSpec((None, tr, C), lambda i, me_ref: (layer, i, 0))
    in_specs = [pl.BlockSpec((P, tr, C), lambda i, me_ref: (0, i, 0))]
    args = [parts]
    if own is not None:
        if own_axis is None:
            own_idx = lambda i, me_ref: (i, 0)
        elif own_axis[0] == 0:
            own_idx = lambda i, me_ref: (_slot(me_ref[0], own_axis[1]) * (R // tr) + i, 0)
        else:
            own_idx = lambda i, me_ref: (i, _slot(me_ref[0], own_axis[1]))
        in_specs.append(pl.BlockSpec((tr, C), own_idx))
        args.append(own)
    in_specs += [row, row, row]
    args += [w, m, v]
    aliases = {}
    if prev is not None:
        in_specs += [ANY_SPEC] * 4
        aliases = {1 + len(args) + k: k for k in range(4)}
        args += list(prev)
    shp = jax.ShapeDtypeStruct((L, R, C), F32)
    res = pl.pallas_call(
        body, name=name,
        grid_spec=pltpu.PrefetchScalarGridSpec(
            num_scalar_prefetch=1, grid=(R // tr,), in_specs=in_specs,
            out_specs=[row] * 4 + [pl.BlockSpec((8, LANES), lambda i, me_ref: (0, 0))]),
        out_shape=[shp] * 4 + [jax.ShapeDtypeStruct((8, LANES), F32)],
        input_output_aliases=aliases, compiler_params=_params(("arbitrary",)),
    )(me, *args)
    return res[:4], res[4]


def _pad_heads(w):
    lead = w.shape[:-1]
    n = w.shape[-1] // QK_DIM
    w = w.reshape(lead + (n, QK_DIM))
    w = jnp.pad(w, [(0, 0)] * len(lead) + [(0, 0), (0, HEAD_PAD - QK_DIM)])
    return w.reshape(lead + (n * HEAD_PAD,))


def _unpad_heads(w):
    lead = w.shape[:-1]
    n = w.shape[-1] // HEAD_PAD
    return w.reshape(lead + (n, HEAD_PAD))[..., :QK_DIM].reshape(lead + (n * QK_DIM,))


def kernel(x, ffn1_norm, ffn1_w_in, ffn1_w_out, mix_norm, ffn2_norm, ffn2_w_in, ffn2_w_out, mla_w_down, mla_g_cq, mla_g_ckv, mla_w_uq, mla_w_ukv, mla_g_qn, mla_g_kn, mla_w_o, dil_w_qkv, dil_g_qn, dil_g_kn, dil_w_o, loss_target, m_ffn1_norm, m_ffn1_w_in, m_ffn1_w_out, m_mix_norm, m_ffn2_norm, m_ffn2_w_in, m_ffn2_w_out, m_mla_w_down, m_mla_g_cq, m_mla_g_ckv, m_mla_w_uq, m_mla_w_ukv, m_mla_g_qn, m_mla_g_kn, m_mla_w_o, m_dil_w_qkv, m_dil_g_qn, m_dil_g_kn, m_dil_w_o, v_ffn1_norm, v_ffn1_w_in, v_ffn1_w_out, v_mix_norm, v_ffn2_norm, v_ffn2_w_in, v_ffn2_w_out, v_mla_w_down, v_mla_g_cq, v_mla_g_ckv, v_mla_w_uq, v_mla_w_ukv, v_mla_g_qn, v_mla_g_kn, v_mla_w_o, v_dil_w_qkv, v_dil_g_qn, v_dil_g_kn, v_dil_w_o):
    names = ["ffn1_norm", "ffn1_w_in", "ffn1_w_out", "mix_norm", "ffn2_norm", "ffn2_w_in", "ffn2_w_out", "mla_w_down",
             "mla_g_cq", "mla_g_ckv", "mla_w_uq", "mla_w_ukv", "mla_g_qn", "mla_g_kn", "mla_w_o", "dil_w_qkv",
             "dil_g_qn", "dil_g_kn", "dil_w_o"]
    W = dict(zip(names, [ffn1_norm, ffn1_w_in, ffn1_w_out, mix_norm, ffn2_norm, ffn2_w_in, ffn2_w_out, mla_w_down,
                         mla_g_cq, mla_g_ckv, mla_w_uq, mla_w_ukv, mla_g_qn, mla_g_kn, mla_w_o, dil_w_qkv,
                         dil_g_qn, dil_g_kn, dil_w_o]))
    M1 = dict(zip(names, [m_ffn1_norm, m_ffn1_w_in, m_ffn1_w_out, m_mix_norm, m_ffn2_norm, m_ffn2_w_in, m_ffn2_w_out,
                          m_mla_w_down, m_mla_g_cq, m_mla_g_ckv, m_mla_w_uq, m_mla_w_ukv, m_mla_g_qn, m_mla_g_kn,
                          m_mla_w_o, m_dil_w_qkv, m_dil_g_qn, m_dil_g_kn, m_dil_w_o]))
    V2 = dict(zip(names, [v_ffn1_norm, v_ffn1_w_in, v_ffn1_w_out, v_mix_norm, v_ffn2_norm, v_ffn2_w_in, v_ffn2_w_out,
                          v_mla_w_down, v_mla_g_cq, v_mla_g_ckv, v_mla_w_uq, v_mla_w_ukv, v_mla_g_qn, v_mla_g_kn,
                          v_mla_w_o, v_dil_w_qkv, v_dil_g_qn, v_dil_g_kn, v_dil_w_o]))
    S, D = x.shape[1], x.shape[2]
    x0 = x.reshape(S, D)
    tgt = loss_target.reshape(S, D)

    big = ["ffn1_w_in", "ffn1_w_out", "ffn2_w_in", "ffn2_w_out", "mla_w_down", "mla_w_uq", "mla_w_ukv", "mla_w_o",
           "dil_w_qkv", "dil_w_o"]
    shard_dim = {"ffn1_w_in": 2, "ffn1_w_out": 1, "ffn2_w_in": 2, "ffn2_w_out": 1, "mla_w_down": 1, "mla_w_uq": 2,
                 "mla_w_ukv": 2, "mla_w_o": 1, "dil_w_qkv": 2, "dil_w_o": 2}
    paired = ("ffn1_w_in", "ffn2_w_in")
    shard_axis = {n: (d, n in paired) for n, d in shard_dim.items()}
    grad_axis = {n: (d - 1, n in paired) for n, d in shard_dim.items()}

    def padded(n, w):
        if n == "mla_w_down":
            return jnp.pad(w, ((0, 0), (0, 0), (0, LAT_PAD - w.shape[2])))
        if n == "mla_w_uq":
            return _pad_heads(w)
        return w

    depth = ffn1_norm.shape[0]
    blocks = []
    for l in range(depth):
        mixer = (["mla_w_down", "mla_w_uq", "mla_w_ukv", "mla_w_o"] if l % 2 == 0 else ["dil_w_qkv", "dil_w_o"])
        blocks.append((f"ffn1_{l}", [("ffn1_w_in", l), ("ffn1_w_out", l)]))
        blocks.append((f"mix_{l}", [(n, l // 2) for n in mixer]))
        blocks.append((f"ffn2_{l}", [("ffn2_w_in", l), ("ffn2_w_out", l)]))
    order = [k for _, keys in blocks for k in keys]
    me = (4 * lax.axis_index("x") + 2 * lax.axis_index("y") + lax.axis_index("c")).astype(jnp.int32).reshape(1)
    def cast(key, deps=()):
        n, l = key
        return _cast_into_gathered(padded(n, W[n]), l, shard_axis[n], me, f"cast_{n}_{l}", deps=deps)

    items0, token0 = _gather_start([cast(order[0])], [shard_axis[order[0][0]]], "gather_start_first")
    rest = order[1:]
    items1, ag_token = _gather_start([cast(k, deps=[token0]) for k in rest], [shard_axis[k[0]] for k in rest],
                                     "gather_start_rest")
    ag_items = dict(zip(order, items0 + items1))
    full = {}

    def relay(keys, after, tag):
        out, token = _gather_relay([ag_items[k] for k in keys], [shard_axis[k[0]] for k in keys], after,
                                   f"gather_relay_{tag}")
        ag_items.update(zip(keys, out))
        return [token]

    def relay_next(bi, after):
        return relay(blocks[bi + 1][1], after, blocks[bi + 1][0]) if bi + 1 < len(blocks) else []

    def fetch(keys, after, tag):
        lands = _gather_wait([ag_items[k] for k in keys], [shard_axis[k[0]] for k in keys], after,
                             f"gather_wait_{tag}")
        full.update(zip(keys, lands))

    g_qn = _pad_heads(mla_g_qn)
    g_kn = _pad_heads(mla_g_kn)
    tabs = _rope_tables(S)
    slopes = jnp.asarray(_alibi_slopes(), F32)

    grads = {}
    gain_g = {}

    out_g, out_d, out_m, out_v = {}, {}, {}, {}
    pending = []
    lag = 3

    def scatter_start(tag, keys):
        items, token = _scatter_start([grads[k] for k in keys], [grad_axis[k[0]] for k in keys],
                                      f"scatter_start_{tag}")
        pending.append((tag, keys, items))
        return token

    def scatter_finish(after):
        tag, keys, items = pending.pop(0)
        lands = _scatter_wait(items, [grad_axis[k[0]] for k in keys], after, f"scatter_wait_{tag}")
        tokens = []
        for (n, l), (own, p) in zip(keys, lands):
            own_axis = grad_axis[n]
            if n in ("mla_w_down", "mla_w_uq"):
                ax, pair = grad_axis[n]
                size = own.shape[ax] // N_DEV
                own = lax.dynamic_slice_in_dim(own, _slot(me[0], pair) * size, size, axis=ax)
                own_axis = None
                if n == "mla_w_down":
                    p, own = p[..., :W[n].shape[2]], own[..., :W[n].shape[2]]
                else:
                    p, own = _unpad_heads(p), _unpad_heads(own)
            prev = (out_g[n], out_d[n], out_m[n], out_v[n]) if n in out_g else None
            (out_g[n], out_d[n], out_m[n], out_v[n]), tok = _adamw(p, own, me, W[n], M1[n], V2[n], l, prev,
                                                                    f"adamw_{n}_{l}", own_axis=own_axis)
            tokens.append(tok)
        return tokens

    def finish_due(after):
        tokens = []
        while len(pending) > lag:
            tokens += scatter_finish(after)
        return tokens

    def ffn_fwd(xin, norm_row, which, l, bi, deps=()):
        tag = blocks[bi][0]
        k_in, k_out = (which + "_w_in", l), (which + "_w_out", l)
        h = _rms_fwd(xin, norm_row, f"rms_fwd_{tag}", deps=deps)
        if bi == 0:
            relay([k_in], h, f"{tag}_in")
        fetch([k_in], h, f"in_{tag}")
        u, a = _ffn_in(h, full[k_in], f"ffn_in_{tag}")
        if bi == 0:
            relay([k_out], a, f"{tag}_out")
        fetch([k_out], a, f"out_{tag}")
        toks = relay_next(bi, a)
        xo = _mm(a, full[k_out], "nn", F32, f"mm_out_{tag}", scale=0.5, res=xin, layer=0, deps=toks)
        return xo, (xin, h, u, a)

    def ffn_bwd(dx_pair, saved, norm_row, which, l, tag, deps=()):
        dxo, dxob = dx_pair
        k_in, k_out = (which + "_w_in", l), (which + "_w_out", l)
        xin, h, u, a = saved
        grads[k_out] = _mm(a, dxob, "tn", BF16, f"mm_dwout_{tag}", scale=0.5, deps=deps)
        t_out = scatter_start(f"{tag}_out", [k_out])
        du = _ffn_da(dxob, full[k_out], u, f"ffn_da_{tag}", deps=[t_out])
        grads[k_in] = _mm(h, du, "tn", BF16, f"mm_dwin_{tag}")
        t_in = scatter_start(f"{tag}_in", [k_in])
        dh = _mm(du, full[k_in], "nt", F32, f"mm_dh_{tag}", layer=0, deps=[t_in])
        toks = finish_due(dh)
        dx, dxb, dg = _rms_bwd(xin, norm_row, dh, dxo, f"rms_bwd_{tag}", deps=toks)
        gain_g.setdefault(which + "_norm", {})[l] = dg
        return dx, dxb

    def mla_fwd(xin, l, bi):
        j = l // 2
        xn = _rms_fwd(xin, mix_norm[l:l + 1], "rms_fwd_mla")
        fetch([(n, j) for n in ("mla_w_down", "mla_w_uq", "mla_w_ukv", "mla_w_o")], xn, "mla")
        lat = _mm(xn, full[("mla_w_down", j)], "nn", F32, "mm_lat", layer=0)
        cq, ckv = _lat_norm_fwd(lat, mla_g_cq[j:j + 1], mla_g_ckv[j:j + 1], "lat_norm_fwd")
        q_raw = _mm(cq, full[("mla_w_uq", j)], "nn", F32, "mm_uq", layer=0)
        kv = _mm(ckv, full[("mla_w_ukv", j)], "nn", F32, "mm_ukv", layer=0)
        qf, kf, vb = _mla_prep_fwd(q_raw, kv, lat, g_qn[j:j + 1], g_kn[j:j + 1], tabs, "mla_prep_fwd")
        o, lse = _flash_fwd(qf, kf, vb, "flash_fwd")
        toks = relay_next(bi, o)
        xo = _mm(o, full[("mla_w_o", j)], "nn", F32, "mm_mla_o", res=xin, layer=0, deps=toks)
        return xo, (xin, xn, lat, cq, ckv, q_raw, kv, qf, kf, vb, o, lse)

    def mla_bwd(dx_pair, saved, l):
        dxo, dxob = dx_pair
        j = l // 2
        xin, xn, lat, cq, ckv, q_raw, kv, qf, kf, vb, o, lse = saved
        do = _mm(dxob, full[("mla_w_o", j)], "nt", BF16, "mm_mla_do", layer=0)
        grads[("mla_w_o", j)] = _mm(o, dxob, "tn", BF16, "mm_mla_dwo")
        delta = _attn_delta(do, o, "attn_delta")
        dqf, dkf, dv = _flash_bwd(qf, kf, vb, do, lse, delta, "flash_bwd")
        dq_raw, dkv, dkpe, dgq, dgk = _mla_prep_bwd(q_raw, kv, lat, g_qn[j:j + 1], g_kn[j:j + 1], tabs, dqf, dkf, dv,
                                                    "mla_prep_bwd")
        gain_g.setdefault("mla_g_qn", {})[j] = dgq
        gain_g.setdefault("mla_g_kn", {})[j] = dgk
        dcq = _mm(dq_raw, full[("mla_w_uq", j)], "nt", F32, "mm_dcq", layer=0)
        grads[("mla_w_uq", j)] = _mm(cq, dq_raw, "tn", BF16, "mm_dwuq")
        dckv = _mm(dkv, full[("mla_w_ukv", j)], "nt", F32, "mm_dckv", layer=0)
        grads[("mla_w_ukv", j)] = _mm(ckv, dkv, "tn", BF16, "mm_dwukv")
        dlat, dgcq, dgckv = _lat_norm_bwd(lat, mla_g_cq[j:j + 1], mla_g_ckv[j:j + 1], dcq, dckv, dkpe, "lat_norm_bwd")
        gain_g.setdefault("mla_g_cq", {})[j] = dgcq
        gain_g.setdefault("mla_g_ckv", {})[j] = dgckv
        dxn = _mm(dlat, full[("mla_w_down", j)], "nt", F32, "mm_dxn_mla", layer=0)
        grads[("mla_w_down", j)] = _mm(xn, dlat, "tn", BF16, "mm_dwdown")
        tok = scatter_start(f"mix_{l}", [(n, j) for n in ("mla_w_down", "mla_w_uq", "mla_w_ukv", "mla_w_o")])
        toks = finish_due(dxn)
        dx, dxb, dg = _rms_bwd(xin, mix_norm[l:l + 1], dxn, dxo, "rms_bwd_mla", deps=[tok] + toks)
        gain_g.setdefault("mix_norm", {})[l] = dg
        return dx, dxb

    def dil_fwd(xin, l, bi):
        j = l // 2
        xn = _rms_fwd(xin, mix_norm[l:l + 1], "rms_fwd_dil")
        fetch([("dil_w_qkv", j), ("dil_w_o", j)], xn, "dil")
        qkv = _mm(xn, full[("dil_w_qkv", j)], "nn", F32, "mm_qkv", layer=0)
        o_g, lse_g = _dil_fwd(qkv, dil_g_qn[j:j + 1], dil_g_kn[j:j + 1], slopes, "dil_fwd")
        o, lse = _dil_merge(o_g, lse_g, "dil_merge")
        toks = relay_next(bi, o)
        xo = _mm(o, full[("dil_w_o", j)], "nn", F32, "mm_dil_o", res=xin, layer=0, deps=toks)
        return xo, (xin, xn, qkv, o, lse)

    def dil_bwd(dx_pair, saved, l):
        dxo, dxob = dx_pair
        j = l // 2
        xin, xn, qkv, o, lse = saved
        do = _mm(dxob, full[("dil_w_o", j)], "nt", F32, "mm_dil_do", layer=0)
        grads[("dil_w_o", j)] = _mm(o, dxob, "tn", BF16, "mm_dil_dwo")
        delta = _attn_delta(do, o, "dil_delta")
        dq, dk, dv, dgq, dgk = _dil_bwd(qkv, dil_g_qn[j:j + 1], dil_g_kn[j:j + 1], slopes, do, delta, lse, "dil_bwd")
        gain_g.setdefault("dil_g_qn", {})[j] = dgq
        gain_g.setdefault("dil_g_kn", {})[j] = dgk
        dqkv = jnp.concatenate([dq, dk, dv], axis=1)
        dxn = _mm(dqkv, full[("dil_w_qkv", j)], "nt", F32, "mm_dxn_dil", layer=0)
        grads[("dil_w_qkv", j)] = _mm(xn, dqkv, "tn", BF16, "mm_dwqkv")
        tok = scatter_start(f"mix_{l}", [("dil_w_qkv", j), ("dil_w_o", j)])
        toks = finish_due(dxn)
        dx, dxb, dg = _rms_bwd(xin, mix_norm[l:l + 1], dxn, dxo, "rms_bwd_dil", deps=[tok] + toks)
        gain_g.setdefault("mix_norm", {})[l] = dg
        return dx, dxb

    saved = []
    xc = x0
    for l in range(depth):
        xc, s1 = ffn_fwd(xc, ffn1_norm[l:l + 1], "ffn1", l, 3 * l, deps=[ag_token] if l == 0 else ())
        xc, s2 = (mla_fwd if l % 2 == 0 else dil_fwd)(xc, l, 3 * l + 1)
        xc, s3 = ffn_fwd(xc, ffn2_norm[l:l + 1], "ffn2", l, 3 * l + 2)
        saved.append((s1, s2, s3))

    dy, dyb, loss_part = _loss_head(xc, tgt, "loss_head")
    dx = (dy, dyb)
    loss = lax.psum(loss_part[0, 0], MESH_AXES)

    for bi in reversed(range(len(blocks))):
        tag, _ = blocks[bi]
        l = bi // 3
        s = saved[l][bi % 3]
        if bi % 3 == 2:
            dx = ffn_bwd(dx, s, ffn2_norm[l:l + 1], "ffn2", l, tag,
                         deps=[loss.reshape(1, 1)] if bi == len(blocks) - 1 else ())
        elif bi % 3 == 1:
            dx = (mla_bwd if l % 2 == 0 else dil_bwd)(dx, s, l)
        else:
            dx = ffn_bwd(dx, s, ffn1_norm[l:l + 1], "ffn1", l, tag)
    grad_x = dx[0].reshape(x.shape)

    small = [n for n in names if n not in big]

    def gain_local(n):
        rows = [gain_g[n][l] for l in range(W[n].shape[0])]
        g = jnp.concatenate(rows, axis=1)
        return g

    def flat_pad(n, a):
        a = a.reshape(1, -1)
        if n in ("mla_g_qn", "mla_g_kn"):
            a = _pad_heads(a)
        return a

    packed_g = jnp.concatenate([gain_local(n) for n in small], axis=1)
    sizes = [gain_local(n).shape[1] for n in small]
    tot_g = _gain_allreduce(packed_g, "gain_allreduce")
    pw = jnp.concatenate([flat_pad(n, W[n]) for n in small], axis=1)
    pm = jnp.concatenate([flat_pad(n, M1[n]) for n in small], axis=1)
    pv = jnp.concatenate([flat_pad(n, V2[n]) for n in small], axis=1)
    res, after = _adamw(tot_g.reshape(1, 1, -1), None, me, pw.reshape(1, 1, -1), pm.reshape(1, 1, -1),
                        pv.reshape(1, 1, -1), 0, None, "adamw_gains")
    while pending:
        after = scatter_finish(after)[-1]
    res = [r.reshape(1, -1) for r in res]
    off = 0
    for n, sz in zip(small, sizes):
        for dst, r in zip((out_g, out_d, out_m, out_v), res):
            piece = r[:, off:off + sz]
            if n in ("mla_g_qn", "mla_g_kn"):
                piece = _unpad_heads(piece)
            dst[n] = piece.reshape(W[n].shape)
        off += sz

    return (loss, grad_x, *[out_g[n] for n in names], *[out_d[n] for n in names],
            *[out_m[n] for n in names], *[out_v[n] for n in names])
```

```python
import functools
import math

import jax
import jax.numpy as jnp
import numpy as np
from jax import lax
from jax.experimental import pallas as pl
from jax.experimental.pallas import tpu as pltpu

EPS = 1e-6
MLA_HEADS = 16
Q_LORA = 512
KV_LORA = 512
NOPE_DIM = 128
ROPE_DIM = 64
V_DIM = 128
QK_DIM = NOPE_DIM + ROPE_DIM
ROPE_THETA = 10000.0
HEAD_PAD = 256
LAT_PAD = Q_LORA + KV_LORA + 128
DIL_PAIRS = ((128, 1), (512, 4), (2048, 16))
DIL_GROUPS = 3
DIL_HEADS = 8
DIL_HEAD_DIM = 128
DIL_BLK = 128
FLASH_HEADS = 2
FLASH_HEADS_FWD = 4
LOG2E = math.log2(math.e)
LN2 = math.log(2.0)
ADAM_LR = 0.001
ADAM_B1 = 0.9
ADAM_B2 = 0.999
ADAM_EPS = 1e-08
ADAM_WD = 0.01
ADAM_STEP = 10

N_DEV = 8
MESH_AXES = ("x", "y", "c")
MESH = pl.DeviceIdType.MESH
NEG_BIG = -1e30
VMEM_LIMIT_V7X = 56 * 1024 * 1024
LANES = 128

BF16 = jnp.bfloat16
F32 = jnp.float32


def _pick(n, cands):
    for c in cands:
        if n % c == 0:
            return c
    raise ValueError(f"no tile for {n}")


def _params(sem):
    return pltpu.CompilerParams(dimension_semantics=sem, vmem_limit_bytes=VMEM_LIMIT_V7X)


ANY_SPEC = pl.BlockSpec(memory_space=pl.ANY)


MM_VMEM_BUDGET = 44 * 1024 * 1024
MM_HBM_BYTES_PER_S = 1.8e12
MM_MXU_FLOPS_PER_S = 8.5e14
MM_STEP_S = 0.4e-6
MM_MAX_TILE_MACS = 3.3e9
MXU_DIM = 256


MM_TIMED_TILES = {
    (2048, 4096, 11264, 2, 2, 2, False): (1024, 1024, 4096, True),
    (4096, 11264, 2048, 2, 2, 4, False): (1024, 1024, 2816, True),
    (4096, 5632, 2048, 2, 2, 4, True): (512, 1024, 5632, False),
    (2048, 4096, 9216, 2, 2, 2, False): (1024, 1024, 4096, True),
}


@functools.lru_cache(maxsize=None)
def _mm_tiles(M, K, N, a_bytes, b_bytes, out_bytes, has_res):
    if (M, K, N, a_bytes, b_bytes, out_bytes, has_res) in MM_TIMED_TILES:
        return MM_TIMED_TILES[(M, K, N, a_bytes, b_bytes, out_bytes, has_res)]
    best = None
    for tk in [K] + [c for c in (1408, 1024, 512, 384, 256, 128) if K % c == 0 and c < K]:
        nk = K // tk
        for tm in [c for c in (2048, 1024, 512, 256, 128) if M % c == 0]:
            for tn in [c for c in (2816, 2048, 1408, 1152, 1024, 512, 384, 256, 128) if N % c == 0]:
                if tm * tk * tn > MM_MAX_TILE_MACS:
                    continue
                fill = (tn / (-(-tn // MXU_DIM) * MXU_DIM)) * (tk / (-(-tk // MXU_DIM) * MXU_DIM))
                fill *= tm / (tm + MXU_DIM // 2)
                vmem = 2 * (tm * tk * a_bytes + tk * tn * b_bytes) + 2 * tm * tn * out_bytes + tm * tn * 4
                vmem += (tm * tk + tk * tn) * 2 if max(a_bytes, b_bytes) > 2 else 0
                vmem += 2 * tm * tn * 4 if has_res else 0
                if vmem > MM_VMEM_BUDGET:
                    continue
                a_all, b_all = M * K * a_bytes, K * N * b_bytes
                if nk == 1:
                    t_i = a_all + (M // tm) * b_all
                    t_j = b_all + (N // tn) * a_all
                    traffic, i_outer = min((t_i, True), (t_j, False))
                else:
                    traffic, i_outer = (N // tn) * a_all + (M // tm) * b_all, True
                traffic += M * N * (out_bytes + (4 if has_res else 0))
                mxu = 2.0 * M * K * N / (MM_MXU_FLOPS_PER_S * fill) * (1.15 if nk > 1 else 1.0)
                cost = max(traffic / MM_HBM_BYTES_PER_S, mxu) + (M // tm) * (N // tn) * nk * MM_STEP_S
                if best is None or cost < best[0]:
                    best = (cost, tm, tn, tk, i_outer)
    assert best is not None, (M, K, N)
    return best[1:]


def _mm(a, b, mode, out_dtype, name, *, scale=1.0, res=None, layer=None, deps=()):
    b2 = b.shape[-2:]
    if mode == "nn":
        (M, K), (Kb, N) = a.shape, b2
    elif mode == "nt":
        (M, K), (N, Kb) = a.shape, b2
    else:
        (K, M), (Kb, N) = a.shape, b2
    assert K == Kb, (a.shape, b.shape, mode)
    tm, tn, tk, i_outer = _mm_tiles(M, K, N, a.dtype.itemsize, b.dtype.itemsize, jnp.dtype(out_dtype).itemsize,
                                    res is not None)
    nk = K // tk
    dims = {"nn": (((1,), (0,)), ((), ())), "nt": (((1,), (1,)), ((), ())), "tn": (((0,), (0,)), ((), ()))}[mode]

    def finish(v, r_ref, o_ref):
        if scale != 1.0:
            v = v * scale
        if r_ref is not None:
            v = r_ref[...] + v
        o_ref[...] = v.astype(o_ref.dtype)

    def body(*refs):
        a_ref, b_ref = refs[:2]
        r_ref = refs[2] if res is not None else None
        prod = lambda: lax.dot_general(a_ref[...].astype(BF16), b_ref[...].astype(BF16), dims,
                                       preferred_element_type=F32)
        if nk == 1:
            finish(prod(), r_ref, refs[-1])
            return
        o_ref, acc = refs[-2:]
        k = pl.program_id(2)

        @pl.when(k == 0)
        def _():
            acc[...] = prod()

        @pl.when(k > 0)
        def _():
            acc[...] += prod()

        @pl.when(k == nk - 1)
        def _():
            finish(acc[...], r_ref, o_ref)

    ij = (lambda p, q: (p, q)) if i_outer else (lambda p, q: (q, p))

    def spec(shape, f, lead=None):
        full = lambda p, q, k: f(*ij(p, q), k)
        if lead is None:
            return pl.BlockSpec(shape, full)
        return pl.BlockSpec((None,) + shape, lambda p, q, k: (lead,) + full(p, q, k))

    a_spec = spec((tk, tm), lambda i, j, k: (k, i)) if mode == "tn" else spec((tm, tk), lambda i, j, k: (i, k))
    lead = layer if b.ndim == 3 else None
    b_spec = spec((tn, tk), lambda i, j, k: (j, k), lead) if mode == "nt" else spec((tk, tn), lambda i, j, k: (k, j), lead)
    in_specs = [a_spec, b_spec]
    args = [a, b]
    if res is not None:
        in_specs.append(spec((tm, tn), lambda i, j, k: (i, j)))
        args.append(res)
    in_specs += [ANY_SPEC] * len(deps)
    args += list(deps)
    outer, inner = (M // tm, N // tn) if i_outer else (N // tn, M // tm)
    return pl.pallas_call(
        body, name=name, grid=(outer, inner, nk),
        in_specs=in_specs, out_specs=spec((tm, tn), lambda i, j, k: (i, j)),
        out_shape=jax.ShapeDtypeStruct((M, N), out_dtype),
        scratch_shapes=[pltpu.VMEM((tm, tn), F32)] if nk > 1 else [],
        compiler_params=_params(("parallel", "parallel", "arbitrary")),
    )(*args)


def _cast_into_gathered(w, layer, axis, me, name, deps=()):
    _, R, C = w.shape
    tr = _pick(R, (512, 256, 128, 64, 32, 16))
    nr = R // tr
    axis, paired = axis

    def body(me_ref, w_ref, *rest):
        o_ref = rest[-1]
        o_ref[...] = w_ref[...].astype(BF16)

    if axis == 1:
        out_idx = lambda i, me_ref: (0, _slot(me_ref[0], paired) * nr + i, 0)
        shape = (1, R * N_DEV, C)
    else:
        out_idx = lambda i, me_ref: (0, i, _slot(me_ref[0], paired))
        shape = (1, R, C * N_DEV)
    return pl.pallas_call(
        body, name=name,
        grid_spec=pltpu.PrefetchScalarGridSpec(
            num_scalar_prefetch=1, grid=(nr,),
            in_specs=[pl.BlockSpec((None, tr, C), lambda i, me_ref: (layer, i, 0))] + [ANY_SPEC] * len(deps),
            out_specs=pl.BlockSpec((None, tr, C), out_idx)),
        out_shape=jax.ShapeDtypeStruct(shape, BF16), compiler_params=_params(("parallel",)),
    )(me, w, *deps)


def _rms_fwd(x, g, name, deps=()):
    T, D = x.shape
    tr = _pick(T, (512, 256, 128))

    def body(x_ref, g_ref, *rest):
        o_ref = rest[-1]
        xv = x_ref[...]
        r = lax.rsqrt(jnp.mean(xv * xv, axis=-1, keepdims=True) + EPS)
        o_ref[...] = ((xv * r) * g_ref[...]).astype(BF16)

    return pl.pallas_call(
        body, name=name, grid=(T // tr,),
        in_specs=[pl.BlockSpec((tr, D), lambda i: (i, 0)), pl.BlockSpec((1, D), lambda i: (0, 0))]
        + [ANY_SPEC] * len(deps),
        out_specs=pl.BlockSpec((tr, D), lambda i: (i, 0)),
        out_shape=jax.ShapeDtypeStruct((T, D), BF16), compiler_params=_params(("parallel",)),
    )(x, g, *deps)


def _rms_bwd(x, g, dh, dres, name, deps=()):
    T, D = x.shape
    tr = _pick(T, (256, 128))

    def body(x_ref, g_ref, dh_ref, dres_ref, *rest):
        dx_ref, dxb_ref, dg_ref = rest[-3:]
        xv = x_ref[...]
        dhv = dh_ref[...]
        r = lax.rsqrt(jnp.mean(xv * xv, axis=-1, keepdims=True) + EPS)
        xhat = xv * r
        dxh = dhv * g_ref[...]
        c = jnp.mean(dxh * xhat, axis=-1, keepdims=True)
        dx = dres_ref[...] + r * (dxh - xhat * c)
        dx_ref[...] = dx
        dxb_ref[...] = dx.astype(BF16)

        @pl.when(pl.program_id(0) == 0)
        def _():
            dg_ref[...] = jnp.zeros_like(dg_ref)

        dg_ref[...] += jnp.sum(dhv * xhat, axis=0, keepdims=True)

    row = pl.BlockSpec((tr, D), lambda i: (i, 0))
    vec = pl.BlockSpec((1, D), lambda i: (0, 0))
    return pl.pallas_call(
        body, name=name, grid=(T // tr,),
        in_specs=[row, vec, row, row] + [ANY_SPEC] * len(deps), out_specs=[row, row, vec],
        out_shape=[jax.ShapeDtypeStruct((T, D), F32), jax.ShapeDtypeStruct((T, D), BF16),
                   jax.ShapeDtypeStruct((1, D), F32)],
        compiler_params=_params(("arbitrary",)),
    )(x, g, dh, dres, *deps)


N_PANEL = N_DEV // 2


def _ffn_in(h, w_in, name):
    T, D = h.shape
    F2 = w_in.shape[2]
    pw = F2 // N_PANEL
    half = pw // 2
    tm = _pick(T, (512, 256, 128))

    def body(h_ref, w_ref, u_ref, a_ref):
        r = jnp.dot(h_ref[...], w_ref[...], preferred_element_type=F32)
        u_ref[...] = r.astype(BF16)
        g, up = r[:, :half], r[:, half:]
        a_ref[...] = (g * jax.nn.sigmoid(g) * up).astype(BF16)

    return pl.pallas_call(
        body, name=name, grid=(N_PANEL, T // tm),
        in_specs=[pl.BlockSpec((tm, D), lambda p, i: (i, 0)), pl.BlockSpec((None, D, pw), lambda p, i: (0, 0, p))],
        out_specs=[pl.BlockSpec((tm, pw), lambda p, i: (i, p)), pl.BlockSpec((tm, half), lambda p, i: (i, p))],
        out_shape=[jax.ShapeDtypeStruct((T, F2), BF16), jax.ShapeDtypeStruct((T, F2 // 2), BF16)],
        compiler_params=_params(("parallel", "parallel")),
    )(h, w_in)


def _ffn_da(dxo, w_out, u, name, deps=()):
    T, D = dxo.shape
    F2 = u.shape[1]
    pw = F2 // N_PANEL
    half = pw // 2
    tm = _pick(T, (512, 256, 128))

    def body(d_ref, w_ref, u_ref, *rest):
        du_ref = rest[-1]
        da = 0.5 * lax.dot_general(d_ref[...], w_ref[...], NT_DIMS, preferred_element_type=F32)
        g = u_ref[:, :half].astype(F32)
        up = u_ref[:, half:].astype(F32)
        sg = jax.nn.sigmoid(g)
        silu = g * sg
        du_ref[:, :half] = (da * up * (sg + silu * (1.0 - sg))).astype(BF16)
        du_ref[:, half:] = (da * silu).astype(BF16)

    return pl.pallas_call(
        body, name=name, grid=(N_PANEL, T // tm),
        in_specs=[pl.BlockSpec((tm, D), lambda p, i: (i, 0)), pl.BlockSpec((None, half, D), lambda p, i: (0, p, 0)),
                  pl.BlockSpec((tm, pw), lambda p, i: (i, p))] + [ANY_SPEC] * len(deps),
        out_specs=pl.BlockSpec((tm, pw), lambda p, i: (i, p)),
        out_shape=jax.ShapeDtypeStruct((T, F2), BF16), compiler_params=_params(("parallel", "parallel")),
    )(dxo, w_out, u, *deps)


def _loss_head(y, t, name):
    T, D = y.shape
    tr = _pick(T, (512, 256, 128))

    def body(y_ref, t_ref, dy_ref, dyb_ref, l_ref):
        e = y_ref[...] - t_ref[...]
        dy = e * (1.0 / D)
        dy_ref[...] = dy
        dyb_ref[...] = dy.astype(BF16)

        @pl.when(pl.program_id(0) == 0)
        def _():
            l_ref[...] = jnp.zeros_like(l_ref)

        l_ref[...] += 0.5 * jnp.sum(jnp.mean(e * e, axis=-1, keepdims=True), axis=0, keepdims=True)

    row = pl.BlockSpec((tr, D), lambda i: (i, 0))
    return pl.pallas_call(
        body, name=name, grid=(T // tr,),
        in_specs=[row, row], out_specs=[row, row, pl.BlockSpec((1, 1), lambda i: (0, 0))],
        out_shape=[jax.ShapeDtypeStruct((T, D), F32), jax.ShapeDtypeStruct((T, D), BF16),
                   jax.ShapeDtypeStruct((1, 1), F32)],
        compiler_params=_params(("arbitrary",)),
    )(y, t)


def _rope_tables(S):
    half = ROPE_DIM // 2
    inv = 1.0 / (ROPE_THETA ** (jnp.arange(0, ROPE_DIM, 2, dtype=F32) / ROPE_DIM))
    ang = jnp.arange(S, dtype=F32)[:, None] * inv[None, :]
    cos, sin = jnp.cos(ang), jnp.sin(ang)
    z = jnp.zeros((S, half), F32)
    z2 = jnp.zeros((S, LANES - ROPE_DIM), F32)
    c = jnp.concatenate([cos, cos, z2], axis=1)
    s1 = jnp.concatenate([-sin, z, z2], axis=1)
    s2 = jnp.concatenate([z, sin, z2], axis=1)
    return c, s1, s2


def _rope(r, c, s1, s2):
    return r * c + pltpu.roll(r, LANES - ROPE_DIM // 2, 1) * s1 + pltpu.roll(r, ROPE_DIM // 2, 1) * s2


def _rope_t(d, c, s1, s2):
    return d * c + pltpu.roll(d * s1, ROPE_DIM // 2, 1) + pltpu.roll(d * s2, LANES - ROPE_DIM // 2, 1)


def _lat_norm_fwd(lat, g_cq, g_ckv, name):
    T = lat.shape[0]
    tr = _pick(T, (512, 256, 128))

    def body(lat_ref, gq_ref, gk_ref, cq_ref, ckv_ref):
        for off, g_ref, o_ref in ((0, gq_ref, cq_ref), (Q_LORA, gk_ref, ckv_ref)):
            xv = lat_ref[:, off:off + Q_LORA]
            r = lax.rsqrt(jnp.mean(xv * xv, axis=-1, keepdims=True) + EPS)
            o_ref[...] = ((xv * r) * g_ref[...]).astype(BF16)

    vec = pl.BlockSpec((1, Q_LORA), lambda i: (0, 0))
    out = pl.BlockSpec((tr, Q_LORA), lambda i: (i, 0))
    return pl.pallas_call(
        body, name=name, grid=(T // tr,),
        in_specs=[pl.BlockSpec((tr, LAT_PAD), lambda i: (i, 0)), vec, vec], out_specs=[out, out],
        out_shape=[jax.ShapeDtypeStruct((T, Q_LORA), BF16)] * 2, compiler_params=_params(("parallel",)),
    )(lat, g_cq, g_ckv)


def _lat_norm_bwd(lat, g_cq, g_ckv, dcq, dckv, dkpe, name):
    T = lat.shape[0]
    tr = _pick(T, (256, 128))

    def body(lat_ref, gq_ref, gk_ref, dcq_ref, dckv_ref, dkpe_ref, dlat_ref, dgq_ref, dgk_ref):
        @pl.when(pl.program_id(0) == 0)
        def _():
            dgq_ref[...] = jnp.zeros_like(dgq_ref)
            dgk_ref[...] = jnp.zeros_like(dgk_ref)

        for off, g_ref, d_ref, dg_ref in ((0, gq_ref, dcq_ref, dgq_ref), (Q_LORA, gk_ref, dckv_ref, dgk_ref)):
            xv = lat_ref[:, off:off + Q_LORA]
            dv = d_ref[...]
            r = lax.rsqrt(jnp.mean(xv * xv, axis=-1, keepdims=True) + EPS)
            xhat = xv * r
            dxh = dv * g_ref[...]
            c = jnp.mean(dxh * xhat, axis=-1, keepdims=True)
            dlat_ref[:, off:off + Q_LORA] = (r * (dxh - xhat * c)).astype(BF16)
            dg_ref[...] += jnp.sum(dv * xhat, axis=0, keepdims=True)
        dlat_ref[:, Q_LORA + KV_LORA:] = dkpe_ref[...].astype(BF16)

    vec = pl.BlockSpec((1, Q_LORA), lambda i: (0, 0))
    half = pl.BlockSpec((tr, Q_LORA), lambda i: (i, 0))
    full = pl.BlockSpec((tr, LAT_PAD), lambda i: (i, 0))
    return pl.pallas_call(
        body, name=name, grid=(T // tr,),
        in_specs=[full, vec, vec, half, half, pl.BlockSpec((tr, LANES), lambda i: (i, 0))],
        out_specs=[full, vec, vec],
        out_shape=[jax.ShapeDtypeStruct((T, LAT_PAD), BF16), jax.ShapeDtypeStruct((1, Q_LORA), F32),
                   jax.ShapeDtypeStruct((1, Q_LORA), F32)],
        compiler_params=_params(("arbitrary",)),
    )(lat, g_cq, g_ckv, dcq, dckv, dkpe)


def _mla_prep_fwd(q_raw, kv, lat, g_qn, g_kn, tabs, name):
    T = q_raw.shape[0]
    H = MLA_HEADS
    tr = _pick(T, (256, 128))
    scale = LOG2E / math.sqrt(QK_DIM)

    def body(q_ref, kv_ref, kpe_ref, gq_ref, gk_ref, c_ref, s1_ref, s2_ref, qf_ref, kf_ref, v_ref):
        c, s1, s2 = c_ref[...], s1_ref[...], s2_ref[...]
        gq, gk = gq_ref[...], gk_ref[...]
        kpe = kpe_ref[...]
        kpe_ss = jnp.sum(kpe * kpe, axis=-1, keepdims=True)
        for h in range(H):
            lo = h * HEAD_PAD
            qa = q_ref[:, lo:lo + LANES]
            qb = q_ref[:, lo + LANES:lo + HEAD_PAD]
            ss = jnp.sum(qa * qa + qb * qb, axis=-1, keepdims=True)
            r = lax.rsqrt(ss * (1.0 / QK_DIM) + EPS)
            qf_ref[:, lo:lo + LANES] = (qa * r * gq[:, :LANES] * scale).astype(BF16)
            qf_ref[:, lo + LANES:lo + HEAD_PAD] = (_rope(qb * r * gq[:, LANES:], c, s1, s2) * scale).astype(BF16)
            ka = kv_ref[:, lo:lo + LANES]
            ss = jnp.sum(ka * ka, axis=-1, keepdims=True) + kpe_ss
            r = lax.rsqrt(ss * (1.0 / QK_DIM) + EPS)
            kf_ref[:, lo:lo + LANES] = (ka * r * gk[:, :LANES]).astype(BF16)
            kf_ref[:, lo + LANES:lo + HEAD_PAD] = _rope(kpe * r * gk[:, LANES:], c, s1, s2).astype(BF16)
            v_ref[:, lo:lo + V_DIM] = kv_ref[:, lo + LANES:lo + HEAD_PAD].astype(BF16)
            v_ref[:, lo + V_DIM:lo + HEAD_PAD] = jnp.ones((tr, HEAD_PAD - V_DIM), BF16)

    wide = pl.BlockSpec((tr, H * HEAD_PAD), lambda i: (i, 0))
    lane = pl.BlockSpec((tr, LANES), lambda i: (i, 0))
    gvec = pl.BlockSpec((1, HEAD_PAD), lambda i: (0, 0))
    return pl.pallas_call(
        body, name=name, grid=(T // tr,),
        in_specs=[wide, wide, pl.BlockSpec((tr, LANES), lambda i: (i, (Q_LORA + KV_LORA) // LANES)), gvec, gvec,
                  lane, lane, lane],
        out_specs=[wide, wide, wide],
        out_shape=[jax.ShapeDtypeStruct((T, H * HEAD_PAD), BF16)] * 3,
        compiler_params=_params(("parallel",)),
    )(q_raw, kv, lat, g_qn, g_kn, *tabs)


def _mla_prep_bwd(q_raw, kv, lat, g_qn, g_kn, tabs, dqf, dkf, dv, name):
    T = q_raw.shape[0]
    H = MLA_HEADS
    tr = _pick(T, (128,))

    def body(q_ref, kv_ref, kpe_ref, gq_ref, gk_ref, c_ref, s1_ref, s2_ref, dqf_ref, dkf_ref, dv_ref,
             dq_ref, dkv_ref, dkpe_ref, dgq_ref, dgk_ref):
        @pl.when(pl.program_id(0) == 0)
        def _():
            dgq_ref[...] = jnp.zeros_like(dgq_ref)
            dgk_ref[...] = jnp.zeros_like(dgk_ref)

        c, s1, s2 = c_ref[...], s1_ref[...], s2_ref[...]
        gq, gk = gq_ref[...], gk_ref[...]
        kpe = kpe_ref[...]
        kpe_ss = jnp.sum(kpe * kpe, axis=-1, keepdims=True)
        dkpe = jnp.zeros_like(kpe)
        dgq_a = jnp.zeros((1, LANES), F32)
        dgq_b = jnp.zeros((1, LANES), F32)
        dgk_a = jnp.zeros((1, LANES), F32)
        dgk_b = jnp.zeros((1, LANES), F32)
        for h in range(H):
            lo = h * HEAD_PAD
            xa = q_ref[:, lo:lo + LANES]
            xb = q_ref[:, lo + LANES:lo + HEAD_PAD]
            ss = jnp.sum(xa * xa + xb * xb, axis=-1, keepdims=True)
            r = lax.rsqrt(ss * (1.0 / QK_DIM) + EPS)
            xa, xb = xa * r, xb * r
            da = dqf_ref[:, lo:lo + LANES].astype(F32)
            db = _rope_t(dqf_ref[:, lo + LANES:lo + HEAD_PAD].astype(F32), c, s1, s2)
            dgq_a += jnp.sum(da * xa, axis=0, keepdims=True)
            dgq_b += jnp.sum(db * xb, axis=0, keepdims=True)
            da, db = da * gq[:, :LANES], db * gq[:, LANES:]
            cc = jnp.sum(da * xa + db * xb, axis=-1, keepdims=True) * (1.0 / QK_DIM)
            dq_ref[:, lo:lo + LANES] = (r * (da - xa * cc)).astype(BF16)
            dq_ref[:, lo + LANES:lo + HEAD_PAD] = (r * (db - xb * cc)).astype(BF16)
            xa = kv_ref[:, lo:lo + LANES]
            ss = jnp.sum(xa * xa, axis=-1, keepdims=True) + kpe_ss
            r = lax.rsqrt(ss * (1.0 / QK_DIM) + EPS)
            xa, xb = xa * r, kpe * r
            da = dkf_ref[:, lo:lo + LANES].astype(F32)
            db = _rope_t(dkf_ref[:, lo + LANES:lo + HEAD_PAD].astype(F32), c, s1, s2)
            dgk_a += jnp.sum(da * xa, axis=0, keepdims=True)
            dgk_b += jnp.sum(db * xb, axis=0, keepdims=True)
            da, db = da * gk[:, :LANES], db * gk[:, LANES:]
            cc = jnp.sum(da * xa + db * xb, axis=-1, keepdims=True) * (1.0 / QK_DIM)
            dkv_ref[:, lo:lo + LANES] = (r * (da - xa * cc)).astype(BF16)
            dkpe = dkpe + r * (db - xb * cc)
            dkv_ref[:, lo + LANES:lo + HEAD_PAD] = dv_ref[:, h * V_DIM:(h + 1) * V_DIM].astype(BF16)
        dkpe_ref[...] = dkpe
        dgq_ref[:, :LANES] += dgq_a
        dgq_ref[:, LANES:] += dgq_b
        dgk_ref[:, :LANES] += dgk_a
        dgk_ref[:, LANES:] += dgk_b

    wide = pl.BlockSpec((tr, H * HEAD_PAD), lambda i: (i, 0))
    lane = pl.BlockSpec((tr, LANES), lambda i: (i, 0))
    gvec = pl.BlockSpec((1, HEAD_PAD), lambda i: (0, 0))
    vspec = pl.BlockSpec((tr, H * V_DIM), lambda i: (i, 0))
    return pl.pallas_call(
        body, name=name, grid=(T // tr,),
        in_specs=[wide, wide, pl.BlockSpec((tr, LANES), lambda i: (i, (Q_LORA + KV_LORA) // LANES)), gvec, gvec,
                  lane, lane, lane, wide, wide, vspec],
        out_specs=[wide, wide, lane, gvec, gvec],
        out_shape=[jax.ShapeDtypeStruct((T, H * HEAD_PAD), BF16), jax.ShapeDtypeStruct((T, H * HEAD_PAD), BF16),
                   jax.ShapeDtypeStruct((T, LANES), F32), jax.ShapeDtypeStruct((1, HEAD_PAD), F32),
                   jax.ShapeDtypeStruct((1, HEAD_PAD), F32)],
        compiler_params=_params(("arbitrary",)),
    )(q_raw, kv, lat, g_qn, g_kn, *tabs, dqf, dkf, dv)


def _causal_mask(tq, tk):
    return lax.broadcasted_iota(jnp.int32, (tq, tk), 1) <= lax.broadcasted_iota(jnp.int32, (tq, tk), 0)


NT_DIMS = (((1,), (1,)), ((), ()))
TN_DIMS = (((0,), (0,)), ((), ()))


def _flash_fwd(qf, kf, v, name):
    T = qf.shape[0]
    H, G = MLA_HEADS, FLASH_HEADS_FWD
    t = _pick(T, (512, 256, 128))
    n = T // t
    pairs = [(i, j) for i in range(n) for j in range(i + 1)]
    qi = jnp.asarray([p[0] for p in pairs], jnp.int32)
    kj = jnp.asarray([p[1] for p in pairs], jnp.int32)

    def body(qi_ref, kj_ref, q_ref, k_ref, v_ref, o_ref, lse_ref, *scratch):
        m_sc, acc_sc = scratch[:G], scratch[G:]
        sid = pl.program_id(1)
        i, j = qi_ref[sid], kj_ref[sid]

        @pl.when(j == 0)
        def _():
            for g in range(G):
                m_sc[g][...] = jnp.full_like(m_sc[g], NEG_BIG)
                acc_sc[g][...] = jnp.zeros_like(acc_sc[g])

        def step(masked):
            for g in range(G):
                qk = slice(g * HEAD_PAD, (g + 1) * HEAD_PAD)
                s = lax.dot_general(q_ref[:, qk], k_ref[:, qk], NT_DIMS, preferred_element_type=F32)
                if masked:
                    s = jnp.where(_causal_mask(t, t), s, NEG_BIG)
                m_prev = m_sc[g][:, :1]
                m_new = jnp.maximum(m_prev, jnp.max(s, axis=-1, keepdims=True))
                a = jnp.exp2(m_prev - m_new)
                p = jnp.exp2((s - m_new).astype(BF16))
                acc_sc[g][...] = a * acc_sc[g][...] + jnp.dot(p, v_ref[:, qk], preferred_element_type=F32)
                m_sc[g][...] = jnp.broadcast_to(m_new, (t, LANES))

        @pl.when(j < i)
        def _():
            step(False)

        @pl.when(j == i)
        def _():
            step(True)
            for g in range(G):
                vo = slice(g * V_DIM, (g + 1) * V_DIM)
                l = acc_sc[g][:, V_DIM:]
                o_ref[:, vo] = (acc_sc[g][:, :V_DIM] / l).astype(BF16)
                lse_ref[:, vo] = m_sc[g][...] + jnp.log2(l)

    row = pl.BlockSpec((t, G * V_DIM), lambda h, s, qi, kj: (qi[s], h))
    return pl.pallas_call(
        body, name=name,
        grid_spec=pltpu.PrefetchScalarGridSpec(
            num_scalar_prefetch=2, grid=(H // G, len(pairs)),
            in_specs=[pl.BlockSpec((t, G * HEAD_PAD), lambda h, s, qi, kj: (qi[s], h)),
                      pl.BlockSpec((t, G * HEAD_PAD), lambda h, s, qi, kj: (kj[s], h)),
                      pl.BlockSpec((t, G * HEAD_PAD), lambda h, s, qi, kj: (kj[s], h))],
            out_specs=[row, row],
            scratch_shapes=[pltpu.VMEM((t, LANES), F32)] * G + [pltpu.VMEM((t, HEAD_PAD), F32)] * G),
        out_shape=[jax.ShapeDtypeStruct((T, H * V_DIM), BF16), jax.ShapeDtypeStruct((T, H * V_DIM), F32)],
        compiler_params=_params(("parallel", "arbitrary")),
    )(qi, kj, qf, kf, v)


def _attn_delta(do, o, name):
    T, W = do.shape
    nh = W // V_DIM
    tr = _pick(T, (512, 256, 128))

    def body(do_ref, o_ref, d_ref):
        for h in range(nh):
            sl = slice(h * V_DIM, (h + 1) * V_DIM)
            d = jnp.sum(do_ref[:, sl].astype(F32) * o_ref[:, sl].astype(F32), axis=-1, keepdims=True)
            d_ref[:, sl] = jnp.broadcast_to(d, (tr, V_DIM))

    row = pl.BlockSpec((tr, W), lambda i: (i, 0))
    return pl.pallas_call(
        body, name=name, grid=(T // tr,), in_specs=[row, row], out_specs=row,
        out_shape=jax.ShapeDtypeStruct((T, W), F32), compiler_params=_params(("parallel",)),
    )(do, o)


def _flash_bwd(qf, kf, v, do, lse, delta, name):
    T = qf.shape[0]
    H, G = MLA_HEADS, FLASH_HEADS
    t = _pick(T, (512, 256, 128))
    n = T // t
    scale = 1.0 / math.sqrt(QK_DIM)
    pairs = [(i, j) for j in range(n) for i in range(j, n)]
    qi = jnp.asarray([p[0] for p in pairs], jnp.int32)
    kj = jnp.asarray([p[1] for p in pairs], jnp.int32)

    def body(qi_ref, kj_ref, q_ref, k_ref, v_ref, do_ref, lse_ref, dl_ref, dq_ref, dk_ref, dv_ref,
             dq_acc, dk_acc, dv_acc):
        sid = pl.program_id(1)
        i, j = qi_ref[sid], kj_ref[sid]

        @pl.when(sid == 0)
        def _():
            dq_acc[...] = jnp.zeros_like(dq_acc)

        def step(masked):
            rows = pl.ds(pl.multiple_of(i * t, t), t)
            for g in range(G):
                qk = slice(g * HEAD_PAD, (g + 1) * HEAD_PAD)
                vo = slice(g * V_DIM, (g + 1) * V_DIM)
                q, k, do_ = q_ref[:, qk], k_ref[:, qk], do_ref[:, vo]
                v_ = v_ref[:, g * HEAD_PAD:g * HEAD_PAD + V_DIM]
                s = lax.dot_general(q, k, NT_DIMS, preferred_element_type=F32)
                if masked:
                    s = jnp.where(_causal_mask(t, t), s, NEG_BIG)
                p = jnp.exp2(s - lse_ref[:, g * V_DIM:g * V_DIM + 1])
                dp = lax.dot_general(do_, v_, NT_DIMS, preferred_element_type=F32)
                ds = (p * (dp - dl_ref[:, g * V_DIM:g * V_DIM + 1])).astype(BF16)
                dv = lax.dot_general(p.astype(BF16), do_, TN_DIMS, preferred_element_type=F32)
                dk = lax.dot_general(ds, q, TN_DIMS, preferred_element_type=F32)
                if masked:
                    dv_acc[:, vo] = dv
                    dk_acc[:, qk] = dk
                else:
                    dv_acc[:, vo] += dv
                    dk_acc[:, qk] += dk
                dq_acc[rows, qk] += jnp.dot(ds, k, preferred_element_type=F32) * scale

        @pl.when(i == j)
        def _():
            step(True)

        @pl.when(i > j)
        def _():
            step(False)

        @pl.when(i == n - 1)
        def _():
            dk_ref[...] = (dk_acc[...] * LN2).astype(BF16)
            dv_ref[...] = dv_acc[...].astype(BF16)

        @pl.when(sid == len(pairs) - 1)
        def _():
            dq_ref[...] = dq_acc[...].astype(BF16)

    qs = pl.BlockSpec((t, G * HEAD_PAD), lambda h, s, qi, kj: (qi[s], h))
    rs = pl.BlockSpec((t, G * V_DIM), lambda h, s, qi, kj: (qi[s], h))
    ks = pl.BlockSpec((t, G * HEAD_PAD), lambda h, s, qi, kj: (kj[s], h))
    vs = pl.BlockSpec((t, G * V_DIM), lambda h, s, qi, kj: (kj[s], h))
    return pl.pallas_call(
        body, name=name,
        grid_spec=pltpu.PrefetchScalarGridSpec(
            num_scalar_prefetch=2, grid=(H // G, len(pairs)), in_specs=[qs, ks, ks, rs, rs, rs],
            out_specs=[pl.BlockSpec((T, G * HEAD_PAD), lambda h, s, qi, kj: (0, h)), ks, vs],
            scratch_shapes=[pltpu.VMEM((T, G * HEAD_PAD), F32), pltpu.VMEM((t, G * HEAD_PAD), F32),
                            pltpu.VMEM((t, G * V_DIM), F32)]),
        out_shape=[jax.ShapeDtypeStruct((T, H * HEAD_PAD), BF16), jax.ShapeDtypeStruct((T, H * HEAD_PAD), BF16),
                   jax.ShapeDtypeStruct((T, H * V_DIM), BF16)],
        compiler_params=_params(("parallel", "arbitrary")),
    )(qi, kj, qf, kf, v, do, lse, delta)


def _alibi_slopes():
    tot = DIL_GROUPS * DIL_HEADS
    return [float(np.float32(2.0) ** (np.float32(-8.0) * np.float32(k) / np.float32(tot))) for k in range(1, tot + 1)]


def _dil_masks():
    iq = lax.broadcasted_iota(jnp.int32, (DIL_BLK, DIL_BLK), 0)
    ik = lax.broadcasted_iota(jnp.int32, (DIL_BLK, DIL_BLK), 1)
    return (ik >= iq), (iq + DIL_BLK - ik).astype(F32), (ik <= iq), (iq - ik).astype(F32)


def _dil_norm(x, g):
    r = lax.rsqrt(jnp.mean(x * x, axis=-1, keepdims=True) + EPS)
    return x * r, r


DIL_SUPER = 8
BNT_DIMS = (((2,), (2,)), ((0,), (0,)))
BNN_DIMS = (((2,), (1,)), ((0,), (0,)))
BTN_DIMS = (((1,), (1,)), ((0,), (0,)))


def _dil_chunk(it, nb, d):
    assert nb & (nb - 1) == 0, nb
    r, n = it >> (nb.bit_length() - 1), it & (nb - 1)
    if d > 1:
        tok = pl.ds(n * (d * DIL_BLK) + r, DIL_BLK, stride=d)
    else:
        tok = pl.ds(pl.multiple_of(it * DIL_BLK, DIL_BLK), DIL_BLK)
    return tok, pl.ds(pl.multiple_of((it + 1) * DIL_BLK, DIL_BLK), DIL_BLK)


def _dil_token_rows(bidx, nb, d):
    r, n = divmod(bidx, nb)
    return pl.ds(n * DIL_BLK * d + r, DIL_BLK, stride=d) if d > 1 else pl.ds(bidx * DIL_BLK, DIL_BLK)


def _dil_super_rows(ss):
    base = (1 + ss * DIL_SUPER) * DIL_BLK
    return pl.ds(base, DIL_SUPER * DIL_BLK), pl.ds(base - DIL_BLK, DIL_SUPER * DIL_BLK)


def _dil_b3(x):
    return x.reshape(DIL_SUPER, DIL_BLK, x.shape[-1])


def _dil_scores(q3, kc3, kp3, slope, d, ss, nb):
    ok_p, dist_p, ok_c, dist_c = _dil_masks()
    scale = 1.0 / math.sqrt(DIL_HEAD_DIM)
    bias_p = jnp.where(ok_p, -slope * d * dist_p, NEG_BIG)
    bias_c = jnp.where(ok_c, -slope * d * dist_c, NEG_BIG)
    s_c = lax.dot_general(q3, kc3, BNT_DIMS, preferred_element_type=F32) * scale + bias_c[None]
    s_p = lax.dot_general(q3, kp3, BNT_DIMS, preferred_element_type=F32) * scale + bias_p[None]
    bidx = ss * DIL_SUPER + lax.broadcasted_iota(jnp.int32, s_p.shape, 0)
    s_p = jnp.where((bidx & (nb - 1)) == 0, NEG_BIG, s_p)
    return s_c, s_p


def _dil_fwd(qkv, g_qn, g_kn, slopes, name):
    T = qkv.shape[0]
    GH = DIL_GROUPS * DIL_HEADS
    scale = 1.0 / math.sqrt(DIL_HEAD_DIM)

    def body(sl_ref, q_ref, k_ref, v_ref, gq_ref, gk_ref, o_ref, lse_ref, qn_pm, kn_pm, v_pm):
        gh = pl.program_id(0)
        slope = sl_ref[gh]
        gq, gk = gq_ref[...], gk_ref[...]
        pad = pl.ds(0, DIL_BLK)
        kn_pm[pad, :] = jnp.zeros((DIL_BLK, DIL_HEAD_DIM), BF16)
        v_pm[pad, :] = jnp.zeros((DIL_BLK, DIL_HEAD_DIM), BF16)
        for g, (_, d) in enumerate(DIL_PAIRS):
            @pl.when((gh >= g * DIL_HEADS) & (gh < (g + 1) * DIL_HEADS))
            def _(d=d):
                nb = T // (d * DIL_BLK)

                def fill(it, _):
                    tok, dst = _dil_chunk(it, nb, d)
                    qn_pm[dst, :] = (_dil_norm(q_ref[tok, :], gq)[0] * gq).astype(BF16)
                    kn_pm[dst, :] = (_dil_norm(k_ref[tok, :], gk)[0] * gk).astype(BF16)
                    v_pm[dst, :] = v_ref[tok, :].astype(BF16)
                    return 0
                lax.fori_loop(0, T // DIL_BLK, fill, 0, unroll=4)

                for ss in range(T // DIL_BLK // DIL_SUPER):
                    cur, prv = _dil_super_rows(ss)
                    q3, kc3, kp3 = _dil_b3(qn_pm[cur, :]), _dil_b3(kn_pm[cur, :]), _dil_b3(kn_pm[prv, :])
                    s_c, s_p = _dil_scores(q3, kc3, kp3, slope, d, ss, nb)
                    m = jnp.max(jnp.maximum(s_c, s_p), axis=-1, keepdims=True)
                    p_c = jnp.exp(s_c - m)
                    p_p = jnp.exp(s_p - m)
                    l = jnp.sum(p_c, axis=-1, keepdims=True) + jnp.sum(p_p, axis=-1, keepdims=True)
                    acc = lax.dot_general(p_c.astype(BF16), _dil_b3(v_pm[cur, :]), BNN_DIMS, preferred_element_type=F32)
                    acc += lax.dot_general(p_p.astype(BF16), _dil_b3(v_pm[prv, :]), BNN_DIMS, preferred_element_type=F32)
                    o3 = acc / l
                    lse3 = jnp.broadcast_to(m + jnp.log(l), o3.shape)
                    for b in range(DIL_SUPER):
                        tok = _dil_token_rows(ss * DIL_SUPER + b, nb, d)
                        o_ref[tok, :] = o3[b]
                        lse_ref[tok, :] = lse3[b]

    col = lambda off: pl.BlockSpec((T, DIL_HEAD_DIM), lambda gh, sl: (0, gh + off))
    gvec = pl.BlockSpec((1, DIL_HEAD_DIM), lambda gh, sl: (0, 0))
    return pl.pallas_call(
        body, name=name,
        grid_spec=pltpu.PrefetchScalarGridSpec(
            num_scalar_prefetch=1, grid=(GH,),
            in_specs=[col(0), col(GH), col(2 * GH), gvec, gvec], out_specs=[col(0), col(0)],
            scratch_shapes=[pltpu.VMEM((DIL_BLK + T, DIL_HEAD_DIM), BF16)] * 3),
        out_shape=[jax.ShapeDtypeStruct((T, GH * DIL_HEAD_DIM), F32)] * 2,
        compiler_params=_params(("parallel",)),
    )(slopes, qkv, qkv, qkv, g_qn, g_kn)


def _dil_merge(o_g, lse_g, name):
    T = o_g.shape[0]
    W = DIL_HEADS * DIL_HEAD_DIM
    tr = _pick(T, (256, 128))

    def body(o0, o1, o2, l0, l1, l2, o_ref, lse_ref):
        a, b, c = l0[...], l1[...], l2[...]
        m = jnp.maximum(jnp.maximum(a, b), c)
        ea, eb, ec = jnp.exp(a - m), jnp.exp(b - m), jnp.exp(c - m)
        tot = ea + eb + ec
        o_ref[...] = ((o0[...] * ea + o1[...] * eb + o2[...] * ec) / tot).astype(BF16)
        lse_ref[...] = m + jnp.log(tot)

    grp = lambda g: pl.BlockSpec((tr, W), lambda i: (i, g))
    out = pl.BlockSpec((tr, W), lambda i: (i, 0))
    return pl.pallas_call(
        body, name=name, grid=(T // tr,),
        in_specs=[grp(0), grp(1), grp(2), grp(0), grp(1), grp(2)], out_specs=[out, out],
        out_shape=[jax.ShapeDtypeStruct((T, W), BF16), jax.ShapeDtypeStruct((T, W), F32)],
        compiler_params=_params(("parallel",)),
    )(o_g, o_g, o_g, lse_g, lse_g, lse_g)


def _dil_bwd(qkv, g_qn, g_kn, slopes, do, delta, lse, name):
    T = qkv.shape[0]
    GH = DIL_GROUPS * DIL_HEADS
    scale = 1.0 / math.sqrt(DIL_HEAD_DIM)
    nchunk = T // DIL_BLK

    def body(sl_ref, q_ref, k_ref, v_ref, gq_ref, gk_ref, do_ref, dl_ref, lse_ref,
             dq_ref, dk_ref, dv_ref, dgq_ref, dgk_ref,
             qn_pm, kn_pm, v_pm, do_pm, lse_pm, dl_pm, dq_pm, dk_pm, dv_pm, tok_sc):
        gh = pl.program_id(0)
        slope = sl_ref[gh]
        gq, gk = gq_ref[...], gk_ref[...]

        @pl.when(gh == 0)
        def _():
            dgq_ref[...] = jnp.zeros_like(dgq_ref)
            dgk_ref[...] = jnp.zeros_like(dgk_ref)

        pad = pl.ds(0, DIL_BLK)
        kn_pm[pad, :] = jnp.zeros((DIL_BLK, DIL_HEAD_DIM), BF16)
        v_pm[pad, :] = jnp.zeros((DIL_BLK, DIL_HEAD_DIM), BF16)
        dk_pm[...] = jnp.zeros_like(dk_pm)
        dv_pm[...] = jnp.zeros_like(dv_pm)
        for g, (_, d) in enumerate(DIL_PAIRS):
            @pl.when((gh >= g * DIL_HEADS) & (gh < (g + 1) * DIL_HEADS))
            def _(d=d):
                nb = T // (d * DIL_BLK)

                def fill(it, _):
                    tok, dst = _dil_chunk(it, nb, d)
                    qn_pm[dst, :] = (_dil_norm(q_ref[tok, :], gq)[0] * gq).astype(BF16)
                    kn_pm[dst, :] = (_dil_norm(k_ref[tok, :], gk)[0] * gk).astype(BF16)
                    v_pm[dst, :] = v_ref[tok, :].astype(BF16)
                    do_pm[dst, :] = do_ref[tok, :].astype(BF16)
                    lse_pm[dst, :] = lse_ref[tok, :]
                    dl_pm[dst, :] = dl_ref[tok, :]
                    return 0
                lax.fori_loop(0, nchunk, fill, 0, unroll=4)

                for ss in range(nchunk // DIL_SUPER):
                    cur, prv = _dil_super_rows(ss)
                    q3, kc3, kp3 = _dil_b3(qn_pm[cur, :]), _dil_b3(kn_pm[cur, :]), _dil_b3(kn_pm[prv, :])
                    vc3, vp3, do3 = _dil_b3(v_pm[cur, :]), _dil_b3(v_pm[prv, :]), _dil_b3(do_pm[cur, :])
                    ls = _dil_b3(lse_pm[cur, :])[:, :, :1]
                    delta = _dil_b3(dl_pm[cur, :])[:, :, :1]
                    s_c, s_p = _dil_scores(q3, kc3, kp3, slope, d, ss, nb)
                    p_c = jnp.exp(s_c - ls)
                    p_p = jnp.exp(s_p - ls)
                    dp_c = lax.dot_general(do3, vc3, BNT_DIMS, preferred_element_type=F32)
                    dp_p = lax.dot_general(do3, vp3, BNT_DIMS, preferred_element_type=F32)
                    ds_c = (p_c * (dp_c - delta)).astype(BF16)
                    ds_p = (p_p * (dp_p - delta)).astype(BF16)
                    dq3 = (lax.dot_general(ds_c, kc3, BNN_DIMS, preferred_element_type=F32)
                           + lax.dot_general(ds_p, kp3, BNN_DIMS, preferred_element_type=F32)) * scale
                    flat = lambda x: x.reshape(DIL_SUPER * DIL_BLK, DIL_HEAD_DIM)
                    dq_pm[cur, :] = flat(dq3)
                    dk_pm[cur, :] += flat(lax.dot_general(ds_c, q3, BTN_DIMS, preferred_element_type=F32)) * scale
                    dv_pm[cur, :] += flat(lax.dot_general(p_c.astype(BF16), do3, BTN_DIMS, preferred_element_type=F32))
                    dk_pm[prv, :] += flat(lax.dot_general(ds_p, q3, BTN_DIMS, preferred_element_type=F32)) * scale
                    dv_pm[prv, :] += flat(lax.dot_general(p_p.astype(BF16), do3, BTN_DIMS, preferred_element_type=F32))

                def to_tokens(src_pm):
                    def move(it, _):
                        tok, src = _dil_chunk(it, nb, d)
                        tok_sc[tok, :] = src_pm[src, :]
                        return 0
                    lax.fori_loop(0, nchunk, move, 0, unroll=4)

                def norm_bwd(x_ref, gvec, out_ref):
                    big = 4 * DIL_BLK

                    def fin(ci, dg):
                        rows = pl.ds(pl.multiple_of(ci * big, big), big)
                        xhat, r = _dil_norm(x_ref[rows, :], gvec)
                        dn = tok_sc[rows, :]
                        dxh = dn * gvec
                        c = jnp.mean(dxh * xhat, axis=-1, keepdims=True)
                        out_ref[rows, :] = (r * (dxh - xhat * c)).astype(BF16)
                        return dg + jnp.sum(dn * xhat, axis=0, keepdims=True)
                    return lax.fori_loop(0, T // big, fin, jnp.zeros((1, DIL_HEAD_DIM), F32))

                to_tokens(dq_pm)
                dgq_ref[...] += norm_bwd(q_ref, gq, dq_ref)
                to_tokens(dk_pm)
                dgk_ref[...] += norm_bwd(k_ref, gk, dk_ref)
                to_tokens(dv_pm)
                dv_ref[...] = tok_sc[...].astype(BF16)

    col = lambda off: pl.BlockSpec((T, DIL_HEAD_DIM), lambda gh, sl: (0, gh + off))
    hcol = pl.BlockSpec((T, DIL_HEAD_DIM), lambda gh, sl: (0, gh % DIL_HEADS))
    gvec = pl.BlockSpec((1, DIL_HEAD_DIM), lambda gh, sl: (0, 0))
    wide = jax.ShapeDtypeStruct((T, GH * DIL_HEAD_DIM), BF16)
    vec = jax.ShapeDtypeStruct((1, DIL_HEAD_DIM), F32)
    pm = lambda dt: pltpu.VMEM((DIL_BLK + T, DIL_HEAD_DIM), dt)
    return pl.pallas_call(
        body, name=name,
        grid_spec=pltpu.PrefetchScalarGridSpec(
            num_scalar_prefetch=1, grid=(GH,),
            in_specs=[col(0), col(GH), col(2 * GH), gvec, gvec, hcol, hcol, hcol],
            out_specs=[col(0), col(0), col(0), gvec, gvec],
            scratch_shapes=[pm(BF16)] * 4 + [pm(F32)] * 5 + [pltpu.VMEM((T, DIL_HEAD_DIM), F32)]),
        out_shape=[wide, wide, wide, vec, vec],
        compiler_params=_params(("arbitrary",)),
    )(slopes, qkv, qkv, qkv, g_qn, g_kn, do, delta, lse)


def _my_pos():
    return lax.axis_index("x"), lax.axis_index("y"), lax.axis_index("c")


def _peer(pos, j):
    x, y, c = pos
    px = 1 - x if j & 4 else x
    py = 1 - y if j & 2 else y
    pc = 1 - c if j & 1 else c
    return (px, py, pc), 4 * px + 2 * py + pc


def _slot(idx, paired):
    if not paired:
        return idx
    return jnp.where(idx < N_DEV // 2, 2 * idx, 2 * idx - (N_DEV - 1))


def _shard_slice(ref, axis, idx, size, paired=False):
    sl = [slice(None)] * len(ref.shape)
    sl[axis] = pl.ds(pl.multiple_of(_slot(idx, paired) * size, 8), size)
    return ref.at[tuple(sl)]


HBM_SPEC = pl.BlockSpec(memory_space=pltpu.HBM)
SEM_SPEC = pl.BlockSpec(memory_space=pltpu.SEMAPHORE)
DATAFLOW = pltpu.SideEffectType.DATAFLOW_SIDE_EFFECTING
N_PEER = N_DEV - 1


def _scatter_copy(axis, grad, slots, frm, to, dev, send_sem, recv_sem):
    ax, paired = axis
    src = _shard_slice(grad, ax, to, grad.shape[ax] // N_DEV, paired)
    return pltpu.make_async_remote_copy(src_ref=src, dst_ref=slots.at[frm], send_sem=send_sem, recv_sem=recv_sem,
                                        device_id=dev, device_id_type=MESH)


def _scatter_start(grads, axes, name):
    n = len(grads)

    def body(*refs):
        outs = refs[2 * n:]
        send, recv, token = outs[:n], outs[n:2 * n], outs[4 * n]
        pos = _my_pos()
        me = 4 * pos[0] + 2 * pos[1] + pos[2]
        for a in range(n):
            for j in range(1, N_DEV):
                dev, pid = _peer(pos, j)
                _scatter_copy(axes[a], refs[2 * a], refs[2 * a + 1], me, pid, dev, send[a].at[j - 1],
                              recv[a].at[j - 1]).start()
        token[...] = jnp.zeros_like(token)

    ops = []
    for g, (ax, _) in zip(grads, axes):
        shp = list(g.shape)
        shp[ax] //= N_DEV
        ops += [g, lax.empty((N_DEV,) + tuple(shp), g.dtype)]
    sems = [pltpu.SemaphoreType.DMA((N_PEER,))] * (2 * n)
    res = pl.pallas_call(
        body, name=name,
        out_shape=sems + [pltpu.HBM(o.shape, o.dtype) for o in ops] + [jax.ShapeDtypeStruct((8, LANES), F32)],
        in_specs=[HBM_SPEC] * len(ops),
        out_specs=[SEM_SPEC] * (2 * n) + [HBM_SPEC] * len(ops) + [pl.BlockSpec(memory_space=pltpu.VMEM)],
        input_output_aliases={i: 2 * n + i for i in range(len(ops))},
        compiler_params=pltpu.CompilerParams(has_side_effects=DATAFLOW),
    )(*[pltpu.with_memory_space_constraint(o, pltpu.HBM) for o in ops])
    items = [(res[a], res[n + a], res[2 * n + 2 * a], res[2 * n + 2 * a + 1]) for a in range(n)]
    return items, res[4 * n]


def _scatter_wait(items, axes, after, name):
    n = len(items)

    def body(*refs):
        send, recv = refs[2 * n:3 * n], refs[3 * n:4 * n]
        pos = _my_pos()
        me = 4 * pos[0] + 2 * pos[1] + pos[2]
        for a in range(n):
            for j in range(1, N_DEV):
                dev, pid = _peer(pos, j)
                cp = _scatter_copy(axes[a], refs[2 * a], refs[2 * a + 1], pid, me, dev, send[a].at[j - 1],
                                   recv[a].at[j - 1])
                cp.wait_send()
                cp.wait_recv()

    ops = [b for it in items for b in it[2:]]
    res = pl.pallas_call(
        body, name=name,
        out_shape=[pltpu.HBM(o.shape, o.dtype) for o in ops],
        in_specs=[HBM_SPEC] * len(ops) + [SEM_SPEC] * (2 * n) + [ANY_SPEC],
        out_specs=[HBM_SPEC] * len(ops),
        input_output_aliases={i: i for i in range(len(ops))},
        compiler_params=pltpu.CompilerParams(has_side_effects=DATAFLOW),
    )(*ops, *[it[0] for it in items], *[it[1] for it in items], after)
    return [(res[2 * a], res[2 * a + 1]) for a in range(n)]


SIBLING = 1
ICI_PEERS = (2, 4, 6)


def _gather_copy(buf, axis, shard, dev, send_sem, recv_sem):
    ax, paired = axis
    piece = _shard_slice(buf, ax, shard, buf.shape[ax] // N_DEV, paired)
    return pltpu.make_async_remote_copy(src_ref=piece, dst_ref=piece, send_sem=send_sem, recv_sem=recv_sem,
                                        device_id=dev, device_id_type=MESH)


def _gather_start(bufs, axes, name):
    n = len(bufs)

    def body(*refs):
        ins, outs = refs[:n], refs[n:]
        send, r_sib, r_ici, token = outs[:n], outs[n:2 * n], outs[2 * n:3 * n], outs[4 * n]
        pos = _my_pos()
        me = 4 * pos[0] + 2 * pos[1] + pos[2]
        for a in range(n):
            dev, _ = _peer(pos, SIBLING)
            _gather_copy(ins[a], axes[a], me, dev, send[a].at[0], r_sib[a].at[0]).start()
            for k, j in enumerate(ICI_PEERS):
                dev, _ = _peer(pos, j)
                _gather_copy(ins[a], axes[a], me, dev, send[a].at[1 + k], r_ici[a].at[k]).start()
        token[...] = jnp.zeros_like(token)

    sems = ([pltpu.SemaphoreType.DMA((1 + len(ICI_PEERS),))] * n + [pltpu.SemaphoreType.DMA((1,))] * n
            + [pltpu.SemaphoreType.DMA((len(ICI_PEERS),))] * n)
    res = pl.pallas_call(
        body, name=name,
        out_shape=sems + [pltpu.HBM(b.shape, b.dtype) for b in bufs] + [jax.ShapeDtypeStruct((8, LANES), F32)],
        in_specs=[HBM_SPEC] * n,
        out_specs=[SEM_SPEC] * (3 * n) + [HBM_SPEC] * n + [pl.BlockSpec(memory_space=pltpu.VMEM)],
        input_output_aliases={i: 3 * n + i for i in range(n)},
        compiler_params=pltpu.CompilerParams(has_side_effects=DATAFLOW),
    )(*[pltpu.with_memory_space_constraint(b, pltpu.HBM) for b in bufs])
    items = [dict(send=res[a], r_sib=res[n + a], r_ici=res[2 * n + a], buf=res[3 * n + a]) for a in range(n)]
    return items, res[4 * n]


def _gather_relay(items, axes, after, name):
    n = len(items)

    def body(*refs):
        ins, r_ici = refs[:n], refs[n:2 * n]
        outs = refs[2 * n + 1:]
        s_rel, r_rel, token = outs[:n], outs[n:2 * n], outs[3 * n]
        pos = _my_pos()
        sib, _ = _peer(pos, SIBLING)
        for a in range(n):
            for k, j in enumerate(ICI_PEERS):
                dev, pid = _peer(pos, j)
                _gather_copy(ins[a], axes[a], pid, dev, s_rel[a].at[k], r_ici[a].at[k]).wait_recv()
                _gather_copy(ins[a], axes[a], pid, sib, s_rel[a].at[k], r_rel[a].at[k]).start()
        token[...] = jnp.zeros_like(token)

    bufs = [it["buf"] for it in items]
    sems = [pltpu.SemaphoreType.DMA((len(ICI_PEERS),))] * (2 * n)
    res = pl.pallas_call(
        body, name=name,
        out_shape=sems + [pltpu.HBM(b.shape, b.dtype) for b in bufs] + [jax.ShapeDtypeStruct((8, LANES), F32)],
        in_specs=[HBM_SPEC] * n + [SEM_SPEC] * n + [ANY_SPEC],
        out_specs=[SEM_SPEC] * (2 * n) + [HBM_SPEC] * n + [pl.BlockSpec(memory_space=pltpu.VMEM)],
        input_output_aliases={i: 2 * n + i for i in range(n)},
        compiler_params=pltpu.CompilerParams(has_side_effects=DATAFLOW),
    )(*bufs, *[it["r_ici"] for it in items], after)
    out = [dict(send=it["send"], r_sib=it["r_sib"], s_rel=res[a], r_rel=res[n + a], buf=res[2 * n + a])
           for a, it in enumerate(items)]
    return out, res[3 * n]


def _gather_wait(items, axes, after, name):
    n = len(items)

    def body(*refs):
        ins = refs[:n]
        send, r_sib, s_rel, r_rel = (refs[(1 + q) * n:(2 + q) * n] for q in range(4))
        pos = _my_pos()
        me = 4 * pos[0] + 2 * pos[1] + pos[2]
        sib, sib_id = _peer(pos, SIBLING)
        for a in range(n):
            for k in range(1 + len(ICI_PEERS)):
                _gather_copy(ins[a], axes[a], me, sib, send[a].at[k], r_sib[a].at[0]).wait_send()
            _gather_copy(ins[a], axes[a], sib_id, sib, send[a].at[0], r_sib[a].at[0]).wait_recv()
            for k, j in enumerate(ICI_PEERS):
                _, pid = _peer(pos, j)
                _, far = _peer(pos, j ^ SIBLING)
                _gather_copy(ins[a], axes[a], pid, sib, s_rel[a].at[k], r_rel[a].at[k]).wait_send()
                _gather_copy(ins[a], axes[a], far, sib, s_rel[a].at[k], r_rel[a].at[k]).wait_recv()

    bufs = [it["buf"] for it in items]
    res = pl.pallas_call(
        body, name=name,
        out_shape=[pltpu.HBM(b.shape, b.dtype) for b in bufs],
        in_specs=[HBM_SPEC] * n + [SEM_SPEC] * (4 * n) + [ANY_SPEC],
        out_specs=[HBM_SPEC] * n,
        input_output_aliases={i: i for i in range(n)},
        compiler_params=pltpu.CompilerParams(has_side_effects=DATAFLOW),
    )(*bufs, *[it["send"] for it in items], *[it["r_sib"] for it in items], *[it["s_rel"] for it in items],
      *[it["r_rel"] for it in items], after)
    return list(res)


def _gain_allreduce(v, name):
    n = v.shape[1]

    def body(v_ref, o_ref, slots, send_sems, recv_sems):
        pos = _my_pos()
        me = 4 * pos[0] + 2 * pos[1] + pos[2]
        slots[me] = v_ref[...]
        copies = []
        for j in range(1, N_DEV):
            dev, _ = _peer(pos, j)
            cp = pltpu.make_async_remote_copy(
                src_ref=slots.at[me], dst_ref=slots.at[me], send_sem=send_sems.at[j], recv_sem=recv_sems.at[j],
                device_id=dev, device_id_type=MESH)
            cp.start()
            copies.append(cp)
        for j in range(1, N_DEV):
            dev, pid = _peer(pos, j)
            pltpu.make_async_remote_copy(
                src_ref=slots.at[me], dst_ref=slots.at[pid], send_sem=send_sems.at[j], recv_sem=recv_sems.at[j],
                device_id=dev, device_id_type=MESH).wait_recv()
        for cp in copies:
            cp.wait_send()
        acc = slots[0]
        for s in range(1, N_DEV):
            acc = acc + slots[s]
        o_ref[...] = acc

    return pl.pallas_call(
        body, name=name, out_shape=jax.ShapeDtypeStruct((1, n), F32),
        in_specs=[pl.BlockSpec(memory_space=pltpu.VMEM)], out_specs=pl.BlockSpec(memory_space=pltpu.VMEM),
        scratch_shapes=[pltpu.VMEM((N_DEV, 1, n), F32), pltpu.SemaphoreType.DMA((N_DEV,)),
                        pltpu.SemaphoreType.DMA((N_DEV,))],
        compiler_params=pltpu.CompilerParams(has_side_effects=True),
    )(v)


def _adamw(parts, own, me, w, m, v, layer, prev, name, own_axis=None):
    L, R, C = w.shape
    P = parts.shape[0]
    tr = _pick(R, (128, 64, 32, 16, 8, 1))
    c1 = 1.0 - ADAM_B1 ** ADAM_STEP
    c2 = 1.0 - ADAM_B2 ** ADAM_STEP
    n_in = 4 if own is None else 5

    def body(me_ref, *refs):
        p_ref = refs[0]
        w_ref, m_ref, v_ref = refs[n_in - 3:n_in]
        g_out, d_out, m_out, v_out, tok = refs[-5:]
        g = None
        for s in range(P):
            part = p_ref[s]
            if own is not None:
                part = jnp.where(me_ref[0] == s, refs[1][...], part)
            g = part.astype(F32) if g is None else g + part.astype(F32)
        mn = ADAM_B1 * m_ref[...] + (1.0 - ADAM_B1) * g
        vn = ADAM_B2 * v_ref[...] + (1.0 - ADAM_B2) * (g * g)
        g_out[...] = g
        m_out[...] = mn
        v_out[...] = vn
        d_out[...] = -ADAM_LR * ((mn / c1) / (jnp.sqrt(vn / c2) + ADAM_EPS) + ADAM_WD * w_ref[...])
        tok[...] = jnp.zeros_like(tok)

    row = pl.BlockSpec((None, tr, C), lambda i, me_ref: (layer, i, 0))
    in_specs = [pl.BlockSpec((P, tr, C), lambda i, me_ref: (0, i, 0))]
    args = [parts]
    if own is not None:
        if own_axis is None:
            own_idx = lambda i, me_ref: (i, 0)
        elif own_axis[0] == 0:
            own_idx = lambda i, me_ref: (_slot(me_ref[0], own_axis[1]) * (R // tr) + i, 0)
        else:
            own_idx = lambda i, me_ref: (i, _slot(me_ref[0], own_axis[1]))
        in_specs.append(pl.BlockSpec((tr, C), own_idx))
        args.append(own)
    in_specs += [row, row, row]
    args += [w, m, v]
    aliases = {}
    if prev is not None:
        in_specs += [ANY_SPEC] * 4
        aliases = {1 + len(args) + k: k for k in range(4)}
        args += list(prev)
    shp = jax.ShapeDtypeStruct((L, R, C), F32)
    res = pl.pallas_call(
        body, name=name,
        grid_spec=pltpu.PrefetchScalarGridSpec(
            num_scalar_prefetch=1, grid=(R // tr,), in_specs=in_specs,
            out_specs=[row] * 4 + [pl.BlockSpec((8, LANES), lambda i, me_ref: (0, 0))]),
        out_shape=[shp] * 4 + [jax.ShapeDtypeStruct((8, LANES), F32)],
        input_output_aliases=aliases, compiler_params=_params(("arbitrary",)),
    )(me, *args)
    return res[:4], res[4]


def _pad_heads(w):
    lead = w.shape[:-1]
    n = w.shape[-1] // QK_DIM
    w = w.reshape(lead + (n, QK_DIM))
    w = jnp.pad(w, [(0, 0)] * len(lead) + [(0, 0), (0, HEAD_PAD - QK_DIM)])
    return w.reshape(lead + (n * HEAD_PAD,))


def _unpad_heads(w):
    lead = w.shape[:-1]
    n = w.shape[-1] // HEAD_PAD
    return w.reshape(lead + (n, HEAD_PAD))[..., :QK_DIM].reshape(lead + (n * QK_DIM,))


def kernel(x, ffn1_norm, ffn1_w_in, ffn1_w_out, mix_norm, ffn2_norm, ffn2_w_in, ffn2_w_out, mla_w_down, mla_g_cq, mla_g_ckv, mla_w_uq, mla_w_ukv, mla_g_qn, mla_g_kn, mla_w_o, dil_w_qkv, dil_g_qn, dil_g_kn, dil_w_o, loss_target, m_ffn1_norm, m_ffn1_w_in, m_ffn1_w_out, m_mix_norm, m_ffn2_norm, m_ffn2_w_in, m_ffn2_w_out, m_mla_w_down, m_mla_g_cq, m_mla_g_ckv, m_mla_w_uq, m_mla_w_ukv, m_mla_g_qn, m_mla_g_kn, m_mla_w_o, m_dil_w_qkv, m_dil_g_qn, m_dil_g_kn, m_dil_w_o, v_ffn1_norm, v_ffn1_w_in, v_ffn1_w_out, v_mix_norm, v_ffn2_norm, v_ffn2_w_in, v_ffn2_w_out, v_mla_w_down, v_mla_g_cq, v_mla_g_ckv, v_mla_w_uq, v_mla_w_ukv, v_mla_g_qn, v_mla_g_kn, v_mla_w_o, v_dil_w_qkv, v_dil_g_qn, v_dil_g_kn, v_dil_w_o):
    names = ["ffn1_norm", "ffn1_w_in", "ffn1_w_out", "mix_norm", "ffn2_norm", "ffn2_w_in", "ffn2_w_out", "mla_w_down",
             "mla_g_cq", "mla_g_ckv", "mla_w_uq", "mla_w_ukv", "mla_g_qn", "mla_g_kn", "mla_w_o", "dil_w_qkv",
             "dil_g_qn", "dil_g_kn", "dil_w_o"]
    W = dict(zip(names, [ffn1_norm, ffn1_w_in, ffn1_w_out, mix_norm, ffn2_norm, ffn2_w_in, ffn2_w_out, mla_w_down,
                         mla_g_cq, mla_g_ckv, mla_w_uq, mla_w_ukv, mla_g_qn, mla_g_kn, mla_w_o, dil_w_qkv,
                         dil_g_qn, dil_g_kn, dil_w_o]))
    M1 = dict(zip(names, [m_ffn1_norm, m_ffn1_w_in, m_ffn1_w_out, m_mix_norm, m_ffn2_norm, m_ffn2_w_in, m_ffn2_w_out,
                          m_mla_w_down, m_mla_g_cq, m_mla_g_ckv, m_mla_w_uq, m_mla_w_ukv, m_mla_g_qn, m_mla_g_kn,
                          m_mla_w_o, m_dil_w_qkv, m_dil_g_qn, m_dil_g_kn, m_dil_w_o]))
    V2 = dict(zip(names, [v_ffn1_norm, v_ffn1_w_in, v_ffn1_w_out, v_mix_norm, v_ffn2_norm, v_ffn2_w_in, v_ffn2_w_out,
                          v_mla_w_down, v_mla_g_cq, v_mla_g_ckv, v_mla_w_uq, v_mla_w_ukv, v_mla_g_qn, v_mla_g_kn,
                          v_mla_w_o, v_dil_w_qkv, v_dil_g_qn, v_dil_g_kn, v_dil_w_o]))
    S, D = x.shape[1], x.shape[2]
    x0 = x.reshape(S, D)
    tgt = loss_target.reshape(S, D)

    big = ["ffn1_w_in", "ffn1_w_out", "ffn2_w_in", "ffn2_w_out", "mla_w_down", "mla_w_uq", "mla_w_ukv", "mla_w_o",
           "dil_w_qkv", "dil_w_o"]
    shard_dim = {"ffn1_w_in": 2, "ffn1_w_out": 1, "ffn2_w_in": 2, "ffn2_w_out": 1, "mla_w_down": 1, "mla_w_uq": 2,
                 "mla_w_ukv": 2, "mla_w_o": 1, "dil_w_qkv": 2, "dil_w_o": 2}
    paired = ("ffn1_w_in", "ffn2_w_in")
    shard_axis = {n: (d, n in paired) for n, d in shard_dim.items()}
    grad_axis = {n: (d - 1, n in paired) for n, d in shard_dim.items()}

    def padded(n, w):
        if n == "mla_w_down":
            return jnp.pad(w, ((0, 0), (0, 0), (0, LAT_PAD - w.shape[2])))
        if n == "mla_w_uq":
            return _pad_heads(w)
        return w

    depth = ffn1_norm.shape[0]
    blocks = []
    for l in range(depth):
        mixer = (["mla_w_down", "mla_w_uq", "mla_w_ukv", "mla_w_o"] if l % 2 == 0 else ["dil_w_qkv", "dil_w_o"])
        blocks.append((f"ffn1_{l}", [("ffn1_w_in", l), ("ffn1_w_out", l)]))
        blocks.append((f"mix_{l}", [(n, l // 2) for n in mixer]))
        blocks.append((f"ffn2_{l}", [("ffn2_w_in", l), ("ffn2_w_out", l)]))
    order = [k for _, keys in blocks for k in keys]
    me = (4 * lax.axis_index("x") + 2 * lax.axis_index("y") + lax.axis_index("c")).astype(jnp.int32).reshape(1)
    def cast(key, deps=()):
        n, l = key
        return _cast_into_gathered(padded(n, W[n]), l, shard_axis[n], me, f"cast_{n}_{l}", deps=deps)

    items0, token0 = _gather_start([cast(order[0])], [shard_axis[order[0][0]]], "gather_start_first")
    rest = order[1:]
    items1, ag_token = _gather_start([cast(k, deps=[token0]) for k in rest], [shard_axis[k[0]] for k in rest],
                                     "gather_start_rest")
    ag_items = dict(zip(order, items0 + items1))
    full = {}

    def relay(keys, after, tag):
        out, token = _gather_relay([ag_items[k] for k in keys], [shard_axis[k[0]] for k in keys], after,
                                   f"gather_relay_{tag}")
        ag_items.update(zip(keys, out))
        return [token]

    def relay_next(bi, after):
        return relay(blocks[bi + 1][1], after, blocks[bi + 1][0]) if bi + 1 < len(blocks) else []

    def fetch(keys, after, tag):
        lands = _gather_wait([ag_items[k] for k in keys], [shard_axis[k[0]] for k in keys], after,
                             f"gather_wait_{tag}")
        full.update(zip(keys, lands))

    g_qn = _pad_heads(mla_g_qn)
    g_kn = _pad_heads(mla_g_kn)
    tabs = _rope_tables(S)
    slopes = jnp.asarray(_alibi_slopes(), F32)

    grads = {}
    gain_g = {}

    out_g, out_d, out_m, out_v = {}, {}, {}, {}
    pending = []
    lag = 3

    def scatter_start(tag, keys):
        items, token = _scatter_start([grads[k] for k in keys], [grad_axis[k[0]] for k in keys],
                                      f"scatter_start_{tag}")
        pending.append((tag, keys, items))
        return token

    def scatter_finish(after):
        tag, keys, items = pending.pop(0)
        lands = _scatter_wait(items, [grad_axis[k[0]] for k in keys], after, f"scatter_wait_{tag}")
        tokens = []
        for (n, l), (own, p) in zip(keys, lands):
            own_axis = grad_axis[n]
            if n in ("mla_w_down", "mla_w_uq"):
                ax, pair = grad_axis[n]
                size = own.shape[ax] // N_DEV
                own = lax.dynamic_slice_in_dim(own, _slot(me[0], pair) * size, size, axis=ax)
                own_axis = None
                if n == "mla_w_down":
                    p, own = p[..., :W[n].shape[2]], own[..., :W[n].shape[2]]
                else:
                    p, own = _unpad_heads(p), _unpad_heads(own)
            prev = (out_g[n], out_d[n], out_m[n], out_v[n]) if n in out_g else None
            (out_g[n], out_d[n], out_m[n], out_v[n]), tok = _adamw(p, own, me, W[n], M1[n], V2[n], l, prev,
                                                                    f"adamw_{n}_{l}", own_axis=own_axis)
            tokens.append(tok)
        return tokens

    def finish_due(after):
        tokens = []
        while len(pending) > lag:
            tokens += scatter_finish(after)
        return tokens

    def ffn_fwd(xin, norm_row, which, l, bi, deps=()):
        tag = blocks[bi][0]
        k_in, k_out = (which + "_w_in", l), (which + "_w_out", l)
        h = _rms_fwd(xin, norm_row, f"rms_fwd_{tag}", deps=deps)
        if bi == 0:
            relay([k_in], h, f"{tag}_in")
        fetch([k_in], h, f"in_{tag}")
        u, a = _ffn_in(h, full[k_in], f"ffn_in_{tag}")
        if bi == 0:
            relay([k_out], a, f"{tag}_out")
        fetch([k_out], a, f"out_{tag}")
        toks = relay_next(bi, a)
        xo = _mm(a, full[k_out], "nn", F32, f"mm_out_{tag}", scale=0.5, res=xin, layer=0, deps=toks)
        return xo, (xin, h, u, a)

    def ffn_bwd(dx_pair, saved, norm_row, which, l, tag, deps=()):
        dxo, dxob = dx_pair
        k_in, k_out = (which + "_w_in", l), (which + "_w_out", l)
        xin, h, u, a = saved
        grads[k_out] = _mm(a, dxob, "tn", BF16, f"mm_dwout_{tag}", scale=0.5, deps=deps)
        t_out = scatter_start(f"{tag}_out", [k_out])
        du = _ffn_da(dxob, full[k_out], u, f"ffn_da_{tag}", deps=[t_out])
        grads[k_in] = _mm(h, du, "tn", BF16, f"mm_dwin_{tag}")
        t_in = scatter_start(f"{tag}_in", [k_in])
        dh = _mm(du, full[k_in], "nt", F32, f"mm_dh_{tag}", layer=0, deps=[t_in])
        toks = finish_due(dh)
        dx, dxb, dg = _rms_bwd(xin, norm_row, dh, dxo, f"rms_bwd_{tag}", deps=toks)
        gain_g.setdefault(which + "_norm", {})[l] = dg
        return dx, dxb

    def mla_fwd(xin, l, bi):
        j = l // 2
        xn = _rms_fwd(xin, mix_norm[l:l + 1], "rms_fwd_mla")
        fetch([(n, j) for n in ("mla_w_down", "mla_w_uq", "mla_w_ukv", "mla_w_o")], xn, "mla")
        lat = _mm(xn, full[("mla_w_down", j)], "nn", F32, "mm_lat", layer=0)
        cq, ckv = _lat_norm_fwd(lat, mla_g_cq[j:j + 1], mla_g_ckv[j:j + 1], "lat_norm_fwd")
        q_raw = _mm(cq, full[("mla_w_uq", j)], "nn", F32, "mm_uq", layer=0)
        kv = _mm(ckv, full[("mla_w_ukv", j)], "nn", F32, "mm_ukv", layer=0)
        qf, kf, vb = _mla_prep_fwd(q_raw, kv, lat, g_qn[j:j + 1], g_kn[j:j + 1], tabs, "mla_prep_fwd")
        o, lse = _flash_fwd(qf, kf, vb, "flash_fwd")
        toks = relay_next(bi, o)
        xo = _mm(o, full[("mla_w_o", j)], "nn", F32, "mm_mla_o", res=xin, layer=0, deps=toks)
        return xo, (xin, xn, lat, cq, ckv, q_raw, kv, qf, kf, vb, o, lse)

    def mla_bwd(dx_pair, saved, l):
        dxo, dxob = dx_pair
        j = l // 2
        xin, xn, lat, cq, ckv, q_raw, kv, qf, kf, vb, o, lse = saved
        do = _mm(dxob, full[("mla_w_o", j)], "nt", BF16, "mm_mla_do", layer=0)
        grads[("mla_w_o", j)] = _mm(o, dxob, "tn", BF16, "mm_mla_dwo")
        delta = _attn_delta(do, o, "attn_delta")
        dqf, dkf, dv = _flash_bwd(qf, kf, vb, do, lse, delta, "flash_bwd")
        dq_raw, dkv, dkpe, dgq, dgk = _mla_prep_bwd(q_raw, kv, lat, g_qn[j:j + 1], g_kn[j:j + 1], tabs, dqf, dkf, dv,
                                                    "mla_prep_bwd")
        gain_g.setdefault("mla_g_qn", {})[j] = dgq
        gain_g.setdefault("mla_g_kn", {})[j] = dgk
        dcq = _mm(dq_raw, full[("mla_w_uq", j)], "nt", F32, "mm_dcq", layer=0)
        grads[("mla_w_uq", j)] = _mm(cq, dq_raw, "tn", BF16, "mm_dwuq")
        dckv = _mm(dkv, full[("mla_w_ukv", j)], "nt", F32, "mm_dckv", layer=0)
        grads[("mla_w_ukv", j)] = _mm(ckv, dkv, "tn", BF16, "mm_dwukv")
        dlat, dgcq, dgckv = _lat_norm_bwd(lat, mla_g_cq[j:j + 1], mla_g_ckv[j:j + 1], dcq, dckv, dkpe, "lat_norm_bwd")
        gain_g.setdefault("mla_g_cq", {})[j] = dgcq
        gain_g.setdefault("mla_g_ckv", {})[j] = dgckv
        dxn = _mm(dlat, full[("mla_w_down", j)], "nt", F32, "mm_dxn_mla", layer=0)
        grads[("mla_w_down", j)] = _mm(xn, dlat, "tn", BF16, "mm_dwdown")
        tok = scatter_start(f"mix_{l}", [(n, j) for n in ("mla_w_down", "mla_w_uq", "mla_w_ukv", "mla_w_o")])
        toks = finish_due(dxn)
        dx, dxb, dg = _rms_bwd(xin, mix_norm[l:l + 1], dxn, dxo, "rms_bwd_mla", deps=[tok] + toks)
        gain_g.setdefault("mix_norm", {})[l] = dg
        return dx, dxb

    def dil_fwd(xin, l, bi):
        j = l // 2
        xn = _rms_fwd(xin, mix_norm[l:l + 1], "rms_fwd_dil")
        fetch([("dil_w_qkv", j), ("dil_w_o", j)], xn, "dil")
        qkv = _mm(xn, full[("dil_w_qkv", j)], "nn", F32, "mm_qkv", layer=0)
        o_g, lse_g = _dil_fwd(qkv, dil_g_qn[j:j + 1], dil_g_kn[j:j + 1], slopes, "dil_fwd")
        o, lse = _dil_merge(o_g, lse_g, "dil_merge")
        toks = relay_next(bi, o)
        xo = _mm(o, full[("dil_w_o", j)], "nn", F32, "mm_dil_o", res=xin, layer=0, deps=toks)
        return xo, (xin, xn, qkv, o, lse)

    def dil_bwd(dx_pair, saved, l):
        dxo, dxob = dx_pair
        j = l // 2
        xin, xn, qkv, o, lse = saved
        do = _mm(dxob, full[("dil_w_o", j)], "nt", F32, "mm_dil_do", layer=0)
        grads[("dil_w_o", j)] = _mm(o, dxob, "tn", BF16, "mm_dil_dwo")
        delta = _attn_delta(do, o, "dil_delta")
        dq, dk, dv, dgq, dgk = _dil_bwd(qkv, dil_g_qn[j:j + 1], dil_g_kn[j:j + 1], slopes, do, delta, lse, "dil_bwd")
        gain_g.setdefault("dil_g_qn", {})[j] = dgq
        gain_g.setdefault("dil_g_kn", {})[j] = dgk
        dqkv = jnp.concatenate([dq, dk, dv], axis=1)
        dxn = _mm(dqkv, full[("dil_w_qkv", j)], "nt", F32, "mm_dxn_dil", layer=0)
        grads[("dil_w_qkv", j)] = _mm(xn, dqkv, "tn", BF16, "mm_dwqkv")
        tok = scatter_start(f"mix_{l}", [("dil_w_qkv", j), ("dil_w_o", j)])
        toks = finish_due(dxn)
        dx, dxb, dg = _rms_bwd(xin, mix_norm[l:l + 1], dxn, dxo, "rms_bwd_dil", deps=[tok] + toks)
        gain_g.setdefault("mix_norm", {})[l] = dg
        return dx, dxb

    saved = []
    xc = x0
    for l in range(depth):
        xc, s1 = ffn_fwd(xc, ffn1_norm[l:l + 1], "ffn1", l, 3 * l, deps=[ag_token] if l == 0 else ())
        xc, s2 = (mla_fwd if l % 2 == 0 else dil_fwd)(xc, l, 3 * l + 1)
        xc, s3 = ffn_fwd(xc, ffn2_norm[l:l + 1], "ffn2", l, 3 * l + 2)
        saved.append((s1, s2, s3))

    dy, dyb, loss_part = _loss_head(xc, tgt, "loss_head")
    dx = (dy, dyb)
    loss = lax.psum(loss_part[0, 0], MESH_AXES)

    for bi in reversed(range(len(blocks))):
        tag, _ = blocks[bi]
        l = bi // 3
        s = saved[l][bi % 3]
        if bi % 3 == 2:
            dx = ffn_bwd(dx, s, ffn2_norm[l:l + 1], "ffn2", l, tag,
                         deps=[loss.reshape(1, 1)] if bi == len(blocks) - 1 else ())
        elif bi % 3 == 1:
            dx = (mla_bwd if l % 2 == 0 else dil_bwd)(dx, s, l)
        else:
            dx = ffn_bwd(dx, s, ffn1_norm[l:l + 1], "ffn1", l, tag)
    grad_x = dx[0].reshape(x.shape)
    after = dx[1]
    while pending:
        after = scatter_finish(after)[-1]

    small = [n for n in names if n not in big]

    def gain_local(n):
        rows = [gain_g[n][l] for l in range(W[n].shape[0])]
        g = jnp.concatenate(rows, axis=1)
        return g

    def flat_pad(n, a):
        a = a.reshape(1, -1)
        if n in ("mla_g_qn", "mla_g_kn"):
            a = _pad_heads(a)
        return a

    packed_g = jnp.concatenate([gain_local(n) for n in small], axis=1)
    sizes = [gain_local(n).shape[1] for n in small]
    tot_g = _gain_allreduce(packed_g, "gain_allreduce")
    pw = jnp.concatenate([flat_pad(n, W[n]) for n in small], axis=1)
    pm = jnp.concatenate([flat_pad(n, M1[n]) for n in small], axis=1)
    pv = jnp.concatenate([flat_pad(n, V2[n]) for n in small], axis=1)
    res, _ = _adamw(tot_g.reshape(1, 1, -1), None, me, pw.reshape(1, 1, -1), pm.reshape(1, 1, -1),
                    pv.reshape(1, 1, -1), 0, None, "adamw_gains")
    res = [r.reshape(1, -1) for r in res]
    off = 0
    for n, sz in zip(small, sizes):
        for dst, r in zip((out_g, out_d, out_m, out_v), res):
            piece = r[:, off:off + sz]
            if n in ("mla_g_qn", "mla_g_kn"):
                piece = _unpad_heads(piece)
            dst[n] = piece.reshape(W[n].shape)
        off += sz

    return (loss, grad_x, *[out_g[n] for n in names], *[out_d[n] for n in names],
            *[out_m[n] for n in names], *[out_v[n] for n in names])
```

```python
import functools
import math

import jax
import jax.numpy as jnp
import numpy as np
from jax import lax
from jax.experimental import pallas as pl
from jax.experimental.pallas import tpu as pltpu

EPS = 1e-6
MLA_HEADS = 16
Q_LORA = 512
KV_LORA = 512
NOPE_DIM = 128
ROPE_DIM = 64
V_DIM = 128
QK_DIM = NOPE_DIM + ROPE_DIM
ROPE_THETA = 10000.0
HEAD_PAD = 256
LAT_PAD = Q_LORA + KV_LORA + 128
DIL_PAIRS = ((128, 1), (512, 4), (2048, 16))
DIL_GROUPS = 3
DIL_HEADS = 8
DIL_HEAD_DIM = 128
DIL_BLK = 128
FLASH_HEADS = 2
FLASH_HEADS_FWD = 4
LOG2E = math.log2(math.e)
LN2 = math.log(2.0)
ADAM_LR = 0.001
ADAM_B1 = 0.9
ADAM_B2 = 0.999
ADAM_EPS = 1e-08
ADAM_WD = 0.01
ADAM_STEP = 10

N_DEV = 8
MESH_AXES = ("x", "y", "c")
MESH = pl.DeviceIdType.MESH
NEG_BIG = -1e30
VMEM_LIMIT_V7X = 56 * 1024 * 1024
LANES = 128

BF16 = jnp.bfloat16
F32 = jnp.float32


def _pick(n, cands):
    for c in cands:
        if n % c == 0:
            return c
    raise ValueError(f"no tile for {n}")


def _params(sem):
    return pltpu.CompilerParams(dimension_semantics=sem, vmem_limit_bytes=VMEM_LIMIT_V7X)


ANY_SPEC = pl.BlockSpec(memory_space=pl.ANY)


MM_VMEM_BUDGET = 44 * 1024 * 1024
MM_HBM_BYTES_PER_S = 1.8e12
MM_MXU_FLOPS_PER_S = 8.5e14
MM_STEP_S = 0.4e-6
MM_MAX_TILE_MACS = 3.3e9
MXU_DIM = 256


MM_TIMED_TILES = {
    (2048, 4096, 11264, 2, 2, 2, False): (1024, 1024, 4096, True),
    (4096, 11264, 2048, 2, 2, 4, False): (1024, 1024, 2816, True),
    (4096, 5632, 2048, 2, 2, 4, True): (512, 1024, 5632, False),
    (2048, 4096, 9216, 2, 2, 2, False): (1024, 1024, 4096, True),
    (4096, 9216, 2048, 2, 2, 4, False): (1024, 1024, 2304, True),
}


@functools.lru_cache(maxsize=None)
def _mm_tiles(M, K, N, a_bytes, b_bytes, out_bytes, has_res):
    if (M, K, N, a_bytes, b_bytes, out_bytes, has_res) in MM_TIMED_TILES:
        return MM_TIMED_TILES[(M, K, N, a_bytes, b_bytes, out_bytes, has_res)]
    best = None
    for tk in [K] + [c for c in (1408, 1024, 512, 384, 256, 128) if K % c == 0 and c < K]:
        nk = K // tk
        for tm in [c for c in (2048, 1024, 512, 256, 128) if M % c == 0]:
            for tn in [c for c in (2816, 2048, 1408, 1152, 1024, 512, 384, 256, 128) if N % c == 0]:
                if tm * tk * tn > MM_MAX_TILE_MACS:
                    continue
                fill = (tn / (-(-tn // MXU_DIM) * MXU_DIM)) * (tk / (-(-tk // MXU_DIM) * MXU_DIM))
                fill *= tm / (tm + MXU_DIM // 2)
                vmem = 2 * (tm * tk * a_bytes + tk * tn * b_bytes) + 2 * tm * tn * out_bytes + tm * tn * 4
                vmem += (tm * tk + tk * tn) * 2 if max(a_bytes, b_bytes) > 2 else 0
                vmem += 2 * tm * tn * 4 if has_res else 0
                if vmem > MM_VMEM_BUDGET:
                    continue
                a_all, b_all = M * K * a_bytes, K * N * b_bytes
                if nk == 1:
                    t_i = a_all + (M // tm) * b_all
                    t_j = b_all + (N // tn) * a_all
                    traffic, i_outer = min((t_i, True), (t_j, False))
                else:
                    traffic, i_outer = (N // tn) * a_all + (M // tm) * b_all, True
                traffic += M * N * (out_bytes + (4 if has_res else 0))
                mxu = 2.0 * M * K * N / (MM_MXU_FLOPS_PER_S * fill) * (1.15 if nk > 1 else 1.0)
                cost = max(traffic / MM_HBM_BYTES_PER_S, mxu) + (M // tm) * (N // tn) * nk * MM_STEP_S
                if best is None or cost < best[0]:
                    best = (cost, tm, tn, tk, i_outer)
    assert best is not None, (M, K, N)
    return best[1:]


def _mm(a, b, mode, out_dtype, name, *, scale=1.0, res=None, layer=None, deps=(), norm_gain=None):
    b2 = b.shape[-2:]
    if mode == "nn":
        (M, K), (Kb, N) = a.shape, b2
    elif mode == "nt":
        (M, K), (N, Kb) = a.shape, b2
    else:
        (K, M), (Kb, N) = a.shape, b2
    assert K == Kb, (a.shape, b.shape, mode)
    tm, tn, tk, i_outer = _mm_tiles(M, K, N, a.dtype.itemsize, b.dtype.itemsize, jnp.dtype(out_dtype).itemsize,
                                    res is not None)
    nk = K // tk
    dims = {"nn": (((1,), (0,)), ((), ())), "nt": (((1,), (1,)), ((), ())), "tn": (((0,), (0,)), ((), ()))}[mode]
    normed = norm_gain is not None
    assert not normed or (nk == 1 and tn == N), (name, tn, N, nk)
    n_in = 2 + (res is not None) + normed + len(deps)

    def finish(v, r_ref, o_ref):
        if scale != 1.0:
            v = v * scale
        if r_ref is not None:
            v = r_ref[...] + v
        o_ref[...] = v.astype(o_ref.dtype)
        return v

    def body(*refs):
        a_ref, b_ref = refs[:2]
        r_ref = refs[2] if res is not None else None
        prod = lambda: lax.dot_general(a_ref[...].astype(BF16), b_ref[...].astype(BF16), dims,
                                       preferred_element_type=F32)
        if nk == 1:
            v = finish(prod(), r_ref, refs[n_in])
            if normed:
                g_ref = refs[2 + (res is not None)]
                r = lax.rsqrt(jnp.mean(v * v, axis=-1, keepdims=True) + EPS)
                refs[n_in + 1][...] = ((v * r) * g_ref[...]).astype(BF16)
            return
        o_ref, acc = refs[-2:]
        k = pl.program_id(2)

        @pl.when(k == 0)
        def _():
            acc[...] = prod()

        @pl.when(k > 0)
        def _():
            acc[...] += prod()

        @pl.when(k == nk - 1)
        def _():
            finish(acc[...], r_ref, o_ref)

    ij = (lambda p, q: (p, q)) if i_outer else (lambda p, q: (q, p))

    def spec(shape, f, lead=None):
        full = lambda p, q, k: f(*ij(p, q), k)
        if lead is None:
            return pl.BlockSpec(shape, full)
        return pl.BlockSpec((None,) + shape, lambda p, q, k: (lead,) + full(p, q, k))

    a_spec = spec((tk, tm), lambda i, j, k: (k, i)) if mode == "tn" else spec((tm, tk), lambda i, j, k: (i, k))
    lead = layer if b.ndim == 3 else None
    b_spec = spec((tn, tk), lambda i, j, k: (j, k), lead) if mode == "nt" else spec((tk, tn), lambda i, j, k: (k, j), lead)
    in_specs = [a_spec, b_spec]
    args = [a, b]
    if res is not None:
        in_specs.append(spec((tm, tn), lambda i, j, k: (i, j)))
        args.append(res)
    if normed:
        in_specs.append(pl.BlockSpec((1, N), lambda p, q, k: (0, 0)))
        args.append(norm_gain)
    in_specs += [ANY_SPEC] * len(deps)
    args += list(deps)
    outer, inner = (M // tm, N // tn) if i_outer else (N // tn, M // tm)
    tile = spec((tm, tn), lambda i, j, k: (i, j))
    out = pl.pallas_call(
        body, name=name, grid=(outer, inner, nk),
        in_specs=in_specs, out_specs=[tile, tile] if normed else tile,
        out_shape=([jax.ShapeDtypeStruct((M, N), out_dtype), jax.ShapeDtypeStruct((M, N), BF16)] if normed
                   else jax.ShapeDtypeStruct((M, N), out_dtype)),
        scratch_shapes=[pltpu.VMEM((tm, tn), F32)] if nk > 1 else [],
        compiler_params=_params(("parallel", "parallel", "arbitrary")),
    )(*args)
    return tuple(out) if normed else out


def _cast_into_gathered(w, layer, axis, me, name, deps=()):
    _, R, C = w.shape
    tr = _pick(R, (512, 256, 128, 64, 32, 16))
    nr = R // tr
    axis, paired = axis

    def body(me_ref, w_ref, *rest):
        o_ref = rest[-1]
        o_ref[...] = w_ref[...].astype(BF16)

    if axis == 1:
        out_idx = lambda i, me_ref: (0, _slot(me_ref[0], paired) * nr + i, 0)
        shape = (1, R * N_DEV, C)
    else:
        out_idx = lambda i, me_ref: (0, i, _slot(me_ref[0], paired))
        shape = (1, R, C * N_DEV)
    return pl.pallas_call(
        body, name=name,
        grid_spec=pltpu.PrefetchScalarGridSpec(
            num_scalar_prefetch=1, grid=(nr,),
            in_specs=[pl.BlockSpec((None, tr, C), lambda i, me_ref: (layer, i, 0))] + [ANY_SPEC] * len(deps),
            out_specs=pl.BlockSpec((None, tr, C), out_idx)),
        out_shape=jax.ShapeDtypeStruct(shape, BF16), compiler_params=_params(("parallel",)),
    )(me, w, *deps)


def _rms_fwd(x, g, name, deps=()):
    T, D = x.shape
    tr = _pick(T, (512, 256, 128))

    def body(x_ref, g_ref, *rest):
        o_ref = rest[-1]
        xv = x_ref[...]
        r = lax.rsqrt(jnp.mean(xv * xv, axis=-1, keepdims=True) + EPS)
        o_ref[...] = ((xv * r) * g_ref[...]).astype(BF16)

    return pl.pallas_call(
        body, name=name, grid=(T // tr,),
        in_specs=[pl.BlockSpec((tr, D), lambda i: (i, 0)), pl.BlockSpec((1, D), lambda i: (0, 0))]
        + [ANY_SPEC] * len(deps),
        out_specs=pl.BlockSpec((tr, D), lambda i: (i, 0)),
        out_shape=jax.ShapeDtypeStruct((T, D), BF16), compiler_params=_params(("parallel",)),
    )(x, g, *deps)


def _rms_bwd(x, g, dh, dres, name, deps=()):
    T, D = x.shape
    tr = _pick(T, (256, 128))

    def body(x_ref, g_ref, dh_ref, dres_ref, *rest):
        dx_ref, dxb_ref, dg_ref = rest[-3:]
        xv = x_ref[...]
        dhv = dh_ref[...]
        r = lax.rsqrt(jnp.mean(xv * xv, axis=-1, keepdims=True) + EPS)
        xhat = xv * r
        dxh = dhv * g_ref[...]
        c = jnp.mean(dxh * xhat, axis=-1, keepdims=True)
        dx = dres_ref[...] + r * (dxh - xhat * c)
        dx_ref[...] = dx
        dxb_ref[...] = dx.astype(BF16)

        @pl.when(pl.program_id(0) == 0)
        def _():
            dg_ref[...] = jnp.zeros_like(dg_ref)

        dg_ref[...] += jnp.sum(dhv * xhat, axis=0, keepdims=True)

    row = pl.BlockSpec((tr, D), lambda i: (i, 0))
    vec = pl.BlockSpec((1, D), lambda i: (0, 0))
    return pl.pallas_call(
        body, name=name, grid=(T // tr,),
        in_specs=[row, vec, row, row] + [ANY_SPEC] * len(deps), out_specs=[row, row, vec],
        out_shape=[jax.ShapeDtypeStruct((T, D), F32), jax.ShapeDtypeStruct((T, D), BF16),
                   jax.ShapeDtypeStruct((1, D), F32)],
        compiler_params=_params(("arbitrary",)),
    )(x, g, dh, dres, *deps)


N_PANEL = N_DEV // 2


def _ffn_in(h, w_in, name):
    T, D = h.shape
    F2 = w_in.shape[2]
    pw = F2 // N_PANEL
    half = pw // 2
    tm = _pick(T, (512, 256, 128))

    def body(h_ref, w_ref, u_ref, a_ref):
        r = jnp.dot(h_ref[...], w_ref[...], preferred_element_type=F32)
        u_ref[...] = r.astype(BF16)
        g, up = r[:, :half], r[:, half:]
        a_ref[...] = (g * jax.nn.sigmoid(g) * up).astype(BF16)

    return pl.pallas_call(
        body, name=name, grid=(N_PANEL, T // tm),
        in_specs=[pl.BlockSpec((tm, D), lambda p, i: (i, 0)), pl.BlockSpec((None, D, pw), lambda p, i: (0, 0, p))],
        out_specs=[pl.BlockSpec((tm, pw), lambda p, i: (i, p)), pl.BlockSpec((tm, half), lambda p, i: (i, p))],
        out_shape=[jax.ShapeDtypeStruct((T, F2), BF16), jax.ShapeDtypeStruct((T, F2 // 2), BF16)],
        compiler_params=_params(("parallel", "parallel")),
    )(h, w_in)


def _ffn_da(dxo, w_out, u, name, deps=()):
    T, D = dxo.shape
    F2 = u.shape[1]
    pw = F2 // N_PANEL
    half = pw // 2
    tm = _pick(T, (512, 256, 128))

    def body(d_ref, w_ref, u_ref, *rest):
        du_ref = rest[-1]
        da = 0.5 * lax.dot_general(d_ref[...], w_ref[...], NT_DIMS, preferred_element_type=F32)
        g = u_ref[:, :half].astype(F32)
        up = u_ref[:, half:].astype(F32)
        sg = jax.nn.sigmoid(g)
        silu = g * sg
        du_ref[:, :half] = (da * up * (sg + silu * (1.0 - sg))).astype(BF16)
        du_ref[:, half:] = (da * silu).astype(BF16)

    return pl.pallas_call(
        body, name=name, grid=(N_PANEL, T // tm),
        in_specs=[pl.BlockSpec((tm, D), lambda p, i: (i, 0)), pl.BlockSpec((None, half, D), lambda p, i: (0, p, 0)),
                  pl.BlockSpec((tm, pw), lambda p, i: (i, p))] + [ANY_SPEC] * len(deps),
        out_specs=pl.BlockSpec((tm, pw), lambda p, i: (i, p)),
        out_shape=jax.ShapeDtypeStruct((T, F2), BF16), compiler_params=_params(("parallel", "parallel")),
    )(dxo, w_out, u, *deps)


def _loss_head(y, t, name):
    T, D = y.shape
    tr = _pick(T, (512, 256, 128))

    def body(y_ref, t_ref, dy_ref, dyb_ref, l_ref):
        e = y_ref[...] - t_ref[...]
        dy = e * (1.0 / D)
        dy_ref[...] = dy
        dyb_ref[...] = dy.astype(BF16)

        @pl.when(pl.program_id(0) == 0)
        def _():
            l_ref[...] = jnp.zeros_like(l_ref)

        l_ref[...] += 0.5 * jnp.sum(jnp.mean(e * e, axis=-1, keepdims=True), axis=0, keepdims=True)

    row = pl.BlockSpec((tr, D), lambda i: (i, 0))
    return pl.pallas_call(
        body, name=name, grid=(T // tr,),
        in_specs=[row, row], out_specs=[row, row, pl.BlockSpec((1, 1), lambda i: (0, 0))],
        out_shape=[jax.ShapeDtypeStruct((T, D), F32), jax.ShapeDtypeStruct((T, D), BF16),
                   jax.ShapeDtypeStruct((1, 1), F32)],
        compiler_params=_params(("arbitrary",)),
    )(y, t)


def _rope_tables(S):
    half = ROPE_DIM // 2
    inv = 1.0 / (ROPE_THETA ** (jnp.arange(0, ROPE_DIM, 2, dtype=F32) / ROPE_DIM))
    ang = jnp.arange(S, dtype=F32)[:, None] * inv[None, :]
    cos, sin = jnp.cos(ang), jnp.sin(ang)
    z = jnp.zeros((S, half), F32)
    z2 = jnp.zeros((S, LANES - ROPE_DIM), F32)
    c = jnp.concatenate([cos, cos, z2], axis=1)
    s1 = jnp.concatenate([-sin, z, z2], axis=1)
    s2 = jnp.concatenate([z, sin, z2], axis=1)
    return c, s1, s2


def _rope(r, c, s1, s2):
    return r * c + pltpu.roll(r, LANES - ROPE_DIM // 2, 1) * s1 + pltpu.roll(r, ROPE_DIM // 2, 1) * s2


def _rope_t(d, c, s1, s2):
    return d * c + pltpu.roll(d * s1, ROPE_DIM // 2, 1) + pltpu.roll(d * s2, LANES - ROPE_DIM // 2, 1)


def _lat_norm_fwd(lat, g_cq, g_ckv, name):
    T = lat.shape[0]
    tr = _pick(T, (512, 256, 128))

    def body(lat_ref, gq_ref, gk_ref, cq_ref, ckv_ref):
        for off, g_ref, o_ref in ((0, gq_ref, cq_ref), (Q_LORA, gk_ref, ckv_ref)):
            xv = lat_ref[:, off:off + Q_LORA]
            r = lax.rsqrt(jnp.mean(xv * xv, axis=-1, keepdims=True) + EPS)
            o_ref[...] = ((xv * r) * g_ref[...]).astype(BF16)

    vec = pl.BlockSpec((1, Q_LORA), lambda i: (0, 0))
    out = pl.BlockSpec((tr, Q_LORA), lambda i: (i, 0))
    return pl.pallas_call(
        body, name=name, grid=(T // tr,),
        in_specs=[pl.BlockSpec((tr, LAT_PAD), lambda i: (i, 0)), vec, vec], out_specs=[out, out],
        out_shape=[jax.ShapeDtypeStruct((T, Q_LORA), BF16)] * 2, compiler_params=_params(("parallel",)),
    )(lat, g_cq, g_ckv)


def _lat_norm_bwd(lat, g_cq, g_ckv, dcq, dckv, dkpe, name):
    T = lat.shape[0]
    tr = _pick(T, (256, 128))

    def body(lat_ref, gq_ref, gk_ref, dcq_ref, dckv_ref, dkpe_ref, dlat_ref, dgq_ref, dgk_ref):
        @pl.when(pl.program_id(0) == 0)
        def _():
            dgq_ref[...] = jnp.zeros_like(dgq_ref)
            dgk_ref[...] = jnp.zeros_like(dgk_ref)

        for off, g_ref, d_ref, dg_ref in ((0, gq_ref, dcq_ref, dgq_ref), (Q_LORA, gk_ref, dckv_ref, dgk_ref)):
            xv = lat_ref[:, off:off + Q_LORA]
            dv = d_ref[...]
            r = lax.rsqrt(jnp.mean(xv * xv, axis=-1, keepdims=True) + EPS)
            xhat = xv * r
            dxh = dv * g_ref[...]
            c = jnp.mean(dxh * xhat, axis=-1, keepdims=True)
            dlat_ref[:, off:off + Q_LORA] = (r * (dxh - xhat * c)).astype(BF16)
            dg_ref[...] += jnp.sum(dv * xhat, axis=0, keepdims=True)
        dlat_ref[:, Q_LORA + KV_LORA:] = dkpe_ref[...].astype(BF16)

    vec = pl.BlockSpec((1, Q_LORA), lambda i: (0, 0))
    half = pl.BlockSpec((tr, Q_LORA), lambda i: (i, 0))
    full = pl.BlockSpec((tr, LAT_PAD), lambda i: (i, 0))
    return pl.pallas_call(
        body, name=name, grid=(T // tr,),
        in_specs=[full, vec, vec, half, half, pl.BlockSpec((tr, LANES), lambda i: (i, 0))],
        out_specs=[full, vec, vec],
        out_shape=[jax.ShapeDtypeStruct((T, LAT_PAD), BF16), jax.ShapeDtypeStruct((1, Q_LORA), F32),
                   jax.ShapeDtypeStruct((1, Q_LORA), F32)],
        compiler_params=_params(("arbitrary",)),
    )(lat, g_cq, g_ckv, dcq, dckv, dkpe)


def _mla_prep_fwd(q_raw, kv, lat, g_qn, g_kn, tabs, name):
    T = q_raw.shape[0]
    H = MLA_HEADS
    tr = _pick(T, (256, 128))
    scale = LOG2E / math.sqrt(QK_DIM)

    def body(q_ref, kv_ref, kpe_ref, gq_ref, gk_ref, c_ref, s1_ref, s2_ref, qf_ref, kf_ref, v_ref):
        c, s1, s2 = c_ref[...], s1_ref[...], s2_ref[...]
        gq, gk = gq_ref[...], gk_ref[...]
        kpe = kpe_ref[...]
        kpe_ss = jnp.sum(kpe * kpe, axis=-1, keepdims=True)
        for h in range(H):
            lo = h * HEAD_PAD
            qa = q_ref[:, lo:lo + LANES]
            qb = q_ref[:, lo + LANES:lo + HEAD_PAD]
            ss = jnp.sum(qa * qa + qb * qb, axis=-1, keepdims=True)
            r = lax.rsqrt(ss * (1.0 / QK_DIM) + EPS)
            qf_ref[:, lo:lo + LANES] = (qa * r * gq[:, :LANES] * scale).astype(BF16)
            qf_ref[:, lo + LANES:lo + HEAD_PAD] = (_rope(qb * r * gq[:, LANES:], c, s1, s2) * scale).astype(BF16)
            ka = kv_ref[:, lo:lo + LANES]
            ss = jnp.sum(ka * ka, axis=-1, keepdims=True) + kpe_ss
            r = lax.rsqrt(ss * (1.0 / QK_DIM) + EPS)
            kf_ref[:, lo:lo + LANES] = (ka * r * gk[:, :LANES]).astype(BF16)
            kf_ref[:, lo + LANES:lo + HEAD_PAD] = _rope(kpe * r * gk[:, LANES:], c, s1, s2).astype(BF16)
            v_ref[:, lo:lo + V_DIM] = kv_ref[:, lo + LANES:lo + HEAD_PAD].astype(BF16)
            v_ref[:, lo + V_DIM:lo + HEAD_PAD] = jnp.ones((tr, HEAD_PAD - V_DIM), BF16)

    wide = pl.BlockSpec((tr, H * HEAD_PAD), lambda i: (i, 0))
    lane = pl.BlockSpec((tr, LANES), lambda i: (i, 0))
    gvec = pl.BlockSpec((1, HEAD_PAD), lambda i: (0, 0))
    return pl.pallas_call(
        body, name=name, grid=(T // tr,),
        in_specs=[wide, wide, pl.BlockSpec((tr, LANES), lambda i: (i, (Q_LORA + KV_LORA) // LANES)), gvec, gvec,
                  lane, lane, lane],
        out_specs=[wide, wide, wide],
        out_shape=[jax.ShapeDtypeStruct((T, H * HEAD_PAD), BF16)] * 3,
        compiler_params=_params(("parallel",)),
    )(q_raw, kv, lat, g_qn, g_kn, *tabs)


def _mla_prep_bwd(q_raw, kv, lat, g_qn, g_kn, tabs, dqf, dkf, dv, name):
    T = q_raw.shape[0]
    H = MLA_HEADS
    tr = _pick(T, (128,))

    def body(q_ref, kv_ref, kpe_ref, gq_ref, gk_ref, c_ref, s1_ref, s2_ref, dqf_ref, dkf_ref, dv_ref,
             dq_ref, dkv_ref, dkpe_ref, dgq_ref, dgk_ref):
        @pl.when(pl.program_id(0) == 0)
        def _():
            dgq_ref[...] = jnp.zeros_like(dgq_ref)
            dgk_ref[...] = jnp.zeros_like(dgk_ref)

        c, s1, s2 = c_ref[...], s1_ref[...], s2_ref[...]
        gq, gk = gq_ref[...], gk_ref[...]
        kpe = kpe_ref[...]
        kpe_ss = jnp.sum(kpe * kpe, axis=-1, keepdims=True)
        dkpe = jnp.zeros_like(kpe)
        dgq_a = jnp.zeros((1, LANES), F32)
        dgq_b = jnp.zeros((1, LANES), F32)
        dgk_a = jnp.zeros((1, LANES), F32)
        dgk_b = jnp.zeros((1, LANES), F32)
        for h in range(H):
            lo = h * HEAD_PAD
            xa = q_ref[:, lo:lo + LANES]
            xb = q_ref[:, lo + LANES:lo + HEAD_PAD]
            ss = jnp.sum(xa * xa + xb * xb, axis=-1, keepdims=True)
            r = lax.rsqrt(ss * (1.0 / QK_DIM) + EPS)
            xa, xb = xa * r, xb * r
            da = dqf_ref[:, lo:lo + LANES].astype(F32)
            db = _rope_t(dqf_ref[:, lo + LANES:lo + HEAD_PAD].astype(F32), c, s1, s2)
            dgq_a += jnp.sum(da * xa, axis=0, keepdims=True)
            dgq_b += jnp.sum(db * xb, axis=0, keepdims=True)
            da, db = da * gq[:, :LANES], db * gq[:, LANES:]
            cc = jnp.sum(da * xa + db * xb, axis=-1, keepdims=True) * (1.0 / QK_DIM)
            dq_ref[:, lo:lo + LANES] = (r * (da - xa * cc)).astype(BF16)
            dq_ref[:, lo + LANES:lo + HEAD_PAD] = (r * (db - xb * cc)).astype(BF16)
            xa = kv_ref[:, lo:lo + LANES]
            ss = jnp.sum(xa * xa, axis=-1, keepdims=True) + kpe_ss
            r = lax.rsqrt(ss * (1.0 / QK_DIM) + EPS)
            xa, xb = xa * r, kpe * r
            da = dkf_ref[:, lo:lo + LANES].astype(F32)
            db = _rope_t(dkf_ref[:, lo + LANES:lo + HEAD_PAD].astype(F32), c, s1, s2)
            dgk_a += jnp.sum(da * xa, axis=0, keepdims=True)
            dgk_b += jnp.sum(db * xb, axis=0, keepdims=True)
            da, db = da * gk[:, :LANES], db * gk[:, LANES:]
            cc = jnp.sum(da * xa + db * xb, axis=-1, keepdims=True) * (1.0 / QK_DIM)
            dkv_ref[:, lo:lo + LANES] = (r * (da - xa * cc)).astype(BF16)
            dkpe = dkpe + r * (db - xb * cc)
            dkv_ref[:, lo + LANES:lo + HEAD_PAD] = dv_ref[:, h * V_DIM:(h + 1) * V_DIM].astype(BF16)
        dkpe_ref[...] = dkpe
        dgq_ref[:, :LANES] += dgq_a
        dgq_ref[:, LANES:] += dgq_b
        dgk_ref[:, :LANES] += dgk_a
        dgk_ref[:, LANES:] += dgk_b

    wide = pl.BlockSpec((tr, H * HEAD_PAD), lambda i: (i, 0))
    lane = pl.BlockSpec((tr, LANES), lambda i: (i, 0))
    gvec = pl.BlockSpec((1, HEAD_PAD), lambda i: (0, 0))
    vspec = pl.BlockSpec((tr, H * V_DIM), lambda i: (i, 0))
    return pl.pallas_call(
        body, name=name, grid=(T // tr,),
        in_specs=[wide, wide, pl.BlockSpec((tr, LANES), lambda i: (i, (Q_LORA + KV_LORA) // LANES)), gvec, gvec,
                  lane, lane, lane, wide, wide, vspec],
        out_specs=[wide, wide, lane, gvec, gvec],
        out_shape=[jax.ShapeDtypeStruct((T, H * HEAD_PAD), BF16), jax.ShapeDtypeStruct((T, H * HEAD_PAD), BF16),
                   jax.ShapeDtypeStruct((T, LANES), F32), jax.ShapeDtypeStruct((1, HEAD_PAD), F32),
                   jax.ShapeDtypeStruct((1, HEAD_PAD), F32)],
        compiler_params=_params(("arbitrary",)),
    )(q_raw, kv, lat, g_qn, g_kn, *tabs, dqf, dkf, dv)


def _causal_mask(tq, tk):
    return lax.broadcasted_iota(jnp.int32, (tq, tk), 1) <= lax.broadcasted_iota(jnp.int32, (tq, tk), 0)


NT_DIMS = (((1,), (1,)), ((), ()))
TN_DIMS = (((0,), (0,)), ((), ()))


def _flash_fwd(qf, kf, v, name):
    T = qf.shape[0]
    H, G = MLA_HEADS, FLASH_HEADS_FWD
    t = _pick(T, (512, 256, 128))
    n = T // t
    pairs = [(i, j) for i in range(n) for j in range(i + 1)]
    qi = jnp.asarray([p[0] for p in pairs], jnp.int32)
    kj = jnp.asarray([p[1] for p in pairs], jnp.int32)

    def body(qi_ref, kj_ref, q_ref, k_ref, v_ref, o_ref, lse_ref, *scratch):
        m_sc, acc_sc = scratch[:G], scratch[G:]
        sid = pl.program_id(1)
        i, j = qi_ref[sid], kj_ref[sid]

        @pl.when(j == 0)
        def _():
            for g in range(G):
                m_sc[g][...] = jnp.full_like(m_sc[g], NEG_BIG)
                acc_sc[g][...] = jnp.zeros_like(acc_sc[g])

        def step(masked):
            for g in range(G):
                qk = slice(g * HEAD_PAD, (g + 1) * HEAD_PAD)
                s = lax.dot_general(q_ref[:, qk], k_ref[:, qk], NT_DIMS, preferred_element_type=F32)
                if masked:
                    s = jnp.where(_causal_mask(t, t), s, NEG_BIG)
                m_prev = m_sc[g][:, :1]
                m_new = jnp.maximum(m_prev, jnp.max(s, axis=-1, keepdims=True))
                a = jnp.exp2(m_prev - m_new)
                p = jnp.exp2((s - m_new).astype(BF16))
                acc_sc[g][...] = a * acc_sc[g][...] + jnp.dot(p, v_ref[:, qk], preferred_element_type=F32)
                m_sc[g][...] = jnp.broadcast_to(m_new, (t, LANES))

        @pl.when(j < i)
        def _():
            step(False)

        @pl.when(j == i)
        def _():
            step(True)
            for g in range(G):
                vo = slice(g * V_DIM, (g + 1) * V_DIM)
                l = acc_sc[g][:, V_DIM:]
                o_ref[:, vo] = (acc_sc[g][:, :V_DIM] / l).astype(BF16)
                lse_ref[:, vo] = m_sc[g][...] + jnp.log2(l)

    row = pl.BlockSpec((t, G * V_DIM), lambda h, s, qi, kj: (qi[s], h))
    return pl.pallas_call(
        body, name=name,
        grid_spec=pltpu.PrefetchScalarGridSpec(
            num_scalar_prefetch=2, grid=(H // G, len(pairs)),
            in_specs=[pl.BlockSpec((t, G * HEAD_PAD), lambda h, s, qi, kj: (qi[s], h)),
                      pl.BlockSpec((t, G * HEAD_PAD), lambda h, s, qi, kj: (kj[s], h)),
                      pl.BlockSpec((t, G * HEAD_PAD), lambda h, s, qi, kj: (kj[s], h))],
            out_specs=[row, row],
            scratch_shapes=[pltpu.VMEM((t, LANES), F32)] * G + [pltpu.VMEM((t, HEAD_PAD), F32)] * G),
        out_shape=[jax.ShapeDtypeStruct((T, H * V_DIM), BF16), jax.ShapeDtypeStruct((T, H * V_DIM), F32)],
        compiler_params=_params(("parallel", "arbitrary")),
    )(qi, kj, qf, kf, v)


def _attn_delta(do, o, name):
    T, W = do.shape
    nh = W // V_DIM
    tr = _pick(T, (512, 256, 128))

    def body(do_ref, o_ref, d_ref):
        for h in range(nh):
            sl = slice(h * V_DIM, (h + 1) * V_DIM)
            d = jnp.sum(do_ref[:, sl].astype(F32) * o_ref[:, sl].astype(F32), axis=-1, keepdims=True)
            d_ref[:, sl] = jnp.broadcast_to(d, (tr, V_DIM))

    row = pl.BlockSpec((tr, W), lambda i: (i, 0))
    return pl.pallas_call(
        body, name=name, grid=(T // tr,), in_specs=[row, row], out_specs=row,
        out_shape=jax.ShapeDtypeStruct((T, W), F32), compiler_params=_params(("parallel",)),
    )(do, o)


def _flash_bwd(qf, kf, v, do, lse, delta, name):
    T = qf.shape[0]
    H, G = MLA_HEADS, FLASH_HEADS
    t = _pick(T, (512, 256, 128))
    n = T // t
    scale = 1.0 / math.sqrt(QK_DIM)
    pairs = [(i, j) for j in range(n) for i in range(j, n)]
    qi = jnp.asarray([p[0] for p in pairs], jnp.int32)
    kj = jnp.asarray([p[1] for p in pairs], jnp.int32)

    def body(qi_ref, kj_ref, q_ref, k_ref, v_ref, do_ref, lse_ref, dl_ref, dq_ref, dk_ref, dv_ref,
             dq_acc, dk_acc, dv_acc):
        sid = pl.program_id(1)
        i, j = qi_ref[sid], kj_ref[sid]

        @pl.when(sid == 0)
        def _():
            dq_acc[...] = jnp.zeros_like(dq_acc)

        def step(masked):
            rows = pl.ds(pl.multiple_of(i * t, t), t)
            for g in range(G):
                qk = slice(g * HEAD_PAD, (g + 1) * HEAD_PAD)
                vo = slice(g * V_DIM, (g + 1) * V_DIM)
                q, k, do_ = q_ref[:, qk], k_ref[:, qk], do_ref[:, vo]
                v_ = v_ref[:, g * HEAD_PAD:g * HEAD_PAD + V_DIM]
                s = lax.dot_general(q, k, NT_DIMS, preferred_element_type=F32)
                if masked:
                    s = jnp.where(_causal_mask(t, t), s, NEG_BIG)
                p = jnp.exp2(s - lse_ref[:, g * V_DIM:g * V_DIM + 1])
                dp = lax.dot_general(do_, v_, NT_DIMS, preferred_element_type=F32)
                ds = (p * (dp - dl_ref[:, g * V_DIM:g * V_DIM + 1])).astype(BF16)
                dv = lax.dot_general(p.astype(BF16), do_, TN_DIMS, preferred_element_type=F32)
                dk = lax.dot_general(ds, q, TN_DIMS, preferred_element_type=F32)
                if masked:
                    dv_acc[:, vo] = dv
                    dk_acc[:, qk] = dk
                else:
                    dv_acc[:, vo] += dv
                    dk_acc[:, qk] += dk
                dq_acc[rows, qk] += jnp.dot(ds, k, preferred_element_type=F32) * scale

        @pl.when(i == j)
        def _():
            step(True)

        @pl.when(i > j)
        def _():
            step(False)

        @pl.when(i == n - 1)
        def _():
            dk_ref[...] = (dk_acc[...] * LN2).astype(BF16)
            dv_ref[...] = dv_acc[...].astype(BF16)

        @pl.when(sid == len(pairs) - 1)
        def _():
            dq_ref[...] = dq_acc[...].astype(BF16)

    qs = pl.BlockSpec((t, G * HEAD_PAD), lambda h, s, qi, kj: (qi[s], h))
    rs = pl.BlockSpec((t, G * V_DIM), lambda h, s, qi, kj: (qi[s], h))
    ks = pl.BlockSpec((t, G * HEAD_PAD), lambda h, s, qi, kj: (kj[s], h))
    vs = pl.BlockSpec((t, G * V_DIM), lambda h, s, qi, kj: (kj[s], h))
    return pl.pallas_call(
        body, name=name,
        grid_spec=pltpu.PrefetchScalarGridSpec(
            num_scalar_prefetch=2, grid=(H // G, len(pairs)), in_specs=[qs, ks, ks, rs, rs, rs],
            out_specs=[pl.BlockSpec((T, G * HEAD_PAD), lambda h, s, qi, kj: (0, h)), ks, vs],
            scratch_shapes=[pltpu.VMEM((T, G * HEAD_PAD), F32), pltpu.VMEM((t, G * HEAD_PAD), F32),
                            pltpu.VMEM((t, G * V_DIM), F32)]),
        out_shape=[jax.ShapeDtypeStruct((T, H * HEAD_PAD), BF16), jax.ShapeDtypeStruct((T, H * HEAD_PAD), BF16),
                   jax.ShapeDtypeStruct((T, H * V_DIM), BF16)],
        compiler_params=_params(("parallel", "arbitrary")),
    )(qi, kj, qf, kf, v, do, lse, delta)


def _alibi_slopes():
    tot = DIL_GROUPS * DIL_HEADS
    return [float(np.float32(2.0) ** (np.float32(-8.0) * np.float32(k) / np.float32(tot))) for k in range(1, tot + 1)]


def _dil_masks():
    iq = lax.broadcasted_iota(jnp.int32, (DIL_BLK, DIL_BLK), 0)
    ik = lax.broadcasted_iota(jnp.int32, (DIL_BLK, DIL_BLK), 1)
    return (ik >= iq), (iq + DIL_BLK - ik).astype(F32), (ik <= iq), (iq - ik).astype(F32)


def _dil_norm(x, g):
    r = lax.rsqrt(jnp.mean(x * x, axis=-1, keepdims=True) + EPS)
    return x * r, r


DIL_SUPER = 8
BNT_DIMS = (((2,), (2,)), ((0,), (0,)))
BNN_DIMS = (((2,), (1,)), ((0,), (0,)))
BTN_DIMS = (((1,), (1,)), ((0,), (0,)))


def _dil_chunk(it, nb, d):
    assert nb & (nb - 1) == 0, nb
    r, n = it >> (nb.bit_length() - 1), it & (nb - 1)
    if d > 1:
        tok = pl.ds(n * (d * DIL_BLK) + r, DIL_BLK, stride=d)
    else:
        tok = pl.ds(pl.multiple_of(it * DIL_BLK, DIL_BLK), DIL_BLK)
    return tok, pl.ds(pl.multiple_of((it + 1) * DIL_BLK, DIL_BLK), DIL_BLK)


def _dil_token_rows(bidx, nb, d):
    r, n = divmod(bidx, nb)
    return pl.ds(n * DIL_BLK * d + r, DIL_BLK, stride=d) if d > 1 else pl.ds(bidx * DIL_BLK, DIL_BLK)


def _dil_super_rows(ss):
    base = (1 + ss * DIL_SUPER) * DIL_BLK
    return pl.ds(base, DIL_SUPER * DIL_BLK), pl.ds(base - DIL_BLK, DIL_SUPER * DIL_BLK)


def _dil_b3(x):
    return x.reshape(DIL_SUPER, DIL_BLK, x.shape[-1])


def _dil_scores(q3, kc3, kp3, slope, d, ss, nb):
    ok_p, dist_p, ok_c, dist_c = _dil_masks()
    scale = 1.0 / math.sqrt(DIL_HEAD_DIM)
    bias_p = jnp.where(ok_p, -slope * d * dist_p, NEG_BIG)
    bias_c = jnp.where(ok_c, -slope * d * dist_c, NEG_BIG)
    s_c = lax.dot_general(q3, kc3, BNT_DIMS, preferred_element_type=F32) * scale + bias_c[None]
    s_p = lax.dot_general(q3, kp3, BNT_DIMS, preferred_element_type=F32) * scale + bias_p[None]
    bidx = ss * DIL_SUPER + lax.broadcasted_iota(jnp.int32, s_p.shape, 0)
    s_p = jnp.where((bidx & (nb - 1)) == 0, NEG_BIG, s_p)
    return s_c, s_p


def _dil_fwd(qkv, g_qn, g_kn, slopes, name):
    T = qkv.shape[0]
    GH = DIL_GROUPS * DIL_HEADS
    scale = 1.0 / math.sqrt(DIL_HEAD_DIM)

    def body(sl_ref, q_ref, k_ref, v_ref, gq_ref, gk_ref, o_ref, lse_ref, qn_pm, kn_pm, v_pm):
        gh = pl.program_id(0)
        slope = sl_ref[gh]
        gq, gk = gq_ref[...], gk_ref[...]
        pad = pl.ds(0, DIL_BLK)
        kn_pm[pad, :] = jnp.zeros((DIL_BLK, DIL_HEAD_DIM), BF16)
        v_pm[pad, :] = jnp.zeros((DIL_BLK, DIL_HEAD_DIM), BF16)
        for g, (_, d) in enumerate(DIL_PAIRS):
            @pl.when((gh >= g * DIL_HEADS) & (gh < (g + 1) * DIL_HEADS))
            def _(d=d):
                nb = T // (d * DIL_BLK)

                def fill(it, _):
                    tok, dst = _dil_chunk(it, nb, d)
                    qn_pm[dst, :] = (_dil_norm(q_ref[tok, :], gq)[0] * gq).astype(BF16)
                    kn_pm[dst, :] = (_dil_norm(k_ref[tok, :], gk)[0] * gk).astype(BF16)
                    v_pm[dst, :] = v_ref[tok, :].astype(BF16)
                    return 0
                lax.fori_loop(0, T // DIL_BLK, fill, 0, unroll=4)

                for ss in range(T // DIL_BLK // DIL_SUPER):
                    cur, prv = _dil_super_rows(ss)
                    q3, kc3, kp3 = _dil_b3(qn_pm[cur, :]), _dil_b3(kn_pm[cur, :]), _dil_b3(kn_pm[prv, :])
                    s_c, s_p = _dil_scores(q3, kc3, kp3, slope, d, ss, nb)
                    m = jnp.max(jnp.maximum(s_c, s_p), axis=-1, keepdims=True)
                    p_c = jnp.exp(s_c - m)
                    p_p = jnp.exp(s_p - m)
                    l = jnp.sum(p_c, axis=-1, keepdims=True) + jnp.sum(p_p, axis=-1, keepdims=True)
                    acc = lax.dot_general(p_c.astype(BF16), _dil_b3(v_pm[cur, :]), BNN_DIMS, preferred_element_type=F32)
                    acc += lax.dot_general(p_p.astype(BF16), _dil_b3(v_pm[prv, :]), BNN_DIMS, preferred_element_type=F32)
                    o3 = acc / l
                    lse3 = jnp.broadcast_to(m + jnp.log(l), o3.shape)
                    for b in range(DIL_SUPER):
                        tok = _dil_token_rows(ss * DIL_SUPER + b, nb, d)
                        o_ref[tok, :] = o3[b]
                        lse_ref[tok, :] = lse3[b]

    col = lambda off: pl.BlockSpec((T, DIL_HEAD_DIM), lambda gh, sl: (0, gh + off))
    gvec = pl.BlockSpec((1, DIL_HEAD_DIM), lambda gh, sl: (0, 0))
    return pl.pallas_call(
        body, name=name,
        grid_spec=pltpu.PrefetchScalarGridSpec(
            num_scalar_prefetch=1, grid=(GH,),
            in_specs=[col(0), col(GH), col(2 * GH), gvec, gvec], out_specs=[col(0), col(0)],
            scratch_shapes=[pltpu.VMEM((DIL_BLK + T, DIL_HEAD_DIM), BF16)] * 3),
        out_shape=[jax.ShapeDtypeStruct((T, GH * DIL_HEAD_DIM), F32)] * 2,
        compiler_params=_params(("parallel",)),
    )(slopes, qkv, qkv, qkv, g_qn, g_kn)


def _dil_merge(o_g, lse_g, name):
    T = o_g.shape[0]
    W = DIL_HEADS * DIL_HEAD_DIM
    tr = _pick(T, (256, 128))

    def body(o0, o1, o2, l0, l1, l2, o_ref, lse_ref):
        a, b, c = l0[...], l1[...], l2[...]
        m = jnp.maximum(jnp.maximum(a, b), c)
        ea, eb, ec = jnp.exp(a - m), jnp.exp(b - m), jnp.exp(c - m)
        tot = ea + eb + ec
        o_ref[...] = ((o0[...] * ea + o1[...] * eb + o2[...] * ec) / tot).astype(BF16)
        lse_ref[...] = m + jnp.log(tot)

    grp = lambda g: pl.BlockSpec((tr, W), lambda i: (i, g))
    out = pl.BlockSpec((tr, W), lambda i: (i, 0))
    return pl.pallas_call(
        body, name=name, grid=(T // tr,),
        in_specs=[grp(0), grp(1), grp(2), grp(0), grp(1), grp(2)], out_specs=[out, out],
        out_shape=[jax.ShapeDtypeStruct((T, W), BF16), jax.ShapeDtypeStruct((T, W), F32)],
        compiler_params=_params(("parallel",)),
    )(o_g, o_g, o_g, lse_g, lse_g, lse_g)


def _dil_bwd(qkv, g_qn, g_kn, slopes, do, delta, lse, name):
    T = qkv.shape[0]
    GH = DIL_GROUPS * DIL_HEADS
    scale = 1.0 / math.sqrt(DIL_HEAD_DIM)
    nchunk = T // DIL_BLK

    def body(sl_ref, q_ref, k_ref, v_ref, gq_ref, gk_ref, do_ref, dl_ref, lse_ref,
             dq_ref, dk_ref, dv_ref, dgq_ref, dgk_ref,
             qn_pm, kn_pm, v_pm, do_pm, lse_pm, dl_pm, dq_pm, dk_pm, dv_pm, tok_sc):
        gh = pl.program_id(0)
        slope = sl_ref[gh]
        gq, gk = gq_ref[...], gk_ref[...]

        @pl.when(gh == 0)
        def _():
            dgq_ref[...] = jnp.zeros_like(dgq_ref)
            dgk_ref[...] = jnp.zeros_like(dgk_ref)

        pad = pl.ds(0, DIL_BLK)
        kn_pm[pad, :] = jnp.zeros((DIL_BLK, DIL_HEAD_DIM), BF16)
        v_pm[pad, :] = jnp.zeros((DIL_BLK, DIL_HEAD_DIM), BF16)
        dk_pm[...] = jnp.zeros_like(dk_pm)
        dv_pm[...] = jnp.zeros_like(dv_pm)
        for g, (_, d) in enumerate(DIL_PAIRS):
            @pl.when((gh >= g * DIL_HEADS) & (gh < (g + 1) * DIL_HEADS))
            def _(d=d):
                nb = T // (d * DIL_BLK)

                def fill(it, _):
                    tok, dst = _dil_chunk(it, nb, d)
                    qn_pm[dst, :] = (_dil_norm(q_ref[tok, :], gq)[0] * gq).astype(BF16)
                    kn_pm[dst, :] = (_dil_norm(k_ref[tok, :], gk)[0] * gk).astype(BF16)
                    v_pm[dst, :] = v_ref[tok, :].astype(BF16)
                    do_pm[dst, :] = do_ref[tok, :].astype(BF16)
                    lse_pm[dst, :] = lse_ref[tok, :]
                    dl_pm[dst, :] = dl_ref[tok, :]
                    return 0
                lax.fori_loop(0, nchunk, fill, 0, unroll=4)

                for ss in range(nchunk // DIL_SUPER):
                    cur, prv = _dil_super_rows(ss)
                    q3, kc3, kp3 = _dil_b3(qn_pm[cur, :]), _dil_b3(kn_pm[cur, :]), _dil_b3(kn_pm[prv, :])
                    vc3, vp3, do3 = _dil_b3(v_pm[cur, :]), _dil_b3(v_pm[prv, :]), _dil_b3(do_pm[cur, :])
                    ls = _dil_b3(lse_pm[cur, :])[:, :, :1]
                    delta = _dil_b3(dl_pm[cur, :])[:, :, :1]
                    s_c, s_p = _dil_scores(q3, kc3, kp3, slope, d, ss, nb)
                    p_c = jnp.exp(s_c - ls)
                    p_p = jnp.exp(s_p - ls)
                    dp_c = lax.dot_general(do3, vc3, BNT_DIMS, preferred_element_type=F32)
                    dp_p = lax.dot_general(do3, vp3, BNT_DIMS, preferred_element_type=F32)
                    ds_c = (p_c * (dp_c - delta)).astype(BF16)
                    ds_p = (p_p * (dp_p - delta)).astype(BF16)
                    dq3 = (lax.dot_general(ds_c, kc3, BNN_DIMS, preferred_element_type=F32)
                           + lax.dot_general(ds_p, kp3, BNN_DIMS, preferred_element_type=F32)) * scale
                    flat = lambda x: x.reshape(DIL_SUPER * DIL_BLK, DIL_HEAD_DIM)
                    dq_pm[cur, :] = flat(dq3)
                    dk_pm[cur, :] += flat(lax.dot_general(ds_c, q3, BTN_DIMS, preferred_element_type=F32)) * scale
                    dv_pm[cur, :] += flat(lax.dot_general(p_c.astype(BF16), do3, BTN_DIMS, preferred_element_type=F32))
                    dk_pm[prv, :] += flat(lax.dot_general(ds_p, q3, BTN_DIMS, preferred_element_type=F32)) * scale
                    dv_pm[prv, :] += flat(lax.dot_general(p_p.astype(BF16), do3, BTN_DIMS, preferred_element_type=F32))

                def to_tokens(src_pm):
                    def move(it, _):
                        tok, src = _dil_chunk(it, nb, d)
                        tok_sc[tok, :] = src_pm[src, :]
                        return 0
                    lax.fori_loop(0, nchunk, move, 0, unroll=4)

                def norm_bwd(x_ref, gvec, out_ref):
                    big = 4 * DIL_BLK

                    def fin(ci, dg):
                        rows = pl.ds(pl.multiple_of(ci * big, big), big)
                        xhat, r = _dil_norm(x_ref[rows, :], gvec)
                        dn = tok_sc[rows, :]
                        dxh = dn * gvec
                        c = jnp.mean(dxh * xhat, axis=-1, keepdims=True)
                        out_ref[rows, :] = (r * (dxh - xhat * c)).astype(BF16)
                        return dg + jnp.sum(dn * xhat, axis=0, keepdims=True)
                    return lax.fori_loop(0, T // big, fin, jnp.zeros((1, DIL_HEAD_DIM), F32))

                to_tokens(dq_pm)
                dgq_ref[...] += norm_bwd(q_ref, gq, dq_ref)
                to_tokens(dk_pm)
                dgk_ref[...] += norm_bwd(k_ref, gk, dk_ref)
                to_tokens(dv_pm)
                dv_ref[...] = tok_sc[...].astype(BF16)

    col = lambda off: pl.BlockSpec((T, DIL_HEAD_DIM), lambda gh, sl: (0, gh + off))
    hcol = pl.BlockSpec((T, DIL_HEAD_DIM), lambda gh, sl: (0, gh % DIL_HEADS))
    gvec = pl.BlockSpec((1, DIL_HEAD_DIM), lambda gh, sl: (0, 0))
    wide = jax.ShapeDtypeStruct((T, GH * DIL_HEAD_DIM), BF16)
    vec = jax.ShapeDtypeStruct((1, DIL_HEAD_DIM), F32)
    pm = lambda dt: pltpu.VMEM((DIL_BLK + T, DIL_HEAD_DIM), dt)
    return pl.pallas_call(
        body, name=name,
        grid_spec=pltpu.PrefetchScalarGridSpec(
            num_scalar_prefetch=1, grid=(GH,),
            in_specs=[col(0), col(GH), col(2 * GH), gvec, gvec, hcol, hcol, hcol],
            out_specs=[col(0), col(0), col(0), gvec, gvec],
            scratch_shapes=[pm(BF16)] * 4 + [pm(F32)] * 5 + [pltpu.VMEM((T, DIL_HEAD_DIM), F32)]),
        out_shape=[wide, wide, wide, vec, vec],
        compiler_params=_params(("arbitrary",)),
    )(slopes, qkv, qkv, qkv, g_qn, g_kn, do, delta, lse)


def _my_pos():
    return lax.axis_index("x"), lax.axis_index("y"), lax.axis_index("c")


def _peer(pos, j):
    x, y, c = pos
    px = 1 - x if j & 4 else x
    py = 1 - y if j & 2 else y
    pc = 1 - c if j & 1 else c
    return (px, py, pc), 4 * px + 2 * py + pc


def _slot(idx, paired):
    if not paired:
        return idx
    return jnp.where(idx < N_DEV // 2, 2 * idx, 2 * idx - (N_DEV - 1))


def _shard_slice(ref, axis, idx, size, paired=False):
    sl = [slice(None)] * len(ref.shape)
    sl[axis] = pl.ds(pl.multiple_of(_slot(idx, paired) * size, 8), size)
    return ref.at[tuple(sl)]


HBM_SPEC = pl.BlockSpec(memory_space=pltpu.HBM)
SEM_SPEC = pl.BlockSpec(memory_space=pltpu.SEMAPHORE)
DATAFLOW = pltpu.SideEffectType.DATAFLOW_SIDE_EFFECTING
N_PEER = N_DEV - 1


def _scatter_copy(axis, grad, slots, frm, to, dev, send_sem, recv_sem):
    ax, paired = axis
    src = _shard_slice(grad, ax, to, grad.shape[ax] // N_DEV, paired)
    return pltpu.make_async_remote_copy(src_ref=src, dst_ref=slots.at[frm], send_sem=send_sem, recv_sem=recv_sem,
                                        device_id=dev, device_id_type=MESH)


def _scatter_start(grads, axes, name):
    n = len(grads)

    def body(*refs):
        outs = refs[2 * n:]
        send, recv, token = outs[:n], outs[n:2 * n], outs[4 * n]
        pos = _my_pos()
        me = 4 * pos[0] + 2 * pos[1] + pos[2]
        for a in range(n):
            for j in range(1, N_DEV):
                dev, pid = _peer(pos, j)
                _scatter_copy(axes[a], refs[2 * a], refs[2 * a + 1], me, pid, dev, send[a].at[j - 1],
                              recv[a].at[j - 1]).start()
        token[...] = jnp.zeros_like(token)

    ops = []
    for g, (ax, _) in zip(grads, axes):
        shp = list(g.shape)
        shp[ax] //= N_DEV
        ops += [g, lax.empty((N_DEV,) + tuple(shp), g.dtype)]
    sems = [pltpu.SemaphoreType.DMA((N_PEER,))] * (2 * n)
    res = pl.pallas_call(
        body, name=name,
        out_shape=sems + [pltpu.HBM(o.shape, o.dtype) for o in ops] + [jax.ShapeDtypeStruct((8, LANES), F32)],
        in_specs=[HBM_SPEC] * len(ops),
        out_specs=[SEM_SPEC] * (2 * n) + [HBM_SPEC] * len(ops) + [pl.BlockSpec(memory_space=pltpu.VMEM)],
        input_output_aliases={i: 2 * n + i for i in range(len(ops))},
        compiler_params=pltpu.CompilerParams(has_side_effects=DATAFLOW),
    )(*[pltpu.with_memory_space_constraint(o, pltpu.HBM) for o in ops])
    items = [(res[a], res[n + a], res[2 * n + 2 * a], res[2 * n + 2 * a + 1]) for a in range(n)]
    return items, res[4 * n]


def _scatter_wait(items, axes, after, name):
    n = len(items)

    def body(*refs):
        send, recv = refs[2 * n:3 * n], refs[3 * n:4 * n]
        pos = _my_pos()
        me = 4 * pos[0] + 2 * pos[1] + pos[2]
        for a in range(n):
            for j in range(1, N_DEV):
                dev, pid = _peer(pos, j)
                cp = _scatter_copy(axes[a], refs[2 * a], refs[2 * a + 1], pid, me, dev, send[a].at[j - 1],
                                   recv[a].at[j - 1])
                cp.wait_send()
                cp.wait_recv()

    ops = [b for it in items for b in it[2:]]
    res = pl.pallas_call(
        body, name=name,
        out_shape=[pltpu.HBM(o.shape, o.dtype) for o in ops],
        in_specs=[HBM_SPEC] * len(ops) + [SEM_SPEC] * (2 * n) + [ANY_SPEC],
        out_specs=[HBM_SPEC] * len(ops),
        input_output_aliases={i: i for i in range(len(ops))},
        compiler_params=pltpu.CompilerParams(has_side_effects=DATAFLOW),
    )(*ops, *[it[0] for it in items], *[it[1] for it in items], after)
    return [(res[2 * a], res[2 * a + 1]) for a in range(n)]


SIBLING = 1
ICI_PEERS = (2, 4, 6)


def _gather_copy(buf, axis, shard, dev, send_sem, recv_sem):
    ax, paired = axis
    piece = _shard_slice(buf, ax, shard, buf.shape[ax] // N_DEV, paired)
    return pltpu.make_async_remote_copy(src_ref=piece, dst_ref=piece, send_sem=send_sem, recv_sem=recv_sem,
                                        device_id=dev, device_id_type=MESH)


def _gather_start(bufs, axes, name):
    n = len(bufs)

    def body(*refs):
        ins, outs = refs[:n], refs[n:]
        send, r_sib, r_ici, token = outs[:n], outs[n:2 * n], outs[2 * n:3 * n], outs[4 * n]
        pos = _my_pos()
        me = 4 * pos[0] + 2 * pos[1] + pos[2]
        for a in range(n):
            dev, _ = _peer(pos, SIBLING)
            _gather_copy(ins[a], axes[a], me, dev, send[a].at[0], r_sib[a].at[0]).start()
            for k, j in enumerate(ICI_PEERS):
                dev, _ = _peer(pos, j)
                _gather_copy(ins[a], axes[a], me, dev, send[a].at[1 + k], r_ici[a].at[k]).start()
        token[...] = jnp.zeros_like(token)

    sems = ([pltpu.SemaphoreType.DMA((1 + len(ICI_PEERS),))] * n + [pltpu.SemaphoreType.DMA((1,))] * n
            + [pltpu.SemaphoreType.DMA((len(ICI_PEERS),))] * n)
    res = pl.pallas_call(
        body, name=name,
        out_shape=sems + [pltpu.HBM(b.shape, b.dtype) for b in bufs] + [jax.ShapeDtypeStruct((8, LANES), F32)],
        in_specs=[HBM_SPEC] * n,
        out_specs=[SEM_SPEC] * (3 * n) + [HBM_SPEC] * n + [pl.BlockSpec(memory_space=pltpu.VMEM)],
        input_output_aliases={i: 3 * n + i for i in range(n)},
        compiler_params=pltpu.CompilerParams(has_side_effects=DATAFLOW),
    )(*[pltpu.with_memory_space_constraint(b, pltpu.HBM) for b in bufs])
    items = [dict(send=res[a], r_sib=res[n + a], r_ici=res[2 * n + a], buf=res[3 * n + a]) for a in range(n)]
    return items, res[4 * n]


def _gather_relay(items, axes, after, name):
    n = len(items)

    def body(*refs):
        ins, r_ici = refs[:n], refs[n:2 * n]
        outs = refs[2 * n + 1:]
        s_rel, r_rel, token = outs[:n], outs[n:2 * n], outs[3 * n]
        pos = _my_pos()
        sib, _ = _peer(pos, SIBLING)
        for a in range(n):
            for k, j in enumerate(ICI_PEERS):
                dev, pid = _peer(pos, j)
                _gather_copy(ins[a], axes[a], pid, dev, s_rel[a].at[k], r_ici[a].at[k]).wait_recv()
                _gather_copy(ins[a], axes[a], pid, sib, s_rel[a].at[k], r_rel[a].at[k]).start()
        token[...] = jnp.zeros_like(token)

    bufs = [it["buf"] for it in items]
    sems = [pltpu.SemaphoreType.DMA((len(ICI_PEERS),))] * (2 * n)
    res = pl.pallas_call(
        body, name=name,
        out_shape=sems + [pltpu.HBM(b.shape, b.dtype) for b in bufs] + [jax.ShapeDtypeStruct((8, LANES), F32)],
        in_specs=[HBM_SPEC] * n + [SEM_SPEC] * n + [ANY_SPEC],
        out_specs=[SEM_SPEC] * (2 * n) + [HBM_SPEC] * n + [pl.BlockSpec(memory_space=pltpu.VMEM)],
        input_output_aliases={i: 2 * n + i for i in range(n)},
        compiler_params=pltpu.CompilerParams(has_side_effects=DATAFLOW),
    )(*bufs, *[it["r_ici"] for it in items], after)
    out = [dict(send=it["send"], r_sib=it["r_sib"], s_rel=res[a], r_rel=res[n + a], buf=res[2 * n + a])
           for a, it in enumerate(items)]
    return out, res[3 * n]


def _gather_wait(items, axes, after, name):
    n = len(items)

    def body(*refs):
        ins = refs[:n]
        send, r_sib, s_rel, r_rel = (refs[(1 + q) * n:(2 + q) * n] for q in range(4))
        pos = _my_pos()
        me = 4 * pos[0] + 2 * pos[1] + pos[2]
        sib, sib_id = _peer(pos, SIBLING)
        for a in range(n):
            for k in range(1 + len(ICI_PEERS)):
                _gather_copy(ins[a], axes[a], me, sib, send[a].at[k], r_sib[a].at[0]).wait_send()
            _gather_copy(ins[a], axes[a], sib_id, sib, send[a].at[0], r_sib[a].at[0]).wait_recv()
            for k, j in enumerate(ICI_PEERS):
                _, pid = _peer(pos, j)
                _, far = _peer(pos, j ^ SIBLING)
                _gather_copy(ins[a], axes[a], pid, sib, s_rel[a].at[k], r_rel[a].at[k]).wait_send()
                _gather_copy(ins[a], axes[a], far, sib, s_rel[a].at[k], r_rel[a].at[k]).wait_recv()

    bufs = [it["buf"] for it in items]
    res = pl.pallas_call(
        body, name=name,
        out_shape=[pltpu.HBM(b.shape, b.dtype) for b in bufs],
        in_specs=[HBM_SPEC] * n + [SEM_SPEC] * (4 * n) + [ANY_SPEC],
        out_specs=[HBM_SPEC] * n,
        input_output_aliases={i: i for i in range(n)},
        compiler_params=pltpu.CompilerParams(has_side_effects=DATAFLOW),
    )(*bufs, *[it["send"] for it in items], *[it["r_sib"] for it in items], *[it["s_rel"] for it in items],
      *[it["r_rel"] for it in items], after)
    return list(res)


def _gain_allreduce(v, name):
    n = v.shape[1]

    def body(v_ref, o_ref, slots, send_sems, recv_sems):
        pos = _my_pos()
        me = 4 * pos[0] + 2 * pos[1] + pos[2]
        slots[me] = v_ref[...]
        copies = []
        for j in range(1, N_DEV):
            dev, _ = _peer(pos, j)
            cp = pltpu.make_async_remote_copy(
                src_ref=slots.at[me], dst_ref=slots.at[me], send_sem=send_sems.at[j], recv_sem=recv_sems.at[j],
                device_id=dev, device_id_type=MESH)
            cp.start()
            copies.append(cp)
        for j in range(1, N_DEV):
            dev, pid = _peer(pos, j)
            pltpu.make_async_remote_copy(
                src_ref=slots.at[me], dst_ref=slots.at[pid], send_sem=send_sems.at[j], recv_sem=recv_sems.at[j],
                device_id=dev, device_id_type=MESH).wait_recv()
        for cp in copies:
            cp.wait_send()
        acc = slots[0]
        for s in range(1, N_DEV):
            acc = acc + slots[s]
        o_ref[...] = acc

    return pl.pallas_call(
        body, name=name, out_shape=jax.ShapeDtypeStruct((1, n), F32),
        in_specs=[pl.BlockSpec(memory_space=pltpu.VMEM)], out_specs=pl.BlockSpec(memory_space=pltpu.VMEM),
        scratch_shapes=[pltpu.VMEM((N_DEV, 1, n), F32), pltpu.SemaphoreType.DMA((N_DEV,)),
                        pltpu.SemaphoreType.DMA((N_DEV,))],
        compiler_params=pltpu.CompilerParams(has_side_effects=True),
    )(v)


def _adamw(parts, own, me, w, m, v, layer, prev, name, own_axis=None):
    L, R, C = w.shape
    P = parts.shape[0]
    tr = _pick(R, (128, 64, 32, 16, 8, 1))
    c1 = 1.0 - ADAM_B1 ** ADAM_STEP
    c2 = 1.0 - ADAM_B2 ** ADAM_STEP
    n_in = 4 if own is None else 5

    def body(me_ref, *refs):
        p_ref = refs[0]
        w_ref, m_ref, v_ref = refs[n_in - 3:n_in]
        g_out, d_out, m_out, v_out, tok = refs[-5:]
        g = None
        for s in range(P):
            part = p_ref[s]
            if own is not None:
                part = jnp.where(me_ref[0] == s, refs[1][...], part)
            g = part.astype(F32) if g is None else g + part.astype(F32)
        mn = ADAM_B1 * m_ref[...] + (1.0 - ADAM_B1) * g
        vn = ADAM_B2 * v_ref[...] + (1.0 - ADAM_B2) * (g * g)
        g_out[...] = g
        m_out[...] = mn
        v_out[...] = vn
        d_out[...] = -ADAM_LR * ((mn / c1) / (jnp.sqrt(vn / c2) + ADAM_EPS) + ADAM_WD * w_ref[...])
        tok[...] = jnp.zeros_like(tok)

    row = pl.BlockSpec((None, tr, C), lambda i, me_ref: (layer, i, 0))
    in_specs = [pl.BlockSpec((P, tr, C), lambda i, me_ref: (0, i, 0))]
    args = [parts]
    if own is not None:
        if own_axis is None:
            own_idx = lambda i, me_ref: (i, 0)
        elif own_axis[0] == 0:
            own_idx = lambda i, me_ref: (_slot(me_ref[0], own_axis[1]) * (R // tr) + i, 0)
        else:
            own_idx = lambda i, me_ref: (i, _slot(me_ref[0], own_axis[1]))
        in_specs.append(pl.BlockSpec((tr, C), own_idx))
        args.append(own)
    in_specs += [row, row, row]
    args += [w, m, v]
    aliases = {}
    if prev is not None:
        in_specs += [ANY_SPEC] * 4
        aliases = {1 + len(args) + k: k for k in range(4)}
        args += list(prev)
    shp = jax.ShapeDtypeStruct((L, R, C), F32)
    res = pl.pallas_call(
        body, name=name,
        grid_spec=pltpu.PrefetchScalarGridSpec(
            num_scalar_prefetch=1, grid=(R // tr,), in_specs=in_specs,
            out_specs=[row] * 4 + [pl.BlockSpec((8, LANES), lambda i, me_ref: (0, 0))]),
        out_shape=[shp] * 4 + [jax.ShapeDtypeStruct((8, LANES), F32)],
        input_output_aliases=aliases, compiler_params=_params(("arbitrary",)),
    )(me, *args)
    return res[:4], res[4]


def _pad_heads(w):
    lead = w.shape[:-1]
    n = w.shape[-1] // QK_DIM
    w = w.reshape(lead + (n, QK_DIM))
    w = jnp.pad(w, [(0, 0)] * len(lead) + [(0, 0), (0, HEAD_PAD - QK_DIM)])
    return w.reshape(lead + (n * HEAD_PAD,))


def _unpad_heads(w):
    lead = w.shape[:-1]
    n = w.shape[-1] // HEAD_PAD
    return w.reshape(lead + (n, HEAD_PAD))[..., :QK_DIM].reshape(lead + (n * QK_DIM,))


def kernel(x, ffn1_norm, ffn1_w_in, ffn1_w_out, mix_norm, ffn2_norm, ffn2_w_in, ffn2_w_out, mla_w_down, mla_g_cq, mla_g_ckv, mla_w_uq, mla_w_ukv, mla_g_qn, mla_g_kn, mla_w_o, dil_w_qkv, dil_g_qn, dil_g_kn, dil_w_o, loss_target, m_ffn1_norm, m_ffn1_w_in, m_ffn1_w_out, m_mix_norm, m_ffn2_norm, m_ffn2_w_in, m_ffn2_w_out, m_mla_w_down, m_mla_g_cq, m_mla_g_ckv, m_mla_w_uq, m_mla_w_ukv, m_mla_g_qn, m_mla_g_kn, m_mla_w_o, m_dil_w_qkv, m_dil_g_qn, m_dil_g_kn, m_dil_w_o, v_ffn1_norm, v_ffn1_w_in, v_ffn1_w_out, v_mix_norm, v_ffn2_norm, v_ffn2_w_in, v_ffn2_w_out, v_mla_w_down, v_mla_g_cq, v_mla_g_ckv, v_mla_w_uq, v_mla_w_ukv, v_mla_g_qn, v_mla_g_kn, v_mla_w_o, v_dil_w_qkv, v_dil_g_qn, v_dil_g_kn, v_dil_w_o):
    names = ["ffn1_norm", "ffn1_w_in", "ffn1_w_out", "mix_norm", "ffn2_norm", "ffn2_w_in", "ffn2_w_out", "mla_w_down",
             "mla_g_cq", "mla_g_ckv", "mla_w_uq", "mla_w_ukv", "mla_g_qn", "mla_g_kn", "mla_w_o", "dil_w_qkv",
             "dil_g_qn", "dil_g_kn", "dil_w_o"]
    W = dict(zip(names, [ffn1_norm, ffn1_w_in, ffn1_w_out, mix_norm, ffn2_norm, ffn2_w_in, ffn2_w_out, mla_w_down,
                         mla_g_cq, mla_g_ckv, mla_w_uq, mla_w_ukv, mla_g_qn, mla_g_kn, mla_w_o, dil_w_qkv,
                         dil_g_qn, dil_g_kn, dil_w_o]))
    M1 = dict(zip(names, [m_ffn1_norm, m_ffn1_w_in, m_ffn1_w_out, m_mix_norm, m_ffn2_norm, m_ffn2_w_in, m_ffn2_w_out,
                          m_mla_w_down, m_mla_g_cq, m_mla_g_ckv, m_mla_w_uq, m_mla_w_ukv, m_mla_g_qn, m_mla_g_kn,
                          m_mla_w_o, m_dil_w_qkv, m_dil_g_qn, m_dil_g_kn, m_dil_w_o]))
    V2 = dict(zip(names, [v_ffn1_norm, v_ffn1_w_in, v_ffn1_w_out, v_mix_norm, v_ffn2_norm, v_ffn2_w_in, v_ffn2_w_out,
                          v_mla_w_down, v_mla_g_cq, v_mla_g_ckv, v_mla_w_uq, v_mla_w_ukv, v_mla_g_qn, v_mla_g_kn,
                          v_mla_w_o, v_dil_w_qkv, v_dil_g_qn, v_dil_g_kn, v_dil_w_o]))
    S, D = x.shape[1], x.shape[2]
    x0 = x.reshape(S, D)
    tgt = loss_target.reshape(S, D)

    big = ["ffn1_w_in", "ffn1_w_out", "ffn2_w_in", "ffn2_w_out", "mla_w_down", "mla_w_uq", "mla_w_ukv", "mla_w_o",
           "dil_w_qkv", "dil_w_o"]
    shard_dim = {"ffn1_w_in": 2, "ffn1_w_out": 1, "ffn2_w_in": 2, "ffn2_w_out": 1, "mla_w_down": 1, "mla_w_uq": 2,
                 "mla_w_ukv": 2, "mla_w_o": 1, "dil_w_qkv": 2, "dil_w_o": 2}
    paired = ("ffn1_w_in", "ffn2_w_in")
    shard_axis = {n: (d, n in paired) for n, d in shard_dim.items()}
    grad_axis = {n: (d - 1, n in paired) for n, d in shard_dim.items()}

    def padded(n, w):
        if n == "mla_w_down":
            return jnp.pad(w, ((0, 0), (0, 0), (0, LAT_PAD - w.shape[2])))
        if n == "mla_w_uq":
            return _pad_heads(w)
        return w

    depth = ffn1_norm.shape[0]
    blocks = []
    for l in range(depth):
        mixer = (["mla_w_down", "mla_w_uq", "mla_w_ukv", "mla_w_o"] if l % 2 == 0 else ["dil_w_qkv", "dil_w_o"])
        blocks.append((f"ffn1_{l}", [("ffn1_w_in", l), ("ffn1_w_out", l)]))
        blocks.append((f"mix_{l}", [(n, l // 2) for n in mixer]))
        blocks.append((f"ffn2_{l}", [("ffn2_w_in", l), ("ffn2_w_out", l)]))
    order = [k for _, keys in blocks for k in keys]
    me = (4 * lax.axis_index("x") + 2 * lax.axis_index("y") + lax.axis_index("c")).astype(jnp.int32).reshape(1)
    def cast(key, deps=()):
        n, l = key
        return _cast_into_gathered(padded(n, W[n]), l, shard_axis[n], me, f"cast_{n}_{l}", deps=deps)

    items0, token0 = _gather_start([cast(order[0])], [shard_axis[order[0][0]]], "gather_start_first")
    rest = order[1:]
    items1, ag_token = _gather_start([cast(k, deps=[token0]) for k in rest], [shard_axis[k[0]] for k in rest],
                                     "gather_start_rest")
    ag_items = dict(zip(order, items0 + items1))
    full = {}

    def relay(keys, after, tag):
        out, token = _gather_relay([ag_items[k] for k in keys], [shard_axis[k[0]] for k in keys], after,
                                   f"gather_relay_{tag}")
        ag_items.update(zip(keys, out))
        return [token]

    def relay_next(bi, after):
        return relay(blocks[bi + 1][1], after, blocks[bi + 1][0]) if bi + 1 < len(blocks) else []

    def fetch(keys, after, tag):
        lands = _gather_wait([ag_items[k] for k in keys], [shard_axis[k[0]] for k in keys], after,
                             f"gather_wait_{tag}")
        full.update(zip(keys, lands))

    g_qn = _pad_heads(mla_g_qn)
    g_kn = _pad_heads(mla_g_kn)
    tabs = _rope_tables(S)
    slopes = jnp.asarray(_alibi_slopes(), F32)

    grads = {}
    gain_g = {}

    out_g, out_d, out_m, out_v = {}, {}, {}, {}
    pending = []
    lag = 3

    def scatter_start(tag, keys):
        items, token = _scatter_start([grads[k] for k in keys], [grad_axis[k[0]] for k in keys],
                                      f"scatter_start_{tag}")
        pending.append((tag, keys, items))
        return token

    def scatter_finish(after):
        tag, keys, items = pending.pop(0)
        lands = _scatter_wait(items, [grad_axis[k[0]] for k in keys], after, f"scatter_wait_{tag}")
        tokens = []
        for (n, l), (own, p) in zip(keys, lands):
            own_axis = grad_axis[n]
            if n in ("mla_w_down", "mla_w_uq"):
                ax, pair = grad_axis[n]
                size = own.shape[ax] // N_DEV
                own = lax.dynamic_slice_in_dim(own, _slot(me[0], pair) * size, size, axis=ax)
                own_axis = None
                if n == "mla_w_down":
                    p, own = p[..., :W[n].shape[2]], own[..., :W[n].shape[2]]
                else:
                    p, own = _unpad_heads(p), _unpad_heads(own)
            prev = (out_g[n], out_d[n], out_m[n], out_v[n]) if n in out_g else None
            (out_g[n], out_d[n], out_m[n], out_v[n]), tok = _adamw(p, own, me, W[n], M1[n], V2[n], l, prev,
                                                                    f"adamw_{n}_{l}", own_axis=own_axis)
            tokens.append(tok)
        return tokens

    def finish_due(after):
        tokens = []
        while len(pending) > lag:
            tokens += scatter_finish(after)
        return tokens

    def mixer_out(o, w_o, xin, l, toks, name):
        tm, tn, tk, _ = _mm_tiles(S, o.shape[1], D, 2, 2, 4, True)
        if tn == D and tk == o.shape[1]:
            return _mm(o, w_o, "nn", F32, name, res=xin, layer=0, deps=toks, norm_gain=ffn2_norm[l:l + 1])
        return _mm(o, w_o, "nn", F32, name, res=xin, layer=0, deps=toks), None

    def ffn_fwd(xin, norm_row, which, l, bi, deps=(), h=None):
        tag = blocks[bi][0]
        k_in, k_out = (which + "_w_in", l), (which + "_w_out", l)
        if h is None:
            h = _rms_fwd(xin, norm_row, f"rms_fwd_{tag}", deps=deps)
        if bi == 0:
            relay([k_in], h, f"{tag}_in")
        fetch([k_in], h, f"in_{tag}")
        u, a = _ffn_in(h, full[k_in], f"ffn_in_{tag}")
        if bi == 0:
            relay([k_out], a, f"{tag}_out")
        fetch([k_out], a, f"out_{tag}")
        toks = relay_next(bi, a)
        xo = _mm(a, full[k_out], "nn", F32, f"mm_out_{tag}", scale=0.5, res=xin, layer=0, deps=toks)
        return xo, (xin, h, u, a)

    def ffn_bwd(dx_pair, saved, norm_row, which, l, tag, deps=()):
        dxo, dxob = dx_pair
        k_in, k_out = (which + "_w_in", l), (which + "_w_out", l)
        xin, h, u, a = saved
        grads[k_out] = _mm(a, dxob, "tn", BF16, f"mm_dwout_{tag}", scale=0.5, deps=deps)
        t_out = scatter_start(f"{tag}_out", [k_out])
        du = _ffn_da(dxob, full[k_out], u, f"ffn_da_{tag}", deps=[t_out])
        grads[k_in] = _mm(h, du, "tn", BF16, f"mm_dwin_{tag}")
        t_in = scatter_start(f"{tag}_in", [k_in])
        dh = _mm(du, full[k_in], "nt", F32, f"mm_dh_{tag}", layer=0, deps=[t_in])
        toks = finish_due(dh)
        dx, dxb, dg = _rms_bwd(xin, norm_row, dh, dxo, f"rms_bwd_{tag}", deps=toks)
        gain_g.setdefault(which + "_norm", {})[l] = dg
        return dx, dxb

    def mla_fwd(xin, l, bi):
        j = l // 2
        xn = _rms_fwd(xin, mix_norm[l:l + 1], "rms_fwd_mla")
        fetch([(n, j) for n in ("mla_w_down", "mla_w_uq", "mla_w_ukv", "mla_w_o")], xn, "mla")
        lat = _mm(xn, full[("mla_w_down", j)], "nn", F32, "mm_lat", layer=0)
        cq, ckv = _lat_norm_fwd(lat, mla_g_cq[j:j + 1], mla_g_ckv[j:j + 1], "lat_norm_fwd")
        q_raw = _mm(cq, full[("mla_w_uq", j)], "nn", F32, "mm_uq", layer=0)
        kv = _mm(ckv, full[("mla_w_ukv", j)], "nn", F32, "mm_ukv", layer=0)
        qf, kf, vb = _mla_prep_fwd(q_raw, kv, lat, g_qn[j:j + 1], g_kn[j:j + 1], tabs, "mla_prep_fwd")
        o, lse = _flash_fwd(qf, kf, vb, "flash_fwd")
        toks = relay_next(bi, o)
        xo, h_next = mixer_out(o, full[("mla_w_o", j)], xin, l, toks, "mm_mla_o")
        return (xo, h_next), (xin, xn, lat, cq, ckv, q_raw, kv, qf, kf, vb, o, lse)

    def mla_bwd(dx_pair, saved, l):
        dxo, dxob = dx_pair
        j = l // 2
        xin, xn, lat, cq, ckv, q_raw, kv, qf, kf, vb, o, lse = saved
        do = _mm(dxob, full[("mla_w_o", j)], "nt", BF16, "mm_mla_do", layer=0)
        grads[("mla_w_o", j)] = _mm(o, dxob, "tn", BF16, "mm_mla_dwo")
        delta = _attn_delta(do, o, "attn_delta")
        dqf, dkf, dv = _flash_bwd(qf, kf, vb, do, lse, delta, "flash_bwd")
        dq_raw, dkv, dkpe, dgq, dgk = _mla_prep_bwd(q_raw, kv, lat, g_qn[j:j + 1], g_kn[j:j + 1], tabs, dqf, dkf, dv,
                                                    "mla_prep_bwd")
        gain_g.setdefault("mla_g_qn", {})[j] = dgq
        gain_g.setdefault("mla_g_kn", {})[j] = dgk
        dcq = _mm(dq_raw, full[("mla_w_uq", j)], "nt", F32, "mm_dcq", layer=0)
        grads[("mla_w_uq", j)] = _mm(cq, dq_raw, "tn", BF16, "mm_dwuq")
        dckv = _mm(dkv, full[("mla_w_ukv", j)], "nt", F32, "mm_dckv", layer=0)
        grads[("mla_w_ukv", j)] = _mm(ckv, dkv, "tn", BF16, "mm_dwukv")
        dlat, dgcq, dgckv = _lat_norm_bwd(lat, mla_g_cq[j:j + 1], mla_g_ckv[j:j + 1], dcq, dckv, dkpe, "lat_norm_bwd")
        gain_g.setdefault("mla_g_cq", {})[j] = dgcq
        gain_g.setdefault("mla_g_ckv", {})[j] = dgckv
        dxn = _mm(dlat, full[("mla_w_down", j)], "nt", F32, "mm_dxn_mla", layer=0)
        grads[("mla_w_down", j)] = _mm(xn, dlat, "tn", BF16, "mm_dwdown")
        tok = scatter_start(f"mix_{l}", [(n, j) for n in ("mla_w_down", "mla_w_uq", "mla_w_ukv", "mla_w_o")])
        toks = finish_due(dxn)
        dx, dxb, dg = _rms_bwd(xin, mix_norm[l:l + 1], dxn, dxo, "rms_bwd_mla", deps=[tok] + toks)
        gain_g.setdefault("mix_norm", {})[l] = dg
        return dx, dxb

    def dil_fwd(xin, l, bi):
        j = l // 2
        xn = _rms_fwd(xin, mix_norm[l:l + 1], "rms_fwd_dil")
        fetch([("dil_w_qkv", j), ("dil_w_o", j)], xn, "dil")
        qkv = _mm(xn, full[("dil_w_qkv", j)], "nn", F32, "mm_qkv", layer=0)
        o_g, lse_g = _dil_fwd(qkv, dil_g_qn[j:j + 1], dil_g_kn[j:j + 1], slopes, "dil_fwd")
        o, lse = _dil_merge(o_g, lse_g, "dil_merge")
        toks = relay_next(bi, o)
        xo, h_next = mixer_out(o, full[("dil_w_o", j)], xin, l, toks, "mm_dil_o")
        return (xo, h_next), (xin, xn, qkv, o, lse)

    def dil_bwd(dx_pair, saved, l):
        dxo, dxob = dx_pair
        j = l // 2
        xin, xn, qkv, o, lse = saved
        do = _mm(dxob, full[("dil_w_o", j)], "nt", F32, "mm_dil_do", layer=0)
        grads[("dil_w_o", j)] = _mm(o, dxob, "tn", BF16, "mm_dil_dwo")
        delta = _attn_delta(do, o, "dil_delta")
        dq, dk, dv, dgq, dgk = _dil_bwd(qkv, dil_g_qn[j:j + 1], dil_g_kn[j:j + 1], slopes, do, delta, lse, "dil_bwd")
        gain_g.setdefault("dil_g_qn", {})[j] = dgq
        gain_g.setdefault("dil_g_kn", {})[j] = dgk
        dqkv = jnp.concatenate([dq, dk, dv], axis=1)
        dxn = _mm(dqkv, full[("dil_w_qkv", j)], "nt", F32, "mm_dxn_dil", layer=0)
        grads[("dil_w_qkv", j)] = _mm(xn, dqkv, "tn", BF16, "mm_dwqkv")
        tok = scatter_start(f"mix_{l}", [("dil_w_qkv", j), ("dil_w_o", j)])
        toks = finish_due(dxn)
        dx, dxb, dg = _rms_bwd(xin, mix_norm[l:l + 1], dxn, dxo, "rms_bwd_dil", deps=[tok] + toks)
        gain_g.setdefault("mix_norm", {})[l] = dg
        return dx, dxb

    saved = []
    xc = x0
    for l in range(depth):
        xc, s1 = ffn_fwd(xc, ffn1_norm[l:l + 1], "ffn1", l, 3 * l, deps=[ag_token] if l == 0 else ())
        (xc, h_next), s2 = (mla_fwd if l % 2 == 0 else dil_fwd)(xc, l, 3 * l + 1)
        xc, s3 = ffn_fwd(xc, ffn2_norm[l:l + 1], "ffn2", l, 3 * l + 2, h=h_next)
        saved.append((s1, s2, s3))

    dy, dyb, loss_part = _loss_head(xc, tgt, "loss_head")
    dx = (dy, dyb)
    loss = lax.psum(loss_part[0, 0], MESH_AXES)

    for bi in reversed(range(len(blocks))):
        tag, _ = blocks[bi]
        l = bi // 3
        s = saved[l][bi % 3]
        if bi % 3 == 2:
            dx = ffn_bwd(dx, s, ffn2_norm[l:l + 1], "ffn2", l, tag,
                         deps=[loss.reshape(1, 1)] if bi == len(blocks) - 1 else ())
        elif bi % 3 == 1:
            dx = (mla_bwd if l % 2 == 0 else dil_bwd)(dx, s, l)
        else:
            dx = ffn_bwd(dx, s, ffn1_norm[l:l + 1], "ffn1", l, tag)
    grad_x = dx[0].reshape(x.shape)
    after = dx[1]
    while pending:
        after = scatter_finish(after)[-1]

    small = [n for n in names if n not in big]

    def gain_local(n):
        rows = [gain_g[n][l] for l in range(W[n].shape[0])]
        g = jnp.concatenate(rows, axis=1)
        return g

    def flat_pad(n, a):
        a = a.reshape(1, -1)
        if n in ("mla_g_qn", "mla_g_kn"):
            a = _pad_heads(a)
        return a

    packed_g = jnp.concatenate([gain_local(n) for n in small], axis=1)
    sizes = [gain_local(n).shape[1] for n in small]
    tot_g = _gain_allreduce(packed_g, "gain_allreduce")
    pw = jnp.concatenate([flat_pad(n, W[n]) for n in small], axis=1)
    pm = jnp.concatenate([flat_pad(n, M1[n]) for n in small], axis=1)
    pv = jnp.concatenate([flat_pad(n, V2[n]) for n in small], axis=1)
    res, _ = _adamw(tot_g.reshape(1, 1, -1), None, me, pw.reshape(1, 1, -1), pm.reshape(1, 1, -1),
                    pv.reshape(1, 1, -1), 0, None, "adamw_gains")
    res = [r.reshape(1, -1) for r in res]
    off = 0
    for n, sz in zip(small, sizes):
        for dst, r in zip((out_g, out_d, out_m, out_v), res):
            piece = r[:, off:off + sz]
            if n in ("mla_g_qn", "mla_g_kn"):
                piece = _unpad_heads(piece)
            dst[n] = piece.reshape(W[n].shape)
        off += sz

    return (loss, grad_x, *[out_g[n] for n in names], *[out_d[n] for n in names],
            *[out_m[n] for n in names], *[out_v[n] for n in names])
```

```python
import functools
import math

import jax
import jax.numpy as jnp
import numpy as np
from jax import lax
from jax.experimental import pallas as pl
from jax.experimental.pallas import tpu as pltpu

EPS = 1e-6
MLA_HEADS = 16
Q_LORA = 512
KV_LORA = 512
NOPE_DIM = 128
ROPE_DIM = 64
V_DIM = 128
QK_DIM = NOPE_DIM + ROPE_DIM
ROPE_THETA = 10000.0
HEAD_PAD = 256
LAT_PAD = Q_LORA + KV_LORA + 128
DIL_PAIRS = ((128, 1), (512, 4), (2048, 16))
DIL_GROUPS = 3
DIL_HEADS = 8
DIL_HEAD_DIM = 128
DIL_BLK = 128
FLASH_HEADS = 2
FLASH_HEADS_FWD = 8
LOG2E = math.log2(math.e)
LN2 = math.log(2.0)
ADAM_LR = 0.001
ADAM_B1 = 0.9
ADAM_B2 = 0.999
ADAM_EPS = 1e-08
ADAM_WD = 0.01
ADAM_STEP = 10

N_DEV = 8
MESH_AXES = ("x", "y", "c")
MESH = pl.DeviceIdType.MESH
NEG_BIG = -1e30
VMEM_LIMIT_V7X = 56 * 1024 * 1024
LANES = 128

BF16 = jnp.bfloat16
F32 = jnp.float32


def _pick(n, cands):
    for c in cands:
        if n % c == 0:
            return c
    raise ValueError(f"no tile for {n}")


def _params(sem):
    return pltpu.CompilerParams(dimension_semantics=sem, vmem_limit_bytes=VMEM_LIMIT_V7X)


ANY_SPEC = pl.BlockSpec(memory_space=pl.ANY)


MM_VMEM_BUDGET = 44 * 1024 * 1024
MM_HBM_BYTES_PER_S = 1.8e12
MM_MXU_FLOPS_PER_S = 8.5e14
MM_STEP_S = 0.4e-6
MM_MAX_TILE_MACS = 3.3e9
MXU_DIM = 256


MM_TIMED_TILES = {
    (2048, 4096, 11264, 2, 2, 2, False): (1024, 1024, 4096, True),
    (4096, 11264, 2048, 2, 2, 4, False): (1024, 1024, 2816, True),
    (4096, 5632, 2048, 2, 2, 4, True): (512, 1024, 5632, False),
    (2048, 4096, 9216, 2, 2, 2, False): (1024, 1024, 4096, True),
    (4096, 9216, 2048, 2, 2, 4, False): (1024, 1024, 2304, True),
}


@functools.lru_cache(maxsize=None)
def _mm_tiles(M, K, N, a_bytes, b_bytes, out_bytes, has_res):
    if (M, K, N, a_bytes, b_bytes, out_bytes, has_res) in MM_TIMED_TILES:
        return MM_TIMED_TILES[(M, K, N, a_bytes, b_bytes, out_bytes, has_res)]
    best = None
    for tk in [K] + [c for c in (1408, 1024, 512, 384, 256, 128) if K % c == 0 and c < K]:
        nk = K // tk
        for tm in [c for c in (2048, 1024, 512, 256, 128) if M % c == 0]:
            for tn in [c for c in (2816, 2048, 1408, 1152, 1024, 512, 384, 256, 128) if N % c == 0]:
                if tm * tk * tn > MM_MAX_TILE_MACS:
                    continue
                fill = (tn / (-(-tn // MXU_DIM) * MXU_DIM)) * (tk / (-(-tk // MXU_DIM) * MXU_DIM))
                fill *= tm / (tm + MXU_DIM // 2)
                vmem = 2 * (tm * tk * a_bytes + tk * tn * b_bytes) + 2 * tm * tn * out_bytes + tm * tn * 4
                vmem += (tm * tk + tk * tn) * 2 if max(a_bytes, b_bytes) > 2 else 0
                vmem += 2 * tm * tn * 4 if has_res else 0
                if vmem > MM_VMEM_BUDGET:
                    continue
                a_all, b_all = M * K * a_bytes, K * N * b_bytes
                if nk == 1:
                    t_i = a_all + (M // tm) * b_all
                    t_j = b_all + (N // tn) * a_all
                    traffic, i_outer = min((t_i, True), (t_j, False))
                else:
                    traffic, i_outer = (N // tn) * a_all + (M // tm) * b_all, True
                traffic += M * N * (out_bytes + (4 if has_res else 0))
                mxu = 2.0 * M * K * N / (MM_MXU_FLOPS_PER_S * fill) * (1.15 if nk > 1 else 1.0)
                cost = max(traffic / MM_HBM_BYTES_PER_S, mxu) + (M // tm) * (N // tn) * nk * MM_STEP_S
                if best is None or cost < best[0]:
                    best = (cost, tm, tn, tk, i_outer)
    assert best is not None, (M, K, N)
    return best[1:]


def _mm(a, b, mode, out_dtype, name, *, scale=1.0, res=None, layer=None, deps=(), norm_gain=None, dot_with=None):
    b2 = b.shape[-2:]
    if mode == "nn":
        (M, K), (Kb, N) = a.shape, b2
    elif mode == "nt":
        (M, K), (N, Kb) = a.shape, b2
    else:
        (K, M), (Kb, N) = a.shape, b2
    assert K == Kb, (a.shape, b.shape, mode)
    tm, tn, tk, i_outer = _mm_tiles(M, K, N, a.dtype.itemsize, b.dtype.itemsize, jnp.dtype(out_dtype).itemsize,
                                    res is not None)
    nk = K // tk
    dims = {"nn": (((1,), (0,)), ((), ())), "nt": (((1,), (1,)), ((), ())), "tn": (((0,), (0,)), ((), ()))}[mode]
    normed = norm_gain is not None
    dotted = dot_with is not None
    assert not normed or (nk == 1 and tn == N), (name, tn, N, nk)
    assert not dotted or (nk == 1 and not normed and tn % LANES == 0), (name, tn, nk)
    n_in = 2 + (res is not None) + normed + dotted + len(deps)

    def finish(v, r_ref, o_ref):
        if scale != 1.0:
            v = v * scale
        if r_ref is not None:
            v = r_ref[...] + v
        o_ref[...] = v.astype(o_ref.dtype)
        return v

    def body(*refs):
        a_ref, b_ref = refs[:2]
        r_ref = refs[2] if res is not None else None
        prod = lambda: lax.dot_general(a_ref[...].astype(BF16), b_ref[...].astype(BF16), dims,
                                       preferred_element_type=F32)
        if nk == 1:
            v = finish(prod(), r_ref, refs[n_in])
            if normed:
                g_ref = refs[2 + (res is not None)]
                r = lax.rsqrt(jnp.mean(v * v, axis=-1, keepdims=True) + EPS)
                refs[n_in + 1][...] = ((v * r) * g_ref[...]).astype(BF16)
            if dotted:
                w_ref, d_ref = refs[2 + (res is not None)], refs[n_in + 1]
                for h in range(tn // LANES):
                    sl = slice(h * LANES, (h + 1) * LANES)
                    d = jnp.sum(v[:, sl] * w_ref[:, sl].astype(F32), axis=-1, keepdims=True)
                    d_ref[:, sl] = jnp.broadcast_to(d, (tm, LANES))
            return
        o_ref, acc = refs[-2:]
        k = pl.program_id(2)

        @pl.when(k == 0)
        def _():
            acc[...] = prod()

        @pl.when(k > 0)
        def _():
            acc[...] += prod()

        @pl.when(k == nk - 1)
        def _():
            finish(acc[...], r_ref, o_ref)

    ij = (lambda p, q: (p, q)) if i_outer else (lambda p, q: (q, p))

    def spec(shape, f, lead=None):
        full = lambda p, q, k: f(*ij(p, q), k)
        if lead is None:
            return pl.BlockSpec(shape, full)
        return pl.BlockSpec((None,) + shape, lambda p, q, k: (lead,) + full(p, q, k))

    a_spec = spec((tk, tm), lambda i, j, k: (k, i)) if mode == "tn" else spec((tm, tk), lambda i, j, k: (i, k))
    lead = layer if b.ndim == 3 else None
    b_spec = spec((tn, tk), lambda i, j, k: (j, k), lead) if mode == "nt" else spec((tk, tn), lambda i, j, k: (k, j), lead)
    in_specs = [a_spec, b_spec]
    args = [a, b]
    if res is not None:
        in_specs.append(spec((tm, tn), lambda i, j, k: (i, j)))
        args.append(res)
    tile = spec((tm, tn), lambda i, j, k: (i, j))
    if normed:
        in_specs.append(pl.BlockSpec((1, N), lambda p, q, k: (0, 0)))
        args.append(norm_gain)
    if dotted:
        in_specs.append(tile)
        args.append(dot_with)
    in_specs += [ANY_SPEC] * len(deps)
    args += list(deps)
    outer, inner = (M // tm, N // tn) if i_outer else (N // tn, M // tm)
    second = [jax.ShapeDtypeStruct((M, N), BF16)] if normed else [jax.ShapeDtypeStruct((M, N), F32)] if dotted else []
    out = pl.pallas_call(
        body, name=name, grid=(outer, inner, nk),
        in_specs=in_specs, out_specs=[tile, tile] if second else tile,
        out_shape=[jax.ShapeDtypeStruct((M, N), out_dtype)] + second if second else jax.ShapeDtypeStruct((M, N), out_dtype),
        scratch_shapes=[pltpu.VMEM((tm, tn), F32)] if nk > 1 else [],
        compiler_params=_params(("parallel", "parallel", "arbitrary")),
    )(*args)
    return tuple(out) if second else out


def _cast_into_gathered(w, layer, axis, me, name, deps=()):
    _, R, C = w.shape
    tr = _pick(R, (512, 256, 128, 64, 32, 16))
    nr = R // tr
    axis, paired = axis

    def body(me_ref, w_ref, *rest):
        o_ref = rest[-1]
        o_ref[...] = w_ref[...].astype(BF16)

    if axis == 1:
        out_idx = lambda i, me_ref: (0, _slot(me_ref[0], paired) * nr + i, 0)
        shape = (1, R * N_DEV, C)
    else:
        out_idx = lambda i, me_ref: (0, i, _slot(me_ref[0], paired))
        shape = (1, R, C * N_DEV)
    return pl.pallas_call(
        body, name=name,
        grid_spec=pltpu.PrefetchScalarGridSpec(
            num_scalar_prefetch=1, grid=(nr,),
            in_specs=[pl.BlockSpec((None, tr, C), lambda i, me_ref: (layer, i, 0))] + [ANY_SPEC] * len(deps),
            out_specs=pl.BlockSpec((None, tr, C), out_idx)),
        out_shape=jax.ShapeDtypeStruct(shape, BF16), compiler_params=_params(("parallel",)),
    )(me, w, *deps)


def _rms_fwd(x, g, name, deps=()):
    T, D = x.shape
    tr = _pick(T, (512, 256, 128))

    def body(x_ref, g_ref, *rest):
        o_ref = rest[-1]
        xv = x_ref[...]
        r = lax.rsqrt(jnp.mean(xv * xv, axis=-1, keepdims=True) + EPS)
        o_ref[...] = ((xv * r) * g_ref[...]).astype(BF16)

    return pl.pallas_call(
        body, name=name, grid=(T // tr,),
        in_specs=[pl.BlockSpec((tr, D), lambda i: (i, 0)), pl.BlockSpec((1, D), lambda i: (0, 0))]
        + [ANY_SPEC] * len(deps),
        out_specs=pl.BlockSpec((tr, D), lambda i: (i, 0)),
        out_shape=jax.ShapeDtypeStruct((T, D), BF16), compiler_params=_params(("parallel",)),
    )(x, g, *deps)


def _rms_bwd(x, g, dh, dres, name, deps=()):
    T, D = x.shape
    tr = _pick(T, (256, 128))

    def body(x_ref, g_ref, dh_ref, dres_ref, *rest):
        dx_ref, dxb_ref, dg_ref = rest[-3:]
        xv = x_ref[...]
        dhv = dh_ref[...]
        r = lax.rsqrt(jnp.mean(xv * xv, axis=-1, keepdims=True) + EPS)
        xhat = xv * r
        dxh = dhv * g_ref[...]
        c = jnp.mean(dxh * xhat, axis=-1, keepdims=True)
        dx = dres_ref[...] + r * (dxh - xhat * c)
        dx_ref[...] = dx
        dxb_ref[...] = dx.astype(BF16)

        @pl.when(pl.program_id(0) == 0)
        def _():
            dg_ref[...] = jnp.zeros_like(dg_ref)

        dg_ref[...] += jnp.sum(dhv * xhat, axis=0, keepdims=True)

    row = pl.BlockSpec((tr, D), lambda i: (i, 0))
    vec = pl.BlockSpec((1, D), lambda i: (0, 0))
    return pl.pallas_call(
        body, name=name, grid=(T // tr,),
        in_specs=[row, vec, row, row] + [ANY_SPEC] * len(deps), out_specs=[row, row, vec],
        out_shape=[jax.ShapeDtypeStruct((T, D), F32), jax.ShapeDtypeStruct((T, D), BF16),
                   jax.ShapeDtypeStruct((1, D), F32)],
        compiler_params=_params(("arbitrary",)),
    )(x, g, dh, dres, *deps)


N_PANEL = N_DEV // 2


def _ffn_in(h, w_in, name):
    T, D = h.shape
    F2 = w_in.shape[2]
    pw = F2 // N_PANEL
    half = pw // 2
    tm = _pick(T, (512, 256, 128))

    def body(h_ref, w_ref, u_ref, a_ref):
        r = jnp.dot(h_ref[...], w_ref[...], preferred_element_type=F32)
        u_ref[...] = r.astype(BF16)
        g, up = r[:, :half], r[:, half:]
        a_ref[...] = (g * jax.nn.sigmoid(g) * up).astype(BF16)

    return pl.pallas_call(
        body, name=name, grid=(N_PANEL, T // tm),
        in_specs=[pl.BlockSpec((tm, D), lambda p, i: (i, 0)), pl.BlockSpec((None, D, pw), lambda p, i: (0, 0, p))],
        out_specs=[pl.BlockSpec((tm, pw), lambda p, i: (i, p)), pl.BlockSpec((tm, half), lambda p, i: (i, p))],
        out_shape=[jax.ShapeDtypeStruct((T, F2), BF16), jax.ShapeDtypeStruct((T, F2 // 2), BF16)],
        compiler_params=_params(("parallel", "parallel")),
    )(h, w_in)


def _ffn_da(dxo, w_out, u, name, deps=()):
    T, D = dxo.shape
    F2 = u.shape[1]
    pw = F2 // N_PANEL
    half = pw // 2
    tm = _pick(T, (512, 256, 128))

    def body(d_ref, w_ref, u_ref, *rest):
        du_ref = rest[-1]
        da = 0.5 * lax.dot_general(d_ref[...], w_ref[...], NT_DIMS, preferred_element_type=F32)
        g = u_ref[:, :half].astype(F32)
        up = u_ref[:, half:].astype(F32)
        sg = jax.nn.sigmoid(g)
        silu = g * sg
        du_ref[:, :half] = (da * up * (sg + silu * (1.0 - sg))).astype(BF16)
        du_ref[:, half:] = (da * silu).astype(BF16)

    return pl.pallas_call(
        body, name=name, grid=(N_PANEL, T // tm),
        in_specs=[pl.BlockSpec((tm, D), lambda p, i: (i, 0)), pl.BlockSpec((None, half, D), lambda p, i: (0, p, 0)),
                  pl.BlockSpec((tm, pw), lambda p, i: (i, p))] + [ANY_SPEC] * len(deps),
        out_specs=pl.BlockSpec((tm, pw), lambda p, i: (i, p)),
        out_shape=jax.ShapeDtypeStruct((T, F2), BF16), compiler_params=_params(("parallel", "parallel")),
    )(dxo, w_out, u, *deps)


def _loss_head(y, t, name):
    T, D = y.shape
    tr = _pick(T, (512, 256, 128))

    def body(y_ref, t_ref, dy_ref, dyb_ref, l_ref):
        e = y_ref[...] - t_ref[...]
        dy = e * (1.0 / D)
        dy_ref[...] = dy
        dyb_ref[...] = dy.astype(BF16)

        @pl.when(pl.program_id(0) == 0)
        def _():
            l_ref[...] = jnp.zeros_like(l_ref)

        l_ref[...] += 0.5 * jnp.sum(jnp.mean(e * e, axis=-1, keepdims=True), axis=0, keepdims=True)

    row = pl.BlockSpec((tr, D), lambda i: (i, 0))
    return pl.pallas_call(
        body, name=name, grid=(T // tr,),
        in_specs=[row, row], out_specs=[row, row, pl.BlockSpec((1, 1), lambda i: (0, 0))],
        out_shape=[jax.ShapeDtypeStruct((T, D), F32), jax.ShapeDtypeStruct((T, D), BF16),
                   jax.ShapeDtypeStruct((1, 1), F32)],
        compiler_params=_params(("arbitrary",)),
    )(y, t)


def _rope_tables(S):
    half = ROPE_DIM // 2
    inv = 1.0 / (ROPE_THETA ** (jnp.arange(0, ROPE_DIM, 2, dtype=F32) / ROPE_DIM))
    ang = jnp.arange(S, dtype=F32)[:, None] * inv[None, :]
    cos, sin = jnp.cos(ang), jnp.sin(ang)
    z = jnp.zeros((S, half), F32)
    z2 = jnp.zeros((S, LANES - ROPE_DIM), F32)
    c = jnp.concatenate([cos, cos, z2], axis=1)
    s1 = jnp.concatenate([-sin, z, z2], axis=1)
    s2 = jnp.concatenate([z, sin, z2], axis=1)
    return c, s1, s2


def _rope(r, c, s1, s2):
    return r * c + pltpu.roll(r, LANES - ROPE_DIM // 2, 1) * s1 + pltpu.roll(r, ROPE_DIM // 2, 1) * s2


def _rope_t(d, c, s1, s2):
    return d * c + pltpu.roll(d * s1, ROPE_DIM // 2, 1) + pltpu.roll(d * s2, LANES - ROPE_DIM // 2, 1)


def _lat_norm_fwd(lat, g_cq, g_ckv, name):
    T = lat.shape[0]
    tr = _pick(T, (512, 256, 128))

    def body(lat_ref, gq_ref, gk_ref, cq_ref, ckv_ref):
        for off, g_ref, o_ref in ((0, gq_ref, cq_ref), (Q_LORA, gk_ref, ckv_ref)):
            xv = lat_ref[:, off:off + Q_LORA]
            r = lax.rsqrt(jnp.mean(xv * xv, axis=-1, keepdims=True) + EPS)
            o_ref[...] = ((xv * r) * g_ref[...]).astype(BF16)

    vec = pl.BlockSpec((1, Q_LORA), lambda i: (0, 0))
    out = pl.BlockSpec((tr, Q_LORA), lambda i: (i, 0))
    return pl.pallas_call(
        body, name=name, grid=(T // tr,),
        in_specs=[pl.BlockSpec((tr, LAT_PAD), lambda i: (i, 0)), vec, vec], out_specs=[out, out],
        out_shape=[jax.ShapeDtypeStruct((T, Q_LORA), BF16)] * 2, compiler_params=_params(("parallel",)),
    )(lat, g_cq, g_ckv)


def _lat_norm_bwd(lat, g_cq, g_ckv, dcq, dckv, dkpe, name):
    T = lat.shape[0]
    tr = _pick(T, (256, 128))

    def body(lat_ref, gq_ref, gk_ref, dcq_ref, dckv_ref, dkpe_ref, dlat_ref, dgq_ref, dgk_ref):
        @pl.when(pl.program_id(0) == 0)
        def _():
            dgq_ref[...] = jnp.zeros_like(dgq_ref)
            dgk_ref[...] = jnp.zeros_like(dgk_ref)

        for off, g_ref, d_ref, dg_ref in ((0, gq_ref, dcq_ref, dgq_ref), (Q_LORA, gk_ref, dckv_ref, dgk_ref)):
            xv = lat_ref[:, off:off + Q_LORA]
            dv = d_ref[...]
            r = lax.rsqrt(jnp.mean(xv * xv, axis=-1, keepdims=True) + EPS)
            xhat = xv * r
            dxh = dv * g_ref[...]
            c = jnp.mean(dxh * xhat, axis=-1, keepdims=True)
            dlat_ref[:, off:off + Q_LORA] = (r * (dxh - xhat * c)).astype(BF16)
            dg_ref[...] += jnp.sum(dv * xhat, axis=0, keepdims=True)
        dlat_ref[:, Q_LORA + KV_LORA:] = dkpe_ref[...].astype(BF16)

    vec = pl.BlockSpec((1, Q_LORA), lambda i: (0, 0))
    half = pl.BlockSpec((tr, Q_LORA), lambda i: (i, 0))
    full = pl.BlockSpec((tr, LAT_PAD), lambda i: (i, 0))
    return pl.pallas_call(
        body, name=name, grid=(T // tr,),
        in_specs=[full, vec, vec, half, half, pl.BlockSpec((tr, LANES), lambda i: (i, 0))],
        out_specs=[full, vec, vec],
        out_shape=[jax.ShapeDtypeStruct((T, LAT_PAD), BF16), jax.ShapeDtypeStruct((1, Q_LORA), F32),
                   jax.ShapeDtypeStruct((1, Q_LORA), F32)],
        compiler_params=_params(("arbitrary",)),
    )(lat, g_cq, g_ckv, dcq, dckv, dkpe)


def _mla_prep_fwd(q_raw, kv, lat, g_qn, g_kn, tabs, name):
    T = q_raw.shape[0]
    H = MLA_HEADS
    tr = _pick(T, (256, 128))
    scale = LOG2E / math.sqrt(QK_DIM)

    def body(q_ref, kv_ref, kpe_ref, gq_ref, gk_ref, c_ref, s1_ref, s2_ref, qf_ref, kf_ref, v_ref):
        c, s1, s2 = c_ref[...], s1_ref[...], s2_ref[...]
        gq, gk = gq_ref[...], gk_ref[...]
        kpe = kpe_ref[...]
        kpe_ss = jnp.sum(kpe * kpe, axis=-1, keepdims=True)
        for h in range(H):
            lo = h * HEAD_PAD
            qa = q_ref[:, lo:lo + LANES]
            qb = q_ref[:, lo + LANES:lo + HEAD_PAD]
            ss = jnp.sum(qa * qa + qb * qb, axis=-1, keepdims=True)
            r = lax.rsqrt(ss * (1.0 / QK_DIM) + EPS)
            qf_ref[:, lo:lo + LANES] = (qa * r * gq[:, :LANES] * scale).astype(BF16)
            qf_ref[:, lo + LANES:lo + HEAD_PAD] = (_rope(qb * r * gq[:, LANES:], c, s1, s2) * scale).astype(BF16)
            ka = kv_ref[:, lo:lo + LANES]
            ss = jnp.sum(ka * ka, axis=-1, keepdims=True) + kpe_ss
            r = lax.rsqrt(ss * (1.0 / QK_DIM) + EPS)
            kf_ref[:, lo:lo + LANES] = (ka * r * gk[:, :LANES]).astype(BF16)
            kf_ref[:, lo + LANES:lo + HEAD_PAD] = _rope(kpe * r * gk[:, LANES:], c, s1, s2).astype(BF16)
            v_ref[:, lo:lo + V_DIM] = kv_ref[:, lo + LANES:lo + HEAD_PAD].astype(BF16)
            v_ref[:, lo + V_DIM:lo + HEAD_PAD] = jnp.ones((tr, HEAD_PAD - V_DIM), BF16)

    wide = pl.BlockSpec((tr, H * HEAD_PAD), lambda i: (i, 0))
    lane = pl.BlockSpec((tr, LANES), lambda i: (i, 0))
    gvec = pl.BlockSpec((1, HEAD_PAD), lambda i: (0, 0))
    return pl.pallas_call(
        body, name=name, grid=(T // tr,),
        in_specs=[wide, wide, pl.BlockSpec((tr, LANES), lambda i: (i, (Q_LORA + KV_LORA) // LANES)), gvec, gvec,
                  lane, lane, lane],
        out_specs=[wide, wide, wide],
        out_shape=[jax.ShapeDtypeStruct((T, H * HEAD_PAD), BF16)] * 3,
        compiler_params=_params(("parallel",)),
    )(q_raw, kv, lat, g_qn, g_kn, *tabs)


def _mla_prep_bwd(q_raw, kv, lat, g_qn, g_kn, tabs, dqf, dkf, dv, name):
    T = q_raw.shape[0]
    H = MLA_HEADS
    tr = _pick(T, (128,))

    def body(q_ref, kv_ref, kpe_ref, gq_ref, gk_ref, c_ref, s1_ref, s2_ref, dqf_ref, dkf_ref, dv_ref,
             dq_ref, dkv_ref, dkpe_ref, dgq_ref, dgk_ref):
        @pl.when(pl.program_id(0) == 0)
        def _():
            dgq_ref[...] = jnp.zeros_like(dgq_ref)
            dgk_ref[...] = jnp.zeros_like(dgk_ref)

        c, s1, s2 = c_ref[...], s1_ref[...], s2_ref[...]
        gq, gk = gq_ref[...], gk_ref[...]
        kpe = kpe_ref[...]
        kpe_ss = jnp.sum(kpe * kpe, axis=-1, keepdims=True)
        dkpe = jnp.zeros_like(kpe)
        dgq_a = jnp.zeros((1, LANES), F32)
        dgq_b = jnp.zeros((1, LANES), F32)
        dgk_a = jnp.zeros((1, LANES), F32)
        dgk_b = jnp.zeros((1, LANES), F32)
        for h in range(H):
            lo = h * HEAD_PAD
            xa = q_ref[:, lo:lo + LANES]
            xb = q_ref[:, lo + LANES:lo + HEAD_PAD]
            ss = jnp.sum(xa * xa + xb * xb, axis=-1, keepdims=True)
            r = lax.rsqrt(ss * (1.0 / QK_DIM) + EPS)
            xa, xb = xa * r, xb * r
            da = dqf_ref[:, lo:lo + LANES].astype(F32)
            db = _rope_t(dqf_ref[:, lo + LANES:lo + HEAD_PAD].astype(F32), c, s1, s2)
            dgq_a += jnp.sum(da * xa, axis=0, keepdims=True)
            dgq_b += jnp.sum(db * xb, axis=0, keepdims=True)
            da, db = da * gq[:, :LANES], db * gq[:, LANES:]
            cc = jnp.sum(da * xa + db * xb, axis=-1, keepdims=True) * (1.0 / QK_DIM)
            dq_ref[:, lo:lo + LANES] = (r * (da - xa * cc)).astype(BF16)
            dq_ref[:, lo + LANES:lo + HEAD_PAD] = (r * (db - xb * cc)).astype(BF16)
            xa = kv_ref[:, lo:lo + LANES]
            ss = jnp.sum(xa * xa, axis=-1, keepdims=True) + kpe_ss
            r = lax.rsqrt(ss * (1.0 / QK_DIM) + EPS)
            xa, xb = xa * r, kpe * r
            da = dkf_ref[:, lo:lo + LANES].astype(F32)
            db = _rope_t(dkf_ref[:, lo + LANES:lo + HEAD_PAD].astype(F32), c, s1, s2)
            dgk_a += jnp.sum(da * xa, axis=0, keepdims=True)
            dgk_b += jnp.sum(db * xb, axis=0, keepdims=True)
            da, db = da * gk[:, :LANES], db * gk[:, LANES:]
            cc = jnp.sum(da * xa + db * xb, axis=-1, keepdims=True) * (1.0 / QK_DIM)
            dkv_ref[:, lo:lo + LANES] = (r * (da - xa * cc)).astype(BF16)
            dkpe = dkpe + r * (db - xb * cc)
            dkv_ref[:, lo + LANES:lo + HEAD_PAD] = dv_ref[:, h * V_DIM:(h + 1) * V_DIM].astype(BF16)
        dkpe_ref[...] = dkpe
        dgq_ref[:, :LANES] += dgq_a
        dgq_ref[:, LANES:] += dgq_b
        dgk_ref[:, :LANES] += dgk_a
        dgk_ref[:, LANES:] += dgk_b

    wide = pl.BlockSpec((tr, H * HEAD_PAD), lambda i: (i, 0))
    lane = pl.BlockSpec((tr, LANES), lambda i: (i, 0))
    gvec = pl.BlockSpec((1, HEAD_PAD), lambda i: (0, 0))
    vspec = pl.BlockSpec((tr, H * V_DIM), lambda i: (i, 0))
    return pl.pallas_call(
        body, name=name, grid=(T // tr,),
        in_specs=[wide, wide, pl.BlockSpec((tr, LANES), lambda i: (i, (Q_LORA + KV_LORA) // LANES)), gvec, gvec,
                  lane, lane, lane, wide, wide, vspec],
        out_specs=[wide, wide, lane, gvec, gvec],
        out_shape=[jax.ShapeDtypeStruct((T, H * HEAD_PAD), BF16), jax.ShapeDtypeStruct((T, H * HEAD_PAD), BF16),
                   jax.ShapeDtypeStruct((T, LANES), F32), jax.ShapeDtypeStruct((1, HEAD_PAD), F32),
                   jax.ShapeDtypeStruct((1, HEAD_PAD), F32)],
        compiler_params=_params(("arbitrary",)),
    )(q_raw, kv, lat, g_qn, g_kn, *tabs, dqf, dkf, dv)


def _causal_mask(tq, tk):
    return lax.broadcasted_iota(jnp.int32, (tq, tk), 1) <= lax.broadcasted_iota(jnp.int32, (tq, tk), 0)


NT_DIMS = (((1,), (1,)), ((), ()))
TN_DIMS = (((0,), (0,)), ((), ()))


def _flash_fwd(qf, kf, v, name):
    T = qf.shape[0]
    H, G = MLA_HEADS, FLASH_HEADS_FWD
    t = _pick(T, (512, 256, 128))
    n = T // t
    pairs = [(i, j) for i in range(n) for j in range(i + 1)]
    qi = jnp.asarray([p[0] for p in pairs], jnp.int32)
    kj = jnp.asarray([p[1] for p in pairs], jnp.int32)

    def body(qi_ref, kj_ref, q_ref, k_ref, v_ref, o_ref, lse_ref, *scratch):
        m_sc, acc_sc = scratch[:G], scratch[G:]
        sid = pl.program_id(1)
        i, j = qi_ref[sid], kj_ref[sid]

        @pl.when(j == 0)
        def _():
            for g in range(G):
                m_sc[g][...] = jnp.full_like(m_sc[g], NEG_BIG)
                acc_sc[g][...] = jnp.zeros_like(acc_sc[g])

        def step(masked):
            for g in range(G):
                qk = slice(g * HEAD_PAD, (g + 1) * HEAD_PAD)
                s = lax.dot_general(q_ref[:, qk], k_ref[:, qk], NT_DIMS, preferred_element_type=F32)
                if masked:
                    s = jnp.where(_causal_mask(t, t), s, NEG_BIG)
                m_prev = m_sc[g][:, :1]
                m_new = jnp.maximum(m_prev, jnp.max(s, axis=-1, keepdims=True))
                a = jnp.exp2(m_prev - m_new)
                p = jnp.exp2((s - m_new).astype(BF16))
                acc_sc[g][...] = a * acc_sc[g][...] + jnp.dot(p, v_ref[:, qk], preferred_element_type=F32)
                m_sc[g][...] = jnp.broadcast_to(m_new, (t, LANES))

        @pl.when(j < i)
        def _():
            step(False)

        @pl.when(j == i)
        def _():
            step(True)
            for g in range(G):
                vo = slice(g * V_DIM, (g + 1) * V_DIM)
                l = acc_sc[g][:, V_DIM:]
                o_ref[:, vo] = (acc_sc[g][:, :V_DIM] / l).astype(BF16)
                lse_ref[:, vo] = m_sc[g][...] + jnp.log2(l)

    row = pl.BlockSpec((t, G * V_DIM), lambda h, s, qi, kj: (qi[s], h))
    return pl.pallas_call(
        body, name=name,
        grid_spec=pltpu.PrefetchScalarGridSpec(
            num_scalar_prefetch=2, grid=(H // G, len(pairs)),
            in_specs=[pl.BlockSpec((t, G * HEAD_PAD), lambda h, s, qi, kj: (qi[s], h)),
                      pl.BlockSpec((t, G * HEAD_PAD), lambda h, s, qi, kj: (kj[s], h)),
                      pl.BlockSpec((t, G * HEAD_PAD), lambda h, s, qi, kj: (kj[s], h))],
            out_specs=[row, row],
            scratch_shapes=[pltpu.VMEM((t, LANES), F32)] * G + [pltpu.VMEM((t, HEAD_PAD), F32)] * G),
        out_shape=[jax.ShapeDtypeStruct((T, H * V_DIM), BF16), jax.ShapeDtypeStruct((T, H * V_DIM), F32)],
        compiler_params=_params(("parallel", "arbitrary")),
    )(qi, kj, qf, kf, v)


def _attn_delta(do, o, name):
    T, W = do.shape
    nh = W // V_DIM
    tr = _pick(T, (512, 256, 128))

    def body(do_ref, o_ref, d_ref):
        for h in range(nh):
            sl = slice(h * V_DIM, (h + 1) * V_DIM)
            d = jnp.sum(do_ref[:, sl].astype(F32) * o_ref[:, sl].astype(F32), axis=-1, keepdims=True)
            d_ref[:, sl] = jnp.broadcast_to(d, (tr, V_DIM))

    row = pl.BlockSpec((tr, W), lambda i: (i, 0))
    return pl.pallas_call(
        body, name=name, grid=(T // tr,), in_specs=[row, row], out_specs=row,
        out_shape=jax.ShapeDtypeStruct((T, W), F32), compiler_params=_params(("parallel",)),
    )(do, o)


def _flash_bwd(qf, kf, v, do, lse, delta, name):
    T = qf.shape[0]
    H, G = MLA_HEADS, FLASH_HEADS
    t = _pick(T, (512, 256, 128))
    n = T // t
    scale = 1.0 / math.sqrt(QK_DIM)
    pairs = [(i, j) for j in range(n) for i in range(j, n)]
    qi = jnp.asarray([p[0] for p in pairs], jnp.int32)
    kj = jnp.asarray([p[1] for p in pairs], jnp.int32)

    def body(qi_ref, kj_ref, q_ref, k_ref, v_ref, do_ref, lse_ref, dl_ref, dq_ref, dk_ref, dv_ref,
             dq_acc, dk_acc, dv_acc):
        sid = pl.program_id(1)
        i, j = qi_ref[sid], kj_ref[sid]

        @pl.when(sid == 0)
        def _():
            dq_acc[...] = jnp.zeros_like(dq_acc)

        def step(masked):
            rows = pl.ds(pl.multiple_of(i * t, t), t)
            for g in range(G):
                qk = slice(g * HEAD_PAD, (g + 1) * HEAD_PAD)
                vo = slice(g * V_DIM, (g + 1) * V_DIM)
                q, k, do_ = q_ref[:, qk], k_ref[:, qk], do_ref[:, vo]
                v_ = v_ref[:, g * HEAD_PAD:g * HEAD_PAD + V_DIM]
                s = lax.dot_general(q, k, NT_DIMS, preferred_element_type=F32)
                if masked:
                    s = jnp.where(_causal_mask(t, t), s, NEG_BIG)
                p = jnp.exp2(s - lse_ref[:, g * V_DIM:g * V_DIM + 1])
                dp = lax.dot_general(do_, v_, NT_DIMS, preferred_element_type=F32)
                ds = (p * (dp - dl_ref[:, g * V_DIM:g * V_DIM + 1])).astype(BF16)
                dv = lax.dot_general(p.astype(BF16), do_, TN_DIMS, preferred_element_type=F32)
                dk = lax.dot_general(ds, q, TN_DIMS, preferred_element_type=F32)
                if masked:
                    dv_acc[:, vo] = dv
                    dk_acc[:, qk] = dk
                else:
                    dv_acc[:, vo] += dv
                    dk_acc[:, qk] += dk
                dq_acc[rows, qk] += jnp.dot(ds, k, preferred_element_type=F32) * scale

        @pl.when(i == j)
        def _():
            step(True)

        @pl.when(i > j)
        def _():
            step(False)

        @pl.when(i == n - 1)
        def _():
            dk_ref[...] = (dk_acc[...] * LN2).astype(BF16)
            dv_ref[...] = dv_acc[...].astype(BF16)

        @pl.when(sid == len(pairs) - 1)
        def _():
            dq_ref[...] = dq_acc[...].astype(BF16)

    qs = pl.BlockSpec((t, G * HEAD_PAD), lambda h, s, qi, kj: (qi[s], h))
    rs = pl.BlockSpec((t, G * V_DIM), lambda h, s, qi, kj: (qi[s], h))
    ks = pl.BlockSpec((t, G * HEAD_PAD), lambda h, s, qi, kj: (kj[s], h))
    vs = pl.BlockSpec((t, G * V_DIM), lambda h, s, qi, kj: (kj[s], h))
    return pl.pallas_call(
        body, name=name,
        grid_spec=pltpu.PrefetchScalarGridSpec(
            num_scalar_prefetch=2, grid=(H // G, len(pairs)), in_specs=[qs, ks, ks, rs, rs, rs],
            out_specs=[pl.BlockSpec((T, G * HEAD_PAD), lambda h, s, qi, kj: (0, h)), ks, vs],
            scratch_shapes=[pltpu.VMEM((T, G * HEAD_PAD), F32), pltpu.VMEM((t, G * HEAD_PAD), F32),
                            pltpu.VMEM((t, G * V_DIM), F32)]),
        out_shape=[jax.ShapeDtypeStruct((T, H * HEAD_PAD), BF16), jax.ShapeDtypeStruct((T, H * HEAD_PAD), BF16),
                   jax.ShapeDtypeStruct((T, H * V_DIM), BF16)],
        compiler_params=_params(("parallel", "arbitrary")),
    )(qi, kj, qf, kf, v, do, lse, delta)


def _alibi_slopes():
    tot = DIL_GROUPS * DIL_HEADS
    return [float(np.float32(2.0) ** (np.float32(-8.0) * np.float32(k) / np.float32(tot))) for k in range(1, tot + 1)]


def _dil_masks():
    iq = lax.broadcasted_iota(jnp.int32, (DIL_BLK, DIL_BLK), 0)
    ik = lax.broadcasted_iota(jnp.int32, (DIL_BLK, DIL_BLK), 1)
    return (ik >= iq), (iq + DIL_BLK - ik).astype(F32), (ik <= iq), (iq - ik).astype(F32)


def _dil_norm(x, g):
    r = lax.rsqrt(jnp.mean(x * x, axis=-1, keepdims=True) + EPS)
    return x * r, r


DIL_SUPER = 8
BNT_DIMS = (((2,), (2,)), ((0,), (0,)))
BNN_DIMS = (((2,), (1,)), ((0,), (0,)))
BTN_DIMS = (((1,), (1,)), ((0,), (0,)))


def _dil_chunk(it, nb, d):
    assert nb & (nb - 1) == 0, nb
    r, n = it >> (nb.bit_length() - 1), it & (nb - 1)
    if d > 1:
        tok = pl.ds(n * (d * DIL_BLK) + r, DIL_BLK, stride=d)
    else:
        tok = pl.ds(pl.multiple_of(it * DIL_BLK, DIL_BLK), DIL_BLK)
    return tok, pl.ds(pl.multiple_of((it + 1) * DIL_BLK, DIL_BLK), DIL_BLK)


def _dil_token_rows(bidx, nb, d):
    r, n = divmod(bidx, nb)
    return pl.ds(n * DIL_BLK * d + r, DIL_BLK, stride=d) if d > 1 else pl.ds(bidx * DIL_BLK, DIL_BLK)


def _dil_super_rows(ss):
    base = (1 + ss * DIL_SUPER) * DIL_BLK
    return pl.ds(base, DIL_SUPER * DIL_BLK), pl.ds(base - DIL_BLK, DIL_SUPER * DIL_BLK)


def _dil_b3(x):
    return x.reshape(DIL_SUPER, DIL_BLK, x.shape[-1])


def _dil_scores(q3, kc3, kp3, slope, d, ss, nb):
    ok_p, dist_p, ok_c, dist_c = _dil_masks()
    scale = 1.0 / math.sqrt(DIL_HEAD_DIM)
    bias_p = jnp.where(ok_p, -slope * d * dist_p, NEG_BIG)
    bias_c = jnp.where(ok_c, -slope * d * dist_c, NEG_BIG)
    s_c = lax.dot_general(q3, kc3, BNT_DIMS, preferred_element_type=F32) * scale + bias_c[None]
    s_p = lax.dot_general(q3, kp3, BNT_DIMS, preferred_element_type=F32) * scale + bias_p[None]
    bidx = ss * DIL_SUPER + lax.broadcasted_iota(jnp.int32, s_p.shape, 0)
    s_p = jnp.where((bidx & (nb - 1)) == 0, NEG_BIG, s_p)
    return s_c, s_p


def _dil_fwd(qkv, g_qn, g_kn, slopes, name):
    T = qkv.shape[0]
    GH = DIL_GROUPS * DIL_HEADS
    scale = 1.0 / math.sqrt(DIL_HEAD_DIM)

    def body(sl_ref, q_ref, k_ref, v_ref, gq_ref, gk_ref, o_ref, lse_ref, qn_pm, kn_pm, v_pm):
        gh = pl.program_id(0)
        slope = sl_ref[gh]
        gq, gk = gq_ref[...], gk_ref[...]
        pad = pl.ds(0, DIL_BLK)
        kn_pm[pad, :] = jnp.zeros((DIL_BLK, DIL_HEAD_DIM), BF16)
        v_pm[pad, :] = jnp.zeros((DIL_BLK, DIL_HEAD_DIM), BF16)
        for g, (_, d) in enumerate(DIL_PAIRS):
            @pl.when((gh >= g * DIL_HEADS) & (gh < (g + 1) * DIL_HEADS))
            def _(d=d):
                nb = T // (d * DIL_BLK)

                def fill(it, _):
                    tok, dst = _dil_chunk(it, nb, d)
                    qn_pm[dst, :] = (_dil_norm(q_ref[tok, :], gq)[0] * gq).astype(BF16)
                    kn_pm[dst, :] = (_dil_norm(k_ref[tok, :], gk)[0] * gk).astype(BF16)
                    v_pm[dst, :] = v_ref[tok, :].astype(BF16)
                    return 0
                lax.fori_loop(0, T // DIL_BLK, fill, 0, unroll=4)

                for ss in range(T // DIL_BLK // DIL_SUPER):
                    cur, prv = _dil_super_rows(ss)
                    q3, kc3, kp3 = _dil_b3(qn_pm[cur, :]), _dil_b3(kn_pm[cur, :]), _dil_b3(kn_pm[prv, :])
                    s_c, s_p = _dil_scores(q3, kc3, kp3, slope, d, ss, nb)
                    m = jnp.max(jnp.maximum(s_c, s_p), axis=-1, keepdims=True)
                    p_c = jnp.exp(s_c - m)
                    p_p = jnp.exp(s_p - m)
                    l = jnp.sum(p_c, axis=-1, keepdims=True) + jnp.sum(p_p, axis=-1, keepdims=True)
                    acc = lax.dot_general(p_c.astype(BF16), _dil_b3(v_pm[cur, :]), BNN_DIMS, preferred_element_type=F32)
                    acc += lax.dot_general(p_p.astype(BF16), _dil_b3(v_pm[prv, :]), BNN_DIMS, preferred_element_type=F32)
                    o3 = acc / l
                    lse3 = jnp.broadcast_to(m + jnp.log(l), o3.shape)
                    for b in range(DIL_SUPER):
                        tok = _dil_token_rows(ss * DIL_SUPER + b, nb, d)
                        o_ref[tok, :] = o3[b]
                        lse_ref[tok, :] = lse3[b]

    col = lambda off: pl.BlockSpec((T, DIL_HEAD_DIM), lambda gh, sl: (0, gh + off))
    gvec = pl.BlockSpec((1, DIL_HEAD_DIM), lambda gh, sl: (0, 0))
    return pl.pallas_call(
        body, name=name,
        grid_spec=pltpu.PrefetchScalarGridSpec(
            num_scalar_prefetch=1, grid=(GH,),
            in_specs=[col(0), col(GH), col(2 * GH), gvec, gvec], out_specs=[col(0), col(0)],
            scratch_shapes=[pltpu.VMEM((DIL_BLK + T, DIL_HEAD_DIM), BF16)] * 3),
        out_shape=[jax.ShapeDtypeStruct((T, GH * DIL_HEAD_DIM), F32)] * 2,
        compiler_params=_params(("parallel",)),
    )(slopes, qkv, qkv, qkv, g_qn, g_kn)


def _dil_merge(o_g, lse_g, name):
    T = o_g.shape[0]
    W = DIL_HEADS * DIL_HEAD_DIM
    tr = _pick(T, (256, 128))

    def body(o0, o1, o2, l0, l1, l2, o_ref, lse_ref):
        a, b, c = l0[...], l1[...], l2[...]
        m = jnp.maximum(jnp.maximum(a, b), c)
        ea, eb, ec = jnp.exp(a - m), jnp.exp(b - m), jnp.exp(c - m)
        tot = ea + eb + ec
        o_ref[...] = ((o0[...] * ea + o1[...] * eb + o2[...] * ec) / tot).astype(BF16)
        lse_ref[...] = m + jnp.log(tot)

    grp = lambda g: pl.BlockSpec((tr, W), lambda i: (i, g))
    out = pl.BlockSpec((tr, W), lambda i: (i, 0))
    return pl.pallas_call(
        body, name=name, grid=(T // tr,),
        in_specs=[grp(0), grp(1), grp(2), grp(0), grp(1), grp(2)], out_specs=[out, out],
        out_shape=[jax.ShapeDtypeStruct((T, W), BF16), jax.ShapeDtypeStruct((T, W), F32)],
        compiler_params=_params(("parallel",)),
    )(o_g, o_g, o_g, lse_g, lse_g, lse_g)


def _dil_bwd(qkv, g_qn, g_kn, slopes, do, delta, lse, name):
    T = qkv.shape[0]
    GH = DIL_GROUPS * DIL_HEADS
    scale = 1.0 / math.sqrt(DIL_HEAD_DIM)
    nchunk = T // DIL_BLK

    def body(sl_ref, q_ref, k_ref, v_ref, gq_ref, gk_ref, do_ref, dl_ref, lse_ref,
             dq_ref, dk_ref, dv_ref, dgq_ref, dgk_ref,
             qn_pm, kn_pm, v_pm, do_pm, lse_pm, dl_pm, dq_pm, dk_pm, dv_pm, tok_sc):
        gh = pl.program_id(0)
        slope = sl_ref[gh]
        gq, gk = gq_ref[...], gk_ref[...]

        @pl.when(gh == 0)
        def _():
            dgq_ref[...] = jnp.zeros_like(dgq_ref)
            dgk_ref[...] = jnp.zeros_like(dgk_ref)

        pad = pl.ds(0, DIL_BLK)
        kn_pm[pad, :] = jnp.zeros((DIL_BLK, DIL_HEAD_DIM), BF16)
        v_pm[pad, :] = jnp.zeros((DIL_BLK, DIL_HEAD_DIM), BF16)
        dk_pm[...] = jnp.zeros_like(dk_pm)
        dv_pm[...] = jnp.zeros_like(dv_pm)
        for g, (_, d) in enumerate(DIL_PAIRS):
            @pl.when((gh >= g * DIL_HEADS) & (gh < (g + 1) * DIL_HEADS))
            def _(d=d):
                nb = T // (d * DIL_BLK)

                def fill(it, _):
                    tok, dst = _dil_chunk(it, nb, d)
                    qn_pm[dst, :] = (_dil_norm(q_ref[tok, :], gq)[0] * gq).astype(BF16)
                    kn_pm[dst, :] = (_dil_norm(k_ref[tok, :], gk)[0] * gk).astype(BF16)
                    v_pm[dst, :] = v_ref[tok, :].astype(BF16)
                    do_pm[dst, :] = do_ref[tok, :].astype(BF16)
                    lse_pm[dst, :] = lse_ref[tok, :]
                    dl_pm[dst, :] = dl_ref[tok, :]
                    return 0
                lax.fori_loop(0, nchunk, fill, 0, unroll=4)

                for ss in range(nchunk // DIL_SUPER):
                    cur, prv = _dil_super_rows(ss)
                    q3, kc3, kp3 = _dil_b3(qn_pm[cur, :]), _dil_b3(kn_pm[cur, :]), _dil_b3(kn_pm[prv, :])
                    vc3, vp3, do3 = _dil_b3(v_pm[cur, :]), _dil_b3(v_pm[prv, :]), _dil_b3(do_pm[cur, :])
                    ls = _dil_b3(lse_pm[cur, :])[:, :, :1]
                    delta = _dil_b3(dl_pm[cur, :])[:, :, :1]
                    s_c, s_p = _dil_scores(q3, kc3, kp3, slope, d, ss, nb)
                    p_c = jnp.exp(s_c - ls)
                    p_p = jnp.exp(s_p - ls)
                    dp_c = lax.dot_general(do3, vc3, BNT_DIMS, preferred_element_type=F32)
                    dp_p = lax.dot_general(do3, vp3, BNT_DIMS, preferred_element_type=F32)
                    ds_c = (p_c * (dp_c - delta)).astype(BF16)
                    ds_p = (p_p * (dp_p - delta)).astype(BF16)
                    dq3 = (lax.dot_general(ds_c, kc3, BNN_DIMS, preferred_element_type=F32)
                           + lax.dot_general(ds_p, kp3, BNN_DIMS, preferred_element_type=F32)) * scale
                    flat = lambda x: x.reshape(DIL_SUPER * DIL_BLK, DIL_HEAD_DIM)
                    dq_pm[cur, :] = flat(dq3)
                    dk_pm[cur, :] += flat(lax.dot_general(ds_c, q3, BTN_DIMS, preferred_element_type=F32)) * scale
                    dv_pm[cur, :] += flat(lax.dot_general(p_c.astype(BF16), do3, BTN_DIMS, preferred_element_type=F32))
                    dk_pm[prv, :] += flat(lax.dot_general(ds_p, q3, BTN_DIMS, preferred_element_type=F32)) * scale
                    dv_pm[prv, :] += flat(lax.dot_general(p_p.astype(BF16), do3, BTN_DIMS, preferred_element_type=F32))

                def to_tokens(src_pm):
                    def move(it, _):
                        tok, src = _dil_chunk(it, nb, d)
                        tok_sc[tok, :] = src_pm[src, :]
                        return 0
                    lax.fori_loop(0, nchunk, move, 0, unroll=4)

                def norm_bwd(x_ref, gvec, out_ref):
                    big = 4 * DIL_BLK

                    def fin(ci, dg):
                        rows = pl.ds(pl.multiple_of(ci * big, big), big)
                        xhat, r = _dil_norm(x_ref[rows, :], gvec)
                        dn = tok_sc[rows, :]
                        dxh = dn * gvec
                        c = jnp.mean(dxh * xhat, axis=-1, keepdims=True)
                        out_ref[rows, :] = (r * (dxh - xhat * c)).astype(BF16)
                        return dg + jnp.sum(dn * xhat, axis=0, keepdims=True)
                    return lax.fori_loop(0, T // big, fin, jnp.zeros((1, DIL_HEAD_DIM), F32))

                to_tokens(dq_pm)
                dgq_ref[...] += norm_bwd(q_ref, gq, dq_ref)
                to_tokens(dk_pm)
                dgk_ref[...] += norm_bwd(k_ref, gk, dk_ref)
                to_tokens(dv_pm)
                dv_ref[...] = tok_sc[...].astype(BF16)

    col = lambda off: pl.BlockSpec((T, DIL_HEAD_DIM), lambda gh, sl: (0, gh + off))
    hcol = pl.BlockSpec((T, DIL_HEAD_DIM), lambda gh, sl: (0, gh % DIL_HEADS))
    gvec = pl.BlockSpec((1, DIL_HEAD_DIM), lambda gh, sl: (0, 0))
    wide = jax.ShapeDtypeStruct((T, GH * DIL_HEAD_DIM), BF16)
    vec = jax.ShapeDtypeStruct((1, DIL_HEAD_DIM), F32)
    pm = lambda dt: pltpu.VMEM((DIL_BLK + T, DIL_HEAD_DIM), dt)
    return pl.pallas_call(
        body, name=name,
        grid_spec=pltpu.PrefetchScalarGridSpec(
            num_scalar_prefetch=1, grid=(GH,),
            in_specs=[col(0), col(GH), col(2 * GH), gvec, gvec, hcol, hcol, hcol],
            out_specs=[col(0), col(0), col(0), gvec, gvec],
            scratch_shapes=[pm(BF16)] * 4 + [pm(F32)] * 5 + [pltpu.VMEM((T, DIL_HEAD_DIM), F32)]),
        out_shape=[wide, wide, wide, vec, vec],
        compiler_params=_params(("arbitrary",)),
    )(slopes, qkv, qkv, qkv, g_qn, g_kn, do, delta, lse)


def _my_pos():
    return lax.axis_index("x"), lax.axis_index("y"), lax.axis_index("c")


def _peer(pos, j):
    x, y, c = pos
    px = 1 - x if j & 4 else x
    py = 1 - y if j & 2 else y
    pc = 1 - c if j & 1 else c
    return (px, py, pc), 4 * px + 2 * py + pc


def _slot(idx, paired):
    if not paired:
        return idx
    return jnp.where(idx < N_DEV // 2, 2 * idx, 2 * idx - (N_DEV - 1))


def _shard_slice(ref, axis, idx, size, paired=False):
    sl = [slice(None)] * len(ref.shape)
    sl[axis] = pl.ds(pl.multiple_of(_slot(idx, paired) * size, 8), size)
    return ref.at[tuple(sl)]


HBM_SPEC = pl.BlockSpec(memory_space=pltpu.HBM)
SEM_SPEC = pl.BlockSpec(memory_space=pltpu.SEMAPHORE)
DATAFLOW = pltpu.SideEffectType.DATAFLOW_SIDE_EFFECTING
N_PEER = N_DEV - 1


def _scatter_copy(axis, grad, slots, frm, to, dev, send_sem, recv_sem):
    ax, paired = axis
    src = _shard_slice(grad, ax, to, grad.shape[ax] // N_DEV, paired)
    return pltpu.make_async_remote_copy(src_ref=src, dst_ref=slots.at[frm], send_sem=send_sem, recv_sem=recv_sem,
                                        device_id=dev, device_id_type=MESH)


def _scatter_start(grads, axes, name):
    n = len(grads)

    def body(*refs):
        outs = refs[2 * n:]
        send, recv, token = outs[:n], outs[n:2 * n], outs[4 * n]
        pos = _my_pos()
        me = 4 * pos[0] + 2 * pos[1] + pos[2]
        for a in range(n):
            for j in range(1, N_DEV):
                dev, pid = _peer(pos, j)
                _scatter_copy(axes[a], refs[2 * a], refs[2 * a + 1], me, pid, dev, send[a].at[j - 1],
                              recv[a].at[j - 1]).start()
        token[...] = jnp.zeros_like(token)

    ops = []
    for g, (ax, _) in zip(grads, axes):
        shp = list(g.shape)
        shp[ax] //= N_DEV
        ops += [g, lax.empty((N_DEV,) + tuple(shp), g.dtype)]
    sems = [pltpu.SemaphoreType.DMA((N_PEER,))] * (2 * n)
    res = pl.pallas_call(
        body, name=name,
        out_shape=sems + [pltpu.HBM(o.shape, o.dtype) for o in ops] + [jax.ShapeDtypeStruct((8, LANES), F32)],
        in_specs=[HBM_SPEC] * len(ops),
        out_specs=[SEM_SPEC] * (2 * n) + [HBM_SPEC] * len(ops) + [pl.BlockSpec(memory_space=pltpu.VMEM)],
        input_output_aliases={i: 2 * n + i for i in range(len(ops))},
        compiler_params=pltpu.CompilerParams(has_side_effects=DATAFLOW),
    )(*[pltpu.with_memory_space_constraint(o, pltpu.HBM) for o in ops])
    items = [(res[a], res[n + a], res[2 * n + 2 * a], res[2 * n + 2 * a + 1]) for a in range(n)]
    return items, res[4 * n]


def _scatter_wait(items, axes, after, name):
    n = len(items)

    def body(*refs):
        send, recv = refs[2 * n:3 * n], refs[3 * n:4 * n]
        pos = _my_pos()
        me = 4 * pos[0] + 2 * pos[1] + pos[2]
        for a in range(n):
            for j in range(1, N_DEV):
                dev, pid = _peer(pos, j)
                cp = _scatter_copy(axes[a], refs[2 * a], refs[2 * a + 1], pid, me, dev, send[a].at[j - 1],
                                   recv[a].at[j - 1])
                cp.wait_send()
                cp.wait_recv()

    ops = [b for it in items for b in it[2:]]
    res = pl.pallas_call(
        body, name=name,
        out_shape=[pltpu.HBM(o.shape, o.dtype) for o in ops],
        in_specs=[HBM_SPEC] * len(ops) + [SEM_SPEC] * (2 * n) + [ANY_SPEC],
        out_specs=[HBM_SPEC] * len(ops),
        input_output_aliases={i: i for i in range(len(ops))},
        compiler_params=pltpu.CompilerParams(has_side_effects=DATAFLOW),
    )(*ops, *[it[0] for it in items], *[it[1] for it in items], after)
    return [(res[2 * a], res[2 * a + 1]) for a in range(n)]


SIBLING = 1
ICI_PEERS = (2, 4, 6)


def _gather_copy(buf, axis, shard, dev, send_sem, recv_sem):
    ax, paired = axis
    piece = _shard_slice(buf, ax, shard, buf.shape[ax] // N_DEV, paired)
    return pltpu.make_async_remote_copy(src_ref=piece, dst_ref=piece, send_sem=send_sem, recv_sem=recv_sem,
                                        device_id=dev, device_id_type=MESH)


def _gather_start(bufs, axes, name):
    n = len(bufs)

    def body(*refs):
        ins, outs = refs[:n], refs[n:]
        send, r_sib, r_ici, token = outs[:n], outs[n:2 * n], outs[2 * n:3 * n], outs[4 * n]
        pos = _my_pos()
        me = 4 * pos[0] + 2 * pos[1] + pos[2]
        for a in range(n):
            dev, _ = _peer(pos, SIBLING)
            _gather_copy(ins[a], axes[a], me, dev, send[a].at[0], r_sib[a].at[0]).start()
            for k, j in enumerate(ICI_PEERS):
                dev, _ = _peer(pos, j)
                _gather_copy(ins[a], axes[a], me, dev, send[a].at[1 + k], r_ici[a].at[k]).start()
        token[...] = jnp.zeros_like(token)

    sems = ([pltpu.SemaphoreType.DMA((1 + len(ICI_PEERS),))] * n + [pltpu.SemaphoreType.DMA((1,))] * n
            + [pltpu.SemaphoreType.DMA((len(ICI_PEERS),))] * n)
    res = pl.pallas_call(
        body, name=name,
        out_shape=sems + [pltpu.HBM(b.shape, b.dtype) for b in bufs] + [jax.ShapeDtypeStruct((8, LANES), F32)],
        in_specs=[HBM_SPEC] * n,
        out_specs=[SEM_SPEC] * (3 * n) + [HBM_SPEC] * n + [pl.BlockSpec(memory_space=pltpu.VMEM)],
        input_output_aliases={i: 3 * n + i for i in range(n)},
        compiler_params=pltpu.CompilerParams(has_side_effects=DATAFLOW),
    )(*[pltpu.with_memory_space_constraint(b, pltpu.HBM) for b in bufs])
    items = [dict(send=res[a], r_sib=res[n + a], r_ici=res[2 * n + a], buf=res[3 * n + a]) for a in range(n)]
    return items, res[4 * n]


def _gather_relay(items, axes, after, name):
    n = len(items)

    def body(*refs):
        ins, r_ici = refs[:n], refs[n:2 * n]
        outs = refs[2 * n + 1:]
        s_rel, r_rel, token = outs[:n], outs[n:2 * n], outs[3 * n]
        pos = _my_pos()
        sib, _ = _peer(pos, SIBLING)
        for a in range(n):
            for k, j in enumerate(ICI_PEERS):
                dev, pid = _peer(pos, j)
                _gather_copy(ins[a], axes[a], pid, dev, s_rel[a].at[k], r_ici[a].at[k]).wait_recv()
                _gather_copy(ins[a], axes[a], pid, sib, s_rel[a].at[k], r_rel[a].at[k]).start()
        token[...] = jnp.zeros_like(token)

    bufs = [it["buf"] for it in items]
    sems = [pltpu.SemaphoreType.DMA((len(ICI_PEERS),))] * (2 * n)
    res = pl.pallas_call(
        body, name=name,
        out_shape=sems + [pltpu.HBM(b.shape, b.dtype) for b in bufs] + [jax.ShapeDtypeStruct((8, LANES), F32)],
        in_specs=[HBM_SPEC] * n + [SEM_SPEC] * n + [ANY_SPEC],
        out_specs=[SEM_SPEC] * (2 * n) + [HBM_SPEC] * n + [pl.BlockSpec(memory_space=pltpu.VMEM)],
        input_output_aliases={i: 2 * n + i for i in range(n)},
        compiler_params=pltpu.CompilerParams(has_side_effects=DATAFLOW),
    )(*bufs, *[it["r_ici"] for it in items], after)
    out = [dict(send=it["send"], r_sib=it["r_sib"], s_rel=res[a], r_rel=res[n + a], buf=res[2 * n + a])
           for a, it in enumerate(items)]
    return out, res[3 * n]


def _gather_wait(items, axes, after, name):
    n = len(items)

    def body(*refs):
        ins = refs[:n]
        send, r_sib, s_rel, r_rel = (refs[(1 + q) * n:(2 + q) * n] for q in range(4))
        pos = _my_pos()
        me = 4 * pos[0] + 2 * pos[1] + pos[2]
        sib, sib_id = _peer(pos, SIBLING)
        for a in range(n):
            for k in range(1 + len(ICI_PEERS)):
                _gather_copy(ins[a], axes[a], me, sib, send[a].at[k], r_sib[a].at[0]).wait_send()
            _gather_copy(ins[a], axes[a], sib_id, sib, send[a].at[0], r_sib[a].at[0]).wait_recv()
            for k, j in enumerate(ICI_PEERS):
                _, pid = _peer(pos, j)
                _, far = _peer(pos, j ^ SIBLING)
                _gather_copy(ins[a], axes[a], pid, sib, s_rel[a].at[k], r_rel[a].at[k]).wait_send()
                _gather_copy(ins[a], axes[a], far, sib, s_rel[a].at[k], r_rel[a].at[k]).wait_recv()

    bufs = [it["buf"] for it in items]
    res = pl.pallas_call(
        body, name=name,
        out_shape=[pltpu.HBM(b.shape, b.dtype) for b in bufs],
        in_specs=[HBM_SPEC] * n + [SEM_SPEC] * (4 * n) + [ANY_SPEC],
        out_specs=[HBM_SPEC] * n,
        input_output_aliases={i: i for i in range(n)},
        compiler_params=pltpu.CompilerParams(has_side_effects=DATAFLOW),
    )(*bufs, *[it["send"] for it in items], *[it["r_sib"] for it in items], *[it["s_rel"] for it in items],
      *[it["r_rel"] for it in items], after)
    return list(res)


def _gain_allreduce(v, name):
    n = v.shape[1]

    def body(v_ref, o_ref, slots, send_sems, recv_sems):
        pos = _my_pos()
        me = 4 * pos[0] + 2 * pos[1] + pos[2]
        slots[me] = v_ref[...]
        copies = []
        for j in range(1, N_DEV):
            dev, _ = _peer(pos, j)
            cp = pltpu.make_async_remote_copy(
                src_ref=slots.at[me], dst_ref=slots.at[me], send_sem=send_sems.at[j], recv_sem=recv_sems.at[j],
                device_id=dev, device_id_type=MESH)
            cp.start()
            copies.append(cp)
        for j in range(1, N_DEV):
            dev, pid = _peer(pos, j)
            pltpu.make_async_remote_copy(
                src_ref=slots.at[me], dst_ref=slots.at[pid], send_sem=send_sems.at[j], recv_sem=recv_sems.at[j],
                device_id=dev, device_id_type=MESH).wait_recv()
        for cp in copies:
            cp.wait_send()
        acc = slots[0]
        for s in range(1, N_DEV):
            acc = acc + slots[s]
        o_ref[...] = acc

    return pl.pallas_call(
        body, name=name, out_shape=jax.ShapeDtypeStruct((1, n), F32),
        in_specs=[pl.BlockSpec(memory_space=pltpu.VMEM)], out_specs=pl.BlockSpec(memory_space=pltpu.VMEM),
        scratch_shapes=[pltpu.VMEM((N_DEV, 1, n), F32), pltpu.SemaphoreType.DMA((N_DEV,)),
                        pltpu.SemaphoreType.DMA((N_DEV,))],
        compiler_params=pltpu.CompilerParams(has_side_effects=True),
    )(v)


def _adamw(parts, own, me, w, m, v, layer, prev, name, own_axis=None):
    L, R, C = w.shape
    P = parts.shape[0]
    tr = _pick(R, (128, 64, 32, 16, 8, 1))
    c1 = 1.0 - ADAM_B1 ** ADAM_STEP
    c2 = 1.0 - ADAM_B2 ** ADAM_STEP
    n_in = 4 if own is None else 5

    def body(me_ref, *refs):
        p_ref = refs[0]
        w_ref, m_ref, v_ref = refs[n_in - 3:n_in]
        g_out, d_out, m_out, v_out, tok = refs[-5:]
        g = None
        for s in range(P):
            part = p_ref[s]
            if own is not None:
                part = jnp.where(me_ref[0] == s, refs[1][...], part)
            g = part.astype(F32) if g is None else g + part.astype(F32)
        mn = ADAM_B1 * m_ref[...] + (1.0 - ADAM_B1) * g
        vn = ADAM_B2 * v_ref[...] + (1.0 - ADAM_B2) * (g * g)
        g_out[...] = g
        m_out[...] = mn
        v_out[...] = vn
        d_out[...] = -ADAM_LR * ((mn / c1) / (jnp.sqrt(vn / c2) + ADAM_EPS) + ADAM_WD * w_ref[...])
        tok[...] = jnp.zeros_like(tok)

    row = pl.BlockSpec((None, tr, C), lambda i, me_ref: (layer, i, 0))
    in_specs = [pl.BlockSpec((P, tr, C), lambda i, me_ref: (0, i, 0))]
    args = [parts]
    if own is not None:
        if own_axis is None:
            own_idx = lambda i, me_ref: (i, 0)
        elif own_axis[0] == 0:
            own_idx = lambda i, me_ref: (_slot(me_ref[0], own_axis[1]) * (R // tr) + i, 0)
        else:
            own_idx = lambda i, me_ref: (i, _slot(me_ref[0], own_axis[1]))
        in_specs.append(pl.BlockSpec((tr, C), own_idx))
        args.append(own)
    in_specs += [row, row, row]
    args += [w, m, v]
    aliases = {}
    if prev is not None:
        in_specs += [ANY_SPEC] * 4
        aliases = {1 + len(args) + k: k for k in range(4)}
        args += list(prev)
    shp = jax.ShapeDtypeStruct((L, R, C), F32)
    res = pl.pallas_call(
        body, name=name,
        grid_spec=pltpu.PrefetchScalarGridSpec(
            num_scalar_prefetch=1, grid=(R // tr,), in_specs=in_specs,
            out_specs=[row] * 4 + [pl.BlockSpec((8, LANES), lambda i, me_ref: (0, 0))]),
        out_shape=[shp] * 4 + [jax.ShapeDtypeStruct((8, LANES), F32)],
        input_output_aliases=aliases, compiler_params=_params(("arbitrary",)),
    )(me, *args)
    return res[:4], res[4]


def _pad_heads(w):
    lead = w.shape[:-1]
    n = w.shape[-1] // QK_DIM
    w = w.reshape(lead + (n, QK_DIM))
    w = jnp.pad(w, [(0, 0)] * len(lead) + [(0, 0), (0, HEAD_PAD - QK_DIM)])
    return w.reshape(lead + (n * HEAD_PAD,))


def _unpad_heads(w):
    lead = w.shape[:-1]
    n = w.shape[-1] // HEAD_PAD
    return w.reshape(lead + (n, HEAD_PAD))[..., :QK_DIM].reshape(lead + (n * QK_DIM,))


def kernel(x, ffn1_norm, ffn1_w_in, ffn1_w_out, mix_norm, ffn2_norm, ffn2_w_in, ffn2_w_out, mla_w_down, mla_g_cq, mla_g_ckv, mla_w_uq, mla_w_ukv, mla_g_qn, mla_g_kn, mla_w_o, dil_w_qkv, dil_g_qn, dil_g_kn, dil_w_o, loss_target, m_ffn1_norm, m_ffn1_w_in, m_ffn1_w_out, m_mix_norm, m_ffn2_norm, m_ffn2_w_in, m_ffn2_w_out, m_mla_w_down, m_mla_g_cq, m_mla_g_ckv, m_mla_w_uq, m_mla_w_ukv, m_mla_g_qn, m_mla_g_kn, m_mla_w_o, m_dil_w_qkv, m_dil_g_qn, m_dil_g_kn, m_dil_w_o, v_ffn1_norm, v_ffn1_w_in, v_ffn1_w_out, v_mix_norm, v_ffn2_norm, v_ffn2_w_in, v_ffn2_w_out, v_mla_w_down, v_mla_g_cq, v_mla_g_ckv, v_mla_w_uq, v_mla_w_ukv, v_mla_g_qn, v_mla_g_kn, v_mla_w_o, v_dil_w_qkv, v_dil_g_qn, v_dil_g_kn, v_dil_w_o):
    names = ["ffn1_norm", "ffn1_w_in", "ffn1_w_out", "mix_norm", "ffn2_norm", "ffn2_w_in", "ffn2_w_out", "mla_w_down",
             "mla_g_cq", "mla_g_ckv", "mla_w_uq", "mla_w_ukv", "mla_g_qn", "mla_g_kn", "mla_w_o", "dil_w_qkv",
             "dil_g_qn", "dil_g_kn", "dil_w_o"]
    W = dict(zip(names, [ffn1_norm, ffn1_w_in, ffn1_w_out, mix_norm, ffn2_norm, ffn2_w_in, ffn2_w_out, mla_w_down,
                         mla_g_cq, mla_g_ckv, mla_w_uq, mla_w_ukv, mla_g_qn, mla_g_kn, mla_w_o, dil_w_qkv,
                         dil_g_qn, dil_g_kn, dil_w_o]))
    M1 = dict(zip(names, [m_ffn1_norm, m_ffn1_w_in, m_ffn1_w_out, m_mix_norm, m_ffn2_norm, m_ffn2_w_in, m_ffn2_w_out,
                          m_mla_w_down, m_mla_g_cq, m_mla_g_ckv, m_mla_w_uq, m_mla_w_ukv, m_mla_g_qn, m_mla_g_kn,
                          m_mla_w_o, m_dil_w_qkv, m_dil_g_qn, m_dil_g_kn, m_dil_w_o]))
    V2 = dict(zip(names, [v_ffn1_norm, v_ffn1_w_in, v_ffn1_w_out, v_mix_norm, v_ffn2_norm, v_ffn2_w_in, v_ffn2_w_out,
                          v_mla_w_down, v_mla_g_cq, v_mla_g_ckv, v_mla_w_uq, v_mla_w_ukv, v_mla_g_qn, v_mla_g_kn,
                          v_mla_w_o, v_dil_w_qkv, v_dil_g_qn, v_dil_g_kn, v_dil_w_o]))
    S, D = x.shape[1], x.shape[2]
    x0 = x.reshape(S, D)
    tgt = loss_target.reshape(S, D)

    big = ["ffn1_w_in", "ffn1_w_out", "ffn2_w_in", "ffn2_w_out", "mla_w_down", "mla_w_uq", "mla_w_ukv", "mla_w_o",
           "dil_w_qkv", "dil_w_o"]
    shard_dim = {"ffn1_w_in": 2, "ffn1_w_out": 1, "ffn2_w_in": 2, "ffn2_w_out": 1, "mla_w_down": 1, "mla_w_uq": 2,
                 "mla_w_ukv": 2, "mla_w_o": 1, "dil_w_qkv": 2, "dil_w_o": 2}
    paired = ("ffn1_w_in", "ffn2_w_in")
    shard_axis = {n: (d, n in paired) for n, d in shard_dim.items()}
    grad_axis = {n: (d - 1, n in paired) for n, d in shard_dim.items()}

    def padded(n, w):
        if n == "mla_w_down":
            return jnp.pad(w, ((0, 0), (0, 0), (0, LAT_PAD - w.shape[2])))
        if n == "mla_w_uq":
            return _pad_heads(w)
        return w

    depth = ffn1_norm.shape[0]
    blocks = []
    for l in range(depth):
        mixer = (["mla_w_down", "mla_w_uq", "mla_w_ukv", "mla_w_o"] if l % 2 == 0 else ["dil_w_qkv", "dil_w_o"])
        blocks.append((f"ffn1_{l}", [("ffn1_w_in", l), ("ffn1_w_out", l)]))
        blocks.append((f"mix_{l}", [(n, l // 2) for n in mixer]))
        blocks.append((f"ffn2_{l}", [("ffn2_w_in", l), ("ffn2_w_out", l)]))
    order = [k for _, keys in blocks for k in keys]
    me = (4 * lax.axis_index("x") + 2 * lax.axis_index("y") + lax.axis_index("c")).astype(jnp.int32).reshape(1)
    def cast(key, deps=()):
        n, l = key
        return _cast_into_gathered(padded(n, W[n]), l, shard_axis[n], me, f"cast_{n}_{l}", deps=deps)

    items0, token0 = _gather_start([cast(order[0])], [shard_axis[order[0][0]]], "gather_start_first")
    rest = order[1:]
    items1, ag_token = _gather_start([cast(k, deps=[token0]) for k in rest], [shard_axis[k[0]] for k in rest],
                                     "gather_start_rest")
    ag_items = dict(zip(order, items0 + items1))
    full = {}

    def relay(keys, after, tag):
        out, token = _gather_relay([ag_items[k] for k in keys], [shard_axis[k[0]] for k in keys], after,
                                   f"gather_relay_{tag}")
        ag_items.update(zip(keys, out))
        return [token]

    def relay_next(bi, after):
        return relay(blocks[bi + 1][1], after, blocks[bi + 1][0]) if bi + 1 < len(blocks) else []

    def fetch(keys, after, tag):
        lands = _gather_wait([ag_items[k] for k in keys], [shard_axis[k[0]] for k in keys], after,
                             f"gather_wait_{tag}")
        full.update(zip(keys, lands))

    g_qn = _pad_heads(mla_g_qn)
    g_kn = _pad_heads(mla_g_kn)
    tabs = _rope_tables(S)
    slopes = jnp.asarray(_alibi_slopes(), F32)

    grads = {}
    gain_g = {}

    out_g, out_d, out_m, out_v = {}, {}, {}, {}
    pending = []
    lag = 3

    def scatter_start(tag, keys):
        items, token = _scatter_start([grads[k] for k in keys], [grad_axis[k[0]] for k in keys],
                                      f"scatter_start_{tag}")
        pending.append((tag, keys, items))
        return token

    def scatter_finish(after):
        tag, keys, items = pending.pop(0)
        lands = _scatter_wait(items, [grad_axis[k[0]] for k in keys], after, f"scatter_wait_{tag}")
        tokens = []
        for (n, l), (own, p) in zip(keys, lands):
            own_axis = grad_axis[n]
            if n in ("mla_w_down", "mla_w_uq"):
                ax, pair = grad_axis[n]
                size = own.shape[ax] // N_DEV
                own = lax.dynamic_slice_in_dim(own, _slot(me[0], pair) * size, size, axis=ax)
                own_axis = None
                if n == "mla_w_down":
                    p, own = p[..., :W[n].shape[2]], own[..., :W[n].shape[2]]
                else:
                    p, own = _unpad_heads(p), _unpad_heads(own)
            prev = (out_g[n], out_d[n], out_m[n], out_v[n]) if n in out_g else None
            (out_g[n], out_d[n], out_m[n], out_v[n]), tok = _adamw(p, own, me, W[n], M1[n], V2[n], l, prev,
                                                                    f"adamw_{n}_{l}", own_axis=own_axis)
            tokens.append(tok)
        return tokens

    def finish_due(after):
        tokens = []
        while len(pending) > lag:
            tokens += scatter_finish(after)
        return tokens

    def mixer_out(o, w_o, xin, l, toks, name):
        tm, tn, tk, _ = _mm_tiles(S, o.shape[1], D, 2, 2, 4, True)
        if tn == D and tk == o.shape[1]:
            return _mm(o, w_o, "nn", F32, name, res=xin, layer=0, deps=toks, norm_gain=ffn2_norm[l:l + 1])
        return _mm(o, w_o, "nn", F32, name, res=xin, layer=0, deps=toks), None

    def mixer_do(dxob, w_o, o, out_dtype, name):
        _, _, tk, _ = _mm_tiles(S, D, o.shape[1], 2, 2, jnp.dtype(out_dtype).itemsize, False)
        if tk == D:
            return _mm(dxob, w_o, "nt", out_dtype, name, layer=0, dot_with=o)
        do = _mm(dxob, w_o, "nt", out_dtype, name, layer=0)
        return do, _attn_delta(do, o, name + "_delta")

    def ffn_fwd(xin, norm_row, which, l, bi, deps=(), h=None):
        tag = blocks[bi][0]
        k_in, k_out = (which + "_w_in", l), (which + "_w_out", l)
        if h is None:
            h = _rms_fwd(xin, norm_row, f"rms_fwd_{tag}", deps=deps)
        if bi == 0:
            relay([k_in], h, f"{tag}_in")
        fetch([k_in], h, f"in_{tag}")
        u, a = _ffn_in(h, full[k_in], f"ffn_in_{tag}")
        if bi == 0:
            relay([k_out], a, f"{tag}_out")
        fetch([k_out], a, f"out_{tag}")
        toks = relay_next(bi, a)
        xo = _mm(a, full[k_out], "nn", F32, f"mm_out_{tag}", scale=0.5, res=xin, layer=0, deps=toks)
        return xo, (xin, h, u, a)

    def ffn_bwd(dx_pair, saved, norm_row, which, l, tag, deps=()):
        dxo, dxob = dx_pair
        k_in, k_out = (which + "_w_in", l), (which + "_w_out", l)
        xin, h, u, a = saved
        grads[k_out] = _mm(a, dxob, "tn", BF16, f"mm_dwout_{tag}", scale=0.5, deps=deps)
        t_out = scatter_start(f"{tag}_out", [k_out])
        du = _ffn_da(dxob, full[k_out], u, f"ffn_da_{tag}", deps=[t_out])
        grads[k_in] = _mm(h, du, "tn", BF16, f"mm_dwin_{tag}")
        t_in = scatter_start(f"{tag}_in", [k_in])
        dh = _mm(du, full[k_in], "nt", F32, f"mm_dh_{tag}", layer=0, deps=[t_in])
        toks = finish_due(dh)
        dx, dxb, dg = _rms_bwd(xin, norm_row, dh, dxo, f"rms_bwd_{tag}", deps=toks)
        gain_g.setdefault(which + "_norm", {})[l] = dg
        return dx, dxb

    def mla_fwd(xin, l, bi):
        j = l // 2
        xn = _rms_fwd(xin, mix_norm[l:l + 1], "rms_fwd_mla")
        fetch([(n, j) for n in ("mla_w_down", "mla_w_uq", "mla_w_ukv", "mla_w_o")], xn, "mla")
        lat = _mm(xn, full[("mla_w_down", j)], "nn", F32, "mm_lat", layer=0)
        cq, ckv = _lat_norm_fwd(lat, mla_g_cq[j:j + 1], mla_g_ckv[j:j + 1], "lat_norm_fwd")
        q_raw = _mm(cq, full[("mla_w_uq", j)], "nn", F32, "mm_uq", layer=0)
        kv = _mm(ckv, full[("mla_w_ukv", j)], "nn", F32, "mm_ukv", layer=0)
        qf, kf, vb = _mla_prep_fwd(q_raw, kv, lat, g_qn[j:j + 1], g_kn[j:j + 1], tabs, "mla_prep_fwd")
        o, lse = _flash_fwd(qf, kf, vb, "flash_fwd")
        toks = relay_next(bi, o)
        xo, h_next = mixer_out(o, full[("mla_w_o", j)], xin, l, toks, "mm_mla_o")
        return (xo, h_next), (xin, xn, lat, cq, ckv, q_raw, kv, qf, kf, vb, o, lse)

    def mla_bwd(dx_pair, saved, l):
        dxo, dxob = dx_pair
        j = l // 2
        xin, xn, lat, cq, ckv, q_raw, kv, qf, kf, vb, o, lse = saved
        do, delta = mixer_do(dxob, full[("mla_w_o", j)], o, BF16, "mm_mla_do")
        grads[("mla_w_o", j)] = _mm(o, dxob, "tn", BF16, "mm_mla_dwo")
        dqf, dkf, dv = _flash_bwd(qf, kf, vb, do, lse, delta, "flash_bwd")
        dq_raw, dkv, dkpe, dgq, dgk = _mla_prep_bwd(q_raw, kv, lat, g_qn[j:j + 1], g_kn[j:j + 1], tabs, dqf, dkf, dv,
                                                    "mla_prep_bwd")
        gain_g.setdefault("mla_g_qn", {})[j] = dgq
        gain_g.setdefault("mla_g_kn", {})[j] = dgk
        dcq = _mm(dq_raw, full[("mla_w_uq", j)], "nt", F32, "mm_dcq", layer=0)
        grads[("mla_w_uq", j)] = _mm(cq, dq_raw, "tn", BF16, "mm_dwuq")
        dckv = _mm(dkv, full[("mla_w_ukv", j)], "nt", F32, "mm_dckv", layer=0)
        grads[("mla_w_ukv", j)] = _mm(ckv, dkv, "tn", BF16, "mm_dwukv")
        dlat, dgcq, dgckv = _lat_norm_bwd(lat, mla_g_cq[j:j + 1], mla_g_ckv[j:j + 1], dcq, dckv, dkpe, "lat_norm_bwd")
        gain_g.setdefault("mla_g_cq", {})[j] = dgcq
        gain_g.setdefault("mla_g_ckv", {})[j] = dgckv
        dxn = _mm(dlat, full[("mla_w_down", j)], "nt", F32, "mm_dxn_mla", layer=0)
        grads[("mla_w_down", j)] = _mm(xn, dlat, "tn", BF16, "mm_dwdown")
        tok = scatter_start(f"mix_{l}", [(n, j) for n in ("mla_w_down", "mla_w_uq", "mla_w_ukv", "mla_w_o")])
        toks = finish_due(dxn)
        dx, dxb, dg = _rms_bwd(xin, mix_norm[l:l + 1], dxn, dxo, "rms_bwd_mla", deps=[tok] + toks)
        gain_g.setdefault("mix_norm", {})[l] = dg
        return dx, dxb

    def dil_fwd(xin, l, bi):
        j = l // 2
        xn = _rms_fwd(xin, mix_norm[l:l + 1], "rms_fwd_dil")
        fetch([("dil_w_qkv", j), ("dil_w_o", j)], xn, "dil")
        qkv = _mm(xn, full[("dil_w_qkv", j)], "nn", F32, "mm_qkv", layer=0)
        o_g, lse_g = _dil_fwd(qkv, dil_g_qn[j:j + 1], dil_g_kn[j:j + 1], slopes, "dil_fwd")
        o, lse = _dil_merge(o_g, lse_g, "dil_merge")
        toks = relay_next(bi, o)
        xo, h_next = mixer_out(o, full[("dil_w_o", j)], xin, l, toks, "mm_dil_o")
        return (xo, h_next), (xin, xn, qkv, o, lse)

    def dil_bwd(dx_pair, saved, l):
        dxo, dxob = dx_pair
        j = l // 2
        xin, xn, qkv, o, lse = saved
        do, delta = mixer_do(dxob, full[("dil_w_o", j)], o, F32, "mm_dil_do")
        grads[("dil_w_o", j)] = _mm(o, dxob, "tn", BF16, "mm_dil_dwo")
        dq, dk, dv, dgq, dgk = _dil_bwd(qkv, dil_g_qn[j:j + 1], dil_g_kn[j:j + 1], slopes, do, delta, lse, "dil_bwd")
        gain_g.setdefault("dil_g_qn", {})[j] = dgq
        gain_g.setdefault("dil_g_kn", {})[j] = dgk
        dqkv = jnp.concatenate([dq, dk, dv], axis=1)
        dxn = _mm(dqkv, full[("dil_w_qkv", j)], "nt", F32, "mm_dxn_dil", layer=0)
        grads[("dil_w_qkv", j)] = _mm(xn, dqkv, "tn", BF16, "mm_dwqkv")
        tok = scatter_start(f"mix_{l}", [("dil_w_qkv", j), ("dil_w_o", j)])
        toks = finish_due(dxn)
        dx, dxb, dg = _rms_bwd(xin, mix_norm[l:l + 1], dxn, dxo, "rms_bwd_dil", deps=[tok] + toks)
        gain_g.setdefault("mix_norm", {})[l] = dg
        return dx, dxb

    saved = []
    xc = x0
    for l in range(depth):
        xc, s1 = ffn_fwd(xc, ffn1_norm[l:l + 1], "ffn1", l, 3 * l, deps=[ag_token] if l == 0 else ())
        (xc, h_next), s2 = (mla_fwd if l % 2 == 0 else dil_fwd)(xc, l, 3 * l + 1)
        xc, s3 = ffn_fwd(xc, ffn2_norm[l:l + 1], "ffn2", l, 3 * l + 2, h=h_next)
        saved.append((s1, s2, s3))

    dy, dyb, loss_part = _loss_head(xc, tgt, "loss_head")
    dx = (dy, dyb)
    loss = lax.psum(loss_part[0, 0], MESH_AXES)

    for bi in reversed(range(len(blocks))):
        tag, _ = blocks[bi]
        l = bi // 3
        s = saved[l][bi % 3]
        if bi % 3 == 2:
            dx = ffn_bwd(dx, s, ffn2_norm[l:l + 1], "ffn2", l, tag,
                         deps=[loss.reshape(1, 1)] if bi == len(blocks) - 1 else ())
        elif bi % 3 == 1:
            dx = (mla_bwd if l % 2 == 0 else dil_bwd)(dx, s, l)
        else:
            dx = ffn_bwd(dx, s, ffn1_norm[l:l + 1], "ffn1", l, tag)
    grad_x = dx[0].reshape(x.shape)
    after = dx[1]
    while pending:
        after = scatter_finish(after)[-1]

    small = [n for n in names if n not in big]

    def gain_local(n):
        rows = [gain_g[n][l] for l in range(W[n].shape[0])]
        g = jnp.concatenate(rows, axis=1)
        return g

    def flat_pad(n, a):
        a = a.reshape(1, -1)
        if n in ("mla_g_qn", "mla_g_kn"):
            a = _pad_heads(a)
        return a

    packed_g = jnp.concatenate([gain_local(n) for n in small], axis=1)
    sizes = [gain_local(n).shape[1] for n in small]
    tot_g = _gain_allreduce(packed_g, "gain_allreduce")
    pw = jnp.concatenate([flat_pad(n, W[n]) for n in small], axis=1)
    pm = jnp.concatenate([flat_pad(n, M1[n]) for n in small], axis=1)
    pv = jnp.concatenate([flat_pad(n, V2[n]) for n in small], axis=1)
    res, _ = _adamw(tot_g.reshape(1, 1, -1), None, me, pw.reshape(1, 1, -1), pm.reshape(1, 1, -1),
                    pv.reshape(1, 1, -1), 0, None, "adamw_gains")
    res = [r.reshape(1, -1) for r in res]
    off = 0
    for n, sz in zip(small, sizes):
        for dst, r in zip((out_g, out_d, out_m, out_v), res):
            piece = r[:, off:off + sz]
            if n in ("mla_g_qn", "mla_g_kn"):
                piece = _unpad_heads(piece)
            dst[n] = piece.reshape(W[n].shape)
        off += sz

    return (loss, grad_x, *[out_g[n] for n in names], *[out_d[n] for n in names],
            *[out_m[n] for n in names], *[out_v[n] for n in names])
```

```python
import functools
import math

import jax
import jax.numpy as jnp
import numpy as np
from jax import lax
from jax.experimental import pallas as pl
from jax.experimental.pallas import tpu as pltpu

EPS = 1e-6
MLA_HEADS = 16
Q_LORA = 512
KV_LORA = 512
NOPE_DIM = 128
ROPE_DIM = 64
V_DIM = 128
QK_DIM = NOPE_DIM + ROPE_DIM
ROPE_THETA = 10000.0
HEAD_PAD = 256
LAT_PAD = Q_LORA + KV_LORA + 128
DIL_PAIRS = ((128, 1), (512, 4), (2048, 16))
DIL_GROUPS = 3
DIL_HEADS = 8
DIL_HEAD_DIM = 128
DIL_BLK = 128
FLASH_HEADS = 2
FLASH_HEADS_FWD = 8
LOG2E = math.log2(math.e)
LN2 = math.log(2.0)
ADAM_LR = 0.001
ADAM_B1 = 0.9
ADAM_B2 = 0.999
ADAM_EPS = 1e-08
ADAM_WD = 0.01
ADAM_STEP = 10

N_DEV = 8
MESH_AXES = ("x", "y", "c")
MESH = pl.DeviceIdType.MESH
NEG_BIG = -1e30
VMEM_LIMIT_V7X = 56 * 1024 * 1024
LANES = 128

BF16 = jnp.bfloat16
F32 = jnp.float32


def _pick(n, cands):
    for c in cands:
        if n % c == 0:
            return c
    raise ValueError(f"no tile for {n}")


def _params(sem):
    return pltpu.CompilerParams(dimension_semantics=sem, vmem_limit_bytes=VMEM_LIMIT_V7X)


ANY_SPEC = pl.BlockSpec(memory_space=pl.ANY)


MM_VMEM_BUDGET = 44 * 1024 * 1024
MM_HBM_BYTES_PER_S = 1.8e12
MM_MXU_FLOPS_PER_S = 8.5e14
MM_STEP_S = 0.4e-6
MM_MAX_TILE_MACS = 3.3e9
MXU_DIM = 256


MM_TIMED_TILES = {
    (2048, 4096, 11264, 2, 2, 2, False): (1024, 1024, 4096, True),
    (4096, 11264, 2048, 2, 2, 4, False): (1024, 1024, 2816, True),
    (4096, 5632, 2048, 2, 2, 4, True): (512, 1024, 5632, False),
    (2048, 4096, 9216, 2, 2, 2, False): (1024, 1024, 4096, True),
    (4096, 9216, 2048, 2, 2, 4, False): (1024, 1024, 2304, True),
}


@functools.lru_cache(maxsize=None)
def _mm_tiles(M, K, N, a_bytes, b_bytes, out_bytes, has_res):
    if (M, K, N, a_bytes, b_bytes, out_bytes, has_res) in MM_TIMED_TILES:
        return MM_TIMED_TILES[(M, K, N, a_bytes, b_bytes, out_bytes, has_res)]
    best = None
    for tk in [K] + [c for c in (1408, 1024, 512, 384, 256, 128) if K % c == 0 and c < K]:
        nk = K // tk
        for tm in [c for c in (2048, 1024, 512, 256, 128) if M % c == 0]:
            for tn in [c for c in (2816, 2048, 1408, 1152, 1024, 512, 384, 256, 128) if N % c == 0]:
                if tm * tk * tn > MM_MAX_TILE_MACS:
                    continue
                fill = (tn / (-(-tn // MXU_DIM) * MXU_DIM)) * (tk / (-(-tk // MXU_DIM) * MXU_DIM))
                fill *= tm / (tm + MXU_DIM // 2)
                vmem = 2 * (tm * tk * a_bytes + tk * tn * b_bytes) + 2 * tm * tn * out_bytes + tm * tn * 4
                vmem += (tm * tk + tk * tn) * 2 if max(a_bytes, b_bytes) > 2 else 0
                vmem += 2 * tm * tn * 4 if has_res else 0
                if vmem > MM_VMEM_BUDGET:
                    continue
                a_all, b_all = M * K * a_bytes, K * N * b_bytes
                if nk == 1:
                    t_i = a_all + (M // tm) * b_all
                    t_j = b_all + (N // tn) * a_all
                    traffic, i_outer = min((t_i, True), (t_j, False))
                else:
                    traffic, i_outer = (N // tn) * a_all + (M // tm) * b_all, True
                traffic += M * N * (out_bytes + (4 if has_res else 0))
                mxu = 2.0 * M * K * N / (MM_MXU_FLOPS_PER_S * fill) * (1.15 if nk > 1 else 1.0)
                cost = max(traffic / MM_HBM_BYTES_PER_S, mxu) + (M // tm) * (N // tn) * nk * MM_STEP_S
                if best is None or cost < best[0]:
                    best = (cost, tm, tn, tk, i_outer)
    assert best is not None, (M, K, N)
    return best[1:]


def _mm(a, b, mode, out_dtype, name, *, scale=1.0, res=None, layer=None, deps=(), norm_gain=None, dot_with=None,
        loss_target=None):
    b2 = b.shape[-2:]
    if mode == "nn":
        (M, K), (Kb, N) = a.shape, b2
    elif mode == "nt":
        (M, K), (N, Kb) = a.shape, b2
    else:
        (K, M), (Kb, N) = a.shape, b2
    assert K == Kb, (a.shape, b.shape, mode)
    tm, tn, tk, i_outer = _mm_tiles(M, K, N, a.dtype.itemsize, b.dtype.itemsize, jnp.dtype(out_dtype).itemsize,
                                    res is not None)
    nk = K // tk
    dims = {"nn": (((1,), (0,)), ((), ())), "nt": (((1,), (1,)), ((), ())), "tn": (((0,), (0,)), ((), ()))}[mode]
    normed = norm_gain is not None
    dotted = dot_with is not None
    assert not normed or (nk == 1 and tn == N), (name, tn, N, nk)
    assert not dotted or (nk == 1 and not normed and tn % LANES == 0), (name, tn, nk)
    lossy = loss_target is not None
    assert not lossy or (nk == 1 and res is not None and not normed and not dotted), name
    n_in = 2 + (res is not None) + normed + dotted + lossy + len(deps)

    def finish(v, r_ref, o_ref):
        if scale != 1.0:
            v = v * scale
        if r_ref is not None:
            v = r_ref[...] + v
        o_ref[...] = v.astype(o_ref.dtype)
        return v

    def body(*refs):
        a_ref, b_ref = refs[:2]
        r_ref = refs[2] if res is not None else None
        prod = lambda: lax.dot_general(a_ref[...].astype(BF16), b_ref[...].astype(BF16), dims,
                                       preferred_element_type=F32)
        if lossy:
            t_ref, (dy_ref, dyb_ref, l_ref) = refs[3], refs[n_in:n_in + 3]
            e = r_ref[...] + prod() * scale - t_ref[...]
            dy = e * (1.0 / N)
            dy_ref[...] = dy
            dyb_ref[...] = dy.astype(BF16)

            @pl.when((pl.program_id(0) == 0) & (pl.program_id(1) == 0))
            def _():
                l_ref[...] = jnp.zeros_like(l_ref)

            l_ref[...] += (0.5 / N) * jnp.sum(jnp.sum(e * e, axis=-1, keepdims=True), axis=0, keepdims=True)
            return
        if nk == 1:
            v = finish(prod(), r_ref, refs[n_in])
            if normed:
                g_ref = refs[2 + (res is not None)]
                r = lax.rsqrt(jnp.mean(v * v, axis=-1, keepdims=True) + EPS)
                refs[n_in + 1][...] = ((v * r) * g_ref[...]).astype(BF16)
            if dotted:
                w_ref, d_ref = refs[2 + (res is not None)], refs[n_in + 1]
                for h in range(tn // LANES):
                    sl = slice(h * LANES, (h + 1) * LANES)
                    d = jnp.sum(v[:, sl] * w_ref[:, sl].astype(F32), axis=-1, keepdims=True)
                    d_ref[:, sl] = jnp.broadcast_to(d, (tm, LANES))
            return
        o_ref, acc = refs[-2:]
        k = pl.program_id(2)

        @pl.when(k == 0)
        def _():
            acc[...] = prod()

        @pl.when(k > 0)
        def _():
            acc[...] += prod()

        @pl.when(k == nk - 1)
        def _():
            finish(acc[...], r_ref, o_ref)

    ij = (lambda p, q: (p, q)) if i_outer else (lambda p, q: (q, p))

    def spec(shape, f, lead=None):
        full = lambda p, q, k: f(*ij(p, q), k)
        if lead is None:
            return pl.BlockSpec(shape, full)
        return pl.BlockSpec((None,) + shape, lambda p, q, k: (lead,) + full(p, q, k))

    a_spec = spec((tk, tm), lambda i, j, k: (k, i)) if mode == "tn" else spec((tm, tk), lambda i, j, k: (i, k))
    lead = layer if b.ndim == 3 else None
    b_spec = spec((tn, tk), lambda i, j, k: (j, k), lead) if mode == "nt" else spec((tk, tn), lambda i, j, k: (k, j), lead)
    in_specs = [a_spec, b_spec]
    args = [a, b]
    if res is not None:
        in_specs.append(spec((tm, tn), lambda i, j, k: (i, j)))
        args.append(res)
    tile = spec((tm, tn), lambda i, j, k: (i, j))
    if normed:
        in_specs.append(pl.BlockSpec((1, N), lambda p, q, k: (0, 0)))
        args.append(norm_gain)
    if dotted:
        in_specs.append(tile)
        args.append(dot_with)
    if lossy:
        in_specs.append(tile)
        args.append(loss_target)
    in_specs += [ANY_SPEC] * len(deps)
    args += list(deps)
    outer, inner = (M // tm, N // tn) if i_outer else (N // tn, M // tm)
    if lossy:
        return tuple(pl.pallas_call(
            body, name=name, grid=(outer, inner, nk), in_specs=in_specs,
            out_specs=[tile, tile, pl.BlockSpec((1, 1), lambda p, q, k: (0, 0))],
            out_shape=[jax.ShapeDtypeStruct((M, N), F32), jax.ShapeDtypeStruct((M, N), BF16),
                       jax.ShapeDtypeStruct((1, 1), F32)],
            compiler_params=_params(("arbitrary", "arbitrary", "arbitrary")),
        )(*args))
    second = [jax.ShapeDtypeStruct((M, N), BF16)] if normed else [jax.ShapeDtypeStruct((M, N), F32)] if dotted else []
    out = pl.pallas_call(
        body, name=name, grid=(outer, inner, nk),
        in_specs=in_specs, out_specs=[tile, tile] if second else tile,
        out_shape=[jax.ShapeDtypeStruct((M, N), out_dtype)] + second if second else jax.ShapeDtypeStruct((M, N), out_dtype),
        scratch_shapes=[pltpu.VMEM((tm, tn), F32)] if nk > 1 else [],
        compiler_params=_params(("parallel", "parallel", "arbitrary")),
    )(*args)
    return tuple(out) if second else out


def _cast_into_gathered(w, layer, axis, me, name, deps=()):
    _, R, C = w.shape
    tr = _pick(R, (512, 256, 128, 64, 32, 16))
    nr = R // tr
    axis, paired = axis

    def body(me_ref, w_ref, *rest):
        o_ref = rest[-1]
        o_ref[...] = w_ref[...].astype(BF16)

    if axis == 1:
        out_idx = lambda i, me_ref: (0, _slot(me_ref[0], paired) * nr + i, 0)
        shape = (1, R * N_DEV, C)
    else:
        out_idx = lambda i, me_ref: (0, i, _slot(me_ref[0], paired))
        shape = (1, R, C * N_DEV)
    return pl.pallas_call(
        body, name=name,
        grid_spec=pltpu.PrefetchScalarGridSpec(
            num_scalar_prefetch=1, grid=(nr,),
            in_specs=[pl.BlockSpec((None, tr, C), lambda i, me_ref: (layer, i, 0))] + [ANY_SPEC] * len(deps),
            out_specs=pl.BlockSpec((None, tr, C), out_idx)),
        out_shape=jax.ShapeDtypeStruct(shape, BF16), compiler_params=_params(("parallel",)),
    )(me, w, *deps)


def _rms_fwd(x, g, name, deps=()):
    T, D = x.shape
    tr = _pick(T, (512, 256, 128))

    def body(x_ref, g_ref, *rest):
        o_ref = rest[-1]
        xv = x_ref[...]
        r = lax.rsqrt(jnp.mean(xv * xv, axis=-1, keepdims=True) + EPS)
        o_ref[...] = ((xv * r) * g_ref[...]).astype(BF16)

    return pl.pallas_call(
        body, name=name, grid=(T // tr,),
        in_specs=[pl.BlockSpec((tr, D), lambda i: (i, 0)), pl.BlockSpec((1, D), lambda i: (0, 0))]
        + [ANY_SPEC] * len(deps),
        out_specs=pl.BlockSpec((tr, D), lambda i: (i, 0)),
        out_shape=jax.ShapeDtypeStruct((T, D), BF16), compiler_params=_params(("parallel",)),
    )(x, g, *deps)


def _rms_bwd(x, g, dh, dres, name, deps=()):
    T, D = x.shape
    tr = _pick(T, (256, 128))

    def body(x_ref, g_ref, dh_ref, dres_ref, *rest):
        dx_ref, dxb_ref, dg_ref = rest[-3:]
        xv = x_ref[...]
        dhv = dh_ref[...]
        r = lax.rsqrt(jnp.mean(xv * xv, axis=-1, keepdims=True) + EPS)
        xhat = xv * r
        dxh = dhv * g_ref[...]
        c = jnp.mean(dxh * xhat, axis=-1, keepdims=True)
        dx = dres_ref[...] + r * (dxh - xhat * c)
        dx_ref[...] = dx
        dxb_ref[...] = dx.astype(BF16)

        @pl.when(pl.program_id(0) == 0)
        def _():
            dg_ref[...] = jnp.zeros_like(dg_ref)

        dg_ref[...] += jnp.sum(dhv * xhat, axis=0, keepdims=True)

    row = pl.BlockSpec((tr, D), lambda i: (i, 0))
    vec = pl.BlockSpec((1, D), lambda i: (0, 0))
    return pl.pallas_call(
        body, name=name, grid=(T // tr,),
        in_specs=[row, vec, row, row] + [ANY_SPEC] * len(deps), out_specs=[row, row, vec],
        out_shape=[jax.ShapeDtypeStruct((T, D), F32), jax.ShapeDtypeStruct((T, D), BF16),
                   jax.ShapeDtypeStruct((1, D), F32)],
        compiler_params=_params(("arbitrary",)),
    )(x, g, dh, dres, *deps)


N_PANEL = N_DEV // 2


def _ffn_in(h, w_in, name):
    T, D = h.shape
    F2 = w_in.shape[2]
    pw = F2 // N_PANEL
    half = pw // 2
    tm = _pick(T, (512, 256, 128))

    def body(h_ref, w_ref, u_ref, a_ref):
        r = jnp.dot(h_ref[...], w_ref[...], preferred_element_type=F32)
        u_ref[...] = r.astype(BF16)
        g, up = r[:, :half], r[:, half:]
        a_ref[...] = (g * jax.nn.sigmoid(g) * up).astype(BF16)

    return pl.pallas_call(
        body, name=name, grid=(N_PANEL, T // tm),
        in_specs=[pl.BlockSpec((tm, D), lambda p, i: (i, 0)), pl.BlockSpec((None, D, pw), lambda p, i: (0, 0, p))],
        out_specs=[pl.BlockSpec((tm, pw), lambda p, i: (i, p)), pl.BlockSpec((tm, half), lambda p, i: (i, p))],
        out_shape=[jax.ShapeDtypeStruct((T, F2), BF16), jax.ShapeDtypeStruct((T, F2 // 2), BF16)],
        compiler_params=_params(("parallel", "parallel")),
    )(h, w_in)


def _ffn_da(dxo, w_out, u, name, deps=()):
    T, D = dxo.shape
    F2 = u.shape[1]
    pw = F2 // N_PANEL
    half = pw // 2
    tm = _pick(T, (512, 256, 128))

    def body(d_ref, w_ref, u_ref, *rest):
        du_ref = rest[-1]
        da = 0.5 * lax.dot_general(d_ref[...], w_ref[...], NT_DIMS, preferred_element_type=F32)
        g = u_ref[:, :half].astype(F32)
        up = u_ref[:, half:].astype(F32)
        sg = jax.nn.sigmoid(g)
        silu = g * sg
        du_ref[:, :half] = (da * up * (sg + silu * (1.0 - sg))).astype(BF16)
        du_ref[:, half:] = (da * silu).astype(BF16)

    return pl.pallas_call(
        body, name=name, grid=(N_PANEL, T // tm),
        in_specs=[pl.BlockSpec((tm, D), lambda p, i: (i, 0)), pl.BlockSpec((None, half, D), lambda p, i: (0, p, 0)),
                  pl.BlockSpec((tm, pw), lambda p, i: (i, p))] + [ANY_SPEC] * len(deps),
        out_specs=pl.BlockSpec((tm, pw), lambda p, i: (i, p)),
        out_shape=jax.ShapeDtypeStruct((T, F2), BF16), compiler_params=_params(("parallel", "parallel")),
    )(dxo, w_out, u, *deps)


def _loss_head(y, t, name):
    T, D = y.shape
    tr = _pick(T, (512, 256, 128))

    def body(y_ref, t_ref, dy_ref, dyb_ref, l_ref):
        e = y_ref[...] - t_ref[...]
        dy = e * (1.0 / D)
        dy_ref[...] = dy
        dyb_ref[...] = dy.astype(BF16)

        @pl.when(pl.program_id(0) == 0)
        def _():
            l_ref[...] = jnp.zeros_like(l_ref)

        l_ref[...] += 0.5 * jnp.sum(jnp.mean(e * e, axis=-1, keepdims=True), axis=0, keepdims=True)

    row = pl.BlockSpec((tr, D), lambda i: (i, 0))
    return pl.pallas_call(
        body, name=name, grid=(T // tr,),
        in_specs=[row, row], out_specs=[row, row, pl.BlockSpec((1, 1), lambda i: (0, 0))],
        out_shape=[jax.ShapeDtypeStruct((T, D), F32), jax.ShapeDtypeStruct((T, D), BF16),
                   jax.ShapeDtypeStruct((1, 1), F32)],
        compiler_params=_params(("arbitrary",)),
    )(y, t)


def _rope_tables(S):
    half = ROPE_DIM // 2
    inv = 1.0 / (ROPE_THETA ** (jnp.arange(0, ROPE_DIM, 2, dtype=F32) / ROPE_DIM))
    ang = jnp.arange(S, dtype=F32)[:, None] * inv[None, :]
    cos, sin = jnp.cos(ang), jnp.sin(ang)
    z = jnp.zeros((S, half), F32)
    z2 = jnp.zeros((S, LANES - ROPE_DIM), F32)
    c = jnp.concatenate([cos, cos, z2], axis=1)
    s1 = jnp.concatenate([-sin, z, z2], axis=1)
    s2 = jnp.concatenate([z, sin, z2], axis=1)
    return c, s1, s2


def _rope(r, c, s1, s2):
    return r * c + pltpu.roll(r, LANES - ROPE_DIM // 2, 1) * s1 + pltpu.roll(r, ROPE_DIM // 2, 1) * s2


def _rope_t(d, c, s1, s2):
    return d * c + pltpu.roll(d * s1, ROPE_DIM // 2, 1) + pltpu.roll(d * s2, LANES - ROPE_DIM // 2, 1)


def _lat_norm_fwd(lat, g_cq, g_ckv, name):
    T = lat.shape[0]
    tr = _pick(T, (512, 256, 128))

    def body(lat_ref, gq_ref, gk_ref, cq_ref, ckv_ref):
        for off, g_ref, o_ref in ((0, gq_ref, cq_ref), (Q_LORA, gk_ref, ckv_ref)):
            xv = lat_ref[:, off:off + Q_LORA]
            r = lax.rsqrt(jnp.mean(xv * xv, axis=-1, keepdims=True) + EPS)
            o_ref[...] = ((xv * r) * g_ref[...]).astype(BF16)

    vec = pl.BlockSpec((1, Q_LORA), lambda i: (0, 0))
    out = pl.BlockSpec((tr, Q_LORA), lambda i: (i, 0))
    return pl.pallas_call(
        body, name=name, grid=(T // tr,),
        in_specs=[pl.BlockSpec((tr, LAT_PAD), lambda i: (i, 0)), vec, vec], out_specs=[out, out],
        out_shape=[jax.ShapeDtypeStruct((T, Q_LORA), BF16)] * 2, compiler_params=_params(("parallel",)),
    )(lat, g_cq, g_ckv)


def _lat_norm_bwd(lat, g_cq, g_ckv, dcq, dckv, dkpe, name):
    T = lat.shape[0]
    tr = _pick(T, (256, 128))

    def body(lat_ref, gq_ref, gk_ref, dcq_ref, dckv_ref, dkpe_ref, dlat_ref, dgq_ref, dgk_ref):
        @pl.when(pl.program_id(0) == 0)
        def _():
            dgq_ref[...] = jnp.zeros_like(dgq_ref)
            dgk_ref[...] = jnp.zeros_like(dgk_ref)

        for off, g_ref, d_ref, dg_ref in ((0, gq_ref, dcq_ref, dgq_ref), (Q_LORA, gk_ref, dckv_ref, dgk_ref)):
            xv = lat_ref[:, off:off + Q_LORA]
            dv = d_ref[...]
            r = lax.rsqrt(jnp.mean(xv * xv, axis=-1, keepdims=True) + EPS)
            xhat = xv * r
            dxh = dv * g_ref[...]
            c = jnp.mean(dxh * xhat, axis=-1, keepdims=True)
            dlat_ref[:, off:off + Q_LORA] = (r * (dxh - xhat * c)).astype(BF16)
            dg_ref[...] += jnp.sum(dv * xhat, axis=0, keepdims=True)
        dlat_ref[:, Q_LORA + KV_LORA:] = dkpe_ref[...].astype(BF16)

    vec = pl.BlockSpec((1, Q_LORA), lambda i: (0, 0))
    half = pl.BlockSpec((tr, Q_LORA), lambda i: (i, 0))
    full = pl.BlockSpec((tr, LAT_PAD), lambda i: (i, 0))
    return pl.pallas_call(
        body, name=name, grid=(T // tr,),
        in_specs=[full, vec, vec, half, half, pl.BlockSpec((tr, LANES), lambda i: (i, 0))],
        out_specs=[full, vec, vec],
        out_shape=[jax.ShapeDtypeStruct((T, LAT_PAD), BF16), jax.ShapeDtypeStruct((1, Q_LORA), F32),
                   jax.ShapeDtypeStruct((1, Q_LORA), F32)],
        compiler_params=_params(("arbitrary",)),
    )(lat, g_cq, g_ckv, dcq, dckv, dkpe)


def _mla_prep_fwd(q_raw, kv, lat, g_qn, g_kn, tabs, name):
    T = q_raw.shape[0]
    H = MLA_HEADS
    tr = _pick(T, (256, 128))
    scale = LOG2E / math.sqrt(QK_DIM)

    def body(q_ref, kv_ref, kpe_ref, gq_ref, gk_ref, c_ref, s1_ref, s2_ref, qf_ref, kf_ref, v_ref):
        c, s1, s2 = c_ref[...], s1_ref[...], s2_ref[...]
        gq, gk = gq_ref[...], gk_ref[...]
        kpe = kpe_ref[...]
        kpe_ss = jnp.sum(kpe * kpe, axis=-1, keepdims=True)
        for h in range(H):
            lo = h * HEAD_PAD
            qa = q_ref[:, lo:lo + LANES]
            qb = q_ref[:, lo + LANES:lo + HEAD_PAD]
            ss = jnp.sum(qa * qa + qb * qb, axis=-1, keepdims=True)
            r = lax.rsqrt(ss * (1.0 / QK_DIM) + EPS)
            qf_ref[:, lo:lo + LANES] = (qa * r * gq[:, :LANES] * scale).astype(BF16)
            qf_ref[:, lo + LANES:lo + HEAD_PAD] = (_rope(qb * r * gq[:, LANES:], c, s1, s2) * scale).astype(BF16)
            ka = kv_ref[:, lo:lo + LANES]
            ss = jnp.sum(ka * ka, axis=-1, keepdims=True) + kpe_ss
            r = lax.rsqrt(ss * (1.0 / QK_DIM) + EPS)
            kf_ref[:, lo:lo + LANES] = (ka * r * gk[:, :LANES]).astype(BF16)
            kf_ref[:, lo + LANES:lo + HEAD_PAD] = _rope(kpe * r * gk[:, LANES:], c, s1, s2).astype(BF16)
            v_ref[:, lo:lo + V_DIM] = kv_ref[:, lo + LANES:lo + HEAD_PAD].astype(BF16)
            v_ref[:, lo + V_DIM:lo + HEAD_PAD] = jnp.ones((tr, HEAD_PAD - V_DIM), BF16)

    wide = pl.BlockSpec((tr, H * HEAD_PAD), lambda i: (i, 0))
    lane = pl.BlockSpec((tr, LANES), lambda i: (i, 0))
    gvec = pl.BlockSpec((1, HEAD_PAD), lambda i: (0, 0))
    return pl.pallas_call(
        body, name=name, grid=(T // tr,),
        in_specs=[wide, wide, pl.BlockSpec((tr, LANES), lambda i: (i, (Q_LORA + KV_LORA) // LANES)), gvec, gvec,
                  lane, lane, lane],
        out_specs=[wide, wide, wide],
        out_shape=[jax.ShapeDtypeStruct((T, H * HEAD_PAD), BF16)] * 3,
        compiler_params=_params(("parallel",)),
    )(q_raw, kv, lat, g_qn, g_kn, *tabs)


def _mla_prep_bwd(q_raw, kv, lat, g_qn, g_kn, tabs, dqf, dkf, dv, name):
    T = q_raw.shape[0]
    H = MLA_HEADS
    tr = _pick(T, (128,))

    def body(q_ref, kv_ref, kpe_ref, gq_ref, gk_ref, c_ref, s1_ref, s2_ref, dqf_ref, dkf_ref, dv_ref,
             dq_ref, dkv_ref, dkpe_ref, dgq_ref, dgk_ref):
        @pl.when(pl.program_id(0) == 0)
        def _():
            dgq_ref[...] = jnp.zeros_like(dgq_ref)
            dgk_ref[...] = jnp.zeros_like(dgk_ref)

        c, s1, s2 = c_ref[...], s1_ref[...], s2_ref[...]
        gq, gk = gq_ref[...], gk_ref[...]
        kpe = kpe_ref[...]
        kpe_ss = jnp.sum(kpe * kpe, axis=-1, keepdims=True)
        dkpe = jnp.zeros_like(kpe)
        dgq_a = jnp.zeros((1, LANES), F32)
        dgq_b = jnp.zeros((1, LANES), F32)
        dgk_a = jnp.zeros((1, LANES), F32)
        dgk_b = jnp.zeros((1, LANES), F32)
        for h in range(H):
            lo = h * HEAD_PAD
            xa = q_ref[:, lo:lo + LANES]
            xb = q_ref[:, lo + LANES:lo + HEAD_PAD]
            ss = jnp.sum(xa * xa + xb * xb, axis=-1, keepdims=True)
            r = lax.rsqrt(ss * (1.0 / QK_DIM) + EPS)
            xa, xb = xa * r, xb * r
            da = dqf_ref[:, lo:lo + LANES].astype(F32)
            db = _rope_t(dqf_ref[:, lo + LANES:lo + HEAD_PAD].astype(F32), c, s1, s2)
            dgq_a += jnp.sum(da * xa, axis=0, keepdims=True)
            dgq_b += jnp.sum(db * xb, axis=0, keepdims=True)
            da, db = da * gq[:, :LANES], db * gq[:, LANES:]
            cc = jnp.sum(da * xa + db * xb, axis=-1, keepdims=True) * (1.0 / QK_DIM)
            dq_ref[:, lo:lo + LANES] = (r * (da - xa * cc)).astype(BF16)
            dq_ref[:, lo + LANES:lo + HEAD_PAD] = (r * (db - xb * cc)).astype(BF16)
            xa = kv_ref[:, lo:lo + LANES]
            ss = jnp.sum(xa * xa, axis=-1, keepdims=True) + kpe_ss
            r = lax.rsqrt(ss * (1.0 / QK_DIM) + EPS)
            xa, xb = xa * r, kpe * r
            da = dkf_ref[:, lo:lo + LANES].astype(F32)
            db = _rope_t(dkf_ref[:, lo + LANES:lo + HEAD_PAD].astype(F32), c, s1, s2)
            dgk_a += jnp.sum(da * xa, axis=0, keepdims=True)
            dgk_b += jnp.sum(db * xb, axis=0, keepdims=True)
            da, db = da * gk[:, :LANES], db * gk[:, LANES:]
            cc = jnp.sum(da * xa + db * xb, axis=-1, keepdims=True) * (1.0 / QK_DIM)
            dkv_ref[:, lo:lo + LANES] = (r * (da - xa * cc)).astype(BF16)
            dkpe = dkpe + r * (db - xb * cc)
            dkv_ref[:, lo + LANES:lo + HEAD_PAD] = dv_ref[:, h * V_DIM:(h + 1) * V_DIM].astype(BF16)
        dkpe_ref[...] = dkpe
        dgq_ref[:, :LANES] += dgq_a
        dgq_ref[:, LANES:] += dgq_b
        dgk_ref[:, :LANES] += dgk_a
        dgk_ref[:, LANES:] += dgk_b

    wide = pl.BlockSpec((tr, H * HEAD_PAD), lambda i: (i, 0))
    lane = pl.BlockSpec((tr, LANES), lambda i: (i, 0))
    gvec = pl.BlockSpec((1, HEAD_PAD), lambda i: (0, 0))
    vspec = pl.BlockSpec((tr, H * V_DIM), lambda i: (i, 0))
    return pl.pallas_call(
        body, name=name, grid=(T // tr,),
        in_specs=[wide, wide, pl.BlockSpec((tr, LANES), lambda i: (i, (Q_LORA + KV_LORA) // LANES)), gvec, gvec,
                  lane, lane, lane, wide, wide, vspec],
        out_specs=[wide, wide, lane, gvec, gvec],
        out_shape=[jax.ShapeDtypeStruct((T, H * HEAD_PAD), BF16), jax.ShapeDtypeStruct((T, H * HEAD_PAD), BF16),
                   jax.ShapeDtypeStruct((T, LANES), F32), jax.ShapeDtypeStruct((1, HEAD_PAD), F32),
                   jax.ShapeDtypeStruct((1, HEAD_PAD), F32)],
        compiler_params=_params(("arbitrary",)),
    )(q_raw, kv, lat, g_qn, g_kn, *tabs, dqf, dkf, dv)


def _causal_mask(tq, tk):
    return lax.broadcasted_iota(jnp.int32, (tq, tk), 1) <= lax.broadcasted_iota(jnp.int32, (tq, tk), 0)


NT_DIMS = (((1,), (1,)), ((), ()))
TN_DIMS = (((0,), (0,)), ((), ()))


def _flash_fwd(qf, kf, v, name):
    T = qf.shape[0]
    H, G = MLA_HEADS, FLASH_HEADS_FWD
    t = _pick(T, (512, 256, 128))
    n = T // t
    pairs = [(i, j) for i in range(n) for j in range(i + 1)]
    qi = jnp.asarray([p[0] for p in pairs], jnp.int32)
    kj = jnp.asarray([p[1] for p in pairs], jnp.int32)

    def body(qi_ref, kj_ref, q_ref, k_ref, v_ref, o_ref, lse_ref, *scratch):
        m_sc, acc_sc = scratch[:G], scratch[G:]
        sid = pl.program_id(1)
        i, j = qi_ref[sid], kj_ref[sid]

        @pl.when(j == 0)
        def _():
            for g in range(G):
                m_sc[g][...] = jnp.full_like(m_sc[g], NEG_BIG)
                acc_sc[g][...] = jnp.zeros_like(acc_sc[g])

        def step(masked):
            for g in range(G):
                qk = slice(g * HEAD_PAD, (g + 1) * HEAD_PAD)
                s = lax.dot_general(q_ref[:, qk], k_ref[:, qk], NT_DIMS, preferred_element_type=F32)
                if masked:
                    s = jnp.where(_causal_mask(t, t), s, NEG_BIG)
                m_prev = m_sc[g][:, :1]
                m_new = jnp.maximum(m_prev, jnp.max(s, axis=-1, keepdims=True))
                a = jnp.exp2(m_prev - m_new)
                p = jnp.exp2((s - m_new).astype(BF16))
                acc_sc[g][...] = a * acc_sc[g][...] + jnp.dot(p, v_ref[:, qk], preferred_element_type=F32)
                m_sc[g][...] = jnp.broadcast_to(m_new, (t, LANES))

        @pl.when(j < i)
        def _():
            step(False)

        @pl.when(j == i)
        def _():
            step(True)
            for g in range(G):
                vo = slice(g * V_DIM, (g + 1) * V_DIM)
                l = acc_sc[g][:, V_DIM:]
                o_ref[:, vo] = (acc_sc[g][:, :V_DIM] / l).astype(BF16)
                lse_ref[:, vo] = m_sc[g][...] + jnp.log2(l)

    row = pl.BlockSpec((t, G * V_DIM), lambda h, s, qi, kj: (qi[s], h))
    return pl.pallas_call(
        body, name=name,
        grid_spec=pltpu.PrefetchScalarGridSpec(
            num_scalar_prefetch=2, grid=(H // G, len(pairs)),
            in_specs=[pl.BlockSpec((t, G * HEAD_PAD), lambda h, s, qi, kj: (qi[s], h)),
                      pl.BlockSpec((t, G * HEAD_PAD), lambda h, s, qi, kj: (kj[s], h)),
                      pl.BlockSpec((t, G * HEAD_PAD), lambda h, s, qi, kj: (kj[s], h))],
            out_specs=[row, row],
            scratch_shapes=[pltpu.VMEM((t, LANES), F32)] * G + [pltpu.VMEM((t, HEAD_PAD), F32)] * G),
        out_shape=[jax.ShapeDtypeStruct((T, H * V_DIM), BF16), jax.ShapeDtypeStruct((T, H * V_DIM), F32)],
        compiler_params=_params(("parallel", "arbitrary")),
    )(qi, kj, qf, kf, v)


def _attn_delta(do, o, name):
    T, W = do.shape
    nh = W // V_DIM
    tr = _pick(T, (512, 256, 128))

    def body(do_ref, o_ref, d_ref):
        for h in range(nh):
            sl = slice(h * V_DIM, (h + 1) * V_DIM)
            d = jnp.sum(do_ref[:, sl].astype(F32) * o_ref[:, sl].astype(F32), axis=-1, keepdims=True)
            d_ref[:, sl] = jnp.broadcast_to(d, (tr, V_DIM))

    row = pl.BlockSpec((tr, W), lambda i: (i, 0))
    return pl.pallas_call(
        body, name=name, grid=(T // tr,), in_specs=[row, row], out_specs=row,
        out_shape=jax.ShapeDtypeStruct((T, W), F32), compiler_params=_params(("parallel",)),
    )(do, o)


def _flash_bwd(qf, kf, v, do, lse, delta, name):
    T = qf.shape[0]
    H, G = MLA_HEADS, FLASH_HEADS
    t = _pick(T, (512, 256, 128))
    n = T // t
    scale = 1.0 / math.sqrt(QK_DIM)
    pairs = [(i, j) for j in range(n) for i in range(j, n)]
    qi = jnp.asarray([p[0] for p in pairs], jnp.int32)
    kj = jnp.asarray([p[1] for p in pairs], jnp.int32)

    def body(qi_ref, kj_ref, q_ref, k_ref, v_ref, do_ref, lse_ref, dl_ref, dq_ref, dk_ref, dv_ref,
             dq_acc, dk_acc, dv_acc):
        sid = pl.program_id(1)
        i, j = qi_ref[sid], kj_ref[sid]

        @pl.when(sid == 0)
        def _():
            dq_acc[...] = jnp.zeros_like(dq_acc)

        def step(masked):
            rows = pl.ds(pl.multiple_of(i * t, t), t)
            for g in range(G):
                qk = slice(g * HEAD_PAD, (g + 1) * HEAD_PAD)
                vo = slice(g * V_DIM, (g + 1) * V_DIM)
                q, k, do_ = q_ref[:, qk], k_ref[:, qk], do_ref[:, vo]
                v_ = v_ref[:, g * HEAD_PAD:g * HEAD_PAD + V_DIM]
                s = lax.dot_general(q, k, NT_DIMS, preferred_element_type=F32)
                if masked:
                    s = jnp.where(_causal_mask(t, t), s, NEG_BIG)
                p = jnp.exp2(s - lse_ref[:, g * V_DIM:g * V_DIM + 1])
                dp = lax.dot_general(do_, v_, NT_DIMS, preferred_element_type=F32)
                ds = (p * (dp - dl_ref[:, g * V_DIM:g * V_DIM + 1])).astype(BF16)
                dv = lax.dot_general(p.astype(BF16), do_, TN_DIMS, preferred_element_type=F32)
                dk = lax.dot_general(ds, q, TN_DIMS, preferred_element_type=F32)
                if masked:
                    dv_acc[:, vo] = dv
                    dk_acc[:, qk] = dk
                else:
                    dv_acc[:, vo] += dv
                    dk_acc[:, qk] += dk
                dq_acc[rows, qk] += jnp.dot(ds, k, preferred_element_type=F32) * scale

        @pl.when(i == j)
        def _():
            step(True)

        @pl.when(i > j)
        def _():
            step(False)

        @pl.when(i == n - 1)
        def _():
            dk_ref[...] = (dk_acc[...] * LN2).astype(BF16)
            dv_ref[...] = dv_acc[...].astype(BF16)

        @pl.when(sid == len(pairs) - 1)
        def _():
            dq_ref[...] = dq_acc[...].astype(BF16)

    qs = pl.BlockSpec((t, G * HEAD_PAD), lambda h, s, qi, kj: (qi[s], h))
    rs = pl.BlockSpec((t, G * V_DIM), lambda h, s, qi, kj: (qi[s], h))
    ks = pl.BlockSpec((t, G * HEAD_PAD), lambda h, s, qi, kj: (kj[s], h))
    vs = pl.BlockSpec((t, G * V_DIM), lambda h, s, qi, kj: (kj[s], h))
    return pl.pallas_call(
        body, name=name,
        grid_spec=pltpu.PrefetchScalarGridSpec(
            num_scalar_prefetch=2, grid=(H // G, len(pairs)), in_specs=[qs, ks, ks, rs, rs, rs],
            out_specs=[pl.BlockSpec((T, G * HEAD_PAD), lambda h, s, qi, kj: (0, h)), ks, vs],
            scratch_shapes=[pltpu.VMEM((T, G * HEAD_PAD), F32), pltpu.VMEM((t, G * HEAD_PAD), F32),
                            pltpu.VMEM((t, G * V_DIM), F32)]),
        out_shape=[jax.ShapeDtypeStruct((T, H * HEAD_PAD), BF16), jax.ShapeDtypeStruct((T, H * HEAD_PAD), BF16),
                   jax.ShapeDtypeStruct((T, H * V_DIM), BF16)],
        compiler_params=_params(("parallel", "arbitrary")),
    )(qi, kj, qf, kf, v, do, lse, delta)


def _alibi_slopes():
    tot = DIL_GROUPS * DIL_HEADS
    return [float(np.float32(2.0) ** (np.float32(-8.0) * np.float32(k) / np.float32(tot))) for k in range(1, tot + 1)]


def _dil_masks():
    iq = lax.broadcasted_iota(jnp.int32, (DIL_BLK, DIL_BLK), 0)
    ik = lax.broadcasted_iota(jnp.int32, (DIL_BLK, DIL_BLK), 1)
    return (ik >= iq), (iq + DIL_BLK - ik).astype(F32), (ik <= iq), (iq - ik).astype(F32)


def _dil_norm(x, g):
    r = lax.rsqrt(jnp.mean(x * x, axis=-1, keepdims=True) + EPS)
    return x * r, r


DIL_SUPER = 8
BNT_DIMS = (((2,), (2,)), ((0,), (0,)))
BNN_DIMS = (((2,), (1,)), ((0,), (0,)))
BTN_DIMS = (((1,), (1,)), ((0,), (0,)))


def _dil_chunk(it, nb, d):
    assert nb & (nb - 1) == 0, nb
    r, n = it >> (nb.bit_length() - 1), it & (nb - 1)
    if d > 1:
        tok = pl.ds(n * (d * DIL_BLK) + r, DIL_BLK, stride=d)
    else:
        tok = pl.ds(pl.multiple_of(it * DIL_BLK, DIL_BLK), DIL_BLK)
    return tok, pl.ds(pl.multiple_of((it + 1) * DIL_BLK, DIL_BLK), DIL_BLK)


def _dil_token_rows(bidx, nb, d):
    r, n = divmod(bidx, nb)
    return pl.ds(n * DIL_BLK * d + r, DIL_BLK, stride=d) if d > 1 else pl.ds(bidx * DIL_BLK, DIL_BLK)


def _dil_super_rows(ss):
    base = (1 + ss * DIL_SUPER) * DIL_BLK
    return pl.ds(base, DIL_SUPER * DIL_BLK), pl.ds(base - DIL_BLK, DIL_SUPER * DIL_BLK)


def _dil_b3(x):
    return x.reshape(DIL_SUPER, DIL_BLK, x.shape[-1])


def _dil_scores(q3, kc3, kp3, slope, d, ss, nb):
    ok_p, dist_p, ok_c, dist_c = _dil_masks()
    scale = 1.0 / math.sqrt(DIL_HEAD_DIM)
    bias_p = jnp.where(ok_p, -slope * d * dist_p, NEG_BIG)
    bias_c = jnp.where(ok_c, -slope * d * dist_c, NEG_BIG)
    s_c = lax.dot_general(q3, kc3, BNT_DIMS, preferred_element_type=F32) * scale + bias_c[None]
    s_p = lax.dot_general(q3, kp3, BNT_DIMS, preferred_element_type=F32) * scale + bias_p[None]
    bidx = ss * DIL_SUPER + lax.broadcasted_iota(jnp.int32, s_p.shape, 0)
    s_p = jnp.where((bidx & (nb - 1)) == 0, NEG_BIG, s_p)
    return s_c, s_p


def _dil_fwd(qkv, g_qn, g_kn, slopes, name):
    T = qkv.shape[0]
    GH = DIL_GROUPS * DIL_HEADS
    scale = 1.0 / math.sqrt(DIL_HEAD_DIM)

    def body(sl_ref, q_ref, k_ref, v_ref, gq_ref, gk_ref, o_ref, lse_ref, qn_pm, kn_pm, v_pm):
        gh = pl.program_id(0)
        slope = sl_ref[gh]
        gq, gk = gq_ref[...], gk_ref[...]
        pad = pl.ds(0, DIL_BLK)
        kn_pm[pad, :] = jnp.zeros((DIL_BLK, DIL_HEAD_DIM), BF16)
        v_pm[pad, :] = jnp.zeros((DIL_BLK, DIL_HEAD_DIM), BF16)
        for g, (_, d) in enumerate(DIL_PAIRS):
            @pl.when((gh >= g * DIL_HEADS) & (gh < (g + 1) * DIL_HEADS))
            def _(d=d):
                nb = T // (d * DIL_BLK)

                def fill(it, _):
                    tok, dst = _dil_chunk(it, nb, d)
                    qn_pm[dst, :] = (_dil_norm(q_ref[tok, :], gq)[0] * gq).astype(BF16)
                    kn_pm[dst, :] = (_dil_norm(k_ref[tok, :], gk)[0] * gk).astype(BF16)
                    v_pm[dst, :] = v_ref[tok, :].astype(BF16)
                    return 0
                lax.fori_loop(0, T // DIL_BLK, fill, 0, unroll=4)

                for ss in range(T // DIL_BLK // DIL_SUPER):
                    cur, prv = _dil_super_rows(ss)
                    q3, kc3, kp3 = _dil_b3(qn_pm[cur, :]), _dil_b3(kn_pm[cur, :]), _dil_b3(kn_pm[prv, :])
                    s_c, s_p = _dil_scores(q3, kc3, kp3, slope, d, ss, nb)
                    m = jnp.max(jnp.maximum(s_c, s_p), axis=-1, keepdims=True)
                    p_c = jnp.exp(s_c - m)
                    p_p = jnp.exp(s_p - m)
                    l = jnp.sum(p_c, axis=-1, keepdims=True) + jnp.sum(p_p, axis=-1, keepdims=True)
                    acc = lax.dot_general(p_c.astype(BF16), _dil_b3(v_pm[cur, :]), BNN_DIMS, preferred_element_type=F32)
                    acc += lax.dot_general(p_p.astype(BF16), _dil_b3(v_pm[prv, :]), BNN_DIMS, preferred_element_type=F32)
                    o3 = acc / l
                    lse3 = jnp.broadcast_to(m + jnp.log(l), o3.shape)
                    for b in range(DIL_SUPER):
                        tok = _dil_token_rows(ss * DIL_SUPER + b, nb, d)
                        o_ref[tok, :] = o3[b]
                        lse_ref[tok, :] = lse3[b]

    col = lambda off: pl.BlockSpec((T, DIL_HEAD_DIM), lambda gh, sl: (0, gh + off))
    gvec = pl.BlockSpec((1, DIL_HEAD_DIM), lambda gh, sl: (0, 0))
    return pl.pallas_call(
        body, name=name,
        grid_spec=pltpu.PrefetchScalarGridSpec(
            num_scalar_prefetch=1, grid=(GH,),
            in_specs=[col(0), col(GH), col(2 * GH), gvec, gvec], out_specs=[col(0), col(0)],
            scratch_shapes=[pltpu.VMEM((DIL_BLK + T, DIL_HEAD_DIM), BF16)] * 3),
        out_shape=[jax.ShapeDtypeStruct((T, GH * DIL_HEAD_DIM), F32)] * 2,
        compiler_params=_params(("parallel",)),
    )(slopes, qkv, qkv, qkv, g_qn, g_kn)


def _dil_merge(o_g, lse_g, name):
    T = o_g.shape[0]
    W = DIL_HEADS * DIL_HEAD_DIM
    tr = _pick(T, (256, 128))

    def body(o0, o1, o2, l0, l1, l2, o_ref, lse_ref):
        a, b, c = l0[...], l1[...], l2[...]
        m = jnp.maximum(jnp.maximum(a, b), c)
        ea, eb, ec = jnp.exp(a - m), jnp.exp(b - m), jnp.exp(c - m)
        tot = ea + eb + ec
        o_ref[...] = ((o0[...] * ea + o1[...] * eb + o2[...] * ec) / tot).astype(BF16)
        lse_ref[...] = m + jnp.log(tot)

    grp = lambda g: pl.BlockSpec((tr, W), lambda i: (i, g))
    out = pl.BlockSpec((tr, W), lambda i: (i, 0))
    return pl.pallas_call(
        body, name=name, grid=(T // tr,),
        in_specs=[grp(0), grp(1), grp(2), grp(0), grp(1), grp(2)], out_specs=[out, out],
        out_shape=[jax.ShapeDtypeStruct((T, W), BF16), jax.ShapeDtypeStruct((T, W), F32)],
        compiler_params=_params(("parallel",)),
    )(o_g, o_g, o_g, lse_g, lse_g, lse_g)


def _dil_bwd(qkv, g_qn, g_kn, slopes, do, delta, lse, name):
    T = qkv.shape[0]
    GH = DIL_GROUPS * DIL_HEADS
    scale = 1.0 / math.sqrt(DIL_HEAD_DIM)
    nchunk = T // DIL_BLK

    def body(sl_ref, q_ref, k_ref, v_ref, gq_ref, gk_ref, do_ref, dl_ref, lse_ref,
             dq_ref, dk_ref, dv_ref, dgq_ref, dgk_ref,
             qn_pm, kn_pm, v_pm, do_pm, lse_pm, dl_pm, dq_pm, dk_pm, dv_pm, tok_sc):
        gh = pl.program_id(0)
        slope = sl_ref[gh]
        gq, gk = gq_ref[...], gk_ref[...]

        @pl.when(gh == 0)
        def _():
            dgq_ref[...] = jnp.zeros_like(dgq_ref)
            dgk_ref[...] = jnp.zeros_like(dgk_ref)

        pad = pl.ds(0, DIL_BLK)
        kn_pm[pad, :] = jnp.zeros((DIL_BLK, DIL_HEAD_DIM), BF16)
        v_pm[pad, :] = jnp.zeros((DIL_BLK, DIL_HEAD_DIM), BF16)
        dk_pm[...] = jnp.zeros_like(dk_pm)
        dv_pm[...] = jnp.zeros_like(dv_pm)
        for g, (_, d) in enumerate(DIL_PAIRS):
            @pl.when((gh >= g * DIL_HEADS) & (gh < (g + 1) * DIL_HEADS))
            def _(d=d):
                nb = T // (d * DIL_BLK)

                def fill(it, _):
                    tok, dst = _dil_chunk(it, nb, d)
                    qn_pm[dst, :] = (_dil_norm(q_ref[tok, :], gq)[0] * gq).astype(BF16)
                    kn_pm[dst, :] = (_dil_norm(k_ref[tok, :], gk)[0] * gk).astype(BF16)
                    v_pm[dst, :] = v_ref[tok, :].astype(BF16)
                    do_pm[dst, :] = do_ref[tok, :].astype(BF16)
                    lse_pm[dst, :] = lse_ref[tok, :]
                    dl_pm[dst, :] = dl_ref[tok, :]
                    return 0
                lax.fori_loop(0, nchunk, fill, 0, unroll=4)

                for ss in range(nchunk // DIL_SUPER):
                    cur, prv = _dil_super_rows(ss)
                    q3, kc3, kp3 = _dil_b3(qn_pm[cur, :]), _dil_b3(kn_pm[cur, :]), _dil_b3(kn_pm[prv, :])
                    vc3, vp3, do3 = _dil_b3(v_pm[cur, :]), _dil_b3(v_pm[prv, :]), _dil_b3(do_pm[cur, :])
                    ls = _dil_b3(lse_pm[cur, :])[:, :, :1]
                    delta = _dil_b3(dl_pm[cur, :])[:, :, :1]
                    s_c, s_p = _dil_scores(q3, kc3, kp3, slope, d, ss, nb)
                    p_c = jnp.exp(s_c - ls)
                    p_p = jnp.exp(s_p - ls)
                    dp_c = lax.dot_general(do3, vc3, BNT_DIMS, preferred_element_type=F32)
                    dp_p = lax.dot_general(do3, vp3, BNT_DIMS, preferred_element_type=F32)
                    ds_c = (p_c * (dp_c - delta)).astype(BF16)
                    ds_p = (p_p * (dp_p - delta)).astype(BF16)
                    dq3 = (lax.dot_general(ds_c, kc3, BNN_DIMS, preferred_element_type=F32)
                           + lax.dot_general(ds_p, kp3, BNN_DIMS, preferred_element_type=F32)) * scale
                    flat = lambda x: x.reshape(DIL_SUPER * DIL_BLK, DIL_HEAD_DIM)
                    dq_pm[cur, :] = flat(dq3)
                    dk_pm[cur, :] += flat(lax.dot_general(ds_c, q3, BTN_DIMS, preferred_element_type=F32)) * scale
                    dv_pm[cur, :] += flat(lax.dot_general(p_c.astype(BF16), do3, BTN_DIMS, preferred_element_type=F32))
                    dk_pm[prv, :] += flat(lax.dot_general(ds_p, q3, BTN_DIMS, preferred_element_type=F32)) * scale
                    dv_pm[prv, :] += flat(lax.dot_general(p_p.astype(BF16), do3, BTN_DIMS, preferred_element_type=F32))

                def to_tokens(src_pm):
                    def move(it, _):
                        tok, src = _dil_chunk(it, nb, d)
                        tok_sc[tok, :] = src_pm[src, :]
                        return 0
                    lax.fori_loop(0, nchunk, move, 0, unroll=4)

                def norm_bwd(x_ref, gvec, out_ref):
                    big = 4 * DIL_BLK

                    def fin(ci, dg):
                        rows = pl.ds(pl.multiple_of(ci * big, big), big)
                        xhat, r = _dil_norm(x_ref[rows, :], gvec)
                        dn = tok_sc[rows, :]
                        dxh = dn * gvec
                        c = jnp.mean(dxh * xhat, axis=-1, keepdims=True)
                        out_ref[rows, :] = (r * (dxh - xhat * c)).astype(BF16)
                        return dg + jnp.sum(dn * xhat, axis=0, keepdims=True)
                    return lax.fori_loop(0, T // big, fin, jnp.zeros((1, DIL_HEAD_DIM), F32))

                to_tokens(dq_pm)
                dgq_ref[...] += norm_bwd(q_ref, gq, dq_ref)
                to_tokens(dk_pm)
                dgk_ref[...] += norm_bwd(k_ref, gk, dk_ref)
                to_tokens(dv_pm)
                dv_ref[...] = tok_sc[...].astype(BF16)

    col = lambda off: pl.BlockSpec((T, DIL_HEAD_DIM), lambda gh, sl: (0, gh + off))
    hcol = pl.BlockSpec((T, DIL_HEAD_DIM), lambda gh, sl: (0, gh % DIL_HEADS))
    gvec = pl.BlockSpec((1, DIL_HEAD_DIM), lambda gh, sl: (0, 0))
    wide = jax.ShapeDtypeStruct((T, GH * DIL_HEAD_DIM), BF16)
    vec = jax.ShapeDtypeStruct((1, DIL_HEAD_DIM), F32)
    pm = lambda dt: pltpu.VMEM((DIL_BLK + T, DIL_HEAD_DIM), dt)
    return pl.pallas_call(
        body, name=name,
        grid_spec=pltpu.PrefetchScalarGridSpec(
            num_scalar_prefetch=1, grid=(GH,),
            in_specs=[col(0), col(GH), col(2 * GH), gvec, gvec, hcol, hcol, hcol],
            out_specs=[col(0), col(0), col(0), gvec, gvec],
            scratch_shapes=[pm(BF16)] * 4 + [pm(F32)] * 5 + [pltpu.VMEM((T, DIL_HEAD_DIM), F32)]),
        out_shape=[wide, wide, wide, vec, vec],
        compiler_params=_params(("arbitrary",)),
    )(slopes, qkv, qkv, qkv, g_qn, g_kn, do, delta, lse)


def _my_pos():
    return lax.axis_index("x"), lax.axis_index("y"), lax.axis_index("c")


def _peer(pos, j):
    x, y, c = pos
    px = 1 - x if j & 4 else x
    py = 1 - y if j & 2 else y
    pc = 1 - c if j & 1 else c
    return (px, py, pc), 4 * px + 2 * py + pc


def _slot(idx, paired):
    if not paired:
        return idx
    return jnp.where(idx < N_DEV // 2, 2 * idx, 2 * idx - (N_DEV - 1))


def _shard_slice(ref, axis, idx, size, paired=False):
    sl = [slice(None)] * len(ref.shape)
    sl[axis] = pl.ds(pl.multiple_of(_slot(idx, paired) * size, 8), size)
    return ref.at[tuple(sl)]


HBM_SPEC = pl.BlockSpec(memory_space=pltpu.HBM)
SEM_SPEC = pl.BlockSpec(memory_space=pltpu.SEMAPHORE)
DATAFLOW = pltpu.SideEffectType.DATAFLOW_SIDE_EFFECTING
N_PEER = N_DEV - 1


def _scatter_copy(axis, grad, slots, frm, to, dev, send_sem, recv_sem):
    ax, paired = axis
    src = _shard_slice(grad, ax, to, grad.shape[ax] // N_DEV, paired)
    return pltpu.make_async_remote_copy(src_ref=src, dst_ref=slots.at[frm], send_sem=send_sem, recv_sem=recv_sem,
                                        device_id=dev, device_id_type=MESH)


def _scatter_start(grads, axes, name):
    n = len(grads)

    def body(*refs):
        outs = refs[2 * n:]
        send, recv, token = outs[:n], outs[n:2 * n], outs[4 * n]
        pos = _my_pos()
        me = 4 * pos[0] + 2 * pos[1] + pos[2]
        for a in range(n):
            for j in range(1, N_DEV):
                dev, pid = _peer(pos, j)
                _scatter_copy(axes[a], refs[2 * a], refs[2 * a + 1], me, pid, dev, send[a].at[j - 1],
                              recv[a].at[j - 1]).start()
        token[...] = jnp.zeros_like(token)

    ops = []
    for g, (ax, _) in zip(grads, axes):
        shp = list(g.shape)
        shp[ax] //= N_DEV
        ops += [g, lax.empty((N_DEV,) + tuple(shp), g.dtype)]
    sems = [pltpu.SemaphoreType.DMA((N_PEER,))] * (2 * n)
    res = pl.pallas_call(
        body, name=name,
        out_shape=sems + [pltpu.HBM(o.shape, o.dtype) for o in ops] + [jax.ShapeDtypeStruct((8, LANES), F32)],
        in_specs=[HBM_SPEC] * len(ops),
        out_specs=[SEM_SPEC] * (2 * n) + [HBM_SPEC] * len(ops) + [pl.BlockSpec(memory_space=pltpu.VMEM)],
        input_output_aliases={i: 2 * n + i for i in range(len(ops))},
        compiler_params=pltpu.CompilerParams(has_side_effects=DATAFLOW),
    )(*[pltpu.with_memory_space_constraint(o, pltpu.HBM) for o in ops])
    items = [(res[a], res[n + a], res[2 * n + 2 * a], res[2 * n + 2 * a + 1]) for a in range(n)]
    return items, res[4 * n]


def _scatter_wait(items, axes, after, name):
    n = len(items)

    def body(*refs):
        send, recv = refs[2 * n:3 * n], refs[3 * n:4 * n]
        pos = _my_pos()
        me = 4 * pos[0] + 2 * pos[1] + pos[2]
        for a in range(n):
            for j in range(1, N_DEV):
                dev, pid = _peer(pos, j)
                cp = _scatter_copy(axes[a], refs[2 * a], refs[2 * a + 1], pid, me, dev, send[a].at[j - 1],
                                   recv[a].at[j - 1])
                cp.wait_send()
                cp.wait_recv()

    ops = [b for it in items for b in it[2:]]
    res = pl.pallas_call(
        body, name=name,
        out_shape=[pltpu.HBM(o.shape, o.dtype) for o in ops],
        in_specs=[HBM_SPEC] * len(ops) + [SEM_SPEC] * (2 * n) + [ANY_SPEC],
        out_specs=[HBM_SPEC] * len(ops),
        input_output_aliases={i: i for i in range(len(ops))},
        compiler_params=pltpu.CompilerParams(has_side_effects=DATAFLOW),
    )(*ops, *[it[0] for it in items], *[it[1] for it in items], after)
    return [(res[2 * a], res[2 * a + 1]) for a in range(n)]


SIBLING = 1
ICI_PEERS = (2, 4, 6)


def _gather_copy(buf, axis, shard, dev, send_sem, recv_sem):
    ax, paired = axis
    piece = _shard_slice(buf, ax, shard, buf.shape[ax] // N_DEV, paired)
    return pltpu.make_async_remote_copy(src_ref=piece, dst_ref=piece, send_sem=send_sem, recv_sem=recv_sem,
                                        device_id=dev, device_id_type=MESH)


def _gather_start(bufs, axes, name):
    n = len(bufs)

    def body(*refs):
        ins, outs = refs[:n], refs[n:]
        send, r_sib, r_ici, token = outs[:n], outs[n:2 * n], outs[2 * n:3 * n], outs[4 * n]
        pos = _my_pos()
        me = 4 * pos[0] + 2 * pos[1] + pos[2]
        for a in range(n):
            dev, _ = _peer(pos, SIBLING)
            _gather_copy(ins[a], axes[a], me, dev, send[a].at[0], r_sib[a].at[0]).start()
            for k, j in enumerate(ICI_PEERS):
                dev, _ = _peer(pos, j)
                _gather_copy(ins[a], axes[a], me, dev, send[a].at[1 + k], r_ici[a].at[k]).start()
        token[...] = jnp.zeros_like(token)

    sems = ([pltpu.SemaphoreType.DMA((1 + len(ICI_PEERS),))] * n + [pltpu.SemaphoreType.DMA((1,))] * n
            + [pltpu.SemaphoreType.DMA((len(ICI_PEERS),))] * n)
    res = pl.pallas_call(
        body, name=name,
        out_shape=sems + [pltpu.HBM(b.shape, b.dtype) for b in bufs] + [jax.ShapeDtypeStruct((8, LANES), F32)],
        in_specs=[HBM_SPEC] * n,
        out_specs=[SEM_SPEC] * (3 * n) + [HBM_SPEC] * n + [pl.BlockSpec(memory_space=pltpu.VMEM)],
        input_output_aliases={i: 3 * n + i for i in range(n)},
        compiler_params=pltpu.CompilerParams(has_side_effects=DATAFLOW),
    )(*[pltpu.with_memory_space_constraint(b, pltpu.HBM) for b in bufs])
    items = [dict(send=res[a], r_sib=res[n + a], r_ici=res[2 * n + a], buf=res[3 * n + a]) for a in range(n)]
    return items, res[4 * n]


def _gather_relay(items, axes, after, name):
    n = len(items)

    def body(*refs):
        ins, r_ici = refs[:n], refs[n:2 * n]
        outs = refs[2 * n + 1:]
        s_rel, r_rel, token = outs[:n], outs[n:2 * n], outs[3 * n]
        pos = _my_pos()
        sib, _ = _peer(pos, SIBLING)
        for a in range(n):
            for k, j in enumerate(ICI_PEERS):
                dev, pid = _peer(pos, j)
                _gather_copy(ins[a], axes[a], pid, dev, s_rel[a].at[k], r_ici[a].at[k]).wait_recv()
                _gather_copy(ins[a], axes[a], pid, sib, s_rel[a].at[k], r_rel[a].at[k]).start()
        token[...] = jnp.zeros_like(token)

    bufs = [it["buf"] for it in items]
    sems = [pltpu.SemaphoreType.DMA((len(ICI_PEERS),))] * (2 * n)
    res = pl.pallas_call(
        body, name=name,
        out_shape=sems + [pltpu.HBM(b.shape, b.dtype) for b in bufs] + [jax.ShapeDtypeStruct((8, LANES), F32)],
        in_specs=[HBM_SPEC] * n + [SEM_SPEC] * n + [ANY_SPEC],
        out_specs=[SEM_SPEC] * (2 * n) + [HBM_SPEC] * n + [pl.BlockSpec(memory_space=pltpu.VMEM)],
        input_output_aliases={i: 2 * n + i for i in range(n)},
        compiler_params=pltpu.CompilerParams(has_side_effects=DATAFLOW),
    )(*bufs, *[it["r_ici"] for it in items], after)
    out = [dict(send=it["send"], r_sib=it["r_sib"], s_rel=res[a], r_rel=res[n + a], buf=res[2 * n + a])
           for a, it in enumerate(items)]
    return out, res[3 * n]


def _gather_wait(items, axes, after, name):
    n = len(items)

    def body(*refs):
        ins = refs[:n]
        send, r_sib, s_rel, r_rel = (refs[(1 + q) * n:(2 + q) * n] for q in range(4))
        pos = _my_pos()
        me = 4 * pos[0] + 2 * pos[1] + pos[2]
        sib, sib_id = _peer(pos, SIBLING)
        for a in range(n):
            for k in range(1 + len(ICI_PEERS)):
                _gather_copy(ins[a], axes[a], me, sib, send[a].at[k], r_sib[a].at[0]).wait_send()
            _gather_copy(ins[a], axes[a], sib_id, sib, send[a].at[0], r_sib[a].at[0]).wait_recv()
            for k, j in enumerate(ICI_PEERS):
                _, pid = _peer(pos, j)
                _, far = _peer(pos, j ^ SIBLING)
                _gather_copy(ins[a], axes[a], pid, sib, s_rel[a].at[k], r_rel[a].at[k]).wait_send()
                _gather_copy(ins[a], axes[a], far, sib, s_rel[a].at[k], r_rel[a].at[k]).wait_recv()

    bufs = [it["buf"] for it in items]
    res = pl.pallas_call(
        body, name=name,
        out_shape=[pltpu.HBM(b.shape, b.dtype) for b in bufs],
        in_specs=[HBM_SPEC] * n + [SEM_SPEC] * (4 * n) + [ANY_SPEC],
        out_specs=[HBM_SPEC] * n,
        input_output_aliases={i: i for i in range(n)},
        compiler_params=pltpu.CompilerParams(has_side_effects=DATAFLOW),
    )(*bufs, *[it["send"] for it in items], *[it["r_sib"] for it in items], *[it["s_rel"] for it in items],
      *[it["r_rel"] for it in items], after)
    return list(res)


def _gain_allreduce(v, name):
    n = v.shape[1]

    def body(v_ref, o_ref, slots, send_sems, recv_sems):
        pos = _my_pos()
        me = 4 * pos[0] + 2 * pos[1] + pos[2]
        slots[me] = v_ref[...]
        copies = []
        for j in range(1, N_DEV):
            dev, _ = _peer(pos, j)
            cp = pltpu.make_async_remote_copy(
                src_ref=slots.at[me], dst_ref=slots.at[me], send_sem=send_sems.at[j], recv_sem=recv_sems.at[j],
                device_id=dev, device_id_type=MESH)
            cp.start()
            copies.append(cp)
        for j in range(1, N_DEV):
            dev, pid = _peer(pos, j)
            pltpu.make_async_remote_copy(
                src_ref=slots.at[me], dst_ref=slots.at[pid], send_sem=send_sems.at[j], recv_sem=recv_sems.at[j],
                device_id=dev, device_id_type=MESH).wait_recv()
        for cp in copies:
            cp.wait_send()
        acc = slots[0]
        for s in range(1, N_DEV):
            acc = acc + slots[s]
        o_ref[...] = acc

    return pl.pallas_call(
        body, name=name, out_shape=jax.ShapeDtypeStruct((1, n), F32),
        in_specs=[pl.BlockSpec(memory_space=pltpu.VMEM)], out_specs=pl.BlockSpec(memory_space=pltpu.VMEM),
        scratch_shapes=[pltpu.VMEM((N_DEV, 1, n), F32), pltpu.SemaphoreType.DMA((N_DEV,)),
                        pltpu.SemaphoreType.DMA((N_DEV,))],
        compiler_params=pltpu.CompilerParams(has_side_effects=True),
    )(v)


def _adamw(parts, own, me, w, m, v, layer, prev, name, own_axis=None):
    L, R, C = w.shape
    P = parts.shape[0]
    tr = _pick(R, (128, 64, 32, 16, 8, 1))
    c1 = 1.0 - ADAM_B1 ** ADAM_STEP
    c2 = 1.0 - ADAM_B2 ** ADAM_STEP
    n_in = 4 if own is None else 5

    def body(me_ref, *refs):
        p_ref = refs[0]
        w_ref, m_ref, v_ref = refs[n_in - 3:n_in]
        g_out, d_out, m_out, v_out, tok = refs[-5:]
        g = None
        for s in range(P):
            part = p_ref[s]
            if own is not None:
                part = jnp.where(me_ref[0] == s, refs[1][...], part)
            g = part.astype(F32) if g is None else g + part.astype(F32)
        mn = ADAM_B1 * m_ref[...] + (1.0 - ADAM_B1) * g
        vn = ADAM_B2 * v_ref[...] + (1.0 - ADAM_B2) * (g * g)
        g_out[...] = g
        m_out[...] = mn
        v_out[...] = vn
        d_out[...] = -ADAM_LR * ((mn / c1) / (jnp.sqrt(vn / c2) + ADAM_EPS) + ADAM_WD * w_ref[...])
        tok[...] = jnp.zeros_like(tok)

    row = pl.BlockSpec((None, tr, C), lambda i, me_ref: (layer, i, 0))
    in_specs = [pl.BlockSpec((P, tr, C), lambda i, me_ref: (0, i, 0))]
    args = [parts]
    if own is not None:
        if own_axis is None:
            own_idx = lambda i, me_ref: (i, 0)
        elif own_axis[0] == 0:
            own_idx = lambda i, me_ref: (_slot(me_ref[0], own_axis[1]) * (R // tr) + i, 0)
        else:
            own_idx = lambda i, me_ref: (i, _slot(me_ref[0], own_axis[1]))
        in_specs.append(pl.BlockSpec((tr, C), own_idx))
        args.append(own)
    in_specs += [row, row, row]
    args += [w, m, v]
    aliases = {}
    if prev is not None:
        in_specs += [ANY_SPEC] * 4
        aliases = {1 + len(args) + k: k for k in range(4)}
        args += list(prev)
    shp = jax.ShapeDtypeStruct((L, R, C), F32)
    res = pl.pallas_call(
        body, name=name,
        grid_spec=pltpu.PrefetchScalarGridSpec(
            num_scalar_prefetch=1, grid=(R // tr,), in_specs=in_specs,
            out_specs=[row] * 4 + [pl.BlockSpec((8, LANES), lambda i, me_ref: (0, 0))]),
        out_shape=[shp] * 4 + [jax.ShapeDtypeStruct((8, LANES), F32)],
        input_output_aliases=aliases, compiler_params=_params(("arbitrary",)),
    )(me, *args)
    return res[:4], res[4]


def _pad_heads(w):
    lead = w.shape[:-1]
    n = w.shape[-1] // QK_DIM
    w = w.reshape(lead + (n, QK_DIM))
    w = jnp.pad(w, [(0, 0)] * len(lead) + [(0, 0), (0, HEAD_PAD - QK_DIM)])
    return w.reshape(lead + (n * HEAD_PAD,))


def _unpad_heads(w):
    lead = w.shape[:-1]
    n = w.shape[-1] // HEAD_PAD
    return w.reshape(lead + (n, HEAD_PAD))[..., :QK_DIM].reshape(lead + (n * QK_DIM,))


def kernel(x, ffn1_norm, ffn1_w_in, ffn1_w_out, mix_norm, ffn2_norm, ffn2_w_in, ffn2_w_out, mla_w_down, mla_g_cq, mla_g_ckv, mla_w_uq, mla_w_ukv, mla_g_qn, mla_g_kn, mla_w_o, dil_w_qkv, dil_g_qn, dil_g_kn, dil_w_o, loss_target, m_ffn1_norm, m_ffn1_w_in, m_ffn1_w_out, m_mix_norm, m_ffn2_norm, m_ffn2_w_in, m_ffn2_w_out, m_mla_w_down, m_mla_g_cq, m_mla_g_ckv, m_mla_w_uq, m_mla_w_ukv, m_mla_g_qn, m_mla_g_kn, m_mla_w_o, m_dil_w_qkv, m_dil_g_qn, m_dil_g_kn, m_dil_w_o, v_ffn1_norm, v_ffn1_w_in, v_ffn1_w_out, v_mix_norm, v_ffn2_norm, v_ffn2_w_in, v_ffn2_w_out, v_mla_w_down, v_mla_g_cq, v_mla_g_ckv, v_mla_w_uq, v_mla_w_ukv, v_mla_g_qn, v_mla_g_kn, v_mla_w_o, v_dil_w_qkv, v_dil_g_qn, v_dil_g_kn, v_dil_w_o):
    names = ["ffn1_norm", "ffn1_w_in", "ffn1_w_out", "mix_norm", "ffn2_norm", "ffn2_w_in", "ffn2_w_out", "mla_w_down",
             "mla_g_cq", "mla_g_ckv", "mla_w_uq", "mla_w_ukv", "mla_g_qn", "mla_g_kn", "mla_w_o", "dil_w_qkv",
             "dil_g_qn", "dil_g_kn", "dil_w_o"]
    W = dict(zip(names, [ffn1_norm, ffn1_w_in, ffn1_w_out, mix_norm, ffn2_norm, ffn2_w_in, ffn2_w_out, mla_w_down,
                         mla_g_cq, mla_g_ckv, mla_w_uq, mla_w_ukv, mla_g_qn, mla_g_kn, mla_w_o, dil_w_qkv,
                         dil_g_qn, dil_g_kn, dil_w_o]))
    M1 = dict(zip(names, [m_ffn1_norm, m_ffn1_w_in, m_ffn1_w_out, m_mix_norm, m_ffn2_norm, m_ffn2_w_in, m_ffn2_w_out,
                          m_mla_w_down, m_mla_g_cq, m_mla_g_ckv, m_mla_w_uq, m_mla_w_ukv, m_mla_g_qn, m_mla_g_kn,
                          m_mla_w_o, m_dil_w_qkv, m_dil_g_qn, m_dil_g_kn, m_dil_w_o]))
    V2 = dict(zip(names, [v_ffn1_norm, v_ffn1_w_in, v_ffn1_w_out, v_mix_norm, v_ffn2_norm, v_ffn2_w_in, v_ffn2_w_out,
                          v_mla_w_down, v_mla_g_cq, v_mla_g_ckv, v_mla_w_uq, v_mla_w_ukv, v_mla_g_qn, v_mla_g_kn,
                          v_mla_w_o, v_dil_w_qkv, v_dil_g_qn, v_dil_g_kn, v_dil_w_o]))
    S, D = x.shape[1], x.shape[2]
    x0 = x.reshape(S, D)
    tgt = loss_target.reshape(S, D)

    big = ["ffn1_w_in", "ffn1_w_out", "ffn2_w_in", "ffn2_w_out", "mla_w_down", "mla_w_uq", "mla_w_ukv", "mla_w_o",
           "dil_w_qkv", "dil_w_o"]
    shard_dim = {"ffn1_w_in": 2, "ffn1_w_out": 1, "ffn2_w_in": 2, "ffn2_w_out": 1, "mla_w_down": 1, "mla_w_uq": 2,
                 "mla_w_ukv": 2, "mla_w_o": 1, "dil_w_qkv": 2, "dil_w_o": 2}
    paired = ("ffn1_w_in", "ffn2_w_in")
    shard_axis = {n: (d, n in paired) for n, d in shard_dim.items()}
    grad_axis = {n: (d - 1, n in paired) for n, d in shard_dim.items()}

    def padded(n, w):
        if n == "mla_w_down":
            return jnp.pad(w, ((0, 0), (0, 0), (0, LAT_PAD - w.shape[2])))
        if n == "mla_w_uq":
            return _pad_heads(w)
        return w

    depth = ffn1_norm.shape[0]
    blocks = []
    for l in range(depth):
        mixer = (["mla_w_down", "mla_w_uq", "mla_w_ukv", "mla_w_o"] if l % 2 == 0 else ["dil_w_qkv", "dil_w_o"])
        blocks.append((f"ffn1_{l}", [("ffn1_w_in", l), ("ffn1_w_out", l)]))
        blocks.append((f"mix_{l}", [(n, l // 2) for n in mixer]))
        blocks.append((f"ffn2_{l}", [("ffn2_w_in", l), ("ffn2_w_out", l)]))
    order = [k for _, keys in blocks for k in keys]
    me = (4 * lax.axis_index("x") + 2 * lax.axis_index("y") + lax.axis_index("c")).astype(jnp.int32).reshape(1)
    def cast(key, deps=()):
        n, l = key
        return _cast_into_gathered(padded(n, W[n]), l, shard_axis[n], me, f"cast_{n}_{l}", deps=deps)

    items0, token0 = _gather_start([cast(order[0])], [shard_axis[order[0][0]]], "gather_start_first")
    rest = order[1:]
    items1, ag_token = _gather_start([cast(k, deps=[token0]) for k in rest], [shard_axis[k[0]] for k in rest],
                                     "gather_start_rest")
    ag_items = dict(zip(order, items0 + items1))
    full = {}

    def relay(keys, after, tag):
        out, token = _gather_relay([ag_items[k] for k in keys], [shard_axis[k[0]] for k in keys], after,
                                   f"gather_relay_{tag}")
        ag_items.update(zip(keys, out))
        return [token]

    def relay_next(bi, after):
        return relay(blocks[bi + 1][1], after, blocks[bi + 1][0]) if bi + 1 < len(blocks) else []

    def fetch(keys, after, tag):
        lands = _gather_wait([ag_items[k] for k in keys], [shard_axis[k[0]] for k in keys], after,
                             f"gather_wait_{tag}")
        full.update(zip(keys, lands))

    g_qn = _pad_heads(mla_g_qn)
    g_kn = _pad_heads(mla_g_kn)
    tabs = _rope_tables(S)
    slopes = jnp.asarray(_alibi_slopes(), F32)

    grads = {}
    gain_g = {}

    out_g, out_d, out_m, out_v = {}, {}, {}, {}
    pending = []
    lag = 3

    def scatter_start(tag, keys):
        items, token = _scatter_start([grads[k] for k in keys], [grad_axis[k[0]] for k in keys],
                                      f"scatter_start_{tag}")
        pending.append((tag, keys, items))
        return token

    def scatter_finish(after):
        tag, keys, items = pending.pop(0)
        lands = _scatter_wait(items, [grad_axis[k[0]] for k in keys], after, f"scatter_wait_{tag}")
        tokens = []
        for (n, l), (own, p) in zip(keys, lands):
            own_axis = grad_axis[n]
            if n in ("mla_w_down", "mla_w_uq"):
                ax, pair = grad_axis[n]
                size = own.shape[ax] // N_DEV
                own = lax.dynamic_slice_in_dim(own, _slot(me[0], pair) * size, size, axis=ax)
                own_axis = None
                if n == "mla_w_down":
                    p, own = p[..., :W[n].shape[2]], own[..., :W[n].shape[2]]
                else:
                    p, own = _unpad_heads(p), _unpad_heads(own)
            prev = (out_g[n], out_d[n], out_m[n], out_v[n]) if n in out_g else None
            (out_g[n], out_d[n], out_m[n], out_v[n]), tok = _adamw(p, own, me, W[n], M1[n], V2[n], l, prev,
                                                                    f"adamw_{n}_{l}", own_axis=own_axis)
            tokens.append(tok)
        return tokens

    def finish_due(after):
        tokens = []
        while len(pending) > lag:
            tokens += scatter_finish(after)
        return tokens

    def mixer_out(o, w_o, xin, l, toks, name):
        tm, tn, tk, _ = _mm_tiles(S, o.shape[1], D, 2, 2, 4, True)
        if tn == D and tk == o.shape[1]:
            return _mm(o, w_o, "nn", F32, name, res=xin, layer=0, deps=toks, norm_gain=ffn2_norm[l:l + 1])
        return _mm(o, w_o, "nn", F32, name, res=xin, layer=0, deps=toks), None

    def mixer_do(dxob, w_o, o, out_dtype, name):
        _, _, tk, _ = _mm_tiles(S, D, o.shape[1], 2, 2, jnp.dtype(out_dtype).itemsize, False)
        if tk == D:
            return _mm(dxob, w_o, "nt", out_dtype, name, layer=0, dot_with=o)
        do = _mm(dxob, w_o, "nt", out_dtype, name, layer=0)
        return do, _attn_delta(do, o, name + "_delta")

    def ffn_fwd(xin, norm_row, which, l, bi, deps=(), h=None):
        tag = blocks[bi][0]
        k_in, k_out = (which + "_w_in", l), (which + "_w_out", l)
        if h is None:
            h = _rms_fwd(xin, norm_row, f"rms_fwd_{tag}", deps=deps)
        if bi == 0:
            relay([k_in], h, f"{tag}_in")
        fetch([k_in], h, f"in_{tag}")
        u, a = _ffn_in(h, full[k_in], f"ffn_in_{tag}")
        if bi == 0:
            relay([k_out], a, f"{tag}_out")
        fetch([k_out], a, f"out_{tag}")
        toks = relay_next(bi, a)
        fused_loss = bi == len(blocks) - 1 and _mm_tiles(S, a.shape[1], D, 2, 2, 4, True)[2] == a.shape[1]
        xo = _mm(a, full[k_out], "nn", F32, f"mm_out_{tag}", scale=0.5, res=xin, layer=0, deps=toks,
                 loss_target=tgt if fused_loss else None)
        return xo, (xin, h, u, a)

    def ffn_bwd(dx_pair, saved, norm_row, which, l, tag, deps=()):
        dxo, dxob = dx_pair
        k_in, k_out = (which + "_w_in", l), (which + "_w_out", l)
        xin, h, u, a = saved
        grads[k_out] = _mm(a, dxob, "tn", BF16, f"mm_dwout_{tag}", scale=0.5, deps=deps)
        t_out = scatter_start(f"{tag}_out", [k_out])
        du = _ffn_da(dxob, full[k_out], u, f"ffn_da_{tag}", deps=[t_out])
        grads[k_in] = _mm(h, du, "tn", BF16, f"mm_dwin_{tag}")
        t_in = scatter_start(f"{tag}_in", [k_in])
        dh = _mm(du, full[k_in], "nt", F32, f"mm_dh_{tag}", layer=0, deps=[t_in])
        toks = finish_due(dh)
        dx, dxb, dg = _rms_bwd(xin, norm_row, dh, dxo, f"rms_bwd_{tag}", deps=toks)
        gain_g.setdefault(which + "_norm", {})[l] = dg
        return dx, dxb

    def mla_fwd(xin, l, bi):
        j = l // 2
        xn = _rms_fwd(xin, mix_norm[l:l + 1], "rms_fwd_mla")
        fetch([(n, j) for n in ("mla_w_down", "mla_w_uq", "mla_w_ukv", "mla_w_o")], xn, "mla")
        lat = _mm(xn, full[("mla_w_down", j)], "nn", F32, "mm_lat", layer=0)
        cq, ckv = _lat_norm_fwd(lat, mla_g_cq[j:j + 1], mla_g_ckv[j:j + 1], "lat_norm_fwd")
        q_raw = _mm(cq, full[("mla_w_uq", j)], "nn", F32, "mm_uq", layer=0)
        kv = _mm(ckv, full[("mla_w_ukv", j)], "nn", F32, "mm_ukv", layer=0)
        qf, kf, vb = _mla_prep_fwd(q_raw, kv, lat, g_qn[j:j + 1], g_kn[j:j + 1], tabs, "mla_prep_fwd")
        o, lse = _flash_fwd(qf, kf, vb, "flash_fwd")
        toks = relay_next(bi, o)
        xo, h_next = mixer_out(o, full[("mla_w_o", j)], xin, l, toks, "mm_mla_o")
        return (xo, h_next), (xin, xn, lat, cq, ckv, q_raw, kv, qf, kf, vb, o, lse)

    def mla_bwd(dx_pair, saved, l):
        dxo, dxob = dx_pair
        j = l // 2
        xin, xn, lat, cq, ckv, q_raw, kv, qf, kf, vb, o, lse = saved
        do, delta = mixer_do(dxob, full[("mla_w_o", j)], o, BF16, "mm_mla_do")
        grads[("mla_w_o", j)] = _mm(o, dxob, "tn", BF16, "mm_mla_dwo")
        dqf, dkf, dv = _flash_bwd(qf, kf, vb, do, lse, delta, "flash_bwd")
        dq_raw, dkv, dkpe, dgq, dgk = _mla_prep_bwd(q_raw, kv, lat, g_qn[j:j + 1], g_kn[j:j + 1], tabs, dqf, dkf, dv,
                                                    "mla_prep_bwd")
        gain_g.setdefault("mla_g_qn", {})[j] = dgq
        gain_g.setdefault("mla_g_kn", {})[j] = dgk
        dcq = _mm(dq_raw, full[("mla_w_uq", j)], "nt", F32, "mm_dcq", layer=0)
        grads[("mla_w_uq", j)] = _mm(cq, dq_raw, "tn", BF16, "mm_dwuq")
        dckv = _mm(dkv, full[("mla_w_ukv", j)], "nt", F32, "mm_dckv", layer=0)
        grads[("mla_w_ukv", j)] = _mm(ckv, dkv, "tn", BF16, "mm_dwukv")
        dlat, dgcq, dgckv = _lat_norm_bwd(lat, mla_g_cq[j:j + 1], mla_g_ckv[j:j + 1], dcq, dckv, dkpe, "lat_norm_bwd")
        gain_g.setdefault("mla_g_cq", {})[j] = dgcq
        gain_g.setdefault("mla_g_ckv", {})[j] = dgckv
        dxn = _mm(dlat, full[("mla_w_down", j)], "nt", F32, "mm_dxn_mla", layer=0)
        grads[("mla_w_down", j)] = _mm(xn, dlat, "tn", BF16, "mm_dwdown")
        tok = scatter_start(f"mix_{l}", [(n, j) for n in ("mla_w_down", "mla_w_uq", "mla_w_ukv", "mla_w_o")])
        toks = finish_due(dxn)
        dx, dxb, dg = _rms_bwd(xin, mix_norm[l:l + 1], dxn, dxo, "rms_bwd_mla", deps=[tok] + toks)
        gain_g.setdefault("mix_norm", {})[l] = dg
        return dx, dxb

    def dil_fwd(xin, l, bi):
        j = l // 2
        xn = _rms_fwd(xin, mix_norm[l:l + 1], "rms_fwd_dil")
        fetch([("dil_w_qkv", j), ("dil_w_o", j)], xn, "dil")
        qkv = _mm(xn, full[("dil_w_qkv", j)], "nn", F32, "mm_qkv", layer=0)
        o_g, lse_g = _dil_fwd(qkv, dil_g_qn[j:j + 1], dil_g_kn[j:j + 1], slopes, "dil_fwd")
        o, lse = _dil_merge(o_g, lse_g, "dil_merge")
        toks = relay_next(bi, o)
        xo, h_next = mixer_out(o, full[("dil_w_o", j)], xin, l, toks, "mm_dil_o")
        return (xo, h_next), (xin, xn, qkv, o, lse)

    def dil_bwd(dx_pair, saved, l):
        dxo, dxob = dx_pair
        j = l // 2
        xin, xn, qkv, o, lse = saved
        do, delta = mixer_do(dxob, full[("dil_w_o", j)], o, F32, "mm_dil_do")
        grads[("dil_w_o", j)] = _mm(o, dxob, "tn", BF16, "mm_dil_dwo")
        dq, dk, dv, dgq, dgk = _dil_bwd(qkv, dil_g_qn[j:j + 1], dil_g_kn[j:j + 1], slopes, do, delta, lse, "dil_bwd")
        gain_g.setdefault("dil_g_qn", {})[j] = dgq
        gain_g.setdefault("dil_g_kn", {})[j] = dgk
        dqkv = jnp.concatenate([dq, dk, dv], axis=1)
        dxn = _mm(dqkv, full[("dil_w_qkv", j)], "nt", F32, "mm_dxn_dil", layer=0)
        grads[("dil_w_qkv", j)] = _mm(xn, dqkv, "tn", BF16, "mm_dwqkv")
        tok = scatter_start(f"mix_{l}", [("dil_w_qkv", j), ("dil_w_o", j)])
        toks = finish_due(dxn)
        dx, dxb, dg = _rms_bwd(xin, mix_norm[l:l + 1], dxn, dxo, "rms_bwd_dil", deps=[tok] + toks)
        gain_g.setdefault("mix_norm", {})[l] = dg
        return dx, dxb

    saved = []
    xc = x0
    for l in range(depth):
        xc, s1 = ffn_fwd(xc, ffn1_norm[l:l + 1], "ffn1", l, 3 * l, deps=[ag_token] if l == 0 else ())
        (xc, h_next), s2 = (mla_fwd if l % 2 == 0 else dil_fwd)(xc, l, 3 * l + 1)
        xc, s3 = ffn_fwd(xc, ffn2_norm[l:l + 1], "ffn2", l, 3 * l + 2, h=h_next)
        saved.append((s1, s2, s3))

    dy, dyb, loss_part = xc if isinstance(xc, tuple) else _loss_head(xc, tgt, "loss_head")
    dx = (dy, dyb)
    loss = lax.psum(loss_part[0, 0], MESH_AXES)

    for bi in reversed(range(len(blocks))):
        tag, _ = blocks[bi]
        l = bi // 3
        s = saved[l][bi % 3]
        if bi % 3 == 2:
            dx = ffn_bwd(dx, s, ffn2_norm[l:l + 1], "ffn2", l, tag,
                         deps=[loss.reshape(1, 1)] if bi == len(blocks) - 1 else ())
        elif bi % 3 == 1:
            dx = (mla_bwd if l % 2 == 0 else dil_bwd)(dx, s, l)
        else:
            dx = ffn_bwd(dx, s, ffn1_norm[l:l + 1], "ffn1", l, tag)
    grad_x = dx[0].reshape(x.shape)
    after = dx[1]
    while pending:
        after = scatter_finish(after)[-1]

    small = [n for n in names if n not in big]

    def gain_local(n):
        rows = [gain_g[n][l] for l in range(W[n].shape[0])]
        g = jnp.concatenate(rows, axis=1)
        return g

    def flat_pad(n, a):
        a = a.reshape(1, -1)
        if n in ("mla_g_qn", "mla_g_kn"):
            a = _pad_heads(a)
        return a

    packed_g = jnp.concatenate([gain_local(n) for n in small], axis=1)
    sizes = [gain_local(n).shape[1] for n in small]
    tot_g = _gain_allreduce(packed_g, "gain_allreduce")
    pw = jnp.concatenate([flat_pad(n, W[n]) for n in small], axis=1)
    pm = jnp.concatenate([flat_pad(n, M1[n]) for n in small], axis=1)
    pv = jnp.concatenate([flat_pad(n, V2[n]) for n in small], axis=1)
    res, _ = _adamw(tot_g.reshape(1, 1, -1), None, me, pw.reshape(1, 1, -1), pm.reshape(1, 1, -1),
                    pv.reshape(1, 1, -1), 0, None, "adamw_gains")
    res = [r.reshape(1, -1) for r in res]
    off = 0
    for n, sz in zip(small, sizes):
        for dst, r in zip((out_g, out_d, out_m, out_v), res):
            piece = r[:, off:off + sz]
            if n in ("mla_g_qn", "mla_g_kn"):
                piece = _unpad_heads(piece)
            dst[n] = piece.reshape(W[n].shape)
        off += sz

    return (loss, grad_x, *[out_g[n] for n in names], *[out_d[n] for n in names],
            *[out_m[n] for n in names], *[out_v[n] for n in names])
```

```python
import functools
import math

import jax
import jax.numpy as jnp
import numpy as np
from jax import lax
from jax.experimental import pallas as pl
from jax.experimental.pallas import tpu as pltpu

EPS = 1e-6
MLA_HEADS = 16
Q_LORA = 512
KV_LORA = 512
NOPE_DIM = 128
ROPE_DIM = 64
V_DIM = 128
QK_DIM = NOPE_DIM + ROPE_DIM
ROPE_THETA = 10000.0
HEAD_PAD = 256
LAT_PAD = Q_LORA + KV_LORA + 128
DIL_PAIRS = ((128, 1), (512, 4), (2048, 16))
DIL_GROUPS = 3
DIL_HEADS = 8
DIL_HEAD_DIM = 128
DIL_BLK = 128
FLASH_HEADS = 4
FLASH_HEADS_FWD = 8
LOG2E = math.log2(math.e)
LN2 = math.log(2.0)
ADAM_LR = 0.001
ADAM_B1 = 0.9
ADAM_B2 = 0.999
ADAM_EPS = 1e-08
ADAM_WD = 0.01
ADAM_STEP = 10

N_DEV = 8
MESH_AXES = ("x", "y", "c")
MESH = pl.DeviceIdType.MESH
NEG_BIG = -1e30
VMEM_LIMIT_V7X = 56 * 1024 * 1024
LANES = 128

BF16 = jnp.bfloat16
F32 = jnp.float32


def _pick(n, cands):
    for c in cands:
        if n % c == 0:
            return c
    raise ValueError(f"no tile for {n}")


def _params(sem):
    return pltpu.CompilerParams(dimension_semantics=sem, vmem_limit_bytes=VMEM_LIMIT_V7X)


ANY_SPEC = pl.BlockSpec(memory_space=pl.ANY)


MM_VMEM_BUDGET = 44 * 1024 * 1024
MM_HBM_BYTES_PER_S = 1.8e12
MM_MXU_FLOPS_PER_S = 8.5e14
MM_STEP_S = 0.4e-6
MM_MAX_TILE_MACS = 3.3e9
MXU_DIM = 256


MM_TIMED_TILES = {
    (2048, 4096, 11264, 2, 2, 2, False): (1024, 1024, 4096, True),
    (4096, 11264, 2048, 2, 2, 4, False): (1024, 1024, 2816, True),
    (4096, 5632, 2048, 2, 2, 4, True): (512, 1024, 5632, False),
    (2048, 4096, 9216, 2, 2, 2, False): (1024, 1024, 4096, True),
    (4096, 9216, 2048, 2, 2, 4, False): (1024, 1024, 2304, True),
}


@functools.lru_cache(maxsize=None)
def _mm_tiles(M, K, N, a_bytes, b_bytes, out_bytes, has_res):
    if (M, K, N, a_bytes, b_bytes, out_bytes, has_res) in MM_TIMED_TILES:
        return MM_TIMED_TILES[(M, K, N, a_bytes, b_bytes, out_bytes, has_res)]
    best = None
    for tk in [K] + [c for c in (1408, 1024, 512, 384, 256, 128) if K % c == 0 and c < K]:
        nk = K // tk
        for tm in [c for c in (2048, 1024, 512, 256, 128) if M % c == 0]:
            for tn in [c for c in (2816, 2048, 1408, 1152, 1024, 512, 384, 256, 128) if N % c == 0]:
                if tm * tk * tn > MM_MAX_TILE_MACS:
                    continue
                fill = (tn / (-(-tn // MXU_DIM) * MXU_DIM)) * (tk / (-(-tk // MXU_DIM) * MXU_DIM))
                fill *= tm / (tm + MXU_DIM // 2)
                vmem = 2 * (tm * tk * a_bytes + tk * tn * b_bytes) + 2 * tm * tn * out_bytes + tm * tn * 4
                vmem += (tm * tk + tk * tn) * 2 if max(a_bytes, b_bytes) > 2 else 0
                vmem += 2 * tm * tn * 4 if has_res else 0
                if vmem > MM_VMEM_BUDGET:
                    continue
                a_all, b_all = M * K * a_bytes, K * N * b_bytes
                if nk == 1:
                    t_i = a_all + (M // tm) * b_all
                    t_j = b_all + (N // tn) * a_all
                    traffic, i_outer = min((t_i, True), (t_j, False))
                else:
                    traffic, i_outer = (N // tn) * a_all + (M // tm) * b_all, True
                traffic += M * N * (out_bytes + (4 if has_res else 0))
                mxu = 2.0 * M * K * N / (MM_MXU_FLOPS_PER_S * fill) * (1.15 if nk > 1 else 1.0)
                cost = max(traffic / MM_HBM_BYTES_PER_S, mxu) + (M // tm) * (N // tn) * nk * MM_STEP_S
                if best is None or cost < best[0]:
                    best = (cost, tm, tn, tk, i_outer)
    assert best is not None, (M, K, N)
    return best[1:]


def _mm(a, b, mode, out_dtype, name, *, scale=1.0, res=None, layer=None, deps=(), norm_gain=None, dot_with=None,
        loss_target=None):
    b2 = b.shape[-2:]
    if mode == "nn":
        (M, K), (Kb, N) = a.shape, b2
    elif mode == "nt":
        (M, K), (N, Kb) = a.shape, b2
    else:
        (K, M), (Kb, N) = a.shape, b2
    assert K == Kb, (a.shape, b.shape, mode)
    tm, tn, tk, i_outer = _mm_tiles(M, K, N, a.dtype.itemsize, b.dtype.itemsize, jnp.dtype(out_dtype).itemsize,
                                    res is not None)
    nk = K // tk
    dims = {"nn": (((1,), (0,)), ((), ())), "nt": (((1,), (1,)), ((), ())), "tn": (((0,), (0,)), ((), ()))}[mode]
    normed = norm_gain is not None
    dotted = dot_with is not None
    assert not normed or (nk == 1 and tn == N), (name, tn, N, nk)
    assert not dotted or (nk == 1 and not normed and tn % LANES == 0), (name, tn, nk)
    lossy = loss_target is not None
    assert not lossy or (nk == 1 and res is not None and not normed and not dotted), name
    n_in = 2 + (res is not None) + normed + dotted + lossy + len(deps)

    def finish(v, r_ref, o_ref):
        if scale != 1.0:
            v = v * scale
        if r_ref is not None:
            v = r_ref[...] + v
        o_ref[...] = v.astype(o_ref.dtype)
        return v

    def body(*refs):
        a_ref, b_ref = refs[:2]
        r_ref = refs[2] if res is not None else None
        prod = lambda: lax.dot_general(a_ref[...].astype(BF16), b_ref[...].astype(BF16), dims,
                                       preferred_element_type=F32)
        if lossy:
            t_ref, (dy_ref, dyb_ref, l_ref) = refs[3], refs[n_in:n_in + 3]
            e = r_ref[...] + prod() * scale - t_ref[...]
            dy = e * (1.0 / N)
            dy_ref[...] = dy
            dyb_ref[...] = dy.astype(BF16)

            @pl.when((pl.program_id(0) == 0) & (pl.program_id(1) == 0))
            def _():
                l_ref[...] = jnp.zeros_like(l_ref)

            l_ref[...] += (0.5 / N) * jnp.sum(jnp.sum(e * e, axis=-1, keepdims=True), axis=0, keepdims=True)
            return
        if nk == 1:
            v = finish(prod(), r_ref, refs[n_in])
            if normed:
                g_ref = refs[2 + (res is not None)]
                r = lax.rsqrt(jnp.mean(v * v, axis=-1, keepdims=True) + EPS)
                refs[n_in + 1][...] = ((v * r) * g_ref[...]).astype(BF16)
            if dotted:
                w_ref, d_ref = refs[2 + (res is not None)], refs[n_in + 1]
                for h in range(tn // LANES):
                    sl = slice(h * LANES, (h + 1) * LANES)
                    d = jnp.sum(v[:, sl] * w_ref[:, sl].astype(F32), axis=-1, keepdims=True)
                    d_ref[:, sl] = jnp.broadcast_to(d, (tm, LANES))
            return
        o_ref, acc = refs[-2:]
        k = pl.program_id(2)

        @pl.when(k == 0)
        def _():
            acc[...] = prod()

        @pl.when(k > 0)
        def _():
            acc[...] += prod()

        @pl.when(k == nk - 1)
        def _():
            finish(acc[...], r_ref, o_ref)

    ij = (lambda p, q: (p, q)) if i_outer else (lambda p, q: (q, p))

    def spec(shape, f, lead=None):
        full = lambda p, q, k: f(*ij(p, q), k)
        if lead is None:
            return pl.BlockSpec(shape, full)
        return pl.BlockSpec((None,) + shape, lambda p, q, k: (lead,) + full(p, q, k))

    a_spec = spec((tk, tm), lambda i, j, k: (k, i)) if mode == "tn" else spec((tm, tk), lambda i, j, k: (i, k))
    lead = layer if b.ndim == 3 else None
    b_spec = spec((tn, tk), lambda i, j, k: (j, k), lead) if mode == "nt" else spec((tk, tn), lambda i, j, k: (k, j), lead)
    in_specs = [a_spec, b_spec]
    args = [a, b]
    if res is not None:
        in_specs.append(spec((tm, tn), lambda i, j, k: (i, j)))
        args.append(res)
    tile = spec((tm, tn), lambda i, j, k: (i, j))
    if normed:
        in_specs.append(pl.BlockSpec((1, N), lambda p, q, k: (0, 0)))
        args.append(norm_gain)
    if dotted:
        in_specs.append(tile)
        args.append(dot_with)
    if lossy:
        in_specs.append(tile)
        args.append(loss_target)
    in_specs += [ANY_SPEC] * len(deps)
    args += list(deps)
    outer, inner = (M // tm, N // tn) if i_outer else (N // tn, M // tm)
    if lossy:
        return tuple(pl.pallas_call(
            body, name=name, grid=(outer, inner, nk), in_specs=in_specs,
            out_specs=[tile, tile, pl.BlockSpec((1, 1), lambda p, q, k: (0, 0))],
            out_shape=[jax.ShapeDtypeStruct((M, N), F32), jax.ShapeDtypeStruct((M, N), BF16),
                       jax.ShapeDtypeStruct((1, 1), F32)],
            compiler_params=_params(("arbitrary", "arbitrary", "arbitrary")),
        )(*args))
    second = [jax.ShapeDtypeStruct((M, N), BF16)] if normed else [jax.ShapeDtypeStruct((M, N), F32)] if dotted else []
    out = pl.pallas_call(
        body, name=name, grid=(outer, inner, nk),
        in_specs=in_specs, out_specs=[tile, tile] if second else tile,
        out_shape=[jax.ShapeDtypeStruct((M, N), out_dtype)] + second if second else jax.ShapeDtypeStruct((M, N), out_dtype),
        scratch_shapes=[pltpu.VMEM((tm, tn), F32)] if nk > 1 else [],
        compiler_params=_params(("parallel", "parallel", "arbitrary")),
    )(*args)
    return tuple(out) if second else out


def _cast_into_gathered(w, layer, axis, me, name, deps=()):
    _, R, C = w.shape
    tr = _pick(R, (512, 256, 128, 64, 32, 16))
    nr = R // tr
    axis, paired = axis

    def body(me_ref, w_ref, *rest):
        o_ref = rest[-1]
        o_ref[...] = w_ref[...].astype(BF16)

    if axis == 1:
        out_idx = lambda i, me_ref: (0, _slot(me_ref[0], paired) * nr + i, 0)
        shape = (1, R * N_DEV, C)
    else:
        out_idx = lambda i, me_ref: (0, i, _slot(me_ref[0], paired))
        shape = (1, R, C * N_DEV)
    return pl.pallas_call(
        body, name=name,
        grid_spec=pltpu.PrefetchScalarGridSpec(
            num_scalar_prefetch=1, grid=(nr,),
            in_specs=[pl.BlockSpec((None, tr, C), lambda i, me_ref: (layer, i, 0))] + [ANY_SPEC] * len(deps),
            out_specs=pl.BlockSpec((None, tr, C), out_idx)),
        out_shape=jax.ShapeDtypeStruct(shape, BF16), compiler_params=_params(("parallel",)),
    )(me, w, *deps)


def _rms_fwd(x, g, name, deps=()):
    T, D = x.shape
    tr = _pick(T, (512, 256, 128))

    def body(x_ref, g_ref, *rest):
        o_ref = rest[-1]
        xv = x_ref[...]
        r = lax.rsqrt(jnp.mean(xv * xv, axis=-1, keepdims=True) + EPS)
        o_ref[...] = ((xv * r) * g_ref[...]).astype(BF16)

    return pl.pallas_call(
        body, name=name, grid=(T // tr,),
        in_specs=[pl.BlockSpec((tr, D), lambda i: (i, 0)), pl.BlockSpec((1, D), lambda i: (0, 0))]
        + [ANY_SPEC] * len(deps),
        out_specs=pl.BlockSpec((tr, D), lambda i: (i, 0)),
        out_shape=jax.ShapeDtypeStruct((T, D), BF16), compiler_params=_params(("parallel",)),
    )(x, g, *deps)


def _rms_bwd(x, g, dh, dres, name, deps=()):
    T, D = x.shape
    tr = _pick(T, (256, 128))

    def body(x_ref, g_ref, dh_ref, dres_ref, *rest):
        dx_ref, dxb_ref, dg_ref = rest[-3:]
        xv = x_ref[...]
        dhv = dh_ref[...]
        r = lax.rsqrt(jnp.mean(xv * xv, axis=-1, keepdims=True) + EPS)
        xhat = xv * r
        dxh = dhv * g_ref[...]
        c = jnp.mean(dxh * xhat, axis=-1, keepdims=True)
        dx = dres_ref[...] + r * (dxh - xhat * c)
        dx_ref[...] = dx
        dxb_ref[...] = dx.astype(BF16)

        @pl.when(pl.program_id(0) == 0)
        def _():
            dg_ref[...] = jnp.zeros_like(dg_ref)

        dg_ref[...] += jnp.sum(dhv * xhat, axis=0, keepdims=True)

    row = pl.BlockSpec((tr, D), lambda i: (i, 0))
    vec = pl.BlockSpec((1, D), lambda i: (0, 0))
    return pl.pallas_call(
        body, name=name, grid=(T // tr,),
        in_specs=[row, vec, row, row] + [ANY_SPEC] * len(deps), out_specs=[row, row, vec],
        out_shape=[jax.ShapeDtypeStruct((T, D), F32), jax.ShapeDtypeStruct((T, D), BF16),
                   jax.ShapeDtypeStruct((1, D), F32)],
        compiler_params=_params(("arbitrary",)),
    )(x, g, dh, dres, *deps)


N_PANEL = N_DEV // 2


def _ffn_in(h, w_in, name):
    T, D = h.shape
    F2 = w_in.shape[2]
    pw = F2 // N_PANEL
    half = pw // 2
    tm = _pick(T, (512, 256, 128))

    def body(h_ref, w_ref, u_ref, a_ref):
        r = jnp.dot(h_ref[...], w_ref[...], preferred_element_type=F32)
        u_ref[...] = r.astype(BF16)
        g, up = r[:, :half], r[:, half:]
        a_ref[...] = (g * jax.nn.sigmoid(g) * up).astype(BF16)

    return pl.pallas_call(
        body, name=name, grid=(N_PANEL, T // tm),
        in_specs=[pl.BlockSpec((tm, D), lambda p, i: (i, 0)), pl.BlockSpec((None, D, pw), lambda p, i: (0, 0, p))],
        out_specs=[pl.BlockSpec((tm, pw), lambda p, i: (i, p)), pl.BlockSpec((tm, half), lambda p, i: (i, p))],
        out_shape=[jax.ShapeDtypeStruct((T, F2), BF16), jax.ShapeDtypeStruct((T, F2 // 2), BF16)],
        compiler_params=_params(("parallel", "parallel")),
    )(h, w_in)


def _ffn_da(dxo, w_out, u, name, deps=()):
    T, D = dxo.shape
    F2 = u.shape[1]
    pw = F2 // N_PANEL
    half = pw // 2
    tm = _pick(T, (512, 256, 128))

    def body(d_ref, w_ref, u_ref, *rest):
        du_ref = rest[-1]
        da = 0.5 * lax.dot_general(d_ref[...], w_ref[...], NT_DIMS, preferred_element_type=F32)
        g = u_ref[:, :half].astype(F32)
        up = u_ref[:, half:].astype(F32)
        sg = jax.nn.sigmoid(g)
        silu = g * sg
        du_ref[:, :half] = (da * up * (sg + silu * (1.0 - sg))).astype(BF16)
        du_ref[:, half:] = (da * silu).astype(BF16)

    return pl.pallas_call(
        body, name=name, grid=(N_PANEL, T // tm),
        in_specs=[pl.BlockSpec((tm, D), lambda p, i: (i, 0)), pl.BlockSpec((None, half, D), lambda p, i: (0, p, 0)),
                  pl.BlockSpec((tm, pw), lambda p, i: (i, p))] + [ANY_SPEC] * len(deps),
        out_specs=pl.BlockSpec((tm, pw), lambda p, i: (i, p)),
        out_shape=jax.ShapeDtypeStruct((T, F2), BF16), compiler_params=_params(("parallel", "parallel")),
    )(dxo, w_out, u, *deps)


def _loss_head(y, t, name):
    T, D = y.shape
    tr = _pick(T, (512, 256, 128))

    def body(y_ref, t_ref, dy_ref, dyb_ref, l_ref):
        e = y_ref[...] - t_ref[...]
        dy = e * (1.0 / D)
        dy_ref[...] = dy
        dyb_ref[...] = dy.astype(BF16)

        @pl.when(pl.program_id(0) == 0)
        def _():
            l_ref[...] = jnp.zeros_like(l_ref)

        l_ref[...] += 0.5 * jnp.sum(jnp.mean(e * e, axis=-1, keepdims=True), axis=0, keepdims=True)

    row = pl.BlockSpec((tr, D), lambda i: (i, 0))
    return pl.pallas_call(
        body, name=name, grid=(T // tr,),
        in_specs=[row, row], out_specs=[row, row, pl.BlockSpec((1, 1), lambda i: (0, 0))],
        out_shape=[jax.ShapeDtypeStruct((T, D), F32), jax.ShapeDtypeStruct((T, D), BF16),
                   jax.ShapeDtypeStruct((1, 1), F32)],
        compiler_params=_params(("arbitrary",)),
    )(y, t)


def _rope_tables(S):
    half = ROPE_DIM // 2
    inv = 1.0 / (ROPE_THETA ** (jnp.arange(0, ROPE_DIM, 2, dtype=F32) / ROPE_DIM))
    ang = jnp.arange(S, dtype=F32)[:, None] * inv[None, :]
    cos, sin = jnp.cos(ang), jnp.sin(ang)
    z = jnp.zeros((S, half), F32)
    z2 = jnp.zeros((S, LANES - ROPE_DIM), F32)
    c = jnp.concatenate([cos, cos, z2], axis=1)
    s1 = jnp.concatenate([-sin, z, z2], axis=1)
    s2 = jnp.concatenate([z, sin, z2], axis=1)
    return c, s1, s2


def _rope(r, c, s1, s2):
    return r * c + pltpu.roll(r, LANES - ROPE_DIM // 2, 1) * s1 + pltpu.roll(r, ROPE_DIM // 2, 1) * s2


def _rope_t(d, c, s1, s2):
    return d * c + pltpu.roll(d * s1, ROPE_DIM // 2, 1) + pltpu.roll(d * s2, LANES - ROPE_DIM // 2, 1)


def _lat_norm_fwd(lat, g_cq, g_ckv, name):
    T = lat.shape[0]
    tr = _pick(T, (512, 256, 128))

    def body(lat_ref, gq_ref, gk_ref, cq_ref, ckv_ref):
        for off, g_ref, o_ref in ((0, gq_ref, cq_ref), (Q_LORA, gk_ref, ckv_ref)):
            xv = lat_ref[:, off:off + Q_LORA]
            r = lax.rsqrt(jnp.mean(xv * xv, axis=-1, keepdims=True) + EPS)
            o_ref[...] = ((xv * r) * g_ref[...]).astype(BF16)

    vec = pl.BlockSpec((1, Q_LORA), lambda i: (0, 0))
    out = pl.BlockSpec((tr, Q_LORA), lambda i: (i, 0))
    return pl.pallas_call(
        body, name=name, grid=(T // tr,),
        in_specs=[pl.BlockSpec((tr, LAT_PAD), lambda i: (i, 0)), vec, vec], out_specs=[out, out],
        out_shape=[jax.ShapeDtypeStruct((T, Q_LORA), BF16)] * 2, compiler_params=_params(("parallel",)),
    )(lat, g_cq, g_ckv)


def _lat_norm_bwd(lat, g_cq, g_ckv, dcq, dckv, dkpe, name):
    T = lat.shape[0]
    tr = _pick(T, (256, 128))

    def body(lat_ref, gq_ref, gk_ref, dcq_ref, dckv_ref, dkpe_ref, dlat_ref, dgq_ref, dgk_ref):
        @pl.when(pl.program_id(0) == 0)
        def _():
            dgq_ref[...] = jnp.zeros_like(dgq_ref)
            dgk_ref[...] = jnp.zeros_like(dgk_ref)

        for off, g_ref, d_ref, dg_ref in ((0, gq_ref, dcq_ref, dgq_ref), (Q_LORA, gk_ref, dckv_ref, dgk_ref)):
            xv = lat_ref[:, off:off + Q_LORA]
            dv = d_ref[...]
            r = lax.rsqrt(jnp.mean(xv * xv, axis=-1, keepdims=True) + EPS)
            xhat = xv * r
            dxh = dv * g_ref[...]
            c = jnp.mean(dxh * xhat, axis=-1, keepdims=True)
            dlat_ref[:, off:off + Q_LORA] = (r * (dxh - xhat * c)).astype(BF16)
            dg_ref[...] += jnp.sum(dv * xhat, axis=0, keepdims=True)
        dlat_ref[:, Q_LORA + KV_LORA:] = dkpe_ref[...].astype(BF16)

    vec = pl.BlockSpec((1, Q_LORA), lambda i: (0, 0))
    half = pl.BlockSpec((tr, Q_LORA), lambda i: (i, 0))
    full = pl.BlockSpec((tr, LAT_PAD), lambda i: (i, 0))
    return pl.pallas_call(
        body, name=name, grid=(T // tr,),
        in_specs=[full, vec, vec, half, half, pl.BlockSpec((tr, LANES), lambda i: (i, 0))],
        out_specs=[full, vec, vec],
        out_shape=[jax.ShapeDtypeStruct((T, LAT_PAD), BF16), jax.ShapeDtypeStruct((1, Q_LORA), F32),
                   jax.ShapeDtypeStruct((1, Q_LORA), F32)],
        compiler_params=_params(("arbitrary",)),
    )(lat, g_cq, g_ckv, dcq, dckv, dkpe)


def _mla_prep_fwd(q_raw, kv, lat, g_qn, g_kn, tabs, name):
    T = q_raw.shape[0]
    H = MLA_HEADS
    tr = _pick(T, (256, 128))
    scale = LOG2E / math.sqrt(QK_DIM)

    def body(q_ref, kv_ref, kpe_ref, gq_ref, gk_ref, c_ref, s1_ref, s2_ref, qf_ref, kf_ref, v_ref):
        c, s1, s2 = c_ref[...], s1_ref[...], s2_ref[...]
        gq, gk = gq_ref[...], gk_ref[...]
        kpe = kpe_ref[...]
        kpe_ss = jnp.sum(kpe * kpe, axis=-1, keepdims=True)
        for h in range(H):
            lo = h * HEAD_PAD
            qa = q_ref[:, lo:lo + LANES]
            qb = q_ref[:, lo + LANES:lo + HEAD_PAD]
            ss = jnp.sum(qa * qa + qb * qb, axis=-1, keepdims=True)
            r = lax.rsqrt(ss * (1.0 / QK_DIM) + EPS)
            qf_ref[:, lo:lo + LANES] = (qa * r * gq[:, :LANES] * scale).astype(BF16)
            qf_ref[:, lo + LANES:lo + HEAD_PAD] = (_rope(qb * r * gq[:, LANES:], c, s1, s2) * scale).astype(BF16)
            ka = kv_ref[:, lo:lo + LANES]
            ss = jnp.sum(ka * ka, axis=-1, keepdims=True) + kpe_ss
            r = lax.rsqrt(ss * (1.0 / QK_DIM) + EPS)
            kf_ref[:, lo:lo + LANES] = (ka * r * gk[:, :LANES]).astype(BF16)
            kf_ref[:, lo + LANES:lo + HEAD_PAD] = _rope(kpe * r * gk[:, LANES:], c, s1, s2).astype(BF16)
            v_ref[:, lo:lo + V_DIM] = kv_ref[:, lo + LANES:lo + HEAD_PAD].astype(BF16)
            v_ref[:, lo + V_DIM:lo + HEAD_PAD] = jnp.ones((tr, HEAD_PAD - V_DIM), BF16)

    wide = pl.BlockSpec((tr, H * HEAD_PAD), lambda i: (i, 0))
    lane = pl.BlockSpec((tr, LANES), lambda i: (i, 0))
    gvec = pl.BlockSpec((1, HEAD_PAD), lambda i: (0, 0))
    return pl.pallas_call(
        body, name=name, grid=(T // tr,),
        in_specs=[wide, wide, pl.BlockSpec((tr, LANES), lambda i: (i, (Q_LORA + KV_LORA) // LANES)), gvec, gvec,
                  lane, lane, lane],
        out_specs=[wide, wide, wide],
        out_shape=[jax.ShapeDtypeStruct((T, H * HEAD_PAD), BF16)] * 3,
        compiler_params=_params(("parallel",)),
    )(q_raw, kv, lat, g_qn, g_kn, *tabs)


def _mla_prep_bwd(q_raw, kv, lat, g_qn, g_kn, tabs, dqf, dkf, dv, name):
    T = q_raw.shape[0]
    H = MLA_HEADS
    tr = _pick(T, (128,))

    def body(q_ref, kv_ref, kpe_ref, gq_ref, gk_ref, c_ref, s1_ref, s2_ref, dqf_ref, dkf_ref, dv_ref,
             dq_ref, dkv_ref, dkpe_ref, dgq_ref, dgk_ref):
        @pl.when(pl.program_id(0) == 0)
        def _():
            dgq_ref[...] = jnp.zeros_like(dgq_ref)
            dgk_ref[...] = jnp.zeros_like(dgk_ref)

        c, s1, s2 = c_ref[...], s1_ref[...], s2_ref[...]
        gq, gk = gq_ref[...], gk_ref[...]
        kpe = kpe_ref[...]
        kpe_ss = jnp.sum(kpe * kpe, axis=-1, keepdims=True)
        dkpe = jnp.zeros_like(kpe)
        dgq_a = jnp.zeros((1, LANES), F32)
        dgq_b = jnp.zeros((1, LANES), F32)
        dgk_a = jnp.zeros((1, LANES), F32)
        dgk_b = jnp.zeros((1, LANES), F32)
        for h in range(H):
            lo = h * HEAD_PAD
            xa = q_ref[:, lo:lo + LANES]
            xb = q_ref[:, lo + LANES:lo + HEAD_PAD]
            ss = jnp.sum(xa * xa + xb * xb, axis=-1, keepdims=True)
            r = lax.rsqrt(ss * (1.0 / QK_DIM) + EPS)
            xa, xb = xa * r, xb * r
            da = dqf_ref[:, lo:lo + LANES].astype(F32)
            db = _rope_t(dqf_ref[:, lo + LANES:lo + HEAD_PAD].astype(F32), c, s1, s2)
            dgq_a += jnp.sum(da * xa, axis=0, keepdims=True)
            dgq_b += jnp.sum(db * xb, axis=0, keepdims=True)
            da, db = da * gq[:, :LANES], db * gq[:, LANES:]
            cc = jnp.sum(da * xa + db * xb, axis=-1, keepdims=True) * (1.0 / QK_DIM)
            dq_ref[:, lo:lo + LANES] = (r * (da - xa * cc)).astype(BF16)
            dq_ref[:, lo + LANES:lo + HEAD_PAD] = (r * (db - xb * cc)).astype(BF16)
            xa = kv_ref[:, lo:lo + LANES]
            ss = jnp.sum(xa * xa, axis=-1, keepdims=True) + kpe_ss
            r = lax.rsqrt(ss * (1.0 / QK_DIM) + EPS)
            xa, xb = xa * r, kpe * r
            da = dkf_ref[:, lo:lo + LANES].astype(F32)
            db = _rope_t(dkf_ref[:, lo + LANES:lo + HEAD_PAD].astype(F32), c, s1, s2)
            dgk_a += jnp.sum(da * xa, axis=0, keepdims=True)
            dgk_b += jnp.sum(db * xb, axis=0, keepdims=True)
            da, db = da * gk[:, :LANES], db * gk[:, LANES:]
            cc = jnp.sum(da * xa + db * xb, axis=-1, keepdims=True) * (1.0 / QK_DIM)
            dkv_ref[:, lo:lo + LANES] = (r * (da - xa * cc)).astype(BF16)
            dkpe = dkpe + r * (db - xb * cc)
            dkv_ref[:, lo + LANES:lo + HEAD_PAD] = dv_ref[:, h * V_DIM:(h + 1) * V_DIM].astype(BF16)
        dkpe_ref[...] = dkpe
        dgq_ref[:, :LANES] += dgq_a
        dgq_ref[:, LANES:] += dgq_b
        dgk_ref[:, :LANES] += dgk_a
        dgk_ref[:, LANES:] += dgk_b

    wide = pl.BlockSpec((tr, H * HEAD_PAD), lambda i: (i, 0))
    lane = pl.BlockSpec((tr, LANES), lambda i: (i, 0))
    gvec = pl.BlockSpec((1, HEAD_PAD), lambda i: (0, 0))
    vspec = pl.BlockSpec((tr, H * V_DIM), lambda i: (i, 0))
    return pl.pallas_call(
        body, name=name, grid=(T // tr,),
        in_specs=[wide, wide, pl.BlockSpec((tr, LANES), lambda i: (i, (Q_LORA + KV_LORA) // LANES)), gvec, gvec,
                  lane, lane, lane, wide, wide, vspec],
        out_specs=[wide, wide, lane, gvec, gvec],
        out_shape=[jax.ShapeDtypeStruct((T, H * HEAD_PAD), BF16), jax.ShapeDtypeStruct((T, H * HEAD_PAD), BF16),
                   jax.ShapeDtypeStruct((T, LANES), F32), jax.ShapeDtypeStruct((1, HEAD_PAD), F32),
                   jax.ShapeDtypeStruct((1, HEAD_PAD), F32)],
        compiler_params=_params(("arbitrary",)),
    )(q_raw, kv, lat, g_qn, g_kn, *tabs, dqf, dkf, dv)


def _causal_mask(tq, tk):
    return lax.broadcasted_iota(jnp.int32, (tq, tk), 1) <= lax.broadcasted_iota(jnp.int32, (tq, tk), 0)


NT_DIMS = (((1,), (1,)), ((), ()))
TN_DIMS = (((0,), (0,)), ((), ()))


def _flash_fwd(qf, kf, v, name):
    T = qf.shape[0]
    H, G = MLA_HEADS, FLASH_HEADS_FWD
    t = _pick(T, (512, 256, 128))
    n = T // t
    pairs = [(i, j) for i in range(n) for j in range(i + 1)]
    qi = jnp.asarray([p[0] for p in pairs], jnp.int32)
    kj = jnp.asarray([p[1] for p in pairs], jnp.int32)

    def body(qi_ref, kj_ref, q_ref, k_ref, v_ref, o_ref, lse_ref, *scratch):
        m_sc, acc_sc = scratch[:G], scratch[G:]
        sid = pl.program_id(1)
        i, j = qi_ref[sid], kj_ref[sid]

        @pl.when(j == 0)
        def _():
            for g in range(G):
                m_sc[g][...] = jnp.full_like(m_sc[g], NEG_BIG)
                acc_sc[g][...] = jnp.zeros_like(acc_sc[g])

        def step(masked):
            for g in range(G):
                qk = slice(g * HEAD_PAD, (g + 1) * HEAD_PAD)
                s = lax.dot_general(q_ref[:, qk], k_ref[:, qk], NT_DIMS, preferred_element_type=F32)
                if masked:
                    s = jnp.where(_causal_mask(t, t), s, NEG_BIG)
                m_prev = m_sc[g][:, :1]
                m_new = jnp.maximum(m_prev, jnp.max(s, axis=-1, keepdims=True))
                a = jnp.exp2(m_prev - m_new)
                p = jnp.exp2((s - m_new).astype(BF16))
                acc_sc[g][...] = a * acc_sc[g][...] + jnp.dot(p, v_ref[:, qk], preferred_element_type=F32)
                m_sc[g][...] = jnp.broadcast_to(m_new, (t, LANES))

        @pl.when(j < i)
        def _():
            step(False)

        @pl.when(j == i)
        def _():
            step(True)
            for g in range(G):
                vo = slice(g * V_DIM, (g + 1) * V_DIM)
                l = acc_sc[g][:, V_DIM:]
                o_ref[:, vo] = (acc_sc[g][:, :V_DIM] / l).astype(BF16)
                lse_ref[:, vo] = m_sc[g][...] + jnp.log2(l)

    row = pl.BlockSpec((t, G * V_DIM), lambda h, s, qi, kj: (qi[s], h))
    return pl.pallas_call(
        body, name=name,
        grid_spec=pltpu.PrefetchScalarGridSpec(
            num_scalar_prefetch=2, grid=(H // G, len(pairs)),
            in_specs=[pl.BlockSpec((t, G * HEAD_PAD), lambda h, s, qi, kj: (qi[s], h)),
                      pl.BlockSpec((t, G * HEAD_PAD), lambda h, s, qi, kj: (kj[s], h)),
                      pl.BlockSpec((t, G * HEAD_PAD), lambda h, s, qi, kj: (kj[s], h))],
            out_specs=[row, row],
            scratch_shapes=[pltpu.VMEM((t, LANES), F32)] * G + [pltpu.VMEM((t, HEAD_PAD), F32)] * G),
        out_shape=[jax.ShapeDtypeStruct((T, H * V_DIM), BF16), jax.ShapeDtypeStruct((T, H * V_DIM), F32)],
        compiler_params=_params(("parallel", "arbitrary")),
    )(qi, kj, qf, kf, v)


def _attn_delta(do, o, name):
    T, W = do.shape
    nh = W // V_DIM
    tr = _pick(T, (512, 256, 128))

    def body(do_ref, o_ref, d_ref):
        for h in range(nh):
            sl = slice(h * V_DIM, (h + 1) * V_DIM)
            d = jnp.sum(do_ref[:, sl].astype(F32) * o_ref[:, sl].astype(F32), axis=-1, keepdims=True)
            d_ref[:, sl] = jnp.broadcast_to(d, (tr, V_DIM))

    row = pl.BlockSpec((tr, W), lambda i: (i, 0))
    return pl.pallas_call(
        body, name=name, grid=(T // tr,), in_specs=[row, row], out_specs=row,
        out_shape=jax.ShapeDtypeStruct((T, W), F32), compiler_params=_params(("parallel",)),
    )(do, o)


def _flash_bwd(qf, kf, v, do, lse, delta, name):
    T = qf.shape[0]
    H, G = MLA_HEADS, FLASH_HEADS
    t = _pick(T, (512, 256, 128))
    n = T // t
    scale = 1.0 / math.sqrt(QK_DIM)
    pairs = [(i, j) for j in range(n) for i in range(j, n)]
    qi = jnp.asarray([p[0] for p in pairs], jnp.int32)
    kj = jnp.asarray([p[1] for p in pairs], jnp.int32)

    def body(qi_ref, kj_ref, q_ref, k_ref, v_ref, do_ref, lse_ref, dl_ref, dq_ref, dk_ref, dv_ref,
             dq_acc, dk_acc, dv_acc):
        sid = pl.program_id(1)
        i, j = qi_ref[sid], kj_ref[sid]

        @pl.when(sid == 0)
        def _():
            dq_acc[...] = jnp.zeros_like(dq_acc)

        def step(masked):
            rows = pl.ds(pl.multiple_of(i * t, t), t)
            for g in range(G):
                qk = slice(g * HEAD_PAD, (g + 1) * HEAD_PAD)
                vo = slice(g * V_DIM, (g + 1) * V_DIM)
                q, k, do_ = q_ref[:, qk], k_ref[:, qk], do_ref[:, vo]
                v_ = v_ref[:, g * HEAD_PAD:g * HEAD_PAD + V_DIM]
                s = lax.dot_general(q, k, NT_DIMS, preferred_element_type=F32)
                if masked:
                    s = jnp.where(_causal_mask(t, t), s, NEG_BIG)
                p = jnp.exp2(s - lse_ref[:, g * V_DIM:g * V_DIM + 1])
                dp = lax.dot_general(do_, v_, NT_DIMS, preferred_element_type=F32)
                ds = (p * (dp - dl_ref[:, g * V_DIM:g * V_DIM + 1])).astype(BF16)
                dv = lax.dot_general(p.astype(BF16), do_, TN_DIMS, preferred_element_type=F32)
                dk = lax.dot_general(ds, q, TN_DIMS, preferred_element_type=F32)
                if masked:
                    dv_acc[:, vo] = dv
                    dk_acc[:, qk] = dk
                else:
                    dv_acc[:, vo] += dv
                    dk_acc[:, qk] += dk
                dq_acc[rows, qk] += jnp.dot(ds, k, preferred_element_type=F32) * scale

        @pl.when(i == j)
        def _():
            step(True)

        @pl.when(i > j)
        def _():
            step(False)

        @pl.when(i == n - 1)
        def _():
            dk_ref[...] = (dk_acc[...] * LN2).astype(BF16)
            dv_ref[...] = dv_acc[...].astype(BF16)

        @pl.when(sid == len(pairs) - 1)
        def _():
            dq_ref[...] = dq_acc[...].astype(BF16)

    qs = pl.BlockSpec((t, G * HEAD_PAD), lambda h, s, qi, kj: (qi[s], h))
    rs = pl.BlockSpec((t, G * V_DIM), lambda h, s, qi, kj: (qi[s], h))
    ks = pl.BlockSpec((t, G * HEAD_PAD), lambda h, s, qi, kj: (kj[s], h))
    vs = pl.BlockSpec((t, G * V_DIM), lambda h, s, qi, kj: (kj[s], h))
    return pl.pallas_call(
        body, name=name,
        grid_spec=pltpu.PrefetchScalarGridSpec(
            num_scalar_prefetch=2, grid=(H // G, len(pairs)), in_specs=[qs, ks, ks, rs, rs, rs],
            out_specs=[pl.BlockSpec((T, G * HEAD_PAD), lambda h, s, qi, kj: (0, h)), ks, vs],
            scratch_shapes=[pltpu.VMEM((T, G * HEAD_PAD), F32), pltpu.VMEM((t, G * HEAD_PAD), F32),
                            pltpu.VMEM((t, G * V_DIM), F32)]),
        out_shape=[jax.ShapeDtypeStruct((T, H * HEAD_PAD), BF16), jax.ShapeDtypeStruct((T, H * HEAD_PAD), BF16),
                   jax.ShapeDtypeStruct((T, H * V_DIM), BF16)],
        compiler_params=_params(("parallel", "arbitrary")),
    )(qi, kj, qf, kf, v, do, lse, delta)


def _alibi_slopes():
    tot = DIL_GROUPS * DIL_HEADS
    return [float(np.float32(2.0) ** (np.float32(-8.0) * np.float32(k) / np.float32(tot))) for k in range(1, tot + 1)]


def _dil_masks():
    iq = lax.broadcasted_iota(jnp.int32, (DIL_BLK, DIL_BLK), 0)
    ik = lax.broadcasted_iota(jnp.int32, (DIL_BLK, DIL_BLK), 1)
    return (ik >= iq), (iq + DIL_BLK - ik).astype(F32), (ik <= iq), (iq - ik).astype(F32)


def _dil_norm(x, g):
    r = lax.rsqrt(jnp.mean(x * x, axis=-1, keepdims=True) + EPS)
    return x * r, r


DIL_SUPER = 8
BNT_DIMS = (((2,), (2,)), ((0,), (0,)))
BNN_DIMS = (((2,), (1,)), ((0,), (0,)))
BTN_DIMS = (((1,), (1,)), ((0,), (0,)))


def _dil_chunk(it, nb, d):
    assert nb & (nb - 1) == 0, nb
    r, n = it >> (nb.bit_length() - 1), it & (nb - 1)
    if d > 1:
        tok = pl.ds(n * (d * DIL_BLK) + r, DIL_BLK, stride=d)
    else:
        tok = pl.ds(pl.multiple_of(it * DIL_BLK, DIL_BLK), DIL_BLK)
    return tok, pl.ds(pl.multiple_of((it + 1) * DIL_BLK, DIL_BLK), DIL_BLK)


def _dil_token_rows(bidx, nb, d):
    r, n = divmod(bidx, nb)
    return pl.ds(n * DIL_BLK * d + r, DIL_BLK, stride=d) if d > 1 else pl.ds(bidx * DIL_BLK, DIL_BLK)


def _dil_super_rows(ss):
    base = (1 + ss * DIL_SUPER) * DIL_BLK
    return pl.ds(base, DIL_SUPER * DIL_BLK), pl.ds(base - DIL_BLK, DIL_SUPER * DIL_BLK)


def _dil_b3(x):
    return x.reshape(DIL_SUPER, DIL_BLK, x.shape[-1])


def _dil_scores(q3, kc3, kp3, slope, d, ss, nb):
    ok_p, dist_p, ok_c, dist_c = _dil_masks()
    scale = 1.0 / math.sqrt(DIL_HEAD_DIM)
    bias_p = jnp.where(ok_p, -slope * d * dist_p, NEG_BIG)
    bias_c = jnp.where(ok_c, -slope * d * dist_c, NEG_BIG)
    s_c = lax.dot_general(q3, kc3, BNT_DIMS, preferred_element_type=F32) * scale + bias_c[None]
    s_p = lax.dot_general(q3, kp3, BNT_DIMS, preferred_element_type=F32) * scale + bias_p[None]
    bidx = ss * DIL_SUPER + lax.broadcasted_iota(jnp.int32, s_p.shape, 0)
    s_p = jnp.where((bidx & (nb - 1)) == 0, NEG_BIG, s_p)
    return s_c, s_p


def _dil_fwd(qkv, g_qn, g_kn, slopes, name):
    T = qkv.shape[0]
    GH = DIL_GROUPS * DIL_HEADS
    scale = 1.0 / math.sqrt(DIL_HEAD_DIM)

    def body(sl_ref, q_ref, k_ref, v_ref, gq_ref, gk_ref, o_ref, lse_ref, qn_pm, kn_pm, v_pm):
        gh = pl.program_id(0)
        slope = sl_ref[gh]
        gq, gk = gq_ref[...], gk_ref[...]
        pad = pl.ds(0, DIL_BLK)
        kn_pm[pad, :] = jnp.zeros((DIL_BLK, DIL_HEAD_DIM), BF16)
        v_pm[pad, :] = jnp.zeros((DIL_BLK, DIL_HEAD_DIM), BF16)
        for g, (_, d) in enumerate(DIL_PAIRS):
            @pl.when((gh >= g * DIL_HEADS) & (gh < (g + 1) * DIL_HEADS))
            def _(d=d):
                nb = T // (d * DIL_BLK)

                def fill(it, _):
                    tok, dst = _dil_chunk(it, nb, d)
                    qn_pm[dst, :] = (_dil_norm(q_ref[tok, :], gq)[0] * gq).astype(BF16)
                    kn_pm[dst, :] = (_dil_norm(k_ref[tok, :], gk)[0] * gk).astype(BF16)
                    v_pm[dst, :] = v_ref[tok, :].astype(BF16)
                    return 0
                lax.fori_loop(0, T // DIL_BLK, fill, 0, unroll=4)

                for ss in range(T // DIL_BLK // DIL_SUPER):
                    cur, prv = _dil_super_rows(ss)
                    q3, kc3, kp3 = _dil_b3(qn_pm[cur, :]), _dil_b3(kn_pm[cur, :]), _dil_b3(kn_pm[prv, :])
                    s_c, s_p = _dil_scores(q3, kc3, kp3, slope, d, ss, nb)
                    m = jnp.max(jnp.maximum(s_c, s_p), axis=-1, keepdims=True)
                    p_c = jnp.exp(s_c - m)
                    p_p = jnp.exp(s_p - m)
                    l = jnp.sum(p_c, axis=-1, keepdims=True) + jnp.sum(p_p, axis=-1, keepdims=True)
                    acc = lax.dot_general(p_c.astype(BF16), _dil_b3(v_pm[cur, :]), BNN_DIMS, preferred_element_type=F32)
                    acc += lax.dot_general(p_p.astype(BF16), _dil_b3(v_pm[prv, :]), BNN_DIMS, preferred_element_type=F32)
                    o3 = acc / l
                    lse3 = jnp.broadcast_to(m + jnp.log(l), o3.shape)
                    for b in range(DIL_SUPER):
                        tok = _dil_token_rows(ss * DIL_SUPER + b, nb, d)
                        o_ref[tok, :] = o3[b]
                        lse_ref[tok, :] = lse3[b]

    col = lambda off: pl.BlockSpec((T, DIL_HEAD_DIM), lambda gh, sl: (0, gh + off))
    gvec = pl.BlockSpec((1, DIL_HEAD_DIM), lambda gh, sl: (0, 0))
    return pl.pallas_call(
        body, name=name,
        grid_spec=pltpu.PrefetchScalarGridSpec(
            num_scalar_prefetch=1, grid=(GH,),
            in_specs=[col(0), col(GH), col(2 * GH), gvec, gvec], out_specs=[col(0), col(0)],
            scratch_shapes=[pltpu.VMEM((DIL_BLK + T, DIL_HEAD_DIM), BF16)] * 3),
        out_shape=[jax.ShapeDtypeStruct((T, GH * DIL_HEAD_DIM), F32)] * 2,
        compiler_params=_params(("parallel",)),
    )(slopes, qkv, qkv, qkv, g_qn, g_kn)


def _dil_merge(o_g, lse_g, name):
    T = o_g.shape[0]
    W = DIL_HEADS * DIL_HEAD_DIM
    tr = _pick(T, (256, 128))

    def body(o0, o1, o2, l0, l1, l2, o_ref, lse_ref):
        a, b, c = l0[...], l1[...], l2[...]
        m = jnp.maximum(jnp.maximum(a, b), c)
        ea, eb, ec = jnp.exp(a - m), jnp.exp(b - m), jnp.exp(c - m)
        tot = ea + eb + ec
        o_ref[...] = ((o0[...] * ea + o1[...] * eb + o2[...] * ec) / tot).astype(BF16)
        lse_ref[...] = m + jnp.log(tot)

    grp = lambda g: pl.BlockSpec((tr, W), lambda i: (i, g))
    out = pl.BlockSpec((tr, W), lambda i: (i, 0))
    return pl.pallas_call(
        body, name=name, grid=(T // tr,),
        in_specs=[grp(0), grp(1), grp(2), grp(0), grp(1), grp(2)], out_specs=[out, out],
        out_shape=[jax.ShapeDtypeStruct((T, W), BF16), jax.ShapeDtypeStruct((T, W), F32)],
        compiler_params=_params(("parallel",)),
    )(o_g, o_g, o_g, lse_g, lse_g, lse_g)


def _dil_bwd(qkv, g_qn, g_kn, slopes, do, delta, lse, name):
    T = qkv.shape[0]
    GH = DIL_GROUPS * DIL_HEADS
    scale = 1.0 / math.sqrt(DIL_HEAD_DIM)
    nchunk = T // DIL_BLK

    def body(sl_ref, q_ref, k_ref, v_ref, gq_ref, gk_ref, do_ref, dl_ref, lse_ref,
             dq_ref, dk_ref, dv_ref, dgq_ref, dgk_ref,
             qn_pm, kn_pm, v_pm, do_pm, lse_pm, dl_pm, dq_pm, dk_pm, dv_pm, tok_sc):
        gh = pl.program_id(0)
        slope = sl_ref[gh]
        gq, gk = gq_ref[...], gk_ref[...]

        @pl.when(gh == 0)
        def _():
            dgq_ref[...] = jnp.zeros_like(dgq_ref)
            dgk_ref[...] = jnp.zeros_like(dgk_ref)

        pad = pl.ds(0, DIL_BLK)
        kn_pm[pad, :] = jnp.zeros((DIL_BLK, DIL_HEAD_DIM), BF16)
        v_pm[pad, :] = jnp.zeros((DIL_BLK, DIL_HEAD_DIM), BF16)
        dk_pm[...] = jnp.zeros_like(dk_pm)
        dv_pm[...] = jnp.zeros_like(dv_pm)
        for g, (_, d) in enumerate(DIL_PAIRS):
            @pl.when((gh >= g * DIL_HEADS) & (gh < (g + 1) * DIL_HEADS))
            def _(d=d):
                nb = T // (d * DIL_BLK)

                def fill(it, _):
                    tok, dst = _dil_chunk(it, nb, d)
                    qn_pm[dst, :] = (_dil_norm(q_ref[tok, :], gq)[0] * gq).astype(BF16)
                    kn_pm[dst, :] = (_dil_norm(k_ref[tok, :], gk)[0] * gk).astype(BF16)
                    v_pm[dst, :] = v_ref[tok, :].astype(BF16)
                    do_pm[dst, :] = do_ref[tok, :].astype(BF16)
                    lse_pm[dst, :] = lse_ref[tok, :]
                    dl_pm[dst, :] = dl_ref[tok, :]
                    return 0
                lax.fori_loop(0, nchunk, fill, 0, unroll=4)

                for ss in range(nchunk // DIL_SUPER):
                    cur, prv = _dil_super_rows(ss)
                    q3, kc3, kp3 = _dil_b3(qn_pm[cur, :]), _dil_b3(kn_pm[cur, :]), _dil_b3(kn_pm[prv, :])
                    vc3, vp3, do3 = _dil_b3(v_pm[cur, :]), _dil_b3(v_pm[prv, :]), _dil_b3(do_pm[cur, :])
                    ls = _dil_b3(lse_pm[cur, :])[:, :, :1]
                    delta = _dil_b3(dl_pm[cur, :])[:, :, :1]
                    s_c, s_p = _dil_scores(q3, kc3, kp3, slope, d, ss, nb)
                    p_c = jnp.exp(s_c - ls)
                    p_p = jnp.exp(s_p - ls)
                    dp_c = lax.dot_general(do3, vc3, BNT_DIMS, preferred_element_type=F32)
                    dp_p = lax.dot_general(do3, vp3, BNT_DIMS, preferred_element_type=F32)
                    ds_c = (p_c * (dp_c - delta)).astype(BF16)
                    ds_p = (p_p * (dp_p - delta)).astype(BF16)
                    dq3 = (lax.dot_general(ds_c, kc3, BNN_DIMS, preferred_element_type=F32)
                           + lax.dot_general(ds_p, kp3, BNN_DIMS, preferred_element_type=F32)) * scale
                    flat = lambda x: x.reshape(DIL_SUPER * DIL_BLK, DIL_HEAD_DIM)
                    dq_pm[cur, :] = flat(dq3)
                    dk_pm[cur, :] += flat(lax.dot_general(ds_c, q3, BTN_DIMS, preferred_element_type=F32)) * scale
                    dv_pm[cur, :] += flat(lax.dot_general(p_c.astype(BF16), do3, BTN_DIMS, preferred_element_type=F32))
                    dk_pm[prv, :] += flat(lax.dot_general(ds_p, q3, BTN_DIMS, preferred_element_type=F32)) * scale
                    dv_pm[prv, :] += flat(lax.dot_general(p_p.astype(BF16), do3, BTN_DIMS, preferred_element_type=F32))

                def to_tokens(src_pm):
                    def move(it, _):
                        tok, src = _dil_chunk(it, nb, d)
                        tok_sc[tok, :] = src_pm[src, :]
                        return 0
                    lax.fori_loop(0, nchunk, move, 0, unroll=4)

                def norm_bwd(x_ref, gvec, out_ref):
                    big = 4 * DIL_BLK

                    def fin(ci, dg):
                        rows = pl.ds(pl.multiple_of(ci * big, big), big)
                        xhat, r = _dil_norm(x_ref[rows, :], gvec)
                        dn = tok_sc[rows, :]
                        dxh = dn * gvec
                        c = jnp.mean(dxh * xhat, axis=-1, keepdims=True)
                        out_ref[rows, :] = (r * (dxh - xhat * c)).astype(BF16)
                        return dg + jnp.sum(dn * xhat, axis=0, keepdims=True)
                    return lax.fori_loop(0, T // big, fin, jnp.zeros((1, DIL_HEAD_DIM), F32))

                to_tokens(dq_pm)
                dgq_ref[...] += norm_bwd(q_ref, gq, dq_ref)
                to_tokens(dk_pm)
                dgk_ref[...] += norm_bwd(k_ref, gk, dk_ref)
                to_tokens(dv_pm)
                dv_ref[...] = tok_sc[...].astype(BF16)

    col = lambda off: pl.BlockSpec((T, DIL_HEAD_DIM), lambda gh, sl: (0, gh + off))
    hcol = pl.BlockSpec((T, DIL_HEAD_DIM), lambda gh, sl: (0, gh % DIL_HEADS))
    gvec = pl.BlockSpec((1, DIL_HEAD_DIM), lambda gh, sl: (0, 0))
    wide = jax.ShapeDtypeStruct((T, GH * DIL_HEAD_DIM), BF16)
    vec = jax.ShapeDtypeStruct((1, DIL_HEAD_DIM), F32)
    pm = lambda dt: pltpu.VMEM((DIL_BLK + T, DIL_HEAD_DIM), dt)
    return pl.pallas_call(
        body, name=name,
        grid_spec=pltpu.PrefetchScalarGridSpec(
            num_scalar_prefetch=1, grid=(GH,),
            in_specs=[col(0), col(GH), col(2 * GH), gvec, gvec, hcol, hcol, hcol],
            out_specs=[col(0), col(0), col(0), gvec, gvec],
            scratch_shapes=[pm(BF16)] * 4 + [pm(F32)] * 5 + [pltpu.VMEM((T, DIL_HEAD_DIM), F32)]),
        out_shape=[wide, wide, wide, vec, vec],
        compiler_params=_params(("arbitrary",)),
    )(slopes, qkv, qkv, qkv, g_qn, g_kn, do, delta, lse)


def _my_pos():
    return lax.axis_index("x"), lax.axis_index("y"), lax.axis_index("c")


def _peer(pos, j):
    x, y, c = pos
    px = 1 - x if j & 4 else x
    py = 1 - y if j & 2 else y
    pc = 1 - c if j & 1 else c
    return (px, py, pc), 4 * px + 2 * py + pc


def _slot(idx, paired):
    if not paired:
        return idx
    return jnp.where(idx < N_DEV // 2, 2 * idx, 2 * idx - (N_DEV - 1))


def _shard_slice(ref, axis, idx, size, paired=False):
    sl = [slice(None)] * len(ref.shape)
    sl[axis] = pl.ds(pl.multiple_of(_slot(idx, paired) * size, 8), size)
    return ref.at[tuple(sl)]


HBM_SPEC = pl.BlockSpec(memory_space=pltpu.HBM)
SEM_SPEC = pl.BlockSpec(memory_space=pltpu.SEMAPHORE)
DATAFLOW = pltpu.SideEffectType.DATAFLOW_SIDE_EFFECTING
N_PEER = N_DEV - 1


def _scatter_copy(axis, grad, slots, frm, to, dev, send_sem, recv_sem):
    ax, paired = axis
    src = _shard_slice(grad, ax, to, grad.shape[ax] // N_DEV, paired)
    return pltpu.make_async_remote_copy(src_ref=src, dst_ref=slots.at[frm], send_sem=send_sem, recv_sem=recv_sem,
                                        device_id=dev, device_id_type=MESH)


def _scatter_start(grads, axes, name):
    n = len(grads)

    def body(*refs):
        outs = refs[2 * n:]
        send, recv, token = outs[:n], outs[n:2 * n], outs[4 * n]
        pos = _my_pos()
        me = 4 * pos[0] + 2 * pos[1] + pos[2]
        for a in range(n):
            for j in range(1, N_DEV):
                dev, pid = _peer(pos, j)
                _scatter_copy(axes[a], refs[2 * a], refs[2 * a + 1], me, pid, dev, send[a].at[j - 1],
                              recv[a].at[j - 1]).start()
        token[...] = jnp.zeros_like(token)

    ops = []
    for g, (ax, _) in zip(grads, axes):
        shp = list(g.shape)
        shp[ax] //= N_DEV
        ops += [g, lax.empty((N_DEV,) + tuple(shp), g.dtype)]
    sems = [pltpu.SemaphoreType.DMA((N_PEER,))] * (2 * n)
    res = pl.pallas_call(
        body, name=name,
        out_shape=sems + [pltpu.HBM(o.shape, o.dtype) for o in ops] + [jax.ShapeDtypeStruct((8, LANES), F32)],
        in_specs=[HBM_SPEC] * len(ops),
        out_specs=[SEM_SPEC] * (2 * n) + [HBM_SPEC] * len(ops) + [pl.BlockSpec(memory_space=pltpu.VMEM)],
        input_output_aliases={i: 2 * n + i for i in range(len(ops))},
        compiler_params=pltpu.CompilerParams(has_side_effects=DATAFLOW),
    )(*[pltpu.with_memory_space_constraint(o, pltpu.HBM) for o in ops])
    items = [(res[a], res[n + a], res[2 * n + 2 * a], res[2 * n + 2 * a + 1]) for a in range(n)]
    return items, res[4 * n]


def _scatter_wait(items, axes, after, name):
    n = len(items)

    def body(*refs):
        send, recv = refs[2 * n:3 * n], refs[3 * n:4 * n]
        pos = _my_pos()
        me = 4 * pos[0] + 2 * pos[1] + pos[2]
        for a in range(n):
            for j in range(1, N_DEV):
                dev, pid = _peer(pos, j)
                cp = _scatter_copy(axes[a], refs[2 * a], refs[2 * a + 1], pid, me, dev, send[a].at[j - 1],
                                   recv[a].at[j - 1])
                cp.wait_send()
                cp.wait_recv()

    ops = [b for it in items for b in it[2:]]
    res = pl.pallas_call(
        body, name=name,
        out_shape=[pltpu.HBM(o.shape, o.dtype) for o in ops],
        in_specs=[HBM_SPEC] * len(ops) + [SEM_SPEC] * (2 * n) + [ANY_SPEC],
        out_specs=[HBM_SPEC] * len(ops),
        input_output_aliases={i: i for i in range(len(ops))},
        compiler_params=pltpu.CompilerParams(has_side_effects=DATAFLOW),
    )(*ops, *[it[0] for it in items], *[it[1] for it in items], after)
    return [(res[2 * a], res[2 * a + 1]) for a in range(n)]


SIBLING = 1
ICI_PEERS = (2, 4, 6)


def _gather_copy(buf, axis, shard, dev, send_sem, recv_sem):
    ax, paired = axis
    piece = _shard_slice(buf, ax, shard, buf.shape[ax] // N_DEV, paired)
    return pltpu.make_async_remote_copy(src_ref=piece, dst_ref=piece, send_sem=send_sem, recv_sem=recv_sem,
                                        device_id=dev, device_id_type=MESH)


def _gather_start(bufs, axes, name):
    n = len(bufs)

    def body(*refs):
        ins, outs = refs[:n], refs[n:]
        send, r_sib, r_ici, token = outs[:n], outs[n:2 * n], outs[2 * n:3 * n], outs[4 * n]
        pos = _my_pos()
        me = 4 * pos[0] + 2 * pos[1] + pos[2]
        for a in range(n):
            dev, _ = _peer(pos, SIBLING)
            _gather_copy(ins[a], axes[a], me, dev, send[a].at[0], r_sib[a].at[0]).start()
            for k, j in enumerate(ICI_PEERS):
                dev, _ = _peer(pos, j)
                _gather_copy(ins[a], axes[a], me, dev, send[a].at[1 + k], r_ici[a].at[k]).start()
        token[...] = jnp.zeros_like(token)

    sems = ([pltpu.SemaphoreType.DMA((1 + len(ICI_PEERS),))] * n + [pltpu.SemaphoreType.DMA((1,))] * n
            + [pltpu.SemaphoreType.DMA((len(ICI_PEERS),))] * n)
    res = pl.pallas_call(
        body, name=name,
        out_shape=sems + [pltpu.HBM(b.shape, b.dtype) for b in bufs] + [jax.ShapeDtypeStruct((8, LANES), F32)],
        in_specs=[HBM_SPEC] * n,
        out_specs=[SEM_SPEC] * (3 * n) + [HBM_SPEC] * n + [pl.BlockSpec(memory_space=pltpu.VMEM)],
        input_output_aliases={i: 3 * n + i for i in range(n)},
        compiler_params=pltpu.CompilerParams(has_side_effects=DATAFLOW),
    )(*[pltpu.with_memory_space_constraint(b, pltpu.HBM) for b in bufs])
    items = [dict(send=res[a], r_sib=res[n + a], r_ici=res[2 * n + a], buf=res[3 * n + a]) for a in range(n)]
    return items, res[4 * n]


def _gather_relay(items, axes, after, name):
    n = len(items)

    def body(*refs):
        ins, r_ici = refs[:n], refs[n:2 * n]
        outs = refs[2 * n + 1:]
        s_rel, r_rel, token = outs[:n], outs[n:2 * n], outs[3 * n]
        pos = _my_pos()
        sib, _ = _peer(pos, SIBLING)
        for a in range(n):
            for k, j in enumerate(ICI_PEERS):
                dev, pid = _peer(pos, j)
                _gather_copy(ins[a], axes[a], pid, dev, s_rel[a].at[k], r_ici[a].at[k]).wait_recv()
                _gather_copy(ins[a], axes[a], pid, sib, s_rel[a].at[k], r_rel[a].at[k]).start()
        token[...] = jnp.zeros_like(token)

    bufs = [it["buf"] for it in items]
    sems = [pltpu.SemaphoreType.DMA((len(ICI_PEERS),))] * (2 * n)
    res = pl.pallas_call(
        body, name=name,
        out_shape=sems + [pltpu.HBM(b.shape, b.dtype) for b in bufs] + [jax.ShapeDtypeStruct((8, LANES), F32)],
        in_specs=[HBM_SPEC] * n + [SEM_SPEC] * n + [ANY_SPEC],
        out_specs=[SEM_SPEC] * (2 * n) + [HBM_SPEC] * n + [pl.BlockSpec(memory_space=pltpu.VMEM)],
        input_output_aliases={i: 2 * n + i for i in range(n)},
        compiler_params=pltpu.CompilerParams(has_side_effects=DATAFLOW),
    )(*bufs, *[it["r_ici"] for it in items], after)
    out = [dict(send=it["send"], r_sib=it["r_sib"], s_rel=res[a], r_rel=res[n + a], buf=res[2 * n + a])
           for a, it in enumerate(items)]
    return out, res[3 * n]


def _gather_wait(items, axes, after, name):
    n = len(items)

    def body(*refs):
        ins = refs[:n]
        send, r_sib, s_rel, r_rel = (refs[(1 + q) * n:(2 + q) * n] for q in range(4))
        pos = _my_pos()
        me = 4 * pos[0] + 2 * pos[1] + pos[2]
        sib, sib_id = _peer(pos, SIBLING)
        for a in range(n):
            for k in range(1 + len(ICI_PEERS)):
                _gather_copy(ins[a], axes[a], me, sib, send[a].at[k], r_sib[a].at[0]).wait_send()
            _gather_copy(ins[a], axes[a], sib_id, sib, send[a].at[0], r_sib[a].at[0]).wait_recv()
            for k, j in enumerate(ICI_PEERS):
                _, pid = _peer(pos, j)
                _, far = _peer(pos, j ^ SIBLING)
                _gather_copy(ins[a], axes[a], pid, sib, s_rel[a].at[k], r_rel[a].at[k]).wait_send()
                _gather_copy(ins[a], axes[a], far, sib, s_rel[a].at[k], r_rel[a].at[k]).wait_recv()

    bufs = [it["buf"] for it in items]
    res = pl.pallas_call(
        body, name=name,
        out_shape=[pltpu.HBM(b.shape, b.dtype) for b in bufs],
        in_specs=[HBM_SPEC] * n + [SEM_SPEC] * (4 * n) + [ANY_SPEC],
        out_specs=[HBM_SPEC] * n,
        input_output_aliases={i: i for i in range(n)},
        compiler_params=pltpu.CompilerParams(has_side_effects=DATAFLOW),
    )(*bufs, *[it["send"] for it in items], *[it["r_sib"] for it in items], *[it["s_rel"] for it in items],
      *[it["r_rel"] for it in items], after)
    return list(res)


def _gain_allreduce(v, name):
    n = v.shape[1]

    def body(v_ref, o_ref, slots, send_sems, recv_sems):
        pos = _my_pos()
        me = 4 * pos[0] + 2 * pos[1] + pos[2]
        slots[me] = v_ref[...]
        copies = []
        for j in range(1, N_DEV):
            dev, _ = _peer(pos, j)
            cp = pltpu.make_async_remote_copy(
                src_ref=slots.at[me], dst_ref=slots.at[me], send_sem=send_sems.at[j], recv_sem=recv_sems.at[j],
                device_id=dev, device_id_type=MESH)
            cp.start()
            copies.append(cp)
        for j in range(1, N_DEV):
            dev, pid = _peer(pos, j)
            pltpu.make_async_remote_copy(
                src_ref=slots.at[me], dst_ref=slots.at[pid], send_sem=send_sems.at[j], recv_sem=recv_sems.at[j],
                device_id=dev, device_id_type=MESH).wait_recv()
        for cp in copies:
            cp.wait_send()
        acc = slots[0]
        for s in range(1, N_DEV):
            acc = acc + slots[s]
        o_ref[...] = acc

    return pl.pallas_call(
        body, name=name, out_shape=jax.ShapeDtypeStruct((1, n), F32),
        in_specs=[pl.BlockSpec(memory_space=pltpu.VMEM)], out_specs=pl.BlockSpec(memory_space=pltpu.VMEM),
        scratch_shapes=[pltpu.VMEM((N_DEV, 1, n), F32), pltpu.SemaphoreType.DMA((N_DEV,)),
                        pltpu.SemaphoreType.DMA((N_DEV,))],
        compiler_params=pltpu.CompilerParams(has_side_effects=True),
    )(v)


def _adamw(parts, own, me, w, m, v, layer, prev, name, own_axis=None):
    L, R, C = w.shape
    P = parts.shape[0]
    tr = _pick(R, (128, 64, 32, 16, 8, 1))
    c1 = 1.0 - ADAM_B1 ** ADAM_STEP
    c2 = 1.0 - ADAM_B2 ** ADAM_STEP
    n_in = 4 if own is None else 5

    def body(me_ref, *refs):
        p_ref = refs[0]
        w_ref, m_ref, v_ref = refs[n_in - 3:n_in]
        g_out, d_out, m_out, v_out, tok = refs[-5:]
        g = None
        for s in range(P):
            part = p_ref[s]
            if own is not None:
                part = jnp.where(me_ref[0] == s, refs[1][...], part)
            g = part.astype(F32) if g is None else g + part.astype(F32)
        mn = ADAM_B1 * m_ref[...] + (1.0 - ADAM_B1) * g
        vn = ADAM_B2 * v_ref[...] + (1.0 - ADAM_B2) * (g * g)
        g_out[...] = g
        m_out[...] = mn
        v_out[...] = vn
        d_out[...] = -ADAM_LR * ((mn / c1) / (jnp.sqrt(vn / c2) + ADAM_EPS) + ADAM_WD * w_ref[...])
        tok[...] = jnp.zeros_like(tok)

    row = pl.BlockSpec((None, tr, C), lambda i, me_ref: (layer, i, 0))
    in_specs = [pl.BlockSpec((P, tr, C), lambda i, me_ref: (0, i, 0))]
    args = [parts]
    if own is not None:
        if own_axis is None:
            own_idx = lambda i, me_ref: (i, 0)
        elif own_axis[0] == 0:
            own_idx = lambda i, me_ref: (_slot(me_ref[0], own_axis[1]) * (R // tr) + i, 0)
        else:
            own_idx = lambda i, me_ref: (i, _slot(me_ref[0], own_axis[1]))
        in_specs.append(pl.BlockSpec((tr, C), own_idx))
        args.append(own)
    in_specs += [row, row, row]
    args += [w, m, v]
    aliases = {}
    if prev is not None:
        in_specs += [ANY_SPEC] * 4
        aliases = {1 + len(args) + k: k for k in range(4)}
        args += list(prev)
    shp = jax.ShapeDtypeStruct((L, R, C), F32)
    res = pl.pallas_call(
        body, name=name,
        grid_spec=pltpu.PrefetchScalarGridSpec(
            num_scalar_prefetch=1, grid=(R // tr,), in_specs=in_specs,
            out_specs=[row] * 4 + [pl.BlockSpec((8, LANES), lambda i, me_ref: (0, 0))]),
        out_shape=[shp] * 4 + [jax.ShapeDtypeStruct((8, LANES), F32)],
        input_output_aliases=aliases, compiler_params=_params(("arbitrary",)),
    )(me, *args)
    return res[:4], res[4]


def _pad_heads(w):
    lead = w.shape[:-1]
    n = w.shape[-1] // QK_DIM
    w = w.reshape(lead + (n, QK_DIM))
    w = jnp.pad(w, [(0, 0)] * len(lead) + [(0, 0), (0, HEAD_PAD - QK_DIM)])
    return w.reshape(lead + (n * HEAD_PAD,))


def _unpad_heads(w):
    lead = w.shape[:-1]
    n = w.shape[-1] // HEAD_PAD
    return w.reshape(lead + (n, HEAD_PAD))[..., :QK_DIM].reshape(lead + (n * QK_DIM,))


def kernel(x, ffn1_norm, ffn1_w_in, ffn1_w_out, mix_norm, ffn2_norm, ffn2_w_in, ffn2_w_out, mla_w_down, mla_g_cq, mla_g_ckv, mla_w_uq, mla_w_ukv, mla_g_qn, mla_g_kn, mla_w_o, dil_w_qkv, dil_g_qn, dil_g_kn, dil_w_o, loss_target, m_ffn1_norm, m_ffn1_w_in, m_ffn1_w_out, m_mix_norm, m_ffn2_norm, m_ffn2_w_in, m_ffn2_w_out, m_mla_w_down, m_mla_g_cq, m_mla_g_ckv, m_mla_w_uq, m_mla_w_ukv, m_mla_g_qn, m_mla_g_kn, m_mla_w_o, m_dil_w_qkv, m_dil_g_qn, m_dil_g_kn, m_dil_w_o, v_ffn1_norm, v_ffn1_w_in, v_ffn1_w_out, v_mix_norm, v_ffn2_norm, v_ffn2_w_in, v_ffn2_w_out, v_mla_w_down, v_mla_g_cq, v_mla_g_ckv, v_mla_w_uq, v_mla_w_ukv, v_mla_g_qn, v_mla_g_kn, v_mla_w_o, v_dil_w_qkv, v_dil_g_qn, v_dil_g_kn, v_dil_w_o):
    names = ["ffn1_norm", "ffn1_w_in", "ffn1_w_out", "mix_norm", "ffn2_norm", "ffn2_w_in", "ffn2_w_out", "mla_w_down",
             "mla_g_cq", "mla_g_ckv", "mla_w_uq", "mla_w_ukv", "mla_g_qn", "mla_g_kn", "mla_w_o", "dil_w_qkv",
             "dil_g_qn", "dil_g_kn", "dil_w_o"]
    W = dict(zip(names, [ffn1_norm, ffn1_w_in, ffn1_w_out, mix_norm, ffn2_norm, ffn2_w_in, ffn2_w_out, mla_w_down,
                         mla_g_cq, mla_g_ckv, mla_w_uq, mla_w_ukv, mla_g_qn, mla_g_kn, mla_w_o, dil_w_qkv,
                         dil_g_qn, dil_g_kn, dil_w_o]))
    M1 = dict(zip(names, [m_ffn1_norm, m_ffn1_w_in, m_ffn1_w_out, m_mix_norm, m_ffn2_norm, m_ffn2_w_in, m_ffn2_w_out,
                          m_mla_w_down, m_mla_g_cq, m_mla_g_ckv, m_mla_w_uq, m_mla_w_ukv, m_mla_g_qn, m_mla_g_kn,
                          m_mla_w_o, m_dil_w_qkv, m_dil_g_qn, m_dil_g_kn, m_dil_w_o]))
    V2 = dict(zip(names, [v_ffn1_norm, v_ffn1_w_in, v_ffn1_w_out, v_mix_norm, v_ffn2_norm, v_ffn2_w_in, v_ffn2_w_out,
                          v_mla_w_down, v_mla_g_cq, v_mla_g_ckv, v_mla_w_uq, v_mla_w_ukv, v_mla_g_qn, v_mla_g_kn,
                          v_mla_w_o, v_dil_w_qkv, v_dil_g_qn, v_dil_g_kn, v_dil_w_o]))
    S, D = x.shape[1], x.shape[2]
    x0 = x.reshape(S, D)
    tgt = loss_target.reshape(S, D)

    big = ["ffn1_w_in", "ffn1_w_out", "ffn2_w_in", "ffn2_w_out", "mla_w_down", "mla_w_uq", "mla_w_ukv", "mla_w_o",
           "dil_w_qkv", "dil_w_o"]
    shard_dim = {"ffn1_w_in": 2, "ffn1_w_out": 1, "ffn2_w_in": 2, "ffn2_w_out": 1, "mla_w_down": 1, "mla_w_uq": 2,
                 "mla_w_ukv": 2, "mla_w_o": 1, "dil_w_qkv": 2, "dil_w_o": 2}
    paired = ("ffn1_w_in", "ffn2_w_in")
    shard_axis = {n: (d, n in paired) for n, d in shard_dim.items()}
    grad_axis = {n: (d - 1, n in paired) for n, d in shard_dim.items()}

    def padded(n, w):
        if n == "mla_w_down":
            return jnp.pad(w, ((0, 0), (0, 0), (0, LAT_PAD - w.shape[2])))
        if n == "mla_w_uq":
            return _pad_heads(w)
        return w

    depth = ffn1_norm.shape[0]
    blocks = []
    for l in range(depth):
        mixer = (["mla_w_down", "mla_w_uq", "mla_w_ukv", "mla_w_o"] if l % 2 == 0 else ["dil_w_qkv", "dil_w_o"])
        blocks.append((f"ffn1_{l}", [("ffn1_w_in", l), ("ffn1_w_out", l)]))
        blocks.append((f"mix_{l}", [(n, l // 2) for n in mixer]))
        blocks.append((f"ffn2_{l}", [("ffn2_w_in", l), ("ffn2_w_out", l)]))
    order = [k for _, keys in blocks for k in keys]
    me = (4 * lax.axis_index("x") + 2 * lax.axis_index("y") + lax.axis_index("c")).astype(jnp.int32).reshape(1)
    def cast(key, deps=()):
        n, l = key
        return _cast_into_gathered(padded(n, W[n]), l, shard_axis[n], me, f"cast_{n}_{l}", deps=deps)

    items0, token0 = _gather_start([cast(order[0])], [shard_axis[order[0][0]]], "gather_start_first")
    rest = order[1:]
    items1, ag_token = _gather_start([cast(k, deps=[token0]) for k in rest], [shard_axis[k[0]] for k in rest],
                                     "gather_start_rest")
    ag_items = dict(zip(order, items0 + items1))
    full = {}

    def relay(keys, after, tag):
        out, token = _gather_relay([ag_items[k] for k in keys], [shard_axis[k[0]] for k in keys], after,
                                   f"gather_relay_{tag}")
        ag_items.update(zip(keys, out))
        return [token]

    def relay_next(bi, after):
        return relay(blocks[bi + 1][1], after, blocks[bi + 1][0]) if bi + 1 < len(blocks) else []

    def fetch(keys, after, tag):
        lands = _gather_wait([ag_items[k] for k in keys], [shard_axis[k[0]] for k in keys], after,
                             f"gather_wait_{tag}")
        full.update(zip(keys, lands))

    g_qn = _pad_heads(mla_g_qn)
    g_kn = _pad_heads(mla_g_kn)
    tabs = _rope_tables(S)
    slopes = jnp.asarray(_alibi_slopes(), F32)

    grads = {}
    gain_g = {}

    out_g, out_d, out_m, out_v = {}, {}, {}, {}
    pending = []
    lag = 3

    def scatter_start(tag, keys):
        items, token = _scatter_start([grads[k] for k in keys], [grad_axis[k[0]] for k in keys],
                                      f"scatter_start_{tag}")
        pending.append((tag, keys, items))
        return token

    def scatter_finish(after):
        tag, keys, items = pending.pop(0)
        lands = _scatter_wait(items, [grad_axis[k[0]] for k in keys], after, f"scatter_wait_{tag}")
        tokens = []
        for (n, l), (own, p) in zip(keys, lands):
            own_axis = grad_axis[n]
            if n in ("mla_w_down", "mla_w_uq"):
                ax, pair = grad_axis[n]
                size = own.shape[ax] // N_DEV
                own = lax.dynamic_slice_in_dim(own, _slot(me[0], pair) * size, size, axis=ax)
                own_axis = None
                if n == "mla_w_down":
                    p, own = p[..., :W[n].shape[2]], own[..., :W[n].shape[2]]
                else:
                    p, own = _unpad_heads(p), _unpad_heads(own)
            prev = (out_g[n], out_d[n], out_m[n], out_v[n]) if n in out_g else None
            (out_g[n], out_d[n], out_m[n], out_v[n]), tok = _adamw(p, own, me, W[n], M1[n], V2[n], l, prev,
                                                                    f"adamw_{n}_{l}", own_axis=own_axis)
            tokens.append(tok)
        return tokens

    def finish_due(after):
        tokens = []
        while len(pending) > lag:
            tokens += scatter_finish(after)
        return tokens

    def mixer_out(o, w_o, xin, l, toks, name):
        tm, tn, tk, _ = _mm_tiles(S, o.shape[1], D, 2, 2, 4, True)
        if tn == D and tk == o.shape[1]:
            return _mm(o, w_o, "nn", F32, name, res=xin, layer=0, deps=toks, norm_gain=ffn2_norm[l:l + 1])
        return _mm(o, w_o, "nn", F32, name, res=xin, layer=0, deps=toks), None

    def mixer_do(dxob, w_o, o, out_dtype, name):
        _, _, tk, _ = _mm_tiles(S, D, o.shape[1], 2, 2, jnp.dtype(out_dtype).itemsize, False)
        if tk == D:
            return _mm(dxob, w_o, "nt", out_dtype, name, layer=0, dot_with=o)
        do = _mm(dxob, w_o, "nt", out_dtype, name, layer=0)
        return do, _attn_delta(do, o, name + "_delta")

    def ffn_fwd(xin, norm_row, which, l, bi, deps=(), h=None):
        tag = blocks[bi][0]
        k_in, k_out = (which + "_w_in", l), (which + "_w_out", l)
        if h is None:
            h = _rms_fwd(xin, norm_row, f"rms_fwd_{tag}", deps=deps)
        if bi == 0:
            relay([k_in], h, f"{tag}_in")
        fetch([k_in], h, f"in_{tag}")
        u, a = _ffn_in(h, full[k_in], f"ffn_in_{tag}")
        if bi == 0:
            relay([k_out], a, f"{tag}_out")
        fetch([k_out], a, f"out_{tag}")
        toks = relay_next(bi, a)
        fused_loss = bi == len(blocks) - 1 and _mm_tiles(S, a.shape[1], D, 2, 2, 4, True)[2] == a.shape[1]
        xo = _mm(a, full[k_out], "nn", F32, f"mm_out_{tag}", scale=0.5, res=xin, layer=0, deps=toks,
                 loss_target=tgt if fused_loss else None)
        return xo, (xin, h, u, a)

    def ffn_bwd(dx_pair, saved, norm_row, which, l, tag, deps=()):
        dxo, dxob = dx_pair
        k_in, k_out = (which + "_w_in", l), (which + "_w_out", l)
        xin, h, u, a = saved
        grads[k_out] = _mm(a, dxob, "tn", BF16, f"mm_dwout_{tag}", scale=0.5, deps=deps)
        t_out = scatter_start(f"{tag}_out", [k_out])
        du = _ffn_da(dxob, full[k_out], u, f"ffn_da_{tag}", deps=[t_out])
        grads[k_in] = _mm(h, du, "tn", BF16, f"mm_dwin_{tag}")
        t_in = scatter_start(f"{tag}_in", [k_in])
        dh = _mm(du, full[k_in], "nt", F32, f"mm_dh_{tag}", layer=0, deps=[t_in])
        toks = finish_due(dh)
        dx, dxb, dg = _rms_bwd(xin, norm_row, dh, dxo, f"rms_bwd_{tag}", deps=toks)
        gain_g.setdefault(which + "_norm", {})[l] = dg
        return dx, dxb

    def mla_fwd(xin, l, bi):
        j = l // 2
        xn = _rms_fwd(xin, mix_norm[l:l + 1], "rms_fwd_mla")
        fetch([(n, j) for n in ("mla_w_down", "mla_w_uq", "mla_w_ukv", "mla_w_o")], xn, "mla")
        lat = _mm(xn, full[("mla_w_down", j)], "nn", F32, "mm_lat", layer=0)
        cq, ckv = _lat_norm_fwd(lat, mla_g_cq[j:j + 1], mla_g_ckv[j:j + 1], "lat_norm_fwd")
        q_raw = _mm(cq, full[("mla_w_uq", j)], "nn", F32, "mm_uq", layer=0)
        kv = _mm(ckv, full[("mla_w_ukv", j)], "nn", F32, "mm_ukv", layer=0)
        qf, kf, vb = _mla_prep_fwd(q_raw, kv, lat, g_qn[j:j + 1], g_kn[j:j + 1], tabs, "mla_prep_fwd")
        o, lse = _flash_fwd(qf, kf, vb, "flash_fwd")
        toks = relay_next(bi, o)
        xo, h_next = mixer_out(o, full[("mla_w_o", j)], xin, l, toks, "mm_mla_o")
        return (xo, h_next), (xin, xn, lat, cq, ckv, q_raw, kv, qf, kf, vb, o, lse)

    def mla_bwd(dx_pair, saved, l):
        dxo, dxob = dx_pair
        j = l // 2
        xin, xn, lat, cq, ckv, q_raw, kv, qf, kf, vb, o, lse = saved
        do, delta = mixer_do(dxob, full[("mla_w_o", j)], o, BF16, "mm_mla_do")
        grads[("mla_w_o", j)] = _mm(o, dxob, "tn", BF16, "mm_mla_dwo")
        dqf, dkf, dv = _flash_bwd(qf, kf, vb, do, lse, delta, "flash_bwd")
        dq_raw, dkv, dkpe, dgq, dgk = _mla_prep_bwd(q_raw, kv, lat, g_qn[j:j + 1], g_kn[j:j + 1], tabs, dqf, dkf, dv,
                                                    "mla_prep_bwd")
        gain_g.setdefault("mla_g_qn", {})[j] = dgq
        gain_g.setdefault("mla_g_kn", {})[j] = dgk
        dcq = _mm(dq_raw, full[("mla_w_uq", j)], "nt", F32, "mm_dcq", layer=0)
        grads[("mla_w_uq", j)] = _mm(cq, dq_raw, "tn", BF16, "mm_dwuq")
        dckv = _mm(dkv, full[("mla_w_ukv", j)], "nt", F32, "mm_dckv", layer=0)
        grads[("mla_w_ukv", j)] = _mm(ckv, dkv, "tn", BF16, "mm_dwukv")
        dlat, dgcq, dgckv = _lat_norm_bwd(lat, mla_g_cq[j:j + 1], mla_g_ckv[j:j + 1], dcq, dckv, dkpe, "lat_norm_bwd")
        gain_g.setdefault("mla_g_cq", {})[j] = dgcq
        gain_g.setdefault("mla_g_ckv", {})[j] = dgckv
        dxn = _mm(dlat, full[("mla_w_down", j)], "nt", F32, "mm_dxn_mla", layer=0)
        grads[("mla_w_down", j)] = _mm(xn, dlat, "tn", BF16, "mm_dwdown")
        tok = scatter_start(f"mix_{l}", [(n, j) for n in ("mla_w_down", "mla_w_uq", "mla_w_ukv", "mla_w_o")])
        toks = finish_due(dxn)
        dx, dxb, dg = _rms_bwd(xin, mix_norm[l:l + 1], dxn, dxo, "rms_bwd_mla", deps=[tok] + toks)
        gain_g.setdefault("mix_norm", {})[l] = dg
        return dx, dxb

    def dil_fwd(xin, l, bi):
        j = l // 2
        xn = _rms_fwd(xin, mix_norm[l:l + 1], "rms_fwd_dil")
        fetch([("dil_w_qkv", j), ("dil_w_o", j)], xn, "dil")
        qkv = _mm(xn, full[("dil_w_qkv", j)], "nn", F32, "mm_qkv", layer=0)
        o_g, lse_g = _dil_fwd(qkv, dil_g_qn[j:j + 1], dil_g_kn[j:j + 1], slopes, "dil_fwd")
        o, lse = _dil_merge(o_g, lse_g, "dil_merge")
        toks = relay_next(bi, o)
        xo, h_next = mixer_out(o, full[("dil_w_o", j)], xin, l, toks, "mm_dil_o")
        return (xo, h_next), (xin, xn, qkv, o, lse)

    def dil_bwd(dx_pair, saved, l):
        dxo, dxob = dx_pair
        j = l // 2
        xin, xn, qkv, o, lse = saved
        do, delta = mixer_do(dxob, full[("dil_w_o", j)], o, F32, "mm_dil_do")
        grads[("dil_w_o", j)] = _mm(o, dxob, "tn", BF16, "mm_dil_dwo")
        dq, dk, dv, dgq, dgk = _dil_bwd(qkv, dil_g_qn[j:j + 1], dil_g_kn[j:j + 1], slopes, do, delta, lse, "dil_bwd")
        gain_g.setdefault("dil_g_qn", {})[j] = dgq
        gain_g.setdefault("dil_g_kn", {})[j] = dgk
        dqkv = jnp.concatenate([dq, dk, dv], axis=1)
        dxn = _mm(dqkv, full[("dil_w_qkv", j)], "nt", F32, "mm_dxn_dil", layer=0)
        grads[("dil_w_qkv", j)] = _mm(xn, dqkv, "tn", BF16, "mm_dwqkv")
        tok = scatter_start(f"mix_{l}", [("dil_w_qkv", j), ("dil_w_o", j)])
        toks = finish_due(dxn)
        dx, dxb, dg = _rms_bwd(xin, mix_norm[l:l + 1], dxn, dxo, "rms_bwd_dil", deps=[tok] + toks)
        gain_g.setdefault("mix_norm", {})[l] = dg
        return dx, dxb

    saved = []
    xc = x0
    for l in range(depth):
        xc, s1 = ffn_fwd(xc, ffn1_norm[l:l + 1], "ffn1", l, 3 * l, deps=[ag_token] if l == 0 else ())
        (xc, h_next), s2 = (mla_fwd if l % 2 == 0 else dil_fwd)(xc, l, 3 * l + 1)
        xc, s3 = ffn_fwd(xc, ffn2_norm[l:l + 1], "ffn2", l, 3 * l + 2, h=h_next)
        saved.append((s1, s2, s3))

    dy, dyb, loss_part = xc if isinstance(xc, tuple) else _loss_head(xc, tgt, "loss_head")
    dx = (dy, dyb)
    loss = lax.psum(loss_part[0, 0], MESH_AXES)

    for bi in reversed(range(len(blocks))):
        tag, _ = blocks[bi]
        l = bi // 3
        s = saved[l][bi % 3]
        if bi % 3 == 2:
            dx = ffn_bwd(dx, s, ffn2_norm[l:l + 1], "ffn2", l, tag,
                         deps=[loss.reshape(1, 1)] if bi == len(blocks) - 1 else ())
        elif bi % 3 == 1:
            dx = (mla_bwd if l % 2 == 0 else dil_bwd)(dx, s, l)
        else:
            dx = ffn_bwd(dx, s, ffn1_norm[l:l + 1], "ffn1", l, tag)
    grad_x = dx[0].reshape(x.shape)
    after = dx[1]
    while pending:
        after = scatter_finish(after)[-1]

    small = [n for n in names if n not in big]

    def gain_local(n):
        rows = [gain_g[n][l] for l in range(W[n].shape[0])]
        g = jnp.concatenate(rows, axis=1)
        return g

    def flat_pad(n, a):
        a = a.reshape(1, -1)
        if n in ("mla_g_qn", "mla_g_kn"):
            a = _pad_heads(a)
        return a

    packed_g = jnp.concatenate([gain_local(n) for n in small], axis=1)
    sizes = [gain_local(n).shape[1] for n in small]
    tot_g = _gain_allreduce(packed_g, "gain_allreduce")
    pw = jnp.concatenate([flat_pad(n, W[n]) for n in small], axis=1)
    pm = jnp.concatenate([flat_pad(n, M1[n]) for n in small], axis=1)
    pv = jnp.concatenate([flat_pad(n, V2[n]) for n in small], axis=1)
    res, _ = _adamw(tot_g.reshape(1, 1, -1), None, me, pw.reshape(1, 1, -1), pm.reshape(1, 1, -1),
                    pv.reshape(1, 1, -1), 0, None, "adamw_gains")
    res = [r.reshape(1, -1) for r in res]
    off = 0
    for n, sz in zip(small, sizes):
        for dst, r in zip((out_g, out_d, out_m, out_v), res):
            piece = r[:, off:off + sz]
            if n in ("mla_g_qn", "mla_g_kn"):
                piece = _unpad_heads(piece)
            dst[n] = piece.reshape(W[n].shape)
        off += sz

    return (loss, grad_x, *[out_g[n] for n in names], *[out_d[n] for n in names],
            *[out_m[n] for n in names], *[out_v[n] for n in names])
```

```python
import functools
import math

import jax
import jax.numpy as jnp
import numpy as np
from jax import lax
from jax.experimental import pallas as pl
from jax.experimental.pallas import tpu as pltpu

EPS = 1e-6
MLA_HEADS = 16
Q_LORA = 512
KV_LORA = 512
NOPE_DIM = 128
ROPE_DIM = 64
V_DIM = 128
QK_DIM = NOPE_DIM + ROPE_DIM
ROPE_THETA = 10000.0
HEAD_PAD = 256
LAT_PAD = Q_LORA + KV_LORA + 128
DIL_PAIRS = ((128, 1), (512, 4), (2048, 16))
DIL_GROUPS = 3
DIL_HEADS = 8
DIL_HEAD_DIM = 128
DIL_BLK = 128
FLASH_HEADS = 4
FLASH_HEADS_FWD = 8
LOG2E = math.log2(math.e)
LN2 = math.log(2.0)
ADAM_LR = 0.001
ADAM_B1 = 0.9
ADAM_B2 = 0.999
ADAM_EPS = 1e-08
ADAM_WD = 0.01
ADAM_STEP = 10

N_DEV = 8
MESH_AXES = ("x", "y", "c")
MESH = pl.DeviceIdType.MESH
NEG_BIG = -1e30
VMEM_LIMIT_V7X = 56 * 1024 * 1024
LANES = 128

BF16 = jnp.bfloat16
F32 = jnp.float32


def _pick(n, cands):
    for c in cands:
        if n % c == 0:
            return c
    raise ValueError(f"no tile for {n}")


def _params(sem):
    return pltpu.CompilerParams(dimension_semantics=sem, vmem_limit_bytes=VMEM_LIMIT_V7X)


ANY_SPEC = pl.BlockSpec(memory_space=pl.ANY)


MM_VMEM_BUDGET = 44 * 1024 * 1024
MM_HBM_BYTES_PER_S = 1.8e12
MM_MXU_FLOPS_PER_S = 8.5e14
MM_STEP_S = 0.4e-6
MM_MAX_TILE_MACS = 3.3e9
MXU_DIM = 256


MM_TIMED_TILES = {
    (2048, 4096, 11264, 2, 2, 2, False): (1024, 1024, 4096, True),
    (4096, 11264, 2048, 2, 2, 4, False): (1024, 1024, 2816, True),
    (4096, 5632, 2048, 2, 2, 4, True): (512, 1024, 5632, False),
    (2048, 4096, 9216, 2, 2, 2, False): (1024, 1024, 4096, True),
    (4096, 9216, 2048, 2, 2, 4, False): (1024, 1024, 2304, True),
}


@functools.lru_cache(maxsize=None)
def _mm_tiles(M, K, N, a_bytes, b_bytes, out_bytes, has_res):
    if (M, K, N, a_bytes, b_bytes, out_bytes, has_res) in MM_TIMED_TILES:
        return MM_TIMED_TILES[(M, K, N, a_bytes, b_bytes, out_bytes, has_res)]
    best = None
    for tk in [K] + [c for c in (1408, 1024, 512, 384, 256, 128) if K % c == 0 and c < K]:
        nk = K // tk
        for tm in [c for c in (2048, 1024, 512, 256, 128) if M % c == 0]:
            for tn in [c for c in (2816, 2048, 1408, 1152, 1024, 512, 384, 256, 128) if N % c == 0]:
                if tm * tk * tn > MM_MAX_TILE_MACS:
                    continue
                fill = (tn / (-(-tn // MXU_DIM) * MXU_DIM)) * (tk / (-(-tk // MXU_DIM) * MXU_DIM))
                fill *= tm / (tm + MXU_DIM // 2)
                vmem = 2 * (tm * tk * a_bytes + tk * tn * b_bytes) + 2 * tm * tn * out_bytes + tm * tn * 4
                vmem += (tm * tk + tk * tn) * 2 if max(a_bytes, b_bytes) > 2 else 0
                vmem += 2 * tm * tn * 4 if has_res else 0
                if vmem > MM_VMEM_BUDGET:
                    continue
                a_all, b_all = M * K * a_bytes, K * N * b_bytes
                if nk == 1:
                    t_i = a_all + (M // tm) * b_all
                    t_j = b_all + (N // tn) * a_all
                    traffic, i_outer = min((t_i, True), (t_j, False))
                else:
                    traffic, i_outer = (N // tn) * a_all + (M // tm) * b_all, True
                traffic += M * N * (out_bytes + (4 if has_res else 0))
                mxu = 2.0 * M * K * N / (MM_MXU_FLOPS_PER_S * fill) * (1.15 if nk > 1 else 1.0)
                cost = max(traffic / MM_HBM_BYTES_PER_S, mxu) + (M // tm) * (N // tn) * nk * MM_STEP_S
                if best is None or cost < best[0]:
                    best = (cost, tm, tn, tk, i_outer)
    assert best is not None, (M, K, N)
    return best[1:]


def _mm(a, b, mode, out_dtype, name, *, scale=1.0, res=None, layer=None, deps=(), norm_gain=None, dot_with=None,
        loss_target=None):
    b2 = b.shape[-2:]
    if mode == "nn":
        (M, K), (Kb, N) = a.shape, b2
    elif mode == "nt":
        (M, K), (N, Kb) = a.shape, b2
    else:
        (K, M), (Kb, N) = a.shape, b2
    assert K == Kb, (a.shape, b.shape, mode)
    tm, tn, tk, i_outer = _mm_tiles(M, K, N, a.dtype.itemsize, b.dtype.itemsize, jnp.dtype(out_dtype).itemsize,
                                    res is not None)
    nk = K // tk
    dims = {"nn": (((1,), (0,)), ((), ())), "nt": (((1,), (1,)), ((), ())), "tn": (((0,), (0,)), ((), ()))}[mode]
    normed = norm_gain is not None
    dotted = dot_with is not None
    assert not normed or (nk == 1 and tn == N), (name, tn, N, nk)
    assert not dotted or (nk == 1 and not normed and tn % LANES == 0), (name, tn, nk)
    lossy = loss_target is not None
    assert not lossy or (nk == 1 and res is not None and not normed and not dotted), name
    n_in = 2 + (res is not None) + normed + dotted + lossy + len(deps)

    def finish(v, r_ref, o_ref):
        if scale != 1.0:
            v = v * scale
        if r_ref is not None:
            v = r_ref[...] + v
        o_ref[...] = v.astype(o_ref.dtype)
        return v

    def body(*refs):
        a_ref, b_ref = refs[:2]
        r_ref = refs[2] if res is not None else None
        prod = lambda: lax.dot_general(a_ref[...].astype(BF16), b_ref[...].astype(BF16), dims,
                                       preferred_element_type=F32)
        if lossy:
            t_ref, (dy_ref, dyb_ref, l_ref) = refs[3], refs[n_in:n_in + 3]
            e = r_ref[...] + prod() * scale - t_ref[...]
            dy = e * (1.0 / N)
            dy_ref[...] = dy
            dyb_ref[...] = dy.astype(BF16)

            @pl.when((pl.program_id(0) == 0) & (pl.program_id(1) == 0))
            def _():
                l_ref[...] = jnp.zeros_like(l_ref)

            l_ref[...] += (0.5 / N) * jnp.sum(jnp.sum(e * e, axis=-1, keepdims=True), axis=0, keepdims=True)
            return
        if nk == 1:
            v = finish(prod(), r_ref, refs[n_in])
            if normed:
                g_ref = refs[2 + (res is not None)]
                r = lax.rsqrt(jnp.mean(v * v, axis=-1, keepdims=True) + EPS)
                refs[n_in + 1][...] = ((v * r) * g_ref[...]).astype(BF16)
            if dotted:
                w_ref, d_ref = refs[2 + (res is not None)], refs[n_in + 1]
                for h in range(tn // LANES):
                    sl = slice(h * LANES, (h + 1) * LANES)
                    d = jnp.sum(v[:, sl] * w_ref[:, sl].astype(F32), axis=-1, keepdims=True)
                    d_ref[:, sl] = jnp.broadcast_to(d, (tm, LANES))
            return
        o_ref, acc = refs[-2:]
        k = pl.program_id(2)

        @pl.when(k == 0)
        def _():
            acc[...] = prod()

        @pl.when(k > 0)
        def _():
            acc[...] += prod()

        @pl.when(k == nk - 1)
        def _():
            finish(acc[...], r_ref, o_ref)

    ij = (lambda p, q: (p, q)) if i_outer else (lambda p, q: (q, p))

    def spec(shape, f, lead=None):
        full = lambda p, q, k: f(*ij(p, q), k)
        if lead is None:
            return pl.BlockSpec(shape, full)
        return pl.BlockSpec((None,) + shape, lambda p, q, k: (lead,) + full(p, q, k))

    a_spec = spec((tk, tm), lambda i, j, k: (k, i)) if mode == "tn" else spec((tm, tk), lambda i, j, k: (i, k))
    lead = layer if b.ndim == 3 else None
    b_spec = spec((tn, tk), lambda i, j, k: (j, k), lead) if mode == "nt" else spec((tk, tn), lambda i, j, k: (k, j), lead)
    in_specs = [a_spec, b_spec]
    args = [a, b]
    if res is not None:
        in_specs.append(spec((tm, tn), lambda i, j, k: (i, j)))
        args.append(res)
    tile = spec((tm, tn), lambda i, j, k: (i, j))
    if normed:
        in_specs.append(pl.BlockSpec((1, N), lambda p, q, k: (0, 0)))
        args.append(norm_gain)
    if dotted:
        in_specs.append(tile)
        args.append(dot_with)
    if lossy:
        in_specs.append(tile)
        args.append(loss_target)
    in_specs += [ANY_SPEC] * len(deps)
    args += list(deps)
    outer, inner = (M // tm, N // tn) if i_outer else (N // tn, M // tm)
    if lossy:
        return tuple(pl.pallas_call(
            body, name=name, grid=(outer, inner, nk), in_specs=in_specs,
            out_specs=[tile, tile, pl.BlockSpec((1, 1), lambda p, q, k: (0, 0))],
            out_shape=[jax.ShapeDtypeStruct((M, N), F32), jax.ShapeDtypeStruct((M, N), BF16),
                       jax.ShapeDtypeStruct((1, 1), F32)],
            compiler_params=_params(("arbitrary", "arbitrary", "arbitrary")),
        )(*args))
    second = [jax.ShapeDtypeStruct((M, N), BF16)] if normed else [jax.ShapeDtypeStruct((M, N), F32)] if dotted else []
    out = pl.pallas_call(
        body, name=name, grid=(outer, inner, nk),
        in_specs=in_specs, out_specs=[tile, tile] if second else tile,
        out_shape=[jax.ShapeDtypeStruct((M, N), out_dtype)] + second if second else jax.ShapeDtypeStruct((M, N), out_dtype),
        scratch_shapes=[pltpu.VMEM((tm, tn), F32)] if nk > 1 else [],
        compiler_params=_params(("parallel", "parallel", "arbitrary")),
    )(*args)
    return tuple(out) if second else out


def _cast_into_gathered(w, layer, axis, me, name, deps=()):
    _, R, C = w.shape
    tr = _pick(R, (512, 256, 128, 64, 32, 16))
    nr = R // tr
    axis, paired = axis

    def body(me_ref, w_ref, *rest):
        o_ref = rest[-1]
        o_ref[...] = w_ref[...].astype(BF16)

    if axis == 1:
        out_idx = lambda i, me_ref: (0, _slot(me_ref[0], paired) * nr + i, 0)
        shape = (1, R * N_DEV, C)
    else:
        out_idx = lambda i, me_ref: (0, i, _slot(me_ref[0], paired))
        shape = (1, R, C * N_DEV)
    return pl.pallas_call(
        body, name=name,
        grid_spec=pltpu.PrefetchScalarGridSpec(
            num_scalar_prefetch=1, grid=(nr,),
            in_specs=[pl.BlockSpec((None, tr, C), lambda i, me_ref: (layer, i, 0))] + [ANY_SPEC] * len(deps),
            out_specs=pl.BlockSpec((None, tr, C), out_idx)),
        out_shape=jax.ShapeDtypeStruct(shape, BF16), compiler_params=_params(("parallel",)),
    )(me, w, *deps)


def _rms_fwd(x, g, name, deps=()):
    T, D = x.shape
    tr = _pick(T, (512, 256, 128))

    def body(x_ref, g_ref, *rest):
        o_ref = rest[-1]
        xv = x_ref[...]
        r = lax.rsqrt(jnp.mean(xv * xv, axis=-1, keepdims=True) + EPS)
        o_ref[...] = ((xv * r) * g_ref[...]).astype(BF16)

    return pl.pallas_call(
        body, name=name, grid=(T // tr,),
        in_specs=[pl.BlockSpec((tr, D), lambda i: (i, 0)), pl.BlockSpec((1, D), lambda i: (0, 0))]
        + [ANY_SPEC] * len(deps),
        out_specs=pl.BlockSpec((tr, D), lambda i: (i, 0)),
        out_shape=jax.ShapeDtypeStruct((T, D), BF16), compiler_params=_params(("parallel",)),
    )(x, g, *deps)


def _rms_bwd(x, g, dh, dres, name, deps=()):
    T, D = x.shape
    tr = _pick(T, (512, 256, 128))

    def body(x_ref, g_ref, dh_ref, dres_ref, *rest):
        dx_ref, dxb_ref, dg_ref = rest[-3:]
        xv = x_ref[...]
        dhv = dh_ref[...]
        r = lax.rsqrt(jnp.mean(xv * xv, axis=-1, keepdims=True) + EPS)
        xhat = xv * r
        dxh = dhv * g_ref[...]
        c = jnp.mean(dxh * xhat, axis=-1, keepdims=True)
        dx = dres_ref[...] + r * (dxh - xhat * c)
        dx_ref[...] = dx
        dxb_ref[...] = dx.astype(BF16)

        @pl.when(pl.program_id(0) == 0)
        def _():
            dg_ref[...] = jnp.zeros_like(dg_ref)

        dg_ref[...] += jnp.sum(dhv * xhat, axis=0, keepdims=True)

    row = pl.BlockSpec((tr, D), lambda i: (i, 0))
    vec = pl.BlockSpec((1, D), lambda i: (0, 0))
    return pl.pallas_call(
        body, name=name, grid=(T // tr,),
        in_specs=[row, vec, row, row] + [ANY_SPEC] * len(deps), out_specs=[row, row, vec],
        out_shape=[jax.ShapeDtypeStruct((T, D), F32), jax.ShapeDtypeStruct((T, D), BF16),
                   jax.ShapeDtypeStruct((1, D), F32)],
        compiler_params=_params(("arbitrary",)),
    )(x, g, dh, dres, *deps)


N_PANEL = N_DEV // 2


def _ffn_in(h, w_in, name):
    T, D = h.shape
    F2 = w_in.shape[2]
    pw = F2 // N_PANEL
    half = pw // 2
    tm = _pick(T, (512, 256, 128))

    def body(h_ref, w_ref, u_ref, a_ref):
        r = jnp.dot(h_ref[...], w_ref[...], preferred_element_type=F32)
        u_ref[...] = r.astype(BF16)
        g, up = r[:, :half], r[:, half:]
        a_ref[...] = (g * jax.nn.sigmoid(g) * up).astype(BF16)

    return pl.pallas_call(
        body, name=name, grid=(N_PANEL, T // tm),
        in_specs=[pl.BlockSpec((tm, D), lambda p, i: (i, 0)), pl.BlockSpec((None, D, pw), lambda p, i: (0, 0, p))],
        out_specs=[pl.BlockSpec((tm, pw), lambda p, i: (i, p)), pl.BlockSpec((tm, half), lambda p, i: (i, p))],
        out_shape=[jax.ShapeDtypeStruct((T, F2), BF16), jax.ShapeDtypeStruct((T, F2 // 2), BF16)],
        compiler_params=_params(("parallel", "parallel")),
    )(h, w_in)


def _ffn_da(dxo, w_out, u, name, deps=()):
    T, D = dxo.shape
    F2 = u.shape[1]
    pw = F2 // N_PANEL
    half = pw // 2
    tm = _pick(T, (512, 256, 128))

    def body(d_ref, w_ref, u_ref, *rest):
        du_ref = rest[-1]
        da = 0.5 * lax.dot_general(d_ref[...], w_ref[...], NT_DIMS, preferred_element_type=F32)
        g = u_ref[:, :half].astype(F32)
        up = u_ref[:, half:].astype(F32)
        sg = jax.nn.sigmoid(g)
        silu = g * sg
        du_ref[:, :half] = (da * up * (sg + silu * (1.0 - sg))).astype(BF16)
        du_ref[:, half:] = (da * silu).astype(BF16)

    return pl.pallas_call(
        body, name=name, grid=(N_PANEL, T // tm),
        in_specs=[pl.BlockSpec((tm, D), lambda p, i: (i, 0)), pl.BlockSpec((None, half, D), lambda p, i: (0, p, 0)),
                  pl.BlockSpec((tm, pw), lambda p, i: (i, p))] + [ANY_SPEC] * len(deps),
        out_specs=pl.BlockSpec((tm, pw), lambda p, i: (i, p)),
        out_shape=jax.ShapeDtypeStruct((T, F2), BF16), compiler_params=_params(("parallel", "parallel")),
    )(dxo, w_out, u, *deps)


def _loss_head(y, t, name):
    T, D = y.shape
    tr = _pick(T, (512, 256, 128))

    def body(y_ref, t_ref, dy_ref, dyb_ref, l_ref):
        e = y_ref[...] - t_ref[...]
        dy = e * (1.0 / D)
        dy_ref[...] = dy
        dyb_ref[...] = dy.astype(BF16)

        @pl.when(pl.program_id(0) == 0)
        def _():
            l_ref[...] = jnp.zeros_like(l_ref)

        l_ref[...] += 0.5 * jnp.sum(jnp.mean(e * e, axis=-1, keepdims=True), axis=0, keepdims=True)

    row = pl.BlockSpec((tr, D), lambda i: (i, 0))
    return pl.pallas_call(
        body, name=name, grid=(T // tr,),
        in_specs=[row, row], out_specs=[row, row, pl.BlockSpec((1, 1), lambda i: (0, 0))],
        out_shape=[jax.ShapeDtypeStruct((T, D), F32), jax.ShapeDtypeStruct((T, D), BF16),
                   jax.ShapeDtypeStruct((1, 1), F32)],
        compiler_params=_params(("arbitrary",)),
    )(y, t)


def _rope_tables(S):
    half = ROPE_DIM // 2
    inv = 1.0 / (ROPE_THETA ** (jnp.arange(0, ROPE_DIM, 2, dtype=F32) / ROPE_DIM))
    ang = jnp.arange(S, dtype=F32)[:, None] * inv[None, :]
    cos, sin = jnp.cos(ang), jnp.sin(ang)
    z = jnp.zeros((S, half), F32)
    z2 = jnp.zeros((S, LANES - ROPE_DIM), F32)
    c = jnp.concatenate([cos, cos, z2], axis=1)
    s1 = jnp.concatenate([-sin, z, z2], axis=1)
    s2 = jnp.concatenate([z, sin, z2], axis=1)
    return c, s1, s2


def _rope(r, c, s1, s2):
    return r * c + pltpu.roll(r, LANES - ROPE_DIM // 2, 1) * s1 + pltpu.roll(r, ROPE_DIM // 2, 1) * s2


def _rope_t(d, c, s1, s2):
    return d * c + pltpu.roll(d * s1, ROPE_DIM // 2, 1) + pltpu.roll(d * s2, LANES - ROPE_DIM // 2, 1)


def _lat_norm_fwd(lat, g_cq, g_ckv, name):
    T = lat.shape[0]
    tr = _pick(T, (512, 256, 128))

    def body(lat_ref, gq_ref, gk_ref, cq_ref, ckv_ref):
        for off, g_ref, o_ref in ((0, gq_ref, cq_ref), (Q_LORA, gk_ref, ckv_ref)):
            xv = lat_ref[:, off:off + Q_LORA]
            r = lax.rsqrt(jnp.mean(xv * xv, axis=-1, keepdims=True) + EPS)
            o_ref[...] = ((xv * r) * g_ref[...]).astype(BF16)

    vec = pl.BlockSpec((1, Q_LORA), lambda i: (0, 0))
    out = pl.BlockSpec((tr, Q_LORA), lambda i: (i, 0))
    return pl.pallas_call(
        body, name=name, grid=(T // tr,),
        in_specs=[pl.BlockSpec((tr, LAT_PAD), lambda i: (i, 0)), vec, vec], out_specs=[out, out],
        out_shape=[jax.ShapeDtypeStruct((T, Q_LORA), BF16)] * 2, compiler_params=_params(("parallel",)),
    )(lat, g_cq, g_ckv)


def _lat_norm_bwd(lat, g_cq, g_ckv, dcq, dckv, dkpe, name):
    T = lat.shape[0]
    tr = _pick(T, (256, 128))

    def body(lat_ref, gq_ref, gk_ref, dcq_ref, dckv_ref, dkpe_ref, dlat_ref, dgq_ref, dgk_ref):
        @pl.when(pl.program_id(0) == 0)
        def _():
            dgq_ref[...] = jnp.zeros_like(dgq_ref)
            dgk_ref[...] = jnp.zeros_like(dgk_ref)

        for off, g_ref, d_ref, dg_ref in ((0, gq_ref, dcq_ref, dgq_ref), (Q_LORA, gk_ref, dckv_ref, dgk_ref)):
            xv = lat_ref[:, off:off + Q_LORA]
            dv = d_ref[...]
            r = lax.rsqrt(jnp.mean(xv * xv, axis=-1, keepdims=True) + EPS)
            xhat = xv * r
            dxh = dv * g_ref[...]
            c = jnp.mean(dxh * xhat, axis=-1, keepdims=True)
            dlat_ref[:, off:off + Q_LORA] = (r * (dxh - xhat * c)).astype(BF16)
            dg_ref[...] += jnp.sum(dv * xhat, axis=0, keepdims=True)
        dlat_ref[:, Q_LORA + KV_LORA:] = dkpe_ref[...].astype(BF16)

    vec = pl.BlockSpec((1, Q_LORA), lambda i: (0, 0))
    half = pl.BlockSpec((tr, Q_LORA), lambda i: (i, 0))
    full = pl.BlockSpec((tr, LAT_PAD), lambda i: (i, 0))
    return pl.pallas_call(
        body, name=name, grid=(T // tr,),
        in_specs=[full, vec, vec, half, half, pl.BlockSpec((tr, LANES), lambda i: (i, 0))],
        out_specs=[full, vec, vec],
        out_shape=[jax.ShapeDtypeStruct((T, LAT_PAD), BF16), jax.ShapeDtypeStruct((1, Q_LORA), F32),
                   jax.ShapeDtypeStruct((1, Q_LORA), F32)],
        compiler_params=_params(("arbitrary",)),
    )(lat, g_cq, g_ckv, dcq, dckv, dkpe)


def _mla_prep_fwd(q_raw, kv, lat, g_qn, g_kn, tabs, name):
    T = q_raw.shape[0]
    H = MLA_HEADS
    tr = _pick(T, (256, 128))
    scale = LOG2E / math.sqrt(QK_DIM)

    def body(q_ref, kv_ref, kpe_ref, gq_ref, gk_ref, c_ref, s1_ref, s2_ref, qf_ref, kf_ref, v_ref):
        c, s1, s2 = c_ref[...], s1_ref[...], s2_ref[...]
        gq, gk = gq_ref[...], gk_ref[...]
        kpe = kpe_ref[...]
        kpe_ss = jnp.sum(kpe * kpe, axis=-1, keepdims=True)
        for h in range(H):
            lo = h * HEAD_PAD
            qa = q_ref[:, lo:lo + LANES]
            qb = q_ref[:, lo + LANES:lo + HEAD_PAD]
            ss = jnp.sum(qa * qa + qb * qb, axis=-1, keepdims=True)
            r = lax.rsqrt(ss * (1.0 / QK_DIM) + EPS)
            qf_ref[:, lo:lo + LANES] = (qa * r * gq[:, :LANES] * scale).astype(BF16)
            qf_ref[:, lo + LANES:lo + HEAD_PAD] = (_rope(qb * r * gq[:, LANES:], c, s1, s2) * scale).astype(BF16)
            ka = kv_ref[:, lo:lo + LANES]
            ss = jnp.sum(ka * ka, axis=-1, keepdims=True) + kpe_ss
            r = lax.rsqrt(ss * (1.0 / QK_DIM) + EPS)
            kf_ref[:, lo:lo + LANES] = (ka * r * gk[:, :LANES]).astype(BF16)
            kf_ref[:, lo + LANES:lo + HEAD_PAD] = _rope(kpe * r * gk[:, LANES:], c, s1, s2).astype(BF16)
            v_ref[:, lo:lo + V_DIM] = kv_ref[:, lo + LANES:lo + HEAD_PAD].astype(BF16)
            v_ref[:, lo + V_DIM:lo + HEAD_PAD] = jnp.ones((tr, HEAD_PAD - V_DIM), BF16)

    wide = pl.BlockSpec((tr, H * HEAD_PAD), lambda i: (i, 0))
    lane = pl.BlockSpec((tr, LANES), lambda i: (i, 0))
    gvec = pl.BlockSpec((1, HEAD_PAD), lambda i: (0, 0))
    return pl.pallas_call(
        body, name=name, grid=(T // tr,),
        in_specs=[wide, wide, pl.BlockSpec((tr, LANES), lambda i: (i, (Q_LORA + KV_LORA) // LANES)), gvec, gvec,
                  lane, lane, lane],
        out_specs=[wide, wide, wide],
        out_shape=[jax.ShapeDtypeStruct((T, H * HEAD_PAD), BF16)] * 3,
        compiler_params=_params(("parallel",)),
    )(q_raw, kv, lat, g_qn, g_kn, *tabs)


def _mla_prep_bwd(q_raw, kv, lat, g_qn, g_kn, tabs, dqf, dkf, dv, name):
    T = q_raw.shape[0]
    H = MLA_HEADS
    tr = _pick(T, (128,))

    def body(q_ref, kv_ref, kpe_ref, gq_ref, gk_ref, c_ref, s1_ref, s2_ref, dqf_ref, dkf_ref, dv_ref,
             dq_ref, dkv_ref, dkpe_ref, dgq_ref, dgk_ref):
        @pl.when(pl.program_id(0) == 0)
        def _():
            dgq_ref[...] = jnp.zeros_like(dgq_ref)
            dgk_ref[...] = jnp.zeros_like(dgk_ref)

        c, s1, s2 = c_ref[...], s1_ref[...], s2_ref[...]
        gq, gk = gq_ref[...], gk_ref[...]
        kpe = kpe_ref[...]
        kpe_ss = jnp.sum(kpe * kpe, axis=-1, keepdims=True)
        dkpe = jnp.zeros_like(kpe)
        dgq_a = jnp.zeros((1, LANES), F32)
        dgq_b = jnp.zeros((1, LANES), F32)
        dgk_a = jnp.zeros((1, LANES), F32)
        dgk_b = jnp.zeros((1, LANES), F32)
        for h in range(H):
            lo = h * HEAD_PAD
            xa = q_ref[:, lo:lo + LANES]
            xb = q_ref[:, lo + LANES:lo + HEAD_PAD]
            ss = jnp.sum(xa * xa + xb * xb, axis=-1, keepdims=True)
            r = lax.rsqrt(ss * (1.0 / QK_DIM) + EPS)
            xa, xb = xa * r, xb * r
            da = dqf_ref[:, lo:lo + LANES].astype(F32)
            db = _rope_t(dqf_ref[:, lo + LANES:lo + HEAD_PAD].astype(F32), c, s1, s2)
            dgq_a += jnp.sum(da * xa, axis=0, keepdims=True)
            dgq_b += jnp.sum(db * xb, axis=0, keepdims=True)
            da, db = da * gq[:, :LANES], db * gq[:, LANES:]
            cc = jnp.sum(da * xa + db * xb, axis=-1, keepdims=True) * (1.0 / QK_DIM)
            dq_ref[:, lo:lo + LANES] = (r * (da - xa * cc)).astype(BF16)
            dq_ref[:, lo + LANES:lo + HEAD_PAD] = (r * (db - xb * cc)).astype(BF16)
            xa = kv_ref[:, lo:lo + LANES]
            ss = jnp.sum(xa * xa, axis=-1, keepdims=True) + kpe_ss
            r = lax.rsqrt(ss * (1.0 / QK_DIM) + EPS)
            xa, xb = xa * r, kpe * r
            da = dkf_ref[:, lo:lo + LANES].astype(F32)
            db = _rope_t(dkf_ref[:, lo + LANES:lo + HEAD_PAD].astype(F32), c, s1, s2)
            dgk_a += jnp.sum(da * xa, axis=0, keepdims=True)
            dgk_b += jnp.sum(db * xb, axis=0, keepdims=True)
            da, db = da * gk[:, :LANES], db * gk[:, LANES:]
            cc = jnp.sum(da * xa + db * xb, axis=-1, keepdims=True) * (1.0 / QK_DIM)
            dkv_ref[:, lo:lo + LANES] = (r * (da - xa * cc)).astype(BF16)
            dkpe = dkpe + r * (db - xb * cc)
            dkv_ref[:, lo + LANES:lo + HEAD_PAD] = dv_ref[:, h * V_DIM:(h + 1) * V_DIM].astype(BF16)
        dkpe_ref[...] = dkpe
        dgq_ref[:, :LANES] += dgq_a
        dgq_ref[:, LANES:] += dgq_b
        dgk_ref[:, :LANES] += dgk_a
        dgk_ref[:, LANES:] += dgk_b

    wide = pl.BlockSpec((tr, H * HEAD_PAD), lambda i: (i, 0))
    lane = pl.BlockSpec((tr, LANES), lambda i: (i, 0))
    gvec = pl.BlockSpec((1, HEAD_PAD), lambda i: (0, 0))
    vspec = pl.BlockSpec((tr, H * V_DIM), lambda i: (i, 0))
    return pl.pallas_call(
        body, name=name, grid=(T // tr,),
        in_specs=[wide, wide, pl.BlockSpec((tr, LANES), lambda i: (i, (Q_LORA + KV_LORA) // LANES)), gvec, gvec,
                  lane, lane, lane, wide, wide, vspec],
        out_specs=[wide, wide, lane, gvec, gvec],
        out_shape=[jax.ShapeDtypeStruct((T, H * HEAD_PAD), BF16), jax.ShapeDtypeStruct((T, H * HEAD_PAD), BF16),
                   jax.ShapeDtypeStruct((T, LANES), F32), jax.ShapeDtypeStruct((1, HEAD_PAD), F32),
                   jax.ShapeDtypeStruct((1, HEAD_PAD), F32)],
        compiler_params=_params(("arbitrary",)),
    )(q_raw, kv, lat, g_qn, g_kn, *tabs, dqf, dkf, dv)


def _causal_mask(tq, tk):
    return lax.broadcasted_iota(jnp.int32, (tq, tk), 1) <= lax.broadcasted_iota(jnp.int32, (tq, tk), 0)


NT_DIMS = (((1,), (1,)), ((), ()))
TN_DIMS = (((0,), (0,)), ((), ()))


def _flash_fwd(qf, kf, v, name):
    T = qf.shape[0]
    H, G = MLA_HEADS, FLASH_HEADS_FWD
    t = _pick(T, (512, 256, 128))
    n = T // t
    pairs = [(i, j) for i in range(n) for j in range(i + 1)]
    qi = jnp.asarray([p[0] for p in pairs], jnp.int32)
    kj = jnp.asarray([p[1] for p in pairs], jnp.int32)

    def body(qi_ref, kj_ref, q_ref, k_ref, v_ref, o_ref, lse_ref, *scratch):
        m_sc, acc_sc = scratch[:G], scratch[G:]
        sid = pl.program_id(1)
        i, j = qi_ref[sid], kj_ref[sid]

        @pl.when(j == 0)
        def _():
            for g in range(G):
                m_sc[g][...] = jnp.full_like(m_sc[g], NEG_BIG)
                acc_sc[g][...] = jnp.zeros_like(acc_sc[g])

        def step(masked):
            for g in range(G):
                qk = slice(g * HEAD_PAD, (g + 1) * HEAD_PAD)
                s = lax.dot_general(q_ref[:, qk], k_ref[:, qk], NT_DIMS, preferred_element_type=F32)
                if masked:
                    s = jnp.where(_causal_mask(t, t), s, NEG_BIG)
                m_prev = m_sc[g][:, :1]
                m_new = jnp.maximum(m_prev, jnp.max(s, axis=-1, keepdims=True))
                a = jnp.exp2(m_prev - m_new)
                p = jnp.exp2((s - m_new).astype(BF16))
                acc_sc[g][...] = a * acc_sc[g][...] + jnp.dot(p, v_ref[:, qk], preferred_element_type=F32)
                m_sc[g][...] = jnp.broadcast_to(m_new, (t, LANES))

        @pl.when(j < i)
        def _():
            step(False)

        @pl.when(j == i)
        def _():
            step(True)
            for g in range(G):
                vo = slice(g * V_DIM, (g + 1) * V_DIM)
                l = acc_sc[g][:, V_DIM:]
                o_ref[:, vo] = (acc_sc[g][:, :V_DIM] / l).astype(BF16)
                lse_ref[:, vo] = m_sc[g][...] + jnp.log2(l)

    row = pl.BlockSpec((t, G * V_DIM), lambda h, s, qi, kj: (qi[s], h))
    return pl.pallas_call(
        body, name=name,
        grid_spec=pltpu.PrefetchScalarGridSpec(
            num_scalar_prefetch=2, grid=(H // G, len(pairs)),
            in_specs=[pl.BlockSpec((t, G * HEAD_PAD), lambda h, s, qi, kj: (qi[s], h)),
                      pl.BlockSpec((t, G * HEAD_PAD), lambda h, s, qi, kj: (kj[s], h)),
                      pl.BlockSpec((t, G * HEAD_PAD), lambda h, s, qi, kj: (kj[s], h))],
            out_specs=[row, row],
            scratch_shapes=[pltpu.VMEM((t, LANES), F32)] * G + [pltpu.VMEM((t, HEAD_PAD), F32)] * G),
        out_shape=[jax.ShapeDtypeStruct((T, H * V_DIM), BF16), jax.ShapeDtypeStruct((T, H * V_DIM), F32)],
        compiler_params=_params(("parallel", "arbitrary")),
    )(qi, kj, qf, kf, v)


def _attn_delta(do, o, name):
    T, W = do.shape
    nh = W // V_DIM
    tr = _pick(T, (512, 256, 128))

    def body(do_ref, o_ref, d_ref):
        for h in range(nh):
            sl = slice(h * V_DIM, (h + 1) * V_DIM)
            d = jnp.sum(do_ref[:, sl].astype(F32) * o_ref[:, sl].astype(F32), axis=-1, keepdims=True)
            d_ref[:, sl] = jnp.broadcast_to(d, (tr, V_DIM))

    row = pl.BlockSpec((tr, W), lambda i: (i, 0))
    return pl.pallas_call(
        body, name=name, grid=(T // tr,), in_specs=[row, row], out_specs=row,
        out_shape=jax.ShapeDtypeStruct((T, W), F32), compiler_params=_params(("parallel",)),
    )(do, o)


def _flash_bwd(qf, kf, v, do, lse, delta, name):
    T = qf.shape[0]
    H, G = MLA_HEADS, FLASH_HEADS
    t = _pick(T, (512, 256, 128))
    n = T // t
    scale = 1.0 / math.sqrt(QK_DIM)
    pairs = [(i, j) for j in range(n) for i in range(j, n)]
    qi = jnp.asarray([p[0] for p in pairs], jnp.int32)
    kj = jnp.asarray([p[1] for p in pairs], jnp.int32)

    def body(qi_ref, kj_ref, q_ref, k_ref, v_ref, do_ref, lse_ref, dl_ref, dq_ref, dk_ref, dv_ref,
             dq_acc, dk_acc, dv_acc):
        sid = pl.program_id(1)
        i, j = qi_ref[sid], kj_ref[sid]

        @pl.when(sid == 0)
        def _():
            dq_acc[...] = jnp.zeros_like(dq_acc)

        def step(masked):
            rows = pl.ds(pl.multiple_of(i * t, t), t)
            for g in range(G):
                qk = slice(g * HEAD_PAD, (g + 1) * HEAD_PAD)
                vo = slice(g * V_DIM, (g + 1) * V_DIM)
                q, k, do_ = q_ref[:, qk], k_ref[:, qk], do_ref[:, vo]
                v_ = v_ref[:, g * HEAD_PAD:g * HEAD_PAD + V_DIM]
                s = lax.dot_general(q, k, NT_DIMS, preferred_element_type=F32)
                if masked:
                    s = jnp.where(_causal_mask(t, t), s, NEG_BIG)
                p = jnp.exp2(s - lse_ref[:, g * V_DIM:g * V_DIM + 1])
                dp = lax.dot_general(do_, v_, NT_DIMS, preferred_element_type=F32)
                ds = (p * (dp - dl_ref[:, g * V_DIM:g * V_DIM + 1])).astype(BF16)
                dv = lax.dot_general(p.astype(BF16), do_, TN_DIMS, preferred_element_type=F32)
                dk = lax.dot_general(ds, q, TN_DIMS, preferred_element_type=F32)
                if masked:
                    dv_acc[:, vo] = dv
                    dk_acc[:, qk] = dk
                else:
                    dv_acc[:, vo] += dv
                    dk_acc[:, qk] += dk
                dq_acc[rows, qk] += jnp.dot(ds, k, preferred_element_type=F32) * scale

        @pl.when(i == j)
        def _():
            step(True)

        @pl.when(i > j)
        def _():
            step(False)

        @pl.when(i == n - 1)
        def _():
            dk_ref[...] = (dk_acc[...] * LN2).astype(BF16)
            dv_ref[...] = dv_acc[...].astype(BF16)

        @pl.when(sid == len(pairs) - 1)
        def _():
            dq_ref[...] = dq_acc[...].astype(BF16)

    qs = pl.BlockSpec((t, G * HEAD_PAD), lambda h, s, qi, kj: (qi[s], h))
    rs = pl.BlockSpec((t, G * V_DIM), lambda h, s, qi, kj: (qi[s], h))
    ks = pl.BlockSpec((t, G * HEAD_PAD), lambda h, s, qi, kj: (kj[s], h))
    vs = pl.BlockSpec((t, G * V_DIM), lambda h, s, qi, kj: (kj[s], h))
    return pl.pallas_call(
        body, name=name,
        grid_spec=pltpu.PrefetchScalarGridSpec(
            num_scalar_prefetch=2, grid=(H // G, len(pairs)), in_specs=[qs, ks, ks, rs, rs, rs],
            out_specs=[pl.BlockSpec((T, G * HEAD_PAD), lambda h, s, qi, kj: (0, h)), ks, vs],
            scratch_shapes=[pltpu.VMEM((T, G * HEAD_PAD), F32), pltpu.VMEM((t, G * HEAD_PAD), F32),
                            pltpu.VMEM((t, G * V_DIM), F32)]),
        out_shape=[jax.ShapeDtypeStruct((T, H * HEAD_PAD), BF16), jax.ShapeDtypeStruct((T, H * HEAD_PAD), BF16),
                   jax.ShapeDtypeStruct((T, H * V_DIM), BF16)],
        compiler_params=_params(("parallel", "arbitrary")),
    )(qi, kj, qf, kf, v, do, lse, delta)


def _alibi_slopes():
    tot = DIL_GROUPS * DIL_HEADS
    return [float(np.float32(2.0) ** (np.float32(-8.0) * np.float32(k) / np.float32(tot))) for k in range(1, tot + 1)]


def _dil_masks():
    iq = lax.broadcasted_iota(jnp.int32, (DIL_BLK, DIL_BLK), 0)
    ik = lax.broadcasted_iota(jnp.int32, (DIL_BLK, DIL_BLK), 1)
    return (ik >= iq), (iq + DIL_BLK - ik).astype(F32), (ik <= iq), (iq - ik).astype(F32)


def _dil_norm(x, g):
    r = lax.rsqrt(jnp.mean(x * x, axis=-1, keepdims=True) + EPS)
    return x * r, r


DIL_SUPER = 8
BNT_DIMS = (((2,), (2,)), ((0,), (0,)))
BNN_DIMS = (((2,), (1,)), ((0,), (0,)))
BTN_DIMS = (((1,), (1,)), ((0,), (0,)))


def _dil_chunk(it, nb, d):
    assert nb & (nb - 1) == 0, nb
    r, n = it >> (nb.bit_length() - 1), it & (nb - 1)
    if d > 1:
        tok = pl.ds(n * (d * DIL_BLK) + r, DIL_BLK, stride=d)
    else:
        tok = pl.ds(pl.multiple_of(it * DIL_BLK, DIL_BLK), DIL_BLK)
    return tok, pl.ds(pl.multiple_of((it + 1) * DIL_BLK, DIL_BLK), DIL_BLK)


def _dil_token_rows(bidx, nb, d):
    r, n = divmod(bidx, nb)
    return pl.ds(n * DIL_BLK * d + r, DIL_BLK, stride=d) if d > 1 else pl.ds(bidx * DIL_BLK, DIL_BLK)


def _dil_super_rows(ss):
    base = (1 + ss * DIL_SUPER) * DIL_BLK
    return pl.ds(base, DIL_SUPER * DIL_BLK), pl.ds(base - DIL_BLK, DIL_SUPER * DIL_BLK)


def _dil_b3(x):
    return x.reshape(DIL_SUPER, DIL_BLK, x.shape[-1])


def _dil_scores(q3, kc3, kp3, slope, d, ss, nb):
    ok_p, dist_p, ok_c, dist_c = _dil_masks()
    scale = 1.0 / math.sqrt(DIL_HEAD_DIM)
    bias_p = jnp.where(ok_p, -slope * d * dist_p, NEG_BIG)
    bias_c = jnp.where(ok_c, -slope * d * dist_c, NEG_BIG)
    s_c = lax.dot_general(q3, kc3, BNT_DIMS, preferred_element_type=F32) * scale + bias_c[None]
    s_p = lax.dot_general(q3, kp3, BNT_DIMS, preferred_element_type=F32) * scale + bias_p[None]
    bidx = ss * DIL_SUPER + lax.broadcasted_iota(jnp.int32, s_p.shape, 0)
    s_p = jnp.where((bidx & (nb - 1)) == 0, NEG_BIG, s_p)
    return s_c, s_p


def _dil_fwd(qkv, g_qn, g_kn, slopes, name):
    T = qkv.shape[0]
    GH = DIL_GROUPS * DIL_HEADS
    scale = 1.0 / math.sqrt(DIL_HEAD_DIM)

    def body(sl_ref, q_ref, k_ref, v_ref, gq_ref, gk_ref, o_ref, lse_ref, qn_pm, kn_pm, v_pm):
        gh = pl.program_id(0)
        slope = sl_ref[gh]
        gq, gk = gq_ref[...], gk_ref[...]
        pad = pl.ds(0, DIL_BLK)
        kn_pm[pad, :] = jnp.zeros((DIL_BLK, DIL_HEAD_DIM), BF16)
        v_pm[pad, :] = jnp.zeros((DIL_BLK, DIL_HEAD_DIM), BF16)
        for g, (_, d) in enumerate(DIL_PAIRS):
            @pl.when((gh >= g * DIL_HEADS) & (gh < (g + 1) * DIL_HEADS))
            def _(d=d):
                nb = T // (d * DIL_BLK)

                def fill(it, _):
                    tok, dst = _dil_chunk(it, nb, d)
                    qn_pm[dst, :] = (_dil_norm(q_ref[tok, :], gq)[0] * gq).astype(BF16)
                    kn_pm[dst, :] = (_dil_norm(k_ref[tok, :], gk)[0] * gk).astype(BF16)
                    v_pm[dst, :] = v_ref[tok, :].astype(BF16)
                    return 0
                lax.fori_loop(0, T // DIL_BLK, fill, 0, unroll=4)

                for ss in range(T // DIL_BLK // DIL_SUPER):
                    cur, prv = _dil_super_rows(ss)
                    q3, kc3, kp3 = _dil_b3(qn_pm[cur, :]), _dil_b3(kn_pm[cur, :]), _dil_b3(kn_pm[prv, :])
                    s_c, s_p = _dil_scores(q3, kc3, kp3, slope, d, ss, nb)
                    m = jnp.max(jnp.maximum(s_c, s_p), axis=-1, keepdims=True)
                    p_c = jnp.exp(s_c - m)
                    p_p = jnp.exp(s_p - m)
                    l = jnp.sum(p_c, axis=-1, keepdims=True) + jnp.sum(p_p, axis=-1, keepdims=True)
                    acc = lax.dot_general(p_c.astype(BF16), _dil_b3(v_pm[cur, :]), BNN_DIMS, preferred_element_type=F32)
                    acc += lax.dot_general(p_p.astype(BF16), _dil_b3(v_pm[prv, :]), BNN_DIMS, preferred_element_type=F32)
                    o3 = acc / l
                    lse3 = jnp.broadcast_to(m + jnp.log(l), o3.shape)
                    for b in range(DIL_SUPER):
                        tok = _dil_token_rows(ss * DIL_SUPER + b, nb, d)
                        o_ref[tok, :] = o3[b]
                        lse_ref[tok, :] = lse3[b]

    col = lambda off: pl.BlockSpec((T, DIL_HEAD_DIM), lambda gh, sl: (0, gh + off))
    gvec = pl.BlockSpec((1, DIL_HEAD_DIM), lambda gh, sl: (0, 0))
    return pl.pallas_call(
        body, name=name,
        grid_spec=pltpu.PrefetchScalarGridSpec(
            num_scalar_prefetch=1, grid=(GH,),
            in_specs=[col(0), col(GH), col(2 * GH), gvec, gvec], out_specs=[col(0), col(0)],
            scratch_shapes=[pltpu.VMEM((DIL_BLK + T, DIL_HEAD_DIM), BF16)] * 3),
        out_shape=[jax.ShapeDtypeStruct((T, GH * DIL_HEAD_DIM), F32)] * 2,
        compiler_params=_params(("parallel",)),
    )(slopes, qkv, qkv, qkv, g_qn, g_kn)


def _dil_merge(o_g, lse_g, name):
    T = o_g.shape[0]
    W = DIL_HEADS * DIL_HEAD_DIM
    tr = _pick(T, (256, 128))

    def body(o0, o1, o2, l0, l1, l2, o_ref, lse_ref):
        a, b, c = l0[...], l1[...], l2[...]
        m = jnp.maximum(jnp.maximum(a, b), c)
        ea, eb, ec = jnp.exp(a - m), jnp.exp(b - m), jnp.exp(c - m)
        tot = ea + eb + ec
        o_ref[...] = ((o0[...] * ea + o1[...] * eb + o2[...] * ec) / tot).astype(BF16)
        lse_ref[...] = m + jnp.log(tot)

    grp = lambda g: pl.BlockSpec((tr, W), lambda i: (i, g))
    out = pl.BlockSpec((tr, W), lambda i: (i, 0))
    return pl.pallas_call(
        body, name=name, grid=(T // tr,),
        in_specs=[grp(0), grp(1), grp(2), grp(0), grp(1), grp(2)], out_specs=[out, out],
        out_shape=[jax.ShapeDtypeStruct((T, W), BF16), jax.ShapeDtypeStruct((T, W), F32)],
        compiler_params=_params(("parallel",)),
    )(o_g, o_g, o_g, lse_g, lse_g, lse_g)


def _dil_bwd(qkv, g_qn, g_kn, slopes, do, delta, lse, name):
    T = qkv.shape[0]
    GH = DIL_GROUPS * DIL_HEADS
    scale = 1.0 / math.sqrt(DIL_HEAD_DIM)
    nchunk = T // DIL_BLK

    def body(sl_ref, q_ref, k_ref, v_ref, gq_ref, gk_ref, do_ref, dl_ref, lse_ref,
             dq_ref, dk_ref, dv_ref, dgq_ref, dgk_ref,
             qn_pm, kn_pm, v_pm, do_pm, lse_pm, dl_pm, dq_pm, dk_pm, dv_pm, tok_sc):
        gh = pl.program_id(0)
        slope = sl_ref[gh]
        gq, gk = gq_ref[...], gk_ref[...]

        @pl.when(gh == 0)
        def _():
            dgq_ref[...] = jnp.zeros_like(dgq_ref)
            dgk_ref[...] = jnp.zeros_like(dgk_ref)

        pad = pl.ds(0, DIL_BLK)
        kn_pm[pad, :] = jnp.zeros((DIL_BLK, DIL_HEAD_DIM), BF16)
        v_pm[pad, :] = jnp.zeros((DIL_BLK, DIL_HEAD_DIM), BF16)
        dk_pm[...] = jnp.zeros_like(dk_pm)
        dv_pm[...] = jnp.zeros_like(dv_pm)
        for g, (_, d) in enumerate(DIL_PAIRS):
            @pl.when((gh >= g * DIL_HEADS) & (gh < (g + 1) * DIL_HEADS))
            def _(d=d):
                nb = T // (d * DIL_BLK)

                def fill(it, _):
                    tok, dst = _dil_chunk(it, nb, d)
                    qn_pm[dst, :] = (_dil_norm(q_ref[tok, :], gq)[0] * gq).astype(BF16)
                    kn_pm[dst, :] = (_dil_norm(k_ref[tok, :], gk)[0] * gk).astype(BF16)
                    v_pm[dst, :] = v_ref[tok, :].astype(BF16)
                    do_pm[dst, :] = do_ref[tok, :].astype(BF16)
                    lse_pm[dst, :] = lse_ref[tok, :]
                    dl_pm[dst, :] = dl_ref[tok, :]
                    return 0
                lax.fori_loop(0, nchunk, fill, 0, unroll=4)

                for ss in range(nchunk // DIL_SUPER):
                    cur, prv = _dil_super_rows(ss)
                    q3, kc3, kp3 = _dil_b3(qn_pm[cur, :]), _dil_b3(kn_pm[cur, :]), _dil_b3(kn_pm[prv, :])
                    vc3, vp3, do3 = _dil_b3(v_pm[cur, :]), _dil_b3(v_pm[prv, :]), _dil_b3(do_pm[cur, :])
                    ls = _dil_b3(lse_pm[cur, :])[:, :, :1]
                    delta = _dil_b3(dl_pm[cur, :])[:, :, :1]
                    s_c, s_p = _dil_scores(q3, kc3, kp3, slope, d, ss, nb)
                    p_c = jnp.exp(s_c - ls)
                    p_p = jnp.exp(s_p - ls)
                    dp_c = lax.dot_general(do3, vc3, BNT_DIMS, preferred_element_type=F32)
                    dp_p = lax.dot_general(do3, vp3, BNT_DIMS, preferred_element_type=F32)
                    ds_c = (p_c * (dp_c - delta)).astype(BF16)
                    ds_p = (p_p * (dp_p - delta)).astype(BF16)
                    dq3 = (lax.dot_general(ds_c, kc3, BNN_DIMS, preferred_element_type=F32)
                           + lax.dot_general(ds_p, kp3, BNN_DIMS, preferred_element_type=F32)) * scale
                    flat = lambda x: x.reshape(DIL_SUPER * DIL_BLK, DIL_HEAD_DIM)
                    dq_pm[cur, :] = flat(dq3)
                    dk_pm[cur, :] += flat(lax.dot_general(ds_c, q3, BTN_DIMS, preferred_element_type=F32)) * scale
                    dv_pm[cur, :] += flat(lax.dot_general(p_c.astype(BF16), do3, BTN_DIMS, preferred_element_type=F32))
                    dk_pm[prv, :] += flat(lax.dot_general(ds_p, q3, BTN_DIMS, preferred_element_type=F32)) * scale
                    dv_pm[prv, :] += flat(lax.dot_general(p_p.astype(BF16), do3, BTN_DIMS, preferred_element_type=F32))

                def to_tokens(src_pm):
                    def move(it, _):
                        tok, src = _dil_chunk(it, nb, d)
                        tok_sc[tok, :] = src_pm[src, :]
                        return 0
                    lax.fori_loop(0, nchunk, move, 0, unroll=4)

                def norm_bwd(x_ref, gvec, out_ref):
                    big = 4 * DIL_BLK

                    def fin(ci, dg):
                        rows = pl.ds(pl.multiple_of(ci * big, big), big)
                        xhat, r = _dil_norm(x_ref[rows, :], gvec)
                        dn = tok_sc[rows, :]
                        dxh = dn * gvec
                        c = jnp.mean(dxh * xhat, axis=-1, keepdims=True)
                        out_ref[rows, :] = (r * (dxh - xhat * c)).astype(BF16)
                        return dg + jnp.sum(dn * xhat, axis=0, keepdims=True)
                    return lax.fori_loop(0, T // big, fin, jnp.zeros((1, DIL_HEAD_DIM), F32))

                to_tokens(dq_pm)
                dgq_ref[...] += norm_bwd(q_ref, gq, dq_ref)
                to_tokens(dk_pm)
                dgk_ref[...] += norm_bwd(k_ref, gk, dk_ref)
                to_tokens(dv_pm)
                dv_ref[...] = tok_sc[...].astype(BF16)

    col = lambda off: pl.BlockSpec((T, DIL_HEAD_DIM), lambda gh, sl: (0, gh + off))
    hcol = pl.BlockSpec((T, DIL_HEAD_DIM), lambda gh, sl: (0, gh % DIL_HEADS))
    gvec = pl.BlockSpec((1, DIL_HEAD_DIM), lambda gh, sl: (0, 0))
    wide = jax.ShapeDtypeStruct((T, GH * DIL_HEAD_DIM), BF16)
    vec = jax.ShapeDtypeStruct((1, DIL_HEAD_DIM), F32)
    pm = lambda dt: pltpu.VMEM((DIL_BLK + T, DIL_HEAD_DIM), dt)
    return pl.pallas_call(
        body, name=name,
        grid_spec=pltpu.PrefetchScalarGridSpec(
            num_scalar_prefetch=1, grid=(GH,),
            in_specs=[col(0), col(GH), col(2 * GH), gvec, gvec, hcol, hcol, hcol],
            out_specs=[col(0), col(0), col(0), gvec, gvec],
            scratch_shapes=[pm(BF16)] * 4 + [pm(F32)] * 5 + [pltpu.VMEM((T, DIL_HEAD_DIM), F32)]),
        out_shape=[wide, wide, wide, vec, vec],
        compiler_params=_params(("arbitrary",)),
    )(slopes, qkv, qkv, qkv, g_qn, g_kn, do, delta, lse)


def _my_pos():
    return lax.axis_index("x"), lax.axis_index("y"), lax.axis_index("c")


def _peer(pos, j):
    x, y, c = pos
    px = 1 - x if j & 4 else x
    py = 1 - y if j & 2 else y
    pc = 1 - c if j & 1 else c
    return (px, py, pc), 4 * px + 2 * py + pc


def _slot(idx, paired):
    if not paired:
        return idx
    return jnp.where(idx < N_DEV // 2, 2 * idx, 2 * idx - (N_DEV - 1))


def _shard_slice(ref, axis, idx, size, paired=False):
    sl = [slice(None)] * len(ref.shape)
    sl[axis] = pl.ds(pl.multiple_of(_slot(idx, paired) * size, 8), size)
    return ref.at[tuple(sl)]


HBM_SPEC = pl.BlockSpec(memory_space=pltpu.HBM)
SEM_SPEC = pl.BlockSpec(memory_space=pltpu.SEMAPHORE)
DATAFLOW = pltpu.SideEffectType.DATAFLOW_SIDE_EFFECTING
N_PEER = N_DEV - 1


def _scatter_copy(axis, grad, slots, frm, to, dev, send_sem, recv_sem):
    ax, paired = axis
    src = _shard_slice(grad, ax, to, grad.shape[ax] // N_DEV, paired)
    return pltpu.make_async_remote_copy(src_ref=src, dst_ref=slots.at[frm], send_sem=send_sem, recv_sem=recv_sem,
                                        device_id=dev, device_id_type=MESH)


def _scatter_start(grads, axes, name):
    n = len(grads)

    def body(*refs):
        outs = refs[2 * n:]
        send, recv, token = outs[:n], outs[n:2 * n], outs[4 * n]
        pos = _my_pos()
        me = 4 * pos[0] + 2 * pos[1] + pos[2]
        for a in range(n):
            for j in range(1, N_DEV):
                dev, pid = _peer(pos, j)
                _scatter_copy(axes[a], refs[2 * a], refs[2 * a + 1], me, pid, dev, send[a].at[j - 1],
                              recv[a].at[j - 1]).start()
        token[...] = jnp.zeros_like(token)

    ops = []
    for g, (ax, _) in zip(grads, axes):
        shp = list(g.shape)
        shp[ax] //= N_DEV
        ops += [g, lax.empty((N_DEV,) + tuple(shp), g.dtype)]
    sems = [pltpu.SemaphoreType.DMA((N_PEER,))] * (2 * n)
    res = pl.pallas_call(
        body, name=name,
        out_shape=sems + [pltpu.HBM(o.shape, o.dtype) for o in ops] + [jax.ShapeDtypeStruct((8, LANES), F32)],
        in_specs=[HBM_SPEC] * len(ops),
        out_specs=[SEM_SPEC] * (2 * n) + [HBM_SPEC] * len(ops) + [pl.BlockSpec(memory_space=pltpu.VMEM)],
        input_output_aliases={i: 2 * n + i for i in range(len(ops))},
        compiler_params=pltpu.CompilerParams(has_side_effects=DATAFLOW),
    )(*[pltpu.with_memory_space_constraint(o, pltpu.HBM) for o in ops])
    items = [(res[a], res[n + a], res[2 * n + 2 * a], res[2 * n + 2 * a + 1]) for a in range(n)]
    return items, res[4 * n]


def _scatter_wait(items, axes, after, name):
    n = len(items)

    def body(*refs):
        send, recv = refs[2 * n:3 * n], refs[3 * n:4 * n]
        pos = _my_pos()
        me = 4 * pos[0] + 2 * pos[1] + pos[2]
        for a in range(n):
            for j in range(1, N_DEV):
                dev, pid = _peer(pos, j)
                cp = _scatter_copy(axes[a], refs[2 * a], refs[2 * a + 1], pid, me, dev, send[a].at[j - 1],
                                   recv[a].at[j - 1])
                cp.wait_send()
                cp.wait_recv()

    ops = [b for it in items for b in it[2:]]
    res = pl.pallas_call(
        body, name=name,
        out_shape=[pltpu.HBM(o.shape, o.dtype) for o in ops],
        in_specs=[HBM_SPEC] * len(ops) + [SEM_SPEC] * (2 * n) + [ANY_SPEC],
        out_specs=[HBM_SPEC] * len(ops),
        input_output_aliases={i: i for i in range(len(ops))},
        compiler_params=pltpu.CompilerParams(has_side_effects=DATAFLOW),
    )(*ops, *[it[0] for it in items], *[it[1] for it in items], after)
    return [(res[2 * a], res[2 * a + 1]) for a in range(n)]


SIBLING = 1
ICI_PEERS = (2, 4, 6)


def _gather_copy(buf, axis, shard, dev, send_sem, recv_sem):
    ax, paired = axis
    piece = _shard_slice(buf, ax, shard, buf.shape[ax] // N_DEV, paired)
    return pltpu.make_async_remote_copy(src_ref=piece, dst_ref=piece, send_sem=send_sem, recv_sem=recv_sem,
                                        device_id=dev, device_id_type=MESH)


def _gather_start(bufs, axes, name):
    n = len(bufs)

    def body(*refs):
        ins, outs = refs[:n], refs[n:]
        send, r_sib, r_ici, token = outs[:n], outs[n:2 * n], outs[2 * n:3 * n], outs[4 * n]
        pos = _my_pos()
        me = 4 * pos[0] + 2 * pos[1] + pos[2]
        for a in range(n):
            dev, _ = _peer(pos, SIBLING)
            _gather_copy(ins[a], axes[a], me, dev, send[a].at[0], r_sib[a].at[0]).start()
            for k, j in enumerate(ICI_PEERS):
                dev, _ = _peer(pos, j)
                _gather_copy(ins[a], axes[a], me, dev, send[a].at[1 + k], r_ici[a].at[k]).start()
        token[...] = jnp.zeros_like(token)

    sems = ([pltpu.SemaphoreType.DMA((1 + len(ICI_PEERS),))] * n + [pltpu.SemaphoreType.DMA((1,))] * n
            + [pltpu.SemaphoreType.DMA((len(ICI_PEERS),))] * n)
    res = pl.pallas_call(
        body, name=name,
        out_shape=sems + [pltpu.HBM(b.shape, b.dtype) for b in bufs] + [jax.ShapeDtypeStruct((8, LANES), F32)],
        in_specs=[HBM_SPEC] * n,
        out_specs=[SEM_SPEC] * (3 * n) + [HBM_SPEC] * n + [pl.BlockSpec(memory_space=pltpu.VMEM)],
        input_output_aliases={i: 3 * n + i for i in range(n)},
        compiler_params=pltpu.CompilerParams(has_side_effects=DATAFLOW),
    )(*[pltpu.with_memory_space_constraint(b, pltpu.HBM) for b in bufs])
    items = [dict(send=res[a], r_sib=res[n + a], r_ici=res[2 * n + a], buf=res[3 * n + a]) for a in range(n)]
    return items, res[4 * n]


def _gather_relay(items, axes, after, name):
    n = len(items)

    def body(*refs):
        ins, r_ici = refs[:n], refs[n:2 * n]
        outs = refs[2 * n + 1:]
        s_rel, r_rel, token = outs[:n], outs[n:2 * n], outs[3 * n]
        pos = _my_pos()
        sib, _ = _peer(pos, SIBLING)
        for a in range(n):
            for k, j in enumerate(ICI_PEERS):
                dev, pid = _peer(pos, j)
                _gather_copy(ins[a], axes[a], pid, dev, s_rel[a].at[k], r_ici[a].at[k]).wait_recv()
                _gather_copy(ins[a], axes[a], pid, sib, s_rel[a].at[k], r_rel[a].at[k]).start()
        token[...] = jnp.zeros_like(token)

    bufs = [it["buf"] for it in items]
    sems = [pltpu.SemaphoreType.DMA((len(ICI_PEERS),))] * (2 * n)
    res = pl.pallas_call(
        body, name=name,
        out_shape=sems + [pltpu.HBM(b.shape, b.dtype) for b in bufs] + [jax.ShapeDtypeStruct((8, LANES), F32)],
        in_specs=[HBM_SPEC] * n + [SEM_SPEC] * n + [ANY_SPEC],
        out_specs=[SEM_SPEC] * (2 * n) + [HBM_SPEC] * n + [pl.BlockSpec(memory_space=pltpu.VMEM)],
        input_output_aliases={i: 2 * n + i for i in range(n)},
        compiler_params=pltpu.CompilerParams(has_side_effects=DATAFLOW),
    )(*bufs, *[it["r_ici"] for it in items], after)
    out = [dict(send=it["send"], r_sib=it["r_sib"], s_rel=res[a], r_rel=res[n + a], buf=res[2 * n + a])
           for a, it in enumerate(items)]
    return out, res[3 * n]


def _gather_wait(items, axes, after, name):
    n = len(items)

    def body(*refs):
        ins = refs[:n]
        send, r_sib, s_rel, r_rel = (refs[(1 + q) * n:(2 + q) * n] for q in range(4))
        pos = _my_pos()
        me = 4 * pos[0] + 2 * pos[1] + pos[2]
        sib, sib_id = _peer(pos, SIBLING)
        for a in range(n):
            for k in range(1 + len(ICI_PEERS)):
                _gather_copy(ins[a], axes[a], me, sib, send[a].at[k], r_sib[a].at[0]).wait_send()
            _gather_copy(ins[a], axes[a], sib_id, sib, send[a].at[0], r_sib[a].at[0]).wait_recv()
            for k, j in enumerate(ICI_PEERS):
                _, pid = _peer(pos, j)
                _, far = _peer(pos, j ^ SIBLING)
                _gather_copy(ins[a], axes[a], pid, sib, s_rel[a].at[k], r_rel[a].at[k]).wait_send()
                _gather_copy(ins[a], axes[a], far, sib, s_rel[a].at[k], r_rel[a].at[k]).wait_recv()

    bufs = [it["buf"] for it in items]
    res = pl.pallas_call(
        body, name=name,
        out_shape=[pltpu.HBM(b.shape, b.dtype) for b in bufs],
        in_specs=[HBM_SPEC] * n + [SEM_SPEC] * (4 * n) + [ANY_SPEC],
        out_specs=[HBM_SPEC] * n,
        input_output_aliases={i: i for i in range(n)},
        compiler_params=pltpu.CompilerParams(has_side_effects=DATAFLOW),
    )(*bufs, *[it["send"] for it in items], *[it["r_sib"] for it in items], *[it["s_rel"] for it in items],
      *[it["r_rel"] for it in items], after)
    return list(res)


def _gain_allreduce(v, name):
    n = v.shape[1]

    def body(v_ref, o_ref, slots, send_sems, recv_sems):
        pos = _my_pos()
        me = 4 * pos[0] + 2 * pos[1] + pos[2]
        slots[me] = v_ref[...]
        copies = []
        for j in range(1, N_DEV):
            dev, _ = _peer(pos, j)
            cp = pltpu.make_async_remote_copy(
                src_ref=slots.at[me], dst_ref=slots.at[me], send_sem=send_sems.at[j], recv_sem=recv_sems.at[j],
                device_id=dev, device_id_type=MESH)
            cp.start()
            copies.append(cp)
        for j in range(1, N_DEV):
            dev, pid = _peer(pos, j)
            pltpu.make_async_remote_copy(
                src_ref=slots.at[me], dst_ref=slots.at[pid], send_sem=send_sems.at[j], recv_sem=recv_sems.at[j],
                device_id=dev, device_id_type=MESH).wait_recv()
        for cp in copies:
            cp.wait_send()
        acc = slots[0]
        for s in range(1, N_DEV):
            acc = acc + slots[s]
        o_ref[...] = acc

    return pl.pallas_call(
        body, name=name, out_shape=jax.ShapeDtypeStruct((1, n), F32),
        in_specs=[pl.BlockSpec(memory_space=pltpu.VMEM)], out_specs=pl.BlockSpec(memory_space=pltpu.VMEM),
        scratch_shapes=[pltpu.VMEM((N_DEV, 1, n), F32), pltpu.SemaphoreType.DMA((N_DEV,)),
                        pltpu.SemaphoreType.DMA((N_DEV,))],
        compiler_params=pltpu.CompilerParams(has_side_effects=True),
    )(v)


def _adamw(parts, own, me, w, m, v, layer, prev, name, own_axis=None):
    L, R, C = w.shape
    P = parts.shape[0]
    tr = _pick(R, (128, 64, 32, 16, 8, 1))
    c1 = 1.0 - ADAM_B1 ** ADAM_STEP
    c2 = 1.0 - ADAM_B2 ** ADAM_STEP
    n_in = 4 if own is None else 5

    def body(me_ref, *refs):
        p_ref = refs[0]
        w_ref, m_ref, v_ref = refs[n_in - 3:n_in]
        g_out, d_out, m_out, v_out, tok = refs[-5:]
        g = None
        for s in range(P):
            part = p_ref[s]
            if own is not None:
                part = jnp.where(me_ref[0] == s, refs[1][...], part)
            g = part.astype(F32) if g is None else g + part.astype(F32)
        mn = ADAM_B1 * m_ref[...] + (1.0 - ADAM_B1) * g
        vn = ADAM_B2 * v_ref[...] + (1.0 - ADAM_B2) * (g * g)
        g_out[...] = g
        m_out[...] = mn
        v_out[...] = vn
        d_out[...] = -ADAM_LR * ((mn / c1) / (jnp.sqrt(vn / c2) + ADAM_EPS) + ADAM_WD * w_ref[...])
        tok[...] = jnp.zeros_like(tok)

    row = pl.BlockSpec((None, tr, C), lambda i, me_ref: (layer, i, 0))
    in_specs = [pl.BlockSpec((P, tr, C), lambda i, me_ref: (0, i, 0))]
    args = [parts]
    if own is not None:
        if own_axis is None:
            own_idx = lambda i, me_ref: (i, 0)
        elif own_axis[0] == 0:
            own_idx = lambda i, me_ref: (_slot(me_ref[0], own_axis[1]) * (R // tr) + i, 0)
        else:
            own_idx = lambda i, me_ref: (i, _slot(me_ref[0], own_axis[1]))
        in_specs.append(pl.BlockSpec((tr, C), own_idx))
        args.append(own)
    in_specs += [row, row, row]
    args += [w, m, v]
    aliases = {}
    if prev is not None:
        in_specs += [ANY_SPEC] * 4
        aliases = {1 + len(args) + k: k for k in range(4)}
        args += list(prev)
    shp = jax.ShapeDtypeStruct((L, R, C), F32)
    res = pl.pallas_call(
        body, name=name,
        grid_spec=pltpu.PrefetchScalarGridSpec(
            num_scalar_prefetch=1, grid=(R // tr,), in_specs=in_specs,
            out_specs=[row] * 4 + [pl.BlockSpec((8, LANES), lambda i, me_ref: (0, 0))]),
        out_shape=[shp] * 4 + [jax.ShapeDtypeStruct((8, LANES), F32)],
        input_output_aliases=aliases, compiler_params=_params(("arbitrary",)),
    )(me, *args)
    return res[:4], res[4]


def _pad_heads(w):
    lead = w.shape[:-1]
    n = w.shape[-1] // QK_DIM
    w = w.reshape(lead + (n, QK_DIM))
    w = jnp.pad(w, [(0, 0)] * len(lead) + [(0, 0), (0, HEAD_PAD - QK_DIM)])
    return w.reshape(lead + (n * HEAD_PAD,))


def _unpad_heads(w):
    lead = w.shape[:-1]
    n = w.shape[-1] // HEAD_PAD
    return w.reshape(lead + (n, HEAD_PAD))[..., :QK_DIM].reshape(lead + (n * QK_DIM,))


def kernel(x, ffn1_norm, ffn1_w_in, ffn1_w_out, mix_norm, ffn2_norm, ffn2_w_in, ffn2_w_out, mla_w_down, mla_g_cq, mla_g_ckv, mla_w_uq, mla_w_ukv, mla_g_qn, mla_g_kn, mla_w_o, dil_w_qkv, dil_g_qn, dil_g_kn, dil_w_o, loss_target, m_ffn1_norm, m_ffn1_w_in, m_ffn1_w_out, m_mix_norm, m_ffn2_norm, m_ffn2_w_in, m_ffn2_w_out, m_mla_w_down, m_mla_g_cq, m_mla_g_ckv, m_mla_w_uq, m_mla_w_ukv, m_mla_g_qn, m_mla_g_kn, m_mla_w_o, m_dil_w_qkv, m_dil_g_qn, m_dil_g_kn, m_dil_w_o, v_ffn1_norm, v_ffn1_w_in, v_ffn1_w_out, v_mix_norm, v_ffn2_norm, v_ffn2_w_in, v_ffn2_w_out, v_mla_w_down, v_mla_g_cq, v_mla_g_ckv, v_mla_w_uq, v_mla_w_ukv, v_mla_g_qn, v_mla_g_kn, v_mla_w_o, v_dil_w_qkv, v_dil_g_qn, v_dil_g_kn, v_dil_w_o):
    names = ["ffn1_norm", "ffn1_w_in", "ffn1_w_out", "mix_norm", "ffn2_norm", "ffn2_w_in", "ffn2_w_out", "mla_w_down",
             "mla_g_cq", "mla_g_ckv", "mla_w_uq", "mla_w_ukv", "mla_g_qn", "mla_g_kn", "mla_w_o", "dil_w_qkv",
             "dil_g_qn", "dil_g_kn", "dil_w_o"]
    W = dict(zip(names, [ffn1_norm, ffn1_w_in, ffn1_w_out, mix_norm, ffn2_norm, ffn2_w_in, ffn2_w_out, mla_w_down,
                         mla_g_cq, mla_g_ckv, mla_w_uq, mla_w_ukv, mla_g_qn, mla_g_kn, mla_w_o, dil_w_qkv,
                         dil_g_qn, dil_g_kn, dil_w_o]))
    M1 = dict(zip(names, [m_ffn1_norm, m_ffn1_w_in, m_ffn1_w_out, m_mix_norm, m_ffn2_norm, m_ffn2_w_in, m_ffn2_w_out,
                          m_mla_w_down, m_mla_g_cq, m_mla_g_ckv, m_mla_w_uq, m_mla_w_ukv, m_mla_g_qn, m_mla_g_kn,
                          m_mla_w_o, m_dil_w_qkv, m_dil_g_qn, m_dil_g_kn, m_dil_w_o]))
    V2 = dict(zip(names, [v_ffn1_norm, v_ffn1_w_in, v_ffn1_w_out, v_mix_norm, v_ffn2_norm, v_ffn2_w_in, v_ffn2_w_out,
                          v_mla_w_down, v_mla_g_cq, v_mla_g_ckv, v_mla_w_uq, v_mla_w_ukv, v_mla_g_qn, v_mla_g_kn,
                          v_mla_w_o, v_dil_w_qkv, v_dil_g_qn, v_dil_g_kn, v_dil_w_o]))
    S, D = x.shape[1], x.shape[2]
    x0 = x.reshape(S, D)
    tgt = loss_target.reshape(S, D)

    big = ["ffn1_w_in", "ffn1_w_out", "ffn2_w_in", "ffn2_w_out", "mla_w_down", "mla_w_uq", "mla_w_ukv", "mla_w_o",
           "dil_w_qkv", "dil_w_o"]
    shard_dim = {"ffn1_w_in": 2, "ffn1_w_out": 1, "ffn2_w_in": 2, "ffn2_w_out": 1, "mla_w_down": 1, "mla_w_uq": 2,
                 "mla_w_ukv": 2, "mla_w_o": 1, "dil_w_qkv": 2, "dil_w_o": 2}
    paired = ("ffn1_w_in", "ffn2_w_in")
    shard_axis = {n: (d, n in paired) for n, d in shard_dim.items()}
    grad_axis = {n: (d - 1, n in paired) for n, d in shard_dim.items()}

    def padded(n, w):
        if n == "mla_w_down":
            return jnp.pad(w, ((0, 0), (0, 0), (0, LAT_PAD - w.shape[2])))
        if n == "mla_w_uq":
            return _pad_heads(w)
        return w

    depth = ffn1_norm.shape[0]
    blocks = []
    for l in range(depth):
        mixer = (["mla_w_down", "mla_w_uq", "mla_w_ukv", "mla_w_o"] if l % 2 == 0 else ["dil_w_qkv", "dil_w_o"])
        blocks.append((f"ffn1_{l}", [("ffn1_w_in", l), ("ffn1_w_out", l)]))
        blocks.append((f"mix_{l}", [(n, l // 2) for n in mixer]))
        blocks.append((f"ffn2_{l}", [("ffn2_w_in", l), ("ffn2_w_out", l)]))
    order = [k for _, keys in blocks for k in keys]
    me = (4 * lax.axis_index("x") + 2 * lax.axis_index("y") + lax.axis_index("c")).astype(jnp.int32).reshape(1)
    def cast(key, deps=()):
        n, l = key
        return _cast_into_gathered(padded(n, W[n]), l, shard_axis[n], me, f"cast_{n}_{l}", deps=deps)

    items0, token0 = _gather_start([cast(order[0])], [shard_axis[order[0][0]]], "gather_start_first")
    rest = order[1:]
    items1, ag_token = _gather_start([cast(k, deps=[token0]) for k in rest], [shard_axis[k[0]] for k in rest],
                                     "gather_start_rest")
    ag_items = dict(zip(order, items0 + items1))
    full = {}

    def relay(keys, after, tag):
        out, token = _gather_relay([ag_items[k] for k in keys], [shard_axis[k[0]] for k in keys], after,
                                   f"gather_relay_{tag}")
        ag_items.update(zip(keys, out))
        return [token]

    def relay_next(bi, after):
        return relay(blocks[bi + 1][1], after, blocks[bi + 1][0]) if bi + 1 < len(blocks) else []

    def fetch(keys, after, tag):
        lands = _gather_wait([ag_items[k] for k in keys], [shard_axis[k[0]] for k in keys], after,
                             f"gather_wait_{tag}")
        full.update(zip(keys, lands))

    g_qn = _pad_heads(mla_g_qn)
    g_kn = _pad_heads(mla_g_kn)
    tabs = _rope_tables(S)
    slopes = jnp.asarray(_alibi_slopes(), F32)

    grads = {}
    gain_g = {}

    out_g, out_d, out_m, out_v = {}, {}, {}, {}
    pending = []
    lag = 3

    def scatter_start(tag, keys):
        items, token = _scatter_start([grads[k] for k in keys], [grad_axis[k[0]] for k in keys],
                                      f"scatter_start_{tag}")
        pending.append((tag, keys, items))
        return token

    def scatter_finish(after):
        tag, keys, items = pending.pop(0)
        lands = _scatter_wait(items, [grad_axis[k[0]] for k in keys], after, f"scatter_wait_{tag}")
        tokens = []
        for (n, l), (own, p) in zip(keys, lands):
            own_axis = grad_axis[n]
            if n in ("mla_w_down", "mla_w_uq"):
                ax, pair = grad_axis[n]
                size = own.shape[ax] // N_DEV
                own = lax.dynamic_slice_in_dim(own, _slot(me[0], pair) * size, size, axis=ax)
                own_axis = None
                if n == "mla_w_down":
                    p, own = p[..., :W[n].shape[2]], own[..., :W[n].shape[2]]
                else:
                    p, own = _unpad_heads(p), _unpad_heads(own)
            prev = (out_g[n], out_d[n], out_m[n], out_v[n]) if n in out_g else None
            (out_g[n], out_d[n], out_m[n], out_v[n]), tok = _adamw(p, own, me, W[n], M1[n], V2[n], l, prev,
                                                                    f"adamw_{n}_{l}", own_axis=own_axis)
            tokens.append(tok)
        return tokens

    def finish_due(after):
        tokens = []
        while len(pending) > lag:
            tokens += scatter_finish(after)
        return tokens

    def mixer_out(o, w_o, xin, l, toks, name):
        tm, tn, tk, _ = _mm_tiles(S, o.shape[1], D, 2, 2, 4, True)
        if tn == D and tk == o.shape[1]:
            return _mm(o, w_o, "nn", F32, name, res=xin, layer=0, deps=toks, norm_gain=ffn2_norm[l:l + 1])
        return _mm(o, w_o, "nn", F32, name, res=xin, layer=0, deps=toks), None

    def mixer_do(dxob, w_o, o, out_dtype, name):
        _, _, tk, _ = _mm_tiles(S, D, o.shape[1], 2, 2, jnp.dtype(out_dtype).itemsize, False)
        if tk == D:
            return _mm(dxob, w_o, "nt", out_dtype, name, layer=0, dot_with=o)
        do = _mm(dxob, w_o, "nt", out_dtype, name, layer=0)
        return do, _attn_delta(do, o, name + "_delta")

    def ffn_fwd(xin, norm_row, which, l, bi, deps=(), h=None):
        tag = blocks[bi][0]
        k_in, k_out = (which + "_w_in", l), (which + "_w_out", l)
        if h is None:
            h = _rms_fwd(xin, norm_row, f"rms_fwd_{tag}", deps=deps)
        if bi == 0:
            relay([k_in], h, f"{tag}_in")
        fetch([k_in], h, f"in_{tag}")
        u, a = _ffn_in(h, full[k_in], f"ffn_in_{tag}")
        if bi == 0:
            relay([k_out], a, f"{tag}_out")
        fetch([k_out], a, f"out_{tag}")
        toks = relay_next(bi, a)
        fused_loss = bi == len(blocks) - 1 and _mm_tiles(S, a.shape[1], D, 2, 2, 4, True)[2] == a.shape[1]
        xo = _mm(a, full[k_out], "nn", F32, f"mm_out_{tag}", scale=0.5, res=xin, layer=0, deps=toks,
                 loss_target=tgt if fused_loss else None)
        return xo, (xin, h, u, a)

    def ffn_bwd(dx_pair, saved, norm_row, which, l, tag, deps=()):
        dxo, dxob = dx_pair
        k_in, k_out = (which + "_w_in", l), (which + "_w_out", l)
        xin, h, u, a = saved
        grads[k_out] = _mm(a, dxob, "tn", BF16, f"mm_dwout_{tag}", scale=0.5, deps=deps)
        t_out = scatter_start(f"{tag}_out", [k_out])
        du = _ffn_da(dxob, full[k_out], u, f"ffn_da_{tag}", deps=[t_out])
        grads[k_in] = _mm(h, du, "tn", BF16, f"mm_dwin_{tag}")
        t_in = scatter_start(f"{tag}_in", [k_in])
        dh = _mm(du, full[k_in], "nt", F32, f"mm_dh_{tag}", layer=0, deps=[t_in])
        toks = finish_due(dh)
        dx, dxb, dg = _rms_bwd(xin, norm_row, dh, dxo, f"rms_bwd_{tag}", deps=toks)
        gain_g.setdefault(which + "_norm", {})[l] = dg
        return dx, dxb

    def mla_fwd(xin, l, bi):
        j = l // 2
        xn = _rms_fwd(xin, mix_norm[l:l + 1], "rms_fwd_mla")
        fetch([(n, j) for n in ("mla_w_down", "mla_w_uq", "mla_w_ukv", "mla_w_o")], xn, "mla")
        lat = _mm(xn, full[("mla_w_down", j)], "nn", F32, "mm_lat", layer=0)
        cq, ckv = _lat_norm_fwd(lat, mla_g_cq[j:j + 1], mla_g_ckv[j:j + 1], "lat_norm_fwd")
        q_raw = _mm(cq, full[("mla_w_uq", j)], "nn", F32, "mm_uq", layer=0)
        kv = _mm(ckv, full[("mla_w_ukv", j)], "nn", F32, "mm_ukv", layer=0)
        qf, kf, vb = _mla_prep_fwd(q_raw, kv, lat, g_qn[j:j + 1], g_kn[j:j + 1], tabs, "mla_prep_fwd")
        o, lse = _flash_fwd(qf, kf, vb, "flash_fwd")
        toks = relay_next(bi, o)
        xo, h_next = mixer_out(o, full[("mla_w_o", j)], xin, l, toks, "mm_mla_o")
        return (xo, h_next), (xin, xn, lat, cq, ckv, q_raw, kv, qf, kf, vb, o, lse)

    def mla_bwd(dx_pair, saved, l):
        dxo, dxob = dx_pair
        j = l // 2
        xin, xn, lat, cq, ckv, q_raw, kv, qf, kf, vb, o, lse = saved
        do, delta = mixer_do(dxob, full[("mla_w_o", j)], o, BF16, "mm_mla_do")
        grads[("mla_w_o", j)] = _mm(o, dxob, "tn", BF16, "mm_mla_dwo")
        dqf, dkf, dv = _flash_bwd(qf, kf, vb, do, lse, delta, "flash_bwd")
        dq_raw, dkv, dkpe, dgq, dgk = _mla_prep_bwd(q_raw, kv, lat, g_qn[j:j + 1], g_kn[j:j + 1], tabs, dqf, dkf, dv,
                                                    "mla_prep_bwd")
        gain_g.setdefault("mla_g_qn", {})[j] = dgq
        gain_g.setdefault("mla_g_kn", {})[j] = dgk
        dcq = _mm(dq_raw, full[("mla_w_uq", j)], "nt", F32, "mm_dcq", layer=0)
        grads[("mla_w_uq", j)] = _mm(cq, dq_raw, "tn", BF16, "mm_dwuq")
        dckv = _mm(dkv, full[("mla_w_ukv", j)], "nt", F32, "mm_dckv", layer=0)
        grads[("mla_w_ukv", j)] = _mm(ckv, dkv, "tn", BF16, "mm_dwukv")
        dlat, dgcq, dgckv = _lat_norm_bwd(lat, mla_g_cq[j:j + 1], mla_g_ckv[j:j + 1], dcq, dckv, dkpe, "lat_norm_bwd")
        gain_g.setdefault("mla_g_cq", {})[j] = dgcq
        gain_g.setdefault("mla_g_ckv", {})[j] = dgckv
        dxn = _mm(dlat, full[("mla_w_down", j)], "nt", F32, "mm_dxn_mla", layer=0)
        grads[("mla_w_down", j)] = _mm(xn, dlat, "tn", BF16, "mm_dwdown")
        tok = scatter_start(f"mix_{l}", [(n, j) for n in ("mla_w_down", "mla_w_uq", "mla_w_ukv", "mla_w_o")])
        toks = finish_due(dxn)
        dx, dxb, dg = _rms_bwd(xin, mix_norm[l:l + 1], dxn, dxo, "rms_bwd_mla", deps=[tok] + toks)
        gain_g.setdefault("mix_norm", {})[l] = dg
        return dx, dxb

    def dil_fwd(xin, l, bi):
        j = l // 2
        xn = _rms_fwd(xin, mix_norm[l:l + 1], "rms_fwd_dil")
        fetch([("dil_w_qkv", j), ("dil_w_o", j)], xn, "dil")
        qkv = _mm(xn, full[("dil_w_qkv", j)], "nn", F32, "mm_qkv", layer=0)
        o_g, lse_g = _dil_fwd(qkv, dil_g_qn[j:j + 1], dil_g_kn[j:j + 1], slopes, "dil_fwd")
        o, lse = _dil_merge(o_g, lse_g, "dil_merge")
        toks = relay_next(bi, o)
        xo, h_next = mixer_out(o, full[("dil_w_o", j)], xin, l, toks, "mm_dil_o")
        return (xo, h_next), (xin, xn, qkv, o, lse)

    def dil_bwd(dx_pair, saved, l):
        dxo, dxob = dx_pair
        j = l // 2
        xin, xn, qkv, o, lse = saved
        do, delta = mixer_do(dxob, full[("dil_w_o", j)], o, F32, "mm_dil_do")
        grads[("dil_w_o", j)] = _mm(o, dxob, "tn", BF16, "mm_dil_dwo")
        dq, dk, dv, dgq, dgk = _dil_bwd(qkv, dil_g_qn[j:j + 1], dil_g_kn[j:j + 1], slopes, do, delta, lse, "dil_bwd")
        gain_g.setdefault("dil_g_qn", {})[j] = dgq
        gain_g.setdefault("dil_g_kn", {})[j] = dgk
        dqkv = jnp.concatenate([dq, dk, dv], axis=1)
        dxn = _mm(dqkv, full[("dil_w_qkv", j)], "nt", F32, "mm_dxn_dil", layer=0)
        grads[("dil_w_qkv", j)] = _mm(xn, dqkv, "tn", BF16, "mm_dwqkv")
        tok = scatter_start(f"mix_{l}", [("dil_w_qkv", j), ("dil_w_o", j)])
        toks = finish_due(dxn)
        dx, dxb, dg = _rms_bwd(xin, mix_norm[l:l + 1], dxn, dxo, "rms_bwd_dil", deps=[tok] + toks)
        gain_g.setdefault("mix_norm", {})[l] = dg
        return dx, dxb

    saved = []
    xc = x0
    for l in range(depth):
        xc, s1 = ffn_fwd(xc, ffn1_norm[l:l + 1], "ffn1", l, 3 * l, deps=[ag_token] if l == 0 else ())
        (xc, h_next), s2 = (mla_fwd if l % 2 == 0 else dil_fwd)(xc, l, 3 * l + 1)
        xc, s3 = ffn_fwd(xc, ffn2_norm[l:l + 1], "ffn2", l, 3 * l + 2, h=h_next)
        saved.append((s1, s2, s3))

    dy, dyb, loss_part = xc if isinstance(xc, tuple) else _loss_head(xc, tgt, "loss_head")
    dx = (dy, dyb)
    loss = lax.psum(loss_part[0, 0], MESH_AXES)

    for bi in reversed(range(len(blocks))):
        tag, _ = blocks[bi]
        l = bi // 3
        s = saved[l][bi % 3]
        if bi % 3 == 2:
            dx = ffn_bwd(dx, s, ffn2_norm[l:l + 1], "ffn2", l, tag,
                         deps=[loss.reshape(1, 1)] if bi == len(blocks) - 1 else ())
        elif bi % 3 == 1:
            dx = (mla_bwd if l % 2 == 0 else dil_bwd)(dx, s, l)
        else:
            dx = ffn_bwd(dx, s, ffn1_norm[l:l + 1], "ffn1", l, tag)
    grad_x = dx[0].reshape(x.shape)
    after = dx[1]
    while pending:
        after = scatter_finish(after)[-1]

    small = [n for n in names if n not in big]

    def gain_local(n):
        rows = [gain_g[n][l] for l in range(W[n].shape[0])]
        g = jnp.concatenate(rows, axis=1)
        return g

    def flat_pad(n, a):
        a = a.reshape(1, -1)
        if n in ("mla_g_qn", "mla_g_kn"):
            a = _pad_heads(a)
        return a

    packed_g = jnp.concatenate([gain_local(n) for n in small], axis=1)
    sizes = [gain_local(n).shape[1] for n in small]
    tot_g = _gain_allreduce(packed_g, "gain_allreduce")
    pw = jnp.concatenate([flat_pad(n, W[n]) for n in small], axis=1)
    pm = jnp.concatenate([flat_pad(n, M1[n]) for n in small], axis=1)
    pv = jnp.concatenate([flat_pad(n, V2[n]) for n in small], axis=1)
    res, _ = _adamw(tot_g.reshape(1, 1, -1), None, me, pw.reshape(1, 1, -1), pm.reshape(1, 1, -1),
                    pv.reshape(1, 1, -1), 0, None, "adamw_gains")
    res = [r.reshape(1, -1) for r in res]
    off = 0
    for n, sz in zip(small, sizes):
        for dst, r in zip((out_g, out_d, out_m, out_v), res):
            piece = r[:, off:off + sz]
            if n in ("mla_g_qn", "mla_g_kn"):
                piece = _unpad_heads(piece)
            dst[n] = piece.reshape(W[n].shape)
        off += sz

    return (loss, grad_x, *[out_g[n] for n in names], *[out_d[n] for n in names],
            *[out_m[n] for n in names], *[out_v[n] for n in names])
```

```python
import functools
import math

import jax
import jax.numpy as jnp
import numpy as np
from jax import lax
from jax.experimental import pallas as pl
from jax.experimental.pallas import tpu as pltpu

EPS = 1e-6
MLA_HEADS = 16
Q_LORA = 512
KV_LORA = 512
NOPE_DIM = 128
ROPE_DIM = 64
V_DIM = 128
QK_DIM = NOPE_DIM + ROPE_DIM
ROPE_THETA = 10000.0
HEAD_PAD = 256
LAT_PAD = Q_LORA + KV_LORA + 128
DIL_PAIRS = ((128, 1), (512, 4), (2048, 16))
DIL_GROUPS = 3
DIL_HEADS = 8
DIL_HEAD_DIM = 128
DIL_BLK = 128
FLASH_HEADS = 4
FLASH_HEADS_FWD = 16
LOG2E = math.log2(math.e)
LN2 = math.log(2.0)
ADAM_LR = 0.001
ADAM_B1 = 0.9
ADAM_B2 = 0.999
ADAM_EPS = 1e-08
ADAM_WD = 0.01
ADAM_STEP = 10

N_DEV = 8
MESH_AXES = ("x", "y", "c")
MESH = pl.DeviceIdType.MESH
NEG_BIG = -1e30
VMEM_LIMIT_V7X = 56 * 1024 * 1024
LANES = 128

BF16 = jnp.bfloat16
F32 = jnp.float32


def _pick(n, cands):
    for c in cands:
        if n % c == 0:
            return c
    raise ValueError(f"no tile for {n}")


def _params(sem):
    return pltpu.CompilerParams(dimension_semantics=sem, vmem_limit_bytes=VMEM_LIMIT_V7X)


ANY_SPEC = pl.BlockSpec(memory_space=pl.ANY)


MM_VMEM_BUDGET = 44 * 1024 * 1024
MM_HBM_BYTES_PER_S = 1.8e12
MM_MXU_FLOPS_PER_S = 8.5e14
MM_STEP_S = 0.4e-6
MM_MAX_TILE_MACS = 3.3e9
MXU_DIM = 256


MM_TIMED_TILES = {
    (2048, 4096, 11264, 2, 2, 2, False): (1024, 1024, 4096, True),
    (4096, 11264, 2048, 2, 2, 4, False): (1024, 1024, 2816, True),
    (4096, 5632, 2048, 2, 2, 4, True): (512, 1024, 5632, False),
    (2048, 4096, 9216, 2, 2, 2, False): (1024, 1024, 4096, True),
    (4096, 9216, 2048, 2, 2, 4, False): (1024, 1024, 2304, True),
}


@functools.lru_cache(maxsize=None)
def _mm_tiles(M, K, N, a_bytes, b_bytes, out_bytes, has_res):
    if (M, K, N, a_bytes, b_bytes, out_bytes, has_res) in MM_TIMED_TILES:
        return MM_TIMED_TILES[(M, K, N, a_bytes, b_bytes, out_bytes, has_res)]
    best = None
    for tk in [K] + [c for c in (1408, 1024, 512, 384, 256, 128) if K % c == 0 and c < K]:
        nk = K // tk
        for tm in [c for c in (2048, 1024, 512, 256, 128) if M % c == 0]:
            for tn in [c for c in (2816, 2048, 1408, 1152, 1024, 512, 384, 256, 128) if N % c == 0]:
                if tm * tk * tn > MM_MAX_TILE_MACS:
                    continue
                fill = (tn / (-(-tn // MXU_DIM) * MXU_DIM)) * (tk / (-(-tk // MXU_DIM) * MXU_DIM))
                fill *= tm / (tm + MXU_DIM // 2)
                vmem = 2 * (tm * tk * a_bytes + tk * tn * b_bytes) + 2 * tm * tn * out_bytes + tm * tn * 4
                vmem += (tm * tk + tk * tn) * 2 if max(a_bytes, b_bytes) > 2 else 0
                vmem += 2 * tm * tn * 4 if has_res else 0
                if vmem > MM_VMEM_BUDGET:
                    continue
                a_all, b_all = M * K * a_bytes, K * N * b_bytes
                if nk == 1:
                    t_i = a_all + (M // tm) * b_all
                    t_j = b_all + (N // tn) * a_all
                    traffic, i_outer = min((t_i, True), (t_j, False))
                else:
                    traffic, i_outer = (N // tn) * a_all + (M // tm) * b_all, True
                traffic += M * N * (out_bytes + (4 if has_res else 0))
                mxu = 2.0 * M * K * N / (MM_MXU_FLOPS_PER_S * fill) * (1.15 if nk > 1 else 1.0)
                cost = max(traffic / MM_HBM_BYTES_PER_S, mxu) + (M // tm) * (N // tn) * nk * MM_STEP_S
                if best is None or cost < best[0]:
                    best = (cost, tm, tn, tk, i_outer)
    assert best is not None, (M, K, N)
    return best[1:]


def _mm(a, b, mode, out_dtype, name, *, scale=1.0, res=None, layer=None, deps=(), norm_gain=None, dot_with=None,
        loss_target=None):
    b2 = b.shape[-2:]
    if mode == "nn":
        (M, K), (Kb, N) = a.shape, b2
    elif mode == "nt":
        (M, K), (N, Kb) = a.shape, b2
    else:
        (K, M), (Kb, N) = a.shape, b2
    assert K == Kb, (a.shape, b.shape, mode)
    tm, tn, tk, i_outer = _mm_tiles(M, K, N, a.dtype.itemsize, b.dtype.itemsize, jnp.dtype(out_dtype).itemsize,
                                    res is not None)
    nk = K // tk
    dims = {"nn": (((1,), (0,)), ((), ())), "nt": (((1,), (1,)), ((), ())), "tn": (((0,), (0,)), ((), ()))}[mode]
    normed = norm_gain is not None
    dotted = dot_with is not None
    assert not normed or (nk == 1 and tn == N), (name, tn, N, nk)
    assert not dotted or (nk == 1 and not normed and tn % LANES == 0), (name, tn, nk)
    lossy = loss_target is not None
    assert not lossy or (nk == 1 and res is not None and not normed and not dotted), name
    n_in = 2 + (res is not None) + normed + dotted + lossy + len(deps)

    def finish(v, r_ref, o_ref):
        if scale != 1.0:
            v = v * scale
        if r_ref is not None:
            v = r_ref[...] + v
        o_ref[...] = v.astype(o_ref.dtype)
        return v

    def body(*refs):
        a_ref, b_ref = refs[:2]
        r_ref = refs[2] if res is not None else None
        prod = lambda: lax.dot_general(a_ref[...].astype(BF16), b_ref[...].astype(BF16), dims,
                                       preferred_element_type=F32)
        if lossy:
            t_ref, (dy_ref, dyb_ref, l_ref) = refs[3], refs[n_in:n_in + 3]
            e = r_ref[...] + prod() * scale - t_ref[...]
            dy = e * (1.0 / N)
            dy_ref[...] = dy
            dyb_ref[...] = dy.astype(BF16)

            @pl.when((pl.program_id(0) == 0) & (pl.program_id(1) == 0))
            def _():
                l_ref[...] = jnp.zeros_like(l_ref)

            l_ref[...] += (0.5 / N) * jnp.sum(jnp.sum(e * e, axis=-1, keepdims=True), axis=0, keepdims=True)
            return
        if nk == 1:
            v = finish(prod(), r_ref, refs[n_in])
            if normed:
                g_ref = refs[2 + (res is not None)]
                r = lax.rsqrt(jnp.mean(v * v, axis=-1, keepdims=True) + EPS)
                refs[n_in + 1][...] = ((v * r) * g_ref[...]).astype(BF16)
            if dotted:
                w_ref, d_ref = refs[2 + (res is not None)], refs[n_in + 1]
                for h in range(tn // LANES):
                    sl = slice(h * LANES, (h + 1) * LANES)
                    d = jnp.sum(v[:, sl] * w_ref[:, sl].astype(F32), axis=-1, keepdims=True)
                    d_ref[:, sl] = jnp.broadcast_to(d, (tm, LANES))
            return
        o_ref, acc = refs[-2:]
        k = pl.program_id(2)

        @pl.when(k == 0)
        def _():
            acc[...] = prod()

        @pl.when(k > 0)
        def _():
            acc[...] += prod()

        @pl.when(k == nk - 1)
        def _():
            finish(acc[...], r_ref, o_ref)

    ij = (lambda p, q: (p, q)) if i_outer else (lambda p, q: (q, p))

    def spec(shape, f, lead=None):
        full = lambda p, q, k: f(*ij(p, q), k)
        if lead is None:
            return pl.BlockSpec(shape, full)
        return pl.BlockSpec((None,) + shape, lambda p, q, k: (lead,) + full(p, q, k))

    a_spec = spec((tk, tm), lambda i, j, k: (k, i)) if mode == "tn" else spec((tm, tk), lambda i, j, k: (i, k))
    lead = layer if b.ndim == 3 else None
    b_spec = spec((tn, tk), lambda i, j, k: (j, k), lead) if mode == "nt" else spec((tk, tn), lambda i, j, k: (k, j), lead)
    in_specs = [a_spec, b_spec]
    args = [a, b]
    if res is not None:
        in_specs.append(spec((tm, tn), lambda i, j, k: (i, j)))
        args.append(res)
    tile = spec((tm, tn), lambda i, j, k: (i, j))
    if normed:
        in_specs.append(pl.BlockSpec((1, N), lambda p, q, k: (0, 0)))
        args.append(norm_gain)
    if dotted:
        in_specs.append(tile)
        args.append(dot_with)
    if lossy:
        in_specs.append(tile)
        args.append(loss_target)
    in_specs += [ANY_SPEC] * len(deps)
    args += list(deps)
    outer, inner = (M // tm, N // tn) if i_outer else (N // tn, M // tm)
    if lossy:
        return tuple(pl.pallas_call(
            body, name=name, grid=(outer, inner, nk), in_specs=in_specs,
            out_specs=[tile, tile, pl.BlockSpec((1, 1), lambda p, q, k: (0, 0))],
            out_shape=[jax.ShapeDtypeStruct((M, N), F32), jax.ShapeDtypeStruct((M, N), BF16),
                       jax.ShapeDtypeStruct((1, 1), F32)],
            compiler_params=_params(("arbitrary", "arbitrary", "arbitrary")),
        )(*args))
    second = [jax.ShapeDtypeStruct((M, N), BF16)] if normed else [jax.ShapeDtypeStruct((M, N), F32)] if dotted else []
    out = pl.pallas_call(
        body, name=name, grid=(outer, inner, nk),
        in_specs=in_specs, out_specs=[tile, tile] if second else tile,
        out_shape=[jax.ShapeDtypeStruct((M, N), out_dtype)] + second if second else jax.ShapeDtypeStruct((M, N), out_dtype),
        scratch_shapes=[pltpu.VMEM((tm, tn), F32)] if nk > 1 else [],
        compiler_params=_params(("parallel", "parallel", "arbitrary")),
    )(*args)
    return tuple(out) if second else out


def _cast_into_gathered(w, layer, axis, me, name, deps=()):
    _, R, C = w.shape
    tr = _pick(R, (512, 256, 128, 64, 32, 16))
    nr = R // tr
    axis, paired = axis

    def body(me_ref, w_ref, *rest):
        o_ref = rest[-1]
        o_ref[...] = w_ref[...].astype(BF16)

    if axis == 1:
        out_idx = lambda i, me_ref: (0, _slot(me_ref[0], paired) * nr + i, 0)
        shape = (1, R * N_DEV, C)
    else:
        out_idx = lambda i, me_ref: (0, i, _slot(me_ref[0], paired))
        shape = (1, R, C * N_DEV)
    return pl.pallas_call(
        body, name=name,
        grid_spec=pltpu.PrefetchScalarGridSpec(
            num_scalar_prefetch=1, grid=(nr,),
            in_specs=[pl.BlockSpec((None, tr, C), lambda i, me_ref: (layer, i, 0))] + [ANY_SPEC] * len(deps),
            out_specs=pl.BlockSpec((None, tr, C), out_idx)),
        out_shape=jax.ShapeDtypeStruct(shape, BF16), compiler_params=_params(("parallel",)),
    )(me, w, *deps)


def _rms_fwd(x, g, name, deps=()):
    T, D = x.shape
    tr = _pick(T, (512, 256, 128))

    def body(x_ref, g_ref, *rest):
        o_ref = rest[-1]
        xv = x_ref[...]
        r = lax.rsqrt(jnp.mean(xv * xv, axis=-1, keepdims=True) + EPS)
        o_ref[...] = ((xv * r) * g_ref[...]).astype(BF16)

    return pl.pallas_call(
        body, name=name, grid=(T // tr,),
        in_specs=[pl.BlockSpec((tr, D), lambda i: (i, 0)), pl.BlockSpec((1, D), lambda i: (0, 0))]
        + [ANY_SPEC] * len(deps),
        out_specs=pl.BlockSpec((tr, D), lambda i: (i, 0)),
        out_shape=jax.ShapeDtypeStruct((T, D), BF16), compiler_params=_params(("parallel",)),
    )(x, g, *deps)


def _rms_bwd(x, g, dh, dres, name, deps=()):
    T, D = x.shape
    tr = _pick(T, (256, 128))

    def body(x_ref, g_ref, dh_ref, dres_ref, *rest):
        dx_ref, dxb_ref, dg_ref = rest[-3:]
        xv = x_ref[...]
        dhv = dh_ref[...]
        r = lax.rsqrt(jnp.mean(xv * xv, axis=-1, keepdims=True) + EPS)
        xhat = xv * r
        dxh = dhv * g_ref[...]
        c = jnp.mean(dxh * xhat, axis=-1, keepdims=True)
        dx = dres_ref[...] + r * (dxh - xhat * c)
        dx_ref[...] = dx
        dxb_ref[...] = dx.astype(BF16)

        @pl.when(pl.program_id(0) == 0)
        def _():
            dg_ref[...] = jnp.zeros_like(dg_ref)

        dg_ref[...] += jnp.sum(dhv * xhat, axis=0, keepdims=True)

    row = pl.BlockSpec((tr, D), lambda i: (i, 0))
    vec = pl.BlockSpec((1, D), lambda i: (0, 0))
    return pl.pallas_call(
        body, name=name, grid=(T // tr,),
        in_specs=[row, vec, row, row] + [ANY_SPEC] * len(deps), out_specs=[row, row, vec],
        out_shape=[jax.ShapeDtypeStruct((T, D), F32), jax.ShapeDtypeStruct((T, D), BF16),
                   jax.ShapeDtypeStruct((1, D), F32)],
        compiler_params=_params(("arbitrary",)),
    )(x, g, dh, dres, *deps)


N_PANEL = N_DEV // 2


def _ffn_in(h, w_in, name):
    T, D = h.shape
    F2 = w_in.shape[2]
    pw = F2 // N_PANEL
    half = pw // 2
    tm = _pick(T, (512, 256, 128))

    def body(h_ref, w_ref, u_ref, a_ref):
        r = jnp.dot(h_ref[...], w_ref[...], preferred_element_type=F32)
        u_ref[...] = r.astype(BF16)
        g, up = r[:, :half], r[:, half:]
        a_ref[...] = (g * jax.nn.sigmoid(g) * up).astype(BF16)

    return pl.pallas_call(
        body, name=name, grid=(N_PANEL, T // tm),
        in_specs=[pl.BlockSpec((tm, D), lambda p, i: (i, 0)), pl.BlockSpec((None, D, pw), lambda p, i: (0, 0, p))],
        out_specs=[pl.BlockSpec((tm, pw), lambda p, i: (i, p)), pl.BlockSpec((tm, half), lambda p, i: (i, p))],
        out_shape=[jax.ShapeDtypeStruct((T, F2), BF16), jax.ShapeDtypeStruct((T, F2 // 2), BF16)],
        compiler_params=_params(("parallel", "parallel")),
    )(h, w_in)


def _ffn_da(dxo, w_out, u, name, deps=()):
    T, D = dxo.shape
    F2 = u.shape[1]
    pw = F2 // N_PANEL
    half = pw // 2
    tm = _pick(T, (512, 256, 128))

    def body(d_ref, w_ref, u_ref, *rest):
        du_ref = rest[-1]
        da = 0.5 * lax.dot_general(d_ref[...], w_ref[...], NT_DIMS, preferred_element_type=F32)
        g = u_ref[:, :half].astype(F32)
        up = u_ref[:, half:].astype(F32)
        sg = jax.nn.sigmoid(g)
        silu = g * sg
        du_ref[:, :half] = (da * up * (sg + silu * (1.0 - sg))).astype(BF16)
        du_ref[:, half:] = (da * silu).astype(BF16)

    return pl.pallas_call(
        body, name=name, grid=(N_PANEL, T // tm),
        in_specs=[pl.BlockSpec((tm, D), lambda p, i: (i, 0)), pl.BlockSpec((None, half, D), lambda p, i: (0, p, 0)),
                  pl.BlockSpec((tm, pw), lambda p, i: (i, p))] + [ANY_SPEC] * len(deps),
        out_specs=pl.BlockSpec((tm, pw), lambda p, i: (i, p)),
        out_shape=jax.ShapeDtypeStruct((T, F2), BF16), compiler_params=_params(("parallel", "parallel")),
    )(dxo, w_out, u, *deps)


def _loss_head(y, t, name):
    T, D = y.shape
    tr = _pick(T, (512, 256, 128))

    def body(y_ref, t_ref, dy_ref, dyb_ref, l_ref):
        e = y_ref[...] - t_ref[...]
        dy = e * (1.0 / D)
        dy_ref[...] = dy
        dyb_ref[...] = dy.astype(BF16)

        @pl.when(pl.program_id(0) == 0)
        def _():
            l_ref[...] = jnp.zeros_like(l_ref)

        l_ref[...] += 0.5 * jnp.sum(jnp.mean(e * e, axis=-1, keepdims=True), axis=0, keepdims=True)

    row = pl.BlockSpec((tr, D), lambda i: (i, 0))
    return pl.pallas_call(
        body, name=name, grid=(T // tr,),
        in_specs=[row, row], out_specs=[row, row, pl.BlockSpec((1, 1), lambda i: (0, 0))],
        out_shape=[jax.ShapeDtypeStruct((T, D), F32), jax.ShapeDtypeStruct((T, D), BF16),
                   jax.ShapeDtypeStruct((1, 1), F32)],
        compiler_params=_params(("arbitrary",)),
    )(y, t)


def _rope_tables(S):
    half = ROPE_DIM // 2
    inv = 1.0 / (ROPE_THETA ** (jnp.arange(0, ROPE_DIM, 2, dtype=F32) / ROPE_DIM))
    ang = jnp.arange(S, dtype=F32)[:, None] * inv[None, :]
    cos, sin = jnp.cos(ang), jnp.sin(ang)
    z = jnp.zeros((S, half), F32)
    z2 = jnp.zeros((S, LANES - ROPE_DIM), F32)
    c = jnp.concatenate([cos, cos, z2], axis=1)
    s1 = jnp.concatenate([-sin, z, z2], axis=1)
    s2 = jnp.concatenate([z, sin, z2], axis=1)
    return c, s1, s2


def _rope(r, c, s1, s2):
    return r * c + pltpu.roll(r, LANES - ROPE_DIM // 2, 1) * s1 + pltpu.roll(r, ROPE_DIM // 2, 1) * s2


def _rope_t(d, c, s1, s2):
    return d * c + pltpu.roll(d * s1, ROPE_DIM // 2, 1) + pltpu.roll(d * s2, LANES - ROPE_DIM // 2, 1)


def _lat_norm_fwd(lat, g_cq, g_ckv, name):
    T = lat.shape[0]
    tr = _pick(T, (512, 256, 128))

    def body(lat_ref, gq_ref, gk_ref, cq_ref, ckv_ref):
        for off, g_ref, o_ref in ((0, gq_ref, cq_ref), (Q_LORA, gk_ref, ckv_ref)):
            xv = lat_ref[:, off:off + Q_LORA]
            r = lax.rsqrt(jnp.mean(xv * xv, axis=-1, keepdims=True) + EPS)
            o_ref[...] = ((xv * r) * g_ref[...]).astype(BF16)

    vec = pl.BlockSpec((1, Q_LORA), lambda i: (0, 0))
    out = pl.BlockSpec((tr, Q_LORA), lambda i: (i, 0))
    return pl.pallas_call(
        body, name=name, grid=(T // tr,),
        in_specs=[pl.BlockSpec((tr, LAT_PAD), lambda i: (i, 0)), vec, vec], out_specs=[out, out],
        out_shape=[jax.ShapeDtypeStruct((T, Q_LORA), BF16)] * 2, compiler_params=_params(("parallel",)),
    )(lat, g_cq, g_ckv)


def _lat_norm_bwd(lat, g_cq, g_ckv, dcq, dckv, dkpe, name):
    T = lat.shape[0]
    tr = _pick(T, (256, 128))

    def body(lat_ref, gq_ref, gk_ref, dcq_ref, dckv_ref, dkpe_ref, dlat_ref, dgq_ref, dgk_ref):
        @pl.when(pl.program_id(0) == 0)
        def _():
            dgq_ref[...] = jnp.zeros_like(dgq_ref)
            dgk_ref[...] = jnp.zeros_like(dgk_ref)

        for off, g_ref, d_ref, dg_ref in ((0, gq_ref, dcq_ref, dgq_ref), (Q_LORA, gk_ref, dckv_ref, dgk_ref)):
            xv = lat_ref[:, off:off + Q_LORA]
            dv = d_ref[...]
            r = lax.rsqrt(jnp.mean(xv * xv, axis=-1, keepdims=True) + EPS)
            xhat = xv * r
            dxh = dv * g_ref[...]
            c = jnp.mean(dxh * xhat, axis=-1, keepdims=True)
            dlat_ref[:, off:off + Q_LORA] = (r * (dxh - xhat * c)).astype(BF16)
            dg_ref[...] += jnp.sum(dv * xhat, axis=0, keepdims=True)
        dlat_ref[:, Q_LORA + KV_LORA:] = dkpe_ref[...].astype(BF16)

    vec = pl.BlockSpec((1, Q_LORA), lambda i: (0, 0))
    half = pl.BlockSpec((tr, Q_LORA), lambda i: (i, 0))
    full = pl.BlockSpec((tr, LAT_PAD), lambda i: (i, 0))
    return pl.pallas_call(
        body, name=name, grid=(T // tr,),
        in_specs=[full, vec, vec, half, half, pl.BlockSpec((tr, LANES), lambda i: (i, 0))],
        out_specs=[full, vec, vec],
        out_shape=[jax.ShapeDtypeStruct((T, LAT_PAD), BF16), jax.ShapeDtypeStruct((1, Q_LORA), F32),
                   jax.ShapeDtypeStruct((1, Q_LORA), F32)],
        compiler_params=_params(("arbitrary",)),
    )(lat, g_cq, g_ckv, dcq, dckv, dkpe)


def _mla_prep_fwd(q_raw, kv, lat, g_qn, g_kn, tabs, name):
    T = q_raw.shape[0]
    H = MLA_HEADS
    tr = _pick(T, (256, 128))
    scale = LOG2E / math.sqrt(QK_DIM)

    def body(q_ref, kv_ref, kpe_ref, gq_ref, gk_ref, c_ref, s1_ref, s2_ref, qf_ref, kf_ref, v_ref):
        c, s1, s2 = c_ref[...], s1_ref[...], s2_ref[...]
        gq, gk = gq_ref[...], gk_ref[...]
        kpe = kpe_ref[...]
        kpe_ss = jnp.sum(kpe * kpe, axis=-1, keepdims=True)
        for h in range(H):
            lo = h * HEAD_PAD
            qa = q_ref[:, lo:lo + LANES]
            qb = q_ref[:, lo + LANES:lo + HEAD_PAD]
            ss = jnp.sum(qa * qa + qb * qb, axis=-1, keepdims=True)
            r = lax.rsqrt(ss * (1.0 / QK_DIM) + EPS)
            qf_ref[:, lo:lo + LANES] = (qa * r * gq[:, :LANES] * scale).astype(BF16)
            qf_ref[:, lo + LANES:lo + HEAD_PAD] = (_rope(qb * r * gq[:, LANES:], c, s1, s2) * scale).astype(BF16)
            ka = kv_ref[:, lo:lo + LANES]
            ss = jnp.sum(ka * ka, axis=-1, keepdims=True) + kpe_ss
            r = lax.rsqrt(ss * (1.0 / QK_DIM) + EPS)
            kf_ref[:, lo:lo + LANES] = (ka * r * gk[:, :LANES]).astype(BF16)
            kf_ref[:, lo + LANES:lo + HEAD_PAD] = _rope(kpe * r * gk[:, LANES:], c, s1, s2).astype(BF16)
            v_ref[:, lo:lo + V_DIM] = kv_ref[:, lo + LANES:lo + HEAD_PAD].astype(BF16)
            v_ref[:, lo + V_DIM:lo + HEAD_PAD] = jnp.ones((tr, HEAD_PAD - V_DIM), BF16)

    wide = pl.BlockSpec((tr, H * HEAD_PAD), lambda i: (i, 0))
    lane = pl.BlockSpec((tr, LANES), lambda i: (i, 0))
    gvec = pl.BlockSpec((1, HEAD_PAD), lambda i: (0, 0))
    return pl.pallas_call(
        body, name=name, grid=(T // tr,),
        in_specs=[wide, wide, pl.BlockSpec((tr, LANES), lambda i: (i, (Q_LORA + KV_LORA) // LANES)), gvec, gvec,
                  lane, lane, lane],
        out_specs=[wide, wide, wide],
        out_shape=[jax.ShapeDtypeStruct((T, H * HEAD_PAD), BF16)] * 3,
        compiler_params=_params(("parallel",)),
    )(q_raw, kv, lat, g_qn, g_kn, *tabs)


def _mla_prep_bwd(q_raw, kv, lat, g_qn, g_kn, tabs, dqf, dkf, dv, name):
    T = q_raw.shape[0]
    H = MLA_HEADS
    tr = _pick(T, (128,))

    def body(q_ref, kv_ref, kpe_ref, gq_ref, gk_ref, c_ref, s1_ref, s2_ref, dqf_ref, dkf_ref, dv_ref,
             dq_ref, dkv_ref, dkpe_ref, dgq_ref, dgk_ref):
        @pl.when(pl.program_id(0) == 0)
        def _():
            dgq_ref[...] = jnp.zeros_like(dgq_ref)
            dgk_ref[...] = jnp.zeros_like(dgk_ref)

        c, s1, s2 = c_ref[...], s1_ref[...], s2_ref[...]
        gq, gk = gq_ref[...], gk_ref[...]
        kpe = kpe_ref[...]
        kpe_ss = jnp.sum(kpe * kpe, axis=-1, keepdims=True)
        dkpe = jnp.zeros_like(kpe)
        dgq_a = jnp.zeros((1, LANES), F32)
        dgq_b = jnp.zeros((1, LANES), F32)
        dgk_a = jnp.zeros((1, LANES), F32)
        dgk_b = jnp.zeros((1, LANES), F32)
        for h in range(H):
            lo = h * HEAD_PAD
            xa = q_ref[:, lo:lo + LANES]
            xb = q_ref[:, lo + LANES:lo + HEAD_PAD]
            ss = jnp.sum(xa * xa + xb * xb, axis=-1, keepdims=True)
            r = lax.rsqrt(ss * (1.0 / QK_DIM) + EPS)
            xa, xb = xa * r, xb * r
            da = dqf_ref[:, lo:lo + LANES].astype(F32)
            db = _rope_t(dqf_ref[:, lo + LANES:lo + HEAD_PAD].astype(F32), c, s1, s2)
            dgq_a += jnp.sum(da * xa, axis=0, keepdims=True)
            dgq_b += jnp.sum(db * xb, axis=0, keepdims=True)
            da, db = da * gq[:, :LANES], db * gq[:, LANES:]
            cc = jnp.sum(da * xa + db * xb, axis=-1, keepdims=True) * (1.0 / QK_DIM)
            dq_ref[:, lo:lo + LANES] = (r * (da - xa * cc)).astype(BF16)
            dq_ref[:, lo + LANES:lo + HEAD_PAD] = (r * (db - xb * cc)).astype(BF16)
            xa = kv_ref[:, lo:lo + LANES]
            ss = jnp.sum(xa * xa, axis=-1, keepdims=True) + kpe_ss
            r = lax.rsqrt(ss * (1.0 / QK_DIM) + EPS)
            xa, xb = xa * r, kpe * r
            da = dkf_ref[:, lo:lo + LANES].astype(F32)
            db = _rope_t(dkf_ref[:, lo + LANES:lo + HEAD_PAD].astype(F32), c, s1, s2)
            dgk_a += jnp.sum(da * xa, axis=0, keepdims=True)
            dgk_b += jnp.sum(db * xb, axis=0, keepdims=True)
            da, db = da * gk[:, :LANES], db * gk[:, LANES:]
            cc = jnp.sum(da * xa + db * xb, axis=-1, keepdims=True) * (1.0 / QK_DIM)
            dkv_ref[:, lo:lo + LANES] = (r * (da - xa * cc)).astype(BF16)
            dkpe = dkpe + r * (db - xb * cc)
            dkv_ref[:, lo + LANES:lo + HEAD_PAD] = dv_ref[:, h * V_DIM:(h + 1) * V_DIM].astype(BF16)
        dkpe_ref[...] = dkpe
        dgq_ref[:, :LANES] += dgq_a
        dgq_ref[:, LANES:] += dgq_b
        dgk_ref[:, :LANES] += dgk_a
        dgk_ref[:, LANES:] += dgk_b

    wide = pl.BlockSpec((tr, H * HEAD_PAD), lambda i: (i, 0))
    lane = pl.BlockSpec((tr, LANES), lambda i: (i, 0))
    gvec = pl.BlockSpec((1, HEAD_PAD), lambda i: (0, 0))
    vspec = pl.BlockSpec((tr, H * V_DIM), lambda i: (i, 0))
    return pl.pallas_call(
        body, name=name, grid=(T // tr,),
        in_specs=[wide, wide, pl.BlockSpec((tr, LANES), lambda i: (i, (Q_LORA + KV_LORA) // LANES)), gvec, gvec,
                  lane, lane, lane, wide, wide, vspec],
        out_specs=[wide, wide, lane, gvec, gvec],
        out_shape=[jax.ShapeDtypeStruct((T, H * HEAD_PAD), BF16), jax.ShapeDtypeStruct((T, H * HEAD_PAD), BF16),
                   jax.ShapeDtypeStruct((T, LANES), F32), jax.ShapeDtypeStruct((1, HEAD_PAD), F32),
                   jax.ShapeDtypeStruct((1, HEAD_PAD), F32)],
        compiler_params=_params(("arbitrary",)),
    )(q_raw, kv, lat, g_qn, g_kn, *tabs, dqf, dkf, dv)


def _causal_mask(tq, tk):
    return lax.broadcasted_iota(jnp.int32, (tq, tk), 1) <= lax.broadcasted_iota(jnp.int32, (tq, tk), 0)


NT_DIMS = (((1,), (1,)), ((), ()))
TN_DIMS = (((0,), (0,)), ((), ()))


def _flash_fwd(qf, kf, v, name):
    T = qf.shape[0]
    H, G = MLA_HEADS, FLASH_HEADS_FWD
    t = _pick(T, (512, 256, 128))
    n = T // t
    pairs = [(i, j) for i in range(n) for j in range(i + 1)]
    qi = jnp.asarray([p[0] for p in pairs], jnp.int32)
    kj = jnp.asarray([p[1] for p in pairs], jnp.int32)

    def body(qi_ref, kj_ref, q_ref, k_ref, v_ref, o_ref, lse_ref, *scratch):
        m_sc, acc_sc = scratch[:G], scratch[G:]
        sid = pl.program_id(1)
        i, j = qi_ref[sid], kj_ref[sid]

        @pl.when(j == 0)
        def _():
            for g in range(G):
                m_sc[g][...] = jnp.full_like(m_sc[g], NEG_BIG)
                acc_sc[g][...] = jnp.zeros_like(acc_sc[g])

        def step(masked):
            for g in range(G):
                qk = slice(g * HEAD_PAD, (g + 1) * HEAD_PAD)
                s = lax.dot_general(q_ref[:, qk], k_ref[:, qk], NT_DIMS, preferred_element_type=F32)
                if masked:
                    s = jnp.where(_causal_mask(t, t), s, NEG_BIG)
                m_prev = m_sc[g][:, :1]
                m_new = jnp.maximum(m_prev, jnp.max(s, axis=-1, keepdims=True))
                a = jnp.exp2(m_prev - m_new)
                p = jnp.exp2((s - m_new).astype(BF16))
                acc_sc[g][...] = a * acc_sc[g][...] + jnp.dot(p, v_ref[:, qk], preferred_element_type=F32)
                m_sc[g][...] = jnp.broadcast_to(m_new, (t, LANES))

        @pl.when(j < i)
        def _():
            step(False)

        @pl.when(j == i)
        def _():
            step(True)
            for g in range(G):
                vo = slice(g * V_DIM, (g + 1) * V_DIM)
                l = acc_sc[g][:, V_DIM:]
                o_ref[:, vo] = (acc_sc[g][:, :V_DIM] / l).astype(BF16)
                lse_ref[:, vo] = m_sc[g][...] + jnp.log2(l)

    row = pl.BlockSpec((t, G * V_DIM), lambda h, s, qi, kj: (qi[s], h))
    return pl.pallas_call(
        body, name=name,
        grid_spec=pltpu.PrefetchScalarGridSpec(
            num_scalar_prefetch=2, grid=(H // G, len(pairs)),
            in_specs=[pl.BlockSpec((t, G * HEAD_PAD), lambda h, s, qi, kj: (qi[s], h)),
                      pl.BlockSpec((t, G * HEAD_PAD), lambda h, s, qi, kj: (kj[s], h)),
                      pl.BlockSpec((t, G * HEAD_PAD), lambda h, s, qi, kj: (kj[s], h))],
            out_specs=[row, row],
            scratch_shapes=[pltpu.VMEM((t, LANES), F32)] * G + [pltpu.VMEM((t, HEAD_PAD), F32)] * G),
        out_shape=[jax.ShapeDtypeStruct((T, H * V_DIM), BF16), jax.ShapeDtypeStruct((T, H * V_DIM), F32)],
        compiler_params=_params(("parallel", "arbitrary")),
    )(qi, kj, qf, kf, v)


def _attn_delta(do, o, name):
    T, W = do.shape
    nh = W // V_DIM
    tr = _pick(T, (512, 256, 128))

    def body(do_ref, o_ref, d_ref):
        for h in range(nh):
            sl = slice(h * V_DIM, (h + 1) * V_DIM)
            d = jnp.sum(do_ref[:, sl].astype(F32) * o_ref[:, sl].astype(F32), axis=-1, keepdims=True)
            d_ref[:, sl] = jnp.broadcast_to(d, (tr, V_DIM))

    row = pl.BlockSpec((tr, W), lambda i: (i, 0))
    return pl.pallas_call(
        body, name=name, grid=(T // tr,), in_specs=[row, row], out_specs=row,
        out_shape=jax.ShapeDtypeStruct((T, W), F32), compiler_params=_params(("parallel",)),
    )(do, o)


def _flash_bwd(qf, kf, v, do, lse, delta, name):
    T = qf.shape[0]
    H, G = MLA_HEADS, FLASH_HEADS
    t = _pick(T, (512, 256, 128))
    n = T // t
    scale = 1.0 / math.sqrt(QK_DIM)
    pairs = [(i, j) for j in range(n) for i in range(j, n)]
    qi = jnp.asarray([p[0] for p in pairs], jnp.int32)
    kj = jnp.asarray([p[1] for p in pairs], jnp.int32)

    def body(qi_ref, kj_ref, q_ref, k_ref, v_ref, do_ref, lse_ref, dl_ref, dq_ref, dk_ref, dv_ref,
             dq_acc, dk_acc, dv_acc):
        sid = pl.program_id(1)
        i, j = qi_ref[sid], kj_ref[sid]

        @pl.when(sid == 0)
        def _():
            dq_acc[...] = jnp.zeros_like(dq_acc)

        def step(masked):
            rows = pl.ds(pl.multiple_of(i * t, t), t)
            for g in range(G):
                qk = slice(g * HEAD_PAD, (g + 1) * HEAD_PAD)
                vo = slice(g * V_DIM, (g + 1) * V_DIM)
                q, k, do_ = q_ref[:, qk], k_ref[:, qk], do_ref[:, vo]
                v_ = v_ref[:, g * HEAD_PAD:g * HEAD_PAD + V_DIM]
                s = lax.dot_general(q, k, NT_DIMS, preferred_element_type=F32)
                if masked:
                    s = jnp.where(_causal_mask(t, t), s, NEG_BIG)
                p = jnp.exp2(s - lse_ref[:, g * V_DIM:g * V_DIM + 1])
                dp = lax.dot_general(do_, v_, NT_DIMS, preferred_element_type=F32)
                ds = (p * (dp - dl_ref[:, g * V_DIM:g * V_DIM + 1])).astype(BF16)
                dv = lax.dot_general(p.astype(BF16), do_, TN_DIMS, preferred_element_type=F32)
                dk = lax.dot_general(ds, q, TN_DIMS, preferred_element_type=F32)
                if masked:
                    dv_acc[:, vo] = dv
                    dk_acc[:, qk] = dk
                else:
                    dv_acc[:, vo] += dv
                    dk_acc[:, qk] += dk
                dq_acc[rows, qk] += jnp.dot(ds, k, preferred_element_type=F32) * scale

        @pl.when(i == j)
        def _():
            step(True)

        @pl.when(i > j)
        def _():
            step(False)

        @pl.when(i == n - 1)
        def _():
            dk_ref[...] = (dk_acc[...] * LN2).astype(BF16)
            dv_ref[...] = dv_acc[...].astype(BF16)

        @pl.when(sid == len(pairs) - 1)
        def _():
            dq_ref[...] = dq_acc[...].astype(BF16)

    qs = pl.BlockSpec((t, G * HEAD_PAD), lambda h, s, qi, kj: (qi[s], h))
    rs = pl.BlockSpec((t, G * V_DIM), lambda h, s, qi, kj: (qi[s], h))
    ks = pl.BlockSpec((t, G * HEAD_PAD), lambda h, s, qi, kj: (kj[s], h))
    vs = pl.BlockSpec((t, G * V_DIM), lambda h, s, qi, kj: (kj[s], h))
    return pl.pallas_call(
        body, name=name,
        grid_spec=pltpu.PrefetchScalarGridSpec(
            num_scalar_prefetch=2, grid=(H // G, len(pairs)), in_specs=[qs, ks, ks, rs, rs, rs],
            out_specs=[pl.BlockSpec((T, G * HEAD_PAD), lambda h, s, qi, kj: (0, h)), ks, vs],
            scratch_shapes=[pltpu.VMEM((T, G * HEAD_PAD), F32), pltpu.VMEM((t, G * HEAD_PAD), F32),
                            pltpu.VMEM((t, G * V_DIM), F32)]),
        out_shape=[jax.ShapeDtypeStruct((T, H * HEAD_PAD), BF16), jax.ShapeDtypeStruct((T, H * HEAD_PAD), BF16),
                   jax.ShapeDtypeStruct((T, H * V_DIM), BF16)],
        compiler_params=_params(("parallel", "arbitrary")),
    )(qi, kj, qf, kf, v, do, lse, delta)


def _alibi_slopes():
    tot = DIL_GROUPS * DIL_HEADS
    return [float(np.float32(2.0) ** (np.float32(-8.0) * np.float32(k) / np.float32(tot))) for k in range(1, tot + 1)]


def _dil_masks():
    iq = lax.broadcasted_iota(jnp.int32, (DIL_BLK, DIL_BLK), 0)
    ik = lax.broadcasted_iota(jnp.int32, (DIL_BLK, DIL_BLK), 1)
    return (ik >= iq), (iq + DIL_BLK - ik).astype(F32), (ik <= iq), (iq - ik).astype(F32)


def _dil_norm(x, g):
    r = lax.rsqrt(jnp.mean(x * x, axis=-1, keepdims=True) + EPS)
    return x * r, r


DIL_SUPER = 8
BNT_DIMS = (((2,), (2,)), ((0,), (0,)))
BNN_DIMS = (((2,), (1,)), ((0,), (0,)))
BTN_DIMS = (((1,), (1,)), ((0,), (0,)))


def _dil_chunk(it, nb, d):
    assert nb & (nb - 1) == 0, nb
    r, n = it >> (nb.bit_length() - 1), it & (nb - 1)
    if d > 1:
        tok = pl.ds(n * (d * DIL_BLK) + r, DIL_BLK, stride=d)
    else:
        tok = pl.ds(pl.multiple_of(it * DIL_BLK, DIL_BLK), DIL_BLK)
    return tok, pl.ds(pl.multiple_of((it + 1) * DIL_BLK, DIL_BLK), DIL_BLK)


def _dil_token_rows(bidx, nb, d):
    r, n = divmod(bidx, nb)
    return pl.ds(n * DIL_BLK * d + r, DIL_BLK, stride=d) if d > 1 else pl.ds(bidx * DIL_BLK, DIL_BLK)


def _dil_super_rows(ss):
    base = (1 + ss * DIL_SUPER) * DIL_BLK
    return pl.ds(base, DIL_SUPER * DIL_BLK), pl.ds(base - DIL_BLK, DIL_SUPER * DIL_BLK)


def _dil_b3(x):
    return x.reshape(DIL_SUPER, DIL_BLK, x.shape[-1])


def _dil_scores(q3, kc3, kp3, slope, d, ss, nb):
    ok_p, dist_p, ok_c, dist_c = _dil_masks()
    scale = 1.0 / math.sqrt(DIL_HEAD_DIM)
    bias_p = jnp.where(ok_p, -slope * d * dist_p, NEG_BIG)
    bias_c = jnp.where(ok_c, -slope * d * dist_c, NEG_BIG)
    s_c = lax.dot_general(q3, kc3, BNT_DIMS, preferred_element_type=F32) * scale + bias_c[None]
    s_p = lax.dot_general(q3, kp3, BNT_DIMS, preferred_element_type=F32) * scale + bias_p[None]
    bidx = ss * DIL_SUPER + lax.broadcasted_iota(jnp.int32, s_p.shape, 0)
    s_p = jnp.where((bidx & (nb - 1)) == 0, NEG_BIG, s_p)
    return s_c, s_p


def _dil_fwd(qkv, g_qn, g_kn, slopes, name):
    T = qkv.shape[0]
    GH = DIL_GROUPS * DIL_HEADS
    scale = 1.0 / math.sqrt(DIL_HEAD_DIM)

    def body(sl_ref, q_ref, k_ref, v_ref, gq_ref, gk_ref, o_ref, lse_ref, qn_pm, kn_pm, v_pm):
        gh = pl.program_id(0)
        slope = sl_ref[gh]
        gq, gk = gq_ref[...], gk_ref[...]
        pad = pl.ds(0, DIL_BLK)
        kn_pm[pad, :] = jnp.zeros((DIL_BLK, DIL_HEAD_DIM), BF16)
        v_pm[pad, :] = jnp.zeros((DIL_BLK, DIL_HEAD_DIM), BF16)
        for g, (_, d) in enumerate(DIL_PAIRS):
            @pl.when((gh >= g * DIL_HEADS) & (gh < (g + 1) * DIL_HEADS))
            def _(d=d):
                nb = T // (d * DIL_BLK)

                def fill(it, _):
                    tok, dst = _dil_chunk(it, nb, d)
                    qn_pm[dst, :] = (_dil_norm(q_ref[tok, :], gq)[0] * gq).astype(BF16)
                    kn_pm[dst, :] = (_dil_norm(k_ref[tok, :], gk)[0] * gk).astype(BF16)
                    v_pm[dst, :] = v_ref[tok, :].astype(BF16)
                    return 0
                lax.fori_loop(0, T // DIL_BLK, fill, 0, unroll=4)

                for ss in range(T // DIL_BLK // DIL_SUPER):
                    cur, prv = _dil_super_rows(ss)
                    q3, kc3, kp3 = _dil_b3(qn_pm[cur, :]), _dil_b3(kn_pm[cur, :]), _dil_b3(kn_pm[prv, :])
                    s_c, s_p = _dil_scores(q3, kc3, kp3, slope, d, ss, nb)
                    m = jnp.max(jnp.maximum(s_c, s_p), axis=-1, keepdims=True)
                    p_c = jnp.exp(s_c - m)
                    p_p = jnp.exp(s_p - m)
                    l = jnp.sum(p_c, axis=-1, keepdims=True) + jnp.sum(p_p, axis=-1, keepdims=True)
                    acc = lax.dot_general(p_c.astype(BF16), _dil_b3(v_pm[cur, :]), BNN_DIMS, preferred_element_type=F32)
                    acc += lax.dot_general(p_p.astype(BF16), _dil_b3(v_pm[prv, :]), BNN_DIMS, preferred_element_type=F32)
                    o3 = acc / l
                    lse3 = jnp.broadcast_to(m + jnp.log(l), o3.shape)
                    for b in range(DIL_SUPER):
                        tok = _dil_token_rows(ss * DIL_SUPER + b, nb, d)
                        o_ref[tok, :] = o3[b]
                        lse_ref[tok, :] = lse3[b]

    col = lambda off: pl.BlockSpec((T, DIL_HEAD_DIM), lambda gh, sl: (0, gh + off))
    gvec = pl.BlockSpec((1, DIL_HEAD_DIM), lambda gh, sl: (0, 0))
    return pl.pallas_call(
        body, name=name,
        grid_spec=pltpu.PrefetchScalarGridSpec(
            num_scalar_prefetch=1, grid=(GH,),
            in_specs=[col(0), col(GH), col(2 * GH), gvec, gvec], out_specs=[col(0), col(0)],
            scratch_shapes=[pltpu.VMEM((DIL_BLK + T, DIL_HEAD_DIM), BF16)] * 3),
        out_shape=[jax.ShapeDtypeStruct((T, GH * DIL_HEAD_DIM), F32)] * 2,
        compiler_params=_params(("parallel",)),
    )(slopes, qkv, qkv, qkv, g_qn, g_kn)


def _dil_merge(o_g, lse_g, name):
    T = o_g.shape[0]
    W = DIL_HEADS * DIL_HEAD_DIM
    tr = _pick(T, (256, 128))

    def body(o0, o1, o2, l0, l1, l2, o_ref, lse_ref):
        a, b, c = l0[...], l1[...], l2[...]
        m = jnp.maximum(jnp.maximum(a, b), c)
        ea, eb, ec = jnp.exp(a - m), jnp.exp(b - m), jnp.exp(c - m)
        tot = ea + eb + ec
        o_ref[...] = ((o0[...] * ea + o1[...] * eb + o2[...] * ec) / tot).astype(BF16)
        lse_ref[...] = m + jnp.log(tot)

    grp = lambda g: pl.BlockSpec((tr, W), lambda i: (i, g))
    out = pl.BlockSpec((tr, W), lambda i: (i, 0))
    return pl.pallas_call(
        body, name=name, grid=(T // tr,),
        in_specs=[grp(0), grp(1), grp(2), grp(0), grp(1), grp(2)], out_specs=[out, out],
        out_shape=[jax.ShapeDtypeStruct((T, W), BF16), jax.ShapeDtypeStruct((T, W), F32)],
        compiler_params=_params(("parallel",)),
    )(o_g, o_g, o_g, lse_g, lse_g, lse_g)


def _dil_bwd(qkv, g_qn, g_kn, slopes, do, delta, lse, name):
    T = qkv.shape[0]
    GH = DIL_GROUPS * DIL_HEADS
    scale = 1.0 / math.sqrt(DIL_HEAD_DIM)
    nchunk = T // DIL_BLK

    def body(sl_ref, q_ref, k_ref, v_ref, gq_ref, gk_ref, do_ref, dl_ref, lse_ref,
             dq_ref, dk_ref, dv_ref, dgq_ref, dgk_ref,
             qn_pm, kn_pm, v_pm, do_pm, lse_pm, dl_pm, dq_pm, dk_pm, dv_pm, tok_sc):
        gh = pl.program_id(0)
        slope = sl_ref[gh]
        gq, gk = gq_ref[...], gk_ref[...]

        @pl.when(gh == 0)
        def _():
            dgq_ref[...] = jnp.zeros_like(dgq_ref)
            dgk_ref[...] = jnp.zeros_like(dgk_ref)

        pad = pl.ds(0, DIL_BLK)
        kn_pm[pad, :] = jnp.zeros((DIL_BLK, DIL_HEAD_DIM), BF16)
        v_pm[pad, :] = jnp.zeros((DIL_BLK, DIL_HEAD_DIM), BF16)
        dk_pm[...] = jnp.zeros_like(dk_pm)
        dv_pm[...] = jnp.zeros_like(dv_pm)
        for g, (_, d) in enumerate(DIL_PAIRS):
            @pl.when((gh >= g * DIL_HEADS) & (gh < (g + 1) * DIL_HEADS))
            def _(d=d):
                nb = T // (d * DIL_BLK)

                def fill(it, _):
                    tok, dst = _dil_chunk(it, nb, d)
                    qn_pm[dst, :] = (_dil_norm(q_ref[tok, :], gq)[0] * gq).astype(BF16)
                    kn_pm[dst, :] = (_dil_norm(k_ref[tok, :], gk)[0] * gk).astype(BF16)
                    v_pm[dst, :] = v_ref[tok, :].astype(BF16)
                    do_pm[dst, :] = do_ref[tok, :].astype(BF16)
                    lse_pm[dst, :] = lse_ref[tok, :]
                    dl_pm[dst, :] = dl_ref[tok, :]
                    return 0
                lax.fori_loop(0, nchunk, fill, 0, unroll=4)

                for ss in range(nchunk // DIL_SUPER):
                    cur, prv = _dil_super_rows(ss)
                    q3, kc3, kp3 = _dil_b3(qn_pm[cur, :]), _dil_b3(kn_pm[cur, :]), _dil_b3(kn_pm[prv, :])
                    vc3, vp3, do3 = _dil_b3(v_pm[cur, :]), _dil_b3(v_pm[prv, :]), _dil_b3(do_pm[cur, :])
                    ls = _dil_b3(lse_pm[cur, :])[:, :, :1]
                    delta = _dil_b3(dl_pm[cur, :])[:, :, :1]
                    s_c, s_p = _dil_scores(q3, kc3, kp3, slope, d, ss, nb)
                    p_c = jnp.exp(s_c - ls)
                    p_p = jnp.exp(s_p - ls)
                    dp_c = lax.dot_general(do3, vc3, BNT_DIMS, preferred_element_type=F32)
                    dp_p = lax.dot_general(do3, vp3, BNT_DIMS, preferred_element_type=F32)
                    ds_c = (p_c * (dp_c - delta)).astype(BF16)
                    ds_p = (p_p * (dp_p - delta)).astype(BF16)
                    dq3 = (lax.dot_general(ds_c, kc3, BNN_DIMS, preferred_element_type=F32)
                           + lax.dot_general(ds_p, kp3, BNN_DIMS, preferred_element_type=F32)) * scale
                    flat = lambda x: x.reshape(DIL_SUPER * DIL_BLK, DIL_HEAD_DIM)
                    dq_pm[cur, :] = flat(dq3)
                    dk_pm[cur, :] += flat(lax.dot_general(ds_c, q3, BTN_DIMS, preferred_element_type=F32)) * scale
                    dv_pm[cur, :] += flat(lax.dot_general(p_c.astype(BF16), do3, BTN_DIMS, preferred_element_type=F32))
                    dk_pm[prv, :] += flat(lax.dot_general(ds_p, q3, BTN_DIMS, preferred_element_type=F32)) * scale
                    dv_pm[prv, :] += flat(lax.dot_general(p_p.astype(BF16), do3, BTN_DIMS, preferred_element_type=F32))

                def to_tokens(src_pm):
                    def move(it, _):
                        tok, src = _dil_chunk(it, nb, d)
                        tok_sc[tok, :] = src_pm[src, :]
                        return 0
                    lax.fori_loop(0, nchunk, move, 0, unroll=4)

                def norm_bwd(x_ref, gvec, out_ref):
                    big = 4 * DIL_BLK

                    def fin(ci, dg):
                        rows = pl.ds(pl.multiple_of(ci * big, big), big)
                        xhat, r = _dil_norm(x_ref[rows, :], gvec)
                        dn = tok_sc[rows, :]
                        dxh = dn * gvec
                        c = jnp.mean(dxh * xhat, axis=-1, keepdims=True)
                        out_ref[rows, :] = (r * (dxh - xhat * c)).astype(BF16)
                        return dg + jnp.sum(dn * xhat, axis=0, keepdims=True)
                    return lax.fori_loop(0, T // big, fin, jnp.zeros((1, DIL_HEAD_DIM), F32))

                to_tokens(dq_pm)
                dgq_ref[...] += norm_bwd(q_ref, gq, dq_ref)
                to_tokens(dk_pm)
                dgk_ref[...] += norm_bwd(k_ref, gk, dk_ref)
                to_tokens(dv_pm)
                dv_ref[...] = tok_sc[...].astype(BF16)

    col = lambda off: pl.BlockSpec((T, DIL_HEAD_DIM), lambda gh, sl: (0, gh + off))
    hcol = pl.BlockSpec((T, DIL_HEAD_DIM), lambda gh, sl: (0, gh % DIL_HEADS))
    gvec = pl.BlockSpec((1, DIL_HEAD_DIM), lambda gh, sl: (0, 0))
    wide = jax.ShapeDtypeStruct((T, GH * DIL_HEAD_DIM), BF16)
    vec = jax.ShapeDtypeStruct((1, DIL_HEAD_DIM), F32)
    pm = lambda dt: pltpu.VMEM((DIL_BLK + T, DIL_HEAD_DIM), dt)
    return pl.pallas_call(
        body, name=name,
        grid_spec=pltpu.PrefetchScalarGridSpec(
            num_scalar_prefetch=1, grid=(GH,),
            in_specs=[col(0), col(GH), col(2 * GH), gvec, gvec, hcol, hcol, hcol],
            out_specs=[col(0), col(0), col(0), gvec, gvec],
            scratch_shapes=[pm(BF16)] * 4 + [pm(F32)] * 5 + [pltpu.VMEM((T, DIL_HEAD_DIM), F32)]),
        out_shape=[wide, wide, wide, vec, vec],
        compiler_params=_params(("arbitrary",)),
    )(slopes, qkv, qkv, qkv, g_qn, g_kn, do, delta, lse)


def _my_pos():
    return lax.axis_index("x"), lax.axis_index("y"), lax.axis_index("c")


def _peer(pos, j):
    x, y, c = pos
    px = 1 - x if j & 4 else x
    py = 1 - y if j & 2 else y
    pc = 1 - c if j & 1 else c
    return (px, py, pc), 4 * px + 2 * py + pc


def _slot(idx, paired):
    if not paired:
        return idx
    return jnp.where(idx < N_DEV // 2, 2 * idx, 2 * idx - (N_DEV - 1))


def _shard_slice(ref, axis, idx, size, paired=False):
    sl = [slice(None)] * len(ref.shape)
    sl[axis] = pl.ds(pl.multiple_of(_slot(idx, paired) * size, 8), size)
    return ref.at[tuple(sl)]


HBM_SPEC = pl.BlockSpec(memory_space=pltpu.HBM)
SEM_SPEC = pl.BlockSpec(memory_space=pltpu.SEMAPHORE)
DATAFLOW = pltpu.SideEffectType.DATAFLOW_SIDE_EFFECTING
N_PEER = N_DEV - 1


def _scatter_copy(axis, grad, slots, frm, to, dev, send_sem, recv_sem):
    ax, paired = axis
    src = _shard_slice(grad, ax, to, grad.shape[ax] // N_DEV, paired)
    return pltpu.make_async_remote_copy(src_ref=src, dst_ref=slots.at[frm], send_sem=send_sem, recv_sem=recv_sem,
                                        device_id=dev, device_id_type=MESH)


def _scatter_start(grads, axes, name):
    n = len(grads)

    def body(*refs):
        outs = refs[2 * n:]
        send, recv, token = outs[:n], outs[n:2 * n], outs[4 * n]
        pos = _my_pos()
        me = 4 * pos[0] + 2 * pos[1] + pos[2]
        for a in range(n):
            for j in range(1, N_DEV):
                dev, pid = _peer(pos, j)
                _scatter_copy(axes[a], refs[2 * a], refs[2 * a + 1], me, pid, dev, send[a].at[j - 1],
                              recv[a].at[j - 1]).start()
        token[...] = jnp.zeros_like(token)

    ops = []
    for g, (ax, _) in zip(grads, axes):
        shp = list(g.shape)
        shp[ax] //= N_DEV
        ops += [g, lax.empty((N_DEV,) + tuple(shp), g.dtype)]
    sems = [pltpu.SemaphoreType.DMA((N_PEER,))] * (2 * n)
    res = pl.pallas_call(
        body, name=name,
        out_shape=sems + [pltpu.HBM(o.shape, o.dtype) for o in ops] + [jax.ShapeDtypeStruct((8, LANES), F32)],
        in_specs=[HBM_SPEC] * len(ops),
        out_specs=[SEM_SPEC] * (2 * n) + [HBM_SPEC] * len(ops) + [pl.BlockSpec(memory_space=pltpu.VMEM)],
        input_output_aliases={i: 2 * n + i for i in range(len(ops))},
        compiler_params=pltpu.CompilerParams(has_side_effects=DATAFLOW),
    )(*[pltpu.with_memory_space_constraint(o, pltpu.HBM) for o in ops])
    items = [(res[a], res[n + a], res[2 * n + 2 * a], res[2 * n + 2 * a + 1]) for a in range(n)]
    return items, res[4 * n]


def _scatter_wait(items, axes, after, name):
    n = len(items)

    def body(*refs):
        send, recv = refs[2 * n:3 * n], refs[3 * n:4 * n]
        pos = _my_pos()
        me = 4 * pos[0] + 2 * pos[1] + pos[2]
        for a in range(n):
            for j in range(1, N_DEV):
                dev, pid = _peer(pos, j)
                cp = _scatter_copy(axes[a], refs[2 * a], refs[2 * a + 1], pid, me, dev, send[a].at[j - 1],
                                   recv[a].at[j - 1])
                cp.wait_send()
                cp.wait_recv()

    ops = [b for it in items for b in it[2:]]
    res = pl.pallas_call(
        body, name=name,
        out_shape=[pltpu.HBM(o.shape, o.dtype) for o in ops],
        in_specs=[HBM_SPEC] * len(ops) + [SEM_SPEC] * (2 * n) + [ANY_SPEC],
        out_specs=[HBM_SPEC] * len(ops),
        input_output_aliases={i: i for i in range(len(ops))},
        compiler_params=pltpu.CompilerParams(has_side_effects=DATAFLOW),
    )(*ops, *[it[0] for it in items], *[it[1] for it in items], after)
    return [(res[2 * a], res[2 * a + 1]) for a in range(n)]


SIBLING = 1
ICI_PEERS = (2, 4, 6)


def _gather_copy(buf, axis, shard, dev, send_sem, recv_sem):
    ax, paired = axis
    piece = _shard_slice(buf, ax, shard, buf.shape[ax] // N_DEV, paired)
    return pltpu.make_async_remote_copy(src_ref=piece, dst_ref=piece, send_sem=send_sem, recv_sem=recv_sem,
                                        device_id=dev, device_id_type=MESH)


def _gather_start(bufs, axes, name):
    n = len(bufs)

    def body(*refs):
        ins, outs = refs[:n], refs[n:]
        send, r_sib, r_ici, token = outs[:n], outs[n:2 * n], outs[2 * n:3 * n], outs[4 * n]
        pos = _my_pos()
        me = 4 * pos[0] + 2 * pos[1] + pos[2]
        for a in range(n):
            dev, _ = _peer(pos, SIBLING)
            _gather_copy(ins[a], axes[a], me, dev, send[a].at[0], r_sib[a].at[0]).start()
            for k, j in enumerate(ICI_PEERS):
                dev, _ = _peer(pos, j)
                _gather_copy(ins[a], axes[a], me, dev, send[a].at[1 + k], r_ici[a].at[k]).start()
        token[...] = jnp.zeros_like(token)

    sems = ([pltpu.SemaphoreType.DMA((1 + len(ICI_PEERS),))] * n + [pltpu.SemaphoreType.DMA((1,))] * n
            + [pltpu.SemaphoreType.DMA((len(ICI_PEERS),))] * n)
    res = pl.pallas_call(
        body, name=name,
        out_shape=sems + [pltpu.HBM(b.shape, b.dtype) for b in bufs] + [jax.ShapeDtypeStruct((8, LANES), F32)],
        in_specs=[HBM_SPEC] * n,
        out_specs=[SEM_SPEC] * (3 * n) + [HBM_SPEC] * n + [pl.BlockSpec(memory_space=pltpu.VMEM)],
        input_output_aliases={i: 3 * n + i for i in range(n)},
        compiler_params=pltpu.CompilerParams(has_side_effects=DATAFLOW),
    )(*[pltpu.with_memory_space_constraint(b, pltpu.HBM) for b in bufs])
    items = [dict(send=res[a], r_sib=res[n + a], r_ici=res[2 * n + a], buf=res[3 * n + a]) for a in range(n)]
    return items, res[4 * n]


def _gather_relay(items, axes, after, name):
    n = len(items)

    def body(*refs):
        ins, r_ici = refs[:n], refs[n:2 * n]
        outs = refs[2 * n + 1:]
        s_rel, r_rel, token = outs[:n], outs[n:2 * n], outs[3 * n]
        pos = _my_pos()
        sib, _ = _peer(pos, SIBLING)
        for a in range(n):
            for k, j in enumerate(ICI_PEERS):
                dev, pid = _peer(pos, j)
                _gather_copy(ins[a], axes[a], pid, dev, s_rel[a].at[k], r_ici[a].at[k]).wait_recv()
                _gather_copy(ins[a], axes[a], pid, sib, s_rel[a].at[k], r_rel[a].at[k]).start()
        token[...] = jnp.zeros_like(token)

    bufs = [it["buf"] for it in items]
    sems = [pltpu.SemaphoreType.DMA((len(ICI_PEERS),))] * (2 * n)
    res = pl.pallas_call(
        body, name=name,
        out_shape=sems + [pltpu.HBM(b.shape, b.dtype) for b in bufs] + [jax.ShapeDtypeStruct((8, LANES), F32)],
        in_specs=[HBM_SPEC] * n + [SEM_SPEC] * n + [ANY_SPEC],
        out_specs=[SEM_SPEC] * (2 * n) + [HBM_SPEC] * n + [pl.BlockSpec(memory_space=pltpu.VMEM)],
        input_output_aliases={i: 2 * n + i for i in range(n)},
        compiler_params=pltpu.CompilerParams(has_side_effects=DATAFLOW),
    )(*bufs, *[it["r_ici"] for it in items], after)
    out = [dict(send=it["send"], r_sib=it["r_sib"], s_rel=res[a], r_rel=res[n + a], buf=res[2 * n + a])
           for a, it in enumerate(items)]
    return out, res[3 * n]


def _gather_wait(items, axes, after, name):
    n = len(items)

    def body(*refs):
        ins = refs[:n]
        send, r_sib, s_rel, r_rel = (refs[(1 + q) * n:(2 + q) * n] for q in range(4))
        pos = _my_pos()
        me = 4 * pos[0] + 2 * pos[1] + pos[2]
        sib, sib_id = _peer(pos, SIBLING)
        for a in range(n):
            for k in range(1 + len(ICI_PEERS)):
                _gather_copy(ins[a], axes[a], me, sib, send[a].at[k], r_sib[a].at[0]).wait_send()
            _gather_copy(ins[a], axes[a], sib_id, sib, send[a].at[0], r_sib[a].at[0]).wait_recv()
            for k, j in enumerate(ICI_PEERS):
                _, pid = _peer(pos, j)
                _, far = _peer(pos, j ^ SIBLING)
                _gather_copy(ins[a], axes[a], pid, sib, s_rel[a].at[k], r_rel[a].at[k]).wait_send()
                _gather_copy(ins[a], axes[a], far, sib, s_rel[a].at[k], r_rel[a].at[k]).wait_recv()

    bufs = [it["buf"] for it in items]
    res = pl.pallas_call(
        body, name=name,
        out_shape=[pltpu.HBM(b.shape, b.dtype) for b in bufs],
        in_specs=[HBM_SPEC] * n + [SEM_SPEC] * (4 * n) + [ANY_SPEC],
        out_specs=[HBM_SPEC] * n,
        input_output_aliases={i: i for i in range(n)},
        compiler_params=pltpu.CompilerParams(has_side_effects=DATAFLOW),
    )(*bufs, *[it["send"] for it in items], *[it["r_sib"] for it in items], *[it["s_rel"] for it in items],
      *[it["r_rel"] for it in items], after)
    return list(res)


def _gain_allreduce(v, name):
    n = v.shape[1]

    def body(v_ref, o_ref, slots, send_sems, recv_sems):
        pos = _my_pos()
        me = 4 * pos[0] + 2 * pos[1] + pos[2]
        slots[me] = v_ref[...]
        copies = []
        for j in range(1, N_DEV):
            dev, _ = _peer(pos, j)
            cp = pltpu.make_async_remote_copy(
                src_ref=slots.at[me], dst_ref=slots.at[me], send_sem=send_sems.at[j], recv_sem=recv_sems.at[j],
                device_id=dev, device_id_type=MESH)
            cp.start()
            copies.append(cp)
        for j in range(1, N_DEV):
            dev, pid = _peer(pos, j)
            pltpu.make_async_remote_copy(
                src_ref=slots.at[me], dst_ref=slots.at[pid], send_sem=send_sems.at[j], recv_sem=recv_sems.at[j],
                device_id=dev, device_id_type=MESH).wait_recv()
        for cp in copies:
            cp.wait_send()
        acc = slots[0]
        for s in range(1, N_DEV):
            acc = acc + slots[s]
        o_ref[...] = acc

    return pl.pallas_call(
        body, name=name, out_shape=jax.ShapeDtypeStruct((1, n), F32),
        in_specs=[pl.BlockSpec(memory_space=pltpu.VMEM)], out_specs=pl.BlockSpec(memory_space=pltpu.VMEM),
        scratch_shapes=[pltpu.VMEM((N_DEV, 1, n), F32), pltpu.SemaphoreType.DMA((N_DEV,)),
                        pltpu.SemaphoreType.DMA((N_DEV,))],
        compiler_params=pltpu.CompilerParams(has_side_effects=True),
    )(v)


def _adamw(parts, own, me, w, m, v, layer, prev, name, own_axis=None):
    L, R, C = w.shape
    P = parts.shape[0]
    tr = _pick(R, (128, 64, 32, 16, 8, 1))
    c1 = 1.0 - ADAM_B1 ** ADAM_STEP
    c2 = 1.0 - ADAM_B2 ** ADAM_STEP
    n_in = 4 if own is None else 5

    def body(me_ref, *refs):
        p_ref = refs[0]
        w_ref, m_ref, v_ref = refs[n_in - 3:n_in]
        g_out, d_out, m_out, v_out, tok = refs[-5:]
        g = None
        for s in range(P):
            part = p_ref[s]
            if own is not None:
                part = jnp.where(me_ref[0] == s, refs[1][...], part)
            g = part.astype(F32) if g is None else g + part.astype(F32)
        mn = ADAM_B1 * m_ref[...] + (1.0 - ADAM_B1) * g
        vn = ADAM_B2 * v_ref[...] + (1.0 - ADAM_B2) * (g * g)
        g_out[...] = g
        m_out[...] = mn
        v_out[...] = vn
        d_out[...] = -ADAM_LR * ((mn / c1) / (jnp.sqrt(vn / c2) + ADAM_EPS) + ADAM_WD * w_ref[...])
        tok[...] = jnp.zeros_like(tok)

    row = pl.BlockSpec((None, tr, C), lambda i, me_ref: (layer, i, 0))
    in_specs = [pl.BlockSpec((P, tr, C), lambda i, me_ref: (0, i, 0))]
    args = [parts]
    if own is not None:
        if own_axis is None:
            own_idx = lambda i, me_ref: (i, 0)
        elif own_axis[0] == 0:
            own_idx = lambda i, me_ref: (_slot(me_ref[0], own_axis[1]) * (R // tr) + i, 0)
        else:
            own_idx = lambda i, me_ref: (i, _slot(me_ref[0], own_axis[1]))
        in_specs.append(pl.BlockSpec((tr, C), own_idx))
        args.append(own)
    in_specs += [row, row, row]
    args += [w, m, v]
    aliases = {}
    if prev is not None:
        in_specs += [ANY_SPEC] * 4
        aliases = {1 + len(args) + k: k for k in range(4)}
        args += list(prev)
    shp = jax.ShapeDtypeStruct((L, R, C), F32)
    res = pl.pallas_call(
        body, name=name,
        grid_spec=pltpu.PrefetchScalarGridSpec(
            num_scalar_prefetch=1, grid=(R // tr,), in_specs=in_specs,
            out_specs=[row] * 4 + [pl.BlockSpec((8, LANES), lambda i, me_ref: (0, 0))]),
        out_shape=[shp] * 4 + [jax.ShapeDtypeStruct((8, LANES), F32)],
        input_output_aliases=aliases, compiler_params=_params(("arbitrary",)),
    )(me, *args)
    return res[:4], res[4]


def _pad_heads(w):
    lead = w.shape[:-1]
    n = w.shape[-1] // QK_DIM
    w = w.reshape(lead + (n, QK_DIM))
    w = jnp.pad(w, [(0, 0)] * len(lead) + [(0, 0), (0, HEAD_PAD - QK_DIM)])
    return w.reshape(lead + (n * HEAD_PAD,))


def _unpad_heads(w):
    lead = w.shape[:-1]
    n = w.shape[-1] // HEAD_PAD
    return w.reshape(lead + (n, HEAD_PAD))[..., :QK_DIM].reshape(lead + (n * QK_DIM,))


def kernel(x, ffn1_norm, ffn1_w_in, ffn1_w_out, mix_norm, ffn2_norm, ffn2_w_in, ffn2_w_out, mla_w_down, mla_g_cq, mla_g_ckv, mla_w_uq, mla_w_ukv, mla_g_qn, mla_g_kn, mla_w_o, dil_w_qkv, dil_g_qn, dil_g_kn, dil_w_o, loss_target, m_ffn1_norm, m_ffn1_w_in, m_ffn1_w_out, m_mix_norm, m_ffn2_norm, m_ffn2_w_in, m_ffn2_w_out, m_mla_w_down, m_mla_g_cq, m_mla_g_ckv, m_mla_w_uq, m_mla_w_ukv, m_mla_g_qn, m_mla_g_kn, m_mla_w_o, m_dil_w_qkv, m_dil_g_qn, m_dil_g_kn, m_dil_w_o, v_ffn1_norm, v_ffn1_w_in, v_ffn1_w_out, v_mix_norm, v_ffn2_norm, v_ffn2_w_in, v_ffn2_w_out, v_mla_w_down, v_mla_g_cq, v_mla_g_ckv, v_mla_w_uq, v_mla_w_ukv, v_mla_g_qn, v_mla_g_kn, v_mla_w_o, v_dil_w_qkv, v_dil_g_qn, v_dil_g_kn, v_dil_w_o):
    names = ["ffn1_norm", "ffn1_w_in", "ffn1_w_out", "mix_norm", "ffn2_norm", "ffn2_w_in", "ffn2_w_out", "mla_w_down",
             "mla_g_cq", "mla_g_ckv", "mla_w_uq", "mla_w_ukv", "mla_g_qn", "mla_g_kn", "mla_w_o", "dil_w_qkv",
             "dil_g_qn", "dil_g_kn", "dil_w_o"]
    W = dict(zip(names, [ffn1_norm, ffn1_w_in, ffn1_w_out, mix_norm, ffn2_norm, ffn2_w_in, ffn2_w_out, mla_w_down,
                         mla_g_cq, mla_g_ckv, mla_w_uq, mla_w_ukv, mla_g_qn, mla_g_kn, mla_w_o, dil_w_qkv,
                         dil_g_qn, dil_g_kn, dil_w_o]))
    M1 = dict(zip(names, [m_ffn1_norm, m_ffn1_w_in, m_ffn1_w_out, m_mix_norm, m_ffn2_norm, m_ffn2_w_in, m_ffn2_w_out,
                          m_mla_w_down, m_mla_g_cq, m_mla_g_ckv, m_mla_w_uq, m_mla_w_ukv, m_mla_g_qn, m_mla_g_kn,
                          m_mla_w_o, m_dil_w_qkv, m_dil_g_qn, m_dil_g_kn, m_dil_w_o]))
    V2 = dict(zip(names, [v_ffn1_norm, v_ffn1_w_in, v_ffn1_w_out, v_mix_norm, v_ffn2_norm, v_ffn2_w_in, v_ffn2_w_out,
                          v_mla_w_down, v_mla_g_cq, v_mla_g_ckv, v_mla_w_uq, v_mla_w_ukv, v_mla_g_qn, v_mla_g_kn,
                          v_mla_w_o, v_dil_w_qkv, v_dil_g_qn, v_dil_g_kn, v_dil_w_o]))
    S, D = x.shape[1], x.shape[2]
    x0 = x.reshape(S, D)
    tgt = loss_target.reshape(S, D)

    big = ["ffn1_w_in", "ffn1_w_out", "ffn2_w_in", "ffn2_w_out", "mla_w_down", "mla_w_uq", "mla_w_ukv", "mla_w_o",
           "dil_w_qkv", "dil_w_o"]
    shard_dim = {"ffn1_w_in": 2, "ffn1_w_out": 1, "ffn2_w_in": 2, "ffn2_w_out": 1, "mla_w_down": 1, "mla_w_uq": 2,
                 "mla_w_ukv": 2, "mla_w_o": 1, "dil_w_qkv": 2, "dil_w_o": 2}
    paired = ("ffn1_w_in", "ffn2_w_in")
    shard_axis = {n: (d, n in paired) for n, d in shard_dim.items()}
    grad_axis = {n: (d - 1, n in paired) for n, d in shard_dim.items()}

    def padded(n, w):
        if n == "mla_w_down":
            return jnp.pad(w, ((0, 0), (0, 0), (0, LAT_PAD - w.shape[2])))
        if n == "mla_w_uq":
            return _pad_heads(w)
        return w

    depth = ffn1_norm.shape[0]
    blocks = []
    for l in range(depth):
        mixer = (["mla_w_down", "mla_w_uq", "mla_w_ukv", "mla_w_o"] if l % 2 == 0 else ["dil_w_qkv", "dil_w_o"])
        blocks.append((f"ffn1_{l}", [("ffn1_w_in", l), ("ffn1_w_out", l)]))
        blocks.append((f"mix_{l}", [(n, l // 2) for n in mixer]))
        blocks.append((f"ffn2_{l}", [("ffn2_w_in", l), ("ffn2_w_out", l)]))
    order = [k for _, keys in blocks for k in keys]
    me = (4 * lax.axis_index("x") + 2 * lax.axis_index("y") + lax.axis_index("c")).astype(jnp.int32).reshape(1)
    def cast(key, deps=()):
        n, l = key
        return _cast_into_gathered(padded(n, W[n]), l, shard_axis[n], me, f"cast_{n}_{l}", deps=deps)

    items0, token0 = _gather_start([cast(order[0])], [shard_axis[order[0][0]]], "gather_start_first")
    rest = order[1:]
    items1, ag_token = _gather_start([cast(k, deps=[token0]) for k in rest], [shard_axis[k[0]] for k in rest],
                                     "gather_start_rest")
    ag_items = dict(zip(order, items0 + items1))
    full = {}

    def relay(keys, after, tag):
        out, token = _gather_relay([ag_items[k] for k in keys], [shard_axis[k[0]] for k in keys], after,
                                   f"gather_relay_{tag}")
        ag_items.update(zip(keys, out))
        return [token]

    def relay_next(bi, after):
        return relay(blocks[bi + 1][1], after, blocks[bi + 1][0]) if bi + 1 < len(blocks) else []

    def fetch(keys, after, tag):
        lands = _gather_wait([ag_items[k] for k in keys], [shard_axis[k[0]] for k in keys], after,
                             f"gather_wait_{tag}")
        full.update(zip(keys, lands))

    g_qn = _pad_heads(mla_g_qn)
    g_kn = _pad_heads(mla_g_kn)
    tabs = _rope_tables(S)
    slopes = jnp.asarray(_alibi_slopes(), F32)

    grads = {}
    gain_g = {}

    out_g, out_d, out_m, out_v = {}, {}, {}, {}
    pending = []
    lag = 3

    def scatter_start(tag, keys):
        items, token = _scatter_start([grads[k] for k in keys], [grad_axis[k[0]] for k in keys],
                                      f"scatter_start_{tag}")
        pending.append((tag, keys, items))
        return token

    def scatter_finish(after):
        tag, keys, items = pending.pop(0)
        lands = _scatter_wait(items, [grad_axis[k[0]] for k in keys], after, f"scatter_wait_{tag}")
        tokens = []
        for (n, l), (own, p) in zip(keys, lands):
            own_axis = grad_axis[n]
            if n in ("mla_w_down", "mla_w_uq"):
                ax, pair = grad_axis[n]
                size = own.shape[ax] // N_DEV
                own = lax.dynamic_slice_in_dim(own, _slot(me[0], pair) * size, size, axis=ax)
                own_axis = None
                if n == "mla_w_down":
                    p, own = p[..., :W[n].shape[2]], own[..., :W[n].shape[2]]
                else:
                    p, own = _unpad_heads(p), _unpad_heads(own)
            prev = (out_g[n], out_d[n], out_m[n], out_v[n]) if n in out_g else None
            (out_g[n], out_d[n], out_m[n], out_v[n]), tok = _adamw(p, own, me, W[n], M1[n], V2[n], l, prev,
                                                                    f"adamw_{n}_{l}", own_axis=own_axis)
            tokens.append(tok)
        return tokens

    def finish_due(after):
        tokens = []
        while len(pending) > lag:
            tokens += scatter_finish(after)
        return tokens

    def mixer_out(o, w_o, xin, l, toks, name):
        tm, tn, tk, _ = _mm_tiles(S, o.shape[1], D, 2, 2, 4, True)
        if tn == D and tk == o.shape[1]:
            return _mm(o, w_o, "nn", F32, name, res=xin, layer=0, deps=toks, norm_gain=ffn2_norm[l:l + 1])
        return _mm(o, w_o, "nn", F32, name, res=xin, layer=0, deps=toks), None

    def mixer_do(dxob, w_o, o, out_dtype, name):
        _, _, tk, _ = _mm_tiles(S, D, o.shape[1], 2, 2, jnp.dtype(out_dtype).itemsize, False)
        if tk == D:
            return _mm(dxob, w_o, "nt", out_dtype, name, layer=0, dot_with=o)
        do = _mm(dxob, w_o, "nt", out_dtype, name, layer=0)
        return do, _attn_delta(do, o, name + "_delta")

    def ffn_fwd(xin, norm_row, which, l, bi, deps=(), h=None):
        tag = blocks[bi][0]
        k_in, k_out = (which + "_w_in", l), (which + "_w_out", l)
        if h is None:
            h = _rms_fwd(xin, norm_row, f"rms_fwd_{tag}", deps=deps)
        if bi == 0:
            relay([k_in], h, f"{tag}_in")
        fetch([k_in], h, f"in_{tag}")
        u, a = _ffn_in(h, full[k_in], f"ffn_in_{tag}")
        if bi == 0:
            relay([k_out], a, f"{tag}_out")
        fetch([k_out], a, f"out_{tag}")
        toks = relay_next(bi, a)
        fused_loss = bi == len(blocks) - 1 and _mm_tiles(S, a.shape[1], D, 2, 2, 4, True)[2] == a.shape[1]
        xo = _mm(a, full[k_out], "nn", F32, f"mm_out_{tag}", scale=0.5, res=xin, layer=0, deps=toks,
                 loss_target=tgt if fused_loss else None)
        return xo, (xin, h, u, a)

    def ffn_bwd(dx_pair, saved, norm_row, which, l, tag, deps=()):
        dxo, dxob = dx_pair
        k_in, k_out = (which + "_w_in", l), (which + "_w_out", l)
        xin, h, u, a = saved
        grads[k_out] = _mm(a, dxob, "tn", BF16, f"mm_dwout_{tag}", scale=0.5, deps=deps)
        t_out = scatter_start(f"{tag}_out", [k_out])
        du = _ffn_da(dxob, full[k_out], u, f"ffn_da_{tag}", deps=[t_out])
        grads[k_in] = _mm(h, du, "tn", BF16, f"mm_dwin_{tag}")
        t_in = scatter_start(f"{tag}_in", [k_in])
        dh = _mm(du, full[k_in], "nt", F32, f"mm_dh_{tag}", layer=0, deps=[t_in])
        toks = finish_due(dh)
        dx, dxb, dg = _rms_bwd(xin, norm_row, dh, dxo, f"rms_bwd_{tag}", deps=toks)
        gain_g.setdefault(which + "_norm", {})[l] = dg
        return dx, dxb

    def mla_fwd(xin, l, bi):
        j = l // 2
        xn = _rms_fwd(xin, mix_norm[l:l + 1], "rms_fwd_mla")
        fetch([(n, j) for n in ("mla_w_down", "mla_w_uq", "mla_w_ukv", "mla_w_o")], xn, "mla")
        lat = _mm(xn, full[("mla_w_down", j)], "nn", F32, "mm_lat", layer=0)
        cq, ckv = _lat_norm_fwd(lat, mla_g_cq[j:j + 1], mla_g_ckv[j:j + 1], "lat_norm_fwd")
        q_raw = _mm(cq, full[("mla_w_uq", j)], "nn", F32, "mm_uq", layer=0)
        kv = _mm(ckv, full[("mla_w_ukv", j)], "nn", F32, "mm_ukv", layer=0)
        qf, kf, vb = _mla_prep_fwd(q_raw, kv, lat, g_qn[j:j + 1], g_kn[j:j + 1], tabs, "mla_prep_fwd")
        o, lse = _flash_fwd(qf, kf, vb, "flash_fwd")
        toks = relay_next(bi, o)
        xo, h_next = mixer_out(o, full[("mla_w_o", j)], xin, l, toks, "mm_mla_o")
        return (xo, h_next), (xin, xn, lat, cq, ckv, q_raw, kv, qf, kf, vb, o, lse)

    def mla_bwd(dx_pair, saved, l):
        dxo, dxob = dx_pair
        j = l // 2
        xin, xn, lat, cq, ckv, q_raw, kv, qf, kf, vb, o, lse = saved
        do, delta = mixer_do(dxob, full[("mla_w_o", j)], o, BF16, "mm_mla_do")
        grads[("mla_w_o", j)] = _mm(o, dxob, "tn", BF16, "mm_mla_dwo")
        dqf, dkf, dv = _flash_bwd(qf, kf, vb, do, lse, delta, "flash_bwd")
        dq_raw, dkv, dkpe, dgq, dgk = _mla_prep_bwd(q_raw, kv, lat, g_qn[j:j + 1], g_kn[j:j + 1], tabs, dqf, dkf, dv,
                                                    "mla_prep_bwd")
        gain_g.setdefault("mla_g_qn", {})[j] = dgq
        gain_g.setdefault("mla_g_kn", {})[j] = dgk
        dcq = _mm(dq_raw, full[("mla_w_uq", j)], "nt", F32, "mm_dcq", layer=0)
        grads[("mla_w_uq", j)] = _mm(cq, dq_raw, "tn", BF16, "mm_dwuq")
        dckv = _mm(dkv, full[("mla_w_ukv", j)], "nt", F32, "mm_dckv", layer=0)
        grads[("mla_w_ukv", j)] = _mm(ckv, dkv, "tn", BF16, "mm_dwukv")
        dlat, dgcq, dgckv = _lat_norm_bwd(lat, mla_g_cq[j:j + 1], mla_g_ckv[j:j + 1], dcq, dckv, dkpe, "lat_norm_bwd")
        gain_g.setdefault("mla_g_cq", {})[j] = dgcq
        gain_g.setdefault("mla_g_ckv", {})[j] = dgckv
        dxn = _mm(dlat, full[("mla_w_down", j)], "nt", F32, "mm_dxn_mla", layer=0)
        grads[("mla_w_down", j)] = _mm(xn, dlat, "tn", BF16, "mm_dwdown")
        tok = scatter_start(f"mix_{l}", [(n, j) for n in ("mla_w_down", "mla_w_uq", "mla_w_ukv", "mla_w_o")])
        toks = finish_due(dxn)
        dx, dxb, dg = _rms_bwd(xin, mix_norm[l:l + 1], dxn, dxo, "rms_bwd_mla", deps=[tok] + toks)
        gain_g.setdefault("mix_norm", {})[l] = dg
        return dx, dxb

    def dil_fwd(xin, l, bi):
        j = l // 2
        xn = _rms_fwd(xin, mix_norm[l:l + 1], "rms_fwd_dil")
        fetch([("dil_w_qkv", j), ("dil_w_o", j)], xn, "dil")
        qkv = _mm(xn, full[("dil_w_qkv", j)], "nn", F32, "mm_qkv", layer=0)
        o_g, lse_g = _dil_fwd(qkv, dil_g_qn[j:j + 1], dil_g_kn[j:j + 1], slopes, "dil_fwd")
        o, lse = _dil_merge(o_g, lse_g, "dil_merge")
        toks = relay_next(bi, o)
        xo, h_next = mixer_out(o, full[("dil_w_o", j)], xin, l, toks, "mm_dil_o")
        return (xo, h_next), (xin, xn, qkv, o, lse)

    def dil_bwd(dx_pair, saved, l):
        dxo, dxob = dx_pair
        j = l // 2
        xin, xn, qkv, o, lse = saved
        do, delta = mixer_do(dxob, full[("dil_w_o", j)], o, F32, "mm_dil_do")
        grads[("dil_w_o", j)] = _mm(o, dxob, "tn", BF16, "mm_dil_dwo")
        dq, dk, dv, dgq, dgk = _dil_bwd(qkv, dil_g_qn[j:j + 1], dil_g_kn[j:j + 1], slopes, do, delta, lse, "dil_bwd")
        gain_g.setdefault("dil_g_qn", {})[j] = dgq
        gain_g.setdefault("dil_g_kn", {})[j] = dgk
        dqkv = jnp.concatenate([dq, dk, dv], axis=1)
        dxn = _mm(dqkv, full[("dil_w_qkv", j)], "nt", F32, "mm_dxn_dil", layer=0)
        grads[("dil_w_qkv", j)] = _mm(xn, dqkv, "tn", BF16, "mm_dwqkv")
        tok = scatter_start(f"mix_{l}", [("dil_w_qkv", j), ("dil_w_o", j)])
        toks = finish_due(dxn)
        dx, dxb, dg = _rms_bwd(xin, mix_norm[l:l + 1], dxn, dxo, "rms_bwd_dil", deps=[tok] + toks)
        gain_g.setdefault("mix_norm", {})[l] = dg
        return dx, dxb

    saved = []
    xc = x0
    for l in range(depth):
        xc, s1 = ffn_fwd(xc, ffn1_norm[l:l + 1], "ffn1", l, 3 * l, deps=[ag_token] if l == 0 else ())
        (xc, h_next), s2 = (mla_fwd if l % 2 == 0 else dil_fwd)(xc, l, 3 * l + 1)
        xc, s3 = ffn_fwd(xc, ffn2_norm[l:l + 1], "ffn2", l, 3 * l + 2, h=h_next)
        saved.append((s1, s2, s3))

    dy, dyb, loss_part = xc if isinstance(xc, tuple) else _loss_head(xc, tgt, "loss_head")
    dx = (dy, dyb)
    loss = lax.psum(loss_part[0, 0], MESH_AXES)

    for bi in reversed(range(len(blocks))):
        tag, _ = blocks[bi]
        l = bi // 3
        s = saved[l][bi % 3]
        if bi % 3 == 2:
            dx = ffn_bwd(dx, s, ffn2_norm[l:l + 1], "ffn2", l, tag,
                         deps=[loss.reshape(1, 1)] if bi == len(blocks) - 1 else ())
        elif bi % 3 == 1:
            dx = (mla_bwd if l % 2 == 0 else dil_bwd)(dx, s, l)
        else:
            dx = ffn_bwd(dx, s, ffn1_norm[l:l + 1], "ffn1", l, tag)
    grad_x = dx[0].reshape(x.shape)
    after = dx[1]
    while pending:
        after = scatter_finish(after)[-1]

    small = [n for n in names if n not in big]

    def gain_local(n):
        rows = [gain_g[n][l] for l in range(W[n].shape[0])]
        g = jnp.concatenate(rows, axis=1)
        return g

    def flat_pad(n, a):
        a = a.reshape(1, -1)
        if n in ("mla_g_qn", "mla_g_kn"):
            a = _pad_heads(a)
        return a

    packed_g = jnp.concatenate([gain_local(n) for n in small], axis=1)
    sizes = [gain_local(n).shape[1] for n in small]
    tot_g = _gain_allreduce(packed_g, "gain_allreduce")
    pw = jnp.concatenate([flat_pad(n, W[n]) for n in small], axis=1)
    pm = jnp.concatenate([flat_pad(n, M1[n]) for n in small], axis=1)
    pv = jnp.concatenate([flat_pad(n, V2[n]) for n in small], axis=1)
    res, _ = _adamw(tot_g.reshape(1, 1, -1), None, me, pw.reshape(1, 1, -1), pm.reshape(1, 1, -1),
                    pv.reshape(1, 1, -1), 0, None, "adamw_gains")
    res = [r.reshape(1, -1) for r in res]
    off = 0
    for n, sz in zip(small, sizes):
        for dst, r in zip((out_g, out_d, out_m, out_v), res):
            piece = r[:, off:off + sz]
            if n in ("mla_g_qn", "mla_g_kn"):
                piece = _unpad_heads(piece)
            dst[n] = piece.reshape(W[n].shape)
        off += sz

    return (loss, grad_x, *[out_g[n] for n in names], *[out_d[n] for n in names],
            *[out_m[n] for n in names], *[out_v[n] for n in names])
```
